```python
import jax, jax.numpy as jnp
from jax import lax
import numpy as np

D_MODEL = 1024
BATCH = 8
SEQ = 4096
DEPTH = 1

MEM_LEN = 256
MIX_WIDTH = D_MODEL
A_WIDTH = MIX_WIDTH // 2
B_WIDTH = MIX_WIDTH - A_WIDTH
A_HEADS = 4
A_HEAD_DIM = A_WIDTH // A_HEADS
B_HEADS = 4
CHUNK = 128
CONV_W = 3
IN_A = 2 * A_WIDTH
IN_B = 3 * B_WIDTH
IN_TOTAL = IN_A + IN_B
X_HEADS = 4
X_HEAD_DIM = D_MODEL // X_HEADS
D_FF = ((8 * D_MODEL // 3 + 255) // 256) * 256
EPS = 1e-6

kernel_name = "hybrid_sgu_shortconv_xattn_layer"


def rms_norm(x, g):
    xf = x.astype(jnp.float32)
    y = xf * lax.rsqrt(jnp.mean(xf * xf, axis=-1, keepdims=True) + EPS)
    return (y * g.astype(jnp.float32)).astype(x.dtype)


def layer_norm(x, g, b):
    xf = x.astype(jnp.float32)
    mu = jnp.mean(xf, axis=-1, keepdims=True)
    var = jnp.mean(jnp.square(xf - mu), axis=-1, keepdims=True)
    y = (xf - mu) * lax.rsqrt(var + EPS)
    return (y * g.astype(jnp.float32) + b.astype(jnp.float32)).astype(x.dtype)


def spatial_gating(a, sgu_ln_g, sgu_ln_b, w_spatial, b_spatial):
    bsz, seq, _ = a.shape
    a = jax.nn.gelu(a)
    u, v = jnp.split(a, 2, axis=-1)
    v = layer_norm(v, sgu_ln_g, sgu_ln_b)
    n_chunks = seq // CHUNK
    v = v.reshape(bsz, n_chunks, CHUNK, A_HEADS, A_HEAD_DIM)
    mask = jnp.tril(jnp.ones((CHUNK, CHUNK), dtype=w_spatial.dtype))
    w = w_spatial * mask[None]
    mixed = jnp.einsum("hts,bnshd->bnthd", w, v)
    mixed = mixed + jnp.transpose(b_spatial)[None, None, :, :, None]
    mixed = mixed.reshape(bsz, seq, A_WIDTH)
    return u * mixed


def short_gated_conv(h, conv_w):
    gate_b, gate_c, val = jnp.split(h, 3, axis=-1)
    z = gate_c * val
    zp = jnp.pad(z, ((0, 0), (CONV_W - 1, 0), (0, 0)))
    seq = z.shape[1]
    conv = (conv_w[0] * zp[:, 0:seq] + conv_w[1] * zp[:, 1:seq + 1]
            + conv_w[2] * zp[:, 2:seq + 2])
    return gate_b * conv


def cross_attention(h, memn, w_q, w_kv, w_o):
    bsz, seq, _ = h.shape
    q = (h @ w_q).reshape(bsz, seq, X_HEADS, X_HEAD_DIM)
    k, v = jnp.split(memn @ w_kv, 2, axis=-1)
    k = k.reshape(bsz, MEM_LEN, X_HEADS, X_HEAD_DIM)
    v = v.reshape(bsz, MEM_LEN, X_HEADS, X_HEAD_DIM)
    scale = X_HEAD_DIM ** -0.5
    s = jnp.einsum("bshd,bmhd->bhsm", q, k).astype(jnp.float32) * scale
    p = jax.nn.softmax(s, axis=-1).astype(v.dtype)
    o = jnp.einsum("bhsm,bmhd->bshd", p, v).reshape(bsz, seq, D_MODEL)
    return o @ w_o


def _fwd_setup_inputs(seed: int = 0) -> dict:
    key = jax.random.key(seed)
    ks = jax.random.split(key, 24)
    f32 = jnp.float32
    nrm = lambda k, shape, scale: jax.random.normal(k, shape, f32) * scale
    gain = lambda k, n: 1.0 + 0.02 * jax.random.normal(k, (n,), f32)
    return {
        "x": jax.random.normal(ks[0], (BATCH, SEQ, D_MODEL), f32),
        "mem": jax.random.normal(ks[1], (BATCH, MEM_LEN, D_MODEL), f32),
        "ln_mix_g": gain(ks[2], D_MODEL),
        "w_in": nrm(ks[3], (D_MODEL, IN_TOTAL), D_MODEL ** -0.5),
        "sgu_ln_g": gain(ks[4], A_WIDTH),
        "sgu_ln_b": nrm(ks[5], (A_WIDTH,), 0.02),
        "w_spatial": nrm(ks[6], (A_HEADS, CHUNK, CHUNK), CHUNK ** -0.5),
        "b_spatial": 1.0 + nrm(ks[7], (A_HEADS, CHUNK), 0.02),
        "conv_w": nrm(ks[8], (CONV_W, B_WIDTH), CONV_W ** -0.5),
        "grp_norm_a": gain(ks[9], A_WIDTH),
        "grp_norm_b": gain(ks[10], B_WIDTH),
        "w_out": nrm(ks[11], (MIX_WIDTH, D_MODEL), MIX_WIDTH ** -0.5),
        "ln_attn_g": gain(ks[12], D_MODEL),
        "ln_mem_g": gain(ks[13], D_MODEL),
        "w_q": nrm(ks[14], (D_MODEL, D_MODEL), D_MODEL ** -0.5),
        "w_kv": nrm(ks[15], (D_MODEL, 2 * D_MODEL), D_MODEL ** -0.5),
        "w_o": nrm(ks[16], (D_MODEL, D_MODEL), D_MODEL ** -0.5),
        "ln_ffn_g": gain(ks[17], D_MODEL),
        "w_gate_up": nrm(ks[18], (D_MODEL, 2 * D_FF), D_MODEL ** -0.5),
        "w_down": nrm(ks[19], (D_FF, D_MODEL), D_FF ** -0.5),
        "ln_final_g": gain(ks[20], D_MODEL),
    }


def _fwd_reference(x, mem, ln_mix_g, w_in, sgu_ln_g, sgu_ln_b, w_spatial, b_spatial,
              conv_w, grp_norm_a, grp_norm_b, w_out, ln_attn_g, ln_mem_g,
              w_q, w_kv, w_o, ln_ffn_g, w_gate_up, w_down, ln_final_g):
    memn = rms_norm(mem, ln_mem_g)
    for _ in range(DEPTH):
        h = rms_norm(x, ln_mix_g) @ w_in
        h_a = h[..., :IN_A]
        h_b = h[..., IN_A:]
        y_a = rms_norm(spatial_gating(h_a, sgu_ln_g, sgu_ln_b, w_spatial, b_spatial), grp_norm_a)
        y_b = rms_norm(short_gated_conv(h_b, conv_w), grp_norm_b)
        x = x + jnp.concatenate([y_a, y_b], axis=-1) @ w_out
        x = x + cross_attention(rms_norm(x, ln_attn_g), memn, w_q, w_kv, w_o)
        g, u = jnp.split(rms_norm(x, ln_ffn_g) @ w_gate_up, 2, axis=-1)
        x = x + (jax.nn.silu(g) * u) @ w_down
    return rms_norm(x, ln_final_g)


import jax as _jax
import jax.numpy as _jnp

TWIN_FORMAT = 'train_step'
FWD_PARAMS = ['x', 'mem', 'ln_mix_g', 'w_in', 'sgu_ln_g', 'sgu_ln_b', 'w_spatial', 'b_spatial', 'conv_w', 'grp_norm_a', 'grp_norm_b', 'w_out', 'ln_attn_g', 'ln_mem_g', 'w_q', 'w_kv', 'w_o', 'ln_ffn_g', 'w_gate_up', 'w_down', 'ln_final_g']
TWIN_WEIGHTS = ['ln_mix_g', 'w_in', 'sgu_ln_g', 'sgu_ln_b', 'w_spatial', 'b_spatial', 'conv_w', 'grp_norm_a', 'grp_norm_b', 'w_out', 'ln_attn_g', 'ln_mem_g', 'w_q', 'w_kv', 'w_o', 'ln_ffn_g', 'w_gate_up', 'w_down', 'ln_final_g']
TWIN_DIFF_INPUT = 'x'
TWIN_INPUTS = ['x', 'mem', 'ln_mix_g', 'w_in', 'sgu_ln_g', 'sgu_ln_b', 'w_spatial', 'b_spatial', 'conv_w', 'grp_norm_a', 'grp_norm_b', 'w_out', 'ln_attn_g', 'ln_mem_g', 'w_q', 'w_kv', 'w_o', 'ln_ffn_g', 'w_gate_up', 'w_down', 'ln_final_g', 'loss_target', 'm_ln_mix_g', 'm_w_in', 'm_sgu_ln_g', 'm_sgu_ln_b', 'm_w_spatial', 'm_b_spatial', 'm_conv_w', 'm_grp_norm_a', 'm_grp_norm_b', 'm_w_out', 'm_ln_attn_g', 'm_ln_mem_g', 'm_w_q', 'm_w_kv', 'm_w_o', 'm_ln_ffn_g', 'm_w_gate_up', 'm_w_down', 'm_ln_final_g', 'v_ln_mix_g', 'v_w_in', 'v_sgu_ln_g', 'v_sgu_ln_b', 'v_w_spatial', 'v_b_spatial', 'v_conv_w', 'v_grp_norm_a', 'v_grp_norm_b', 'v_w_out', 'v_ln_attn_g', 'v_ln_mem_g', 'v_w_q', 'v_w_kv', 'v_w_o', 'v_ln_ffn_g', 'v_w_gate_up', 'v_w_down', 'v_ln_final_g']
TWIN_OUTPUTS = ['loss', 'grad_x', 'grad_ln_mix_g', 'grad_w_in', 'grad_sgu_ln_g', 'grad_sgu_ln_b', 'grad_w_spatial', 'grad_b_spatial', 'grad_conv_w', 'grad_grp_norm_a', 'grad_grp_norm_b', 'grad_w_out', 'grad_ln_attn_g', 'grad_ln_mem_g', 'grad_w_q', 'grad_w_kv', 'grad_w_o', 'grad_ln_ffn_g', 'grad_w_gate_up', 'grad_w_down', 'grad_ln_final_g', 'delta_ln_mix_g', 'delta_w_in', 'delta_sgu_ln_g', 'delta_sgu_ln_b', 'delta_w_spatial', 'delta_b_spatial', 'delta_conv_w', 'delta_grp_norm_a', 'delta_grp_norm_b', 'delta_w_out', 'delta_ln_attn_g', 'delta_ln_mem_g', 'delta_w_q', 'delta_w_kv', 'delta_w_o', 'delta_ln_ffn_g', 'delta_w_gate_up', 'delta_w_down', 'delta_ln_final_g', 'new_m_ln_mix_g', 'new_m_w_in', 'new_m_sgu_ln_g', 'new_m_sgu_ln_b', 'new_m_w_spatial', 'new_m_b_spatial', 'new_m_conv_w', 'new_m_grp_norm_a', 'new_m_grp_norm_b', 'new_m_w_out', 'new_m_ln_attn_g', 'new_m_ln_mem_g', 'new_m_w_q', 'new_m_w_kv', 'new_m_w_o', 'new_m_ln_ffn_g', 'new_m_w_gate_up', 'new_m_w_down', 'new_m_ln_final_g', 'new_v_ln_mix_g', 'new_v_w_in', 'new_v_sgu_ln_g', 'new_v_sgu_ln_b', 'new_v_w_spatial', 'new_v_b_spatial', 'new_v_conv_w', 'new_v_grp_norm_a', 'new_v_grp_norm_b', 'new_v_w_out', 'new_v_ln_attn_g', 'new_v_ln_mem_g', 'new_v_w_q', 'new_v_w_kv', 'new_v_w_o', 'new_v_ln_ffn_g', 'new_v_w_gate_up', 'new_v_w_down', 'new_v_ln_final_g']
TWIN_LEAF_KINDS = {'loss': 'loss', 'grad_x': 'grad_x', 'grad_ln_mix_g': 'grad_w', 'grad_w_in': 'grad_w', 'grad_sgu_ln_g': 'grad_w', 'grad_sgu_ln_b': 'grad_w', 'grad_w_spatial': 'grad_w', 'grad_b_spatial': 'grad_w', 'grad_conv_w': 'grad_w', 'grad_grp_norm_a': 'grad_w', 'grad_grp_norm_b': 'grad_w', 'grad_w_out': 'grad_w', 'grad_ln_attn_g': 'grad_w', 'grad_ln_mem_g': 'grad_w', 'grad_w_q': 'grad_w', 'grad_w_kv': 'grad_w', 'grad_w_o': 'grad_w', 'grad_ln_ffn_g': 'grad_w', 'grad_w_gate_up': 'grad_w', 'grad_w_down': 'grad_w', 'grad_ln_final_g': 'grad_w', 'delta_ln_mix_g': 'delta_w', 'delta_w_in': 'delta_w', 'delta_sgu_ln_g': 'delta_w', 'delta_sgu_ln_b': 'delta_w', 'delta_w_spatial': 'delta_w', 'delta_b_spatial': 'delta_w', 'delta_conv_w': 'delta_w', 'delta_grp_norm_a': 'delta_w', 'delta_grp_norm_b': 'delta_w', 'delta_w_out': 'delta_w', 'delta_ln_attn_g': 'delta_w', 'delta_ln_mem_g': 'delta_w', 'delta_w_q': 'delta_w', 'delta_w_kv': 'delta_w', 'delta_w_o': 'delta_w', 'delta_ln_ffn_g': 'delta_w', 'delta_w_gate_up': 'delta_w', 'delta_w_down': 'delta_w', 'delta_ln_final_g': 'delta_w', 'new_m_ln_mix_g': 'new_m', 'new_m_w_in': 'new_m', 'new_m_sgu_ln_g': 'new_m', 'new_m_sgu_ln_b': 'new_m', 'new_m_w_spatial': 'new_m', 'new_m_b_spatial': 'new_m', 'new_m_conv_w': 'new_m', 'new_m_grp_norm_a': 'new_m', 'new_m_grp_norm_b': 'new_m', 'new_m_w_out': 'new_m', 'new_m_ln_attn_g': 'new_m', 'new_m_ln_mem_g': 'new_m', 'new_m_w_q': 'new_m', 'new_m_w_kv': 'new_m', 'new_m_w_o': 'new_m', 'new_m_ln_ffn_g': 'new_m', 'new_m_w_gate_up': 'new_m', 'new_m_w_down': 'new_m', 'new_m_ln_final_g': 'new_m', 'new_v_ln_mix_g': 'new_v', 'new_v_w_in': 'new_v', 'new_v_sgu_ln_g': 'new_v', 'new_v_sgu_ln_b': 'new_v', 'new_v_w_spatial': 'new_v', 'new_v_b_spatial': 'new_v', 'new_v_conv_w': 'new_v', 'new_v_grp_norm_a': 'new_v', 'new_v_grp_norm_b': 'new_v', 'new_v_w_out': 'new_v', 'new_v_ln_attn_g': 'new_v', 'new_v_ln_mem_g': 'new_v', 'new_v_w_q': 'new_v', 'new_v_w_kv': 'new_v', 'new_v_w_o': 'new_v', 'new_v_ln_ffn_g': 'new_v', 'new_v_w_gate_up': 'new_v', 'new_v_w_down': 'new_v', 'new_v_ln_final_g': 'new_v'}


def _forward(args):
    return _fwd_reference(*[args[k] for k in FWD_PARAMS])


def _output_shape():
    out = _jax.eval_shape(lambda: _forward(_fwd_setup_inputs(0)))
    return out.shape, out.dtype

N_MICROBATCH = 1
ADAM_LR = 0.001
ADAM_B1 = 0.9
ADAM_B2 = 0.999
ADAM_EPS = 1e-08
ADAM_WD = 0.01
ADAM_STEP = 10
PER_EXAMPLE_BATCH_AXIS = {'x': 0, 'mem': 0, 'loss_target': 0}
SHARED_INPUTS = []
_WEIGHT_DTYPES = {'ln_mix_g': _jnp.float32, 'w_in': _jnp.float32, 'sgu_ln_g': _jnp.float32, 'sgu_ln_b': _jnp.float32, 'w_spatial': _jnp.float32, 'b_spatial': _jnp.float32, 'conv_w': _jnp.float32, 'grp_norm_a': _jnp.float32, 'grp_norm_b': _jnp.float32, 'w_out': _jnp.float32, 'ln_attn_g': _jnp.float32, 'ln_mem_g': _jnp.float32, 'w_q': _jnp.float32, 'w_kv': _jnp.float32, 'w_o': _jnp.float32, 'ln_ffn_g': _jnp.float32, 'w_gate_up': _jnp.float32, 'w_down': _jnp.float32, 'ln_final_g': _jnp.float32}
MOMENT_SCALE = {'ln_mix_g': 2.024795e-01, 'w_in': 1.289393e-01, 'sgu_ln_g': 7.485136e-02, 'sgu_ln_b': 8.926593e-02, 'w_spatial': 7.520222e-02, 'b_spatial': 1.111330e-01, 'conv_w': 1.470439e-01, 'grp_norm_a': 1.332131e-01, 'grp_norm_b': 1.335259e-01, 'w_out': 1.365311e-01, 'ln_attn_g': 1.457558e-02, 'ln_mem_g': 2.114628e-02, 'w_q': 1.421687e-02, 'w_kv': 1.423097e-02, 'w_o': 1.443340e-02, 'ln_ffn_g': 1.039631e-01, 'w_gate_up': 4.236743e-02, 'w_down': 6.900552e-02, 'ln_final_g': 3.204279e+01}


def _to_microbatches(a, axis):
    t = _jnp.moveaxis(a, axis, 0)
    t = t.reshape((N_MICROBATCH, t.shape[0] // N_MICROBATCH) + t.shape[1:])
    return _jnp.moveaxis(t, 1, axis + 1)


def setup_inputs(seed: int = 0) -> dict:
    inp = _fwd_setup_inputs(seed)
    key = _jax.random.fold_in(_jax.random.key(seed), 7919)
    shape, _ = _output_shape()
    out = dict(inp)
    out["loss_target"] = _jax.random.normal(_jax.random.fold_in(key, 0), shape, _jnp.float32)
    for i, name in enumerate(TWIN_WEIGHTS):
        w = inp[name].astype(_jnp.float32)
        if MOMENT_SCALE is None:
            s = _jnp.sqrt(_jnp.mean(_jnp.square(w)) + 1e-30)
        else:
            s = MOMENT_SCALE[name]
        km, kv = _jax.random.split(_jax.random.fold_in(key, i + 1))
        out[name] = w
        out["m_" + name] = s * _jax.random.normal(km, w.shape, _jnp.float32)
        out["v_" + name] = (s * s) * _jax.random.uniform(kv, w.shape, _jnp.float32, 0.5, 1.5)
    if N_MICROBATCH > 1:
        for name, axis in PER_EXAMPLE_BATCH_AXIS.items():
            out[name] = _to_microbatches(out[name], axis)
    return {'x': out['x'], 'mem': out['mem'], 'ln_mix_g': out['ln_mix_g'], 'w_in': out['w_in'], 'sgu_ln_g': out['sgu_ln_g'], 'sgu_ln_b': out['sgu_ln_b'], 'w_spatial': out['w_spatial'], 'b_spatial': out['b_spatial'], 'conv_w': out['conv_w'], 'grp_norm_a': out['grp_norm_a'], 'grp_norm_b': out['grp_norm_b'], 'w_out': out['w_out'], 'ln_attn_g': out['ln_attn_g'], 'ln_mem_g': out['ln_mem_g'], 'w_q': out['w_q'], 'w_kv': out['w_kv'], 'w_o': out['w_o'], 'ln_ffn_g': out['ln_ffn_g'], 'w_gate_up': out['w_gate_up'], 'w_down': out['w_down'], 'ln_final_g': out['ln_final_g'], 'loss_target': out['loss_target'], 'm_ln_mix_g': out['m_ln_mix_g'], 'm_w_in': out['m_w_in'], 'm_sgu_ln_g': out['m_sgu_ln_g'], 'm_sgu_ln_b': out['m_sgu_ln_b'], 'm_w_spatial': out['m_w_spatial'], 'm_b_spatial': out['m_b_spatial'], 'm_conv_w': out['m_conv_w'], 'm_grp_norm_a': out['m_grp_norm_a'], 'm_grp_norm_b': out['m_grp_norm_b'], 'm_w_out': out['m_w_out'], 'm_ln_attn_g': out['m_ln_attn_g'], 'm_ln_mem_g': out['m_ln_mem_g'], 'm_w_q': out['m_w_q'], 'm_w_kv': out['m_w_kv'], 'm_w_o': out['m_w_o'], 'm_ln_ffn_g': out['m_ln_ffn_g'], 'm_w_gate_up': out['m_w_gate_up'], 'm_w_down': out['m_w_down'], 'm_ln_final_g': out['m_ln_final_g'], 'v_ln_mix_g': out['v_ln_mix_g'], 'v_w_in': out['v_w_in'], 'v_sgu_ln_g': out['v_sgu_ln_g'], 'v_sgu_ln_b': out['v_sgu_ln_b'], 'v_w_spatial': out['v_w_spatial'], 'v_b_spatial': out['v_b_spatial'], 'v_conv_w': out['v_conv_w'], 'v_grp_norm_a': out['v_grp_norm_a'], 'v_grp_norm_b': out['v_grp_norm_b'], 'v_w_out': out['v_w_out'], 'v_ln_attn_g': out['v_ln_attn_g'], 'v_ln_mem_g': out['v_ln_mem_g'], 'v_w_q': out['v_w_q'], 'v_w_kv': out['v_w_kv'], 'v_w_o': out['v_w_o'], 'v_ln_ffn_g': out['v_ln_ffn_g'], 'v_w_gate_up': out['v_w_gate_up'], 'v_w_down': out['v_w_down'], 'v_ln_final_g': out['v_ln_final_g']}


def _loss(weights, diff, rest, loss_target):
    with _jax.named_scope("forward"):
        args = {**rest, TWIN_DIFF_INPUT: diff, **{k: w.astype(_WEIGHT_DTYPES[k]) for k, w in weights.items()}}
        y = _forward(args)
    with _jax.named_scope("loss_head"):
        err = _jnp.square(y.astype(_jnp.float32) - loss_target)
        return 0.5 * _jnp.sum(_jnp.mean(err, axis=-1)) if err.ndim else 0.5 * err


def _adamw(w, g, m, v):
    m = ADAM_B1 * m + (1.0 - ADAM_B1) * g
    v = ADAM_B2 * v + (1.0 - ADAM_B2) * _jnp.square(g)
    m_hat = m / (1.0 - ADAM_B1 ** ADAM_STEP)
    v_hat = v / (1.0 - ADAM_B2 ** ADAM_STEP)
    delta = -ADAM_LR * (m_hat / (_jnp.sqrt(v_hat) + ADAM_EPS) + ADAM_WD * w)
    return delta, m, v


def reference(x, mem, ln_mix_g, w_in, sgu_ln_g, sgu_ln_b, w_spatial, b_spatial, conv_w, grp_norm_a, grp_norm_b, w_out, ln_attn_g, ln_mem_g, w_q, w_kv, w_o, ln_ffn_g, w_gate_up, w_down, ln_final_g, loss_target, m_ln_mix_g, m_w_in, m_sgu_ln_g, m_sgu_ln_b, m_w_spatial, m_b_spatial, m_conv_w, m_grp_norm_a, m_grp_norm_b, m_w_out, m_ln_attn_g, m_ln_mem_g, m_w_q, m_w_kv, m_w_o, m_ln_ffn_g, m_w_gate_up, m_w_down, m_ln_final_g, v_ln_mix_g, v_w_in, v_sgu_ln_g, v_sgu_ln_b, v_w_spatial, v_b_spatial, v_conv_w, v_grp_norm_a, v_grp_norm_b, v_w_out, v_ln_attn_g, v_ln_mem_g, v_w_q, v_w_kv, v_w_o, v_ln_ffn_g, v_w_gate_up, v_w_down, v_ln_final_g):
    given = dict(x=x, mem=mem, ln_mix_g=ln_mix_g, w_in=w_in, sgu_ln_g=sgu_ln_g, sgu_ln_b=sgu_ln_b, w_spatial=w_spatial, b_spatial=b_spatial, conv_w=conv_w, grp_norm_a=grp_norm_a, grp_norm_b=grp_norm_b, w_out=w_out, ln_attn_g=ln_attn_g, ln_mem_g=ln_mem_g, w_q=w_q, w_kv=w_kv, w_o=w_o, ln_ffn_g=ln_ffn_g, w_gate_up=w_gate_up, w_down=w_down, ln_final_g=ln_final_g, loss_target=loss_target, m_ln_mix_g=m_ln_mix_g, m_w_in=m_w_in, m_sgu_ln_g=m_sgu_ln_g, m_sgu_ln_b=m_sgu_ln_b, m_w_spatial=m_w_spatial, m_b_spatial=m_b_spatial, m_conv_w=m_conv_w, m_grp_norm_a=m_grp_norm_a, m_grp_norm_b=m_grp_norm_b, m_w_out=m_w_out, m_ln_attn_g=m_ln_attn_g, m_ln_mem_g=m_ln_mem_g, m_w_q=m_w_q, m_w_kv=m_w_kv, m_w_o=m_w_o, m_ln_ffn_g=m_ln_ffn_g, m_w_gate_up=m_w_gate_up, m_w_down=m_w_down, m_ln_final_g=m_ln_final_g, v_ln_mix_g=v_ln_mix_g, v_w_in=v_w_in, v_sgu_ln_g=v_sgu_ln_g, v_sgu_ln_b=v_sgu_ln_b, v_w_spatial=v_w_spatial, v_b_spatial=v_b_spatial, v_conv_w=v_conv_w, v_grp_norm_a=v_grp_norm_a, v_grp_norm_b=v_grp_norm_b, v_w_out=v_w_out, v_ln_attn_g=v_ln_attn_g, v_ln_mem_g=v_ln_mem_g, v_w_q=v_w_q, v_w_kv=v_w_kv, v_w_o=v_w_o, v_ln_ffn_g=v_ln_ffn_g, v_w_gate_up=v_w_gate_up, v_w_down=v_w_down, v_ln_final_g=v_ln_final_g)
    weights = {n: given[n] for n in TWIN_WEIGHTS}
    shared = {n: given[n] for n in SHARED_INPUTS}
    per_example = {n: given[n] for n in ['x', 'mem']}
    grad_fn = _jax.value_and_grad(_loss, argnums=(0, 1))

    def one_microbatch(ex, loss_target):
        ex = dict(ex)
        diff = ex.pop(TWIN_DIFF_INPUT)
        return grad_fn(weights, diff, {**shared, **ex}, loss_target)

    if N_MICROBATCH == 1:
        loss, (grad_w, grad_x) = one_microbatch(per_example, given["loss_target"])
    else:
        def body(carry, xs):
            loss_sum, grad_sum = carry
            l_k, (gw_k, gx_k) = one_microbatch(xs[0], xs[1])
            with _jax.named_scope("update"):
                return (loss_sum + l_k, _jax.tree.map(_jnp.add, grad_sum, gw_k)), gx_k

        init = (_jnp.zeros((), _jnp.float32), _jax.tree.map(_jnp.zeros_like, weights))
        (loss, grad_w), grad_x = _jax.lax.scan(body, init, (per_example, given["loss_target"]))
    with _jax.named_scope("update"):
        delta_w, new_m, new_v = {}, {}, {}
        for n in TWIN_WEIGHTS:
            delta_w[n], new_m[n], new_v[n] = _adamw(weights[n], grad_w[n], given["m_" + n], given["v_" + n])
    return (loss, grad_x, *[grad_w[n] for n in TWIN_WEIGHTS], *[delta_w[n] for n in TWIN_WEIGHTS],
            *[new_m[n] for n in TWIN_WEIGHTS], *[new_v[n] for n in TWIN_WEIGHTS])
```

```python
import functools

import jax
import jax.numpy as jnp
from jax import lax
from jax.experimental import pallas as pl
from jax.experimental.pallas import tpu as pltpu

F32 = jnp.float32
BF16 = jnp.bfloat16
SDS = jax.ShapeDtypeStruct
MESH = pl.DeviceIdType.MESH

EPS = 1e-6
N_DEV = 8
HEADS = 4
CHUNK = 128
HALO = 16
SUB = 8
LANES = 128
TOKEN_TILE = 512
WIRE_COLS = 1024

ADAM_LR = 0.001
ADAM_B1 = 0.9
ADAM_B2 = 0.999
ADAM_EPS = 1e-08
ADAM_WD = 0.01
ADAM_STEP = 10

BIG = ("w_in", "w_out", "w_q", "w_kv", "w_o", "w_gate_up", "w_down")
COL_SHARDED = ("w_in", "w_kv", "w_gate_up")
SMALL = ("ln_mix_g", "sgu_ln_g", "sgu_ln_b", "w_spatial", "b_spatial", "grp_norm_a", "grp_norm_b",
         "ln_attn_g", "ln_mem_g", "ln_ffn_g", "ln_final_g")


def _pcall(body, **kw):
    return pl.pallas_call(body, **kw)


def _arb(n):
    return pltpu.CompilerParams(dimension_semantics=("arbitrary",) * n)


def _tile(n, target, mult):
    best = None
    for t in range(mult, min(n, target) + 1, mult):
        if n % t == 0:
            best = t
    return n if best is None else best


def _round_up(n, m):
    return (n + m - 1) // m * m


def _dot(a, b):
    return jnp.dot(a, b, preferred_element_type=F32)


def _dot_nt(a, b):
    return lax.dot_general(a, b, (((1,), (1,)), ((), ())), preferred_element_type=F32)


def _dot_tn(a, b):
    return lax.dot_general(a, b, (((0,), (0,)), ((), ())), preferred_element_type=F32)


def _rstd(x):
    return lax.rsqrt(jnp.mean(x * x, axis=-1, keepdims=True) + EPS)


def _rms_bwd(dy, x, r, g):
    gdy = dy * g
    proj = jnp.sum(gdy * x, axis=-1, keepdims=True) * (1.0 / x.shape[-1])
    dx = r * gdy - x * (r * r * r) * proj
    dg = jnp.sum(dy * (x * r), axis=0, keepdims=True)
    return dx, dg


_GELU_C = 0.7978845608028654
_GELU_A = 0.044715


def _gelu(x):
    t = jnp.tanh(_GELU_C * (x + _GELU_A * x * x * x))
    return 0.5 * x * (1.0 + t), t


def _gelu_grad(x, t):
    return 0.5 * (1.0 + t) + 0.5 * x * (1.0 - t * t) * (_GELU_C * (1.0 + 3.0 * _GELU_A * x * x))


def _sigmoid(x):
    return 1.0 / (1.0 + jnp.exp(-x))


def _softmax(s):
    m = jnp.max(s, axis=-1, keepdims=True)
    e = jnp.exp(s - m)
    return e / jnp.sum(e, axis=-1, keepdims=True)


def _adamw(w, g, m, v):
    m = ADAM_B1 * m + (1.0 - ADAM_B1) * g
    v = ADAM_B2 * v + (1.0 - ADAM_B2) * (g * g)
    m_hat = m / (1.0 - ADAM_B1 ** ADAM_STEP)
    v_hat = v / (1.0 - ADAM_B2 ** ADAM_STEP)
    delta = -ADAM_LR * (m_hat / (jnp.sqrt(v_hat) + ADAM_EPS) + ADAM_WD * w)
    return delta, m, v


def _tril_mask():
    t = lax.broadcasted_iota(jnp.int32, (CHUNK, CHUNK), 0)
    s = lax.broadcasted_iota(jnp.int32, (CHUNK, CHUNK), 1)
    return (s <= t).astype(F32)


def _sgu_forward(ha, lng, lnb, wm, bt, mixed_s):
    aw = ha.shape[1] // 2
    hd = aw // HEADS
    a, th = _gelu(ha)
    u = a[:, :aw]
    v = a[:, aw:]
    mu = jnp.mean(v, axis=-1, keepdims=True)
    vc = v - mu
    rl = lax.rsqrt(jnp.mean(vc * vc, axis=-1, keepdims=True) + EPS)
    xhat = vc * rl
    vln = (xhat * lng + lnb).astype(BF16)
    for n in range(ha.shape[0] // CHUNK):
        rows = slice(n * CHUNK, (n + 1) * CHUNK)
        for h in range(HEADS):
            cols = slice(h * hd, (h + 1) * hd)
            mixed_s[rows, cols] = _dot(wm[h], vln[rows, cols]) + bt[:, h:h + 1]
    return th, u, xhat, rl, vln


def _conv_taps(zext):
    return pltpu.roll(zext, 2, 0), pltpu.roll(zext, 1, 0)


def _kv_forward(mem, g_mem, w_kv):
    ml, d = mem.shape

    def body(mem_ref, g_ref, w_ref, memn_ref, kv_ref):
        x = mem_ref[...]
        memn = (x * _rstd(x) * g_ref[...]).astype(BF16)
        memn_ref[...] = memn
        kv_ref[...] = _dot(memn, w_ref[...]).astype(BF16)

    return _pcall(body, out_shape=(SDS((ml, d), BF16), SDS((ml, 2 * d), BF16)), name="kv_forward")(mem, g_mem, w_kv)


def _in_forward(x, g, w_in, tm):
    s, d = x.shape
    n_in = w_in.shape[1]

    def body(x_ref, g_ref, w_ref, xn_ref, h_ref):
        xv = x_ref[...]
        xn = (xv * _rstd(xv) * g_ref[...]).astype(BF16)
        xn_ref[...] = xn
        h_ref[...] = _dot(xn, w_ref[...])

    return _pcall(
        body, grid=(s // tm,),
        in_specs=[pl.BlockSpec((tm, d), lambda i: (i, 0)), pl.BlockSpec((1, d), lambda i: (0, 0)),
                  pl.BlockSpec((d, n_in), lambda i: (0, 0))],
        out_specs=[pl.BlockSpec((tm, d), lambda i: (i, 0)), pl.BlockSpec((tm, n_in), lambda i: (i, 0))],
        out_shape=(SDS((s, d), BF16), SDS((s, n_in), F32)),
        compiler_params=_arb(1), name="in_forward")(x, g, w_in)


def _mix_forward(h, x, lng, lnb, w_sp, bt, conv_w, ga, gb, w_out, tm):
    s, d = x.shape
    n_in = h.shape[1]
    aw = lng.shape[1]
    bw = d - aw
    in_a = 2 * aw
    hb_blocks = tm // HALO

    def body(h_ref, hprev_ref, x_ref, lng_ref, lnb_ref, wsp_ref, bt_ref, cw_ref, ga_ref, gb_ref, wout_ref,
             ycat_ref, x1_ref, mixed_s):
        i = pl.program_id(0)
        mask = _tril_mask()
        wm = [(wsp_ref[hh] * mask).astype(BF16) for hh in range(HEADS)]
        hv = h_ref[...]
        _, u, _, _, _ = _sgu_forward(hv[:, :in_a], lng_ref[...], lnb_ref[...], wm, bt_ref[...], mixed_s)
        sg = u * mixed_s[...]
        ycat_ref[:, :aw] = (sg * _rstd(sg) * ga_ref[...]).astype(BF16)

        gate_b = hv[:, in_a:in_a + bw]
        z = hv[:, in_a + bw:in_a + 2 * bw] * hv[:, in_a + 2 * bw:]
        hp = hprev_ref[...]
        zp = hp[:, in_a + bw:in_a + 2 * bw] * hp[:, in_a + 2 * bw:]
        zp = jnp.where(i == 0, 0.0, zp)
        zext = jnp.concatenate([zp, z], axis=0)
        z2, z1 = _conv_taps(zext)
        cw = cw_ref[...]
        conv = cw[0:1] * z2[HALO:] + cw[1:2] * z1[HALO:] + cw[2:3] * z
        sc = gate_b * conv
        ycat_ref[:, aw:] = (sc * _rstd(sc) * gb_ref[...]).astype(BF16)
        x1_ref[...] = x_ref[...] + _dot(ycat_ref[...], wout_ref[...])

    full = lambda shape: pl.BlockSpec(shape, lambda i: (0,) * len(shape))
    return _pcall(
        body, grid=(s // tm,),
        in_specs=[pl.BlockSpec((tm, n_in), lambda i: (i, 0)),
                  pl.BlockSpec((HALO, n_in), lambda i: (jnp.maximum(i * hb_blocks - 1, 0), 0)),
                  pl.BlockSpec((tm, d), lambda i: (i, 0)),
                  full((1, aw)), full((1, aw)), full((HEADS, CHUNK, CHUNK)), full((CHUNK, HEADS)),
                  full((3, bw)), full((1, aw)), full((1, bw)), full((d, d))],
        out_specs=[pl.BlockSpec((tm, d), lambda i: (i, 0)), pl.BlockSpec((tm, d), lambda i: (i, 0))],
        out_shape=(SDS((s, d), BF16), SDS((s, d), F32)),
        scratch_shapes=[pltpu.VMEM((tm, aw), F32)],
        compiler_params=_arb(1), name="mix_forward")(h, h, x, lng, lnb, w_sp, bt, conv_w, ga, gb, w_out)


def _attn_forward(x1, g, w_q, kv, w_o, tm):
    s, d = x1.shape
    ml = kv.shape[0]
    xd = d // HEADS
    scale = xd ** -0.5

    def body(x1_ref, g_ref, wq_ref, kv_ref, wo_ref, xn_ref, q_ref, o_ref, x2_ref):
        xv = x1_ref[...]
        xn = (xv * _rstd(xv) * g_ref[...]).astype(BF16)
        xn_ref[...] = xn
        q_ref[...] = _dot(xn, wq_ref[...]).astype(BF16)
        for hh in range(HEADS):
            cols = slice(hh * xd, (hh + 1) * xd)
            p = _softmax(_dot_nt(q_ref[:, cols], kv_ref[:, cols]) * scale)
            o_ref[:, cols] = _dot(p.astype(BF16), kv_ref[:, d + hh * xd:d + (hh + 1) * xd]).astype(BF16)
        x2_ref[...] = xv + _dot(o_ref[...], wo_ref[...])

    tok = pl.BlockSpec((tm, d), lambda i: (i, 0))
    return _pcall(
        body, grid=(s // tm,),
        in_specs=[tok, pl.BlockSpec((1, d), lambda i: (0, 0)), pl.BlockSpec((d, d), lambda i: (0, 0)),
                  pl.BlockSpec((ml, 2 * d), lambda i: (0, 0)), pl.BlockSpec((d, d), lambda i: (0, 0))],
        out_specs=[tok, tok, tok, tok],
        out_shape=(SDS((s, d), BF16), SDS((s, d), BF16), SDS((s, d), BF16), SDS((s, d), F32)),
        compiler_params=_arb(1), name="attn_forward")(x1, g, w_q, kv, w_o)


def _ffn_forward(x2, g, w_gate_up, w_down, tm, tf):
    s, d = x2.shape
    dff = w_down.shape[0]
    nf = dff // tf

    def body(x2_ref, g_ref, wg_ref, wu_ref, wd_ref, xn_ref, gate_ref, up_ref, x3_ref):
        f = pl.program_id(1)

        @pl.when(f == 0)
        def _():
            xv = x2_ref[...]
            xn_ref[...] = (xv * _rstd(xv) * g_ref[...]).astype(BF16)
            x3_ref[...] = xv

        xn = xn_ref[...]
        gate = _dot(xn, wg_ref[...])
        up = _dot(xn, wu_ref[...])
        gate_ref[...] = gate.astype(BF16)
        up_ref[...] = up.astype(BF16)
        act = (gate * _sigmoid(gate) * up).astype(BF16)
        x3_ref[...] += _dot(act, wd_ref[...])

    tok = pl.BlockSpec((tm, d), lambda i, f: (i, 0))
    ff = pl.BlockSpec((tm, tf), lambda i, f: (i, f))
    return _pcall(
        body, grid=(s // tm, nf),
        in_specs=[tok, pl.BlockSpec((1, d), lambda i, f: (0, 0)),
                  pl.BlockSpec((d, tf), lambda i, f: (0, f)), pl.BlockSpec((d, tf), lambda i, f: (0, f + nf)),
                  pl.BlockSpec((tf, d), lambda i, f: (f, 0))],
        out_specs=[tok, ff, ff, tok],
        out_shape=(SDS((s, d), BF16), SDS((s, dff), BF16), SDS((s, dff), BF16), SDS((s, d), F32)),
        compiler_params=_arb(2), name="ffn_forward")(x2, g, w_gate_up, w_gate_up, w_down)


def _ffn_backward(x3, target, g_final, x2, g_ffn, gate, up, w_gate_up, w_down, tm, tf):
    s, d = x3.shape
    dff = w_down.shape[0]
    nf = dff // tf

    def body(x3_ref, tgt_ref, gf_ref, x2_ref, g2_ref, gate_ref, up_ref, wg_ref, wu_ref, wd_ref,
             loss_ref, dgf_ref, dg2_ref, act_ref, dgate_ref, dup_ref, dx3b_ref, dx2_ref, dx3_s, acc_s):
        i = pl.program_id(0)
        f = pl.program_id(1)

        @pl.when((i == 0) & (f == 0))
        def _():
            loss_ref[...] = jnp.zeros_like(loss_ref)
            dgf_ref[...] = jnp.zeros_like(dgf_ref)
            dg2_ref[...] = jnp.zeros_like(dg2_ref)

        @pl.when(f == 0)
        def _():
            xv = x3_ref[...]
            r = _rstd(xv)
            diff = xv * r * gf_ref[...] - tgt_ref[...]
            loss_ref[...] += 0.5 * jnp.sum(jnp.sum(diff * diff, axis=-1, keepdims=True), axis=0, keepdims=True) * (1.0 / d)
            dx3, dgf = _rms_bwd(diff * (1.0 / d), xv, r, gf_ref[...])
            dgf_ref[...] += dgf
            dx3_s[...] = dx3
            dx3b_ref[...] = dx3.astype(BF16)
            acc_s[...] = jnp.zeros_like(acc_s)

        dact = _dot_nt(dx3b_ref[...], wd_ref[...])
        gv = gate_ref[...].astype(F32)
        uv = up_ref[...].astype(F32)
        sg = _sigmoid(gv)
        silu = gv * sg
        act_ref[...] = (silu * uv).astype(BF16)
        dgate = (dact * uv * (sg * (1.0 + gv * (1.0 - sg)))).astype(BF16)
        dup = (dact * silu).astype(BF16)
        dgate_ref[...] = dgate
        dup_ref[...] = dup
        acc_s[...] += _dot_nt(dgate, wg_ref[...]) + _dot_nt(dup, wu_ref[...])

        @pl.when(f == nf - 1)
        def _():
            xv = x2_ref[...]
            dxn, dg2 = _rms_bwd(acc_s[...], xv, _rstd(xv), g2_ref[...])
            dg2_ref[...] += dg2
            dx2_ref[...] = dx3_s[...] + dxn

    tok = pl.BlockSpec((tm, d), lambda i, f: (i, 0))
    ff = pl.BlockSpec((tm, tf), lambda i, f: (i, f))
    vec = pl.BlockSpec((1, d), lambda i, f: (0, 0))
    return _pcall(
        body, grid=(s // tm, nf),
        in_specs=[tok, tok, vec, tok, vec, ff, ff,
                  pl.BlockSpec((d, tf), lambda i, f: (0, f)), pl.BlockSpec((d, tf), lambda i, f: (0, f + nf)),
                  pl.BlockSpec((tf, d), lambda i, f: (f, 0))],
        out_specs=[pl.BlockSpec((SUB, LANES), lambda i, f: (0, 0)), vec, vec, ff, ff, ff, tok, tok],
        out_shape=(SDS((SUB, LANES), F32), SDS((1, d), F32), SDS((1, d), F32), SDS((s, dff), BF16),
                   SDS((s, dff), BF16), SDS((s, dff), BF16), SDS((s, d), BF16), SDS((s, d), F32)),
        scratch_shapes=[pltpu.VMEM((tm, d), F32), pltpu.VMEM((tm, d), F32)],
        compiler_params=_arb(2), name="ffn_backward")(x3, target, g_final, x2, g_ffn, gate, up, w_gate_up, w_gate_up, w_down)


def _attn_backward(dx2, x1, g, q, kv, w_q, w_o, tm):
    s, d = x1.shape
    ml = kv.shape[0]
    xd = d // HEADS
    scale = xd ** -0.5

    def body(dx2_ref, x1_ref, g_ref, q_ref, kv_ref, wq_ref, wo_ref,
             dx2b_ref, dq_ref, dx1_ref, dkv_ref, dg_ref, do_s):
        i = pl.program_id(0)

        @pl.when(i == 0)
        def _():
            dkv_ref[...] = jnp.zeros_like(dkv_ref)
            dg_ref[...] = jnp.zeros_like(dg_ref)

        dx2 = dx2_ref[...]
        dx2b_ref[...] = dx2.astype(BF16)
        do_s[...] = _dot_nt(dx2b_ref[...], wo_ref[...]).astype(BF16)
        for hh in range(HEADS):
            kc = slice(hh * xd, (hh + 1) * xd)
            vc = slice(d + hh * xd, d + (hh + 1) * xd)
            qh = q_ref[:, kc]
            kh = kv_ref[:, kc]
            doh = do_s[:, kc]
            p = _softmax(_dot_nt(qh, kh) * scale)
            dp = _dot_nt(doh, kv_ref[:, vc])
            dkv_ref[:, vc] += _dot_tn(p.astype(BF16), doh)
            ds = (p * (dp - jnp.sum(dp * p, axis=-1, keepdims=True)) * scale).astype(BF16)
            dq_ref[:, kc] = _dot(ds, kh).astype(BF16)
            dkv_ref[:, kc] += _dot_tn(ds, qh)
        dxn = _dot_nt(dq_ref[...], wq_ref[...])
        xv = x1_ref[...]
        dx, dg = _rms_bwd(dxn, xv, _rstd(xv), g_ref[...])
        dg_ref[...] += dg
        dx1_ref[...] = dx2 + dx

    tok = pl.BlockSpec((tm, d), lambda i: (i, 0))
    vec = pl.BlockSpec((1, d), lambda i: (0, 0))
    sq = pl.BlockSpec((d, d), lambda i: (0, 0))
    kvs = pl.BlockSpec((ml, 2 * d), lambda i: (0, 0))
    return _pcall(
        body, grid=(s // tm,),
        in_specs=[tok, tok, vec, tok, kvs, sq, sq],
        out_specs=[tok, tok, tok, kvs, vec],
        out_shape=(SDS((s, d), BF16), SDS((s, d), BF16), SDS((s, d), F32), SDS((ml, 2 * d), F32), SDS((1, d), F32)),
        scratch_shapes=[pltpu.VMEM((tm, d), BF16)],
        compiler_params=_arb(1), name="attn_backward")(dx2, x1, g, q, kv, w_q, w_o)


def _kv_backward(dkv, memn, mem, g_mem, w_kv):
    ml, d = mem.shape

    def body(dkv_ref, memn_ref, mem_ref, g_ref, w_ref, dw_ref, dg_ref):
        dkvb = dkv_ref[...].astype(BF16)
        dw_ref[...] = _dot_tn(memn_ref[...], dkvb)
        dmemn = _dot_nt(dkvb, w_ref[...])
        x = mem_ref[...]
        dg_ref[...] = jnp.sum(dmemn * (x * _rstd(x)), axis=0, keepdims=True)

    return _pcall(body, out_shape=(SDS((d, 2 * d), F32), SDS((1, d), F32)), name="kv_backward")(dkv, memn, mem, g_mem, w_kv)


def _mix_backward(dx1, x, g_mix, h, lng, lnb, w_sp, bt, conv_w, ga, gb, w_out, w_in, tm):
    s, d = x.shape
    n_in = h.shape[1]
    aw = lng.shape[1]
    bw = d - aw
    hd = aw // HEADS
    in_a = 2 * aw
    hb_blocks = tm // HALO
    last_blk = s // HALO - 1
    nt = s // tm
    te = tm + HALO
    tee = tm + 2 * HALO

    def body(dx1_ref, dx1n_ref, x_ref, gm_ref, h_ref, hp_ref, hn_ref, lng_ref, lnb_ref, wsp_ref, bt_ref, cw_ref,
             ga_ref, gb_ref, wout_ref, win_ref,
             dx1b_ref, dh_ref, dx_ref, dga_ref, dgb_ref, dcw_ref, dlng_ref, dlnb_ref, dwsp_ref, dbs_ref, dgm_ref,
             mixed_s, dvln_s):
        i = pl.program_id(0)

        @pl.when(i == 0)
        def _():
            for ref in (dga_ref, dgb_ref, dcw_ref, dlng_ref, dlnb_ref, dwsp_ref, dbs_ref, dgm_ref):
                ref[...] = jnp.zeros_like(ref)

        mask = _tril_mask()
        wm = [(wsp_ref[hh] * mask).astype(BF16) for hh in range(HEADS)]
        hv = h_ref[...]
        dx1 = dx1_ref[...]
        dx1b_ref[...] = dx1.astype(BF16)
        dx1e = jnp.concatenate([dx1, dx1n_ref[...]], axis=0).astype(BF16)
        dycat = _dot_nt(dx1e, wout_ref[...])

        hbe = jnp.concatenate([hp_ref[:, in_a:], hv[:, in_a:], hn_ref[:, in_a:]], axis=0)
        row = lax.broadcasted_iota(jnp.int32, (tee, 1), 0)
        zext = hbe[:, bw:2 * bw] * hbe[:, 2 * bw:]
        zext = jnp.where((i == 0) & (row < HALO), 0.0, zext)
        z2e, z1e = _conv_taps(zext)
        cw = cw_ref[...]
        conv_e = (cw[0:1] * z2e + cw[1:2] * z1e + cw[2:3] * zext)[HALO:]
        gate_b_e = hbe[HALO:, :bw]
        sc_e = gate_b_e * conv_e
        rb = _rstd(sc_e)
        dyb = dycat[:, aw:]
        gdy = dyb * gb_ref[...]
        dsc_e = rb * gdy - sc_e * (rb * rb * rb) * (jnp.sum(gdy * sc_e, axis=-1, keepdims=True) * (1.0 / bw))
        dgb_ref[...] += jnp.sum((dyb * (sc_e * rb))[:tm], axis=0, keepdims=True)
        dconv_e = dsc_e * gate_b_e
        dconv_e = jnp.where((i == nt - 1) & (row[:te] >= tm), 0.0, dconv_e)
        dconv = dconv_e[:tm]
        dc1 = pltpu.roll(dconv_e, te - 1, 0)[:tm]
        dc2 = pltpu.roll(dconv_e, te - 2, 0)[:tm]
        dz = cw[2:3] * dconv + cw[1:2] * dc1 + cw[0:1] * dc2
        z = zext[HALO:HALO + tm]
        z1 = z1e[HALO:HALO + tm]
        z2 = z2e[HALO:HALO + tm]
        dcw_ref[0:1, :] += jnp.sum(dconv * z2, axis=0, keepdims=True)
        dcw_ref[1:2, :] += jnp.sum(dconv * z1, axis=0, keepdims=True)
        dcw_ref[2:3, :] += jnp.sum(dconv * z, axis=0, keepdims=True)
        dh_ref[:, in_a:in_a + bw] = (dsc_e[:tm] * conv_e[:tm]).astype(BF16)
        dh_ref[:, in_a + bw:in_a + 2 * bw] = (dz * hv[:, in_a + 2 * bw:]).astype(BF16)
        dh_ref[:, in_a + 2 * bw:] = (dz * hv[:, in_a + bw:in_a + 2 * bw]).astype(BF16)

        ha = hv[:, :in_a]
        th, u, xhat, rl, vln = _sgu_forward(ha, lng_ref[...], lnb_ref[...], wm, bt_ref[...], mixed_s)
        mixed = mixed_s[...]
        sg = u * mixed
        dsg, dga = _rms_bwd(dycat[:tm, :aw], sg, _rstd(sg), ga_ref[...])
        dga_ref[...] += dga
        du = dsg * mixed
        dmixed = dsg * u
        dmb = dmixed.astype(BF16)
        for n in range(tm // CHUNK):
            rows = slice(n * CHUNK, (n + 1) * CHUNK)
            dbs_ref[...] += dmixed[rows]
            for hh in range(HEADS):
                cols = slice(hh * hd, (hh + 1) * hd)
                dvln_s[rows, cols] = _dot_tn(wm[hh], dmb[rows, cols])
                dwsp_ref[hh] += mask * _dot_nt(dmb[rows, cols], vln[rows, cols])
        dvln = dvln_s[...]
        dlng_ref[...] += jnp.sum(dvln * xhat, axis=0, keepdims=True)
        dlnb_ref[...] += jnp.sum(dvln, axis=0, keepdims=True)
        dxh = dvln * lng_ref[...]
        dv = rl * (dxh - jnp.mean(dxh, axis=-1, keepdims=True) - xhat * jnp.mean(dxh * xhat, axis=-1, keepdims=True))
        dh_ref[:, :in_a] = (jnp.concatenate([du, dv], axis=-1) * _gelu_grad(ha, th)).astype(BF16)

        dxn = _dot_nt(dh_ref[...], win_ref[...])
        xv = x_ref[...]
        dx, dgm = _rms_bwd(dxn, xv, _rstd(xv), gm_ref[...])
        dgm_ref[...] += dgm
        dx_ref[...] = dx1 + dx

    full = lambda shape: pl.BlockSpec(shape, lambda i: (0,) * len(shape))
    tok = pl.BlockSpec((tm, d), lambda i: (i, 0))
    nxt = lambda i: (jnp.minimum((i + 1) * hb_blocks, last_blk), 0)
    prv = lambda i: (jnp.maximum(i * hb_blocks - 1, 0), 0)
    return _pcall(
        body, grid=(nt,),
        in_specs=[tok, pl.BlockSpec((HALO, d), nxt), tok, full((1, d)),
                  pl.BlockSpec((tm, n_in), lambda i: (i, 0)), pl.BlockSpec((HALO, n_in), prv), pl.BlockSpec((HALO, n_in), nxt),
                  full((1, aw)), full((1, aw)), full((HEADS, CHUNK, CHUNK)), full((CHUNK, HEADS)), full((3, bw)),
                  full((1, aw)), full((1, bw)), full((d, d)), full((d, n_in))],
        out_specs=[tok, pl.BlockSpec((tm, n_in), lambda i: (i, 0)), tok,
                   full((1, aw)), full((1, bw)), full((SUB, bw)), full((1, aw)), full((1, aw)),
                   full((HEADS, CHUNK, CHUNK)), full((CHUNK, aw)), full((1, d))],
        out_shape=(SDS((s, d), BF16), SDS((s, n_in), BF16), SDS((s, d), F32),
                   SDS((1, aw), F32), SDS((1, bw), F32), SDS((SUB, bw), F32), SDS((1, aw), F32), SDS((1, aw), F32),
                   SDS((HEADS, CHUNK, CHUNK), F32), SDS((CHUNK, aw), F32), SDS((1, d), F32)),
        scratch_shapes=[pltpu.VMEM((tm, aw), F32), pltpu.VMEM((tm, aw), F32)],
        compiler_params=_arb(1), name="mix_backward")(dx1, dx1, x, g_mix, h, h, h, lng, lnb, w_sp, bt, conv_w, ga, gb, w_out, w_in)


def _bias_grad(dbs):
    aw = dbs.shape[1]
    hd = aw // HEADS

    def body(dbs_ref, out_ref):
        ones = jnp.ones((SUB, hd), F32)
        for hh in range(HEADS):
            r = lax.dot_general(ones, dbs_ref[:, hh * hd:(hh + 1) * hd], (((1,), (1,)), ((), ())),
                                precision=lax.Precision.HIGHEST, preferred_element_type=F32)
            out_ref[hh:hh + 1, :] = r[0:1]

    return _pcall(body, out_shape=SDS((HEADS, CHUNK), F32), name="bias_grad")(dbs)


def _wgrad(a, b, name):
    k, m = a.shape
    n = b.shape[1]
    tm = _tile(m, 1024, LANES)
    tn = _tile(n, 1024, LANES)
    tk = _tile(k, 512, LANES)

    def body(a_ref, b_ref, o_ref):
        @pl.when(pl.program_id(2) == 0)
        def _():
            o_ref[...] = jnp.zeros_like(o_ref)

        o_ref[...] += _dot_tn(a_ref[...], b_ref[...])

    return _pcall(
        body, grid=(m // tm, n // tn, k // tk),
        in_specs=[pl.BlockSpec((tk, tm), lambda i, j, kk: (kk, i)), pl.BlockSpec((tk, tn), lambda i, j, kk: (kk, j))],
        out_specs=pl.BlockSpec((tm, tn), lambda i, j, kk: (i, j)),
        out_shape=SDS((m, n), F32), compiler_params=_arb(3), name=name)(a, b)


def _all_gather(shard, name):
    r, cdim = shard.shape

    def body(x_ref, out_ref, send_sems, recv_sems, local_sem):
        x, y, c = lax.axis_index("x"), lax.axis_index("y"), lax.axis_index("c")
        me, sibling = (x, y, c), (x, y, 1 - c)
        chips = [(1 - x, y), (x, 1 - y), (1 - x, 1 - y)]

        def rows(px, py, pc):
            return out_ref.at[pl.ds((4 * px + 2 * py + pc) * r, r), :]

        def copy(k, block, to, src=None):
            return pltpu.make_async_remote_copy(
                src_ref=rows(*block) if src is None else src, dst_ref=rows(*block),
                send_sem=send_sems.at[k], recv_sem=recv_sems.at[k], device_id=to, device_id_type=MESH)

        mine = pltpu.make_async_copy(x_ref, rows(*me), local_sem)
        mine.start()
        first = [copy(0, me, sibling, src=x_ref)]
        first += [copy(1 + j, me, (*chip, c), src=x_ref) for j, chip in enumerate(chips)]
        for cp in first:
            cp.start()
        passed = [copy(4 + j, (*chip, c), sibling) for j, chip in enumerate(chips)]
        for j, chip in enumerate(chips):
            copy(1 + j, (*chip, c), me).wait_recv()
            passed[j].start()
        copy(0, sibling, me).wait_recv()
        for j, chip in enumerate(chips):
            copy(4 + j, (*chip, 1 - c), me).wait_recv()
        for cp in first + passed:
            cp.wait_send()
        mine.wait()

    return _pcall(
        body, out_shape=SDS((N_DEV * r, cdim), shard.dtype),
        in_specs=[pl.BlockSpec(memory_space=pl.ANY)], out_specs=pl.BlockSpec(memory_space=pl.ANY),
        scratch_shapes=[pltpu.SemaphoreType.DMA((7,)), pltpu.SemaphoreType.DMA((7,)), pltpu.SemaphoreType.DMA],
        name=name)(shard)


def _exchange(send, axis, name):
    def body(s_ref, r_ref, send_sem, recv_sem):
        pos = {a: lax.axis_index(a) for a in ("x", "y", "c")}
        pos[axis] = 1 - pos[axis]
        cp = pltpu.make_async_remote_copy(src_ref=s_ref, dst_ref=r_ref, send_sem=send_sem, recv_sem=recv_sem,
                                          device_id=(pos["x"], pos["y"], pos["c"]), device_id_type=MESH)
        cp.start()
        cp.wait()

    return _pcall(
        body, out_shape=SDS(send.shape, send.dtype),
        in_specs=[pl.BlockSpec(memory_space=pl.ANY)], out_specs=pl.BlockSpec(memory_space=pl.ANY),
        scratch_shapes=[pltpu.SemaphoreType.DMA, pltpu.SemaphoreType.DMA], name=name)(send)


def _rs_pick(g, pos, tr):
    _, r, cdim = g.shape

    def body(pos_ref, g_ref, o_ref):
        o_ref[...] = g_ref[...].astype(BF16)

    return _pcall(
        body, out_shape=SDS((4, r, cdim), BF16),
        grid_spec=pltpu.PrefetchScalarGridSpec(
            num_scalar_prefetch=1, grid=(4, r // tr),
            in_specs=[pl.BlockSpec((1, tr, cdim), lambda k, j, p: (2 * k + 1 - p[2], j, 0))],
            out_specs=pl.BlockSpec((1, tr, cdim), lambda k, j, p: (k, j, 0))),
        compiler_params=_arb(2), name="rs_pick")(pos, g)


def _rs_combine_c(g, recv, pos, tr):
    _, r, cdim = g.shape

    def body(pos_ref, gk_ref, rk_ref, gs_ref, rs_ref, keep_ref, send_ref):
        keep_ref[...] = gk_ref[...] + rk_ref[...].astype(F32)
        send_ref[...] = (gs_ref[...] + rs_ref[...].astype(F32)).astype(BF16)

    blk = (1, tr, cdim)
    return _pcall(
        body, out_shape=(SDS((2, r, cdim), F32), SDS((2, r, cdim), BF16)),
        grid_spec=pltpu.PrefetchScalarGridSpec(
            num_scalar_prefetch=1, grid=(2, r // tr),
            in_specs=[pl.BlockSpec(blk, lambda k, j, p: (4 * p[0] + 2 * k + p[2], j, 0)),
                      pl.BlockSpec(blk, lambda k, j, p: (2 * p[0] + k, j, 0)),
                      pl.BlockSpec(blk, lambda k, j, p: (4 * (1 - p[0]) + 2 * k + p[2], j, 0)),
                      pl.BlockSpec(blk, lambda k, j, p: (2 * (1 - p[0]) + k, j, 0))],
            out_specs=[pl.BlockSpec(blk, lambda k, j, p: (k, j, 0)), pl.BlockSpec(blk, lambda k, j, p: (k, j, 0))]),
        compiler_params=_arb(2), name="rs_combine_c")(pos, g, recv, g, recv)


def _rs_combine_x(keep, recv, pos, tr):
    _, r, cdim = keep.shape

    def body(pos_ref, kk_ref, rk_ref, ks_ref, rs_ref, keep_ref, send_ref):
        keep_ref[...] = kk_ref[...] + rk_ref[...].astype(F32)
        send_ref[...] = (ks_ref[...] + rs_ref[...].astype(F32)).astype(BF16)

    blk = (1, tr, cdim)
    mine = lambda j, p: (p[1], j, 0)
    other = lambda j, p: (1 - p[1], j, 0)
    first = lambda j, p: (0, j, 0)
    return _pcall(
        body, out_shape=(SDS((1, r, cdim), F32), SDS((1, r, cdim), BF16)),
        grid_spec=pltpu.PrefetchScalarGridSpec(
            num_scalar_prefetch=1, grid=(r // tr,),
            in_specs=[pl.BlockSpec(blk, mine), pl.BlockSpec(blk, mine), pl.BlockSpec(blk, other), pl.BlockSpec(blk, other)],
            out_specs=[pl.BlockSpec(blk, first), pl.BlockSpec(blk, first)]),
        compiler_params=_arb(1), name="rs_combine_x")(pos, keep, recv, keep, recv)


def _adamw_big(keep, recv, w, m, v, tr):
    r, cdim = w.shape

    def body(k_ref, r_ref, w_ref, m_ref, v_ref, g_ref, d_ref, nm_ref, nv_ref):
        g = k_ref[...] + r_ref[...].astype(F32)
        g_ref[...] = g
        d_ref[...], nm_ref[...], nv_ref[...] = _adamw(w_ref[...], g, m_ref[...], v_ref[...])

    blk = pl.BlockSpec((tr, cdim), lambda j: (j, 0))
    out = SDS((r, cdim), F32)
    return _pcall(body, grid=(r // tr,), in_specs=[blk] * 5, out_specs=[blk] * 4, out_shape=(out,) * 4,
                  compiler_params=_arb(1), name="adamw_big")(keep, recv, w, m, v)


def _adamw_small(gathered, seg, params, conv_rows):
    rows = gathered.shape[0] // N_DEV
    names = list(params)
    c0, cn = conv_rows

    def body(*refs):
        gat_ref = refs[0]
        ins = refs[1:1 + 3 * len(names)]
        outs = refs[1 + 3 * len(names):]

        def total(r0, rn):
            tot = gat_ref[r0:r0 + rn, :]
            for dev in range(1, N_DEV):
                tot = tot + gat_ref[dev * rows + r0:dev * rows + r0 + rn, :]
            return tot

        for k, nm in enumerate(names):
            g = total(*seg[nm])
            w_ref, m_ref, v_ref = ins[3 * k:3 * k + 3]
            g_ref, d_ref, nm_ref, nv_ref = outs[4 * k:4 * k + 4]
            g_ref[...] = g
            d_ref[...], nm_ref[...], nv_ref[...] = _adamw(w_ref[...], g, m_ref[...], v_ref[...])
        outs[-2][...] = total(c0, cn)
        outs[-1][...] = total(*seg["loss"])

    flat_in = [a for nm in names for a in params[nm]]
    out_shape = []
    for nm in names:
        out_shape += [SDS(params[nm][0].shape, F32)] * 4
    out_shape += [SDS((cn, LANES), F32), SDS((seg["loss"][1], LANES), F32)]
    res = _pcall(body, out_shape=tuple(out_shape), name="adamw_small")(gathered, *flat_in)
    per = {nm: res[4 * k:4 * k + 4] for k, nm in enumerate(names)}
    return per, res[-2], res[-1]


def _adamw_one(w, g, m, v, name):
    def body(w_ref, g_ref, m_ref, v_ref, d_ref, nm_ref, nv_ref):
        d_ref[...], nm_ref[...], nv_ref[...] = _adamw(w_ref[...], g_ref[...], m_ref[...], v_ref[...])

    return _pcall(body, out_shape=(SDS(w.shape, F32),) * 3, name=name)(w, g, m, v)


def _local_grads(x, mem, target, wf, sp):
    s, d = x.shape
    tm = min(TOKEN_TILE, s)
    dff = wf["w_down"].shape[0]
    tf = _tile(dff, 512, LANES)
    bt = sp["b_spatial"].T

    memn, kv = _kv_forward(mem, sp["ln_mem_g"], wf["w_kv"])
    xn1, h = _in_forward(x, sp["ln_mix_g"], wf["w_in"], tm)
    ycat, x1 = _mix_forward(h, x, sp["sgu_ln_g"], sp["sgu_ln_b"], sp["w_spatial"], bt, sp["conv_w"],
                            sp["grp_norm_a"], sp["grp_norm_b"], wf["w_out"], tm)
    xn2, q, o, x2 = _attn_forward(x1, sp["ln_attn_g"], wf["w_q"], kv, wf["w_o"], tm)
    xn3, gate, up, x3 = _ffn_forward(x2, sp["ln_ffn_g"], wf["w_gate_up"], wf["w_down"], tm, tf)

    loss, d_lnf, d_lnffn, act, dgate, dup, dx3b, dx2 = _ffn_backward(
        x3, target, sp["ln_final_g"], x2, sp["ln_ffn_g"], gate, up, wf["w_gate_up"], wf["w_down"], tm, tf)
    gw = {}
    gw["w_down"] = _wgrad(act, dx3b, "wgrad_down")
    gw["w_gate_up"] = jnp.concatenate([_wgrad(xn3, dgate, "wgrad_gate"), _wgrad(xn3, dup, "wgrad_up")], axis=1)
    dx2b, dq, dx1, dkv, d_lnattn = _attn_backward(dx2, x1, sp["ln_attn_g"], q, kv, wf["w_q"], wf["w_o"], tm)
    gw["w_o"] = _wgrad(o, dx2b, "wgrad_o")
    gw["w_q"] = _wgrad(xn2, dq, "wgrad_q")
    gw["w_kv"], d_lnmem = _kv_backward(dkv, memn, mem, sp["ln_mem_g"], wf["w_kv"])
    (dx1b, dh, dx, d_ga, d_gb, d_cw, d_lng, d_lnb, d_wsp, d_bs, d_lnmix) = _mix_backward(
        dx1, x, sp["ln_mix_g"], h, sp["sgu_ln_g"], sp["sgu_ln_b"], sp["w_spatial"], bt, sp["conv_w"],
        sp["grp_norm_a"], sp["grp_norm_b"], wf["w_out"], wf["w_in"], tm)
    gw["w_out"] = _wgrad(ycat, dx1b, "wgrad_out")
    gw["w_in"] = _wgrad(xn1, dh, "wgrad_in")
    gs = {"ln_mix_g": d_lnmix, "sgu_ln_g": d_lng, "sgu_ln_b": d_lnb, "w_spatial": d_wsp, "b_spatial": _bias_grad(d_bs),
          "conv_w": d_cw[:3], "grp_norm_a": d_ga, "grp_norm_b": d_gb, "ln_attn_g": d_lnattn, "ln_mem_g": d_lnmem,
          "ln_ffn_g": d_lnffn, "ln_final_g": d_lnf}
    return loss, dx, gw, gs


def _rows128(a):
    return a.reshape(-1, LANES)


def _shard_rows(shape):
    n = 1
    for k in shape:
        n *= k
    assert n % (WIRE_COLS * 16) == 0, shape
    return n // WIRE_COLS


def _full_from_gathered(blocks, name, shard_shape):
    r, c = shard_shape
    w = blocks.reshape(N_DEV, r, c)
    if name in COL_SHARDED:
        return w.transpose(1, 0, 2).reshape(r, N_DEV * c)
    return w.reshape(N_DEV * r, c)


def _to_dest_major(gfull, name, shard_shape):
    r, c = shard_shape
    if name in COL_SHARDED:
        g = gfull.reshape(r, N_DEV, c).transpose(1, 0, 2)
    else:
        g = gfull.reshape(N_DEV, r, c)
    return g.reshape(N_DEV, -1, WIRE_COLS)


def kernel(x, mem, ln_mix_g, w_in, sgu_ln_g, sgu_ln_b, w_spatial, b_spatial, conv_w, grp_norm_a, grp_norm_b, w_out, ln_attn_g, ln_mem_g, w_q, w_kv, w_o, ln_ffn_g, w_gate_up, w_down, ln_final_g, loss_target, m_ln_mix_g, m_w_in, m_sgu_ln_g, m_sgu_ln_b, m_w_spatial, m_b_spatial, m_conv_w, m_grp_norm_a, m_grp_norm_b, m_w_out, m_ln_attn_g, m_ln_mem_g, m_w_q, m_w_kv, m_w_o, m_ln_ffn_g, m_w_gate_up, m_w_down, m_ln_final_g, v_ln_mix_g, v_w_in, v_sgu_ln_g, v_sgu_ln_b, v_w_spatial, v_b_spatial, v_conv_w, v_grp_norm_a, v_grp_norm_b, v_w_out, v_ln_attn_g, v_ln_mem_g, v_w_q, v_w_kv, v_w_o, v_ln_ffn_g, v_w_gate_up, v_w_down, v_ln_final_g):
    order = ["ln_mix_g", "w_in", "sgu_ln_g", "sgu_ln_b", "w_spatial", "b_spatial", "conv_w", "grp_norm_a", "grp_norm_b",
             "w_out", "ln_attn_g", "ln_mem_g", "w_q", "w_kv", "w_o", "ln_ffn_g", "w_gate_up", "w_down", "ln_final_g"]
    W = dict(ln_mix_g=ln_mix_g, w_in=w_in, sgu_ln_g=sgu_ln_g, sgu_ln_b=sgu_ln_b, w_spatial=w_spatial, b_spatial=b_spatial,
             conv_w=conv_w, grp_norm_a=grp_norm_a, grp_norm_b=grp_norm_b, w_out=w_out, ln_attn_g=ln_attn_g,
             ln_mem_g=ln_mem_g, w_q=w_q, w_kv=w_kv, w_o=w_o, ln_ffn_g=ln_ffn_g, w_gate_up=w_gate_up, w_down=w_down,
             ln_final_g=ln_final_g)
    M = dict(ln_mix_g=m_ln_mix_g, w_in=m_w_in, sgu_ln_g=m_sgu_ln_g, sgu_ln_b=m_sgu_ln_b, w_spatial=m_w_spatial,
             b_spatial=m_b_spatial, conv_w=m_conv_w, grp_norm_a=m_grp_norm_a, grp_norm_b=m_grp_norm_b, w_out=m_w_out,
             ln_attn_g=m_ln_attn_g, ln_mem_g=m_ln_mem_g, w_q=m_w_q, w_kv=m_w_kv, w_o=m_w_o, ln_ffn_g=m_ln_ffn_g,
             w_gate_up=m_w_gate_up, w_down=m_w_down, ln_final_g=m_ln_final_g)
    V = dict(ln_mix_g=v_ln_mix_g, w_in=v_w_in, sgu_ln_g=v_sgu_ln_g, sgu_ln_b=v_sgu_ln_b, w_spatial=v_w_spatial,
             b_spatial=v_b_spatial, conv_w=v_conv_w, grp_norm_a=v_grp_norm_a, grp_norm_b=v_grp_norm_b, w_out=v_w_out,
             ln_attn_g=v_ln_attn_g, ln_mem_g=v_ln_mem_g, w_q=v_w_q, w_kv=v_w_kv, w_o=v_w_o, ln_ffn_g=v_ln_ffn_g,
             w_gate_up=v_w_gate_up, w_down=v_w_down, ln_final_g=v_ln_final_g)

    s, d = x.shape[1], x.shape[2]
    bw = conv_w.shape[1] * N_DEV
    pos = jnp.stack([lax.axis_index("x"), lax.axis_index("y"), lax.axis_index("c")]).astype(jnp.int32)
    me = 4 * pos[0] + 2 * pos[1] + pos[2]

    big_rows = {nm: _shard_rows(W[nm].shape) for nm in BIG}
    conv_bits = lax.bitcast_convert_type(conv_w, BF16).reshape(-1)
    conv_pad = jnp.zeros((16 * WIRE_COLS,), BF16).at[:conv_bits.shape[0]].set(conv_bits).reshape(16, WIRE_COLS)
    wire = jnp.concatenate([W[nm].astype(BF16).reshape(-1, WIRE_COLS) for nm in BIG] + [conv_pad], axis=0)
    r_all = wire.shape[0]
    gathered = _all_gather(wire, "gather_weights").reshape(N_DEV, r_all, WIRE_COLS)
    wf, off = {}, 0
    for nm in BIG:
        wf[nm] = _full_from_gathered(gathered[:, off:off + big_rows[nm]], nm, W[nm].shape)
        off += big_rows[nm]
    conv_all = gathered[:, off, :2 * 3 * conv_w.shape[1]].reshape(N_DEV, 3, conv_w.shape[1], 2)
    conv_full = lax.bitcast_convert_type(conv_all, F32).transpose(1, 0, 2).reshape(3, bw)

    sp = {nm: (W[nm].reshape(1, -1) if W[nm].ndim == 1 else W[nm]) for nm in SMALL}
    sp["conv_w"] = conv_full
    loss_tile, grad_x, gw, gs = _local_grads(x[0], mem[0], loss_target[0], wf, sp)

    g8 = jnp.concatenate([_to_dest_major(gw[nm], nm, W[nm].shape) for nm in BIG], axis=1)
    r_big = g8.shape[1]
    tr = _tile(r_big, 512, 16)
    recv_c = _exchange(_rs_pick(g8, pos, tr), "c", "exchange_c")
    keep_x, send_x = _rs_combine_c(g8, recv_c, pos, tr)
    recv_x = _exchange(send_x, "x", "exchange_x")
    keep_y, send_y = _rs_combine_x(keep_x, recv_x, pos, tr)
    recv_y = _exchange(send_y, "y", "exchange_y")
    flat = lambda src: jnp.concatenate([src[nm].reshape(-1, WIRE_COLS) for nm in BIG], axis=0)
    g_big, d_big, m_big, v_big = _adamw_big(keep_y[0], recv_y[0], flat(W), flat(M), flat(V), tr)

    out = {}
    off = 0
    for nm in BIG:
        rs = slice(off, off + big_rows[nm])
        out[nm] = tuple(a[rs].reshape(W[nm].shape) for a in (g_big, d_big, m_big, v_big))
        off += big_rows[nm]

    seg, pieces, row = {}, [], 0
    for nm in SMALL + ("conv_w", "loss"):
        piece = loss_tile if nm == "loss" else _rows128(gs[nm])
        rn = _round_up(piece.shape[0], SUB)
        pieces.append(jnp.pad(piece, ((0, rn - piece.shape[0]), (0, 0))))
        seg[nm] = (row, piece.shape[0])
        row += rn
    packed = jnp.concatenate(pieces, axis=0)
    small_all = _all_gather(packed, "gather_small")
    params = {nm: (_rows128(W[nm]), _rows128(M[nm]), _rows128(V[nm])) for nm in SMALL}
    per, conv_g_rows, loss_sum = _adamw_small(small_all, seg, params, seg["conv_w"])
    for nm in SMALL:
        out[nm] = tuple(a.reshape(W[nm].shape) for a in per[nm])
    conv_g = lax.dynamic_slice_in_dim(conv_g_rows.reshape(3, bw), me * conv_w.shape[1], conv_w.shape[1], axis=1)
    out["conv_w"] = (conv_g,) + tuple(_adamw_one(conv_w, conv_g, m_conv_w, v_conv_w, "adamw_conv"))

    loss = loss_sum[0, 0]
    res = [loss, grad_x[None]]
    for k in range(4):
        res += [out[nm][k] for nm in order]
    return tuple(res)
```

```python
import functools

import jax
import jax.numpy as jnp
from jax import lax
from jax.experimental import pallas as pl
from jax.experimental.pallas import tpu as pltpu

F32 = jnp.float32
BF16 = jnp.bfloat16
SDS = jax.ShapeDtypeStruct
MESH = pl.DeviceIdType.MESH

EPS = 1e-6
N_DEV = 8
HEADS = 4
CHUNK = 128
HALO = 16
SUB = 8
LANES = 128
TOKEN_TILE = 512

ADAM_LR = 0.001
ADAM_B1 = 0.9
ADAM_B2 = 0.999
ADAM_EPS = 1e-08
ADAM_WD = 0.01
ADAM_STEP = 10

BIG = ("w_in", "w_out", "w_q", "w_kv", "w_o", "w_gate_up", "w_down")
SMALL = ("ln_mix_g", "sgu_ln_g", "sgu_ln_b", "w_spatial", "b_spatial", "grp_norm_a", "grp_norm_b",
         "ln_attn_g", "ln_mem_g", "ln_ffn_g", "ln_final_g")


def _pcall(body, **kw):
    return pl.pallas_call(body, **kw)


def _arb(n):
    return pltpu.CompilerParams(dimension_semantics=("arbitrary",) * n)


def _tile(n, target, mult):
    best = None
    for t in range(mult, min(n, target) + 1, mult):
        if n % t == 0:
            best = t
    return n if best is None else best


def _round_up(n, m):
    return (n + m - 1) // m * m


def _dot(a, b):
    return jnp.dot(a, b, preferred_element_type=F32)


def _dot_nt(a, b):
    return lax.dot_general(a, b, (((1,), (1,)), ((), ())), preferred_element_type=F32)


def _dot_tn(a, b):
    return lax.dot_general(a, b, (((0,), (0,)), ((), ())), preferred_element_type=F32)


def _rstd(x):
    return lax.rsqrt(jnp.mean(x * x, axis=-1, keepdims=True) + EPS)


def _rms_bwd(dy, x, r, g):
    gdy = dy * g
    proj = jnp.sum(gdy * x, axis=-1, keepdims=True) * (1.0 / x.shape[-1])
    dx = r * gdy - x * (r * r * r) * proj
    dg = jnp.sum(dy * (x * r), axis=0, keepdims=True)
    return dx, dg


_GELU_C = 0.7978845608028654
_GELU_A = 0.044715


def _gelu(x):
    t = jnp.tanh(_GELU_C * (x + _GELU_A * x * x * x))
    return 0.5 * x * (1.0 + t), t


def _gelu_grad(x, t):
    return 0.5 * (1.0 + t) + 0.5 * x * (1.0 - t * t) * (_GELU_C * (1.0 + 3.0 * _GELU_A * x * x))


def _sigmoid(x):
    return 1.0 / (1.0 + jnp.exp(-x))


def _softmax(s):
    m = jnp.max(s, axis=-1, keepdims=True)
    e = jnp.exp(s - m)
    return e / jnp.sum(e, axis=-1, keepdims=True)


def _adamw(w, g, m, v):
    m = ADAM_B1 * m + (1.0 - ADAM_B1) * g
    v = ADAM_B2 * v + (1.0 - ADAM_B2) * (g * g)
    m_hat = m / (1.0 - ADAM_B1 ** ADAM_STEP)
    v_hat = v / (1.0 - ADAM_B2 ** ADAM_STEP)
    delta = -ADAM_LR * (m_hat / (jnp.sqrt(v_hat) + ADAM_EPS) + ADAM_WD * w)
    return delta, m, v


def _tril_mask():
    t = lax.broadcasted_iota(jnp.int32, (CHUNK, CHUNK), 0)
    s = lax.broadcasted_iota(jnp.int32, (CHUNK, CHUNK), 1)
    return (s <= t).astype(F32)


def _sgu_forward(ha, lng, lnb, wm, bt, mixed_s):
    aw = ha.shape[1] // 2
    hd = aw // HEADS
    a, th = _gelu(ha)
    u = a[:, :aw]
    v = a[:, aw:]
    mu = jnp.mean(v, axis=-1, keepdims=True)
    vc = v - mu
    rl = lax.rsqrt(jnp.mean(vc * vc, axis=-1, keepdims=True) + EPS)
    xhat = vc * rl
    vln = (xhat * lng + lnb).astype(BF16)
    for n in range(ha.shape[0] // CHUNK):
        rows = slice(n * CHUNK, (n + 1) * CHUNK)
        for h in range(HEADS):
            cols = slice(h * hd, (h + 1) * hd)
            mixed_s[rows, cols] = _dot(wm[h], vln[rows, cols]) + bt[:, h:h + 1]
    return th, u, xhat, rl, vln


def _conv_taps(zext):
    return pltpu.roll(zext, 2, 0), pltpu.roll(zext, 1, 0)


def _kv_forward(mem, g_mem, w_kv):
    ml, d = mem.shape
    xd = w_kv.shape[2]

    def body(mem_ref, g_ref, w_ref, memn_ref, kv_ref):
        x = mem_ref[...]
        memn = (x * _rstd(x) * g_ref[...]).astype(BF16)
        memn_ref[...] = memn
        for j in range(2 * HEADS):
            kv_ref[j] = _dot(memn, w_ref[j]).astype(BF16)

    return _pcall(body, out_shape=(SDS((ml, d), BF16), SDS((2 * HEADS, ml, xd), BF16)), name="kv_forward")(mem, g_mem, w_kv)


def _in_forward(x, g, w_in, tm):
    s, d = x.shape
    n_in = w_in.shape[1]

    def body(x_ref, g_ref, w_ref, xn_ref, h_ref):
        xv = x_ref[...]
        xn = (xv * _rstd(xv) * g_ref[...]).astype(BF16)
        xn_ref[...] = xn
        h_ref[...] = _dot(xn, w_ref[...])

    return _pcall(
        body, grid=(s // tm,),
        in_specs=[pl.BlockSpec((tm, d), lambda i: (i, 0)), pl.BlockSpec((1, d), lambda i: (0, 0)),
                  pl.BlockSpec((d, n_in), lambda i: (0, 0))],
        out_specs=[pl.BlockSpec((tm, d), lambda i: (i, 0)), pl.BlockSpec((tm, n_in), lambda i: (i, 0))],
        out_shape=(SDS((s, d), BF16), SDS((s, n_in), F32)),
        compiler_params=_arb(1), name="in_forward")(x, g, w_in)


def _mix_forward(h, x, lng, lnb, w_sp, bt, conv_w, ga, gb, w_out, tm):
    s, d = x.shape
    n_in = h.shape[1]
    aw = lng.shape[1]
    bw = d - aw
    in_a = 2 * aw
    hb_blocks = tm // HALO

    def body(h_ref, hprev_ref, x_ref, lng_ref, lnb_ref, wsp_ref, bt_ref, cw_ref, ga_ref, gb_ref, wout_ref,
             ycat_ref, x1_ref, mixed_s):
        i = pl.program_id(0)
        mask = _tril_mask()
        wm = [(wsp_ref[hh] * mask).astype(BF16) for hh in range(HEADS)]
        hv = h_ref[...]
        _, u, _, _, _ = _sgu_forward(hv[:, :in_a], lng_ref[...], lnb_ref[...], wm, bt_ref[...], mixed_s)
        sg = u * mixed_s[...]
        ycat_ref[:, :aw] = (sg * _rstd(sg) * ga_ref[...]).astype(BF16)

        gate_b = hv[:, in_a:in_a + bw]
        z = hv[:, in_a + bw:in_a + 2 * bw] * hv[:, in_a + 2 * bw:]
        hp = hprev_ref[...]
        zp = hp[:, in_a + bw:in_a + 2 * bw] * hp[:, in_a + 2 * bw:]
        zp = jnp.where(i == 0, 0.0, zp)
        zext = jnp.concatenate([zp, z], axis=0)
        z2, z1 = _conv_taps(zext)
        cw = cw_ref[...]
        conv = cw[0:1] * z2[HALO:] + cw[1:2] * z1[HALO:] + cw[2:3] * z
        sc = gate_b * conv
        ycat_ref[:, aw:] = (sc * _rstd(sc) * gb_ref[...]).astype(BF16)
        x1_ref[...] = x_ref[...] + _dot(ycat_ref[...], wout_ref[...])

    full = lambda shape: pl.BlockSpec(shape, lambda i: (0,) * len(shape))
    return _pcall(
        body, grid=(s // tm,),
        in_specs=[pl.BlockSpec((tm, n_in), lambda i: (i, 0)),
                  pl.BlockSpec((HALO, n_in), lambda i: (jnp.maximum(i * hb_blocks - 1, 0), 0)),
                  pl.BlockSpec((tm, d), lambda i: (i, 0)),
                  full((1, aw)), full((1, aw)), full((HEADS, CHUNK, CHUNK)), full((CHUNK, HEADS)),
                  full((3, bw)), full((1, aw)), full((1, bw)), full((d, d))],
        out_specs=[pl.BlockSpec((tm, d), lambda i: (i, 0)), pl.BlockSpec((tm, d), lambda i: (i, 0))],
        out_shape=(SDS((s, d), BF16), SDS((s, d), F32)),
        scratch_shapes=[pltpu.VMEM((tm, aw), F32)],
        compiler_params=_arb(1), name="mix_forward")(h, h, x, lng, lnb, w_sp, bt, conv_w, ga, gb, w_out)


def _attn_forward(x1, g, w_q, kv, w_o, tm):
    s, d = x1.shape
    _, ml, xd = kv.shape
    scale = xd ** -0.5

    def body(x1_ref, g_ref, wq_ref, kv_ref, wo_ref, xn_ref, q_ref, o_ref, x2_ref):
        xv = x1_ref[...]
        xn = (xv * _rstd(xv) * g_ref[...]).astype(BF16)
        xn_ref[...] = xn
        q_ref[...] = _dot(xn, wq_ref[...]).astype(BF16)
        for hh in range(HEADS):
            cols = slice(hh * xd, (hh + 1) * xd)
            p = _softmax(_dot_nt(q_ref[:, cols], kv_ref[hh]) * scale)
            o_ref[:, cols] = _dot(p.astype(BF16), kv_ref[HEADS + hh]).astype(BF16)
        x2_ref[...] = xv + _dot(o_ref[...], wo_ref[...])

    tok = pl.BlockSpec((tm, d), lambda i: (i, 0))
    return _pcall(
        body, grid=(s // tm,),
        in_specs=[tok, pl.BlockSpec((1, d), lambda i: (0, 0)), pl.BlockSpec((d, d), lambda i: (0, 0)),
                  pl.BlockSpec((2 * HEADS, ml, xd), lambda i: (0, 0, 0)), pl.BlockSpec((d, d), lambda i: (0, 0))],
        out_specs=[tok, tok, tok, tok],
        out_shape=(SDS((s, d), BF16), SDS((s, d), BF16), SDS((s, d), BF16), SDS((s, d), F32)),
        compiler_params=_arb(1), name="attn_forward")(x1, g, w_q, kv, w_o)


def _ffn_forward(x2, g, w_gu, w_down, tm):
    s, d = x2.shape
    _, nf, _, tf = w_gu.shape

    def body(x2_ref, g_ref, wgu_ref, wd_ref, xn_ref, gu_ref, x3_ref):
        f = pl.program_id(1)

        @pl.when(f == 0)
        def _():
            xv = x2_ref[...]
            xn_ref[...] = (xv * _rstd(xv) * g_ref[...]).astype(BF16)
            x3_ref[...] = xv

        xn = xn_ref[...]
        gate = _dot(xn, wgu_ref[0])
        up = _dot(xn, wgu_ref[1])
        gu_ref[0] = gate.astype(BF16)
        gu_ref[1] = up.astype(BF16)
        act = (gate * _sigmoid(gate) * up).astype(BF16)
        x3_ref[...] += _dot(act, wd_ref[...])

    tok = pl.BlockSpec((tm, d), lambda i, f: (i, 0))
    return _pcall(
        body, grid=(s // tm, nf),
        in_specs=[tok, pl.BlockSpec((1, d), lambda i, f: (0, 0)),
                  pl.BlockSpec((2, None, d, tf), lambda i, f: (0, f, 0, 0)),
                  pl.BlockSpec((tf, d), lambda i, f: (f, 0))],
        out_specs=[tok, pl.BlockSpec((2, None, tm, tf), lambda i, f: (0, f, i, 0)), tok],
        out_shape=(SDS((s, d), BF16), SDS((2, nf, s, tf), BF16), SDS((s, d), F32)),
        compiler_params=_arb(2), name="ffn_forward")(x2, g, w_gu, w_down)


def _ffn_backward(x3, target, g_final, x2, g_ffn, gu, w_gu, w_down, tm):
    s, d = x3.shape
    _, nf, _, tf = w_gu.shape

    def body(x3_ref, tgt_ref, gf_ref, x2_ref, g2_ref, gu_ref, wgu_ref, wd_ref,
             loss_ref, dgf_ref, dg2_ref, act_ref, dgu_ref, dx3b_ref, dx2_ref, dx3_s, acc_s):
        i = pl.program_id(0)
        f = pl.program_id(1)

        @pl.when((i == 0) & (f == 0))
        def _():
            loss_ref[...] = jnp.zeros_like(loss_ref)
            dgf_ref[...] = jnp.zeros_like(dgf_ref)
            dg2_ref[...] = jnp.zeros_like(dg2_ref)

        @pl.when(f == 0)
        def _():
            xv = x3_ref[...]
            r = _rstd(xv)
            diff = xv * r * gf_ref[...] - tgt_ref[...]
            loss_ref[...] += 0.5 * jnp.sum(jnp.sum(diff * diff, axis=-1, keepdims=True), axis=0, keepdims=True) * (1.0 / d)
            dx3, dgf = _rms_bwd(diff * (1.0 / d), xv, r, gf_ref[...])
            dgf_ref[...] += dgf
            dx3_s[...] = dx3
            dx3b_ref[...] = dx3.astype(BF16)
            acc_s[...] = jnp.zeros_like(acc_s)

        dact = _dot_nt(dx3b_ref[...], wd_ref[...])
        gv = gu_ref[0].astype(F32)
        uv = gu_ref[1].astype(F32)
        sg = _sigmoid(gv)
        silu = gv * sg
        act_ref[...] = (silu * uv).astype(BF16)
        dgate = (dact * uv * (sg * (1.0 + gv * (1.0 - sg)))).astype(BF16)
        dup = (dact * silu).astype(BF16)
        dgu_ref[0] = dgate
        dgu_ref[1] = dup
        acc_s[...] += _dot_nt(dgate, wgu_ref[0]) + _dot_nt(dup, wgu_ref[1])

        @pl.when(f == nf - 1)
        def _():
            xv = x2_ref[...]
            dxn, dg2 = _rms_bwd(acc_s[...], xv, _rstd(xv), g2_ref[...])
            dg2_ref[...] += dg2
            dx2_ref[...] = dx3_s[...] + dxn

    tok = pl.BlockSpec((tm, d), lambda i, f: (i, 0))
    pair = pl.BlockSpec((2, None, tm, tf), lambda i, f: (0, f, i, 0))
    vec = pl.BlockSpec((1, d), lambda i, f: (0, 0))
    return _pcall(
        body, grid=(s // tm, nf),
        in_specs=[tok, tok, vec, tok, vec, pair,
                  pl.BlockSpec((2, None, d, tf), lambda i, f: (0, f, 0, 0)),
                  pl.BlockSpec((tf, d), lambda i, f: (f, 0))],
        out_specs=[pl.BlockSpec((SUB, LANES), lambda i, f: (0, 0)), vec, vec,
                   pl.BlockSpec((None, tm, tf), lambda i, f: (f, i, 0)), pair, tok, tok],
        out_shape=(SDS((SUB, LANES), F32), SDS((1, d), F32), SDS((1, d), F32), SDS((nf, s, tf), BF16),
                   SDS((2, nf, s, tf), BF16), SDS((s, d), BF16), SDS((s, d), F32)),
        scratch_shapes=[pltpu.VMEM((tm, d), F32), pltpu.VMEM((tm, d), F32)],
        compiler_params=_arb(2), name="ffn_backward")(x3, target, g_final, x2, g_ffn, gu, w_gu, w_down)


def _attn_backward(dx2, x1, g, q, kv, w_q, w_o, tm):
    s, d = x1.shape
    _, ml, xd = kv.shape
    scale = xd ** -0.5

    def body(dx2_ref, x1_ref, g_ref, q_ref, kv_ref, wq_ref, wo_ref,
             dx2b_ref, dq_ref, dx1_ref, dkv_ref, dg_ref, do_s):
        i = pl.program_id(0)

        @pl.when(i == 0)
        def _():
            dkv_ref[...] = jnp.zeros_like(dkv_ref)
            dg_ref[...] = jnp.zeros_like(dg_ref)

        dx2 = dx2_ref[...]
        dx2b_ref[...] = dx2.astype(BF16)
        do_s[...] = _dot_nt(dx2b_ref[...], wo_ref[...]).astype(BF16)
        for hh in range(HEADS):
            kc = slice(hh * xd, (hh + 1) * xd)
            qh = q_ref[:, kc]
            kh = kv_ref[hh]
            doh = do_s[:, kc]
            p = _softmax(_dot_nt(qh, kh) * scale)
            dp = _dot_nt(doh, kv_ref[HEADS + hh])
            dkv_ref[HEADS + hh] += _dot_tn(p.astype(BF16), doh)
            ds = (p * (dp - jnp.sum(dp * p, axis=-1, keepdims=True)) * scale).astype(BF16)
            dq_ref[:, kc] = _dot(ds, kh).astype(BF16)
            dkv_ref[hh] += _dot_tn(ds, qh)
        dxn = _dot_nt(dq_ref[...], wq_ref[...])
        xv = x1_ref[...]
        dx, dg = _rms_bwd(dxn, xv, _rstd(xv), g_ref[...])
        dg_ref[...] += dg
        dx1_ref[...] = dx2 + dx

    tok = pl.BlockSpec((tm, d), lambda i: (i, 0))
    vec = pl.BlockSpec((1, d), lambda i: (0, 0))
    sq = pl.BlockSpec((d, d), lambda i: (0, 0))
    kvs = pl.BlockSpec((2 * HEADS, ml, xd), lambda i: (0, 0, 0))
    return _pcall(
        body, grid=(s // tm,),
        in_specs=[tok, tok, vec, tok, kvs, sq, sq],
        out_specs=[tok, tok, tok, kvs, vec],
        out_shape=(SDS((s, d), BF16), SDS((s, d), BF16), SDS((s, d), F32), SDS((2 * HEADS, ml, xd), F32), SDS((1, d), F32)),
        scratch_shapes=[pltpu.VMEM((tm, d), BF16)],
        compiler_params=_arb(1), name="attn_backward")(dx2, x1, g, q, kv, w_q, w_o)


def _kv_backward(dkv, memn, mem, g_mem, w_kv):
    ml, d = mem.shape
    xd = w_kv.shape[2]

    def body(dkv_ref, memn_ref, mem_ref, g_ref, w_ref, dw_ref, dg_ref):
        dmemn = jnp.zeros((ml, d), F32)
        for j in range(2 * HEADS):
            dkvb = dkv_ref[j].astype(BF16)
            dw_ref[j] = _dot_tn(memn_ref[...], dkvb)
            dmemn = dmemn + _dot_nt(dkvb, w_ref[j])
        x = mem_ref[...]
        dg_ref[...] = jnp.sum(dmemn * (x * _rstd(x)), axis=0, keepdims=True)

    return _pcall(body, out_shape=(SDS((2 * HEADS, d, xd), F32), SDS((1, d), F32)), name="kv_backward")(dkv, memn, mem, g_mem, w_kv)


def _mix_backward(dx1, x, g_mix, h, lng, lnb, w_sp, bt, conv_w, ga, gb, w_out, w_in, tm):
    s, d = x.shape
    n_in = h.shape[1]
    aw = lng.shape[1]
    bw = d - aw
    hd = aw // HEADS
    in_a = 2 * aw
    hb_blocks = tm // HALO
    last_blk = s // HALO - 1
    nt = s // tm
    te = tm + HALO
    tee = tm + 2 * HALO

    def body(dx1_ref, dx1n_ref, x_ref, gm_ref, h_ref, hp_ref, hn_ref, lng_ref, lnb_ref, wsp_ref, bt_ref, cw_ref,
             ga_ref, gb_ref, wout_ref, win_ref,
             dx1b_ref, dh_ref, dx_ref, dga_ref, dgb_ref, dcw_ref, dlng_ref, dlnb_ref, dwsp_ref, dbs_ref, dgm_ref,
             mixed_s, dvln_s):
        i = pl.program_id(0)

        @pl.when(i == 0)
        def _():
            for ref in (dga_ref, dgb_ref, dcw_ref, dlng_ref, dlnb_ref, dwsp_ref, dbs_ref, dgm_ref):
                ref[...] = jnp.zeros_like(ref)

        mask = _tril_mask()
        wm = [(wsp_ref[hh] * mask).astype(BF16) for hh in range(HEADS)]
        hv = h_ref[...]
        dx1 = dx1_ref[...]
        dx1b_ref[...] = dx1.astype(BF16)
        dx1e = jnp.concatenate([dx1, dx1n_ref[...]], axis=0).astype(BF16)
        dycat = _dot_nt(dx1e, wout_ref[...])

        hbe = jnp.concatenate([hp_ref[:, in_a:], hv[:, in_a:], hn_ref[:, in_a:]], axis=0)
        row = lax.broadcasted_iota(jnp.int32, (tee, 1), 0)
        zext = hbe[:, bw:2 * bw] * hbe[:, 2 * bw:]
        zext = jnp.where((i == 0) & (row < HALO), 0.0, zext)
        z2e, z1e = _conv_taps(zext)
        cw = cw_ref[...]
        conv_e = (cw[0:1] * z2e + cw[1:2] * z1e + cw[2:3] * zext)[HALO:]
        gate_b_e = hbe[HALO:, :bw]
        sc_e = gate_b_e * conv_e
        rb = _rstd(sc_e)
        dyb = dycat[:, aw:]
        gdy = dyb * gb_ref[...]
        dsc_e = rb * gdy - sc_e * (rb * rb * rb) * (jnp.sum(gdy * sc_e, axis=-1, keepdims=True) * (1.0 / bw))
        dgb_ref[...] += jnp.sum((dyb * (sc_e * rb))[:tm], axis=0, keepdims=True)
        dconv_e = dsc_e * gate_b_e
        dconv_e = jnp.where((i == nt - 1) & (row[:te] >= tm), 0.0, dconv_e)
        dconv = dconv_e[:tm]
        dc1 = pltpu.roll(dconv_e, te - 1, 0)[:tm]
        dc2 = pltpu.roll(dconv_e, te - 2, 0)[:tm]
        dz = cw[2:3] * dconv + cw[1:2] * dc1 + cw[0:1] * dc2
        z = zext[HALO:HALO + tm]
        z1 = z1e[HALO:HALO + tm]
        z2 = z2e[HALO:HALO + tm]
        dcw_ref[0:1, :] += jnp.sum(dconv * z2, axis=0, keepdims=True)
        dcw_ref[1:2, :] += jnp.sum(dconv * z1, axis=0, keepdims=True)
        dcw_ref[2:3, :] += jnp.sum(dconv * z, axis=0, keepdims=True)
        dh_ref[:, in_a:in_a + bw] = (dsc_e[:tm] * conv_e[:tm]).astype(BF16)
        dh_ref[:, in_a + bw:in_a + 2 * bw] = (dz * hv[:, in_a + 2 * bw:]).astype(BF16)
        dh_ref[:, in_a + 2 * bw:] = (dz * hv[:, in_a + bw:in_a + 2 * bw]).astype(BF16)

        ha = hv[:, :in_a]
        th, u, xhat, rl, vln = _sgu_forward(ha, lng_ref[...], lnb_ref[...], wm, bt_ref[...], mixed_s)
        mixed = mixed_s[...]
        sg = u * mixed
        dsg, dga = _rms_bwd(dycat[:tm, :aw], sg, _rstd(sg), ga_ref[...])
        dga_ref[...] += dga
        du = dsg * mixed
        dmixed = dsg * u
        dmb = dmixed.astype(BF16)
        for n in range(tm // CHUNK):
            rows = slice(n * CHUNK, (n + 1) * CHUNK)
            dbs_ref[...] += dmixed[rows]
            for hh in range(HEADS):
                cols = slice(hh * hd, (hh + 1) * hd)
                dvln_s[rows, cols] = _dot_tn(wm[hh], dmb[rows, cols])
                dwsp_ref[hh] += mask * _dot_nt(dmb[rows, cols], vln[rows, cols])
        dvln = dvln_s[...]
        dlng_ref[...] += jnp.sum(dvln * xhat, axis=0, keepdims=True)
        dlnb_ref[...] += jnp.sum(dvln, axis=0, keepdims=True)
        dxh = dvln * lng_ref[...]
        dv = rl * (dxh - jnp.mean(dxh, axis=-1, keepdims=True) - xhat * jnp.mean(dxh * xhat, axis=-1, keepdims=True))
        dh_ref[:, :in_a] = (jnp.concatenate([du, dv], axis=-1) * _gelu_grad(ha, th)).astype(BF16)

        dxn = _dot_nt(dh_ref[...], win_ref[...])
        xv = x_ref[...]
        dx, dgm = _rms_bwd(dxn, xv, _rstd(xv), gm_ref[...])
        dgm_ref[...] += dgm
        dx_ref[...] = dx1 + dx

    full = lambda shape: pl.BlockSpec(shape, lambda i: (0,) * len(shape))
    tok = pl.BlockSpec((tm, d), lambda i: (i, 0))
    nxt = lambda i: (jnp.minimum((i + 1) * hb_blocks, last_blk), 0)
    prv = lambda i: (jnp.maximum(i * hb_blocks - 1, 0), 0)
    return _pcall(
        body, grid=(nt,),
        in_specs=[tok, pl.BlockSpec((HALO, d), nxt), tok, full((1, d)),
                  pl.BlockSpec((tm, n_in), lambda i: (i, 0)), pl.BlockSpec((HALO, n_in), prv), pl.BlockSpec((HALO, n_in), nxt),
                  full((1, aw)), full((1, aw)), full((HEADS, CHUNK, CHUNK)), full((CHUNK, HEADS)), full((3, bw)),
                  full((1, aw)), full((1, bw)), full((d, d)), full((d, n_in))],
        out_specs=[tok, pl.BlockSpec((tm, n_in), lambda i: (i, 0)), tok,
                   full((1, aw)), full((1, bw)), full((SUB, bw)), full((1, aw)), full((1, aw)),
                   full((HEADS, CHUNK, CHUNK)), full((CHUNK, aw)), full((1, d))],
        out_shape=(SDS((s, d), BF16), SDS((s, n_in), BF16), SDS((s, d), F32),
                   SDS((1, aw), F32), SDS((1, bw), F32), SDS((SUB, bw), F32), SDS((1, aw), F32), SDS((1, aw), F32),
                   SDS((HEADS, CHUNK, CHUNK), F32), SDS((CHUNK, aw), F32), SDS((1, d), F32)),
        scratch_shapes=[pltpu.VMEM((tm, aw), F32), pltpu.VMEM((tm, aw), F32)],
        compiler_params=_arb(1), name="mix_backward")(dx1, dx1, x, g_mix, h, h, h, lng, lnb, w_sp, bt, conv_w, ga, gb, w_out, w_in)


def _bias_grad(dbs):
    aw = dbs.shape[1]
    hd = aw // HEADS

    def body(dbs_ref, out_ref):
        ones = jnp.ones((SUB, hd), F32)
        for hh in range(HEADS):
            r = lax.dot_general(ones, dbs_ref[:, hh * hd:(hh + 1) * hd], (((1,), (1,)), ((), ())),
                                precision=lax.Precision.HIGHEST, preferred_element_type=F32)
            out_ref[hh:hh + 1, :] = r[0:1]

    return _pcall(body, out_shape=SDS((HEADS, CHUNK), F32), name="bias_grad")(dbs)


def _wgrad_body(a_ref, b_ref, o_ref):
    o_ref[...] = _dot_tn(a_ref[...], b_ref[...])


def _wgrad(a, b, name):
    k, m = a.shape
    n = b.shape[1]
    tm = _tile(m, 512, LANES)
    tn = _tile(n, 1024, LANES)
    return _pcall(
        functools.partial(_wgrad_body), grid=(m // tm, n // tn),
        in_specs=[pl.BlockSpec((k, tm), lambda i, j: (0, i)), pl.BlockSpec((k, tn), lambda i, j: (0, j))],
        out_specs=pl.BlockSpec((tm, tn), lambda i, j: (i, j)),
        out_shape=SDS((m, n), F32), compiler_params=_arb(2), name=name)(a, b)


def _wgrad_blocked_lhs(a, b, name):
    nb, k, t = a.shape
    n = b.shape[1]
    tn = _tile(n, 1024, LANES)
    return _pcall(
        functools.partial(_wgrad_body), grid=(nb, n // tn),
        in_specs=[pl.BlockSpec((None, k, t), lambda i, j: (i, 0, 0)), pl.BlockSpec((k, tn), lambda i, j: (0, j))],
        out_specs=pl.BlockSpec((t, tn), lambda i, j: (i, j)),
        out_shape=SDS((nb * t, n), F32), compiler_params=_arb(2), name=name)(a, b)


def _wgrad_blocked_rhs(a, b, name):
    k, m = a.shape
    nb, _, t = b.shape
    tm = _tile(m, 512, LANES)
    return _pcall(
        functools.partial(_wgrad_body), grid=(m // tm, nb),
        in_specs=[pl.BlockSpec((k, tm), lambda i, j: (0, i)), pl.BlockSpec((None, k, t), lambda i, j: (j, 0, 0))],
        out_specs=pl.BlockSpec((None, tm, t), lambda i, j: (j, i, 0)),
        out_shape=SDS((nb, m, t), F32), compiler_params=_arb(2), name=name)(a, b)


def _unblock_cols(wb, name):
    nb, r, t = wb.shape
    tr = _tile(r, 256, 16)

    def body(w_ref, o_ref):
        o_ref[...] = jnp.concatenate([w_ref[j].astype(F32) for j in range(nb)], axis=-1).astype(o_ref.dtype)

    return _pcall(
        body, grid=(r // tr,),
        in_specs=[pl.BlockSpec((nb, tr, t), lambda i: (0, i, 0))], out_specs=pl.BlockSpec((tr, nb * t), lambda i: (i, 0)),
        out_shape=SDS((r, nb * t), wb.dtype), compiler_params=_arb(1), name=name)(wb)


def _block_cols(w, nb, name):
    r, n = w.shape
    t = n // nb
    tr = _tile(r, 256, 16)

    def body(w_ref, o_ref):
        wv = w_ref[...]
        for j in range(nb):
            o_ref[j] = wv[:, j * t:(j + 1) * t]

    return _pcall(
        body, grid=(r // tr,),
        in_specs=[pl.BlockSpec((tr, n), lambda i: (i, 0))], out_specs=pl.BlockSpec((nb, tr, t), lambda i: (0, i, 0)),
        out_shape=SDS((nb, r, t), w.dtype), compiler_params=_arb(1), name=name)(w)


def _hbm_specs(n):
    return [pl.BlockSpec(memory_space=pl.ANY)] * n


def _all_gather(shards, name):
    n = len(shards)

    def body(*refs):
        ins, outs = refs[:n], refs[n:2 * n]
        send_sems, recv_sems, local_sems = refs[2 * n:]
        x, y, c = lax.axis_index("x"), lax.axis_index("y"), lax.axis_index("c")
        me, sibling = (x, y, c), (x, y, 1 - c)
        chips = [(1 - x, y), (x, 1 - y), (1 - x, 1 - y)]

        def blk(a, px, py, pc):
            return outs[a].at[4 * px + 2 * py + pc]

        def copy(a, k, block, to, src=None):
            return pltpu.make_async_remote_copy(
                src_ref=blk(a, *block) if src is None else src, dst_ref=blk(a, *block),
                send_sem=send_sems.at[7 * a + k], recv_sem=recv_sems.at[7 * a + k], device_id=to, device_id_type=MESH)

        first = []
        for a in range(n):
            first += [copy(a, 1 + j, me, (*chip, c), src=ins[a]) for j, chip in enumerate(chips)]
        for a in range(n):
            first.append(copy(a, 0, me, sibling, src=ins[a]))
        for cp in first:
            cp.start()
        mine = [pltpu.make_async_copy(ins[a], blk(a, *me), local_sems.at[a]) for a in range(n)]
        for cp in mine:
            cp.start()
        passed = []
        for a in range(n):
            for j, chip in enumerate(chips):
                copy(a, 1 + j, (*chip, c), me).wait_recv()
                passed.append(copy(a, 4 + j, (*chip, c), sibling))
                passed[-1].start()
        for a in range(n):
            copy(a, 0, sibling, me).wait_recv()
            for j, chip in enumerate(chips):
                copy(a, 4 + j, (*chip, 1 - c), me).wait_recv()
        for cp in first + passed:
            cp.wait_send()
        for cp in mine:
            cp.wait()

    return _pcall(
        body, out_shape=tuple(SDS((N_DEV,) + s.shape, s.dtype) for s in shards),
        in_specs=_hbm_specs(n), out_specs=_hbm_specs(n),
        scratch_shapes=[pltpu.SemaphoreType.DMA((7 * n,)), pltpu.SemaphoreType.DMA((7 * n,)), pltpu.SemaphoreType.DMA((n,))],
        name=name)(*shards)


def _exchange_c(gs, name):
    n = len(gs)

    def body(*refs):
        ins, outs = refs[:n], refs[n:2 * n]
        send_sems, recv_sems = refs[2 * n:]
        x, y, c = lax.axis_index("x"), lax.axis_index("y"), lax.axis_index("c")
        cps = [pltpu.make_async_remote_copy(
                   src_ref=ins[a].at[2 * k + 1 - c], dst_ref=outs[a].at[k],
                   send_sem=send_sems.at[4 * a + k], recv_sem=recv_sems.at[4 * a + k],
                   device_id=(x, y, 1 - c), device_id_type=MESH)
               for a in range(n) for k in range(4)]
        for cp in cps:
            cp.start()
        for cp in cps:
            cp.wait()

    return _pcall(
        body, out_shape=tuple(SDS((4,) + g.shape[1:], g.dtype) for g in gs),
        in_specs=_hbm_specs(n), out_specs=_hbm_specs(n),
        scratch_shapes=[pltpu.SemaphoreType.DMA((4 * n,)), pltpu.SemaphoreType.DMA((4 * n,))], name=name)(*gs)


def _exchange_xy(sends, name):
    n = len(sends)

    def body(*refs):
        ins, outs = refs[:n], refs[n:2 * n]
        send_sems, recv_sems = refs[2 * n:]
        x, y, c = lax.axis_index("x"), lax.axis_index("y"), lax.axis_index("c")
        peers = [(1 - x, y, c), (x, 1 - y, c), (1 - x, 1 - y, c)]
        cps = [pltpu.make_async_remote_copy(
                   src_ref=ins[a].at[t], dst_ref=outs[a].at[t],
                   send_sem=send_sems.at[3 * a + t], recv_sem=recv_sems.at[3 * a + t],
                   device_id=peers[t], device_id_type=MESH)
               for a in range(n) for t in range(3)]
        for cp in cps:
            cp.start()
        for cp in cps:
            cp.wait()

    return _pcall(
        body, out_shape=tuple(SDS(s.shape, s.dtype) for s in sends),
        in_specs=_hbm_specs(n), out_specs=_hbm_specs(n),
        scratch_shapes=[pltpu.SemaphoreType.DMA((3 * n,)), pltpu.SemaphoreType.DMA((3 * n,))], name=name)(*sends)


def _rs_combine(g, recv, pos, name):
    _, r, cdim = g.shape
    tr = _tile(r, 256, 16)

    def body(pos_ref, g0, r0, g1, r1, g2, r2, g3, r3, keep_ref, send_ref):
        keep_ref[...] = g0[...] + r0[...]
        send_ref[0] = (g1[...] + r1[...]).astype(BF16)
        send_ref[1] = (g2[...] + r2[...]).astype(BF16)
        send_ref[2] = (g3[...] + r3[...]).astype(BF16)

    def k_of(p, t):
        px = p[0] if t in (0, 2) else 1 - p[0]
        py = p[1] if t in (0, 1) else 1 - p[1]
        return 2 * px + py

    blk = (None, tr, cdim)
    in_specs = []
    for t in range(4):
        in_specs.append(pl.BlockSpec(blk, functools.partial(lambda j, p, t: (2 * k_of(p, t) + p[2], j, 0), t=t)))
        in_specs.append(pl.BlockSpec(blk, functools.partial(lambda j, p, t: (k_of(p, t), j, 0), t=t)))
    return _pcall(
        body, out_shape=(SDS((r, cdim), F32), SDS((3, r, cdim), BF16)),
        grid_spec=pltpu.PrefetchScalarGridSpec(
            num_scalar_prefetch=1, grid=(r // tr,), in_specs=in_specs,
            out_specs=[pl.BlockSpec((tr, cdim), lambda j, p: (j, 0)), pl.BlockSpec((3, tr, cdim), lambda j, p: (0, j, 0))]),
        compiler_params=_arb(1), name=name)(pos, g, recv, g, recv, g, recv, g, recv)


def _adamw_shard(keep, recv, w, m, v, name):
    r, cdim = w.shape
    tr = _tile(r, 256, 16)

    def body(k_ref, r_ref, w_ref, m_ref, v_ref, g_ref, d_ref, nm_ref, nv_ref):
        g = ((k_ref[...] + r_ref[0].astype(F32)) + r_ref[1].astype(F32)) + r_ref[2].astype(F32)
        g_ref[...] = g
        d_ref[...], nm_ref[...], nv_ref[...] = _adamw(w_ref[...], g, m_ref[...], v_ref[...])

    blk = pl.BlockSpec((tr, cdim), lambda j: (j, 0))
    out = SDS((r, cdim), F32)
    return _pcall(body, grid=(r // tr,), in_specs=[blk, pl.BlockSpec((3, tr, cdim), lambda j: (0, j, 0)), blk, blk, blk],
                  out_specs=[blk] * 4, out_shape=(out,) * 4, compiler_params=_arb(1), name=name)(keep, recv, w, m, v)


def _adamw_small(gathered, seg, params, conv_rows):
    names = list(params)
    c0, cn = conv_rows

    def body(*refs):
        gat_ref = refs[0]
        ins = refs[1:1 + 3 * len(names)]
        outs = refs[1 + 3 * len(names):]

        def total(r0, rn):
            tot = gat_ref[0, r0:r0 + rn, :]
            for dev in range(1, N_DEV):
                tot = tot + gat_ref[dev, r0:r0 + rn, :]
            return tot

        for k, nm in enumerate(names):
            g = total(*seg[nm])
            w_ref, m_ref, v_ref = ins[3 * k:3 * k + 3]
            g_ref, d_ref, nm_ref, nv_ref = outs[4 * k:4 * k + 4]
            g_ref[...] = g
            d_ref[...], nm_ref[...], nv_ref[...] = _adamw(w_ref[...], g, m_ref[...], v_ref[...])
        outs[-2][...] = total(c0, cn)
        outs[-1][...] = total(*seg["loss"])

    flat_in = [a for nm in names for a in params[nm]]
    out_shape = []
    for nm in names:
        out_shape += [SDS(params[nm][0].shape, F32)] * 4
    out_shape += [SDS((cn, LANES), F32), SDS((seg["loss"][1], LANES), F32)]
    res = _pcall(body, out_shape=tuple(out_shape), name="adamw_small")(gathered, *flat_in)
    per = {nm: res[4 * k:4 * k + 4] for k, nm in enumerate(names)}
    return per, res[-2], res[-1]


def _adamw_one(w, g, m, v, name):
    def body(w_ref, g_ref, m_ref, v_ref, d_ref, nm_ref, nv_ref):
        d_ref[...], nm_ref[...], nv_ref[...] = _adamw(w_ref[...], g_ref[...], m_ref[...], v_ref[...])

    return _pcall(body, out_shape=(SDS(w.shape, F32),) * 3, name=name)(w, g, m, v)


def _local_grads(x, mem, target, w8, sp):
    s, d = x.shape
    tm = min(TOKEN_TILE, s)
    rows = lambda nm: w8[nm].reshape(-1, w8[nm].shape[2])
    shards = lambda g: g.reshape((N_DEV, -1) + g.shape[1:])
    w_in = _unblock_cols(w8["w_in"], "unblock_w_in")
    w_gu = w8["w_gate_up"].reshape((2, N_DEV // 2) + w8["w_gate_up"].shape[1:])
    w_out, w_q, w_o, w_down = rows("w_out"), rows("w_q"), rows("w_o"), rows("w_down")
    bt = sp["b_spatial"].T

    memn, kv = _kv_forward(mem, sp["ln_mem_g"], w8["w_kv"])
    xn1, h = _in_forward(x, sp["ln_mix_g"], w_in, tm)
    ycat, x1 = _mix_forward(h, x, sp["sgu_ln_g"], sp["sgu_ln_b"], sp["w_spatial"], bt, sp["conv_w"],
                            sp["grp_norm_a"], sp["grp_norm_b"], w_out, tm)
    xn2, q, o, x2 = _attn_forward(x1, sp["ln_attn_g"], w_q, kv, w_o, tm)
    xn3, gu, x3 = _ffn_forward(x2, sp["ln_ffn_g"], w_gu, w_down, tm)

    loss, d_lnf, d_lnffn, act, dgu, dx3b, dx2 = _ffn_backward(
        x3, target, sp["ln_final_g"], x2, sp["ln_ffn_g"], gu, w_gu, w_down, tm)
    gw = {}
    gw["w_down"] = shards(_wgrad_blocked_lhs(act, dx3b, "wgrad_down"))
    gw["w_gate_up"] = _wgrad_blocked_rhs(xn3, dgu.reshape((N_DEV,) + dgu.shape[2:]), "wgrad_gate_up")
    dx2b, dq, dx1, dkv, d_lnattn = _attn_backward(dx2, x1, sp["ln_attn_g"], q, kv, w_q, w_o, tm)
    gw["w_o"] = shards(_wgrad(o, dx2b, "wgrad_o"))
    gw["w_q"] = shards(_wgrad(xn2, dq, "wgrad_q"))
    gw["w_kv"], d_lnmem = _kv_backward(dkv, memn, mem, sp["ln_mem_g"], w8["w_kv"])
    (dx1b, dh, dx, d_ga, d_gb, d_cw, d_lng, d_lnb, d_wsp, d_bs, d_lnmix) = _mix_backward(
        dx1, x, sp["ln_mix_g"], h, sp["sgu_ln_g"], sp["sgu_ln_b"], sp["w_spatial"], bt, sp["conv_w"],
        sp["grp_norm_a"], sp["grp_norm_b"], w_out, w_in, tm)
    gw["w_out"] = shards(_wgrad(ycat, dx1b, "wgrad_out"))
    gw["w_in"] = _block_cols(_wgrad(xn1, dh, "wgrad_in"), N_DEV, "block_grad_w_in")
    gs = {"ln_mix_g": d_lnmix, "sgu_ln_g": d_lng, "sgu_ln_b": d_lnb, "w_spatial": d_wsp, "b_spatial": _bias_grad(d_bs),
          "conv_w": d_cw[:3], "grp_norm_a": d_ga, "grp_norm_b": d_gb, "ln_attn_g": d_lnattn, "ln_mem_g": d_lnmem,
          "ln_ffn_g": d_lnffn, "ln_final_g": d_lnf}
    return loss, dx, gw, gs


def _rows128(a):
    return a.reshape(-1, LANES)


def kernel(x, mem, ln_mix_g, w_in, sgu_ln_g, sgu_ln_b, w_spatial, b_spatial, conv_w, grp_norm_a, grp_norm_b, w_out, ln_attn_g, ln_mem_g, w_q, w_kv, w_o, ln_ffn_g, w_gate_up, w_down, ln_final_g, loss_target, m_ln_mix_g, m_w_in, m_sgu_ln_g, m_sgu_ln_b, m_w_spatial, m_b_spatial, m_conv_w, m_grp_norm_a, m_grp_norm_b, m_w_out, m_ln_attn_g, m_ln_mem_g, m_w_q, m_w_kv, m_w_o, m_ln_ffn_g, m_w_gate_up, m_w_down, m_ln_final_g, v_ln_mix_g, v_w_in, v_sgu_ln_g, v_sgu_ln_b, v_w_spatial, v_b_spatial, v_conv_w, v_grp_norm_a, v_grp_norm_b, v_w_out, v_ln_attn_g, v_ln_mem_g, v_w_q, v_w_kv, v_w_o, v_ln_ffn_g, v_w_gate_up, v_w_down, v_ln_final_g):
    order = ["ln_mix_g", "w_in", "sgu_ln_g", "sgu_ln_b", "w_spatial", "b_spatial", "conv_w", "grp_norm_a", "grp_norm_b",
             "w_out", "ln_attn_g", "ln_mem_g", "w_q", "w_kv", "w_o", "ln_ffn_g", "w_gate_up", "w_down", "ln_final_g"]
    W = dict(ln_mix_g=ln_mix_g, w_in=w_in, sgu_ln_g=sgu_ln_g, sgu_ln_b=sgu_ln_b, w_spatial=w_spatial, b_spatial=b_spatial,
             conv_w=conv_w, grp_norm_a=grp_norm_a, grp_norm_b=grp_norm_b, w_out=w_out, ln_attn_g=ln_attn_g,
             ln_mem_g=ln_mem_g, w_q=w_q, w_kv=w_kv, w_o=w_o, ln_ffn_g=ln_ffn_g, w_gate_up=w_gate_up, w_down=w_down,
             ln_final_g=ln_final_g)
    M = dict(ln_mix_g=m_ln_mix_g, w_in=m_w_in, sgu_ln_g=m_sgu_ln_g, sgu_ln_b=m_sgu_ln_b, w_spatial=m_w_spatial,
             b_spatial=m_b_spatial, conv_w=m_conv_w, grp_norm_a=m_grp_norm_a, grp_norm_b=m_grp_norm_b, w_out=m_w_out,
             ln_attn_g=m_ln_attn_g, ln_mem_g=m_ln_mem_g, w_q=m_w_q, w_kv=m_w_kv, w_o=m_w_o, ln_ffn_g=m_ln_ffn_g,
             w_gate_up=m_w_gate_up, w_down=m_w_down, ln_final_g=m_ln_final_g)
    V = dict(ln_mix_g=v_ln_mix_g, w_in=v_w_in, sgu_ln_g=v_sgu_ln_g, sgu_ln_b=v_sgu_ln_b, w_spatial=v_w_spatial,
             b_spatial=v_b_spatial, conv_w=v_conv_w, grp_norm_a=v_grp_norm_a, grp_norm_b=v_grp_norm_b, w_out=v_w_out,
             ln_attn_g=v_ln_attn_g, ln_mem_g=v_ln_mem_g, w_q=v_w_q, w_kv=v_w_kv, w_o=v_w_o, ln_ffn_g=v_ln_ffn_g,
             w_gate_up=v_w_gate_up, w_down=v_w_down, ln_final_g=v_ln_final_g)

    s, d = x.shape[1], x.shape[2]
    bw = conv_w.shape[1] * N_DEV
    pos = jnp.stack([lax.axis_index("x"), lax.axis_index("y"), lax.axis_index("c")]).astype(jnp.int32)
    me = 4 * pos[0] + 2 * pos[1] + pos[2]

    gathered = _all_gather([W[nm].astype(BF16) for nm in BIG] + [conv_w], "gather_weights")
    w8 = dict(zip(BIG, gathered[:-1]))
    conv_full = gathered[-1].transpose(1, 0, 2).reshape(3, bw)

    sp = {nm: (W[nm].reshape(1, -1) if W[nm].ndim == 1 else W[nm]) for nm in SMALL}
    sp["conv_w"] = conv_full
    loss_tile, grad_x, gw, gs = _local_grads(x[0], mem[0], loss_target[0], w8, sp)

    recv_c = _exchange_c([gw[nm] for nm in BIG], "exchange_c")
    combined = [_rs_combine(gw[nm], rc, pos, "rs_combine_" + nm) for nm, rc in zip(BIG, recv_c)]
    recv_xy = _exchange_xy([send for _, send in combined], "exchange_xy")
    out = {}
    for nm, (keep, _), rxy in zip(BIG, combined, recv_xy):
        out[nm] = tuple(_adamw_shard(keep, rxy, W[nm], M[nm], V[nm], "adamw_" + nm))

    seg, pieces, row = {}, [], 0
    for nm in SMALL + ("conv_w", "loss"):
        piece = loss_tile if nm == "loss" else _rows128(gs[nm])
        rn = _round_up(piece.shape[0], SUB)
        pieces.append(jnp.pad(piece, ((0, rn - piece.shape[0]), (0, 0))))
        seg[nm] = (row, piece.shape[0])
        row += rn
    packed = jnp.concatenate(pieces, axis=0)
    small_all, = _all_gather([packed], "gather_small")
    params = {nm: (_rows128(W[nm]), _rows128(M[nm]), _rows128(V[nm])) for nm in SMALL}
    per, conv_g_rows, loss_sum = _adamw_small(small_all, seg, params, seg["conv_w"])
    for nm in SMALL:
        out[nm] = tuple(a.reshape(W[nm].shape) for a in per[nm])
    conv_g = lax.dynamic_slice_in_dim(conv_g_rows.reshape(3, bw), me * conv_w.shape[1], conv_w.shape[1], axis=1)
    out["conv_w"] = (conv_g,) + tuple(_adamw_one(conv_w, conv_g, m_conv_w, v_conv_w, "adamw_conv"))

    loss = loss_sum[0, 0]
    res = [loss, grad_x[None]]
    for k in range(4):
        res += [out[nm][k] for nm in order]
    return tuple(res)
```

```python
import functools

import jax
import jax.numpy as jnp
from jax import lax
from jax.experimental import pallas as pl
from jax.experimental.pallas import tpu as pltpu

F32 = jnp.float32
BF16 = jnp.bfloat16
SDS = jax.ShapeDtypeStruct
MESH = pl.DeviceIdType.MESH

EPS = 1e-6
N_DEV = 8
HEADS = 4
CHUNK = 128
HALO = 16
SUB = 8
LANES = 128
TOKEN_TILE = 512

ADAM_LR = 0.001
ADAM_B1 = 0.9
ADAM_B2 = 0.999
ADAM_EPS = 1e-08
ADAM_WD = 0.01
ADAM_STEP = 10

BIG = ("w_in", "w_out", "w_q", "w_kv", "w_o", "w_gate_up", "w_down")
SMALL = ("ln_mix_g", "sgu_ln_g", "sgu_ln_b", "w_spatial", "b_spatial", "grp_norm_a", "grp_norm_b",
         "ln_attn_g", "ln_mem_g", "ln_ffn_g", "ln_final_g")


class _Exchange:
    def __init__(self, ins, out_shape, sems, start, finish):
        self.ins, self.out_shape, self.sems, self.start, self.finish = list(ins), list(out_shape), list(sems), start, finish


def _pcall(body, carry=(), **kw):
    if carry:
        return functools.partial(_carrying_call, body, tuple(carry), kw)
    return pl.pallas_call(body, **kw)


def _carrying_call(body, carry, kw, *args):
    kw = dict(kw)
    out_shape = kw.pop("out_shape")
    single = not isinstance(out_shape, (tuple, list))
    outs_shape = (out_shape,) if single else tuple(out_shape)
    out_specs = kw.pop("out_specs")
    out_specs = [out_specs] if single else list(out_specs)
    in_specs = list(kw.pop("in_specs"))
    scratch = list(kw.pop("scratch_shapes", ()))
    grid = tuple(kw.get("grid", ()))
    n_in, n_out, n_scr = len(args), len(outs_shape), len(scratch)

    def split(refs, k, counts):
        parts = []
        for cnt in counts:
            parts.append(refs[k:k + cnt])
            k += cnt
        return parts, k

    def wrapped(*refs):
        cins, k = split(refs, n_in, [len(p.ins) for p in carry])
        outs = refs[k:k + n_out]
        couts, k = split(refs, k + n_out, [len(p.out_shape) for p in carry])
        scr = refs[k:k + n_scr]
        csems, _ = split(refs, k + n_scr, [len(p.sems) for p in carry])
        first, last = True, True
        for a, g in enumerate(grid):
            first = (pl.program_id(a) == 0) & first
            last = (pl.program_id(a) == g - 1) & last

        def start_all():
            for p, ci, co, cs in zip(carry, cins, couts, csems):
                p.start(ci, co, cs)

        def finish_all():
            for p, ci, co, cs in zip(carry, cins, couts, csems):
                p.finish(ci, co, cs)

        start_all() if not grid else pl.when(first)(start_all)
        body(*refs[:n_in], *outs, *scr)
        finish_all() if not grid else pl.when(last)(finish_all)

    c_in = [a for p in carry for a in p.ins]
    c_out = [s for p in carry for s in p.out_shape]
    c_sems = [s for p in carry for s in p.sems]
    res = _pcall(wrapped, out_shape=outs_shape + tuple(c_out), in_specs=in_specs + _hbm_specs(len(c_in)),
                 out_specs=out_specs + _hbm_specs(len(c_out)), scratch_shapes=scratch + c_sems, **kw)(*args, *c_in)
    own = res[0] if single else tuple(res[:n_out])
    landed, k = [], n_out
    for p in carry:
        landed.append(list(res[k:k + len(p.out_shape)]))
        k += len(p.out_shape)
    return own, landed


def _hbm_specs(n):
    return [pl.BlockSpec(memory_space=pl.ANY)] * n


def _hosted(body, carry, **kw):
    if carry:
        return _pcall(body, carry=carry, **kw)
    call = _pcall(body, **kw)
    return lambda *args: (call(*args), [])


def _run_exchanges(parts, name):
    def body(*refs):
        pass

    _, landed = _pcall(body, carry=parts, out_shape=(), in_specs=[], out_specs=[], name=name)()
    return landed


def _arb(n):
    return pltpu.CompilerParams(dimension_semantics=("arbitrary",) * n)


def _tile(n, target, mult):
    best = None
    for t in range(mult, min(n, target) + 1, mult):
        if n % t == 0:
            best = t
    return n if best is None else best


def _round_up(n, m):
    return (n + m - 1) // m * m


def _dot(a, b):
    return jnp.dot(a, b, preferred_element_type=F32)


def _dot_nt(a, b):
    return lax.dot_general(a, b, (((1,), (1,)), ((), ())), preferred_element_type=F32)


def _dot_tn(a, b):
    return lax.dot_general(a, b, (((0,), (0,)), ((), ())), preferred_element_type=F32)


def _rstd(x):
    return lax.rsqrt(jnp.mean(x * x, axis=-1, keepdims=True) + EPS)


def _rms_bwd(dy, x, r, g):
    gdy = dy * g
    proj = jnp.sum(gdy * x, axis=-1, keepdims=True) * (1.0 / x.shape[-1])
    dx = r * gdy - x * (r * r * r) * proj
    dg = jnp.sum(dy * (x * r), axis=0, keepdims=True)
    return dx, dg


_GELU_C = 0.7978845608028654
_GELU_A = 0.044715


def _gelu(x):
    t = jnp.tanh(_GELU_C * (x + _GELU_A * x * x * x))
    return 0.5 * x * (1.0 + t), t


def _gelu_grad(x, t):
    return 0.5 * (1.0 + t) + 0.5 * x * (1.0 - t * t) * (_GELU_C * (1.0 + 3.0 * _GELU_A * x * x))


def _sigmoid(x):
    return 1.0 / (1.0 + jnp.exp(-x))


def _softmax(s):
    m = jnp.max(s, axis=-1, keepdims=True)
    e = jnp.exp(s - m)
    return e / jnp.sum(e, axis=-1, keepdims=True)


def _adamw(w, g, m, v):
    m = ADAM_B1 * m + (1.0 - ADAM_B1) * g
    v = ADAM_B2 * v + (1.0 - ADAM_B2) * (g * g)
    m_hat = m / (1.0 - ADAM_B1 ** ADAM_STEP)
    v_hat = v / (1.0 - ADAM_B2 ** ADAM_STEP)
    delta = -ADAM_LR * (m_hat / (jnp.sqrt(v_hat) + ADAM_EPS) + ADAM_WD * w)
    return delta, m, v


def _tril_mask():
    t = lax.broadcasted_iota(jnp.int32, (CHUNK, CHUNK), 0)
    s = lax.broadcasted_iota(jnp.int32, (CHUNK, CHUNK), 1)
    return (s <= t).astype(F32)


def _sgu_forward(ha, lng, lnb, wm, bt, mixed_s):
    aw = ha.shape[1] // 2
    hd = aw // HEADS
    a, th = _gelu(ha)
    u = a[:, :aw]
    v = a[:, aw:]
    mu = jnp.mean(v, axis=-1, keepdims=True)
    vc = v - mu
    rl = lax.rsqrt(jnp.mean(vc * vc, axis=-1, keepdims=True) + EPS)
    xhat = vc * rl
    vln = (xhat * lng + lnb).astype(BF16)
    for n in range(ha.shape[0] // CHUNK):
        rows = slice(n * CHUNK, (n + 1) * CHUNK)
        for h in range(HEADS):
            cols = slice(h * hd, (h + 1) * hd)
            mixed_s[rows, cols] = _dot(wm[h], vln[rows, cols]) + bt[:, h:h + 1]
    return th, u, xhat, rl, vln


def _conv_taps(zext):
    return pltpu.roll(zext, 2, 0), pltpu.roll(zext, 1, 0)


def _kv_forward(mem, g_mem, w_kv):
    ml, d = mem.shape
    xd = w_kv.shape[2]

    def body(mem_ref, g_ref, w_ref, memn_ref, kv_ref):
        x = mem_ref[...]
        memn = (x * _rstd(x) * g_ref[...]).astype(BF16)
        memn_ref[...] = memn
        for j in range(2 * HEADS):
            kv_ref[j] = _dot(memn, w_ref[j]).astype(BF16)

    return _pcall(body, out_shape=(SDS((ml, d), BF16), SDS((2 * HEADS, ml, xd), BF16)), name="kv_forward")(mem, g_mem, w_kv)


def _in_forward(x, g, w_in, tm, carry=()):
    s, d = x.shape
    n_in = w_in.shape[1]

    def body(x_ref, g_ref, w_ref, xn_ref, h_ref):
        xv = x_ref[...]
        xn = (xv * _rstd(xv) * g_ref[...]).astype(BF16)
        xn_ref[...] = xn
        h_ref[...] = _dot(xn, w_ref[...])

    return _hosted(
        body, carry, grid=(s // tm,),
        in_specs=[pl.BlockSpec((tm, d), lambda i: (i, 0)), pl.BlockSpec((1, d), lambda i: (0, 0)),
                  pl.BlockSpec((d, n_in), lambda i: (0, 0))],
        out_specs=[pl.BlockSpec((tm, d), lambda i: (i, 0)), pl.BlockSpec((tm, n_in), lambda i: (i, 0))],
        out_shape=(SDS((s, d), BF16), SDS((s, n_in), F32)),
        compiler_params=_arb(1), name="in_forward")(x, g, w_in)


def _mix_forward(h, x, lng, lnb, w_sp, bt, conv_w, ga, gb, w_out, tm, carry=()):
    s, d = x.shape
    n_in = h.shape[1]
    aw = lng.shape[1]
    bw = d - aw
    in_a = 2 * aw
    hb_blocks = tm // HALO

    def body(h_ref, hprev_ref, x_ref, lng_ref, lnb_ref, wsp_ref, bt_ref, cw_ref, ga_ref, gb_ref, wout_ref,
             ycat_ref, x1_ref, mixed_s):
        i = pl.program_id(0)
        mask = _tril_mask()
        wm = [(wsp_ref[hh] * mask).astype(BF16) for hh in range(HEADS)]
        hv = h_ref[...]
        _, u, _, _, _ = _sgu_forward(hv[:, :in_a], lng_ref[...], lnb_ref[...], wm, bt_ref[...], mixed_s)
        sg = u * mixed_s[...]
        ycat_ref[:, :aw] = (sg * _rstd(sg) * ga_ref[...]).astype(BF16)

        gate_b = hv[:, in_a:in_a + bw]
        z = hv[:, in_a + bw:in_a + 2 * bw] * hv[:, in_a + 2 * bw:]
        hp = hprev_ref[...]
        zp = hp[:, in_a + bw:in_a + 2 * bw] * hp[:, in_a + 2 * bw:]
        zp = jnp.where(i == 0, 0.0, zp)
        zext = jnp.concatenate([zp, z], axis=0)
        z2, z1 = _conv_taps(zext)
        cw = cw_ref[...]
        conv = cw[0:1] * z2[HALO:] + cw[1:2] * z1[HALO:] + cw[2:3] * z
        sc = gate_b * conv
        ycat_ref[:, aw:] = (sc * _rstd(sc) * gb_ref[...]).astype(BF16)
        x1_ref[...] = x_ref[...] + _dot(ycat_ref[...], wout_ref[...])

    full = lambda shape: pl.BlockSpec(shape, lambda i: (0,) * len(shape))
    return _hosted(
        body, carry, grid=(s // tm,),
        in_specs=[pl.BlockSpec((tm, n_in), lambda i: (i, 0)),
                  pl.BlockSpec((HALO, n_in), lambda i: (jnp.maximum(i * hb_blocks - 1, 0), 0)),
                  pl.BlockSpec((tm, d), lambda i: (i, 0)),
                  full((1, aw)), full((1, aw)), full((HEADS, CHUNK, CHUNK)), full((CHUNK, HEADS)),
                  full((3, bw)), full((1, aw)), full((1, bw)), full((d, d))],
        out_specs=[pl.BlockSpec((tm, d), lambda i: (i, 0)), pl.BlockSpec((tm, d), lambda i: (i, 0))],
        out_shape=(SDS((s, d), BF16), SDS((s, d), F32)),
        scratch_shapes=[pltpu.VMEM((tm, aw), F32)],
        compiler_params=_arb(1), name="mix_forward")(h, h, x, lng, lnb, w_sp, bt, conv_w, ga, gb, w_out)


def _attn_forward(x1, g, w_q, kv, w_o, tm, carry=()):
    s, d = x1.shape
    _, ml, xd = kv.shape
    scale = xd ** -0.5

    def body(x1_ref, g_ref, wq_ref, kv_ref, wo_ref, xn_ref, q_ref, o_ref, x2_ref):
        xv = x1_ref[...]
        xn = (xv * _rstd(xv) * g_ref[...]).astype(BF16)
        xn_ref[...] = xn
        q_ref[...] = _dot(xn, wq_ref[...]).astype(BF16)
        for hh in range(HEADS):
            cols = slice(hh * xd, (hh + 1) * xd)
            p = _softmax(_dot_nt(q_ref[:, cols], kv_ref[hh]) * scale)
            o_ref[:, cols] = _dot(p.astype(BF16), kv_ref[HEADS + hh]).astype(BF16)
        x2_ref[...] = xv + _dot(o_ref[...], wo_ref[...])

    tok = pl.BlockSpec((tm, d), lambda i: (i, 0))
    return _hosted(
        body, carry, grid=(s // tm,),
        in_specs=[tok, pl.BlockSpec((1, d), lambda i: (0, 0)), pl.BlockSpec((d, d), lambda i: (0, 0)),
                  pl.BlockSpec((2 * HEADS, ml, xd), lambda i: (0, 0, 0)), pl.BlockSpec((d, d), lambda i: (0, 0))],
        out_specs=[tok, tok, tok, tok],
        out_shape=(SDS((s, d), BF16), SDS((s, d), BF16), SDS((s, d), BF16), SDS((s, d), F32)),
        compiler_params=_arb(1), name="attn_forward")(x1, g, w_q, kv, w_o)


def _ffn_forward(x2, g, w_gu, w_down, tm):
    s, d = x2.shape
    _, nf, _, tf = w_gu.shape

    def body(x2_ref, g_ref, wgu_ref, wd_ref, xn_ref, gu_ref, x3_ref):
        f = pl.program_id(1)

        @pl.when(f == 0)
        def _():
            xv = x2_ref[...]
            xn_ref[...] = (xv * _rstd(xv) * g_ref[...]).astype(BF16)
            x3_ref[...] = xv

        xn = xn_ref[...]
        gate = _dot(xn, wgu_ref[0])
        up = _dot(xn, wgu_ref[1])
        gu_ref[0] = gate.astype(BF16)
        gu_ref[1] = up.astype(BF16)
        act = (gate * _sigmoid(gate) * up).astype(BF16)
        x3_ref[...] += _dot(act, wd_ref[...])

    tok = pl.BlockSpec((tm, d), lambda i, f: (i, 0))
    return _pcall(
        body, grid=(s // tm, nf),
        in_specs=[tok, pl.BlockSpec((1, d), lambda i, f: (0, 0)),
                  pl.BlockSpec((2, None, d, tf), lambda i, f: (0, f, 0, 0)),
                  pl.BlockSpec((tf, d), lambda i, f: (f, 0))],
        out_specs=[tok, pl.BlockSpec((2, None, tm, tf), lambda i, f: (0, f, i, 0)), tok],
        out_shape=(SDS((s, d), BF16), SDS((2, nf, s, tf), BF16), SDS((s, d), F32)),
        compiler_params=_arb(2), name="ffn_forward")(x2, g, w_gu, w_down)


def _ffn_backward(x3, target, g_final, x2, g_ffn, gu, w_gu, w_down, tm):
    s, d = x3.shape
    _, nf, _, tf = w_gu.shape

    def body(x3_ref, tgt_ref, gf_ref, x2_ref, g2_ref, gu_ref, wgu_ref, wd_ref,
             loss_ref, dgf_ref, dg2_ref, act_ref, dgu_ref, dx3b_ref, dx2_ref, dx3_s, acc_s):
        i = pl.program_id(0)
        f = pl.program_id(1)

        @pl.when((i == 0) & (f == 0))
        def _():
            loss_ref[...] = jnp.zeros_like(loss_ref)
            dgf_ref[...] = jnp.zeros_like(dgf_ref)
            dg2_ref[...] = jnp.zeros_like(dg2_ref)

        @pl.when(f == 0)
        def _():
            xv = x3_ref[...]
            r = _rstd(xv)
            diff = xv * r * gf_ref[...] - tgt_ref[...]
            loss_ref[...] += 0.5 * jnp.sum(jnp.sum(diff * diff, axis=-1, keepdims=True), axis=0, keepdims=True) * (1.0 / d)
            dx3, dgf = _rms_bwd(diff * (1.0 / d), xv, r, gf_ref[...])
            dgf_ref[...] += dgf
            dx3_s[...] = dx3
            dx3b_ref[...] = dx3.astype(BF16)
            acc_s[...] = jnp.zeros_like(acc_s)

        dact = _dot_nt(dx3b_ref[...], wd_ref[...])
        gv = gu_ref[0].astype(F32)
        uv = gu_ref[1].astype(F32)
        sg = _sigmoid(gv)
        silu = gv * sg
        act_ref[...] = (silu * uv).astype(BF16)
        dgate = (dact * uv * (sg * (1.0 + gv * (1.0 - sg)))).astype(BF16)
        dup = (dact * silu).astype(BF16)
        dgu_ref[0] = dgate
        dgu_ref[1] = dup
        acc_s[...] += _dot_nt(dgate, wgu_ref[0]) + _dot_nt(dup, wgu_ref[1])

        @pl.when(f == nf - 1)
        def _():
            xv = x2_ref[...]
            dxn, dg2 = _rms_bwd(acc_s[...], xv, _rstd(xv), g2_ref[...])
            dg2_ref[...] += dg2
            dx2_ref[...] = dx3_s[...] + dxn

    tok = pl.BlockSpec((tm, d), lambda i, f: (i, 0))
    pair = pl.BlockSpec((2, None, tm, tf), lambda i, f: (0, f, i, 0))
    vec = pl.BlockSpec((1, d), lambda i, f: (0, 0))
    return _pcall(
        body, grid=(s // tm, nf),
        in_specs=[tok, tok, vec, tok, vec, pair,
                  pl.BlockSpec((2, None, d, tf), lambda i, f: (0, f, 0, 0)),
                  pl.BlockSpec((tf, d), lambda i, f: (f, 0))],
        out_specs=[pl.BlockSpec((SUB, LANES), lambda i, f: (0, 0)), vec, vec,
                   pl.BlockSpec((None, tm, tf), lambda i, f: (f, i, 0)), pair, tok, tok],
        out_shape=(SDS((SUB, LANES), F32), SDS((1, d), F32), SDS((1, d), F32), SDS((nf, s, tf), BF16),
                   SDS((2, nf, s, tf), BF16), SDS((s, d), BF16), SDS((s, d), F32)),
        scratch_shapes=[pltpu.VMEM((tm, d), F32), pltpu.VMEM((tm, d), F32)],
        compiler_params=_arb(2), name="ffn_backward")(x3, target, g_final, x2, g_ffn, gu, w_gu, w_down)


def _attn_backward(dx2, x1, g, q, kv, w_q, w_o, tm, carry=()):
    s, d = x1.shape
    _, ml, xd = kv.shape
    scale = xd ** -0.5

    def body(dx2_ref, x1_ref, g_ref, q_ref, kv_ref, wq_ref, wo_ref,
             dx2b_ref, dq_ref, dx1_ref, dkv_ref, dg_ref, do_s):
        i = pl.program_id(0)

        @pl.when(i == 0)
        def _():
            dkv_ref[...] = jnp.zeros_like(dkv_ref)
            dg_ref[...] = jnp.zeros_like(dg_ref)

        dx2 = dx2_ref[...]
        dx2b_ref[...] = dx2.astype(BF16)
        do_s[...] = _dot_nt(dx2b_ref[...], wo_ref[...]).astype(BF16)
        for hh in range(HEADS):
            kc = slice(hh * xd, (hh + 1) * xd)
            qh = q_ref[:, kc]
            kh = kv_ref[hh]
            doh = do_s[:, kc]
            p = _softmax(_dot_nt(qh, kh) * scale)
            dp = _dot_nt(doh, kv_ref[HEADS + hh])
            dkv_ref[HEADS + hh] += _dot_tn(p.astype(BF16), doh)
            ds = (p * (dp - jnp.sum(dp * p, axis=-1, keepdims=True)) * scale).astype(BF16)
            dq_ref[:, kc] = _dot(ds, kh).astype(BF16)
            dkv_ref[hh] += _dot_tn(ds, qh)
        dxn = _dot_nt(dq_ref[...], wq_ref[...])
        xv = x1_ref[...]
        dx, dg = _rms_bwd(dxn, xv, _rstd(xv), g_ref[...])
        dg_ref[...] += dg
        dx1_ref[...] = dx2 + dx

    tok = pl.BlockSpec((tm, d), lambda i: (i, 0))
    vec = pl.BlockSpec((1, d), lambda i: (0, 0))
    sq = pl.BlockSpec((d, d), lambda i: (0, 0))
    kvs = pl.BlockSpec((2 * HEADS, ml, xd), lambda i: (0, 0, 0))
    return _hosted(
        body, carry, grid=(s // tm,),
        in_specs=[tok, tok, vec, tok, kvs, sq, sq],
        out_specs=[tok, tok, tok, kvs, vec],
        out_shape=(SDS((s, d), BF16), SDS((s, d), BF16), SDS((s, d), F32), SDS((2 * HEADS, ml, xd), F32), SDS((1, d), F32)),
        scratch_shapes=[pltpu.VMEM((tm, d), BF16)],
        compiler_params=_arb(1), name="attn_backward")(dx2, x1, g, q, kv, w_q, w_o)


def _kv_backward(dkv, memn, mem, g_mem, w_kv):
    ml, d = mem.shape
    xd = w_kv.shape[2]

    def body(dkv_ref, memn_ref, mem_ref, g_ref, w_ref, dw_ref, dg_ref):
        dmemn = jnp.zeros((ml, d), F32)
        for j in range(2 * HEADS):
            dkvb = dkv_ref[j].astype(BF16)
            dw_ref[j] = _dot_tn(memn_ref[...], dkvb)
            dmemn = dmemn + _dot_nt(dkvb, w_ref[j])
        x = mem_ref[...]
        dg_ref[...] = jnp.sum(dmemn * (x * _rstd(x)), axis=0, keepdims=True)

    return _pcall(body, out_shape=(SDS((2 * HEADS, d, xd), F32), SDS((1, d), F32)), name="kv_backward")(dkv, memn, mem, g_mem, w_kv)


def _mix_backward(dx1, x, g_mix, h, lng, lnb, w_sp, bt, conv_w, ga, gb, w_out, w_in, tm, carry=()):
    s, d = x.shape
    n_in = h.shape[1]
    aw = lng.shape[1]
    bw = d - aw
    hd = aw // HEADS
    in_a = 2 * aw
    hb_blocks = tm // HALO
    last_blk = s // HALO - 1
    nt = s // tm
    te = tm + HALO
    tee = tm + 2 * HALO

    def body(dx1_ref, dx1n_ref, x_ref, gm_ref, h_ref, hp_ref, hn_ref, lng_ref, lnb_ref, wsp_ref, bt_ref, cw_ref,
             ga_ref, gb_ref, wout_ref, win_ref,
             dx1b_ref, dh_ref, dx_ref, dga_ref, dgb_ref, dcw_ref, dlng_ref, dlnb_ref, dwsp_ref, dbs_ref, dgm_ref,
             mixed_s, dvln_s):
        i = pl.program_id(0)

        @pl.when(i == 0)
        def _():
            for ref in (dga_ref, dgb_ref, dcw_ref, dlng_ref, dlnb_ref, dwsp_ref, dbs_ref, dgm_ref):
                ref[...] = jnp.zeros_like(ref)

        mask = _tril_mask()
        wm = [(wsp_ref[hh] * mask).astype(BF16) for hh in range(HEADS)]
        hv = h_ref[...]
        dx1 = dx1_ref[...]
        dx1b_ref[...] = dx1.astype(BF16)
        dx1e = jnp.concatenate([dx1, dx1n_ref[...]], axis=0).astype(BF16)
        dycat = _dot_nt(dx1e, wout_ref[...])

        hbe = jnp.concatenate([hp_ref[:, in_a:], hv[:, in_a:], hn_ref[:, in_a:]], axis=0)
        row = lax.broadcasted_iota(jnp.int32, (tee, 1), 0)
        zext = hbe[:, bw:2 * bw] * hbe[:, 2 * bw:]
        zext = jnp.where((i == 0) & (row < HALO), 0.0, zext)
        z2e, z1e = _conv_taps(zext)
        cw = cw_ref[...]
        conv_e = (cw[0:1] * z2e + cw[1:2] * z1e + cw[2:3] * zext)[HALO:]
        gate_b_e = hbe[HALO:, :bw]
        sc_e = gate_b_e * conv_e
        rb = _rstd(sc_e)
        dyb = dycat[:, aw:]
        gdy = dyb * gb_ref[...]
        dsc_e = rb * gdy - sc_e * (rb * rb * rb) * (jnp.sum(gdy * sc_e, axis=-1, keepdims=True) * (1.0 / bw))
        dgb_ref[...] += jnp.sum((dyb * (sc_e * rb))[:tm], axis=0, keepdims=True)
        dconv_e = dsc_e * gate_b_e
        dconv_e = jnp.where((i == nt - 1) & (row[:te] >= tm), 0.0, dconv_e)
        dconv = dconv_e[:tm]
        dc1 = pltpu.roll(dconv_e, te - 1, 0)[:tm]
        dc2 = pltpu.roll(dconv_e, te - 2, 0)[:tm]
        dz = cw[2:3] * dconv + cw[1:2] * dc1 + cw[0:1] * dc2
        z = zext[HALO:HALO + tm]
        z1 = z1e[HALO:HALO + tm]
        z2 = z2e[HALO:HALO + tm]
        dcw_ref[0:1, :] += jnp.sum(dconv * z2, axis=0, keepdims=True)
        dcw_ref[1:2, :] += jnp.sum(dconv * z1, axis=0, keepdims=True)
        dcw_ref[2:3, :] += jnp.sum(dconv * z, axis=0, keepdims=True)
        dh_ref[:, in_a:in_a + bw] = (dsc_e[:tm] * conv_e[:tm]).astype(BF16)
        dh_ref[:, in_a + bw:in_a + 2 * bw] = (dz * hv[:, in_a + 2 * bw:]).astype(BF16)
        dh_ref[:, in_a + 2 * bw:] = (dz * hv[:, in_a + bw:in_a + 2 * bw]).astype(BF16)

        ha = hv[:, :in_a]
        th, u, xhat, rl, vln = _sgu_forward(ha, lng_ref[...], lnb_ref[...], wm, bt_ref[...], mixed_s)
        mixed = mixed_s[...]
        sg = u * mixed
        dsg, dga = _rms_bwd(dycat[:tm, :aw], sg, _rstd(sg), ga_ref[...])
        dga_ref[...] += dga
        du = dsg * mixed
        dmixed = dsg * u
        dmb = dmixed.astype(BF16)
        for n in range(tm // CHUNK):
            rows = slice(n * CHUNK, (n + 1) * CHUNK)
            dbs_ref[...] += dmixed[rows]
            for hh in range(HEADS):
                cols = slice(hh * hd, (hh + 1) * hd)
                dvln_s[rows, cols] = _dot_tn(wm[hh], dmb[rows, cols])
                dwsp_ref[hh] += mask * _dot_nt(dmb[rows, cols], vln[rows, cols])
        dvln = dvln_s[...]
        dlng_ref[...] += jnp.sum(dvln * xhat, axis=0, keepdims=True)
        dlnb_ref[...] += jnp.sum(dvln, axis=0, keepdims=True)
        dxh = dvln * lng_ref[...]
        dv = rl * (dxh - jnp.mean(dxh, axis=-1, keepdims=True) - xhat * jnp.mean(dxh * xhat, axis=-1, keepdims=True))
        dh_ref[:, :in_a] = (jnp.concatenate([du, dv], axis=-1) * _gelu_grad(ha, th)).astype(BF16)

        dxn = _dot_nt(dh_ref[...], win_ref[...])
        xv = x_ref[...]
        dx, dgm = _rms_bwd(dxn, xv, _rstd(xv), gm_ref[...])
        dgm_ref[...] += dgm
        dx_ref[...] = dx1 + dx

    full = lambda shape: pl.BlockSpec(shape, lambda i: (0,) * len(shape))
    tok = pl.BlockSpec((tm, d), lambda i: (i, 0))
    nxt = lambda i: (jnp.minimum((i + 1) * hb_blocks, last_blk), 0)
    prv = lambda i: (jnp.maximum(i * hb_blocks - 1, 0), 0)
    return _hosted(
        body, carry, grid=(nt,),
        in_specs=[tok, pl.BlockSpec((HALO, d), nxt), tok, full((1, d)),
                  pl.BlockSpec((tm, n_in), lambda i: (i, 0)), pl.BlockSpec((HALO, n_in), prv), pl.BlockSpec((HALO, n_in), nxt),
                  full((1, aw)), full((1, aw)), full((HEADS, CHUNK, CHUNK)), full((CHUNK, HEADS)), full((3, bw)),
                  full((1, aw)), full((1, bw)), full((d, d)), full((d, n_in))],
        out_specs=[tok, pl.BlockSpec((tm, n_in), lambda i: (i, 0)), tok,
                   full((1, aw)), full((1, bw)), full((SUB, bw)), full((1, aw)), full((1, aw)),
                   full((HEADS, CHUNK, CHUNK)), full((CHUNK, aw)), full((1, d))],
        out_shape=(SDS((s, d), BF16), SDS((s, n_in), BF16), SDS((s, d), F32),
                   SDS((1, aw), F32), SDS((1, bw), F32), SDS((SUB, bw), F32), SDS((1, aw), F32), SDS((1, aw), F32),
                   SDS((HEADS, CHUNK, CHUNK), F32), SDS((CHUNK, aw), F32), SDS((1, d), F32)),
        scratch_shapes=[pltpu.VMEM((tm, aw), F32), pltpu.VMEM((tm, aw), F32)],
        compiler_params=_arb(1), name="mix_backward")(dx1, dx1, x, g_mix, h, h, h, lng, lnb, w_sp, bt, conv_w, ga, gb, w_out, w_in)


def _bias_grad(dbs):
    aw = dbs.shape[1]
    hd = aw // HEADS

    def body(dbs_ref, out_ref):
        ones = jnp.ones((SUB, hd), F32)
        for hh in range(HEADS):
            r = lax.dot_general(ones, dbs_ref[:, hh * hd:(hh + 1) * hd], (((1,), (1,)), ((), ())),
                                precision=lax.Precision.HIGHEST, preferred_element_type=F32)
            out_ref[hh:hh + 1, :] = r[0:1]

    return _pcall(body, out_shape=SDS((HEADS, CHUNK), F32), name="bias_grad")(dbs)


def _wgrad_body(a_ref, b_ref, o_ref):
    o_ref[...] = _dot_tn(a_ref[...], b_ref[...])


def _wgrad(a, b, name, carry=()):
    k, m = a.shape
    n = b.shape[1]
    tm = _tile(m, 512, LANES)
    tn = _tile(n, 1024, LANES)
    return _hosted(
        functools.partial(_wgrad_body), carry, grid=(m // tm, n // tn),
        in_specs=[pl.BlockSpec((k, tm), lambda i, j: (0, i)), pl.BlockSpec((k, tn), lambda i, j: (0, j))],
        out_specs=pl.BlockSpec((tm, tn), lambda i, j: (i, j)),
        out_shape=SDS((m, n), F32), compiler_params=_arb(2), name=name)(a, b)


def _wgrad_blocked_lhs(a, b, name, carry=()):
    nb, k, t = a.shape
    n = b.shape[1]
    tn = _tile(n, 1024, LANES)
    return _hosted(
        functools.partial(_wgrad_body), carry, grid=(nb, n // tn),
        in_specs=[pl.BlockSpec((None, k, t), lambda i, j: (i, 0, 0)), pl.BlockSpec((k, tn), lambda i, j: (0, j))],
        out_specs=pl.BlockSpec((t, tn), lambda i, j: (i, j)),
        out_shape=SDS((nb * t, n), F32), compiler_params=_arb(2), name=name)(a, b)


def _wgrad_blocked_rhs(a, b, name, carry=()):
    k, m = a.shape
    nb, _, t = b.shape
    tm = _tile(m, 512, LANES)
    return _hosted(
        functools.partial(_wgrad_body), carry, grid=(m // tm, nb),
        in_specs=[pl.BlockSpec((k, tm), lambda i, j: (0, i)), pl.BlockSpec((None, k, t), lambda i, j: (j, 0, 0))],
        out_specs=pl.BlockSpec((None, tm, t), lambda i, j: (j, i, 0)),
        out_shape=SDS((nb, m, t), F32), compiler_params=_arb(2), name=name)(a, b)


def _unblock_cols(wb, name):
    nb, r, t = wb.shape
    tr = _tile(r, 256, 16)

    def body(w_ref, o_ref):
        o_ref[...] = jnp.concatenate([w_ref[j].astype(F32) for j in range(nb)], axis=-1).astype(o_ref.dtype)

    return _pcall(
        body, grid=(r // tr,),
        in_specs=[pl.BlockSpec((nb, tr, t), lambda i: (0, i, 0))], out_specs=pl.BlockSpec((tr, nb * t), lambda i: (i, 0)),
        out_shape=SDS((r, nb * t), wb.dtype), compiler_params=_arb(1), name=name)(wb)


def _block_cols(w, nb, name, carry=()):
    r, n = w.shape
    t = n // nb
    tr = _tile(r, 256, 16)

    def body(w_ref, o_ref):
        wv = w_ref[...]
        for j in range(nb):
            o_ref[j] = wv[:, j * t:(j + 1) * t]

    return _hosted(
        body, carry, grid=(r // tr,),
        in_specs=[pl.BlockSpec((tr, n), lambda i: (i, 0))], out_specs=pl.BlockSpec((nb, tr, t), lambda i: (0, i, 0)),
        out_shape=SDS((nb, r, t), w.dtype), compiler_params=_arb(1), name=name)(w)


def _place():
    x, y, c = lax.axis_index("x"), lax.axis_index("y"), lax.axis_index("c")
    return x, y, c, [(1 - x, y), (x, 1 - y), (1 - x, 1 - y)]


def _all_gather(shards):
    n = len(shards)

    def copies(ins, outs, sems):
        send_sems, recv_sems, local_sems = sems
        x, y, c, chips = _place()
        me, sibling = (x, y, c), (x, y, 1 - c)

        def blk(a, px, py, pc):
            return outs[a].at[4 * px + 2 * py + pc]

        def copy(a, k, block, to, src=None):
            return pltpu.make_async_remote_copy(
                src_ref=blk(a, *block) if src is None else src, dst_ref=blk(a, *block),
                send_sem=send_sems.at[7 * a + k], recv_sem=recv_sems.at[7 * a + k], device_id=to, device_id_type=MESH)

        first = [copy(a, 1 + j, me, (*chip, c), src=ins[a]) for a in range(n) for j, chip in enumerate(chips)]
        first += [copy(a, 0, me, sibling, src=ins[a]) for a in range(n)]
        mine = [pltpu.make_async_copy(ins[a], blk(a, *me), local_sems.at[a]) for a in range(n)]
        landed = [[copy(a, 1 + j, (*chip, c), me) for j, chip in enumerate(chips)] for a in range(n)]
        passed = [[copy(a, 4 + j, (*chip, c), sibling) for j, chip in enumerate(chips)] for a in range(n)]
        from_sibling = [copy(a, 0, sibling, me) for a in range(n)]
        from_sibling += [copy(a, 4 + j, (*chip, 1 - c), me) for a in range(n) for j, chip in enumerate(chips)]
        return first, mine, landed, passed, from_sibling

    def start(ins, outs, sems):
        first, mine, _, _, _ = copies(ins, outs, sems)
        for cp in first + mine:
            cp.start()

    def finish(ins, outs, sems):
        first, mine, landed, passed, from_sibling = copies(ins, outs, sems)
        for a in range(n):
            for arrived, onward in zip(landed[a], passed[a]):
                arrived.wait_recv()
                onward.start()
        for cp in from_sibling:
            cp.wait_recv()
        for cp in first + [cp for row in passed for cp in row]:
            cp.wait_send()
        for cp in mine:
            cp.wait()

    return _Exchange(shards, [SDS((N_DEV,) + s.shape, s.dtype) for s in shards],
                     [pltpu.SemaphoreType.DMA((7 * n,)), pltpu.SemaphoreType.DMA((7 * n,)), pltpu.SemaphoreType.DMA((n,))],
                     start, finish)


def _swap_exchange(ins, out_shape, per, copies):
    def start(i, o, sems):
        for cp in copies(i, o, sems):
            cp.start()

    def finish(i, o, sems):
        for cp in copies(i, o, sems):
            cp.wait()

    n = per * len(ins)
    return _Exchange(ins, out_shape, [pltpu.SemaphoreType.DMA((n,)), pltpu.SemaphoreType.DMA((n,))], start, finish)


def _exchange_c(gs):
    def copies(ins, outs, sems):
        x, y, c, _ = _place()
        return [pltpu.make_async_remote_copy(
                    src_ref=ins[a].at[2 * k + 1 - c], dst_ref=outs[a].at[k],
                    send_sem=sems[0].at[4 * a + k], recv_sem=sems[1].at[4 * a + k],
                    device_id=(x, y, 1 - c), device_id_type=MESH)
                for a in range(len(gs)) for k in range(4)]

    return _swap_exchange(gs, [SDS((4,) + g.shape[1:], g.dtype) for g in gs], 4, copies)


def _exchange_xy(sends):
    def copies(ins, outs, sems):
        x, y, c, chips = _place()
        return [pltpu.make_async_remote_copy(
                    src_ref=ins[a].at[t], dst_ref=outs[a].at[t],
                    send_sem=sems[0].at[3 * a + t], recv_sem=sems[1].at[3 * a + t],
                    device_id=(*chips[t], c), device_id_type=MESH)
                for a in range(len(sends)) for t in range(3)]

    return _swap_exchange(sends, [SDS(s.shape, s.dtype) for s in sends], 3, copies)


def _rs_combine(g, recv, pos, name):
    _, r, cdim = g.shape
    tr = _tile(r, 256, 16)

    def body(pos_ref, g0, r0, g1, r1, g2, r2, g3, r3, keep_ref, send_ref):
        keep_ref[...] = g0[...] + r0[...]
        send_ref[0] = (g1[...] + r1[...]).astype(BF16)
        send_ref[1] = (g2[...] + r2[...]).astype(BF16)
        send_ref[2] = (g3[...] + r3[...]).astype(BF16)

    def k_of(p, t):
        px = p[0] if t in (0, 2) else 1 - p[0]
        py = p[1] if t in (0, 1) else 1 - p[1]
        return 2 * px + py

    blk = (None, tr, cdim)
    in_specs = []
    for t in range(4):
        in_specs.append(pl.BlockSpec(blk, functools.partial(lambda j, p, t: (2 * k_of(p, t) + p[2], j, 0), t=t)))
        in_specs.append(pl.BlockSpec(blk, functools.partial(lambda j, p, t: (k_of(p, t), j, 0), t=t)))
    return _pcall(
        body, out_shape=(SDS((r, cdim), F32), SDS((3, r, cdim), BF16)),
        grid_spec=pltpu.PrefetchScalarGridSpec(
            num_scalar_prefetch=1, grid=(r // tr,), in_specs=in_specs,
            out_specs=[pl.BlockSpec((tr, cdim), lambda j, p: (j, 0)), pl.BlockSpec((3, tr, cdim), lambda j, p: (0, j, 0))]),
        compiler_params=_arb(1), name=name)(pos, g, recv, g, recv, g, recv, g, recv)


def _adamw_shard(keep, recv, w, m, v, name):
    r, cdim = w.shape
    tr = _tile(r, 256, 16)

    def body(k_ref, r_ref, w_ref, m_ref, v_ref, g_ref, d_ref, nm_ref, nv_ref):
        g = ((k_ref[...] + r_ref[0].astype(F32)) + r_ref[1].astype(F32)) + r_ref[2].astype(F32)
        g_ref[...] = g
        d_ref[...], nm_ref[...], nv_ref[...] = _adamw(w_ref[...], g, m_ref[...], v_ref[...])

    blk = pl.BlockSpec((tr, cdim), lambda j: (j, 0))
    out = SDS((r, cdim), F32)
    return _pcall(body, grid=(r // tr,), in_specs=[blk, pl.BlockSpec((3, tr, cdim), lambda j: (0, j, 0)), blk, blk, blk],
                  out_specs=[blk] * 4, out_shape=(out,) * 4, compiler_params=_arb(1), name=name)(keep, recv, w, m, v)


def _adamw_small(gathered, seg, params, conv_rows):
    names = list(params)
    c0, cn = conv_rows

    def body(*refs):
        gat_ref = refs[0]
        ins = refs[1:1 + 3 * len(names)]
        outs = refs[1 + 3 * len(names):]

        def total(r0, rn):
            tot = gat_ref[0, r0:r0 + rn, :]
            for dev in range(1, N_DEV):
                tot = tot + gat_ref[dev, r0:r0 + rn, :]
            return tot

        for k, nm in enumerate(names):
            g = total(*seg[nm])
            w_ref, m_ref, v_ref = ins[3 * k:3 * k + 3]
            g_ref, d_ref, nm_ref, nv_ref = outs[4 * k:4 * k + 4]
            g_ref[...] = g
            d_ref[...], nm_ref[...], nv_ref[...] = _adamw(w_ref[...], g, m_ref[...], v_ref[...])
        outs[-2][...] = total(c0, cn)
        outs[-1][...] = total(*seg["loss"])

    flat_in = [a for nm in names for a in params[nm]]
    out_shape = []
    for nm in names:
        out_shape += [SDS(params[nm][0].shape, F32)] * 4
    out_shape += [SDS((cn, LANES), F32), SDS((seg["loss"][1], LANES), F32)]
    res = _pcall(body, out_shape=tuple(out_shape), name="adamw_small")(gathered, *flat_in)
    per = {nm: res[4 * k:4 * k + 4] for k, nm in enumerate(names)}
    return per, res[-2], res[-1]


def _adamw_one(w, g, m, v, name):
    def body(w_ref, g_ref, m_ref, v_ref, d_ref, nm_ref, nv_ref):
        d_ref[...], nm_ref[...], nv_ref[...] = _adamw(w_ref[...], g_ref[...], m_ref[...], v_ref[...])

    return _pcall(body, out_shape=(SDS(w.shape, F32),) * 3, name=name)(w, g, m, v)


def _rows128(a):
    return a.reshape(-1, LANES)


def _pack_small(gs, loss_tile):
    seg, pieces, row = {}, [], 0
    for nm in SMALL + ("conv_w", "loss"):
        piece = loss_tile if nm == "loss" else _rows128(gs[nm])
        rn = _round_up(piece.shape[0], SUB)
        pieces.append(jnp.pad(piece, ((0, rn - piece.shape[0]), (0, 0))))
        seg[nm] = (row, piece.shape[0])
        row += rn
    return jnp.concatenate(pieces, axis=0), seg


def _step(x, mem, target, wb, conv_w, sp, pos):
    s, d = x.shape
    tm = min(TOKEN_TILE, s)
    rows = lambda w8: w8.reshape(-1, w8.shape[2])
    shards = lambda g: g.reshape((N_DEV, -1) + g.shape[1:])
    bt = sp["b_spatial"].T

    (w_in8, conv8), = _run_exchanges([_all_gather([wb["w_in"], conv_w])], "gather_w_in")
    conv_full = conv8.transpose(1, 0, 2).reshape(3, -1)
    w_in = _unblock_cols(w_in8, "unblock_w_in")
    (xn1, h), ((w_out8, w_kv8),) = _in_forward(
        x, sp["ln_mix_g"], w_in, tm, carry=[_all_gather([wb["w_out"], wb["w_kv"]])])
    w_out = rows(w_out8)
    (ycat, x1), ((w_q8, w_o8, w_down8),) = _mix_forward(
        h, x, sp["sgu_ln_g"], sp["sgu_ln_b"], sp["w_spatial"], bt, conv_full, sp["grp_norm_a"], sp["grp_norm_b"], w_out, tm,
        carry=[_all_gather([wb["w_q"], wb["w_o"], wb["w_down"]])])
    w_q, w_o, w_down = rows(w_q8), rows(w_o8), rows(w_down8)
    memn, kv = _kv_forward(mem, sp["ln_mem_g"], w_kv8)
    (xn2, q, o, x2), ((w_gu8,),) = _attn_forward(
        x1, sp["ln_attn_g"], w_q, kv, w_o, tm, carry=[_all_gather([wb["w_gate_up"]])])
    w_gu = w_gu8.reshape((2, N_DEV // 2) + w_gu8.shape[1:])
    xn3, gu, x3 = _ffn_forward(x2, sp["ln_ffn_g"], w_gu, w_down, tm)

    loss, d_lnf, d_lnffn, act, dgu, dx3b, dx2 = _ffn_backward(
        x3, target, sp["ln_final_g"], x2, sp["ln_ffn_g"], gu, w_gu, w_down, tm)
    part = {}
    g_gu, _ = _wgrad_blocked_rhs(xn3, dgu.reshape((N_DEV,) + dgu.shape[2:]), "wgrad_gate_up")
    g_down, ((rc_gu,),) = _wgrad_blocked_lhs(act, dx3b, "wgrad_down", carry=[_exchange_c([g_gu])])
    g_down = shards(g_down)
    keep_gu, send_gu = _rs_combine(g_gu, rc_gu, pos, "rs_combine_w_gate_up")
    (dx2b, dq, dx1, dkv, d_lnattn), ((rxy_gu,), (rc_down,)) = _attn_backward(
        dx2, x1, sp["ln_attn_g"], q, kv, w_q, w_o, tm, carry=[_exchange_xy([send_gu]), _exchange_c([g_down])])
    part["w_gate_up"] = (keep_gu, rxy_gu)
    keep_down, send_down = _rs_combine(g_down, rc_down, pos, "rs_combine_w_down")
    g_o, _ = _wgrad(o, dx2b, "wgrad_o")
    g_q, _ = _wgrad(xn2, dq, "wgrad_q")
    g_o, g_q = shards(g_o), shards(g_q)
    g_kv, d_lnmem = _kv_backward(dkv, memn, mem, sp["ln_mem_g"], w_kv8)
    ((dx1b, dh, dx, d_ga, d_gb, d_cw, d_lng, d_lnb, d_wsp, d_bs, d_lnmix),
     ((rxy_down,), (rc_o, rc_q, rc_kv))) = _mix_backward(
        dx1, x, sp["ln_mix_g"], h, sp["sgu_ln_g"], sp["sgu_ln_b"], sp["w_spatial"], bt, conv_full,
        sp["grp_norm_a"], sp["grp_norm_b"], w_out, w_in, tm,
        carry=[_exchange_xy([send_down]), _exchange_c([g_o, g_q, g_kv])])
    part["w_down"] = (keep_down, rxy_down)
    keep_o, send_o = _rs_combine(g_o, rc_o, pos, "rs_combine_w_o")
    keep_q, send_q = _rs_combine(g_q, rc_q, pos, "rs_combine_w_q")
    keep_kv, send_kv = _rs_combine(g_kv, rc_kv, pos, "rs_combine_w_kv")
    gs = {"ln_mix_g": d_lnmix, "sgu_ln_g": d_lng, "sgu_ln_b": d_lnb, "w_spatial": d_wsp, "b_spatial": _bias_grad(d_bs),
          "conv_w": d_cw[:3], "grp_norm_a": d_ga, "grp_norm_b": d_gb, "ln_attn_g": d_lnattn, "ln_mem_g": d_lnmem,
          "ln_ffn_g": d_lnffn, "ln_final_g": d_lnf}
    packed, seg = _pack_small(gs, loss)
    g_in, ((rxy_o, rxy_q, rxy_kv),) = _wgrad(xn1, dh, "wgrad_in", carry=[_exchange_xy([send_o, send_q, send_kv])])
    part["w_o"], part["w_q"], part["w_kv"] = (keep_o, rxy_o), (keep_q, rxy_q), (keep_kv, rxy_kv)
    g_in, ((small_all,),) = _block_cols(g_in, N_DEV, "block_grad_w_in", carry=[_all_gather([packed])])
    g_out, ((rc_in,),) = _wgrad(ycat, dx1b, "wgrad_out", carry=[_exchange_c([g_in])])
    g_out = shards(g_out)
    keep_in, send_in = _rs_combine(g_in, rc_in, pos, "rs_combine_w_in")
    (rc_out,), = _run_exchanges([_exchange_c([g_out])], "exchange_c_w_out")
    keep_out, send_out = _rs_combine(g_out, rc_out, pos, "rs_combine_w_out")
    (rxy_in, rxy_out), = _run_exchanges([_exchange_xy([send_in, send_out])], "exchange_xy_w_in_w_out")
    part["w_in"], part["w_out"] = (keep_in, rxy_in), (keep_out, rxy_out)
    return dx, part, small_all, seg


def kernel(x, mem, ln_mix_g, w_in, sgu_ln_g, sgu_ln_b, w_spatial, b_spatial, conv_w, grp_norm_a, grp_norm_b, w_out, ln_attn_g, ln_mem_g, w_q, w_kv, w_o, ln_ffn_g, w_gate_up, w_down, ln_final_g, loss_target, m_ln_mix_g, m_w_in, m_sgu_ln_g, m_sgu_ln_b, m_w_spatial, m_b_spatial, m_conv_w, m_grp_norm_a, m_grp_norm_b, m_w_out, m_ln_attn_g, m_ln_mem_g, m_w_q, m_w_kv, m_w_o, m_ln_ffn_g, m_w_gate_up, m_w_down, m_ln_final_g, v_ln_mix_g, v_w_in, v_sgu_ln_g, v_sgu_ln_b, v_w_spatial, v_b_spatial, v_conv_w, v_grp_norm_a, v_grp_norm_b, v_w_out, v_ln_attn_g, v_ln_mem_g, v_w_q, v_w_kv, v_w_o, v_ln_ffn_g, v_w_gate_up, v_w_down, v_ln_final_g):
    order = ["ln_mix_g", "w_in", "sgu_ln_g", "sgu_ln_b", "w_spatial", "b_spatial", "conv_w", "grp_norm_a", "grp_norm_b",
             "w_out", "ln_attn_g", "ln_mem_g", "w_q", "w_kv", "w_o", "ln_ffn_g", "w_gate_up", "w_down", "ln_final_g"]
    W = dict(ln_mix_g=ln_mix_g, w_in=w_in, sgu_ln_g=sgu_ln_g, sgu_ln_b=sgu_ln_b, w_spatial=w_spatial, b_spatial=b_spatial,
             conv_w=conv_w, grp_norm_a=grp_norm_a, grp_norm_b=grp_norm_b, w_out=w_out, ln_attn_g=ln_attn_g,
             ln_mem_g=ln_mem_g, w_q=w_q, w_kv=w_kv, w_o=w_o, ln_ffn_g=ln_ffn_g, w_gate_up=w_gate_up, w_down=w_down,
             ln_final_g=ln_final_g)
    M = dict(ln_mix_g=m_ln_mix_g, w_in=m_w_in, sgu_ln_g=m_sgu_ln_g, sgu_ln_b=m_sgu_ln_b, w_spatial=m_w_spatial,
             b_spatial=m_b_spatial, conv_w=m_conv_w, grp_norm_a=m_grp_norm_a, grp_norm_b=m_grp_norm_b, w_out=m_w_out,
             ln_attn_g=m_ln_attn_g, ln_mem_g=m_ln_mem_g, w_q=m_w_q, w_kv=m_w_kv, w_o=m_w_o, ln_ffn_g=m_ln_ffn_g,
             w_gate_up=m_w_gate_up, w_down=m_w_down, ln_final_g=m_ln_final_g)
    V = dict(ln_mix_g=v_ln_mix_g, w_in=v_w_in, sgu_ln_g=v_sgu_ln_g, sgu_ln_b=v_sgu_ln_b, w_spatial=v_w_spatial,
             b_spatial=v_b_spatial, conv_w=v_conv_w, grp_norm_a=v_grp_norm_a, grp_norm_b=v_grp_norm_b, w_out=v_w_out,
             ln_attn_g=v_ln_attn_g, ln_mem_g=v_ln_mem_g, w_q=v_w_q, w_kv=v_w_kv, w_o=v_w_o, ln_ffn_g=v_ln_ffn_g,
             w_gate_up=v_w_gate_up, w_down=v_w_down, ln_final_g=v_ln_final_g)

    bw = conv_w.shape[1] * N_DEV
    pos = jnp.stack([lax.axis_index("x"), lax.axis_index("y"), lax.axis_index("c")]).astype(jnp.int32)
    me = 4 * pos[0] + 2 * pos[1] + pos[2]

    sp = {nm: (W[nm].reshape(1, -1) if W[nm].ndim == 1 else W[nm]) for nm in SMALL}
    wb = {nm: W[nm].astype(BF16) for nm in BIG}
    grad_x, part, small_all, seg = _step(x[0], mem[0], loss_target[0], wb, conv_w, sp, pos)

    out = {}
    for nm in BIG:
        out[nm] = tuple(_adamw_shard(part[nm][0], part[nm][1], W[nm], M[nm], V[nm], "adamw_" + nm))

    params = {nm: (_rows128(W[nm]), _rows128(M[nm]), _rows128(V[nm])) for nm in SMALL}
    per, conv_g_rows, loss_sum = _adamw_small(small_all, seg, params, seg["conv_w"])
    for nm in SMALL:
        out[nm] = tuple(a.reshape(W[nm].shape) for a in per[nm])
    conv_g = lax.dynamic_slice_in_dim(conv_g_rows.reshape(3, bw), me * conv_w.shape[1], conv_w.shape[1], axis=1)
    out["conv_w"] = (conv_g,) + tuple(_adamw_one(conv_w, conv_g, m_conv_w, v_conv_w, "adamw_conv"))

    loss = loss_sum[0, 0]
    res = [loss, grad_x[None]]
    for k in range(4):
        res += [out[nm][k] for nm in order]
    return tuple(res)
```

```python
import functools

import jax
import jax.numpy as jnp
from jax import lax
from jax.experimental import pallas as pl
from jax.experimental.pallas import tpu as pltpu

F32 = jnp.float32
BF16 = jnp.bfloat16
SDS = jax.ShapeDtypeStruct
MESH = pl.DeviceIdType.MESH

EPS = 1e-6
N_DEV = 8
HEADS = 4
CHUNK = 128
HALO = 16
SUB = 8
LANES = 128
TOKEN_TILE = 512

ADAM_LR = 0.001
ADAM_B1 = 0.9
ADAM_B2 = 0.999
ADAM_EPS = 1e-08
ADAM_WD = 0.01
ADAM_STEP = 10

BIG = ("w_in", "w_out", "w_q", "w_kv", "w_o", "w_gate_up", "w_down")
SMALL = ("ln_mix_g", "sgu_ln_g", "sgu_ln_b", "w_spatial", "b_spatial", "grp_norm_a", "grp_norm_b",
         "ln_attn_g", "ln_mem_g", "ln_ffn_g", "ln_final_g")


class _Exchange:
    def __init__(self, ins, out_shape, sems, start, finish):
        self.ins, self.out_shape, self.sems, self.start, self.finish = list(ins), list(out_shape), list(sems), start, finish


def _pcall(body, carry=(), **kw):
    if carry:
        return functools.partial(_carrying_call, body, tuple(carry), kw)
    return pl.pallas_call(body, **kw)


def _carrying_call(body, carry, kw, *args):
    kw = dict(kw)
    out_shape = kw.pop("out_shape")
    single = not isinstance(out_shape, (tuple, list))
    outs_shape = (out_shape,) if single else tuple(out_shape)
    out_specs = kw.pop("out_specs")
    out_specs = [out_specs] if single else list(out_specs)
    in_specs = list(kw.pop("in_specs"))
    scratch = list(kw.pop("scratch_shapes", ()))
    grid = tuple(kw.get("grid", ()))
    n_in, n_out, n_scr = len(args), len(outs_shape), len(scratch)

    def split(refs, k, counts):
        parts = []
        for cnt in counts:
            parts.append(refs[k:k + cnt])
            k += cnt
        return parts, k

    def wrapped(*refs):
        cins, k = split(refs, n_in, [len(p.ins) for p in carry])
        outs = refs[k:k + n_out]
        couts, k = split(refs, k + n_out, [len(p.out_shape) for p in carry])
        scr = refs[k:k + n_scr]
        csems, _ = split(refs, k + n_scr, [len(p.sems) for p in carry])
        first, last = True, True
        for a, g in enumerate(grid):
            first = (pl.program_id(a) == 0) & first
            last = (pl.program_id(a) == g - 1) & last

        def start_all():
            for p, ci, co, cs in zip(carry, cins, couts, csems):
                p.start(ci, co, cs)

        def finish_all():
            for p, ci, co, cs in zip(carry, cins, couts, csems):
                p.finish(ci, co, cs)

        start_all() if not grid else pl.when(first)(start_all)
        body(*refs[:n_in], *outs, *scr)
        finish_all() if not grid else pl.when(last)(finish_all)

    c_in = [a for p in carry for a in p.ins]
    c_out = [s for p in carry for s in p.out_shape]
    c_sems = [s for p in carry for s in p.sems]
    res = _pcall(wrapped, out_shape=outs_shape + tuple(c_out), in_specs=in_specs + _hbm_specs(len(c_in)),
                 out_specs=out_specs + _hbm_specs(len(c_out)), scratch_shapes=scratch + c_sems, **kw)(*args, *c_in)
    own = res[0] if single else tuple(res[:n_out])
    landed, k = [], n_out
    for p in carry:
        landed.append(list(res[k:k + len(p.out_shape)]))
        k += len(p.out_shape)
    return own, landed


def _hbm_specs(n):
    return [pl.BlockSpec(memory_space=pl.ANY)] * n


def _hosted(body, carry, **kw):
    if carry:
        return _pcall(body, carry=carry, **kw)
    call = _pcall(body, **kw)
    return lambda *args: (call(*args), [])


def _run_exchanges(parts, name):
    def body(*refs):
        pass

    _, landed = _pcall(body, carry=parts, out_shape=(), in_specs=[], out_specs=[], name=name)()
    return landed


def _arb(n):
    return pltpu.CompilerParams(dimension_semantics=("arbitrary",) * n)


def _tile(n, target, mult):
    best = None
    for t in range(mult, min(n, target) + 1, mult):
        if n % t == 0:
            best = t
    return n if best is None else best


def _round_up(n, m):
    return (n + m - 1) // m * m


def _dot(a, b):
    return jnp.dot(a, b, preferred_element_type=F32)


def _dot_nt(a, b):
    return lax.dot_general(a, b, (((1,), (1,)), ((), ())), preferred_element_type=F32)


def _dot_tn(a, b):
    return lax.dot_general(a, b, (((0,), (0,)), ((), ())), preferred_element_type=F32)


def _rstd(x):
    return lax.rsqrt(jnp.mean(x * x, axis=-1, keepdims=True) + EPS)


def _rms_bwd(dy, x, r, g):
    gdy = dy * g
    proj = jnp.sum(gdy * x, axis=-1, keepdims=True) * (1.0 / x.shape[-1])
    dx = r * gdy - x * (r * r * r) * proj
    dg = jnp.sum(dy * (x * r), axis=0, keepdims=True)
    return dx, dg


_GELU_C = 0.7978845608028654
_GELU_A = 0.044715


def _gelu(x):
    t = jnp.tanh(_GELU_C * (x + _GELU_A * x * x * x))
    return 0.5 * x * (1.0 + t), t


def _gelu_grad(x, t):
    return 0.5 * (1.0 + t) + 0.5 * x * (1.0 - t * t) * (_GELU_C * (1.0 + 3.0 * _GELU_A * x * x))


def _sigmoid(x):
    return 1.0 / (1.0 + jnp.exp(-x))


def _softmax(s):
    m = jnp.max(s, axis=-1, keepdims=True)
    e = jnp.exp(s - m)
    return e / jnp.sum(e, axis=-1, keepdims=True)


def _adamw(w, g, m, v):
    m = ADAM_B1 * m + (1.0 - ADAM_B1) * g
    v = ADAM_B2 * v + (1.0 - ADAM_B2) * (g * g)
    m_hat = m / (1.0 - ADAM_B1 ** ADAM_STEP)
    v_hat = v / (1.0 - ADAM_B2 ** ADAM_STEP)
    delta = -ADAM_LR * (m_hat / (jnp.sqrt(v_hat) + ADAM_EPS) + ADAM_WD * w)
    return delta, m, v


def _tril_mask():
    t = lax.broadcasted_iota(jnp.int32, (CHUNK, CHUNK), 0)
    s = lax.broadcasted_iota(jnp.int32, (CHUNK, CHUNK), 1)
    return (s <= t).astype(F32)


def _sgu_forward(ha, lng, lnb, wm, bt, mixed_s):
    aw = ha.shape[1] // 2
    hd = aw // HEADS
    a, th = _gelu(ha)
    u = a[:, :aw]
    v = a[:, aw:]
    mu = jnp.mean(v, axis=-1, keepdims=True)
    vc = v - mu
    rl = lax.rsqrt(jnp.mean(vc * vc, axis=-1, keepdims=True) + EPS)
    xhat = vc * rl
    vln = (xhat * lng + lnb).astype(BF16)
    for n in range(ha.shape[0] // CHUNK):
        rows = slice(n * CHUNK, (n + 1) * CHUNK)
        for h in range(HEADS):
            cols = slice(h * hd, (h + 1) * hd)
            mixed_s[rows, cols] = _dot(wm[h], vln[rows, cols]) + bt[:, h:h + 1]
    return th, u, xhat, rl, vln


def _conv_taps(zext):
    return pltpu.roll(zext, 2, 0), pltpu.roll(zext, 1, 0)


def _kv_forward(mem, g_mem, w_kv):
    ml, d = mem.shape
    xd = w_kv.shape[2]

    def body(mem_ref, g_ref, w_ref, memn_ref, kv_ref):
        x = mem_ref[...]
        memn = (x * _rstd(x) * g_ref[...]).astype(BF16)
        memn_ref[...] = memn
        for j in range(2 * HEADS):
            kv_ref[j] = _dot(memn, w_ref[j]).astype(BF16)

    return _pcall(body, out_shape=(SDS((ml, d), BF16), SDS((2 * HEADS, ml, xd), BF16)), name="kv_forward")(mem, g_mem, w_kv)


def _in_forward(x, g, w_in, tm, carry=()):
    s, d = x.shape
    n_in = w_in.shape[1]

    def body(x_ref, g_ref, w_ref, xn_ref, h_ref):
        xv = x_ref[...]
        xn = (xv * _rstd(xv) * g_ref[...]).astype(BF16)
        xn_ref[...] = xn
        h_ref[...] = _dot(xn, w_ref[...])

    return _hosted(
        body, carry, grid=(s // tm,),
        in_specs=[pl.BlockSpec((tm, d), lambda i: (i, 0)), pl.BlockSpec((1, d), lambda i: (0, 0)),
                  pl.BlockSpec((d, n_in), lambda i: (0, 0))],
        out_specs=[pl.BlockSpec((tm, d), lambda i: (i, 0)), pl.BlockSpec((tm, n_in), lambda i: (i, 0))],
        out_shape=(SDS((s, d), BF16), SDS((s, n_in), F32)),
        compiler_params=_arb(1), name="in_forward")(x, g, w_in)


def _mix_forward(h, x, lng, lnb, w_sp, bt, conv_w, ga, gb, w_out, tm, carry=()):
    s, d = x.shape
    n_in = h.shape[1]
    aw = lng.shape[1]
    bw = d - aw
    in_a = 2 * aw
    hb_blocks = tm // HALO

    def body(h_ref, hprev_ref, x_ref, lng_ref, lnb_ref, wsp_ref, bt_ref, cw_ref, ga_ref, gb_ref, wout_ref,
             ycat_ref, x1_ref, mixed_s):
        i = pl.program_id(0)
        mask = _tril_mask()
        wm = [(wsp_ref[hh] * mask).astype(BF16) for hh in range(HEADS)]
        hv = h_ref[...]
        _, u, _, _, _ = _sgu_forward(hv[:, :in_a], lng_ref[...], lnb_ref[...], wm, bt_ref[...], mixed_s)
        sg = u * mixed_s[...]
        ycat_ref[:, :aw] = (sg * _rstd(sg) * ga_ref[...]).astype(BF16)

        gate_b = hv[:, in_a:in_a + bw]
        z = hv[:, in_a + bw:in_a + 2 * bw] * hv[:, in_a + 2 * bw:]
        hp = hprev_ref[...]
        zp = hp[:, in_a + bw:in_a + 2 * bw] * hp[:, in_a + 2 * bw:]
        zp = jnp.where(i == 0, 0.0, zp)
        zext = jnp.concatenate([zp, z], axis=0)
        z2, z1 = _conv_taps(zext)
        cw = cw_ref[...]
        conv = cw[0:1] * z2[HALO:] + cw[1:2] * z1[HALO:] + cw[2:3] * z
        sc = gate_b * conv
        ycat_ref[:, aw:] = (sc * _rstd(sc) * gb_ref[...]).astype(BF16)
        x1_ref[...] = x_ref[...] + _dot(ycat_ref[...], wout_ref[...])

    full = lambda shape: pl.BlockSpec(shape, lambda i: (0,) * len(shape))
    return _hosted(
        body, carry, grid=(s // tm,),
        in_specs=[pl.BlockSpec((tm, n_in), lambda i: (i, 0)),
                  pl.BlockSpec((HALO, n_in), lambda i: (jnp.maximum(i * hb_blocks - 1, 0), 0)),
                  pl.BlockSpec((tm, d), lambda i: (i, 0)),
                  full((1, aw)), full((1, aw)), full((HEADS, CHUNK, CHUNK)), full((CHUNK, HEADS)),
                  full((3, bw)), full((1, aw)), full((1, bw)), full((d, d))],
        out_specs=[pl.BlockSpec((tm, d), lambda i: (i, 0)), pl.BlockSpec((tm, d), lambda i: (i, 0))],
        out_shape=(SDS((s, d), BF16), SDS((s, d), F32)),
        scratch_shapes=[pltpu.VMEM((tm, aw), F32)],
        compiler_params=_arb(1), name="mix_forward")(h, h, x, lng, lnb, w_sp, bt, conv_w, ga, gb, w_out)


def _attn_forward(x1, g, w_q, kv, w_o, tm, carry=()):
    s, d = x1.shape
    _, ml, xd = kv.shape
    scale = xd ** -0.5

    def body(x1_ref, g_ref, wq_ref, kv_ref, wo_ref, xn_ref, q_ref, o_ref, x2_ref):
        xv = x1_ref[...]
        xn = (xv * _rstd(xv) * g_ref[...]).astype(BF16)
        xn_ref[...] = xn
        q_ref[...] = _dot(xn, wq_ref[...]).astype(BF16)
        for hh in range(HEADS):
            cols = slice(hh * xd, (hh + 1) * xd)
            p = _softmax(_dot_nt(q_ref[:, cols], kv_ref[hh]) * scale)
            o_ref[:, cols] = _dot(p.astype(BF16), kv_ref[HEADS + hh]).astype(BF16)
        x2_ref[...] = xv + _dot(o_ref[...], wo_ref[...])

    tok = pl.BlockSpec((tm, d), lambda i: (i, 0))
    return _hosted(
        body, carry, grid=(s // tm,),
        in_specs=[tok, pl.BlockSpec((1, d), lambda i: (0, 0)), pl.BlockSpec((d, d), lambda i: (0, 0)),
                  pl.BlockSpec((2 * HEADS, ml, xd), lambda i: (0, 0, 0)), pl.BlockSpec((d, d), lambda i: (0, 0))],
        out_specs=[tok, tok, tok, tok],
        out_shape=(SDS((s, d), BF16), SDS((s, d), BF16), SDS((s, d), BF16), SDS((s, d), F32)),
        compiler_params=_arb(1), name="attn_forward")(x1, g, w_q, kv, w_o)


def _ffn_forward(x2, g, w_gu, w_down, tm):
    s, d = x2.shape
    _, nf, _, tf = w_gu.shape

    def body(x2_ref, g_ref, wgu_ref, wd_ref, xn_ref, gu_ref, x3_ref):
        f = pl.program_id(1)

        @pl.when(f == 0)
        def _():
            xv = x2_ref[...]
            xn_ref[...] = (xv * _rstd(xv) * g_ref[...]).astype(BF16)
            x3_ref[...] = xv

        xn = xn_ref[...]
        gate = _dot(xn, wgu_ref[0])
        up = _dot(xn, wgu_ref[1])
        gu_ref[0] = gate.astype(BF16)
        gu_ref[1] = up.astype(BF16)
        act = (gate * _sigmoid(gate) * up).astype(BF16)
        x3_ref[...] += _dot(act, wd_ref[...])

    tok = pl.BlockSpec((tm, d), lambda i, f: (i, 0))
    return _pcall(
        body, grid=(s // tm, nf),
        in_specs=[tok, pl.BlockSpec((1, d), lambda i, f: (0, 0)),
                  pl.BlockSpec((2, None, d, tf), lambda i, f: (0, f, 0, 0)),
                  pl.BlockSpec((tf, d), lambda i, f: (f, 0))],
        out_specs=[tok, pl.BlockSpec((2, None, tm, tf), lambda i, f: (0, f, i, 0)), tok],
        out_shape=(SDS((s, d), BF16), SDS((2, nf, s, tf), BF16), SDS((s, d), F32)),
        compiler_params=_arb(2), name="ffn_forward")(x2, g, w_gu, w_down)


def _ffn_backward(x3, target, g_final, x2, g_ffn, gu, w_gu_t, w_down_t, tm):
    s, d = x3.shape
    _, nf, tf, _ = w_gu_t.shape

    def body(x3_ref, tgt_ref, gf_ref, x2_ref, g2_ref, gu_ref, wgu_ref, wd_ref,
             loss_ref, dgf_ref, dg2_ref, act_ref, dgu_ref, dx3b_ref, dx2_ref, dx3_s, acc_s):
        i = pl.program_id(0)
        f = pl.program_id(1)

        @pl.when((i == 0) & (f == 0))
        def _():
            loss_ref[...] = jnp.zeros_like(loss_ref)
            dgf_ref[...] = jnp.zeros_like(dgf_ref)
            dg2_ref[...] = jnp.zeros_like(dg2_ref)

        @pl.when(f == 0)
        def _():
            xv = x3_ref[...]
            r = _rstd(xv)
            diff = xv * r * gf_ref[...] - tgt_ref[...]
            loss_ref[...] += 0.5 * jnp.sum(jnp.sum(diff * diff, axis=-1, keepdims=True), axis=0, keepdims=True) * (1.0 / d)
            dx3, dgf = _rms_bwd(diff * (1.0 / d), xv, r, gf_ref[...])
            dgf_ref[...] += dgf
            dx3_s[...] = dx3
            dx3b_ref[...] = dx3.astype(BF16)
            acc_s[...] = jnp.zeros_like(acc_s)

        dact = _dot(dx3b_ref[...], wd_ref[...])
        gv = gu_ref[0].astype(F32)
        uv = gu_ref[1].astype(F32)
        sg = _sigmoid(gv)
        silu = gv * sg
        act_ref[...] = (silu * uv).astype(BF16)
        dgate = (dact * uv * (sg * (1.0 + gv * (1.0 - sg)))).astype(BF16)
        dup = (dact * silu).astype(BF16)
        dgu_ref[0] = dgate
        dgu_ref[1] = dup
        acc_s[...] += _dot(dgate, wgu_ref[0]) + _dot(dup, wgu_ref[1])

        @pl.when(f == nf - 1)
        def _():
            xv = x2_ref[...]
            dxn, dg2 = _rms_bwd(acc_s[...], xv, _rstd(xv), g2_ref[...])
            dg2_ref[...] += dg2
            dx2_ref[...] = dx3_s[...] + dxn

    tok = pl.BlockSpec((tm, d), lambda i, f: (i, 0))
    pair = pl.BlockSpec((2, None, tm, tf), lambda i, f: (0, f, i, 0))
    vec = pl.BlockSpec((1, d), lambda i, f: (0, 0))
    return _pcall(
        body, grid=(s // tm, nf),
        in_specs=[tok, tok, vec, tok, vec, pair,
                  pl.BlockSpec((2, None, tf, d), lambda i, f: (0, f, 0, 0)),
                  pl.BlockSpec((None, d, tf), lambda i, f: (f, 0, 0))],
        out_specs=[pl.BlockSpec((SUB, LANES), lambda i, f: (0, 0)), vec, vec,
                   pl.BlockSpec((None, tm, tf), lambda i, f: (f, i, 0)), pair, tok, tok],
        out_shape=(SDS((SUB, LANES), F32), SDS((1, d), F32), SDS((1, d), F32), SDS((nf, s, tf), BF16),
                   SDS((2, nf, s, tf), BF16), SDS((s, d), BF16), SDS((s, d), F32)),
        scratch_shapes=[pltpu.VMEM((tm, d), F32), pltpu.VMEM((tm, d), F32)],
        compiler_params=_arb(2), name="ffn_backward")(x3, target, g_final, x2, g_ffn, gu, w_gu_t, w_down_t)


def _attn_backward(dx2, x1, g, q, kv, w_q, w_o, tm, carry=()):
    s, d = x1.shape
    _, ml, xd = kv.shape
    scale = xd ** -0.5

    def body(dx2_ref, x1_ref, g_ref, q_ref, kv_ref, wq_ref, wo_ref,
             dx2b_ref, dq_ref, dx1_ref, dkv_ref, dg_ref, do_s):
        i = pl.program_id(0)

        @pl.when(i == 0)
        def _():
            dkv_ref[...] = jnp.zeros_like(dkv_ref)
            dg_ref[...] = jnp.zeros_like(dg_ref)

        dx2 = dx2_ref[...]
        dx2b_ref[...] = dx2.astype(BF16)
        do_s[...] = _dot(dx2b_ref[...], wo_ref[...]).astype(BF16)
        for hh in range(HEADS):
            kc = slice(hh * xd, (hh + 1) * xd)
            qh = q_ref[:, kc]
            kh = kv_ref[hh]
            doh = do_s[:, kc]
            p = _softmax(_dot_nt(qh, kh) * scale)
            dp = _dot_nt(doh, kv_ref[HEADS + hh])
            dkv_ref[HEADS + hh] += _dot_tn(p.astype(BF16), doh)
            ds = (p * (dp - jnp.sum(dp * p, axis=-1, keepdims=True)) * scale).astype(BF16)
            dq_ref[:, kc] = _dot(ds, kh).astype(BF16)
            dkv_ref[hh] += _dot_tn(ds, qh)
        dxn = _dot(dq_ref[...], wq_ref[...])
        xv = x1_ref[...]
        dx, dg = _rms_bwd(dxn, xv, _rstd(xv), g_ref[...])
        dg_ref[...] += dg
        dx1_ref[...] = dx2 + dx

    tok = pl.BlockSpec((tm, d), lambda i: (i, 0))
    vec = pl.BlockSpec((1, d), lambda i: (0, 0))
    sq = pl.BlockSpec((d, d), lambda i: (0, 0))
    kvs = pl.BlockSpec((2 * HEADS, ml, xd), lambda i: (0, 0, 0))
    return _hosted(
        body, carry, grid=(s // tm,),
        in_specs=[tok, tok, vec, tok, kvs, sq, sq],
        out_specs=[tok, tok, tok, kvs, vec],
        out_shape=(SDS((s, d), BF16), SDS((s, d), BF16), SDS((s, d), F32), SDS((2 * HEADS, ml, xd), F32), SDS((1, d), F32)),
        scratch_shapes=[pltpu.VMEM((tm, d), BF16)],
        compiler_params=_arb(1), name="attn_backward")(dx2, x1, g, q, kv, w_q, w_o)


def _kv_backward(dkv, memn, mem, g_mem, w_kv):
    ml, d = mem.shape
    xd = w_kv.shape[2]

    def body(dkv_ref, memn_ref, mem_ref, g_ref, w_ref, dw_ref, dg_ref):
        dmemn = jnp.zeros((ml, d), F32)
        for j in range(2 * HEADS):
            dkvb = dkv_ref[j].astype(BF16)
            dw_ref[j] = _dot_tn(memn_ref[...], dkvb)
            dmemn = dmemn + _dot_nt(dkvb, w_ref[j])
        x = mem_ref[...]
        dg_ref[...] = jnp.sum(dmemn * (x * _rstd(x)), axis=0, keepdims=True)

    return _pcall(body, out_shape=(SDS((2 * HEADS, d, xd), F32), SDS((1, d), F32)), name="kv_backward")(dkv, memn, mem, g_mem, w_kv)


def _mix_backward(dx1, x, g_mix, h, lng, lnb, w_sp, bt, conv_w, ga, gb, w_out, w_in, tm, carry=()):
    s, d = x.shape
    n_in = h.shape[1]
    aw = lng.shape[1]
    bw = d - aw
    hd = aw // HEADS
    in_a = 2 * aw
    hb_blocks = tm // HALO
    last_blk = s // HALO - 1
    nt = s // tm
    te = tm + HALO
    tee = tm + 2 * HALO

    def body(dx1_ref, dx1n_ref, x_ref, gm_ref, h_ref, hp_ref, hn_ref, lng_ref, lnb_ref, wsp_ref, bt_ref, cw_ref,
             ga_ref, gb_ref, wout_ref, win_ref,
             dx1b_ref, dh_ref, dx_ref, dga_ref, dgb_ref, dcw_ref, dlng_ref, dlnb_ref, dwsp_ref, dbs_ref, dgm_ref,
             mixed_s, dvln_s):
        i = pl.program_id(0)

        @pl.when(i == 0)
        def _():
            for ref in (dga_ref, dgb_ref, dcw_ref, dlng_ref, dlnb_ref, dwsp_ref, dbs_ref, dgm_ref):
                ref[...] = jnp.zeros_like(ref)

        mask = _tril_mask()
        wm = [(wsp_ref[hh] * mask).astype(BF16) for hh in range(HEADS)]
        hv = h_ref[...]
        dx1 = dx1_ref[...]
        dx1b_ref[...] = dx1.astype(BF16)
        dx1e = jnp.concatenate([dx1, dx1n_ref[...]], axis=0).astype(BF16)
        dycat = _dot(dx1e, wout_ref[...])

        hbe = jnp.concatenate([hp_ref[:, in_a:], hv[:, in_a:], hn_ref[:, in_a:]], axis=0)
        row = lax.broadcasted_iota(jnp.int32, (tee, 1), 0)
        zext = hbe[:, bw:2 * bw] * hbe[:, 2 * bw:]
        zext = jnp.where((i == 0) & (row < HALO), 0.0, zext)
        z2e, z1e = _conv_taps(zext)
        cw = cw_ref[...]
        conv_e = (cw[0:1] * z2e + cw[1:2] * z1e + cw[2:3] * zext)[HALO:]
        gate_b_e = hbe[HALO:, :bw]
        sc_e = gate_b_e * conv_e
        rb = _rstd(sc_e)
        dyb = dycat[:, aw:]
        gdy = dyb * gb_ref[...]
        dsc_e = rb * gdy - sc_e * (rb * rb * rb) * (jnp.sum(gdy * sc_e, axis=-1, keepdims=True) * (1.0 / bw))
        dgb_ref[...] += jnp.sum((dyb * (sc_e * rb))[:tm], axis=0, keepdims=True)
        dconv_e = dsc_e * gate_b_e
        dconv_e = jnp.where((i == nt - 1) & (row[:te] >= tm), 0.0, dconv_e)
        dconv = dconv_e[:tm]
        dc1 = pltpu.roll(dconv_e, te - 1, 0)[:tm]
        dc2 = pltpu.roll(dconv_e, te - 2, 0)[:tm]
        dz = cw[2:3] * dconv + cw[1:2] * dc1 + cw[0:1] * dc2
        z = zext[HALO:HALO + tm]
        z1 = z1e[HALO:HALO + tm]
        z2 = z2e[HALO:HALO + tm]
        dcw_ref[0:1, :] += jnp.sum(dconv * z2, axis=0, keepdims=True)
        dcw_ref[1:2, :] += jnp.sum(dconv * z1, axis=0, keepdims=True)
        dcw_ref[2:3, :] += jnp.sum(dconv * z, axis=0, keepdims=True)
        dh_ref[:, in_a:in_a + bw] = (dsc_e[:tm] * conv_e[:tm]).astype(BF16)
        dh_ref[:, in_a + bw:in_a + 2 * bw] = (dz * hv[:, in_a + 2 * bw:]).astype(BF16)
        dh_ref[:, in_a + 2 * bw:] = (dz * hv[:, in_a + bw:in_a + 2 * bw]).astype(BF16)

        ha = hv[:, :in_a]
        th, u, xhat, rl, vln = _sgu_forward(ha, lng_ref[...], lnb_ref[...], wm, bt_ref[...], mixed_s)
        mixed = mixed_s[...]
        sg = u * mixed
        dsg, dga = _rms_bwd(dycat[:tm, :aw], sg, _rstd(sg), ga_ref[...])
        dga_ref[...] += dga
        du = dsg * mixed
        dmixed = dsg * u
        dmb = dmixed.astype(BF16)
        for n in range(tm // CHUNK):
            rows = slice(n * CHUNK, (n + 1) * CHUNK)
            dbs_ref[...] += dmixed[rows]
            for hh in range(HEADS):
                cols = slice(hh * hd, (hh + 1) * hd)
                dvln_s[rows, cols] = _dot_tn(wm[hh], dmb[rows, cols])
                dwsp_ref[hh] += mask * _dot_nt(dmb[rows, cols], vln[rows, cols])
        dvln = dvln_s[...]
        dlng_ref[...] += jnp.sum(dvln * xhat, axis=0, keepdims=True)
        dlnb_ref[...] += jnp.sum(dvln, axis=0, keepdims=True)
        dxh = dvln * lng_ref[...]
        dv = rl * (dxh - jnp.mean(dxh, axis=-1, keepdims=True) - xhat * jnp.mean(dxh * xhat, axis=-1, keepdims=True))
        dh_ref[:, :in_a] = (jnp.concatenate([du, dv], axis=-1) * _gelu_grad(ha, th)).astype(BF16)

        dxn = _dot(dh_ref[...], win_ref[...])
        xv = x_ref[...]
        dx, dgm = _rms_bwd(dxn, xv, _rstd(xv), gm_ref[...])
        dgm_ref[...] += dgm
        dx_ref[...] = dx1 + dx

    full = lambda shape: pl.BlockSpec(shape, lambda i: (0,) * len(shape))
    tok = pl.BlockSpec((tm, d), lambda i: (i, 0))
    nxt = lambda i: (jnp.minimum((i + 1) * hb_blocks, last_blk), 0)
    prv = lambda i: (jnp.maximum(i * hb_blocks - 1, 0), 0)
    return _hosted(
        body, carry, grid=(nt,),
        in_specs=[tok, pl.BlockSpec((HALO, d), nxt), tok, full((1, d)),
                  pl.BlockSpec((tm, n_in), lambda i: (i, 0)), pl.BlockSpec((HALO, n_in), prv), pl.BlockSpec((HALO, n_in), nxt),
                  full((1, aw)), full((1, aw)), full((HEADS, CHUNK, CHUNK)), full((CHUNK, HEADS)), full((3, bw)),
                  full((1, aw)), full((1, bw)), full((d, d)), full((n_in, d))],
        out_specs=[tok, pl.BlockSpec((tm, n_in), lambda i: (i, 0)), tok,
                   full((1, aw)), full((1, bw)), full((SUB, bw)), full((1, aw)), full((1, aw)),
                   full((HEADS, CHUNK, CHUNK)), full((CHUNK, aw)), full((1, d))],
        out_shape=(SDS((s, d), BF16), SDS((s, n_in), BF16), SDS((s, d), F32),
                   SDS((1, aw), F32), SDS((1, bw), F32), SDS((SUB, bw), F32), SDS((1, aw), F32), SDS((1, aw), F32),
                   SDS((HEADS, CHUNK, CHUNK), F32), SDS((CHUNK, aw), F32), SDS((1, d), F32)),
        scratch_shapes=[pltpu.VMEM((tm, aw), F32), pltpu.VMEM((tm, aw), F32)],
        compiler_params=_arb(1), name="mix_backward")(dx1, dx1, x, g_mix, h, h, h, lng, lnb, w_sp, bt, conv_w, ga, gb, w_out, w_in)


def _bias_grad(dbs):
    aw = dbs.shape[1]
    hd = aw // HEADS

    def body(dbs_ref, out_ref):
        ones = jnp.ones((SUB, hd), F32)
        for hh in range(HEADS):
            r = lax.dot_general(ones, dbs_ref[:, hh * hd:(hh + 1) * hd], (((1,), (1,)), ((), ())),
                                precision=lax.Precision.HIGHEST, preferred_element_type=F32)
            out_ref[hh:hh + 1, :] = r[0:1]

    return _pcall(body, out_shape=SDS((HEADS, CHUNK), F32), name="bias_grad")(dbs)


def _wgrad_body(a_ref, b_ref, o_ref):
    o_ref[...] = _dot_tn(a_ref[...], b_ref[...])


def _wgrad(a, b, name, carry=()):
    k, m = a.shape
    n = b.shape[1]
    tm = _tile(m, 512, LANES)
    tn = _tile(n, 1024, LANES)
    return _hosted(
        functools.partial(_wgrad_body), carry, grid=(m // tm, n // tn),
        in_specs=[pl.BlockSpec((k, tm), lambda i, j: (0, i)), pl.BlockSpec((k, tn), lambda i, j: (0, j))],
        out_specs=pl.BlockSpec((tm, tn), lambda i, j: (i, j)),
        out_shape=SDS((m, n), F32), compiler_params=_arb(2), name=name)(a, b)


def _wgrad_blocked_lhs(a, b, name, carry=()):
    nb, k, t = a.shape
    n = b.shape[1]
    tn = _tile(n, 1024, LANES)
    return _hosted(
        functools.partial(_wgrad_body), carry, grid=(nb, n // tn),
        in_specs=[pl.BlockSpec((None, k, t), lambda i, j: (i, 0, 0)), pl.BlockSpec((k, tn), lambda i, j: (0, j))],
        out_specs=pl.BlockSpec((t, tn), lambda i, j: (i, j)),
        out_shape=SDS((nb * t, n), F32), compiler_params=_arb(2), name=name)(a, b)


def _wgrad_blocked_rhs(a, b, name, carry=()):
    k, m = a.shape
    nb, _, t = b.shape
    tm = _tile(m, 512, LANES)
    return _hosted(
        functools.partial(_wgrad_body), carry, grid=(m // tm, nb),
        in_specs=[pl.BlockSpec((k, tm), lambda i, j: (0, i)), pl.BlockSpec((None, k, t), lambda i, j: (j, 0, 0))],
        out_specs=pl.BlockSpec((None, tm, t), lambda i, j: (j, i, 0)),
        out_shape=SDS((nb, m, t), F32), compiler_params=_arb(2), name=name)(a, b)


def _transpose_blocks(w, name, carry=()):
    nb, r, c = w.shape
    tr = _tile(r, 256, LANES)

    def body(w_ref, o_ref):
        o_ref[...] = w_ref[...].astype(F32).T.astype(o_ref.dtype)

    return _hosted(
        body, carry, grid=(nb, r // tr),
        in_specs=[pl.BlockSpec((None, tr, c), lambda b, i: (b, i, 0))],
        out_specs=pl.BlockSpec((None, c, tr), lambda b, i: (b, 0, i)),
        out_shape=SDS((nb, c, r), w.dtype), compiler_params=_arb(2), name=name)(w)


def _unblock_cols(wb, name):
    nb, r, t = wb.shape
    tr = _tile(r, 256, 16)

    def body(w_ref, o_ref):
        o_ref[...] = jnp.concatenate([w_ref[j].astype(F32) for j in range(nb)], axis=-1).astype(o_ref.dtype)

    return _pcall(
        body, grid=(r // tr,),
        in_specs=[pl.BlockSpec((nb, tr, t), lambda i: (0, i, 0))], out_specs=pl.BlockSpec((tr, nb * t), lambda i: (i, 0)),
        out_shape=SDS((r, nb * t), wb.dtype), compiler_params=_arb(1), name=name)(wb)


def _block_cols(w, nb, name, carry=()):
    r, n = w.shape
    t = n // nb
    tr = _tile(r, 256, 16)

    def body(w_ref, o_ref):
        wv = w_ref[...]
        for j in range(nb):
            o_ref[j] = wv[:, j * t:(j + 1) * t]

    return _hosted(
        body, carry, grid=(r // tr,),
        in_specs=[pl.BlockSpec((tr, n), lambda i: (i, 0))], out_specs=pl.BlockSpec((nb, tr, t), lambda i: (0, i, 0)),
        out_shape=SDS((nb, r, t), w.dtype), compiler_params=_arb(1), name=name)(w)


def _place():
    x, y, c = lax.axis_index("x"), lax.axis_index("y"), lax.axis_index("c")
    return x, y, c, [(1 - x, y), (x, 1 - y), (1 - x, 1 - y)]


def _all_gather(shards):
    n = len(shards)

    def copies(ins, outs, sems, incoming):
        send_sems, recv_sems, local_sems = sems
        x, y, c, chips = _place()
        me, sibling = (x, y, c), (x, y, 1 - c)

        def blk(a, px, py, pc):
            return outs[a].at[4 * px + 2 * py + pc]

        def copy(a, k, block, to, src=None):
            return pltpu.make_async_remote_copy(
                src_ref=blk(a, *block) if src is None else src, dst_ref=blk(a, *block),
                send_sem=send_sems.at[7 * a + k], recv_sem=recv_sems.at[7 * a + k], device_id=to, device_id_type=MESH)

        if not incoming:
            first = [copy(a, 1 + j, me, (*chip, c), src=ins[a]) for a in range(n) for j, chip in enumerate(chips)]
            first += [copy(a, 0, me, sibling, src=ins[a]) for a in range(n)]
            return first + [pltpu.make_async_copy(ins[a], blk(a, *me), local_sems.at[a]) for a in range(n)]
        landed = [copy(a, 1 + j, (*chip, c), me) for a in range(n) for j, chip in enumerate(chips)]
        passed = [copy(a, 4 + j, (*chip, c), sibling) for a in range(n) for j, chip in enumerate(chips)]
        from_sibling = [copy(a, 0, sibling, me) for a in range(n)]
        from_sibling += [copy(a, 4 + j, (*chip, 1 - c), me) for a in range(n) for j, chip in enumerate(chips)]
        return landed, passed, from_sibling

    def start(ins, outs, sems):
        for cp in copies(ins, outs, sems, False):
            cp.start()

    def finish(ins, outs, sems):
        landed, passed, from_sibling = copies(ins, outs, sems, True)
        for arrived, onward in zip(landed, passed):
            arrived.wait_recv()
            onward.start()
        for cp in from_sibling:
            cp.wait_recv()
        sent = copies(ins, outs, sems, False)
        for cp in sent[:4 * n] + passed:
            cp.wait_send()
        for cp in sent[4 * n:]:
            cp.wait()

    return _Exchange(shards, [SDS((N_DEV,) + s.shape, s.dtype) for s in shards],
                     [pltpu.SemaphoreType.DMA((7 * n,)), pltpu.SemaphoreType.DMA((7 * n,)), pltpu.SemaphoreType.DMA((n,))],
                     start, finish)


def _swap_exchange(ins, out_shape, per, copies):
    def start(i, o, sems):
        for cp in copies(i, o, sems):
            cp.start()

    def finish(i, o, sems):
        for cp in copies(i, o, sems):
            cp.wait()

    n = per * len(ins)
    return _Exchange(ins, out_shape, [pltpu.SemaphoreType.DMA((n,)), pltpu.SemaphoreType.DMA((n,))], start, finish)


def _exchange_c(gs):
    def copies(ins, outs, sems):
        x, y, c, _ = _place()
        return [pltpu.make_async_remote_copy(
                    src_ref=ins[a].at[2 * k + 1 - c], dst_ref=outs[a].at[k],
                    send_sem=sems[0].at[4 * a + k], recv_sem=sems[1].at[4 * a + k],
                    device_id=(x, y, 1 - c), device_id_type=MESH)
                for a in range(len(gs)) for k in range(4)]

    return _swap_exchange(gs, [SDS((4,) + g.shape[1:], g.dtype) for g in gs], 4, copies)


def _exchange_xy(sends):
    def copies(ins, outs, sems):
        x, y, c, chips = _place()
        return [pltpu.make_async_remote_copy(
                    src_ref=ins[a].at[t], dst_ref=outs[a].at[t],
                    send_sem=sems[0].at[3 * a + t], recv_sem=sems[1].at[3 * a + t],
                    device_id=(*chips[t], c), device_id_type=MESH)
                for a in range(len(sends)) for t in range(3)]

    return _swap_exchange(sends, [SDS(s.shape, s.dtype) for s in sends], 3, copies)


def _rs_combine(g, recv, pos, name):
    _, r, cdim = g.shape
    tr = _tile(r, 256, 16)

    def body(pos_ref, g0, r0, g1, r1, g2, r2, g3, r3, keep_ref, send_ref):
        keep_ref[...] = g0[...] + r0[...]
        send_ref[0] = (g1[...] + r1[...]).astype(BF16)
        send_ref[1] = (g2[...] + r2[...]).astype(BF16)
        send_ref[2] = (g3[...] + r3[...]).astype(BF16)

    def k_of(p, t):
        px = p[0] if t in (0, 2) else 1 - p[0]
        py = p[1] if t in (0, 1) else 1 - p[1]
        return 2 * px + py

    blk = (None, tr, cdim)
    in_specs = []
    for t in range(4):
        in_specs.append(pl.BlockSpec(blk, functools.partial(lambda j, p, t: (2 * k_of(p, t) + p[2], j, 0), t=t)))
        in_specs.append(pl.BlockSpec(blk, functools.partial(lambda j, p, t: (k_of(p, t), j, 0), t=t)))
    return _pcall(
        body, out_shape=(SDS((r, cdim), F32), SDS((3, r, cdim), BF16)),
        grid_spec=pltpu.PrefetchScalarGridSpec(
            num_scalar_prefetch=1, grid=(r // tr,), in_specs=in_specs,
            out_specs=[pl.BlockSpec((tr, cdim), lambda j, p: (j, 0)), pl.BlockSpec((3, tr, cdim), lambda j, p: (0, j, 0))]),
        compiler_params=_arb(1), name=name)(pos, g, recv, g, recv, g, recv, g, recv)


def _adamw_shard(keep, recv, w, m, v, name):
    r, cdim = w.shape
    tr = _tile(r, 256, 16)

    def body(k_ref, r_ref, w_ref, m_ref, v_ref, g_ref, d_ref, nm_ref, nv_ref):
        g = ((k_ref[...] + r_ref[0].astype(F32)) + r_ref[1].astype(F32)) + r_ref[2].astype(F32)
        g_ref[...] = g
        d_ref[...], nm_ref[...], nv_ref[...] = _adamw(w_ref[...], g, m_ref[...], v_ref[...])

    blk = pl.BlockSpec((tr, cdim), lambda j: (j, 0))
    out = SDS((r, cdim), F32)
    return _pcall(body, grid=(r // tr,), in_specs=[blk, pl.BlockSpec((3, tr, cdim), lambda j: (0, j, 0)), blk, blk, blk],
                  out_specs=[blk] * 4, out_shape=(out,) * 4, compiler_params=_arb(1), name=name)(keep, recv, w, m, v)


def _adamw_small(gathered, seg, params, conv_rows):
    names = list(params)
    c0, cn = conv_rows

    def body(*refs):
        gat_ref = refs[0]
        ins = refs[1:1 + 3 * len(names)]
        outs = refs[1 + 3 * len(names):]

        def total(r0, rn):
            tot = gat_ref[0, r0:r0 + rn, :]
            for dev in range(1, N_DEV):
                tot = tot + gat_ref[dev, r0:r0 + rn, :]
            return tot

        for k, nm in enumerate(names):
            g = total(*seg[nm])
            w_ref, m_ref, v_ref = ins[3 * k:3 * k + 3]
            g_ref, d_ref, nm_ref, nv_ref = outs[4 * k:4 * k + 4]
            g_ref[...] = g
            d_ref[...], nm_ref[...], nv_ref[...] = _adamw(w_ref[...], g, m_ref[...], v_ref[...])
        outs[-2][...] = total(c0, cn)
        outs[-1][...] = total(*seg["loss"])

    flat_in = [a for nm in names for a in params[nm]]
    out_shape = []
    for nm in names:
        out_shape += [SDS(params[nm][0].shape, F32)] * 4
    out_shape += [SDS((cn, LANES), F32), SDS((seg["loss"][1], LANES), F32)]
    res = _pcall(body, out_shape=tuple(out_shape), name="adamw_small")(gathered, *flat_in)
    per = {nm: res[4 * k:4 * k + 4] for k, nm in enumerate(names)}
    return per, res[-2], res[-1]


def _adamw_one(w, g, m, v, name):
    def body(w_ref, g_ref, m_ref, v_ref, d_ref, nm_ref, nv_ref):
        d_ref[...], nm_ref[...], nv_ref[...] = _adamw(w_ref[...], g_ref[...], m_ref[...], v_ref[...])

    return _pcall(body, out_shape=(SDS(w.shape, F32),) * 3, name=name)(w, g, m, v)


def _rows128(a):
    return a.reshape(-1, LANES)


def _pack_small(gs, loss_tile):
    seg, pieces, row = {}, [], 0
    for nm in SMALL + ("conv_w", "loss"):
        piece = loss_tile if nm == "loss" else _rows128(gs[nm])
        rn = _round_up(piece.shape[0], SUB)
        pieces.append(jnp.pad(piece, ((0, rn - piece.shape[0]), (0, 0))))
        seg[nm] = (row, piece.shape[0])
        row += rn
    return jnp.concatenate(pieces, axis=0), seg


def _step(x, mem, target, wb, conv_w, sp, pos):
    s, d = x.shape
    tm = min(TOKEN_TILE, s)
    tm_wide = min(2 * TOKEN_TILE, s)
    rows = lambda w8: w8.reshape(-1, w8.shape[2])
    shards = lambda g: g.reshape((N_DEV, -1) + g.shape[1:])
    bt = sp["b_spatial"].T

    (w_in8, conv8), = _run_exchanges([_all_gather([wb["w_in"], conv_w])], "gather_w_in")
    conv_full = conv8.transpose(1, 0, 2).reshape(3, -1)
    w_in = _unblock_cols(w_in8, "unblock_w_in")
    transposed = lambda w, nm, carry=(): _transpose_blocks(w[None], "transpose_" + nm, carry)
    w_in_t, ((w_out8,),) = transposed(w_in, "w_in", [_all_gather([wb["w_out"]])])
    (xn1, h), ((w_kv8, w_q8),) = _in_forward(
        x, sp["ln_mix_g"], w_in, tm_wide, carry=[_all_gather([wb["w_kv"], wb["w_q"]])])
    w_out = rows(w_out8)
    w_out_t, _ = transposed(w_out, "w_out")
    (ycat, x1), ((w_o8, w_down8),) = _mix_forward(
        h, x, sp["sgu_ln_g"], sp["sgu_ln_b"], sp["w_spatial"], bt, conv_full, sp["grp_norm_a"], sp["grp_norm_b"], w_out, tm,
        carry=[_all_gather([wb["w_o"], wb["w_down"]])])
    w_q, w_o, w_down = rows(w_q8), rows(w_o8), rows(w_down8)
    memn, kv = _kv_forward(mem, sp["ln_mem_g"], w_kv8)
    w_q_t, _ = transposed(w_q, "w_q")
    w_o_t, _ = transposed(w_o, "w_o")
    (xn2, q, o, x2), ((w_gu8,),) = _attn_forward(
        x1, sp["ln_attn_g"], w_q, kv, w_o, tm_wide, carry=[_all_gather([wb["w_gate_up"]])])
    nf = N_DEV // 2
    w_gu = w_gu8.reshape((2, nf) + w_gu8.shape[1:])
    xn3, gu, x3 = _ffn_forward(x2, sp["ln_ffn_g"], w_gu, w_down, tm_wide)
    w_down_t, _ = _transpose_blocks(w_down.reshape(nf, -1, d), "transpose_w_down")
    w_gu_t, _ = _transpose_blocks(w_gu8, "transpose_w_gate_up")
    w_gu_t = w_gu_t.reshape((2, nf) + w_gu_t.shape[1:])

    loss, d_lnf, d_lnffn, act, dgu, dx3b, dx2 = _ffn_backward(
        x3, target, sp["ln_final_g"], x2, sp["ln_ffn_g"], gu, w_gu_t, w_down_t, tm)
    part = {}
    g_gu, _ = _wgrad_blocked_rhs(xn3, dgu.reshape((N_DEV,) + dgu.shape[2:]), "wgrad_gate_up")
    g_down, ((rc_gu,),) = _wgrad_blocked_lhs(act, dx3b, "wgrad_down", carry=[_exchange_c([g_gu])])
    g_down = shards(g_down)
    keep_gu, send_gu = _rs_combine(g_gu, rc_gu, pos, "rs_combine_w_gate_up")
    (dx2b, dq, dx1, dkv, d_lnattn), ((rxy_gu,), (rc_down,)) = _attn_backward(
        dx2, x1, sp["ln_attn_g"], q, kv, w_q_t[0], w_o_t[0], tm, carry=[_exchange_xy([send_gu]), _exchange_c([g_down])])
    part["w_gate_up"] = (keep_gu, rxy_gu)
    keep_down, send_down = _rs_combine(g_down, rc_down, pos, "rs_combine_w_down")
    g_o, _ = _wgrad(o, dx2b, "wgrad_o")
    g_q, _ = _wgrad(xn2, dq, "wgrad_q")
    g_o, g_q = shards(g_o), shards(g_q)
    g_kv, d_lnmem = _kv_backward(dkv, memn, mem, sp["ln_mem_g"], w_kv8)
    ((dx1b, dh, dx, d_ga, d_gb, d_cw, d_lng, d_lnb, d_wsp, d_bs, d_lnmix),
     ((rxy_down,), (rc_o, rc_q, rc_kv))) = _mix_backward(
        dx1, x, sp["ln_mix_g"], h, sp["sgu_ln_g"], sp["sgu_ln_b"], sp["w_spatial"], bt, conv_full,
        sp["grp_norm_a"], sp["grp_norm_b"], w_out_t[0], w_in_t[0], tm,
        carry=[_exchange_xy([send_down]), _exchange_c([g_o, g_q, g_kv])])
    part["w_down"] = (keep_down, rxy_down)
    keep_o, send_o = _rs_combine(g_o, rc_o, pos, "rs_combine_w_o")
    keep_q, send_q = _rs_combine(g_q, rc_q, pos, "rs_combine_w_q")
    keep_kv, send_kv = _rs_combine(g_kv, rc_kv, pos, "rs_combine_w_kv")
    gs = {"ln_mix_g": d_lnmix, "sgu_ln_g": d_lng, "sgu_ln_b": d_lnb, "w_spatial": d_wsp, "b_spatial": _bias_grad(d_bs),
          "conv_w": d_cw[:3], "grp_norm_a": d_ga, "grp_norm_b": d_gb, "ln_attn_g": d_lnattn, "ln_mem_g": d_lnmem,
          "ln_ffn_g": d_lnffn, "ln_final_g": d_lnf}
    packed, seg = _pack_small(gs, loss)
    g_in, ((rxy_o, rxy_q, rxy_kv),) = _wgrad(xn1, dh, "wgrad_in", carry=[_exchange_xy([send_o, send_q, send_kv])])
    part["w_o"], part["w_q"], part["w_kv"] = (keep_o, rxy_o), (keep_q, rxy_q), (keep_kv, rxy_kv)
    g_in, ((small_all,),) = _block_cols(g_in, N_DEV, "block_grad_w_in", carry=[_all_gather([packed])])
    g_out, ((rc_in,),) = _wgrad(ycat, dx1b, "wgrad_out", carry=[_exchange_c([g_in])])
    g_out = shards(g_out)
    keep_in, send_in = _rs_combine(g_in, rc_in, pos, "rs_combine_w_in")
    (rc_out,), = _run_exchanges([_exchange_c([g_out])], "exchange_c_w_out")
    keep_out, send_out = _rs_combine(g_out, rc_out, pos, "rs_combine_w_out")
    (rxy_in, rxy_out), = _run_exchanges([_exchange_xy([send_in, send_out])], "exchange_xy_w_in_w_out")
    part["w_in"], part["w_out"] = (keep_in, rxy_in), (keep_out, rxy_out)
    return dx, part, small_all, seg


def kernel(x, mem, ln_mix_g, w_in, sgu_ln_g, sgu_ln_b, w_spatial, b_spatial, conv_w, grp_norm_a, grp_norm_b, w_out, ln_attn_g, ln_mem_g, w_q, w_kv, w_o, ln_ffn_g, w_gate_up, w_down, ln_final_g, loss_target, m_ln_mix_g, m_w_in, m_sgu_ln_g, m_sgu_ln_b, m_w_spatial, m_b_spatial, m_conv_w, m_grp_norm_a, m_grp_norm_b, m_w_out, m_ln_attn_g, m_ln_mem_g, m_w_q, m_w_kv, m_w_o, m_ln_ffn_g, m_w_gate_up, m_w_down, m_ln_final_g, v_ln_mix_g, v_w_in, v_sgu_ln_g, v_sgu_ln_b, v_w_spatial, v_b_spatial, v_conv_w, v_grp_norm_a, v_grp_norm_b, v_w_out, v_ln_attn_g, v_ln_mem_g, v_w_q, v_w_kv, v_w_o, v_ln_ffn_g, v_w_gate_up, v_w_down, v_ln_final_g):
    order = ["ln_mix_g", "w_in", "sgu_ln_g", "sgu_ln_b", "w_spatial", "b_spatial", "conv_w", "grp_norm_a", "grp_norm_b",
             "w_out", "ln_attn_g", "ln_mem_g", "w_q", "w_kv", "w_o", "ln_ffn_g", "w_gate_up", "w_down", "ln_final_g"]
    W = dict(ln_mix_g=ln_mix_g, w_in=w_in, sgu_ln_g=sgu_ln_g, sgu_ln_b=sgu_ln_b, w_spatial=w_spatial, b_spatial=b_spatial,
             conv_w=conv_w, grp_norm_a=grp_norm_a, grp_norm_b=grp_norm_b, w_out=w_out, ln_attn_g=ln_attn_g,
             ln_mem_g=ln_mem_g, w_q=w_q, w_kv=w_kv, w_o=w_o, ln_ffn_g=ln_ffn_g, w_gate_up=w_gate_up, w_down=w_down,
             ln_final_g=ln_final_g)
    M = dict(ln_mix_g=m_ln_mix_g, w_in=m_w_in, sgu_ln_g=m_sgu_ln_g, sgu_ln_b=m_sgu_ln_b, w_spatial=m_w_spatial,
             b_spatial=m_b_spatial, conv_w=m_conv_w, grp_norm_a=m_grp_norm_a, grp_norm_b=m_grp_norm_b, w_out=m_w_out,
             ln_attn_g=m_ln_attn_g, ln_mem_g=m_ln_mem_g, w_q=m_w_q, w_kv=m_w_kv, w_o=m_w_o, ln_ffn_g=m_ln_ffn_g,
             w_gate_up=m_w_gate_up, w_down=m_w_down, ln_final_g=m_ln_final_g)
    V = dict(ln_mix_g=v_ln_mix_g, w_in=v_w_in, sgu_ln_g=v_sgu_ln_g, sgu_ln_b=v_sgu_ln_b, w_spatial=v_w_spatial,
             b_spatial=v_b_spatial, conv_w=v_conv_w, grp_norm_a=v_grp_norm_a, grp_norm_b=v_grp_norm_b, w_out=v_w_out,
             ln_attn_g=v_ln_attn_g, ln_mem_g=v_ln_mem_g, w_q=v_w_q, w_kv=v_w_kv, w_o=v_w_o, ln_ffn_g=v_ln_ffn_g,
             w_gate_up=v_w_gate_up, w_down=v_w_down, ln_final_g=v_ln_final_g)

    bw = conv_w.shape[1] * N_DEV
    pos = jnp.stack([lax.axis_index("x"), lax.axis_index("y"), lax.axis_index("c")]).astype(jnp.int32)
    me = 4 * pos[0] + 2 * pos[1] + pos[2]

    sp = {nm: (W[nm].reshape(1, -1) if W[nm].ndim == 1 else W[nm]) for nm in SMALL}
    wb = {nm: W[nm].astype(BF16) for nm in BIG}
    grad_x, part, small_all, seg = _step(x[0], mem[0], loss_target[0], wb, conv_w, sp, pos)

    out = {}
    for nm in BIG:
        out[nm] = tuple(_adamw_shard(part[nm][0], part[nm][1], W[nm], M[nm], V[nm], "adamw_" + nm))

    params = {nm: (_rows128(W[nm]), _rows128(M[nm]), _rows128(V[nm])) for nm in SMALL}
    per, conv_g_rows, loss_sum = _adamw_small(small_all, seg, params, seg["conv_w"])
    for nm in SMALL:
        out[nm] = tuple(a.reshape(W[nm].shape) for a in per[nm])
    conv_g = lax.dynamic_slice_in_dim(conv_g_rows.reshape(3, bw), me * conv_w.shape[1], conv_w.shape[1], axis=1)
    out["conv_w"] = (conv_g,) + tuple(_adamw_one(conv_w, conv_g, m_conv_w, v_conv_w, "adamw_conv"))

    loss = loss_sum[0, 0]
    res = [loss, grad_x[None]]
    for k in range(4):
        res += [out[nm][k] for nm in order]
    return tuple(res)
```

```python
import functools

import jax
import jax.numpy as jnp
from jax import lax
from jax.experimental import pallas as pl
from jax.experimental.pallas import tpu as pltpu

F32 = jnp.float32
BF16 = jnp.bfloat16
SDS = jax.ShapeDtypeStruct
MESH = pl.DeviceIdType.MESH

EPS = 1e-6
N_DEV = 8
HEADS = 4
CHUNK = 128
HALO = 16
SUB = 8
LANES = 128
TOKEN_TILE = 512

ADAM_LR = 0.001
ADAM_B1 = 0.9
ADAM_B2 = 0.999
ADAM_EPS = 1e-08
ADAM_WD = 0.01
ADAM_STEP = 10

BIG = ("w_in", "w_out", "w_q", "w_kv", "w_o", "w_gate_up", "w_down")
SMALL = ("ln_mix_g", "sgu_ln_g", "sgu_ln_b", "w_spatial", "b_spatial", "grp_norm_a", "grp_norm_b",
         "ln_attn_g", "ln_mem_g", "ln_ffn_g", "ln_final_g")


class _Exchange:
    def __init__(self, ins, out_shape, sems, start, finish, relay=None):
        self.ins, self.out_shape, self.sems = list(ins), list(out_shape), list(sems)
        self.start, self.finish, self.relay = start, finish, relay


def _pcall(body, carry=(), n_prefetch=0, **kw):
    if carry:
        return functools.partial(_carrying_call, body, tuple(carry), n_prefetch, kw)
    if n_prefetch:
        kw["grid_spec"] = pltpu.PrefetchScalarGridSpec(
            num_scalar_prefetch=n_prefetch, grid=kw.pop("grid"), in_specs=kw.pop("in_specs"),
            out_specs=kw.pop("out_specs"), scratch_shapes=kw.pop("scratch_shapes", ()))
    return pl.pallas_call(body, **kw)


def _carrying_call(body, carry, n_prefetch, kw, *args):
    kw = dict(kw)
    out_shape = kw.pop("out_shape")
    single = not isinstance(out_shape, (tuple, list))
    outs_shape = (out_shape,) if single else tuple(out_shape)
    out_specs = kw.pop("out_specs")
    out_specs = [out_specs] if single else list(out_specs)
    in_specs = list(kw.pop("in_specs"))
    scratch = list(kw.pop("scratch_shapes", ()))
    grid = tuple(kw.get("grid", ()))
    n_in, n_out, n_scr = len(args), len(outs_shape), len(scratch)

    def split(refs, k, counts):
        parts = []
        for cnt in counts:
            parts.append(refs[k:k + cnt])
            k += cnt
        return parts, k

    def wrapped(*refs):
        cins, k = split(refs, n_in, [len(p.ins) for p in carry])
        outs = refs[k:k + n_out]
        couts, k = split(refs, k + n_out, [len(p.out_shape) for p in carry])
        scr = refs[k:k + n_scr]
        csems, _ = split(refs, k + n_scr, [len(p.sems) for p in carry])
        first, last = True, True
        for a, g in enumerate(grid):
            first = (pl.program_id(a) == 0) & first
            last = (pl.program_id(a) == g - 1) & last

        def start_all():
            for p, ci, co, cs in zip(carry, cins, couts, csems):
                p.start(ci, co, cs)

        def relay_all():
            for p, ci, co, cs in zip(carry, cins, couts, csems):
                if p.relay is not None:
                    p.relay(ci, co, cs)

        def finish_all():
            for p, ci, co, cs in zip(carry, cins, couts, csems):
                p.finish(ci, co, cs)

        start_all() if not grid else pl.when(first)(start_all)
        relay_all() if not grid else pl.when(last)(relay_all)
        body(*refs[:n_in], *outs, *scr)
        finish_all() if not grid else pl.when(last)(finish_all)

    c_in = [a for p in carry for a in p.ins]
    c_out = [s for p in carry for s in p.out_shape]
    c_sems = [s for p in carry for s in p.sems]
    res = _pcall(wrapped, n_prefetch=n_prefetch, out_shape=outs_shape + tuple(c_out),
                 in_specs=in_specs + _hbm_specs(len(c_in)), out_specs=out_specs + _hbm_specs(len(c_out)),
                 scratch_shapes=scratch + c_sems, **kw)(*args, *c_in)
    own = res[0] if single else tuple(res[:n_out])
    landed, k = [], n_out
    for p in carry:
        landed.append(list(res[k:k + len(p.out_shape)]))
        k += len(p.out_shape)
    return own, landed


def _hbm_specs(n):
    return [pl.BlockSpec(memory_space=pl.ANY)] * n


def _hosted(body, carry, **kw):
    if carry:
        return _pcall(body, carry=carry, **kw)
    call = _pcall(body, **kw)
    return lambda *args: (call(*args), [])


def _run_exchanges(parts, name):
    def body(*refs):
        pass

    _, landed = _pcall(body, carry=parts, out_shape=(), in_specs=[], out_specs=[], name=name)()
    return landed


def _arb(n):
    return pltpu.CompilerParams(dimension_semantics=("arbitrary",) * n)


def _tile(n, target, mult):
    best = None
    for t in range(mult, min(n, target) + 1, mult):
        if n % t == 0:
            best = t
    return n if best is None else best


def _round_up(n, m):
    return (n + m - 1) // m * m


def _dot(a, b):
    return jnp.dot(a, b, preferred_element_type=F32)


def _dot_nt(a, b):
    return lax.dot_general(a, b, (((1,), (1,)), ((), ())), preferred_element_type=F32)


def _dot_tn(a, b):
    return lax.dot_general(a, b, (((0,), (0,)), ((), ())), preferred_element_type=F32)


def _rstd(x):
    return lax.rsqrt(jnp.mean(x * x, axis=-1, keepdims=True) + EPS)


def _rms_bwd(dy, x, r, g):
    gdy = dy * g
    proj = jnp.sum(gdy * x, axis=-1, keepdims=True) * (1.0 / x.shape[-1])
    dx = r * gdy - x * (r * r * r) * proj
    dg = jnp.sum(dy * (x * r), axis=0, keepdims=True)
    return dx, dg


_GELU_C = 0.7978845608028654
_GELU_A = 0.044715


def _gelu(x):
    t = jnp.tanh(_GELU_C * (x + _GELU_A * x * x * x))
    return 0.5 * x * (1.0 + t), t


def _gelu_grad(x, t):
    return 0.5 * (1.0 + t) + 0.5 * x * (1.0 - t * t) * (_GELU_C * (1.0 + 3.0 * _GELU_A * x * x))


def _sigmoid(x):
    return 1.0 / (1.0 + jnp.exp(-x))


def _softmax(s):
    m = jnp.max(s, axis=-1, keepdims=True)
    e = jnp.exp(s - m)
    return e / jnp.sum(e, axis=-1, keepdims=True)


def _adamw(w, g, m, v):
    m = ADAM_B1 * m + (1.0 - ADAM_B1) * g
    v = ADAM_B2 * v + (1.0 - ADAM_B2) * (g * g)
    m_hat = m / (1.0 - ADAM_B1 ** ADAM_STEP)
    v_hat = v / (1.0 - ADAM_B2 ** ADAM_STEP)
    delta = -ADAM_LR * (m_hat / (jnp.sqrt(v_hat) + ADAM_EPS) + ADAM_WD * w)
    return delta, m, v


def _tril_mask():
    t = lax.broadcasted_iota(jnp.int32, (CHUNK, CHUNK), 0)
    s = lax.broadcasted_iota(jnp.int32, (CHUNK, CHUNK), 1)
    return (s <= t).astype(F32)


def _sgu_forward(ha, lng, lnb, wm, bt, mixed_s):
    aw = ha.shape[1] // 2
    hd = aw // HEADS
    a, th = _gelu(ha)
    u = a[:, :aw]
    v = a[:, aw:]
    mu = jnp.mean(v, axis=-1, keepdims=True)
    vc = v - mu
    rl = lax.rsqrt(jnp.mean(vc * vc, axis=-1, keepdims=True) + EPS)
    xhat = vc * rl
    vln = (xhat * lng + lnb).astype(BF16)
    for n in range(ha.shape[0] // CHUNK):
        rows = slice(n * CHUNK, (n + 1) * CHUNK)
        for h in range(HEADS):
            cols = slice(h * hd, (h + 1) * hd)
            mixed_s[rows, cols] = _dot(wm[h], vln[rows, cols]) + bt[:, h:h + 1]
    return th, u, xhat, rl, vln


def _conv_taps(zext):
    return pltpu.roll(zext, 2, 0), pltpu.roll(zext, 1, 0)


def _kv_forward(mem, g_mem, w_kv):
    ml, d = mem.shape
    xd = w_kv.shape[2]

    def body(mem_ref, g_ref, w_ref, memn_ref, kv_ref):
        x = mem_ref[...]
        memn = (x * _rstd(x) * g_ref[...]).astype(BF16)
        memn_ref[...] = memn
        for j in range(2 * HEADS):
            kv_ref[j] = _dot(memn, w_ref[j]).astype(BF16)

    return _pcall(body, out_shape=(SDS((ml, d), BF16), SDS((2 * HEADS, ml, xd), BF16)), name="kv_forward")(mem, g_mem, w_kv)


def _in_forward(x, g, w_in, tm, carry=()):
    s, d = x.shape
    n_in = w_in.shape[1]

    def body(x_ref, g_ref, w_ref, xn_ref, h_ref):
        xv = x_ref[...]
        xn = (xv * _rstd(xv) * g_ref[...]).astype(BF16)
        xn_ref[...] = xn
        h_ref[...] = _dot(xn, w_ref[...])

    return _hosted(
        body, carry, grid=(s // tm,),
        in_specs=[pl.BlockSpec((tm, d), lambda i: (i, 0)), pl.BlockSpec((1, d), lambda i: (0, 0)),
                  pl.BlockSpec((d, n_in), lambda i: (0, 0))],
        out_specs=[pl.BlockSpec((tm, d), lambda i: (i, 0)), pl.BlockSpec((tm, n_in), lambda i: (i, 0))],
        out_shape=(SDS((s, d), BF16), SDS((s, n_in), F32)),
        compiler_params=_arb(1), name="in_forward")(x, g, w_in)


def _mix_forward(h, x, lng, lnb, w_sp, bt, conv_w, ga, gb, w_out, tm, carry=()):
    s, d = x.shape
    n_in = h.shape[1]
    aw = lng.shape[1]
    bw = d - aw
    in_a = 2 * aw
    hb_blocks = tm // HALO

    def body(h_ref, hprev_ref, x_ref, lng_ref, lnb_ref, wsp_ref, bt_ref, cw_ref, ga_ref, gb_ref, wout_ref,
             ycat_ref, x1_ref, mixed_s):
        i = pl.program_id(0)
        mask = _tril_mask()
        wm = [(wsp_ref[hh] * mask).astype(BF16) for hh in range(HEADS)]
        hv = h_ref[...]
        _, u, _, _, _ = _sgu_forward(hv[:, :in_a], lng_ref[...], lnb_ref[...], wm, bt_ref[...], mixed_s)
        sg = u * mixed_s[...]
        ycat_ref[:, :aw] = (sg * _rstd(sg) * ga_ref[...]).astype(BF16)

        gate_b = hv[:, in_a:in_a + bw]
        z = hv[:, in_a + bw:in_a + 2 * bw] * hv[:, in_a + 2 * bw:]
        hp = hprev_ref[...]
        zp = hp[:, in_a + bw:in_a + 2 * bw] * hp[:, in_a + 2 * bw:]
        zp = jnp.where(i == 0, 0.0, zp)
        zext = jnp.concatenate([zp, z], axis=0)
        z2, z1 = _conv_taps(zext)
        cw = cw_ref[...]
        conv = cw[0:1] * z2[HALO:] + cw[1:2] * z1[HALO:] + cw[2:3] * z
        sc = gate_b * conv
        ycat_ref[:, aw:] = (sc * _rstd(sc) * gb_ref[...]).astype(BF16)
        x1_ref[...] = x_ref[...] + _dot(ycat_ref[...], wout_ref[...])

    full = lambda shape: pl.BlockSpec(shape, lambda i: (0,) * len(shape))
    return _hosted(
        body, carry, grid=(s // tm,),
        in_specs=[pl.BlockSpec((tm, n_in), lambda i: (i, 0)),
                  pl.BlockSpec((HALO, n_in), lambda i: (jnp.maximum(i * hb_blocks - 1, 0), 0)),
                  pl.BlockSpec((tm, d), lambda i: (i, 0)),
                  full((1, aw)), full((1, aw)), full((HEADS, CHUNK, CHUNK)), full((CHUNK, HEADS)),
                  full((3, bw)), full((1, aw)), full((1, bw)), full((d, d))],
        out_specs=[pl.BlockSpec((tm, d), lambda i: (i, 0)), pl.BlockSpec((tm, d), lambda i: (i, 0))],
        out_shape=(SDS((s, d), BF16), SDS((s, d), F32)),
        scratch_shapes=[pltpu.VMEM((tm, aw), F32)],
        compiler_params=_arb(1), name="mix_forward")(h, h, x, lng, lnb, w_sp, bt, conv_w, ga, gb, w_out)


def _attn_forward(x1, g, w_q, kv, w_o, tm, carry=()):
    s, d = x1.shape
    _, ml, xd = kv.shape
    scale = xd ** -0.5

    def body(x1_ref, g_ref, wq_ref, kv_ref, wo_ref, xn_ref, q_ref, o_ref, x2_ref):
        xv = x1_ref[...]
        xn = (xv * _rstd(xv) * g_ref[...]).astype(BF16)
        xn_ref[...] = xn
        q_ref[...] = _dot(xn, wq_ref[...]).astype(BF16)
        for hh in range(HEADS):
            cols = slice(hh * xd, (hh + 1) * xd)
            p = _softmax(_dot_nt(q_ref[:, cols], kv_ref[hh]) * scale)
            o_ref[:, cols] = _dot(p.astype(BF16), kv_ref[HEADS + hh]).astype(BF16)
        x2_ref[...] = xv + _dot(o_ref[...], wo_ref[...])

    tok = pl.BlockSpec((tm, d), lambda i: (i, 0))
    return _hosted(
        body, carry, grid=(s // tm,),
        in_specs=[tok, pl.BlockSpec((1, d), lambda i: (0, 0)), pl.BlockSpec((d, d), lambda i: (0, 0)),
                  pl.BlockSpec((2 * HEADS, ml, xd), lambda i: (0, 0, 0)), pl.BlockSpec((d, d), lambda i: (0, 0))],
        out_specs=[tok, tok, tok, tok],
        out_shape=(SDS((s, d), BF16), SDS((s, d), BF16), SDS((s, d), BF16), SDS((s, d), F32)),
        compiler_params=_arb(1), name="attn_forward")(x1, g, w_q, kv, w_o)


def _ffn_forward(x2, g, w_gu, w_down, tm):
    s, d = x2.shape
    _, nf, _, tf = w_gu.shape

    def body(x2_ref, g_ref, wgu_ref, wd_ref, xn_ref, gu_ref, x3_ref):
        f = pl.program_id(1)

        @pl.when(f == 0)
        def _():
            xv = x2_ref[...]
            xn_ref[...] = (xv * _rstd(xv) * g_ref[...]).astype(BF16)
            x3_ref[...] = xv

        xn = xn_ref[...]
        gate = _dot(xn, wgu_ref[0])
        up = _dot(xn, wgu_ref[1])
        gu_ref[0] = gate.astype(BF16)
        gu_ref[1] = up.astype(BF16)
        act = (gate * _sigmoid(gate) * up).astype(BF16)
        x3_ref[...] += _dot(act, wd_ref[...])

    tok = pl.BlockSpec((tm, d), lambda i, f: (i, 0))
    return _pcall(
        body, grid=(s // tm, nf),
        in_specs=[tok, pl.BlockSpec((1, d), lambda i, f: (0, 0)),
                  pl.BlockSpec((2, None, d, tf), lambda i, f: (0, f, 0, 0)),
                  pl.BlockSpec((tf, d), lambda i, f: (f, 0))],
        out_specs=[tok, pl.BlockSpec((2, None, tm, tf), lambda i, f: (0, f, i, 0)), tok],
        out_shape=(SDS((s, d), BF16), SDS((2, nf, s, tf), BF16), SDS((s, d), F32)),
        compiler_params=_arb(2), name="ffn_forward")(x2, g, w_gu, w_down)


def _ffn_backward(x3, target, g_final, x2, g_ffn, gu, w_gu, w_down, tm):
    s, d = x3.shape
    _, nf, _, tf = w_gu.shape

    def body(x3_ref, tgt_ref, gf_ref, x2_ref, g2_ref, gu_ref, wgu_ref, wd_ref,
             loss_ref, dgf_ref, dg2_ref, act_ref, dgu_ref, dx3b_ref, dx2_ref, dx3_s, acc_s):
        i = pl.program_id(0)
        f = pl.program_id(1)

        @pl.when((i == 0) & (f == 0))
        def _():
            loss_ref[...] = jnp.zeros_like(loss_ref)
            dgf_ref[...] = jnp.zeros_like(dgf_ref)
            dg2_ref[...] = jnp.zeros_like(dg2_ref)

        @pl.when(f == 0)
        def _():
            xv = x3_ref[...]
            r = _rstd(xv)
            diff = xv * r * gf_ref[...] - tgt_ref[...]
            loss_ref[...] += 0.5 * jnp.sum(jnp.sum(diff * diff, axis=-1, keepdims=True), axis=0, keepdims=True) * (1.0 / d)
            dx3, dgf = _rms_bwd(diff * (1.0 / d), xv, r, gf_ref[...])
            dgf_ref[...] += dgf
            dx3_s[...] = dx3
            dx3b_ref[...] = dx3.astype(BF16)
            acc_s[...] = jnp.zeros_like(acc_s)

        dact = _dot_nt(dx3b_ref[...], wd_ref[...])
        gv = gu_ref[0].astype(F32)
        uv = gu_ref[1].astype(F32)
        sg = _sigmoid(gv)
        silu = gv * sg
        act_ref[...] = (silu * uv).astype(BF16)
        dgate = (dact * uv * (sg * (1.0 + gv * (1.0 - sg)))).astype(BF16)
        dup = (dact * silu).astype(BF16)
        dgu_ref[0] = dgate
        dgu_ref[1] = dup
        acc_s[...] += _dot_nt(dgate, wgu_ref[0]) + _dot_nt(dup, wgu_ref[1])

        @pl.when(f == nf - 1)
        def _():
            xv = x2_ref[...]
            dxn, dg2 = _rms_bwd(acc_s[...], xv, _rstd(xv), g2_ref[...])
            dg2_ref[...] += dg2
            dx2_ref[...] = dx3_s[...] + dxn

    tok = pl.BlockSpec((tm, d), lambda i, f: (i, 0))
    pair = pl.BlockSpec((2, None, tm, tf), lambda i, f: (0, f, i, 0))
    vec = pl.BlockSpec((1, d), lambda i, f: (0, 0))
    return _pcall(
        body, grid=(s // tm, nf),
        in_specs=[tok, tok, vec, tok, vec, pair,
                  pl.BlockSpec((2, None, d, tf), lambda i, f: (0, f, 0, 0)),
                  pl.BlockSpec((tf, d), lambda i, f: (f, 0))],
        out_specs=[pl.BlockSpec((SUB, LANES), lambda i, f: (0, 0)), vec, vec,
                   pl.BlockSpec((None, tm, tf), lambda i, f: (f, i, 0)), pair, tok, tok],
        out_shape=(SDS((SUB, LANES), F32), SDS((1, d), F32), SDS((1, d), F32), SDS((nf, s, tf), BF16),
                   SDS((2, nf, s, tf), BF16), SDS((s, d), BF16), SDS((s, d), F32)),
        scratch_shapes=[pltpu.VMEM((tm, d), F32), pltpu.VMEM((tm, d), F32)],
        compiler_params=_arb(2), name="ffn_backward")(x3, target, g_final, x2, g_ffn, gu, w_gu, w_down)


def _attn_backward(dx2, x1, g, q, kv, w_q, w_o, tm, carry=()):
    s, d = x1.shape
    _, ml, xd = kv.shape
    scale = xd ** -0.5

    def body(dx2_ref, x1_ref, g_ref, q_ref, kv_ref, wq_ref, wo_ref,
             dx2b_ref, dq_ref, dx1_ref, dkv_ref, dg_ref, do_s):
        i = pl.program_id(0)

        @pl.when(i == 0)
        def _():
            dkv_ref[...] = jnp.zeros_like(dkv_ref)
            dg_ref[...] = jnp.zeros_like(dg_ref)

        dx2 = dx2_ref[...]
        dx2b_ref[...] = dx2.astype(BF16)
        do_s[...] = _dot_nt(dx2b_ref[...], wo_ref[...]).astype(BF16)
        for hh in range(HEADS):
            kc = slice(hh * xd, (hh + 1) * xd)
            qh = q_ref[:, kc]
            kh = kv_ref[hh]
            doh = do_s[:, kc]
            p = _softmax(_dot_nt(qh, kh) * scale)
            dp = _dot_nt(doh, kv_ref[HEADS + hh])
            dkv_ref[HEADS + hh] += _dot_tn(p.astype(BF16), doh)
            ds = (p * (dp - jnp.sum(dp * p, axis=-1, keepdims=True)) * scale).astype(BF16)
            dq_ref[:, kc] = _dot(ds, kh).astype(BF16)
            dkv_ref[hh] += _dot_tn(ds, qh)
        dxn = _dot_nt(dq_ref[...], wq_ref[...])
        xv = x1_ref[...]
        dx, dg = _rms_bwd(dxn, xv, _rstd(xv), g_ref[...])
        dg_ref[...] += dg
        dx1_ref[...] = dx2 + dx

    tok = pl.BlockSpec((tm, d), lambda i: (i, 0))
    vec = pl.BlockSpec((1, d), lambda i: (0, 0))
    sq = pl.BlockSpec((d, d), lambda i: (0, 0))
    kvs = pl.BlockSpec((2 * HEADS, ml, xd), lambda i: (0, 0, 0))
    return _hosted(
        body, carry, grid=(s // tm,),
        in_specs=[tok, tok, vec, tok, kvs, sq, sq],
        out_specs=[tok, tok, tok, kvs, vec],
        out_shape=(SDS((s, d), BF16), SDS((s, d), BF16), SDS((s, d), F32), SDS((2 * HEADS, ml, xd), F32), SDS((1, d), F32)),
        scratch_shapes=[pltpu.VMEM((tm, d), BF16)],
        compiler_params=_arb(1), name="attn_backward")(dx2, x1, g, q, kv, w_q, w_o)


def _kv_backward(dkv, memn, mem, g_mem, w_kv):
    ml, d = mem.shape
    xd = w_kv.shape[2]

    def body(dkv_ref, memn_ref, mem_ref, g_ref, w_ref, dw_ref, dg_ref):
        dmemn = jnp.zeros((ml, d), F32)
        for j in range(2 * HEADS):
            dkvb = dkv_ref[j].astype(BF16)
            dw_ref[j] = _dot_tn(memn_ref[...], dkvb)
            dmemn = dmemn + _dot_nt(dkvb, w_ref[j])
        x = mem_ref[...]
        dg_ref[...] = jnp.sum(dmemn * (x * _rstd(x)), axis=0, keepdims=True)

    return _pcall(body, out_shape=(SDS((2 * HEADS, d, xd), F32), SDS((1, d), F32)), name="kv_backward")(dkv, memn, mem, g_mem, w_kv)


def _mix_backward(dx1, x, g_mix, h, lng, lnb, w_sp, bt, conv_w, ga, gb, w_out, w_in, tm, carry=()):
    s, d = x.shape
    n_in = h.shape[1]
    aw = lng.shape[1]
    bw = d - aw
    hd = aw // HEADS
    in_a = 2 * aw
    hb_blocks = tm // HALO
    last_blk = s // HALO - 1
    nt = s // tm
    te = tm + HALO
    tee = tm + 2 * HALO

    def body(dx1_ref, dx1n_ref, x_ref, gm_ref, h_ref, hp_ref, hn_ref, lng_ref, lnb_ref, wsp_ref, bt_ref, cw_ref,
             ga_ref, gb_ref, wout_ref, win_ref,
             dx1b_ref, dh_ref, dx_ref, dga_ref, dgb_ref, dcw_ref, dlng_ref, dlnb_ref, dwsp_ref, dbs_ref, dgm_ref,
             mixed_s, dvln_s):
        i = pl.program_id(0)

        @pl.when(i == 0)
        def _():
            for ref in (dga_ref, dgb_ref, dcw_ref, dlng_ref, dlnb_ref, dwsp_ref, dbs_ref, dgm_ref):
                ref[...] = jnp.zeros_like(ref)

        mask = _tril_mask()
        wm = [(wsp_ref[hh] * mask).astype(BF16) for hh in range(HEADS)]
        hv = h_ref[...]
        dx1 = dx1_ref[...]
        dx1b_ref[...] = dx1.astype(BF16)
        dx1e = jnp.concatenate([dx1, dx1n_ref[...]], axis=0).astype(BF16)
        dycat = _dot_nt(dx1e, wout_ref[...])

        hbe = jnp.concatenate([hp_ref[:, in_a:], hv[:, in_a:], hn_ref[:, in_a:]], axis=0)
        row = lax.broadcasted_iota(jnp.int32, (tee, 1), 0)
        zext = hbe[:, bw:2 * bw] * hbe[:, 2 * bw:]
        zext = jnp.where((i == 0) & (row < HALO), 0.0, zext)
        z2e, z1e = _conv_taps(zext)
        cw = cw_ref[...]
        conv_e = (cw[0:1] * z2e + cw[1:2] * z1e + cw[2:3] * zext)[HALO:]
        gate_b_e = hbe[HALO:, :bw]
        sc_e = gate_b_e * conv_e
        rb = _rstd(sc_e)
        dyb = dycat[:, aw:]
        gdy = dyb * gb_ref[...]
        dsc_e = rb * gdy - sc_e * (rb * rb * rb) * (jnp.sum(gdy * sc_e, axis=-1, keepdims=True) * (1.0 / bw))
        dgb_ref[...] += jnp.sum((dyb * (sc_e * rb))[:tm], axis=0, keepdims=True)
        dconv_e = dsc_e * gate_b_e
        dconv_e = jnp.where((i == nt - 1) & (row[:te] >= tm), 0.0, dconv_e)
        dconv = dconv_e[:tm]
        dc1 = pltpu.roll(dconv_e, te - 1, 0)[:tm]
        dc2 = pltpu.roll(dconv_e, te - 2, 0)[:tm]
        dz = cw[2:3] * dconv + cw[1:2] * dc1 + cw[0:1] * dc2
        z = zext[HALO:HALO + tm]
        z1 = z1e[HALO:HALO + tm]
        z2 = z2e[HALO:HALO + tm]
        dcw_ref[0:1, :] += jnp.sum(dconv * z2, axis=0, keepdims=True)
        dcw_ref[1:2, :] += jnp.sum(dconv * z1, axis=0, keepdims=True)
        dcw_ref[2:3, :] += jnp.sum(dconv * z, axis=0, keepdims=True)
        dh_ref[:, in_a:in_a + bw] = (dsc_e[:tm] * conv_e[:tm]).astype(BF16)
        dh_ref[:, in_a + bw:in_a + 2 * bw] = (dz * hv[:, in_a + 2 * bw:]).astype(BF16)
        dh_ref[:, in_a + 2 * bw:] = (dz * hv[:, in_a + bw:in_a + 2 * bw]).astype(BF16)

        ha = hv[:, :in_a]
        th, u, xhat, rl, vln = _sgu_forward(ha, lng_ref[...], lnb_ref[...], wm, bt_ref[...], mixed_s)
        mixed = mixed_s[...]
        sg = u * mixed
        dsg, dga = _rms_bwd(dycat[:tm, :aw], sg, _rstd(sg), ga_ref[...])
        dga_ref[...] += dga
        du = dsg * mixed
        dmixed = dsg * u
        dmb = dmixed.astype(BF16)
        for n in range(tm // CHUNK):
            rows = slice(n * CHUNK, (n + 1) * CHUNK)
            dbs_ref[...] += dmixed[rows]
            for hh in range(HEADS):
                cols = slice(hh * hd, (hh + 1) * hd)
                dvln_s[rows, cols] = _dot_tn(wm[hh], dmb[rows, cols])
                dwsp_ref[hh] += mask * _dot_nt(dmb[rows, cols], vln[rows, cols])
        dvln = dvln_s[...]
        dlng_ref[...] += jnp.sum(dvln * xhat, axis=0, keepdims=True)
        dlnb_ref[...] += jnp.sum(dvln, axis=0, keepdims=True)
        dxh = dvln * lng_ref[...]
        dv = rl * (dxh - jnp.mean(dxh, axis=-1, keepdims=True) - xhat * jnp.mean(dxh * xhat, axis=-1, keepdims=True))
        dh_ref[:, :in_a] = (jnp.concatenate([du, dv], axis=-1) * _gelu_grad(ha, th)).astype(BF16)

        dxn = _dot_nt(dh_ref[...], win_ref[...])
        xv = x_ref[...]
        dx, dgm = _rms_bwd(dxn, xv, _rstd(xv), gm_ref[...])
        dgm_ref[...] += dgm
        dx_ref[...] = dx1 + dx

    full = lambda shape: pl.BlockSpec(shape, lambda i: (0,) * len(shape))
    tok = pl.BlockSpec((tm, d), lambda i: (i, 0))
    nxt = lambda i: (jnp.minimum((i + 1) * hb_blocks, last_blk), 0)
    prv = lambda i: (jnp.maximum(i * hb_blocks - 1, 0), 0)
    return _hosted(
        body, carry, grid=(nt,),
        in_specs=[tok, pl.BlockSpec((HALO, d), nxt), tok, full((1, d)),
                  pl.BlockSpec((tm, n_in), lambda i: (i, 0)), pl.BlockSpec((HALO, n_in), prv), pl.BlockSpec((HALO, n_in), nxt),
                  full((1, aw)), full((1, aw)), full((HEADS, CHUNK, CHUNK)), full((CHUNK, HEADS)), full((3, bw)),
                  full((1, aw)), full((1, bw)), full((d, d)), full((d, n_in))],
        out_specs=[tok, pl.BlockSpec((tm, n_in), lambda i: (i, 0)), tok,
                   full((1, aw)), full((1, bw)), full((SUB, bw)), full((1, aw)), full((1, aw)),
                   full((HEADS, CHUNK, CHUNK)), full((CHUNK, aw)), full((1, d))],
        out_shape=(SDS((s, d), BF16), SDS((s, n_in), BF16), SDS((s, d), F32),
                   SDS((1, aw), F32), SDS((1, bw), F32), SDS((SUB, bw), F32), SDS((1, aw), F32), SDS((1, aw), F32),
                   SDS((HEADS, CHUNK, CHUNK), F32), SDS((CHUNK, aw), F32), SDS((1, d), F32)),
        scratch_shapes=[pltpu.VMEM((tm, aw), F32), pltpu.VMEM((tm, aw), F32)],
        compiler_params=_arb(1), name="mix_backward")(dx1, dx1, x, g_mix, h, h, h, lng, lnb, w_sp, bt, conv_w, ga, gb, w_out, w_in)


def _bias_grad(dbs):
    aw = dbs.shape[1]
    hd = aw // HEADS

    def body(dbs_ref, out_ref):
        ones = jnp.ones((SUB, hd), F32)
        for hh in range(HEADS):
            r = lax.dot_general(ones, dbs_ref[:, hh * hd:(hh + 1) * hd], (((1,), (1,)), ((), ())),
                                precision=lax.Precision.HIGHEST, preferred_element_type=F32)
            out_ref[hh:hh + 1, :] = r[0:1]

    return _pcall(body, out_shape=SDS((HEADS, CHUNK), F32), name="bias_grad")(dbs)


def _wgrad_body(a_ref, b_ref, o_ref):
    o_ref[...] = _dot_tn(a_ref[...], b_ref[...])


def _wgrad(a, b, name, carry=()):
    k, m = a.shape
    n = b.shape[1]
    tm = _tile(m, 512, LANES)
    tn = _tile(n, 1024, LANES)
    return _hosted(
        functools.partial(_wgrad_body), carry, grid=(m // tm, n // tn),
        in_specs=[pl.BlockSpec((k, tm), lambda i, j: (0, i)), pl.BlockSpec((k, tn), lambda i, j: (0, j))],
        out_specs=pl.BlockSpec((tm, tn), lambda i, j: (i, j)),
        out_shape=SDS((m, n), F32), compiler_params=_arb(2), name=name)(a, b)


def _wgrad_blocked_lhs(a, b, name, carry=()):
    nb, k, t = a.shape
    n = b.shape[1]
    tn = _tile(n, 1024, LANES)
    return _hosted(
        functools.partial(_wgrad_body), carry, grid=(nb, n // tn),
        in_specs=[pl.BlockSpec((None, k, t), lambda i, j: (i, 0, 0)), pl.BlockSpec((k, tn), lambda i, j: (0, j))],
        out_specs=pl.BlockSpec((t, tn), lambda i, j: (i, j)),
        out_shape=SDS((nb * t, n), F32), compiler_params=_arb(2), name=name)(a, b)


def _wgrad_blocked_rhs(a, b, name, carry=()):
    k, m = a.shape
    nb, _, t = b.shape
    tm = _tile(m, 512, LANES)
    return _hosted(
        functools.partial(_wgrad_body), carry, grid=(m // tm, nb),
        in_specs=[pl.BlockSpec((k, tm), lambda i, j: (0, i)), pl.BlockSpec((None, k, t), lambda i, j: (j, 0, 0))],
        out_specs=pl.BlockSpec((None, tm, t), lambda i, j: (j, i, 0)),
        out_shape=SDS((nb, m, t), F32), compiler_params=_arb(2), name=name)(a, b)


def _unblock_cols(wb, name, carry=()):
    nb, r, t = wb.shape
    tr = _tile(r, 256, 16)

    def body(w_ref, o_ref):
        o_ref[...] = jnp.concatenate([w_ref[j].astype(F32) for j in range(nb)], axis=-1).astype(o_ref.dtype)

    return _hosted(
        body, carry, grid=(r // tr,),
        in_specs=[pl.BlockSpec((nb, tr, t), lambda i: (0, i, 0))], out_specs=pl.BlockSpec((tr, nb * t), lambda i: (i, 0)),
        out_shape=SDS((r, nb * t), wb.dtype), compiler_params=_arb(1), name=name)(wb)


def _block_cols(w, nb, name, carry=()):
    r, n = w.shape
    t = n // nb
    tr = _tile(r, 256, 16)

    def body(w_ref, o_ref):
        wv = w_ref[...]
        for j in range(nb):
            o_ref[j] = wv[:, j * t:(j + 1) * t]

    return _hosted(
        body, carry, grid=(r // tr,),
        in_specs=[pl.BlockSpec((tr, n), lambda i: (i, 0))], out_specs=pl.BlockSpec((nb, tr, t), lambda i: (0, i, 0)),
        out_shape=SDS((nb, r, t), w.dtype), compiler_params=_arb(1), name=name)(w)


def _place():
    x, y, c = lax.axis_index("x"), lax.axis_index("y"), lax.axis_index("c")
    return x, y, c, [(1 - x, y), (x, 1 - y), (1 - x, 1 - y)]


def _all_gather(shards):
    n = len(shards)

    def copies(ins, outs, sems, incoming):
        send_sems, recv_sems, local_sems = sems
        x, y, c, chips = _place()
        me, sibling = (x, y, c), (x, y, 1 - c)

        def blk(a, px, py, pc):
            return outs[a].at[4 * px + 2 * py + pc]

        def copy(a, k, block, to, src=None):
            return pltpu.make_async_remote_copy(
                src_ref=blk(a, *block) if src is None else src, dst_ref=blk(a, *block),
                send_sem=send_sems.at[7 * a + k], recv_sem=recv_sems.at[7 * a + k], device_id=to, device_id_type=MESH)

        if not incoming:
            first = [copy(a, 1 + j, me, (*chip, c), src=ins[a]) for a in range(n) for j, chip in enumerate(chips)]
            first += [copy(a, 0, me, sibling, src=ins[a]) for a in range(n)]
            return first + [pltpu.make_async_copy(ins[a], blk(a, *me), local_sems.at[a]) for a in range(n)]
        landed = [copy(a, 1 + j, (*chip, c), me) for a in range(n) for j, chip in enumerate(chips)]
        passed = [copy(a, 4 + j, (*chip, c), sibling) for a in range(n) for j, chip in enumerate(chips)]
        from_sibling = [copy(a, 0, sibling, me) for a in range(n)]
        from_sibling += [copy(a, 4 + j, (*chip, 1 - c), me) for a in range(n) for j, chip in enumerate(chips)]
        return landed, passed, from_sibling

    def start(ins, outs, sems):
        for cp in copies(ins, outs, sems, False):
            cp.start()

    def relay(ins, outs, sems):
        landed, passed, _ = copies(ins, outs, sems, True)
        for arrived, onward in zip(landed, passed):
            arrived.wait_recv()
            onward.start()

    def finish(ins, outs, sems):
        _, passed, from_sibling = copies(ins, outs, sems, True)
        for cp in from_sibling:
            cp.wait_recv()
        sent = copies(ins, outs, sems, False)
        for cp in sent[:4 * n] + passed:
            cp.wait_send()
        for cp in sent[4 * n:]:
            cp.wait()

    return _Exchange(shards, [SDS((N_DEV,) + s.shape, s.dtype) for s in shards],
                     [pltpu.SemaphoreType.DMA((7 * n,)), pltpu.SemaphoreType.DMA((7 * n,)), pltpu.SemaphoreType.DMA((n,))],
                     start, finish, relay)


def _swap_exchange(ins, out_shape, per, copies):
    def start(i, o, sems):
        for cp in copies(i, o, sems):
            cp.start()

    def finish(i, o, sems):
        for cp in copies(i, o, sems):
            cp.wait()

    n = per * len(ins)
    return _Exchange(ins, out_shape, [pltpu.SemaphoreType.DMA((n,)), pltpu.SemaphoreType.DMA((n,))], start, finish)


def _exchange_c(gs):
    def copies(ins, outs, sems):
        x, y, c, _ = _place()
        return [pltpu.make_async_remote_copy(
                    src_ref=ins[a].at[2 * k + 1 - c], dst_ref=outs[a].at[k],
                    send_sem=sems[0].at[4 * a + k], recv_sem=sems[1].at[4 * a + k],
                    device_id=(x, y, 1 - c), device_id_type=MESH)
                for a in range(len(gs)) for k in range(4)]

    return _swap_exchange(gs, [SDS((4,) + g.shape[1:], g.dtype) for g in gs], 4, copies)


def _exchange_xy(sends):
    def copies(ins, outs, sems):
        x, y, c, chips = _place()
        return [pltpu.make_async_remote_copy(
                    src_ref=ins[a].at[t], dst_ref=outs[a].at[t],
                    send_sem=sems[0].at[3 * a + t], recv_sem=sems[1].at[3 * a + t],
                    device_id=(*chips[t], c), device_id_type=MESH)
                for a in range(len(sends)) for t in range(3)]

    return _swap_exchange(sends, [SDS(s.shape, s.dtype) for s in sends], 3, copies)


def _rs_combine(g, recv, pos, name, carry=()):
    _, r, cdim = g.shape
    tr = _tile(r, 256, 16)

    def body(pos_ref, g0, r0, g1, r1, g2, r2, g3, r3, keep_ref, send_ref):
        keep_ref[...] = g0[...] + r0[...]
        send_ref[0] = (g1[...] + r1[...]).astype(BF16)
        send_ref[1] = (g2[...] + r2[...]).astype(BF16)
        send_ref[2] = (g3[...] + r3[...]).astype(BF16)

    def k_of(p, t):
        px = p[0] if t in (0, 2) else 1 - p[0]
        py = p[1] if t in (0, 1) else 1 - p[1]
        return 2 * px + py

    blk = (None, tr, cdim)
    in_specs = []
    for t in range(4):
        in_specs.append(pl.BlockSpec(blk, functools.partial(lambda j, p, t: (2 * k_of(p, t) + p[2], j, 0), t=t)))
        in_specs.append(pl.BlockSpec(blk, functools.partial(lambda j, p, t: (k_of(p, t), j, 0), t=t)))
    return _hosted(
        body, carry, n_prefetch=1, out_shape=(SDS((r, cdim), F32), SDS((3, r, cdim), BF16)),
        grid=(r // tr,), in_specs=in_specs,
        out_specs=[pl.BlockSpec((tr, cdim), lambda j, p: (j, 0)), pl.BlockSpec((3, tr, cdim), lambda j, p: (0, j, 0))],
        compiler_params=_arb(1), name=name)(pos, g, recv, g, recv, g, recv, g, recv)


def _adamw_shard(keep, recv, w, m, v, name):
    r, cdim = w.shape
    tr = _tile(r, 256, 16)

    def body(k_ref, r_ref, w_ref, m_ref, v_ref, g_ref, d_ref, nm_ref, nv_ref):
        g = ((k_ref[...] + r_ref[0].astype(F32)) + r_ref[1].astype(F32)) + r_ref[2].astype(F32)
        g_ref[...] = g
        d_ref[...], nm_ref[...], nv_ref[...] = _adamw(w_ref[...], g, m_ref[...], v_ref[...])

    blk = pl.BlockSpec((tr, cdim), lambda j: (j, 0))
    out = SDS((r, cdim), F32)
    return _pcall(body, grid=(r // tr,), in_specs=[blk, pl.BlockSpec((3, tr, cdim), lambda j: (0, j, 0)), blk, blk, blk],
                  out_specs=[blk] * 4, out_shape=(out,) * 4, compiler_params=_arb(1), name=name)(keep, recv, w, m, v)


def _adamw_small(gathered, seg, params, conv_rows):
    names = list(params)
    c0, cn = conv_rows

    def body(*refs):
        gat_ref = refs[0]
        ins = refs[1:1 + 3 * len(names)]
        outs = refs[1 + 3 * len(names):]

        def total(r0, rn):
            tot = gat_ref[0, r0:r0 + rn, :]
            for dev in range(1, N_DEV):
                tot = tot + gat_ref[dev, r0:r0 + rn, :]
            return tot

        for k, nm in enumerate(names):
            g = total(*seg[nm])
            w_ref, m_ref, v_ref = ins[3 * k:3 * k + 3]
            g_ref, d_ref, nm_ref, nv_ref = outs[4 * k:4 * k + 4]
            g_ref[...] = g
            d_ref[...], nm_ref[...], nv_ref[...] = _adamw(w_ref[...], g, m_ref[...], v_ref[...])
        outs[-2][...] = total(c0, cn)
        outs[-1][...] = total(*seg["loss"])

    flat_in = [a for nm in names for a in params[nm]]
    out_shape = []
    for nm in names:
        out_shape += [SDS(params[nm][0].shape, F32)] * 4
    out_shape += [SDS((cn, LANES), F32), SDS((seg["loss"][1], LANES), F32)]
    res = _pcall(body, out_shape=tuple(out_shape), name="adamw_small")(gathered, *flat_in)
    per = {nm: res[4 * k:4 * k + 4] for k, nm in enumerate(names)}
    return per, res[-2], res[-1]


def _adamw_one(w, g, m, v, name):
    def body(w_ref, g_ref, m_ref, v_ref, d_ref, nm_ref, nv_ref):
        d_ref[...], nm_ref[...], nv_ref[...] = _adamw(w_ref[...], g_ref[...], m_ref[...], v_ref[...])

    return _pcall(body, out_shape=(SDS(w.shape, F32),) * 3, name=name)(w, g, m, v)


def _rows128(a):
    return a.reshape(-1, LANES)


def _pack_small(gs, loss_tile):
    seg, pieces, row = {}, [], 0
    for nm in SMALL + ("conv_w", "loss"):
        piece = loss_tile if nm == "loss" else _rows128(gs[nm])
        rn = _round_up(piece.shape[0], SUB)
        pieces.append(jnp.pad(piece, ((0, rn - piece.shape[0]), (0, 0))))
        seg[nm] = (row, piece.shape[0])
        row += rn
    return jnp.concatenate(pieces, axis=0), seg


def _step(x, mem, target, wb, conv_w, sp, pos):
    s, d = x.shape
    tm = min(TOKEN_TILE, s)
    tm_wide = min(2 * TOKEN_TILE, s)
    rows = lambda w8: w8.reshape(-1, w8.shape[2])
    shards = lambda g: g.reshape((N_DEV, -1) + g.shape[1:])
    bt = sp["b_spatial"].T

    (w_in8, conv8), = _run_exchanges([_all_gather([wb["w_in"], conv_w])], "gather_w_in")
    conv_full = conv8.transpose(1, 0, 2).reshape(3, -1)
    w_in, ((w_out8,),) = _unblock_cols(w_in8, "unblock_w_in", [_all_gather([wb["w_out"]])])
    (xn1, h), ((w_kv8, w_q8),) = _in_forward(
        x, sp["ln_mix_g"], w_in, tm_wide, carry=[_all_gather([wb["w_kv"], wb["w_q"]])])
    w_out = rows(w_out8)
    (ycat, x1), ((w_o8, w_down8),) = _mix_forward(
        h, x, sp["sgu_ln_g"], sp["sgu_ln_b"], sp["w_spatial"], bt, conv_full, sp["grp_norm_a"], sp["grp_norm_b"], w_out, tm,
        carry=[_all_gather([wb["w_o"], wb["w_down"]])])
    w_q, w_o, w_down = rows(w_q8), rows(w_o8), rows(w_down8)
    memn, kv = _kv_forward(mem, sp["ln_mem_g"], w_kv8)
    (xn2, q, o, x2), ((w_gu8,),) = _attn_forward(
        x1, sp["ln_attn_g"], w_q, kv, w_o, tm_wide, carry=[_all_gather([wb["w_gate_up"]])])
    w_gu = w_gu8.reshape((2, N_DEV // 2) + w_gu8.shape[1:])
    xn3, gu, x3 = _ffn_forward(x2, sp["ln_ffn_g"], w_gu, w_down, tm_wide)

    loss, d_lnf, d_lnffn, act, dgu, dx3b, dx2 = _ffn_backward(
        x3, target, sp["ln_final_g"], x2, sp["ln_ffn_g"], gu, w_gu, w_down, tm)
    part = {}
    g_gu, _ = _wgrad_blocked_rhs(xn3, dgu.reshape((N_DEV,) + dgu.shape[2:]), "wgrad_gate_up")
    g_down, ((rc_gu,),) = _wgrad_blocked_lhs(act, dx3b, "wgrad_down", carry=[_exchange_c([g_gu])])
    g_down = shards(g_down)
    (keep_gu, send_gu), _ = _rs_combine(g_gu, rc_gu, pos, "rs_combine_w_gate_up")
    (dx2b, dq, dx1, dkv, d_lnattn), ((rxy_gu,), (rc_down,)) = _attn_backward(
        dx2, x1, sp["ln_attn_g"], q, kv, w_q, w_o, tm, carry=[_exchange_xy([send_gu]), _exchange_c([g_down])])
    part["w_gate_up"] = (keep_gu, rxy_gu)
    (keep_down, send_down), _ = _rs_combine(g_down, rc_down, pos, "rs_combine_w_down")
    g_o, _ = _wgrad(o, dx2b, "wgrad_o")
    g_o = shards(g_o)
    g_q, ((rc_o,),) = _wgrad(xn2, dq, "wgrad_q", carry=[_exchange_c([g_o])])
    g_q = shards(g_q)
    g_kv, d_lnmem = _kv_backward(dkv, memn, mem, sp["ln_mem_g"], w_kv8)
    (keep_o, send_o), ((rc_q,),) = _rs_combine(g_o, rc_o, pos, "rs_combine_w_o", carry=[_exchange_c([g_q])])
    (keep_q, send_q), _ = _rs_combine(g_q, rc_q, pos, "rs_combine_w_q")
    ((dx1b, dh, dx, d_ga, d_gb, d_cw, d_lng, d_lnb, d_wsp, d_bs, d_lnmix),
     ((rxy_down, rxy_o, rxy_q), (rc_kv,))) = _mix_backward(
        dx1, x, sp["ln_mix_g"], h, sp["sgu_ln_g"], sp["sgu_ln_b"], sp["w_spatial"], bt, conv_full,
        sp["grp_norm_a"], sp["grp_norm_b"], w_out, w_in, tm,
        carry=[_exchange_xy([send_down, send_o, send_q]), _exchange_c([g_kv])])
    part["w_down"], part["w_o"], part["w_q"] = (keep_down, rxy_down), (keep_o, rxy_o), (keep_q, rxy_q)
    (keep_kv, send_kv), _ = _rs_combine(g_kv, rc_kv, pos, "rs_combine_w_kv")
    gs = {"ln_mix_g": d_lnmix, "sgu_ln_g": d_lng, "sgu_ln_b": d_lnb, "w_spatial": d_wsp, "b_spatial": _bias_grad(d_bs),
          "conv_w": d_cw[:3], "grp_norm_a": d_ga, "grp_norm_b": d_gb, "ln_attn_g": d_lnattn, "ln_mem_g": d_lnmem,
          "ln_ffn_g": d_lnffn, "ln_final_g": d_lnf}
    packed, seg = _pack_small(gs, loss)
    g_in, ((rxy_kv,), (small_all,)) = _wgrad(
        xn1, dh, "wgrad_in", carry=[_exchange_xy([send_kv]), _all_gather([packed])])
    part["w_kv"] = (keep_kv, rxy_kv)
    g_in, _ = _block_cols(g_in, N_DEV, "block_grad_w_in")
    g_out, ((rc_in,),) = _wgrad(ycat, dx1b, "wgrad_out", carry=[_exchange_c([g_in])])
    g_out = shards(g_out)
    (keep_in, send_in), ((rc_out,),) = _rs_combine(g_in, rc_in, pos, "rs_combine_w_in", carry=[_exchange_c([g_out])])
    (keep_out, send_out), _ = _rs_combine(g_out, rc_out, pos, "rs_combine_w_out")
    (rxy_in, rxy_out), = _run_exchanges([_exchange_xy([send_in, send_out])], "exchange_xy_w_in_w_out")
    part["w_in"], part["w_out"] = (keep_in, rxy_in), (keep_out, rxy_out)
    return dx, part, small_all, seg


def kernel(x, mem, ln_mix_g, w_in, sgu_ln_g, sgu_ln_b, w_spatial, b_spatial, conv_w, grp_norm_a, grp_norm_b, w_out, ln_attn_g, ln_mem_g, w_q, w_kv, w_o, ln_ffn_g, w_gate_up, w_down, ln_final_g, loss_target, m_ln_mix_g, m_w_in, m_sgu_ln_g, m_sgu_ln_b, m_w_spatial, m_b_spatial, m_conv_w, m_grp_norm_a, m_grp_norm_b, m_w_out, m_ln_attn_g, m_ln_mem_g, m_w_q, m_w_kv, m_w_o, m_ln_ffn_g, m_w_gate_up, m_w_down, m_ln_final_g, v_ln_mix_g, v_w_in, v_sgu_ln_g, v_sgu_ln_b, v_w_spatial, v_b_spatial, v_conv_w, v_grp_norm_a, v_grp_norm_b, v_w_out, v_ln_attn_g, v_ln_mem_g, v_w_q, v_w_kv, v_w_o, v_ln_ffn_g, v_w_gate_up, v_w_down, v_ln_final_g):
    order = ["ln_mix_g", "w_in", "sgu_ln_g", "sgu_ln_b", "w_spatial", "b_spatial", "conv_w", "grp_norm_a", "grp_norm_b",
             "w_out", "ln_attn_g", "ln_mem_g", "w_q", "w_kv", "w_o", "ln_ffn_g", "w_gate_up", "w_down", "ln_final_g"]
    W = dict(ln_mix_g=ln_mix_g, w_in=w_in, sgu_ln_g=sgu_ln_g, sgu_ln_b=sgu_ln_b, w_spatial=w_spatial, b_spatial=b_spatial,
             conv_w=conv_w, grp_norm_a=grp_norm_a, grp_norm_b=grp_norm_b, w_out=w_out, ln_attn_g=ln_attn_g,
             ln_mem_g=ln_mem_g, w_q=w_q, w_kv=w_kv, w_o=w_o, ln_ffn_g=ln_ffn_g, w_gate_up=w_gate_up, w_down=w_down,
             ln_final_g=ln_final_g)
    M = dict(ln_mix_g=m_ln_mix_g, w_in=m_w_in, sgu_ln_g=m_sgu_ln_g, sgu_ln_b=m_sgu_ln_b, w_spatial=m_w_spatial,
             b_spatial=m_b_spatial, conv_w=m_conv_w, grp_norm_a=m_grp_norm_a, grp_norm_b=m_grp_norm_b, w_out=m_w_out,
             ln_attn_g=m_ln_attn_g, ln_mem_g=m_ln_mem_g, w_q=m_w_q, w_kv=m_w_kv, w_o=m_w_o, ln_ffn_g=m_ln_ffn_g,
             w_gate_up=m_w_gate_up, w_down=m_w_down, ln_final_g=m_ln_final_g)
    V = dict(ln_mix_g=v_ln_mix_g, w_in=v_w_in, sgu_ln_g=v_sgu_ln_g, sgu_ln_b=v_sgu_ln_b, w_spatial=v_w_spatial,
             b_spatial=v_b_spatial, conv_w=v_conv_w, grp_norm_a=v_grp_norm_a, grp_norm_b=v_grp_norm_b, w_out=v_w_out,
             ln_attn_g=v_ln_attn_g, ln_mem_g=v_ln_mem_g, w_q=v_w_q, w_kv=v_w_kv, w_o=v_w_o, ln_ffn_g=v_ln_ffn_g,
             w_gate_up=v_w_gate_up, w_down=v_w_down, ln_final_g=v_ln_final_g)

    bw = conv_w.shape[1] * N_DEV
    pos = jnp.stack([lax.axis_index("x"), lax.axis_index("y"), lax.axis_index("c")]).astype(jnp.int32)
    me = 4 * pos[0] + 2 * pos[1] + pos[2]

    sp = {nm: (W[nm].reshape(1, -1) if W[nm].ndim == 1 else W[nm]) for nm in SMALL}
    wb = {nm: W[nm].astype(BF16) for nm in BIG}
    grad_x, part, small_all, seg = _step(x[0], mem[0], loss_target[0], wb, conv_w, sp, pos)

    out = {}
    for nm in BIG:
        out[nm] = tuple(_adamw_shard(part[nm][0], part[nm][1], W[nm], M[nm], V[nm], "adamw_" + nm))

    params = {nm: (_rows128(W[nm]), _rows128(M[nm]), _rows128(V[nm])) for nm in SMALL}
    per, conv_g_rows, loss_sum = _adamw_small(small_all, seg, params, seg["conv_w"])
    for nm in SMALL:
        out[nm] = tuple(a.reshape(W[nm].shape) for a in per[nm])
    conv_g = lax.dynamic_slice_in_dim(conv_g_rows.reshape(3, bw), me * conv_w.shape[1], conv_w.shape[1], axis=1)
    out["conv_w"] = (conv_g,) + tuple(_adamw_one(conv_w, conv_g, m_conv_w, v_conv_w, "adamw_conv"))

    loss = loss_sum[0, 0]
    res = [loss, grad_x[None]]
    for k in range(4):
        res += [out[nm][k] for nm in order]
    return tuple(res)
```

```python
import functools

import jax
import jax.numpy as jnp
from jax import lax
from jax.experimental import pallas as pl
from jax.experimental.pallas import tpu as pltpu

F32 = jnp.float32
BF16 = jnp.bfloat16
SDS = jax.ShapeDtypeStruct
MESH = pl.DeviceIdType.MESH

EPS = 1e-6
N_DEV = 8
HEADS = 4
CHUNK = 128
HALO = 16
SUB = 8
LANES = 128
TOKEN_TILE = 512

ADAM_LR = 0.001
ADAM_B1 = 0.9
ADAM_B2 = 0.999
ADAM_EPS = 1e-08
ADAM_WD = 0.01
ADAM_STEP = 10

BIG = ("w_in", "w_out", "w_q", "w_kv", "w_o", "w_gate_up", "w_down")
TRANSPOSED = ("w_in", "w_gate_up")
SMALL = ("ln_mix_g", "sgu_ln_g", "sgu_ln_b", "w_spatial", "b_spatial", "grp_norm_a", "grp_norm_b",
         "ln_attn_g", "ln_mem_g", "ln_ffn_g", "ln_final_g")


class _Exchange:
    def __init__(self, ins, out_shape, sems, start, finish, relay=None):
        self.ins, self.out_shape, self.sems = list(ins), list(out_shape), list(sems)
        self.start, self.finish, self.relay = start, finish, relay


def _pcall(body, carry=(), n_prefetch=0, **kw):
    if carry:
        return functools.partial(_carrying_call, body, tuple(carry), n_prefetch, kw)
    if n_prefetch:
        kw["grid_spec"] = pltpu.PrefetchScalarGridSpec(
            num_scalar_prefetch=n_prefetch, grid=kw.pop("grid"), in_specs=kw.pop("in_specs"),
            out_specs=kw.pop("out_specs"), scratch_shapes=kw.pop("scratch_shapes", ()))
    return pl.pallas_call(body, **kw)


def _carrying_call(body, carry, n_prefetch, kw, *args):
    kw = dict(kw)
    out_shape = kw.pop("out_shape")
    single = not isinstance(out_shape, (tuple, list))
    outs_shape = (out_shape,) if single else tuple(out_shape)
    out_specs = kw.pop("out_specs")
    out_specs = [out_specs] if single else list(out_specs)
    in_specs = list(kw.pop("in_specs"))
    scratch = list(kw.pop("scratch_shapes", ()))
    grid = tuple(kw.get("grid", ()))
    n_in, n_out, n_scr = len(args), len(outs_shape), len(scratch)

    def split(refs, k, counts):
        parts = []
        for cnt in counts:
            parts.append(refs[k:k + cnt])
            k += cnt
        return parts, k

    def wrapped(*refs):
        cins, k = split(refs, n_in, [len(p.ins) for p in carry])
        outs = refs[k:k + n_out]
        couts, k = split(refs, k + n_out, [len(p.out_shape) for p in carry])
        scr = refs[k:k + n_scr]
        csems, _ = split(refs, k + n_scr, [len(p.sems) for p in carry])
        first, last = True, True
        for a, g in enumerate(grid):
            first = (pl.program_id(a) == 0) & first
            last = (pl.program_id(a) == g - 1) & last

        def start_all():
            for p, ci, co, cs in zip(carry, cins, couts, csems):
                p.start(ci, co, cs)

        def relay_all():
            for p, ci, co, cs in zip(carry, cins, couts, csems):
                if p.relay is not None:
                    p.relay(ci, co, cs)

        def finish_all():
            for p, ci, co, cs in zip(carry, cins, couts, csems):
                p.finish(ci, co, cs)

        start_all() if not grid else pl.when(first)(start_all)
        relay_all() if not grid else pl.when(last)(relay_all)
        body(*refs[:n_in], *outs, *scr)
        finish_all() if not grid else pl.when(last)(finish_all)

    c_in = [a for p in carry for a in p.ins]
    c_out = [s for p in carry for s in p.out_shape]
    c_sems = [s for p in carry for s in p.sems]
    res = _pcall(wrapped, n_prefetch=n_prefetch, out_shape=outs_shape + tuple(c_out),
                 in_specs=in_specs + _hbm_specs(len(c_in)), out_specs=out_specs + _hbm_specs(len(c_out)),
                 scratch_shapes=scratch + c_sems, **kw)(*args, *c_in)
    own = res[0] if single else tuple(res[:n_out])
    landed, k = [], n_out
    for p in carry:
        landed.append(list(res[k:k + len(p.out_shape)]))
        k += len(p.out_shape)
    return own, landed


def _hbm_specs(n):
    return [pl.BlockSpec(memory_space=pl.ANY)] * n


def _hosted(body, carry, **kw):
    if carry:
        return _pcall(body, carry=carry, **kw)
    call = _pcall(body, **kw)
    return lambda *args: (call(*args), [])


def _run_exchanges(parts, name):
    def body(*refs):
        pass

    _, landed = _pcall(body, carry=parts, out_shape=(), in_specs=[], out_specs=[], name=name)()
    return landed


def _arb(n):
    return pltpu.CompilerParams(dimension_semantics=("arbitrary",) * n)


def _tile(n, target, mult):
    best = None
    for t in range(mult, min(n, target) + 1, mult):
        if n % t == 0:
            best = t
    return n if best is None else best


def _round_up(n, m):
    return (n + m - 1) // m * m


def _dot(a, b):
    return jnp.dot(a, b, preferred_element_type=F32)


def _dot_nt(a, b):
    return lax.dot_general(a, b, (((1,), (1,)), ((), ())), preferred_element_type=F32)


def _dot_tn(a, b):
    return lax.dot_general(a, b, (((0,), (0,)), ((), ())), preferred_element_type=F32)


def _rstd(x):
    return lax.rsqrt(jnp.mean(x * x, axis=-1, keepdims=True) + EPS)


def _rms_bwd(dy, x, r, g):
    gdy = dy * g
    proj = jnp.sum(gdy * x, axis=-1, keepdims=True) * (1.0 / x.shape[-1])
    dx = r * gdy - x * (r * r * r) * proj
    dg = jnp.sum(dy * (x * r), axis=0, keepdims=True)
    return dx, dg


_GELU_C = 0.7978845608028654
_GELU_A = 0.044715


def _gelu(x):
    t = jnp.tanh(_GELU_C * (x + _GELU_A * x * x * x))
    return 0.5 * x * (1.0 + t), t


def _gelu_grad(x, t):
    return 0.5 * (1.0 + t) + 0.5 * x * (1.0 - t * t) * (_GELU_C * (1.0 + 3.0 * _GELU_A * x * x))


def _sigmoid(x):
    return 1.0 / (1.0 + jnp.exp(-x))


def _softmax(s):
    m = jnp.max(s, axis=-1, keepdims=True)
    e = jnp.exp(s - m)
    return e / jnp.sum(e, axis=-1, keepdims=True)


def _adamw(w, g, m, v):
    m = ADAM_B1 * m + (1.0 - ADAM_B1) * g
    v = ADAM_B2 * v + (1.0 - ADAM_B2) * (g * g)
    m_hat = m / (1.0 - ADAM_B1 ** ADAM_STEP)
    v_hat = v / (1.0 - ADAM_B2 ** ADAM_STEP)
    delta = -ADAM_LR * (m_hat / (jnp.sqrt(v_hat) + ADAM_EPS) + ADAM_WD * w)
    return delta, m, v


def _tril_mask():
    t = lax.broadcasted_iota(jnp.int32, (CHUNK, CHUNK), 0)
    s = lax.broadcasted_iota(jnp.int32, (CHUNK, CHUNK), 1)
    return (s <= t).astype(F32)


def _sgu_forward(ha, lng, lnb, wm, bt, mixed_s):
    aw = ha.shape[1] // 2
    hd = aw // HEADS
    a, th = _gelu(ha)
    u = a[:, :aw]
    v = a[:, aw:]
    mu = jnp.mean(v, axis=-1, keepdims=True)
    vc = v - mu
    rl = lax.rsqrt(jnp.mean(vc * vc, axis=-1, keepdims=True) + EPS)
    xhat = vc * rl
    vln = (xhat * lng + lnb).astype(BF16)
    for n in range(ha.shape[0] // CHUNK):
        rows = slice(n * CHUNK, (n + 1) * CHUNK)
        for h in range(HEADS):
            cols = slice(h * hd, (h + 1) * hd)
            mixed_s[rows, cols] = _dot(wm[h], vln[rows, cols]) + bt[:, h:h + 1]
    return th, u, xhat, rl, vln


def _conv_taps(zext):
    return pltpu.roll(zext, 2, 0), pltpu.roll(zext, 1, 0)


def _kv_forward(mem, g_mem, w_kv):
    ml, d = mem.shape
    xd = w_kv.shape[2]

    def body(mem_ref, g_ref, w_ref, memn_ref, kv_ref):
        x = mem_ref[...]
        memn = (x * _rstd(x) * g_ref[...]).astype(BF16)
        memn_ref[...] = memn
        for j in range(2 * HEADS):
            kv_ref[j] = _dot(memn, w_ref[j]).astype(BF16)

    return _pcall(body, out_shape=(SDS((ml, d), BF16), SDS((2 * HEADS, ml, xd), BF16)), name="kv_forward")(mem, g_mem, w_kv)


def _in_forward(x, g, w_in_t, tm, carry=()):
    s, d = x.shape
    n_in = w_in_t.shape[0]

    def body(x_ref, g_ref, w_ref, xn_ref, h_ref):
        xv = x_ref[...]
        xn = (xv * _rstd(xv) * g_ref[...]).astype(BF16)
        xn_ref[...] = xn
        h_ref[...] = _dot_nt(xn, w_ref[...])

    return _hosted(
        body, carry, grid=(s // tm,),
        in_specs=[pl.BlockSpec((tm, d), lambda i: (i, 0)), pl.BlockSpec((1, d), lambda i: (0, 0)),
                  pl.BlockSpec((n_in, d), lambda i: (0, 0))],
        out_specs=[pl.BlockSpec((tm, d), lambda i: (i, 0)), pl.BlockSpec((tm, n_in), lambda i: (i, 0))],
        out_shape=(SDS((s, d), BF16), SDS((s, n_in), F32)),
        compiler_params=_arb(1), name="in_forward")(x, g, w_in_t)


def _mix_forward(h, x, lng, lnb, w_sp, bt, conv_w, ga, gb, w_out, tm, carry=()):
    s, d = x.shape
    n_in = h.shape[1]
    aw = lng.shape[1]
    bw = d - aw
    in_a = 2 * aw
    hb_blocks = tm // HALO

    def body(h_ref, hprev_ref, x_ref, lng_ref, lnb_ref, wsp_ref, bt_ref, cw_ref, ga_ref, gb_ref, wout_ref,
             ycat_ref, x1_ref, mixed_s):
        i = pl.program_id(0)
        mask = _tril_mask()
        wm = [(wsp_ref[hh] * mask).astype(BF16) for hh in range(HEADS)]
        hv = h_ref[...]
        _, u, _, _, _ = _sgu_forward(hv[:, :in_a], lng_ref[...], lnb_ref[...], wm, bt_ref[...], mixed_s)
        sg = u * mixed_s[...]
        ycat_ref[:, :aw] = (sg * _rstd(sg) * ga_ref[...]).astype(BF16)

        gate_b = hv[:, in_a:in_a + bw]
        z = hv[:, in_a + bw:in_a + 2 * bw] * hv[:, in_a + 2 * bw:]
        hp = hprev_ref[...]
        zp = hp[:, in_a + bw:in_a + 2 * bw] * hp[:, in_a + 2 * bw:]
        zp = jnp.where(i == 0, 0.0, zp)
        zext = jnp.concatenate([zp, z], axis=0)
        z2, z1 = _conv_taps(zext)
        cw = cw_ref[...]
        conv = cw[0:1] * z2[HALO:] + cw[1:2] * z1[HALO:] + cw[2:3] * z
        sc = gate_b * conv
        ycat_ref[:, aw:] = (sc * _rstd(sc) * gb_ref[...]).astype(BF16)
        x1_ref[...] = x_ref[...] + _dot(ycat_ref[...], wout_ref[...])

    full = lambda shape: pl.BlockSpec(shape, lambda i: (0,) * len(shape))
    return _hosted(
        body, carry, grid=(s // tm,),
        in_specs=[pl.BlockSpec((tm, n_in), lambda i: (i, 0)),
                  pl.BlockSpec((HALO, n_in), lambda i: (jnp.maximum(i * hb_blocks - 1, 0), 0)),
                  pl.BlockSpec((tm, d), lambda i: (i, 0)),
                  full((1, aw)), full((1, aw)), full((HEADS, CHUNK, CHUNK)), full((CHUNK, HEADS)),
                  full((3, bw)), full((1, aw)), full((1, bw)), full((d, d))],
        out_specs=[pl.BlockSpec((tm, d), lambda i: (i, 0)), pl.BlockSpec((tm, d), lambda i: (i, 0))],
        out_shape=(SDS((s, d), BF16), SDS((s, d), F32)),
        scratch_shapes=[pltpu.VMEM((tm, aw), F32)],
        compiler_params=_arb(1), name="mix_forward")(h, h, x, lng, lnb, w_sp, bt, conv_w, ga, gb, w_out)


def _attn_forward(x1, g, w_q, kv, w_o, tm, carry=()):
    s, d = x1.shape
    _, ml, xd = kv.shape
    scale = xd ** -0.5

    def body(x1_ref, g_ref, wq_ref, kv_ref, wo_ref, xn_ref, q_ref, o_ref, x2_ref):
        xv = x1_ref[...]
        xn = (xv * _rstd(xv) * g_ref[...]).astype(BF16)
        xn_ref[...] = xn
        q_ref[...] = _dot(xn, wq_ref[...]).astype(BF16)
        for hh in range(HEADS):
            cols = slice(hh * xd, (hh + 1) * xd)
            p = _softmax(_dot_nt(q_ref[:, cols], kv_ref[hh]) * scale)
            o_ref[:, cols] = _dot(p.astype(BF16), kv_ref[HEADS + hh]).astype(BF16)
        x2_ref[...] = xv + _dot(o_ref[...], wo_ref[...])

    tok = pl.BlockSpec((tm, d), lambda i: (i, 0))
    return _hosted(
        body, carry, grid=(s // tm,),
        in_specs=[tok, pl.BlockSpec((1, d), lambda i: (0, 0)), pl.BlockSpec((d, d), lambda i: (0, 0)),
                  pl.BlockSpec((2 * HEADS, ml, xd), lambda i: (0, 0, 0)), pl.BlockSpec((d, d), lambda i: (0, 0))],
        out_specs=[tok, tok, tok, tok],
        out_shape=(SDS((s, d), BF16), SDS((s, d), BF16), SDS((s, d), BF16), SDS((s, d), F32)),
        compiler_params=_arb(1), name="attn_forward")(x1, g, w_q, kv, w_o)


def _ffn_forward(x2, g, w_gu, w_down, tm):
    s, d = x2.shape
    _, nf, tf, _ = w_gu.shape

    def body(x2_ref, g_ref, wgu_ref, wd_ref, xn_ref, gu_ref, x3_ref):
        f = pl.program_id(1)

        @pl.when(f == 0)
        def _():
            xv = x2_ref[...]
            xn_ref[...] = (xv * _rstd(xv) * g_ref[...]).astype(BF16)
            x3_ref[...] = xv

        xn = xn_ref[...]
        gate = _dot_nt(xn, wgu_ref[0])
        up = _dot_nt(xn, wgu_ref[1])
        gu_ref[0] = gate.astype(BF16)
        gu_ref[1] = up.astype(BF16)
        act = (gate * _sigmoid(gate) * up).astype(BF16)
        x3_ref[...] += _dot(act, wd_ref[...])

    tok = pl.BlockSpec((tm, d), lambda i, f: (i, 0))
    return _pcall(
        body, grid=(s // tm, nf),
        in_specs=[tok, pl.BlockSpec((1, d), lambda i, f: (0, 0)),
                  pl.BlockSpec((2, None, tf, d), lambda i, f: (0, f, 0, 0)),
                  pl.BlockSpec((tf, d), lambda i, f: (f, 0))],
        out_specs=[tok, pl.BlockSpec((2, None, tm, tf), lambda i, f: (0, f, i, 0)), tok],
        out_shape=(SDS((s, d), BF16), SDS((2, nf, s, tf), BF16), SDS((s, d), F32)),
        compiler_params=_arb(2), name="ffn_forward")(x2, g, w_gu, w_down)


def _ffn_backward(x3, target, g_final, x2, g_ffn, gu, w_gu, w_down, tm):
    s, d = x3.shape
    _, nf, tf, _ = w_gu.shape

    def body(x3_ref, tgt_ref, gf_ref, x2_ref, g2_ref, gu_ref, wgu_ref, wd_ref,
             loss_ref, dgf_ref, dg2_ref, act_ref, dgu_ref, dx3b_ref, dx2_ref, dx3_s, acc_s):
        i = pl.program_id(0)
        f = pl.program_id(1)

        @pl.when((i == 0) & (f == 0))
        def _():
            loss_ref[...] = jnp.zeros_like(loss_ref)
            dgf_ref[...] = jnp.zeros_like(dgf_ref)
            dg2_ref[...] = jnp.zeros_like(dg2_ref)

        @pl.when(f == 0)
        def _():
            xv = x3_ref[...]
            r = _rstd(xv)
            diff = xv * r * gf_ref[...] - tgt_ref[...]
            loss_ref[...] += 0.5 * jnp.sum(jnp.sum(diff * diff, axis=-1, keepdims=True), axis=0, keepdims=True) * (1.0 / d)
            dx3, dgf = _rms_bwd(diff * (1.0 / d), xv, r, gf_ref[...])
            dgf_ref[...] += dgf
            dx3_s[...] = dx3
            dx3b_ref[...] = dx3.astype(BF16)
            acc_s[...] = jnp.zeros_like(acc_s)

        dact = _dot_nt(dx3b_ref[...], wd_ref[...])
        gv = gu_ref[0].astype(F32)
        uv = gu_ref[1].astype(F32)
        sg = _sigmoid(gv)
        silu = gv * sg
        act_ref[...] = (silu * uv).astype(BF16)
        dgate = (dact * uv * (sg * (1.0 + gv * (1.0 - sg)))).astype(BF16)
        dup = (dact * silu).astype(BF16)
        dgu_ref[0] = dgate
        dgu_ref[1] = dup
        acc_s[...] += _dot(dgate, wgu_ref[0]) + _dot(dup, wgu_ref[1])

        @pl.when(f == nf - 1)
        def _():
            xv = x2_ref[...]
            dxn, dg2 = _rms_bwd(acc_s[...], xv, _rstd(xv), g2_ref[...])
            dg2_ref[...] += dg2
            dx2_ref[...] = dx3_s[...] + dxn

    tok = pl.BlockSpec((tm, d), lambda i, f: (i, 0))
    pair = pl.BlockSpec((2, None, tm, tf), lambda i, f: (0, f, i, 0))
    vec = pl.BlockSpec((1, d), lambda i, f: (0, 0))
    return _pcall(
        body, grid=(s // tm, nf),
        in_specs=[tok, tok, vec, tok, vec, pair,
                  pl.BlockSpec((2, None, tf, d), lambda i, f: (0, f, 0, 0)),
                  pl.BlockSpec((tf, d), lambda i, f: (f, 0))],
        out_specs=[pl.BlockSpec((SUB, LANES), lambda i, f: (0, 0)), vec, vec,
                   pl.BlockSpec((None, tm, tf), lambda i, f: (f, i, 0)), pair, tok, tok],
        out_shape=(SDS((SUB, LANES), F32), SDS((1, d), F32), SDS((1, d), F32), SDS((nf, s, tf), BF16),
                   SDS((2, nf, s, tf), BF16), SDS((s, d), BF16), SDS((s, d), F32)),
        scratch_shapes=[pltpu.VMEM((tm, d), F32), pltpu.VMEM((tm, d), F32)],
        compiler_params=_arb(2), name="ffn_backward")(x3, target, g_final, x2, g_ffn, gu, w_gu, w_down)


def _attn_backward(dx2, x1, g, q, kv, w_q, w_o, tm, carry=()):
    s, d = x1.shape
    _, ml, xd = kv.shape
    scale = xd ** -0.5

    def body(dx2_ref, x1_ref, g_ref, q_ref, kv_ref, wq_ref, wo_ref,
             dx2b_ref, dq_ref, dx1_ref, dkv_ref, dg_ref, do_s):
        i = pl.program_id(0)

        @pl.when(i == 0)
        def _():
            dkv_ref[...] = jnp.zeros_like(dkv_ref)
            dg_ref[...] = jnp.zeros_like(dg_ref)

        dx2 = dx2_ref[...]
        dx2b_ref[...] = dx2.astype(BF16)
        do_s[...] = _dot_nt(dx2b_ref[...], wo_ref[...]).astype(BF16)
        for hh in range(HEADS):
            kc = slice(hh * xd, (hh + 1) * xd)
            qh = q_ref[:, kc]
            kh = kv_ref[hh]
            doh = do_s[:, kc]
            p = _softmax(_dot_nt(qh, kh) * scale)
            dp = _dot_nt(doh, kv_ref[HEADS + hh])
            dkv_ref[HEADS + hh] += _dot_tn(p.astype(BF16), doh)
            ds = (p * (dp - jnp.sum(dp * p, axis=-1, keepdims=True)) * scale).astype(BF16)
            dq_ref[:, kc] = _dot(ds, kh).astype(BF16)
            dkv_ref[hh] += _dot_tn(ds, qh)
        dxn = _dot_nt(dq_ref[...], wq_ref[...])
        xv = x1_ref[...]
        dx, dg = _rms_bwd(dxn, xv, _rstd(xv), g_ref[...])
        dg_ref[...] += dg
        dx1_ref[...] = dx2 + dx

    tok = pl.BlockSpec((tm, d), lambda i: (i, 0))
    vec = pl.BlockSpec((1, d), lambda i: (0, 0))
    sq = pl.BlockSpec((d, d), lambda i: (0, 0))
    kvs = pl.BlockSpec((2 * HEADS, ml, xd), lambda i: (0, 0, 0))
    return _hosted(
        body, carry, grid=(s // tm,),
        in_specs=[tok, tok, vec, tok, kvs, sq, sq],
        out_specs=[tok, tok, tok, kvs, vec],
        out_shape=(SDS((s, d), BF16), SDS((s, d), BF16), SDS((s, d), F32), SDS((2 * HEADS, ml, xd), F32), SDS((1, d), F32)),
        scratch_shapes=[pltpu.VMEM((tm, d), BF16)],
        compiler_params=_arb(1), name="attn_backward")(dx2, x1, g, q, kv, w_q, w_o)


def _kv_backward(dkv, memn, mem, g_mem, w_kv):
    ml, d = mem.shape
    xd = w_kv.shape[2]

    def body(dkv_ref, memn_ref, mem_ref, g_ref, w_ref, dw_ref, dg_ref):
        dmemn = jnp.zeros((ml, d), F32)
        for j in range(2 * HEADS):
            dkvb = dkv_ref[j].astype(BF16)
            dw_ref[j] = _dot_tn(memn_ref[...], dkvb)
            dmemn = dmemn + _dot_nt(dkvb, w_ref[j])
        x = mem_ref[...]
        dg_ref[...] = jnp.sum(dmemn * (x * _rstd(x)), axis=0, keepdims=True)

    return _pcall(body, out_shape=(SDS((2 * HEADS, d, xd), F32), SDS((1, d), F32)), name="kv_backward")(dkv, memn, mem, g_mem, w_kv)


def _mix_backward(dx1, x, g_mix, h, lng, lnb, w_sp, bt, conv_w, ga, gb, w_out, w_in, tm, carry=()):
    s, d = x.shape
    n_in = h.shape[1]
    aw = lng.shape[1]
    bw = d - aw
    hd = aw // HEADS
    in_a = 2 * aw
    hb_blocks = tm // HALO
    last_blk = s // HALO - 1
    nt = s // tm
    te = tm + HALO
    tee = tm + 2 * HALO

    def body(dx1_ref, dx1n_ref, x_ref, gm_ref, h_ref, hp_ref, hn_ref, lng_ref, lnb_ref, wsp_ref, bt_ref, cw_ref,
             ga_ref, gb_ref, wout_ref, win_ref,
             dx1b_ref, dh_ref, dx_ref, dga_ref, dgb_ref, dcw_ref, dlng_ref, dlnb_ref, dwsp_ref, dbs_ref, dgm_ref,
             mixed_s, dvln_s):
        i = pl.program_id(0)

        @pl.when(i == 0)
        def _():
            for ref in (dga_ref, dgb_ref, dcw_ref, dlng_ref, dlnb_ref, dwsp_ref, dbs_ref, dgm_ref):
                ref[...] = jnp.zeros_like(ref)

        mask = _tril_mask()
        wm = [(wsp_ref[hh] * mask).astype(BF16) for hh in range(HEADS)]
        hv = h_ref[...]
        dx1 = dx1_ref[...]
        dx1b_ref[...] = dx1.astype(BF16)
        dx1e = jnp.concatenate([dx1, dx1n_ref[...]], axis=0).astype(BF16)
        dycat = _dot_nt(dx1e, wout_ref[...])

        hbe = jnp.concatenate([hp_ref[:, in_a:], hv[:, in_a:], hn_ref[:, in_a:]], axis=0)
        row = lax.broadcasted_iota(jnp.int32, (tee, 1), 0)
        zext = hbe[:, bw:2 * bw] * hbe[:, 2 * bw:]
        zext = jnp.where((i == 0) & (row < HALO), 0.0, zext)
        z2e, z1e = _conv_taps(zext)
        cw = cw_ref[...]
        conv_e = (cw[0:1] * z2e + cw[1:2] * z1e + cw[2:3] * zext)[HALO:]
        gate_b_e = hbe[HALO:, :bw]
        sc_e = gate_b_e * conv_e
        rb = _rstd(sc_e)
        dyb = dycat[:, aw:]
        gdy = dyb * gb_ref[...]
        dsc_e = rb * gdy - sc_e * (rb * rb * rb) * (jnp.sum(gdy * sc_e, axis=-1, keepdims=True) * (1.0 / bw))
        dgb_ref[...] += jnp.sum((dyb * (sc_e * rb))[:tm], axis=0, keepdims=True)
        dconv_e = dsc_e * gate_b_e
        dconv_e = jnp.where((i == nt - 1) & (row[:te] >= tm), 0.0, dconv_e)
        dconv = dconv_e[:tm]
        dc1 = pltpu.roll(dconv_e, te - 1, 0)[:tm]
        dc2 = pltpu.roll(dconv_e, te - 2, 0)[:tm]
        dz = cw[2:3] * dconv + cw[1:2] * dc1 + cw[0:1] * dc2
        z = zext[HALO:HALO + tm]
        z1 = z1e[HALO:HALO + tm]
        z2 = z2e[HALO:HALO + tm]
        dcw_ref[0:1, :] += jnp.sum(dconv * z2, axis=0, keepdims=True)
        dcw_ref[1:2, :] += jnp.sum(dconv * z1, axis=0, keepdims=True)
        dcw_ref[2:3, :] += jnp.sum(dconv * z, axis=0, keepdims=True)
        dh_ref[:, in_a:in_a + bw] = (dsc_e[:tm] * conv_e[:tm]).astype(BF16)
        dh_ref[:, in_a + bw:in_a + 2 * bw] = (dz * hv[:, in_a + 2 * bw:]).astype(BF16)
        dh_ref[:, in_a + 2 * bw:] = (dz * hv[:, in_a + bw:in_a + 2 * bw]).astype(BF16)

        ha = hv[:, :in_a]
        th, u, xhat, rl, vln = _sgu_forward(ha, lng_ref[...], lnb_ref[...], wm, bt_ref[...], mixed_s)
        mixed = mixed_s[...]
        sg = u * mixed
        dsg, dga = _rms_bwd(dycat[:tm, :aw], sg, _rstd(sg), ga_ref[...])
        dga_ref[...] += dga
        du = dsg * mixed
        dmixed = dsg * u
        dmb = dmixed.astype(BF16)
        for n in range(tm // CHUNK):
            rows = slice(n * CHUNK, (n + 1) * CHUNK)
            dbs_ref[...] += dmixed[rows]
            for hh in range(HEADS):
                cols = slice(hh * hd, (hh + 1) * hd)
                dvln_s[rows, cols] = _dot_tn(wm[hh], dmb[rows, cols])
                dwsp_ref[hh] += mask * _dot_nt(dmb[rows, cols], vln[rows, cols])
        dvln = dvln_s[...]
        dlng_ref[...] += jnp.sum(dvln * xhat, axis=0, keepdims=True)
        dlnb_ref[...] += jnp.sum(dvln, axis=0, keepdims=True)
        dxh = dvln * lng_ref[...]
        dv = rl * (dxh - jnp.mean(dxh, axis=-1, keepdims=True) - xhat * jnp.mean(dxh * xhat, axis=-1, keepdims=True))
        dh_ref[:, :in_a] = (jnp.concatenate([du, dv], axis=-1) * _gelu_grad(ha, th)).astype(BF16)

        dxn = _dot(dh_ref[...], win_ref[...])
        xv = x_ref[...]
        dx, dgm = _rms_bwd(dxn, xv, _rstd(xv), gm_ref[...])
        dgm_ref[...] += dgm
        dx_ref[...] = dx1 + dx

    full = lambda shape: pl.BlockSpec(shape, lambda i: (0,) * len(shape))
    tok = pl.BlockSpec((tm, d), lambda i: (i, 0))
    nxt = lambda i: (jnp.minimum((i + 1) * hb_blocks, last_blk), 0)
    prv = lambda i: (jnp.maximum(i * hb_blocks - 1, 0), 0)
    return _hosted(
        body, carry, grid=(nt,),
        in_specs=[tok, pl.BlockSpec((HALO, d), nxt), tok, full((1, d)),
                  pl.BlockSpec((tm, n_in), lambda i: (i, 0)), pl.BlockSpec((HALO, n_in), prv), pl.BlockSpec((HALO, n_in), nxt),
                  full((1, aw)), full((1, aw)), full((HEADS, CHUNK, CHUNK)), full((CHUNK, HEADS)), full((3, bw)),
                  full((1, aw)), full((1, bw)), full((d, d)), full((n_in, d))],
        out_specs=[tok, pl.BlockSpec((tm, n_in), lambda i: (i, 0)), tok,
                   full((1, aw)), full((1, bw)), full((SUB, bw)), full((1, aw)), full((1, aw)),
                   full((HEADS, CHUNK, CHUNK)), full((CHUNK, aw)), full((1, d))],
        out_shape=(SDS((s, d), BF16), SDS((s, n_in), BF16), SDS((s, d), F32),
                   SDS((1, aw), F32), SDS((1, bw), F32), SDS((SUB, bw), F32), SDS((1, aw), F32), SDS((1, aw), F32),
                   SDS((HEADS, CHUNK, CHUNK), F32), SDS((CHUNK, aw), F32), SDS((1, d), F32)),
        scratch_shapes=[pltpu.VMEM((tm, aw), F32), pltpu.VMEM((tm, aw), F32)],
        compiler_params=_arb(1), name="mix_backward")(dx1, dx1, x, g_mix, h, h, h, lng, lnb, w_sp, bt, conv_w, ga, gb, w_out, w_in)


def _bias_grad(dbs):
    aw = dbs.shape[1]
    hd = aw // HEADS

    def body(dbs_ref, out_ref):
        ones = jnp.ones((SUB, hd), F32)
        for hh in range(HEADS):
            r = lax.dot_general(ones, dbs_ref[:, hh * hd:(hh + 1) * hd], (((1,), (1,)), ((), ())),
                                precision=lax.Precision.HIGHEST, preferred_element_type=F32)
            out_ref[hh:hh + 1, :] = r[0:1]

    return _pcall(body, out_shape=SDS((HEADS, CHUNK), F32), name="bias_grad")(dbs)


def _wgrad_body(a_ref, b_ref, o_ref):
    o_ref[...] = _dot_tn(a_ref[...], b_ref[...])


def _wgrad(a, b, name, carry=()):
    k, m = a.shape
    n = b.shape[1]
    tm = _tile(m, 512, LANES)
    tn = _tile(n, 1024, LANES)
    return _hosted(
        functools.partial(_wgrad_body), carry, grid=(m // tm, n // tn),
        in_specs=[pl.BlockSpec((k, tm), lambda i, j: (0, i)), pl.BlockSpec((k, tn), lambda i, j: (0, j))],
        out_specs=pl.BlockSpec((tm, tn), lambda i, j: (i, j)),
        out_shape=SDS((m, n), F32), compiler_params=_arb(2), name=name)(a, b)


def _wgrad_blocked_lhs(a, b, name, carry=()):
    nb, k, t = a.shape
    n = b.shape[1]
    tn = _tile(n, 1024, LANES)
    return _hosted(
        functools.partial(_wgrad_body), carry, grid=(nb, n // tn),
        in_specs=[pl.BlockSpec((None, k, t), lambda i, j: (i, 0, 0)), pl.BlockSpec((k, tn), lambda i, j: (0, j))],
        out_specs=pl.BlockSpec((t, tn), lambda i, j: (i, j)),
        out_shape=SDS((nb * t, n), F32), compiler_params=_arb(2), name=name)(a, b)


def _wgrad_blocked_rhs(a, b, name, carry=()):
    k, m = a.shape
    nb, _, t = b.shape
    tm = _tile(m, 512, LANES)
    return _hosted(
        functools.partial(_wgrad_body), carry, grid=(m // tm, nb),
        in_specs=[pl.BlockSpec((k, tm), lambda i, j: (0, i)), pl.BlockSpec((None, k, t), lambda i, j: (j, 0, 0))],
        out_specs=pl.BlockSpec((None, tm, t), lambda i, j: (j, i, 0)),
        out_shape=SDS((nb, m, t), F32), compiler_params=_arb(2), name=name)(a, b)


def _unblock_cols(wb, name, carry=()):
    nb, r, t = wb.shape
    tr = _tile(r, 256, 16)

    def body(w_ref, o_ref):
        o_ref[...] = jnp.concatenate([w_ref[j].astype(F32) for j in range(nb)], axis=-1).astype(o_ref.dtype)

    return _hosted(
        body, carry, grid=(r // tr,),
        in_specs=[pl.BlockSpec((nb, tr, t), lambda i: (0, i, 0))], out_specs=pl.BlockSpec((tr, nb * t), lambda i: (i, 0)),
        out_shape=SDS((r, nb * t), wb.dtype), compiler_params=_arb(1), name=name)(wb)


def _block_cols(w, nb, name, carry=()):
    r, n = w.shape
    t = n // nb
    tr = _tile(r, 256, 16)

    def body(w_ref, o_ref):
        wv = w_ref[...]
        for j in range(nb):
            o_ref[j] = wv[:, j * t:(j + 1) * t]

    return _hosted(
        body, carry, grid=(r // tr,),
        in_specs=[pl.BlockSpec((tr, n), lambda i: (i, 0))], out_specs=pl.BlockSpec((nb, tr, t), lambda i: (0, i, 0)),
        out_shape=SDS((nb, r, t), w.dtype), compiler_params=_arb(1), name=name)(w)


def _place():
    x, y, c = lax.axis_index("x"), lax.axis_index("y"), lax.axis_index("c")
    return x, y, c, [(1 - x, y), (x, 1 - y), (1 - x, 1 - y)]


def _all_gather(shards):
    n = len(shards)

    def copies(ins, outs, sems, incoming):
        send_sems, recv_sems, local_sems = sems
        x, y, c, chips = _place()
        me, sibling = (x, y, c), (x, y, 1 - c)

        def blk(a, px, py, pc):
            return outs[a].at[4 * px + 2 * py + pc]

        def copy(a, k, block, to, src=None):
            return pltpu.make_async_remote_copy(
                src_ref=blk(a, *block) if src is None else src, dst_ref=blk(a, *block),
                send_sem=send_sems.at[7 * a + k], recv_sem=recv_sems.at[7 * a + k], device_id=to, device_id_type=MESH)

        if not incoming:
            first = [copy(a, 1 + j, me, (*chip, c), src=ins[a]) for a in range(n) for j, chip in enumerate(chips)]
            first += [copy(a, 0, me, sibling, src=ins[a]) for a in range(n)]
            return first + [pltpu.make_async_copy(ins[a], blk(a, *me), local_sems.at[a]) for a in range(n)]
        landed = [copy(a, 1 + j, (*chip, c), me) for a in range(n) for j, chip in enumerate(chips)]
        passed = [copy(a, 4 + j, (*chip, c), sibling) for a in range(n) for j, chip in enumerate(chips)]
        from_sibling = [copy(a, 0, sibling, me) for a in range(n)]
        from_sibling += [copy(a, 4 + j, (*chip, 1 - c), me) for a in range(n) for j, chip in enumerate(chips)]
        return landed, passed, from_sibling

    def start(ins, outs, sems):
        for cp in copies(ins, outs, sems, False):
            cp.start()

    def relay(ins, outs, sems):
        landed, passed, _ = copies(ins, outs, sems, True)
        for arrived, onward in zip(landed, passed):
            arrived.wait_recv()
            onward.start()

    def finish(ins, outs, sems):
        _, passed, from_sibling = copies(ins, outs, sems, True)
        for cp in from_sibling:
            cp.wait_recv()
        sent = copies(ins, outs, sems, False)
        for cp in sent[:4 * n] + passed:
            cp.wait_send()
        for cp in sent[4 * n:]:
            cp.wait()

    return _Exchange(shards, [SDS((N_DEV,) + s.shape, s.dtype) for s in shards],
                     [pltpu.SemaphoreType.DMA((7 * n,)), pltpu.SemaphoreType.DMA((7 * n,)), pltpu.SemaphoreType.DMA((n,))],
                     start, finish, relay)


def _swap_exchange(ins, out_shape, per, copies):
    def start(i, o, sems):
        for cp in copies(i, o, sems):
            cp.start()

    def finish(i, o, sems):
        for cp in copies(i, o, sems):
            cp.wait()

    n = per * len(ins)
    return _Exchange(ins, out_shape, [pltpu.SemaphoreType.DMA((n,)), pltpu.SemaphoreType.DMA((n,))], start, finish)


def _exchange_c(gs):
    def copies(ins, outs, sems):
        x, y, c, _ = _place()
        return [pltpu.make_async_remote_copy(
                    src_ref=ins[a].at[2 * k + 1 - c], dst_ref=outs[a].at[k],
                    send_sem=sems[0].at[4 * a + k], recv_sem=sems[1].at[4 * a + k],
                    device_id=(x, y, 1 - c), device_id_type=MESH)
                for a in range(len(gs)) for k in range(4)]

    return _swap_exchange(gs, [SDS((4,) + g.shape[1:], g.dtype) for g in gs], 4, copies)


def _exchange_xy(sends):
    def copies(ins, outs, sems):
        x, y, c, chips = _place()
        return [pltpu.make_async_remote_copy(
                    src_ref=ins[a].at[t], dst_ref=outs[a].at[t],
                    send_sem=sems[0].at[3 * a + t], recv_sem=sems[1].at[3 * a + t],
                    device_id=(*chips[t], c), device_id_type=MESH)
                for a in range(len(sends)) for t in range(3)]

    return _swap_exchange(sends, [SDS(s.shape, s.dtype) for s in sends], 3, copies)


def _rs_combine(g, recv, pos, name, carry=()):
    _, r, cdim = g.shape
    tr = _tile(r, 256, 16)

    def body(pos_ref, g0, r0, g1, r1, g2, r2, g3, r3, keep_ref, send_ref):
        keep_ref[...] = g0[...] + r0[...]
        send_ref[0] = (g1[...] + r1[...]).astype(BF16)
        send_ref[1] = (g2[...] + r2[...]).astype(BF16)
        send_ref[2] = (g3[...] + r3[...]).astype(BF16)

    def k_of(p, t):
        px = p[0] if t in (0, 2) else 1 - p[0]
        py = p[1] if t in (0, 1) else 1 - p[1]
        return 2 * px + py

    blk = (None, tr, cdim)
    in_specs = []
    for t in range(4):
        in_specs.append(pl.BlockSpec(blk, functools.partial(lambda j, p, t: (2 * k_of(p, t) + p[2], j, 0), t=t)))
        in_specs.append(pl.BlockSpec(blk, functools.partial(lambda j, p, t: (k_of(p, t), j, 0), t=t)))
    return _hosted(
        body, carry, n_prefetch=1, out_shape=(SDS((r, cdim), F32), SDS((3, r, cdim), BF16)),
        grid=(r // tr,), in_specs=in_specs,
        out_specs=[pl.BlockSpec((tr, cdim), lambda j, p: (j, 0)), pl.BlockSpec((3, tr, cdim), lambda j, p: (0, j, 0))],
        compiler_params=_arb(1), name=name)(pos, g, recv, g, recv, g, recv, g, recv)


def _adamw_shard(keep, recv, w, m, v, name):
    r, cdim = w.shape
    tr = _tile(r, 256, 16)

    def body(k_ref, r_ref, w_ref, m_ref, v_ref, g_ref, d_ref, nm_ref, nv_ref):
        g = ((k_ref[...] + r_ref[0].astype(F32)) + r_ref[1].astype(F32)) + r_ref[2].astype(F32)
        g_ref[...] = g
        d_ref[...], nm_ref[...], nv_ref[...] = _adamw(w_ref[...], g, m_ref[...], v_ref[...])

    blk = pl.BlockSpec((tr, cdim), lambda j: (j, 0))
    out = SDS((r, cdim), F32)
    return _pcall(body, grid=(r // tr,), in_specs=[blk, pl.BlockSpec((3, tr, cdim), lambda j: (0, j, 0)), blk, blk, blk],
                  out_specs=[blk] * 4, out_shape=(out,) * 4, compiler_params=_arb(1), name=name)(keep, recv, w, m, v)


def _adamw_small(gathered, seg, params, conv_rows):
    names = list(params)
    c0, cn = conv_rows

    def body(*refs):
        gat_ref = refs[0]
        ins = refs[1:1 + 3 * len(names)]
        outs = refs[1 + 3 * len(names):]

        def total(r0, rn):
            tot = gat_ref[0, r0:r0 + rn, :]
            for dev in range(1, N_DEV):
                tot = tot + gat_ref[dev, r0:r0 + rn, :]
            return tot

        for k, nm in enumerate(names):
            g = total(*seg[nm])
            w_ref, m_ref, v_ref = ins[3 * k:3 * k + 3]
            g_ref, d_ref, nm_ref, nv_ref = outs[4 * k:4 * k + 4]
            g_ref[...] = g
            d_ref[...], nm_ref[...], nv_ref[...] = _adamw(w_ref[...], g, m_ref[...], v_ref[...])
        outs[-2][...] = total(c0, cn)
        outs[-1][...] = total(*seg["loss"])

    flat_in = [a for nm in names for a in params[nm]]
    out_shape = []
    for nm in names:
        out_shape += [SDS(params[nm][0].shape, F32)] * 4
    out_shape += [SDS((cn, LANES), F32), SDS((seg["loss"][1], LANES), F32)]
    res = _pcall(body, out_shape=tuple(out_shape), name="adamw_small")(gathered, *flat_in)
    per = {nm: res[4 * k:4 * k + 4] for k, nm in enumerate(names)}
    return per, res[-2], res[-1]


def _adamw_one(w, g, m, v, name):
    def body(w_ref, g_ref, m_ref, v_ref, d_ref, nm_ref, nv_ref):
        d_ref[...], nm_ref[...], nv_ref[...] = _adamw(w_ref[...], g_ref[...], m_ref[...], v_ref[...])

    return _pcall(body, out_shape=(SDS(w.shape, F32),) * 3, name=name)(w, g, m, v)


def _rows128(a):
    return a.reshape(-1, LANES)


def _pack_small(gs, loss_tile):
    seg, pieces, row = {}, [], 0
    for nm in SMALL + ("conv_w", "loss"):
        piece = loss_tile if nm == "loss" else _rows128(gs[nm])
        rn = _round_up(piece.shape[0], SUB)
        pieces.append(jnp.pad(piece, ((0, rn - piece.shape[0]), (0, 0))))
        seg[nm] = (row, piece.shape[0])
        row += rn
    return jnp.concatenate(pieces, axis=0), seg


def _step(x, mem, target, wb, conv_w, sp, pos):
    s, d = x.shape
    tm = min(TOKEN_TILE, s)
    tm_wide = min(2 * TOKEN_TILE, s)
    rows = lambda w8: w8.reshape(-1, w8.shape[2])
    shards = lambda g: g.reshape((N_DEV, -1) + g.shape[1:])
    bt = sp["b_spatial"].T

    (w_in8, conv8), = _run_exchanges([_all_gather([wb["w_in"], conv_w])], "gather_w_in")
    conv_full = conv8.transpose(1, 0, 2).reshape(3, -1)
    w_in_t = rows(w_in8)
    (xn1, h), ((w_out8, w_kv8, w_q8),) = _in_forward(
        x, sp["ln_mix_g"], w_in_t, tm_wide, carry=[_all_gather([wb["w_out"], wb["w_kv"], wb["w_q"]])])
    w_out = rows(w_out8)
    (ycat, x1), ((w_o8, w_down8),) = _mix_forward(
        h, x, sp["sgu_ln_g"], sp["sgu_ln_b"], sp["w_spatial"], bt, conv_full, sp["grp_norm_a"], sp["grp_norm_b"], w_out, tm,
        carry=[_all_gather([wb["w_o"], wb["w_down"]])])
    w_q, w_o, w_down = rows(w_q8), rows(w_o8), rows(w_down8)
    memn, kv = _kv_forward(mem, sp["ln_mem_g"], w_kv8)
    (xn2, q, o, x2), ((w_gu8,),) = _attn_forward(
        x1, sp["ln_attn_g"], w_q, kv, w_o, tm_wide, carry=[_all_gather([wb["w_gate_up"]])])
    w_gu = w_gu8.reshape((2, N_DEV // 2) + w_gu8.shape[1:])
    xn3, gu, x3 = _ffn_forward(x2, sp["ln_ffn_g"], w_gu, w_down, tm_wide)

    loss, d_lnf, d_lnffn, act, dgu, dx3b, dx2 = _ffn_backward(
        x3, target, sp["ln_final_g"], x2, sp["ln_ffn_g"], gu, w_gu, w_down, tm)
    part = {}
    g_gu, _ = _wgrad_blocked_lhs(dgu.reshape((N_DEV,) + dgu.shape[2:]), xn3, "wgrad_gate_up")
    g_gu = shards(g_gu)
    g_down, ((rc_gu,),) = _wgrad_blocked_lhs(act, dx3b, "wgrad_down", carry=[_exchange_c([g_gu])])
    g_down = shards(g_down)
    (keep_gu, send_gu), _ = _rs_combine(g_gu, rc_gu, pos, "rs_combine_w_gate_up")
    (dx2b, dq, dx1, dkv, d_lnattn), ((rxy_gu,), (rc_down,)) = _attn_backward(
        dx2, x1, sp["ln_attn_g"], q, kv, w_q, w_o, tm, carry=[_exchange_xy([send_gu]), _exchange_c([g_down])])
    part["w_gate_up"] = (keep_gu, rxy_gu)
    (keep_down, send_down), _ = _rs_combine(g_down, rc_down, pos, "rs_combine_w_down")
    g_o, _ = _wgrad(o, dx2b, "wgrad_o")
    g_o = shards(g_o)
    g_q, ((rc_o,),) = _wgrad(xn2, dq, "wgrad_q", carry=[_exchange_c([g_o])])
    g_q = shards(g_q)
    g_kv, d_lnmem = _kv_backward(dkv, memn, mem, sp["ln_mem_g"], w_kv8)
    (keep_o, send_o), ((rc_q,),) = _rs_combine(g_o, rc_o, pos, "rs_combine_w_o", carry=[_exchange_c([g_q])])
    (keep_q, send_q), _ = _rs_combine(g_q, rc_q, pos, "rs_combine_w_q")
    ((dx1b, dh, dx, d_ga, d_gb, d_cw, d_lng, d_lnb, d_wsp, d_bs, d_lnmix),
     ((rxy_down, rxy_o, rxy_q), (rc_kv,))) = _mix_backward(
        dx1, x, sp["ln_mix_g"], h, sp["sgu_ln_g"], sp["sgu_ln_b"], sp["w_spatial"], bt, conv_full,
        sp["grp_norm_a"], sp["grp_norm_b"], w_out, w_in_t, tm,
        carry=[_exchange_xy([send_down, send_o, send_q]), _exchange_c([g_kv])])
    part["w_down"], part["w_o"], part["w_q"] = (keep_down, rxy_down), (keep_o, rxy_o), (keep_q, rxy_q)
    (keep_kv, send_kv), _ = _rs_combine(g_kv, rc_kv, pos, "rs_combine_w_kv")
    gs = {"ln_mix_g": d_lnmix, "sgu_ln_g": d_lng, "sgu_ln_b": d_lnb, "w_spatial": d_wsp, "b_spatial": _bias_grad(d_bs),
          "conv_w": d_cw[:3], "grp_norm_a": d_ga, "grp_norm_b": d_gb, "ln_attn_g": d_lnattn, "ln_mem_g": d_lnmem,
          "ln_ffn_g": d_lnffn, "ln_final_g": d_lnf}
    packed, seg = _pack_small(gs, loss)
    g_in, ((rxy_kv,), (small_all,)) = _wgrad(
        dh, xn1, "wgrad_in", carry=[_exchange_xy([send_kv]), _all_gather([packed])])
    g_in = shards(g_in)
    part["w_kv"] = (keep_kv, rxy_kv)
    g_out, ((rc_in,),) = _wgrad(ycat, dx1b, "wgrad_out", carry=[_exchange_c([g_in])])
    g_out = shards(g_out)
    (keep_in, send_in), ((rc_out,),) = _rs_combine(g_in, rc_in, pos, "rs_combine_w_in", carry=[_exchange_c([g_out])])
    (keep_out, send_out), _ = _rs_combine(g_out, rc_out, pos, "rs_combine_w_out")
    (rxy_in, rxy_out), = _run_exchanges([_exchange_xy([send_in, send_out])], "exchange_xy_w_in_w_out")
    part["w_in"], part["w_out"] = (keep_in, rxy_in), (keep_out, rxy_out)
    return dx, part, small_all, seg


def kernel(x, mem, ln_mix_g, w_in, sgu_ln_g, sgu_ln_b, w_spatial, b_spatial, conv_w, grp_norm_a, grp_norm_b, w_out, ln_attn_g, ln_mem_g, w_q, w_kv, w_o, ln_ffn_g, w_gate_up, w_down, ln_final_g, loss_target, m_ln_mix_g, m_w_in, m_sgu_ln_g, m_sgu_ln_b, m_w_spatial, m_b_spatial, m_conv_w, m_grp_norm_a, m_grp_norm_b, m_w_out, m_ln_attn_g, m_ln_mem_g, m_w_q, m_w_kv, m_w_o, m_ln_ffn_g, m_w_gate_up, m_w_down, m_ln_final_g, v_ln_mix_g, v_w_in, v_sgu_ln_g, v_sgu_ln_b, v_w_spatial, v_b_spatial, v_conv_w, v_grp_norm_a, v_grp_norm_b, v_w_out, v_ln_attn_g, v_ln_mem_g, v_w_q, v_w_kv, v_w_o, v_ln_ffn_g, v_w_gate_up, v_w_down, v_ln_final_g):
    order = ["ln_mix_g", "w_in", "sgu_ln_g", "sgu_ln_b", "w_spatial", "b_spatial", "conv_w", "grp_norm_a", "grp_norm_b",
             "w_out", "ln_attn_g", "ln_mem_g", "w_q", "w_kv", "w_o", "ln_ffn_g", "w_gate_up", "w_down", "ln_final_g"]
    W = dict(ln_mix_g=ln_mix_g, w_in=w_in, sgu_ln_g=sgu_ln_g, sgu_ln_b=sgu_ln_b, w_spatial=w_spatial, b_spatial=b_spatial,
             conv_w=conv_w, grp_norm_a=grp_norm_a, grp_norm_b=grp_norm_b, w_out=w_out, ln_attn_g=ln_attn_g,
             ln_mem_g=ln_mem_g, w_q=w_q, w_kv=w_kv, w_o=w_o, ln_ffn_g=ln_ffn_g, w_gate_up=w_gate_up, w_down=w_down,
             ln_final_g=ln_final_g)
    M = dict(ln_mix_g=m_ln_mix_g, w_in=m_w_in, sgu_ln_g=m_sgu_ln_g, sgu_ln_b=m_sgu_ln_b, w_spatial=m_w_spatial,
             b_spatial=m_b_spatial, conv_w=m_conv_w, grp_norm_a=m_grp_norm_a, grp_norm_b=m_grp_norm_b, w_out=m_w_out,
             ln_attn_g=m_ln_attn_g, ln_mem_g=m_ln_mem_g, w_q=m_w_q, w_kv=m_w_kv, w_o=m_w_o, ln_ffn_g=m_ln_ffn_g,
             w_gate_up=m_w_gate_up, w_down=m_w_down, ln_final_g=m_ln_final_g)
    V = dict(ln_mix_g=v_ln_mix_g, w_in=v_w_in, sgu_ln_g=v_sgu_ln_g, sgu_ln_b=v_sgu_ln_b, w_spatial=v_w_spatial,
             b_spatial=v_b_spatial, conv_w=v_conv_w, grp_norm_a=v_grp_norm_a, grp_norm_b=v_grp_norm_b, w_out=v_w_out,
             ln_attn_g=v_ln_attn_g, ln_mem_g=v_ln_mem_g, w_q=v_w_q, w_kv=v_w_kv, w_o=v_w_o, ln_ffn_g=v_ln_ffn_g,
             w_gate_up=v_w_gate_up, w_down=v_w_down, ln_final_g=v_ln_final_g)

    bw = conv_w.shape[1] * N_DEV
    pos = jnp.stack([lax.axis_index("x"), lax.axis_index("y"), lax.axis_index("c")]).astype(jnp.int32)
    me = 4 * pos[0] + 2 * pos[1] + pos[2]

    sp = {nm: (W[nm].reshape(1, -1) if W[nm].ndim == 1 else W[nm]) for nm in SMALL}
    view = lambda a, nm: a.T if nm in TRANSPOSED else a
    wb = {nm: view(W[nm], nm).astype(BF16) for nm in BIG}
    grad_x, part, small_all, seg = _step(x[0], mem[0], loss_target[0], wb, conv_w, sp, pos)

    out = {}
    for nm in BIG:
        res = _adamw_shard(part[nm][0], part[nm][1], view(W[nm], nm), view(M[nm], nm), view(V[nm], nm), "adamw_" + nm)
        out[nm] = tuple(view(a, nm) for a in res)

    params = {nm: (_rows128(W[nm]), _rows128(M[nm]), _rows128(V[nm])) for nm in SMALL}
    per, conv_g_rows, loss_sum = _adamw_small(small_all, seg, params, seg["conv_w"])
    for nm in SMALL:
        out[nm] = tuple(a.reshape(W[nm].shape) for a in per[nm])
    conv_g = lax.dynamic_slice_in_dim(conv_g_rows.reshape(3, bw), me * conv_w.shape[1], conv_w.shape[1], axis=1)
    out["conv_w"] = (conv_g,) + tuple(_adamw_one(conv_w, conv_g, m_conv_w, v_conv_w, "adamw_conv"))

    loss = loss_sum[0, 0]
    res = [loss, grad_x[None]]
    for k in range(4):
        res += [out[nm][k] for nm in order]
    return tuple(res)
```

```python
import functools

import jax
import jax.numpy as jnp
from jax import lax
from jax.experimental import pallas as pl
from jax.experimental.pallas import tpu as pltpu

F32 = jnp.float32
BF16 = jnp.bfloat16
SDS = jax.ShapeDtypeStruct
MESH = pl.DeviceIdType.MESH

EPS = 1e-6
N_DEV = 8
HEADS = 4
CHUNK = 128
HALO = 16
SUB = 8
LANES = 128
TOKEN_TILE = 512

ADAM_LR = 0.001
ADAM_B1 = 0.9
ADAM_B2 = 0.999
ADAM_EPS = 1e-08
ADAM_WD = 0.01
ADAM_STEP = 10

BIG = ("w_in", "w_out", "w_q", "w_kv", "w_o", "w_gate_up", "w_down")
TRANSPOSED = ("w_in", "w_gate_up")
SMALL = ("ln_mix_g", "sgu_ln_g", "sgu_ln_b", "w_spatial", "b_spatial", "grp_norm_a", "grp_norm_b",
         "ln_attn_g", "ln_mem_g", "ln_ffn_g", "ln_final_g")


class _Exchange:
    def __init__(self, ins, out_shape, sems, start, finish, relay=None):
        self.ins, self.out_shape, self.sems = list(ins), list(out_shape), list(sems)
        self.start, self.finish, self.relay = start, finish, relay


def _pcall(body, carry=(), n_prefetch=0, **kw):
    if carry:
        return functools.partial(_carrying_call, body, tuple(carry), n_prefetch, kw)
    if n_prefetch:
        kw["grid_spec"] = pltpu.PrefetchScalarGridSpec(
            num_scalar_prefetch=n_prefetch, grid=kw.pop("grid"), in_specs=kw.pop("in_specs"),
            out_specs=kw.pop("out_specs"), scratch_shapes=kw.pop("scratch_shapes", ()))
    return pl.pallas_call(body, **kw)


def _carrying_call(body, carry, n_prefetch, kw, *args):
    kw = dict(kw)
    out_shape = kw.pop("out_shape")
    single = not isinstance(out_shape, (tuple, list))
    outs_shape = (out_shape,) if single else tuple(out_shape)
    out_specs = kw.pop("out_specs")
    out_specs = [out_specs] if single else list(out_specs)
    in_specs = list(kw.pop("in_specs"))
    scratch = list(kw.pop("scratch_shapes", ()))
    grid = tuple(kw.get("grid", ()))
    n_in, n_out, n_scr = len(args), len(outs_shape), len(scratch)

    def split(refs, k, counts):
        parts = []
        for cnt in counts:
            parts.append(refs[k:k + cnt])
            k += cnt
        return parts, k

    def wrapped(*refs):
        cins, k = split(refs, n_in, [len(p.ins) for p in carry])
        outs = refs[k:k + n_out]
        couts, k = split(refs, k + n_out, [len(p.out_shape) for p in carry])
        scr = refs[k:k + n_scr]
        csems, _ = split(refs, k + n_scr, [len(p.sems) for p in carry])
        first, last = True, True
        for a, g in enumerate(grid):
            first = (pl.program_id(a) == 0) & first
            last = (pl.program_id(a) == g - 1) & last

        def start_all():
            for p, ci, co, cs in zip(carry, cins, couts, csems):
                p.start(ci, co, cs)

        def relay_all():
            for p, ci, co, cs in zip(carry, cins, couts, csems):
                if p.relay is not None:
                    p.relay(ci, co, cs)

        def finish_all():
            for p, ci, co, cs in zip(carry, cins, couts, csems):
                p.finish(ci, co, cs)

        start_all() if not grid else pl.when(first)(start_all)
        relay_all() if not grid else pl.when(last)(relay_all)
        body(*refs[:n_in], *outs, *scr)
        finish_all() if not grid else pl.when(last)(finish_all)

    c_in = [a for p in carry for a in p.ins]
    c_out = [s for p in carry for s in p.out_shape]
    c_sems = [s for p in carry for s in p.sems]
    res = _pcall(wrapped, n_prefetch=n_prefetch, out_shape=outs_shape + tuple(c_out),
                 in_specs=in_specs + _hbm_specs(len(c_in)), out_specs=out_specs + _hbm_specs(len(c_out)),
                 scratch_shapes=scratch + c_sems, **kw)(*args, *c_in)
    own = res[0] if single else tuple(res[:n_out])
    landed, k = [], n_out
    for p in carry:
        landed.append(list(res[k:k + len(p.out_shape)]))
        k += len(p.out_shape)
    return own, landed


def _hbm_specs(n):
    return [pl.BlockSpec(memory_space=pl.ANY)] * n


def _hosted(body, carry, **kw):
    if carry:
        return _pcall(body, carry=carry, **kw)
    call = _pcall(body, **kw)
    return lambda *args: (call(*args), [])


def _run_exchanges(parts, name):
    def body(*refs):
        pass

    _, landed = _pcall(body, carry=parts, out_shape=(), in_specs=[], out_specs=[], name=name)()
    return landed


def _arb(n):
    return pltpu.CompilerParams(dimension_semantics=("arbitrary",) * n)


def _tile(n, target, mult):
    best = None
    for t in range(mult, min(n, target) + 1, mult):
        if n % t == 0:
            best = t
    return n if best is None else best


def _round_up(n, m):
    return (n + m - 1) // m * m


def _dot(a, b):
    return jnp.dot(a, b, preferred_element_type=F32)


def _dot_nt(a, b):
    return lax.dot_general(a, b, (((1,), (1,)), ((), ())), preferred_element_type=F32)


def _dot_tn(a, b):
    return lax.dot_general(a, b, (((0,), (0,)), ((), ())), preferred_element_type=F32)


def _rstd(x):
    return lax.rsqrt(jnp.mean(x * x, axis=-1, keepdims=True) + EPS)


def _rms_bwd(dy, x, r, g):
    gdy = dy * g
    proj = jnp.sum(gdy * x, axis=-1, keepdims=True) * (1.0 / x.shape[-1])
    dx = r * gdy - x * (r * r * r) * proj
    dg = jnp.sum(dy * (x * r), axis=0, keepdims=True)
    return dx, dg


_GELU_C = 0.7978845608028654
_GELU_A = 0.044715


def _gelu(x):
    t = jnp.tanh(_GELU_C * (x + _GELU_A * x * x * x))
    return 0.5 * x * (1.0 + t), t


def _gelu_grad(x, t):
    return 0.5 * (1.0 + t) + 0.5 * x * (1.0 - t * t) * (_GELU_C * (1.0 + 3.0 * _GELU_A * x * x))


def _sigmoid(x):
    return 1.0 / (1.0 + jnp.exp(-x))


def _softmax(s):
    m = jnp.max(s, axis=-1, keepdims=True)
    e = jnp.exp(s - m)
    return e / jnp.sum(e, axis=-1, keepdims=True)


def _adamw(w, g, m, v):
    m = ADAM_B1 * m + (1.0 - ADAM_B1) * g
    v = ADAM_B2 * v + (1.0 - ADAM_B2) * (g * g)
    m_hat = m / (1.0 - ADAM_B1 ** ADAM_STEP)
    v_hat = v / (1.0 - ADAM_B2 ** ADAM_STEP)
    delta = -ADAM_LR * (m_hat / (jnp.sqrt(v_hat) + ADAM_EPS) + ADAM_WD * w)
    return delta, m, v


def _tril_mask():
    t = lax.broadcasted_iota(jnp.int32, (CHUNK, CHUNK), 0)
    s = lax.broadcasted_iota(jnp.int32, (CHUNK, CHUNK), 1)
    return (s <= t).astype(F32)


def _sgu_forward(ha, lng, lnb, wm, bt, mixed_s):
    aw = ha.shape[1] // 2
    hd = aw // HEADS
    a, th = _gelu(ha)
    u = a[:, :aw]
    v = a[:, aw:]
    mu = jnp.mean(v, axis=-1, keepdims=True)
    vc = v - mu
    rl = lax.rsqrt(jnp.mean(vc * vc, axis=-1, keepdims=True) + EPS)
    xhat = vc * rl
    vln = (xhat * lng + lnb).astype(BF16)
    for n in range(ha.shape[0] // CHUNK):
        rows = slice(n * CHUNK, (n + 1) * CHUNK)
        for h in range(HEADS):
            cols = slice(h * hd, (h + 1) * hd)
            mixed_s[rows, cols] = _dot(wm[h], vln[rows, cols]) + bt[:, h:h + 1]
    return th, u, xhat, rl, vln


def _conv_taps(zext):
    return pltpu.roll(zext, 2, 0), pltpu.roll(zext, 1, 0)


def _kv_forward(mem, g_mem, w_kv):
    ml, d = mem.shape
    xd = w_kv.shape[2]

    def body(mem_ref, g_ref, w_ref, memn_ref, kv_ref):
        x = mem_ref[...]
        memn = (x * _rstd(x) * g_ref[...]).astype(BF16)
        memn_ref[...] = memn
        for j in range(2 * HEADS):
            kv_ref[j] = _dot(memn, w_ref[j]).astype(BF16)

    return _pcall(body, out_shape=(SDS((ml, d), BF16), SDS((2 * HEADS, ml, xd), BF16)), name="kv_forward")(mem, g_mem, w_kv)


def _in_forward(x, g, w_in_t, tm, carry=()):
    s, d = x.shape
    n_in = w_in_t.shape[0]

    def body(x_ref, g_ref, w_ref, xn_ref, h_ref):
        xv = x_ref[...]
        xn = (xv * _rstd(xv) * g_ref[...]).astype(BF16)
        xn_ref[...] = xn
        h_ref[...] = _dot_nt(xn, w_ref[...])

    return _hosted(
        body, carry, grid=(s // tm,),
        in_specs=[pl.BlockSpec((tm, d), lambda i: (i, 0)), pl.BlockSpec((1, d), lambda i: (0, 0)),
                  pl.BlockSpec((n_in, d), lambda i: (0, 0))],
        out_specs=[pl.BlockSpec((tm, d), lambda i: (i, 0)), pl.BlockSpec((tm, n_in), lambda i: (i, 0))],
        out_shape=(SDS((s, d), BF16), SDS((s, n_in), F32)),
        compiler_params=_arb(1), name="in_forward")(x, g, w_in_t)


def _mix_forward(h, x, lng, lnb, w_sp, bt, conv_w, ga, gb, w_out, tm, carry=()):
    s, d = x.shape
    n_in = h.shape[1]
    aw = lng.shape[1]
    bw = d - aw
    in_a = 2 * aw
    hb_blocks = tm // HALO

    def body(h_ref, hprev_ref, x_ref, lng_ref, lnb_ref, wsp_ref, bt_ref, cw_ref, ga_ref, gb_ref, wout_ref,
             ycat_ref, x1_ref, mixed_s):
        i = pl.program_id(0)
        mask = _tril_mask()
        wm = [(wsp_ref[hh] * mask).astype(BF16) for hh in range(HEADS)]
        hv = h_ref[...]
        _, u, _, _, _ = _sgu_forward(hv[:, :in_a], lng_ref[...], lnb_ref[...], wm, bt_ref[...], mixed_s)
        sg = u * mixed_s[...]
        ycat_ref[:, :aw] = (sg * _rstd(sg) * ga_ref[...]).astype(BF16)

        gate_b = hv[:, in_a:in_a + bw]
        z = hv[:, in_a + bw:in_a + 2 * bw] * hv[:, in_a + 2 * bw:]
        hp = hprev_ref[...]
        zp = hp[:, in_a + bw:in_a + 2 * bw] * hp[:, in_a + 2 * bw:]
        zp = jnp.where(i == 0, 0.0, zp)
        zext = jnp.concatenate([zp, z], axis=0)
        z2, z1 = _conv_taps(zext)
        cw = cw_ref[...]
        conv = cw[0:1] * z2[HALO:] + cw[1:2] * z1[HALO:] + cw[2:3] * z
        sc = gate_b * conv
        ycat_ref[:, aw:] = (sc * _rstd(sc) * gb_ref[...]).astype(BF16)
        x1_ref[...] = x_ref[...] + _dot(ycat_ref[...], wout_ref[...])

    full = lambda shape: pl.BlockSpec(shape, lambda i: (0,) * len(shape))
    return _hosted(
        body, carry, grid=(s // tm,),
        in_specs=[pl.BlockSpec((tm, n_in), lambda i: (i, 0)),
                  pl.BlockSpec((HALO, n_in), lambda i: (jnp.maximum(i * hb_blocks - 1, 0), 0)),
                  pl.BlockSpec((tm, d), lambda i: (i, 0)),
                  full((1, aw)), full((1, aw)), full((HEADS, CHUNK, CHUNK)), full((CHUNK, HEADS)),
                  full((3, bw)), full((1, aw)), full((1, bw)), full((d, d))],
        out_specs=[pl.BlockSpec((tm, d), lambda i: (i, 0)), pl.BlockSpec((tm, d), lambda i: (i, 0))],
        out_shape=(SDS((s, d), BF16), SDS((s, d), F32)),
        scratch_shapes=[pltpu.VMEM((tm, aw), F32)],
        compiler_params=_arb(1), name="mix_forward")(h, h, x, lng, lnb, w_sp, bt, conv_w, ga, gb, w_out)


def _attn_forward(x1, g, w_q, kv, w_o, tm, carry=()):
    s, d = x1.shape
    _, ml, xd = kv.shape
    scale = xd ** -0.5

    def body(x1_ref, g_ref, wq_ref, kv_ref, wo_ref, xn_ref, q_ref, o_ref, x2_ref):
        xv = x1_ref[...]
        xn = (xv * _rstd(xv) * g_ref[...]).astype(BF16)
        xn_ref[...] = xn
        q_ref[...] = _dot(xn, wq_ref[...]).astype(BF16)
        for hh in range(HEADS):
            cols = slice(hh * xd, (hh + 1) * xd)
            p = _softmax(_dot_nt(q_ref[:, cols], kv_ref[hh]) * scale)
            o_ref[:, cols] = _dot(p.astype(BF16), kv_ref[HEADS + hh]).astype(BF16)
        x2_ref[...] = xv + _dot(o_ref[...], wo_ref[...])

    tok = pl.BlockSpec((tm, d), lambda i: (i, 0))
    return _hosted(
        body, carry, grid=(s // tm,),
        in_specs=[tok, pl.BlockSpec((1, d), lambda i: (0, 0)), pl.BlockSpec((d, d), lambda i: (0, 0)),
                  pl.BlockSpec((2 * HEADS, ml, xd), lambda i: (0, 0, 0)), pl.BlockSpec((d, d), lambda i: (0, 0))],
        out_specs=[tok, tok, tok, tok],
        out_shape=(SDS((s, d), BF16), SDS((s, d), BF16), SDS((s, d), BF16), SDS((s, d), F32)),
        compiler_params=_arb(1), name="attn_forward")(x1, g, w_q, kv, w_o)


def _ffn_forward(x2, g, w_gu, w_down, tm):
    s, d = x2.shape
    _, nf, tf, _ = w_gu.shape

    def body(x2_ref, g_ref, wgu_ref, wd_ref, xn_ref, gu_ref, x3_ref):
        f = pl.program_id(1)

        @pl.when(f == 0)
        def _():
            xv = x2_ref[...]
            xn_ref[...] = (xv * _rstd(xv) * g_ref[...]).astype(BF16)
            x3_ref[...] = xv

        xn = xn_ref[...]
        gate = _dot_nt(xn, wgu_ref[0])
        up = _dot_nt(xn, wgu_ref[1])
        gu_ref[0] = gate.astype(BF16)
        gu_ref[1] = up.astype(BF16)
        act = (gate * _sigmoid(gate) * up).astype(BF16)
        x3_ref[...] += _dot(act, wd_ref[...])

    tok = pl.BlockSpec((tm, d), lambda i, f: (i, 0))
    return _pcall(
        body, grid=(s // tm, nf),
        in_specs=[tok, pl.BlockSpec((1, d), lambda i, f: (0, 0)),
                  pl.BlockSpec((2, None, tf, d), lambda i, f: (0, f, 0, 0)),
                  pl.BlockSpec((tf, d), lambda i, f: (f, 0))],
        out_specs=[tok, pl.BlockSpec((2, None, tm, tf), lambda i, f: (0, f, i, 0)), tok],
        out_shape=(SDS((s, d), BF16), SDS((2, nf, s, tf), BF16), SDS((s, d), F32)),
        compiler_params=_arb(2), name="ffn_forward")(x2, g, w_gu, w_down)


def _ffn_backward(x3, target, g_final, x2, g_ffn, gu, w_gu, w_down, tm):
    s, d = x3.shape
    _, nf, tf, _ = w_gu.shape

    def body(x3_ref, tgt_ref, gf_ref, x2_ref, g2_ref, gu_ref, wgu_ref, wd_ref,
             loss_ref, dgf_ref, dg2_ref, act_ref, dgu_ref, dx3b_ref, dx2_ref, dx3_s, acc_s):
        i = pl.program_id(0)
        f = pl.program_id(1)

        @pl.when((i == 0) & (f == 0))
        def _():
            loss_ref[...] = jnp.zeros_like(loss_ref)
            dgf_ref[...] = jnp.zeros_like(dgf_ref)
            dg2_ref[...] = jnp.zeros_like(dg2_ref)

        @pl.when(f == 0)
        def _():
            xv = x3_ref[...]
            r = _rstd(xv)
            diff = xv * r * gf_ref[...] - tgt_ref[...]
            loss_ref[...] += 0.5 * jnp.sum(jnp.sum(diff * diff, axis=-1, keepdims=True), axis=0, keepdims=True) * (1.0 / d)
            dx3, dgf = _rms_bwd(diff * (1.0 / d), xv, r, gf_ref[...])
            dgf_ref[...] += dgf
            dx3_s[...] = dx3
            dx3b_ref[...] = dx3.astype(BF16)
            acc_s[...] = jnp.zeros_like(acc_s)

        dact = _dot_nt(dx3b_ref[...], wd_ref[...])
        gv = gu_ref[0].astype(F32)
        uv = gu_ref[1].astype(F32)
        sg = _sigmoid(gv)
        silu = gv * sg
        act_ref[...] = (silu * uv).astype(BF16)
        dgate = (dact * uv * (sg * (1.0 + gv * (1.0 - sg)))).astype(BF16)
        dup = (dact * silu).astype(BF16)
        dgu_ref[0] = dgate
        dgu_ref[1] = dup
        acc_s[...] += _dot(dgate, wgu_ref[0]) + _dot(dup, wgu_ref[1])

        @pl.when(f == nf - 1)
        def _():
            xv = x2_ref[...]
            dxn, dg2 = _rms_bwd(acc_s[...], xv, _rstd(xv), g2_ref[...])
            dg2_ref[...] += dg2
            dx2_ref[...] = dx3_s[...] + dxn

    tok = pl.BlockSpec((tm, d), lambda i, f: (i, 0))
    pair = pl.BlockSpec((2, None, tm, tf), lambda i, f: (0, f, i, 0))
    vec = pl.BlockSpec((1, d), lambda i, f: (0, 0))
    return _pcall(
        body, grid=(s // tm, nf),
        in_specs=[tok, tok, vec, tok, vec, pair,
                  pl.BlockSpec((2, None, tf, d), lambda i, f: (0, f, 0, 0)),
                  pl.BlockSpec((tf, d), lambda i, f: (f, 0))],
        out_specs=[pl.BlockSpec((SUB, LANES), lambda i, f: (0, 0)), vec, vec,
                   pl.BlockSpec((None, tm, tf), lambda i, f: (f, i, 0)), pair, tok, tok],
        out_shape=(SDS((SUB, LANES), F32), SDS((1, d), F32), SDS((1, d), F32), SDS((nf, s, tf), BF16),
                   SDS((2, nf, s, tf), BF16), SDS((s, d), BF16), SDS((s, d), F32)),
        scratch_shapes=[pltpu.VMEM((tm, d), F32), pltpu.VMEM((tm, d), F32)],
        compiler_params=_arb(2), name="ffn_backward")(x3, target, g_final, x2, g_ffn, gu, w_gu, w_down)


def _attn_backward(dx2, x1, g, q, kv, w_q, w_o, tm, carry=()):
    s, d = x1.shape
    _, ml, xd = kv.shape
    scale = xd ** -0.5

    def body(dx2_ref, x1_ref, g_ref, q_ref, kv_ref, wq_ref, wo_ref,
             dx2b_ref, dq_ref, dx1_ref, dkv_ref, dg_ref, do_s):
        i = pl.program_id(0)

        @pl.when(i == 0)
        def _():
            dkv_ref[...] = jnp.zeros_like(dkv_ref)
            dg_ref[...] = jnp.zeros_like(dg_ref)

        dx2 = dx2_ref[...]
        dx2b_ref[...] = dx2.astype(BF16)
        do_s[...] = _dot_nt(dx2b_ref[...], wo_ref[...]).astype(BF16)
        for hh in range(HEADS):
            kc = slice(hh * xd, (hh + 1) * xd)
            qh = q_ref[:, kc]
            kh = kv_ref[hh]
            doh = do_s[:, kc]
            p = _softmax(_dot_nt(qh, kh) * scale)
            dp = _dot_nt(doh, kv_ref[HEADS + hh])
            dkv_ref[HEADS + hh] += _dot_tn(p.astype(BF16), doh)
            ds = (p * (dp - jnp.sum(dp * p, axis=-1, keepdims=True)) * scale).astype(BF16)
            dq_ref[:, kc] = _dot(ds, kh).astype(BF16)
            dkv_ref[hh] += _dot_tn(ds, qh)
        dxn = _dot_nt(dq_ref[...], wq_ref[...])
        xv = x1_ref[...]
        dx, dg = _rms_bwd(dxn, xv, _rstd(xv), g_ref[...])
        dg_ref[...] += dg
        dx1_ref[...] = dx2 + dx

    tok = pl.BlockSpec((tm, d), lambda i: (i, 0))
    vec = pl.BlockSpec((1, d), lambda i: (0, 0))
    sq = pl.BlockSpec((d, d), lambda i: (0, 0))
    kvs = pl.BlockSpec((2 * HEADS, ml, xd), lambda i: (0, 0, 0))
    return _hosted(
        body, carry, grid=(s // tm,),
        in_specs=[tok, tok, vec, tok, kvs, sq, sq],
        out_specs=[tok, tok, tok, kvs, vec],
        out_shape=(SDS((s, d), BF16), SDS((s, d), BF16), SDS((s, d), F32), SDS((2 * HEADS, ml, xd), F32), SDS((1, d), F32)),
        scratch_shapes=[pltpu.VMEM((tm, d), BF16)],
        compiler_params=_arb(1), name="attn_backward")(dx2, x1, g, q, kv, w_q, w_o)


def _kv_backward(dkv, memn, mem, g_mem, w_kv):
    ml, d = mem.shape
    xd = w_kv.shape[2]

    def body(dkv_ref, memn_ref, mem_ref, g_ref, w_ref, dw_ref, dg_ref):
        dmemn = jnp.zeros((ml, d), F32)
        for j in range(2 * HEADS):
            dkvb = dkv_ref[j].astype(BF16)
            dw_ref[j] = _dot_tn(memn_ref[...], dkvb)
            dmemn = dmemn + _dot_nt(dkvb, w_ref[j])
        x = mem_ref[...]
        dg_ref[...] = jnp.sum(dmemn * (x * _rstd(x)), axis=0, keepdims=True)

    return _pcall(body, out_shape=(SDS((2 * HEADS, d, xd), F32), SDS((1, d), F32)), name="kv_backward")(dkv, memn, mem, g_mem, w_kv)


def _mix_backward(dx1, x, g_mix, h, lng, lnb, w_sp, bt, conv_w, ga, gb, w_out, w_in, tm, carry=()):
    s, d = x.shape
    n_in = h.shape[1]
    aw = lng.shape[1]
    bw = d - aw
    hd = aw // HEADS
    in_a = 2 * aw
    hb_blocks = tm // HALO
    last_blk = s // HALO - 1
    nt = s // tm
    te = tm + HALO
    tee = tm + 2 * HALO

    def body(dx1_ref, dx1n_ref, x_ref, gm_ref, h_ref, hp_ref, hn_ref, lng_ref, lnb_ref, wsp_ref, bt_ref, cw_ref,
             ga_ref, gb_ref, wout_ref, win_ref,
             dx1b_ref, dh_ref, dx_ref, dga_ref, dgb_ref, dcw_ref, dlng_ref, dlnb_ref, dwsp_ref, dbs_ref, dgm_ref,
             mixed_s, dvln_s):
        i = pl.program_id(0)

        @pl.when(i == 0)
        def _():
            for ref in (dga_ref, dgb_ref, dcw_ref, dlng_ref, dlnb_ref, dwsp_ref, dbs_ref, dgm_ref):
                ref[...] = jnp.zeros_like(ref)

        mask = _tril_mask()
        wm = [(wsp_ref[hh] * mask).astype(BF16) for hh in range(HEADS)]
        hv = h_ref[...]
        dx1 = dx1_ref[...]
        dx1b_ref[...] = dx1.astype(BF16)
        dx1e = jnp.concatenate([dx1, dx1n_ref[...]], axis=0).astype(BF16)
        dycat = _dot_nt(dx1e, wout_ref[...])

        hbe = jnp.concatenate([hp_ref[:, in_a:], hv[:, in_a:], hn_ref[:, in_a:]], axis=0)
        row = lax.broadcasted_iota(jnp.int32, (tee, 1), 0)
        zext = hbe[:, bw:2 * bw] * hbe[:, 2 * bw:]
        zext = jnp.where((i == 0) & (row < HALO), 0.0, zext)
        z2e, z1e = _conv_taps(zext)
        cw = cw_ref[...]
        conv_e = (cw[0:1] * z2e + cw[1:2] * z1e + cw[2:3] * zext)[HALO:]
        gate_b_e = hbe[HALO:, :bw]
        sc_e = gate_b_e * conv_e
        rb = _rstd(sc_e)
        dyb = dycat[:, aw:]
        gdy = dyb * gb_ref[...]
        dsc_e = rb * gdy - sc_e * (rb * rb * rb) * (jnp.sum(gdy * sc_e, axis=-1, keepdims=True) * (1.0 / bw))
        dgb_ref[...] += jnp.sum((dyb * (sc_e * rb))[:tm], axis=0, keepdims=True)
        dconv_e = dsc_e * gate_b_e
        dconv_e = jnp.where((i == nt - 1) & (row[:te] >= tm), 0.0, dconv_e)
        dconv = dconv_e[:tm]
        dc1 = pltpu.roll(dconv_e, te - 1, 0)[:tm]
        dc2 = pltpu.roll(dconv_e, te - 2, 0)[:tm]
        dz = cw[2:3] * dconv + cw[1:2] * dc1 + cw[0:1] * dc2
        z = zext[HALO:HALO + tm]
        z1 = z1e[HALO:HALO + tm]
        z2 = z2e[HALO:HALO + tm]
        dcw_ref[0:1, :] += jnp.sum(dconv * z2, axis=0, keepdims=True)
        dcw_ref[1:2, :] += jnp.sum(dconv * z1, axis=0, keepdims=True)
        dcw_ref[2:3, :] += jnp.sum(dconv * z, axis=0, keepdims=True)
        dh_ref[:, in_a:in_a + bw] = (dsc_e[:tm] * conv_e[:tm]).astype(BF16)
        dh_ref[:, in_a + bw:in_a + 2 * bw] = (dz * hv[:, in_a + 2 * bw:]).astype(BF16)
        dh_ref[:, in_a + 2 * bw:] = (dz * hv[:, in_a + bw:in_a + 2 * bw]).astype(BF16)

        ha = hv[:, :in_a]
        th, u, xhat, rl, vln = _sgu_forward(ha, lng_ref[...], lnb_ref[...], wm, bt_ref[...], mixed_s)
        mixed = mixed_s[...]
        sg = u * mixed
        dsg, dga = _rms_bwd(dycat[:tm, :aw], sg, _rstd(sg), ga_ref[...])
        dga_ref[...] += dga
        du = dsg * mixed
        dmixed = dsg * u
        dmb = dmixed.astype(BF16)
        for n in range(tm // CHUNK):
            rows = slice(n * CHUNK, (n + 1) * CHUNK)
            dbs_ref[...] += dmixed[rows]
            for hh in range(HEADS):
                cols = slice(hh * hd, (hh + 1) * hd)
                dvln_s[rows, cols] = _dot_tn(wm[hh], dmb[rows, cols])
                dwsp_ref[hh] += mask * _dot_nt(dmb[rows, cols], vln[rows, cols])
        dvln = dvln_s[...]
        dlng_ref[...] += jnp.sum(dvln * xhat, axis=0, keepdims=True)
        dlnb_ref[...] += jnp.sum(dvln, axis=0, keepdims=True)
        dxh = dvln * lng_ref[...]
        dv = rl * (dxh - jnp.mean(dxh, axis=-1, keepdims=True) - xhat * jnp.mean(dxh * xhat, axis=-1, keepdims=True))
        dh_ref[:, :in_a] = (jnp.concatenate([du, dv], axis=-1) * _gelu_grad(ha, th)).astype(BF16)

        dxn = _dot(dh_ref[...], win_ref[...])
        xv = x_ref[...]
        dx, dgm = _rms_bwd(dxn, xv, _rstd(xv), gm_ref[...])
        dgm_ref[...] += dgm
        dx_ref[...] = dx1 + dx

    full = lambda shape: pl.BlockSpec(shape, lambda i: (0,) * len(shape))
    tok = pl.BlockSpec((tm, d), lambda i: (i, 0))
    nxt = lambda i: (jnp.minimum((i + 1) * hb_blocks, last_blk), 0)
    prv = lambda i: (jnp.maximum(i * hb_blocks - 1, 0), 0)
    return _hosted(
        body, carry, grid=(nt,),
        in_specs=[tok, pl.BlockSpec((HALO, d), nxt), tok, full((1, d)),
                  pl.BlockSpec((tm, n_in), lambda i: (i, 0)), pl.BlockSpec((HALO, n_in), prv), pl.BlockSpec((HALO, n_in), nxt),
                  full((1, aw)), full((1, aw)), full((HEADS, CHUNK, CHUNK)), full((CHUNK, HEADS)), full((3, bw)),
                  full((1, aw)), full((1, bw)), full((d, d)), full((n_in, d))],
        out_specs=[tok, pl.BlockSpec((tm, n_in), lambda i: (i, 0)), tok,
                   full((1, aw)), full((1, bw)), full((SUB, bw)), full((1, aw)), full((1, aw)),
                   full((HEADS, CHUNK, CHUNK)), full((CHUNK, aw)), full((1, d))],
        out_shape=(SDS((s, d), BF16), SDS((s, n_in), BF16), SDS((s, d), F32),
                   SDS((1, aw), F32), SDS((1, bw), F32), SDS((SUB, bw), F32), SDS((1, aw), F32), SDS((1, aw), F32),
                   SDS((HEADS, CHUNK, CHUNK), F32), SDS((CHUNK, aw), F32), SDS((1, d), F32)),
        scratch_shapes=[pltpu.VMEM((tm, aw), F32), pltpu.VMEM((tm, aw), F32)],
        compiler_params=_arb(1), name="mix_backward")(dx1, dx1, x, g_mix, h, h, h, lng, lnb, w_sp, bt, conv_w, ga, gb, w_out, w_in)


def _bias_grad(dbs):
    aw = dbs.shape[1]
    hd = aw // HEADS

    def body(dbs_ref, out_ref):
        ones = jnp.ones((SUB, hd), F32)
        for hh in range(HEADS):
            r = lax.dot_general(ones, dbs_ref[:, hh * hd:(hh + 1) * hd], (((1,), (1,)), ((), ())),
                                precision=lax.Precision.HIGHEST, preferred_element_type=F32)
            out_ref[hh:hh + 1, :] = r[0:1]

    return _pcall(body, out_shape=SDS((HEADS, CHUNK), F32), name="bias_grad")(dbs)


def _wgrad_body(a_ref, b_ref, o_ref):
    o_ref[...] = _dot_tn(a_ref[...], b_ref[...])


def _wgrad(a, b, name, carry=()):
    k, m = a.shape
    n = b.shape[1]
    tm = _tile(m, 512, LANES)
    tn = _tile(n, 1024, LANES)
    return _hosted(
        functools.partial(_wgrad_body), carry, grid=(m // tm, n // tn),
        in_specs=[pl.BlockSpec((k, tm), lambda i, j: (0, i)), pl.BlockSpec((k, tn), lambda i, j: (0, j))],
        out_specs=pl.BlockSpec((tm, tn), lambda i, j: (i, j)),
        out_shape=SDS((m, n), F32), compiler_params=_arb(2), name=name)(a, b)


def _wgrad_blocked_lhs(a, b, name, carry=()):
    nb, k, t = a.shape
    n = b.shape[1]
    tn = _tile(n, 1024, LANES)
    return _hosted(
        functools.partial(_wgrad_body), carry, grid=(nb, n // tn),
        in_specs=[pl.BlockSpec((None, k, t), lambda i, j: (i, 0, 0)), pl.BlockSpec((k, tn), lambda i, j: (0, j))],
        out_specs=pl.BlockSpec((t, tn), lambda i, j: (i, j)),
        out_shape=SDS((nb * t, n), F32), compiler_params=_arb(2), name=name)(a, b)


def _wgrad_blocked_rhs(a, b, name, carry=()):
    k, m = a.shape
    nb, _, t = b.shape
    tm = _tile(m, 512, LANES)
    return _hosted(
        functools.partial(_wgrad_body), carry, grid=(m // tm, nb),
        in_specs=[pl.BlockSpec((k, tm), lambda i, j: (0, i)), pl.BlockSpec((None, k, t), lambda i, j: (j, 0, 0))],
        out_specs=pl.BlockSpec((None, tm, t), lambda i, j: (j, i, 0)),
        out_shape=SDS((nb, m, t), F32), compiler_params=_arb(2), name=name)(a, b)


def _unblock_cols(wb, name, carry=()):
    nb, r, t = wb.shape
    tr = _tile(r, 256, 16)

    def body(w_ref, o_ref):
        o_ref[...] = jnp.concatenate([w_ref[j].astype(F32) for j in range(nb)], axis=-1).astype(o_ref.dtype)

    return _hosted(
        body, carry, grid=(r // tr,),
        in_specs=[pl.BlockSpec((nb, tr, t), lambda i: (0, i, 0))], out_specs=pl.BlockSpec((tr, nb * t), lambda i: (i, 0)),
        out_shape=SDS((r, nb * t), wb.dtype), compiler_params=_arb(1), name=name)(wb)


def _block_cols(w, nb, name, carry=()):
    r, n = w.shape
    t = n // nb
    tr = _tile(r, 256, 16)

    def body(w_ref, o_ref):
        wv = w_ref[...]
        for j in range(nb):
            o_ref[j] = wv[:, j * t:(j + 1) * t]

    return _hosted(
        body, carry, grid=(r // tr,),
        in_specs=[pl.BlockSpec((tr, n), lambda i: (i, 0))], out_specs=pl.BlockSpec((nb, tr, t), lambda i: (0, i, 0)),
        out_shape=SDS((nb, r, t), w.dtype), compiler_params=_arb(1), name=name)(w)


def _place():
    x, y, c = lax.axis_index("x"), lax.axis_index("y"), lax.axis_index("c")
    return x, y, c, [(1 - x, y), (x, 1 - y), (1 - x, 1 - y)]


def _all_gather(shards):
    n = len(shards)
    slots = 9
    cut = [(s.shape[0] // 32) * 16 for s in shards]

    def build(ins, outs, sems):
        send_sems, recv_sems, local_sems = sems
        x, y, c, _ = _place()
        me, sib, xn, yn, dg = (x, y, c), (x, y, 1 - c), (1 - x, y, c), (x, 1 - y, c), (1 - x, 1 - y, c)
        other = lambda p: (p[0], p[1], 1 - p[2])

        def rows(a, p, part=None):
            ref = outs[a].at[4 * p[0] + 2 * p[1] + p[2]]
            if part is None or cut[a] == 0:
                return ref if part in (None, 0) else None
            return ref.at[pl.ds(0, cut[a])] if part == 0 else ref.at[pl.ds(cut[a], shards[a].shape[0] - cut[a])]

        def copy(a, k, ref, to, src=None):
            if ref is None:
                return None
            return pltpu.make_async_remote_copy(
                src_ref=ref if src is None else src, dst_ref=ref, send_sem=send_sems.at[slots * a + k],
                recv_sem=recv_sems.at[slots * a + k], device_id=to, device_id_type=MESH)

        def real(cps):
            return [cp for cp in cps if cp is not None]

        class Copies:
            own = lambda a: [copy(a, 1, rows(a, me), xn, ins[a]), copy(a, 2, rows(a, me), yn, ins[a]),
                             copy(a, 0, rows(a, me), sib, ins[a])]
            local = lambda a: pltpu.make_async_copy(ins[a], rows(a, me), local_sems.at[a])
            from_x = lambda a: copy(a, 1, rows(a, xn), me)
            from_y = lambda a: copy(a, 2, rows(a, yn), me)
            after_x = lambda a: real([copy(a, 4, rows(a, xn, 1), yn), copy(a, 5, rows(a, xn), sib)])
            after_y = lambda a: real([copy(a, 3, rows(a, yn, 0), xn), copy(a, 6, rows(a, yn), sib)])
            diag_in = lambda a: real([copy(a, 3, rows(a, dg, 0), me), copy(a, 4, rows(a, dg, 1), me)])
            diag_on = lambda a: real([copy(a, 7, rows(a, dg, 0), sib), copy(a, 8, rows(a, dg, 1), sib)])
            from_sib = lambda a: real([copy(a, 0, rows(a, sib), me), copy(a, 5, rows(a, other(xn)), me),
                                       copy(a, 6, rows(a, other(yn)), me), copy(a, 7, rows(a, other(dg), 0), me),
                                       copy(a, 8, rows(a, other(dg), 1), me)])

        return Copies

    def start(ins, outs, sems):
        cps = build(ins, outs, sems)
        for a in range(n):
            for cp in cps.own(a):
                cp.start()
        for a in range(n):
            cps.local(a).start()

    def relay(ins, outs, sems):
        cps = build(ins, outs, sems)
        for a in range(n):
            cps.from_x(a).wait_recv()
            for cp in cps.after_x(a):
                cp.start()
            cps.from_y(a).wait_recv()
            for cp in cps.after_y(a):
                cp.start()

    def finish(ins, outs, sems):
        cps = build(ins, outs, sems)
        for a in range(n):
            for arrived, onward in zip(cps.diag_in(a), cps.diag_on(a)):
                arrived.wait_recv()
                onward.start()
        for a in range(n):
            for cp in cps.from_sib(a):
                cp.wait_recv()
            for cp in cps.own(a) + cps.after_x(a) + cps.after_y(a) + cps.diag_on(a):
                cp.wait_send()
            cps.local(a).wait()

    return _Exchange(shards, [SDS((N_DEV,) + s.shape, s.dtype) for s in shards],
                     [pltpu.SemaphoreType.DMA((slots * n,)), pltpu.SemaphoreType.DMA((slots * n,)),
                      pltpu.SemaphoreType.DMA((n,))], start, finish, relay)


def _swap_exchange(ins, out_shape, per, copies):
    def start(i, o, sems):
        for cp in copies(i, o, sems):
            cp.start()

    def finish(i, o, sems):
        for cp in copies(i, o, sems):
            cp.wait()

    n = per * len(ins)
    return _Exchange(ins, out_shape, [pltpu.SemaphoreType.DMA((n,)), pltpu.SemaphoreType.DMA((n,))], start, finish)


def _exchange_c(gs):
    def copies(ins, outs, sems):
        x, y, c, _ = _place()
        return [pltpu.make_async_remote_copy(
                    src_ref=ins[a].at[2 * k + 1 - c], dst_ref=outs[a].at[k],
                    send_sem=sems[0].at[4 * a + k], recv_sem=sems[1].at[4 * a + k],
                    device_id=(x, y, 1 - c), device_id_type=MESH)
                for a in range(len(gs)) for k in range(4)]

    return _swap_exchange(gs, [SDS((4,) + g.shape[1:], g.dtype) for g in gs], 4, copies)


def _exchange_xy(sends):
    def copies(ins, outs, sems):
        x, y, c, chips = _place()
        return [pltpu.make_async_remote_copy(
                    src_ref=ins[a].at[t], dst_ref=outs[a].at[t],
                    send_sem=sems[0].at[3 * a + t], recv_sem=sems[1].at[3 * a + t],
                    device_id=(*chips[t], c), device_id_type=MESH)
                for a in range(len(sends)) for t in range(3)]

    return _swap_exchange(sends, [SDS(s.shape, s.dtype) for s in sends], 3, copies)


def _rs_combine(g, recv, pos, name, carry=()):
    _, r, cdim = g.shape
    tr = _tile(r, 256, 16)

    def body(pos_ref, g0, r0, g1, r1, g2, r2, g3, r3, keep_ref, send_ref):
        keep_ref[...] = g0[...] + r0[...]
        send_ref[0] = (g1[...] + r1[...]).astype(BF16)
        send_ref[1] = (g2[...] + r2[...]).astype(BF16)
        send_ref[2] = (g3[...] + r3[...]).astype(BF16)

    def k_of(p, t):
        px = p[0] if t in (0, 2) else 1 - p[0]
        py = p[1] if t in (0, 1) else 1 - p[1]
        return 2 * px + py

    blk = (None, tr, cdim)
    in_specs = []
    for t in range(4):
        in_specs.append(pl.BlockSpec(blk, functools.partial(lambda j, p, t: (2 * k_of(p, t) + p[2], j, 0), t=t)))
        in_specs.append(pl.BlockSpec(blk, functools.partial(lambda j, p, t: (k_of(p, t), j, 0), t=t)))
    return _hosted(
        body, carry, n_prefetch=1, out_shape=(SDS((r, cdim), F32), SDS((3, r, cdim), BF16)),
        grid=(r // tr,), in_specs=in_specs,
        out_specs=[pl.BlockSpec((tr, cdim), lambda j, p: (j, 0)), pl.BlockSpec((3, tr, cdim), lambda j, p: (0, j, 0))],
        compiler_params=_arb(1), name=name)(pos, g, recv, g, recv, g, recv, g, recv)


def _adamw_shard(keep, recv, w, m, v, name):
    r, cdim = w.shape
    tr = _tile(r, 256, 16)

    def body(k_ref, r_ref, w_ref, m_ref, v_ref, g_ref, d_ref, nm_ref, nv_ref):
        g = ((k_ref[...] + r_ref[0].astype(F32)) + r_ref[1].astype(F32)) + r_ref[2].astype(F32)
        g_ref[...] = g
        d_ref[...], nm_ref[...], nv_ref[...] = _adamw(w_ref[...], g, m_ref[...], v_ref[...])

    blk = pl.BlockSpec((tr, cdim), lambda j: (j, 0))
    out = SDS((r, cdim), F32)
    return _pcall(body, grid=(r // tr,), in_specs=[blk, pl.BlockSpec((3, tr, cdim), lambda j: (0, j, 0)), blk, blk, blk],
                  out_specs=[blk] * 4, out_shape=(out,) * 4, compiler_params=_arb(1), name=name)(keep, recv, w, m, v)


def _adamw_small(gathered, seg, params, conv_rows):
    names = list(params)
    c0, cn = conv_rows

    def body(*refs):
        gat_ref = refs[0]
        ins = refs[1:1 + 3 * len(names)]
        outs = refs[1 + 3 * len(names):]

        def total(r0, rn):
            tot = gat_ref[0, r0:r0 + rn, :]
            for dev in range(1, N_DEV):
                tot = tot + gat_ref[dev, r0:r0 + rn, :]
            return tot

        for k, nm in enumerate(names):
            g = total(*seg[nm])
            w_ref, m_ref, v_ref = ins[3 * k:3 * k + 3]
            g_ref, d_ref, nm_ref, nv_ref = outs[4 * k:4 * k + 4]
            g_ref[...] = g
            d_ref[...], nm_ref[...], nv_ref[...] = _adamw(w_ref[...], g, m_ref[...], v_ref[...])
        outs[-2][...] = total(c0, cn)
        outs[-1][...] = total(*seg["loss"])

    flat_in = [a for nm in names for a in params[nm]]
    out_shape = []
    for nm in names:
        out_shape += [SDS(params[nm][0].shape, F32)] * 4
    out_shape += [SDS((cn, LANES), F32), SDS((seg["loss"][1], LANES), F32)]
    res = _pcall(body, out_shape=tuple(out_shape), name="adamw_small")(gathered, *flat_in)
    per = {nm: res[4 * k:4 * k + 4] for k, nm in enumerate(names)}
    return per, res[-2], res[-1]


def _adamw_one(w, g, m, v, name):
    def body(w_ref, g_ref, m_ref, v_ref, d_ref, nm_ref, nv_ref):
        d_ref[...], nm_ref[...], nv_ref[...] = _adamw(w_ref[...], g_ref[...], m_ref[...], v_ref[...])

    return _pcall(body, out_shape=(SDS(w.shape, F32),) * 3, name=name)(w, g, m, v)


def _rows128(a):
    return a.reshape(-1, LANES)


def _pack_small(gs, loss_tile):
    seg, pieces, row = {}, [], 0
    for nm in SMALL + ("conv_w", "loss"):
        piece = loss_tile if nm == "loss" else _rows128(gs[nm])
        rn = _round_up(piece.shape[0], SUB)
        pieces.append(jnp.pad(piece, ((0, rn - piece.shape[0]), (0, 0))))
        seg[nm] = (row, piece.shape[0])
        row += rn
    return jnp.concatenate(pieces, axis=0), seg


def _step(x, mem, target, wb, conv_w, sp, pos):
    s, d = x.shape
    tm = min(TOKEN_TILE, s)
    tm_wide = min(2 * TOKEN_TILE, s)
    rows = lambda w8: w8.reshape(-1, w8.shape[2])
    shards = lambda g: g.reshape((N_DEV, -1) + g.shape[1:])
    bt = sp["b_spatial"].T

    (w_in8, conv8), = _run_exchanges([_all_gather([wb["w_in"], conv_w])], "gather_w_in")
    conv_full = conv8.transpose(1, 0, 2).reshape(3, -1)
    w_in_t = rows(w_in8)
    (xn1, h), ((w_out8, w_kv8, w_q8),) = _in_forward(
        x, sp["ln_mix_g"], w_in_t, tm_wide, carry=[_all_gather([wb["w_out"], wb["w_kv"], wb["w_q"]])])
    w_out = rows(w_out8)
    (ycat, x1), ((w_o8, w_down8),) = _mix_forward(
        h, x, sp["sgu_ln_g"], sp["sgu_ln_b"], sp["w_spatial"], bt, conv_full, sp["grp_norm_a"], sp["grp_norm_b"], w_out, tm,
        carry=[_all_gather([wb["w_o"], wb["w_down"]])])
    w_q, w_o, w_down = rows(w_q8), rows(w_o8), rows(w_down8)
    memn, kv = _kv_forward(mem, sp["ln_mem_g"], w_kv8)
    (xn2, q, o, x2), ((w_gu8,),) = _attn_forward(
        x1, sp["ln_attn_g"], w_q, kv, w_o, tm_wide, carry=[_all_gather([wb["w_gate_up"]])])
    w_gu = w_gu8.reshape((2, N_DEV // 2) + w_gu8.shape[1:])
    xn3, gu, x3 = _ffn_forward(x2, sp["ln_ffn_g"], w_gu, w_down, tm_wide)

    loss, d_lnf, d_lnffn, act, dgu, dx3b, dx2 = _ffn_backward(
        x3, target, sp["ln_final_g"], x2, sp["ln_ffn_g"], gu, w_gu, w_down, tm)
    part = {}
    g_gu, _ = _wgrad_blocked_lhs(dgu.reshape((N_DEV,) + dgu.shape[2:]), xn3, "wgrad_gate_up")
    g_gu = shards(g_gu)
    g_down, ((rc_gu,),) = _wgrad_blocked_lhs(act, dx3b, "wgrad_down", carry=[_exchange_c([g_gu])])
    g_down = shards(g_down)
    (keep_gu, send_gu), _ = _rs_combine(g_gu, rc_gu, pos, "rs_combine_w_gate_up")
    (dx2b, dq, dx1, dkv, d_lnattn), ((rxy_gu,), (rc_down,)) = _attn_backward(
        dx2, x1, sp["ln_attn_g"], q, kv, w_q, w_o, tm, carry=[_exchange_xy([send_gu]), _exchange_c([g_down])])
    part["w_gate_up"] = (keep_gu, rxy_gu)
    (keep_down, send_down), _ = _rs_combine(g_down, rc_down, pos, "rs_combine_w_down")
    g_o, _ = _wgrad(o, dx2b, "wgrad_o")
    g_o = shards(g_o)
    g_q, ((rc_o,),) = _wgrad(xn2, dq, "wgrad_q", carry=[_exchange_c([g_o])])
    g_q = shards(g_q)
    g_kv, d_lnmem = _kv_backward(dkv, memn, mem, sp["ln_mem_g"], w_kv8)
    (keep_o, send_o), ((rc_q,),) = _rs_combine(g_o, rc_o, pos, "rs_combine_w_o", carry=[_exchange_c([g_q])])
    (keep_q, send_q), _ = _rs_combine(g_q, rc_q, pos, "rs_combine_w_q")
    ((dx1b, dh, dx, d_ga, d_gb, d_cw, d_lng, d_lnb, d_wsp, d_bs, d_lnmix),
     ((rxy_down, rxy_o, rxy_q), (rc_kv,))) = _mix_backward(
        dx1, x, sp["ln_mix_g"], h, sp["sgu_ln_g"], sp["sgu_ln_b"], sp["w_spatial"], bt, conv_full,
        sp["grp_norm_a"], sp["grp_norm_b"], w_out, w_in_t, tm,
        carry=[_exchange_xy([send_down, send_o, send_q]), _exchange_c([g_kv])])
    part["w_down"], part["w_o"], part["w_q"] = (keep_down, rxy_down), (keep_o, rxy_o), (keep_q, rxy_q)
    (keep_kv, send_kv), _ = _rs_combine(g_kv, rc_kv, pos, "rs_combine_w_kv")
    gs = {"ln_mix_g": d_lnmix, "sgu_ln_g": d_lng, "sgu_ln_b": d_lnb, "w_spatial": d_wsp, "b_spatial": _bias_grad(d_bs),
          "conv_w": d_cw[:3], "grp_norm_a": d_ga, "grp_norm_b": d_gb, "ln_attn_g": d_lnattn, "ln_mem_g": d_lnmem,
          "ln_ffn_g": d_lnffn, "ln_final_g": d_lnf}
    packed, seg = _pack_small(gs, loss)
    g_in, ((rxy_kv,), (small_all,)) = _wgrad(
        dh, xn1, "wgrad_in", carry=[_exchange_xy([send_kv]), _all_gather([packed])])
    g_in = shards(g_in)
    part["w_kv"] = (keep_kv, rxy_kv)
    g_out, ((rc_in,),) = _wgrad(ycat, dx1b, "wgrad_out", carry=[_exchange_c([g_in])])
    g_out = shards(g_out)
    (keep_in, send_in), ((rc_out,),) = _rs_combine(g_in, rc_in, pos, "rs_combine_w_in", carry=[_exchange_c([g_out])])
    (keep_out, send_out), _ = _rs_combine(g_out, rc_out, pos, "rs_combine_w_out")
    (rxy_in, rxy_out), = _run_exchanges([_exchange_xy([send_in, send_out])], "exchange_xy_w_in_w_out")
    part["w_in"], part["w_out"] = (keep_in, rxy_in), (keep_out, rxy_out)
    return dx, part, small_all, seg


def kernel(x, mem, ln_mix_g, w_in, sgu_ln_g, sgu_ln_b, w_spatial, b_spatial, conv_w, grp_norm_a, grp_norm_b, w_out, ln_attn_g, ln_mem_g, w_q, w_kv, w_o, ln_ffn_g, w_gate_up, w_down, ln_final_g, loss_target, m_ln_mix_g, m_w_in, m_sgu_ln_g, m_sgu_ln_b, m_w_spatial, m_b_spatial, m_conv_w, m_grp_norm_a, m_grp_norm_b, m_w_out, m_ln_attn_g, m_ln_mem_g, m_w_q, m_w_kv, m_w_o, m_ln_ffn_g, m_w_gate_up, m_w_down, m_ln_final_g, v_ln_mix_g, v_w_in, v_sgu_ln_g, v_sgu_ln_b, v_w_spatial, v_b_spatial, v_conv_w, v_grp_norm_a, v_grp_norm_b, v_w_out, v_ln_attn_g, v_ln_mem_g, v_w_q, v_w_kv, v_w_o, v_ln_ffn_g, v_w_gate_up, v_w_down, v_ln_final_g):
    order = ["ln_mix_g", "w_in", "sgu_ln_g", "sgu_ln_b", "w_spatial", "b_spatial", "conv_w", "grp_norm_a", "grp_norm_b",
             "w_out", "ln_attn_g", "ln_mem_g", "w_q", "w_kv", "w_o", "ln_ffn_g", "w_gate_up", "w_down", "ln_final_g"]
    W = dict(ln_mix_g=ln_mix_g, w_in=w_in, sgu_ln_g=sgu_ln_g, sgu_ln_b=sgu_ln_b, w_spatial=w_spatial, b_spatial=b_spatial,
             conv_w=conv_w, grp_norm_a=grp_norm_a, grp_norm_b=grp_norm_b, w_out=w_out, ln_attn_g=ln_attn_g,
             ln_mem_g=ln_mem_g, w_q=w_q, w_kv=w_kv, w_o=w_o, ln_ffn_g=ln_ffn_g, w_gate_up=w_gate_up, w_down=w_down,
             ln_final_g=ln_final_g)
    M = dict(ln_mix_g=m_ln_mix_g, w_in=m_w_in, sgu_ln_g=m_sgu_ln_g, sgu_ln_b=m_sgu_ln_b, w_spatial=m_w_spatial,
             b_spatial=m_b_spatial, conv_w=m_conv_w, grp_norm_a=m_grp_norm_a, grp_norm_b=m_grp_norm_b, w_out=m_w_out,
             ln_attn_g=m_ln_attn_g, ln_mem_g=m_ln_mem_g, w_q=m_w_q, w_kv=m_w_kv, w_o=m_w_o, ln_ffn_g=m_ln_ffn_g,
             w_gate_up=m_w_gate_up, w_down=m_w_down, ln_final_g=m_ln_final_g)
    V = dict(ln_mix_g=v_ln_mix_g, w_in=v_w_in, sgu_ln_g=v_sgu_ln_g, sgu_ln_b=v_sgu_ln_b, w_spatial=v_w_spatial,
             b_spatial=v_b_spatial, conv_w=v_conv_w, grp_norm_a=v_grp_norm_a, grp_norm_b=v_grp_norm_b, w_out=v_w_out,
             ln_attn_g=v_ln_attn_g, ln_mem_g=v_ln_mem_g, w_q=v_w_q, w_kv=v_w_kv, w_o=v_w_o, ln_ffn_g=v_ln_ffn_g,
             w_gate_up=v_w_gate_up, w_down=v_w_down, ln_final_g=v_ln_final_g)

    bw = conv_w.shape[1] * N_DEV
    pos = jnp.stack([lax.axis_index("x"), lax.axis_index("y"), lax.axis_index("c")]).astype(jnp.int32)
    me = 4 * pos[0] + 2 * pos[1] + pos[2]

    sp = {nm: (W[nm].reshape(1, -1) if W[nm].ndim == 1 else W[nm]) for nm in SMALL}
    view = lambda a, nm: a.T if nm in TRANSPOSED else a
    wb = {nm: view(W[nm], nm).astype(BF16) for nm in BIG}
    grad_x, part, small_all, seg = _step(x[0], mem[0], loss_target[0], wb, conv_w, sp, pos)

    out = {}
    for nm in BIG:
        res = _adamw_shard(part[nm][0], part[nm][1], view(W[nm], nm), view(M[nm], nm), view(V[nm], nm), "adamw_" + nm)
        out[nm] = tuple(view(a, nm) for a in res)

    params = {nm: (_rows128(W[nm]), _rows128(M[nm]), _rows128(V[nm])) for nm in SMALL}
    per, conv_g_rows, loss_sum = _adamw_small(small_all, seg, params, seg["conv_w"])
    for nm in SMALL:
        out[nm] = tuple(a.reshape(W[nm].shape) for a in per[nm])
    conv_g = lax.dynamic_slice_in_dim(conv_g_rows.reshape(3, bw), me * conv_w.shape[1], conv_w.shape[1], axis=1)
    out["conv_w"] = (conv_g,) + tuple(_adamw_one(conv_w, conv_g, m_conv_w, v_conv_w, "adamw_conv"))

    loss = loss_sum[0, 0]
    res = [loss, grad_x[None]]
    for k in range(4):
        res += [out[nm][k] for nm in order]
    return tuple(res)
```

```python
import functools

import jax
import jax.numpy as jnp
from jax import lax
from jax.experimental import pallas as pl
from jax.experimental.pallas import tpu as pltpu

F32 = jnp.float32
BF16 = jnp.bfloat16
SDS = jax.ShapeDtypeStruct
MESH = pl.DeviceIdType.MESH

EPS = 1e-6
N_DEV = 8
HEADS = 4
CHUNK = 128
HALO = 16
SUB = 8
LANES = 128
TOKEN_TILE = 512

ADAM_LR = 0.001
ADAM_B1 = 0.9
ADAM_B2 = 0.999
ADAM_EPS = 1e-08
ADAM_WD = 0.01
ADAM_STEP = 10

BIG = ("w_in", "w_out", "w_q", "w_kv", "w_o", "w_gate_up", "w_down")
TRANSPOSED = ("w_in", "w_gate_up")
SMALL = ("ln_mix_g", "sgu_ln_g", "sgu_ln_b", "w_spatial", "b_spatial", "grp_norm_a", "grp_norm_b",
         "ln_attn_g", "ln_mem_g", "ln_ffn_g", "ln_final_g")


class _Exchange:
    def __init__(self, ins, out_shape, sems, start, finish, relay=None):
        self.ins, self.out_shape, self.sems = list(ins), list(out_shape), list(sems)
        self.start, self.finish, self.relay = start, finish, relay


def _pcall(body, carry=(), n_prefetch=0, **kw):
    if carry:
        return functools.partial(_carrying_call, body, tuple(carry), n_prefetch, kw)
    if n_prefetch:
        kw["grid_spec"] = pltpu.PrefetchScalarGridSpec(
            num_scalar_prefetch=n_prefetch, grid=kw.pop("grid"), in_specs=kw.pop("in_specs"),
            out_specs=kw.pop("out_specs"), scratch_shapes=kw.pop("scratch_shapes", ()))
    return pl.pallas_call(body, **kw)


def _carrying_call(body, carry, n_prefetch, kw, *args):
    kw = dict(kw)
    out_shape = kw.pop("out_shape")
    single = not isinstance(out_shape, (tuple, list))
    outs_shape = (out_shape,) if single else tuple(out_shape)
    out_specs = kw.pop("out_specs")
    out_specs = [out_specs] if single else list(out_specs)
    in_specs = list(kw.pop("in_specs"))
    scratch = list(kw.pop("scratch_shapes", ()))
    grid = tuple(kw.get("grid", ()))
    n_in, n_out, n_scr = len(args), len(outs_shape), len(scratch)

    def split(refs, k, counts):
        parts = []
        for cnt in counts:
            parts.append(refs[k:k + cnt])
            k += cnt
        return parts, k

    def wrapped(*refs):
        cins, k = split(refs, n_in, [len(p.ins) for p in carry])
        outs = refs[k:k + n_out]
        couts, k = split(refs, k + n_out, [len(p.out_shape) for p in carry])
        scr = refs[k:k + n_scr]
        csems, _ = split(refs, k + n_scr, [len(p.sems) for p in carry])
        first, last = True, True
        for a, g in enumerate(grid):
            first = (pl.program_id(a) == 0) & first
            last = (pl.program_id(a) == g - 1) & last

        def start_all():
            for p, ci, co, cs in zip(carry, cins, couts, csems):
                p.start(ci, co, cs)

        def relay_all():
            for p, ci, co, cs in zip(carry, cins, couts, csems):
                if p.relay is not None:
                    p.relay(ci, co, cs)

        def finish_all():
            for p, ci, co, cs in zip(carry, cins, couts, csems):
                p.finish(ci, co, cs)

        start_all() if not grid else pl.when(first)(start_all)
        relay_all() if not grid else pl.when(last)(relay_all)
        body(*refs[:n_in], *outs, *scr)
        finish_all() if not grid else pl.when(last)(finish_all)

    c_in = [a for p in carry for a in p.ins]
    c_out = [s for p in carry for s in p.out_shape]
    c_sems = [s for p in carry for s in p.sems]
    res = _pcall(wrapped, n_prefetch=n_prefetch, out_shape=outs_shape + tuple(c_out),
                 in_specs=in_specs + _hbm_specs(len(c_in)), out_specs=out_specs + _hbm_specs(len(c_out)),
                 scratch_shapes=scratch + c_sems, **kw)(*args, *c_in)
    own = res[0] if single else tuple(res[:n_out])
    landed, k = [], n_out
    for p in carry:
        landed.append(list(res[k:k + len(p.out_shape)]))
        k += len(p.out_shape)
    return own, landed


def _hbm_specs(n):
    return [pl.BlockSpec(memory_space=pl.ANY)] * n


def _hosted(body, carry, **kw):
    if carry:
        return _pcall(body, carry=carry, **kw)
    call = _pcall(body, **kw)
    return lambda *args: (call(*args), [])


def _run_exchanges(parts, name):
    def body(*refs):
        pass

    _, landed = _pcall(body, carry=parts, out_shape=(), in_specs=[], out_specs=[], name=name)()
    return landed


def _arb(n):
    return pltpu.CompilerParams(dimension_semantics=("arbitrary",) * n)


def _tile(n, target, mult):
    best = None
    for t in range(mult, min(n, target) + 1, mult):
        if n % t == 0:
            best = t
    return n if best is None else best


def _round_up(n, m):
    return (n + m - 1) // m * m


def _dot(a, b):
    return jnp.dot(a, b, preferred_element_type=F32)


def _dot_nt(a, b):
    return lax.dot_general(a, b, (((1,), (1,)), ((), ())), preferred_element_type=F32)


def _dot_tn(a, b):
    return lax.dot_general(a, b, (((0,), (0,)), ((), ())), preferred_element_type=F32)


def _rstd(x):
    return lax.rsqrt(jnp.mean(x * x, axis=-1, keepdims=True) + EPS)


def _rms_bwd(dy, x, r, g):
    gdy = dy * g
    proj = jnp.sum(gdy * x, axis=-1, keepdims=True) * (1.0 / x.shape[-1])
    dx = r * gdy - x * (r * r * r) * proj
    dg = jnp.sum(dy * (x * r), axis=0, keepdims=True)
    return dx, dg


_GELU_C = 0.7978845608028654
_GELU_A = 0.044715


def _gelu(x):
    t = jnp.tanh(_GELU_C * (x + _GELU_A * x * x * x))
    return 0.5 * x * (1.0 + t), t


def _gelu_grad(x, t):
    return 0.5 * (1.0 + t) + 0.5 * x * (1.0 - t * t) * (_GELU_C * (1.0 + 3.0 * _GELU_A * x * x))


def _sigmoid(x):
    return 1.0 / (1.0 + jnp.exp(-x))


def _softmax(s):
    m = jnp.max(s, axis=-1, keepdims=True)
    e = jnp.exp(s - m)
    return e / jnp.sum(e, axis=-1, keepdims=True)


def _adamw(w, g, m, v):
    m = ADAM_B1 * m + (1.0 - ADAM_B1) * g
    v = ADAM_B2 * v + (1.0 - ADAM_B2) * (g * g)
    m_hat = m / (1.0 - ADAM_B1 ** ADAM_STEP)
    v_hat = v / (1.0 - ADAM_B2 ** ADAM_STEP)
    delta = -ADAM_LR * (m_hat / (jnp.sqrt(v_hat) + ADAM_EPS) + ADAM_WD * w)
    return delta, m, v


def _tril_mask():
    t = lax.broadcasted_iota(jnp.int32, (CHUNK, CHUNK), 0)
    s = lax.broadcasted_iota(jnp.int32, (CHUNK, CHUNK), 1)
    return (s <= t).astype(F32)


def _sgu_forward(ha, lng, lnb, wm, bt, mixed_s):
    aw = ha.shape[1] // 2
    hd = aw // HEADS
    a, th = _gelu(ha)
    u = a[:, :aw]
    v = a[:, aw:]
    mu = jnp.mean(v, axis=-1, keepdims=True)
    vc = v - mu
    rl = lax.rsqrt(jnp.mean(vc * vc, axis=-1, keepdims=True) + EPS)
    xhat = vc * rl
    vln = (xhat * lng + lnb).astype(BF16)
    for n in range(ha.shape[0] // CHUNK):
        rows = slice(n * CHUNK, (n + 1) * CHUNK)
        for h in range(HEADS):
            cols = slice(h * hd, (h + 1) * hd)
            mixed_s[rows, cols] = _dot(wm[h], vln[rows, cols]) + bt[:, h:h + 1]
    return th, u, xhat, rl, vln


def _conv_taps(zext):
    return pltpu.roll(zext, 2, 0), pltpu.roll(zext, 1, 0)


def _kv_forward(mem, g_mem, w_kv):
    ml, d = mem.shape
    xd = w_kv.shape[2]

    def body(mem_ref, g_ref, w_ref, memn_ref, kv_ref):
        x = mem_ref[...]
        memn = (x * _rstd(x) * g_ref[...]).astype(BF16)
        memn_ref[...] = memn
        for j in range(2 * HEADS):
            kv_ref[j] = _dot(memn, w_ref[j]).astype(BF16)

    return _pcall(body, out_shape=(SDS((ml, d), BF16), SDS((2 * HEADS, ml, xd), BF16)), name="kv_forward")(mem, g_mem, w_kv)


def _in_forward(x, g, w_in_t, tm, carry=()):
    s, d = x.shape
    n_in = w_in_t.shape[0]

    def body(x_ref, g_ref, w_ref, xn_ref, h_ref):
        xv = x_ref[...]
        xn = (xv * _rstd(xv) * g_ref[...]).astype(BF16)
        xn_ref[...] = xn
        h_ref[...] = _dot_nt(xn, w_ref[...])

    return _hosted(
        body, carry, grid=(s // tm,),
        in_specs=[pl.BlockSpec((tm, d), lambda i: (i, 0)), pl.BlockSpec((1, d), lambda i: (0, 0)),
                  pl.BlockSpec((n_in, d), lambda i: (0, 0))],
        out_specs=[pl.BlockSpec((tm, d), lambda i: (i, 0)), pl.BlockSpec((tm, n_in), lambda i: (i, 0))],
        out_shape=(SDS((s, d), BF16), SDS((s, n_in), F32)),
        compiler_params=_arb(1), name="in_forward")(x, g, w_in_t)


def _mix_forward(h, x, lng, lnb, w_sp, bt, conv_w, ga, gb, w_out, tm, carry=()):
    s, d = x.shape
    n_in = h.shape[1]
    aw = lng.shape[1]
    bw = d - aw
    in_a = 2 * aw
    hb_blocks = tm // HALO

    def body(h_ref, hprev_ref, x_ref, lng_ref, lnb_ref, wsp_ref, bt_ref, cw_ref, ga_ref, gb_ref, wout_ref,
             ycat_ref, x1_ref, mixed_s):
        i = pl.program_id(0)
        mask = _tril_mask()
        wm = [(wsp_ref[hh] * mask).astype(BF16) for hh in range(HEADS)]
        hv = h_ref[...]
        _, u, _, _, _ = _sgu_forward(hv[:, :in_a], lng_ref[...], lnb_ref[...], wm, bt_ref[...], mixed_s)
        sg = u * mixed_s[...]
        ycat_ref[:, :aw] = (sg * _rstd(sg) * ga_ref[...]).astype(BF16)

        gate_b = hv[:, in_a:in_a + bw]
        z = hv[:, in_a + bw:in_a + 2 * bw] * hv[:, in_a + 2 * bw:]
        hp = hprev_ref[...]
        zp = hp[:, in_a + bw:in_a + 2 * bw] * hp[:, in_a + 2 * bw:]
        zp = jnp.where(i == 0, 0.0, zp)
        zext = jnp.concatenate([zp, z], axis=0)
        z2, z1 = _conv_taps(zext)
        cw = cw_ref[...]
        conv = cw[0:1] * z2[HALO:] + cw[1:2] * z1[HALO:] + cw[2:3] * z
        sc = gate_b * conv
        ycat_ref[:, aw:] = (sc * _rstd(sc) * gb_ref[...]).astype(BF16)
        x1_ref[...] = x_ref[...] + _dot(ycat_ref[...], wout_ref[...])

    full = lambda shape: pl.BlockSpec(shape, lambda i: (0,) * len(shape))
    return _hosted(
        body, carry, grid=(s // tm,),
        in_specs=[pl.BlockSpec((tm, n_in), lambda i: (i, 0)),
                  pl.BlockSpec((HALO, n_in), lambda i: (jnp.maximum(i * hb_blocks - 1, 0), 0)),
                  pl.BlockSpec((tm, d), lambda i: (i, 0)),
                  full((1, aw)), full((1, aw)), full((HEADS, CHUNK, CHUNK)), full((CHUNK, HEADS)),
                  full((3, bw)), full((1, aw)), full((1, bw)), full((d, d))],
        out_specs=[pl.BlockSpec((tm, d), lambda i: (i, 0)), pl.BlockSpec((tm, d), lambda i: (i, 0))],
        out_shape=(SDS((s, d), BF16), SDS((s, d), F32)),
        scratch_shapes=[pltpu.VMEM((tm, aw), F32)],
        compiler_params=_arb(1), name="mix_forward")(h, h, x, lng, lnb, w_sp, bt, conv_w, ga, gb, w_out)


def _attn_forward(x1, g, w_q, kv, w_o, tm, carry=()):
    s, d = x1.shape
    _, ml, xd = kv.shape
    scale = xd ** -0.5

    def body(x1_ref, g_ref, wq_ref, kv_ref, wo_ref, xn_ref, q_ref, o_ref, x2_ref):
        xv = x1_ref[...]
        xn = (xv * _rstd(xv) * g_ref[...]).astype(BF16)
        xn_ref[...] = xn
        q_ref[...] = _dot(xn, wq_ref[...]).astype(BF16)
        for hh in range(HEADS):
            cols = slice(hh * xd, (hh + 1) * xd)
            p = _softmax(_dot_nt(q_ref[:, cols], kv_ref[hh]) * scale)
            o_ref[:, cols] = _dot(p.astype(BF16), kv_ref[HEADS + hh]).astype(BF16)
        x2_ref[...] = xv + _dot(o_ref[...], wo_ref[...])

    tok = pl.BlockSpec((tm, d), lambda i: (i, 0))
    return _hosted(
        body, carry, grid=(s // tm,),
        in_specs=[tok, pl.BlockSpec((1, d), lambda i: (0, 0)), pl.BlockSpec((d, d), lambda i: (0, 0)),
                  pl.BlockSpec((2 * HEADS, ml, xd), lambda i: (0, 0, 0)), pl.BlockSpec((d, d), lambda i: (0, 0))],
        out_specs=[tok, tok, tok, tok],
        out_shape=(SDS((s, d), BF16), SDS((s, d), BF16), SDS((s, d), BF16), SDS((s, d), F32)),
        compiler_params=_arb(1), name="attn_forward")(x1, g, w_q, kv, w_o)


def _ffn_forward(x2, g, w_gu, w_down, tm):
    s, d = x2.shape
    _, nf, tf, _ = w_gu.shape

    def body(x2_ref, g_ref, wgu_ref, wd_ref, xn_ref, gu_ref, x3_ref):
        f = pl.program_id(1)

        @pl.when(f == 0)
        def _():
            xv = x2_ref[...]
            xn_ref[...] = (xv * _rstd(xv) * g_ref[...]).astype(BF16)
            x3_ref[...] = xv

        xn = xn_ref[...]
        gate = _dot_nt(xn, wgu_ref[0])
        up = _dot_nt(xn, wgu_ref[1])
        gu_ref[0] = gate.astype(BF16)
        gu_ref[1] = up.astype(BF16)
        act = (gate * _sigmoid(gate) * up).astype(BF16)
        x3_ref[...] += _dot(act, wd_ref[...])

    tok = pl.BlockSpec((tm, d), lambda i, f: (i, 0))
    return _pcall(
        body, grid=(s // tm, nf),
        in_specs=[tok, pl.BlockSpec((1, d), lambda i, f: (0, 0)),
                  pl.BlockSpec((2, None, tf, d), lambda i, f: (0, f, 0, 0)),
                  pl.BlockSpec((tf, d), lambda i, f: (f, 0))],
        out_specs=[tok, pl.BlockSpec((2, None, tm, tf), lambda i, f: (0, f, i, 0)), tok],
        out_shape=(SDS((s, d), BF16), SDS((2, nf, s, tf), BF16), SDS((s, d), F32)),
        compiler_params=_arb(2), name="ffn_forward")(x2, g, w_gu, w_down)


def _final_backward(x3, target, g_final, tm):
    s, d = x3.shape

    def body(x3_ref, tgt_ref, gf_ref, loss_ref, dgf_ref, dx3_ref, dx3b_ref):
        @pl.when(pl.program_id(0) == 0)
        def _():
            loss_ref[...] = jnp.zeros_like(loss_ref)
            dgf_ref[...] = jnp.zeros_like(dgf_ref)

        xv = x3_ref[...]
        r = _rstd(xv)
        diff = xv * r * gf_ref[...] - tgt_ref[...]
        loss_ref[...] += 0.5 * jnp.sum(jnp.sum(diff * diff, axis=-1, keepdims=True), axis=0, keepdims=True) * (1.0 / d)
        dx3, dgf = _rms_bwd(diff * (1.0 / d), xv, r, gf_ref[...])
        dgf_ref[...] += dgf
        dx3_ref[...] = dx3
        dx3b_ref[...] = dx3.astype(BF16)

    tok = pl.BlockSpec((tm, d), lambda i: (i, 0))
    vec = pl.BlockSpec((1, d), lambda i: (0, 0))
    return _pcall(
        body, grid=(s // tm,), in_specs=[tok, tok, vec],
        out_specs=[pl.BlockSpec((SUB, LANES), lambda i: (0, 0)), vec, tok, tok],
        out_shape=(SDS((SUB, LANES), F32), SDS((1, d), F32), SDS((s, d), F32), SDS((s, d), BF16)),
        compiler_params=_arb(1), name="final_backward")(x3, target, g_final)


def _swiglu_backward(dx3b, gu, w_gu, w_down, tm):
    s, d = dx3b.shape
    _, nf, tf, _ = w_gu.shape

    def body(dx3b_ref, gu_ref, wgu_ref, wd_ref, act_ref, dgu_ref, dxn_ref):
        dact = _dot_nt(dx3b_ref[...], wd_ref[...])
        gv = gu_ref[0].astype(F32)
        uv = gu_ref[1].astype(F32)
        sg = _sigmoid(gv)
        silu = gv * sg
        act_ref[...] = (silu * uv).astype(BF16)
        dgate = (dact * uv * (sg * (1.0 + gv * (1.0 - sg)))).astype(BF16)
        dup = (dact * silu).astype(BF16)
        dgu_ref[0] = dgate
        dgu_ref[1] = dup
        part = _dot(dgate, wgu_ref[0]) + _dot(dup, wgu_ref[1])

        @pl.when(pl.program_id(1) == 0)
        def _():
            dxn_ref[...] = part

        @pl.when(pl.program_id(1) > 0)
        def _():
            dxn_ref[...] += part

    tok = pl.BlockSpec((tm, d), lambda i, f: (i, 0))
    pair = pl.BlockSpec((2, None, tm, tf), lambda i, f: (0, f, i, 0))
    return _pcall(
        body, grid=(s // tm, nf),
        in_specs=[tok, pair, pl.BlockSpec((2, None, tf, d), lambda i, f: (0, f, 0, 0)),
                  pl.BlockSpec((tf, d), lambda i, f: (f, 0))],
        out_specs=[pl.BlockSpec((None, tm, tf), lambda i, f: (f, i, 0)), pair, tok],
        out_shape=(SDS((nf, s, tf), BF16), SDS((2, nf, s, tf), BF16), SDS((s, d), F32)),
        compiler_params=_arb(2), name="swiglu_backward")(dx3b, gu, w_gu, w_down)


def _attn_backward(dx3, dxn3, x2, g_ffn, x1, g, q, kv, w_q, w_o, tm, carry=()):
    s, d = x1.shape
    _, ml, xd = kv.shape
    scale = xd ** -0.5

    def body(dx3_ref, dxn3_ref, x2_ref, g2_ref, x1_ref, g_ref, q_ref, kv_ref, wq_ref, wo_ref,
             dx2b_ref, dq_ref, dx1_ref, dkv_ref, dg_ref, dg2_ref, do_s):
        i = pl.program_id(0)

        @pl.when(i == 0)
        def _():
            dkv_ref[...] = jnp.zeros_like(dkv_ref)
            dg_ref[...] = jnp.zeros_like(dg_ref)
            dg2_ref[...] = jnp.zeros_like(dg2_ref)

        x2v = x2_ref[...]
        dx2n, dg2 = _rms_bwd(dxn3_ref[...], x2v, _rstd(x2v), g2_ref[...])
        dg2_ref[...] += dg2
        dx2 = dx3_ref[...] + dx2n
        dx2b_ref[...] = dx2.astype(BF16)
        do_s[...] = _dot_nt(dx2b_ref[...], wo_ref[...]).astype(BF16)
        for hh in range(HEADS):
            kc = slice(hh * xd, (hh + 1) * xd)
            qh = q_ref[:, kc]
            kh = kv_ref[hh]
            doh = do_s[:, kc]
            p = _softmax(_dot_nt(qh, kh) * scale)
            dp = _dot_nt(doh, kv_ref[HEADS + hh])
            dkv_ref[HEADS + hh] += _dot_tn(p.astype(BF16), doh)
            ds = (p * (dp - jnp.sum(dp * p, axis=-1, keepdims=True)) * scale).astype(BF16)
            dq_ref[:, kc] = _dot(ds, kh).astype(BF16)
            dkv_ref[hh] += _dot_tn(ds, qh)
        dxn = _dot_nt(dq_ref[...], wq_ref[...])
        xv = x1_ref[...]
        dx, dg = _rms_bwd(dxn, xv, _rstd(xv), g_ref[...])
        dg_ref[...] += dg
        dx1_ref[...] = dx2 + dx

    tok = pl.BlockSpec((tm, d), lambda i: (i, 0))
    vec = pl.BlockSpec((1, d), lambda i: (0, 0))
    sq = pl.BlockSpec((d, d), lambda i: (0, 0))
    kvs = pl.BlockSpec((2 * HEADS, ml, xd), lambda i: (0, 0, 0))
    return _hosted(
        body, carry, grid=(s // tm,),
        in_specs=[tok, tok, tok, vec, tok, vec, tok, kvs, sq, sq],
        out_specs=[tok, tok, tok, kvs, vec, vec],
        out_shape=(SDS((s, d), BF16), SDS((s, d), BF16), SDS((s, d), F32), SDS((2 * HEADS, ml, xd), F32), SDS((1, d), F32),
                   SDS((1, d), F32)),
        scratch_shapes=[pltpu.VMEM((tm, d), BF16)],
        compiler_params=_arb(1), name="attn_backward")(dx3, dxn3, x2, g_ffn, x1, g, q, kv, w_q, w_o)


def _kv_backward(dkv, memn, mem, g_mem, w_kv):
    ml, d = mem.shape
    xd = w_kv.shape[2]

    def body(dkv_ref, memn_ref, mem_ref, g_ref, w_ref, dw_ref, dg_ref):
        dmemn = jnp.zeros((ml, d), F32)
        for j in range(2 * HEADS):
            dkvb = dkv_ref[j].astype(BF16)
            dw_ref[j] = _dot_tn(memn_ref[...], dkvb)
            dmemn = dmemn + _dot_nt(dkvb, w_ref[j])
        x = mem_ref[...]
        dg_ref[...] = jnp.sum(dmemn * (x * _rstd(x)), axis=0, keepdims=True)

    return _pcall(body, out_shape=(SDS((2 * HEADS, d, xd), F32), SDS((1, d), F32)), name="kv_backward")(dkv, memn, mem, g_mem, w_kv)


def _mix_backward(dx1, x, g_mix, h, lng, lnb, w_sp, bt, conv_w, ga, gb, w_out, w_in, tm, carry=()):
    s, d = x.shape
    n_in = h.shape[1]
    aw = lng.shape[1]
    bw = d - aw
    hd = aw // HEADS
    in_a = 2 * aw
    hb_blocks = tm // HALO
    last_blk = s // HALO - 1
    nt = s // tm
    te = tm + HALO
    tee = tm + 2 * HALO

    def body(dx1_ref, dx1n_ref, x_ref, gm_ref, h_ref, hp_ref, hn_ref, lng_ref, lnb_ref, wsp_ref, bt_ref, cw_ref,
             ga_ref, gb_ref, wout_ref, win_ref,
             dx1b_ref, dh_ref, dx_ref, dga_ref, dgb_ref, dcw_ref, dlng_ref, dlnb_ref, dwsp_ref, dbs_ref, dgm_ref,
             mixed_s, dvln_s):
        i = pl.program_id(0)

        @pl.when(i == 0)
        def _():
            for ref in (dga_ref, dgb_ref, dcw_ref, dlng_ref, dlnb_ref, dwsp_ref, dbs_ref, dgm_ref):
                ref[...] = jnp.zeros_like(ref)

        mask = _tril_mask()
        wm = [(wsp_ref[hh] * mask).astype(BF16) for hh in range(HEADS)]
        hv = h_ref[...]
        dx1 = dx1_ref[...]
        dx1b_ref[...] = dx1.astype(BF16)
        dx1e = jnp.concatenate([dx1, dx1n_ref[...]], axis=0).astype(BF16)
        dycat = _dot_nt(dx1e, wout_ref[...])

        hbe = jnp.concatenate([hp_ref[:, in_a:], hv[:, in_a:], hn_ref[:, in_a:]], axis=0)
        row = lax.broadcasted_iota(jnp.int32, (tee, 1), 0)
        zext = hbe[:, bw:2 * bw] * hbe[:, 2 * bw:]
        zext = jnp.where((i == 0) & (row < HALO), 0.0, zext)
        z2e, z1e = _conv_taps(zext)
        cw = cw_ref[...]
        conv_e = (cw[0:1] * z2e + cw[1:2] * z1e + cw[2:3] * zext)[HALO:]
        gate_b_e = hbe[HALO:, :bw]
        sc_e = gate_b_e * conv_e
        rb = _rstd(sc_e)
        dyb = dycat[:, aw:]
        gdy = dyb * gb_ref[...]
        dsc_e = rb * gdy - sc_e * (rb * rb * rb) * (jnp.sum(gdy * sc_e, axis=-1, keepdims=True) * (1.0 / bw))
        dgb_ref[...] += jnp.sum((dyb * (sc_e * rb))[:tm], axis=0, keepdims=True)
        dconv_e = dsc_e * gate_b_e
        dconv_e = jnp.where((i == nt - 1) & (row[:te] >= tm), 0.0, dconv_e)
        dconv = dconv_e[:tm]
        dc1 = pltpu.roll(dconv_e, te - 1, 0)[:tm]
        dc2 = pltpu.roll(dconv_e, te - 2, 0)[:tm]
        dz = cw[2:3] * dconv + cw[1:2] * dc1 + cw[0:1] * dc2
        z = zext[HALO:HALO + tm]
        z1 = z1e[HALO:HALO + tm]
        z2 = z2e[HALO:HALO + tm]
        dcw_ref[0:1, :] += jnp.sum(dconv * z2, axis=0, keepdims=True)
        dcw_ref[1:2, :] += jnp.sum(dconv * z1, axis=0, keepdims=True)
        dcw_ref[2:3, :] += jnp.sum(dconv * z, axis=0, keepdims=True)
        dh_ref[:, in_a:in_a + bw] = (dsc_e[:tm] * conv_e[:tm]).astype(BF16)
        dh_ref[:, in_a + bw:in_a + 2 * bw] = (dz * hv[:, in_a + 2 * bw:]).astype(BF16)
        dh_ref[:, in_a + 2 * bw:] = (dz * hv[:, in_a + bw:in_a + 2 * bw]).astype(BF16)

        ha = hv[:, :in_a]
        th, u, xhat, rl, vln = _sgu_forward(ha, lng_ref[...], lnb_ref[...], wm, bt_ref[...], mixed_s)
        mixed = mixed_s[...]
        sg = u * mixed
        dsg, dga = _rms_bwd(dycat[:tm, :aw], sg, _rstd(sg), ga_ref[...])
        dga_ref[...] += dga
        du = dsg * mixed
        dmixed = dsg * u
        dmb = dmixed.astype(BF16)
        for n in range(tm // CHUNK):
            rows = slice(n * CHUNK, (n + 1) * CHUNK)
            dbs_ref[...] += dmixed[rows]
            for hh in range(HEADS):
                cols = slice(hh * hd, (hh + 1) * hd)
                dvln_s[rows, cols] = _dot_tn(wm[hh], dmb[rows, cols])
                dwsp_ref[hh] += mask * _dot_nt(dmb[rows, cols], vln[rows, cols])
        dvln = dvln_s[...]
        dlng_ref[...] += jnp.sum(dvln * xhat, axis=0, keepdims=True)
        dlnb_ref[...] += jnp.sum(dvln, axis=0, keepdims=True)
        dxh = dvln * lng_ref[...]
        dv = rl * (dxh - jnp.mean(dxh, axis=-1, keepdims=True) - xhat * jnp.mean(dxh * xhat, axis=-1, keepdims=True))
        dh_ref[:, :in_a] = (jnp.concatenate([du, dv], axis=-1) * _gelu_grad(ha, th)).astype(BF16)

        dxn = _dot(dh_ref[...], win_ref[...])
        xv = x_ref[...]
        dx, dgm = _rms_bwd(dxn, xv, _rstd(xv), gm_ref[...])
        dgm_ref[...] += dgm
        dx_ref[...] = dx1 + dx

    full = lambda shape: pl.BlockSpec(shape, lambda i: (0,) * len(shape))
    tok = pl.BlockSpec((tm, d), lambda i: (i, 0))
    nxt = lambda i: (jnp.minimum((i + 1) * hb_blocks, last_blk), 0)
    prv = lambda i: (jnp.maximum(i * hb_blocks - 1, 0), 0)
    return _hosted(
        body, carry, grid=(nt,),
        in_specs=[tok, pl.BlockSpec((HALO, d), nxt), tok, full((1, d)),
                  pl.BlockSpec((tm, n_in), lambda i: (i, 0)), pl.BlockSpec((HALO, n_in), prv), pl.BlockSpec((HALO, n_in), nxt),
                  full((1, aw)), full((1, aw)), full((HEADS, CHUNK, CHUNK)), full((CHUNK, HEADS)), full((3, bw)),
                  full((1, aw)), full((1, bw)), full((d, d)), full((n_in, d))],
        out_specs=[tok, pl.BlockSpec((tm, n_in), lambda i: (i, 0)), tok,
                   full((1, aw)), full((1, bw)), full((SUB, bw)), full((1, aw)), full((1, aw)),
                   full((HEADS, CHUNK, CHUNK)), full((CHUNK, aw)), full((1, d))],
        out_shape=(SDS((s, d), BF16), SDS((s, n_in), BF16), SDS((s, d), F32),
                   SDS((1, aw), F32), SDS((1, bw), F32), SDS((SUB, bw), F32), SDS((1, aw), F32), SDS((1, aw), F32),
                   SDS((HEADS, CHUNK, CHUNK), F32), SDS((CHUNK, aw), F32), SDS((1, d), F32)),
        scratch_shapes=[pltpu.VMEM((tm, aw), F32), pltpu.VMEM((tm, aw), F32)],
        compiler_params=_arb(1), name="mix_backward")(dx1, dx1, x, g_mix, h, h, h, lng, lnb, w_sp, bt, conv_w, ga, gb, w_out, w_in)


def _bias_grad(dbs):
    aw = dbs.shape[1]
    hd = aw // HEADS

    def body(dbs_ref, out_ref):
        ones = jnp.ones((SUB, hd), F32)
        for hh in range(HEADS):
            r = lax.dot_general(ones, dbs_ref[:, hh * hd:(hh + 1) * hd], (((1,), (1,)), ((), ())),
                                precision=lax.Precision.HIGHEST, preferred_element_type=F32)
            out_ref[hh:hh + 1, :] = r[0:1]

    return _pcall(body, out_shape=SDS((HEADS, CHUNK), F32), name="bias_grad")(dbs)


def _wgrad_body(a_ref, b_ref, o_ref):
    o_ref[...] = _dot_tn(a_ref[...], b_ref[...])


def _wgrad(a, b, name, carry=()):
    k, m = a.shape
    n = b.shape[1]
    tm = _tile(m, 512, LANES)
    tn = _tile(n, 1024, LANES)
    return _hosted(
        functools.partial(_wgrad_body), carry, grid=(m // tm, n // tn),
        in_specs=[pl.BlockSpec((k, tm), lambda i, j: (0, i)), pl.BlockSpec((k, tn), lambda i, j: (0, j))],
        out_specs=pl.BlockSpec((tm, tn), lambda i, j: (i, j)),
        out_shape=SDS((m, n), F32), compiler_params=_arb(2), name=name)(a, b)


def _wgrad_blocked_lhs(a, b, name, carry=()):
    nb, k, t = a.shape
    n = b.shape[1]
    tn = _tile(n, 1024, LANES)
    return _hosted(
        functools.partial(_wgrad_body), carry, grid=(nb, n // tn),
        in_specs=[pl.BlockSpec((None, k, t), lambda i, j: (i, 0, 0)), pl.BlockSpec((k, tn), lambda i, j: (0, j))],
        out_specs=pl.BlockSpec((t, tn), lambda i, j: (i, j)),
        out_shape=SDS((nb * t, n), F32), compiler_params=_arb(2), name=name)(a, b)


def _wgrad_blocked_rhs(a, b, name, carry=()):
    k, m = a.shape
    nb, _, t = b.shape
    tm = _tile(m, 512, LANES)
    return _hosted(
        functools.partial(_wgrad_body), carry, grid=(m // tm, nb),
        in_specs=[pl.BlockSpec((k, tm), lambda i, j: (0, i)), pl.BlockSpec((None, k, t), lambda i, j: (j, 0, 0))],
        out_specs=pl.BlockSpec((None, tm, t), lambda i, j: (j, i, 0)),
        out_shape=SDS((nb, m, t), F32), compiler_params=_arb(2), name=name)(a, b)


def _unblock_cols(wb, name, carry=()):
    nb, r, t = wb.shape
    tr = _tile(r, 256, 16)

    def body(w_ref, o_ref):
        o_ref[...] = jnp.concatenate([w_ref[j].astype(F32) for j in range(nb)], axis=-1).astype(o_ref.dtype)

    return _hosted(
        body, carry, grid=(r // tr,),
        in_specs=[pl.BlockSpec((nb, tr, t), lambda i: (0, i, 0))], out_specs=pl.BlockSpec((tr, nb * t), lambda i: (i, 0)),
        out_shape=SDS((r, nb * t), wb.dtype), compiler_params=_arb(1), name=name)(wb)


def _block_cols(w, nb, name, carry=()):
    r, n = w.shape
    t = n // nb
    tr = _tile(r, 256, 16)

    def body(w_ref, o_ref):
        wv = w_ref[...]
        for j in range(nb):
            o_ref[j] = wv[:, j * t:(j + 1) * t]

    return _hosted(
        body, carry, grid=(r // tr,),
        in_specs=[pl.BlockSpec((tr, n), lambda i: (i, 0))], out_specs=pl.BlockSpec((nb, tr, t), lambda i: (0, i, 0)),
        out_shape=SDS((nb, r, t), w.dtype), compiler_params=_arb(1), name=name)(w)


def _place():
    x, y, c = lax.axis_index("x"), lax.axis_index("y"), lax.axis_index("c")
    return x, y, c, [(1 - x, y), (x, 1 - y), (1 - x, 1 - y)]


def _all_gather(shards):
    n = len(shards)
    slots = 9
    cut = [(s.shape[0] // 32) * 16 for s in shards]

    def build(ins, outs, sems):
        send_sems, recv_sems, local_sems = sems
        x, y, c, _ = _place()
        me, sib, xn, yn, dg = (x, y, c), (x, y, 1 - c), (1 - x, y, c), (x, 1 - y, c), (1 - x, 1 - y, c)
        other = lambda p: (p[0], p[1], 1 - p[2])

        def rows(a, p, part=None):
            ref = outs[a].at[4 * p[0] + 2 * p[1] + p[2]]
            if part is None or cut[a] == 0:
                return ref if part in (None, 0) else None
            return ref.at[pl.ds(0, cut[a])] if part == 0 else ref.at[pl.ds(cut[a], shards[a].shape[0] - cut[a])]

        def copy(a, k, ref, to, src=None):
            if ref is None:
                return None
            return pltpu.make_async_remote_copy(
                src_ref=ref if src is None else src, dst_ref=ref, send_sem=send_sems.at[slots * a + k],
                recv_sem=recv_sems.at[slots * a + k], device_id=to, device_id_type=MESH)

        def real(cps):
            return [cp for cp in cps if cp is not None]

        class Copies:
            own = lambda a: [copy(a, 1, rows(a, me), xn, ins[a]), copy(a, 2, rows(a, me), yn, ins[a]),
                             copy(a, 0, rows(a, me), sib, ins[a])]
            local = lambda a: pltpu.make_async_copy(ins[a], rows(a, me), local_sems.at[a])
            from_x = lambda a: copy(a, 1, rows(a, xn), me)
            from_y = lambda a: copy(a, 2, rows(a, yn), me)
            after_x = lambda a: real([copy(a, 4, rows(a, xn, 1), yn), copy(a, 5, rows(a, xn), sib)])
            after_y = lambda a: real([copy(a, 3, rows(a, yn, 0), xn), copy(a, 6, rows(a, yn), sib)])
            diag_in = lambda a: real([copy(a, 3, rows(a, dg, 0), me), copy(a, 4, rows(a, dg, 1), me)])
            diag_on = lambda a: real([copy(a, 7, rows(a, dg, 0), sib), copy(a, 8, rows(a, dg, 1), sib)])
            from_sib = lambda a: real([copy(a, 0, rows(a, sib), me), copy(a, 5, rows(a, other(xn)), me),
                                       copy(a, 6, rows(a, other(yn)), me), copy(a, 7, rows(a, other(dg), 0), me),
                                       copy(a, 8, rows(a, other(dg), 1), me)])

        return Copies

    def start(ins, outs, sems):
        cps = build(ins, outs, sems)
        for a in range(n):
            for cp in cps.own(a):
                cp.start()
        for a in range(n):
            cps.local(a).start()

    def relay(ins, outs, sems):
        cps = build(ins, outs, sems)
        for a in range(n):
            cps.from_x(a).wait_recv()
            for cp in cps.after_x(a):
                cp.start()
            cps.from_y(a).wait_recv()
            for cp in cps.after_y(a):
                cp.start()

    def finish(ins, outs, sems):
        cps = build(ins, outs, sems)
        for a in range(n):
            for arrived, onward in zip(cps.diag_in(a), cps.diag_on(a)):
                arrived.wait_recv()
                onward.start()
        for a in range(n):
            for cp in cps.from_sib(a):
                cp.wait_recv()
            for cp in cps.own(a) + cps.after_x(a) + cps.after_y(a) + cps.diag_on(a):
                cp.wait_send()
            cps.local(a).wait()

    return _Exchange(shards, [SDS((N_DEV,) + s.shape, s.dtype) for s in shards],
                     [pltpu.SemaphoreType.DMA((slots * n,)), pltpu.SemaphoreType.DMA((slots * n,)),
                      pltpu.SemaphoreType.DMA((n,))], start, finish, relay)


def _swap_exchange(ins, out_shape, per, copies):
    def start(i, o, sems):
        for cp in copies(i, o, sems):
            cp.start()

    def finish(i, o, sems):
        for cp in copies(i, o, sems):
            cp.wait()

    n = per * len(ins)
    return _Exchange(ins, out_shape, [pltpu.SemaphoreType.DMA((n,)), pltpu.SemaphoreType.DMA((n,))], start, finish)


def _exchange_c(gs):
    def copies(ins, outs, sems):
        x, y, c, _ = _place()
        return [pltpu.make_async_remote_copy(
                    src_ref=ins[a].at[2 * k + 1 - c], dst_ref=outs[a].at[k],
                    send_sem=sems[0].at[4 * a + k], recv_sem=sems[1].at[4 * a + k],
                    device_id=(x, y, 1 - c), device_id_type=MESH)
                for a in range(len(gs)) for k in range(4)]

    return _swap_exchange(gs, [SDS((4,) + g.shape[1:], g.dtype) for g in gs], 4, copies)


def _exchange_xy(sends):
    def copies(ins, outs, sems):
        x, y, c, chips = _place()
        return [pltpu.make_async_remote_copy(
                    src_ref=ins[a].at[t], dst_ref=outs[a].at[t],
                    send_sem=sems[0].at[3 * a + t], recv_sem=sems[1].at[3 * a + t],
                    device_id=(*chips[t], c), device_id_type=MESH)
                for a in range(len(sends)) for t in range(3)]

    return _swap_exchange(sends, [SDS(s.shape, s.dtype) for s in sends], 3, copies)


def _rs_combine(g, recv, pos, name, carry=()):
    _, r, cdim = g.shape
    tr = _tile(r, 256, 16)

    def body(pos_ref, g0, r0, g1, r1, g2, r2, g3, r3, keep_ref, send_ref):
        keep_ref[...] = g0[...] + r0[...]
        send_ref[0] = (g1[...] + r1[...]).astype(BF16)
        send_ref[1] = (g2[...] + r2[...]).astype(BF16)
        send_ref[2] = (g3[...] + r3[...]).astype(BF16)

    def k_of(p, t):
        px = p[0] if t in (0, 2) else 1 - p[0]
        py = p[1] if t in (0, 1) else 1 - p[1]
        return 2 * px + py

    blk = (None, tr, cdim)
    in_specs = []
    for t in range(4):
        in_specs.append(pl.BlockSpec(blk, functools.partial(lambda j, p, t: (2 * k_of(p, t) + p[2], j, 0), t=t)))
        in_specs.append(pl.BlockSpec(blk, functools.partial(lambda j, p, t: (k_of(p, t), j, 0), t=t)))
    return _hosted(
        body, carry, n_prefetch=1, out_shape=(SDS((r, cdim), F32), SDS((3, r, cdim), BF16)),
        grid=(r // tr,), in_specs=in_specs,
        out_specs=[pl.BlockSpec((tr, cdim), lambda j, p: (j, 0)), pl.BlockSpec((3, tr, cdim), lambda j, p: (0, j, 0))],
        compiler_params=_arb(1), name=name)(pos, g, recv, g, recv, g, recv, g, recv)


def _adamw_shard(keep, recv, w, m, v, name):
    r, cdim = w.shape
    tr = _tile(r, 256, 16)

    def body(k_ref, r_ref, w_ref, m_ref, v_ref, g_ref, d_ref, nm_ref, nv_ref):
        g = ((k_ref[...] + r_ref[0].astype(F32)) + r_ref[1].astype(F32)) + r_ref[2].astype(F32)
        g_ref[...] = g
        d_ref[...], nm_ref[...], nv_ref[...] = _adamw(w_ref[...], g, m_ref[...], v_ref[...])

    blk = pl.BlockSpec((tr, cdim), lambda j: (j, 0))
    out = SDS((r, cdim), F32)
    return _pcall(body, grid=(r // tr,), in_specs=[blk, pl.BlockSpec((3, tr, cdim), lambda j: (0, j, 0)), blk, blk, blk],
                  out_specs=[blk] * 4, out_shape=(out,) * 4, compiler_params=_arb(1), name=name)(keep, recv, w, m, v)


def _adamw_small(gathered, seg, params, conv_rows):
    names = list(params)
    c0, cn = conv_rows

    def body(*refs):
        gat_ref = refs[0]
        ins = refs[1:1 + 3 * len(names)]
        outs = refs[1 + 3 * len(names):]

        def total(r0, rn):
            tot = gat_ref[0, r0:r0 + rn, :]
            for dev in range(1, N_DEV):
                tot = tot + gat_ref[dev, r0:r0 + rn, :]
            return tot

        for k, nm in enumerate(names):
            g = total(*seg[nm])
            w_ref, m_ref, v_ref = ins[3 * k:3 * k + 3]
            g_ref, d_ref, nm_ref, nv_ref = outs[4 * k:4 * k + 4]
            g_ref[...] = g
            d_ref[...], nm_ref[...], nv_ref[...] = _adamw(w_ref[...], g, m_ref[...], v_ref[...])
        outs[-2][...] = total(c0, cn)
        outs[-1][...] = total(*seg["loss"])

    flat_in = [a for nm in names for a in params[nm]]
    out_shape = []
    for nm in names:
        out_shape += [SDS(params[nm][0].shape, F32)] * 4
    out_shape += [SDS((cn, LANES), F32), SDS((seg["loss"][1], LANES), F32)]
    res = _pcall(body, out_shape=tuple(out_shape), name="adamw_small")(gathered, *flat_in)
    per = {nm: res[4 * k:4 * k + 4] for k, nm in enumerate(names)}
    return per, res[-2], res[-1]


def _adamw_one(w, g, m, v, name):
    def body(w_ref, g_ref, m_ref, v_ref, d_ref, nm_ref, nv_ref):
        d_ref[...], nm_ref[...], nv_ref[...] = _adamw(w_ref[...], g_ref[...], m_ref[...], v_ref[...])

    return _pcall(body, out_shape=(SDS(w.shape, F32),) * 3, name=name)(w, g, m, v)


def _rows128(a):
    return a.reshape(-1, LANES)


def _pack_small(gs, loss_tile):
    seg, pieces, row = {}, [], 0
    for nm in SMALL + ("conv_w", "loss"):
        piece = loss_tile if nm == "loss" else _rows128(gs[nm])
        rn = _round_up(piece.shape[0], SUB)
        pieces.append(jnp.pad(piece, ((0, rn - piece.shape[0]), (0, 0))))
        seg[nm] = (row, piece.shape[0])
        row += rn
    return jnp.concatenate(pieces, axis=0), seg


def _step(x, mem, target, wb, conv_w, sp, pos):
    s, d = x.shape
    tm = min(TOKEN_TILE, s)
    tm_wide = min(2 * TOKEN_TILE, s)
    rows = lambda w8: w8.reshape(-1, w8.shape[2])
    shards = lambda g: g.reshape((N_DEV, -1) + g.shape[1:])
    bt = sp["b_spatial"].T

    (w_in8, conv8), = _run_exchanges([_all_gather([wb["w_in"], conv_w])], "gather_w_in")
    conv_full = conv8.transpose(1, 0, 2).reshape(3, -1)
    w_in_t = rows(w_in8)
    (xn1, h), ((w_out8, w_kv8, w_q8),) = _in_forward(
        x, sp["ln_mix_g"], w_in_t, tm_wide, carry=[_all_gather([wb["w_out"], wb["w_kv"], wb["w_q"]])])
    w_out = rows(w_out8)
    (ycat, x1), ((w_o8, w_down8),) = _mix_forward(
        h, x, sp["sgu_ln_g"], sp["sgu_ln_b"], sp["w_spatial"], bt, conv_full, sp["grp_norm_a"], sp["grp_norm_b"], w_out, tm,
        carry=[_all_gather([wb["w_o"], wb["w_down"]])])
    w_q, w_o, w_down = rows(w_q8), rows(w_o8), rows(w_down8)
    memn, kv = _kv_forward(mem, sp["ln_mem_g"], w_kv8)
    (xn2, q, o, x2), ((w_gu8,),) = _attn_forward(
        x1, sp["ln_attn_g"], w_q, kv, w_o, tm_wide, carry=[_all_gather([wb["w_gate_up"]])])
    w_gu = w_gu8.reshape((2, N_DEV // 2) + w_gu8.shape[1:])
    xn3, gu, x3 = _ffn_forward(x2, sp["ln_ffn_g"], w_gu, w_down, tm_wide)

    loss, d_lnf, dx3, dx3b = _final_backward(x3, target, sp["ln_final_g"], tm_wide)
    act, dgu, dxn3 = _swiglu_backward(dx3b, gu, w_gu, w_down, tm_wide)
    part = {}
    g_gu, _ = _wgrad_blocked_lhs(dgu.reshape((N_DEV,) + dgu.shape[2:]), xn3, "wgrad_gate_up")
    g_gu = shards(g_gu)
    g_down, ((rc_gu,),) = _wgrad_blocked_lhs(act, dx3b, "wgrad_down", carry=[_exchange_c([g_gu])])
    g_down = shards(g_down)
    (keep_gu, send_gu), _ = _rs_combine(g_gu, rc_gu, pos, "rs_combine_w_gate_up")
    (dx2b, dq, dx1, dkv, d_lnattn, d_lnffn), ((rxy_gu,), (rc_down,)) = _attn_backward(
        dx3, dxn3, x2, sp["ln_ffn_g"], x1, sp["ln_attn_g"], q, kv, w_q, w_o, tm,
        carry=[_exchange_xy([send_gu]), _exchange_c([g_down])])
    part["w_gate_up"] = (keep_gu, rxy_gu)
    (keep_down, send_down), _ = _rs_combine(g_down, rc_down, pos, "rs_combine_w_down")
    g_o, _ = _wgrad(o, dx2b, "wgrad_o")
    g_o = shards(g_o)
    g_q, ((rc_o,),) = _wgrad(xn2, dq, "wgrad_q", carry=[_exchange_c([g_o])])
    g_q = shards(g_q)
    g_kv, d_lnmem = _kv_backward(dkv, memn, mem, sp["ln_mem_g"], w_kv8)
    (keep_o, send_o), ((rc_q,),) = _rs_combine(g_o, rc_o, pos, "rs_combine_w_o", carry=[_exchange_c([g_q])])
    (keep_q, send_q), _ = _rs_combine(g_q, rc_q, pos, "rs_combine_w_q")
    ((dx1b, dh, dx, d_ga, d_gb, d_cw, d_lng, d_lnb, d_wsp, d_bs, d_lnmix),
     ((rxy_down, rxy_o, rxy_q), (rc_kv,))) = _mix_backward(
        dx1, x, sp["ln_mix_g"], h, sp["sgu_ln_g"], sp["sgu_ln_b"], sp["w_spatial"], bt, conv_full,
        sp["grp_norm_a"], sp["grp_norm_b"], w_out, w_in_t, tm,
        carry=[_exchange_xy([send_down, send_o, send_q]), _exchange_c([g_kv])])
    part["w_down"], part["w_o"], part["w_q"] = (keep_down, rxy_down), (keep_o, rxy_o), (keep_q, rxy_q)
    (keep_kv, send_kv), _ = _rs_combine(g_kv, rc_kv, pos, "rs_combine_w_kv")
    gs = {"ln_mix_g": d_lnmix, "sgu_ln_g": d_lng, "sgu_ln_b": d_lnb, "w_spatial": d_wsp, "b_spatial": _bias_grad(d_bs),
          "conv_w": d_cw[:3], "grp_norm_a": d_ga, "grp_norm_b": d_gb, "ln_attn_g": d_lnattn, "ln_mem_g": d_lnmem,
          "ln_ffn_g": d_lnffn, "ln_final_g": d_lnf}
    packed, seg = _pack_small(gs, loss)
    g_in, ((rxy_kv,), (small_all,)) = _wgrad(
        dh, xn1, "wgrad_in", carry=[_exchange_xy([send_kv]), _all_gather([packed])])
    g_in = shards(g_in)
    part["w_kv"] = (keep_kv, rxy_kv)
    g_out, ((rc_in,),) = _wgrad(ycat, dx1b, "wgrad_out", carry=[_exchange_c([g_in])])
    g_out = shards(g_out)
    (keep_in, send_in), ((rc_out,),) = _rs_combine(g_in, rc_in, pos, "rs_combine_w_in", carry=[_exchange_c([g_out])])
    (keep_out, send_out), _ = _rs_combine(g_out, rc_out, pos, "rs_combine_w_out")
    (rxy_in, rxy_out), = _run_exchanges([_exchange_xy([send_in, send_out])], "exchange_xy_w_in_w_out")
    part["w_in"], part["w_out"] = (keep_in, rxy_in), (keep_out, rxy_out)
    return dx, part, small_all, seg


def kernel(x, mem, ln_mix_g, w_in, sgu_ln_g, sgu_ln_b, w_spatial, b_spatial, conv_w, grp_norm_a, grp_norm_b, w_out, ln_attn_g, ln_mem_g, w_q, w_kv, w_o, ln_ffn_g, w_gate_up, w_down, ln_final_g, loss_target, m_ln_mix_g, m_w_in, m_sgu_ln_g, m_sgu_ln_b, m_w_spatial, m_b_spatial, m_conv_w, m_grp_norm_a, m_grp_norm_b, m_w_out, m_ln_attn_g, m_ln_mem_g, m_w_q, m_w_kv, m_w_o, m_ln_ffn_g, m_w_gate_up, m_w_down, m_ln_final_g, v_ln_mix_g, v_w_in, v_sgu_ln_g, v_sgu_ln_b, v_w_spatial, v_b_spatial, v_conv_w, v_grp_norm_a, v_grp_norm_b, v_w_out, v_ln_attn_g, v_ln_mem_g, v_w_q, v_w_kv, v_w_o, v_ln_ffn_g, v_w_gate_up, v_w_down, v_ln_final_g):
    order = ["ln_mix_g", "w_in", "sgu_ln_g", "sgu_ln_b", "w_spatial", "b_spatial", "conv_w", "grp_norm_a", "grp_norm_b",
             "w_out", "ln_attn_g", "ln_mem_g", "w_q", "w_kv", "w_o", "ln_ffn_g", "w_gate_up", "w_down", "ln_final_g"]
    W = dict(ln_mix_g=ln_mix_g, w_in=w_in, sgu_ln_g=sgu_ln_g, sgu_ln_b=sgu_ln_b, w_spatial=w_spatial, b_spatial=b_spatial,
             conv_w=conv_w, grp_norm_a=grp_norm_a, grp_norm_b=grp_norm_b, w_out=w_out, ln_attn_g=ln_attn_g,
             ln_mem_g=ln_mem_g, w_q=w_q, w_kv=w_kv, w_o=w_o, ln_ffn_g=ln_ffn_g, w_gate_up=w_gate_up, w_down=w_down,
             ln_final_g=ln_final_g)
    M = dict(ln_mix_g=m_ln_mix_g, w_in=m_w_in, sgu_ln_g=m_sgu_ln_g, sgu_ln_b=m_sgu_ln_b, w_spatial=m_w_spatial,
             b_spatial=m_b_spatial, conv_w=m_conv_w, grp_norm_a=m_grp_norm_a, grp_norm_b=m_grp_norm_b, w_out=m_w_out,
             ln_attn_g=m_ln_attn_g, ln_mem_g=m_ln_mem_g, w_q=m_w_q, w_kv=m_w_kv, w_o=m_w_o, ln_ffn_g=m_ln_ffn_g,
             w_gate_up=m_w_gate_up, w_down=m_w_down, ln_final_g=m_ln_final_g)
    V = dict(ln_mix_g=v_ln_mix_g, w_in=v_w_in, sgu_ln_g=v_sgu_ln_g, sgu_ln_b=v_sgu_ln_b, w_spatial=v_w_spatial,
             b_spatial=v_b_spatial, conv_w=v_conv_w, grp_norm_a=v_grp_norm_a, grp_norm_b=v_grp_norm_b, w_out=v_w_out,
             ln_attn_g=v_ln_attn_g, ln_mem_g=v_ln_mem_g, w_q=v_w_q, w_kv=v_w_kv, w_o=v_w_o, ln_ffn_g=v_ln_ffn_g,
             w_gate_up=v_w_gate_up, w_down=v_w_down, ln_final_g=v_ln_final_g)

    bw = conv_w.shape[1] * N_DEV
    pos = jnp.stack([lax.axis_index("x"), lax.axis_index("y"), lax.axis_index("c")]).astype(jnp.int32)
    me = 4 * pos[0] + 2 * pos[1] + pos[2]

    sp = {nm: (W[nm].reshape(1, -1) if W[nm].ndim == 1 else W[nm]) for nm in SMALL}
    view = lambda a, nm: a.T if nm in TRANSPOSED else a
    wb = {nm: view(W[nm], nm).astype(BF16) for nm in BIG}
    grad_x, part, small_all, seg = _step(x[0], mem[0], loss_target[0], wb, conv_w, sp, pos)

    out = {}
    for nm in BIG:
        res = _adamw_shard(part[nm][0], part[nm][1], view(W[nm], nm), view(M[nm], nm), view(V[nm], nm), "adamw_" + nm)
        out[nm] = tuple(view(a, nm) for a in res)

    params = {nm: (_rows128(W[nm]), _rows128(M[nm]), _rows128(V[nm])) for nm in SMALL}
    per, conv_g_rows, loss_sum = _adamw_small(small_all, seg, params, seg["conv_w"])
    for nm in SMALL:
        out[nm] = tuple(a.reshape(W[nm].shape) for a in per[nm])
    conv_g = lax.dynamic_slice_in_dim(conv_g_rows.reshape(3, bw), me * conv_w.shape[1], conv_w.shape[1], axis=1)
    out["conv_w"] = (conv_g,) + tuple(_adamw_one(conv_w, conv_g, m_conv_w, v_conv_w, "adamw_conv"))

    loss = loss_sum[0, 0]
    res = [loss, grad_x[None]]
    for k in range(4):
        res += [out[nm][k] for nm in order]
    return tuple(res)
```

```python
import functools

import jax
import jax.numpy as jnp
from jax import lax
from jax.experimental import pallas as pl
from jax.experimental.pallas import tpu as pltpu

F32 = jnp.float32
BF16 = jnp.bfloat16
SDS = jax.ShapeDtypeStruct
MESH = pl.DeviceIdType.MESH

EPS = 1e-6
N_DEV = 8
HEADS = 4
CHUNK = 128
HALO = 16
SUB = 8
LANES = 128
TOKEN_TILE = 512
ROW_CHUNK = 256

ADAM_LR = 0.001
ADAM_B1 = 0.9
ADAM_B2 = 0.999
ADAM_EPS = 1e-08
ADAM_WD = 0.01
ADAM_STEP = 10

BIG = ("w_in", "w_out", "w_q", "w_kv", "w_o", "w_gate_up", "w_down")
TRANSPOSED = ("w_in", "w_gate_up")
SMALL = ("ln_mix_g", "sgu_ln_g", "sgu_ln_b", "w_spatial", "b_spatial", "grp_norm_a", "grp_norm_b",
         "ln_attn_g", "ln_mem_g", "ln_ffn_g", "ln_final_g")


class _Exchange:
    def __init__(self, ins, out_shape, sems, start, finish, relay=None):
        self.ins, self.out_shape, self.sems = list(ins), list(out_shape), list(sems)
        self.start, self.finish, self.relay = start, finish, relay


def _pcall(body, carry=(), n_prefetch=0, **kw):
    if carry:
        return functools.partial(_carrying_call, body, tuple(carry), n_prefetch, kw)
    if n_prefetch:
        kw["grid_spec"] = pltpu.PrefetchScalarGridSpec(
            num_scalar_prefetch=n_prefetch, grid=kw.pop("grid"), in_specs=kw.pop("in_specs"),
            out_specs=kw.pop("out_specs"), scratch_shapes=kw.pop("scratch_shapes", ()))
    return pl.pallas_call(body, **kw)


def _carrying_call(body, carry, n_prefetch, kw, *args):
    kw = dict(kw)
    out_shape = kw.pop("out_shape")
    single = not isinstance(out_shape, (tuple, list))
    outs_shape = (out_shape,) if single else tuple(out_shape)
    out_specs = kw.pop("out_specs")
    out_specs = [out_specs] if single else list(out_specs)
    in_specs = list(kw.pop("in_specs"))
    scratch = list(kw.pop("scratch_shapes", ()))
    grid = tuple(kw.get("grid", ()))
    n_in, n_out, n_scr = len(args), len(outs_shape), len(scratch)

    def split(refs, k, counts):
        parts = []
        for cnt in counts:
            parts.append(refs[k:k + cnt])
            k += cnt
        return parts, k

    def wrapped(*refs):
        cins, k = split(refs, n_in, [len(p.ins) for p in carry])
        outs = refs[k:k + n_out]
        couts, k = split(refs, k + n_out, [len(p.out_shape) for p in carry])
        scr = refs[k:k + n_scr]
        csems, _ = split(refs, k + n_scr, [len(p.sems) for p in carry])
        first, last = True, True
        for a, g in enumerate(grid):
            first = (pl.program_id(a) == 0) & first
            last = (pl.program_id(a) == g - 1) & last

        def start_all():
            for p, ci, co, cs in zip(carry, cins, couts, csems):
                p.start(ci, co, cs)

        def relay_all():
            for p, ci, co, cs in zip(carry, cins, couts, csems):
                if p.relay is not None:
                    p.relay(ci, co, cs)

        def finish_all():
            for p, ci, co, cs in zip(carry, cins, couts, csems):
                p.finish(ci, co, cs)

        start_all() if not grid else pl.when(first)(start_all)
        relay_all() if not grid else pl.when(last)(relay_all)
        body(*refs[:n_in], *outs, *scr)
        finish_all() if not grid else pl.when(last)(finish_all)

    c_in = [a for p in carry for a in p.ins]
    c_out = [s for p in carry for s in p.out_shape]
    c_sems = [s for p in carry for s in p.sems]
    res = _pcall(wrapped, n_prefetch=n_prefetch, out_shape=outs_shape + tuple(c_out),
                 in_specs=in_specs + _hbm_specs(len(c_in)), out_specs=out_specs + _hbm_specs(len(c_out)),
                 scratch_shapes=scratch + c_sems, **kw)(*args, *c_in)
    own = res[0] if single else tuple(res[:n_out])
    landed, k = [], n_out
    for p in carry:
        landed.append(list(res[k:k + len(p.out_shape)]))
        k += len(p.out_shape)
    return own, landed


def _hbm_specs(n):
    return [pl.BlockSpec(memory_space=pl.ANY)] * n


def _hosted(body, carry, **kw):
    if carry:
        return _pcall(body, carry=carry, **kw)
    call = _pcall(body, **kw)
    return lambda *args: (call(*args), [])


def _run_exchanges(parts, name):
    def body(*refs):
        pass

    _, landed = _pcall(body, carry=parts, out_shape=(), in_specs=[], out_specs=[], name=name)()
    return landed


def _arb(n):
    return pltpu.CompilerParams(dimension_semantics=("arbitrary",) * n)


def _tile(n, target, mult):
    best = None
    for t in range(mult, min(n, target) + 1, mult):
        if n % t == 0:
            best = t
    return n if best is None else best


def _round_up(n, m):
    return (n + m - 1) // m * m


def _dot(a, b):
    return jnp.dot(a, b, preferred_element_type=F32)


def _dot_nt(a, b):
    return lax.dot_general(a, b, (((1,), (1,)), ((), ())), preferred_element_type=F32)


def _dot_tn(a, b):
    return lax.dot_general(a, b, (((0,), (0,)), ((), ())), preferred_element_type=F32)


def _rstd(x):
    return lax.rsqrt(jnp.mean(x * x, axis=-1, keepdims=True) + EPS)


def _rms_bwd(dy, x, r, g):
    gdy = dy * g
    proj = jnp.sum(gdy * x, axis=-1, keepdims=True) * (1.0 / x.shape[-1])
    dx = r * gdy - x * (r * r * r) * proj
    dg = jnp.sum(dy * (x * r), axis=0, keepdims=True)
    return dx, dg


_GELU_C = 0.7978845608028654
_GELU_A = 0.044715


def _gelu(x):
    t = jnp.tanh(_GELU_C * (x + _GELU_A * x * x * x))
    return 0.5 * x * (1.0 + t), t


def _gelu_grad(x, t):
    return 0.5 * (1.0 + t) + 0.5 * x * (1.0 - t * t) * (_GELU_C * (1.0 + 3.0 * _GELU_A * x * x))


def _sigmoid(x):
    return 1.0 / (1.0 + jnp.exp(-x))


def _softmax(s):
    m = jnp.max(s, axis=-1, keepdims=True)
    e = jnp.exp(s - m)
    return e / jnp.sum(e, axis=-1, keepdims=True)


def _adamw(w, g, m, v):
    m = ADAM_B1 * m + (1.0 - ADAM_B1) * g
    v = ADAM_B2 * v + (1.0 - ADAM_B2) * (g * g)
    m_hat = m / (1.0 - ADAM_B1 ** ADAM_STEP)
    v_hat = v / (1.0 - ADAM_B2 ** ADAM_STEP)
    delta = -ADAM_LR * (m_hat / (jnp.sqrt(v_hat) + ADAM_EPS) + ADAM_WD * w)
    return delta, m, v


def _tril_mask():
    t = lax.broadcasted_iota(jnp.int32, (CHUNK, CHUNK), 0)
    s = lax.broadcasted_iota(jnp.int32, (CHUNK, CHUNK), 1)
    return (s <= t).astype(F32)


def _sgu_forward(ha, lng, lnb, wm, bt, mixed_s):
    aw = ha.shape[1] // 2
    hd = aw // HEADS
    a, th = _gelu(ha)
    u = a[:, :aw]
    v = a[:, aw:]
    mu = jnp.mean(v, axis=-1, keepdims=True)
    vc = v - mu
    rl = lax.rsqrt(jnp.mean(vc * vc, axis=-1, keepdims=True) + EPS)
    xhat = vc * rl
    vln = (xhat * lng + lnb).astype(BF16)
    for n in range(ha.shape[0] // CHUNK):
        rows = slice(n * CHUNK, (n + 1) * CHUNK)
        for h in range(HEADS):
            cols = slice(h * hd, (h + 1) * hd)
            mixed_s[rows, cols] = _dot(wm[h], vln[rows, cols]) + bt[:, h:h + 1]
    return th, u, xhat, rl, vln


def _conv_taps(zext):
    return pltpu.roll(zext, 2, 0), pltpu.roll(zext, 1, 0)


def _kv_forward(mem, g_mem, w_kv):
    ml, d = mem.shape
    xd = w_kv.shape[2]

    def body(mem_ref, g_ref, w_ref, memn_ref, kv_ref):
        x = mem_ref[...]
        memn = (x * _rstd(x) * g_ref[...]).astype(BF16)
        memn_ref[...] = memn
        for j in range(2 * HEADS):
            kv_ref[j] = _dot(memn, w_ref[j]).astype(BF16)

    return _pcall(body, out_shape=(SDS((ml, d), BF16), SDS((2 * HEADS, ml, xd), BF16)), name="kv_forward")(mem, g_mem, w_kv)


def _in_forward(x, g, w_in_t, tm, carry=()):
    s, d = x.shape
    n_in = w_in_t.shape[0]

    def body(x_ref, g_ref, w_ref, xn_ref, h_ref):
        xv = x_ref[...]
        xn = (xv * _rstd(xv) * g_ref[...]).astype(BF16)
        xn_ref[...] = xn
        h_ref[...] = _dot_nt(xn, w_ref[...])

    return _hosted(
        body, carry, grid=(s // tm,),
        in_specs=[pl.BlockSpec((tm, d), lambda i: (i, 0)), pl.BlockSpec((1, d), lambda i: (0, 0)),
                  pl.BlockSpec((n_in, d), lambda i: (0, 0))],
        out_specs=[pl.BlockSpec((tm, d), lambda i: (i, 0)), pl.BlockSpec((tm, n_in), lambda i: (i, 0))],
        out_shape=(SDS((s, d), BF16), SDS((s, n_in), F32)),
        compiler_params=_arb(1), name="in_forward")(x, g, w_in_t)


def _mix_forward(h, x, lng, lnb, w_sp, bt, conv_w, ga, gb, w_out, tm, carry=()):
    s, d = x.shape
    n_in = h.shape[1]
    aw = lng.shape[1]
    bw = d - aw
    in_a = 2 * aw
    hb_blocks = tm // HALO

    def body(h_ref, hprev_ref, x_ref, lng_ref, lnb_ref, wsp_ref, bt_ref, cw_ref, ga_ref, gb_ref, wout_ref,
             ycat_ref, x1_ref, mixed_s):
        i = pl.program_id(0)
        mask = _tril_mask()
        wm = [(wsp_ref[hh] * mask).astype(BF16) for hh in range(HEADS)]
        hv = h_ref[...]
        _, u, _, _, _ = _sgu_forward(hv[:, :in_a], lng_ref[...], lnb_ref[...], wm, bt_ref[...], mixed_s)
        sg = u * mixed_s[...]
        ycat_ref[:, :aw] = (sg * _rstd(sg) * ga_ref[...]).astype(BF16)

        gate_b = hv[:, in_a:in_a + bw]
        z = hv[:, in_a + bw:in_a + 2 * bw] * hv[:, in_a + 2 * bw:]
        hp = hprev_ref[...]
        zp = hp[:, in_a + bw:in_a + 2 * bw] * hp[:, in_a + 2 * bw:]
        zp = jnp.where(i == 0, 0.0, zp)
        zext = jnp.concatenate([zp, z], axis=0)
        z2, z1 = _conv_taps(zext)
        cw = cw_ref[...]
        conv = cw[0:1] * z2[HALO:] + cw[1:2] * z1[HALO:] + cw[2:3] * z
        sc = gate_b * conv
        ycat_ref[:, aw:] = (sc * _rstd(sc) * gb_ref[...]).astype(BF16)
        x1_ref[...] = x_ref[...] + _dot(ycat_ref[...], wout_ref[...])

    full = lambda shape: pl.BlockSpec(shape, lambda i: (0,) * len(shape))
    return _hosted(
        body, carry, grid=(s // tm,),
        in_specs=[pl.BlockSpec((tm, n_in), lambda i: (i, 0)),
                  pl.BlockSpec((HALO, n_in), lambda i: (jnp.maximum(i * hb_blocks - 1, 0), 0)),
                  pl.BlockSpec((tm, d), lambda i: (i, 0)),
                  full((1, aw)), full((1, aw)), full((HEADS, CHUNK, CHUNK)), full((CHUNK, HEADS)),
                  full((3, bw)), full((1, aw)), full((1, bw)), full((d, d))],
        out_specs=[pl.BlockSpec((tm, d), lambda i: (i, 0)), pl.BlockSpec((tm, d), lambda i: (i, 0))],
        out_shape=(SDS((s, d), BF16), SDS((s, d), F32)),
        scratch_shapes=[pltpu.VMEM((tm, aw), F32)],
        compiler_params=_arb(1), name="mix_forward")(h, h, x, lng, lnb, w_sp, bt, conv_w, ga, gb, w_out)


def _attn_forward(x1, g, w_q, kv, w_o, tm, carry=()):
    s, d = x1.shape
    _, ml, xd = kv.shape
    scale = xd ** -0.5

    def body(x1_ref, g_ref, wq_ref, kv_ref, wo_ref, xn_ref, q_ref, o_ref, x2_ref):
        xv = x1_ref[...]
        xn = (xv * _rstd(xv) * g_ref[...]).astype(BF16)
        xn_ref[...] = xn
        q_ref[...] = _dot(xn, wq_ref[...]).astype(BF16)
        for hh in range(HEADS):
            cols = slice(hh * xd, (hh + 1) * xd)
            p = _softmax(_dot_nt(q_ref[:, cols], kv_ref[hh]) * scale)
            o_ref[:, cols] = _dot(p.astype(BF16), kv_ref[HEADS + hh]).astype(BF16)
        x2_ref[...] = xv + _dot(o_ref[...], wo_ref[...])

    tok = pl.BlockSpec((tm, d), lambda i: (i, 0))
    return _hosted(
        body, carry, grid=(s // tm,),
        in_specs=[tok, pl.BlockSpec((1, d), lambda i: (0, 0)), pl.BlockSpec((d, d), lambda i: (0, 0)),
                  pl.BlockSpec((2 * HEADS, ml, xd), lambda i: (0, 0, 0)), pl.BlockSpec((d, d), lambda i: (0, 0))],
        out_specs=[tok, tok, tok, tok],
        out_shape=(SDS((s, d), BF16), SDS((s, d), BF16), SDS((s, d), BF16), SDS((s, d), F32)),
        compiler_params=_arb(1), name="attn_forward")(x1, g, w_q, kv, w_o)


def _ffn_forward(x2, g, w_gu, w_down, tm):
    s, d = x2.shape
    _, nf, tf, _ = w_gu.shape

    def body(x2_ref, g_ref, wgu_ref, wd_ref, xn_ref, gu_ref, x3_ref):
        f = pl.program_id(1)

        @pl.when(f == 0)
        def _():
            xv = x2_ref[...]
            xn_ref[...] = (xv * _rstd(xv) * g_ref[...]).astype(BF16)
            x3_ref[...] = xv

        xn = xn_ref[...]
        gate = _dot_nt(xn, wgu_ref[0])
        up = _dot_nt(xn, wgu_ref[1])
        gu_ref[0] = gate.astype(BF16)
        gu_ref[1] = up.astype(BF16)
        act = (gate * _sigmoid(gate) * up).astype(BF16)
        x3_ref[...] += _dot(act, wd_ref[...])

    tok = pl.BlockSpec((tm, d), lambda i, f: (i, 0))
    return _pcall(
        body, grid=(s // tm, nf),
        in_specs=[tok, pl.BlockSpec((1, d), lambda i, f: (0, 0)),
                  pl.BlockSpec((2, None, tf, d), lambda i, f: (0, f, 0, 0)),
                  pl.BlockSpec((tf, d), lambda i, f: (f, 0))],
        out_specs=[tok, pl.BlockSpec((2, None, tm, tf), lambda i, f: (0, f, i, 0)), tok],
        out_shape=(SDS((s, d), BF16), SDS((2, nf, s, tf), BF16), SDS((s, d), F32)),
        compiler_params=_arb(2), name="ffn_forward")(x2, g, w_gu, w_down)


def _final_backward(x3, target, g_final, tm):
    s, d = x3.shape

    def body(x3_ref, tgt_ref, gf_ref, loss_ref, dgf_ref, dx3_ref, dx3b_ref):
        @pl.when(pl.program_id(0) == 0)
        def _():
            loss_ref[...] = jnp.zeros_like(loss_ref)
            dgf_ref[...] = jnp.zeros_like(dgf_ref)

        xv = x3_ref[...]
        r = _rstd(xv)
        diff = xv * r * gf_ref[...] - tgt_ref[...]
        loss_ref[...] += 0.5 * jnp.sum(jnp.sum(diff * diff, axis=-1, keepdims=True), axis=0, keepdims=True) * (1.0 / d)
        dx3, dgf = _rms_bwd(diff * (1.0 / d), xv, r, gf_ref[...])
        dgf_ref[...] += dgf
        dx3_ref[...] = dx3
        dx3b_ref[...] = dx3.astype(BF16)

    tok = pl.BlockSpec((tm, d), lambda i: (i, 0))
    vec = pl.BlockSpec((1, d), lambda i: (0, 0))
    return _pcall(
        body, grid=(s // tm,), in_specs=[tok, tok, vec],
        out_specs=[pl.BlockSpec((SUB, LANES), lambda i: (0, 0)), vec, tok, tok],
        out_shape=(SDS((SUB, LANES), F32), SDS((1, d), F32), SDS((s, d), F32), SDS((s, d), BF16)),
        compiler_params=_arb(1), name="final_backward")(x3, target, g_final)


def _swiglu_backward(dx3b, gu, w_gu, w_down, tm):
    s, d = dx3b.shape
    _, nf, tf, _ = w_gu.shape

    def body(dx3b_ref, gu_ref, wgu_ref, wd_ref, act_ref, dgu_ref, dxn_ref):
        @pl.when(pl.program_id(1) == 0)
        def _():
            dxn_ref[...] = jnp.zeros_like(dxn_ref)

        for r0 in range(0, tm, ROW_CHUNK):
            rows = slice(r0, r0 + ROW_CHUNK)
            dact = _dot_nt(dx3b_ref[rows, :], wd_ref[...])
            gv = gu_ref[0, rows, :].astype(F32)
            uv = gu_ref[1, rows, :].astype(F32)
            sg = _sigmoid(gv)
            silu = gv * sg
            act_ref[rows, :] = (silu * uv).astype(BF16)
            dgate = (dact * uv * (sg * (1.0 + gv * (1.0 - sg)))).astype(BF16)
            dup = (dact * silu).astype(BF16)
            dgu_ref[0, rows, :] = dgate
            dgu_ref[1, rows, :] = dup
            part = _dot(dgate, wgu_ref[0]) + _dot(dup, wgu_ref[1])
            dxn_ref[rows, :] += part

    tok = pl.BlockSpec((tm, d), lambda i, f: (i, 0))
    pair = pl.BlockSpec((2, None, tm, tf), lambda i, f: (0, f, i, 0))
    return _pcall(
        body, grid=(s // tm, nf),
        in_specs=[tok, pair, pl.BlockSpec((2, None, tf, d), lambda i, f: (0, f, 0, 0)),
                  pl.BlockSpec((tf, d), lambda i, f: (f, 0))],
        out_specs=[pl.BlockSpec((None, tm, tf), lambda i, f: (f, i, 0)), pair, tok],
        out_shape=(SDS((nf, s, tf), BF16), SDS((2, nf, s, tf), BF16), SDS((s, d), F32)),
        compiler_params=_arb(2), name="swiglu_backward")(dx3b, gu, w_gu, w_down)


def _attn_backward(dx3, dxn3, x2, g_ffn, x1, g, q, kv, w_q, w_o, tm, carry=()):
    s, d = x1.shape
    _, ml, xd = kv.shape
    scale = xd ** -0.5

    def body(dx3_ref, dxn3_ref, x2_ref, g2_ref, x1_ref, g_ref, q_ref, kv_ref, wq_ref, wo_ref,
             dx2b_ref, dq_ref, dx1_ref, dkv_ref, dg_ref, dg2_ref, do_s):
        i = pl.program_id(0)

        @pl.when(i == 0)
        def _():
            dkv_ref[...] = jnp.zeros_like(dkv_ref)
            dg_ref[...] = jnp.zeros_like(dg_ref)
            dg2_ref[...] = jnp.zeros_like(dg2_ref)

        x2v = x2_ref[...]
        dx2n, dg2 = _rms_bwd(dxn3_ref[...], x2v, _rstd(x2v), g2_ref[...])
        dg2_ref[...] += dg2
        dx2 = dx3_ref[...] + dx2n
        dx2b_ref[...] = dx2.astype(BF16)
        do_s[...] = _dot_nt(dx2b_ref[...], wo_ref[...]).astype(BF16)
        for hh in range(HEADS):
            kc = slice(hh * xd, (hh + 1) * xd)
            qh = q_ref[:, kc]
            kh = kv_ref[hh]
            doh = do_s[:, kc]
            p = _softmax(_dot_nt(qh, kh) * scale)
            dp = _dot_nt(doh, kv_ref[HEADS + hh])
            dkv_ref[HEADS + hh] += _dot_tn(p.astype(BF16), doh)
            ds = (p * (dp - jnp.sum(dp * p, axis=-1, keepdims=True)) * scale).astype(BF16)
            dq_ref[:, kc] = _dot(ds, kh).astype(BF16)
            dkv_ref[hh] += _dot_tn(ds, qh)
        dg_tile = jnp.zeros_like(dg_ref)
        for r0 in range(0, tm, min(ROW_CHUNK, tm)):
            rows = slice(r0, r0 + min(ROW_CHUNK, tm))
            dxn = _dot_nt(dq_ref[rows, :], wq_ref[...])
            xv = x1_ref[rows, :]
            dx, dg = _rms_bwd(dxn, xv, _rstd(xv), g_ref[...])
            dg_tile = dg_tile + dg
            dx1_ref[rows, :] = dx2[rows] + dx
        dg_ref[...] += dg_tile

    tok = pl.BlockSpec((tm, d), lambda i: (i, 0))
    vec = pl.BlockSpec((1, d), lambda i: (0, 0))
    sq = pl.BlockSpec((d, d), lambda i: (0, 0))
    kvs = pl.BlockSpec((2 * HEADS, ml, xd), lambda i: (0, 0, 0))
    return _hosted(
        body, carry, grid=(s // tm,),
        in_specs=[tok, tok, tok, vec, tok, vec, tok, kvs, sq, sq],
        out_specs=[tok, tok, tok, kvs, vec, vec],
        out_shape=(SDS((s, d), BF16), SDS((s, d), BF16), SDS((s, d), F32), SDS((2 * HEADS, ml, xd), F32), SDS((1, d), F32),
                   SDS((1, d), F32)),
        scratch_shapes=[pltpu.VMEM((tm, d), BF16)],
        compiler_params=_arb(1), name="attn_backward")(dx3, dxn3, x2, g_ffn, x1, g, q, kv, w_q, w_o)


def _kv_backward(dkv, memn, mem, g_mem, w_kv):
    ml, d = mem.shape
    xd = w_kv.shape[2]

    def body(dkv_ref, memn_ref, mem_ref, g_ref, w_ref, dw_ref, dg_ref):
        dmemn = jnp.zeros((ml, d), F32)
        for j in range(2 * HEADS):
            dkvb = dkv_ref[j].astype(BF16)
            dw_ref[j] = _dot_tn(memn_ref[...], dkvb)
            dmemn = dmemn + _dot_nt(dkvb, w_ref[j])
        x = mem_ref[...]
        dg_ref[...] = jnp.sum(dmemn * (x * _rstd(x)), axis=0, keepdims=True)

    return _pcall(body, out_shape=(SDS((2 * HEADS, d, xd), F32), SDS((1, d), F32)), name="kv_backward")(dkv, memn, mem, g_mem, w_kv)


def _mix_backward(dx1, x, g_mix, h, lng, lnb, w_sp, bt, conv_w, ga, gb, w_out, w_in, tm, carry=()):
    s, d = x.shape
    n_in = h.shape[1]
    aw = lng.shape[1]
    bw = d - aw
    hd = aw // HEADS
    in_a = 2 * aw
    hb_blocks = tm // HALO
    last_blk = s // HALO - 1
    nt = s // tm
    te = tm + HALO
    tee = tm + 2 * HALO

    def body(dx1_ref, dx1n_ref, x_ref, gm_ref, h_ref, hp_ref, hn_ref, lng_ref, lnb_ref, wsp_ref, bt_ref, cw_ref,
             ga_ref, gb_ref, wout_ref, win_ref,
             dx1b_ref, dh_ref, dx_ref, dga_ref, dgb_ref, dcw_ref, dlng_ref, dlnb_ref, dwsp_ref, dbs_ref, dgm_ref,
             mixed_s, dvln_s):
        i = pl.program_id(0)

        @pl.when(i == 0)
        def _():
            for ref in (dga_ref, dgb_ref, dcw_ref, dlng_ref, dlnb_ref, dwsp_ref, dbs_ref, dgm_ref):
                ref[...] = jnp.zeros_like(ref)

        mask = _tril_mask()
        wm = [(wsp_ref[hh] * mask).astype(BF16) for hh in range(HEADS)]
        hv = h_ref[...]
        dx1 = dx1_ref[...]
        dx1b_ref[...] = dx1.astype(BF16)
        dx1e = jnp.concatenate([dx1, dx1n_ref[...]], axis=0).astype(BF16)
        dycat = _dot_nt(dx1e, wout_ref[...])

        hbe = jnp.concatenate([hp_ref[:, in_a:], hv[:, in_a:], hn_ref[:, in_a:]], axis=0)
        row = lax.broadcasted_iota(jnp.int32, (tee, 1), 0)
        zext = hbe[:, bw:2 * bw] * hbe[:, 2 * bw:]
        zext = jnp.where((i == 0) & (row < HALO), 0.0, zext)
        z2e, z1e = _conv_taps(zext)
        cw = cw_ref[...]
        conv_e = (cw[0:1] * z2e + cw[1:2] * z1e + cw[2:3] * zext)[HALO:]
        gate_b_e = hbe[HALO:, :bw]
        sc_e = gate_b_e * conv_e
        rb = _rstd(sc_e)
        dyb = dycat[:, aw:]
        gdy = dyb * gb_ref[...]
        dsc_e = rb * gdy - sc_e * (rb * rb * rb) * (jnp.sum(gdy * sc_e, axis=-1, keepdims=True) * (1.0 / bw))
        dgb_ref[...] += jnp.sum((dyb * (sc_e * rb))[:tm], axis=0, keepdims=True)
        dconv_e = dsc_e * gate_b_e
        dconv_e = jnp.where((i == nt - 1) & (row[:te] >= tm), 0.0, dconv_e)
        dconv = dconv_e[:tm]
        dc1 = pltpu.roll(dconv_e, te - 1, 0)[:tm]
        dc2 = pltpu.roll(dconv_e, te - 2, 0)[:tm]
        dz = cw[2:3] * dconv + cw[1:2] * dc1 + cw[0:1] * dc2
        z = zext[HALO:HALO + tm]
        z1 = z1e[HALO:HALO + tm]
        z2 = z2e[HALO:HALO + tm]
        dcw_ref[0:1, :] += jnp.sum(dconv * z2, axis=0, keepdims=True)
        dcw_ref[1:2, :] += jnp.sum(dconv * z1, axis=0, keepdims=True)
        dcw_ref[2:3, :] += jnp.sum(dconv * z, axis=0, keepdims=True)
        dh_ref[:, in_a:in_a + bw] = (dsc_e[:tm] * conv_e[:tm]).astype(BF16)
        dh_ref[:, in_a + bw:in_a + 2 * bw] = (dz * hv[:, in_a + 2 * bw:]).astype(BF16)
        dh_ref[:, in_a + 2 * bw:] = (dz * hv[:, in_a + bw:in_a + 2 * bw]).astype(BF16)

        ha = hv[:, :in_a]
        th, u, xhat, rl, vln = _sgu_forward(ha, lng_ref[...], lnb_ref[...], wm, bt_ref[...], mixed_s)
        mixed = mixed_s[...]
        sg = u * mixed
        dsg, dga = _rms_bwd(dycat[:tm, :aw], sg, _rstd(sg), ga_ref[...])
        dga_ref[...] += dga
        du = dsg * mixed
        dmixed = dsg * u
        dmb = dmixed.astype(BF16)
        for n in range(tm // CHUNK):
            rows = slice(n * CHUNK, (n + 1) * CHUNK)
            dbs_ref[...] += dmixed[rows]
            for hh in range(HEADS):
                cols = slice(hh * hd, (hh + 1) * hd)
                dvln_s[rows, cols] = _dot_tn(wm[hh], dmb[rows, cols])
                dwsp_ref[hh] += mask * _dot_nt(dmb[rows, cols], vln[rows, cols])
        dvln = dvln_s[...]
        dlng_ref[...] += jnp.sum(dvln * xhat, axis=0, keepdims=True)
        dlnb_ref[...] += jnp.sum(dvln, axis=0, keepdims=True)
        dxh = dvln * lng_ref[...]
        dv = rl * (dxh - jnp.mean(dxh, axis=-1, keepdims=True) - xhat * jnp.mean(dxh * xhat, axis=-1, keepdims=True))
        dh_ref[:, :in_a] = (jnp.concatenate([du, dv], axis=-1) * _gelu_grad(ha, th)).astype(BF16)

        dgm_tile = jnp.zeros_like(dgm_ref)
        for r0 in range(0, tm, min(ROW_CHUNK, tm)):
            rows = slice(r0, r0 + min(ROW_CHUNK, tm))
            dxn = _dot(dh_ref[rows, :], win_ref[...])
            xv = x_ref[rows, :]
            dx, dgm = _rms_bwd(dxn, xv, _rstd(xv), gm_ref[...])
            dgm_tile = dgm_tile + dgm
            dx_ref[rows, :] = dx1_ref[rows, :] + dx
        dgm_ref[...] += dgm_tile

    full = lambda shape: pl.BlockSpec(shape, lambda i: (0,) * len(shape))
    tok = pl.BlockSpec((tm, d), lambda i: (i, 0))
    nxt = lambda i: (jnp.minimum((i + 1) * hb_blocks, last_blk), 0)
    prv = lambda i: (jnp.maximum(i * hb_blocks - 1, 0), 0)
    return _hosted(
        body, carry, grid=(nt,),
        in_specs=[tok, pl.BlockSpec((HALO, d), nxt), tok, full((1, d)),
                  pl.BlockSpec((tm, n_in), lambda i: (i, 0)), pl.BlockSpec((HALO, n_in), prv), pl.BlockSpec((HALO, n_in), nxt),
                  full((1, aw)), full((1, aw)), full((HEADS, CHUNK, CHUNK)), full((CHUNK, HEADS)), full((3, bw)),
                  full((1, aw)), full((1, bw)), full((d, d)), full((n_in, d))],
        out_specs=[tok, pl.BlockSpec((tm, n_in), lambda i: (i, 0)), tok,
                   full((1, aw)), full((1, bw)), full((SUB, bw)), full((1, aw)), full((1, aw)),
                   full((HEADS, CHUNK, CHUNK)), full((CHUNK, aw)), full((1, d))],
        out_shape=(SDS((s, d), BF16), SDS((s, n_in), BF16), SDS((s, d), F32),
                   SDS((1, aw), F32), SDS((1, bw), F32), SDS((SUB, bw), F32), SDS((1, aw), F32), SDS((1, aw), F32),
                   SDS((HEADS, CHUNK, CHUNK), F32), SDS((CHUNK, aw), F32), SDS((1, d), F32)),
        scratch_shapes=[pltpu.VMEM((tm, aw), F32), pltpu.VMEM((tm, aw), F32)],
        compiler_params=_arb(1), name="mix_backward")(dx1, dx1, x, g_mix, h, h, h, lng, lnb, w_sp, bt, conv_w, ga, gb, w_out, w_in)


def _bias_grad(dbs):
    aw = dbs.shape[1]
    hd = aw // HEADS

    def body(dbs_ref, out_ref):
        ones = jnp.ones((SUB, hd), F32)
        for hh in range(HEADS):
            r = lax.dot_general(ones, dbs_ref[:, hh * hd:(hh + 1) * hd], (((1,), (1,)), ((), ())),
                                precision=lax.Precision.HIGHEST, preferred_element_type=F32)
            out_ref[hh:hh + 1, :] = r[0:1]

    return _pcall(body, out_shape=SDS((HEADS, CHUNK), F32), name="bias_grad")(dbs)


def _wgrad_body(a_ref, b_ref, o_ref):
    o_ref[...] = _dot_tn(a_ref[...], b_ref[...])


def _wgrad(a, b, name, carry=()):
    k, m = a.shape
    n = b.shape[1]
    tm = _tile(m, 512, LANES)
    tn = _tile(n, 1024, LANES)
    return _hosted(
        functools.partial(_wgrad_body), carry, grid=(m // tm, n // tn),
        in_specs=[pl.BlockSpec((k, tm), lambda i, j: (0, i)), pl.BlockSpec((k, tn), lambda i, j: (0, j))],
        out_specs=pl.BlockSpec((tm, tn), lambda i, j: (i, j)),
        out_shape=SDS((m, n), F32), compiler_params=_arb(2), name=name)(a, b)


def _wgrad_blocked_lhs(a, b, name, carry=()):
    nb, k, t = a.shape
    n = b.shape[1]
    tn = _tile(n, 1024, LANES)
    return _hosted(
        functools.partial(_wgrad_body), carry, grid=(nb, n // tn),
        in_specs=[pl.BlockSpec((None, k, t), lambda i, j: (i, 0, 0)), pl.BlockSpec((k, tn), lambda i, j: (0, j))],
        out_specs=pl.BlockSpec((t, tn), lambda i, j: (i, j)),
        out_shape=SDS((nb * t, n), F32), compiler_params=_arb(2), name=name)(a, b)


def _wgrad_blocked_rhs(a, b, name, carry=()):
    k, m = a.shape
    nb, _, t = b.shape
    tm = _tile(m, 512, LANES)
    return _hosted(
        functools.partial(_wgrad_body), carry, grid=(m // tm, nb),
        in_specs=[pl.BlockSpec((k, tm), lambda i, j: (0, i)), pl.BlockSpec((None, k, t), lambda i, j: (j, 0, 0))],
        out_specs=pl.BlockSpec((None, tm, t), lambda i, j: (j, i, 0)),
        out_shape=SDS((nb, m, t), F32), compiler_params=_arb(2), name=name)(a, b)


def _unblock_cols(wb, name, carry=()):
    nb, r, t = wb.shape
    tr = _tile(r, 256, 16)

    def body(w_ref, o_ref):
        o_ref[...] = jnp.concatenate([w_ref[j].astype(F32) for j in range(nb)], axis=-1).astype(o_ref.dtype)

    return _hosted(
        body, carry, grid=(r // tr,),
        in_specs=[pl.BlockSpec((nb, tr, t), lambda i: (0, i, 0))], out_specs=pl.BlockSpec((tr, nb * t), lambda i: (i, 0)),
        out_shape=SDS((r, nb * t), wb.dtype), compiler_params=_arb(1), name=name)(wb)


def _block_cols(w, nb, name, carry=()):
    r, n = w.shape
    t = n // nb
    tr = _tile(r, 256, 16)

    def body(w_ref, o_ref):
        wv = w_ref[...]
        for j in range(nb):
            o_ref[j] = wv[:, j * t:(j + 1) * t]

    return _hosted(
        body, carry, grid=(r // tr,),
        in_specs=[pl.BlockSpec((tr, n), lambda i: (i, 0))], out_specs=pl.BlockSpec((nb, tr, t), lambda i: (0, i, 0)),
        out_shape=SDS((nb, r, t), w.dtype), compiler_params=_arb(1), name=name)(w)


def _place():
    x, y, c = lax.axis_index("x"), lax.axis_index("y"), lax.axis_index("c")
    return x, y, c, [(1 - x, y), (x, 1 - y), (1 - x, 1 - y)]


def _all_gather(shards):
    n = len(shards)
    slots = 9
    cut = [(s.shape[0] // 32) * 16 for s in shards]

    def build(ins, outs, sems):
        send_sems, recv_sems, local_sems = sems
        x, y, c, _ = _place()
        me, sib, xn, yn, dg = (x, y, c), (x, y, 1 - c), (1 - x, y, c), (x, 1 - y, c), (1 - x, 1 - y, c)
        other = lambda p: (p[0], p[1], 1 - p[2])

        def rows(a, p, part=None):
            ref = outs[a].at[4 * p[0] + 2 * p[1] + p[2]]
            if part is None or cut[a] == 0:
                return ref if part in (None, 0) else None
            return ref.at[pl.ds(0, cut[a])] if part == 0 else ref.at[pl.ds(cut[a], shards[a].shape[0] - cut[a])]

        def copy(a, k, ref, to, src=None):
            if ref is None:
                return None
            return pltpu.make_async_remote_copy(
                src_ref=ref if src is None else src, dst_ref=ref, send_sem=send_sems.at[slots * a + k],
                recv_sem=recv_sems.at[slots * a + k], device_id=to, device_id_type=MESH)

        def real(cps):
            return [cp for cp in cps if cp is not None]

        class Copies:
            own = lambda a: [copy(a, 1, rows(a, me), xn, ins[a]), copy(a, 2, rows(a, me), yn, ins[a]),
                             copy(a, 0, rows(a, me), sib, ins[a])]
            local = lambda a: pltpu.make_async_copy(ins[a], rows(a, me), local_sems.at[a])
            from_x = lambda a: copy(a, 1, rows(a, xn), me)
            from_y = lambda a: copy(a, 2, rows(a, yn), me)
            after_x = lambda a: real([copy(a, 4, rows(a, xn, 1), yn), copy(a, 5, rows(a, xn), sib)])
            after_y = lambda a: real([copy(a, 3, rows(a, yn, 0), xn), copy(a, 6, rows(a, yn), sib)])
            diag_in = lambda a: real([copy(a, 3, rows(a, dg, 0), me), copy(a, 4, rows(a, dg, 1), me)])
            diag_on = lambda a: real([copy(a, 7, rows(a, dg, 0), sib), copy(a, 8, rows(a, dg, 1), sib)])
            from_sib = lambda a: real([copy(a, 0, rows(a, sib), me), copy(a, 5, rows(a, other(xn)), me),
                                       copy(a, 6, rows(a, other(yn)), me), copy(a, 7, rows(a, other(dg), 0), me),
                                       copy(a, 8, rows(a, other(dg), 1), me)])

        return Copies

    def start(ins, outs, sems):
        cps = build(ins, outs, sems)
        for a in range(n):
            for cp in cps.own(a):
                cp.start()
        for a in range(n):
            cps.local(a).start()

    def relay(ins, outs, sems):
        cps = build(ins, outs, sems)
        for a in range(n):
            cps.from_x(a).wait_recv()
            for cp in cps.after_x(a):
                cp.start()
            cps.from_y(a).wait_recv()
            for cp in cps.after_y(a):
                cp.start()

    def finish(ins, outs, sems):
        cps = build(ins, outs, sems)
        for a in range(n):
            for arrived, onward in zip(cps.diag_in(a), cps.diag_on(a)):
                arrived.wait_recv()
                onward.start()
        for a in range(n):
            for cp in cps.from_sib(a):
                cp.wait_recv()
            for cp in cps.own(a) + cps.after_x(a) + cps.after_y(a) + cps.diag_on(a):
                cp.wait_send()
            cps.local(a).wait()

    return _Exchange(shards, [SDS((N_DEV,) + s.shape, s.dtype) for s in shards],
                     [pltpu.SemaphoreType.DMA((slots * n,)), pltpu.SemaphoreType.DMA((slots * n,)),
                      pltpu.SemaphoreType.DMA((n,))], start, finish, relay)


def _swap_exchange(ins, out_shape, per, copies):
    def start(i, o, sems):
        for cp in copies(i, o, sems):
            cp.start()

    def finish(i, o, sems):
        for cp in copies(i, o, sems):
            cp.wait()

    n = per * len(ins)
    return _Exchange(ins, out_shape, [pltpu.SemaphoreType.DMA((n,)), pltpu.SemaphoreType.DMA((n,))], start, finish)


def _exchange_c(gs):
    def copies(ins, outs, sems):
        x, y, c, _ = _place()
        return [pltpu.make_async_remote_copy(
                    src_ref=ins[a].at[2 * k + 1 - c], dst_ref=outs[a].at[k],
                    send_sem=sems[0].at[4 * a + k], recv_sem=sems[1].at[4 * a + k],
                    device_id=(x, y, 1 - c), device_id_type=MESH)
                for a in range(len(gs)) for k in range(4)]

    return _swap_exchange(gs, [SDS((4,) + g.shape[1:], g.dtype) for g in gs], 4, copies)


def _exchange_xy(sends):
    def copies(ins, outs, sems):
        x, y, c, chips = _place()
        return [pltpu.make_async_remote_copy(
                    src_ref=ins[a].at[t], dst_ref=outs[a].at[t],
                    send_sem=sems[0].at[3 * a + t], recv_sem=sems[1].at[3 * a + t],
                    device_id=(*chips[t], c), device_id_type=MESH)
                for a in range(len(sends)) for t in range(3)]

    return _swap_exchange(sends, [SDS(s.shape, s.dtype) for s in sends], 3, copies)


def _rs_combine(g, recv, pos, name, carry=()):
    _, r, cdim = g.shape
    tr = _tile(r, 256, 16)

    def body(pos_ref, g0, r0, g1, r1, g2, r2, g3, r3, keep_ref, send_ref):
        keep_ref[...] = g0[...] + r0[...]
        send_ref[0] = (g1[...] + r1[...]).astype(BF16)
        send_ref[1] = (g2[...] + r2[...]).astype(BF16)
        send_ref[2] = (g3[...] + r3[...]).astype(BF16)

    def k_of(p, t):
        px = p[0] if t in (0, 2) else 1 - p[0]
        py = p[1] if t in (0, 1) else 1 - p[1]
        return 2 * px + py

    blk = (None, tr, cdim)
    in_specs = []
    for t in range(4):
        in_specs.append(pl.BlockSpec(blk, functools.partial(lambda j, p, t: (2 * k_of(p, t) + p[2], j, 0), t=t)))
        in_specs.append(pl.BlockSpec(blk, functools.partial(lambda j, p, t: (k_of(p, t), j, 0), t=t)))
    return _hosted(
        body, carry, n_prefetch=1, out_shape=(SDS((r, cdim), F32), SDS((3, r, cdim), BF16)),
        grid=(r // tr,), in_specs=in_specs,
        out_specs=[pl.BlockSpec((tr, cdim), lambda j, p: (j, 0)), pl.BlockSpec((3, tr, cdim), lambda j, p: (0, j, 0))],
        compiler_params=_arb(1), name=name)(pos, g, recv, g, recv, g, recv, g, recv)


def _adamw_shard(keep, recv, w, m, v, name):
    r, cdim = w.shape
    tr = _tile(r, 256, 16)

    def body(k_ref, r_ref, w_ref, m_ref, v_ref, g_ref, d_ref, nm_ref, nv_ref):
        g = ((k_ref[...] + r_ref[0].astype(F32)) + r_ref[1].astype(F32)) + r_ref[2].astype(F32)
        g_ref[...] = g
        d_ref[...], nm_ref[...], nv_ref[...] = _adamw(w_ref[...], g, m_ref[...], v_ref[...])

    blk = pl.BlockSpec((tr, cdim), lambda j: (j, 0))
    out = SDS((r, cdim), F32)
    return _pcall(body, grid=(r // tr,), in_specs=[blk, pl.BlockSpec((3, tr, cdim), lambda j: (0, j, 0)), blk, blk, blk],
                  out_specs=[blk] * 4, out_shape=(out,) * 4, compiler_params=_arb(1), name=name)(keep, recv, w, m, v)


def _adamw_small(gathered, seg, params, conv_rows):
    names = list(params)
    c0, cn = conv_rows

    def body(*refs):
        gat_ref = refs[0]
        ins = refs[1:1 + 3 * len(names)]
        outs = refs[1 + 3 * len(names):]

        def total(r0, rn):
            tot = gat_ref[0, r0:r0 + rn, :]
            for dev in range(1, N_DEV):
                tot = tot + gat_ref[dev, r0:r0 + rn, :]
            return tot

        for k, nm in enumerate(names):
            g = total(*seg[nm])
            w_ref, m_ref, v_ref = ins[3 * k:3 * k + 3]
            g_ref, d_ref, nm_ref, nv_ref = outs[4 * k:4 * k + 4]
            g_ref[...] = g
            d_ref[...], nm_ref[...], nv_ref[...] = _adamw(w_ref[...], g, m_ref[...], v_ref[...])
        outs[-2][...] = total(c0, cn)
        outs[-1][...] = total(*seg["loss"])

    flat_in = [a for nm in names for a in params[nm]]
    out_shape = []
    for nm in names:
        out_shape += [SDS(params[nm][0].shape, F32)] * 4
    out_shape += [SDS((cn, LANES), F32), SDS((seg["loss"][1], LANES), F32)]
    res = _pcall(body, out_shape=tuple(out_shape), name="adamw_small")(gathered, *flat_in)
    per = {nm: res[4 * k:4 * k + 4] for k, nm in enumerate(names)}
    return per, res[-2], res[-1]


def _adamw_one(w, g, m, v, name):
    def body(w_ref, g_ref, m_ref, v_ref, d_ref, nm_ref, nv_ref):
        d_ref[...], nm_ref[...], nv_ref[...] = _adamw(w_ref[...], g_ref[...], m_ref[...], v_ref[...])

    return _pcall(body, out_shape=(SDS(w.shape, F32),) * 3, name=name)(w, g, m, v)


def _rows128(a):
    return a.reshape(-1, LANES)


def _pack_small(gs, loss_tile):
    seg, pieces, row = {}, [], 0
    for nm in SMALL + ("conv_w", "loss"):
        piece = loss_tile if nm == "loss" else _rows128(gs[nm])
        rn = _round_up(piece.shape[0], SUB)
        pieces.append(jnp.pad(piece, ((0, rn - piece.shape[0]), (0, 0))))
        seg[nm] = (row, piece.shape[0])
        row += rn
    return jnp.concatenate(pieces, axis=0), seg


def _step(x, mem, target, wb, conv_w, sp, pos):
    s, d = x.shape
    tm = min(TOKEN_TILE, s)
    tm_wide = min(2 * TOKEN_TILE, s)
    rows = lambda w8: w8.reshape(-1, w8.shape[2])
    shards = lambda g: g.reshape((N_DEV, -1) + g.shape[1:])
    bt = sp["b_spatial"].T

    (w_in8, conv8), = _run_exchanges([_all_gather([wb["w_in"], conv_w])], "gather_w_in")
    conv_full = conv8.transpose(1, 0, 2).reshape(3, -1)
    w_in_t = rows(w_in8)
    (xn1, h), ((w_out8, w_kv8, w_q8),) = _in_forward(
        x, sp["ln_mix_g"], w_in_t, tm_wide, carry=[_all_gather([wb["w_out"], wb["w_kv"], wb["w_q"]])])
    w_out = rows(w_out8)
    (ycat, x1), ((w_o8, w_down8),) = _mix_forward(
        h, x, sp["sgu_ln_g"], sp["sgu_ln_b"], sp["w_spatial"], bt, conv_full, sp["grp_norm_a"], sp["grp_norm_b"], w_out, tm,
        carry=[_all_gather([wb["w_o"], wb["w_down"]])])
    w_q, w_o, w_down = rows(w_q8), rows(w_o8), rows(w_down8)
    memn, kv = _kv_forward(mem, sp["ln_mem_g"], w_kv8)
    (xn2, q, o, x2), ((w_gu8,),) = _attn_forward(
        x1, sp["ln_attn_g"], w_q, kv, w_o, tm_wide, carry=[_all_gather([wb["w_gate_up"]])])
    w_gu = w_gu8.reshape((2, N_DEV // 2) + w_gu8.shape[1:])
    xn3, gu, x3 = _ffn_forward(x2, sp["ln_ffn_g"], w_gu, w_down, tm_wide)

    loss, d_lnf, dx3, dx3b = _final_backward(x3, target, sp["ln_final_g"], tm_wide)
    act, dgu, dxn3 = _swiglu_backward(dx3b, gu, w_gu, w_down, tm_wide)
    part = {}
    g_gu, _ = _wgrad_blocked_lhs(dgu.reshape((N_DEV,) + dgu.shape[2:]), xn3, "wgrad_gate_up")
    g_gu = shards(g_gu)
    g_down, ((rc_gu,),) = _wgrad_blocked_lhs(act, dx3b, "wgrad_down", carry=[_exchange_c([g_gu])])
    g_down = shards(g_down)
    (keep_gu, send_gu), _ = _rs_combine(g_gu, rc_gu, pos, "rs_combine_w_gate_up")
    (dx2b, dq, dx1, dkv, d_lnattn, d_lnffn), ((rxy_gu,), (rc_down,)) = _attn_backward(
        dx3, dxn3, x2, sp["ln_ffn_g"], x1, sp["ln_attn_g"], q, kv, w_q, w_o, tm,
        carry=[_exchange_xy([send_gu]), _exchange_c([g_down])])
    part["w_gate_up"] = (keep_gu, rxy_gu)
    (keep_down, send_down), _ = _rs_combine(g_down, rc_down, pos, "rs_combine_w_down")
    g_o, _ = _wgrad(o, dx2b, "wgrad_o")
    g_o = shards(g_o)
    g_q, ((rc_o,),) = _wgrad(xn2, dq, "wgrad_q", carry=[_exchange_c([g_o])])
    g_q = shards(g_q)
    g_kv, d_lnmem = _kv_backward(dkv, memn, mem, sp["ln_mem_g"], w_kv8)
    (keep_o, send_o), ((rc_q,),) = _rs_combine(g_o, rc_o, pos, "rs_combine_w_o", carry=[_exchange_c([g_q])])
    (keep_q, send_q), _ = _rs_combine(g_q, rc_q, pos, "rs_combine_w_q")
    ((dx1b, dh, dx, d_ga, d_gb, d_cw, d_lng, d_lnb, d_wsp, d_bs, d_lnmix),
     ((rxy_down, rxy_o, rxy_q), (rc_kv,))) = _mix_backward(
        dx1, x, sp["ln_mix_g"], h, sp["sgu_ln_g"], sp["sgu_ln_b"], sp["w_spatial"], bt, conv_full,
        sp["grp_norm_a"], sp["grp_norm_b"], w_out, w_in_t, tm,
        carry=[_exchange_xy([send_down, send_o, send_q]), _exchange_c([g_kv])])
    part["w_down"], part["w_o"], part["w_q"] = (keep_down, rxy_down), (keep_o, rxy_o), (keep_q, rxy_q)
    (keep_kv, send_kv), _ = _rs_combine(g_kv, rc_kv, pos, "rs_combine_w_kv")
    gs = {"ln_mix_g": d_lnmix, "sgu_ln_g": d_lng, "sgu_ln_b": d_lnb, "w_spatial": d_wsp, "b_spatial": _bias_grad(d_bs),
          "conv_w": d_cw[:3], "grp_norm_a": d_ga, "grp_norm_b": d_gb, "ln_attn_g": d_lnattn, "ln_mem_g": d_lnmem,
          "ln_ffn_g": d_lnffn, "ln_final_g": d_lnf}
    packed, seg = _pack_small(gs, loss)
    g_in, ((rxy_kv,), (small_all,)) = _wgrad(
        dh, xn1, "wgrad_in", carry=[_exchange_xy([send_kv]), _all_gather([packed])])
    g_in = shards(g_in)
    part["w_kv"] = (keep_kv, rxy_kv)
    g_out, ((rc_in,),) = _wgrad(ycat, dx1b, "wgrad_out", carry=[_exchange_c([g_in])])
    g_out = shards(g_out)
    (keep_in, send_in), ((rc_out,),) = _rs_combine(g_in, rc_in, pos, "rs_combine_w_in", carry=[_exchange_c([g_out])])
    (keep_out, send_out), _ = _rs_combine(g_out, rc_out, pos, "rs_combine_w_out")
    (rxy_in, rxy_out), = _run_exchanges([_exchange_xy([send_in, send_out])], "exchange_xy_w_in_w_out")
    part["w_in"], part["w_out"] = (keep_in, rxy_in), (keep_out, rxy_out)
    return dx, part, small_all, seg


def kernel(x, mem, ln_mix_g, w_in, sgu_ln_g, sgu_ln_b, w_spatial, b_spatial, conv_w, grp_norm_a, grp_norm_b, w_out, ln_attn_g, ln_mem_g, w_q, w_kv, w_o, ln_ffn_g, w_gate_up, w_down, ln_final_g, loss_target, m_ln_mix_g, m_w_in, m_sgu_ln_g, m_sgu_ln_b, m_w_spatial, m_b_spatial, m_conv_w, m_grp_norm_a, m_grp_norm_b, m_w_out, m_ln_attn_g, m_ln_mem_g, m_w_q, m_w_kv, m_w_o, m_ln_ffn_g, m_w_gate_up, m_w_down, m_ln_final_g, v_ln_mix_g, v_w_in, v_sgu_ln_g, v_sgu_ln_b, v_w_spatial, v_b_spatial, v_conv_w, v_grp_norm_a, v_grp_norm_b, v_w_out, v_ln_attn_g, v_ln_mem_g, v_w_q, v_w_kv, v_w_o, v_ln_ffn_g, v_w_gate_up, v_w_down, v_ln_final_g):
    order = ["ln_mix_g", "w_in", "sgu_ln_g", "sgu_ln_b", "w_spatial", "b_spatial", "conv_w", "grp_norm_a", "grp_norm_b",
             "w_out", "ln_attn_g", "ln_mem_g", "w_q", "w_kv", "w_o", "ln_ffn_g", "w_gate_up", "w_down", "ln_final_g"]
    W = dict(ln_mix_g=ln_mix_g, w_in=w_in, sgu_ln_g=sgu_ln_g, sgu_ln_b=sgu_ln_b, w_spatial=w_spatial, b_spatial=b_spatial,
             conv_w=conv_w, grp_norm_a=grp_norm_a, grp_norm_b=grp_norm_b, w_out=w_out, ln_attn_g=ln_attn_g,
             ln_mem_g=ln_mem_g, w_q=w_q, w_kv=w_kv, w_o=w_o, ln_ffn_g=ln_ffn_g, w_gate_up=w_gate_up, w_down=w_down,
             ln_final_g=ln_final_g)
    M = dict(ln_mix_g=m_ln_mix_g, w_in=m_w_in, sgu_ln_g=m_sgu_ln_g, sgu_ln_b=m_sgu_ln_b, w_spatial=m_w_spatial,
             b_spatial=m_b_spatial, conv_w=m_conv_w, grp_norm_a=m_grp_norm_a, grp_norm_b=m_grp_norm_b, w_out=m_w_out,
             ln_attn_g=m_ln_attn_g, ln_mem_g=m_ln_mem_g, w_q=m_w_q, w_kv=m_w_kv, w_o=m_w_o, ln_ffn_g=m_ln_ffn_g,
             w_gate_up=m_w_gate_up, w_down=m_w_down, ln_final_g=m_ln_final_g)
    V = dict(ln_mix_g=v_ln_mix_g, w_in=v_w_in, sgu_ln_g=v_sgu_ln_g, sgu_ln_b=v_sgu_ln_b, w_spatial=v_w_spatial,
             b_spatial=v_b_spatial, conv_w=v_conv_w, grp_norm_a=v_grp_norm_a, grp_norm_b=v_grp_norm_b, w_out=v_w_out,
             ln_attn_g=v_ln_attn_g, ln_mem_g=v_ln_mem_g, w_q=v_w_q, w_kv=v_w_kv, w_o=v_w_o, ln_ffn_g=v_ln_ffn_g,
             w_gate_up=v_w_gate_up, w_down=v_w_down, ln_final_g=v_ln_final_g)

    bw = conv_w.shape[1] * N_DEV
    pos = jnp.stack([lax.axis_index("x"), lax.axis_index("y"), lax.axis_index("c")]).astype(jnp.int32)
    me = 4 * pos[0] + 2 * pos[1] + pos[2]

    sp = {nm: (W[nm].reshape(1, -1) if W[nm].ndim == 1 else W[nm]) for nm in SMALL}
    view = lambda a, nm: a.T if nm in TRANSPOSED else a
    wb = {nm: view(W[nm], nm).astype(BF16) for nm in BIG}
    grad_x, part, small_all, seg = _step(x[0], mem[0], loss_target[0], wb, conv_w, sp, pos)

    out = {}
    for nm in BIG:
        res = _adamw_shard(part[nm][0], part[nm][1], view(W[nm], nm), view(M[nm], nm), view(V[nm], nm), "adamw_" + nm)
        out[nm] = tuple(view(a, nm) for a in res)

    params = {nm: (_rows128(W[nm]), _rows128(M[nm]), _rows128(V[nm])) for nm in SMALL}
    per, conv_g_rows, loss_sum = _adamw_small(small_all, seg, params, seg["conv_w"])
    for nm in SMALL:
        out[nm] = tuple(a.reshape(W[nm].shape) for a in per[nm])
    conv_g = lax.dynamic_slice_in_dim(conv_g_rows.reshape(3, bw), me * conv_w.shape[1], conv_w.shape[1], axis=1)
    out["conv_w"] = (conv_g,) + tuple(_adamw_one(conv_w, conv_g, m_conv_w, v_conv_w, "adamw_conv"))

    loss = loss_sum[0, 0]
    res = [loss, grad_x[None]]
    for k in range(4):
        res += [out[nm][k] for nm in order]
    return tuple(res)
```

```python
import functools

import jax
import jax.numpy as jnp
from jax import lax
from jax.experimental import pallas as pl
from jax.experimental.pallas import tpu as pltpu

F32 = jnp.float32
BF16 = jnp.bfloat16
SDS = jax.ShapeDtypeStruct
MESH = pl.DeviceIdType.MESH

EPS = 1e-6
N_DEV = 8
HEADS = 4
CHUNK = 128
HALO = 16
SUB = 8
LANES = 128
TOKEN_TILE = 512
ROW_CHUNK = 256
RELAY_AT = 0.7

ADAM_LR = 0.001
ADAM_B1 = 0.9
ADAM_B2 = 0.999
ADAM_EPS = 1e-08
ADAM_WD = 0.01
ADAM_STEP = 10

BIG = ("w_in", "w_out", "w_q", "w_kv", "w_o", "w_gate_up", "w_down")
TRANSPOSED = ("w_in", "w_gate_up")
SMALL = ("ln_mix_g", "sgu_ln_g", "sgu_ln_b", "w_spatial", "b_spatial", "grp_norm_a", "grp_norm_b",
         "ln_attn_g", "ln_mem_g", "ln_ffn_g", "ln_final_g")


class _Exchange:
    def __init__(self, ins, out_shape, sems, start, finish, relay=None):
        self.ins, self.out_shape, self.sems = list(ins), list(out_shape), list(sems)
        self.start, self.finish, self.relay = start, finish, relay


def _pcall(body, carry=(), n_prefetch=0, **kw):
    if carry:
        return functools.partial(_carrying_call, body, tuple(carry), n_prefetch, kw)
    if n_prefetch:
        kw["grid_spec"] = pltpu.PrefetchScalarGridSpec(
            num_scalar_prefetch=n_prefetch, grid=kw.pop("grid"), in_specs=kw.pop("in_specs"),
            out_specs=kw.pop("out_specs"), scratch_shapes=kw.pop("scratch_shapes", ()))
    return pl.pallas_call(body, **kw)


def _carrying_call(body, carry, n_prefetch, kw, *args):
    kw = dict(kw)
    out_shape = kw.pop("out_shape")
    single = not isinstance(out_shape, (tuple, list))
    outs_shape = (out_shape,) if single else tuple(out_shape)
    out_specs = kw.pop("out_specs")
    out_specs = [out_specs] if single else list(out_specs)
    in_specs = list(kw.pop("in_specs"))
    scratch = list(kw.pop("scratch_shapes", ()))
    grid = tuple(kw.get("grid", ()))
    n_in, n_out, n_scr = len(args), len(outs_shape), len(scratch)

    def split(refs, k, counts):
        parts = []
        for cnt in counts:
            parts.append(refs[k:k + cnt])
            k += cnt
        return parts, k

    def wrapped(*refs):
        cins, k = split(refs, n_in, [len(p.ins) for p in carry])
        outs = refs[k:k + n_out]
        couts, k = split(refs, k + n_out, [len(p.out_shape) for p in carry])
        scr = refs[k:k + n_scr]
        csems, _ = split(refs, k + n_scr, [len(p.sems) for p in carry])
        first, last = True, True
        for a, g in enumerate(grid):
            first = (pl.program_id(a) == 0) & first
            last = (pl.program_id(a) == g - 1) & last

        def start_all():
            for p, ci, co, cs in zip(carry, cins, couts, csems):
                p.start(ci, co, cs)

        def relay_all():
            for p, ci, co, cs in zip(carry, cins, couts, csems):
                if p.relay is not None:
                    p.relay(ci, co, cs)

        def finish_all():
            for p, ci, co, cs in zip(carry, cins, couts, csems):
                p.finish(ci, co, cs)

        if len(grid) == 1:
            relay_now = pl.program_id(0) == min(int(RELAY_AT * grid[0]), grid[0] - 1)
        else:
            relay_now = last
        start_all() if not grid else pl.when(first)(start_all)
        relay_all() if not grid else pl.when(relay_now)(relay_all)
        body(*refs[:n_in], *outs, *scr)
        finish_all() if not grid else pl.when(last)(finish_all)

    c_in = [a for p in carry for a in p.ins]
    c_out = [s for p in carry for s in p.out_shape]
    c_sems = [s for p in carry for s in p.sems]
    res = _pcall(wrapped, n_prefetch=n_prefetch, out_shape=outs_shape + tuple(c_out),
                 in_specs=in_specs + _hbm_specs(len(c_in)), out_specs=out_specs + _hbm_specs(len(c_out)),
                 scratch_shapes=scratch + c_sems, **kw)(*args, *c_in)
    own = res[0] if single else tuple(res[:n_out])
    landed, k = [], n_out
    for p in carry:
        landed.append(list(res[k:k + len(p.out_shape)]))
        k += len(p.out_shape)
    return own, landed


def _hbm_specs(n):
    return [pl.BlockSpec(memory_space=pl.ANY)] * n


def _hosted(body, carry, **kw):
    if carry:
        return _pcall(body, carry=carry, **kw)
    call = _pcall(body, **kw)
    return lambda *args: (call(*args), [])


def _run_exchanges(parts, name):
    def body(*refs):
        pass

    _, landed = _pcall(body, carry=parts, out_shape=(), in_specs=[], out_specs=[], name=name)()
    return landed


def _arb(n):
    return pltpu.CompilerParams(dimension_semantics=("arbitrary",) * n)


def _tile(n, target, mult):
    best = None
    for t in range(mult, min(n, target) + 1, mult):
        if n % t == 0:
            best = t
    return n if best is None else best


def _round_up(n, m):
    return (n + m - 1) // m * m


def _dot(a, b):
    return jnp.dot(a, b, preferred_element_type=F32)


def _dot_nt(a, b):
    return lax.dot_general(a, b, (((1,), (1,)), ((), ())), preferred_element_type=F32)


def _dot_tn(a, b):
    return lax.dot_general(a, b, (((0,), (0,)), ((), ())), preferred_element_type=F32)


def _rstd(x):
    return lax.rsqrt(jnp.mean(x * x, axis=-1, keepdims=True) + EPS)


def _rms_bwd(dy, x, r, g):
    gdy = dy * g
    proj = jnp.sum(gdy * x, axis=-1, keepdims=True) * (1.0 / x.shape[-1])
    dx = r * gdy - x * (r * r * r) * proj
    dg = jnp.sum(dy * (x * r), axis=0, keepdims=True)
    return dx, dg


_GELU_C = 0.7978845608028654
_GELU_A = 0.044715


def _gelu(x):
    t = jnp.tanh(_GELU_C * (x + _GELU_A * x * x * x))
    return 0.5 * x * (1.0 + t), t


def _gelu_grad(x, t):
    return 0.5 * (1.0 + t) + 0.5 * x * (1.0 - t * t) * (_GELU_C * (1.0 + 3.0 * _GELU_A * x * x))


def _sigmoid(x):
    return 1.0 / (1.0 + jnp.exp(-x))


def _softmax(s):
    m = jnp.max(s, axis=-1, keepdims=True)
    e = jnp.exp(s - m)
    return e / jnp.sum(e, axis=-1, keepdims=True)


def _adamw(w, g, m, v):
    m = ADAM_B1 * m + (1.0 - ADAM_B1) * g
    v = ADAM_B2 * v + (1.0 - ADAM_B2) * (g * g)
    m_hat = m / (1.0 - ADAM_B1 ** ADAM_STEP)
    v_hat = v / (1.0 - ADAM_B2 ** ADAM_STEP)
    delta = -ADAM_LR * (m_hat / (jnp.sqrt(v_hat) + ADAM_EPS) + ADAM_WD * w)
    return delta, m, v


def _tril_mask():
    t = lax.broadcasted_iota(jnp.int32, (CHUNK, CHUNK), 0)
    s = lax.broadcasted_iota(jnp.int32, (CHUNK, CHUNK), 1)
    return (s <= t).astype(F32)


def _sgu_forward(ha, lng, lnb, wm, bt, mixed_s):
    aw = ha.shape[1] // 2
    hd = aw // HEADS
    a, th = _gelu(ha)
    u = a[:, :aw]
    v = a[:, aw:]
    mu = jnp.mean(v, axis=-1, keepdims=True)
    vc = v - mu
    rl = lax.rsqrt(jnp.mean(vc * vc, axis=-1, keepdims=True) + EPS)
    xhat = vc * rl
    vln = (xhat * lng + lnb).astype(BF16)
    for n in range(ha.shape[0] // CHUNK):
        rows = slice(n * CHUNK, (n + 1) * CHUNK)
        for h in range(HEADS):
            cols = slice(h * hd, (h + 1) * hd)
            mixed_s[rows, cols] = _dot(wm[h], vln[rows, cols]) + bt[:, h:h + 1]
    return th, u, xhat, rl, vln


def _conv_taps(zext):
    return pltpu.roll(zext, 2, 0), pltpu.roll(zext, 1, 0)


def _kv_forward(mem, g_mem, w_kv):
    ml, d = mem.shape
    xd = w_kv.shape[2]

    def body(mem_ref, g_ref, w_ref, memn_ref, kv_ref):
        x = mem_ref[...]
        memn = (x * _rstd(x) * g_ref[...]).astype(BF16)
        memn_ref[...] = memn
        for j in range(2 * HEADS):
            kv_ref[j] = _dot(memn, w_ref[j]).astype(BF16)

    return _pcall(body, out_shape=(SDS((ml, d), BF16), SDS((2 * HEADS, ml, xd), BF16)), name="kv_forward")(mem, g_mem, w_kv)


def _in_forward(x, g, w_in_t, tm, carry=()):
    s, d = x.shape
    n_in = w_in_t.shape[0]

    def body(x_ref, g_ref, w_ref, xn_ref, h_ref):
        xv = x_ref[...]
        xn = (xv * _rstd(xv) * g_ref[...]).astype(BF16)
        xn_ref[...] = xn
        h_ref[...] = _dot_nt(xn, w_ref[...])

    return _hosted(
        body, carry, grid=(s // tm,),
        in_specs=[pl.BlockSpec((tm, d), lambda i: (i, 0)), pl.BlockSpec((1, d), lambda i: (0, 0)),
                  pl.BlockSpec((n_in, d), lambda i: (0, 0))],
        out_specs=[pl.BlockSpec((tm, d), lambda i: (i, 0)), pl.BlockSpec((tm, n_in), lambda i: (i, 0))],
        out_shape=(SDS((s, d), BF16), SDS((s, n_in), F32)),
        compiler_params=_arb(1), name="in_forward")(x, g, w_in_t)


def _mix_forward(h, x, lng, lnb, w_sp, bt, conv_w, ga, gb, w_out, tm, carry=()):
    s, d = x.shape
    n_in = h.shape[1]
    aw = lng.shape[1]
    bw = d - aw
    in_a = 2 * aw
    hb_blocks = tm // HALO

    def body(h_ref, hprev_ref, x_ref, lng_ref, lnb_ref, wsp_ref, bt_ref, cw_ref, ga_ref, gb_ref, wout_ref,
             ycat_ref, x1_ref, mixed_s):
        i = pl.program_id(0)
        mask = _tril_mask()
        wm = [(wsp_ref[hh] * mask).astype(BF16) for hh in range(HEADS)]
        hv = h_ref[...]
        _, u, _, _, _ = _sgu_forward(hv[:, :in_a], lng_ref[...], lnb_ref[...], wm, bt_ref[...], mixed_s)
        sg = u * mixed_s[...]
        ycat_ref[:, :aw] = (sg * _rstd(sg) * ga_ref[...]).astype(BF16)

        gate_b = hv[:, in_a:in_a + bw]
        z = hv[:, in_a + bw:in_a + 2 * bw] * hv[:, in_a + 2 * bw:]
        hp = hprev_ref[...]
        zp = hp[:, in_a + bw:in_a + 2 * bw] * hp[:, in_a + 2 * bw:]
        zp = jnp.where(i == 0, 0.0, zp)
        zext = jnp.concatenate([zp, z], axis=0)
        z2, z1 = _conv_taps(zext)
        cw = cw_ref[...]
        conv = cw[0:1] * z2[HALO:] + cw[1:2] * z1[HALO:] + cw[2:3] * z
        sc = gate_b * conv
        ycat_ref[:, aw:] = (sc * _rstd(sc) * gb_ref[...]).astype(BF16)
        x1_ref[...] = x_ref[...] + _dot(ycat_ref[...], wout_ref[...])

    full = lambda shape: pl.BlockSpec(shape, lambda i: (0,) * len(shape))
    return _hosted(
        body, carry, grid=(s // tm,),
        in_specs=[pl.BlockSpec((tm, n_in), lambda i: (i, 0)),
                  pl.BlockSpec((HALO, n_in), lambda i: (jnp.maximum(i * hb_blocks - 1, 0), 0)),
                  pl.BlockSpec((tm, d), lambda i: (i, 0)),
                  full((1, aw)), full((1, aw)), full((HEADS, CHUNK, CHUNK)), full((CHUNK, HEADS)),
                  full((3, bw)), full((1, aw)), full((1, bw)), full((d, d))],
        out_specs=[pl.BlockSpec((tm, d), lambda i: (i, 0)), pl.BlockSpec((tm, d), lambda i: (i, 0))],
        out_shape=(SDS((s, d), BF16), SDS((s, d), F32)),
        scratch_shapes=[pltpu.VMEM((tm, aw), F32)],
        compiler_params=_arb(1), name="mix_forward")(h, h, x, lng, lnb, w_sp, bt, conv_w, ga, gb, w_out)


def _attn_forward(x1, g, w_q, kv, w_o, tm, carry=()):
    s, d = x1.shape
    _, ml, xd = kv.shape
    scale = xd ** -0.5

    def body(x1_ref, g_ref, wq_ref, kv_ref, wo_ref, xn_ref, q_ref, o_ref, x2_ref):
        xv = x1_ref[...]
        xn = (xv * _rstd(xv) * g_ref[...]).astype(BF16)
        xn_ref[...] = xn
        q_ref[...] = _dot(xn, wq_ref[...]).astype(BF16)
        for hh in range(HEADS):
            cols = slice(hh * xd, (hh + 1) * xd)
            p = _softmax(_dot_nt(q_ref[:, cols], kv_ref[hh]) * scale)
            o_ref[:, cols] = _dot(p.astype(BF16), kv_ref[HEADS + hh]).astype(BF16)
        x2_ref[...] = xv + _dot(o_ref[...], wo_ref[...])

    tok = pl.BlockSpec((tm, d), lambda i: (i, 0))
    return _hosted(
        body, carry, grid=(s // tm,),
        in_specs=[tok, pl.BlockSpec((1, d), lambda i: (0, 0)), pl.BlockSpec((d, d), lambda i: (0, 0)),
                  pl.BlockSpec((2 * HEADS, ml, xd), lambda i: (0, 0, 0)), pl.BlockSpec((d, d), lambda i: (0, 0))],
        out_specs=[tok, tok, tok, tok],
        out_shape=(SDS((s, d), BF16), SDS((s, d), BF16), SDS((s, d), BF16), SDS((s, d), F32)),
        compiler_params=_arb(1), name="attn_forward")(x1, g, w_q, kv, w_o)


def _ffn_forward(x2, g, w_gu, w_down, tm):
    s, d = x2.shape
    _, nf, tf, _ = w_gu.shape

    def body(x2_ref, g_ref, wgu_ref, wd_ref, xn_ref, gu_ref, x3_ref):
        f = pl.program_id(1)

        @pl.when(f == 0)
        def _():
            xv = x2_ref[...]
            xn_ref[...] = (xv * _rstd(xv) * g_ref[...]).astype(BF16)
            x3_ref[...] = xv

        xn = xn_ref[...]
        gate = _dot_nt(xn, wgu_ref[0])
        up = _dot_nt(xn, wgu_ref[1])
        gu_ref[0] = gate.astype(BF16)
        gu_ref[1] = up.astype(BF16)
        act = (gate * _sigmoid(gate) * up).astype(BF16)
        x3_ref[...] += _dot(act, wd_ref[...])

    tok = pl.BlockSpec((tm, d), lambda i, f: (i, 0))
    return _pcall(
        body, grid=(s // tm, nf),
        in_specs=[tok, pl.BlockSpec((1, d), lambda i, f: (0, 0)),
                  pl.BlockSpec((2, None, tf, d), lambda i, f: (0, f, 0, 0)),
                  pl.BlockSpec((tf, d), lambda i, f: (f, 0))],
        out_specs=[tok, pl.BlockSpec((2, None, tm, tf), lambda i, f: (0, f, i, 0)), tok],
        out_shape=(SDS((s, d), BF16), SDS((2, nf, s, tf), BF16), SDS((s, d), F32)),
        compiler_params=_arb(2), name="ffn_forward")(x2, g, w_gu, w_down)


def _final_backward(x3, target, g_final, tm):
    s, d = x3.shape

    def body(x3_ref, tgt_ref, gf_ref, loss_ref, dgf_ref, dx3_ref, dx3b_ref):
        @pl.when(pl.program_id(0) == 0)
        def _():
            loss_ref[...] = jnp.zeros_like(loss_ref)
            dgf_ref[...] = jnp.zeros_like(dgf_ref)

        xv = x3_ref[...]
        r = _rstd(xv)
        diff = xv * r * gf_ref[...] - tgt_ref[...]
        loss_ref[...] += 0.5 * jnp.sum(jnp.sum(diff * diff, axis=-1, keepdims=True), axis=0, keepdims=True) * (1.0 / d)
        dx3, dgf = _rms_bwd(diff * (1.0 / d), xv, r, gf_ref[...])
        dgf_ref[...] += dgf
        dx3_ref[...] = dx3
        dx3b_ref[...] = dx3.astype(BF16)

    tok = pl.BlockSpec((tm, d), lambda i: (i, 0))
    vec = pl.BlockSpec((1, d), lambda i: (0, 0))
    return _pcall(
        body, grid=(s // tm,), in_specs=[tok, tok, vec],
        out_specs=[pl.BlockSpec((SUB, LANES), lambda i: (0, 0)), vec, tok, tok],
        out_shape=(SDS((SUB, LANES), F32), SDS((1, d), F32), SDS((s, d), F32), SDS((s, d), BF16)),
        compiler_params=_arb(1), name="final_backward")(x3, target, g_final)


def _swiglu_backward(dx3b, gu, w_gu, w_down, tm):
    s, d = dx3b.shape
    _, nf, tf, _ = w_gu.shape

    def body(dx3b_ref, gu_ref, wgu_ref, wd_ref, act_ref, dgu_ref, dxn_ref):
        @pl.when(pl.program_id(1) == 0)
        def _():
            dxn_ref[...] = jnp.zeros_like(dxn_ref)

        for r0 in range(0, tm, ROW_CHUNK):
            rows = slice(r0, r0 + ROW_CHUNK)
            dact = _dot_nt(dx3b_ref[rows, :], wd_ref[...])
            gv = gu_ref[0, rows, :].astype(F32)
            uv = gu_ref[1, rows, :].astype(F32)
            sg = _sigmoid(gv)
            silu = gv * sg
            act_ref[rows, :] = (silu * uv).astype(BF16)
            dgate = (dact * uv * (sg * (1.0 + gv * (1.0 - sg)))).astype(BF16)
            dup = (dact * silu).astype(BF16)
            dgu_ref[0, rows, :] = dgate
            dgu_ref[1, rows, :] = dup
            part = _dot(dgate, wgu_ref[0]) + _dot(dup, wgu_ref[1])
            dxn_ref[rows, :] += part

    tok = pl.BlockSpec((tm, d), lambda i, f: (i, 0))
    pair = pl.BlockSpec((2, None, tm, tf), lambda i, f: (0, f, i, 0))
    return _pcall(
        body, grid=(s // tm, nf),
        in_specs=[tok, pair, pl.BlockSpec((2, None, tf, d), lambda i, f: (0, f, 0, 0)),
                  pl.BlockSpec((tf, d), lambda i, f: (f, 0))],
        out_specs=[pl.BlockSpec((None, tm, tf), lambda i, f: (f, i, 0)), pair, tok],
        out_shape=(SDS((nf, s, tf), BF16), SDS((2, nf, s, tf), BF16), SDS((s, d), F32)),
        compiler_params=_arb(2), name="swiglu_backward")(dx3b, gu, w_gu, w_down)


def _attn_backward(dx3, dxn3, x2, g_ffn, x1, g, q, kv, w_q, w_o, tm, carry=()):
    s, d = x1.shape
    _, ml, xd = kv.shape
    scale = xd ** -0.5

    def body(dx3_ref, dxn3_ref, x2_ref, g2_ref, x1_ref, g_ref, q_ref, kv_ref, wq_ref, wo_ref,
             dx2b_ref, dq_ref, dx1_ref, dx1b_ref, dkv_ref, dg_ref, dg2_ref, do_s):
        i = pl.program_id(0)

        @pl.when(i == 0)
        def _():
            dkv_ref[...] = jnp.zeros_like(dkv_ref)
            dg_ref[...] = jnp.zeros_like(dg_ref)
            dg2_ref[...] = jnp.zeros_like(dg2_ref)

        x2v = x2_ref[...]
        dx2n, dg2 = _rms_bwd(dxn3_ref[...], x2v, _rstd(x2v), g2_ref[...])
        dg2_ref[...] += dg2
        dx2 = dx3_ref[...] + dx2n
        dx2b_ref[...] = dx2.astype(BF16)
        do_s[...] = _dot_nt(dx2b_ref[...], wo_ref[...]).astype(BF16)
        for hh in range(HEADS):
            kc = slice(hh * xd, (hh + 1) * xd)
            qh = q_ref[:, kc]
            kh = kv_ref[hh]
            doh = do_s[:, kc]
            p = _softmax(_dot_nt(qh, kh) * scale)
            dp = _dot_nt(doh, kv_ref[HEADS + hh])
            dkv_ref[HEADS + hh] += _dot_tn(p.astype(BF16), doh)
            ds = (p * (dp - jnp.sum(dp * p, axis=-1, keepdims=True)) * scale).astype(BF16)
            dq_ref[:, kc] = _dot(ds, kh).astype(BF16)
            dkv_ref[hh] += _dot_tn(ds, qh)
        dg_tile = jnp.zeros_like(dg_ref)
        for r0 in range(0, tm, min(ROW_CHUNK, tm)):
            rows = slice(r0, r0 + min(ROW_CHUNK, tm))
            dxn = _dot_nt(dq_ref[rows, :], wq_ref[...])
            xv = x1_ref[rows, :]
            dx, dg = _rms_bwd(dxn, xv, _rstd(xv), g_ref[...])
            dg_tile = dg_tile + dg
            dx1 = dx2[rows] + dx
            dx1_ref[rows, :] = dx1
            dx1b_ref[rows, :] = dx1.astype(BF16)
        dg_ref[...] += dg_tile

    tok = pl.BlockSpec((tm, d), lambda i: (i, 0))
    vec = pl.BlockSpec((1, d), lambda i: (0, 0))
    sq = pl.BlockSpec((d, d), lambda i: (0, 0))
    kvs = pl.BlockSpec((2 * HEADS, ml, xd), lambda i: (0, 0, 0))
    return _hosted(
        body, carry, grid=(s // tm,),
        in_specs=[tok, tok, tok, vec, tok, vec, tok, kvs, sq, sq],
        out_specs=[tok, tok, tok, tok, kvs, vec, vec],
        out_shape=(SDS((s, d), BF16), SDS((s, d), BF16), SDS((s, d), F32), SDS((s, d), BF16),
                   SDS((2 * HEADS, ml, xd), F32), SDS((1, d), F32), SDS((1, d), F32)),
        scratch_shapes=[pltpu.VMEM((tm, d), BF16)],
        compiler_params=_arb(1), name="attn_backward")(dx3, dxn3, x2, g_ffn, x1, g, q, kv, w_q, w_o)


def _kv_backward(dkv, memn, mem, g_mem, w_kv):
    ml, d = mem.shape
    xd = w_kv.shape[2]

    def body(dkv_ref, memn_ref, mem_ref, g_ref, w_ref, dw_ref, dg_ref):
        dmemn = jnp.zeros((ml, d), F32)
        for j in range(2 * HEADS):
            dkvb = dkv_ref[j].astype(BF16)
            dw_ref[j] = _dot_tn(memn_ref[...], dkvb)
            dmemn = dmemn + _dot_nt(dkvb, w_ref[j])
        x = mem_ref[...]
        dg_ref[...] = jnp.sum(dmemn * (x * _rstd(x)), axis=0, keepdims=True)

    return _pcall(body, out_shape=(SDS((2 * HEADS, d, xd), F32), SDS((1, d), F32)), name="kv_backward")(dkv, memn, mem, g_mem, w_kv)


def _mix_backward(dx1, x, g_mix, h, lng, lnb, w_sp, bt, conv_w, ga, gb, w_out, w_in, tm, carry=()):
    s, d = x.shape
    n_in = h.shape[1]
    aw = lng.shape[1]
    bw = d - aw
    hd = aw // HEADS
    in_a = 2 * aw
    hb_blocks = tm // HALO
    last_blk = s // HALO - 1
    nt = s // tm
    te = tm + HALO
    tee = tm + 2 * HALO

    def body(dx1_ref, dx1n_ref, x_ref, gm_ref, h_ref, hp_ref, hn_ref, lng_ref, lnb_ref, wsp_ref, bt_ref, cw_ref,
             ga_ref, gb_ref, wout_ref, win_ref,
             dh_ref, dx_ref, dga_ref, dgb_ref, dcw_ref, dlng_ref, dlnb_ref, dwsp_ref, dbs_ref, dgm_ref,
             mixed_s, dvln_s):
        i = pl.program_id(0)

        @pl.when(i == 0)
        def _():
            for ref in (dga_ref, dgb_ref, dcw_ref, dlng_ref, dlnb_ref, dwsp_ref, dbs_ref, dgm_ref):
                ref[...] = jnp.zeros_like(ref)

        mask = _tril_mask()
        wm = [(wsp_ref[hh] * mask).astype(BF16) for hh in range(HEADS)]
        hv = h_ref[...]
        dx1 = dx1_ref[...]
        dx1e = jnp.concatenate([dx1, dx1n_ref[...]], axis=0).astype(BF16)
        dycat = _dot_nt(dx1e, wout_ref[...])

        hbe = jnp.concatenate([hp_ref[:, in_a:], hv[:, in_a:], hn_ref[:, in_a:]], axis=0)
        row = lax.broadcasted_iota(jnp.int32, (tee, 1), 0)
        zext = hbe[:, bw:2 * bw] * hbe[:, 2 * bw:]
        zext = jnp.where((i == 0) & (row < HALO), 0.0, zext)
        z2e, z1e = _conv_taps(zext)
        cw = cw_ref[...]
        conv_e = (cw[0:1] * z2e + cw[1:2] * z1e + cw[2:3] * zext)[HALO:]
        gate_b_e = hbe[HALO:, :bw]
        sc_e = gate_b_e * conv_e
        rb = _rstd(sc_e)
        dyb = dycat[:, aw:]
        gdy = dyb * gb_ref[...]
        dsc_e = rb * gdy - sc_e * (rb * rb * rb) * (jnp.sum(gdy * sc_e, axis=-1, keepdims=True) * (1.0 / bw))
        dgb_ref[...] += jnp.sum((dyb * (sc_e * rb))[:tm], axis=0, keepdims=True)
        dconv_e = dsc_e * gate_b_e
        dconv_e = jnp.where((i == nt - 1) & (row[:te] >= tm), 0.0, dconv_e)
        dconv = dconv_e[:tm]
        dc1 = pltpu.roll(dconv_e, te - 1, 0)[:tm]
        dc2 = pltpu.roll(dconv_e, te - 2, 0)[:tm]
        dz = cw[2:3] * dconv + cw[1:2] * dc1 + cw[0:1] * dc2
        z = zext[HALO:HALO + tm]
        z1 = z1e[HALO:HALO + tm]
        z2 = z2e[HALO:HALO + tm]
        dcw_ref[0:1, :] += jnp.sum(dconv * z2, axis=0, keepdims=True)
        dcw_ref[1:2, :] += jnp.sum(dconv * z1, axis=0, keepdims=True)
        dcw_ref[2:3, :] += jnp.sum(dconv * z, axis=0, keepdims=True)
        dh_ref[:, in_a:in_a + bw] = (dsc_e[:tm] * conv_e[:tm]).astype(BF16)
        dh_ref[:, in_a + bw:in_a + 2 * bw] = (dz * hv[:, in_a + 2 * bw:]).astype(BF16)
        dh_ref[:, in_a + 2 * bw:] = (dz * hv[:, in_a + bw:in_a + 2 * bw]).astype(BF16)

        ha = hv[:, :in_a]
        th, u, xhat, rl, vln = _sgu_forward(ha, lng_ref[...], lnb_ref[...], wm, bt_ref[...], mixed_s)
        mixed = mixed_s[...]
        sg = u * mixed
        dsg, dga = _rms_bwd(dycat[:tm, :aw], sg, _rstd(sg), ga_ref[...])
        dga_ref[...] += dga
        du = dsg * mixed
        dmixed = dsg * u
        dmb = dmixed.astype(BF16)
        for n in range(tm // CHUNK):
            rows = slice(n * CHUNK, (n + 1) * CHUNK)
            dbs_ref[...] += dmixed[rows]
            for hh in range(HEADS):
                cols = slice(hh * hd, (hh + 1) * hd)
                dvln_s[rows, cols] = _dot_tn(wm[hh], dmb[rows, cols])
                dwsp_ref[hh] += mask * _dot_nt(dmb[rows, cols], vln[rows, cols])
        dvln = dvln_s[...]
        dlng_ref[...] += jnp.sum(dvln * xhat, axis=0, keepdims=True)
        dlnb_ref[...] += jnp.sum(dvln, axis=0, keepdims=True)
        dxh = dvln * lng_ref[...]
        dv = rl * (dxh - jnp.mean(dxh, axis=-1, keepdims=True) - xhat * jnp.mean(dxh * xhat, axis=-1, keepdims=True))
        dh_ref[:, :in_a] = (jnp.concatenate([du, dv], axis=-1) * _gelu_grad(ha, th)).astype(BF16)

        dgm_tile = jnp.zeros_like(dgm_ref)
        for r0 in range(0, tm, min(ROW_CHUNK, tm)):
            rows = slice(r0, r0 + min(ROW_CHUNK, tm))
            dxn = _dot(dh_ref[rows, :], win_ref[...])
            xv = x_ref[rows, :]
            dx, dgm = _rms_bwd(dxn, xv, _rstd(xv), gm_ref[...])
            dgm_tile = dgm_tile + dgm
            dx_ref[rows, :] = dx1_ref[rows, :] + dx
        dgm_ref[...] += dgm_tile

    full = lambda shape: pl.BlockSpec(shape, lambda i: (0,) * len(shape))
    tok = pl.BlockSpec((tm, d), lambda i: (i, 0))
    nxt = lambda i: (jnp.minimum((i + 1) * hb_blocks, last_blk), 0)
    prv = lambda i: (jnp.maximum(i * hb_blocks - 1, 0), 0)
    return _hosted(
        body, carry, grid=(nt,),
        in_specs=[tok, pl.BlockSpec((HALO, d), nxt), tok, full((1, d)),
                  pl.BlockSpec((tm, n_in), lambda i: (i, 0)), pl.BlockSpec((HALO, n_in), prv), pl.BlockSpec((HALO, n_in), nxt),
                  full((1, aw)), full((1, aw)), full((HEADS, CHUNK, CHUNK)), full((CHUNK, HEADS)), full((3, bw)),
                  full((1, aw)), full((1, bw)), full((d, d)), full((n_in, d))],
        out_specs=[pl.BlockSpec((tm, n_in), lambda i: (i, 0)), tok,
                   full((1, aw)), full((1, bw)), full((SUB, bw)), full((1, aw)), full((1, aw)),
                   full((HEADS, CHUNK, CHUNK)), full((CHUNK, aw)), full((1, d))],
        out_shape=(SDS((s, n_in), BF16), SDS((s, d), F32),
                   SDS((1, aw), F32), SDS((1, bw), F32), SDS((SUB, bw), F32), SDS((1, aw), F32), SDS((1, aw), F32),
                   SDS((HEADS, CHUNK, CHUNK), F32), SDS((CHUNK, aw), F32), SDS((1, d), F32)),
        scratch_shapes=[pltpu.VMEM((tm, aw), F32), pltpu.VMEM((tm, aw), F32)],
        compiler_params=_arb(1), name="mix_backward")(dx1, dx1, x, g_mix, h, h, h, lng, lnb, w_sp, bt, conv_w, ga, gb, w_out, w_in)


def _bias_grad(dbs):
    aw = dbs.shape[1]
    hd = aw // HEADS

    def body(dbs_ref, out_ref):
        ones = jnp.ones((SUB, hd), F32)
        for hh in range(HEADS):
            r = lax.dot_general(ones, dbs_ref[:, hh * hd:(hh + 1) * hd], (((1,), (1,)), ((), ())),
                                precision=lax.Precision.HIGHEST, preferred_element_type=F32)
            out_ref[hh:hh + 1, :] = r[0:1]

    return _pcall(body, out_shape=SDS((HEADS, CHUNK), F32), name="bias_grad")(dbs)


def _wgrad_body(a_ref, b_ref, o_ref):
    o_ref[...] = _dot_tn(a_ref[...], b_ref[...])


def _wgrad(a, b, name, carry=()):
    k, m = a.shape
    n = b.shape[1]
    tm = _tile(m, 512, LANES)
    tn = _tile(n, 1024, LANES)
    return _hosted(
        functools.partial(_wgrad_body), carry, grid=(m // tm, n // tn),
        in_specs=[pl.BlockSpec((k, tm), lambda i, j: (0, i)), pl.BlockSpec((k, tn), lambda i, j: (0, j))],
        out_specs=pl.BlockSpec((tm, tn), lambda i, j: (i, j)),
        out_shape=SDS((m, n), F32), compiler_params=_arb(2), name=name)(a, b)


def _wgrad_blocked_lhs(a, b, name, carry=()):
    nb, k, t = a.shape
    n = b.shape[1]
    tn = _tile(n, 1024, LANES)
    return _hosted(
        functools.partial(_wgrad_body), carry, grid=(nb, n // tn),
        in_specs=[pl.BlockSpec((None, k, t), lambda i, j: (i, 0, 0)), pl.BlockSpec((k, tn), lambda i, j: (0, j))],
        out_specs=pl.BlockSpec((t, tn), lambda i, j: (i, j)),
        out_shape=SDS((nb * t, n), F32), compiler_params=_arb(2), name=name)(a, b)


def _wgrad_blocked_rhs(a, b, name, carry=()):
    k, m = a.shape
    nb, _, t = b.shape
    tm = _tile(m, 512, LANES)
    return _hosted(
        functools.partial(_wgrad_body), carry, grid=(m // tm, nb),
        in_specs=[pl.BlockSpec((k, tm), lambda i, j: (0, i)), pl.BlockSpec((None, k, t), lambda i, j: (j, 0, 0))],
        out_specs=pl.BlockSpec((None, tm, t), lambda i, j: (j, i, 0)),
        out_shape=SDS((nb, m, t), F32), compiler_params=_arb(2), name=name)(a, b)


def _unblock_cols(wb, name, carry=()):
    nb, r, t = wb.shape
    tr = _tile(r, 256, 16)

    def body(w_ref, o_ref):
        o_ref[...] = jnp.concatenate([w_ref[j].astype(F32) for j in range(nb)], axis=-1).astype(o_ref.dtype)

    return _hosted(
        body, carry, grid=(r // tr,),
        in_specs=[pl.BlockSpec((nb, tr, t), lambda i: (0, i, 0))], out_specs=pl.BlockSpec((tr, nb * t), lambda i: (i, 0)),
        out_shape=SDS((r, nb * t), wb.dtype), compiler_params=_arb(1), name=name)(wb)


def _block_cols(w, nb, name, carry=()):
    r, n = w.shape
    t = n // nb
    tr = _tile(r, 256, 16)

    def body(w_ref, o_ref):
        wv = w_ref[...]
        for j in range(nb):
            o_ref[j] = wv[:, j * t:(j + 1) * t]

    return _hosted(
        body, carry, grid=(r // tr,),
        in_specs=[pl.BlockSpec((tr, n), lambda i: (i, 0))], out_specs=pl.BlockSpec((nb, tr, t), lambda i: (0, i, 0)),
        out_shape=SDS((nb, r, t), w.dtype), compiler_params=_arb(1), name=name)(w)


def _place():
    x, y, c = lax.axis_index("x"), lax.axis_index("y"), lax.axis_index("c")
    return x, y, c, [(1 - x, y), (x, 1 - y), (1 - x, 1 - y)]


def _all_gather(shards):
    n = len(shards)
    slots = 9
    cut = [(s.shape[0] // 32) * 16 for s in shards]

    def build(ins, outs, sems):
        send_sems, recv_sems, local_sems = sems
        x, y, c, _ = _place()
        me, sib, xn, yn, dg = (x, y, c), (x, y, 1 - c), (1 - x, y, c), (x, 1 - y, c), (1 - x, 1 - y, c)
        other = lambda p: (p[0], p[1], 1 - p[2])

        def rows(a, p, part=None):
            ref = outs[a].at[4 * p[0] + 2 * p[1] + p[2]]
            if part is None or cut[a] == 0:
                return ref if part in (None, 0) else None
            return ref.at[pl.ds(0, cut[a])] if part == 0 else ref.at[pl.ds(cut[a], shards[a].shape[0] - cut[a])]

        def copy(a, k, ref, to, src=None):
            if ref is None:
                return None
            return pltpu.make_async_remote_copy(
                src_ref=ref if src is None else src, dst_ref=ref, send_sem=send_sems.at[slots * a + k],
                recv_sem=recv_sems.at[slots * a + k], device_id=to, device_id_type=MESH)

        def real(cps):
            return [cp for cp in cps if cp is not None]

        class Copies:
            own = lambda a: [copy(a, 1, rows(a, me), xn, ins[a]), copy(a, 2, rows(a, me), yn, ins[a]),
                             copy(a, 0, rows(a, me), sib, ins[a])]
            local = lambda a: pltpu.make_async_copy(ins[a], rows(a, me), local_sems.at[a])
            from_x = lambda a: copy(a, 1, rows(a, xn), me)
            from_y = lambda a: copy(a, 2, rows(a, yn), me)
            after_x = lambda a: real([copy(a, 4, rows(a, xn, 1), yn), copy(a, 5, rows(a, xn), sib)])
            after_y = lambda a: real([copy(a, 3, rows(a, yn, 0), xn), copy(a, 6, rows(a, yn), sib)])
            diag_in = lambda a: real([copy(a, 3, rows(a, dg, 0), me), copy(a, 4, rows(a, dg, 1), me)])
            diag_on = lambda a: real([copy(a, 7, rows(a, dg, 0), sib), copy(a, 8, rows(a, dg, 1), sib)])
            from_sib = lambda a: real([copy(a, 0, rows(a, sib), me), copy(a, 5, rows(a, other(xn)), me),
                                       copy(a, 6, rows(a, other(yn)), me), copy(a, 7, rows(a, other(dg), 0), me),
                                       copy(a, 8, rows(a, other(dg), 1), me)])

        return Copies

    def start(ins, outs, sems):
        cps = build(ins, outs, sems)
        for a in range(n):
            for cp in cps.own(a):
                cp.start()
        for a in range(n):
            cps.local(a).start()

    def relay(ins, outs, sems):
        cps = build(ins, outs, sems)
        for a in range(n):
            cps.from_x(a).wait_recv()
            for cp in cps.after_x(a):
                cp.start()
            cps.from_y(a).wait_recv()
            for cp in cps.after_y(a):
                cp.start()

    def finish(ins, outs, sems):
        cps = build(ins, outs, sems)
        for a in range(n):
            for arrived, onward in zip(cps.diag_in(a), cps.diag_on(a)):
                arrived.wait_recv()
                onward.start()
        for a in range(n):
            for cp in cps.from_sib(a):
                cp.wait_recv()
            for cp in cps.own(a) + cps.after_x(a) + cps.after_y(a) + cps.diag_on(a):
                cp.wait_send()
            cps.local(a).wait()

    return _Exchange(shards, [SDS((N_DEV,) + s.shape, s.dtype) for s in shards],
                     [pltpu.SemaphoreType.DMA((slots * n,)), pltpu.SemaphoreType.DMA((slots * n,)),
                      pltpu.SemaphoreType.DMA((n,))], start, finish, relay)


def _swap_exchange(ins, out_shape, per, copies):
    def start(i, o, sems):
        for cp in copies(i, o, sems):
            cp.start()

    def finish(i, o, sems):
        for cp in copies(i, o, sems):
            cp.wait()

    n = per * len(ins)
    return _Exchange(ins, out_shape, [pltpu.SemaphoreType.DMA((n,)), pltpu.SemaphoreType.DMA((n,))], start, finish)


def _exchange_c(gs):
    def copies(ins, outs, sems):
        x, y, c, _ = _place()
        return [pltpu.make_async_remote_copy(
                    src_ref=ins[a].at[2 * k + 1 - c], dst_ref=outs[a].at[k],
                    send_sem=sems[0].at[4 * a + k], recv_sem=sems[1].at[4 * a + k],
                    device_id=(x, y, 1 - c), device_id_type=MESH)
                for a in range(len(gs)) for k in range(4)]

    return _swap_exchange(gs, [SDS((4,) + g.shape[1:], g.dtype) for g in gs], 4, copies)


def _exchange_xy(sends):
    def copies(ins, outs, sems):
        x, y, c, chips = _place()
        return [pltpu.make_async_remote_copy(
                    src_ref=ins[a].at[t], dst_ref=outs[a].at[t],
                    send_sem=sems[0].at[3 * a + t], recv_sem=sems[1].at[3 * a + t],
                    device_id=(*chips[t], c), device_id_type=MESH)
                for a in range(len(sends)) for t in range(3)]

    return _swap_exchange(sends, [SDS(s.shape, s.dtype) for s in sends], 3, copies)


def _rs_combine(g, recv, pos, name, carry=()):
    _, r, cdim = g.shape
    tr = _tile(r, 256, 16)

    def body(pos_ref, g0, r0, g1, r1, g2, r2, g3, r3, keep_ref, send_ref):
        keep_ref[...] = g0[...] + r0[...]
        send_ref[0] = (g1[...] + r1[...]).astype(BF16)
        send_ref[1] = (g2[...] + r2[...]).astype(BF16)
        send_ref[2] = (g3[...] + r3[...]).astype(BF16)

    def k_of(p, t):
        px = p[0] if t in (0, 2) else 1 - p[0]
        py = p[1] if t in (0, 1) else 1 - p[1]
        return 2 * px + py

    blk = (None, tr, cdim)
    in_specs = []
    for t in range(4):
        in_specs.append(pl.BlockSpec(blk, functools.partial(lambda j, p, t: (2 * k_of(p, t) + p[2], j, 0), t=t)))
        in_specs.append(pl.BlockSpec(blk, functools.partial(lambda j, p, t: (k_of(p, t), j, 0), t=t)))
    return _hosted(
        body, carry, n_prefetch=1, out_shape=(SDS((r, cdim), F32), SDS((3, r, cdim), BF16)),
        grid=(r // tr,), in_specs=in_specs,
        out_specs=[pl.BlockSpec((tr, cdim), lambda j, p: (j, 0)), pl.BlockSpec((3, tr, cdim), lambda j, p: (0, j, 0))],
        compiler_params=_arb(1), name=name)(pos, g, recv, g, recv, g, recv, g, recv)


def _adamw_shard(keep, recv, w, m, v, name):
    r, cdim = w.shape
    tr = _tile(r, 256, 16)

    def body(k_ref, r_ref, w_ref, m_ref, v_ref, g_ref, d_ref, nm_ref, nv_ref):
        g = ((k_ref[...] + r_ref[0].astype(F32)) + r_ref[1].astype(F32)) + r_ref[2].astype(F32)
        g_ref[...] = g
        d_ref[...], nm_ref[...], nv_ref[...] = _adamw(w_ref[...], g, m_ref[...], v_ref[...])

    blk = pl.BlockSpec((tr, cdim), lambda j: (j, 0))
    out = SDS((r, cdim), F32)
    return _pcall(body, grid=(r // tr,), in_specs=[blk, pl.BlockSpec((3, tr, cdim), lambda j: (0, j, 0)), blk, blk, blk],
                  out_specs=[blk] * 4, out_shape=(out,) * 4, compiler_params=_arb(1), name=name)(keep, recv, w, m, v)


def _adamw_small(gathered, seg, params, conv_rows):
    names = list(params)
    c0, cn = conv_rows

    def body(*refs):
        gat_ref = refs[0]
        ins = refs[1:1 + 3 * len(names)]
        outs = refs[1 + 3 * len(names):]

        def total(r0, rn):
            tot = gat_ref[0, r0:r0 + rn, :]
            for dev in range(1, N_DEV):
                tot = tot + gat_ref[dev, r0:r0 + rn, :]
            return tot

        for k, nm in enumerate(names):
            g = total(*seg[nm])
            w_ref, m_ref, v_ref = ins[3 * k:3 * k + 3]
            g_ref, d_ref, nm_ref, nv_ref = outs[4 * k:4 * k + 4]
            g_ref[...] = g
            d_ref[...], nm_ref[...], nv_ref[...] = _adamw(w_ref[...], g, m_ref[...], v_ref[...])
        outs[-2][...] = total(c0, cn)
        outs[-1][...] = total(*seg["loss"])

    flat_in = [a for nm in names for a in params[nm]]
    out_shape = []
    for nm in names:
        out_shape += [SDS(params[nm][0].shape, F32)] * 4
    out_shape += [SDS((cn, LANES), F32), SDS((seg["loss"][1], LANES), F32)]
    res = _pcall(body, out_shape=tuple(out_shape), name="adamw_small")(gathered, *flat_in)
    per = {nm: res[4 * k:4 * k + 4] for k, nm in enumerate(names)}
    return per, res[-2], res[-1]


def _adamw_one(w, g, m, v, name):
    def body(w_ref, g_ref, m_ref, v_ref, d_ref, nm_ref, nv_ref):
        d_ref[...], nm_ref[...], nv_ref[...] = _adamw(w_ref[...], g_ref[...], m_ref[...], v_ref[...])

    return _pcall(body, out_shape=(SDS(w.shape, F32),) * 3, name=name)(w, g, m, v)


def _rows128(a):
    return a.reshape(-1, LANES)


def _pack_small(gs, loss_tile):
    seg, pieces, row = {}, [], 0
    for nm in SMALL + ("conv_w", "loss"):
        piece = loss_tile if nm == "loss" else _rows128(gs[nm])
        rn = _round_up(piece.shape[0], SUB)
        pieces.append(jnp.pad(piece, ((0, rn - piece.shape[0]), (0, 0))))
        seg[nm] = (row, piece.shape[0])
        row += rn
    return jnp.concatenate(pieces, axis=0), seg


def _step(x, mem, target, wb, conv_w, sp, pos):
    s, d = x.shape
    tm = min(TOKEN_TILE, s)
    tm_wide = min(2 * TOKEN_TILE, s)
    rows = lambda w8: w8.reshape(-1, w8.shape[2])
    shards = lambda g: g.reshape((N_DEV, -1) + g.shape[1:])
    bt = sp["b_spatial"].T

    (w_in8, conv8), = _run_exchanges([_all_gather([wb["w_in"], conv_w])], "gather_w_in")
    conv_full = conv8.transpose(1, 0, 2).reshape(3, -1)
    w_in_t = rows(w_in8)
    (xn1, h), ((w_out8, w_kv8, w_q8),) = _in_forward(
        x, sp["ln_mix_g"], w_in_t, tm, carry=[_all_gather([wb["w_out"], wb["w_kv"], wb["w_q"]])])
    w_out = rows(w_out8)
    (ycat, x1), ((w_o8, w_down8),) = _mix_forward(
        h, x, sp["sgu_ln_g"], sp["sgu_ln_b"], sp["w_spatial"], bt, conv_full, sp["grp_norm_a"], sp["grp_norm_b"], w_out, tm,
        carry=[_all_gather([wb["w_o"], wb["w_down"]])])
    w_q, w_o, w_down = rows(w_q8), rows(w_o8), rows(w_down8)
    memn, kv = _kv_forward(mem, sp["ln_mem_g"], w_kv8)
    (xn2, q, o, x2), ((w_gu8,),) = _attn_forward(
        x1, sp["ln_attn_g"], w_q, kv, w_o, tm, carry=[_all_gather([wb["w_gate_up"]])])
    w_gu = w_gu8.reshape((2, N_DEV // 2) + w_gu8.shape[1:])
    xn3, gu, x3 = _ffn_forward(x2, sp["ln_ffn_g"], w_gu, w_down, tm_wide)

    loss, d_lnf, dx3, dx3b = _final_backward(x3, target, sp["ln_final_g"], tm_wide)
    act, dgu, dxn3 = _swiglu_backward(dx3b, gu, w_gu, w_down, tm_wide)
    part = {}
    g_gu, _ = _wgrad_blocked_lhs(dgu.reshape((N_DEV,) + dgu.shape[2:]), xn3, "wgrad_gate_up")
    g_gu = shards(g_gu)
    g_down, ((rc_gu,),) = _wgrad_blocked_lhs(act, dx3b, "wgrad_down", carry=[_exchange_c([g_gu])])
    g_down = shards(g_down)
    (keep_gu, send_gu), _ = _rs_combine(g_gu, rc_gu, pos, "rs_combine_w_gate_up")
    (dx2b, dq, dx1, dx1b, dkv, d_lnattn, d_lnffn), ((rxy_gu,), (rc_down,)) = _attn_backward(
        dx3, dxn3, x2, sp["ln_ffn_g"], x1, sp["ln_attn_g"], q, kv, w_q, w_o, tm,
        carry=[_exchange_xy([send_gu]), _exchange_c([g_down])])
    part["w_gate_up"] = (keep_gu, rxy_gu)
    (keep_down, send_down), _ = _rs_combine(g_down, rc_down, pos, "rs_combine_w_down")
    g_o, _ = _wgrad(o, dx2b, "wgrad_o")
    g_o = shards(g_o)
    g_q, ((rc_o,),) = _wgrad(xn2, dq, "wgrad_q", carry=[_exchange_c([g_o])])
    g_q = shards(g_q)
    g_out, ((rc_q,),) = _wgrad(ycat, dx1b, "wgrad_out", carry=[_exchange_c([g_q])])
    g_out = shards(g_out)
    g_kv, d_lnmem = _kv_backward(dkv, memn, mem, sp["ln_mem_g"], w_kv8)
    (keep_o, send_o), ((rc_out,),) = _rs_combine(g_o, rc_o, pos, "rs_combine_w_o", carry=[_exchange_c([g_out])])
    (keep_q, send_q), _ = _rs_combine(g_q, rc_q, pos, "rs_combine_w_q")
    (keep_out, send_out), _ = _rs_combine(g_out, rc_out, pos, "rs_combine_w_out")
    ((dh, dx, d_ga, d_gb, d_cw, d_lng, d_lnb, d_wsp, d_bs, d_lnmix),
     ((rxy_down, rxy_o, rxy_q, rxy_out), (rc_kv,))) = _mix_backward(
        dx1, x, sp["ln_mix_g"], h, sp["sgu_ln_g"], sp["sgu_ln_b"], sp["w_spatial"], bt, conv_full,
        sp["grp_norm_a"], sp["grp_norm_b"], w_out, w_in_t, tm,
        carry=[_exchange_xy([send_down, send_o, send_q, send_out]), _exchange_c([g_kv])])
    part["w_down"], part["w_o"], part["w_q"] = (keep_down, rxy_down), (keep_o, rxy_o), (keep_q, rxy_q)
    part["w_out"] = (keep_out, rxy_out)
    (keep_kv, send_kv), _ = _rs_combine(g_kv, rc_kv, pos, "rs_combine_w_kv")
    gs = {"ln_mix_g": d_lnmix, "sgu_ln_g": d_lng, "sgu_ln_b": d_lnb, "w_spatial": d_wsp, "b_spatial": _bias_grad(d_bs),
          "conv_w": d_cw[:3], "grp_norm_a": d_ga, "grp_norm_b": d_gb, "ln_attn_g": d_lnattn, "ln_mem_g": d_lnmem,
          "ln_ffn_g": d_lnffn, "ln_final_g": d_lnf}
    packed, seg = _pack_small(gs, loss)
    g_in, ((rxy_kv,), (small_all,)) = _wgrad(
        dh, xn1, "wgrad_in", carry=[_exchange_xy([send_kv]), _all_gather([packed])])
    g_in = shards(g_in)
    part["w_kv"] = (keep_kv, rxy_kv)
    (rc_in,), = _run_exchanges([_exchange_c([g_in])], "exchange_c_w_in")
    (keep_in, send_in), _ = _rs_combine(g_in, rc_in, pos, "rs_combine_w_in")
    (rxy_in,), = _run_exchanges([_exchange_xy([send_in])], "exchange_xy_w_in")
    part["w_in"] = (keep_in, rxy_in)
    return dx, part, small_all, seg


def kernel(x, mem, ln_mix_g, w_in, sgu_ln_g, sgu_ln_b, w_spatial, b_spatial, conv_w, grp_norm_a, grp_norm_b, w_out, ln_attn_g, ln_mem_g, w_q, w_kv, w_o, ln_ffn_g, w_gate_up, w_down, ln_final_g, loss_target, m_ln_mix_g, m_w_in, m_sgu_ln_g, m_sgu_ln_b, m_w_spatial, m_b_spatial, m_conv_w, m_grp_norm_a, m_grp_norm_b, m_w_out, m_ln_attn_g, m_ln_mem_g, m_w_q, m_w_kv, m_w_o, m_ln_ffn_g, m_w_gate_up, m_w_down, m_ln_final_g, v_ln_mix_g, v_w_in, v_sgu_ln_g, v_sgu_ln_b, v_w_spatial, v_b_spatial, v_conv_w, v_grp_norm_a, v_grp_norm_b, v_w_out, v_ln_attn_g, v_ln_mem_g, v_w_q, v_w_kv, v_w_o, v_ln_ffn_g, v_w_gate_up, v_w_down, v_ln_final_g):
    order = ["ln_mix_g", "w_in", "sgu_ln_g", "sgu_ln_b", "w_spatial", "b_spatial", "conv_w", "grp_norm_a", "grp_norm_b",
             "w_out", "ln_attn_g", "ln_mem_g", "w_q", "w_kv", "w_o", "ln_ffn_g", "w_gate_up", "w_down", "ln_final_g"]
    W = dict(ln_mix_g=ln_mix_g, w_in=w_in, sgu_ln_g=sgu_ln_g, sgu_ln_b=sgu_ln_b, w_spatial=w_spatial, b_spatial=b_spatial,
             conv_w=conv_w, grp_norm_a=grp_norm_a, grp_norm_b=grp_norm_b, w_out=w_out, ln_attn_g=ln_attn_g,
             ln_mem_g=ln_mem_g, w_q=w_q, w_kv=w_kv, w_o=w_o, ln_ffn_g=ln_ffn_g, w_gate_up=w_gate_up, w_down=w_down,
             ln_final_g=ln_final_g)
    M = dict(ln_mix_g=m_ln_mix_g, w_in=m_w_in, sgu_ln_g=m_sgu_ln_g, sgu_ln_b=m_sgu_ln_b, w_spatial=m_w_spatial,
             b_spatial=m_b_spatial, conv_w=m_conv_w, grp_norm_a=m_grp_norm_a, grp_norm_b=m_grp_norm_b, w_out=m_w_out,
             ln_attn_g=m_ln_attn_g, ln_mem_g=m_ln_mem_g, w_q=m_w_q, w_kv=m_w_kv, w_o=m_w_o, ln_ffn_g=m_ln_ffn_g,
             w_gate_up=m_w_gate_up, w_down=m_w_down, ln_final_g=m_ln_final_g)
    V = dict(ln_mix_g=v_ln_mix_g, w_in=v_w_in, sgu_ln_g=v_sgu_ln_g, sgu_ln_b=v_sgu_ln_b, w_spatial=v_w_spatial,
             b_spatial=v_b_spatial, conv_w=v_conv_w, grp_norm_a=v_grp_norm_a, grp_norm_b=v_grp_norm_b, w_out=v_w_out,
             ln_attn_g=v_ln_attn_g, ln_mem_g=v_ln_mem_g, w_q=v_w_q, w_kv=v_w_kv, w_o=v_w_o, ln_ffn_g=v_ln_ffn_g,
             w_gate_up=v_w_gate_up, w_down=v_w_down, ln_final_g=v_ln_final_g)

    bw = conv_w.shape[1] * N_DEV
    pos = jnp.stack([lax.axis_index("x"), lax.axis_index("y"), lax.axis_index("c")]).astype(jnp.int32)
    me = 4 * pos[0] + 2 * pos[1] + pos[2]

    sp = {nm: (W[nm].reshape(1, -1) if W[nm].ndim == 1 else W[nm]) for nm in SMALL}
    view = lambda a, nm: a.T if nm in TRANSPOSED else a
    wb = {nm: view(W[nm], nm).astype(BF16) for nm in BIG}
    grad_x, part, small_all, seg = _step(x[0], mem[0], loss_target[0], wb, conv_w, sp, pos)

    out = {}
    for nm in BIG:
        res = _adamw_shard(part[nm][0], part[nm][1], view(W[nm], nm), view(M[nm], nm), view(V[nm], nm), "adamw_" + nm)
        out[nm] = tuple(view(a, nm) for a in res)

    params = {nm: (_rows128(W[nm]), _rows128(M[nm]), _rows128(V[nm])) for nm in SMALL}
    per, conv_g_rows, loss_sum = _adamw_small(small_all, seg, params, seg["conv_w"])
    for nm in SMALL:
        out[nm] = tuple(a.reshape(W[nm].shape) for a in per[nm])
    conv_g = lax.dynamic_slice_in_dim(conv_g_rows.reshape(3, bw), me * conv_w.shape[1], conv_w.shape[1], axis=1)
    out["conv_w"] = (conv_g,) + tuple(_adamw_one(conv_w, conv_g, m_conv_w, v_conv_w, "adamw_conv"))

    loss = loss_sum[0, 0]
    res = [loss, grad_x[None]]
    for k in range(4):
        res += [out[nm][k] for nm in order]
    return tuple(res)
```

```python
import functools

import jax
import jax.numpy as jnp
from jax import lax
from jax.experimental import pallas as pl
from jax.experimental.pallas import tpu as pltpu

F32 = jnp.float32
BF16 = jnp.bfloat16
SDS = jax.ShapeDtypeStruct
MESH = pl.DeviceIdType.MESH

EPS = 1e-6
N_DEV = 8
HEADS = 4
CHUNK = 128
HALO = 16
SUB = 8
LANES = 128
TOKEN_TILE = 512
ROW_CHUNK = 256
RELAY_AT = 0.7

ADAM_LR = 0.001
ADAM_B1 = 0.9
ADAM_B2 = 0.999
ADAM_EPS = 1e-08
ADAM_WD = 0.01
ADAM_STEP = 10

BIG = ("w_in", "w_out", "w_q", "w_kv", "w_o", "w_gate_up", "w_down")
TRANSPOSED = ("w_in", "w_gate_up")
SMALL = ("ln_mix_g", "sgu_ln_g", "sgu_ln_b", "w_spatial", "b_spatial", "grp_norm_a", "grp_norm_b",
         "ln_attn_g", "ln_mem_g", "ln_ffn_g", "ln_final_g")


class _Exchange:
    def __init__(self, ins, out_shape, sems, start, finish, relay=None):
        self.ins, self.out_shape, self.sems = list(ins), list(out_shape), list(sems)
        self.start, self.finish, self.relay = start, finish, relay


def _pcall(body, carry=(), n_prefetch=0, **kw):
    if carry:
        return functools.partial(_carrying_call, body, tuple(carry), n_prefetch, kw)
    if n_prefetch:
        kw["grid_spec"] = pltpu.PrefetchScalarGridSpec(
            num_scalar_prefetch=n_prefetch, grid=kw.pop("grid"), in_specs=kw.pop("in_specs"),
            out_specs=kw.pop("out_specs"), scratch_shapes=kw.pop("scratch_shapes", ()))
    return pl.pallas_call(body, **kw)


def _carrying_call(body, carry, n_prefetch, kw, *args):
    kw = dict(kw)
    out_shape = kw.pop("out_shape")
    single = not isinstance(out_shape, (tuple, list))
    outs_shape = (out_shape,) if single else tuple(out_shape)
    out_specs = kw.pop("out_specs")
    out_specs = [out_specs] if single else list(out_specs)
    in_specs = list(kw.pop("in_specs"))
    scratch = list(kw.pop("scratch_shapes", ()))
    grid = tuple(kw.get("grid", ()))
    n_in, n_out, n_scr = len(args), len(outs_shape), len(scratch)

    def split(refs, k, counts):
        parts = []
        for cnt in counts:
            parts.append(refs[k:k + cnt])
            k += cnt
        return parts, k

    def wrapped(*refs):
        cins, k = split(refs, n_in, [len(p.ins) for p in carry])
        outs = refs[k:k + n_out]
        couts, k = split(refs, k + n_out, [len(p.out_shape) for p in carry])
        scr = refs[k:k + n_scr]
        csems, _ = split(refs, k + n_scr, [len(p.sems) for p in carry])
        first, last = True, True
        for a, g in enumerate(grid):
            first = (pl.program_id(a) == 0) & first
            last = (pl.program_id(a) == g - 1) & last

        def start_all():
            for p, ci, co, cs in zip(carry, cins, couts, csems):
                p.start(ci, co, cs)

        def relay_all():
            for p, ci, co, cs in zip(carry, cins, couts, csems):
                if p.relay is not None:
                    p.relay(ci, co, cs)

        def finish_all():
            for p, ci, co, cs in zip(carry, cins, couts, csems):
                p.finish(ci, co, cs)

        if len(grid) == 1:
            relay_now = pl.program_id(0) == min(int(RELAY_AT * grid[0]), grid[0] - 1)
        else:
            relay_now = last
        start_all() if not grid else pl.when(first)(start_all)
        relay_all() if not grid else pl.when(relay_now)(relay_all)
        body(*refs[:n_in], *outs, *scr)
        finish_all() if not grid else pl.when(last)(finish_all)

    c_in = [a for p in carry for a in p.ins]
    c_out = [s for p in carry for s in p.out_shape]
    c_sems = [s for p in carry for s in p.sems]
    res = _pcall(wrapped, n_prefetch=n_prefetch, out_shape=outs_shape + tuple(c_out),
                 in_specs=in_specs + _hbm_specs(len(c_in)), out_specs=out_specs + _hbm_specs(len(c_out)),
                 scratch_shapes=scratch + c_sems, **kw)(*args, *c_in)
    own = res[0] if single else tuple(res[:n_out])
    landed, k = [], n_out
    for p in carry:
        landed.append(list(res[k:k + len(p.out_shape)]))
        k += len(p.out_shape)
    return own, landed


def _hbm_specs(n):
    return [pl.BlockSpec(memory_space=pl.ANY)] * n


def _hosted(body, carry, **kw):
    if carry:
        return _pcall(body, carry=carry, **kw)
    call = _pcall(body, **kw)
    return lambda *args: (call(*args), [])


def _run_exchanges(parts, name):
    def body(*refs):
        pass

    _, landed = _pcall(body, carry=parts, out_shape=(), in_specs=[], out_specs=[], name=name)()
    return landed


def _arb(n):
    return pltpu.CompilerParams(dimension_semantics=("arbitrary",) * n)


def _tile(n, target, mult):
    best = None
    for t in range(mult, min(n, target) + 1, mult):
        if n % t == 0:
            best = t
    return n if best is None else best


def _round_up(n, m):
    return (n + m - 1) // m * m


def _dot(a, b):
    return jnp.dot(a, b, preferred_element_type=F32)


def _dot_nt(a, b):
    return lax.dot_general(a, b, (((1,), (1,)), ((), ())), preferred_element_type=F32)


def _dot_tn(a, b):
    return lax.dot_general(a, b, (((0,), (0,)), ((), ())), preferred_element_type=F32)


def _rstd(x):
    return lax.rsqrt(jnp.mean(x * x, axis=-1, keepdims=True) + EPS)


def _rms_bwd(dy, x, r, g):
    gdy = dy * g
    proj = jnp.sum(gdy * x, axis=-1, keepdims=True) * (1.0 / x.shape[-1])
    dx = r * gdy - x * (r * r * r) * proj
    dg = jnp.sum(dy * (x * r), axis=0, keepdims=True)
    return dx, dg


_GELU_C = 0.7978845608028654
_GELU_A = 0.044715


def _gelu(x):
    t = jnp.tanh(_GELU_C * (x + _GELU_A * x * x * x))
    return 0.5 * x * (1.0 + t), t


def _gelu_grad(x, t):
    return 0.5 * (1.0 + t) + 0.5 * x * (1.0 - t * t) * (_GELU_C * (1.0 + 3.0 * _GELU_A * x * x))


def _sigmoid(x):
    return 1.0 / (1.0 + jnp.exp(-x))


def _softmax(s):
    m = jnp.max(s, axis=-1, keepdims=True)
    e = jnp.exp(s - m)
    return e / jnp.sum(e, axis=-1, keepdims=True)


def _adamw(w, g, m, v):
    m = ADAM_B1 * m + (1.0 - ADAM_B1) * g
    v = ADAM_B2 * v + (1.0 - ADAM_B2) * (g * g)
    m_hat = m / (1.0 - ADAM_B1 ** ADAM_STEP)
    v_hat = v / (1.0 - ADAM_B2 ** ADAM_STEP)
    delta = -ADAM_LR * (m_hat / (jnp.sqrt(v_hat) + ADAM_EPS) + ADAM_WD * w)
    return delta, m, v


def _tril_mask():
    t = lax.broadcasted_iota(jnp.int32, (CHUNK, CHUNK), 0)
    s = lax.broadcasted_iota(jnp.int32, (CHUNK, CHUNK), 1)
    return (s <= t).astype(F32)


def _sgu_forward(ha, lng, lnb, wm, bt, mixed_s):
    aw = ha.shape[1] // 2
    hd = aw // HEADS
    a, th = _gelu(ha)
    u = a[:, :aw]
    v = a[:, aw:]
    mu = jnp.mean(v, axis=-1, keepdims=True)
    vc = v - mu
    rl = lax.rsqrt(jnp.mean(vc * vc, axis=-1, keepdims=True) + EPS)
    xhat = vc * rl
    vln = (xhat * lng + lnb).astype(BF16)
    for n in range(ha.shape[0] // CHUNK):
        rows = slice(n * CHUNK, (n + 1) * CHUNK)
        for h in range(HEADS):
            cols = slice(h * hd, (h + 1) * hd)
            mixed_s[rows, cols] = _dot(wm[h], vln[rows, cols]) + bt[:, h:h + 1]
    return th, u, xhat, rl, vln


def _conv_taps(zext):
    return pltpu.roll(zext, 2, 0), pltpu.roll(zext, 1, 0)


def _kv_forward(mem, g_mem, w_kv):
    ml, d = mem.shape
    xd = w_kv.shape[2]

    def body(mem_ref, g_ref, w_ref, memn_ref, kv_ref):
        x = mem_ref[...]
        memn = (x * _rstd(x) * g_ref[...]).astype(BF16)
        memn_ref[...] = memn
        for j in range(2 * HEADS):
            kv_ref[j] = _dot(memn, w_ref[j]).astype(BF16)

    return _pcall(body, out_shape=(SDS((ml, d), BF16), SDS((2 * HEADS, ml, xd), BF16)), name="kv_forward")(mem, g_mem, w_kv)


def _in_forward(x, g, w_in_t, tm, carry=()):
    s, d = x.shape
    n_in = w_in_t.shape[0]

    def body(x_ref, g_ref, w_ref, xn_ref, h_ref):
        xv = x_ref[...]
        xn = (xv * _rstd(xv) * g_ref[...]).astype(BF16)
        xn_ref[...] = xn
        h_ref[...] = _dot_nt(xn, w_ref[...])

    return _hosted(
        body, carry, grid=(s // tm,),
        in_specs=[pl.BlockSpec((tm, d), lambda i: (i, 0)), pl.BlockSpec((1, d), lambda i: (0, 0)),
                  pl.BlockSpec((n_in, d), lambda i: (0, 0))],
        out_specs=[pl.BlockSpec((tm, d), lambda i: (i, 0)), pl.BlockSpec((tm, n_in), lambda i: (i, 0))],
        out_shape=(SDS((s, d), BF16), SDS((s, n_in), F32)),
        compiler_params=_arb(1), name="in_forward")(x, g, w_in_t)


def _mix_forward(h, x, lng, lnb, w_sp, bt, conv_w, ga, gb, w_out, tm, carry=()):
    s, d = x.shape
    n_in = h.shape[1]
    aw = lng.shape[1]
    bw = d - aw
    in_a = 2 * aw
    hb_blocks = tm // HALO

    def body(h_ref, hprev_ref, x_ref, lng_ref, lnb_ref, wsp_ref, bt_ref, cw_ref, ga_ref, gb_ref, wout_ref,
             ycat_ref, x1_ref, mixed_s):
        i = pl.program_id(0)
        mask = _tril_mask()
        wm = [(wsp_ref[hh] * mask).astype(BF16) for hh in range(HEADS)]
        hv = h_ref[...]
        _, u, _, _, _ = _sgu_forward(hv[:, :in_a], lng_ref[...], lnb_ref[...], wm, bt_ref[...], mixed_s)
        sg = u * mixed_s[...]
        ycat_ref[:, :aw] = (sg * _rstd(sg) * ga_ref[...]).astype(BF16)

        gate_b = hv[:, in_a:in_a + bw]
        z = hv[:, in_a + bw:in_a + 2 * bw] * hv[:, in_a + 2 * bw:]
        hp = hprev_ref[...]
        zp = hp[:, in_a + bw:in_a + 2 * bw] * hp[:, in_a + 2 * bw:]
        zp = jnp.where(i == 0, 0.0, zp)
        zext = jnp.concatenate([zp, z], axis=0)
        z2, z1 = _conv_taps(zext)
        cw = cw_ref[...]
        conv = cw[0:1] * z2[HALO:] + cw[1:2] * z1[HALO:] + cw[2:3] * z
        sc = gate_b * conv
        ycat_ref[:, aw:] = (sc * _rstd(sc) * gb_ref[...]).astype(BF16)
        x1_ref[...] = x_ref[...] + _dot(ycat_ref[...], wout_ref[...])

    full = lambda shape: pl.BlockSpec(shape, lambda i: (0,) * len(shape))
    return _hosted(
        body, carry, grid=(s // tm,),
        in_specs=[pl.BlockSpec((tm, n_in), lambda i: (i, 0)),
                  pl.BlockSpec((HALO, n_in), lambda i: (jnp.maximum(i * hb_blocks - 1, 0), 0)),
                  pl.BlockSpec((tm, d), lambda i: (i, 0)),
                  full((1, aw)), full((1, aw)), full((HEADS, CHUNK, CHUNK)), full((CHUNK, HEADS)),
                  full((3, bw)), full((1, aw)), full((1, bw)), full((d, d))],
        out_specs=[pl.BlockSpec((tm, d), lambda i: (i, 0)), pl.BlockSpec((tm, d), lambda i: (i, 0))],
        out_shape=(SDS((s, d), BF16), SDS((s, d), F32)),
        scratch_shapes=[pltpu.VMEM((tm, aw), F32)],
        compiler_params=_arb(1), name="mix_forward")(h, h, x, lng, lnb, w_sp, bt, conv_w, ga, gb, w_out)


def _attn_forward(x1, g, w_q, kv, w_o, tm, carry=()):
    s, d = x1.shape
    _, ml, xd = kv.shape
    scale = xd ** -0.5

    def body(x1_ref, g_ref, wq_ref, kv_ref, wo_ref, xn_ref, q_ref, o_ref, x2_ref):
        xv = x1_ref[...]
        xn = (xv * _rstd(xv) * g_ref[...]).astype(BF16)
        xn_ref[...] = xn
        q_ref[...] = _dot(xn, wq_ref[...]).astype(BF16)
        for hh in range(HEADS):
            cols = slice(hh * xd, (hh + 1) * xd)
            p = _softmax(_dot_nt(q_ref[:, cols], kv_ref[hh]) * scale)
            o_ref[:, cols] = _dot(p.astype(BF16), kv_ref[HEADS + hh]).astype(BF16)
        x2_ref[...] = xv + _dot(o_ref[...], wo_ref[...])

    tok = pl.BlockSpec((tm, d), lambda i: (i, 0))
    return _hosted(
        body, carry, grid=(s // tm,),
        in_specs=[tok, pl.BlockSpec((1, d), lambda i: (0, 0)), pl.BlockSpec((d, d), lambda i: (0, 0)),
                  pl.BlockSpec((2 * HEADS, ml, xd), lambda i: (0, 0, 0)), pl.BlockSpec((d, d), lambda i: (0, 0))],
        out_specs=[tok, tok, tok, tok],
        out_shape=(SDS((s, d), BF16), SDS((s, d), BF16), SDS((s, d), BF16), SDS((s, d), F32)),
        compiler_params=_arb(1), name="attn_forward")(x1, g, w_q, kv, w_o)


def _ffn_forward(x2, g, w_gu, w_down, tm):
    s, d = x2.shape
    _, nf, tf, _ = w_gu.shape

    def body(x2_ref, g_ref, wgu_ref, wd_ref, xn_ref, gu_ref, x3_ref):
        f = pl.program_id(1)

        @pl.when(f == 0)
        def _():
            xv = x2_ref[...]
            xn_ref[...] = (xv * _rstd(xv) * g_ref[...]).astype(BF16)
            x3_ref[...] = xv

        xn = xn_ref[...]
        gate = _dot_nt(xn, wgu_ref[0])
        up = _dot_nt(xn, wgu_ref[1])
        gu_ref[0] = gate.astype(BF16)
        gu_ref[1] = up.astype(BF16)
        act = (gate * _sigmoid(gate) * up).astype(BF16)
        x3_ref[...] += _dot(act, wd_ref[...])

    tok = pl.BlockSpec((tm, d), lambda i, f: (i, 0))
    return _pcall(
        body, grid=(s // tm, nf),
        in_specs=[tok, pl.BlockSpec((1, d), lambda i, f: (0, 0)),
                  pl.BlockSpec((2, None, tf, d), lambda i, f: (0, f, 0, 0)),
                  pl.BlockSpec((tf, d), lambda i, f: (f, 0))],
        out_specs=[tok, pl.BlockSpec((2, None, tm, tf), lambda i, f: (0, f, i, 0)), tok],
        out_shape=(SDS((s, d), BF16), SDS((2, nf, s, tf), BF16), SDS((s, d), F32)),
        compiler_params=_arb(2), name="ffn_forward")(x2, g, w_gu, w_down)


def _final_backward(x3, target, g_final, tm):
    s, d = x3.shape

    def body(x3_ref, tgt_ref, gf_ref, loss_ref, dgf_ref, dx3_ref, dx3b_ref):
        @pl.when(pl.program_id(0) == 0)
        def _():
            loss_ref[...] = jnp.zeros_like(loss_ref)
            dgf_ref[...] = jnp.zeros_like(dgf_ref)

        xv = x3_ref[...]
        r = _rstd(xv)
        diff = xv * r * gf_ref[...] - tgt_ref[...]
        loss_ref[...] += 0.5 * jnp.sum(jnp.sum(diff * diff, axis=-1, keepdims=True), axis=0, keepdims=True) * (1.0 / d)
        dx3, dgf = _rms_bwd(diff * (1.0 / d), xv, r, gf_ref[...])
        dgf_ref[...] += dgf
        dx3_ref[...] = dx3
        dx3b_ref[...] = dx3.astype(BF16)

    tok = pl.BlockSpec((tm, d), lambda i: (i, 0))
    vec = pl.BlockSpec((1, d), lambda i: (0, 0))
    return _pcall(
        body, grid=(s // tm,), in_specs=[tok, tok, vec],
        out_specs=[pl.BlockSpec((SUB, LANES), lambda i: (0, 0)), vec, tok, tok],
        out_shape=(SDS((SUB, LANES), F32), SDS((1, d), F32), SDS((s, d), F32), SDS((s, d), BF16)),
        compiler_params=_arb(1), name="final_backward")(x3, target, g_final)


def _swiglu_backward(dx3b, gu, w_gu, w_down, tm):
    s, d = dx3b.shape
    _, nf, tf, _ = w_gu.shape

    def body(dx3b_ref, gu_ref, wgu_ref, wd_ref, act_ref, dgu_ref, dxn_ref):
        @pl.when(pl.program_id(1) == 0)
        def _():
            dxn_ref[...] = jnp.zeros_like(dxn_ref)

        for r0 in range(0, tm, ROW_CHUNK):
            rows = slice(r0, r0 + ROW_CHUNK)
            dact = _dot_nt(dx3b_ref[rows, :], wd_ref[...])
            gv = gu_ref[0, rows, :].astype(F32)
            uv = gu_ref[1, rows, :].astype(F32)
            sg = _sigmoid(gv)
            silu = gv * sg
            act_ref[rows, :] = (silu * uv).astype(BF16)
            dgate = (dact * uv * (sg * (1.0 + gv * (1.0 - sg)))).astype(BF16)
            dup = (dact * silu).astype(BF16)
            dgu_ref[0, rows, :] = dgate
            dgu_ref[1, rows, :] = dup
            part = _dot(dgate, wgu_ref[0]) + _dot(dup, wgu_ref[1])
            dxn_ref[rows, :] += part

    tok = pl.BlockSpec((tm, d), lambda i, f: (i, 0))
    pair = pl.BlockSpec((2, None, tm, tf), lambda i, f: (0, f, i, 0))
    return _pcall(
        body, grid=(s // tm, nf),
        in_specs=[tok, pair, pl.BlockSpec((2, None, tf, d), lambda i, f: (0, f, 0, 0)),
                  pl.BlockSpec((tf, d), lambda i, f: (f, 0))],
        out_specs=[pl.BlockSpec((None, tm, tf), lambda i, f: (f, i, 0)), pair, tok],
        out_shape=(SDS((nf, s, tf), BF16), SDS((2, nf, s, tf), BF16), SDS((s, d), F32)),
        compiler_params=_arb(2), name="swiglu_backward")(dx3b, gu, w_gu, w_down)


def _attn_backward(dx3, dxn3, x2, g_ffn, x1, g, q, kv, w_q, w_o, tm, carry=()):
    s, d = x1.shape
    _, ml, xd = kv.shape
    scale = xd ** -0.5

    def body(dx3_ref, dxn3_ref, x2_ref, g2_ref, x1_ref, g_ref, q_ref, kv_ref, wq_ref, wo_ref,
             dx2b_ref, dq_ref, dx1_ref, dx1b_ref, dkv_ref, dg_ref, dg2_ref, do_s):
        i = pl.program_id(0)

        @pl.when(i == 0)
        def _():
            dkv_ref[...] = jnp.zeros_like(dkv_ref)
            dg_ref[...] = jnp.zeros_like(dg_ref)
            dg2_ref[...] = jnp.zeros_like(dg2_ref)

        x2v = x2_ref[...]
        dx2n, dg2 = _rms_bwd(dxn3_ref[...], x2v, _rstd(x2v), g2_ref[...])
        dg2_ref[...] += dg2
        dx2 = dx3_ref[...] + dx2n
        dx2b_ref[...] = dx2.astype(BF16)
        do_s[...] = _dot_nt(dx2b_ref[...], wo_ref[...]).astype(BF16)
        for hh in range(HEADS):
            kc = slice(hh * xd, (hh + 1) * xd)
            qh = q_ref[:, kc]
            kh = kv_ref[hh]
            doh = do_s[:, kc]
            p = _softmax(_dot_nt(qh, kh) * scale)
            dp = _dot_nt(doh, kv_ref[HEADS + hh])
            dkv_ref[HEADS + hh] += _dot_tn(p.astype(BF16), doh)
            ds = (p * (dp - jnp.sum(dp * p, axis=-1, keepdims=True)) * scale).astype(BF16)
            dq_ref[:, kc] = _dot(ds, kh).astype(BF16)
            dkv_ref[hh] += _dot_tn(ds, qh)
        dg_tile = jnp.zeros_like(dg_ref)
        for r0 in range(0, tm, min(ROW_CHUNK, tm)):
            rows = slice(r0, r0 + min(ROW_CHUNK, tm))
            dxn = _dot_nt(dq_ref[rows, :], wq_ref[...])
            xv = x1_ref[rows, :]
            dx, dg = _rms_bwd(dxn, xv, _rstd(xv), g_ref[...])
            dg_tile = dg_tile + dg
            dx1 = dx2[rows] + dx
            dx1_ref[rows, :] = dx1
            dx1b_ref[rows, :] = dx1.astype(BF16)
        dg_ref[...] += dg_tile

    tok = pl.BlockSpec((tm, d), lambda i: (i, 0))
    vec = pl.BlockSpec((1, d), lambda i: (0, 0))
    sq = pl.BlockSpec((d, d), lambda i: (0, 0))
    kvs = pl.BlockSpec((2 * HEADS, ml, xd), lambda i: (0, 0, 0))
    return _hosted(
        body, carry, grid=(s // tm,),
        in_specs=[tok, tok, tok, vec, tok, vec, tok, kvs, sq, sq],
        out_specs=[tok, tok, tok, tok, kvs, vec, vec],
        out_shape=(SDS((s, d), BF16), SDS((s, d), BF16), SDS((s, d), F32), SDS((s, d), BF16),
                   SDS((2 * HEADS, ml, xd), F32), SDS((1, d), F32), SDS((1, d), F32)),
        scratch_shapes=[pltpu.VMEM((tm, d), BF16)],
        compiler_params=_arb(1), name="attn_backward")(dx3, dxn3, x2, g_ffn, x1, g, q, kv, w_q, w_o)


def _kv_backward(dkv, memn, mem, g_mem, w_kv):
    ml, d = mem.shape
    xd = w_kv.shape[2]

    def body(dkv_ref, memn_ref, mem_ref, g_ref, w_ref, dw_ref, dg_ref):
        dmemn = jnp.zeros((ml, d), F32)
        for j in range(2 * HEADS):
            dkvb = dkv_ref[j].astype(BF16)
            dw_ref[j] = _dot_tn(memn_ref[...], dkvb)
            dmemn = dmemn + _dot_nt(dkvb, w_ref[j])
        x = mem_ref[...]
        dg_ref[...] = jnp.sum(dmemn * (x * _rstd(x)), axis=0, keepdims=True)

    return _pcall(body, out_shape=(SDS((2 * HEADS, d, xd), F32), SDS((1, d), F32)), name="kv_backward")(dkv, memn, mem, g_mem, w_kv)


def _mix_backward(dx1, x, g_mix, h, lng, lnb, w_sp, bt, conv_w, ga, gb, w_out, w_in, tm, carry=()):
    s, d = x.shape
    n_in = h.shape[1]
    aw = lng.shape[1]
    bw = d - aw
    hd = aw // HEADS
    in_a = 2 * aw
    hb_blocks = tm // HALO
    last_blk = s // HALO - 1
    nt = s // tm
    te = tm + HALO
    tee = tm + 2 * HALO

    def body(dx1_ref, dx1n_ref, x_ref, gm_ref, h_ref, hp_ref, hn_ref, lng_ref, lnb_ref, wsp_ref, bt_ref, cw_ref,
             ga_ref, gb_ref, wout_ref, win_ref,
             dh_ref, dx_ref, dga_ref, dgb_ref, dcw_ref, dlng_ref, dlnb_ref, dwsp_ref, dbs_ref, dgm_ref,
             mixed_s, dvln_s):
        i = pl.program_id(0)

        @pl.when(i == 0)
        def _():
            for ref in (dga_ref, dgb_ref, dcw_ref, dlng_ref, dlnb_ref, dwsp_ref, dbs_ref, dgm_ref):
                ref[...] = jnp.zeros_like(ref)

        mask = _tril_mask()
        wm = [(wsp_ref[hh] * mask).astype(BF16) for hh in range(HEADS)]
        hv = h_ref[...]
        dx1 = dx1_ref[...]
        dx1e = jnp.concatenate([dx1, dx1n_ref[...]], axis=0).astype(BF16)
        dycat = _dot_nt(dx1e, wout_ref[...])

        hbe = jnp.concatenate([hp_ref[:, in_a:], hv[:, in_a:], hn_ref[:, in_a:]], axis=0)
        row = lax.broadcasted_iota(jnp.int32, (tee, 1), 0)
        zext = hbe[:, bw:2 * bw] * hbe[:, 2 * bw:]
        zext = jnp.where((i == 0) & (row < HALO), 0.0, zext)
        z2e, z1e = _conv_taps(zext)
        cw = cw_ref[...]
        conv_e = (cw[0:1] * z2e + cw[1:2] * z1e + cw[2:3] * zext)[HALO:]
        gate_b_e = hbe[HALO:, :bw]
        sc_e = gate_b_e * conv_e
        rb = _rstd(sc_e)
        dyb = dycat[:, aw:]
        gdy = dyb * gb_ref[...]
        dsc_e = rb * gdy - sc_e * (rb * rb * rb) * (jnp.sum(gdy * sc_e, axis=-1, keepdims=True) * (1.0 / bw))
        dgb_ref[...] += jnp.sum((dyb * (sc_e * rb))[:tm], axis=0, keepdims=True)
        dconv_e = dsc_e * gate_b_e
        dconv_e = jnp.where((i == nt - 1) & (row[:te] >= tm), 0.0, dconv_e)
        dconv = dconv_e[:tm]
        dc1 = pltpu.roll(dconv_e, te - 1, 0)[:tm]
        dc2 = pltpu.roll(dconv_e, te - 2, 0)[:tm]
        dz = cw[2:3] * dconv + cw[1:2] * dc1 + cw[0:1] * dc2
        z = zext[HALO:HALO + tm]
        z1 = z1e[HALO:HALO + tm]
        z2 = z2e[HALO:HALO + tm]
        dcw_ref[0:1, :] += jnp.sum(dconv * z2, axis=0, keepdims=True)
        dcw_ref[1:2, :] += jnp.sum(dconv * z1, axis=0, keepdims=True)
        dcw_ref[2:3, :] += jnp.sum(dconv * z, axis=0, keepdims=True)
        dh_ref[:, in_a:in_a + bw] = (dsc_e[:tm] * conv_e[:tm]).astype(BF16)
        dh_ref[:, in_a + bw:in_a + 2 * bw] = (dz * hv[:, in_a + 2 * bw:]).astype(BF16)
        dh_ref[:, in_a + 2 * bw:] = (dz * hv[:, in_a + bw:in_a + 2 * bw]).astype(BF16)

        ha = hv[:, :in_a]
        th, u, xhat, rl, vln = _sgu_forward(ha, lng_ref[...], lnb_ref[...], wm, bt_ref[...], mixed_s)
        mixed = mixed_s[...]
        sg = u * mixed
        dsg, dga = _rms_bwd(dycat[:tm, :aw], sg, _rstd(sg), ga_ref[...])
        dga_ref[...] += dga
        du = dsg * mixed
        dmixed = dsg * u
        dmb = dmixed.astype(BF16)
        for n in range(tm // CHUNK):
            rows = slice(n * CHUNK, (n + 1) * CHUNK)
            dbs_ref[...] += dmixed[rows]
            for hh in range(HEADS):
                cols = slice(hh * hd, (hh + 1) * hd)
                dvln_s[rows, cols] = _dot_tn(wm[hh], dmb[rows, cols])
                dwsp_ref[hh] += mask * _dot_nt(dmb[rows, cols], vln[rows, cols])
        dvln = dvln_s[...]
        dlng_ref[...] += jnp.sum(dvln * xhat, axis=0, keepdims=True)
        dlnb_ref[...] += jnp.sum(dvln, axis=0, keepdims=True)
        dxh = dvln * lng_ref[...]
        dv = rl * (dxh - jnp.mean(dxh, axis=-1, keepdims=True) - xhat * jnp.mean(dxh * xhat, axis=-1, keepdims=True))
        dh_ref[:, :in_a] = (jnp.concatenate([du, dv], axis=-1) * _gelu_grad(ha, th)).astype(BF16)

        dgm_tile = jnp.zeros_like(dgm_ref)
        for r0 in range(0, tm, min(ROW_CHUNK, tm)):
            rows = slice(r0, r0 + min(ROW_CHUNK, tm))
            dxn = _dot(dh_ref[rows, :], win_ref[...])
            xv = x_ref[rows, :]
            dx, dgm = _rms_bwd(dxn, xv, _rstd(xv), gm_ref[...])
            dgm_tile = dgm_tile + dgm
            dx_ref[rows, :] = dx1_ref[rows, :] + dx
        dgm_ref[...] += dgm_tile

    full = lambda shape: pl.BlockSpec(shape, lambda i: (0,) * len(shape))
    tok = pl.BlockSpec((tm, d), lambda i: (i, 0))
    nxt = lambda i: (jnp.minimum((i + 1) * hb_blocks, last_blk), 0)
    prv = lambda i: (jnp.maximum(i * hb_blocks - 1, 0), 0)
    return _hosted(
        body, carry, grid=(nt,),
        in_specs=[tok, pl.BlockSpec((HALO, d), nxt), tok, full((1, d)),
                  pl.BlockSpec((tm, n_in), lambda i: (i, 0)), pl.BlockSpec((HALO, n_in), prv), pl.BlockSpec((HALO, n_in), nxt),
                  full((1, aw)), full((1, aw)), full((HEADS, CHUNK, CHUNK)), full((CHUNK, HEADS)), full((3, bw)),
                  full((1, aw)), full((1, bw)), full((d, d)), full((n_in, d))],
        out_specs=[pl.BlockSpec((tm, n_in), lambda i: (i, 0)), tok,
                   full((1, aw)), full((1, bw)), full((SUB, bw)), full((1, aw)), full((1, aw)),
                   full((HEADS, CHUNK, CHUNK)), full((CHUNK, aw)), full((1, d))],
        out_shape=(SDS((s, n_in), BF16), SDS((s, d), F32),
                   SDS((1, aw), F32), SDS((1, bw), F32), SDS((SUB, bw), F32), SDS((1, aw), F32), SDS((1, aw), F32),
                   SDS((HEADS, CHUNK, CHUNK), F32), SDS((CHUNK, aw), F32), SDS((1, d), F32)),
        scratch_shapes=[pltpu.VMEM((tm, aw), F32), pltpu.VMEM((tm, aw), F32)],
        compiler_params=_arb(1), name="mix_backward")(dx1, dx1, x, g_mix, h, h, h, lng, lnb, w_sp, bt, conv_w, ga, gb, w_out, w_in)


def _bias_grad(dbs):
    aw = dbs.shape[1]
    hd = aw // HEADS

    def body(dbs_ref, out_ref):
        ones = jnp.ones((SUB, hd), F32)
        for hh in range(HEADS):
            r = lax.dot_general(ones, dbs_ref[:, hh * hd:(hh + 1) * hd], (((1,), (1,)), ((), ())),
                                precision=lax.Precision.HIGHEST, preferred_element_type=F32)
            out_ref[hh:hh + 1, :] = r[0:1]

    return _pcall(body, out_shape=SDS((HEADS, CHUNK), F32), name="bias_grad")(dbs)


def _wgrad_body(a_ref, b_ref, o_ref):
    o_ref[...] = _dot_tn(a_ref[...], b_ref[...])


def _wgrad(a, b, name, carry=()):
    k, m = a.shape
    n = b.shape[1]
    tm = _tile(m, 512, LANES)
    tn = _tile(n, 1024, LANES)
    return _hosted(
        functools.partial(_wgrad_body), carry, grid=(m // tm, n // tn),
        in_specs=[pl.BlockSpec((k, tm), lambda i, j: (0, i)), pl.BlockSpec((k, tn), lambda i, j: (0, j))],
        out_specs=pl.BlockSpec((tm, tn), lambda i, j: (i, j)),
        out_shape=SDS((m, n), F32), compiler_params=_arb(2), name=name)(a, b)


def _wgrad_blocked_lhs(a, b, name, carry=()):
    nb, k, t = a.shape
    n = b.shape[1]
    tn = _tile(n, 1024, LANES)
    return _hosted(
        functools.partial(_wgrad_body), carry, grid=(nb, n // tn),
        in_specs=[pl.BlockSpec((None, k, t), lambda i, j: (i, 0, 0)), pl.BlockSpec((k, tn), lambda i, j: (0, j))],
        out_specs=pl.BlockSpec((t, tn), lambda i, j: (i, j)),
        out_shape=SDS((nb * t, n), F32), compiler_params=_arb(2), name=name)(a, b)


def _wgrad_blocked_rhs(a, b, name, carry=()):
    k, m = a.shape
    nb, _, t = b.shape
    tm = _tile(m, 512, LANES)
    return _hosted(
        functools.partial(_wgrad_body), carry, grid=(m // tm, nb),
        in_specs=[pl.BlockSpec((k, tm), lambda i, j: (0, i)), pl.BlockSpec((None, k, t), lambda i, j: (j, 0, 0))],
        out_specs=pl.BlockSpec((None, tm, t), lambda i, j: (j, i, 0)),
        out_shape=SDS((nb, m, t), F32), compiler_params=_arb(2), name=name)(a, b)


def _unblock_cols(wb, name, carry=()):
    nb, r, t = wb.shape
    tr = _tile(r, 256, 16)

    def body(w_ref, o_ref):
        o_ref[...] = jnp.concatenate([w_ref[j].astype(F32) for j in range(nb)], axis=-1).astype(o_ref.dtype)

    return _hosted(
        body, carry, grid=(r // tr,),
        in_specs=[pl.BlockSpec((nb, tr, t), lambda i: (0, i, 0))], out_specs=pl.BlockSpec((tr, nb * t), lambda i: (i, 0)),
        out_shape=SDS((r, nb * t), wb.dtype), compiler_params=_arb(1), name=name)(wb)


def _block_cols(w, nb, name, carry=()):
    r, n = w.shape
    t = n // nb
    tr = _tile(r, 256, 16)

    def body(w_ref, o_ref):
        wv = w_ref[...]
        for j in range(nb):
            o_ref[j] = wv[:, j * t:(j + 1) * t]

    return _hosted(
        body, carry, grid=(r // tr,),
        in_specs=[pl.BlockSpec((tr, n), lambda i: (i, 0))], out_specs=pl.BlockSpec((nb, tr, t), lambda i: (0, i, 0)),
        out_shape=SDS((nb, r, t), w.dtype), compiler_params=_arb(1), name=name)(w)


def _place():
    x, y, c = lax.axis_index("x"), lax.axis_index("y"), lax.axis_index("c")
    return x, y, c, [(1 - x, y), (x, 1 - y), (1 - x, 1 - y)]


def _all_gather(shards):
    n = len(shards)
    slots = 9
    cut = [(s.shape[0] // 32) * 16 for s in shards]

    def build(ins, outs, sems):
        send_sems, recv_sems, local_sems = sems
        x, y, c, _ = _place()
        me, sib, xn, yn, dg = (x, y, c), (x, y, 1 - c), (1 - x, y, c), (x, 1 - y, c), (1 - x, 1 - y, c)
        other = lambda p: (p[0], p[1], 1 - p[2])

        def rows(a, p, part=None):
            ref = outs[a].at[4 * p[0] + 2 * p[1] + p[2]]
            if part is None or cut[a] == 0:
                return ref if part in (None, 0) else None
            return ref.at[pl.ds(0, cut[a])] if part == 0 else ref.at[pl.ds(cut[a], shards[a].shape[0] - cut[a])]

        def copy(a, k, ref, to, src=None):
            if ref is None:
                return None
            return pltpu.make_async_remote_copy(
                src_ref=ref if src is None else src, dst_ref=ref, send_sem=send_sems.at[slots * a + k],
                recv_sem=recv_sems.at[slots * a + k], device_id=to, device_id_type=MESH)

        def real(cps):
            return [cp for cp in cps if cp is not None]

        class Copies:
            own = lambda a: [copy(a, 1, rows(a, me), xn, ins[a]), copy(a, 2, rows(a, me), yn, ins[a]),
                             copy(a, 0, rows(a, me), sib, ins[a])]
            local = lambda a: pltpu.make_async_copy(ins[a], rows(a, me), local_sems.at[a])
            from_x = lambda a: copy(a, 1, rows(a, xn), me)
            from_y = lambda a: copy(a, 2, rows(a, yn), me)
            after_x = lambda a: real([copy(a, 4, rows(a, xn, 1), yn), copy(a, 5, rows(a, xn), sib)])
            after_y = lambda a: real([copy(a, 3, rows(a, yn, 0), xn), copy(a, 6, rows(a, yn), sib)])
            diag_in = lambda a: real([copy(a, 3, rows(a, dg, 0), me), copy(a, 4, rows(a, dg, 1), me)])
            diag_on = lambda a: real([copy(a, 7, rows(a, dg, 0), sib), copy(a, 8, rows(a, dg, 1), sib)])
            from_sib = lambda a: real([copy(a, 0, rows(a, sib), me), copy(a, 5, rows(a, other(xn)), me),
                                       copy(a, 6, rows(a, other(yn)), me), copy(a, 7, rows(a, other(dg), 0), me),
                                       copy(a, 8, rows(a, other(dg), 1), me)])

        return Copies

    def start(ins, outs, sems):
        cps = build(ins, outs, sems)
        for a in range(n):
            for cp in cps.own(a):
                cp.start()
        for a in range(n):
            cps.local(a).start()

    def relay(ins, outs, sems):
        cps = build(ins, outs, sems)
        for a in range(n):
            cps.from_x(a).wait_recv()
            for cp in cps.after_x(a):
                cp.start()
            cps.from_y(a).wait_recv()
            for cp in cps.after_y(a):
                cp.start()

    def finish(ins, outs, sems):
        cps = build(ins, outs, sems)
        for a in range(n):
            for arrived, onward in zip(cps.diag_in(a), cps.diag_on(a)):
                arrived.wait_recv()
                onward.start()
        for a in range(n):
            for cp in cps.from_sib(a):
                cp.wait_recv()
            for cp in cps.own(a) + cps.after_x(a) + cps.after_y(a) + cps.diag_on(a):
                cp.wait_send()
            cps.local(a).wait()

    return _Exchange(shards, [SDS((N_DEV,) + s.shape, s.dtype) for s in shards],
                     [pltpu.SemaphoreType.DMA((slots * n,)), pltpu.SemaphoreType.DMA((slots * n,)),
                      pltpu.SemaphoreType.DMA((n,))], start, finish, relay)


def _swap_exchange(ins, out_shape, per, copies):
    def start(i, o, sems):
        for cp in copies(i, o, sems):
            cp.start()

    def finish(i, o, sems):
        for cp in copies(i, o, sems):
            cp.wait()

    n = per * len(ins)
    return _Exchange(ins, out_shape, [pltpu.SemaphoreType.DMA((n,)), pltpu.SemaphoreType.DMA((n,))], start, finish)


def _exchange_c(gs):
    def copies(ins, outs, sems):
        x, y, c, _ = _place()
        return [pltpu.make_async_remote_copy(
                    src_ref=ins[a].at[2 * k + 1 - c], dst_ref=outs[a].at[k],
                    send_sem=sems[0].at[4 * a + k], recv_sem=sems[1].at[4 * a + k],
                    device_id=(x, y, 1 - c), device_id_type=MESH)
                for a in range(len(gs)) for k in range(4)]

    return _swap_exchange(gs, [SDS((4,) + g.shape[1:], g.dtype) for g in gs], 4, copies)


def _exchange_xy(sends):
    def copies(ins, outs, sems):
        x, y, c, chips = _place()
        return [pltpu.make_async_remote_copy(
                    src_ref=ins[a].at[t], dst_ref=outs[a].at[t],
                    send_sem=sems[0].at[3 * a + t], recv_sem=sems[1].at[3 * a + t],
                    device_id=(*chips[t], c), device_id_type=MESH)
                for a in range(len(sends)) for t in range(3)]

    return _swap_exchange(sends, [SDS(s.shape, s.dtype) for s in sends], 3, copies)


def _rs_combine(g, recv, pos, name, carry=()):
    _, r, cdim = g.shape
    tr = _tile(r, 256, 16)

    def body(pos_ref, g0, r0, g1, r1, g2, r2, g3, r3, keep_ref, send_ref):
        keep_ref[...] = g0[...] + r0[...]
        send_ref[0] = (g1[...] + r1[...]).astype(BF16)
        send_ref[1] = (g2[...] + r2[...]).astype(BF16)
        send_ref[2] = (g3[...] + r3[...]).astype(BF16)

    def k_of(p, t):
        px = p[0] if t in (0, 2) else 1 - p[0]
        py = p[1] if t in (0, 1) else 1 - p[1]
        return 2 * px + py

    blk = (None, tr, cdim)
    in_specs = []
    for t in range(4):
        in_specs.append(pl.BlockSpec(blk, functools.partial(lambda j, p, t: (2 * k_of(p, t) + p[2], j, 0), t=t)))
        in_specs.append(pl.BlockSpec(blk, functools.partial(lambda j, p, t: (k_of(p, t), j, 0), t=t)))
    return _hosted(
        body, carry, n_prefetch=1, out_shape=(SDS((r, cdim), F32), SDS((3, r, cdim), BF16)),
        grid=(r // tr,), in_specs=in_specs,
        out_specs=[pl.BlockSpec((tr, cdim), lambda j, p: (j, 0)), pl.BlockSpec((3, tr, cdim), lambda j, p: (0, j, 0))],
        compiler_params=_arb(1), name=name)(pos, g, recv, g, recv, g, recv, g, recv)


def _adamw_shard(keep, recv, w, m, v, name, after=None):
    r, cdim = w.shape
    tr = _tile(r, 256, 16)

    def body(k_ref, r_ref, w_ref, m_ref, v_ref, *rest):
        g_ref, d_ref, nm_ref, nv_ref = rest[-4:]
        g = ((k_ref[...] + r_ref[0].astype(F32)) + r_ref[1].astype(F32)) + r_ref[2].astype(F32)
        g_ref[...] = g
        d_ref[...], nm_ref[...], nv_ref[...] = _adamw(w_ref[...], g, m_ref[...], v_ref[...])

    blk = pl.BlockSpec((tr, cdim), lambda j: (j, 0))
    out = SDS((r, cdim), F32)
    order = [] if after is None else [after]
    return _pcall(body, grid=(r // tr,),
                  in_specs=[blk, pl.BlockSpec((3, tr, cdim), lambda j: (0, j, 0)), blk, blk, blk] + _hbm_specs(len(order)),
                  out_specs=[blk] * 4, out_shape=(out,) * 4, compiler_params=_arb(1), name=name)(keep, recv, w, m, v, *order)


_HBM = pl.BlockSpec(memory_space=pltpu.HBM)
_SEM = pl.BlockSpec(memory_space=pltpu.SEMAPHORE)
_SPLIT = pltpu.CompilerParams(has_side_effects=pltpu.SideEffectType.DATAFLOW_SIDE_EFFECTING)


def _xy_copies(s_ref, land_ref, send_sems, recv_sems):
    x, y, c, chips = _place()
    return [pltpu.make_async_remote_copy(src_ref=s_ref.at[t], dst_ref=land_ref.at[t], send_sem=send_sems.at[t],
                                         recv_sem=recv_sems.at[t], device_id=(*chips[t], c), device_id_type=MESH)
            for t in range(3)]


def _exchange_xy_start(send, name):
    def body(s_ref, land_ref, send_sems, recv_sems, s_thru, land_thru, token):
        for cp in _xy_copies(s_ref, land_ref, send_sems, recv_sems):
            cp.start()
        token[...] = jnp.zeros_like(token)

    hbm = lambda a: pltpu.with_memory_space_constraint(a, pltpu.HBM)
    res = _pcall(
        body, name=name,
        out_shape=(pltpu.SemaphoreType.DMA((3,)), pltpu.SemaphoreType.DMA((3,)), pltpu.HBM(send.shape, send.dtype),
                   pltpu.HBM(send.shape, send.dtype), SDS((SUB, LANES), F32)),
        in_specs=(_HBM, _HBM), out_specs=(_SEM, _SEM, _HBM, _HBM, pl.BlockSpec(memory_space=pltpu.VMEM)),
        input_output_aliases={0: 2, 1: 3}, compiler_params=_SPLIT)(hbm(send), hbm(lax.empty(send.shape, send.dtype)))
    return res[:4], res[4]


def _exchange_xy_wait(started, after, name):
    send_sems, recv_sems, s_thru, land_thru = started

    def body(s_ref, land_ref, send_sems, recv_sems, after_ref, s_dead, got_ref):
        for cp in _xy_copies(s_ref, land_ref, send_sems, recv_sems):
            cp.wait_send()
            cp.wait_recv()

    return _pcall(
        body, name=name, out_shape=(pltpu.HBM(s_thru.shape, s_thru.dtype), pltpu.HBM(s_thru.shape, s_thru.dtype)),
        in_specs=(_HBM, _HBM, _SEM, _SEM, pl.BlockSpec(memory_space=pl.ANY)), out_specs=(_HBM, _HBM),
        input_output_aliases={0: 0, 1: 1}, compiler_params=_SPLIT)(s_thru, land_thru, send_sems, recv_sems, after)[1]


def _adamw_small(gathered, seg, params, conv_rows):
    names = list(params)
    c0, cn = conv_rows

    def body(*refs):
        gat_ref = refs[0]
        ins = refs[1:1 + 3 * len(names)]
        outs = refs[1 + 3 * len(names):]

        def total(r0, rn):
            tot = gat_ref[0, r0:r0 + rn, :]
            for dev in range(1, N_DEV):
                tot = tot + gat_ref[dev, r0:r0 + rn, :]
            return tot

        for k, nm in enumerate(names):
            g = total(*seg[nm])
            w_ref, m_ref, v_ref = ins[3 * k:3 * k + 3]
            g_ref, d_ref, nm_ref, nv_ref = outs[4 * k:4 * k + 4]
            g_ref[...] = g
            d_ref[...], nm_ref[...], nv_ref[...] = _adamw(w_ref[...], g, m_ref[...], v_ref[...])
        outs[-2][...] = total(c0, cn)
        outs[-1][...] = total(*seg["loss"])

    flat_in = [a for nm in names for a in params[nm]]
    out_shape = []
    for nm in names:
        out_shape += [SDS(params[nm][0].shape, F32)] * 4
    out_shape += [SDS((cn, LANES), F32), SDS((seg["loss"][1], LANES), F32)]
    res = _pcall(body, out_shape=tuple(out_shape), name="adamw_small")(gathered, *flat_in)
    per = {nm: res[4 * k:4 * k + 4] for k, nm in enumerate(names)}
    return per, res[-2], res[-1]


def _adamw_one(w, g, m, v, name):
    def body(w_ref, g_ref, m_ref, v_ref, d_ref, nm_ref, nv_ref):
        d_ref[...], nm_ref[...], nv_ref[...] = _adamw(w_ref[...], g_ref[...], m_ref[...], v_ref[...])

    return _pcall(body, out_shape=(SDS(w.shape, F32),) * 3, name=name)(w, g, m, v)


def _rows128(a):
    return a.reshape(-1, LANES)


def _pack_small(gs, loss_tile):
    seg, pieces, row = {}, [], 0
    for nm in SMALL + ("conv_w", "loss"):
        piece = loss_tile if nm == "loss" else _rows128(gs[nm])
        rn = _round_up(piece.shape[0], SUB)
        pieces.append(jnp.pad(piece, ((0, rn - piece.shape[0]), (0, 0))))
        seg[nm] = (row, piece.shape[0])
        row += rn
    return jnp.concatenate(pieces, axis=0), seg


def _step(x, mem, target, wb, conv_w, sp, pos):
    s, d = x.shape
    tm = min(TOKEN_TILE, s)
    tm_wide = min(2 * TOKEN_TILE, s)
    rows = lambda w8: w8.reshape(-1, w8.shape[2])
    shards = lambda g: g.reshape((N_DEV, -1) + g.shape[1:])
    bt = sp["b_spatial"].T

    (w_in8, conv8), = _run_exchanges([_all_gather([wb["w_in"], conv_w])], "gather_w_in")
    conv_full = conv8.transpose(1, 0, 2).reshape(3, -1)
    w_in_t = rows(w_in8)
    (xn1, h), ((w_out8, w_kv8, w_q8),) = _in_forward(
        x, sp["ln_mix_g"], w_in_t, tm, carry=[_all_gather([wb["w_out"], wb["w_kv"], wb["w_q"]])])
    w_out = rows(w_out8)
    (ycat, x1), ((w_o8, w_down8),) = _mix_forward(
        h, x, sp["sgu_ln_g"], sp["sgu_ln_b"], sp["w_spatial"], bt, conv_full, sp["grp_norm_a"], sp["grp_norm_b"], w_out, tm,
        carry=[_all_gather([wb["w_o"], wb["w_down"]])])
    w_q, w_o, w_down = rows(w_q8), rows(w_o8), rows(w_down8)
    memn, kv = _kv_forward(mem, sp["ln_mem_g"], w_kv8)
    (xn2, q, o, x2), ((w_gu8,),) = _attn_forward(
        x1, sp["ln_attn_g"], w_q, kv, w_o, tm, carry=[_all_gather([wb["w_gate_up"]])])
    w_gu = w_gu8.reshape((2, N_DEV // 2) + w_gu8.shape[1:])
    xn3, gu, x3 = _ffn_forward(x2, sp["ln_ffn_g"], w_gu, w_down, tm_wide)

    loss, d_lnf, dx3, dx3b = _final_backward(x3, target, sp["ln_final_g"], tm_wide)
    act, dgu, dxn3 = _swiglu_backward(dx3b, gu, w_gu, w_down, tm_wide)
    part = {}
    g_gu, _ = _wgrad_blocked_lhs(dgu.reshape((N_DEV,) + dgu.shape[2:]), xn3, "wgrad_gate_up")
    g_gu = shards(g_gu)
    g_down, ((rc_gu,),) = _wgrad_blocked_lhs(act, dx3b, "wgrad_down", carry=[_exchange_c([g_gu])])
    g_down = shards(g_down)
    (keep_gu, send_gu), _ = _rs_combine(g_gu, rc_gu, pos, "rs_combine_w_gate_up")
    (dx2b, dq, dx1, dx1b, dkv, d_lnattn, d_lnffn), ((rxy_gu,), (rc_down,)) = _attn_backward(
        dx3, dxn3, x2, sp["ln_ffn_g"], x1, sp["ln_attn_g"], q, kv, w_q, w_o, tm,
        carry=[_exchange_xy([send_gu]), _exchange_c([g_down])])
    part["w_gate_up"] = (keep_gu, rxy_gu)
    (keep_down, send_down), _ = _rs_combine(g_down, rc_down, pos, "rs_combine_w_down")
    g_o, _ = _wgrad(o, dx2b, "wgrad_o")
    g_o = shards(g_o)
    g_q, ((rc_o,),) = _wgrad(xn2, dq, "wgrad_q", carry=[_exchange_c([g_o])])
    g_q = shards(g_q)
    g_out, ((rc_q,),) = _wgrad(ycat, dx1b, "wgrad_out", carry=[_exchange_c([g_q])])
    g_out = shards(g_out)
    g_kv, d_lnmem = _kv_backward(dkv, memn, mem, sp["ln_mem_g"], w_kv8)
    (keep_o, send_o), ((rc_out,),) = _rs_combine(g_o, rc_o, pos, "rs_combine_w_o", carry=[_exchange_c([g_out])])
    (keep_q, send_q), _ = _rs_combine(g_q, rc_q, pos, "rs_combine_w_q")
    (keep_out, send_out), _ = _rs_combine(g_out, rc_out, pos, "rs_combine_w_out")
    ((dh, dx, d_ga, d_gb, d_cw, d_lng, d_lnb, d_wsp, d_bs, d_lnmix),
     ((rxy_down, rxy_o, rxy_q, rxy_out), (rc_kv,))) = _mix_backward(
        dx1, x, sp["ln_mix_g"], h, sp["sgu_ln_g"], sp["sgu_ln_b"], sp["w_spatial"], bt, conv_full,
        sp["grp_norm_a"], sp["grp_norm_b"], w_out, w_in_t, tm,
        carry=[_exchange_xy([send_down, send_o, send_q, send_out]), _exchange_c([g_kv])])
    part["w_down"], part["w_o"], part["w_q"] = (keep_down, rxy_down), (keep_o, rxy_o), (keep_q, rxy_q)
    part["w_out"] = (keep_out, rxy_out)
    (keep_kv, send_kv), _ = _rs_combine(g_kv, rc_kv, pos, "rs_combine_w_kv")
    gs = {"ln_mix_g": d_lnmix, "sgu_ln_g": d_lng, "sgu_ln_b": d_lnb, "w_spatial": d_wsp, "b_spatial": _bias_grad(d_bs),
          "conv_w": d_cw[:3], "grp_norm_a": d_ga, "grp_norm_b": d_gb, "ln_attn_g": d_lnattn, "ln_mem_g": d_lnmem,
          "ln_ffn_g": d_lnffn, "ln_final_g": d_lnf}
    packed, seg = _pack_small(gs, loss)
    g_in, ((rxy_kv,), (small_all,)) = _wgrad(
        dh, xn1, "wgrad_in", carry=[_exchange_xy([send_kv]), _all_gather([packed])])
    g_in = shards(g_in)
    part["w_kv"] = (keep_kv, rxy_kv)
    (rc_in,), = _run_exchanges([_exchange_c([g_in])], "exchange_c_w_in")
    (keep_in, send_in), _ = _rs_combine(g_in, rc_in, pos, "rs_combine_w_in")
    part["w_in"] = (keep_in,) + _exchange_xy_start(send_in, "exchange_xy_w_in_start")
    return dx, part, small_all, seg


def kernel(x, mem, ln_mix_g, w_in, sgu_ln_g, sgu_ln_b, w_spatial, b_spatial, conv_w, grp_norm_a, grp_norm_b, w_out, ln_attn_g, ln_mem_g, w_q, w_kv, w_o, ln_ffn_g, w_gate_up, w_down, ln_final_g, loss_target, m_ln_mix_g, m_w_in, m_sgu_ln_g, m_sgu_ln_b, m_w_spatial, m_b_spatial, m_conv_w, m_grp_norm_a, m_grp_norm_b, m_w_out, m_ln_attn_g, m_ln_mem_g, m_w_q, m_w_kv, m_w_o, m_ln_ffn_g, m_w_gate_up, m_w_down, m_ln_final_g, v_ln_mix_g, v_w_in, v_sgu_ln_g, v_sgu_ln_b, v_w_spatial, v_b_spatial, v_conv_w, v_grp_norm_a, v_grp_norm_b, v_w_out, v_ln_attn_g, v_ln_mem_g, v_w_q, v_w_kv, v_w_o, v_ln_ffn_g, v_w_gate_up, v_w_down, v_ln_final_g):
    order = ["ln_mix_g", "w_in", "sgu_ln_g", "sgu_ln_b", "w_spatial", "b_spatial", "conv_w", "grp_norm_a", "grp_norm_b",
             "w_out", "ln_attn_g", "ln_mem_g", "w_q", "w_kv", "w_o", "ln_ffn_g", "w_gate_up", "w_down", "ln_final_g"]
    W = dict(ln_mix_g=ln_mix_g, w_in=w_in, sgu_ln_g=sgu_ln_g, sgu_ln_b=sgu_ln_b, w_spatial=w_spatial, b_spatial=b_spatial,
             conv_w=conv_w, grp_norm_a=grp_norm_a, grp_norm_b=grp_norm_b, w_out=w_out, ln_attn_g=ln_attn_g,
             ln_mem_g=ln_mem_g, w_q=w_q, w_kv=w_kv, w_o=w_o, ln_ffn_g=ln_ffn_g, w_gate_up=w_gate_up, w_down=w_down,
             ln_final_g=ln_final_g)
    M = dict(ln_mix_g=m_ln_mix_g, w_in=m_w_in, sgu_ln_g=m_sgu_ln_g, sgu_ln_b=m_sgu_ln_b, w_spatial=m_w_spatial,
             b_spatial=m_b_spatial, conv_w=m_conv_w, grp_norm_a=m_grp_norm_a, grp_norm_b=m_grp_norm_b, w_out=m_w_out,
             ln_attn_g=m_ln_attn_g, ln_mem_g=m_ln_mem_g, w_q=m_w_q, w_kv=m_w_kv, w_o=m_w_o, ln_ffn_g=m_ln_ffn_g,
             w_gate_up=m_w_gate_up, w_down=m_w_down, ln_final_g=m_ln_final_g)
    V = dict(ln_mix_g=v_ln_mix_g, w_in=v_w_in, sgu_ln_g=v_sgu_ln_g, sgu_ln_b=v_sgu_ln_b, w_spatial=v_w_spatial,
             b_spatial=v_b_spatial, conv_w=v_conv_w, grp_norm_a=v_grp_norm_a, grp_norm_b=v_grp_norm_b, w_out=v_w_out,
             ln_attn_g=v_ln_attn_g, ln_mem_g=v_ln_mem_g, w_q=v_w_q, w_kv=v_w_kv, w_o=v_w_o, ln_ffn_g=v_ln_ffn_g,
             w_gate_up=v_w_gate_up, w_down=v_w_down, ln_final_g=v_ln_final_g)

    bw = conv_w.shape[1] * N_DEV
    pos = jnp.stack([lax.axis_index("x"), lax.axis_index("y"), lax.axis_index("c")]).astype(jnp.int32)
    me = 4 * pos[0] + 2 * pos[1] + pos[2]

    sp = {nm: (W[nm].reshape(1, -1) if W[nm].ndim == 1 else W[nm]) for nm in SMALL}
    view = lambda a, nm: a.T if nm in TRANSPOSED else a
    wb = {nm: view(W[nm], nm).astype(BF16) for nm in BIG}
    grad_x, part, small_all, seg = _step(x[0], mem[0], loss_target[0], wb, conv_w, sp, pos)

    out = {}
    keep_in, started, token = part["w_in"]
    for nm in BIG[1:]:
        res = _adamw_shard(part[nm][0], part[nm][1], view(W[nm], nm), view(M[nm], nm), view(V[nm], nm), "adamw_" + nm,
                           after=token)
        out[nm] = tuple(view(a, nm) for a in res)
        token = res[0]
    nm = BIG[0]
    rxy_in = _exchange_xy_wait(started, token, "exchange_xy_w_in_wait")
    res = _adamw_shard(keep_in, rxy_in, view(W[nm], nm), view(M[nm], nm), view(V[nm], nm), "adamw_" + nm)
    out[nm] = tuple(view(a, nm) for a in res)

    params = {nm: (_rows128(W[nm]), _rows128(M[nm]), _rows128(V[nm])) for nm in SMALL}
    per, conv_g_rows, loss_sum = _adamw_small(small_all, seg, params, seg["conv_w"])
    for nm in SMALL:
        out[nm] = tuple(a.reshape(W[nm].shape) for a in per[nm])
    conv_g = lax.dynamic_slice_in_dim(conv_g_rows.reshape(3, bw), me * conv_w.shape[1], conv_w.shape[1], axis=1)
    out["conv_w"] = (conv_g,) + tuple(_adamw_one(conv_w, conv_g, m_conv_w, v_conv_w, "adamw_conv"))

    loss = loss_sum[0, 0]
    res = [loss, grad_x[None]]
    for k in range(4):
        res += [out[nm][k] for nm in order]
    return tuple(res)
```

```python
import functools

import jax
import jax.numpy as jnp
from jax import lax
from jax.experimental import pallas as pl
from jax.experimental.pallas import tpu as pltpu

F32 = jnp.float32
BF16 = jnp.bfloat16
SDS = jax.ShapeDtypeStruct
MESH = pl.DeviceIdType.MESH

EPS = 1e-6
N_DEV = 8
HEADS = 4
CHUNK = 128
HALO = 16
SUB = 8
LANES = 128
TOKEN_TILE = 512
ROW_CHUNK = 256
RELAY_AT = 0.7

ADAM_LR = 0.001
ADAM_B1 = 0.9
ADAM_B2 = 0.999
ADAM_EPS = 1e-08
ADAM_WD = 0.01
ADAM_STEP = 10

BIG = ("w_in", "w_out", "w_q", "w_kv", "w_o", "w_gate_up", "w_down")
TRANSPOSED = ("w_in", "w_gate_up")
SMALL = ("ln_mix_g", "sgu_ln_g", "sgu_ln_b", "w_spatial", "b_spatial", "grp_norm_a", "grp_norm_b",
         "ln_attn_g", "ln_mem_g", "ln_ffn_g", "ln_final_g")


class _Exchange:
    def __init__(self, ins, out_shape, sems, start, finish, relay=None):
        self.ins, self.out_shape, self.sems = list(ins), list(out_shape), list(sems)
        self.start, self.finish, self.relay = start, finish, relay


def _pcall(body, carry=(), n_prefetch=0, **kw):
    if carry:
        return functools.partial(_carrying_call, body, tuple(carry), n_prefetch, kw)
    if n_prefetch:
        kw["grid_spec"] = pltpu.PrefetchScalarGridSpec(
            num_scalar_prefetch=n_prefetch, grid=kw.pop("grid"), in_specs=kw.pop("in_specs"),
            out_specs=kw.pop("out_specs"), scratch_shapes=kw.pop("scratch_shapes", ()))
    return pl.pallas_call(body, **kw)


def _carrying_call(body, carry, n_prefetch, kw, *args):
    kw = dict(kw)
    out_shape = kw.pop("out_shape")
    single = not isinstance(out_shape, (tuple, list))
    outs_shape = (out_shape,) if single else tuple(out_shape)
    out_specs = kw.pop("out_specs")
    out_specs = [out_specs] if single else list(out_specs)
    in_specs = list(kw.pop("in_specs"))
    scratch = list(kw.pop("scratch_shapes", ()))
    grid = tuple(kw.get("grid", ()))
    n_in, n_out, n_scr = len(args), len(outs_shape), len(scratch)

    def split(refs, k, counts):
        parts = []
        for cnt in counts:
            parts.append(refs[k:k + cnt])
            k += cnt
        return parts, k

    def wrapped(*refs):
        cins, k = split(refs, n_in, [len(p.ins) for p in carry])
        outs = refs[k:k + n_out]
        couts, k = split(refs, k + n_out, [len(p.out_shape) for p in carry])
        scr = refs[k:k + n_scr]
        csems, _ = split(refs, k + n_scr, [len(p.sems) for p in carry])
        first, last = True, True
        for a, g in enumerate(grid):
            first = (pl.program_id(a) == 0) & first
            last = (pl.program_id(a) == g - 1) & last

        def start_all():
            for p, ci, co, cs in zip(carry, cins, couts, csems):
                p.start(ci, co, cs)

        def relay_all():
            for p, ci, co, cs in zip(carry, cins, couts, csems):
                if p.relay is not None:
                    p.relay(ci, co, cs)

        def finish_all():
            for p, ci, co, cs in zip(carry, cins, couts, csems):
                p.finish(ci, co, cs)

        if len(grid) == 1:
            relay_now = pl.program_id(0) == min(int(RELAY_AT * grid[0]), grid[0] - 1)
        else:
            relay_now = last
        start_all() if not grid else pl.when(first)(start_all)
        relay_all() if not grid else pl.when(relay_now)(relay_all)
        body(*refs[:n_in], *outs, *scr)
        finish_all() if not grid else pl.when(last)(finish_all)

    c_in = [a for p in carry for a in p.ins]
    c_out = [s for p in carry for s in p.out_shape]
    c_sems = [s for p in carry for s in p.sems]
    res = _pcall(wrapped, n_prefetch=n_prefetch, out_shape=outs_shape + tuple(c_out),
                 in_specs=in_specs + _hbm_specs(len(c_in)), out_specs=out_specs + _hbm_specs(len(c_out)),
                 scratch_shapes=scratch + c_sems, **kw)(*args, *c_in)
    own = res[0] if single else tuple(res[:n_out])
    landed, k = [], n_out
    for p in carry:
        landed.append(list(res[k:k + len(p.out_shape)]))
        k += len(p.out_shape)
    return own, landed


def _hbm_specs(n):
    return [pl.BlockSpec(memory_space=pl.ANY)] * n


def _hosted(body, carry, **kw):
    if carry:
        return _pcall(body, carry=carry, **kw)
    call = _pcall(body, **kw)
    return lambda *args: (call(*args), [])


def _run_exchanges(parts, name):
    def body(*refs):
        pass

    _, landed = _pcall(body, carry=parts, out_shape=(), in_specs=[], out_specs=[], name=name)()
    return landed


def _arb(n):
    return pltpu.CompilerParams(dimension_semantics=("arbitrary",) * n)


def _tile(n, target, mult):
    best = None
    for t in range(mult, min(n, target) + 1, mult):
        if n % t == 0:
            best = t
    return n if best is None else best


def _round_up(n, m):
    return (n + m - 1) // m * m


def _dot(a, b):
    return jnp.dot(a, b, preferred_element_type=F32)


def _dot_nt(a, b):
    return lax.dot_general(a, b, (((1,), (1,)), ((), ())), preferred_element_type=F32)


def _dot_tn(a, b):
    return lax.dot_general(a, b, (((0,), (0,)), ((), ())), preferred_element_type=F32)


def _rstd(x):
    return lax.rsqrt(jnp.mean(x * x, axis=-1, keepdims=True) + EPS)


def _rms_bwd(dy, x, r, g):
    gdy = dy * g
    proj = jnp.sum(gdy * x, axis=-1, keepdims=True) * (1.0 / x.shape[-1])
    dx = r * gdy - x * (r * r * r) * proj
    dg = jnp.sum(dy * (x * r), axis=0, keepdims=True)
    return dx, dg


_GELU_C = 0.7978845608028654
_GELU_A = 0.044715


def _gelu(x):
    t = jnp.tanh(_GELU_C * (x + _GELU_A * x * x * x))
    return 0.5 * x * (1.0 + t), t


def _gelu_grad(x, t):
    return 0.5 * (1.0 + t) + 0.5 * x * (1.0 - t * t) * (_GELU_C * (1.0 + 3.0 * _GELU_A * x * x))


def _sigmoid(x):
    return 1.0 / (1.0 + jnp.exp(-x))


def _softmax(s):
    m = jnp.max(s, axis=-1, keepdims=True)
    e = jnp.exp(s - m)
    return e / jnp.sum(e, axis=-1, keepdims=True)


def _adamw(w, g, m, v):
    m = ADAM_B1 * m + (1.0 - ADAM_B1) * g
    v = ADAM_B2 * v + (1.0 - ADAM_B2) * (g * g)
    m_hat = m / (1.0 - ADAM_B1 ** ADAM_STEP)
    v_hat = v / (1.0 - ADAM_B2 ** ADAM_STEP)
    delta = -ADAM_LR * (m_hat / (jnp.sqrt(v_hat) + ADAM_EPS) + ADAM_WD * w)
    return delta, m, v


def _tril_mask():
    t = lax.broadcasted_iota(jnp.int32, (CHUNK, CHUNK), 0)
    s = lax.broadcasted_iota(jnp.int32, (CHUNK, CHUNK), 1)
    return (s <= t).astype(F32)


def _sgu_forward(ha, lng, lnb, wm, bt, mixed_s):
    aw = ha.shape[1] // 2
    hd = aw // HEADS
    a, th = _gelu(ha)
    u = a[:, :aw]
    v = a[:, aw:]
    mu = jnp.mean(v, axis=-1, keepdims=True)
    vc = v - mu
    rl = lax.rsqrt(jnp.mean(vc * vc, axis=-1, keepdims=True) + EPS)
    xhat = vc * rl
    vln = (xhat * lng + lnb).astype(BF16)
    for n in range(ha.shape[0] // CHUNK):
        rows = slice(n * CHUNK, (n + 1) * CHUNK)
        for h in range(HEADS):
            cols = slice(h * hd, (h + 1) * hd)
            mixed_s[rows, cols] = _dot(wm[h], vln[rows, cols]) + bt[:, h:h + 1]
    return th, u, xhat, rl, vln


def _conv_taps(zext):
    return pltpu.roll(zext, 2, 0), pltpu.roll(zext, 1, 0)


def _kv_forward(mem, g_mem, w_kv):
    ml, d = mem.shape
    xd = w_kv.shape[2]

    def body(mem_ref, g_ref, w_ref, memn_ref, kv_ref):
        x = mem_ref[...]
        memn = (x * _rstd(x) * g_ref[...]).astype(BF16)
        memn_ref[...] = memn
        for j in range(2 * HEADS):
            kv_ref[j] = _dot(memn, w_ref[j]).astype(BF16)

    return _pcall(body, out_shape=(SDS((ml, d), BF16), SDS((2 * HEADS, ml, xd), BF16)), name="kv_forward")(mem, g_mem, w_kv)


def _in_forward(x, g, w_in_t, tm, carry=()):
    s, d = x.shape
    n_in = w_in_t.shape[0]

    def body(x_ref, g_ref, w_ref, xn_ref, h_ref):
        xv = x_ref[...]
        xn = (xv * _rstd(xv) * g_ref[...]).astype(BF16)
        xn_ref[...] = xn
        h_ref[...] = _dot_nt(xn, w_ref[...])

    return _hosted(
        body, carry, grid=(s // tm,),
        in_specs=[pl.BlockSpec((tm, d), lambda i: (i, 0)), pl.BlockSpec((1, d), lambda i: (0, 0)),
                  pl.BlockSpec((n_in, d), lambda i: (0, 0))],
        out_specs=[pl.BlockSpec((tm, d), lambda i: (i, 0)), pl.BlockSpec((tm, n_in), lambda i: (i, 0))],
        out_shape=(SDS((s, d), BF16), SDS((s, n_in), F32)),
        compiler_params=_arb(1), name="in_forward")(x, g, w_in_t)


def _mix_forward(h, x, lng, lnb, w_sp, bt, conv_w, ga, gb, w_out, tm, carry=()):
    s, d = x.shape
    n_in = h.shape[1]
    aw = lng.shape[1]
    bw = d - aw
    in_a = 2 * aw
    hb_blocks = tm // HALO

    def body(h_ref, hprev_ref, x_ref, lng_ref, lnb_ref, wsp_ref, bt_ref, cw_ref, ga_ref, gb_ref, wout_ref,
             ycat_ref, x1_ref, mixed_s):
        i = pl.program_id(0)
        mask = _tril_mask()
        wm = [(wsp_ref[hh] * mask).astype(BF16) for hh in range(HEADS)]
        hv = h_ref[...]
        _, u, _, _, _ = _sgu_forward(hv[:, :in_a], lng_ref[...], lnb_ref[...], wm, bt_ref[...], mixed_s)
        sg = u * mixed_s[...]
        ycat_ref[:, :aw] = (sg * _rstd(sg) * ga_ref[...]).astype(BF16)

        gate_b = hv[:, in_a:in_a + bw]
        z = hv[:, in_a + bw:in_a + 2 * bw] * hv[:, in_a + 2 * bw:]
        hp = hprev_ref[...]
        zp = hp[:, in_a + bw:in_a + 2 * bw] * hp[:, in_a + 2 * bw:]
        zp = jnp.where(i == 0, 0.0, zp)
        zext = jnp.concatenate([zp, z], axis=0)
        z2, z1 = _conv_taps(zext)
        cw = cw_ref[...]
        conv = cw[0:1] * z2[HALO:] + cw[1:2] * z1[HALO:] + cw[2:3] * z
        sc = gate_b * conv
        ycat_ref[:, aw:] = (sc * _rstd(sc) * gb_ref[...]).astype(BF16)
        x1_ref[...] = x_ref[...] + _dot(ycat_ref[...], wout_ref[...])

    full = lambda shape: pl.BlockSpec(shape, lambda i: (0,) * len(shape))
    return _hosted(
        body, carry, grid=(s // tm,),
        in_specs=[pl.BlockSpec((tm, n_in), lambda i: (i, 0)),
                  pl.BlockSpec((HALO, n_in), lambda i: (jnp.maximum(i * hb_blocks - 1, 0), 0)),
                  pl.BlockSpec((tm, d), lambda i: (i, 0)),
                  full((1, aw)), full((1, aw)), full((HEADS, CHUNK, CHUNK)), full((CHUNK, HEADS)),
                  full((3, bw)), full((1, aw)), full((1, bw)), full((d, d))],
        out_specs=[pl.BlockSpec((tm, d), lambda i: (i, 0)), pl.BlockSpec((tm, d), lambda i: (i, 0))],
        out_shape=(SDS((s, d), BF16), SDS((s, d), F32)),
        scratch_shapes=[pltpu.VMEM((tm, aw), F32)],
        compiler_params=_arb(1), name="mix_forward")(h, h, x, lng, lnb, w_sp, bt, conv_w, ga, gb, w_out)


def _attn_forward(x1, g, w_q, kv, w_o, tm, carry=()):
    s, d = x1.shape
    _, ml, xd = kv.shape
    scale = xd ** -0.5

    def body(x1_ref, g_ref, wq_ref, kv_ref, wo_ref, xn_ref, q_ref, o_ref, x2_ref):
        xv = x1_ref[...]
        xn = (xv * _rstd(xv) * g_ref[...]).astype(BF16)
        xn_ref[...] = xn
        q_ref[...] = _dot(xn, wq_ref[...]).astype(BF16)
        for hh in range(HEADS):
            cols = slice(hh * xd, (hh + 1) * xd)
            p = _softmax(_dot_nt(q_ref[:, cols], kv_ref[hh]) * scale)
            o_ref[:, cols] = _dot(p.astype(BF16), kv_ref[HEADS + hh]).astype(BF16)
        x2_ref[...] = xv + _dot(o_ref[...], wo_ref[...])

    tok = pl.BlockSpec((tm, d), lambda i: (i, 0))
    return _hosted(
        body, carry, grid=(s // tm,),
        in_specs=[tok, pl.BlockSpec((1, d), lambda i: (0, 0)), pl.BlockSpec((d, d), lambda i: (0, 0)),
                  pl.BlockSpec((2 * HEADS, ml, xd), lambda i: (0, 0, 0)), pl.BlockSpec((d, d), lambda i: (0, 0))],
        out_specs=[tok, tok, tok, tok],
        out_shape=(SDS((s, d), BF16), SDS((s, d), BF16), SDS((s, d), BF16), SDS((s, d), F32)),
        compiler_params=_arb(1), name="attn_forward")(x1, g, w_q, kv, w_o)


def _ffn_forward(x2, g, w_gu, w_down, tm):
    s, d = x2.shape
    _, nf, tf, _ = w_gu.shape

    def body(x2_ref, g_ref, wgu_ref, wd_ref, xn_ref, gu_ref, x3_ref):
        f = pl.program_id(1)

        @pl.when(f == 0)
        def _():
            xv = x2_ref[...]
            xn_ref[...] = (xv * _rstd(xv) * g_ref[...]).astype(BF16)
            x3_ref[...] = xv

        xn = xn_ref[...]
        gate = _dot_nt(xn, wgu_ref[0])
        up = _dot_nt(xn, wgu_ref[1])
        gu_ref[0] = gate.astype(BF16)
        gu_ref[1] = up.astype(BF16)
        act = (gate * _sigmoid(gate) * up).astype(BF16)
        x3_ref[...] += _dot(act, wd_ref[...])

    tok = pl.BlockSpec((tm, d), lambda i, f: (i, 0))
    return _pcall(
        body, grid=(s // tm, nf),
        in_specs=[tok, pl.BlockSpec((1, d), lambda i, f: (0, 0)),
                  pl.BlockSpec((2, None, tf, d), lambda i, f: (0, f, 0, 0)),
                  pl.BlockSpec((tf, d), lambda i, f: (f, 0))],
        out_specs=[tok, pl.BlockSpec((2, None, tm, tf), lambda i, f: (0, f, i, 0)), tok],
        out_shape=(SDS((s, d), BF16), SDS((2, nf, s, tf), BF16), SDS((s, d), F32)),
        compiler_params=_arb(2), name="ffn_forward")(x2, g, w_gu, w_down)


def _final_backward(x3, target, g_final, tm):
    s, d = x3.shape

    def body(x3_ref, tgt_ref, gf_ref, loss_ref, dgf_ref, dx3_ref, dx3b_ref):
        @pl.when(pl.program_id(0) == 0)
        def _():
            loss_ref[...] = jnp.zeros_like(loss_ref)
            dgf_ref[...] = jnp.zeros_like(dgf_ref)

        xv = x3_ref[...]
        r = _rstd(xv)
        diff = xv * r * gf_ref[...] - tgt_ref[...]
        loss_ref[...] += 0.5 * jnp.sum(jnp.sum(diff * diff, axis=-1, keepdims=True), axis=0, keepdims=True) * (1.0 / d)
        dx3, dgf = _rms_bwd(diff * (1.0 / d), xv, r, gf_ref[...])
        dgf_ref[...] += dgf
        dx3_ref[...] = dx3
        dx3b_ref[...] = dx3.astype(BF16)

    tok = pl.BlockSpec((tm, d), lambda i: (i, 0))
    vec = pl.BlockSpec((1, d), lambda i: (0, 0))
    return _pcall(
        body, grid=(s // tm,), in_specs=[tok, tok, vec],
        out_specs=[pl.BlockSpec((SUB, LANES), lambda i: (0, 0)), vec, tok, tok],
        out_shape=(SDS((SUB, LANES), F32), SDS((1, d), F32), SDS((s, d), F32), SDS((s, d), BF16)),
        compiler_params=_arb(1), name="final_backward")(x3, target, g_final)


def _swiglu_backward(dx3b, gu, w_gu, w_down, tm):
    s, d = dx3b.shape
    _, nf, tf, _ = w_gu.shape

    def body(dx3b_ref, gu_ref, wgu_ref, wd_ref, act_ref, dgu_ref, dxn_ref):
        @pl.when(pl.program_id(1) == 0)
        def _():
            dxn_ref[...] = jnp.zeros_like(dxn_ref)

        for r0 in range(0, tm, ROW_CHUNK):
            rows = slice(r0, r0 + ROW_CHUNK)
            dact = _dot_nt(dx3b_ref[rows, :], wd_ref[...])
            gv = gu_ref[0, rows, :].astype(F32)
            uv = gu_ref[1, rows, :].astype(F32)
            sg = _sigmoid(gv)
            silu = gv * sg
            act_ref[rows, :] = (silu * uv).astype(BF16)
            dgate = (dact * uv * (sg * (1.0 + gv * (1.0 - sg)))).astype(BF16)
            dup = (dact * silu).astype(BF16)
            dgu_ref[0, rows, :] = dgate
            dgu_ref[1, rows, :] = dup
            part = _dot(dgate, wgu_ref[0]) + _dot(dup, wgu_ref[1])
            dxn_ref[rows, :] += part

    tok = pl.BlockSpec((tm, d), lambda i, f: (i, 0))
    pair = pl.BlockSpec((2, None, tm, tf), lambda i, f: (0, f, i, 0))
    return _pcall(
        body, grid=(s // tm, nf),
        in_specs=[tok, pair, pl.BlockSpec((2, None, tf, d), lambda i, f: (0, f, 0, 0)),
                  pl.BlockSpec((tf, d), lambda i, f: (f, 0))],
        out_specs=[pl.BlockSpec((None, tm, tf), lambda i, f: (f, i, 0)), pair, tok],
        out_shape=(SDS((nf, s, tf), BF16), SDS((2, nf, s, tf), BF16), SDS((s, d), F32)),
        compiler_params=_arb(2), name="swiglu_backward")(dx3b, gu, w_gu, w_down)


def _attn_backward(dx3, dxn3, x2, g_ffn, x1, g, q, kv, w_q, w_o, tm, carry=()):
    s, d = x1.shape
    _, ml, xd = kv.shape
    scale = xd ** -0.5

    def body(dx3_ref, dxn3_ref, x2_ref, g2_ref, x1_ref, g_ref, q_ref, kv_ref, wq_ref, wo_ref,
             dx2b_ref, dq_ref, dx1_ref, dx1b_ref, dkv_ref, dg_ref, dg2_ref, do_s):
        i = pl.program_id(0)

        @pl.when(i == 0)
        def _():
            dkv_ref[...] = jnp.zeros_like(dkv_ref)
            dg_ref[...] = jnp.zeros_like(dg_ref)
            dg2_ref[...] = jnp.zeros_like(dg2_ref)

        x2v = x2_ref[...]
        dx2n, dg2 = _rms_bwd(dxn3_ref[...], x2v, _rstd(x2v), g2_ref[...])
        dg2_ref[...] += dg2
        dx2 = dx3_ref[...] + dx2n
        dx2b_ref[...] = dx2.astype(BF16)
        do_s[...] = _dot_nt(dx2b_ref[...], wo_ref[...]).astype(BF16)
        for hh in range(HEADS):
            kc = slice(hh * xd, (hh + 1) * xd)
            qh = q_ref[:, kc]
            kh = kv_ref[hh]
            doh = do_s[:, kc]
            p = _softmax(_dot_nt(qh, kh) * scale)
            dp = _dot_nt(doh, kv_ref[HEADS + hh])
            dkv_ref[HEADS + hh] += _dot_tn(p.astype(BF16), doh)
            ds = (p * (dp - jnp.sum(dp * p, axis=-1, keepdims=True)) * scale).astype(BF16)
            dq_ref[:, kc] = _dot(ds, kh).astype(BF16)
            dkv_ref[hh] += _dot_tn(ds, qh)
        dg_tile = jnp.zeros_like(dg_ref)
        for r0 in range(0, tm, min(ROW_CHUNK, tm)):
            rows = slice(r0, r0 + min(ROW_CHUNK, tm))
            dxn = _dot_nt(dq_ref[rows, :], wq_ref[...])
            xv = x1_ref[rows, :]
            dx, dg = _rms_bwd(dxn, xv, _rstd(xv), g_ref[...])
            dg_tile = dg_tile + dg
            dx1 = dx2[rows] + dx
            dx1_ref[rows, :] = dx1
            dx1b_ref[rows, :] = dx1.astype(BF16)
        dg_ref[...] += dg_tile

    tok = pl.BlockSpec((tm, d), lambda i: (i, 0))
    vec = pl.BlockSpec((1, d), lambda i: (0, 0))
    sq = pl.BlockSpec((d, d), lambda i: (0, 0))
    kvs = pl.BlockSpec((2 * HEADS, ml, xd), lambda i: (0, 0, 0))
    return _hosted(
        body, carry, grid=(s // tm,),
        in_specs=[tok, tok, tok, vec, tok, vec, tok, kvs, sq, sq],
        out_specs=[tok, tok, tok, tok, kvs, vec, vec],
        out_shape=(SDS((s, d), BF16), SDS((s, d), BF16), SDS((s, d), F32), SDS((s, d), BF16),
                   SDS((2 * HEADS, ml, xd), F32), SDS((1, d), F32), SDS((1, d), F32)),
        scratch_shapes=[pltpu.VMEM((tm, d), BF16)],
        compiler_params=_arb(1), name="attn_backward")(dx3, dxn3, x2, g_ffn, x1, g, q, kv, w_q, w_o)


def _kv_backward(dkv, memn, mem, g_mem, w_kv):
    ml, d = mem.shape
    xd = w_kv.shape[2]

    def body(dkv_ref, memn_ref, mem_ref, g_ref, w_ref, dw_ref, dg_ref):
        dmemn = jnp.zeros((ml, d), F32)
        for j in range(2 * HEADS):
            dkvb = dkv_ref[j].astype(BF16)
            dw_ref[j] = _dot_tn(memn_ref[...], dkvb)
            dmemn = dmemn + _dot_nt(dkvb, w_ref[j])
        x = mem_ref[...]
        dg_ref[...] = jnp.sum(dmemn * (x * _rstd(x)), axis=0, keepdims=True)

    return _pcall(body, out_shape=(SDS((2 * HEADS, d, xd), F32), SDS((1, d), F32)), name="kv_backward")(dkv, memn, mem, g_mem, w_kv)


def _mix_backward(dx1, x, g_mix, h, lng, lnb, w_sp, bt, conv_w, ga, gb, w_out, w_in, tm, carry=()):
    s, d = x.shape
    n_in = h.shape[1]
    aw = lng.shape[1]
    bw = d - aw
    hd = aw // HEADS
    in_a = 2 * aw
    hb_blocks = tm // HALO
    last_blk = s // HALO - 1
    nt = s // tm
    te = tm + HALO
    tee = tm + 2 * HALO

    def body(dx1_ref, dx1n_ref, x_ref, gm_ref, h_ref, hp_ref, hn_ref, lng_ref, lnb_ref, wsp_ref, bt_ref, cw_ref,
             ga_ref, gb_ref, wout_ref, win_ref,
             dh_ref, dx_ref, dga_ref, dgb_ref, dcw_ref, dlng_ref, dlnb_ref, dwsp_ref, dbs_ref, dgm_ref,
             mixed_s, dvln_s):
        i = pl.program_id(0)

        @pl.when(i == 0)
        def _():
            for ref in (dga_ref, dgb_ref, dcw_ref, dlng_ref, dlnb_ref, dwsp_ref, dbs_ref, dgm_ref):
                ref[...] = jnp.zeros_like(ref)

        mask = _tril_mask()
        wm = [(wsp_ref[hh] * mask).astype(BF16) for hh in range(HEADS)]
        hv = h_ref[...]
        dx1 = dx1_ref[...]
        dx1e = jnp.concatenate([dx1, dx1n_ref[...]], axis=0).astype(BF16)
        dycat = _dot_nt(dx1e, wout_ref[...])

        hbe = jnp.concatenate([hp_ref[:, in_a:], hv[:, in_a:], hn_ref[:, in_a:]], axis=0)
        row = lax.broadcasted_iota(jnp.int32, (tee, 1), 0)
        zext = hbe[:, bw:2 * bw] * hbe[:, 2 * bw:]
        zext = jnp.where((i == 0) & (row < HALO), 0.0, zext)
        z2e, z1e = _conv_taps(zext)
        cw = cw_ref[...]
        conv_e = (cw[0:1] * z2e + cw[1:2] * z1e + cw[2:3] * zext)[HALO:]
        gate_b_e = hbe[HALO:, :bw]
        sc_e = gate_b_e * conv_e
        rb = _rstd(sc_e)
        dyb = dycat[:, aw:]
        gdy = dyb * gb_ref[...]
        dsc_e = rb * gdy - sc_e * (rb * rb * rb) * (jnp.sum(gdy * sc_e, axis=-1, keepdims=True) * (1.0 / bw))
        dgb_ref[...] += jnp.sum((dyb * (sc_e * rb))[:tm], axis=0, keepdims=True)
        dconv_e = dsc_e * gate_b_e
        dconv_e = jnp.where((i == nt - 1) & (row[:te] >= tm), 0.0, dconv_e)
        dconv = dconv_e[:tm]
        dc1 = pltpu.roll(dconv_e, te - 1, 0)[:tm]
        dc2 = pltpu.roll(dconv_e, te - 2, 0)[:tm]
        dz = cw[2:3] * dconv + cw[1:2] * dc1 + cw[0:1] * dc2
        z = zext[HALO:HALO + tm]
        z1 = z1e[HALO:HALO + tm]
        z2 = z2e[HALO:HALO + tm]
        dcw_ref[0:1, :] += jnp.sum(dconv * z2, axis=0, keepdims=True)
        dcw_ref[1:2, :] += jnp.sum(dconv * z1, axis=0, keepdims=True)
        dcw_ref[2:3, :] += jnp.sum(dconv * z, axis=0, keepdims=True)
        dh_ref[:, in_a:in_a + bw] = (dsc_e[:tm] * conv_e[:tm]).astype(BF16)
        dh_ref[:, in_a + bw:in_a + 2 * bw] = (dz * hv[:, in_a + 2 * bw:]).astype(BF16)
        dh_ref[:, in_a + 2 * bw:] = (dz * hv[:, in_a + bw:in_a + 2 * bw]).astype(BF16)

        ha = hv[:, :in_a]
        th, u, xhat, rl, vln = _sgu_forward(ha, lng_ref[...], lnb_ref[...], wm, bt_ref[...], mixed_s)
        mixed = mixed_s[...]
        sg = u * mixed
        dsg, dga = _rms_bwd(dycat[:tm, :aw], sg, _rstd(sg), ga_ref[...])
        dga_ref[...] += dga
        du = dsg * mixed
        dmixed = dsg * u
        dmb = dmixed.astype(BF16)
        for n in range(tm // CHUNK):
            rows = slice(n * CHUNK, (n + 1) * CHUNK)
            dbs_ref[...] += dmixed[rows]
            for hh in range(HEADS):
                cols = slice(hh * hd, (hh + 1) * hd)
                dvln_s[rows, cols] = _dot_tn(wm[hh], dmb[rows, cols])
                dwsp_ref[hh] += mask * _dot_nt(dmb[rows, cols], vln[rows, cols])
        dvln = dvln_s[...]
        dlng_ref[...] += jnp.sum(dvln * xhat, axis=0, keepdims=True)
        dlnb_ref[...] += jnp.sum(dvln, axis=0, keepdims=True)
        dxh = dvln * lng_ref[...]
        dv = rl * (dxh - jnp.mean(dxh, axis=-1, keepdims=True) - xhat * jnp.mean(dxh * xhat, axis=-1, keepdims=True))
        dh_ref[:, :in_a] = (jnp.concatenate([du, dv], axis=-1) * _gelu_grad(ha, th)).astype(BF16)

        dgm_tile = jnp.zeros_like(dgm_ref)
        for r0 in range(0, tm, min(ROW_CHUNK, tm)):
            rows = slice(r0, r0 + min(ROW_CHUNK, tm))
            dxn = _dot(dh_ref[rows, :], win_ref[...])
            xv = x_ref[rows, :]
            dx, dgm = _rms_bwd(dxn, xv, _rstd(xv), gm_ref[...])
            dgm_tile = dgm_tile + dgm
            dx_ref[rows, :] = dx1_ref[rows, :] + dx
        dgm_ref[...] += dgm_tile

    full = lambda shape: pl.BlockSpec(shape, lambda i: (0,) * len(shape))
    tok = pl.BlockSpec((tm, d), lambda i: (i, 0))
    nxt = lambda i: (jnp.minimum((i + 1) * hb_blocks, last_blk), 0)
    prv = lambda i: (jnp.maximum(i * hb_blocks - 1, 0), 0)
    return _hosted(
        body, carry, grid=(nt,),
        in_specs=[tok, pl.BlockSpec((HALO, d), nxt), tok, full((1, d)),
                  pl.BlockSpec((tm, n_in), lambda i: (i, 0)), pl.BlockSpec((HALO, n_in), prv), pl.BlockSpec((HALO, n_in), nxt),
                  full((1, aw)), full((1, aw)), full((HEADS, CHUNK, CHUNK)), full((CHUNK, HEADS)), full((3, bw)),
                  full((1, aw)), full((1, bw)), full((d, d)), full((n_in, d))],
        out_specs=[pl.BlockSpec((tm, n_in), lambda i: (i, 0)), tok,
                   full((1, aw)), full((1, bw)), full((SUB, bw)), full((1, aw)), full((1, aw)),
                   full((HEADS, CHUNK, CHUNK)), full((CHUNK, aw)), full((1, d))],
        out_shape=(SDS((s, n_in), BF16), SDS((s, d), F32),
                   SDS((1, aw), F32), SDS((1, bw), F32), SDS((SUB, bw), F32), SDS((1, aw), F32), SDS((1, aw), F32),
                   SDS((HEADS, CHUNK, CHUNK), F32), SDS((CHUNK, aw), F32), SDS((1, d), F32)),
        scratch_shapes=[pltpu.VMEM((tm, aw), F32), pltpu.VMEM((tm, aw), F32)],
        compiler_params=_arb(1), name="mix_backward")(dx1, dx1, x, g_mix, h, h, h, lng, lnb, w_sp, bt, conv_w, ga, gb, w_out, w_in)


def _bias_grad(dbs):
    aw = dbs.shape[1]
    hd = aw // HEADS

    def body(dbs_ref, out_ref):
        ones = jnp.ones((SUB, hd), F32)
        for hh in range(HEADS):
            r = lax.dot_general(ones, dbs_ref[:, hh * hd:(hh + 1) * hd], (((1,), (1,)), ((), ())),
                                precision=lax.Precision.HIGHEST, preferred_element_type=F32)
            out_ref[hh:hh + 1, :] = r[0:1]

    return _pcall(body, out_shape=SDS((HEADS, CHUNK), F32), name="bias_grad")(dbs)


def _wgrad_body(a_ref, b_ref, o_ref):
    o_ref[...] = _dot_tn(a_ref[...], b_ref[...])


def _wgrad(a, b, name, carry=()):
    k, m = a.shape
    n = b.shape[1]
    tm = _tile(m, 512, LANES)
    tn = _tile(n, 1024, LANES)
    return _hosted(
        functools.partial(_wgrad_body), carry, grid=(m // tm, n // tn),
        in_specs=[pl.BlockSpec((k, tm), lambda i, j: (0, i)), pl.BlockSpec((k, tn), lambda i, j: (0, j))],
        out_specs=pl.BlockSpec((tm, tn), lambda i, j: (i, j)),
        out_shape=SDS((m, n), F32), compiler_params=_arb(2), name=name)(a, b)


def _wgrad_blocked_lhs(a, b, name, carry=()):
    nb, k, t = a.shape
    n = b.shape[1]
    tn = _tile(n, 1024, LANES)
    return _hosted(
        functools.partial(_wgrad_body), carry, grid=(nb, n // tn),
        in_specs=[pl.BlockSpec((None, k, t), lambda i, j: (i, 0, 0)), pl.BlockSpec((k, tn), lambda i, j: (0, j))],
        out_specs=pl.BlockSpec((t, tn), lambda i, j: (i, j)),
        out_shape=SDS((nb * t, n), F32), compiler_params=_arb(2), name=name)(a, b)


def _wgrad_blocked_rhs(a, b, name, carry=()):
    k, m = a.shape
    nb, _, t = b.shape
    tm = _tile(m, 512, LANES)
    return _hosted(
        functools.partial(_wgrad_body), carry, grid=(m // tm, nb),
        in_specs=[pl.BlockSpec((k, tm), lambda i, j: (0, i)), pl.BlockSpec((None, k, t), lambda i, j: (j, 0, 0))],
        out_specs=pl.BlockSpec((None, tm, t), lambda i, j: (j, i, 0)),
        out_shape=SDS((nb, m, t), F32), compiler_params=_arb(2), name=name)(a, b)


def _unblock_cols(wb, name, carry=()):
    nb, r, t = wb.shape
    tr = _tile(r, 256, 16)

    def body(w_ref, o_ref):
        o_ref[...] = jnp.concatenate([w_ref[j].astype(F32) for j in range(nb)], axis=-1).astype(o_ref.dtype)

    return _hosted(
        body, carry, grid=(r // tr,),
        in_specs=[pl.BlockSpec((nb, tr, t), lambda i: (0, i, 0))], out_specs=pl.BlockSpec((tr, nb * t), lambda i: (i, 0)),
        out_shape=SDS((r, nb * t), wb.dtype), compiler_params=_arb(1), name=name)(wb)


def _block_cols(w, nb, name, carry=()):
    r, n = w.shape
    t = n // nb
    tr = _tile(r, 256, 16)

    def body(w_ref, o_ref):
        wv = w_ref[...]
        for j in range(nb):
            o_ref[j] = wv[:, j * t:(j + 1) * t]

    return _hosted(
        body, carry, grid=(r // tr,),
        in_specs=[pl.BlockSpec((tr, n), lambda i: (i, 0))], out_specs=pl.BlockSpec((nb, tr, t), lambda i: (0, i, 0)),
        out_shape=SDS((nb, r, t), w.dtype), compiler_params=_arb(1), name=name)(w)


def _place():
    x, y, c = lax.axis_index("x"), lax.axis_index("y"), lax.axis_index("c")
    return x, y, c, [(1 - x, y), (x, 1 - y), (1 - x, 1 - y)]


def _all_gather(shards):
    n = len(shards)
    slots = 9
    cut = [(s.shape[0] // 32) * 16 for s in shards]

    def build(ins, outs, sems):
        send_sems, recv_sems, local_sems = sems
        x, y, c, _ = _place()
        me, sib, xn, yn, dg = (x, y, c), (x, y, 1 - c), (1 - x, y, c), (x, 1 - y, c), (1 - x, 1 - y, c)
        other = lambda p: (p[0], p[1], 1 - p[2])

        def rows(a, p, part=None):
            ref = outs[a].at[4 * p[0] + 2 * p[1] + p[2]]
            if part is None or cut[a] == 0:
                return ref if part in (None, 0) else None
            return ref.at[pl.ds(0, cut[a])] if part == 0 else ref.at[pl.ds(cut[a], shards[a].shape[0] - cut[a])]

        def copy(a, k, ref, to, src=None):
            if ref is None:
                return None
            return pltpu.make_async_remote_copy(
                src_ref=ref if src is None else src, dst_ref=ref, send_sem=send_sems.at[slots * a + k],
                recv_sem=recv_sems.at[slots * a + k], device_id=to, device_id_type=MESH)

        def real(cps):
            return [cp for cp in cps if cp is not None]

        class Copies:
            own = lambda a: [copy(a, 1, rows(a, me), xn, ins[a]), copy(a, 2, rows(a, me), yn, ins[a]),
                             copy(a, 0, rows(a, me), sib, ins[a])]
            local = lambda a: pltpu.make_async_copy(ins[a], rows(a, me), local_sems.at[a])
            from_x = lambda a: copy(a, 1, rows(a, xn), me)
            from_y = lambda a: copy(a, 2, rows(a, yn), me)
            after_x = lambda a: real([copy(a, 4, rows(a, xn, 1), yn), copy(a, 5, rows(a, xn), sib)])
            after_y = lambda a: real([copy(a, 3, rows(a, yn, 0), xn), copy(a, 6, rows(a, yn), sib)])
            diag_in = lambda a: real([copy(a, 3, rows(a, dg, 0), me), copy(a, 4, rows(a, dg, 1), me)])
            diag_on = lambda a: real([copy(a, 7, rows(a, dg, 0), sib), copy(a, 8, rows(a, dg, 1), sib)])
            from_sib = lambda a: real([copy(a, 0, rows(a, sib), me), copy(a, 5, rows(a, other(xn)), me),
                                       copy(a, 6, rows(a, other(yn)), me), copy(a, 7, rows(a, other(dg), 0), me),
                                       copy(a, 8, rows(a, other(dg), 1), me)])

        return Copies

    def start(ins, outs, sems):
        cps = build(ins, outs, sems)
        for a in range(n):
            for cp in cps.own(a):
                cp.start()
        for a in range(n):
            cps.local(a).start()

    def relay(ins, outs, sems):
        cps = build(ins, outs, sems)
        for a in range(n):
            cps.from_x(a).wait_recv()
            for cp in cps.after_x(a):
                cp.start()
            cps.from_y(a).wait_recv()
            for cp in cps.after_y(a):
                cp.start()

    def finish(ins, outs, sems):
        cps = build(ins, outs, sems)
        for a in range(n):
            for arrived, onward in zip(cps.diag_in(a), cps.diag_on(a)):
                arrived.wait_recv()
                onward.start()
        for a in range(n):
            for cp in cps.from_sib(a):
                cp.wait_recv()
            for cp in cps.own(a) + cps.after_x(a) + cps.after_y(a) + cps.diag_on(a):
                cp.wait_send()
            cps.local(a).wait()

    return _Exchange(shards, [SDS((N_DEV,) + s.shape, s.dtype) for s in shards],
                     [pltpu.SemaphoreType.DMA((slots * n,)), pltpu.SemaphoreType.DMA((slots * n,)),
                      pltpu.SemaphoreType.DMA((n,))], start, finish, relay)


def _swap_exchange(ins, out_shape, per, copies):
    def start(i, o, sems):
        for cp in copies(i, o, sems):
            cp.start()

    def finish(i, o, sems):
        for cp in copies(i, o, sems):
            cp.wait()

    n = per * len(ins)
    return _Exchange(ins, out_shape, [pltpu.SemaphoreType.DMA((n,)), pltpu.SemaphoreType.DMA((n,))], start, finish)


def _exchange_c(gs):
    def copies(ins, outs, sems):
        x, y, c, _ = _place()
        return [pltpu.make_async_remote_copy(
                    src_ref=ins[a].at[2 * k + 1 - c], dst_ref=outs[a].at[k],
                    send_sem=sems[0].at[4 * a + k], recv_sem=sems[1].at[4 * a + k],
                    device_id=(x, y, 1 - c), device_id_type=MESH)
                for a in range(len(gs)) for k in range(4)]

    return _swap_exchange(gs, [SDS((4,) + g.shape[1:], g.dtype) for g in gs], 4, copies)


def _exchange_xy(sends):
    def copies(ins, outs, sems):
        x, y, c, chips = _place()
        return [pltpu.make_async_remote_copy(
                    src_ref=ins[a].at[t], dst_ref=outs[a].at[t],
                    send_sem=sems[0].at[3 * a + t], recv_sem=sems[1].at[3 * a + t],
                    device_id=(*chips[t], c), device_id_type=MESH)
                for a in range(len(sends)) for t in range(3)]

    return _swap_exchange(sends, [SDS(s.shape, s.dtype) for s in sends], 3, copies)


def _rs_combine(g, recv, pos, name, carry=()):
    _, r, cdim = g.shape
    tr = _tile(r, 256, 16)

    def body(pos_ref, g0, r0, g1, r1, g2, r2, g3, r3, keep_ref, send_ref):
        keep_ref[...] = g0[...] + r0[...]
        send_ref[0] = (g1[...] + r1[...]).astype(BF16)
        send_ref[1] = (g2[...] + r2[...]).astype(BF16)
        send_ref[2] = (g3[...] + r3[...]).astype(BF16)

    def k_of(p, t):
        px = p[0] if t in (0, 2) else 1 - p[0]
        py = p[1] if t in (0, 1) else 1 - p[1]
        return 2 * px + py

    blk = (None, tr, cdim)
    in_specs = []
    for t in range(4):
        in_specs.append(pl.BlockSpec(blk, functools.partial(lambda j, p, t: (2 * k_of(p, t) + p[2], j, 0), t=t)))
        in_specs.append(pl.BlockSpec(blk, functools.partial(lambda j, p, t: (k_of(p, t), j, 0), t=t)))
    return _hosted(
        body, carry, n_prefetch=1, out_shape=(SDS((r, cdim), F32), SDS((3, r, cdim), BF16)),
        grid=(r // tr,), in_specs=in_specs,
        out_specs=[pl.BlockSpec((tr, cdim), lambda j, p: (j, 0)), pl.BlockSpec((3, tr, cdim), lambda j, p: (0, j, 0))],
        compiler_params=_arb(1), name=name)(pos, g, recv, g, recv, g, recv, g, recv)


def _adamw_shard(keep, recv, w, m, v, name):
    r, cdim = w.shape
    tr = _tile(r, 256, 16)

    def body(k_ref, r_ref, w_ref, m_ref, v_ref, g_ref, d_ref, nm_ref, nv_ref):
        g = ((k_ref[...] + r_ref[0].astype(F32)) + r_ref[1].astype(F32)) + r_ref[2].astype(F32)
        g_ref[...] = g
        d_ref[...], nm_ref[...], nv_ref[...] = _adamw(w_ref[...], g, m_ref[...], v_ref[...])

    blk = pl.BlockSpec((tr, cdim), lambda j: (j, 0))
    out = SDS((r, cdim), F32)
    return _pcall(body, grid=(r // tr,), in_specs=[blk, pl.BlockSpec((3, tr, cdim), lambda j: (0, j, 0)), blk, blk, blk],
                  out_specs=[blk] * 4, out_shape=(out,) * 4, compiler_params=_arb(1), name=name)(keep, recv, w, m, v)


_HBM = pl.BlockSpec(memory_space=pltpu.HBM)
_SEM = pl.BlockSpec(memory_space=pltpu.SEMAPHORE)
_SPLIT = pltpu.CompilerParams(has_side_effects=pltpu.SideEffectType.DATAFLOW_SIDE_EFFECTING)


def _xy_copies(n, refs):
    srcs, lands, (send_sems, recv_sems) = refs[:n], refs[n:2 * n], refs[2 * n:2 * n + 2]
    x, y, c, chips = _place()
    return [pltpu.make_async_remote_copy(src_ref=srcs[a].at[t], dst_ref=lands[a].at[t], send_sem=send_sems.at[3 * a + t],
                                         recv_sem=recv_sems.at[3 * a + t], device_id=(*chips[t], c), device_id_type=MESH)
            for a in range(n) for t in range(3)]


def _exchange_xy_start(sends, name):
    n = len(sends)

    def body(*refs):
        for cp in _xy_copies(n, refs):
            cp.start()
        refs[-1][...] = jnp.zeros_like(refs[-1])

    hbm = lambda a: pltpu.with_memory_space_constraint(a, pltpu.HBM)
    bufs = [pltpu.HBM(s.shape, s.dtype) for s in sends]
    sems = pltpu.SemaphoreType.DMA((3 * n,))
    res = _pcall(
        body, name=name, out_shape=(sems, sems, *bufs, *bufs, SDS((SUB, LANES), F32)),
        in_specs=[_HBM] * (2 * n), out_specs=[_SEM, _SEM] + [_HBM] * (2 * n) + [pl.BlockSpec(memory_space=pltpu.VMEM)],
        input_output_aliases={k: 2 + k for k in range(2 * n)}, compiler_params=_SPLIT)(
            *[hbm(s) for s in sends], *[hbm(lax.empty(s.shape, s.dtype)) for s in sends])
    return (n, res[:-1]), res[-1]


def _exchange_xy_wait(started, after, name):
    n, (send_sems, recv_sems, *bufs) = started

    def body(*refs):
        for cp in _xy_copies(n, refs):
            cp.wait_send()
            cp.wait_recv()

    shapes = [pltpu.HBM(b.shape, b.dtype) for b in bufs]
    res = _pcall(
        body, name=name, out_shape=tuple(shapes),
        in_specs=[_HBM] * (2 * n) + [_SEM, _SEM, pl.BlockSpec(memory_space=pl.ANY)], out_specs=[_HBM] * (2 * n),
        input_output_aliases={k: k for k in range(2 * n)}, compiler_params=_SPLIT)(*bufs, send_sems, recv_sems, after)
    return list(res[n:])


def _follow(token):
    nothing = lambda ins, outs, sems: None
    return _Exchange([token], [], [], nothing, nothing)


def _adamw_small(gathered, seg, params, conv_rows):
    names = list(params)
    c0, cn = conv_rows

    def body(*refs):
        gat_ref = refs[0]
        ins = refs[1:1 + 3 * len(names)]
        outs = refs[1 + 3 * len(names):]

        def total(r0, rn):
            tot = gat_ref[0, r0:r0 + rn, :]
            for dev in range(1, N_DEV):
                tot = tot + gat_ref[dev, r0:r0 + rn, :]
            return tot

        for k, nm in enumerate(names):
            g = total(*seg[nm])
            w_ref, m_ref, v_ref = ins[3 * k:3 * k + 3]
            g_ref, d_ref, nm_ref, nv_ref = outs[4 * k:4 * k + 4]
            g_ref[...] = g
            d_ref[...], nm_ref[...], nv_ref[...] = _adamw(w_ref[...], g, m_ref[...], v_ref[...])
        outs[-2][...] = total(c0, cn)
        outs[-1][...] = total(*seg["loss"])

    flat_in = [a for nm in names for a in params[nm]]
    out_shape = []
    for nm in names:
        out_shape += [SDS(params[nm][0].shape, F32)] * 4
    out_shape += [SDS((cn, LANES), F32), SDS((seg["loss"][1], LANES), F32)]
    res = _pcall(body, out_shape=tuple(out_shape), name="adamw_small")(gathered, *flat_in)
    per = {nm: res[4 * k:4 * k + 4] for k, nm in enumerate(names)}
    return per, res[-2], res[-1]


def _adamw_one(w, g, m, v, name):
    def body(w_ref, g_ref, m_ref, v_ref, d_ref, nm_ref, nv_ref):
        d_ref[...], nm_ref[...], nv_ref[...] = _adamw(w_ref[...], g_ref[...], m_ref[...], v_ref[...])

    return _pcall(body, out_shape=(SDS(w.shape, F32),) * 3, name=name)(w, g, m, v)


def _rows128(a):
    return a.reshape(-1, LANES)


def _pack_small(gs, loss_tile):
    seg, pieces, row = {}, [], 0
    for nm in SMALL + ("conv_w", "loss"):
        piece = loss_tile if nm == "loss" else _rows128(gs[nm])
        rn = _round_up(piece.shape[0], SUB)
        pieces.append(jnp.pad(piece, ((0, rn - piece.shape[0]), (0, 0))))
        seg[nm] = (row, piece.shape[0])
        row += rn
    return jnp.concatenate(pieces, axis=0), seg


def _step(x, mem, target, wb, conv_w, sp, pos):
    s, d = x.shape
    tm = min(TOKEN_TILE, s)
    tm_wide = min(2 * TOKEN_TILE, s)
    rows = lambda w8: w8.reshape(-1, w8.shape[2])
    shards = lambda g: g.reshape((N_DEV, -1) + g.shape[1:])
    bt = sp["b_spatial"].T

    (w_in8, conv8), = _run_exchanges([_all_gather([wb["w_in"], conv_w])], "gather_w_in")
    conv_full = conv8.transpose(1, 0, 2).reshape(3, -1)
    w_in_t = rows(w_in8)
    (xn1, h), ((w_out8, w_kv8, w_q8),) = _in_forward(
        x, sp["ln_mix_g"], w_in_t, tm, carry=[_all_gather([wb["w_out"], wb["w_kv"], wb["w_q"]])])
    w_out = rows(w_out8)
    (ycat, x1), ((w_o8, w_down8),) = _mix_forward(
        h, x, sp["sgu_ln_g"], sp["sgu_ln_b"], sp["w_spatial"], bt, conv_full, sp["grp_norm_a"], sp["grp_norm_b"], w_out, tm,
        carry=[_all_gather([wb["w_o"], wb["w_down"]])])
    w_q, w_o, w_down = rows(w_q8), rows(w_o8), rows(w_down8)
    memn, kv = _kv_forward(mem, sp["ln_mem_g"], w_kv8)
    (xn2, q, o, x2), ((w_gu8,),) = _attn_forward(
        x1, sp["ln_attn_g"], w_q, kv, w_o, tm, carry=[_all_gather([wb["w_gate_up"]])])
    w_gu = w_gu8.reshape((2, N_DEV // 2) + w_gu8.shape[1:])
    xn3, gu, x3 = _ffn_forward(x2, sp["ln_ffn_g"], w_gu, w_down, tm_wide)

    loss, d_lnf, dx3, dx3b = _final_backward(x3, target, sp["ln_final_g"], tm_wide)
    act, dgu, dxn3 = _swiglu_backward(dx3b, gu, w_gu, w_down, tm_wide)
    g_gu, _ = _wgrad_blocked_lhs(dgu.reshape((N_DEV,) + dgu.shape[2:]), xn3, "wgrad_gate_up")
    g_gu = shards(g_gu)
    g_down, ((rc_gu,),) = _wgrad_blocked_lhs(act, dx3b, "wgrad_down", carry=[_exchange_c([g_gu])])
    g_down = shards(g_down)
    keep, pending = {}, []
    (keep["w_gate_up"], send_gu), _ = _rs_combine(g_gu, rc_gu, pos, "rs_combine_w_gate_up")
    started, token = _exchange_xy_start([send_gu], "exchange_xy_1_start")
    pending.append((("w_gate_up",), started))
    (dx2b, dq, dx1, dx1b, dkv, d_lnattn, d_lnffn), (_, (rc_down,)) = _attn_backward(
        dx3, dxn3, x2, sp["ln_ffn_g"], x1, sp["ln_attn_g"], q, kv, w_q, w_o, tm,
        carry=[_follow(token), _exchange_c([g_down])])
    (keep["w_down"], send_down), _ = _rs_combine(g_down, rc_down, pos, "rs_combine_w_down")
    g_o, _ = _wgrad(o, dx2b, "wgrad_o")
    g_o = shards(g_o)
    g_q, ((rc_o,),) = _wgrad(xn2, dq, "wgrad_q", carry=[_exchange_c([g_o])])
    g_q = shards(g_q)
    g_out, ((rc_q,),) = _wgrad(ycat, dx1b, "wgrad_out", carry=[_exchange_c([g_q])])
    g_out = shards(g_out)
    g_kv, d_lnmem = _kv_backward(dkv, memn, mem, sp["ln_mem_g"], w_kv8)
    (keep["w_o"], send_o), ((rc_out,),) = _rs_combine(g_o, rc_o, pos, "rs_combine_w_o", carry=[_exchange_c([g_out])])
    (keep["w_q"], send_q), ((rc_kv,),) = _rs_combine(g_q, rc_q, pos, "rs_combine_w_q", carry=[_exchange_c([g_kv])])
    (keep["w_out"], send_out), _ = _rs_combine(g_out, rc_out, pos, "rs_combine_w_out")
    (keep["w_kv"], send_kv), _ = _rs_combine(g_kv, rc_kv, pos, "rs_combine_w_kv")
    started, token = _exchange_xy_start([send_down, send_o, send_q, send_out, send_kv], "exchange_xy_2_start")
    pending.append((("w_down", "w_o", "w_q", "w_out", "w_kv"), started))
    (dh, dx, d_ga, d_gb, d_cw, d_lng, d_lnb, d_wsp, d_bs, d_lnmix), _ = _mix_backward(
        dx1, x, sp["ln_mix_g"], h, sp["sgu_ln_g"], sp["sgu_ln_b"], sp["w_spatial"], bt, conv_full,
        sp["grp_norm_a"], sp["grp_norm_b"], w_out, w_in_t, tm, carry=[_follow(token)])
    gs = {"ln_mix_g": d_lnmix, "sgu_ln_g": d_lng, "sgu_ln_b": d_lnb, "w_spatial": d_wsp, "b_spatial": _bias_grad(d_bs),
          "conv_w": d_cw[:3], "grp_norm_a": d_ga, "grp_norm_b": d_gb, "ln_attn_g": d_lnattn, "ln_mem_g": d_lnmem,
          "ln_ffn_g": d_lnffn, "ln_final_g": d_lnf}
    packed, seg = _pack_small(gs, loss)
    g_in, (_, (small_all,)) = _wgrad(dh, xn1, "wgrad_in", carry=[_follow(token), _all_gather([packed])])
    g_in = shards(g_in)
    (rc_in,), = _run_exchanges([_exchange_c([g_in])], "exchange_c_w_in")
    (keep["w_in"], send_in), _ = _rs_combine(g_in, rc_in, pos, "rs_combine_w_in")
    started, token = _exchange_xy_start([send_in], "exchange_xy_3_start")
    pending.append((("w_in",), started))
    return dx, keep, pending, token, small_all, seg


def kernel(x, mem, ln_mix_g, w_in, sgu_ln_g, sgu_ln_b, w_spatial, b_spatial, conv_w, grp_norm_a, grp_norm_b, w_out, ln_attn_g, ln_mem_g, w_q, w_kv, w_o, ln_ffn_g, w_gate_up, w_down, ln_final_g, loss_target, m_ln_mix_g, m_w_in, m_sgu_ln_g, m_sgu_ln_b, m_w_spatial, m_b_spatial, m_conv_w, m_grp_norm_a, m_grp_norm_b, m_w_out, m_ln_attn_g, m_ln_mem_g, m_w_q, m_w_kv, m_w_o, m_ln_ffn_g, m_w_gate_up, m_w_down, m_ln_final_g, v_ln_mix_g, v_w_in, v_sgu_ln_g, v_sgu_ln_b, v_w_spatial, v_b_spatial, v_conv_w, v_grp_norm_a, v_grp_norm_b, v_w_out, v_ln_attn_g, v_ln_mem_g, v_w_q, v_w_kv, v_w_o, v_ln_ffn_g, v_w_gate_up, v_w_down, v_ln_final_g):
    order = ["ln_mix_g", "w_in", "sgu_ln_g", "sgu_ln_b", "w_spatial", "b_spatial", "conv_w", "grp_norm_a", "grp_norm_b",
             "w_out", "ln_attn_g", "ln_mem_g", "w_q", "w_kv", "w_o", "ln_ffn_g", "w_gate_up", "w_down", "ln_final_g"]
    W = dict(ln_mix_g=ln_mix_g, w_in=w_in, sgu_ln_g=sgu_ln_g, sgu_ln_b=sgu_ln_b, w_spatial=w_spatial, b_spatial=b_spatial,
             conv_w=conv_w, grp_norm_a=grp_norm_a, grp_norm_b=grp_norm_b, w_out=w_out, ln_attn_g=ln_attn_g,
             ln_mem_g=ln_mem_g, w_q=w_q, w_kv=w_kv, w_o=w_o, ln_ffn_g=ln_ffn_g, w_gate_up=w_gate_up, w_down=w_down,
             ln_final_g=ln_final_g)
    M = dict(ln_mix_g=m_ln_mix_g, w_in=m_w_in, sgu_ln_g=m_sgu_ln_g, sgu_ln_b=m_sgu_ln_b, w_spatial=m_w_spatial,
             b_spatial=m_b_spatial, conv_w=m_conv_w, grp_norm_a=m_grp_norm_a, grp_norm_b=m_grp_norm_b, w_out=m_w_out,
             ln_attn_g=m_ln_attn_g, ln_mem_g=m_ln_mem_g, w_q=m_w_q, w_kv=m_w_kv, w_o=m_w_o, ln_ffn_g=m_ln_ffn_g,
             w_gate_up=m_w_gate_up, w_down=m_w_down, ln_final_g=m_ln_final_g)
    V = dict(ln_mix_g=v_ln_mix_g, w_in=v_w_in, sgu_ln_g=v_sgu_ln_g, sgu_ln_b=v_sgu_ln_b, w_spatial=v_w_spatial,
             b_spatial=v_b_spatial, conv_w=v_conv_w, grp_norm_a=v_grp_norm_a, grp_norm_b=v_grp_norm_b, w_out=v_w_out,
             ln_attn_g=v_ln_attn_g, ln_mem_g=v_ln_mem_g, w_q=v_w_q, w_kv=v_w_kv, w_o=v_w_o, ln_ffn_g=v_ln_ffn_g,
             w_gate_up=v_w_gate_up, w_down=v_w_down, ln_final_g=v_ln_final_g)

    bw = conv_w.shape[1] * N_DEV
    pos = jnp.stack([lax.axis_index("x"), lax.axis_index("y"), lax.axis_index("c")]).astype(jnp.int32)
    me = 4 * pos[0] + 2 * pos[1] + pos[2]

    sp = {nm: (W[nm].reshape(1, -1) if W[nm].ndim == 1 else W[nm]) for nm in SMALL}
    view = lambda a, nm: a.T if nm in TRANSPOSED else a
    wb = {nm: view(W[nm], nm).astype(BF16) for nm in BIG}
    grad_x, keep, pending, token, small_all, seg = _step(x[0], mem[0], loss_target[0], wb, conv_w, sp, pos)

    out = {}
    for k, (names, started) in enumerate(pending):
        landed = _exchange_xy_wait(started, token, "exchange_xy_%d_wait" % (k + 1))
        for nm, rxy in zip(names, landed):
            res = _adamw_shard(keep[nm], rxy, view(W[nm], nm), view(M[nm], nm), view(V[nm], nm), "adamw_" + nm)
            out[nm] = tuple(view(a, nm) for a in res)
            token = res[0]

    params = {nm: (_rows128(W[nm]), _rows128(M[nm]), _rows128(V[nm])) for nm in SMALL}
    per, conv_g_rows, loss_sum = _adamw_small(small_all, seg, params, seg["conv_w"])
    for nm in SMALL:
        out[nm] = tuple(a.reshape(W[nm].shape) for a in per[nm])
    conv_g = lax.dynamic_slice_in_dim(conv_g_rows.reshape(3, bw), me * conv_w.shape[1], conv_w.shape[1], axis=1)
    out["conv_w"] = (conv_g,) + tuple(_adamw_one(conv_w, conv_g, m_conv_w, v_conv_w, "adamw_conv"))

    loss = loss_sum[0, 0]
    res = [loss, grad_x[None]]
    for k in range(4):
        res += [out[nm][k] for nm in order]
    return tuple(res)
```

```python
import functools

import jax
import jax.numpy as jnp
from jax import lax
from jax.experimental import pallas as pl
from jax.experimental.pallas import tpu as pltpu

F32 = jnp.float32
BF16 = jnp.bfloat16
SDS = jax.ShapeDtypeStruct
MESH = pl.DeviceIdType.MESH

EPS = 1e-6
N_DEV = 8
HEADS = 4
CHUNK = 128
HALO = 16
SUB = 8
LANES = 128
TOKEN_TILE = 512
ROW_CHUNK = 256
RELAY_AT = 0.7

ADAM_LR = 0.001
ADAM_B1 = 0.9
ADAM_B2 = 0.999
ADAM_EPS = 1e-08
ADAM_WD = 0.01
ADAM_STEP = 10

BIG = ("w_in", "w_out", "w_q", "w_kv", "w_o", "w_gate_up", "w_down")
TRANSPOSED = ("w_in", "w_gate_up")
SMALL = ("ln_mix_g", "sgu_ln_g", "sgu_ln_b", "w_spatial", "b_spatial", "grp_norm_a", "grp_norm_b",
         "ln_attn_g", "ln_mem_g", "ln_ffn_g", "ln_final_g")


class _Exchange:
    def __init__(self, ins, out_shape, sems, start, finish, relay=None):
        self.ins, self.out_shape, self.sems = list(ins), list(out_shape), list(sems)
        self.start, self.finish, self.relay = start, finish, relay


def _pcall(body, carry=(), n_prefetch=0, **kw):
    if carry:
        return functools.partial(_carrying_call, body, tuple(carry), n_prefetch, kw)
    if n_prefetch:
        kw["grid_spec"] = pltpu.PrefetchScalarGridSpec(
            num_scalar_prefetch=n_prefetch, grid=kw.pop("grid"), in_specs=kw.pop("in_specs"),
            out_specs=kw.pop("out_specs"), scratch_shapes=kw.pop("scratch_shapes", ()))
    return pl.pallas_call(body, **kw)


def _carrying_call(body, carry, n_prefetch, kw, *args):
    kw = dict(kw)
    out_shape = kw.pop("out_shape")
    single = not isinstance(out_shape, (tuple, list))
    outs_shape = (out_shape,) if single else tuple(out_shape)
    out_specs = kw.pop("out_specs")
    out_specs = [out_specs] if single else list(out_specs)
    in_specs = list(kw.pop("in_specs"))
    scratch = list(kw.pop("scratch_shapes", ()))
    grid = tuple(kw.get("grid", ()))
    n_in, n_out, n_scr = len(args), len(outs_shape), len(scratch)

    def split(refs, k, counts):
        parts = []
        for cnt in counts:
            parts.append(refs[k:k + cnt])
            k += cnt
        return parts, k

    def wrapped(*refs):
        cins, k = split(refs, n_in, [len(p.ins) for p in carry])
        outs = refs[k:k + n_out]
        couts, k = split(refs, k + n_out, [len(p.out_shape) for p in carry])
        scr = refs[k:k + n_scr]
        csems, _ = split(refs, k + n_scr, [len(p.sems) for p in carry])
        first, last = True, True
        for a, g in enumerate(grid):
            first = (pl.program_id(a) == 0) & first
            last = (pl.program_id(a) == g - 1) & last

        def start_all():
            for p, ci, co, cs in zip(carry, cins, couts, csems):
                p.start(ci, co, cs)

        def relay_all():
            for p, ci, co, cs in zip(carry, cins, couts, csems):
                if p.relay is not None:
                    p.relay(ci, co, cs)

        def finish_all():
            for p, ci, co, cs in zip(carry, cins, couts, csems):
                p.finish(ci, co, cs)

        if len(grid) == 1:
            relay_now = pl.program_id(0) == min(int(RELAY_AT * grid[0]), grid[0] - 1)
        else:
            relay_now = last
        start_all() if not grid else pl.when(first)(start_all)
        relay_all() if not grid else pl.when(relay_now)(relay_all)
        body(*refs[:n_in], *outs, *scr)
        finish_all() if not grid else pl.when(last)(finish_all)

    c_in = [a for p in carry for a in p.ins]
    c_out = [s for p in carry for s in p.out_shape]
    c_sems = [s for p in carry for s in p.sems]
    res = _pcall(wrapped, n_prefetch=n_prefetch, out_shape=outs_shape + tuple(c_out),
                 in_specs=in_specs + _hbm_specs(len(c_in)), out_specs=out_specs + _hbm_specs(len(c_out)),
                 scratch_shapes=scratch + c_sems, **kw)(*args, *c_in)
    own = res[0] if single else tuple(res[:n_out])
    landed, k = [], n_out
    for p in carry:
        landed.append(list(res[k:k + len(p.out_shape)]))
        k += len(p.out_shape)
    return own, landed


def _hbm_specs(n):
    return [pl.BlockSpec(memory_space=pl.ANY)] * n


def _hosted(body, carry, **kw):
    if carry:
        return _pcall(body, carry=carry, **kw)
    call = _pcall(body, **kw)
    return lambda *args: (call(*args), [])


def _run_exchanges(parts, name):
    def body(*refs):
        pass

    _, landed = _pcall(body, carry=parts, out_shape=(), in_specs=[], out_specs=[], name=name)()
    return landed


def _arb(n):
    return pltpu.CompilerParams(dimension_semantics=("arbitrary",) * n)


def _tile(n, target, mult):
    best = None
    for t in range(mult, min(n, target) + 1, mult):
        if n % t == 0:
            best = t
    return n if best is None else best


def _round_up(n, m):
    return (n + m - 1) // m * m


def _dot(a, b):
    return jnp.dot(a, b, preferred_element_type=F32)


def _dot_nt(a, b):
    return lax.dot_general(a, b, (((1,), (1,)), ((), ())), preferred_element_type=F32)


def _dot_tn(a, b):
    return lax.dot_general(a, b, (((0,), (0,)), ((), ())), preferred_element_type=F32)


def _rstd(x):
    return lax.rsqrt(jnp.mean(x * x, axis=-1, keepdims=True) + EPS)


def _rms_bwd(dy, x, r, g):
    gdy = dy * g
    proj = jnp.sum(gdy * x, axis=-1, keepdims=True) * (1.0 / x.shape[-1])
    dx = r * gdy - x * (r * r * r) * proj
    dg = jnp.sum(dy * (x * r), axis=0, keepdims=True)
    return dx, dg


_GELU_C = 0.7978845608028654
_GELU_A = 0.044715


def _gelu(x):
    t = jnp.tanh(_GELU_C * (x + _GELU_A * x * x * x))
    return 0.5 * x * (1.0 + t), t


def _gelu_grad(x, t):
    return 0.5 * (1.0 + t) + 0.5 * x * (1.0 - t * t) * (_GELU_C * (1.0 + 3.0 * _GELU_A * x * x))


def _sigmoid(x):
    return 1.0 / (1.0 + jnp.exp(-x))


def _softmax(s):
    m = jnp.max(s, axis=-1, keepdims=True)
    e = jnp.exp(s - m)
    return e / jnp.sum(e, axis=-1, keepdims=True)


def _adamw(w, g, m, v):
    m = ADAM_B1 * m + (1.0 - ADAM_B1) * g
    v = ADAM_B2 * v + (1.0 - ADAM_B2) * (g * g)
    m_hat = m / (1.0 - ADAM_B1 ** ADAM_STEP)
    v_hat = v / (1.0 - ADAM_B2 ** ADAM_STEP)
    delta = -ADAM_LR * (m_hat / (jnp.sqrt(v_hat) + ADAM_EPS) + ADAM_WD * w)
    return delta, m, v


def _tril_mask():
    t = lax.broadcasted_iota(jnp.int32, (CHUNK, CHUNK), 0)
    s = lax.broadcasted_iota(jnp.int32, (CHUNK, CHUNK), 1)
    return (s <= t).astype(F32)


def _sgu_forward(ha, lng, lnb, wm, bt, mixed_s):
    aw = ha.shape[1] // 2
    hd = aw // HEADS
    a, th = _gelu(ha)
    u = a[:, :aw]
    v = a[:, aw:]
    mu = jnp.mean(v, axis=-1, keepdims=True)
    vc = v - mu
    rl = lax.rsqrt(jnp.mean(vc * vc, axis=-1, keepdims=True) + EPS)
    xhat = vc * rl
    vln = (xhat * lng + lnb).astype(BF16)
    for n in range(ha.shape[0] // CHUNK):
        rows = slice(n * CHUNK, (n + 1) * CHUNK)
        for h in range(HEADS):
            cols = slice(h * hd, (h + 1) * hd)
            mixed_s[rows, cols] = _dot(wm[h], vln[rows, cols]) + bt[:, h:h + 1]
    return th, u, xhat, rl, vln


def _conv_taps(zext):
    return pltpu.roll(zext, 2, 0), pltpu.roll(zext, 1, 0)


def _kv_forward(mem, g_mem, w_kv):
    ml, d = mem.shape
    xd = w_kv.shape[2]

    def body(mem_ref, g_ref, w_ref, memn_ref, kv_ref):
        x = mem_ref[...]
        memn = (x * _rstd(x) * g_ref[...]).astype(BF16)
        memn_ref[...] = memn
        for j in range(2 * HEADS):
            kv_ref[j] = _dot(memn, w_ref[j]).astype(BF16)

    return _pcall(body, out_shape=(SDS((ml, d), BF16), SDS((2 * HEADS, ml, xd), BF16)), name="kv_forward")(mem, g_mem, w_kv)


def _in_forward(x, g, w_in_t, tm, carry=()):
    s, d = x.shape
    n_in = w_in_t.shape[0]

    def body(x_ref, g_ref, w_ref, xn_ref, h_ref):
        xv = x_ref[...]
        xn = (xv * _rstd(xv) * g_ref[...]).astype(BF16)
        xn_ref[...] = xn
        h_ref[...] = _dot_nt(xn, w_ref[...])

    return _hosted(
        body, carry, grid=(s // tm,),
        in_specs=[pl.BlockSpec((tm, d), lambda i: (i, 0)), pl.BlockSpec((1, d), lambda i: (0, 0)),
                  pl.BlockSpec((n_in, d), lambda i: (0, 0))],
        out_specs=[pl.BlockSpec((tm, d), lambda i: (i, 0)), pl.BlockSpec((tm, n_in), lambda i: (i, 0))],
        out_shape=(SDS((s, d), BF16), SDS((s, n_in), F32)),
        compiler_params=_arb(1), name="in_forward")(x, g, w_in_t)


def _mix_forward(h, x, lng, lnb, w_sp, bt, conv_w, ga, gb, w_out, tm, carry=()):
    s, d = x.shape
    n_in = h.shape[1]
    aw = lng.shape[1]
    bw = d - aw
    in_a = 2 * aw
    hb_blocks = tm // HALO

    def body(h_ref, hprev_ref, x_ref, lng_ref, lnb_ref, wsp_ref, bt_ref, cw_ref, ga_ref, gb_ref, wout_ref,
             ycat_ref, x1_ref, mixed_s):
        i = pl.program_id(0)
        mask = _tril_mask()
        wm = [(wsp_ref[hh] * mask).astype(BF16) for hh in range(HEADS)]
        hv = h_ref[...]
        _, u, _, _, _ = _sgu_forward(hv[:, :in_a], lng_ref[...], lnb_ref[...], wm, bt_ref[...], mixed_s)
        sg = u * mixed_s[...]
        ycat_ref[:, :aw] = (sg * _rstd(sg) * ga_ref[...]).astype(BF16)

        gate_b = hv[:, in_a:in_a + bw]
        z = hv[:, in_a + bw:in_a + 2 * bw] * hv[:, in_a + 2 * bw:]
        hp = hprev_ref[...]
        zp = hp[:, in_a + bw:in_a + 2 * bw] * hp[:, in_a + 2 * bw:]
        zp = jnp.where(i == 0, 0.0, zp)
        zext = jnp.concatenate([zp, z], axis=0)
        z2, z1 = _conv_taps(zext)
        cw = cw_ref[...]
        conv = cw[0:1] * z2[HALO:] + cw[1:2] * z1[HALO:] + cw[2:3] * z
        sc = gate_b * conv
        ycat_ref[:, aw:] = (sc * _rstd(sc) * gb_ref[...]).astype(BF16)
        x1_ref[...] = x_ref[...] + _dot(ycat_ref[...], wout_ref[...])

    full = lambda shape: pl.BlockSpec(shape, lambda i: (0,) * len(shape))
    return _hosted(
        body, carry, grid=(s // tm,),
        in_specs=[pl.BlockSpec((tm, n_in), lambda i: (i, 0)),
                  pl.BlockSpec((HALO, n_in), lambda i: (jnp.maximum(i * hb_blocks - 1, 0), 0)),
                  pl.BlockSpec((tm, d), lambda i: (i, 0)),
                  full((1, aw)), full((1, aw)), full((HEADS, CHUNK, CHUNK)), full((CHUNK, HEADS)),
                  full((3, bw)), full((1, aw)), full((1, bw)), full((d, d))],
        out_specs=[pl.BlockSpec((tm, d), lambda i: (i, 0)), pl.BlockSpec((tm, d), lambda i: (i, 0))],
        out_shape=(SDS((s, d), BF16), SDS((s, d), F32)),
        scratch_shapes=[pltpu.VMEM((tm, aw), F32)],
        compiler_params=_arb(1), name="mix_forward")(h, h, x, lng, lnb, w_sp, bt, conv_w, ga, gb, w_out)


def _attn_forward(x1, g, w_q, kv, w_o, tm, carry=()):
    s, d = x1.shape
    _, ml, xd = kv.shape
    scale = xd ** -0.5

    def body(x1_ref, g_ref, wq_ref, kv_ref, wo_ref, xn_ref, q_ref, o_ref, x2_ref):
        xv = x1_ref[...]
        xn = (xv * _rstd(xv) * g_ref[...]).astype(BF16)
        xn_ref[...] = xn
        q_ref[...] = _dot(xn, wq_ref[...]).astype(BF16)
        for hh in range(HEADS):
            cols = slice(hh * xd, (hh + 1) * xd)
            p = _softmax(_dot_nt(q_ref[:, cols], kv_ref[hh]) * scale)
            o_ref[:, cols] = _dot(p.astype(BF16), kv_ref[HEADS + hh]).astype(BF16)
        x2_ref[...] = xv + _dot(o_ref[...], wo_ref[...])

    tok = pl.BlockSpec((tm, d), lambda i: (i, 0))
    return _hosted(
        body, carry, grid=(s // tm,),
        in_specs=[tok, pl.BlockSpec((1, d), lambda i: (0, 0)), pl.BlockSpec((d, d), lambda i: (0, 0)),
                  pl.BlockSpec((2 * HEADS, ml, xd), lambda i: (0, 0, 0)), pl.BlockSpec((d, d), lambda i: (0, 0))],
        out_specs=[tok, tok, tok, tok],
        out_shape=(SDS((s, d), BF16), SDS((s, d), BF16), SDS((s, d), BF16), SDS((s, d), F32)),
        compiler_params=_arb(1), name="attn_forward")(x1, g, w_q, kv, w_o)


def _ffn_forward(x2, g, w_gu, w_down, tm):
    s, d = x2.shape
    _, nf, tf, _ = w_gu.shape

    def body(x2_ref, g_ref, wgu_ref, wd_ref, xn_ref, gu_ref, x3_ref):
        f = pl.program_id(1)

        @pl.when(f == 0)
        def _():
            xv = x2_ref[...]
            xn_ref[...] = (xv * _rstd(xv) * g_ref[...]).astype(BF16)
            x3_ref[...] = xv

        xn = xn_ref[...]
        gate = _dot_nt(xn, wgu_ref[0])
        up = _dot_nt(xn, wgu_ref[1])
        gu_ref[0] = gate.astype(BF16)
        gu_ref[1] = up.astype(BF16)
        act = (gate * _sigmoid(gate) * up).astype(BF16)
        x3_ref[...] += _dot(act, wd_ref[...])

    tok = pl.BlockSpec((tm, d), lambda i, f: (i, 0))
    return _pcall(
        body, grid=(s // tm, nf),
        in_specs=[tok, pl.BlockSpec((1, d), lambda i, f: (0, 0)),
                  pl.BlockSpec((2, None, tf, d), lambda i, f: (0, f, 0, 0)),
                  pl.BlockSpec((tf, d), lambda i, f: (f, 0))],
        out_specs=[tok, pl.BlockSpec((2, None, tm, tf), lambda i, f: (0, f, i, 0)), tok],
        out_shape=(SDS((s, d), BF16), SDS((2, nf, s, tf), BF16), SDS((s, d), F32)),
        compiler_params=_arb(2), name="ffn_forward")(x2, g, w_gu, w_down)


def _final_backward(x3, target, g_final, tm):
    s, d = x3.shape

    def body(x3_ref, tgt_ref, gf_ref, loss_ref, dgf_ref, dx3_ref, dx3b_ref):
        @pl.when(pl.program_id(0) == 0)
        def _():
            loss_ref[...] = jnp.zeros_like(loss_ref)
            dgf_ref[...] = jnp.zeros_like(dgf_ref)

        xv = x3_ref[...]
        r = _rstd(xv)
        diff = xv * r * gf_ref[...] - tgt_ref[...]
        loss_ref[...] += 0.5 * jnp.sum(jnp.sum(diff * diff, axis=-1, keepdims=True), axis=0, keepdims=True) * (1.0 / d)
        dx3, dgf = _rms_bwd(diff * (1.0 / d), xv, r, gf_ref[...])
        dgf_ref[...] += dgf
        dx3_ref[...] = dx3
        dx3b_ref[...] = dx3.astype(BF16)

    tok = pl.BlockSpec((tm, d), lambda i: (i, 0))
    vec = pl.BlockSpec((1, d), lambda i: (0, 0))
    return _pcall(
        body, grid=(s // tm,), in_specs=[tok, tok, vec],
        out_specs=[pl.BlockSpec((SUB, LANES), lambda i: (0, 0)), vec, tok, tok],
        out_shape=(SDS((SUB, LANES), F32), SDS((1, d), F32), SDS((s, d), F32), SDS((s, d), BF16)),
        compiler_params=_arb(1), name="final_backward")(x3, target, g_final)


def _swiglu_backward(dx3b, gu, w_gu, w_down, tm):
    s, d = dx3b.shape
    _, nf, tf, _ = w_gu.shape

    def body(dx3b_ref, gu_ref, wgu_ref, wd_ref, act_ref, dgu_ref, dxn_ref):
        @pl.when(pl.program_id(1) == 0)
        def _():
            dxn_ref[...] = jnp.zeros_like(dxn_ref)

        for r0 in range(0, tm, ROW_CHUNK):
            rows = slice(r0, r0 + ROW_CHUNK)
            dact = _dot_nt(dx3b_ref[rows, :], wd_ref[...])
            gv = gu_ref[0, rows, :].astype(F32)
            uv = gu_ref[1, rows, :].astype(F32)
            sg = _sigmoid(gv)
            silu = gv * sg
            act_ref[rows, :] = (silu * uv).astype(BF16)
            dgate = (dact * uv * (sg * (1.0 + gv * (1.0 - sg)))).astype(BF16)
            dup = (dact * silu).astype(BF16)
            dgu_ref[0, rows, :] = dgate
            dgu_ref[1, rows, :] = dup
            part = _dot(dgate, wgu_ref[0]) + _dot(dup, wgu_ref[1])
            dxn_ref[rows, :] += part

    tok = pl.BlockSpec((tm, d), lambda i, f: (i, 0))
    pair = pl.BlockSpec((2, None, tm, tf), lambda i, f: (0, f, i, 0))
    return _pcall(
        body, grid=(s // tm, nf),
        in_specs=[tok, pair, pl.BlockSpec((2, None, tf, d), lambda i, f: (0, f, 0, 0)),
                  pl.BlockSpec((tf, d), lambda i, f: (f, 0))],
        out_specs=[pl.BlockSpec((None, tm, tf), lambda i, f: (f, i, 0)), pair, tok],
        out_shape=(SDS((nf, s, tf), BF16), SDS((2, nf, s, tf), BF16), SDS((s, d), F32)),
        compiler_params=_arb(2), name="swiglu_backward")(dx3b, gu, w_gu, w_down)


def _attn_backward(dx3, dxn3, x2, g_ffn, x1, g, q, kv, w_q, w_o, tm, carry=()):
    s, d = x1.shape
    _, ml, xd = kv.shape
    scale = xd ** -0.5

    def body(dx3_ref, dxn3_ref, x2_ref, g2_ref, x1_ref, g_ref, q_ref, kv_ref, wq_ref, wo_ref,
             dx2b_ref, dq_ref, dx1_ref, dx1b_ref, dkv_ref, dg_ref, dg2_ref, do_s):
        i = pl.program_id(0)

        @pl.when(i == 0)
        def _():
            dkv_ref[...] = jnp.zeros_like(dkv_ref)
            dg_ref[...] = jnp.zeros_like(dg_ref)
            dg2_ref[...] = jnp.zeros_like(dg2_ref)

        x2v = x2_ref[...]
        dx2n, dg2 = _rms_bwd(dxn3_ref[...], x2v, _rstd(x2v), g2_ref[...])
        dg2_ref[...] += dg2
        dx2 = dx3_ref[...] + dx2n
        dx2b_ref[...] = dx2.astype(BF16)
        do_s[...] = _dot_nt(dx2b_ref[...], wo_ref[...]).astype(BF16)
        for hh in range(HEADS):
            kc = slice(hh * xd, (hh + 1) * xd)
            qh = q_ref[:, kc]
            kh = kv_ref[hh]
            doh = do_s[:, kc]
            p = _softmax(_dot_nt(qh, kh) * scale)
            dp = _dot_nt(doh, kv_ref[HEADS + hh])
            dkv_ref[HEADS + hh] += _dot_tn(p.astype(BF16), doh)
            ds = (p * (dp - jnp.sum(dp * p, axis=-1, keepdims=True)) * scale).astype(BF16)
            dq_ref[:, kc] = _dot(ds, kh).astype(BF16)
            dkv_ref[hh] += _dot_tn(ds, qh)
        dg_tile = jnp.zeros_like(dg_ref)
        for r0 in range(0, tm, min(ROW_CHUNK, tm)):
            rows = slice(r0, r0 + min(ROW_CHUNK, tm))
            dxn = _dot_nt(dq_ref[rows, :], wq_ref[...])
            xv = x1_ref[rows, :]
            dx, dg = _rms_bwd(dxn, xv, _rstd(xv), g_ref[...])
            dg_tile = dg_tile + dg
            dx1 = dx2[rows] + dx
            dx1_ref[rows, :] = dx1
            dx1b_ref[rows, :] = dx1.astype(BF16)
        dg_ref[...] += dg_tile

    tok = pl.BlockSpec((tm, d), lambda i: (i, 0))
    vec = pl.BlockSpec((1, d), lambda i: (0, 0))
    sq = pl.BlockSpec((d, d), lambda i: (0, 0))
    kvs = pl.BlockSpec((2 * HEADS, ml, xd), lambda i: (0, 0, 0))
    return _hosted(
        body, carry, grid=(s // tm,),
        in_specs=[tok, tok, tok, vec, tok, vec, tok, kvs, sq, sq],
        out_specs=[tok, tok, tok, tok, kvs, vec, vec],
        out_shape=(SDS((s, d), BF16), SDS((s, d), BF16), SDS((s, d), F32), SDS((s, d), BF16),
                   SDS((2 * HEADS, ml, xd), F32), SDS((1, d), F32), SDS((1, d), F32)),
        scratch_shapes=[pltpu.VMEM((tm, d), BF16)],
        compiler_params=_arb(1), name="attn_backward")(dx3, dxn3, x2, g_ffn, x1, g, q, kv, w_q, w_o)


def _kv_backward(dkv, memn, mem, g_mem, w_kv):
    ml, d = mem.shape
    xd = w_kv.shape[2]

    def body(dkv_ref, memn_ref, mem_ref, g_ref, w_ref, dw_ref, dg_ref):
        dmemn = jnp.zeros((ml, d), F32)
        for j in range(2 * HEADS):
            dkvb = dkv_ref[j].astype(BF16)
            dw_ref[j] = _dot_tn(memn_ref[...], dkvb)
            dmemn = dmemn + _dot_nt(dkvb, w_ref[j])
        x = mem_ref[...]
        dg_ref[...] = jnp.sum(dmemn * (x * _rstd(x)), axis=0, keepdims=True)

    return _pcall(body, out_shape=(SDS((2 * HEADS, d, xd), F32), SDS((1, d), F32)), name="kv_backward")(dkv, memn, mem, g_mem, w_kv)


def _mix_backward(dx1, x, g_mix, h, lng, lnb, w_sp, bt, conv_w, ga, gb, w_out, w_in, tm, carry=()):
    s, d = x.shape
    n_in = h.shape[1]
    aw = lng.shape[1]
    bw = d - aw
    hd = aw // HEADS
    in_a = 2 * aw
    hb_blocks = tm // HALO
    last_blk = s // HALO - 1
    nt = s // tm
    te = tm + HALO
    tee = tm + 2 * HALO

    def body(dx1_ref, dx1n_ref, x_ref, gm_ref, h_ref, hp_ref, hn_ref, lng_ref, lnb_ref, wsp_ref, bt_ref, cw_ref,
             ga_ref, gb_ref, wout_ref, win_ref,
             dh_ref, dx_ref, dga_ref, dgb_ref, dcw_ref, dlng_ref, dlnb_ref, dwsp_ref, dbs_ref, dgm_ref,
             mixed_s, dvln_s):
        i = pl.program_id(0)

        @pl.when(i == 0)
        def _():
            for ref in (dga_ref, dgb_ref, dcw_ref, dlng_ref, dlnb_ref, dwsp_ref, dbs_ref, dgm_ref):
                ref[...] = jnp.zeros_like(ref)

        mask = _tril_mask()
        wm = [(wsp_ref[hh] * mask).astype(BF16) for hh in range(HEADS)]
        hv = h_ref[...]
        dx1 = dx1_ref[...]
        dx1e = jnp.concatenate([dx1, dx1n_ref[...]], axis=0).astype(BF16)
        dycat = _dot_nt(dx1e, wout_ref[...])

        hbe = jnp.concatenate([hp_ref[:, in_a:], hv[:, in_a:], hn_ref[:, in_a:]], axis=0)
        row = lax.broadcasted_iota(jnp.int32, (tee, 1), 0)
        zext = hbe[:, bw:2 * bw] * hbe[:, 2 * bw:]
        zext = jnp.where((i == 0) & (row < HALO), 0.0, zext)
        z2e, z1e = _conv_taps(zext)
        cw = cw_ref[...]
        conv_e = (cw[0:1] * z2e + cw[1:2] * z1e + cw[2:3] * zext)[HALO:]
        gate_b_e = hbe[HALO:, :bw]
        sc_e = gate_b_e * conv_e
        rb = _rstd(sc_e)
        dyb = dycat[:, aw:]
        gdy = dyb * gb_ref[...]
        dsc_e = rb * gdy - sc_e * (rb * rb * rb) * (jnp.sum(gdy * sc_e, axis=-1, keepdims=True) * (1.0 / bw))
        dgb_ref[...] += jnp.sum((dyb * (sc_e * rb))[:tm], axis=0, keepdims=True)
        dconv_e = dsc_e * gate_b_e
        dconv_e = jnp.where((i == nt - 1) & (row[:te] >= tm), 0.0, dconv_e)
        dconv = dconv_e[:tm]
        dc1 = pltpu.roll(dconv_e, te - 1, 0)[:tm]
        dc2 = pltpu.roll(dconv_e, te - 2, 0)[:tm]
        dz = cw[2:3] * dconv + cw[1:2] * dc1 + cw[0:1] * dc2
        z = zext[HALO:HALO + tm]
        z1 = z1e[HALO:HALO + tm]
        z2 = z2e[HALO:HALO + tm]
        dcw_ref[0:1, :] += jnp.sum(dconv * z2, axis=0, keepdims=True)
        dcw_ref[1:2, :] += jnp.sum(dconv * z1, axis=0, keepdims=True)
        dcw_ref[2:3, :] += jnp.sum(dconv * z, axis=0, keepdims=True)
        dh_ref[:, in_a:in_a + bw] = (dsc_e[:tm] * conv_e[:tm]).astype(BF16)
        dh_ref[:, in_a + bw:in_a + 2 * bw] = (dz * hv[:, in_a + 2 * bw:]).astype(BF16)
        dh_ref[:, in_a + 2 * bw:] = (dz * hv[:, in_a + bw:in_a + 2 * bw]).astype(BF16)

        ha = hv[:, :in_a]
        th, u, xhat, rl, vln = _sgu_forward(ha, lng_ref[...], lnb_ref[...], wm, bt_ref[...], mixed_s)
        mixed = mixed_s[...]
        sg = u * mixed
        dsg, dga = _rms_bwd(dycat[:tm, :aw], sg, _rstd(sg), ga_ref[...])
        dga_ref[...] += dga
        du = dsg * mixed
        dmixed = dsg * u
        dmb = dmixed.astype(BF16)
        for n in range(tm // CHUNK):
            rows = slice(n * CHUNK, (n + 1) * CHUNK)
            dbs_ref[...] += dmixed[rows]
            for hh in range(HEADS):
                cols = slice(hh * hd, (hh + 1) * hd)
                dvln_s[rows, cols] = _dot_tn(wm[hh], dmb[rows, cols])
                dwsp_ref[hh] += mask * _dot_nt(dmb[rows, cols], vln[rows, cols])
        dvln = dvln_s[...]
        dlng_ref[...] += jnp.sum(dvln * xhat, axis=0, keepdims=True)
        dlnb_ref[...] += jnp.sum(dvln, axis=0, keepdims=True)
        dxh = dvln * lng_ref[...]
        dv = rl * (dxh - jnp.mean(dxh, axis=-1, keepdims=True) - xhat * jnp.mean(dxh * xhat, axis=-1, keepdims=True))
        dh_ref[:, :in_a] = (jnp.concatenate([du, dv], axis=-1) * _gelu_grad(ha, th)).astype(BF16)

        dgm_tile = jnp.zeros_like(dgm_ref)
        for r0 in range(0, tm, min(ROW_CHUNK, tm)):
            rows = slice(r0, r0 + min(ROW_CHUNK, tm))
            dxn = _dot(dh_ref[rows, :], win_ref[...])
            xv = x_ref[rows, :]
            dx, dgm = _rms_bwd(dxn, xv, _rstd(xv), gm_ref[...])
            dgm_tile = dgm_tile + dgm
            dx_ref[rows, :] = dx1_ref[rows, :] + dx
        dgm_ref[...] += dgm_tile

    full = lambda shape: pl.BlockSpec(shape, lambda i: (0,) * len(shape))
    tok = pl.BlockSpec((tm, d), lambda i: (i, 0))
    nxt = lambda i: (jnp.minimum((i + 1) * hb_blocks, last_blk), 0)
    prv = lambda i: (jnp.maximum(i * hb_blocks - 1, 0), 0)
    return _hosted(
        body, carry, grid=(nt,),
        in_specs=[tok, pl.BlockSpec((HALO, d), nxt), tok, full((1, d)),
                  pl.BlockSpec((tm, n_in), lambda i: (i, 0)), pl.BlockSpec((HALO, n_in), prv), pl.BlockSpec((HALO, n_in), nxt),
                  full((1, aw)), full((1, aw)), full((HEADS, CHUNK, CHUNK)), full((CHUNK, HEADS)), full((3, bw)),
                  full((1, aw)), full((1, bw)), full((d, d)), full((n_in, d))],
        out_specs=[pl.BlockSpec((tm, n_in), lambda i: (i, 0)), tok,
                   full((1, aw)), full((1, bw)), full((SUB, bw)), full((1, aw)), full((1, aw)),
                   full((HEADS, CHUNK, CHUNK)), full((CHUNK, aw)), full((1, d))],
        out_shape=(SDS((s, n_in), BF16), SDS((s, d), F32),
                   SDS((1, aw), F32), SDS((1, bw), F32), SDS((SUB, bw), F32), SDS((1, aw), F32), SDS((1, aw), F32),
                   SDS((HEADS, CHUNK, CHUNK), F32), SDS((CHUNK, aw), F32), SDS((1, d), F32)),
        scratch_shapes=[pltpu.VMEM((tm, aw), F32), pltpu.VMEM((tm, aw), F32)],
        compiler_params=_arb(1), name="mix_backward")(dx1, dx1, x, g_mix, h, h, h, lng, lnb, w_sp, bt, conv_w, ga, gb, w_out, w_in)


def _bias_grad(dbs):
    aw = dbs.shape[1]
    hd = aw // HEADS

    def body(dbs_ref, out_ref):
        ones = jnp.ones((SUB, hd), F32)
        for hh in range(HEADS):
            r = lax.dot_general(ones, dbs_ref[:, hh * hd:(hh + 1) * hd], (((1,), (1,)), ((), ())),
                                precision=lax.Precision.HIGHEST, preferred_element_type=F32)
            out_ref[hh:hh + 1, :] = r[0:1]

    return _pcall(body, out_shape=SDS((HEADS, CHUNK), F32), name="bias_grad")(dbs)


def _wgrad_body(a_ref, b_ref, o_ref):
    o_ref[...] = _dot_tn(a_ref[...], b_ref[...])


def _wgrad(a, b, name, carry=()):
    k, m = a.shape
    n = b.shape[1]
    tm = _tile(m, 512, LANES)
    tn = _tile(n, 1024, LANES)
    return _hosted(
        functools.partial(_wgrad_body), carry, grid=(m // tm, n // tn),
        in_specs=[pl.BlockSpec((k, tm), lambda i, j: (0, i)), pl.BlockSpec((k, tn), lambda i, j: (0, j))],
        out_specs=pl.BlockSpec((tm, tn), lambda i, j: (i, j)),
        out_shape=SDS((m, n), F32), compiler_params=_arb(2), name=name)(a, b)


def _wgrad_blocked_lhs(a, b, name, carry=()):
    nb, k, t = a.shape
    n = b.shape[1]
    tn = _tile(n, 1024, LANES)
    return _hosted(
        functools.partial(_wgrad_body), carry, grid=(nb, n // tn),
        in_specs=[pl.BlockSpec((None, k, t), lambda i, j: (i, 0, 0)), pl.BlockSpec((k, tn), lambda i, j: (0, j))],
        out_specs=pl.BlockSpec((t, tn), lambda i, j: (i, j)),
        out_shape=SDS((nb * t, n), F32), compiler_params=_arb(2), name=name)(a, b)


def _wgrad_blocked_rhs(a, b, name, carry=()):
    k, m = a.shape
    nb, _, t = b.shape
    tm = _tile(m, 512, LANES)
    return _hosted(
        functools.partial(_wgrad_body), carry, grid=(m // tm, nb),
        in_specs=[pl.BlockSpec((k, tm), lambda i, j: (0, i)), pl.BlockSpec((None, k, t), lambda i, j: (j, 0, 0))],
        out_specs=pl.BlockSpec((None, tm, t), lambda i, j: (j, i, 0)),
        out_shape=SDS((nb, m, t), F32), compiler_params=_arb(2), name=name)(a, b)


def _unblock_cols(wb, name, carry=()):
    nb, r, t = wb.shape
    tr = _tile(r, 256, 16)

    def body(w_ref, o_ref):
        o_ref[...] = jnp.concatenate([w_ref[j].astype(F32) for j in range(nb)], axis=-1).astype(o_ref.dtype)

    return _hosted(
        body, carry, grid=(r // tr,),
        in_specs=[pl.BlockSpec((nb, tr, t), lambda i: (0, i, 0))], out_specs=pl.BlockSpec((tr, nb * t), lambda i: (i, 0)),
        out_shape=SDS((r, nb * t), wb.dtype), compiler_params=_arb(1), name=name)(wb)


def _block_cols(w, nb, name, carry=()):
    r, n = w.shape
    t = n // nb
    tr = _tile(r, 256, 16)

    def body(w_ref, o_ref):
        wv = w_ref[...]
        for j in range(nb):
            o_ref[j] = wv[:, j * t:(j + 1) * t]

    return _hosted(
        body, carry, grid=(r // tr,),
        in_specs=[pl.BlockSpec((tr, n), lambda i: (i, 0))], out_specs=pl.BlockSpec((nb, tr, t), lambda i: (0, i, 0)),
        out_shape=SDS((nb, r, t), w.dtype), compiler_params=_arb(1), name=name)(w)


def _place():
    x, y, c = lax.axis_index("x"), lax.axis_index("y"), lax.axis_index("c")
    return x, y, c, [(1 - x, y), (x, 1 - y), (1 - x, 1 - y)]


def _all_gather(shards):
    n = len(shards)
    slots = 9
    cut = [(s.shape[0] // 32) * 16 for s in shards]

    def build(ins, outs, sems):
        send_sems, recv_sems, local_sems = sems
        x, y, c, _ = _place()
        me, sib, xn, yn, dg = (x, y, c), (x, y, 1 - c), (1 - x, y, c), (x, 1 - y, c), (1 - x, 1 - y, c)
        other = lambda p: (p[0], p[1], 1 - p[2])

        def rows(a, p, part=None):
            ref = outs[a].at[4 * p[0] + 2 * p[1] + p[2]]
            if part is None or cut[a] == 0:
                return ref if part in (None, 0) else None
            return ref.at[pl.ds(0, cut[a])] if part == 0 else ref.at[pl.ds(cut[a], shards[a].shape[0] - cut[a])]

        def copy(a, k, ref, to, src=None):
            if ref is None:
                return None
            return pltpu.make_async_remote_copy(
                src_ref=ref if src is None else src, dst_ref=ref, send_sem=send_sems.at[slots * a + k],
                recv_sem=recv_sems.at[slots * a + k], device_id=to, device_id_type=MESH)

        def real(cps):
            return [cp for cp in cps if cp is not None]

        class Copies:
            own = lambda a: [copy(a, 1, rows(a, me), xn, ins[a]), copy(a, 2, rows(a, me), yn, ins[a]),
                             copy(a, 0, rows(a, me), sib, ins[a])]
            local = lambda a: pltpu.make_async_copy(ins[a], rows(a, me), local_sems.at[a])
            from_x = lambda a: copy(a, 1, rows(a, xn), me)
            from_y = lambda a: copy(a, 2, rows(a, yn), me)
            after_x = lambda a: real([copy(a, 4, rows(a, xn, 1), yn), copy(a, 5, rows(a, xn), sib)])
            after_y = lambda a: real([copy(a, 3, rows(a, yn, 0), xn), copy(a, 6, rows(a, yn), sib)])
            diag_in = lambda a: real([copy(a, 3, rows(a, dg, 0), me), copy(a, 4, rows(a, dg, 1), me)])
            diag_on = lambda a: real([copy(a, 7, rows(a, dg, 0), sib), copy(a, 8, rows(a, dg, 1), sib)])
            from_sib = lambda a: real([copy(a, 0, rows(a, sib), me), copy(a, 5, rows(a, other(xn)), me),
                                       copy(a, 6, rows(a, other(yn)), me), copy(a, 7, rows(a, other(dg), 0), me),
                                       copy(a, 8, rows(a, other(dg), 1), me)])

        return Copies

    def start(ins, outs, sems):
        cps = build(ins, outs, sems)
        for a in range(n):
            for cp in cps.own(a):
                cp.start()
        for a in range(n):
            cps.local(a).start()

    def relay(ins, outs, sems):
        cps = build(ins, outs, sems)
        for a in range(n):
            cps.from_x(a).wait_recv()
            for cp in cps.after_x(a):
                cp.start()
            cps.from_y(a).wait_recv()
            for cp in cps.after_y(a):
                cp.start()

    def finish(ins, outs, sems):
        cps = build(ins, outs, sems)
        for a in range(n):
            for arrived, onward in zip(cps.diag_in(a), cps.diag_on(a)):
                arrived.wait_recv()
                onward.start()
        for a in range(n):
            for cp in cps.from_sib(a):
                cp.wait_recv()
            for cp in cps.own(a) + cps.after_x(a) + cps.after_y(a) + cps.diag_on(a):
                cp.wait_send()
            cps.local(a).wait()

    return _Exchange(shards, [SDS((N_DEV,) + s.shape, s.dtype) for s in shards],
                     [pltpu.SemaphoreType.DMA((slots * n,)), pltpu.SemaphoreType.DMA((slots * n,)),
                      pltpu.SemaphoreType.DMA((n,))], start, finish, relay)


def _swap_exchange(ins, out_shape, per, copies):
    def start(i, o, sems):
        for cp in copies(i, o, sems):
            cp.start()

    def finish(i, o, sems):
        for cp in copies(i, o, sems):
            cp.wait()

    n = per * len(ins)
    return _Exchange(ins, out_shape, [pltpu.SemaphoreType.DMA((n,)), pltpu.SemaphoreType.DMA((n,))], start, finish)


def _exchange_c(gs):
    def copies(ins, outs, sems):
        x, y, c, _ = _place()
        return [pltpu.make_async_remote_copy(
                    src_ref=ins[a].at[2 * k + 1 - c], dst_ref=outs[a].at[k],
                    send_sem=sems[0].at[4 * a + k], recv_sem=sems[1].at[4 * a + k],
                    device_id=(x, y, 1 - c), device_id_type=MESH)
                for a in range(len(gs)) for k in range(4)]

    return _swap_exchange(gs, [SDS((4,) + g.shape[1:], g.dtype) for g in gs], 4, copies)


def _exchange_xy(sends):
    def copies(ins, outs, sems):
        x, y, c, chips = _place()
        return [pltpu.make_async_remote_copy(
                    src_ref=ins[a].at[t], dst_ref=outs[a].at[t],
                    send_sem=sems[0].at[3 * a + t], recv_sem=sems[1].at[3 * a + t],
                    device_id=(*chips[t], c), device_id_type=MESH)
                for a in range(len(sends)) for t in range(3)]

    return _swap_exchange(sends, [SDS(s.shape, s.dtype) for s in sends], 3, copies)


def _rs_combine(g, recv, pos, name, carry=()):
    _, r, cdim = g.shape
    tr = _tile(r, 256, 16)

    def body(pos_ref, g0, r0, g1, r1, g2, r2, g3, r3, keep_ref, send_ref):
        keep_ref[...] = g0[...] + r0[...]
        send_ref[0] = (g1[...] + r1[...]).astype(BF16)
        send_ref[1] = (g2[...] + r2[...]).astype(BF16)
        send_ref[2] = (g3[...] + r3[...]).astype(BF16)

    def k_of(p, t):
        px = p[0] if t in (0, 2) else 1 - p[0]
        py = p[1] if t in (0, 1) else 1 - p[1]
        return 2 * px + py

    blk = (None, tr, cdim)
    in_specs = []
    for t in range(4):
        in_specs.append(pl.BlockSpec(blk, functools.partial(lambda j, p, t: (2 * k_of(p, t) + p[2], j, 0), t=t)))
        in_specs.append(pl.BlockSpec(blk, functools.partial(lambda j, p, t: (k_of(p, t), j, 0), t=t)))
    return _hosted(
        body, carry, n_prefetch=1, out_shape=(SDS((r, cdim), F32), SDS((3, r, cdim), BF16)),
        grid=(r // tr,), in_specs=in_specs,
        out_specs=[pl.BlockSpec((tr, cdim), lambda j, p: (j, 0)), pl.BlockSpec((3, tr, cdim), lambda j, p: (0, j, 0))],
        compiler_params=_arb(1), name=name)(pos, g, recv, g, recv, g, recv, g, recv)


def _adamw_shard(keep, recv, w, m, v, name):
    r, cdim = w.shape
    tr = _tile(r, 256, 16)

    def body(k_ref, r_ref, w_ref, m_ref, v_ref, g_ref, d_ref, nm_ref, nv_ref):
        g = ((k_ref[...] + r_ref[0].astype(F32)) + r_ref[1].astype(F32)) + r_ref[2].astype(F32)
        g_ref[...] = g
        d_ref[...], nm_ref[...], nv_ref[...] = _adamw(w_ref[...], g, m_ref[...], v_ref[...])

    blk = pl.BlockSpec((tr, cdim), lambda j: (j, 0))
    out = SDS((r, cdim), F32)
    return _pcall(body, grid=(r // tr,), in_specs=[blk, pl.BlockSpec((3, tr, cdim), lambda j: (0, j, 0)), blk, blk, blk],
                  out_specs=[blk] * 4, out_shape=(out,) * 4, compiler_params=_arb(1), name=name)(keep, recv, w, m, v)


_HBM = pl.BlockSpec(memory_space=pltpu.HBM)
_SEM = pl.BlockSpec(memory_space=pltpu.SEMAPHORE)
_SPLIT = pltpu.CompilerParams(has_side_effects=pltpu.SideEffectType.DATAFLOW_SIDE_EFFECTING)


def _split_copies(kind, n, refs):
    srcs, lands, (send_sems, recv_sems) = refs[:n], refs[n:2 * n], refs[2 * n:2 * n + 2]
    x, y, c, chips = _place()
    per = _SPLIT_COPIES[kind]
    if kind == "xy":
        ends = lambda a, t: (srcs[a].at[t], lands[a].at[t], (*chips[t], c))
    else:
        ends = lambda a, k: (srcs[a].at[2 * k + 1 - c], lands[a].at[k], (x, y, 1 - c))
    cps = []
    for a in range(n):
        for t in range(per):
            src, dst, to = ends(a, t)
            cps.append(pltpu.make_async_remote_copy(src_ref=src, dst_ref=dst, send_sem=send_sems.at[per * a + t],
                                                    recv_sem=recv_sems.at[per * a + t], device_id=to, device_id_type=MESH))
    return cps


_SPLIT_COPIES = {"xy": 3, "c": 4}


def _exchange_start(kind, arrays, name, after=None):
    n = len(arrays)
    order = [] if after is None else [after]

    def body(*refs):
        refs = refs[:2 * n] + refs[2 * n + len(order):]
        for cp in _split_copies(kind, n, refs):
            cp.start()
        refs[-1][...] = jnp.zeros_like(refs[-1])

    hbm = lambda a: pltpu.with_memory_space_constraint(a, pltpu.HBM)
    land = [a.shape if kind == "xy" else (4,) + a.shape[1:] for a in arrays]
    bufs = [pltpu.HBM(a.shape, a.dtype) for a in arrays] + [pltpu.HBM(s, a.dtype) for s, a in zip(land, arrays)]
    sems = pltpu.SemaphoreType.DMA((_SPLIT_COPIES[kind] * n,))
    res = _pcall(
        body, name=name, out_shape=(sems, sems, *bufs, SDS((SUB, LANES), F32)),
        in_specs=[_HBM] * (2 * n) + _hbm_specs(len(order)),
        out_specs=[_SEM, _SEM] + [_HBM] * (2 * n) + [pl.BlockSpec(memory_space=pltpu.VMEM)],
        input_output_aliases={k: 2 + k for k in range(2 * n)}, compiler_params=_SPLIT)(
            *[hbm(a) for a in arrays], *[hbm(lax.empty(s, a.dtype)) for s, a in zip(land, arrays)], *order)
    return (kind, n, res[:-1]), res[-1]


def _exchange_wait(started, after, name):
    kind, n, (send_sems, recv_sems, *bufs) = started

    def body(*refs):
        for cp in _split_copies(kind, n, refs):
            cp.wait_send()
            cp.wait_recv()

    shapes = [pltpu.HBM(b.shape, b.dtype) for b in bufs]
    res = _pcall(
        body, name=name, out_shape=tuple(shapes),
        in_specs=[_HBM] * (2 * n) + [_SEM, _SEM, pl.BlockSpec(memory_space=pl.ANY)], out_specs=[_HBM] * (2 * n),
        input_output_aliases={k: k for k in range(2 * n)}, compiler_params=_SPLIT)(*bufs, send_sems, recv_sems, after)
    return list(res[n:])


def _follow(token):
    nothing = lambda ins, outs, sems: None
    return _Exchange([token], [], [], nothing, nothing)


def _adamw_small(gathered, seg, params, conv_rows):
    names = list(params)
    c0, cn = conv_rows

    def body(*refs):
        gat_ref = refs[0]
        ins = refs[1:1 + 3 * len(names)]
        outs = refs[1 + 3 * len(names):]

        def total(r0, rn):
            tot = gat_ref[0, r0:r0 + rn, :]
            for dev in range(1, N_DEV):
                tot = tot + gat_ref[dev, r0:r0 + rn, :]
            return tot

        for k, nm in enumerate(names):
            g = total(*seg[nm])
            w_ref, m_ref, v_ref = ins[3 * k:3 * k + 3]
            g_ref, d_ref, nm_ref, nv_ref = outs[4 * k:4 * k + 4]
            g_ref[...] = g
            d_ref[...], nm_ref[...], nv_ref[...] = _adamw(w_ref[...], g, m_ref[...], v_ref[...])
        outs[-2][...] = total(c0, cn)
        outs[-1][...] = total(*seg["loss"])

    flat_in = [a for nm in names for a in params[nm]]
    out_shape = []
    for nm in names:
        out_shape += [SDS(params[nm][0].shape, F32)] * 4
    out_shape += [SDS((cn, LANES), F32), SDS((seg["loss"][1], LANES), F32)]
    res = _pcall(body, out_shape=tuple(out_shape), name="adamw_small")(gathered, *flat_in)
    per = {nm: res[4 * k:4 * k + 4] for k, nm in enumerate(names)}
    return per, res[-2], res[-1]


def _adamw_one(w, g, m, v, name):
    def body(w_ref, g_ref, m_ref, v_ref, d_ref, nm_ref, nv_ref):
        d_ref[...], nm_ref[...], nv_ref[...] = _adamw(w_ref[...], g_ref[...], m_ref[...], v_ref[...])

    return _pcall(body, out_shape=(SDS(w.shape, F32),) * 3, name=name)(w, g, m, v)


def _rows128(a):
    return a.reshape(-1, LANES)


def _pack_small(gs, loss_tile):
    seg, pieces, row = {}, [], 0
    for nm in SMALL + ("conv_w", "loss"):
        piece = loss_tile if nm == "loss" else _rows128(gs[nm])
        rn = _round_up(piece.shape[0], SUB)
        pieces.append(jnp.pad(piece, ((0, rn - piece.shape[0]), (0, 0))))
        seg[nm] = (row, piece.shape[0])
        row += rn
    return jnp.concatenate(pieces, axis=0), seg


def _step(x, mem, target, wb, conv_w, sp, pos):
    s, d = x.shape
    tm = min(TOKEN_TILE, s)
    tm_wide = min(2 * TOKEN_TILE, s)
    rows = lambda w8: w8.reshape(-1, w8.shape[2])
    shards = lambda g: g.reshape((N_DEV, -1) + g.shape[1:])
    bt = sp["b_spatial"].T

    (w_in8, conv8), = _run_exchanges([_all_gather([wb["w_in"], conv_w])], "gather_w_in")
    conv_full = conv8.transpose(1, 0, 2).reshape(3, -1)
    w_in_t = rows(w_in8)
    (xn1, h), ((w_out8, w_kv8, w_q8),) = _in_forward(
        x, sp["ln_mix_g"], w_in_t, tm, carry=[_all_gather([wb["w_out"], wb["w_kv"], wb["w_q"]])])
    w_out = rows(w_out8)
    (ycat, x1), ((w_o8, w_down8),) = _mix_forward(
        h, x, sp["sgu_ln_g"], sp["sgu_ln_b"], sp["w_spatial"], bt, conv_full, sp["grp_norm_a"], sp["grp_norm_b"], w_out, tm,
        carry=[_all_gather([wb["w_o"], wb["w_down"]])])
    w_q, w_o, w_down = rows(w_q8), rows(w_o8), rows(w_down8)
    memn, kv = _kv_forward(mem, sp["ln_mem_g"], w_kv8)
    (xn2, q, o, x2), ((w_gu8,),) = _attn_forward(
        x1, sp["ln_attn_g"], w_q, kv, w_o, tm, carry=[_all_gather([wb["w_gate_up"]])])
    w_gu = w_gu8.reshape((2, N_DEV // 2) + w_gu8.shape[1:])
    xn3, gu, x3 = _ffn_forward(x2, sp["ln_ffn_g"], w_gu, w_down, tm_wide)

    loss, d_lnf, dx3, dx3b = _final_backward(x3, target, sp["ln_final_g"], tm_wide)
    act, dgu, dxn3 = _swiglu_backward(dx3b, gu, w_gu, w_down, tm_wide)
    g_gu, _ = _wgrad_blocked_lhs(dgu.reshape((N_DEV,) + dgu.shape[2:]), xn3, "wgrad_gate_up")
    g_gu = shards(g_gu)
    g_down, ((rc_gu,),) = _wgrad_blocked_lhs(act, dx3b, "wgrad_down", carry=[_exchange_c([g_gu])])
    g_down = shards(g_down)
    keep, pending = {}, []
    (keep["w_gate_up"], send_gu), _ = _rs_combine(g_gu, rc_gu, pos, "rs_combine_w_gate_up")
    started, token = _exchange_start("xy", [send_gu], "exchange_xy_1_start")
    pending.append((("w_gate_up",), started))
    c_down, token = _exchange_start("c", [g_down], "exchange_c_1_start", after=token)
    (dx2b, dq, dx1, dx1b, dkv, d_lnattn, d_lnffn), _ = _attn_backward(
        dx3, dxn3, x2, sp["ln_ffn_g"], x1, sp["ln_attn_g"], q, kv, w_q, w_o, tm, carry=[_follow(token)])
    rc_down, = _exchange_wait(c_down, dx1b, "exchange_c_1_wait")
    (keep["w_down"], send_down), _ = _rs_combine(g_down, rc_down, pos, "rs_combine_w_down")
    g_o, _ = _wgrad(o, dx2b, "wgrad_o")
    g_q, _ = _wgrad(xn2, dq, "wgrad_q")
    g_o, g_q = shards(g_o), shards(g_q)
    c_oq, token = _exchange_start("c", [g_o, g_q], "exchange_c_2_start")
    g_out, _ = _wgrad(ycat, dx1b, "wgrad_out", carry=[_follow(token)])
    g_out = shards(g_out)
    g_kv, d_lnmem = _kv_backward(dkv, memn, mem, sp["ln_mem_g"], w_kv8)
    rc_o, rc_q = _exchange_wait(c_oq, g_out, "exchange_c_2_wait")
    c_outkv, token = _exchange_start("c", [g_out, g_kv], "exchange_c_3_start")
    (keep["w_o"], send_o), _ = _rs_combine(g_o, rc_o, pos, "rs_combine_w_o", carry=[_follow(token)])
    (keep["w_q"], send_q), _ = _rs_combine(g_q, rc_q, pos, "rs_combine_w_q")
    rc_out, rc_kv = _exchange_wait(c_outkv, send_q, "exchange_c_3_wait")
    (keep["w_out"], send_out), _ = _rs_combine(g_out, rc_out, pos, "rs_combine_w_out")
    (keep["w_kv"], send_kv), _ = _rs_combine(g_kv, rc_kv, pos, "rs_combine_w_kv")
    started, token = _exchange_start("xy", [send_down, send_o, send_q, send_out, send_kv], "exchange_xy_2_start")
    pending.append((("w_down", "w_o", "w_q", "w_out", "w_kv"), started))
    (dh, dx, d_ga, d_gb, d_cw, d_lng, d_lnb, d_wsp, d_bs, d_lnmix), _ = _mix_backward(
        dx1, x, sp["ln_mix_g"], h, sp["sgu_ln_g"], sp["sgu_ln_b"], sp["w_spatial"], bt, conv_full,
        sp["grp_norm_a"], sp["grp_norm_b"], w_out, w_in_t, tm, carry=[_follow(token)])
    gs = {"ln_mix_g": d_lnmix, "sgu_ln_g": d_lng, "sgu_ln_b": d_lnb, "w_spatial": d_wsp, "b_spatial": _bias_grad(d_bs),
          "conv_w": d_cw[:3], "grp_norm_a": d_ga, "grp_norm_b": d_gb, "ln_attn_g": d_lnattn, "ln_mem_g": d_lnmem,
          "ln_ffn_g": d_lnffn, "ln_final_g": d_lnf}
    packed, seg = _pack_small(gs, loss)
    g_in, (_, (small_all,)) = _wgrad(dh, xn1, "wgrad_in", carry=[_follow(token), _all_gather([packed])])
    g_in = shards(g_in)
    c_in, token = _exchange_start("c", [g_in], "exchange_c_4_start")
    return dx, keep, pending, (g_in, c_in), token, small_all, seg


def kernel(x, mem, ln_mix_g, w_in, sgu_ln_g, sgu_ln_b, w_spatial, b_spatial, conv_w, grp_norm_a, grp_norm_b, w_out, ln_attn_g, ln_mem_g, w_q, w_kv, w_o, ln_ffn_g, w_gate_up, w_down, ln_final_g, loss_target, m_ln_mix_g, m_w_in, m_sgu_ln_g, m_sgu_ln_b, m_w_spatial, m_b_spatial, m_conv_w, m_grp_norm_a, m_grp_norm_b, m_w_out, m_ln_attn_g, m_ln_mem_g, m_w_q, m_w_kv, m_w_o, m_ln_ffn_g, m_w_gate_up, m_w_down, m_ln_final_g, v_ln_mix_g, v_w_in, v_sgu_ln_g, v_sgu_ln_b, v_w_spatial, v_b_spatial, v_conv_w, v_grp_norm_a, v_grp_norm_b, v_w_out, v_ln_attn_g, v_ln_mem_g, v_w_q, v_w_kv, v_w_o, v_ln_ffn_g, v_w_gate_up, v_w_down, v_ln_final_g):
    order = ["ln_mix_g", "w_in", "sgu_ln_g", "sgu_ln_b", "w_spatial", "b_spatial", "conv_w", "grp_norm_a", "grp_norm_b",
             "w_out", "ln_attn_g", "ln_mem_g", "w_q", "w_kv", "w_o", "ln_ffn_g", "w_gate_up", "w_down", "ln_final_g"]
    W = dict(ln_mix_g=ln_mix_g, w_in=w_in, sgu_ln_g=sgu_ln_g, sgu_ln_b=sgu_ln_b, w_spatial=w_spatial, b_spatial=b_spatial,
             conv_w=conv_w, grp_norm_a=grp_norm_a, grp_norm_b=grp_norm_b, w_out=w_out, ln_attn_g=ln_attn_g,
             ln_mem_g=ln_mem_g, w_q=w_q, w_kv=w_kv, w_o=w_o, ln_ffn_g=ln_ffn_g, w_gate_up=w_gate_up, w_down=w_down,
             ln_final_g=ln_final_g)
    M = dict(ln_mix_g=m_ln_mix_g, w_in=m_w_in, sgu_ln_g=m_sgu_ln_g, sgu_ln_b=m_sgu_ln_b, w_spatial=m_w_spatial,
             b_spatial=m_b_spatial, conv_w=m_conv_w, grp_norm_a=m_grp_norm_a, grp_norm_b=m_grp_norm_b, w_out=m_w_out,
             ln_attn_g=m_ln_attn_g, ln_mem_g=m_ln_mem_g, w_q=m_w_q, w_kv=m_w_kv, w_o=m_w_o, ln_ffn_g=m_ln_ffn_g,
             w_gate_up=m_w_gate_up, w_down=m_w_down, ln_final_g=m_ln_final_g)
    V = dict(ln_mix_g=v_ln_mix_g, w_in=v_w_in, sgu_ln_g=v_sgu_ln_g, sgu_ln_b=v_sgu_ln_b, w_spatial=v_w_spatial,
             b_spatial=v_b_spatial, conv_w=v_conv_w, grp_norm_a=v_grp_norm_a, grp_norm_b=v_grp_norm_b, w_out=v_w_out,
             ln_attn_g=v_ln_attn_g, ln_mem_g=v_ln_mem_g, w_q=v_w_q, w_kv=v_w_kv, w_o=v_w_o, ln_ffn_g=v_ln_ffn_g,
             w_gate_up=v_w_gate_up, w_down=v_w_down, ln_final_g=v_ln_final_g)

    bw = conv_w.shape[1] * N_DEV
    pos = jnp.stack([lax.axis_index("x"), lax.axis_index("y"), lax.axis_index("c")]).astype(jnp.int32)
    me = 4 * pos[0] + 2 * pos[1] + pos[2]

    sp = {nm: (W[nm].reshape(1, -1) if W[nm].ndim == 1 else W[nm]) for nm in SMALL}
    view = lambda a, nm: a.T if nm in TRANSPOSED else a
    wb = {nm: view(W[nm], nm).astype(BF16) for nm in BIG}
    grad_x, keep, pending, (g_in, c_in), token, small_all, seg = _step(
        x[0], mem[0], loss_target[0], wb, conv_w, sp, pos)

    out = {}

    def update(k, names, started, token):
        landed = _exchange_wait(started, token, "exchange_xy_%d_wait" % k)
        for nm, rxy in zip(names, landed):
            res = _adamw_shard(keep[nm], rxy, view(W[nm], nm), view(M[nm], nm), view(V[nm], nm), "adamw_" + nm)
            out[nm] = tuple(view(a, nm) for a in res)
            token = res[0]
        return token

    token = update(1, *pending[0], token)
    rc_in, = _exchange_wait(c_in, token, "exchange_c_4_wait")
    (keep["w_in"], send_in), _ = _rs_combine(g_in, rc_in, pos, "rs_combine_w_in")
    xy_in, token = _exchange_start("xy", [send_in], "exchange_xy_3_start")
    token = update(2, *pending[1], token)
    update(3, ("w_in",), xy_in, token)

    params = {nm: (_rows128(W[nm]), _rows128(M[nm]), _rows128(V[nm])) for nm in SMALL}
    per, conv_g_rows, loss_sum = _adamw_small(small_all, seg, params, seg["conv_w"])
    for nm in SMALL:
        out[nm] = tuple(a.reshape(W[nm].shape) for a in per[nm])
    conv_g = lax.dynamic_slice_in_dim(conv_g_rows.reshape(3, bw), me * conv_w.shape[1], conv_w.shape[1], axis=1)
    out["conv_w"] = (conv_g,) + tuple(_adamw_one(conv_w, conv_g, m_conv_w, v_conv_w, "adamw_conv"))

    loss = loss_sum[0, 0]
    res = [loss, grad_x[None]]
    for k in range(4):
        res += [out[nm][k] for nm in order]
    return tuple(res)
```

```python
import functools

import jax
import jax.numpy as jnp
from jax import lax
from jax.experimental import pallas as pl
from jax.experimental.pallas import tpu as pltpu

F32 = jnp.float32
BF16 = jnp.bfloat16
SDS = jax.ShapeDtypeStruct
MESH = pl.DeviceIdType.MESH

EPS = 1e-6
N_DEV = 8
HEADS = 4
CHUNK = 128
HALO = 16
SUB = 8
LANES = 128
TOKEN_TILE = 512
ROW_CHUNK = 256
RELAY_AT = 0.7

ADAM_LR = 0.001
ADAM_B1 = 0.9
ADAM_B2 = 0.999
ADAM_EPS = 1e-08
ADAM_WD = 0.01
ADAM_STEP = 10

BIG = ("w_in", "w_out", "w_q", "w_kv", "w_o", "w_gate_up", "w_down")
TRANSPOSED = ("w_in", "w_gate_up")
SMALL = ("ln_mix_g", "sgu_ln_g", "sgu_ln_b", "w_spatial", "b_spatial", "grp_norm_a", "grp_norm_b",
         "ln_attn_g", "ln_mem_g", "ln_ffn_g", "ln_final_g")


class _Exchange:
    def __init__(self, ins, out_shape, sems, start, finish, relay=None):
        self.ins, self.out_shape, self.sems = list(ins), list(out_shape), list(sems)
        self.start, self.finish, self.relay = start, finish, relay


def _pcall(body, carry=(), n_prefetch=0, **kw):
    if carry:
        return functools.partial(_carrying_call, body, tuple(carry), n_prefetch, kw)
    if n_prefetch:
        kw["grid_spec"] = pltpu.PrefetchScalarGridSpec(
            num_scalar_prefetch=n_prefetch, grid=kw.pop("grid"), in_specs=kw.pop("in_specs"),
            out_specs=kw.pop("out_specs"), scratch_shapes=kw.pop("scratch_shapes", ()))
    return pl.pallas_call(body, **kw)


def _carrying_call(body, carry, n_prefetch, kw, *args):
    kw = dict(kw)
    out_shape = kw.pop("out_shape")
    single = not isinstance(out_shape, (tuple, list))
    outs_shape = (out_shape,) if single else tuple(out_shape)
    out_specs = kw.pop("out_specs")
    out_specs = [out_specs] if single else list(out_specs)
    in_specs = list(kw.pop("in_specs"))
    scratch = list(kw.pop("scratch_shapes", ()))
    grid = tuple(kw.get("grid", ()))
    n_in, n_out, n_scr = len(args), len(outs_shape), len(scratch)

    def split(refs, k, counts):
        parts = []
        for cnt in counts:
            parts.append(refs[k:k + cnt])
            k += cnt
        return parts, k

    def wrapped(*refs):
        cins, k = split(refs, n_in, [len(p.ins) for p in carry])
        outs = refs[k:k + n_out]
        couts, k = split(refs, k + n_out, [len(p.out_shape) for p in carry])
        scr = refs[k:k + n_scr]
        csems, _ = split(refs, k + n_scr, [len(p.sems) for p in carry])
        first, last = True, True
        for a, g in enumerate(grid):
            first = (pl.program_id(a) == 0) & first
            last = (pl.program_id(a) == g - 1) & last

        def start_all():
            for p, ci, co, cs in zip(carry, cins, couts, csems):
                p.start(ci, co, cs)

        def relay_all():
            for p, ci, co, cs in zip(carry, cins, couts, csems):
                if p.relay is not None:
                    p.relay(ci, co, cs)

        def finish_all():
            for p, ci, co, cs in zip(carry, cins, couts, csems):
                p.finish(ci, co, cs)

        if len(grid) == 1:
            relay_now = pl.program_id(0) == min(int(RELAY_AT * grid[0]), grid[0] - 1)
        else:
            relay_now = last
        start_all() if not grid else pl.when(first)(start_all)
        relay_all() if not grid else pl.when(relay_now)(relay_all)
        body(*refs[:n_in], *outs, *scr)
        finish_all() if not grid else pl.when(last)(finish_all)

    c_in = [a for p in carry for a in p.ins]
    c_out = [s for p in carry for s in p.out_shape]
    c_sems = [s for p in carry for s in p.sems]
    res = _pcall(wrapped, n_prefetch=n_prefetch, out_shape=outs_shape + tuple(c_out),
                 in_specs=in_specs + _hbm_specs(len(c_in)), out_specs=out_specs + _hbm_specs(len(c_out)),
                 scratch_shapes=scratch + c_sems, **kw)(*args, *c_in)
    own = res[0] if single else tuple(res[:n_out])
    landed, k = [], n_out
    for p in carry:
        landed.append(list(res[k:k + len(p.out_shape)]))
        k += len(p.out_shape)
    return own, landed


def _hbm_specs(n):
    return [pl.BlockSpec(memory_space=pl.ANY)] * n


def _hosted(body, carry, **kw):
    if carry:
        return _pcall(body, carry=carry, **kw)
    call = _pcall(body, **kw)
    return lambda *args: (call(*args), [])


def _run_exchanges(parts, name):
    def body(*refs):
        pass

    _, landed = _pcall(body, carry=parts, out_shape=(), in_specs=[], out_specs=[], name=name)()
    return landed


def _arb(n):
    return pltpu.CompilerParams(dimension_semantics=("arbitrary",) * n)


def _tile(n, target, mult):
    best = None
    for t in range(mult, min(n, target) + 1, mult):
        if n % t == 0:
            best = t
    return n if best is None else best


def _round_up(n, m):
    return (n + m - 1) // m * m


def _dot(a, b):
    return jnp.dot(a, b, preferred_element_type=F32)


def _dot_nt(a, b):
    return lax.dot_general(a, b, (((1,), (1,)), ((), ())), preferred_element_type=F32)


def _dot_tn(a, b):
    return lax.dot_general(a, b, (((0,), (0,)), ((), ())), preferred_element_type=F32)


def _rstd(x):
    return lax.rsqrt(jnp.mean(x * x, axis=-1, keepdims=True) + EPS)


def _rms_bwd(dy, x, r, g):
    gdy = dy * g
    proj = jnp.sum(gdy * x, axis=-1, keepdims=True) * (1.0 / x.shape[-1])
    dx = r * gdy - x * (r * r * r) * proj
    dg = jnp.sum(dy * (x * r), axis=0, keepdims=True)
    return dx, dg


_GELU_C = 0.7978845608028654
_GELU_A = 0.044715


def _gelu(x):
    t = jnp.tanh(_GELU_C * (x + _GELU_A * x * x * x))
    return 0.5 * x * (1.0 + t), t


def _gelu_grad(x, t):
    return 0.5 * (1.0 + t) + 0.5 * x * (1.0 - t * t) * (_GELU_C * (1.0 + 3.0 * _GELU_A * x * x))


def _sigmoid(x):
    return 1.0 / (1.0 + jnp.exp(-x))


def _softmax(s):
    m = jnp.max(s, axis=-1, keepdims=True)
    e = jnp.exp(s - m)
    return e / jnp.sum(e, axis=-1, keepdims=True)


def _adamw(w, g, m, v):
    m = ADAM_B1 * m + (1.0 - ADAM_B1) * g
    v = ADAM_B2 * v + (1.0 - ADAM_B2) * (g * g)
    m_hat = m / (1.0 - ADAM_B1 ** ADAM_STEP)
    v_hat = v / (1.0 - ADAM_B2 ** ADAM_STEP)
    delta = -ADAM_LR * (m_hat / (jnp.sqrt(v_hat) + ADAM_EPS) + ADAM_WD * w)
    return delta, m, v


def _tril_mask():
    t = lax.broadcasted_iota(jnp.int32, (CHUNK, CHUNK), 0)
    s = lax.broadcasted_iota(jnp.int32, (CHUNK, CHUNK), 1)
    return (s <= t).astype(F32)


def _sgu_forward(ha, lng, lnb, wm, bt, mixed_s):
    aw = ha.shape[1] // 2
    hd = aw // HEADS
    a, th = _gelu(ha)
    u = a[:, :aw]
    v = a[:, aw:]
    mu = jnp.mean(v, axis=-1, keepdims=True)
    vc = v - mu
    rl = lax.rsqrt(jnp.mean(vc * vc, axis=-1, keepdims=True) + EPS)
    xhat = vc * rl
    vln = (xhat * lng + lnb).astype(BF16)
    for n in range(ha.shape[0] // CHUNK):
        rows = slice(n * CHUNK, (n + 1) * CHUNK)
        for h in range(HEADS):
            cols = slice(h * hd, (h + 1) * hd)
            mixed_s[rows, cols] = _dot(wm[h], vln[rows, cols]) + bt[:, h:h + 1]
    return th, u, xhat, rl, vln


def _conv_taps(zext):
    return pltpu.roll(zext, 2, 0), pltpu.roll(zext, 1, 0)


def _kv_forward(mem, g_mem, w_kv):
    ml, d = mem.shape
    xd = w_kv.shape[2]

    def body(mem_ref, g_ref, w_ref, memn_ref, kv_ref):
        x = mem_ref[...]
        memn = (x * _rstd(x) * g_ref[...]).astype(BF16)
        memn_ref[...] = memn
        for j in range(2 * HEADS):
            kv_ref[j] = _dot(memn, w_ref[j]).astype(BF16)

    return _pcall(body, out_shape=(SDS((ml, d), BF16), SDS((2 * HEADS, ml, xd), BF16)), name="kv_forward")(mem, g_mem, w_kv)


def _in_forward(x, g, w_in_t, tm, carry=()):
    s, d = x.shape
    n_in = w_in_t.shape[0]

    def body(x_ref, g_ref, w_ref, xn_ref, h_ref):
        xv = x_ref[...]
        xn = (xv * _rstd(xv) * g_ref[...]).astype(BF16)
        xn_ref[...] = xn
        h_ref[...] = _dot_nt(xn, w_ref[...])

    return _hosted(
        body, carry, grid=(s // tm,),
        in_specs=[pl.BlockSpec((tm, d), lambda i: (i, 0)), pl.BlockSpec((1, d), lambda i: (0, 0)),
                  pl.BlockSpec((n_in, d), lambda i: (0, 0))],
        out_specs=[pl.BlockSpec((tm, d), lambda i: (i, 0)), pl.BlockSpec((tm, n_in), lambda i: (i, 0))],
        out_shape=(SDS((s, d), BF16), SDS((s, n_in), F32)),
        compiler_params=_arb(1), name="in_forward")(x, g, w_in_t)


def _mix_forward(h, x, lng, lnb, w_sp, bt, conv_w, ga, gb, w_out, tm, carry=()):
    s, d = x.shape
    n_in = h.shape[1]
    aw = lng.shape[1]
    bw = d - aw
    in_a = 2 * aw
    hb_blocks = tm // HALO

    def body(h_ref, hprev_ref, x_ref, lng_ref, lnb_ref, wsp_ref, bt_ref, cw_ref, ga_ref, gb_ref, wout_ref,
             ycat_ref, x1_ref, mixed_s):
        i = pl.program_id(0)
        mask = _tril_mask()
        wm = [(wsp_ref[hh] * mask).astype(BF16) for hh in range(HEADS)]
        hv = h_ref[...]
        _, u, _, _, _ = _sgu_forward(hv[:, :in_a], lng_ref[...], lnb_ref[...], wm, bt_ref[...], mixed_s)
        sg = u * mixed_s[...]
        ycat_ref[:, :aw] = (sg * _rstd(sg) * ga_ref[...]).astype(BF16)

        gate_b = hv[:, in_a:in_a + bw]
        z = hv[:, in_a + bw:in_a + 2 * bw] * hv[:, in_a + 2 * bw:]
        hp = hprev_ref[...]
        zp = hp[:, in_a + bw:in_a + 2 * bw] * hp[:, in_a + 2 * bw:]
        zp = jnp.where(i == 0, 0.0, zp)
        zext = jnp.concatenate([zp, z], axis=0)
        z2, z1 = _conv_taps(zext)
        cw = cw_ref[...]
        conv = cw[0:1] * z2[HALO:] + cw[1:2] * z1[HALO:] + cw[2:3] * z
        sc = gate_b * conv
        ycat_ref[:, aw:] = (sc * _rstd(sc) * gb_ref[...]).astype(BF16)
        x1_ref[...] = x_ref[...] + _dot(ycat_ref[...], wout_ref[...])

    full = lambda shape: pl.BlockSpec(shape, lambda i: (0,) * len(shape))
    return _hosted(
        body, carry, grid=(s // tm,),
        in_specs=[pl.BlockSpec((tm, n_in), lambda i: (i, 0)),
                  pl.BlockSpec((HALO, n_in), lambda i: (jnp.maximum(i * hb_blocks - 1, 0), 0)),
                  pl.BlockSpec((tm, d), lambda i: (i, 0)),
                  full((1, aw)), full((1, aw)), full((HEADS, CHUNK, CHUNK)), full((CHUNK, HEADS)),
                  full((3, bw)), full((1, aw)), full((1, bw)), full((d, d))],
        out_specs=[pl.BlockSpec((tm, d), lambda i: (i, 0)), pl.BlockSpec((tm, d), lambda i: (i, 0))],
        out_shape=(SDS((s, d), BF16), SDS((s, d), F32)),
        scratch_shapes=[pltpu.VMEM((tm, aw), F32)],
        compiler_params=_arb(1), name="mix_forward")(h, h, x, lng, lnb, w_sp, bt, conv_w, ga, gb, w_out)


def _attn_forward(x1, g, w_q, kv, w_o, tm, carry=()):
    s, d = x1.shape
    _, ml, xd = kv.shape
    scale = xd ** -0.5

    def body(x1_ref, g_ref, wq_ref, kv_ref, wo_ref, xn_ref, q_ref, o_ref, x2_ref):
        xv = x1_ref[...]
        xn = (xv * _rstd(xv) * g_ref[...]).astype(BF16)
        xn_ref[...] = xn
        q_ref[...] = _dot(xn, wq_ref[...]).astype(BF16)
        for hh in range(HEADS):
            cols = slice(hh * xd, (hh + 1) * xd)
            p = _softmax(_dot_nt(q_ref[:, cols], kv_ref[hh]) * scale)
            o_ref[:, cols] = _dot(p.astype(BF16), kv_ref[HEADS + hh]).astype(BF16)
        x2_ref[...] = xv + _dot(o_ref[...], wo_ref[...])

    tok = pl.BlockSpec((tm, d), lambda i: (i, 0))
    return _hosted(
        body, carry, grid=(s // tm,),
        in_specs=[tok, pl.BlockSpec((1, d), lambda i: (0, 0)), pl.BlockSpec((d, d), lambda i: (0, 0)),
                  pl.BlockSpec((2 * HEADS, ml, xd), lambda i: (0, 0, 0)), pl.BlockSpec((d, d), lambda i: (0, 0))],
        out_specs=[tok, tok, tok, tok],
        out_shape=(SDS((s, d), BF16), SDS((s, d), BF16), SDS((s, d), BF16), SDS((s, d), F32)),
        compiler_params=_arb(1), name="attn_forward")(x1, g, w_q, kv, w_o)


def _ffn_forward(x2, g, w_gu, w_down, tm):
    s, d = x2.shape
    _, nf, tf, _ = w_gu.shape

    def body(x2_ref, g_ref, wgu_ref, wd_ref, xn_ref, gu_ref, x3_ref):
        f = pl.program_id(1)

        @pl.when(f == 0)
        def _():
            xv = x2_ref[...]
            xn_ref[...] = (xv * _rstd(xv) * g_ref[...]).astype(BF16)
            x3_ref[...] = xv

        xn = xn_ref[...]
        gate = _dot_nt(xn, wgu_ref[0])
        up = _dot_nt(xn, wgu_ref[1])
        gu_ref[0] = gate.astype(BF16)
        gu_ref[1] = up.astype(BF16)
        act = (gate * _sigmoid(gate) * up).astype(BF16)
        x3_ref[...] += _dot(act, wd_ref[...])

    tok = pl.BlockSpec((tm, d), lambda i, f: (i, 0))
    return _pcall(
        body, grid=(s // tm, nf),
        in_specs=[tok, pl.BlockSpec((1, d), lambda i, f: (0, 0)),
                  pl.BlockSpec((2, None, tf, d), lambda i, f: (0, f, 0, 0)),
                  pl.BlockSpec((tf, d), lambda i, f: (f, 0))],
        out_specs=[tok, pl.BlockSpec((2, None, tm, tf), lambda i, f: (0, f, i, 0)), tok],
        out_shape=(SDS((s, d), BF16), SDS((2, nf, s, tf), BF16), SDS((s, d), F32)),
        compiler_params=_arb(2), name="ffn_forward")(x2, g, w_gu, w_down)


def _final_backward(x3, target, g_final, tm):
    s, d = x3.shape

    def body(x3_ref, tgt_ref, gf_ref, loss_ref, dgf_ref, dx3_ref, dx3b_ref):
        @pl.when(pl.program_id(0) == 0)
        def _():
            loss_ref[...] = jnp.zeros_like(loss_ref)
            dgf_ref[...] = jnp.zeros_like(dgf_ref)

        xv = x3_ref[...]
        r = _rstd(xv)
        diff = xv * r * gf_ref[...] - tgt_ref[...]
        loss_ref[...] += 0.5 * jnp.sum(jnp.sum(diff * diff, axis=-1, keepdims=True), axis=0, keepdims=True) * (1.0 / d)
        dx3, dgf = _rms_bwd(diff * (1.0 / d), xv, r, gf_ref[...])
        dgf_ref[...] += dgf
        dx3_ref[...] = dx3
        dx3b_ref[...] = dx3.astype(BF16)

    tok = pl.BlockSpec((tm, d), lambda i: (i, 0))
    vec = pl.BlockSpec((1, d), lambda i: (0, 0))
    return _pcall(
        body, grid=(s // tm,), in_specs=[tok, tok, vec],
        out_specs=[pl.BlockSpec((SUB, LANES), lambda i: (0, 0)), vec, tok, tok],
        out_shape=(SDS((SUB, LANES), F32), SDS((1, d), F32), SDS((s, d), F32), SDS((s, d), BF16)),
        compiler_params=_arb(1), name="final_backward")(x3, target, g_final)


def _swiglu_backward(dx3b, gu, w_gu, w_down, tm):
    s, d = dx3b.shape
    _, nf, tf, _ = w_gu.shape

    def body(dx3b_ref, gu_ref, wgu_ref, wd_ref, act_ref, dgu_ref, dxn_ref):
        @pl.when(pl.program_id(1) == 0)
        def _():
            dxn_ref[...] = jnp.zeros_like(dxn_ref)

        for r0 in range(0, tm, ROW_CHUNK):
            rows = slice(r0, r0 + ROW_CHUNK)
            dact = _dot_nt(dx3b_ref[rows, :], wd_ref[...])
            gv = gu_ref[0, rows, :].astype(F32)
            uv = gu_ref[1, rows, :].astype(F32)
            sg = _sigmoid(gv)
            silu = gv * sg
            act_ref[rows, :] = (silu * uv).astype(BF16)
            dgate = (dact * uv * (sg * (1.0 + gv * (1.0 - sg)))).astype(BF16)
            dup = (dact * silu).astype(BF16)
            dgu_ref[0, rows, :] = dgate
            dgu_ref[1, rows, :] = dup
            part = _dot(dgate, wgu_ref[0]) + _dot(dup, wgu_ref[1])
            dxn_ref[rows, :] += part

    tok = pl.BlockSpec((tm, d), lambda i, f: (i, 0))
    pair = pl.BlockSpec((2, None, tm, tf), lambda i, f: (0, f, i, 0))
    return _pcall(
        body, grid=(s // tm, nf),
        in_specs=[tok, pair, pl.BlockSpec((2, None, tf, d), lambda i, f: (0, f, 0, 0)),
                  pl.BlockSpec((tf, d), lambda i, f: (f, 0))],
        out_specs=[pl.BlockSpec((None, tm, tf), lambda i, f: (f, i, 0)), pair, tok],
        out_shape=(SDS((nf, s, tf), BF16), SDS((2, nf, s, tf), BF16), SDS((s, d), F32)),
        compiler_params=_arb(2), name="swiglu_backward")(dx3b, gu, w_gu, w_down)


def _attn_backward(dx3, dxn3, x2, g_ffn, x1, g, q, kv, w_q, w_o, tm, carry=()):
    s, d = x1.shape
    _, ml, xd = kv.shape
    scale = xd ** -0.5

    def body(dx3_ref, dxn3_ref, x2_ref, g2_ref, x1_ref, g_ref, q_ref, kv_ref, wq_ref, wo_ref,
             dx2b_ref, dq_ref, dx1_ref, dx1b_ref, dkv_ref, dg_ref, dg2_ref, do_s):
        i = pl.program_id(0)

        @pl.when(i == 0)
        def _():
            dkv_ref[...] = jnp.zeros_like(dkv_ref)
            dg_ref[...] = jnp.zeros_like(dg_ref)
            dg2_ref[...] = jnp.zeros_like(dg2_ref)

        x2v = x2_ref[...]
        dx2n, dg2 = _rms_bwd(dxn3_ref[...], x2v, _rstd(x2v), g2_ref[...])
        dg2_ref[...] += dg2
        dx2 = dx3_ref[...] + dx2n
        dx2b_ref[...] = dx2.astype(BF16)
        do_s[...] = _dot_nt(dx2b_ref[...], wo_ref[...]).astype(BF16)
        for hh in range(HEADS):
            kc = slice(hh * xd, (hh + 1) * xd)
            qh = q_ref[:, kc]
            kh = kv_ref[hh]
            doh = do_s[:, kc]
            p = _softmax(_dot_nt(qh, kh) * scale)
            dp = _dot_nt(doh, kv_ref[HEADS + hh])
            dkv_ref[HEADS + hh] += _dot_tn(p.astype(BF16), doh)
            ds = (p * (dp - jnp.sum(dp * p, axis=-1, keepdims=True)) * scale).astype(BF16)
            dq_ref[:, kc] = _dot(ds, kh).astype(BF16)
            dkv_ref[hh] += _dot_tn(ds, qh)
        dg_tile = jnp.zeros_like(dg_ref)
        for r0 in range(0, tm, min(ROW_CHUNK, tm)):
            rows = slice(r0, r0 + min(ROW_CHUNK, tm))
            dxn = _dot_nt(dq_ref[rows, :], wq_ref[...])
            xv = x1_ref[rows, :]
            dx, dg = _rms_bwd(dxn, xv, _rstd(xv), g_ref[...])
            dg_tile = dg_tile + dg
            dx1 = dx2[rows] + dx
            dx1_ref[rows, :] = dx1
            dx1b_ref[rows, :] = dx1.astype(BF16)
        dg_ref[...] += dg_tile

    tok = pl.BlockSpec((tm, d), lambda i: (i, 0))
    vec = pl.BlockSpec((1, d), lambda i: (0, 0))
    sq = pl.BlockSpec((d, d), lambda i: (0, 0))
    kvs = pl.BlockSpec((2 * HEADS, ml, xd), lambda i: (0, 0, 0))
    return _hosted(
        body, carry, grid=(s // tm,),
        in_specs=[tok, tok, tok, vec, tok, vec, tok, kvs, sq, sq],
        out_specs=[tok, tok, tok, tok, kvs, vec, vec],
        out_shape=(SDS((s, d), BF16), SDS((s, d), BF16), SDS((s, d), F32), SDS((s, d), BF16),
                   SDS((2 * HEADS, ml, xd), F32), SDS((1, d), F32), SDS((1, d), F32)),
        scratch_shapes=[pltpu.VMEM((tm, d), BF16)],
        compiler_params=_arb(1), name="attn_backward")(dx3, dxn3, x2, g_ffn, x1, g, q, kv, w_q, w_o)


def _kv_backward(dkv, memn, mem, g_mem, w_kv):
    ml, d = mem.shape
    xd = w_kv.shape[2]

    def body(dkv_ref, memn_ref, mem_ref, g_ref, w_ref, dw_ref, dg_ref):
        dmemn = jnp.zeros((ml, d), F32)
        for j in range(2 * HEADS):
            dkvb = dkv_ref[j].astype(BF16)
            dw_ref[j] = _dot_tn(memn_ref[...], dkvb)
            dmemn = dmemn + _dot_nt(dkvb, w_ref[j])
        x = mem_ref[...]
        dg_ref[...] = jnp.sum(dmemn * (x * _rstd(x)), axis=0, keepdims=True)

    return _pcall(body, out_shape=(SDS((2 * HEADS, d, xd), F32), SDS((1, d), F32)), name="kv_backward")(dkv, memn, mem, g_mem, w_kv)


def _mix_backward(dx1, x, g_mix, h, lng, lnb, w_sp, bt, conv_w, ga, gb, w_out, w_in, tm, carry=()):
    s, d = x.shape
    n_in = h.shape[1]
    aw = lng.shape[1]
    bw = d - aw
    hd = aw // HEADS
    in_a = 2 * aw
    hb_blocks = tm // HALO
    last_blk = s // HALO - 1
    nt = s // tm
    te = tm + HALO
    tee = tm + 2 * HALO

    def body(dx1_ref, dx1n_ref, x_ref, gm_ref, h_ref, hp_ref, hn_ref, lng_ref, lnb_ref, wsp_ref, bt_ref, cw_ref,
             ga_ref, gb_ref, wout_ref, win_ref,
             dh_ref, dx_ref, dga_ref, dgb_ref, dcw_ref, dlng_ref, dlnb_ref, dwsp_ref, dbs_ref, dgm_ref,
             mixed_s, dvln_s):
        i = pl.program_id(0)

        @pl.when(i == 0)
        def _():
            for ref in (dga_ref, dgb_ref, dcw_ref, dlng_ref, dlnb_ref, dwsp_ref, dbs_ref, dgm_ref):
                ref[...] = jnp.zeros_like(ref)

        mask = _tril_mask()
        wm = [(wsp_ref[hh] * mask).astype(BF16) for hh in range(HEADS)]
        hv = h_ref[...]
        dx1 = dx1_ref[...]
        dx1e = jnp.concatenate([dx1, dx1n_ref[...]], axis=0).astype(BF16)
        dycat = _dot_nt(dx1e, wout_ref[...])

        hbe = jnp.concatenate([hp_ref[:, in_a:], hv[:, in_a:], hn_ref[:, in_a:]], axis=0)
        row = lax.broadcasted_iota(jnp.int32, (tee, 1), 0)
        zext = hbe[:, bw:2 * bw] * hbe[:, 2 * bw:]
        zext = jnp.where((i == 0) & (row < HALO), 0.0, zext)
        z2e, z1e = _conv_taps(zext)
        cw = cw_ref[...]
        conv_e = (cw[0:1] * z2e + cw[1:2] * z1e + cw[2:3] * zext)[HALO:]
        gate_b_e = hbe[HALO:, :bw]
        sc_e = gate_b_e * conv_e
        rb = _rstd(sc_e)
        dyb = dycat[:, aw:]
        gdy = dyb * gb_ref[...]
        dsc_e = rb * gdy - sc_e * (rb * rb * rb) * (jnp.sum(gdy * sc_e, axis=-1, keepdims=True) * (1.0 / bw))
        dgb_ref[...] += jnp.sum((dyb * (sc_e * rb))[:tm], axis=0, keepdims=True)
        dconv_e = dsc_e * gate_b_e
        dconv_e = jnp.where((i == nt - 1) & (row[:te] >= tm), 0.0, dconv_e)
        dconv = dconv_e[:tm]
        dc1 = pltpu.roll(dconv_e, te - 1, 0)[:tm]
        dc2 = pltpu.roll(dconv_e, te - 2, 0)[:tm]
        dz = cw[2:3] * dconv + cw[1:2] * dc1 + cw[0:1] * dc2
        z = zext[HALO:HALO + tm]
        z1 = z1e[HALO:HALO + tm]
        z2 = z2e[HALO:HALO + tm]
        dcw_ref[0:1, :] += jnp.sum(dconv * z2, axis=0, keepdims=True)
        dcw_ref[1:2, :] += jnp.sum(dconv * z1, axis=0, keepdims=True)
        dcw_ref[2:3, :] += jnp.sum(dconv * z, axis=0, keepdims=True)
        dh_ref[:, in_a:in_a + bw] = (dsc_e[:tm] * conv_e[:tm]).astype(BF16)
        dh_ref[:, in_a + bw:in_a + 2 * bw] = (dz * hv[:, in_a + 2 * bw:]).astype(BF16)
        dh_ref[:, in_a + 2 * bw:] = (dz * hv[:, in_a + bw:in_a + 2 * bw]).astype(BF16)

        ha = hv[:, :in_a]
        th, u, xhat, rl, vln = _sgu_forward(ha, lng_ref[...], lnb_ref[...], wm, bt_ref[...], mixed_s)
        mixed = mixed_s[...]
        sg = u * mixed
        dsg, dga = _rms_bwd(dycat[:tm, :aw], sg, _rstd(sg), ga_ref[...])
        dga_ref[...] += dga
        du = dsg * mixed
        dmixed = dsg * u
        dmb = dmixed.astype(BF16)
        for n in range(tm // CHUNK):
            rows = slice(n * CHUNK, (n + 1) * CHUNK)
            dbs_ref[...] += dmixed[rows]
            for hh in range(HEADS):
                cols = slice(hh * hd, (hh + 1) * hd)
                dvln_s[rows, cols] = _dot_tn(wm[hh], dmb[rows, cols])
                dwsp_ref[hh] += mask * _dot_nt(dmb[rows, cols], vln[rows, cols])
        dvln = dvln_s[...]
        dlng_ref[...] += jnp.sum(dvln * xhat, axis=0, keepdims=True)
        dlnb_ref[...] += jnp.sum(dvln, axis=0, keepdims=True)
        dxh = dvln * lng_ref[...]
        dv = rl * (dxh - jnp.mean(dxh, axis=-1, keepdims=True) - xhat * jnp.mean(dxh * xhat, axis=-1, keepdims=True))
        dh_ref[:, :in_a] = (jnp.concatenate([du, dv], axis=-1) * _gelu_grad(ha, th)).astype(BF16)

        dgm_tile = jnp.zeros_like(dgm_ref)
        for r0 in range(0, tm, min(ROW_CHUNK, tm)):
            rows = slice(r0, r0 + min(ROW_CHUNK, tm))
            dxn = _dot(dh_ref[rows, :], win_ref[...])
            xv = x_ref[rows, :]
            dx, dgm = _rms_bwd(dxn, xv, _rstd(xv), gm_ref[...])
            dgm_tile = dgm_tile + dgm
            dx_ref[rows, :] = dx1_ref[rows, :] + dx
        dgm_ref[...] += dgm_tile

    full = lambda shape: pl.BlockSpec(shape, lambda i: (0,) * len(shape))
    tok = pl.BlockSpec((tm, d), lambda i: (i, 0))
    nxt = lambda i: (jnp.minimum((i + 1) * hb_blocks, last_blk), 0)
    prv = lambda i: (jnp.maximum(i * hb_blocks - 1, 0), 0)
    return _hosted(
        body, carry, grid=(nt,),
        in_specs=[tok, pl.BlockSpec((HALO, d), nxt), tok, full((1, d)),
                  pl.BlockSpec((tm, n_in), lambda i: (i, 0)), pl.BlockSpec((HALO, n_in), prv), pl.BlockSpec((HALO, n_in), nxt),
                  full((1, aw)), full((1, aw)), full((HEADS, CHUNK, CHUNK)), full((CHUNK, HEADS)), full((3, bw)),
                  full((1, aw)), full((1, bw)), full((d, d)), full((n_in, d))],
        out_specs=[pl.BlockSpec((tm, n_in), lambda i: (i, 0)), tok,
                   full((1, aw)), full((1, bw)), full((SUB, bw)), full((1, aw)), full((1, aw)),
                   full((HEADS, CHUNK, CHUNK)), full((CHUNK, aw)), full((1, d))],
        out_shape=(SDS((s, n_in), BF16), SDS((s, d), F32),
                   SDS((1, aw), F32), SDS((1, bw), F32), SDS((SUB, bw), F32), SDS((1, aw), F32), SDS((1, aw), F32),
                   SDS((HEADS, CHUNK, CHUNK), F32), SDS((CHUNK, aw), F32), SDS((1, d), F32)),
        scratch_shapes=[pltpu.VMEM((tm, aw), F32), pltpu.VMEM((tm, aw), F32)],
        compiler_params=_arb(1), name="mix_backward")(dx1, dx1, x, g_mix, h, h, h, lng, lnb, w_sp, bt, conv_w, ga, gb, w_out, w_in)


def _bias_grad(dbs):
    aw = dbs.shape[1]
    hd = aw // HEADS

    def body(dbs_ref, out_ref):
        ones = jnp.ones((SUB, hd), F32)
        for hh in range(HEADS):
            r = lax.dot_general(ones, dbs_ref[:, hh * hd:(hh + 1) * hd], (((1,), (1,)), ((), ())),
                                precision=lax.Precision.HIGHEST, preferred_element_type=F32)
            out_ref[hh:hh + 1, :] = r[0:1]

    return _pcall(body, out_shape=SDS((HEADS, CHUNK), F32), name="bias_grad")(dbs)


def _wgrad_body(a_ref, b_ref, o_ref):
    o_ref[...] = _dot_tn(a_ref[...], b_ref[...])


def _wgrad(a, b, name, carry=()):
    k, m = a.shape
    n = b.shape[1]
    tm = _tile(m, 512, LANES)
    tn = _tile(n, 1024, LANES)
    return _hosted(
        functools.partial(_wgrad_body), carry, grid=(m // tm, n // tn),
        in_specs=[pl.BlockSpec((k, tm), lambda i, j: (0, i)), pl.BlockSpec((k, tn), lambda i, j: (0, j))],
        out_specs=pl.BlockSpec((tm, tn), lambda i, j: (i, j)),
        out_shape=SDS((m, n), F32), compiler_params=_arb(2), name=name)(a, b)


def _wgrad_blocked_lhs(a, b, name, carry=()):
    nb, k, t = a.shape
    n = b.shape[1]
    tn = _tile(n, 1024, LANES)
    return _hosted(
        functools.partial(_wgrad_body), carry, grid=(nb, n // tn),
        in_specs=[pl.BlockSpec((None, k, t), lambda i, j: (i, 0, 0)), pl.BlockSpec((k, tn), lambda i, j: (0, j))],
        out_specs=pl.BlockSpec((t, tn), lambda i, j: (i, j)),
        out_shape=SDS((nb * t, n), F32), compiler_params=_arb(2), name=name)(a, b)


def _wgrad_blocked_rhs(a, b, name, carry=()):
    k, m = a.shape
    nb, _, t = b.shape
    tm = _tile(m, 512, LANES)
    return _hosted(
        functools.partial(_wgrad_body), carry, grid=(m // tm, nb),
        in_specs=[pl.BlockSpec((k, tm), lambda i, j: (0, i)), pl.BlockSpec((None, k, t), lambda i, j: (j, 0, 0))],
        out_specs=pl.BlockSpec((None, tm, t), lambda i, j: (j, i, 0)),
        out_shape=SDS((nb, m, t), F32), compiler_params=_arb(2), name=name)(a, b)


def _unblock_cols(wb, name, carry=()):
    nb, r, t = wb.shape
    tr = _tile(r, 256, 16)

    def body(w_ref, o_ref):
        o_ref[...] = jnp.concatenate([w_ref[j].astype(F32) for j in range(nb)], axis=-1).astype(o_ref.dtype)

    return _hosted(
        body, carry, grid=(r // tr,),
        in_specs=[pl.BlockSpec((nb, tr, t), lambda i: (0, i, 0))], out_specs=pl.BlockSpec((tr, nb * t), lambda i: (i, 0)),
        out_shape=SDS((r, nb * t), wb.dtype), compiler_params=_arb(1), name=name)(wb)


def _block_cols(w, nb, name, carry=()):
    r, n = w.shape
    t = n // nb
    tr = _tile(r, 256, 16)

    def body(w_ref, o_ref):
        wv = w_ref[...]
        for j in range(nb):
            o_ref[j] = wv[:, j * t:(j + 1) * t]

    return _hosted(
        body, carry, grid=(r // tr,),
        in_specs=[pl.BlockSpec((tr, n), lambda i: (i, 0))], out_specs=pl.BlockSpec((nb, tr, t), lambda i: (0, i, 0)),
        out_shape=SDS((nb, r, t), w.dtype), compiler_params=_arb(1), name=name)(w)


def _place():
    x, y, c = lax.axis_index("x"), lax.axis_index("y"), lax.axis_index("c")
    return x, y, c, [(1 - x, y), (x, 1 - y), (1 - x, 1 - y)]


def _all_gather(shards):
    n = len(shards)
    slots = 9
    cut = [(s.shape[0] // 32) * 16 for s in shards]

    def build(ins, outs, sems):
        send_sems, recv_sems, local_sems = sems
        x, y, c, _ = _place()
        me, sib, xn, yn, dg = (x, y, c), (x, y, 1 - c), (1 - x, y, c), (x, 1 - y, c), (1 - x, 1 - y, c)
        other = lambda p: (p[0], p[1], 1 - p[2])

        def rows(a, p, part=None):
            ref = outs[a].at[4 * p[0] + 2 * p[1] + p[2]]
            if part is None or cut[a] == 0:
                return ref if part in (None, 0) else None
            return ref.at[pl.ds(0, cut[a])] if part == 0 else ref.at[pl.ds(cut[a], shards[a].shape[0] - cut[a])]

        def copy(a, k, ref, to, src=None):
            if ref is None:
                return None
            return pltpu.make_async_remote_copy(
                src_ref=ref if src is None else src, dst_ref=ref, send_sem=send_sems.at[slots * a + k],
                recv_sem=recv_sems.at[slots * a + k], device_id=to, device_id_type=MESH)

        def real(cps):
            return [cp for cp in cps if cp is not None]

        class Copies:
            own = lambda a: [copy(a, 1, rows(a, me), xn, ins[a]), copy(a, 2, rows(a, me), yn, ins[a]),
                             copy(a, 0, rows(a, me), sib, ins[a])]
            local = lambda a: pltpu.make_async_copy(ins[a], rows(a, me), local_sems.at[a])
            from_x = lambda a: copy(a, 1, rows(a, xn), me)
            from_y = lambda a: copy(a, 2, rows(a, yn), me)
            after_x = lambda a: real([copy(a, 4, rows(a, xn, 1), yn), copy(a, 5, rows(a, xn), sib)])
            after_y = lambda a: real([copy(a, 3, rows(a, yn, 0), xn), copy(a, 6, rows(a, yn), sib)])
            diag_in = lambda a: real([copy(a, 3, rows(a, dg, 0), me), copy(a, 4, rows(a, dg, 1), me)])
            diag_on = lambda a: real([copy(a, 7, rows(a, dg, 0), sib), copy(a, 8, rows(a, dg, 1), sib)])
            from_sib = lambda a: real([copy(a, 0, rows(a, sib), me), copy(a, 5, rows(a, other(xn)), me),
                                       copy(a, 6, rows(a, other(yn)), me), copy(a, 7, rows(a, other(dg), 0), me),
                                       copy(a, 8, rows(a, other(dg), 1), me)])

        return Copies

    def start(ins, outs, sems):
        cps = build(ins, outs, sems)
        for a in range(n):
            for cp in cps.own(a):
                cp.start()
        for a in range(n):
            cps.local(a).start()

    def relay(ins, outs, sems):
        cps = build(ins, outs, sems)
        for a in range(n):
            cps.from_x(a).wait_recv()
            for cp in cps.after_x(a):
                cp.start()
            cps.from_y(a).wait_recv()
            for cp in cps.after_y(a):
                cp.start()

    def finish(ins, outs, sems):
        cps = build(ins, outs, sems)
        for a in range(n):
            for arrived, onward in zip(cps.diag_in(a), cps.diag_on(a)):
                arrived.wait_recv()
                onward.start()
        for a in range(n):
            for cp in cps.from_sib(a):
                cp.wait_recv()
            for cp in cps.own(a) + cps.after_x(a) + cps.after_y(a) + cps.diag_on(a):
                cp.wait_send()
            cps.local(a).wait()

    return _Exchange(shards, [SDS((N_DEV,) + s.shape, s.dtype) for s in shards],
                     [pltpu.SemaphoreType.DMA((slots * n,)), pltpu.SemaphoreType.DMA((slots * n,)),
                      pltpu.SemaphoreType.DMA((n,))], start, finish, relay)


def _swap_exchange(ins, out_shape, per, copies):
    def start(i, o, sems):
        for cp in copies(i, o, sems):
            cp.start()

    def finish(i, o, sems):
        for cp in copies(i, o, sems):
            cp.wait()

    n = per * len(ins)
    return _Exchange(ins, out_shape, [pltpu.SemaphoreType.DMA((n,)), pltpu.SemaphoreType.DMA((n,))], start, finish)


def _exchange_c(gs):
    def copies(ins, outs, sems):
        x, y, c, _ = _place()
        return [pltpu.make_async_remote_copy(
                    src_ref=ins[a].at[2 * k + 1 - c], dst_ref=outs[a].at[k],
                    send_sem=sems[0].at[4 * a + k], recv_sem=sems[1].at[4 * a + k],
                    device_id=(x, y, 1 - c), device_id_type=MESH)
                for a in range(len(gs)) for k in range(4)]

    return _swap_exchange(gs, [SDS((4,) + g.shape[1:], g.dtype) for g in gs], 4, copies)


def _exchange_xy(sends):
    def copies(ins, outs, sems):
        x, y, c, chips = _place()
        return [pltpu.make_async_remote_copy(
                    src_ref=ins[a].at[t], dst_ref=outs[a].at[t],
                    send_sem=sems[0].at[3 * a + t], recv_sem=sems[1].at[3 * a + t],
                    device_id=(*chips[t], c), device_id_type=MESH)
                for a in range(len(sends)) for t in range(3)]

    return _swap_exchange(sends, [SDS(s.shape, s.dtype) for s in sends], 3, copies)


def _rs_combine(g, recv, pos, name, carry=()):
    _, r, cdim = g.shape
    tr = _tile(r, 256, 16)

    def body(pos_ref, g0, r0, g1, r1, g2, r2, g3, r3, keep_ref, send_ref):
        keep_ref[...] = g0[...] + r0[...]
        send_ref[0] = (g1[...] + r1[...]).astype(BF16)
        send_ref[1] = (g2[...] + r2[...]).astype(BF16)
        send_ref[2] = (g3[...] + r3[...]).astype(BF16)

    def k_of(p, t):
        px = p[0] if t in (0, 2) else 1 - p[0]
        py = p[1] if t in (0, 1) else 1 - p[1]
        return 2 * px + py

    blk = (None, tr, cdim)
    in_specs = []
    for t in range(4):
        in_specs.append(pl.BlockSpec(blk, functools.partial(lambda j, p, t: (2 * k_of(p, t) + p[2], j, 0), t=t)))
        in_specs.append(pl.BlockSpec(blk, functools.partial(lambda j, p, t: (k_of(p, t), j, 0), t=t)))
    return _hosted(
        body, carry, n_prefetch=1, out_shape=(SDS((r, cdim), F32), SDS((3, r, cdim), BF16)),
        grid=(r // tr,), in_specs=in_specs,
        out_specs=[pl.BlockSpec((tr, cdim), lambda j, p: (j, 0)), pl.BlockSpec((3, tr, cdim), lambda j, p: (0, j, 0))],
        compiler_params=_arb(1), name=name)(pos, g, recv, g, recv, g, recv, g, recv)


def _adamw_shard(keep, recv, w, m, v, name):
    r, cdim = w.shape
    tr = _tile(r, 256, 16)

    def body(k_ref, r_ref, w_ref, m_ref, v_ref, g_ref, d_ref, nm_ref, nv_ref):
        g = ((k_ref[...] + r_ref[0].astype(F32)) + r_ref[1].astype(F32)) + r_ref[2].astype(F32)
        g_ref[...] = g
        d_ref[...], nm_ref[...], nv_ref[...] = _adamw(w_ref[...], g, m_ref[...], v_ref[...])

    blk = pl.BlockSpec((tr, cdim), lambda j: (j, 0))
    out = SDS((r, cdim), F32)
    return _pcall(body, grid=(r // tr,), in_specs=[blk, pl.BlockSpec((3, tr, cdim), lambda j: (0, j, 0)), blk, blk, blk],
                  out_specs=[blk] * 4, out_shape=(out,) * 4, compiler_params=_arb(1), name=name)(keep, recv, w, m, v)


_HBM = pl.BlockSpec(memory_space=pltpu.HBM)
_SEM = pl.BlockSpec(memory_space=pltpu.SEMAPHORE)
_SPLIT = pltpu.CompilerParams(has_side_effects=pltpu.SideEffectType.DATAFLOW_SIDE_EFFECTING)


def _split_copies(kind, n, refs):
    srcs, lands, (send_sems, recv_sems) = refs[:n], refs[n:2 * n], refs[2 * n:2 * n + 2]
    x, y, c, chips = _place()
    per = _SPLIT_COPIES[kind]
    if kind == "xy":
        ends = lambda a, t: (srcs[a].at[t], lands[a].at[t], (*chips[t], c))
    else:
        ends = lambda a, k: (srcs[a].at[2 * k + 1 - c], lands[a].at[k], (x, y, 1 - c))
    cps = []
    for a in range(n):
        for t in range(per):
            src, dst, to = ends(a, t)
            cps.append(pltpu.make_async_remote_copy(src_ref=src, dst_ref=dst, send_sem=send_sems.at[per * a + t],
                                                    recv_sem=recv_sems.at[per * a + t], device_id=to, device_id_type=MESH))
    return cps


_SPLIT_COPIES = {"xy": 3, "c": 4}


def _exchange_start(kind, arrays, name, after=None):
    n = len(arrays)
    order = [] if after is None else [after]

    def body(*refs):
        refs = refs[:2 * n] + refs[2 * n + len(order):]
        for cp in _split_copies(kind, n, refs):
            cp.start()
        refs[-1][...] = jnp.zeros_like(refs[-1])

    hbm = lambda a: pltpu.with_memory_space_constraint(a, pltpu.HBM)
    land = [a.shape if kind == "xy" else (4,) + a.shape[1:] for a in arrays]
    bufs = [pltpu.HBM(a.shape, a.dtype) for a in arrays] + [pltpu.HBM(s, a.dtype) for s, a in zip(land, arrays)]
    sems = pltpu.SemaphoreType.DMA((_SPLIT_COPIES[kind] * n,))
    res = _pcall(
        body, name=name, out_shape=(sems, sems, *bufs, SDS((SUB, LANES), F32)),
        in_specs=[_HBM] * (2 * n) + _hbm_specs(len(order)),
        out_specs=[_SEM, _SEM] + [_HBM] * (2 * n) + [pl.BlockSpec(memory_space=pltpu.VMEM)],
        input_output_aliases={k: 2 + k for k in range(2 * n)}, compiler_params=_SPLIT)(
            *[hbm(a) for a in arrays], *[hbm(lax.empty(s, a.dtype)) for s, a in zip(land, arrays)], *order)
    return (kind, n, res[:-1]), res[-1]


def _exchange_wait(started, after, name, sources=False):
    kind, n, (send_sems, recv_sems, *bufs) = started

    def body(*refs):
        for cp in _split_copies(kind, n, refs):
            cp.wait_send()
            cp.wait_recv()

    shapes = [pltpu.HBM(b.shape, b.dtype) for b in bufs]
    res = _pcall(
        body, name=name, out_shape=tuple(shapes),
        in_specs=[_HBM] * (2 * n) + [_SEM, _SEM, pl.BlockSpec(memory_space=pl.ANY)], out_specs=[_HBM] * (2 * n),
        input_output_aliases={k: k for k in range(2 * n)}, compiler_params=_SPLIT)(*bufs, send_sems, recv_sems, after)
    return (list(res[:n]), list(res[n:])) if sources else list(res[n:])


def _follow(token):
    nothing = lambda ins, outs, sems: None
    return _Exchange([token], [], [], nothing, nothing)


def _adamw_small(gathered, seg, params, conv_rows):
    names = list(params)
    c0, cn = conv_rows

    def body(*refs):
        gat_ref = refs[0]
        ins = refs[1:1 + 3 * len(names)]
        outs = refs[1 + 3 * len(names):]

        def total(r0, rn):
            tot = gat_ref[0, r0:r0 + rn, :]
            for dev in range(1, N_DEV):
                tot = tot + gat_ref[dev, r0:r0 + rn, :]
            return tot

        for k, nm in enumerate(names):
            g = total(*seg[nm])
            w_ref, m_ref, v_ref = ins[3 * k:3 * k + 3]
            g_ref, d_ref, nm_ref, nv_ref = outs[4 * k:4 * k + 4]
            g_ref[...] = g
            d_ref[...], nm_ref[...], nv_ref[...] = _adamw(w_ref[...], g, m_ref[...], v_ref[...])
        outs[-2][...] = total(c0, cn)
        outs[-1][...] = total(*seg["loss"])

    flat_in = [a for nm in names for a in params[nm]]
    out_shape = []
    for nm in names:
        out_shape += [SDS(params[nm][0].shape, F32)] * 4
    out_shape += [SDS((cn, LANES), F32), SDS((seg["loss"][1], LANES), F32)]
    res = _pcall(body, out_shape=tuple(out_shape), name="adamw_small")(gathered, *flat_in)
    per = {nm: res[4 * k:4 * k + 4] for k, nm in enumerate(names)}
    return per, res[-2], res[-1]


def _adamw_one(w, g, m, v, name):
    def body(w_ref, g_ref, m_ref, v_ref, d_ref, nm_ref, nv_ref):
        d_ref[...], nm_ref[...], nv_ref[...] = _adamw(w_ref[...], g_ref[...], m_ref[...], v_ref[...])

    return _pcall(body, out_shape=(SDS(w.shape, F32),) * 3, name=name)(w, g, m, v)


def _rows128(a):
    return a.reshape(-1, LANES)


def _pack_small(gs, loss_tile):
    seg, pieces, row = {}, [], 0
    for nm in SMALL + ("conv_w", "loss"):
        piece = loss_tile if nm == "loss" else _rows128(gs[nm])
        rn = _round_up(piece.shape[0], SUB)
        pieces.append(jnp.pad(piece, ((0, rn - piece.shape[0]), (0, 0))))
        seg[nm] = (row, piece.shape[0])
        row += rn
    return jnp.concatenate(pieces, axis=0), seg


def _step(x, mem, target, wb, conv_w, sp, pos):
    s, d = x.shape
    tm = min(TOKEN_TILE, s)
    tm_wide = min(2 * TOKEN_TILE, s)
    rows = lambda w8: w8.reshape(-1, w8.shape[2])
    shards = lambda g: g.reshape((N_DEV, -1) + g.shape[1:])
    bt = sp["b_spatial"].T

    (w_in8, conv8), = _run_exchanges([_all_gather([wb["w_in"], conv_w])], "gather_w_in")
    conv_full = conv8.transpose(1, 0, 2).reshape(3, -1)
    w_in_t = rows(w_in8)
    (xn1, h), ((w_out8, w_kv8, w_q8),) = _in_forward(
        x, sp["ln_mix_g"], w_in_t, tm, carry=[_all_gather([wb["w_out"], wb["w_kv"], wb["w_q"]])])
    w_out = rows(w_out8)
    (ycat, x1), ((w_o8, w_down8),) = _mix_forward(
        h, x, sp["sgu_ln_g"], sp["sgu_ln_b"], sp["w_spatial"], bt, conv_full, sp["grp_norm_a"], sp["grp_norm_b"], w_out, tm,
        carry=[_all_gather([wb["w_o"], wb["w_down"]])])
    w_q, w_o, w_down = rows(w_q8), rows(w_o8), rows(w_down8)
    memn, kv = _kv_forward(mem, sp["ln_mem_g"], w_kv8)
    (xn2, q, o, x2), ((w_gu8,),) = _attn_forward(
        x1, sp["ln_attn_g"], w_q, kv, w_o, tm, carry=[_all_gather([wb["w_gate_up"]])])
    w_gu = w_gu8.reshape((2, N_DEV // 2) + w_gu8.shape[1:])
    xn3, gu, x3 = _ffn_forward(x2, sp["ln_ffn_g"], w_gu, w_down, tm_wide)

    loss, d_lnf, dx3, dx3b = _final_backward(x3, target, sp["ln_final_g"], tm_wide)
    act, dgu, dxn3 = _swiglu_backward(dx3b, gu, w_gu, w_down, tm_wide)
    g_gu, _ = _wgrad_blocked_lhs(dgu.reshape((N_DEV,) + dgu.shape[2:]), xn3, "wgrad_gate_up")
    g_gu = shards(g_gu)
    g_down, ((rc_gu,),) = _wgrad_blocked_lhs(act, dx3b, "wgrad_down", carry=[_exchange_c([g_gu])])
    g_down = shards(g_down)
    keep, pending = {}, []
    (keep["w_gate_up"], send_gu), _ = _rs_combine(g_gu, rc_gu, pos, "rs_combine_w_gate_up")
    started, token = _exchange_start("xy", [send_gu], "exchange_xy_1_start")
    pending.append((("w_gate_up",), started))
    c_down, token = _exchange_start("c", [g_down], "exchange_c_1_start", after=token)
    (dx2b, dq, dx1, dx1b, dkv, d_lnattn, d_lnffn), _ = _attn_backward(
        dx3, dxn3, x2, sp["ln_ffn_g"], x1, sp["ln_attn_g"], q, kv, w_q, w_o, tm, carry=[_follow(token)])
    (g_down,), (rc_down,) = _exchange_wait(c_down, dx1b, "exchange_c_1_wait", sources=True)
    (keep["w_down"], send_down), _ = _rs_combine(g_down, rc_down, pos, "rs_combine_w_down")
    g_o, _ = _wgrad(o, dx2b, "wgrad_o")
    g_q, _ = _wgrad(xn2, dq, "wgrad_q")
    g_o, g_q = shards(g_o), shards(g_q)
    c_oq, token = _exchange_start("c", [g_o, g_q], "exchange_c_2_start")
    g_out, _ = _wgrad(ycat, dx1b, "wgrad_out", carry=[_follow(token)])
    g_out = shards(g_out)
    g_kv, d_lnmem = _kv_backward(dkv, memn, mem, sp["ln_mem_g"], w_kv8)
    (g_o, g_q), (rc_o, rc_q) = _exchange_wait(c_oq, g_out, "exchange_c_2_wait", sources=True)
    c_outkv, token = _exchange_start("c", [g_out, g_kv], "exchange_c_3_start")
    (keep["w_o"], send_o), _ = _rs_combine(g_o, rc_o, pos, "rs_combine_w_o", carry=[_follow(token)])
    (keep["w_q"], send_q), _ = _rs_combine(g_q, rc_q, pos, "rs_combine_w_q")
    (g_out, g_kv), (rc_out, rc_kv) = _exchange_wait(c_outkv, send_q, "exchange_c_3_wait", sources=True)
    (keep["w_out"], send_out), _ = _rs_combine(g_out, rc_out, pos, "rs_combine_w_out")
    (keep["w_kv"], send_kv), _ = _rs_combine(g_kv, rc_kv, pos, "rs_combine_w_kv")
    started, token = _exchange_start("xy", [send_down, send_o, send_q, send_out, send_kv], "exchange_xy_2_start")
    pending.append((("w_down", "w_o", "w_q", "w_out", "w_kv"), started))
    (dh, dx, d_ga, d_gb, d_cw, d_lng, d_lnb, d_wsp, d_bs, d_lnmix), _ = _mix_backward(
        dx1, x, sp["ln_mix_g"], h, sp["sgu_ln_g"], sp["sgu_ln_b"], sp["w_spatial"], bt, conv_full,
        sp["grp_norm_a"], sp["grp_norm_b"], w_out, w_in_t, tm, carry=[_follow(token)])
    gs = {"ln_mix_g": d_lnmix, "sgu_ln_g": d_lng, "sgu_ln_b": d_lnb, "w_spatial": d_wsp, "b_spatial": _bias_grad(d_bs),
          "conv_w": d_cw[:3], "grp_norm_a": d_ga, "grp_norm_b": d_gb, "ln_attn_g": d_lnattn, "ln_mem_g": d_lnmem,
          "ln_ffn_g": d_lnffn, "ln_final_g": d_lnf}
    packed, seg = _pack_small(gs, loss)
    g_in, (_, (small_all,)) = _wgrad(dh, xn1, "wgrad_in", carry=[_follow(token), _all_gather([packed])])
    g_in = shards(g_in)
    c_in, token = _exchange_start("c", [g_in], "exchange_c_4_start")
    return dx, keep, pending, (g_in, c_in), token, small_all, seg


def kernel(x, mem, ln_mix_g, w_in, sgu_ln_g, sgu_ln_b, w_spatial, b_spatial, conv_w, grp_norm_a, grp_norm_b, w_out, ln_attn_g, ln_mem_g, w_q, w_kv, w_o, ln_ffn_g, w_gate_up, w_down, ln_final_g, loss_target, m_ln_mix_g, m_w_in, m_sgu_ln_g, m_sgu_ln_b, m_w_spatial, m_b_spatial, m_conv_w, m_grp_norm_a, m_grp_norm_b, m_w_out, m_ln_attn_g, m_ln_mem_g, m_w_q, m_w_kv, m_w_o, m_ln_ffn_g, m_w_gate_up, m_w_down, m_ln_final_g, v_ln_mix_g, v_w_in, v_sgu_ln_g, v_sgu_ln_b, v_w_spatial, v_b_spatial, v_conv_w, v_grp_norm_a, v_grp_norm_b, v_w_out, v_ln_attn_g, v_ln_mem_g, v_w_q, v_w_kv, v_w_o, v_ln_ffn_g, v_w_gate_up, v_w_down, v_ln_final_g):
    order = ["ln_mix_g", "w_in", "sgu_ln_g", "sgu_ln_b", "w_spatial", "b_spatial", "conv_w", "grp_norm_a", "grp_norm_b",
             "w_out", "ln_attn_g", "ln_mem_g", "w_q", "w_kv", "w_o", "ln_ffn_g", "w_gate_up", "w_down", "ln_final_g"]
    W = dict(ln_mix_g=ln_mix_g, w_in=w_in, sgu_ln_g=sgu_ln_g, sgu_ln_b=sgu_ln_b, w_spatial=w_spatial, b_spatial=b_spatial,
             conv_w=conv_w, grp_norm_a=grp_norm_a, grp_norm_b=grp_norm_b, w_out=w_out, ln_attn_g=ln_attn_g,
             ln_mem_g=ln_mem_g, w_q=w_q, w_kv=w_kv, w_o=w_o, ln_ffn_g=ln_ffn_g, w_gate_up=w_gate_up, w_down=w_down,
             ln_final_g=ln_final_g)
    M = dict(ln_mix_g=m_ln_mix_g, w_in=m_w_in, sgu_ln_g=m_sgu_ln_g, sgu_ln_b=m_sgu_ln_b, w_spatial=m_w_spatial,
             b_spatial=m_b_spatial, conv_w=m_conv_w, grp_norm_a=m_grp_norm_a, grp_norm_b=m_grp_norm_b, w_out=m_w_out,
             ln_attn_g=m_ln_attn_g, ln_mem_g=m_ln_mem_g, w_q=m_w_q, w_kv=m_w_kv, w_o=m_w_o, ln_ffn_g=m_ln_ffn_g,
             w_gate_up=m_w_gate_up, w_down=m_w_down, ln_final_g=m_ln_final_g)
    V = dict(ln_mix_g=v_ln_mix_g, w_in=v_w_in, sgu_ln_g=v_sgu_ln_g, sgu_ln_b=v_sgu_ln_b, w_spatial=v_w_spatial,
             b_spatial=v_b_spatial, conv_w=v_conv_w, grp_norm_a=v_grp_norm_a, grp_norm_b=v_grp_norm_b, w_out=v_w_out,
             ln_attn_g=v_ln_attn_g, ln_mem_g=v_ln_mem_g, w_q=v_w_q, w_kv=v_w_kv, w_o=v_w_o, ln_ffn_g=v_ln_ffn_g,
             w_gate_up=v_w_gate_up, w_down=v_w_down, ln_final_g=v_ln_final_g)

    bw = conv_w.shape[1] * N_DEV
    pos = jnp.stack([lax.axis_index("x"), lax.axis_index("y"), lax.axis_index("c")]).astype(jnp.int32)
    me = 4 * pos[0] + 2 * pos[1] + pos[2]

    sp = {nm: (W[nm].reshape(1, -1) if W[nm].ndim == 1 else W[nm]) for nm in SMALL}
    view = lambda a, nm: a.T if nm in TRANSPOSED else a
    wb = {nm: view(W[nm], nm).astype(BF16) for nm in BIG}
    grad_x, keep, pending, (g_in, c_in), token, small_all, seg = _step(
        x[0], mem[0], loss_target[0], wb, conv_w, sp, pos)

    out = {}

    def update(k, names, started, token):
        landed = _exchange_wait(started, token, "exchange_xy_%d_wait" % k)
        for nm, rxy in zip(names, landed):
            res = _adamw_shard(keep[nm], rxy, view(W[nm], nm), view(M[nm], nm), view(V[nm], nm), "adamw_" + nm)
            out[nm] = tuple(view(a, nm) for a in res)
            token = res[0]
        return token

    token = update(1, *pending[0], token)
    (g_in,), (rc_in,) = _exchange_wait(c_in, token, "exchange_c_4_wait", sources=True)
    (keep["w_in"], send_in), _ = _rs_combine(g_in, rc_in, pos, "rs_combine_w_in")
    xy_in, token = _exchange_start("xy", [send_in], "exchange_xy_3_start")
    token = update(2, *pending[1], token)
    update(3, ("w_in",), xy_in, token)

    params = {nm: (_rows128(W[nm]), _rows128(M[nm]), _rows128(V[nm])) for nm in SMALL}
    per, conv_g_rows, loss_sum = _adamw_small(small_all, seg, params, seg["conv_w"])
    for nm in SMALL:
        out[nm] = tuple(a.reshape(W[nm].shape) for a in per[nm])
    conv_g = lax.dynamic_slice_in_dim(conv_g_rows.reshape(3, bw), me * conv_w.shape[1], conv_w.shape[1], axis=1)
    out["conv_w"] = (conv_g,) + tuple(_adamw_one(conv_w, conv_g, m_conv_w, v_conv_w, "adamw_conv"))

    loss = loss_sum[0, 0]
    res = [loss, grad_x[None]]
    for k in range(4):
        res += [out[nm][k] for nm in order]
    return tuple(res)
```

```python
import functools

import jax
import jax.numpy as jnp
from jax import lax
from jax.experimental import pallas as pl
from jax.experimental.pallas import tpu as pltpu

F32 = jnp.float32
BF16 = jnp.bfloat16
SDS = jax.ShapeDtypeStruct
MESH = pl.DeviceIdType.MESH

EPS = 1e-6
N_DEV = 8
HEADS = 4
CHUNK = 128
HALO = 16
SUB = 8
LANES = 128
TOKEN_TILE = 512
ROW_CHUNK = 256
RELAY_AT = 0.7

ADAM_LR = 0.001
ADAM_B1 = 0.9
ADAM_B2 = 0.999
ADAM_EPS = 1e-08
ADAM_WD = 0.01
ADAM_STEP = 10

BIG = ("w_in", "w_out", "w_q", "w_kv", "w_o", "w_gate_up", "w_down")
TRANSPOSED = ("w_in", "w_gate_up")
SMALL = ("ln_mix_g", "sgu_ln_g", "sgu_ln_b", "w_spatial", "b_spatial", "grp_norm_a", "grp_norm_b",
         "ln_attn_g", "ln_mem_g", "ln_ffn_g", "ln_final_g")


class _Exchange:
    def __init__(self, ins, out_shape, sems, start, finish, relay=None):
        self.ins, self.out_shape, self.sems = list(ins), list(out_shape), list(sems)
        self.start, self.finish, self.relay = start, finish, relay


def _pcall(body, carry=(), n_prefetch=0, **kw):
    if carry:
        return functools.partial(_carrying_call, body, tuple(carry), n_prefetch, kw)
    if n_prefetch:
        kw["grid_spec"] = pltpu.PrefetchScalarGridSpec(
            num_scalar_prefetch=n_prefetch, grid=kw.pop("grid"), in_specs=kw.pop("in_specs"),
            out_specs=kw.pop("out_specs"), scratch_shapes=kw.pop("scratch_shapes", ()))
    return pl.pallas_call(body, **kw)


def _carrying_call(body, carry, n_prefetch, kw, *args):
    kw = dict(kw)
    out_shape = kw.pop("out_shape")
    single = not isinstance(out_shape, (tuple, list))
    outs_shape = (out_shape,) if single else tuple(out_shape)
    out_specs = kw.pop("out_specs")
    out_specs = [out_specs] if single else list(out_specs)
    in_specs = list(kw.pop("in_specs"))
    scratch = list(kw.pop("scratch_shapes", ()))
    grid = tuple(kw.get("grid", ()))
    n_in, n_out, n_scr = len(args), len(outs_shape), len(scratch)

    def split(refs, k, counts):
        parts = []
        for cnt in counts:
            parts.append(refs[k:k + cnt])
            k += cnt
        return parts, k

    def wrapped(*refs):
        cins, k = split(refs, n_in, [len(p.ins) for p in carry])
        outs = refs[k:k + n_out]
        couts, k = split(refs, k + n_out, [len(p.out_shape) for p in carry])
        scr = refs[k:k + n_scr]
        csems, _ = split(refs, k + n_scr, [len(p.sems) for p in carry])
        first, last = True, True
        for a, g in enumerate(grid):
            first = (pl.program_id(a) == 0) & first
            last = (pl.program_id(a) == g - 1) & last

        def start_all():
            for p, ci, co, cs in zip(carry, cins, couts, csems):
                p.start(ci, co, cs)

        def relay_all():
            for p, ci, co, cs in zip(carry, cins, couts, csems):
                if p.relay is not None:
                    p.relay(ci, co, cs)

        def finish_all():
            for p, ci, co, cs in zip(carry, cins, couts, csems):
                p.finish(ci, co, cs)

        if len(grid) == 1:
            relay_now = pl.program_id(0) == min(int(RELAY_AT * grid[0]), grid[0] - 1)
        else:
            relay_now = last
        start_all() if not grid else pl.when(first)(start_all)
        relay_all() if not grid else pl.when(relay_now)(relay_all)
        body(*refs[:n_in], *outs, *scr)
        finish_all() if not grid else pl.when(last)(finish_all)

    c_in = [a for p in carry for a in p.ins]
    c_out = [s for p in carry for s in p.out_shape]
    c_sems = [s for p in carry for s in p.sems]
    res = _pcall(wrapped, n_prefetch=n_prefetch, out_shape=outs_shape + tuple(c_out),
                 in_specs=in_specs + _hbm_specs(len(c_in)), out_specs=out_specs + _hbm_specs(len(c_out)),
                 scratch_shapes=scratch + c_sems, **kw)(*args, *c_in)
    own = res[0] if single else tuple(res[:n_out])
    landed, k = [], n_out
    for p in carry:
        landed.append(list(res[k:k + len(p.out_shape)]))
        k += len(p.out_shape)
    return own, landed


def _hbm_specs(n):
    return [pl.BlockSpec(memory_space=pl.ANY)] * n


def _hosted(body, carry, **kw):
    if carry:
        return _pcall(body, carry=carry, **kw)
    call = _pcall(body, **kw)
    return lambda *args: (call(*args), [])


def _run_exchanges(parts, name):
    def body(*refs):
        pass

    _, landed = _pcall(body, carry=parts, out_shape=(), in_specs=[], out_specs=[], name=name)()
    return landed


def _arb(n):
    return pltpu.CompilerParams(dimension_semantics=("arbitrary",) * n)


def _tile(n, target, mult):
    best = None
    for t in range(mult, min(n, target) + 1, mult):
        if n % t == 0:
            best = t
    return n if best is None else best


def _round_up(n, m):
    return (n + m - 1) // m * m


def _dot(a, b):
    return jnp.dot(a, b, preferred_element_type=F32)


def _dot_nt(a, b):
    return lax.dot_general(a, b, (((1,), (1,)), ((), ())), preferred_element_type=F32)


def _dot_tn(a, b):
    return lax.dot_general(a, b, (((0,), (0,)), ((), ())), preferred_element_type=F32)


def _rstd(x):
    return lax.rsqrt(jnp.mean(x * x, axis=-1, keepdims=True) + EPS)


def _rms_bwd(dy, x, r, g):
    gdy = dy * g
    proj = jnp.sum(gdy * x, axis=-1, keepdims=True) * (1.0 / x.shape[-1])
    dx = r * gdy - x * (r * r * r) * proj
    dg = jnp.sum(dy * (x * r), axis=0, keepdims=True)
    return dx, dg


_GELU_C = 0.7978845608028654
_GELU_A = 0.044715


def _gelu(x):
    t = jnp.tanh(_GELU_C * (x + _GELU_A * x * x * x))
    return 0.5 * x * (1.0 + t), t


def _gelu_grad(x, t):
    return 0.5 * (1.0 + t) + 0.5 * x * (1.0 - t * t) * (_GELU_C * (1.0 + 3.0 * _GELU_A * x * x))


def _sigmoid(x):
    return 1.0 / (1.0 + jnp.exp(-x))


def _softmax(s):
    m = jnp.max(s, axis=-1, keepdims=True)
    e = jnp.exp(s - m)
    return e / jnp.sum(e, axis=-1, keepdims=True)


def _adamw(w, g, m, v):
    m = ADAM_B1 * m + (1.0 - ADAM_B1) * g
    v = ADAM_B2 * v + (1.0 - ADAM_B2) * (g * g)
    m_hat = m / (1.0 - ADAM_B1 ** ADAM_STEP)
    v_hat = v / (1.0 - ADAM_B2 ** ADAM_STEP)
    delta = -ADAM_LR * (m_hat / (jnp.sqrt(v_hat) + ADAM_EPS) + ADAM_WD * w)
    return delta, m, v


def _tril_mask():
    t = lax.broadcasted_iota(jnp.int32, (CHUNK, CHUNK), 0)
    s = lax.broadcasted_iota(jnp.int32, (CHUNK, CHUNK), 1)
    return (s <= t).astype(F32)


def _sgu_forward(ha, lng, lnb, wm, bt, mixed_s):
    aw = ha.shape[1] // 2
    hd = aw // HEADS
    a, th = _gelu(ha)
    u = a[:, :aw]
    v = a[:, aw:]
    mu = jnp.mean(v, axis=-1, keepdims=True)
    vc = v - mu
    rl = lax.rsqrt(jnp.mean(vc * vc, axis=-1, keepdims=True) + EPS)
    xhat = vc * rl
    vln = (xhat * lng + lnb).astype(BF16)
    for n in range(ha.shape[0] // CHUNK):
        rows = slice(n * CHUNK, (n + 1) * CHUNK)
        for h in range(HEADS):
            cols = slice(h * hd, (h + 1) * hd)
            mixed_s[rows, cols] = _dot(wm[h], vln[rows, cols]) + bt[:, h:h + 1]
    return th, u, xhat, rl, vln


def _conv_taps(zext):
    return pltpu.roll(zext, 2, 0), pltpu.roll(zext, 1, 0)


def _kv_forward(mem, g_mem, w_kv):
    ml, d = mem.shape
    xd = w_kv.shape[2]

    def body(mem_ref, g_ref, w_ref, memn_ref, kv_ref):
        x = mem_ref[...]
        memn = (x * _rstd(x) * g_ref[...]).astype(BF16)
        memn_ref[...] = memn
        for j in range(2 * HEADS):
            kv_ref[j] = _dot(memn, w_ref[j]).astype(BF16)

    return _pcall(body, out_shape=(SDS((ml, d), BF16), SDS((2 * HEADS, ml, xd), BF16)), name="kv_forward")(mem, g_mem, w_kv)


def _in_forward(x, g, w_in_t, tm, carry=()):
    s, d = x.shape
    n_in = w_in_t.shape[0]

    def body(x_ref, g_ref, w_ref, xn_ref, h_ref):
        xv = x_ref[...]
        xn = (xv * _rstd(xv) * g_ref[...]).astype(BF16)
        xn_ref[...] = xn
        h_ref[...] = _dot_nt(xn, w_ref[...])

    return _hosted(
        body, carry, grid=(s // tm,),
        in_specs=[pl.BlockSpec((tm, d), lambda i: (i, 0)), pl.BlockSpec((1, d), lambda i: (0, 0)),
                  pl.BlockSpec((n_in, d), lambda i: (0, 0))],
        out_specs=[pl.BlockSpec((tm, d), lambda i: (i, 0)), pl.BlockSpec((tm, n_in), lambda i: (i, 0))],
        out_shape=(SDS((s, d), BF16), SDS((s, n_in), F32)),
        compiler_params=_arb(1), name="in_forward")(x, g, w_in_t)


def _mix_forward(h, x, lng, lnb, w_sp, bt, conv_w, ga, gb, w_out, tm, carry=()):
    s, d = x.shape
    n_in = h.shape[1]
    aw = lng.shape[1]
    bw = d - aw
    in_a = 2 * aw
    hb_blocks = tm // HALO

    def body(h_ref, hprev_ref, x_ref, lng_ref, lnb_ref, wsp_ref, bt_ref, cw_ref, ga_ref, gb_ref, wout_ref,
             ycat_ref, x1_ref, mixed_s):
        i = pl.program_id(0)
        mask = _tril_mask()
        wm = [(wsp_ref[hh] * mask).astype(BF16) for hh in range(HEADS)]
        hv = h_ref[...]
        _, u, _, _, _ = _sgu_forward(hv[:, :in_a], lng_ref[...], lnb_ref[...], wm, bt_ref[...], mixed_s)
        sg = u * mixed_s[...]
        ycat_ref[:, :aw] = (sg * _rstd(sg) * ga_ref[...]).astype(BF16)

        gate_b = hv[:, in_a:in_a + bw]
        z = hv[:, in_a + bw:in_a + 2 * bw] * hv[:, in_a + 2 * bw:]
        hp = hprev_ref[...]
        zp = hp[:, in_a + bw:in_a + 2 * bw] * hp[:, in_a + 2 * bw:]
        zp = jnp.where(i == 0, 0.0, zp)
        zext = jnp.concatenate([zp, z], axis=0)
        z2, z1 = _conv_taps(zext)
        cw = cw_ref[...]
        conv = cw[0:1] * z2[HALO:] + cw[1:2] * z1[HALO:] + cw[2:3] * z
        sc = gate_b * conv
        ycat_ref[:, aw:] = (sc * _rstd(sc) * gb_ref[...]).astype(BF16)
        x1_ref[...] = x_ref[...] + _dot(ycat_ref[...], wout_ref[...])

    full = lambda shape: pl.BlockSpec(shape, lambda i: (0,) * len(shape))
    return _hosted(
        body, carry, grid=(s // tm,),
        in_specs=[pl.BlockSpec((tm, n_in), lambda i: (i, 0)),
                  pl.BlockSpec((HALO, n_in), lambda i: (jnp.maximum(i * hb_blocks - 1, 0), 0)),
                  pl.BlockSpec((tm, d), lambda i: (i, 0)),
                  full((1, aw)), full((1, aw)), full((HEADS, CHUNK, CHUNK)), full((CHUNK, HEADS)),
                  full((3, bw)), full((1, aw)), full((1, bw)), full((d, d))],
        out_specs=[pl.BlockSpec((tm, d), lambda i: (i, 0)), pl.BlockSpec((tm, d), lambda i: (i, 0))],
        out_shape=(SDS((s, d), BF16), SDS((s, d), F32)),
        scratch_shapes=[pltpu.VMEM((tm, aw), F32)],
        compiler_params=_arb(1), name="mix_forward")(h, h, x, lng, lnb, w_sp, bt, conv_w, ga, gb, w_out)


def _attn_forward(x1, g, w_q, kv, w_o, tm, carry=()):
    s, d = x1.shape
    _, ml, xd = kv.shape
    scale = xd ** -0.5

    def body(x1_ref, g_ref, wq_ref, kv_ref, wo_ref, xn_ref, q_ref, o_ref, x2_ref):
        xv = x1_ref[...]
        xn = (xv * _rstd(xv) * g_ref[...]).astype(BF16)
        xn_ref[...] = xn
        q_ref[...] = _dot(xn, wq_ref[...]).astype(BF16)
        for hh in range(HEADS):
            cols = slice(hh * xd, (hh + 1) * xd)
            p = _softmax(_dot_nt(q_ref[:, cols], kv_ref[hh]) * scale)
            o_ref[:, cols] = _dot(p.astype(BF16), kv_ref[HEADS + hh]).astype(BF16)
        x2_ref[...] = xv + _dot(o_ref[...], wo_ref[...])

    tok = pl.BlockSpec((tm, d), lambda i: (i, 0))
    return _hosted(
        body, carry, grid=(s // tm,),
        in_specs=[tok, pl.BlockSpec((1, d), lambda i: (0, 0)), pl.BlockSpec((d, d), lambda i: (0, 0)),
                  pl.BlockSpec((2 * HEADS, ml, xd), lambda i: (0, 0, 0)), pl.BlockSpec((d, d), lambda i: (0, 0))],
        out_specs=[tok, tok, tok, tok],
        out_shape=(SDS((s, d), BF16), SDS((s, d), BF16), SDS((s, d), BF16), SDS((s, d), F32)),
        compiler_params=_arb(1), name="attn_forward")(x1, g, w_q, kv, w_o)


def _ffn_forward(x2, g, w_gu, w_down, tm):
    s, d = x2.shape
    _, nf, tf, _ = w_gu.shape

    def body(x2_ref, g_ref, wgu_ref, wd_ref, xn_ref, gu_ref, x3_ref):
        f = pl.program_id(1)

        @pl.when(f == 0)
        def _():
            xv = x2_ref[...]
            xn_ref[...] = (xv * _rstd(xv) * g_ref[...]).astype(BF16)
            x3_ref[...] = xv

        xn = xn_ref[...]
        gate = _dot_nt(xn, wgu_ref[0])
        up = _dot_nt(xn, wgu_ref[1])
        gu_ref[0] = gate.astype(BF16)
        gu_ref[1] = up.astype(BF16)
        act = (gate * _sigmoid(gate) * up).astype(BF16)
        x3_ref[...] += _dot(act, wd_ref[...])

    tok = pl.BlockSpec((tm, d), lambda i, f: (i, 0))
    return _pcall(
        body, grid=(s // tm, nf),
        in_specs=[tok, pl.BlockSpec((1, d), lambda i, f: (0, 0)),
                  pl.BlockSpec((2, None, tf, d), lambda i, f: (0, f, 0, 0)),
                  pl.BlockSpec((tf, d), lambda i, f: (f, 0))],
        out_specs=[tok, pl.BlockSpec((2, None, tm, tf), lambda i, f: (0, f, i, 0)), tok],
        out_shape=(SDS((s, d), BF16), SDS((2, nf, s, tf), BF16), SDS((s, d), F32)),
        compiler_params=_arb(2), name="ffn_forward")(x2, g, w_gu, w_down)


def _final_backward(x3, target, g_final, tm):
    s, d = x3.shape

    def body(x3_ref, tgt_ref, gf_ref, loss_ref, dgf_ref, dx3_ref, dx3b_ref):
        @pl.when(pl.program_id(0) == 0)
        def _():
            loss_ref[...] = jnp.zeros_like(loss_ref)
            dgf_ref[...] = jnp.zeros_like(dgf_ref)

        xv = x3_ref[...]
        r = _rstd(xv)
        diff = xv * r * gf_ref[...] - tgt_ref[...]
        loss_ref[...] += 0.5 * jnp.sum(jnp.sum(diff * diff, axis=-1, keepdims=True), axis=0, keepdims=True) * (1.0 / d)
        dx3, dgf = _rms_bwd(diff * (1.0 / d), xv, r, gf_ref[...])
        dgf_ref[...] += dgf
        dx3_ref[...] = dx3
        dx3b_ref[...] = dx3.astype(BF16)

    tok = pl.BlockSpec((tm, d), lambda i: (i, 0))
    vec = pl.BlockSpec((1, d), lambda i: (0, 0))
    return _pcall(
        body, grid=(s // tm,), in_specs=[tok, tok, vec],
        out_specs=[pl.BlockSpec((SUB, LANES), lambda i: (0, 0)), vec, tok, tok],
        out_shape=(SDS((SUB, LANES), F32), SDS((1, d), F32), SDS((s, d), F32), SDS((s, d), BF16)),
        compiler_params=_arb(1), name="final_backward")(x3, target, g_final)


def _swiglu_backward(dx3b, gu, w_gu, w_down, tm):
    s, d = dx3b.shape
    _, nf, tf, _ = w_gu.shape

    def body(dx3b_ref, gu_ref, wgu_ref, wd_ref, act_ref, dgu_ref, dxn_ref):
        @pl.when(pl.program_id(1) == 0)
        def _():
            dxn_ref[...] = jnp.zeros_like(dxn_ref)

        for r0 in range(0, tm, ROW_CHUNK):
            rows = slice(r0, r0 + ROW_CHUNK)
            dact = _dot_nt(dx3b_ref[rows, :], wd_ref[...])
            gv = gu_ref[0, rows, :].astype(F32)
            uv = gu_ref[1, rows, :].astype(F32)
            sg = _sigmoid(gv)
            silu = gv * sg
            act_ref[rows, :] = (silu * uv).astype(BF16)
            dgate = (dact * uv * (sg * (1.0 + gv * (1.0 - sg)))).astype(BF16)
            dup = (dact * silu).astype(BF16)
            dgu_ref[0, rows, :] = dgate
            dgu_ref[1, rows, :] = dup
            part = _dot(dgate, wgu_ref[0]) + _dot(dup, wgu_ref[1])
            dxn_ref[rows, :] += part

    tok = pl.BlockSpec((tm, d), lambda i, f: (i, 0))
    pair = pl.BlockSpec((2, None, tm, tf), lambda i, f: (0, f, i, 0))
    return _pcall(
        body, grid=(s // tm, nf),
        in_specs=[tok, pair, pl.BlockSpec((2, None, tf, d), lambda i, f: (0, f, 0, 0)),
                  pl.BlockSpec((tf, d), lambda i, f: (f, 0))],
        out_specs=[pl.BlockSpec((None, tm, tf), lambda i, f: (f, i, 0)), pair, tok],
        out_shape=(SDS((nf, s, tf), BF16), SDS((2, nf, s, tf), BF16), SDS((s, d), F32)),
        compiler_params=_arb(2), name="swiglu_backward")(dx3b, gu, w_gu, w_down)


def _attn_backward(dx3, dxn3, x2, g_ffn, x1, g, q, kv, w_q, w_o, tm, carry=()):
    s, d = x1.shape
    _, ml, xd = kv.shape
    scale = xd ** -0.5

    def body(dx3_ref, dxn3_ref, x2_ref, g2_ref, x1_ref, g_ref, q_ref, kv_ref, wq_ref, wo_ref,
             dx2b_ref, dq_ref, dx1_ref, dx1b_ref, dkv_ref, dg_ref, dg2_ref, do_s):
        i = pl.program_id(0)

        @pl.when(i == 0)
        def _():
            dkv_ref[...] = jnp.zeros_like(dkv_ref)
            dg_ref[...] = jnp.zeros_like(dg_ref)
            dg2_ref[...] = jnp.zeros_like(dg2_ref)

        x2v = x2_ref[...]
        dx2n, dg2 = _rms_bwd(dxn3_ref[...], x2v, _rstd(x2v), g2_ref[...])
        dg2_ref[...] += dg2
        dx2 = dx3_ref[...] + dx2n
        dx2b_ref[...] = dx2.astype(BF16)
        do_s[...] = _dot_nt(dx2b_ref[...], wo_ref[...]).astype(BF16)
        for hh in range(HEADS):
            kc = slice(hh * xd, (hh + 1) * xd)
            qh = q_ref[:, kc]
            kh = kv_ref[hh]
            doh = do_s[:, kc]
            p = _softmax(_dot_nt(qh, kh) * scale)
            dp = _dot_nt(doh, kv_ref[HEADS + hh])
            dkv_ref[HEADS + hh] += _dot_tn(p.astype(BF16), doh)
            ds = (p * (dp - jnp.sum(dp * p, axis=-1, keepdims=True)) * scale).astype(BF16)
            dq_ref[:, kc] = _dot(ds, kh).astype(BF16)
            dkv_ref[hh] += _dot_tn(ds, qh)
        dg_tile = jnp.zeros_like(dg_ref)
        for r0 in range(0, tm, min(ROW_CHUNK, tm)):
            rows = slice(r0, r0 + min(ROW_CHUNK, tm))
            dxn = _dot_nt(dq_ref[rows, :], wq_ref[...])
            xv = x1_ref[rows, :]
            dx, dg = _rms_bwd(dxn, xv, _rstd(xv), g_ref[...])
            dg_tile = dg_tile + dg
            dx1 = dx2[rows] + dx
            dx1_ref[rows, :] = dx1
            dx1b_ref[rows, :] = dx1.astype(BF16)
        dg_ref[...] += dg_tile

    tok = pl.BlockSpec((tm, d), lambda i: (i, 0))
    vec = pl.BlockSpec((1, d), lambda i: (0, 0))
    sq = pl.BlockSpec((d, d), lambda i: (0, 0))
    kvs = pl.BlockSpec((2 * HEADS, ml, xd), lambda i: (0, 0, 0))
    return _hosted(
        body, carry, grid=(s // tm,),
        in_specs=[tok, tok, tok, vec, tok, vec, tok, kvs, sq, sq],
        out_specs=[tok, tok, tok, tok, kvs, vec, vec],
        out_shape=(SDS((s, d), BF16), SDS((s, d), BF16), SDS((s, d), F32), SDS((s, d), BF16),
                   SDS((2 * HEADS, ml, xd), F32), SDS((1, d), F32), SDS((1, d), F32)),
        scratch_shapes=[pltpu.VMEM((tm, d), BF16)],
        compiler_params=_arb(1), name="attn_backward")(dx3, dxn3, x2, g_ffn, x1, g, q, kv, w_q, w_o)


def _kv_backward(dkv, memn, mem, g_mem, w_kv):
    ml, d = mem.shape
    xd = w_kv.shape[2]

    def body(dkv_ref, memn_ref, mem_ref, g_ref, w_ref, dw_ref, dg_ref):
        dmemn = jnp.zeros((ml, d), F32)
        for j in range(2 * HEADS):
            dkvb = dkv_ref[j].astype(BF16)
            dw_ref[j] = _dot_tn(memn_ref[...], dkvb)
            dmemn = dmemn + _dot_nt(dkvb, w_ref[j])
        x = mem_ref[...]
        dg_ref[...] = jnp.sum(dmemn * (x * _rstd(x)), axis=0, keepdims=True)

    return _pcall(body, out_shape=(SDS((2 * HEADS, d, xd), F32), SDS((1, d), F32)), name="kv_backward")(dkv, memn, mem, g_mem, w_kv)


def _mix_backward(dx1, x, g_mix, h, lng, lnb, w_sp, bt, conv_w, ga, gb, w_out, w_in, tm, carry=()):
    s, d = x.shape
    n_in = h.shape[1]
    aw = lng.shape[1]
    bw = d - aw
    hd = aw // HEADS
    in_a = 2 * aw
    hb_blocks = tm // HALO
    last_blk = s // HALO - 1
    nt = s // tm
    tc = min(ROW_CHUNK, tm)
    te = tc + HALO
    tee = tc + 2 * HALO

    def body(dx1_ref, dx1n_ref, x_ref, gm_ref, h_ref, hp_ref, hn_ref, lng_ref, lnb_ref, wsp_ref, bt_ref, cw_ref,
             ga_ref, gb_ref, wout_ref, win_ref,
             dh_ref, dx_ref, dga_ref, dgb_ref, dcw_ref, dlng_ref, dlnb_ref, dwsp_ref, dbs_ref, dgm_ref,
             mixed_s, dvln_s):
        i = pl.program_id(0)

        @pl.when(i == 0)
        def _():
            for ref in (dga_ref, dgb_ref, dcw_ref, dlng_ref, dlnb_ref, dwsp_ref, dbs_ref, dgm_ref):
                ref[...] = jnp.zeros_like(ref)

        mask = _tril_mask()
        wm = [(wsp_ref[hh] * mask).astype(BF16) for hh in range(HEADS)]
        cw = cw_ref[...]

        def chain(r0):
            rows = slice(r0, r0 + tc)
            first, last = r0 == 0, r0 + tc == tm
            hv = h_ref[rows, :]
            dx1 = dx1_ref[rows, :]
            dx1n = dx1n_ref[...] if last else dx1_ref[r0 + tc:r0 + tc + HALO, :]
            hp = hp_ref[:, in_a:] if first else h_ref[r0 - HALO:r0, in_a:]
            hn = hn_ref[:, in_a:] if last else h_ref[r0 + tc:r0 + tc + HALO, in_a:]
            dx1e = jnp.concatenate([dx1, dx1n], axis=0).astype(BF16)
            dycat = _dot_nt(dx1e, wout_ref[...])

            hbe = jnp.concatenate([hp, hv[:, in_a:], hn], axis=0)
            row = lax.broadcasted_iota(jnp.int32, (tee, 1), 0)
            zext = hbe[:, bw:2 * bw] * hbe[:, 2 * bw:]
            if first:
                zext = jnp.where((i == 0) & (row < HALO), 0.0, zext)
            z2e, z1e = _conv_taps(zext)
            conv_e = (cw[0:1] * z2e + cw[1:2] * z1e + cw[2:3] * zext)[HALO:]
            gate_b_e = hbe[HALO:, :bw]
            sc_e = gate_b_e * conv_e
            rb = _rstd(sc_e)
            dyb = dycat[:, aw:]
            gdy = dyb * gb_ref[...]
            dsc_e = rb * gdy - sc_e * (rb * rb * rb) * (jnp.sum(gdy * sc_e, axis=-1, keepdims=True) * (1.0 / bw))
            dgb_ref[...] += jnp.sum((dyb * (sc_e * rb))[:tc], axis=0, keepdims=True)
            dconv_e = dsc_e * gate_b_e
            if last:
                dconv_e = jnp.where((i == nt - 1) & (row[:te] >= tc), 0.0, dconv_e)
            dconv = dconv_e[:tc]
            dc1 = pltpu.roll(dconv_e, te - 1, 0)[:tc]
            dc2 = pltpu.roll(dconv_e, te - 2, 0)[:tc]
            dz = cw[2:3] * dconv + cw[1:2] * dc1 + cw[0:1] * dc2
            z = zext[HALO:HALO + tc]
            z1 = z1e[HALO:HALO + tc]
            z2 = z2e[HALO:HALO + tc]
            dcw_ref[0:1, :] += jnp.sum(dconv * z2, axis=0, keepdims=True)
            dcw_ref[1:2, :] += jnp.sum(dconv * z1, axis=0, keepdims=True)
            dcw_ref[2:3, :] += jnp.sum(dconv * z, axis=0, keepdims=True)
            dh_ref[rows, in_a:in_a + bw] = (dsc_e[:tc] * conv_e[:tc]).astype(BF16)
            dh_ref[rows, in_a + bw:in_a + 2 * bw] = (dz * hv[:, in_a + 2 * bw:]).astype(BF16)
            dh_ref[rows, in_a + 2 * bw:] = (dz * hv[:, in_a + bw:in_a + 2 * bw]).astype(BF16)

            ha = hv[:, :in_a]
            mixed_c, dvln_c = mixed_s.at[rows, :], dvln_s.at[rows, :]
            th, u, xhat, rl, vln = _sgu_forward(ha, lng_ref[...], lnb_ref[...], wm, bt_ref[...], mixed_c)
            mixed = mixed_c[...]
            sg = u * mixed
            dsg, dga = _rms_bwd(dycat[:tc, :aw], sg, _rstd(sg), ga_ref[...])
            dga_ref[...] += dga
            du = dsg * mixed
            dmixed = dsg * u
            dmb = dmixed.astype(BF16)
            for n in range(tc // CHUNK):
                blk = slice(n * CHUNK, (n + 1) * CHUNK)
                dbs_ref[...] += dmixed[blk]
                for hh in range(HEADS):
                    cols = slice(hh * hd, (hh + 1) * hd)
                    dvln_c[blk, cols] = _dot_tn(wm[hh], dmb[blk, cols])
                    dwsp_ref[hh] += mask * _dot_nt(dmb[blk, cols], vln[blk, cols])
            dvln = dvln_c[...]
            dlng_ref[...] += jnp.sum(dvln * xhat, axis=0, keepdims=True)
            dlnb_ref[...] += jnp.sum(dvln, axis=0, keepdims=True)
            dxh = dvln * lng_ref[...]
            dv = rl * (dxh - jnp.mean(dxh, axis=-1, keepdims=True) - xhat * jnp.mean(dxh * xhat, axis=-1, keepdims=True))
            dh_ref[rows, :in_a] = (jnp.concatenate([du, dv], axis=-1) * _gelu_grad(ha, th)).astype(BF16)

            dxn = _dot(dh_ref[rows, :], win_ref[...])
            xv = x_ref[rows, :]
            dx, dgm = _rms_bwd(dxn, xv, _rstd(xv), gm_ref[...])
            dgm_ref[...] += dgm
            dx_ref[rows, :] = dx1 + dx

        for r0 in range(0, tm, tc):
            chain(r0)

    full = lambda shape: pl.BlockSpec(shape, lambda i: (0,) * len(shape))
    tok = pl.BlockSpec((tm, d), lambda i: (i, 0))
    nxt = lambda i: (jnp.minimum((i + 1) * hb_blocks, last_blk), 0)
    prv = lambda i: (jnp.maximum(i * hb_blocks - 1, 0), 0)
    return _hosted(
        body, carry, grid=(nt,),
        in_specs=[tok, pl.BlockSpec((HALO, d), nxt), tok, full((1, d)),
                  pl.BlockSpec((tm, n_in), lambda i: (i, 0)), pl.BlockSpec((HALO, n_in), prv), pl.BlockSpec((HALO, n_in), nxt),
                  full((1, aw)), full((1, aw)), full((HEADS, CHUNK, CHUNK)), full((CHUNK, HEADS)), full((3, bw)),
                  full((1, aw)), full((1, bw)), full((d, d)), full((n_in, d))],
        out_specs=[pl.BlockSpec((tm, n_in), lambda i: (i, 0)), tok,
                   full((1, aw)), full((1, bw)), full((SUB, bw)), full((1, aw)), full((1, aw)),
                   full((HEADS, CHUNK, CHUNK)), full((CHUNK, aw)), full((1, d))],
        out_shape=(SDS((s, n_in), BF16), SDS((s, d), F32),
                   SDS((1, aw), F32), SDS((1, bw), F32), SDS((SUB, bw), F32), SDS((1, aw), F32), SDS((1, aw), F32),
                   SDS((HEADS, CHUNK, CHUNK), F32), SDS((CHUNK, aw), F32), SDS((1, d), F32)),
        scratch_shapes=[pltpu.VMEM((tm, aw), F32), pltpu.VMEM((tm, aw), F32)],
        compiler_params=_arb(1), name="mix_backward")(dx1, dx1, x, g_mix, h, h, h, lng, lnb, w_sp, bt, conv_w, ga, gb, w_out, w_in)


def _bias_grad(dbs):
    aw = dbs.shape[1]
    hd = aw // HEADS

    def body(dbs_ref, out_ref):
        ones = jnp.ones((SUB, hd), F32)
        for hh in range(HEADS):
            r = lax.dot_general(ones, dbs_ref[:, hh * hd:(hh + 1) * hd], (((1,), (1,)), ((), ())),
                                precision=lax.Precision.HIGHEST, preferred_element_type=F32)
            out_ref[hh:hh + 1, :] = r[0:1]

    return _pcall(body, out_shape=SDS((HEADS, CHUNK), F32), name="bias_grad")(dbs)


def _wgrad_body(a_ref, b_ref, o_ref):
    o_ref[...] = _dot_tn(a_ref[...], b_ref[...])


def _wgrad(a, b, name, carry=()):
    k, m = a.shape
    n = b.shape[1]
    tm = _tile(m, 512, LANES)
    tn = _tile(n, 1024, LANES)
    return _hosted(
        functools.partial(_wgrad_body), carry, grid=(m // tm, n // tn),
        in_specs=[pl.BlockSpec((k, tm), lambda i, j: (0, i)), pl.BlockSpec((k, tn), lambda i, j: (0, j))],
        out_specs=pl.BlockSpec((tm, tn), lambda i, j: (i, j)),
        out_shape=SDS((m, n), F32), compiler_params=_arb(2), name=name)(a, b)


def _wgrad_blocked_lhs(a, b, name, carry=()):
    nb, k, t = a.shape
    n = b.shape[1]
    tn = _tile(n, 1024, LANES)
    return _hosted(
        functools.partial(_wgrad_body), carry, grid=(nb, n // tn),
        in_specs=[pl.BlockSpec((None, k, t), lambda i, j: (i, 0, 0)), pl.BlockSpec((k, tn), lambda i, j: (0, j))],
        out_specs=pl.BlockSpec((t, tn), lambda i, j: (i, j)),
        out_shape=SDS((nb * t, n), F32), compiler_params=_arb(2), name=name)(a, b)


def _wgrad_blocked_rhs(a, b, name, carry=()):
    k, m = a.shape
    nb, _, t = b.shape
    tm = _tile(m, 512, LANES)
    return _hosted(
        functools.partial(_wgrad_body), carry, grid=(m // tm, nb),
        in_specs=[pl.BlockSpec((k, tm), lambda i, j: (0, i)), pl.BlockSpec((None, k, t), lambda i, j: (j, 0, 0))],
        out_specs=pl.BlockSpec((None, tm, t), lambda i, j: (j, i, 0)),
        out_shape=SDS((nb, m, t), F32), compiler_params=_arb(2), name=name)(a, b)


def _unblock_cols(wb, name, carry=()):
    nb, r, t = wb.shape
    tr = _tile(r, 256, 16)

    def body(w_ref, o_ref):
        o_ref[...] = jnp.concatenate([w_ref[j].astype(F32) for j in range(nb)], axis=-1).astype(o_ref.dtype)

    return _hosted(
        body, carry, grid=(r // tr,),
        in_specs=[pl.BlockSpec((nb, tr, t), lambda i: (0, i, 0))], out_specs=pl.BlockSpec((tr, nb * t), lambda i: (i, 0)),
        out_shape=SDS((r, nb * t), wb.dtype), compiler_params=_arb(1), name=name)(wb)


def _block_cols(w, nb, name, carry=()):
    r, n = w.shape
    t = n // nb
    tr = _tile(r, 256, 16)

    def body(w_ref, o_ref):
        wv = w_ref[...]
        for j in range(nb):
            o_ref[j] = wv[:, j * t:(j + 1) * t]

    return _hosted(
        body, carry, grid=(r // tr,),
        in_specs=[pl.BlockSpec((tr, n), lambda i: (i, 0))], out_specs=pl.BlockSpec((nb, tr, t), lambda i: (0, i, 0)),
        out_shape=SDS((nb, r, t), w.dtype), compiler_params=_arb(1), name=name)(w)


def _place():
    x, y, c = lax.axis_index("x"), lax.axis_index("y"), lax.axis_index("c")
    return x, y, c, [(1 - x, y), (x, 1 - y), (1 - x, 1 - y)]


def _all_gather(shards):
    n = len(shards)
    slots = 9
    cut = [(s.shape[0] // 32) * 16 for s in shards]

    def build(ins, outs, sems):
        send_sems, recv_sems, local_sems = sems
        x, y, c, _ = _place()
        me, sib, xn, yn, dg = (x, y, c), (x, y, 1 - c), (1 - x, y, c), (x, 1 - y, c), (1 - x, 1 - y, c)
        other = lambda p: (p[0], p[1], 1 - p[2])

        def rows(a, p, part=None):
            ref = outs[a].at[4 * p[0] + 2 * p[1] + p[2]]
            if part is None or cut[a] == 0:
                return ref if part in (None, 0) else None
            return ref.at[pl.ds(0, cut[a])] if part == 0 else ref.at[pl.ds(cut[a], shards[a].shape[0] - cut[a])]

        def copy(a, k, ref, to, src=None):
            if ref is None:
                return None
            return pltpu.make_async_remote_copy(
                src_ref=ref if src is None else src, dst_ref=ref, send_sem=send_sems.at[slots * a + k],
                recv_sem=recv_sems.at[slots * a + k], device_id=to, device_id_type=MESH)

        def real(cps):
            return [cp for cp in cps if cp is not None]

        class Copies:
            own = lambda a: [copy(a, 1, rows(a, me), xn, ins[a]), copy(a, 2, rows(a, me), yn, ins[a]),
                             copy(a, 0, rows(a, me), sib, ins[a])]
            local = lambda a: pltpu.make_async_copy(ins[a], rows(a, me), local_sems.at[a])
            from_x = lambda a: copy(a, 1, rows(a, xn), me)
            from_y = lambda a: copy(a, 2, rows(a, yn), me)
            after_x = lambda a: real([copy(a, 4, rows(a, xn, 1), yn), copy(a, 5, rows(a, xn), sib)])
            after_y = lambda a: real([copy(a, 3, rows(a, yn, 0), xn), copy(a, 6, rows(a, yn), sib)])
            diag_in = lambda a: real([copy(a, 3, rows(a, dg, 0), me), copy(a, 4, rows(a, dg, 1), me)])
            diag_on = lambda a: real([copy(a, 7, rows(a, dg, 0), sib), copy(a, 8, rows(a, dg, 1), sib)])
            from_sib = lambda a: real([copy(a, 0, rows(a, sib), me), copy(a, 5, rows(a, other(xn)), me),
                                       copy(a, 6, rows(a, other(yn)), me), copy(a, 7, rows(a, other(dg), 0), me),
                                       copy(a, 8, rows(a, other(dg), 1), me)])

        return Copies

    def start(ins, outs, sems):
        cps = build(ins, outs, sems)
        for a in range(n):
            for cp in cps.own(a):
                cp.start()
        for a in range(n):
            cps.local(a).start()

    def relay(ins, outs, sems):
        cps = build(ins, outs, sems)
        for a in range(n):
            cps.from_x(a).wait_recv()
            for cp in cps.after_x(a):
                cp.start()
            cps.from_y(a).wait_recv()
            for cp in cps.after_y(a):
                cp.start()

    def finish(ins, outs, sems):
        cps = build(ins, outs, sems)
        for a in range(n):
            for arrived, onward in zip(cps.diag_in(a), cps.diag_on(a)):
                arrived.wait_recv()
                onward.start()
        for a in range(n):
            for cp in cps.from_sib(a):
                cp.wait_recv()
            for cp in cps.own(a) + cps.after_x(a) + cps.after_y(a) + cps.diag_on(a):
                cp.wait_send()
            cps.local(a).wait()

    return _Exchange(shards, [SDS((N_DEV,) + s.shape, s.dtype) for s in shards],
                     [pltpu.SemaphoreType.DMA((slots * n,)), pltpu.SemaphoreType.DMA((slots * n,)),
                      pltpu.SemaphoreType.DMA((n,))], start, finish, relay)


def _swap_exchange(ins, out_shape, per, copies):
    def start(i, o, sems):
        for cp in copies(i, o, sems):
            cp.start()

    def finish(i, o, sems):
        for cp in copies(i, o, sems):
            cp.wait()

    n = per * len(ins)
    return _Exchange(ins, out_shape, [pltpu.SemaphoreType.DMA((n,)), pltpu.SemaphoreType.DMA((n,))], start, finish)


def _exchange_c(gs):
    def copies(ins, outs, sems):
        x, y, c, _ = _place()
        return [pltpu.make_async_remote_copy(
                    src_ref=ins[a].at[2 * k + 1 - c], dst_ref=outs[a].at[k],
                    send_sem=sems[0].at[4 * a + k], recv_sem=sems[1].at[4 * a + k],
                    device_id=(x, y, 1 - c), device_id_type=MESH)
                for a in range(len(gs)) for k in range(4)]

    return _swap_exchange(gs, [SDS((4,) + g.shape[1:], g.dtype) for g in gs], 4, copies)


def _exchange_xy(sends):
    def copies(ins, outs, sems):
        x, y, c, chips = _place()
        return [pltpu.make_async_remote_copy(
                    src_ref=ins[a].at[t], dst_ref=outs[a].at[t],
                    send_sem=sems[0].at[3 * a + t], recv_sem=sems[1].at[3 * a + t],
                    device_id=(*chips[t], c), device_id_type=MESH)
                for a in range(len(sends)) for t in range(3)]

    return _swap_exchange(sends, [SDS(s.shape, s.dtype) for s in sends], 3, copies)


def _rs_combine(g, recv, pos, name, carry=()):
    _, r, cdim = g.shape
    tr = _tile(r, 256, 16)

    def body(pos_ref, g0, r0, g1, r1, g2, r2, g3, r3, keep_ref, send_ref):
        keep_ref[...] = g0[...] + r0[...]
        send_ref[0] = (g1[...] + r1[...]).astype(BF16)
        send_ref[1] = (g2[...] + r2[...]).astype(BF16)
        send_ref[2] = (g3[...] + r3[...]).astype(BF16)

    def k_of(p, t):
        px = p[0] if t in (0, 2) else 1 - p[0]
        py = p[1] if t in (0, 1) else 1 - p[1]
        return 2 * px + py

    blk = (None, tr, cdim)
    in_specs = []
    for t in range(4):
        in_specs.append(pl.BlockSpec(blk, functools.partial(lambda j, p, t: (2 * k_of(p, t) + p[2], j, 0), t=t)))
        in_specs.append(pl.BlockSpec(blk, functools.partial(lambda j, p, t: (k_of(p, t), j, 0), t=t)))
    return _hosted(
        body, carry, n_prefetch=1, out_shape=(SDS((r, cdim), F32), SDS((3, r, cdim), BF16)),
        grid=(r // tr,), in_specs=in_specs,
        out_specs=[pl.BlockSpec((tr, cdim), lambda j, p: (j, 0)), pl.BlockSpec((3, tr, cdim), lambda j, p: (0, j, 0))],
        compiler_params=_arb(1), name=name)(pos, g, recv, g, recv, g, recv, g, recv)


def _adamw_shard(keep, recv, w, m, v, name):
    r, cdim = w.shape
    tr = _tile(r, 256, 16)

    def body(k_ref, r_ref, w_ref, m_ref, v_ref, g_ref, d_ref, nm_ref, nv_ref):
        g = ((k_ref[...] + r_ref[0].astype(F32)) + r_ref[1].astype(F32)) + r_ref[2].astype(F32)
        g_ref[...] = g
        d_ref[...], nm_ref[...], nv_ref[...] = _adamw(w_ref[...], g, m_ref[...], v_ref[...])

    blk = pl.BlockSpec((tr, cdim), lambda j: (j, 0))
    out = SDS((r, cdim), F32)
    return _pcall(body, grid=(r // tr,), in_specs=[blk, pl.BlockSpec((3, tr, cdim), lambda j: (0, j, 0)), blk, blk, blk],
                  out_specs=[blk] * 4, out_shape=(out,) * 4, compiler_params=_arb(1), name=name)(keep, recv, w, m, v)


_HBM = pl.BlockSpec(memory_space=pltpu.HBM)
_SEM = pl.BlockSpec(memory_space=pltpu.SEMAPHORE)
_SPLIT = pltpu.CompilerParams(has_side_effects=pltpu.SideEffectType.DATAFLOW_SIDE_EFFECTING)


def _split_copies(kind, n, refs):
    srcs, lands, (send_sems, recv_sems) = refs[:n], refs[n:2 * n], refs[2 * n:2 * n + 2]
    x, y, c, chips = _place()
    per = _SPLIT_COPIES[kind]
    if kind == "xy":
        ends = lambda a, t: (srcs[a].at[t], lands[a].at[t], (*chips[t], c))
    else:
        ends = lambda a, k: (srcs[a].at[2 * k + 1 - c], lands[a].at[k], (x, y, 1 - c))
    cps = []
    for a in range(n):
        for t in range(per):
            src, dst, to = ends(a, t)
            cps.append(pltpu.make_async_remote_copy(src_ref=src, dst_ref=dst, send_sem=send_sems.at[per * a + t],
                                                    recv_sem=recv_sems.at[per * a + t], device_id=to, device_id_type=MESH))
    return cps


_SPLIT_COPIES = {"xy": 3, "c": 4}


def _exchange_start(kind, arrays, name, after=None):
    n = len(arrays)
    order = [] if after is None else [after]

    def body(*refs):
        refs = refs[:2 * n] + refs[2 * n + len(order):]
        for cp in _split_copies(kind, n, refs):
            cp.start()
        refs[-1][...] = jnp.zeros_like(refs[-1])

    hbm = lambda a: pltpu.with_memory_space_constraint(a, pltpu.HBM)
    land = [a.shape if kind == "xy" else (4,) + a.shape[1:] for a in arrays]
    bufs = [pltpu.HBM(a.shape, a.dtype) for a in arrays] + [pltpu.HBM(s, a.dtype) for s, a in zip(land, arrays)]
    sems = pltpu.SemaphoreType.DMA((_SPLIT_COPIES[kind] * n,))
    res = _pcall(
        body, name=name, out_shape=(sems, sems, *bufs, SDS((SUB, LANES), F32)),
        in_specs=[_HBM] * (2 * n) + _hbm_specs(len(order)),
        out_specs=[_SEM, _SEM] + [_HBM] * (2 * n) + [pl.BlockSpec(memory_space=pltpu.VMEM)],
        input_output_aliases={k: 2 + k for k in range(2 * n)}, compiler_params=_SPLIT)(
            *[hbm(a) for a in arrays], *[hbm(lax.empty(s, a.dtype)) for s, a in zip(land, arrays)], *order)
    return (kind, n, res[:-1]), res[-1]


def _exchange_wait(started, after, name, sources=False):
    kind, n, (send_sems, recv_sems, *bufs) = started

    def body(*refs):
        for cp in _split_copies(kind, n, refs):
            cp.wait_send()
            cp.wait_recv()

    shapes = [pltpu.HBM(b.shape, b.dtype) for b in bufs]
    res = _pcall(
        body, name=name, out_shape=tuple(shapes),
        in_specs=[_HBM] * (2 * n) + [_SEM, _SEM, pl.BlockSpec(memory_space=pl.ANY)], out_specs=[_HBM] * (2 * n),
        input_output_aliases={k: k for k in range(2 * n)}, compiler_params=_SPLIT)(*bufs, send_sems, recv_sems, after)
    return (list(res[:n]), list(res[n:])) if sources else list(res[n:])


def _follow(token):
    nothing = lambda ins, outs, sems: None
    return _Exchange([token], [], [], nothing, nothing)


def _adamw_small(gathered, seg, params, conv_rows):
    names = list(params)
    c0, cn = conv_rows

    def body(*refs):
        gat_ref = refs[0]
        ins = refs[1:1 + 3 * len(names)]
        outs = refs[1 + 3 * len(names):]

        def total(r0, rn):
            tot = gat_ref[0, r0:r0 + rn, :]
            for dev in range(1, N_DEV):
                tot = tot + gat_ref[dev, r0:r0 + rn, :]
            return tot

        for k, nm in enumerate(names):
            g = total(*seg[nm])
            w_ref, m_ref, v_ref = ins[3 * k:3 * k + 3]
            g_ref, d_ref, nm_ref, nv_ref = outs[4 * k:4 * k + 4]
            g_ref[...] = g
            d_ref[...], nm_ref[...], nv_ref[...] = _adamw(w_ref[...], g, m_ref[...], v_ref[...])
        outs[-2][...] = total(c0, cn)
        outs[-1][...] = total(*seg["loss"])

    flat_in = [a for nm in names for a in params[nm]]
    out_shape = []
    for nm in names:
        out_shape += [SDS(params[nm][0].shape, F32)] * 4
    out_shape += [SDS((cn, LANES), F32), SDS((seg["loss"][1], LANES), F32)]
    res = _pcall(body, out_shape=tuple(out_shape), name="adamw_small")(gathered, *flat_in)
    per = {nm: res[4 * k:4 * k + 4] for k, nm in enumerate(names)}
    return per, res[-2], res[-1]


def _adamw_one(w, g, m, v, name):
    def body(w_ref, g_ref, m_ref, v_ref, d_ref, nm_ref, nv_ref):
        d_ref[...], nm_ref[...], nv_ref[...] = _adamw(w_ref[...], g_ref[...], m_ref[...], v_ref[...])

    return _pcall(body, out_shape=(SDS(w.shape, F32),) * 3, name=name)(w, g, m, v)


def _rows128(a):
    return a.reshape(-1, LANES)


def _pack_small(gs, loss_tile):
    seg, pieces, row = {}, [], 0
    for nm in SMALL + ("conv_w", "loss"):
        piece = loss_tile if nm == "loss" else _rows128(gs[nm])
        rn = _round_up(piece.shape[0], SUB)
        pieces.append(jnp.pad(piece, ((0, rn - piece.shape[0]), (0, 0))))
        seg[nm] = (row, piece.shape[0])
        row += rn
    return jnp.concatenate(pieces, axis=0), seg


def _step(x, mem, target, wb, conv_w, sp, pos):
    s, d = x.shape
    tm = min(TOKEN_TILE, s)
    tm_wide = min(2 * TOKEN_TILE, s)
    rows = lambda w8: w8.reshape(-1, w8.shape[2])
    shards = lambda g: g.reshape((N_DEV, -1) + g.shape[1:])
    bt = sp["b_spatial"].T

    (w_in8, conv8), = _run_exchanges([_all_gather([wb["w_in"], conv_w])], "gather_w_in")
    conv_full = conv8.transpose(1, 0, 2).reshape(3, -1)
    w_in_t = rows(w_in8)
    (xn1, h), ((w_out8, w_kv8, w_q8),) = _in_forward(
        x, sp["ln_mix_g"], w_in_t, tm, carry=[_all_gather([wb["w_out"], wb["w_kv"], wb["w_q"]])])
    w_out = rows(w_out8)
    (ycat, x1), ((w_o8, w_down8),) = _mix_forward(
        h, x, sp["sgu_ln_g"], sp["sgu_ln_b"], sp["w_spatial"], bt, conv_full, sp["grp_norm_a"], sp["grp_norm_b"], w_out, tm,
        carry=[_all_gather([wb["w_o"], wb["w_down"]])])
    w_q, w_o, w_down = rows(w_q8), rows(w_o8), rows(w_down8)
    memn, kv = _kv_forward(mem, sp["ln_mem_g"], w_kv8)
    (xn2, q, o, x2), ((w_gu8,),) = _attn_forward(
        x1, sp["ln_attn_g"], w_q, kv, w_o, tm, carry=[_all_gather([wb["w_gate_up"]])])
    w_gu = w_gu8.reshape((2, N_DEV // 2) + w_gu8.shape[1:])
    xn3, gu, x3 = _ffn_forward(x2, sp["ln_ffn_g"], w_gu, w_down, tm_wide)

    loss, d_lnf, dx3, dx3b = _final_backward(x3, target, sp["ln_final_g"], tm_wide)
    act, dgu, dxn3 = _swiglu_backward(dx3b, gu, w_gu, w_down, tm_wide)
    g_gu, _ = _wgrad_blocked_lhs(dgu.reshape((N_DEV,) + dgu.shape[2:]), xn3, "wgrad_gate_up")
    g_gu = shards(g_gu)
    g_down, ((rc_gu,),) = _wgrad_blocked_lhs(act, dx3b, "wgrad_down", carry=[_exchange_c([g_gu])])
    g_down = shards(g_down)
    keep, pending = {}, []
    (keep["w_gate_up"], send_gu), _ = _rs_combine(g_gu, rc_gu, pos, "rs_combine_w_gate_up")
    started, token = _exchange_start("xy", [send_gu], "exchange_xy_1_start")
    pending.append((("w_gate_up",), started))
    c_down, token = _exchange_start("c", [g_down], "exchange_c_1_start", after=token)
    (dx2b, dq, dx1, dx1b, dkv, d_lnattn, d_lnffn), _ = _attn_backward(
        dx3, dxn3, x2, sp["ln_ffn_g"], x1, sp["ln_attn_g"], q, kv, w_q, w_o, tm, carry=[_follow(token)])
    (g_down,), (rc_down,) = _exchange_wait(c_down, dx1b, "exchange_c_1_wait", sources=True)
    (keep["w_down"], send_down), _ = _rs_combine(g_down, rc_down, pos, "rs_combine_w_down")
    g_o, _ = _wgrad(o, dx2b, "wgrad_o")
    g_q, _ = _wgrad(xn2, dq, "wgrad_q")
    g_o, g_q = shards(g_o), shards(g_q)
    c_oq, token = _exchange_start("c", [g_o, g_q], "exchange_c_2_start")
    g_out, _ = _wgrad(ycat, dx1b, "wgrad_out", carry=[_follow(token)])
    g_out = shards(g_out)
    g_kv, d_lnmem = _kv_backward(dkv, memn, mem, sp["ln_mem_g"], w_kv8)
    (g_o, g_q), (rc_o, rc_q) = _exchange_wait(c_oq, g_out, "exchange_c_2_wait", sources=True)
    c_outkv, token = _exchange_start("c", [g_out, g_kv], "exchange_c_3_start")
    (keep["w_o"], send_o), _ = _rs_combine(g_o, rc_o, pos, "rs_combine_w_o", carry=[_follow(token)])
    (keep["w_q"], send_q), _ = _rs_combine(g_q, rc_q, pos, "rs_combine_w_q")
    (g_out, g_kv), (rc_out, rc_kv) = _exchange_wait(c_outkv, send_q, "exchange_c_3_wait", sources=True)
    (keep["w_out"], send_out), _ = _rs_combine(g_out, rc_out, pos, "rs_combine_w_out")
    (keep["w_kv"], send_kv), _ = _rs_combine(g_kv, rc_kv, pos, "rs_combine_w_kv")
    started, token = _exchange_start("xy", [send_down, send_o, send_q, send_out, send_kv], "exchange_xy_2_start")
    pending.append((("w_down", "w_o", "w_q", "w_out", "w_kv"), started))
    (dh, dx, d_ga, d_gb, d_cw, d_lng, d_lnb, d_wsp, d_bs, d_lnmix), _ = _mix_backward(
        dx1, x, sp["ln_mix_g"], h, sp["sgu_ln_g"], sp["sgu_ln_b"], sp["w_spatial"], bt, conv_full,
        sp["grp_norm_a"], sp["grp_norm_b"], w_out, w_in_t, tm, carry=[_follow(token)])
    gs = {"ln_mix_g": d_lnmix, "sgu_ln_g": d_lng, "sgu_ln_b": d_lnb, "w_spatial": d_wsp, "b_spatial": _bias_grad(d_bs),
          "conv_w": d_cw[:3], "grp_norm_a": d_ga, "grp_norm_b": d_gb, "ln_attn_g": d_lnattn, "ln_mem_g": d_lnmem,
          "ln_ffn_g": d_lnffn, "ln_final_g": d_lnf}
    packed, seg = _pack_small(gs, loss)
    g_in, (_, (small_all,)) = _wgrad(dh, xn1, "wgrad_in", carry=[_follow(token), _all_gather([packed])])
    g_in = shards(g_in)
    c_in, token = _exchange_start("c", [g_in], "exchange_c_4_start")
    return dx, keep, pending, (g_in, c_in), token, small_all, seg


def kernel(x, mem, ln_mix_g, w_in, sgu_ln_g, sgu_ln_b, w_spatial, b_spatial, conv_w, grp_norm_a, grp_norm_b, w_out, ln_attn_g, ln_mem_g, w_q, w_kv, w_o, ln_ffn_g, w_gate_up, w_down, ln_final_g, loss_target, m_ln_mix_g, m_w_in, m_sgu_ln_g, m_sgu_ln_b, m_w_spatial, m_b_spatial, m_conv_w, m_grp_norm_a, m_grp_norm_b, m_w_out, m_ln_attn_g, m_ln_mem_g, m_w_q, m_w_kv, m_w_o, m_ln_ffn_g, m_w_gate_up, m_w_down, m_ln_final_g, v_ln_mix_g, v_w_in, v_sgu_ln_g, v_sgu_ln_b, v_w_spatial, v_b_spatial, v_conv_w, v_grp_norm_a, v_grp_norm_b, v_w_out, v_ln_attn_g, v_ln_mem_g, v_w_q, v_w_kv, v_w_o, v_ln_ffn_g, v_w_gate_up, v_w_down, v_ln_final_g):
    order = ["ln_mix_g", "w_in", "sgu_ln_g", "sgu_ln_b", "w_spatial", "b_spatial", "conv_w", "grp_norm_a", "grp_norm_b",
             "w_out", "ln_attn_g", "ln_mem_g", "w_q", "w_kv", "w_o", "ln_ffn_g", "w_gate_up", "w_down", "ln_final_g"]
    W = dict(ln_mix_g=ln_mix_g, w_in=w_in, sgu_ln_g=sgu_ln_g, sgu_ln_b=sgu_ln_b, w_spatial=w_spatial, b_spatial=b_spatial,
             conv_w=conv_w, grp_norm_a=grp_norm_a, grp_norm_b=grp_norm_b, w_out=w_out, ln_attn_g=ln_attn_g,
             ln_mem_g=ln_mem_g, w_q=w_q, w_kv=w_kv, w_o=w_o, ln_ffn_g=ln_ffn_g, w_gate_up=w_gate_up, w_down=w_down,
             ln_final_g=ln_final_g)
    M = dict(ln_mix_g=m_ln_mix_g, w_in=m_w_in, sgu_ln_g=m_sgu_ln_g, sgu_ln_b=m_sgu_ln_b, w_spatial=m_w_spatial,
             b_spatial=m_b_spatial, conv_w=m_conv_w, grp_norm_a=m_grp_norm_a, grp_norm_b=m_grp_norm_b, w_out=m_w_out,
             ln_attn_g=m_ln_attn_g, ln_mem_g=m_ln_mem_g, w_q=m_w_q, w_kv=m_w_kv, w_o=m_w_o, ln_ffn_g=m_ln_ffn_g,
             w_gate_up=m_w_gate_up, w_down=m_w_down, ln_final_g=m_ln_final_g)
    V = dict(ln_mix_g=v_ln_mix_g, w_in=v_w_in, sgu_ln_g=v_sgu_ln_g, sgu_ln_b=v_sgu_ln_b, w_spatial=v_w_spatial,
             b_spatial=v_b_spatial, conv_w=v_conv_w, grp_norm_a=v_grp_norm_a, grp_norm_b=v_grp_norm_b, w_out=v_w_out,
             ln_attn_g=v_ln_attn_g, ln_mem_g=v_ln_mem_g, w_q=v_w_q, w_kv=v_w_kv, w_o=v_w_o, ln_ffn_g=v_ln_ffn_g,
             w_gate_up=v_w_gate_up, w_down=v_w_down, ln_final_g=v_ln_final_g)

    bw = conv_w.shape[1] * N_DEV
    pos = jnp.stack([lax.axis_index("x"), lax.axis_index("y"), lax.axis_index("c")]).astype(jnp.int32)
    me = 4 * pos[0] + 2 * pos[1] + pos[2]

    sp = {nm: (W[nm].reshape(1, -1) if W[nm].ndim == 1 else W[nm]) for nm in SMALL}
    view = lambda a, nm: a.T if nm in TRANSPOSED else a
    wb = {nm: view(W[nm], nm).astype(BF16) for nm in BIG}
    grad_x, keep, pending, (g_in, c_in), token, small_all, seg = _step(
        x[0], mem[0], loss_target[0], wb, conv_w, sp, pos)

    out = {}

    def update(k, names, started, token):
        landed = _exchange_wait(started, token, "exchange_xy_%d_wait" % k)
        for nm, rxy in zip(names, landed):
            res = _adamw_shard(keep[nm], rxy, view(W[nm], nm), view(M[nm], nm), view(V[nm], nm), "adamw_" + nm)
            out[nm] = tuple(view(a, nm) for a in res)
            token = res[0]
        return token

    token = update(1, *pending[0], token)
    (g_in,), (rc_in,) = _exchange_wait(c_in, token, "exchange_c_4_wait", sources=True)
    (keep["w_in"], send_in), _ = _rs_combine(g_in, rc_in, pos, "rs_combine_w_in")
    xy_in, token = _exchange_start("xy", [send_in], "exchange_xy_3_start")
    token = update(2, *pending[1], token)
    update(3, ("w_in",), xy_in, token)

    params = {nm: (_rows128(W[nm]), _rows128(M[nm]), _rows128(V[nm])) for nm in SMALL}
    per, conv_g_rows, loss_sum = _adamw_small(small_all, seg, params, seg["conv_w"])
    for nm in SMALL:
        out[nm] = tuple(a.reshape(W[nm].shape) for a in per[nm])
    conv_g = lax.dynamic_slice_in_dim(conv_g_rows.reshape(3, bw), me * conv_w.shape[1], conv_w.shape[1], axis=1)
    out["conv_w"] = (conv_g,) + tuple(_adamw_one(conv_w, conv_g, m_conv_w, v_conv_w, "adamw_conv"))

    loss = loss_sum[0, 0]
    res = [loss, grad_x[None]]
    for k in range(4):
        res += [out[nm][k] for nm in order]
    return tuple(res)
```

```python
import functools

import jax
import jax.numpy as jnp
from jax import lax
from jax.experimental import pallas as pl
from jax.experimental.pallas import tpu as pltpu

F32 = jnp.float32
BF16 = jnp.bfloat16
SDS = jax.ShapeDtypeStruct
MESH = pl.DeviceIdType.MESH

EPS = 1e-6
N_DEV = 8
HEADS = 4
CHUNK = 128
HALO = 16
SUB = 8
LANES = 128
TOKEN_TILE = 512
ROW_CHUNK = 256
RELAY_AT = 0.7

ADAM_LR = 0.001
ADAM_B1 = 0.9
ADAM_B2 = 0.999
ADAM_EPS = 1e-08
ADAM_WD = 0.01
ADAM_STEP = 10

BIG = ("w_in", "w_out", "w_q", "w_kv", "w_o", "w_gate_up", "w_down")
TRANSPOSED = ("w_in", "w_gate_up")
SMALL = ("ln_mix_g", "sgu_ln_g", "sgu_ln_b", "w_spatial", "b_spatial", "grp_norm_a", "grp_norm_b",
         "ln_attn_g", "ln_mem_g", "ln_ffn_g", "ln_final_g")


class _Exchange:
    def __init__(self, ins, out_shape, sems, start, finish, relay=None):
        self.ins, self.out_shape, self.sems = list(ins), list(out_shape), list(sems)
        self.start, self.finish, self.relay = start, finish, relay


def _pcall(body, carry=(), n_prefetch=0, **kw):
    if carry:
        return functools.partial(_carrying_call, body, tuple(carry), n_prefetch, kw)
    if n_prefetch:
        kw["grid_spec"] = pltpu.PrefetchScalarGridSpec(
            num_scalar_prefetch=n_prefetch, grid=kw.pop("grid"), in_specs=kw.pop("in_specs"),
            out_specs=kw.pop("out_specs"), scratch_shapes=kw.pop("scratch_shapes", ()))
    return pl.pallas_call(body, **kw)


def _carrying_call(body, carry, n_prefetch, kw, *args):
    kw = dict(kw)
    out_shape = kw.pop("out_shape")
    single = not isinstance(out_shape, (tuple, list))
    outs_shape = (out_shape,) if single else tuple(out_shape)
    out_specs = kw.pop("out_specs")
    out_specs = [out_specs] if single else list(out_specs)
    in_specs = list(kw.pop("in_specs"))
    scratch = list(kw.pop("scratch_shapes", ()))
    grid = tuple(kw.get("grid", ()))
    n_in, n_out, n_scr = len(args), len(outs_shape), len(scratch)

    def split(refs, k, counts):
        parts = []
        for cnt in counts:
            parts.append(refs[k:k + cnt])
            k += cnt
        return parts, k

    def wrapped(*refs):
        cins, k = split(refs, n_in, [len(p.ins) for p in carry])
        outs = refs[k:k + n_out]
        couts, k = split(refs, k + n_out, [len(p.out_shape) for p in carry])
        scr = refs[k:k + n_scr]
        csems, _ = split(refs, k + n_scr, [len(p.sems) for p in carry])
        first, last = True, True
        for a, g in enumerate(grid):
            first = (pl.program_id(a) == 0) & first
            last = (pl.program_id(a) == g - 1) & last

        def start_all():
            for p, ci, co, cs in zip(carry, cins, couts, csems):
                p.start(ci, co, cs)

        def relay_all():
            for p, ci, co, cs in zip(carry, cins, couts, csems):
                if p.relay is not None:
                    p.relay(ci, co, cs)

        def finish_all():
            for p, ci, co, cs in zip(carry, cins, couts, csems):
                p.finish(ci, co, cs)

        if len(grid) == 1:
            relay_now = pl.program_id(0) == min(int(RELAY_AT * grid[0]), grid[0] - 1)
        else:
            relay_now = last
        start_all() if not grid else pl.when(first)(start_all)
        relay_all() if not grid else pl.when(relay_now)(relay_all)
        body(*refs[:n_in], *outs, *scr)
        finish_all() if not grid else pl.when(last)(finish_all)

    c_in = [a for p in carry for a in p.ins]
    c_out = [s for p in carry for s in p.out_shape]
    c_sems = [s for p in carry for s in p.sems]
    res = _pcall(wrapped, n_prefetch=n_prefetch, out_shape=outs_shape + tuple(c_out),
                 in_specs=in_specs + _hbm_specs(len(c_in)), out_specs=out_specs + _hbm_specs(len(c_out)),
                 scratch_shapes=scratch + c_sems, **kw)(*args, *c_in)
    own = res[0] if single else tuple(res[:n_out])
    landed, k = [], n_out
    for p in carry:
        landed.append(list(res[k:k + len(p.out_shape)]))
        k += len(p.out_shape)
    return own, landed


def _hbm_specs(n):
    return [pl.BlockSpec(memory_space=pl.ANY)] * n


def _hosted(body, carry, **kw):
    if carry:
        return _pcall(body, carry=carry, **kw)
    call = _pcall(body, **kw)
    return lambda *args: (call(*args), [])


def _run_exchanges(parts, name):
    def body(*refs):
        pass

    _, landed = _pcall(body, carry=parts, out_shape=(), in_specs=[], out_specs=[], name=name)()
    return landed


def _arb(n):
    return pltpu.CompilerParams(dimension_semantics=("arbitrary",) * n)


def _tile(n, target, mult):
    best = None
    for t in range(mult, min(n, target) + 1, mult):
        if n % t == 0:
            best = t
    return n if best is None else best


def _round_up(n, m):
    return (n + m - 1) // m * m


def _dot(a, b):
    return jnp.dot(a, b, preferred_element_type=F32)


def _dot_nt(a, b):
    return lax.dot_general(a, b, (((1,), (1,)), ((), ())), preferred_element_type=F32)


def _dot_tn(a, b):
    return lax.dot_general(a, b, (((0,), (0,)), ((), ())), preferred_element_type=F32)


def _rstd(x):
    return lax.rsqrt(jnp.mean(x * x, axis=-1, keepdims=True) + EPS)


def _rms_bwd(dy, x, r, g):
    gdy = dy * g
    proj = jnp.sum(gdy * x, axis=-1, keepdims=True) * (1.0 / x.shape[-1])
    dx = r * gdy - x * (r * r * r) * proj
    dg = jnp.sum(dy * (x * r), axis=0, keepdims=True)
    return dx, dg


_GELU_C = 0.7978845608028654
_GELU_A = 0.044715


def _gelu(x):
    t = jnp.tanh(_GELU_C * (x + _GELU_A * x * x * x))
    return 0.5 * x * (1.0 + t), t


def _gelu_grad(x, t):
    return 0.5 * (1.0 + t) + 0.5 * x * (1.0 - t * t) * (_GELU_C * (1.0 + 3.0 * _GELU_A * x * x))


def _sigmoid(x):
    return 1.0 / (1.0 + jnp.exp(-x))


def _softmax(s):
    m = jnp.max(s, axis=-1, keepdims=True)
    e = jnp.exp(s - m)
    return e / jnp.sum(e, axis=-1, keepdims=True)


def _adamw(w, g, m, v):
    m = ADAM_B1 * m + (1.0 - ADAM_B1) * g
    v = ADAM_B2 * v + (1.0 - ADAM_B2) * (g * g)
    m_hat = m / (1.0 - ADAM_B1 ** ADAM_STEP)
    v_hat = v / (1.0 - ADAM_B2 ** ADAM_STEP)
    delta = -ADAM_LR * (m_hat / (jnp.sqrt(v_hat) + ADAM_EPS) + ADAM_WD * w)
    return delta, m, v


def _tril_mask():
    t = lax.broadcasted_iota(jnp.int32, (CHUNK, CHUNK), 0)
    s = lax.broadcasted_iota(jnp.int32, (CHUNK, CHUNK), 1)
    return (s <= t).astype(F32)


def _sgu_forward(ha, lng, lnb, wm, bt, mixed_s):
    aw = ha.shape[1] // 2
    hd = aw // HEADS
    a, th = _gelu(ha)
    u = a[:, :aw]
    v = a[:, aw:]
    mu = jnp.mean(v, axis=-1, keepdims=True)
    vc = v - mu
    rl = lax.rsqrt(jnp.mean(vc * vc, axis=-1, keepdims=True) + EPS)
    xhat = vc * rl
    vln = (xhat * lng + lnb).astype(BF16)
    for n in range(ha.shape[0] // CHUNK):
        rows = slice(n * CHUNK, (n + 1) * CHUNK)
        for h in range(HEADS):
            cols = slice(h * hd, (h + 1) * hd)
            mixed_s[rows, cols] = _dot(wm[h], vln[rows, cols]) + bt[:, h:h + 1]
    return th, u, xhat, rl, vln


def _conv_taps(zext):
    return pltpu.roll(zext, 2, 0), pltpu.roll(zext, 1, 0)


def _kv_forward(mem, g_mem, w_kv):
    ml, d = mem.shape
    xd = w_kv.shape[2]

    def body(mem_ref, g_ref, w_ref, memn_ref, kv_ref):
        x = mem_ref[...]
        memn = (x * _rstd(x) * g_ref[...]).astype(BF16)
        memn_ref[...] = memn
        for j in range(2 * HEADS):
            kv_ref[j] = _dot(memn, w_ref[j]).astype(BF16)

    return _pcall(body, out_shape=(SDS((ml, d), BF16), SDS((2 * HEADS, ml, xd), BF16)), name="kv_forward")(mem, g_mem, w_kv)


def _in_forward(x, g, w_in_t, tm, carry=()):
    s, d = x.shape
    n_in = w_in_t.shape[0]

    def body(x_ref, g_ref, w_ref, xn_ref, h_ref):
        xv = x_ref[...]
        xn = (xv * _rstd(xv) * g_ref[...]).astype(BF16)
        xn_ref[...] = xn
        h_ref[...] = _dot_nt(xn, w_ref[...])

    return _hosted(
        body, carry, grid=(s // tm,),
        in_specs=[pl.BlockSpec((tm, d), lambda i: (i, 0)), pl.BlockSpec((1, d), lambda i: (0, 0)),
                  pl.BlockSpec((n_in, d), lambda i: (0, 0))],
        out_specs=[pl.BlockSpec((tm, d), lambda i: (i, 0)), pl.BlockSpec((tm, n_in), lambda i: (i, 0))],
        out_shape=(SDS((s, d), BF16), SDS((s, n_in), F32)),
        compiler_params=_arb(1), name="in_forward")(x, g, w_in_t)


def _mix_forward(h, x, lng, lnb, w_sp, bt, conv_w, ga, gb, w_out, tm, carry=()):
    s, d = x.shape
    n_in = h.shape[1]
    aw = lng.shape[1]
    bw = d - aw
    in_a = 2 * aw
    hb_blocks = tm // HALO

    def body(h_ref, hprev_ref, x_ref, lng_ref, lnb_ref, wsp_ref, bt_ref, cw_ref, ga_ref, gb_ref, wout_ref,
             ycat_ref, x1_ref, mixed_s):
        i = pl.program_id(0)
        mask = _tril_mask()
        wm = [(wsp_ref[hh] * mask).astype(BF16) for hh in range(HEADS)]
        hv = h_ref[...]
        _, u, _, _, _ = _sgu_forward(hv[:, :in_a], lng_ref[...], lnb_ref[...], wm, bt_ref[...], mixed_s)
        sg = u * mixed_s[...]
        ycat_ref[:, :aw] = (sg * _rstd(sg) * ga_ref[...]).astype(BF16)

        gate_b = hv[:, in_a:in_a + bw]
        z = hv[:, in_a + bw:in_a + 2 * bw] * hv[:, in_a + 2 * bw:]
        hp = hprev_ref[...]
        zp = hp[:, in_a + bw:in_a + 2 * bw] * hp[:, in_a + 2 * bw:]
        zp = jnp.where(i == 0, 0.0, zp)
        zext = jnp.concatenate([zp, z], axis=0)
        z2, z1 = _conv_taps(zext)
        cw = cw_ref[...]
        conv = cw[0:1] * z2[HALO:] + cw[1:2] * z1[HALO:] + cw[2:3] * z
        sc = gate_b * conv
        ycat_ref[:, aw:] = (sc * _rstd(sc) * gb_ref[...]).astype(BF16)
        x1_ref[...] = x_ref[...] + _dot(ycat_ref[...], wout_ref[...])

    full = lambda shape: pl.BlockSpec(shape, lambda i: (0,) * len(shape))
    return _hosted(
        body, carry, grid=(s // tm,),
        in_specs=[pl.BlockSpec((tm, n_in), lambda i: (i, 0)),
                  pl.BlockSpec((HALO, n_in), lambda i: (jnp.maximum(i * hb_blocks - 1, 0), 0)),
                  pl.BlockSpec((tm, d), lambda i: (i, 0)),
                  full((1, aw)), full((1, aw)), full((HEADS, CHUNK, CHUNK)), full((CHUNK, HEADS)),
                  full((3, bw)), full((1, aw)), full((1, bw)), full((d, d))],
        out_specs=[pl.BlockSpec((tm, d), lambda i: (i, 0)), pl.BlockSpec((tm, d), lambda i: (i, 0))],
        out_shape=(SDS((s, d), BF16), SDS((s, d), F32)),
        scratch_shapes=[pltpu.VMEM((tm, aw), F32)],
        compiler_params=_arb(1), name="mix_forward")(h, h, x, lng, lnb, w_sp, bt, conv_w, ga, gb, w_out)


def _attn_forward(x1, g, w_q, kv, w_o, tm, carry=()):
    s, d = x1.shape
    _, ml, xd = kv.shape
    scale = xd ** -0.5

    def body(x1_ref, g_ref, wq_ref, kv_ref, wo_ref, xn_ref, q_ref, o_ref, x2_ref):
        xv = x1_ref[...]
        xn = (xv * _rstd(xv) * g_ref[...]).astype(BF16)
        xn_ref[...] = xn
        q_ref[...] = _dot(xn, wq_ref[...]).astype(BF16)
        for hh in range(HEADS):
            cols = slice(hh * xd, (hh + 1) * xd)
            p = _softmax(_dot_nt(q_ref[:, cols], kv_ref[hh]) * scale)
            o_ref[:, cols] = _dot(p.astype(BF16), kv_ref[HEADS + hh]).astype(BF16)
        x2_ref[...] = xv + _dot(o_ref[...], wo_ref[...])

    tok = pl.BlockSpec((tm, d), lambda i: (i, 0))
    return _hosted(
        body, carry, grid=(s // tm,),
        in_specs=[tok, pl.BlockSpec((1, d), lambda i: (0, 0)), pl.BlockSpec((d, d), lambda i: (0, 0)),
                  pl.BlockSpec((2 * HEADS, ml, xd), lambda i: (0, 0, 0)), pl.BlockSpec((d, d), lambda i: (0, 0))],
        out_specs=[tok, tok, tok, tok],
        out_shape=(SDS((s, d), BF16), SDS((s, d), BF16), SDS((s, d), BF16), SDS((s, d), F32)),
        compiler_params=_arb(1), name="attn_forward")(x1, g, w_q, kv, w_o)


def _ffn_forward(x2, g, w_gu, w_down, tm):
    s, d = x2.shape
    _, nf, tf, _ = w_gu.shape

    def body(x2_ref, g_ref, wgu_ref, wd_ref, xn_ref, gu_ref, x3_ref):
        f = pl.program_id(1)

        @pl.when(f == 0)
        def _():
            xv = x2_ref[...]
            xn_ref[...] = (xv * _rstd(xv) * g_ref[...]).astype(BF16)
            x3_ref[...] = xv

        xn = xn_ref[...]
        gate = _dot_nt(xn, wgu_ref[0])
        up = _dot_nt(xn, wgu_ref[1])
        gu_ref[0] = gate.astype(BF16)
        gu_ref[1] = up.astype(BF16)
        act = (gate * _sigmoid(gate) * up).astype(BF16)
        x3_ref[...] += _dot(act, wd_ref[...])

    tok = pl.BlockSpec((tm, d), lambda i, f: (i, 0))
    return _pcall(
        body, grid=(s // tm, nf),
        in_specs=[tok, pl.BlockSpec((1, d), lambda i, f: (0, 0)),
                  pl.BlockSpec((2, None, tf, d), lambda i, f: (0, f, 0, 0)),
                  pl.BlockSpec((tf, d), lambda i, f: (f, 0))],
        out_specs=[tok, pl.BlockSpec((2, None, tm, tf), lambda i, f: (0, f, i, 0)), tok],
        out_shape=(SDS((s, d), BF16), SDS((2, nf, s, tf), BF16), SDS((s, d), F32)),
        compiler_params=_arb(2), name="ffn_forward")(x2, g, w_gu, w_down)


def _final_backward(x3, target, g_final, tm):
    s, d = x3.shape

    def body(x3_ref, tgt_ref, gf_ref, loss_ref, dgf_ref, dx3_ref, dx3b_ref):
        @pl.when(pl.program_id(0) == 0)
        def _():
            loss_ref[...] = jnp.zeros_like(loss_ref)
            dgf_ref[...] = jnp.zeros_like(dgf_ref)

        xv = x3_ref[...]
        r = _rstd(xv)
        diff = xv * r * gf_ref[...] - tgt_ref[...]
        loss_ref[...] += 0.5 * jnp.sum(jnp.sum(diff * diff, axis=-1, keepdims=True), axis=0, keepdims=True) * (1.0 / d)
        dx3, dgf = _rms_bwd(diff * (1.0 / d), xv, r, gf_ref[...])
        dgf_ref[...] += dgf
        dx3_ref[...] = dx3
        dx3b_ref[...] = dx3.astype(BF16)

    tok = pl.BlockSpec((tm, d), lambda i: (i, 0))
    vec = pl.BlockSpec((1, d), lambda i: (0, 0))
    return _pcall(
        body, grid=(s // tm,), in_specs=[tok, tok, vec],
        out_specs=[pl.BlockSpec((SUB, LANES), lambda i: (0, 0)), vec, tok, tok],
        out_shape=(SDS((SUB, LANES), F32), SDS((1, d), F32), SDS((s, d), F32), SDS((s, d), BF16)),
        compiler_params=_arb(1), name="final_backward")(x3, target, g_final)


def _swiglu_backward(dx3b, gu, w_gu, w_down, tm):
    s, d = dx3b.shape
    _, nf, tf, _ = w_gu.shape

    def body(dx3b_ref, gu_ref, wgu_ref, wd_ref, act_ref, dgu_ref, dxn_ref):
        @pl.when(pl.program_id(1) == 0)
        def _():
            dxn_ref[...] = jnp.zeros_like(dxn_ref)

        for r0 in range(0, tm, ROW_CHUNK):
            rows = slice(r0, r0 + ROW_CHUNK)
            dact = _dot_nt(dx3b_ref[rows, :], wd_ref[...])
            gv = gu_ref[0, rows, :].astype(F32)
            uv = gu_ref[1, rows, :].astype(F32)
            sg = _sigmoid(gv)
            silu = gv * sg
            act_ref[rows, :] = (silu * uv).astype(BF16)
            dgate = (dact * uv * (sg * (1.0 + gv * (1.0 - sg)))).astype(BF16)
            dup = (dact * silu).astype(BF16)
            dgu_ref[0, rows, :] = dgate
            dgu_ref[1, rows, :] = dup
            part = _dot(dgate, wgu_ref[0]) + _dot(dup, wgu_ref[1])
            dxn_ref[rows, :] += part

    tok = pl.BlockSpec((tm, d), lambda i, f: (i, 0))
    pair = pl.BlockSpec((2, None, tm, tf), lambda i, f: (0, f, i, 0))
    return _pcall(
        body, grid=(s // tm, nf),
        in_specs=[tok, pair, pl.BlockSpec((2, None, tf, d), lambda i, f: (0, f, 0, 0)),
                  pl.BlockSpec((tf, d), lambda i, f: (f, 0))],
        out_specs=[pl.BlockSpec((None, tm, tf), lambda i, f: (f, i, 0)), pair, tok],
        out_shape=(SDS((nf, s, tf), BF16), SDS((2, nf, s, tf), BF16), SDS((s, d), F32)),
        compiler_params=_arb(2), name="swiglu_backward")(dx3b, gu, w_gu, w_down)


def _attn_backward(dx3, dxn3, x2, g_ffn, x1, g, q, kv, w_q, w_o, tm, carry=()):
    s, d = x1.shape
    _, ml, xd = kv.shape
    scale = xd ** -0.5

    def body(dx3_ref, dxn3_ref, x2_ref, g2_ref, x1_ref, g_ref, q_ref, kv_ref, wq_ref, wo_ref,
             dx2b_ref, dq_ref, dx1_ref, dx1b_ref, dkv_ref, dg_ref, dg2_ref, do_s):
        i = pl.program_id(0)

        @pl.when(i == 0)
        def _():
            dkv_ref[...] = jnp.zeros_like(dkv_ref)
            dg_ref[...] = jnp.zeros_like(dg_ref)
            dg2_ref[...] = jnp.zeros_like(dg2_ref)

        x2v = x2_ref[...]
        dx2n, dg2 = _rms_bwd(dxn3_ref[...], x2v, _rstd(x2v), g2_ref[...])
        dg2_ref[...] += dg2
        dx2 = dx3_ref[...] + dx2n
        dx2b_ref[...] = dx2.astype(BF16)
        do_s[...] = _dot_nt(dx2b_ref[...], wo_ref[...]).astype(BF16)
        for hh in range(HEADS):
            kc = slice(hh * xd, (hh + 1) * xd)
            qh = q_ref[:, kc]
            kh = kv_ref[hh]
            doh = do_s[:, kc]
            p = _softmax(_dot_nt(qh, kh) * scale)
            dp = _dot_nt(doh, kv_ref[HEADS + hh])
            dkv_ref[HEADS + hh] += _dot_tn(p.astype(BF16), doh)
            ds = (p * (dp - jnp.sum(dp * p, axis=-1, keepdims=True)) * scale).astype(BF16)
            dq_ref[:, kc] = _dot(ds, kh).astype(BF16)
            dkv_ref[hh] += _dot_tn(ds, qh)
        dxn = _dot_nt(dq_ref[...], wq_ref[...])
        xv = x1_ref[...]
        dx, dg = _rms_bwd(dxn, xv, _rstd(xv), g_ref[...])
        dg_ref[...] += dg
        dx1 = dx2 + dx
        dx1_ref[...] = dx1
        dx1b_ref[...] = dx1.astype(BF16)

    tok = pl.BlockSpec((tm, d), lambda i: (i, 0))
    vec = pl.BlockSpec((1, d), lambda i: (0, 0))
    sq = pl.BlockSpec((d, d), lambda i: (0, 0))
    kvs = pl.BlockSpec((2 * HEADS, ml, xd), lambda i: (0, 0, 0))
    return _hosted(
        body, carry, grid=(s // tm,),
        in_specs=[tok, tok, tok, vec, tok, vec, tok, kvs, sq, sq],
        out_specs=[tok, tok, tok, tok, kvs, vec, vec],
        out_shape=(SDS((s, d), BF16), SDS((s, d), BF16), SDS((s, d), F32), SDS((s, d), BF16),
                   SDS((2 * HEADS, ml, xd), F32), SDS((1, d), F32), SDS((1, d), F32)),
        scratch_shapes=[pltpu.VMEM((tm, d), BF16)],
        compiler_params=_arb(1), name="attn_backward")(dx3, dxn3, x2, g_ffn, x1, g, q, kv, w_q, w_o)


def _kv_backward(dkv, memn, mem, g_mem, w_kv):
    ml, d = mem.shape
    xd = w_kv.shape[2]

    def body(dkv_ref, memn_ref, mem_ref, g_ref, w_ref, dw_ref, dg_ref):
        dmemn = jnp.zeros((ml, d), F32)
        for j in range(2 * HEADS):
            dkvb = dkv_ref[j].astype(BF16)
            dw_ref[j] = _dot_tn(memn_ref[...], dkvb)
            dmemn = dmemn + _dot_nt(dkvb, w_ref[j])
        x = mem_ref[...]
        dg_ref[...] = jnp.sum(dmemn * (x * _rstd(x)), axis=0, keepdims=True)

    return _pcall(body, out_shape=(SDS((2 * HEADS, d, xd), F32), SDS((1, d), F32)), name="kv_backward")(dkv, memn, mem, g_mem, w_kv)


def _mix_backward(dx1, x, g_mix, h, lng, lnb, w_sp, bt, conv_w, ga, gb, w_out, w_in, tm, carry=()):
    s, d = x.shape
    n_in = h.shape[1]
    aw = lng.shape[1]
    bw = d - aw
    hd = aw // HEADS
    in_a = 2 * aw
    hb_blocks = tm // HALO
    last_blk = s // HALO - 1
    nt = s // tm
    tc = tm
    te = tc + HALO
    tee = tc + 2 * HALO

    def body(dx1_ref, dx1n_ref, x_ref, gm_ref, h_ref, hp_ref, hn_ref, lng_ref, lnb_ref, wsp_ref, bt_ref, cw_ref,
             ga_ref, gb_ref, wout_ref, win_ref,
             dh_ref, dx_ref, dga_ref, dgb_ref, dcw_ref, dlng_ref, dlnb_ref, dwsp_ref, dbs_ref, dgm_ref,
             mixed_s, dvln_s):
        i = pl.program_id(0)

        @pl.when(i == 0)
        def _():
            for ref in (dga_ref, dgb_ref, dcw_ref, dlng_ref, dlnb_ref, dwsp_ref, dbs_ref, dgm_ref):
                ref[...] = jnp.zeros_like(ref)

        mask = _tril_mask()
        wm = [(wsp_ref[hh] * mask).astype(BF16) for hh in range(HEADS)]
        cw = cw_ref[...]

        def chain(r0):
            rows = slice(r0, r0 + tc)
            first, last = r0 == 0, r0 + tc == tm
            hv = h_ref[rows, :]
            dx1 = dx1_ref[rows, :]
            dx1n = dx1n_ref[...] if last else dx1_ref[r0 + tc:r0 + tc + HALO, :]
            hp = hp_ref[:, in_a:] if first else h_ref[r0 - HALO:r0, in_a:]
            hn = hn_ref[:, in_a:] if last else h_ref[r0 + tc:r0 + tc + HALO, in_a:]
            dx1e = jnp.concatenate([dx1, dx1n], axis=0).astype(BF16)
            dycat = _dot_nt(dx1e, wout_ref[...])

            hbe = jnp.concatenate([hp, hv[:, in_a:], hn], axis=0)
            row = lax.broadcasted_iota(jnp.int32, (tee, 1), 0)
            zext = hbe[:, bw:2 * bw] * hbe[:, 2 * bw:]
            if first:
                zext = jnp.where((i == 0) & (row < HALO), 0.0, zext)
            z2e, z1e = _conv_taps(zext)
            conv_e = (cw[0:1] * z2e + cw[1:2] * z1e + cw[2:3] * zext)[HALO:]
            gate_b_e = hbe[HALO:, :bw]
            sc_e = gate_b_e * conv_e
            rb = _rstd(sc_e)
            dyb = dycat[:, aw:]
            gdy = dyb * gb_ref[...]
            dsc_e = rb * gdy - sc_e * (rb * rb * rb) * (jnp.sum(gdy * sc_e, axis=-1, keepdims=True) * (1.0 / bw))
            dgb_ref[...] += jnp.sum((dyb * (sc_e * rb))[:tc], axis=0, keepdims=True)
            dconv_e = dsc_e * gate_b_e
            if last:
                dconv_e = jnp.where((i == nt - 1) & (row[:te] >= tc), 0.0, dconv_e)
            dconv = dconv_e[:tc]
            dc1 = pltpu.roll(dconv_e, te - 1, 0)[:tc]
            dc2 = pltpu.roll(dconv_e, te - 2, 0)[:tc]
            dz = cw[2:3] * dconv + cw[1:2] * dc1 + cw[0:1] * dc2
            z = zext[HALO:HALO + tc]
            z1 = z1e[HALO:HALO + tc]
            z2 = z2e[HALO:HALO + tc]
            dcw_ref[0:1, :] += jnp.sum(dconv * z2, axis=0, keepdims=True)
            dcw_ref[1:2, :] += jnp.sum(dconv * z1, axis=0, keepdims=True)
            dcw_ref[2:3, :] += jnp.sum(dconv * z, axis=0, keepdims=True)
            dh_ref[rows, in_a:in_a + bw] = (dsc_e[:tc] * conv_e[:tc]).astype(BF16)
            dh_ref[rows, in_a + bw:in_a + 2 * bw] = (dz * hv[:, in_a + 2 * bw:]).astype(BF16)
            dh_ref[rows, in_a + 2 * bw:] = (dz * hv[:, in_a + bw:in_a + 2 * bw]).astype(BF16)

            ha = hv[:, :in_a]
            mixed_c, dvln_c = mixed_s.at[rows, :], dvln_s.at[rows, :]
            th, u, xhat, rl, vln = _sgu_forward(ha, lng_ref[...], lnb_ref[...], wm, bt_ref[...], mixed_c)
            mixed = mixed_c[...]
            sg = u * mixed
            dsg, dga = _rms_bwd(dycat[:tc, :aw], sg, _rstd(sg), ga_ref[...])
            dga_ref[...] += dga
            du = dsg * mixed
            dmixed = dsg * u
            dmb = dmixed.astype(BF16)
            for n in range(tc // CHUNK):
                blk = slice(n * CHUNK, (n + 1) * CHUNK)
                dbs_ref[...] += dmixed[blk]
                for hh in range(HEADS):
                    cols = slice(hh * hd, (hh + 1) * hd)
                    dvln_c[blk, cols] = _dot_tn(wm[hh], dmb[blk, cols])
                    dwsp_ref[hh] += mask * _dot_nt(dmb[blk, cols], vln[blk, cols])
            dvln = dvln_c[...]
            dlng_ref[...] += jnp.sum(dvln * xhat, axis=0, keepdims=True)
            dlnb_ref[...] += jnp.sum(dvln, axis=0, keepdims=True)
            dxh = dvln * lng_ref[...]
            dv = rl * (dxh - jnp.mean(dxh, axis=-1, keepdims=True) - xhat * jnp.mean(dxh * xhat, axis=-1, keepdims=True))
            dh_ref[rows, :in_a] = (jnp.concatenate([du, dv], axis=-1) * _gelu_grad(ha, th)).astype(BF16)

            dxn = _dot(dh_ref[rows, :], win_ref[...])
            xv = x_ref[rows, :]
            dx, dgm = _rms_bwd(dxn, xv, _rstd(xv), gm_ref[...])
            dgm_ref[...] += dgm
            dx_ref[rows, :] = dx1 + dx

        for r0 in range(0, tm, tc):
            chain(r0)

    full = lambda shape: pl.BlockSpec(shape, lambda i: (0,) * len(shape))
    tok = pl.BlockSpec((tm, d), lambda i: (i, 0))
    nxt = lambda i: (jnp.minimum((i + 1) * hb_blocks, last_blk), 0)
    prv = lambda i: (jnp.maximum(i * hb_blocks - 1, 0), 0)
    return _hosted(
        body, carry, grid=(nt,),
        in_specs=[tok, pl.BlockSpec((HALO, d), nxt), tok, full((1, d)),
                  pl.BlockSpec((tm, n_in), lambda i: (i, 0)), pl.BlockSpec((HALO, n_in), prv), pl.BlockSpec((HALO, n_in), nxt),
                  full((1, aw)), full((1, aw)), full((HEADS, CHUNK, CHUNK)), full((CHUNK, HEADS)), full((3, bw)),
                  full((1, aw)), full((1, bw)), full((d, d)), full((n_in, d))],
        out_specs=[pl.BlockSpec((tm, n_in), lambda i: (i, 0)), tok,
                   full((1, aw)), full((1, bw)), full((SUB, bw)), full((1, aw)), full((1, aw)),
                   full((HEADS, CHUNK, CHUNK)), full((CHUNK, aw)), full((1, d))],
        out_shape=(SDS((s, n_in), BF16), SDS((s, d), F32),
                   SDS((1, aw), F32), SDS((1, bw), F32), SDS((SUB, bw), F32), SDS((1, aw), F32), SDS((1, aw), F32),
                   SDS((HEADS, CHUNK, CHUNK), F32), SDS((CHUNK, aw), F32), SDS((1, d), F32)),
        scratch_shapes=[pltpu.VMEM((tm, aw), F32), pltpu.VMEM((tm, aw), F32)],
        compiler_params=_arb(1), name="mix_backward")(dx1, dx1, x, g_mix, h, h, h, lng, lnb, w_sp, bt, conv_w, ga, gb, w_out, w_in)


def _bias_grad(dbs):
    aw = dbs.shape[1]
    hd = aw // HEADS

    def body(dbs_ref, out_ref):
        ones = jnp.ones((SUB, hd), F32)
        for hh in range(HEADS):
            r = lax.dot_general(ones, dbs_ref[:, hh * hd:(hh + 1) * hd], (((1,), (1,)), ((), ())),
                                precision=lax.Precision.HIGHEST, preferred_element_type=F32)
            out_ref[hh:hh + 1, :] = r[0:1]

    return _pcall(body, out_shape=SDS((HEADS, CHUNK), F32), name="bias_grad")(dbs)


def _wgrad_body(a_ref, b_ref, o_ref):
    o_ref[...] = _dot_tn(a_ref[...], b_ref[...])


def _wgrad(a, b, name, carry=()):
    k, m = a.shape
    n = b.shape[1]
    tm = _tile(m, 512, LANES)
    tn = _tile(n, 1024, LANES)
    return _hosted(
        functools.partial(_wgrad_body), carry, grid=(m // tm, n // tn),
        in_specs=[pl.BlockSpec((k, tm), lambda i, j: (0, i)), pl.BlockSpec((k, tn), lambda i, j: (0, j))],
        out_specs=pl.BlockSpec((tm, tn), lambda i, j: (i, j)),
        out_shape=SDS((m, n), F32), compiler_params=_arb(2), name=name)(a, b)


def _wgrad_blocked_lhs(a, b, name, carry=()):
    nb, k, t = a.shape
    n = b.shape[1]
    tn = _tile(n, 1024, LANES)
    return _hosted(
        functools.partial(_wgrad_body), carry, grid=(nb, n // tn),
        in_specs=[pl.BlockSpec((None, k, t), lambda i, j: (i, 0, 0)), pl.BlockSpec((k, tn), lambda i, j: (0, j))],
        out_specs=pl.BlockSpec((t, tn), lambda i, j: (i, j)),
        out_shape=SDS((nb * t, n), F32), compiler_params=_arb(2), name=name)(a, b)


def _wgrad_blocked_rhs(a, b, name, carry=()):
    k, m = a.shape
    nb, _, t = b.shape
    tm = _tile(m, 512, LANES)
    return _hosted(
        functools.partial(_wgrad_body), carry, grid=(m // tm, nb),
        in_specs=[pl.BlockSpec((k, tm), lambda i, j: (0, i)), pl.BlockSpec((None, k, t), lambda i, j: (j, 0, 0))],
        out_specs=pl.BlockSpec((None, tm, t), lambda i, j: (j, i, 0)),
        out_shape=SDS((nb, m, t), F32), compiler_params=_arb(2), name=name)(a, b)


def _unblock_cols(wb, name, carry=()):
    nb, r, t = wb.shape
    tr = _tile(r, 256, 16)

    def body(w_ref, o_ref):
        o_ref[...] = jnp.concatenate([w_ref[j].astype(F32) for j in range(nb)], axis=-1).astype(o_ref.dtype)

    return _hosted(
        body, carry, grid=(r // tr,),
        in_specs=[pl.BlockSpec((nb, tr, t), lambda i: (0, i, 0))], out_specs=pl.BlockSpec((tr, nb * t), lambda i: (i, 0)),
        out_shape=SDS((r, nb * t), wb.dtype), compiler_params=_arb(1), name=name)(wb)


def _block_cols(w, nb, name, carry=()):
    r, n = w.shape
    t = n // nb
    tr = _tile(r, 256, 16)

    def body(w_ref, o_ref):
        wv = w_ref[...]
        for j in range(nb):
            o_ref[j] = wv[:, j * t:(j + 1) * t]

    return _hosted(
        body, carry, grid=(r // tr,),
        in_specs=[pl.BlockSpec((tr, n), lambda i: (i, 0))], out_specs=pl.BlockSpec((nb, tr, t), lambda i: (0, i, 0)),
        out_shape=SDS((nb, r, t), w.dtype), compiler_params=_arb(1), name=name)(w)


def _place():
    x, y, c = lax.axis_index("x"), lax.axis_index("y"), lax.axis_index("c")
    return x, y, c, [(1 - x, y), (x, 1 - y), (1 - x, 1 - y)]


def _all_gather(shards):
    n = len(shards)
    slots = 9
    cut = [(s.shape[0] // 32) * 16 for s in shards]

    def build(ins, outs, sems):
        send_sems, recv_sems, local_sems = sems
        x, y, c, _ = _place()
        me, sib, xn, yn, dg = (x, y, c), (x, y, 1 - c), (1 - x, y, c), (x, 1 - y, c), (1 - x, 1 - y, c)
        other = lambda p: (p[0], p[1], 1 - p[2])

        def rows(a, p, part=None):
            ref = outs[a].at[4 * p[0] + 2 * p[1] + p[2]]
            if part is None or cut[a] == 0:
                return ref if part in (None, 0) else None
            return ref.at[pl.ds(0, cut[a])] if part == 0 else ref.at[pl.ds(cut[a], shards[a].shape[0] - cut[a])]

        def copy(a, k, ref, to, src=None):
            if ref is None:
                return None
            return pltpu.make_async_remote_copy(
                src_ref=ref if src is None else src, dst_ref=ref, send_sem=send_sems.at[slots * a + k],
                recv_sem=recv_sems.at[slots * a + k], device_id=to, device_id_type=MESH)

        def real(cps):
            return [cp for cp in cps if cp is not None]

        class Copies:
            own = lambda a: [copy(a, 1, rows(a, me), xn, ins[a]), copy(a, 2, rows(a, me), yn, ins[a]),
                             copy(a, 0, rows(a, me), sib, ins[a])]
            local = lambda a: pltpu.make_async_copy(ins[a], rows(a, me), local_sems.at[a])
            from_x = lambda a: copy(a, 1, rows(a, xn), me)
            from_y = lambda a: copy(a, 2, rows(a, yn), me)
            after_x = lambda a: real([copy(a, 4, rows(a, xn, 1), yn), copy(a, 5, rows(a, xn), sib)])
            after_y = lambda a: real([copy(a, 3, rows(a, yn, 0), xn), copy(a, 6, rows(a, yn), sib)])
            diag_in = lambda a: real([copy(a, 3, rows(a, dg, 0), me), copy(a, 4, rows(a, dg, 1), me)])
            diag_on = lambda a: real([copy(a, 7, rows(a, dg, 0), sib), copy(a, 8, rows(a, dg, 1), sib)])
            from_sib = lambda a: real([copy(a, 0, rows(a, sib), me), copy(a, 5, rows(a, other(xn)), me),
                                       copy(a, 6, rows(a, other(yn)), me), copy(a, 7, rows(a, other(dg), 0), me),
                                       copy(a, 8, rows(a, other(dg), 1), me)])

        return Copies

    def start(ins, outs, sems):
        cps = build(ins, outs, sems)
        for a in range(n):
            for cp in cps.own(a):
                cp.start()
        for a in range(n):
            cps.local(a).start()

    def relay(ins, outs, sems):
        cps = build(ins, outs, sems)
        for a in range(n):
            cps.from_x(a).wait_recv()
            for cp in cps.after_x(a):
                cp.start()
            cps.from_y(a).wait_recv()
            for cp in cps.after_y(a):
                cp.start()

    def finish(ins, outs, sems):
        cps = build(ins, outs, sems)
        for a in range(n):
            for arrived, onward in zip(cps.diag_in(a), cps.diag_on(a)):
                arrived.wait_recv()
                onward.start()
        for a in range(n):
            for cp in cps.from_sib(a):
                cp.wait_recv()
            for cp in cps.own(a) + cps.after_x(a) + cps.after_y(a) + cps.diag_on(a):
                cp.wait_send()
            cps.local(a).wait()

    return _Exchange(shards, [SDS((N_DEV,) + s.shape, s.dtype) for s in shards],
                     [pltpu.SemaphoreType.DMA((slots * n,)), pltpu.SemaphoreType.DMA((slots * n,)),
                      pltpu.SemaphoreType.DMA((n,))], start, finish, relay)


def _swap_exchange(ins, out_shape, per, copies):
    def start(i, o, sems):
        for cp in copies(i, o, sems):
            cp.start()

    def finish(i, o, sems):
        for cp in copies(i, o, sems):
            cp.wait()

    n = per * len(ins)
    return _Exchange(ins, out_shape, [pltpu.SemaphoreType.DMA((n,)), pltpu.SemaphoreType.DMA((n,))], start, finish)


def _exchange_c(gs):
    def copies(ins, outs, sems):
        x, y, c, _ = _place()
        return [pltpu.make_async_remote_copy(
                    src_ref=ins[a].at[2 * k + 1 - c], dst_ref=outs[a].at[k],
                    send_sem=sems[0].at[4 * a + k], recv_sem=sems[1].at[4 * a + k],
                    device_id=(x, y, 1 - c), device_id_type=MESH)
                for a in range(len(gs)) for k in range(4)]

    return _swap_exchange(gs, [SDS((4,) + g.shape[1:], g.dtype) for g in gs], 4, copies)


def _exchange_xy(sends):
    def copies(ins, outs, sems):
        x, y, c, chips = _place()
        return [pltpu.make_async_remote_copy(
                    src_ref=ins[a].at[t], dst_ref=outs[a].at[t],
                    send_sem=sems[0].at[3 * a + t], recv_sem=sems[1].at[3 * a + t],
                    device_id=(*chips[t], c), device_id_type=MESH)
                for a in range(len(sends)) for t in range(3)]

    return _swap_exchange(sends, [SDS(s.shape, s.dtype) for s in sends], 3, copies)


def _rs_combine(g, recv, pos, name, carry=()):
    _, r, cdim = g.shape
    tr = _tile(r, 256, 16)

    def body(pos_ref, g0, r0, g1, r1, g2, r2, g3, r3, keep_ref, send_ref):
        keep_ref[...] = g0[...] + r0[...]
        send_ref[0] = (g1[...] + r1[...]).astype(BF16)
        send_ref[1] = (g2[...] + r2[...]).astype(BF16)
        send_ref[2] = (g3[...] + r3[...]).astype(BF16)

    def k_of(p, t):
        px = p[0] if t in (0, 2) else 1 - p[0]
        py = p[1] if t in (0, 1) else 1 - p[1]
        return 2 * px + py

    blk = (None, tr, cdim)
    in_specs = []
    for t in range(4):
        in_specs.append(pl.BlockSpec(blk, functools.partial(lambda j, p, t: (2 * k_of(p, t) + p[2], j, 0), t=t)))
        in_specs.append(pl.BlockSpec(blk, functools.partial(lambda j, p, t: (k_of(p, t), j, 0), t=t)))
    return _hosted(
        body, carry, n_prefetch=1, out_shape=(SDS((r, cdim), F32), SDS((3, r, cdim), BF16)),
        grid=(r // tr,), in_specs=in_specs,
        out_specs=[pl.BlockSpec((tr, cdim), lambda j, p: (j, 0)), pl.BlockSpec((3, tr, cdim), lambda j, p: (0, j, 0))],
        compiler_params=_arb(1), name=name)(pos, g, recv, g, recv, g, recv, g, recv)


def _adamw_shard(keep, recv, w, m, v, name):
    r, cdim = w.shape
    tr = _tile(r, 256, 16)

    def body(k_ref, r_ref, w_ref, m_ref, v_ref, g_ref, d_ref, nm_ref, nv_ref):
        g = ((k_ref[...] + r_ref[0].astype(F32)) + r_ref[1].astype(F32)) + r_ref[2].astype(F32)
        g_ref[...] = g
        d_ref[...], nm_ref[...], nv_ref[...] = _adamw(w_ref[...], g, m_ref[...], v_ref[...])

    blk = pl.BlockSpec((tr, cdim), lambda j: (j, 0))
    out = SDS((r, cdim), F32)
    return _pcall(body, grid=(r // tr,), in_specs=[blk, pl.BlockSpec((3, tr, cdim), lambda j: (0, j, 0)), blk, blk, blk],
                  out_specs=[blk] * 4, out_shape=(out,) * 4, compiler_params=_arb(1), name=name)(keep, recv, w, m, v)


_HBM = pl.BlockSpec(memory_space=pltpu.HBM)
_SEM = pl.BlockSpec(memory_space=pltpu.SEMAPHORE)
_SPLIT = pltpu.CompilerParams(has_side_effects=pltpu.SideEffectType.DATAFLOW_SIDE_EFFECTING)


def _split_copies(kind, n, refs):
    srcs, lands, (send_sems, recv_sems) = refs[:n], refs[n:2 * n], refs[2 * n:2 * n + 2]
    x, y, c, chips = _place()
    per = _SPLIT_COPIES[kind]
    if kind == "xy":
        ends = lambda a, t: (srcs[a].at[t], lands[a].at[t], (*chips[t], c))
    else:
        ends = lambda a, k: (srcs[a].at[2 * k + 1 - c], lands[a].at[k], (x, y, 1 - c))
    cps = []
    for a in range(n):
        for t in range(per):
            src, dst, to = ends(a, t)
            cps.append(pltpu.make_async_remote_copy(src_ref=src, dst_ref=dst, send_sem=send_sems.at[per * a + t],
                                                    recv_sem=recv_sems.at[per * a + t], device_id=to, device_id_type=MESH))
    return cps


_SPLIT_COPIES = {"xy": 3, "c": 4}


def _exchange_start(kind, arrays, name, after=None):
    n = len(arrays)
    order = [] if after is None else [after]

    def body(*refs):
        refs = refs[:2 * n] + refs[2 * n + len(order):]
        for cp in _split_copies(kind, n, refs):
            cp.start()
        refs[-1][...] = jnp.zeros_like(refs[-1])

    hbm = lambda a: pltpu.with_memory_space_constraint(a, pltpu.HBM)
    land = [a.shape if kind == "xy" else (4,) + a.shape[1:] for a in arrays]
    bufs = [pltpu.HBM(a.shape, a.dtype) for a in arrays] + [pltpu.HBM(s, a.dtype) for s, a in zip(land, arrays)]
    sems = pltpu.SemaphoreType.DMA((_SPLIT_COPIES[kind] * n,))
    res = _pcall(
        body, name=name, out_shape=(sems, sems, *bufs, SDS((SUB, LANES), F32)),
        in_specs=[_HBM] * (2 * n) + _hbm_specs(len(order)),
        out_specs=[_SEM, _SEM] + [_HBM] * (2 * n) + [pl.BlockSpec(memory_space=pltpu.VMEM)],
        input_output_aliases={k: 2 + k for k in range(2 * n)}, compiler_params=_SPLIT)(
            *[hbm(a) for a in arrays], *[hbm(lax.empty(s, a.dtype)) for s, a in zip(land, arrays)], *order)
    return (kind, n, res[:-1]), res[-1]


def _exchange_wait(started, after, name, sources=False):
    kind, n, (send_sems, recv_sems, *bufs) = started

    def body(*refs):
        for cp in _split_copies(kind, n, refs):
            cp.wait_send()
            cp.wait_recv()

    shapes = [pltpu.HBM(b.shape, b.dtype) for b in bufs]
    res = _pcall(
        body, name=name, out_shape=tuple(shapes),
        in_specs=[_HBM] * (2 * n) + [_SEM, _SEM, pl.BlockSpec(memory_space=pl.ANY)], out_specs=[_HBM] * (2 * n),
        input_output_aliases={k: k for k in range(2 * n)}, compiler_params=_SPLIT)(*bufs, send_sems, recv_sems, after)
    return (list(res[:n]), list(res[n:])) if sources else list(res[n:])


def _follow(token):
    nothing = lambda ins, outs, sems: None
    return _Exchange([token], [], [], nothing, nothing)


def _adamw_small(gathered, seg, params, conv_rows):
    names = list(params)
    c0, cn = conv_rows

    def body(*refs):
        gat_ref = refs[0]
        ins = refs[1:1 + 3 * len(names)]
        outs = refs[1 + 3 * len(names):]

        def total(r0, rn):
            tot = gat_ref[0, r0:r0 + rn, :]
            for dev in range(1, N_DEV):
                tot = tot + gat_ref[dev, r0:r0 + rn, :]
            return tot

        for k, nm in enumerate(names):
            g = total(*seg[nm])
            w_ref, m_ref, v_ref = ins[3 * k:3 * k + 3]
            g_ref, d_ref, nm_ref, nv_ref = outs[4 * k:4 * k + 4]
            g_ref[...] = g
            d_ref[...], nm_ref[...], nv_ref[...] = _adamw(w_ref[...], g, m_ref[...], v_ref[...])
        outs[-2][...] = total(c0, cn)
        outs[-1][...] = total(*seg["loss"])

    flat_in = [a for nm in names for a in params[nm]]
    out_shape = []
    for nm in names:
        out_shape += [SDS(params[nm][0].shape, F32)] * 4
    out_shape += [SDS((cn, LANES), F32), SDS((seg["loss"][1], LANES), F32)]
    res = _pcall(body, out_shape=tuple(out_shape), name="adamw_small")(gathered, *flat_in)
    per = {nm: res[4 * k:4 * k + 4] for k, nm in enumerate(names)}
    return per, res[-2], res[-1]


def _adamw_one(w, g, m, v, name):
    def body(w_ref, g_ref, m_ref, v_ref, d_ref, nm_ref, nv_ref):
        d_ref[...], nm_ref[...], nv_ref[...] = _adamw(w_ref[...], g_ref[...], m_ref[...], v_ref[...])

    return _pcall(body, out_shape=(SDS(w.shape, F32),) * 3, name=name)(w, g, m, v)


def _rows128(a):
    return a.reshape(-1, LANES)


def _pack_small(gs, loss_tile):
    seg, pieces, row = {}, [], 0
    for nm in SMALL + ("conv_w", "loss"):
        piece = loss_tile if nm == "loss" else _rows128(gs[nm])
        rn = _round_up(piece.shape[0], SUB)
        pieces.append(jnp.pad(piece, ((0, rn - piece.shape[0]), (0, 0))))
        seg[nm] = (row, piece.shape[0])
        row += rn
    return jnp.concatenate(pieces, axis=0), seg


def _step(x, mem, target, wb, conv_w, sp, pos):
    s, d = x.shape
    tm = min(TOKEN_TILE, s)
    tm_wide = min(2 * TOKEN_TILE, s)
    rows = lambda w8: w8.reshape(-1, w8.shape[2])
    shards = lambda g: g.reshape((N_DEV, -1) + g.shape[1:])
    bt = sp["b_spatial"].T

    (w_in8, conv8), = _run_exchanges([_all_gather([wb["w_in"], conv_w])], "gather_w_in")
    conv_full = conv8.transpose(1, 0, 2).reshape(3, -1)
    w_in_t = rows(w_in8)
    (xn1, h), ((w_out8, w_kv8, w_q8),) = _in_forward(
        x, sp["ln_mix_g"], w_in_t, tm, carry=[_all_gather([wb["w_out"], wb["w_kv"], wb["w_q"]])])
    w_out = rows(w_out8)
    (ycat, x1), ((w_o8, w_down8),) = _mix_forward(
        h, x, sp["sgu_ln_g"], sp["sgu_ln_b"], sp["w_spatial"], bt, conv_full, sp["grp_norm_a"], sp["grp_norm_b"], w_out, tm,
        carry=[_all_gather([wb["w_o"], wb["w_down"]])])
    w_q, w_o, w_down = rows(w_q8), rows(w_o8), rows(w_down8)
    memn, kv = _kv_forward(mem, sp["ln_mem_g"], w_kv8)
    (xn2, q, o, x2), ((w_gu8,),) = _attn_forward(
        x1, sp["ln_attn_g"], w_q, kv, w_o, tm, carry=[_all_gather([wb["w_gate_up"]])])
    w_gu = w_gu8.reshape((2, N_DEV // 2) + w_gu8.shape[1:])
    xn3, gu, x3 = _ffn_forward(x2, sp["ln_ffn_g"], w_gu, w_down, tm_wide)

    loss, d_lnf, dx3, dx3b = _final_backward(x3, target, sp["ln_final_g"], tm_wide)
    act, dgu, dxn3 = _swiglu_backward(dx3b, gu, w_gu, w_down, tm_wide)
    g_gu, _ = _wgrad_blocked_lhs(dgu.reshape((N_DEV,) + dgu.shape[2:]), xn3, "wgrad_gate_up")
    g_gu = shards(g_gu)
    g_down, ((rc_gu,),) = _wgrad_blocked_lhs(act, dx3b, "wgrad_down", carry=[_exchange_c([g_gu])])
    g_down = shards(g_down)
    keep, pending = {}, []
    (keep["w_gate_up"], send_gu), _ = _rs_combine(g_gu, rc_gu, pos, "rs_combine_w_gate_up")
    started, token = _exchange_start("xy", [send_gu], "exchange_xy_1_start")
    pending.append((("w_gate_up",), started))
    c_down, token = _exchange_start("c", [g_down], "exchange_c_1_start", after=token)
    (dx2b, dq, dx1, dx1b, dkv, d_lnattn, d_lnffn), _ = _attn_backward(
        dx3, dxn3, x2, sp["ln_ffn_g"], x1, sp["ln_attn_g"], q, kv, w_q, w_o, tm, carry=[_follow(token)])
    (g_down,), (rc_down,) = _exchange_wait(c_down, dx1b, "exchange_c_1_wait", sources=True)
    (keep["w_down"], send_down), _ = _rs_combine(g_down, rc_down, pos, "rs_combine_w_down")
    g_o, _ = _wgrad(o, dx2b, "wgrad_o")
    g_q, _ = _wgrad(xn2, dq, "wgrad_q")
    g_o, g_q = shards(g_o), shards(g_q)
    c_oq, token = _exchange_start("c", [g_o, g_q], "exchange_c_2_start")
    g_out, _ = _wgrad(ycat, dx1b, "wgrad_out", carry=[_follow(token)])
    g_out = shards(g_out)
    g_kv, d_lnmem = _kv_backward(dkv, memn, mem, sp["ln_mem_g"], w_kv8)
    (g_o, g_q), (rc_o, rc_q) = _exchange_wait(c_oq, g_out, "exchange_c_2_wait", sources=True)
    c_outkv, token = _exchange_start("c", [g_out, g_kv], "exchange_c_3_start")
    (keep["w_o"], send_o), _ = _rs_combine(g_o, rc_o, pos, "rs_combine_w_o", carry=[_follow(token)])
    (keep["w_q"], send_q), _ = _rs_combine(g_q, rc_q, pos, "rs_combine_w_q")
    (g_out, g_kv), (rc_out, rc_kv) = _exchange_wait(c_outkv, send_q, "exchange_c_3_wait", sources=True)
    (keep["w_out"], send_out), _ = _rs_combine(g_out, rc_out, pos, "rs_combine_w_out")
    (keep["w_kv"], send_kv), _ = _rs_combine(g_kv, rc_kv, pos, "rs_combine_w_kv")
    started, token = _exchange_start("xy", [send_down, send_o, send_q, send_out, send_kv], "exchange_xy_2_start")
    pending.append((("w_down", "w_o", "w_q", "w_out", "w_kv"), started))
    (dh, dx, d_ga, d_gb, d_cw, d_lng, d_lnb, d_wsp, d_bs, d_lnmix), _ = _mix_backward(
        dx1, x, sp["ln_mix_g"], h, sp["sgu_ln_g"], sp["sgu_ln_b"], sp["w_spatial"], bt, conv_full,
        sp["grp_norm_a"], sp["grp_norm_b"], w_out, w_in_t, tm, carry=[_follow(token)])
    gs = {"ln_mix_g": d_lnmix, "sgu_ln_g": d_lng, "sgu_ln_b": d_lnb, "w_spatial": d_wsp, "b_spatial": _bias_grad(d_bs),
          "conv_w": d_cw[:3], "grp_norm_a": d_ga, "grp_norm_b": d_gb, "ln_attn_g": d_lnattn, "ln_mem_g": d_lnmem,
          "ln_ffn_g": d_lnffn, "ln_final_g": d_lnf}
    packed, seg = _pack_small(gs, loss)
    g_in, (_, (small_all,)) = _wgrad(dh, xn1, "wgrad_in", carry=[_follow(token), _all_gather([packed])])
    g_in = shards(g_in)
    c_in, token = _exchange_start("c", [g_in], "exchange_c_4_start")
    return dx, keep, pending, (g_in, c_in), token, small_all, seg


def kernel(x, mem, ln_mix_g, w_in, sgu_ln_g, sgu_ln_b, w_spatial, b_spatial, conv_w, grp_norm_a, grp_norm_b, w_out, ln_attn_g, ln_mem_g, w_q, w_kv, w_o, ln_ffn_g, w_gate_up, w_down, ln_final_g, loss_target, m_ln_mix_g, m_w_in, m_sgu_ln_g, m_sgu_ln_b, m_w_spatial, m_b_spatial, m_conv_w, m_grp_norm_a, m_grp_norm_b, m_w_out, m_ln_attn_g, m_ln_mem_g, m_w_q, m_w_kv, m_w_o, m_ln_ffn_g, m_w_gate_up, m_w_down, m_ln_final_g, v_ln_mix_g, v_w_in, v_sgu_ln_g, v_sgu_ln_b, v_w_spatial, v_b_spatial, v_conv_w, v_grp_norm_a, v_grp_norm_b, v_w_out, v_ln_attn_g, v_ln_mem_g, v_w_q, v_w_kv, v_w_o, v_ln_ffn_g, v_w_gate_up, v_w_down, v_ln_final_g):
    order = ["ln_mix_g", "w_in", "sgu_ln_g", "sgu_ln_b", "w_spatial", "b_spatial", "conv_w", "grp_norm_a", "grp_norm_b",
             "w_out", "ln_attn_g", "ln_mem_g", "w_q", "w_kv", "w_o", "ln_ffn_g", "w_gate_up", "w_down", "ln_final_g"]
    W = dict(ln_mix_g=ln_mix_g, w_in=w_in, sgu_ln_g=sgu_ln_g, sgu_ln_b=sgu_ln_b, w_spatial=w_spatial, b_spatial=b_spatial,
             conv_w=conv_w, grp_norm_a=grp_norm_a, grp_norm_b=grp_norm_b, w_out=w_out, ln_attn_g=ln_attn_g,
             ln_mem_g=ln_mem_g, w_q=w_q, w_kv=w_kv, w_o=w_o, ln_ffn_g=ln_ffn_g, w_gate_up=w_gate_up, w_down=w_down,
             ln_final_g=ln_final_g)
    M = dict(ln_mix_g=m_ln_mix_g, w_in=m_w_in, sgu_ln_g=m_sgu_ln_g, sgu_ln_b=m_sgu_ln_b, w_spatial=m_w_spatial,
             b_spatial=m_b_spatial, conv_w=m_conv_w, grp_norm_a=m_grp_norm_a, grp_norm_b=m_grp_norm_b, w_out=m_w_out,
             ln_attn_g=m_ln_attn_g, ln_mem_g=m_ln_mem_g, w_q=m_w_q, w_kv=m_w_kv, w_o=m_w_o, ln_ffn_g=m_ln_ffn_g,
             w_gate_up=m_w_gate_up, w_down=m_w_down, ln_final_g=m_ln_final_g)
    V = dict(ln_mix_g=v_ln_mix_g, w_in=v_w_in, sgu_ln_g=v_sgu_ln_g, sgu_ln_b=v_sgu_ln_b, w_spatial=v_w_spatial,
             b_spatial=v_b_spatial, conv_w=v_conv_w, grp_norm_a=v_grp_norm_a, grp_norm_b=v_grp_norm_b, w_out=v_w_out,
             ln_attn_g=v_ln_attn_g, ln_mem_g=v_ln_mem_g, w_q=v_w_q, w_kv=v_w_kv, w_o=v_w_o, ln_ffn_g=v_ln_ffn_g,
             w_gate_up=v_w_gate_up, w_down=v_w_down, ln_final_g=v_ln_final_g)

    bw = conv_w.shape[1] * N_DEV
    pos = jnp.stack([lax.axis_index("x"), lax.axis_index("y"), lax.axis_index("c")]).astype(jnp.int32)
    me = 4 * pos[0] + 2 * pos[1] + pos[2]

    sp = {nm: (W[nm].reshape(1, -1) if W[nm].ndim == 1 else W[nm]) for nm in SMALL}
    view = lambda a, nm: a.T if nm in TRANSPOSED else a
    wb = {nm: view(W[nm], nm).astype(BF16) for nm in BIG}
    grad_x, keep, pending, (g_in, c_in), token, small_all, seg = _step(
        x[0], mem[0], loss_target[0], wb, conv_w, sp, pos)

    out = {}

    def update(k, names, started, token):
        landed = _exchange_wait(started, token, "exchange_xy_%d_wait" % k)
        for nm, rxy in zip(names, landed):
            res = _adamw_shard(keep[nm], rxy, view(W[nm], nm), view(M[nm], nm), view(V[nm], nm), "adamw_" + nm)
            out[nm] = tuple(view(a, nm) for a in res)
            token = res[0]
        return token

    token = update(1, *pending[0], token)
    (g_in,), (rc_in,) = _exchange_wait(c_in, token, "exchange_c_4_wait", sources=True)
    (keep["w_in"], send_in), _ = _rs_combine(g_in, rc_in, pos, "rs_combine_w_in")
    xy_in, token = _exchange_start("xy", [send_in], "exchange_xy_3_start")
    token = update(2, *pending[1], token)
    update(3, ("w_in",), xy_in, token)

    params = {nm: (_rows128(W[nm]), _rows128(M[nm]), _rows128(V[nm])) for nm in SMALL}
    per, conv_g_rows, loss_sum = _adamw_small(small_all, seg, params, seg["conv_w"])
    for nm in SMALL:
        out[nm] = tuple(a.reshape(W[nm].shape) for a in per[nm])
    conv_g = lax.dynamic_slice_in_dim(conv_g_rows.reshape(3, bw), me * conv_w.shape[1], conv_w.shape[1], axis=1)
    out["conv_w"] = (conv_g,) + tuple(_adamw_one(conv_w, conv_g, m_conv_w, v_conv_w, "adamw_conv"))

    loss = loss_sum[0, 0]
    res = [loss, grad_x[None]]
    for k in range(4):
        res += [out[nm][k] for nm in order]
    return tuple(res)
```

```python
import functools

import jax
import jax.numpy as jnp
from jax import lax
from jax.experimental import pallas as pl
from jax.experimental.pallas import tpu as pltpu

F32 = jnp.float32
BF16 = jnp.bfloat16
SDS = jax.ShapeDtypeStruct
MESH = pl.DeviceIdType.MESH

EPS = 1e-6
N_DEV = 8
HEADS = 4
CHUNK = 128
HALO = 16
SUB = 8
LANES = 128
TOKEN_TILE = 512
ROW_CHUNK = 256
RELAY_AT = 0.7

ADAM_LR = 0.001
ADAM_B1 = 0.9
ADAM_B2 = 0.999
ADAM_EPS = 1e-08
ADAM_WD = 0.01
ADAM_STEP = 10

BIG = ("w_in", "w_out", "w_q", "w_kv", "w_o", "w_gate_up", "w_down")
TRANSPOSED = ("w_in", "w_gate_up")
SMALL = ("ln_mix_g", "sgu_ln_g", "sgu_ln_b", "w_spatial", "b_spatial", "grp_norm_a", "grp_norm_b",
         "ln_attn_g", "ln_mem_g", "ln_ffn_g", "ln_final_g")


class _Exchange:
    def __init__(self, ins, out_shape, sems, start, finish, relay=None):
        self.ins, self.out_shape, self.sems = list(ins), list(out_shape), list(sems)
        self.start, self.finish, self.relay = start, finish, relay


def _pcall(body, carry=(), n_prefetch=0, **kw):
    if carry:
        return functools.partial(_carrying_call, body, tuple(carry), n_prefetch, kw)
    if n_prefetch:
        kw["grid_spec"] = pltpu.PrefetchScalarGridSpec(
            num_scalar_prefetch=n_prefetch, grid=kw.pop("grid"), in_specs=kw.pop("in_specs"),
            out_specs=kw.pop("out_specs"), scratch_shapes=kw.pop("scratch_shapes", ()))
    return pl.pallas_call(body, **kw)


def _carrying_call(body, carry, n_prefetch, kw, *args):
    kw = dict(kw)
    out_shape = kw.pop("out_shape")
    single = not isinstance(out_shape, (tuple, list))
    outs_shape = (out_shape,) if single else tuple(out_shape)
    out_specs = kw.pop("out_specs")
    out_specs = [out_specs] if single else list(out_specs)
    in_specs = list(kw.pop("in_specs"))
    scratch = list(kw.pop("scratch_shapes", ()))
    grid = tuple(kw.get("grid", ()))
    n_in, n_out, n_scr = len(args), len(outs_shape), len(scratch)

    def split(refs, k, counts):
        parts = []
        for cnt in counts:
            parts.append(refs[k:k + cnt])
            k += cnt
        return parts, k

    def wrapped(*refs):
        cins, k = split(refs, n_in, [len(p.ins) for p in carry])
        outs = refs[k:k + n_out]
        couts, k = split(refs, k + n_out, [len(p.out_shape) for p in carry])
        scr = refs[k:k + n_scr]
        csems, _ = split(refs, k + n_scr, [len(p.sems) for p in carry])
        first, last = True, True
        for a, g in enumerate(grid):
            first = (pl.program_id(a) == 0) & first
            last = (pl.program_id(a) == g - 1) & last

        def start_all():
            for p, ci, co, cs in zip(carry, cins, couts, csems):
                p.start(ci, co, cs)

        def relay_all():
            for p, ci, co, cs in zip(carry, cins, couts, csems):
                if p.relay is not None:
                    p.relay(ci, co, cs)

        def finish_all():
            for p, ci, co, cs in zip(carry, cins, couts, csems):
                p.finish(ci, co, cs)

        if len(grid) == 1:
            relay_now = pl.program_id(0) == min(int(RELAY_AT * grid[0]), grid[0] - 1)
        else:
            relay_now = last
        start_all() if not grid else pl.when(first)(start_all)
        relay_all() if not grid else pl.when(relay_now)(relay_all)
        body(*refs[:n_in], *outs, *scr)
        finish_all() if not grid else pl.when(last)(finish_all)

    c_in = [a for p in carry for a in p.ins]
    c_out = [s for p in carry for s in p.out_shape]
    c_sems = [s for p in carry for s in p.sems]
    res = _pcall(wrapped, n_prefetch=n_prefetch, out_shape=outs_shape + tuple(c_out),
                 in_specs=in_specs + _hbm_specs(len(c_in)), out_specs=out_specs + _hbm_specs(len(c_out)),
                 scratch_shapes=scratch + c_sems, **kw)(*args, *c_in)
    own = res[0] if single else tuple(res[:n_out])
    landed, k = [], n_out
    for p in carry:
        landed.append(list(res[k:k + len(p.out_shape)]))
        k += len(p.out_shape)
    return own, landed


def _hbm_specs(n):
    return [pl.BlockSpec(memory_space=pl.ANY)] * n


def _hosted(body, carry, **kw):
    if carry:
        return _pcall(body, carry=carry, **kw)
    call = _pcall(body, **kw)
    return lambda *args: (call(*args), [])


def _run_exchanges(parts, name):
    def body(*refs):
        pass

    _, landed = _pcall(body, carry=parts, out_shape=(), in_specs=[], out_specs=[], name=name)()
    return landed


def _arb(n):
    return pltpu.CompilerParams(dimension_semantics=("arbitrary",) * n)


def _tile(n, target, mult):
    best = None
    for t in range(mult, min(n, target) + 1, mult):
        if n % t == 0:
            best = t
    return n if best is None else best


def _round_up(n, m):
    return (n + m - 1) // m * m


def _dot(a, b):
    return jnp.dot(a, b, preferred_element_type=F32)


def _dot_nt(a, b):
    return lax.dot_general(a, b, (((1,), (1,)), ((), ())), preferred_element_type=F32)


def _dot_tn(a, b):
    return lax.dot_general(a, b, (((0,), (0,)), ((), ())), preferred_element_type=F32)


def _rstd(x):
    return lax.rsqrt(jnp.mean(x * x, axis=-1, keepdims=True) + EPS)


def _rms_bwd(dy, x, r, g):
    gdy = dy * g
    proj = jnp.sum(gdy * x, axis=-1, keepdims=True) * (1.0 / x.shape[-1])
    dx = r * gdy - x * (r * r * r) * proj
    dg = jnp.sum(dy * (x * r), axis=0, keepdims=True)
    return dx, dg


_GELU_C = 0.7978845608028654
_GELU_A = 0.044715


def _gelu(x):
    t = jnp.tanh(_GELU_C * (x + _GELU_A * x * x * x))
    return 0.5 * x * (1.0 + t), t


def _gelu_grad(x, t):
    return 0.5 * (1.0 + t) + 0.5 * x * (1.0 - t * t) * (_GELU_C * (1.0 + 3.0 * _GELU_A * x * x))


def _sigmoid(x):
    return 1.0 / (1.0 + jnp.exp(-x))


def _softmax(s):
    m = jnp.max(s, axis=-1, keepdims=True)
    e = jnp.exp(s - m)
    return e / jnp.sum(e, axis=-1, keepdims=True)


def _adamw(w, g, m, v):
    m = ADAM_B1 * m + (1.0 - ADAM_B1) * g
    v = ADAM_B2 * v + (1.0 - ADAM_B2) * (g * g)
    m_hat = m / (1.0 - ADAM_B1 ** ADAM_STEP)
    v_hat = v / (1.0 - ADAM_B2 ** ADAM_STEP)
    delta = -ADAM_LR * (m_hat / (jnp.sqrt(v_hat) + ADAM_EPS) + ADAM_WD * w)
    return delta, m, v


def _tril_mask():
    t = lax.broadcasted_iota(jnp.int32, (CHUNK, CHUNK), 0)
    s = lax.broadcasted_iota(jnp.int32, (CHUNK, CHUNK), 1)
    return (s <= t).astype(F32)


def _sgu_forward(ha, lng, lnb, wm, bt, mixed_s):
    aw = ha.shape[1] // 2
    hd = aw // HEADS
    a, th = _gelu(ha)
    u = a[:, :aw]
    v = a[:, aw:]
    mu = jnp.mean(v, axis=-1, keepdims=True)
    vc = v - mu
    rl = lax.rsqrt(jnp.mean(vc * vc, axis=-1, keepdims=True) + EPS)
    xhat = vc * rl
    vln = (xhat * lng + lnb).astype(BF16)
    for n in range(ha.shape[0] // CHUNK):
        rows = slice(n * CHUNK, (n + 1) * CHUNK)
        for h in range(HEADS):
            cols = slice(h * hd, (h + 1) * hd)
            mixed_s[rows, cols] = _dot(wm[h], vln[rows, cols]) + bt[:, h:h + 1]
    return th, u, xhat, rl, vln


def _conv_taps(zext):
    return pltpu.roll(zext, 2, 0), pltpu.roll(zext, 1, 0)


def _kv_forward(mem, g_mem, w_kv):
    ml, d = mem.shape
    xd = w_kv.shape[2]

    def body(mem_ref, g_ref, w_ref, memn_ref, kv_ref):
        x = mem_ref[...]
        memn = (x * _rstd(x) * g_ref[...]).astype(BF16)
        memn_ref[...] = memn
        for j in range(2 * HEADS):
            kv_ref[j] = _dot(memn, w_ref[j]).astype(BF16)

    return _pcall(body, out_shape=(SDS((ml, d), BF16), SDS((2 * HEADS, ml, xd), BF16)), name="kv_forward")(mem, g_mem, w_kv)


def _in_forward(x, g, w_in_t, tm, carry=()):
    s, d = x.shape
    n_in = w_in_t.shape[0]

    def body(x_ref, g_ref, w_ref, xn_ref, h_ref):
        xv = x_ref[...]
        xn = (xv * _rstd(xv) * g_ref[...]).astype(BF16)
        xn_ref[...] = xn
        h_ref[...] = _dot_nt(xn, w_ref[...])

    return _hosted(
        body, carry, grid=(s // tm,),
        in_specs=[pl.BlockSpec((tm, d), lambda i: (i, 0)), pl.BlockSpec((1, d), lambda i: (0, 0)),
                  pl.BlockSpec((n_in, d), lambda i: (0, 0))],
        out_specs=[pl.BlockSpec((tm, d), lambda i: (i, 0)), pl.BlockSpec((tm, n_in), lambda i: (i, 0))],
        out_shape=(SDS((s, d), BF16), SDS((s, n_in), F32)),
        compiler_params=_arb(1), name="in_forward")(x, g, w_in_t)


def _mix_forward(h, x, lng, lnb, w_sp, bt, conv_w, ga, gb, w_out, tm, carry=()):
    s, d = x.shape
    n_in = h.shape[1]
    aw = lng.shape[1]
    bw = d - aw
    in_a = 2 * aw
    hb_blocks = tm // HALO

    def body(h_ref, hprev_ref, x_ref, lng_ref, lnb_ref, wsp_ref, bt_ref, cw_ref, ga_ref, gb_ref, wout_ref,
             ycat_ref, x1_ref, mixed_s):
        i = pl.program_id(0)
        mask = _tril_mask()
        wm = [(wsp_ref[hh] * mask).astype(BF16) for hh in range(HEADS)]
        hv = h_ref[...]
        _, u, _, _, _ = _sgu_forward(hv[:, :in_a], lng_ref[...], lnb_ref[...], wm, bt_ref[...], mixed_s)
        sg = u * mixed_s[...]
        ycat_ref[:, :aw] = (sg * _rstd(sg) * ga_ref[...]).astype(BF16)

        gate_b = hv[:, in_a:in_a + bw]
        z = hv[:, in_a + bw:in_a + 2 * bw] * hv[:, in_a + 2 * bw:]
        hp = hprev_ref[...]
        zp = hp[:, in_a + bw:in_a + 2 * bw] * hp[:, in_a + 2 * bw:]
        zp = jnp.where(i == 0, 0.0, zp)
        zext = jnp.concatenate([zp, z], axis=0)
        z2, z1 = _conv_taps(zext)
        cw = cw_ref[...]
        conv = cw[0:1] * z2[HALO:] + cw[1:2] * z1[HALO:] + cw[2:3] * z
        sc = gate_b * conv
        ycat_ref[:, aw:] = (sc * _rstd(sc) * gb_ref[...]).astype(BF16)
        x1_ref[...] = x_ref[...] + _dot(ycat_ref[...], wout_ref[...])

    full = lambda shape: pl.BlockSpec(shape, lambda i: (0,) * len(shape))
    return _hosted(
        body, carry, grid=(s // tm,),
        in_specs=[pl.BlockSpec((tm, n_in), lambda i: (i, 0)),
                  pl.BlockSpec((HALO, n_in), lambda i: (jnp.maximum(i * hb_blocks - 1, 0), 0)),
                  pl.BlockSpec((tm, d), lambda i: (i, 0)),
                  full((1, aw)), full((1, aw)), full((HEADS, CHUNK, CHUNK)), full((CHUNK, HEADS)),
                  full((3, bw)), full((1, aw)), full((1, bw)), full((d, d))],
        out_specs=[pl.BlockSpec((tm, d), lambda i: (i, 0)), pl.BlockSpec((tm, d), lambda i: (i, 0))],
        out_shape=(SDS((s, d), BF16), SDS((s, d), F32)),
        scratch_shapes=[pltpu.VMEM((tm, aw), F32)],
        compiler_params=_arb(1), name="mix_forward")(h, h, x, lng, lnb, w_sp, bt, conv_w, ga, gb, w_out)


def _attn_forward(x1, g, w_q, kv, w_o, tm, carry=()):
    s, d = x1.shape
    _, ml, xd = kv.shape
    scale = xd ** -0.5

    def body(x1_ref, g_ref, wq_ref, kv_ref, wo_ref, xn_ref, q_ref, o_ref, x2_ref):
        xv = x1_ref[...]
        xn = (xv * _rstd(xv) * g_ref[...]).astype(BF16)
        xn_ref[...] = xn
        q_ref[...] = _dot(xn, wq_ref[...]).astype(BF16)
        for hh in range(HEADS):
            cols = slice(hh * xd, (hh + 1) * xd)
            p = _softmax(_dot_nt(q_ref[:, cols], kv_ref[hh]) * scale)
            o_ref[:, cols] = _dot(p.astype(BF16), kv_ref[HEADS + hh]).astype(BF16)
        x2_ref[...] = xv + _dot(o_ref[...], wo_ref[...])

    tok = pl.BlockSpec((tm, d), lambda i: (i, 0))
    return _hosted(
        body, carry, grid=(s // tm,),
        in_specs=[tok, pl.BlockSpec((1, d), lambda i: (0, 0)), pl.BlockSpec((d, d), lambda i: (0, 0)),
                  pl.BlockSpec((2 * HEADS, ml, xd), lambda i: (0, 0, 0)), pl.BlockSpec((d, d), lambda i: (0, 0))],
        out_specs=[tok, tok, tok, tok],
        out_shape=(SDS((s, d), BF16), SDS((s, d), BF16), SDS((s, d), BF16), SDS((s, d), F32)),
        compiler_params=_arb(1), name="attn_forward")(x1, g, w_q, kv, w_o)


def _ffn_forward(x2, g, w_gu, w_down, tm):
    s, d = x2.shape
    _, nf, tf, _ = w_gu.shape

    def body(x2_ref, g_ref, wgu_ref, wd_ref, xn_ref, gu_ref, x3_ref):
        f = pl.program_id(1)

        @pl.when(f == 0)
        def _():
            xv = x2_ref[...]
            xn_ref[...] = (xv * _rstd(xv) * g_ref[...]).astype(BF16)
            x3_ref[...] = xv

        xn = xn_ref[...]
        gate = _dot_nt(xn, wgu_ref[0])
        up = _dot_nt(xn, wgu_ref[1])
        gu_ref[0] = gate.astype(BF16)
        gu_ref[1] = up.astype(BF16)
        act = (gate * _sigmoid(gate) * up).astype(BF16)
        x3_ref[...] += _dot(act, wd_ref[...])

    tok = pl.BlockSpec((tm, d), lambda i, f: (i, 0))
    return _pcall(
        body, grid=(s // tm, nf),
        in_specs=[tok, pl.BlockSpec((1, d), lambda i, f: (0, 0)),
                  pl.BlockSpec((2, None, tf, d), lambda i, f: (0, f, 0, 0)),
                  pl.BlockSpec((tf, d), lambda i, f: (f, 0))],
        out_specs=[tok, pl.BlockSpec((2, None, tm, tf), lambda i, f: (0, f, i, 0)), tok],
        out_shape=(SDS((s, d), BF16), SDS((2, nf, s, tf), BF16), SDS((s, d), F32)),
        compiler_params=_arb(2), name="ffn_forward")(x2, g, w_gu, w_down)


def _final_backward(x3, target, g_final, tm):
    s, d = x3.shape

    def body(x3_ref, tgt_ref, gf_ref, loss_ref, dgf_ref, dx3_ref, dx3b_ref):
        @pl.when(pl.program_id(0) == 0)
        def _():
            loss_ref[...] = jnp.zeros_like(loss_ref)
            dgf_ref[...] = jnp.zeros_like(dgf_ref)

        xv = x3_ref[...]
        r = _rstd(xv)
        diff = xv * r * gf_ref[...] - tgt_ref[...]
        loss_ref[...] += 0.5 * jnp.sum(jnp.sum(diff * diff, axis=-1, keepdims=True), axis=0, keepdims=True) * (1.0 / d)
        dx3, dgf = _rms_bwd(diff * (1.0 / d), xv, r, gf_ref[...])
        dgf_ref[...] += dgf
        dx3_ref[...] = dx3
        dx3b_ref[...] = dx3.astype(BF16)

    tok = pl.BlockSpec((tm, d), lambda i: (i, 0))
    vec = pl.BlockSpec((1, d), lambda i: (0, 0))
    return _pcall(
        body, grid=(s // tm,), in_specs=[tok, tok, vec],
        out_specs=[pl.BlockSpec((SUB, LANES), lambda i: (0, 0)), vec, tok, tok],
        out_shape=(SDS((SUB, LANES), F32), SDS((1, d), F32), SDS((s, d), F32), SDS((s, d), BF16)),
        compiler_params=_arb(1), name="final_backward")(x3, target, g_final)


def _swiglu_backward(dx3b, gu, w_gu, w_down, tm):
    s, d = dx3b.shape
    _, nf, tf, _ = w_gu.shape

    def body(dx3b_ref, gu_ref, wgu_ref, wd_ref, act_ref, dgu_ref, dxn_ref):
        @pl.when(pl.program_id(1) == 0)
        def _():
            dxn_ref[...] = jnp.zeros_like(dxn_ref)

        for r0 in range(0, tm, ROW_CHUNK):
            rows = slice(r0, r0 + ROW_CHUNK)
            dact = _dot_nt(dx3b_ref[rows, :], wd_ref[...])
            gv = gu_ref[0, rows, :].astype(F32)
            uv = gu_ref[1, rows, :].astype(F32)
            sg = _sigmoid(gv)
            silu = gv * sg
            act_ref[rows, :] = (silu * uv).astype(BF16)
            dgate = (dact * uv * (sg * (1.0 + gv * (1.0 - sg)))).astype(BF16)
            dup = (dact * silu).astype(BF16)
            dgu_ref[0, rows, :] = dgate
            dgu_ref[1, rows, :] = dup
            part = _dot(dgate, wgu_ref[0]) + _dot(dup, wgu_ref[1])
            dxn_ref[rows, :] += part

    tok = pl.BlockSpec((tm, d), lambda i, f: (i, 0))
    pair = pl.BlockSpec((2, None, tm, tf), lambda i, f: (0, f, i, 0))
    return _pcall(
        body, grid=(s // tm, nf),
        in_specs=[tok, pair, pl.BlockSpec((2, None, tf, d), lambda i, f: (0, f, 0, 0)),
                  pl.BlockSpec((tf, d), lambda i, f: (f, 0))],
        out_specs=[pl.BlockSpec((None, tm, tf), lambda i, f: (f, i, 0)), pair, tok],
        out_shape=(SDS((nf, s, tf), BF16), SDS((2, nf, s, tf), BF16), SDS((s, d), F32)),
        compiler_params=_arb(2), name="swiglu_backward")(dx3b, gu, w_gu, w_down)


def _attn_backward(dx3, dxn3, x2, g_ffn, x1, g, q, kv, w_q, w_o, tm, carry=()):
    s, d = x1.shape
    _, ml, xd = kv.shape
    scale = xd ** -0.5

    def body(dx3_ref, dxn3_ref, x2_ref, g2_ref, x1_ref, g_ref, q_ref, kv_ref, wq_ref, wo_ref,
             dx2b_ref, dq_ref, dx1_ref, dx1b_ref, dkv_ref, dg_ref, dg2_ref, do_s):
        i = pl.program_id(0)

        @pl.when(i == 0)
        def _():
            dkv_ref[...] = jnp.zeros_like(dkv_ref)
            dg_ref[...] = jnp.zeros_like(dg_ref)
            dg2_ref[...] = jnp.zeros_like(dg2_ref)

        x2v = x2_ref[...]
        dx2n, dg2 = _rms_bwd(dxn3_ref[...], x2v, _rstd(x2v), g2_ref[...])
        dg2_ref[...] += dg2
        dx2 = dx3_ref[...] + dx2n
        dx2b_ref[...] = dx2.astype(BF16)
        do_s[...] = _dot_nt(dx2b_ref[...], wo_ref[...]).astype(BF16)
        for hh in range(HEADS):
            kc = slice(hh * xd, (hh + 1) * xd)
            qh = q_ref[:, kc]
            kh = kv_ref[hh]
            doh = do_s[:, kc]
            p = _softmax(_dot_nt(qh, kh) * scale)
            dp = _dot_nt(doh, kv_ref[HEADS + hh])
            dkv_ref[HEADS + hh] += _dot_tn(p.astype(BF16), doh)
            ds = (p * (dp - jnp.sum(dp * p, axis=-1, keepdims=True)) * scale).astype(BF16)
            dq_ref[:, kc] = _dot(ds, kh).astype(BF16)
            dkv_ref[hh] += _dot_tn(ds, qh)
        dxn = _dot_nt(dq_ref[...], wq_ref[...])
        xv = x1_ref[...]
        dx, dg = _rms_bwd(dxn, xv, _rstd(xv), g_ref[...])
        dg_ref[...] += dg
        dx1 = dx2 + dx
        dx1_ref[...] = dx1
        dx1b_ref[...] = dx1.astype(BF16)

    tok = pl.BlockSpec((tm, d), lambda i: (i, 0))
    vec = pl.BlockSpec((1, d), lambda i: (0, 0))
    sq = pl.BlockSpec((d, d), lambda i: (0, 0))
    kvs = pl.BlockSpec((2 * HEADS, ml, xd), lambda i: (0, 0, 0))
    return _hosted(
        body, carry, grid=(s // tm,),
        in_specs=[tok, tok, tok, vec, tok, vec, tok, kvs, sq, sq],
        out_specs=[tok, tok, tok, tok, kvs, vec, vec],
        out_shape=(SDS((s, d), BF16), SDS((s, d), BF16), SDS((s, d), F32), SDS((s, d), BF16),
                   SDS((2 * HEADS, ml, xd), F32), SDS((1, d), F32), SDS((1, d), F32)),
        scratch_shapes=[pltpu.VMEM((tm, d), BF16)],
        compiler_params=_arb(1), name="attn_backward")(dx3, dxn3, x2, g_ffn, x1, g, q, kv, w_q, w_o)


def _kv_backward(dkv, memn, mem, g_mem, w_kv):
    ml, d = mem.shape
    xd = w_kv.shape[2]

    def body(dkv_ref, memn_ref, mem_ref, g_ref, w_ref, dw_ref, dg_ref):
        dmemn = jnp.zeros((ml, d), F32)
        for j in range(2 * HEADS):
            dkvb = dkv_ref[j].astype(BF16)
            dw_ref[j] = _dot_tn(memn_ref[...], dkvb)
            dmemn = dmemn + _dot_nt(dkvb, w_ref[j])
        x = mem_ref[...]
        dg_ref[...] = jnp.sum(dmemn * (x * _rstd(x)), axis=0, keepdims=True)

    return _pcall(body, out_shape=(SDS((2 * HEADS, d, xd), F32), SDS((1, d), F32)), name="kv_backward")(dkv, memn, mem, g_mem, w_kv)


def _mix_backward(dx1, x, g_mix, h, lng, lnb, w_sp, bt, conv_w, ga, gb, w_out, w_in, tm, carry=()):
    s, d = x.shape
    n_in = h.shape[1]
    aw = lng.shape[1]
    bw = d - aw
    hd = aw // HEADS
    in_a = 2 * aw
    hb_blocks = tm // HALO
    last_blk = s // HALO - 1
    nt = s // tm
    tc = tm
    te = tc + HALO
    tee = tc + 2 * HALO

    def body(dx1_ref, dx1n_ref, x_ref, gm_ref, h_ref, hp_ref, hn_ref, lng_ref, lnb_ref, wsp_ref, bt_ref, cw_ref,
             ga_ref, gb_ref, wout_ref, win_ref,
             dh_ref, dx_ref, dga_ref, dgb_ref, dcw_ref, dlng_ref, dlnb_ref, dwsp_ref, dbs_ref, dgm_ref,
             mixed_s, dvln_s):
        i = pl.program_id(0)

        @pl.when(i == 0)
        def _():
            for ref in (dga_ref, dgb_ref, dcw_ref, dlng_ref, dlnb_ref, dwsp_ref, dbs_ref, dgm_ref):
                ref[...] = jnp.zeros_like(ref)

        mask = _tril_mask()
        wm = [(wsp_ref[hh] * mask).astype(BF16) for hh in range(HEADS)]
        cw = cw_ref[...]

        def chain(r0):
            rows = slice(r0, r0 + tc)
            first, last = r0 == 0, r0 + tc == tm
            hv = h_ref[rows, :]
            dx1 = dx1_ref[rows, :]
            dx1n = dx1n_ref[...] if last else dx1_ref[r0 + tc:r0 + tc + HALO, :]
            hp = hp_ref[:, in_a:] if first else h_ref[r0 - HALO:r0, in_a:]
            hn = hn_ref[:, in_a:] if last else h_ref[r0 + tc:r0 + tc + HALO, in_a:]
            dx1e = jnp.concatenate([dx1, dx1n], axis=0).astype(BF16)
            dycat = _dot_nt(dx1e, wout_ref[...])

            hbe = jnp.concatenate([hp, hv[:, in_a:], hn], axis=0)
            row = lax.broadcasted_iota(jnp.int32, (tee, 1), 0)
            zext = hbe[:, bw:2 * bw] * hbe[:, 2 * bw:]
            if first:
                zext = jnp.where((i == 0) & (row < HALO), 0.0, zext)
            z2e, z1e = _conv_taps(zext)
            conv_e = (cw[0:1] * z2e + cw[1:2] * z1e + cw[2:3] * zext)[HALO:]
            gate_b_e = hbe[HALO:, :bw]
            sc_e = gate_b_e * conv_e
            rb = _rstd(sc_e)
            dyb = dycat[:, aw:]
            gdy = dyb * gb_ref[...]
            dsc_e = rb * gdy - sc_e * (rb * rb * rb) * (jnp.sum(gdy * sc_e, axis=-1, keepdims=True) * (1.0 / bw))
            dgb_ref[...] += jnp.sum((dyb * (sc_e * rb))[:tc], axis=0, keepdims=True)
            dconv_e = dsc_e * gate_b_e
            if last:
                dconv_e = jnp.where((i == nt - 1) & (row[:te] >= tc), 0.0, dconv_e)
            dconv = dconv_e[:tc]
            dc1 = pltpu.roll(dconv_e, te - 1, 0)[:tc]
            dc2 = pltpu.roll(dconv_e, te - 2, 0)[:tc]
            dz = cw[2:3] * dconv + cw[1:2] * dc1 + cw[0:1] * dc2
            z = zext[HALO:HALO + tc]
            z1 = z1e[HALO:HALO + tc]
            z2 = z2e[HALO:HALO + tc]
            dcw_ref[0:1, :] += jnp.sum(dconv * z2, axis=0, keepdims=True)
            dcw_ref[1:2, :] += jnp.sum(dconv * z1, axis=0, keepdims=True)
            dcw_ref[2:3, :] += jnp.sum(dconv * z, axis=0, keepdims=True)
            dh_ref[rows, in_a:in_a + bw] = (dsc_e[:tc] * conv_e[:tc]).astype(BF16)
            dh_ref[rows, in_a + bw:in_a + 2 * bw] = (dz * hv[:, in_a + 2 * bw:]).astype(BF16)
            dh_ref[rows, in_a + 2 * bw:] = (dz * hv[:, in_a + bw:in_a + 2 * bw]).astype(BF16)

            ha = hv[:, :in_a]
            mixed_c, dvln_c = mixed_s.at[rows, :], dvln_s.at[rows, :]
            th, u, xhat, rl, vln = _sgu_forward(ha, lng_ref[...], lnb_ref[...], wm, bt_ref[...], mixed_c)
            mixed = mixed_c[...]
            sg = u * mixed
            dsg, dga = _rms_bwd(dycat[:tc, :aw], sg, _rstd(sg), ga_ref[...])
            dga_ref[...] += dga
            du = dsg * mixed
            dmixed = dsg * u
            dmb = dmixed.astype(BF16)
            for n in range(tc // CHUNK):
                blk = slice(n * CHUNK, (n + 1) * CHUNK)
                dbs_ref[...] += dmixed[blk]
                for hh in range(HEADS):
                    cols = slice(hh * hd, (hh + 1) * hd)
                    dvln_c[blk, cols] = _dot_tn(wm[hh], dmb[blk, cols])
                    dwsp_ref[hh] += mask * _dot_nt(dmb[blk, cols], vln[blk, cols])
            dvln = dvln_c[...]
            dlng_ref[...] += jnp.sum(dvln * xhat, axis=0, keepdims=True)
            dlnb_ref[...] += jnp.sum(dvln, axis=0, keepdims=True)
            dxh = dvln * lng_ref[...]
            dv = rl * (dxh - jnp.mean(dxh, axis=-1, keepdims=True) - xhat * jnp.mean(dxh * xhat, axis=-1, keepdims=True))
            dh_ref[rows, :in_a] = (jnp.concatenate([du, dv], axis=-1) * _gelu_grad(ha, th)).astype(BF16)

            dxn = _dot(dh_ref[rows, :], win_ref[...])
            xv = x_ref[rows, :]
            dx, dgm = _rms_bwd(dxn, xv, _rstd(xv), gm_ref[...])
            dgm_ref[...] += dgm
            dx_ref[rows, :] = dx1 + dx

        for r0 in range(0, tm, tc):
            chain(r0)

    full = lambda shape: pl.BlockSpec(shape, lambda i: (0,) * len(shape))
    tok = pl.BlockSpec((tm, d), lambda i: (i, 0))
    nxt = lambda i: (jnp.minimum((i + 1) * hb_blocks, last_blk), 0)
    prv = lambda i: (jnp.maximum(i * hb_blocks - 1, 0), 0)
    return _hosted(
        body, carry, grid=(nt,),
        in_specs=[tok, pl.BlockSpec((HALO, d), nxt), tok, full((1, d)),
                  pl.BlockSpec((tm, n_in), lambda i: (i, 0)), pl.BlockSpec((HALO, n_in), prv), pl.BlockSpec((HALO, n_in), nxt),
                  full((1, aw)), full((1, aw)), full((HEADS, CHUNK, CHUNK)), full((CHUNK, HEADS)), full((3, bw)),
                  full((1, aw)), full((1, bw)), full((d, d)), full((n_in, d))],
        out_specs=[pl.BlockSpec((tm, n_in), lambda i: (i, 0)), tok,
                   full((1, aw)), full((1, bw)), full((SUB, bw)), full((1, aw)), full((1, aw)),
                   full((HEADS, CHUNK, CHUNK)), full((CHUNK, aw)), full((1, d))],
        out_shape=(SDS((s, n_in), BF16), SDS((s, d), F32),
                   SDS((1, aw), F32), SDS((1, bw), F32), SDS((SUB, bw), F32), SDS((1, aw), F32), SDS((1, aw), F32),
                   SDS((HEADS, CHUNK, CHUNK), F32), SDS((CHUNK, aw), F32), SDS((1, d), F32)),
        scratch_shapes=[pltpu.VMEM((tm, aw), F32), pltpu.VMEM((tm, aw), F32)],
        compiler_params=_arb(1), name="mix_backward")(dx1, dx1, x, g_mix, h, h, h, lng, lnb, w_sp, bt, conv_w, ga, gb, w_out, w_in)


def _bias_grad(dbs):
    aw = dbs.shape[1]
    hd = aw // HEADS

    def body(dbs_ref, out_ref):
        ones = jnp.ones((SUB, hd), F32)
        for hh in range(HEADS):
            r = lax.dot_general(ones, dbs_ref[:, hh * hd:(hh + 1) * hd], (((1,), (1,)), ((), ())),
                                precision=lax.Precision.HIGHEST, preferred_element_type=F32)
            out_ref[hh:hh + 1, :] = r[0:1]

    return _pcall(body, out_shape=SDS((HEADS, CHUNK), F32), name="bias_grad")(dbs)


def _wgrad_body(a_ref, b_ref, o_ref):
    o_ref[...] = _dot_tn(a_ref[...], b_ref[...])


def _wgrad(a, b, name, carry=()):
    k, m = a.shape
    n = b.shape[1]
    tm = _tile(m, 512, LANES)
    tn = _tile(n, 1024, LANES)
    return _hosted(
        functools.partial(_wgrad_body), carry, grid=(m // tm, n // tn),
        in_specs=[pl.BlockSpec((k, tm), lambda i, j: (0, i)), pl.BlockSpec((k, tn), lambda i, j: (0, j))],
        out_specs=pl.BlockSpec((tm, tn), lambda i, j: (i, j)),
        out_shape=SDS((m, n), F32), compiler_params=_arb(2), name=name)(a, b)


def _wgrad_blocked_lhs(a, b, name, carry=()):
    nb, k, t = a.shape
    n = b.shape[1]
    tn = _tile(n, 1024, LANES)
    return _hosted(
        functools.partial(_wgrad_body), carry, grid=(nb, n // tn),
        in_specs=[pl.BlockSpec((None, k, t), lambda i, j: (i, 0, 0)), pl.BlockSpec((k, tn), lambda i, j: (0, j))],
        out_specs=pl.BlockSpec((t, tn), lambda i, j: (i, j)),
        out_shape=SDS((nb * t, n), F32), compiler_params=_arb(2), name=name)(a, b)


def _wgrad_blocked_rhs(a, b, name, carry=()):
    k, m = a.shape
    nb, _, t = b.shape
    tm = _tile(m, 512, LANES)
    return _hosted(
        functools.partial(_wgrad_body), carry, grid=(m // tm, nb),
        in_specs=[pl.BlockSpec((k, tm), lambda i, j: (0, i)), pl.BlockSpec((None, k, t), lambda i, j: (j, 0, 0))],
        out_specs=pl.BlockSpec((None, tm, t), lambda i, j: (j, i, 0)),
        out_shape=SDS((nb, m, t), F32), compiler_params=_arb(2), name=name)(a, b)


def _unblock_cols(wb, name, carry=()):
    nb, r, t = wb.shape
    tr = _tile(r, 256, 16)

    def body(w_ref, o_ref):
        o_ref[...] = jnp.concatenate([w_ref[j].astype(F32) for j in range(nb)], axis=-1).astype(o_ref.dtype)

    return _hosted(
        body, carry, grid=(r // tr,),
        in_specs=[pl.BlockSpec((nb, tr, t), lambda i: (0, i, 0))], out_specs=pl.BlockSpec((tr, nb * t), lambda i: (i, 0)),
        out_shape=SDS((r, nb * t), wb.dtype), compiler_params=_arb(1), name=name)(wb)


def _block_cols(w, nb, name, carry=()):
    r, n = w.shape
    t = n // nb
    tr = _tile(r, 256, 16)

    def body(w_ref, o_ref):
        wv = w_ref[...]
        for j in range(nb):
            o_ref[j] = wv[:, j * t:(j + 1) * t]

    return _hosted(
        body, carry, grid=(r // tr,),
        in_specs=[pl.BlockSpec((tr, n), lambda i: (i, 0))], out_specs=pl.BlockSpec((nb, tr, t), lambda i: (0, i, 0)),
        out_shape=SDS((nb, r, t), w.dtype), compiler_params=_arb(1), name=name)(w)


def _place():
    x, y, c = lax.axis_index("x"), lax.axis_index("y"), lax.axis_index("c")
    return x, y, c, [(1 - x, y), (x, 1 - y), (1 - x, 1 - y)]


def _all_gather(shards):
    n = len(shards)
    slots = 9
    cut = [(s.shape[0] // 32) * 16 for s in shards]

    def build(ins, outs, sems):
        send_sems, recv_sems, local_sems = sems
        x, y, c, _ = _place()
        me, sib, xn, yn, dg = (x, y, c), (x, y, 1 - c), (1 - x, y, c), (x, 1 - y, c), (1 - x, 1 - y, c)
        other = lambda p: (p[0], p[1], 1 - p[2])

        def rows(a, p, part=None):
            ref = outs[a].at[4 * p[0] + 2 * p[1] + p[2]]
            if part is None or cut[a] == 0:
                return ref if part in (None, 0) else None
            return ref.at[pl.ds(0, cut[a])] if part == 0 else ref.at[pl.ds(cut[a], shards[a].shape[0] - cut[a])]

        def copy(a, k, ref, to, src=None):
            if ref is None:
                return None
            return pltpu.make_async_remote_copy(
                src_ref=ref if src is None else src, dst_ref=ref, send_sem=send_sems.at[slots * a + k],
                recv_sem=recv_sems.at[slots * a + k], device_id=to, device_id_type=MESH)

        def real(cps):
            return [cp for cp in cps if cp is not None]

        class Copies:
            own = lambda a: [copy(a, 1, rows(a, me), xn, ins[a]), copy(a, 2, rows(a, me), yn, ins[a]),
                             copy(a, 0, rows(a, me), sib, ins[a])]
            local = lambda a: pltpu.make_async_copy(ins[a], rows(a, me), local_sems.at[a])
            from_x = lambda a: copy(a, 1, rows(a, xn), me)
            from_y = lambda a: copy(a, 2, rows(a, yn), me)
            after_x = lambda a: real([copy(a, 4, rows(a, xn, 1), yn), copy(a, 5, rows(a, xn), sib)])
            after_y = lambda a: real([copy(a, 3, rows(a, yn, 0), xn), copy(a, 6, rows(a, yn), sib)])
            diag_in = lambda a: real([copy(a, 3, rows(a, dg, 0), me), copy(a, 4, rows(a, dg, 1), me)])
            diag_on = lambda a: real([copy(a, 7, rows(a, dg, 0), sib), copy(a, 8, rows(a, dg, 1), sib)])
            from_sib = lambda a: real([copy(a, 0, rows(a, sib), me), copy(a, 5, rows(a, other(xn)), me),
                                       copy(a, 6, rows(a, other(yn)), me), copy(a, 7, rows(a, other(dg), 0), me),
                                       copy(a, 8, rows(a, other(dg), 1), me)])

        return Copies

    def start(ins, outs, sems):
        cps = build(ins, outs, sems)
        for a in range(n):
            for cp in cps.own(a):
                cp.start()
        for a in range(n):
            cps.local(a).start()

    def relay(ins, outs, sems):
        cps = build(ins, outs, sems)
        for a in range(n):
            cps.from_x(a).wait_recv()
            for cp in cps.after_x(a):
                cp.start()
            cps.from_y(a).wait_recv()
            for cp in cps.after_y(a):
                cp.start()

    def finish(ins, outs, sems):
        cps = build(ins, outs, sems)
        for a in range(n):
            for arrived, onward in zip(cps.diag_in(a), cps.diag_on(a)):
                arrived.wait_recv()
                onward.start()
        for a in range(n):
            for cp in cps.from_sib(a):
                cp.wait_recv()
            for cp in cps.own(a) + cps.after_x(a) + cps.after_y(a) + cps.diag_on(a):
                cp.wait_send()
            cps.local(a).wait()

    return _Exchange(shards, [SDS((N_DEV,) + s.shape, s.dtype) for s in shards],
                     [pltpu.SemaphoreType.DMA((slots * n,)), pltpu.SemaphoreType.DMA((slots * n,)),
                      pltpu.SemaphoreType.DMA((n,))], start, finish, relay)


def _swap_exchange(ins, out_shape, per, copies):
    def start(i, o, sems):
        for cp in copies(i, o, sems):
            cp.start()

    def finish(i, o, sems):
        for cp in copies(i, o, sems):
            cp.wait()

    n = per * len(ins)
    return _Exchange(ins, out_shape, [pltpu.SemaphoreType.DMA((n,)), pltpu.SemaphoreType.DMA((n,))], start, finish)


def _exchange_c(gs):
    def copies(ins, outs, sems):
        x, y, c, _ = _place()
        return [pltpu.make_async_remote_copy(
                    src_ref=ins[a].at[2 * k + 1 - c], dst_ref=outs[a].at[k],
                    send_sem=sems[0].at[4 * a + k], recv_sem=sems[1].at[4 * a + k],
                    device_id=(x, y, 1 - c), device_id_type=MESH)
                for a in range(len(gs)) for k in range(4)]

    return _swap_exchange(gs, [SDS((4,) + g.shape[1:], g.dtype) for g in gs], 4, copies)


def _exchange_xy(sends):
    def copies(ins, outs, sems):
        x, y, c, chips = _place()
        return [pltpu.make_async_remote_copy(
                    src_ref=ins[a].at[t], dst_ref=outs[a].at[t],
                    send_sem=sems[0].at[3 * a + t], recv_sem=sems[1].at[3 * a + t],
                    device_id=(*chips[t], c), device_id_type=MESH)
                for a in range(len(sends)) for t in range(3)]

    return _swap_exchange(sends, [SDS(s.shape, s.dtype) for s in sends], 3, copies)


def _rs_combine(g, recv, pos, name, carry=()):
    _, r, cdim = g.shape
    tr = _tile(r, 256, 16)

    def body(pos_ref, g0, r0, g1, r1, g2, r2, g3, r3, keep_ref, send_ref):
        keep_ref[...] = g0[...] + r0[...]
        send_ref[0] = (g1[...] + r1[...]).astype(BF16)
        send_ref[1] = (g2[...] + r2[...]).astype(BF16)
        send_ref[2] = (g3[...] + r3[...]).astype(BF16)

    def k_of(p, t):
        px = p[0] if t in (0, 2) else 1 - p[0]
        py = p[1] if t in (0, 1) else 1 - p[1]
        return 2 * px + py

    blk = (None, tr, cdim)
    in_specs = []
    for t in range(4):
        in_specs.append(pl.BlockSpec(blk, functools.partial(lambda j, p, t: (2 * k_of(p, t) + p[2], j, 0), t=t)))
        in_specs.append(pl.BlockSpec(blk, functools.partial(lambda j, p, t: (k_of(p, t), j, 0), t=t)))
    return _hosted(
        body, carry, n_prefetch=1, out_shape=(SDS((r, cdim), F32), SDS((3, r, cdim), BF16)),
        grid=(r // tr,), in_specs=in_specs,
        out_specs=[pl.BlockSpec((tr, cdim), lambda j, p: (j, 0)), pl.BlockSpec((3, tr, cdim), lambda j, p: (0, j, 0))],
        compiler_params=_arb(1), name=name)(pos, g, recv, g, recv, g, recv, g, recv)


def _adamw_shard(keep, recv, w, m, v, name):
    r, cdim = w.shape
    tr = _tile(r, 256, 16)

    def body(k_ref, r_ref, w_ref, m_ref, v_ref, g_ref, d_ref, nm_ref, nv_ref):
        g = ((k_ref[...] + r_ref[0].astype(F32)) + r_ref[1].astype(F32)) + r_ref[2].astype(F32)
        g_ref[...] = g
        d_ref[...], nm_ref[...], nv_ref[...] = _adamw(w_ref[...], g, m_ref[...], v_ref[...])

    blk = pl.BlockSpec((tr, cdim), lambda j: (j, 0))
    out = SDS((r, cdim), F32)
    return _pcall(body, grid=(r // tr,), in_specs=[blk, pl.BlockSpec((3, tr, cdim), lambda j: (0, j, 0)), blk, blk, blk],
                  out_specs=[blk] * 4, out_shape=(out,) * 4, compiler_params=_arb(1), name=name)(keep, recv, w, m, v)


_HBM = pl.BlockSpec(memory_space=pltpu.HBM)
_SEM = pl.BlockSpec(memory_space=pltpu.SEMAPHORE)
_SPLIT = pltpu.CompilerParams(has_side_effects=pltpu.SideEffectType.DATAFLOW_SIDE_EFFECTING)


def _split_copies(kind, n, refs):
    srcs, lands, (send_sems, recv_sems) = refs[:n], refs[n:2 * n], refs[2 * n:2 * n + 2]
    x, y, c, chips = _place()
    per = _SPLIT_COPIES[kind]
    if kind == "xy":
        ends = lambda a, t: (srcs[a].at[t], lands[a].at[t], (*chips[t], c))
    else:
        ends = lambda a, k: (srcs[a].at[2 * k + 1 - c], lands[a].at[k], (x, y, 1 - c))
    cps = []
    for a in range(n):
        for t in range(per):
            src, dst, to = ends(a, t)
            cps.append(pltpu.make_async_remote_copy(src_ref=src, dst_ref=dst, send_sem=send_sems.at[per * a + t],
                                                    recv_sem=recv_sems.at[per * a + t], device_id=to, device_id_type=MESH))
    return cps


_SPLIT_COPIES = {"xy": 3, "c": 4}


def _exchange_start(kind, arrays, name, after=None):
    n = len(arrays)
    order = [] if after is None else [after]

    def body(*refs):
        refs = refs[:2 * n] + refs[2 * n + len(order):]
        for cp in _split_copies(kind, n, refs):
            cp.start()
        refs[-1][...] = jnp.zeros_like(refs[-1])

    hbm = lambda a: pltpu.with_memory_space_constraint(a, pltpu.HBM)
    land = [a.shape if kind == "xy" else (4,) + a.shape[1:] for a in arrays]
    bufs = [pltpu.HBM(a.shape, a.dtype) for a in arrays] + [pltpu.HBM(s, a.dtype) for s, a in zip(land, arrays)]
    sems = pltpu.SemaphoreType.DMA((_SPLIT_COPIES[kind] * n,))
    res = _pcall(
        body, name=name, out_shape=(sems, sems, *bufs, SDS((SUB, LANES), F32)),
        in_specs=[_HBM] * (2 * n) + _hbm_specs(len(order)),
        out_specs=[_SEM, _SEM] + [_HBM] * (2 * n) + [pl.BlockSpec(memory_space=pltpu.VMEM)],
        input_output_aliases={k: 2 + k for k in range(2 * n)}, compiler_params=_SPLIT)(
            *[hbm(a) for a in arrays], *[hbm(lax.empty(s, a.dtype)) for s, a in zip(land, arrays)], *order)
    return (kind, n, res[:-1]), res[-1]


def _exchange_wait(started, after, name, sources=False):
    kind, n, (send_sems, recv_sems, *bufs) = started

    def body(*refs):
        for cp in _split_copies(kind, n, refs):
            cp.wait_send()
            cp.wait_recv()

    shapes = [pltpu.HBM(b.shape, b.dtype) for b in bufs]
    res = _pcall(
        body, name=name, out_shape=tuple(shapes),
        in_specs=[_HBM] * (2 * n) + [_SEM, _SEM, pl.BlockSpec(memory_space=pl.ANY)], out_specs=[_HBM] * (2 * n),
        input_output_aliases={k: k for k in range(2 * n)}, compiler_params=_SPLIT)(*bufs, send_sems, recv_sems, after)
    return (list(res[:n]), list(res[n:])) if sources else list(res[n:])


def _follow(token):
    nothing = lambda ins, outs, sems: None
    return _Exchange([token], [], [], nothing, nothing)


def _adamw_small(gathered, seg, params, conv_rows):
    names = list(params)
    c0, cn = conv_rows

    def body(*refs):
        gat_ref = refs[0]
        ins = refs[1:1 + 3 * len(names)]
        outs = refs[1 + 3 * len(names):]

        def total(r0, rn):
            tot = gat_ref[0, r0:r0 + rn, :]
            for dev in range(1, N_DEV):
                tot = tot + gat_ref[dev, r0:r0 + rn, :]
            return tot

        for k, nm in enumerate(names):
            g = total(*seg[nm])
            w_ref, m_ref, v_ref = ins[3 * k:3 * k + 3]
            g_ref, d_ref, nm_ref, nv_ref = outs[4 * k:4 * k + 4]
            g_ref[...] = g
            d_ref[...], nm_ref[...], nv_ref[...] = _adamw(w_ref[...], g, m_ref[...], v_ref[...])
        outs[-2][...] = total(c0, cn)
        outs[-1][...] = total(*seg["loss"])

    flat_in = [a for nm in names for a in params[nm]]
    out_shape = []
    for nm in names:
        out_shape += [SDS(params[nm][0].shape, F32)] * 4
    out_shape += [SDS((cn, LANES), F32), SDS((seg["loss"][1], LANES), F32)]
    res = _pcall(body, out_shape=tuple(out_shape), name="adamw_small")(gathered, *flat_in)
    per = {nm: res[4 * k:4 * k + 4] for k, nm in enumerate(names)}
    return per, res[-2], res[-1]


def _adamw_one(w, g, m, v, name):
    def body(w_ref, g_ref, m_ref, v_ref, d_ref, nm_ref, nv_ref):
        d_ref[...], nm_ref[...], nv_ref[...] = _adamw(w_ref[...], g_ref[...], m_ref[...], v_ref[...])

    return _pcall(body, out_shape=(SDS(w.shape, F32),) * 3, name=name)(w, g, m, v)


def _rows128(a):
    return a.reshape(-1, LANES)


def _pack_small(gs, loss_tile):
    seg, pieces, row = {}, [], 0
    for nm in SMALL + ("conv_w", "loss"):
        piece = loss_tile if nm == "loss" else _rows128(gs[nm])
        rn = _round_up(piece.shape[0], SUB)
        pieces.append(jnp.pad(piece, ((0, rn - piece.shape[0]), (0, 0))))
        seg[nm] = (row, piece.shape[0])
        row += rn
    return jnp.concatenate(pieces, axis=0), seg


def _step(x, mem, target, wb, conv_w, sp, pos):
    s, d = x.shape
    tm = min(TOKEN_TILE, s)
    tm_wide = min(2 * TOKEN_TILE, s)
    rows = lambda w8: w8.reshape(-1, w8.shape[2])
    shards = lambda g: g.reshape((N_DEV, -1) + g.shape[1:])
    bt = sp["b_spatial"].T

    (w_in8, conv8), = _run_exchanges([_all_gather([wb["w_in"], conv_w])], "gather_w_in")
    conv_full = conv8.transpose(1, 0, 2).reshape(3, -1)
    w_in_t = rows(w_in8)
    (xn1, h), ((w_out8, w_kv8, w_q8),) = _in_forward(
        x, sp["ln_mix_g"], w_in_t, tm, carry=[_all_gather([wb["w_out"], wb["w_kv"], wb["w_q"]])])
    w_out = rows(w_out8)
    (ycat, x1), ((w_o8, w_down8),) = _mix_forward(
        h, x, sp["sgu_ln_g"], sp["sgu_ln_b"], sp["w_spatial"], bt, conv_full, sp["grp_norm_a"], sp["grp_norm_b"], w_out, tm,
        carry=[_all_gather([wb["w_o"], wb["w_down"]])])
    w_q, w_o, w_down = rows(w_q8), rows(w_o8), rows(w_down8)
    memn, kv = _kv_forward(mem, sp["ln_mem_g"], w_kv8)
    (xn2, q, o, x2), ((w_gu8,),) = _attn_forward(
        x1, sp["ln_attn_g"], w_q, kv, w_o, tm, carry=[_all_gather([wb["w_gate_up"]])])
    w_gu = w_gu8.reshape((2, N_DEV // 2) + w_gu8.shape[1:])
    xn3, gu, x3 = _ffn_forward(x2, sp["ln_ffn_g"], w_gu, w_down, tm_wide)

    loss, d_lnf, dx3, dx3b = _final_backward(x3, target, sp["ln_final_g"], tm_wide)
    act, dgu, dxn3 = _swiglu_backward(dx3b, gu, w_gu, w_down, tm_wide)
    g_gu, _ = _wgrad_blocked_lhs(dgu.reshape((N_DEV,) + dgu.shape[2:]), xn3, "wgrad_gate_up")
    g_gu = shards(g_gu)
    g_down, ((rc_gu,),) = _wgrad_blocked_lhs(act, dx3b, "wgrad_down", carry=[_exchange_c([g_gu])])
    g_down = shards(g_down)
    keep, pending = {}, []
    (keep["w_gate_up"], send_gu), _ = _rs_combine(g_gu, rc_gu, pos, "rs_combine_w_gate_up")
    started, token = _exchange_start("xy", [send_gu], "exchange_xy_1_start")
    pending.append((("w_gate_up",), started))
    c_down, token = _exchange_start("c", [g_down], "exchange_c_1_start", after=token)
    (dx2b, dq, dx1, dx1b, dkv, d_lnattn, d_lnffn), _ = _attn_backward(
        dx3, dxn3, x2, sp["ln_ffn_g"], x1, sp["ln_attn_g"], q, kv, w_q, w_o, tm, carry=[_follow(token)])
    (g_down,), (rc_down,) = _exchange_wait(c_down, dx1b, "exchange_c_1_wait", sources=True)
    (keep["w_down"], send_down), _ = _rs_combine(g_down, rc_down, pos, "rs_combine_w_down")
    g_o, _ = _wgrad(o, dx2b, "wgrad_o")
    g_q, _ = _wgrad(xn2, dq, "wgrad_q")
    g_o, g_q = shards(g_o), shards(g_q)
    g_kv, d_lnmem = _kv_backward(dkv, memn, mem, sp["ln_mem_g"], w_kv8)
    c_oqkv, token = _exchange_start("c", [g_o, g_q, g_kv], "exchange_c_2_start")
    g_out, _ = _wgrad(ycat, dx1b, "wgrad_out", carry=[_follow(token)])
    g_out = shards(g_out)
    (g_o, g_q, g_kv), (rc_o, rc_q, rc_kv) = _exchange_wait(c_oqkv, g_out, "exchange_c_2_wait", sources=True)
    c_out, token = _exchange_start("c", [g_out], "exchange_c_3_start")
    (keep["w_o"], send_o), _ = _rs_combine(g_o, rc_o, pos, "rs_combine_w_o", carry=[_follow(token)])
    (keep["w_q"], send_q), _ = _rs_combine(g_q, rc_q, pos, "rs_combine_w_q")
    (keep["w_kv"], send_kv), _ = _rs_combine(g_kv, rc_kv, pos, "rs_combine_w_kv")
    (g_out,), (rc_out,) = _exchange_wait(c_out, send_kv, "exchange_c_3_wait", sources=True)
    (keep["w_out"], send_out), _ = _rs_combine(g_out, rc_out, pos, "rs_combine_w_out")
    started, token = _exchange_start("xy", [send_down, send_o, send_q, send_out, send_kv], "exchange_xy_2_start")
    pending.append((("w_down", "w_o", "w_q", "w_out", "w_kv"), started))
    (dh, dx, d_ga, d_gb, d_cw, d_lng, d_lnb, d_wsp, d_bs, d_lnmix), _ = _mix_backward(
        dx1, x, sp["ln_mix_g"], h, sp["sgu_ln_g"], sp["sgu_ln_b"], sp["w_spatial"], bt, conv_full,
        sp["grp_norm_a"], sp["grp_norm_b"], w_out, w_in_t, tm, carry=[_follow(token)])
    gs = {"ln_mix_g": d_lnmix, "sgu_ln_g": d_lng, "sgu_ln_b": d_lnb, "w_spatial": d_wsp, "b_spatial": _bias_grad(d_bs),
          "conv_w": d_cw[:3], "grp_norm_a": d_ga, "grp_norm_b": d_gb, "ln_attn_g": d_lnattn, "ln_mem_g": d_lnmem,
          "ln_ffn_g": d_lnffn, "ln_final_g": d_lnf}
    packed, seg = _pack_small(gs, loss)
    g_in, (_, (small_all,)) = _wgrad(dh, xn1, "wgrad_in", carry=[_follow(token), _all_gather([packed])])
    g_in = shards(g_in)
    c_in, token = _exchange_start("c", [g_in], "exchange_c_4_start")
    return dx, keep, pending, (g_in, c_in), token, small_all, seg


def kernel(x, mem, ln_mix_g, w_in, sgu_ln_g, sgu_ln_b, w_spatial, b_spatial, conv_w, grp_norm_a, grp_norm_b, w_out, ln_attn_g, ln_mem_g, w_q, w_kv, w_o, ln_ffn_g, w_gate_up, w_down, ln_final_g, loss_target, m_ln_mix_g, m_w_in, m_sgu_ln_g, m_sgu_ln_b, m_w_spatial, m_b_spatial, m_conv_w, m_grp_norm_a, m_grp_norm_b, m_w_out, m_ln_attn_g, m_ln_mem_g, m_w_q, m_w_kv, m_w_o, m_ln_ffn_g, m_w_gate_up, m_w_down, m_ln_final_g, v_ln_mix_g, v_w_in, v_sgu_ln_g, v_sgu_ln_b, v_w_spatial, v_b_spatial, v_conv_w, v_grp_norm_a, v_grp_norm_b, v_w_out, v_ln_attn_g, v_ln_mem_g, v_w_q, v_w_kv, v_w_o, v_ln_ffn_g, v_w_gate_up, v_w_down, v_ln_final_g):
    order = ["ln_mix_g", "w_in", "sgu_ln_g", "sgu_ln_b", "w_spatial", "b_spatial", "conv_w", "grp_norm_a", "grp_norm_b",
             "w_out", "ln_attn_g", "ln_mem_g", "w_q", "w_kv", "w_o", "ln_ffn_g", "w_gate_up", "w_down", "ln_final_g"]
    W = dict(ln_mix_g=ln_mix_g, w_in=w_in, sgu_ln_g=sgu_ln_g, sgu_ln_b=sgu_ln_b, w_spatial=w_spatial, b_spatial=b_spatial,
             conv_w=conv_w, grp_norm_a=grp_norm_a, grp_norm_b=grp_norm_b, w_out=w_out, ln_attn_g=ln_attn_g,
             ln_mem_g=ln_mem_g, w_q=w_q, w_kv=w_kv, w_o=w_o, ln_ffn_g=ln_ffn_g, w_gate_up=w_gate_up, w_down=w_down,
             ln_final_g=ln_final_g)
    M = dict(ln_mix_g=m_ln_mix_g, w_in=m_w_in, sgu_ln_g=m_sgu_ln_g, sgu_ln_b=m_sgu_ln_b, w_spatial=m_w_spatial,
             b_spatial=m_b_spatial, conv_w=m_conv_w, grp_norm_a=m_grp_norm_a, grp_norm_b=m_grp_norm_b, w_out=m_w_out,
             ln_attn_g=m_ln_attn_g, ln_mem_g=m_ln_mem_g, w_q=m_w_q, w_kv=m_w_kv, w_o=m_w_o, ln_ffn_g=m_ln_ffn_g,
             w_gate_up=m_w_gate_up, w_down=m_w_down, ln_final_g=m_ln_final_g)
    V = dict(ln_mix_g=v_ln_mix_g, w_in=v_w_in, sgu_ln_g=v_sgu_ln_g, sgu_ln_b=v_sgu_ln_b, w_spatial=v_w_spatial,
             b_spatial=v_b_spatial, conv_w=v_conv_w, grp_norm_a=v_grp_norm_a, grp_norm_b=v_grp_norm_b, w_out=v_w_out,
             ln_attn_g=v_ln_attn_g, ln_mem_g=v_ln_mem_g, w_q=v_w_q, w_kv=v_w_kv, w_o=v_w_o, ln_ffn_g=v_ln_ffn_g,
             w_gate_up=v_w_gate_up, w_down=v_w_down, ln_final_g=v_ln_final_g)

    bw = conv_w.shape[1] * N_DEV
    pos = jnp.stack([lax.axis_index("x"), lax.axis_index("y"), lax.axis_index("c")]).astype(jnp.int32)
    me = 4 * pos[0] + 2 * pos[1] + pos[2]

    sp = {nm: (W[nm].reshape(1, -1) if W[nm].ndim == 1 else W[nm]) for nm in SMALL}
    view = lambda a, nm: a.T if nm in TRANSPOSED else a
    wb = {nm: view(W[nm], nm).astype(BF16) for nm in BIG}
    grad_x, keep, pending, (g_in, c_in), token, small_all, seg = _step(
        x[0], mem[0], loss_target[0], wb, conv_w, sp, pos)

    out = {}

    def update(k, names, started, token):
        landed = _exchange_wait(started, token, "exchange_xy_%d_wait" % k)
        for nm, rxy in zip(names, landed):
            res = _adamw_shard(keep[nm], rxy, view(W[nm], nm), view(M[nm], nm), view(V[nm], nm), "adamw_" + nm)
            out[nm] = tuple(view(a, nm) for a in res)
            token = res[0]
        return token

    token = update(1, *pending[0], token)
    (g_in,), (rc_in,) = _exchange_wait(c_in, token, "exchange_c_4_wait", sources=True)
    (keep["w_in"], send_in), _ = _rs_combine(g_in, rc_in, pos, "rs_combine_w_in")
    xy_in, token = _exchange_start("xy", [send_in], "exchange_xy_3_start")
    token = update(2, *pending[1], token)
    update(3, ("w_in",), xy_in, token)

    params = {nm: (_rows128(W[nm]), _rows128(M[nm]), _rows128(V[nm])) for nm in SMALL}
    per, conv_g_rows, loss_sum = _adamw_small(small_all, seg, params, seg["conv_w"])
    for nm in SMALL:
        out[nm] = tuple(a.reshape(W[nm].shape) for a in per[nm])
    conv_g = lax.dynamic_slice_in_dim(conv_g_rows.reshape(3, bw), me * conv_w.shape[1], conv_w.shape[1], axis=1)
    out["conv_w"] = (conv_g,) + tuple(_adamw_one(conv_w, conv_g, m_conv_w, v_conv_w, "adamw_conv"))

    loss = loss_sum[0, 0]
    res = [loss, grad_x[None]]
    for k in range(4):
        res += [out[nm][k] for nm in order]
    return tuple(res)
```

```python
import functools

import jax
import jax.numpy as jnp
from jax import lax
from jax.experimental import pallas as pl
from jax.experimental.pallas import tpu as pltpu

F32 = jnp.float32
BF16 = jnp.bfloat16
SDS = jax.ShapeDtypeStruct
MESH = pl.DeviceIdType.MESH

EPS = 1e-6
N_DEV = 8
HEADS = 4
CHUNK = 128
HALO = 16
SUB = 8
LANES = 128
TOKEN_TILE = 512
ROW_CHUNK = 256
RELAY_AT = 0.7

ADAM_LR = 0.001
ADAM_B1 = 0.9
ADAM_B2 = 0.999
ADAM_EPS = 1e-08
ADAM_WD = 0.01
ADAM_STEP = 10

BIG = ("w_in", "w_out", "w_q", "w_kv", "w_o", "w_gate_up", "w_down")
TRANSPOSED = ("w_in", "w_gate_up")
SMALL = ("ln_mix_g", "sgu_ln_g", "sgu_ln_b", "w_spatial", "b_spatial", "grp_norm_a", "grp_norm_b",
         "ln_attn_g", "ln_mem_g", "ln_ffn_g", "ln_final_g")


class _Exchange:
    def __init__(self, ins, out_shape, sems, start, finish, relay=None):
        self.ins, self.out_shape, self.sems = list(ins), list(out_shape), list(sems)
        self.start, self.finish, self.relay = start, finish, relay


def _pcall(body, carry=(), n_prefetch=0, **kw):
    if carry:
        return functools.partial(_carrying_call, body, tuple(carry), n_prefetch, kw)
    if n_prefetch:
        kw["grid_spec"] = pltpu.PrefetchScalarGridSpec(
            num_scalar_prefetch=n_prefetch, grid=kw.pop("grid"), in_specs=kw.pop("in_specs"),
            out_specs=kw.pop("out_specs"), scratch_shapes=kw.pop("scratch_shapes", ()))
    return pl.pallas_call(body, **kw)


def _carrying_call(body, carry, n_prefetch, kw, *args):
    kw = dict(kw)
    out_shape = kw.pop("out_shape")
    single = not isinstance(out_shape, (tuple, list))
    outs_shape = (out_shape,) if single else tuple(out_shape)
    out_specs = kw.pop("out_specs")
    out_specs = [out_specs] if single else list(out_specs)
    in_specs = list(kw.pop("in_specs"))
    scratch = list(kw.pop("scratch_shapes", ()))
    grid = tuple(kw.get("grid", ()))
    n_in, n_out, n_scr = len(args), len(outs_shape), len(scratch)

    def split(refs, k, counts):
        parts = []
        for cnt in counts:
            parts.append(refs[k:k + cnt])
            k += cnt
        return parts, k

    def wrapped(*refs):
        cins, k = split(refs, n_in, [len(p.ins) for p in carry])
        outs = refs[k:k + n_out]
        couts, k = split(refs, k + n_out, [len(p.out_shape) for p in carry])
        scr = refs[k:k + n_scr]
        csems, _ = split(refs, k + n_scr, [len(p.sems) for p in carry])
        first, last = True, True
        for a, g in enumerate(grid):
            first = (pl.program_id(a) == 0) & first
            last = (pl.program_id(a) == g - 1) & last

        def start_all():
            for p, ci, co, cs in zip(carry, cins, couts, csems):
                p.start(ci, co, cs)

        def relay_all():
            for p, ci, co, cs in zip(carry, cins, couts, csems):
                if p.relay is not None:
                    p.relay(ci, co, cs)

        def finish_all():
            for p, ci, co, cs in zip(carry, cins, couts, csems):
                p.finish(ci, co, cs)

        if len(grid) == 1:
            relay_now = pl.program_id(0) == min(int(RELAY_AT * grid[0]), grid[0] - 1)
        else:
            relay_now = last
        start_all() if not grid else pl.when(first)(start_all)
        relay_all() if not grid else pl.when(relay_now)(relay_all)
        body(*refs[:n_in], *outs, *scr)
        finish_all() if not grid else pl.when(last)(finish_all)

    c_in = [a for p in carry for a in p.ins]
    c_out = [s for p in carry for s in p.out_shape]
    c_sems = [s for p in carry for s in p.sems]
    res = _pcall(wrapped, n_prefetch=n_prefetch, out_shape=outs_shape + tuple(c_out),
                 in_specs=in_specs + _hbm_specs(len(c_in)), out_specs=out_specs + _hbm_specs(len(c_out)),
                 scratch_shapes=scratch + c_sems, **kw)(*args, *c_in)
    own = res[0] if single else tuple(res[:n_out])
    landed, k = [], n_out
    for p in carry:
        landed.append(list(res[k:k + len(p.out_shape)]))
        k += len(p.out_shape)
    return own, landed


def _hbm_specs(n):
    return [pl.BlockSpec(memory_space=pl.ANY)] * n


def _hosted(body, carry, **kw):
    if carry:
        return _pcall(body, carry=carry, **kw)
    call = _pcall(body, **kw)
    return lambda *args: (call(*args), [])


def _run_exchanges(parts, name):
    def body(*refs):
        pass

    _, landed = _pcall(body, carry=parts, out_shape=(), in_specs=[], out_specs=[], name=name)()
    return landed


def _arb(n):
    return pltpu.CompilerParams(dimension_semantics=("arbitrary",) * n)


def _tile(n, target, mult):
    best = None
    for t in range(mult, min(n, target) + 1, mult):
        if n % t == 0:
            best = t
    return n if best is None else best


def _round_up(n, m):
    return (n + m - 1) // m * m


def _dot(a, b):
    return jnp.dot(a, b, preferred_element_type=F32)


def _dot_nt(a, b):
    return lax.dot_general(a, b, (((1,), (1,)), ((), ())), preferred_element_type=F32)


def _dot_tn(a, b):
    return lax.dot_general(a, b, (((0,), (0,)), ((), ())), preferred_element_type=F32)


def _rstd(x):
    return lax.rsqrt(jnp.mean(x * x, axis=-1, keepdims=True) + EPS)


def _rms_bwd(dy, x, r, g):
    gdy = dy * g
    proj = jnp.sum(gdy * x, axis=-1, keepdims=True) * (1.0 / x.shape[-1])
    dx = r * gdy - x * (r * r * r) * proj
    dg = jnp.sum(dy * (x * r), axis=0, keepdims=True)
    return dx, dg


_GELU_C = 0.7978845608028654
_GELU_A = 0.044715


def _gelu(x):
    t = jnp.tanh(_GELU_C * (x + _GELU_A * x * x * x))
    return 0.5 * x * (1.0 + t), t


def _gelu_grad(x, t):
    return 0.5 * (1.0 + t) + 0.5 * x * (1.0 - t * t) * (_GELU_C * (1.0 + 3.0 * _GELU_A * x * x))


def _sigmoid(x):
    return 1.0 / (1.0 + jnp.exp(-x))


def _softmax(s):
    m = jnp.max(s, axis=-1, keepdims=True)
    e = jnp.exp(s - m)
    return e / jnp.sum(e, axis=-1, keepdims=True)


def _adamw(w, g, m, v):
    m = ADAM_B1 * m + (1.0 - ADAM_B1) * g
    v = ADAM_B2 * v + (1.0 - ADAM_B2) * (g * g)
    m_hat = m / (1.0 - ADAM_B1 ** ADAM_STEP)
    v_hat = v / (1.0 - ADAM_B2 ** ADAM_STEP)
    delta = -ADAM_LR * (m_hat / (jnp.sqrt(v_hat) + ADAM_EPS) + ADAM_WD * w)
    return delta, m, v


def _tril_mask():
    t = lax.broadcasted_iota(jnp.int32, (CHUNK, CHUNK), 0)
    s = lax.broadcasted_iota(jnp.int32, (CHUNK, CHUNK), 1)
    return (s <= t).astype(F32)


def _sgu_forward(ha, lng, lnb, wm, bt, mixed_s):
    aw = ha.shape[1] // 2
    hd = aw // HEADS
    a, th = _gelu(ha)
    u = a[:, :aw]
    v = a[:, aw:]
    mu = jnp.mean(v, axis=-1, keepdims=True)
    vc = v - mu
    rl = lax.rsqrt(jnp.mean(vc * vc, axis=-1, keepdims=True) + EPS)
    xhat = vc * rl
    vln = (xhat * lng + lnb).astype(BF16)
    for n in range(ha.shape[0] // CHUNK):
        rows = slice(n * CHUNK, (n + 1) * CHUNK)
        for h in range(HEADS):
            cols = slice(h * hd, (h + 1) * hd)
            mixed_s[rows, cols] = _dot(wm[h], vln[rows, cols]) + bt[:, h:h + 1]
    return th, u, xhat, rl, vln


def _conv_taps(zext):
    return pltpu.roll(zext, 2, 0), pltpu.roll(zext, 1, 0)


def _kv_forward(mem, g_mem, w_kv):
    ml, d = mem.shape
    xd = w_kv.shape[2]

    def body(mem_ref, g_ref, w_ref, memn_ref, kv_ref):
        x = mem_ref[...]
        memn = (x * _rstd(x) * g_ref[...]).astype(BF16)
        memn_ref[...] = memn
        for j in range(2 * HEADS):
            kv_ref[j] = _dot(memn, w_ref[j]).astype(BF16)

    return _pcall(body, out_shape=(SDS((ml, d), BF16), SDS((2 * HEADS, ml, xd), BF16)), name="kv_forward")(mem, g_mem, w_kv)


def _in_forward(x, g, w_in_t, tm, carry=()):
    s, d = x.shape
    n_in = w_in_t.shape[0]

    def body(x_ref, g_ref, w_ref, xn_ref, h_ref):
        xv = x_ref[...]
        xn = (xv * _rstd(xv) * g_ref[...]).astype(BF16)
        xn_ref[...] = xn
        h_ref[...] = _dot_nt(xn, w_ref[...])

    return _hosted(
        body, carry, grid=(s // tm,),
        in_specs=[pl.BlockSpec((tm, d), lambda i: (i, 0)), pl.BlockSpec((1, d), lambda i: (0, 0)),
                  pl.BlockSpec((n_in, d), lambda i: (0, 0))],
        out_specs=[pl.BlockSpec((tm, d), lambda i: (i, 0)), pl.BlockSpec((tm, n_in), lambda i: (i, 0))],
        out_shape=(SDS((s, d), BF16), SDS((s, n_in), F32)),
        compiler_params=_arb(1), name="in_forward")(x, g, w_in_t)


def _mix_forward(h, x, lng, lnb, w_sp, bt, conv_w, ga, gb, w_out, tm, carry=()):
    s, d = x.shape
    n_in = h.shape[1]
    aw = lng.shape[1]
    bw = d - aw
    in_a = 2 * aw
    hb_blocks = tm // HALO

    def body(h_ref, hprev_ref, x_ref, lng_ref, lnb_ref, wsp_ref, bt_ref, cw_ref, ga_ref, gb_ref, wout_ref,
             ycat_ref, x1_ref, mixed_s):
        i = pl.program_id(0)
        mask = _tril_mask()
        wm = [(wsp_ref[hh] * mask).astype(BF16) for hh in range(HEADS)]
        hv = h_ref[...]
        _, u, _, _, _ = _sgu_forward(hv[:, :in_a], lng_ref[...], lnb_ref[...], wm, bt_ref[...], mixed_s)
        sg = u * mixed_s[...]
        ycat_ref[:, :aw] = (sg * _rstd(sg) * ga_ref[...]).astype(BF16)

        gate_b = hv[:, in_a:in_a + bw]
        z = hv[:, in_a + bw:in_a + 2 * bw] * hv[:, in_a + 2 * bw:]
        hp = hprev_ref[...]
        zp = hp[:, in_a + bw:in_a + 2 * bw] * hp[:, in_a + 2 * bw:]
        zp = jnp.where(i == 0, 0.0, zp)
        zext = jnp.concatenate([zp, z], axis=0)
        z2, z1 = _conv_taps(zext)
        cw = cw_ref[...]
        conv = cw[0:1] * z2[HALO:] + cw[1:2] * z1[HALO:] + cw[2:3] * z
        sc = gate_b * conv
        ycat_ref[:, aw:] = (sc * _rstd(sc) * gb_ref[...]).astype(BF16)
        x1_ref[...] = x_ref[...] + _dot(ycat_ref[...], wout_ref[...])

    full = lambda shape: pl.BlockSpec(shape, lambda i: (0,) * len(shape))
    return _hosted(
        body, carry, grid=(s // tm,),
        in_specs=[pl.BlockSpec((tm, n_in), lambda i: (i, 0)),
                  pl.BlockSpec((HALO, n_in), lambda i: (jnp.maximum(i * hb_blocks - 1, 0), 0)),
                  pl.BlockSpec((tm, d), lambda i: (i, 0)),
                  full((1, aw)), full((1, aw)), full((HEADS, CHUNK, CHUNK)), full((CHUNK, HEADS)),
                  full((3, bw)), full((1, aw)), full((1, bw)), full((d, d))],
        out_specs=[pl.BlockSpec((tm, d), lambda i: (i, 0)), pl.BlockSpec((tm, d), lambda i: (i, 0))],
        out_shape=(SDS((s, d), BF16), SDS((s, d), F32)),
        scratch_shapes=[pltpu.VMEM((tm, aw), F32)],
        compiler_params=_arb(1), name="mix_forward")(h, h, x, lng, lnb, w_sp, bt, conv_w, ga, gb, w_out)


def _attn_forward(x1, g, w_q, kv, w_o, tm, carry=()):
    s, d = x1.shape
    _, ml, xd = kv.shape
    scale = xd ** -0.5

    def body(x1_ref, g_ref, wq_ref, kv_ref, wo_ref, xn_ref, q_ref, o_ref, x2_ref):
        xv = x1_ref[...]
        xn = (xv * _rstd(xv) * g_ref[...]).astype(BF16)
        xn_ref[...] = xn
        q_ref[...] = _dot(xn, wq_ref[...]).astype(BF16)
        for hh in range(HEADS):
            cols = slice(hh * xd, (hh + 1) * xd)
            p = _softmax(_dot_nt(q_ref[:, cols], kv_ref[hh]) * scale)
            o_ref[:, cols] = _dot(p.astype(BF16), kv_ref[HEADS + hh]).astype(BF16)
        x2_ref[...] = xv + _dot(o_ref[...], wo_ref[...])

    tok = pl.BlockSpec((tm, d), lambda i: (i, 0))
    return _hosted(
        body, carry, grid=(s // tm,),
        in_specs=[tok, pl.BlockSpec((1, d), lambda i: (0, 0)), pl.BlockSpec((d, d), lambda i: (0, 0)),
                  pl.BlockSpec((2 * HEADS, ml, xd), lambda i: (0, 0, 0)), pl.BlockSpec((d, d), lambda i: (0, 0))],
        out_specs=[tok, tok, tok, tok],
        out_shape=(SDS((s, d), BF16), SDS((s, d), BF16), SDS((s, d), BF16), SDS((s, d), F32)),
        compiler_params=_arb(1), name="attn_forward")(x1, g, w_q, kv, w_o)


def _ffn_forward(x2, g, w_gu, w_down, tm):
    s, d = x2.shape
    _, nf, tf, _ = w_gu.shape

    def body(x2_ref, g_ref, wgu_ref, wd_ref, xn_ref, gu_ref, x3_ref):
        f = pl.program_id(1)

        @pl.when(f == 0)
        def _():
            xv = x2_ref[...]
            xn_ref[...] = (xv * _rstd(xv) * g_ref[...]).astype(BF16)
            x3_ref[...] = xv

        xn = xn_ref[...]
        gate = _dot_nt(xn, wgu_ref[0])
        up = _dot_nt(xn, wgu_ref[1])
        gu_ref[0] = gate.astype(BF16)
        gu_ref[1] = up.astype(BF16)
        act = (gate * _sigmoid(gate) * up).astype(BF16)
        x3_ref[...] += _dot(act, wd_ref[...])

    tok = pl.BlockSpec((tm, d), lambda i, f: (i, 0))
    return _pcall(
        body, grid=(s // tm, nf),
        in_specs=[tok, pl.BlockSpec((1, d), lambda i, f: (0, 0)),
                  pl.BlockSpec((2, None, tf, d), lambda i, f: (0, f, 0, 0)),
                  pl.BlockSpec((tf, d), lambda i, f: (f, 0))],
        out_specs=[tok, pl.BlockSpec((2, None, tm, tf), lambda i, f: (0, f, i, 0)), tok],
        out_shape=(SDS((s, d), BF16), SDS((2, nf, s, tf), BF16), SDS((s, d), F32)),
        compiler_params=_arb(2), name="ffn_forward")(x2, g, w_gu, w_down)


def _final_backward(x3, target, g_final, tm):
    s, d = x3.shape

    def body(x3_ref, tgt_ref, gf_ref, loss_ref, dgf_ref, dx3_ref, dx3b_ref):
        @pl.when(pl.program_id(0) == 0)
        def _():
            loss_ref[...] = jnp.zeros_like(loss_ref)
            dgf_ref[...] = jnp.zeros_like(dgf_ref)

        xv = x3_ref[...]
        r = _rstd(xv)
        diff = xv * r * gf_ref[...] - tgt_ref[...]
        loss_ref[...] += 0.5 * jnp.sum(jnp.sum(diff * diff, axis=-1, keepdims=True), axis=0, keepdims=True) * (1.0 / d)
        dx3, dgf = _rms_bwd(diff * (1.0 / d), xv, r, gf_ref[...])
        dgf_ref[...] += dgf
        dx3_ref[...] = dx3
        dx3b_ref[...] = dx3.astype(BF16)

    tok = pl.BlockSpec((tm, d), lambda i: (i, 0))
    vec = pl.BlockSpec((1, d), lambda i: (0, 0))
    return _pcall(
        body, grid=(s // tm,), in_specs=[tok, tok, vec],
        out_specs=[pl.BlockSpec((SUB, LANES), lambda i: (0, 0)), vec, tok, tok],
        out_shape=(SDS((SUB, LANES), F32), SDS((1, d), F32), SDS((s, d), F32), SDS((s, d), BF16)),
        compiler_params=_arb(1), name="final_backward")(x3, target, g_final)


def _swiglu_backward(dx3b, gu, w_gu, w_down, tm):
    s, d = dx3b.shape
    _, nf, tf, _ = w_gu.shape

    def body(dx3b_ref, gu_ref, wgu_ref, wd_ref, act_ref, dgu_ref, dxn_ref):
        @pl.when(pl.program_id(1) == 0)
        def _():
            dxn_ref[...] = jnp.zeros_like(dxn_ref)

        for r0 in range(0, tm, ROW_CHUNK):
            rows = slice(r0, r0 + ROW_CHUNK)
            dact = _dot_nt(dx3b_ref[rows, :], wd_ref[...])
            gv = gu_ref[0, rows, :].astype(F32)
            uv = gu_ref[1, rows, :].astype(F32)
            sg = _sigmoid(gv)
            silu = gv * sg
            act_ref[rows, :] = (silu * uv).astype(BF16)
            dgate = (dact * uv * (sg * (1.0 + gv * (1.0 - sg)))).astype(BF16)
            dup = (dact * silu).astype(BF16)
            dgu_ref[0, rows, :] = dgate
            dgu_ref[1, rows, :] = dup
            part = _dot(dgate, wgu_ref[0]) + _dot(dup, wgu_ref[1])
            dxn_ref[rows, :] += part

    tok = pl.BlockSpec((tm, d), lambda i, f: (i, 0))
    pair = pl.BlockSpec((2, None, tm, tf), lambda i, f: (0, f, i, 0))
    return _pcall(
        body, grid=(s // tm, nf),
        in_specs=[tok, pair, pl.BlockSpec((2, None, tf, d), lambda i, f: (0, f, 0, 0)),
                  pl.BlockSpec((tf, d), lambda i, f: (f, 0))],
        out_specs=[pl.BlockSpec((None, tm, tf), lambda i, f: (f, i, 0)), pair, tok],
        out_shape=(SDS((nf, s, tf), BF16), SDS((2, nf, s, tf), BF16), SDS((s, d), F32)),
        compiler_params=_arb(2), name="swiglu_backward")(dx3b, gu, w_gu, w_down)


def _attn_backward(dx3, dxn3, x2, g_ffn, x1, g, q, kv, w_q, w_o, tm, carry=()):
    s, d = x1.shape
    _, ml, xd = kv.shape
    scale = xd ** -0.5

    def body(dx3_ref, dxn3_ref, x2_ref, g2_ref, x1_ref, g_ref, q_ref, kv_ref, wq_ref, wo_ref,
             dx2b_ref, dq_ref, dx1_ref, dx1b_ref, dkv_ref, dg_ref, dg2_ref, do_s):
        i = pl.program_id(0)

        @pl.when(i == 0)
        def _():
            dkv_ref[...] = jnp.zeros_like(dkv_ref)
            dg_ref[...] = jnp.zeros_like(dg_ref)
            dg2_ref[...] = jnp.zeros_like(dg2_ref)

        x2v = x2_ref[...]
        dx2n, dg2 = _rms_bwd(dxn3_ref[...], x2v, _rstd(x2v), g2_ref[...])
        dg2_ref[...] += dg2
        dx2 = dx3_ref[...] + dx2n
        dx2b_ref[...] = dx2.astype(BF16)
        do_s[...] = _dot_nt(dx2b_ref[...], wo_ref[...]).astype(BF16)
        for hh in range(HEADS):
            kc = slice(hh * xd, (hh + 1) * xd)
            qh = q_ref[:, kc]
            kh = kv_ref[hh]
            doh = do_s[:, kc]
            p = _softmax(_dot_nt(qh, kh) * scale)
            dp = _dot_nt(doh, kv_ref[HEADS + hh])
            dkv_ref[HEADS + hh] += _dot_tn(p.astype(BF16), doh)
            ds = (p * (dp - jnp.sum(dp * p, axis=-1, keepdims=True)) * scale).astype(BF16)
            dq_ref[:, kc] = _dot(ds, kh).astype(BF16)
            dkv_ref[hh] += _dot_tn(ds, qh)
        dxn = _dot_nt(dq_ref[...], wq_ref[...])
        xv = x1_ref[...]
        dx, dg = _rms_bwd(dxn, xv, _rstd(xv), g_ref[...])
        dg_ref[...] += dg
        dx1 = dx2 + dx
        dx1_ref[...] = dx1
        dx1b_ref[...] = dx1.astype(BF16)

    tok = pl.BlockSpec((tm, d), lambda i: (i, 0))
    vec = pl.BlockSpec((1, d), lambda i: (0, 0))
    sq = pl.BlockSpec((d, d), lambda i: (0, 0))
    kvs = pl.BlockSpec((2 * HEADS, ml, xd), lambda i: (0, 0, 0))
    return _hosted(
        body, carry, grid=(s // tm,),
        in_specs=[tok, tok, tok, vec, tok, vec, tok, kvs, sq, sq],
        out_specs=[tok, tok, tok, tok, kvs, vec, vec],
        out_shape=(SDS((s, d), BF16), SDS((s, d), BF16), SDS((s, d), F32), SDS((s, d), BF16),
                   SDS((2 * HEADS, ml, xd), F32), SDS((1, d), F32), SDS((1, d), F32)),
        scratch_shapes=[pltpu.VMEM((tm, d), BF16)],
        compiler_params=_arb(1), name="attn_backward")(dx3, dxn3, x2, g_ffn, x1, g, q, kv, w_q, w_o)


def _kv_backward(dkv, memn, mem, g_mem, w_kv):
    ml, d = mem.shape
    xd = w_kv.shape[2]

    def body(dkv_ref, memn_ref, mem_ref, g_ref, w_ref, dw_ref, dg_ref):
        dmemn = jnp.zeros((ml, d), F32)
        for j in range(2 * HEADS):
            dkvb = dkv_ref[j].astype(BF16)
            dw_ref[j] = _dot_tn(memn_ref[...], dkvb)
            dmemn = dmemn + _dot_nt(dkvb, w_ref[j])
        x = mem_ref[...]
        dg_ref[...] = jnp.sum(dmemn * (x * _rstd(x)), axis=0, keepdims=True)

    return _pcall(body, out_shape=(SDS((2 * HEADS, d, xd), F32), SDS((1, d), F32)), name="kv_backward")(dkv, memn, mem, g_mem, w_kv)


def _mix_backward(dx1, x, g_mix, h, lng, lnb, w_sp, bt, conv_w, ga, gb, w_out, w_in, tm, carry=()):
    s, d = x.shape
    n_in = h.shape[1]
    aw = lng.shape[1]
    bw = d - aw
    hd = aw // HEADS
    in_a = 2 * aw
    hb_blocks = tm // HALO
    last_blk = s // HALO - 1
    nt = s // tm
    tc = tm
    te = tc + HALO
    tee = tc + 2 * HALO

    def body(dx1_ref, dx1n_ref, x_ref, gm_ref, h_ref, hp_ref, hn_ref, lng_ref, lnb_ref, wsp_ref, bt_ref, cw_ref,
             ga_ref, gb_ref, wout_ref, win_ref,
             dh_ref, dx_ref, dga_ref, dgb_ref, dcw_ref, dlng_ref, dlnb_ref, dwsp_ref, dbs_ref, dgm_ref,
             mixed_s, dvln_s):
        i = pl.program_id(0)

        @pl.when(i == 0)
        def _():
            for ref in (dga_ref, dgb_ref, dcw_ref, dlng_ref, dlnb_ref, dwsp_ref, dbs_ref, dgm_ref):
                ref[...] = jnp.zeros_like(ref)

        mask = _tril_mask()
        wm = [(wsp_ref[hh] * mask).astype(BF16) for hh in range(HEADS)]
        cw = cw_ref[...]

        def chain(r0):
            rows = slice(r0, r0 + tc)
            first, last = r0 == 0, r0 + tc == tm
            hv = h_ref[rows, :]
            dx1 = dx1_ref[rows, :]
            dx1n = dx1n_ref[...] if last else dx1_ref[r0 + tc:r0 + tc + HALO, :]
            hp = hp_ref[:, in_a:] if first else h_ref[r0 - HALO:r0, in_a:]
            hn = hn_ref[:, in_a:] if last else h_ref[r0 + tc:r0 + tc + HALO, in_a:]
            dx1e = jnp.concatenate([dx1, dx1n], axis=0).astype(BF16)
            dycat = _dot_nt(dx1e, wout_ref[...])

            hbe = jnp.concatenate([hp, hv[:, in_a:], hn], axis=0)
            row = lax.broadcasted_iota(jnp.int32, (tee, 1), 0)
            zext = hbe[:, bw:2 * bw] * hbe[:, 2 * bw:]
            if first:
                zext = jnp.where((i == 0) & (row < HALO), 0.0, zext)
            z2e, z1e = _conv_taps(zext)
            conv_e = (cw[0:1] * z2e + cw[1:2] * z1e + cw[2:3] * zext)[HALO:]
            gate_b_e = hbe[HALO:, :bw]
            sc_e = gate_b_e * conv_e
            rb = _rstd(sc_e)
            dyb = dycat[:, aw:]
            gdy = dyb * gb_ref[...]
            dsc_e = rb * gdy - sc_e * (rb * rb * rb) * (jnp.sum(gdy * sc_e, axis=-1, keepdims=True) * (1.0 / bw))
            dgb_ref[...] += jnp.sum((dyb * (sc_e * rb))[:tc], axis=0, keepdims=True)
            dconv_e = dsc_e * gate_b_e
            if last:
                dconv_e = jnp.where((i == nt - 1) & (row[:te] >= tc), 0.0, dconv_e)
            dconv = dconv_e[:tc]
            dc1 = pltpu.roll(dconv_e, te - 1, 0)[:tc]
            dc2 = pltpu.roll(dconv_e, te - 2, 0)[:tc]
            dz = cw[2:3] * dconv + cw[1:2] * dc1 + cw[0:1] * dc2
            z = zext[HALO:HALO + tc]
            z1 = z1e[HALO:HALO + tc]
            z2 = z2e[HALO:HALO + tc]
            dcw_ref[0:1, :] += jnp.sum(dconv * z2, axis=0, keepdims=True)
            dcw_ref[1:2, :] += jnp.sum(dconv * z1, axis=0, keepdims=True)
            dcw_ref[2:3, :] += jnp.sum(dconv * z, axis=0, keepdims=True)
            dh_ref[rows, in_a:in_a + bw] = (dsc_e[:tc] * conv_e[:tc]).astype(BF16)
            dh_ref[rows, in_a + bw:in_a + 2 * bw] = (dz * hv[:, in_a + 2 * bw:]).astype(BF16)
            dh_ref[rows, in_a + 2 * bw:] = (dz * hv[:, in_a + bw:in_a + 2 * bw]).astype(BF16)

            ha = hv[:, :in_a]
            mixed_c, dvln_c = mixed_s.at[rows, :], dvln_s.at[rows, :]
            th, u, xhat, rl, vln = _sgu_forward(ha, lng_ref[...], lnb_ref[...], wm, bt_ref[...], mixed_c)
            mixed = mixed_c[...]
            sg = u * mixed
            dsg, dga = _rms_bwd(dycat[:tc, :aw], sg, _rstd(sg), ga_ref[...])
            dga_ref[...] += dga
            du = dsg * mixed
            dmixed = dsg * u
            dmb = dmixed.astype(BF16)
            for n in range(tc // CHUNK):
                blk = slice(n * CHUNK, (n + 1) * CHUNK)
                dbs_ref[...] += dmixed[blk]
                for hh in range(HEADS):
                    cols = slice(hh * hd, (hh + 1) * hd)
                    dvln_c[blk, cols] = _dot_tn(wm[hh], dmb[blk, cols])
                    dwsp_ref[hh] += mask * _dot_nt(dmb[blk, cols], vln[blk, cols])
            dvln = dvln_c[...]
            dlng_ref[...] += jnp.sum(dvln * xhat, axis=0, keepdims=True)
            dlnb_ref[...] += jnp.sum(dvln, axis=0, keepdims=True)
            dxh = dvln * lng_ref[...]
            dv = rl * (dxh - jnp.mean(dxh, axis=-1, keepdims=True) - xhat * jnp.mean(dxh * xhat, axis=-1, keepdims=True))
            dh_ref[rows, :in_a] = (jnp.concatenate([du, dv], axis=-1) * _gelu_grad(ha, th)).astype(BF16)

            dxn = _dot(dh_ref[rows, :], win_ref[...])
            xv = x_ref[rows, :]
            dx, dgm = _rms_bwd(dxn, xv, _rstd(xv), gm_ref[...])
            dgm_ref[...] += dgm
            dx_ref[rows, :] = dx1 + dx

        for r0 in range(0, tm, tc):
            chain(r0)

    full = lambda shape: pl.BlockSpec(shape, lambda i: (0,) * len(shape))
    tok = pl.BlockSpec((tm, d), lambda i: (i, 0))
    nxt = lambda i: (jnp.minimum((i + 1) * hb_blocks, last_blk), 0)
    prv = lambda i: (jnp.maximum(i * hb_blocks - 1, 0), 0)
    return _hosted(
        body, carry, grid=(nt,),
        in_specs=[tok, pl.BlockSpec((HALO, d), nxt), tok, full((1, d)),
                  pl.BlockSpec((tm, n_in), lambda i: (i, 0)), pl.BlockSpec((HALO, n_in), prv), pl.BlockSpec((HALO, n_in), nxt),
                  full((1, aw)), full((1, aw)), full((HEADS, CHUNK, CHUNK)), full((CHUNK, HEADS)), full((3, bw)),
                  full((1, aw)), full((1, bw)), full((d, d)), full((n_in, d))],
        out_specs=[pl.BlockSpec((tm, n_in), lambda i: (i, 0)), tok,
                   full((1, aw)), full((1, bw)), full((SUB, bw)), full((1, aw)), full((1, aw)),
                   full((HEADS, CHUNK, CHUNK)), full((CHUNK, aw)), full((1, d))],
        out_shape=(SDS((s, n_in), BF16), SDS((s, d), F32),
                   SDS((1, aw), F32), SDS((1, bw), F32), SDS((SUB, bw), F32), SDS((1, aw), F32), SDS((1, aw), F32),
                   SDS((HEADS, CHUNK, CHUNK), F32), SDS((CHUNK, aw), F32), SDS((1, d), F32)),
        scratch_shapes=[pltpu.VMEM((tm, aw), F32), pltpu.VMEM((tm, aw), F32)],
        compiler_params=_arb(1), name="mix_backward")(dx1, dx1, x, g_mix, h, h, h, lng, lnb, w_sp, bt, conv_w, ga, gb, w_out, w_in)


def _bias_grad(dbs):
    aw = dbs.shape[1]
    hd = aw // HEADS

    def body(dbs_ref, out_ref):
        ones = jnp.ones((SUB, hd), F32)
        for hh in range(HEADS):
            r = lax.dot_general(ones, dbs_ref[:, hh * hd:(hh + 1) * hd], (((1,), (1,)), ((), ())),
                                precision=lax.Precision.HIGHEST, preferred_element_type=F32)
            out_ref[hh:hh + 1, :] = r[0:1]

    return _pcall(body, out_shape=SDS((HEADS, CHUNK), F32), name="bias_grad")(dbs)


def _wgrad_body(a_ref, b_ref, o_ref):
    o_ref[...] = _dot_tn(a_ref[...], b_ref[...])


def _wgrad(a, b, name, carry=()):
    k, m = a.shape
    n = b.shape[1]
    tm = _tile(m, 512, LANES)
    tn = _tile(n, 1024, LANES)
    return _hosted(
        functools.partial(_wgrad_body), carry, grid=(m // tm, n // tn),
        in_specs=[pl.BlockSpec((k, tm), lambda i, j: (0, i)), pl.BlockSpec((k, tn), lambda i, j: (0, j))],
        out_specs=pl.BlockSpec((tm, tn), lambda i, j: (i, j)),
        out_shape=SDS((m, n), F32), compiler_params=_arb(2), name=name)(a, b)


def _wgrad_blocked_lhs(a, b, name, carry=()):
    nb, k, t = a.shape
    n = b.shape[1]
    tn = _tile(n, 1024, LANES)
    return _hosted(
        functools.partial(_wgrad_body), carry, grid=(nb, n // tn),
        in_specs=[pl.BlockSpec((None, k, t), lambda i, j: (i, 0, 0)), pl.BlockSpec((k, tn), lambda i, j: (0, j))],
        out_specs=pl.BlockSpec((t, tn), lambda i, j: (i, j)),
        out_shape=SDS((nb * t, n), F32), compiler_params=_arb(2), name=name)(a, b)


def _wgrad_blocked_rhs(a, b, name, carry=()):
    k, m = a.shape
    nb, _, t = b.shape
    tm = _tile(m, 512, LANES)
    return _hosted(
        functools.partial(_wgrad_body), carry, grid=(m // tm, nb),
        in_specs=[pl.BlockSpec((k, tm), lambda i, j: (0, i)), pl.BlockSpec((None, k, t), lambda i, j: (j, 0, 0))],
        out_specs=pl.BlockSpec((None, tm, t), lambda i, j: (j, i, 0)),
        out_shape=SDS((nb, m, t), F32), compiler_params=_arb(2), name=name)(a, b)


def _unblock_cols(wb, name, carry=()):
    nb, r, t = wb.shape
    tr = _tile(r, 256, 16)

    def body(w_ref, o_ref):
        o_ref[...] = jnp.concatenate([w_ref[j].astype(F32) for j in range(nb)], axis=-1).astype(o_ref.dtype)

    return _hosted(
        body, carry, grid=(r // tr,),
        in_specs=[pl.BlockSpec((nb, tr, t), lambda i: (0, i, 0))], out_specs=pl.BlockSpec((tr, nb * t), lambda i: (i, 0)),
        out_shape=SDS((r, nb * t), wb.dtype), compiler_params=_arb(1), name=name)(wb)


def _block_cols(w, nb, name, carry=()):
    r, n = w.shape
    t = n // nb
    tr = _tile(r, 256, 16)

    def body(w_ref, o_ref):
        wv = w_ref[...]
        for j in range(nb):
            o_ref[j] = wv[:, j * t:(j + 1) * t]

    return _hosted(
        body, carry, grid=(r // tr,),
        in_specs=[pl.BlockSpec((tr, n), lambda i: (i, 0))], out_specs=pl.BlockSpec((nb, tr, t), lambda i: (0, i, 0)),
        out_shape=SDS((nb, r, t), w.dtype), compiler_params=_arb(1), name=name)(w)


def _place():
    x, y, c = lax.axis_index("x"), lax.axis_index("y"), lax.axis_index("c")
    return x, y, c, [(1 - x, y), (x, 1 - y), (1 - x, 1 - y)]


def _all_gather(shards):
    n = len(shards)
    slots = 9
    cut = [(s.shape[0] // 32) * 16 for s in shards]

    def build(ins, outs, sems):
        send_sems, recv_sems, local_sems = sems
        x, y, c, _ = _place()
        me, sib, xn, yn, dg = (x, y, c), (x, y, 1 - c), (1 - x, y, c), (x, 1 - y, c), (1 - x, 1 - y, c)
        other = lambda p: (p[0], p[1], 1 - p[2])

        def rows(a, p, part=None):
            ref = outs[a].at[4 * p[0] + 2 * p[1] + p[2]]
            if part is None or cut[a] == 0:
                return ref if part in (None, 0) else None
            return ref.at[pl.ds(0, cut[a])] if part == 0 else ref.at[pl.ds(cut[a], shards[a].shape[0] - cut[a])]

        def copy(a, k, ref, to, src=None):
            if ref is None:
                return None
            return pltpu.make_async_remote_copy(
                src_ref=ref if src is None else src, dst_ref=ref, send_sem=send_sems.at[slots * a + k],
                recv_sem=recv_sems.at[slots * a + k], device_id=to, device_id_type=MESH)

        def real(cps):
            return [cp for cp in cps if cp is not None]

        class Copies:
            own = lambda a: [copy(a, 1, rows(a, me), xn, ins[a]), copy(a, 2, rows(a, me), yn, ins[a]),
                             copy(a, 0, rows(a, me), sib, ins[a])]
            local = lambda a: pltpu.make_async_copy(ins[a], rows(a, me), local_sems.at[a])
            from_x = lambda a: copy(a, 1, rows(a, xn), me)
            from_y = lambda a: copy(a, 2, rows(a, yn), me)
            after_x = lambda a: real([copy(a, 4, rows(a, xn, 1), yn), copy(a, 5, rows(a, xn), sib)])
            after_y = lambda a: real([copy(a, 3, rows(a, yn, 0), xn), copy(a, 6, rows(a, yn), sib)])
            diag_in = lambda a: real([copy(a, 3, rows(a, dg, 0), me), copy(a, 4, rows(a, dg, 1), me)])
            diag_on = lambda a: real([copy(a, 7, rows(a, dg, 0), sib), copy(a, 8, rows(a, dg, 1), sib)])
            from_sib = lambda a: real([copy(a, 0, rows(a, sib), me), copy(a, 5, rows(a, other(xn)), me),
                                       copy(a, 6, rows(a, other(yn)), me), copy(a, 7, rows(a, other(dg), 0), me),
                                       copy(a, 8, rows(a, other(dg), 1), me)])

        return Copies

    def start(ins, outs, sems):
        cps = build(ins, outs, sems)
        for a in range(n):
            for cp in cps.own(a):
                cp.start()
        for a in range(n):
            cps.local(a).start()

    def relay(ins, outs, sems):
        cps = build(ins, outs, sems)
        for a in range(n):
            cps.from_x(a).wait_recv()
            for cp in cps.after_x(a):
                cp.start()
            cps.from_y(a).wait_recv()
            for cp in cps.after_y(a):
                cp.start()

    def finish(ins, outs, sems):
        cps = build(ins, outs, sems)
        for a in range(n):
            for arrived, onward in zip(cps.diag_in(a), cps.diag_on(a)):
                arrived.wait_recv()
                onward.start()
        for a in range(n):
            for cp in cps.from_sib(a):
                cp.wait_recv()
            for cp in cps.own(a) + cps.after_x(a) + cps.after_y(a) + cps.diag_on(a):
                cp.wait_send()
            cps.local(a).wait()

    return _Exchange(shards, [SDS((N_DEV,) + s.shape, s.dtype) for s in shards],
                     [pltpu.SemaphoreType.DMA((slots * n,)), pltpu.SemaphoreType.DMA((slots * n,)),
                      pltpu.SemaphoreType.DMA((n,))], start, finish, relay)


def _swap_exchange(ins, out_shape, per, copies):
    def start(i, o, sems):
        for cp in copies(i, o, sems):
            cp.start()

    def finish(i, o, sems):
        for cp in copies(i, o, sems):
            cp.wait()

    n = per * len(ins)
    return _Exchange(ins, out_shape, [pltpu.SemaphoreType.DMA((n,)), pltpu.SemaphoreType.DMA((n,))], start, finish)


def _exchange_c(gs):
    def copies(ins, outs, sems):
        x, y, c, _ = _place()
        return [pltpu.make_async_remote_copy(
                    src_ref=ins[a].at[2 * k + 1 - c], dst_ref=outs[a].at[k],
                    send_sem=sems[0].at[4 * a + k], recv_sem=sems[1].at[4 * a + k],
                    device_id=(x, y, 1 - c), device_id_type=MESH)
                for a in range(len(gs)) for k in range(4)]

    return _swap_exchange(gs, [SDS((4,) + g.shape[1:], g.dtype) for g in gs], 4, copies)


def _exchange_xy(sends):
    def copies(ins, outs, sems):
        x, y, c, chips = _place()
        return [pltpu.make_async_remote_copy(
                    src_ref=ins[a].at[t], dst_ref=outs[a].at[t],
                    send_sem=sems[0].at[3 * a + t], recv_sem=sems[1].at[3 * a + t],
                    device_id=(*chips[t], c), device_id_type=MESH)
                for a in range(len(sends)) for t in range(3)]

    return _swap_exchange(sends, [SDS(s.shape, s.dtype) for s in sends], 3, copies)


def _rs_combine(g, recv, pos, name, carry=()):
    _, r, cdim = g.shape
    tr = _tile(r, 256, 16)

    def body(pos_ref, g0, r0, g1, r1, g2, r2, g3, r3, keep_ref, send_ref):
        keep_ref[...] = g0[...] + r0[...]
        send_ref[0] = (g1[...] + r1[...]).astype(BF16)
        send_ref[1] = (g2[...] + r2[...]).astype(BF16)
        send_ref[2] = (g3[...] + r3[...]).astype(BF16)

    def k_of(p, t):
        px = p[0] if t in (0, 2) else 1 - p[0]
        py = p[1] if t in (0, 1) else 1 - p[1]
        return 2 * px + py

    blk = (None, tr, cdim)
    in_specs = []
    for t in range(4):
        in_specs.append(pl.BlockSpec(blk, functools.partial(lambda j, p, t: (2 * k_of(p, t) + p[2], j, 0), t=t)))
        in_specs.append(pl.BlockSpec(blk, functools.partial(lambda j, p, t: (k_of(p, t), j, 0), t=t)))
    return _hosted(
        body, carry, n_prefetch=1, out_shape=(SDS((r, cdim), F32), SDS((3, r, cdim), BF16)),
        grid=(r // tr,), in_specs=in_specs,
        out_specs=[pl.BlockSpec((tr, cdim), lambda j, p: (j, 0)), pl.BlockSpec((3, tr, cdim), lambda j, p: (0, j, 0))],
        compiler_params=_arb(1), name=name)(pos, g, recv, g, recv, g, recv, g, recv)


def _adamw_shard(keep, recv, w, m, v, name):
    r, cdim = w.shape
    tr = _tile(r, 256, 16)

    def body(k_ref, r_ref, w_ref, m_ref, v_ref, g_ref, d_ref, nm_ref, nv_ref):
        g = ((k_ref[...] + r_ref[0].astype(F32)) + r_ref[1].astype(F32)) + r_ref[2].astype(F32)
        g_ref[...] = g
        d_ref[...], nm_ref[...], nv_ref[...] = _adamw(w_ref[...], g, m_ref[...], v_ref[...])

    blk = pl.BlockSpec((tr, cdim), lambda j: (j, 0))
    out = SDS((r, cdim), F32)
    return _pcall(body, grid=(r // tr,), in_specs=[blk, pl.BlockSpec((3, tr, cdim), lambda j: (0, j, 0)), blk, blk, blk],
                  out_specs=[blk] * 4, out_shape=(out,) * 4, compiler_params=_arb(1), name=name)(keep, recv, w, m, v)


_HBM = pl.BlockSpec(memory_space=pltpu.HBM)
_SEM = pl.BlockSpec(memory_space=pltpu.SEMAPHORE)
_SPLIT = pltpu.CompilerParams(has_side_effects=pltpu.SideEffectType.DATAFLOW_SIDE_EFFECTING)


def _split_copies(kind, n, refs):
    srcs, lands, (send_sems, recv_sems) = refs[:n], refs[n:2 * n], refs[2 * n:2 * n + 2]
    x, y, c, chips = _place()
    per = _SPLIT_COPIES[kind]
    if kind == "xy":
        ends = lambda a, t: (srcs[a].at[t], lands[a].at[t], (*chips[t], c))
    else:
        ends = lambda a, k: (srcs[a].at[2 * k + 1 - c], lands[a].at[k], (x, y, 1 - c))
    cps = []
    for a in range(n):
        for t in range(per):
            src, dst, to = ends(a, t)
            cps.append(pltpu.make_async_remote_copy(src_ref=src, dst_ref=dst, send_sem=send_sems.at[per * a + t],
                                                    recv_sem=recv_sems.at[per * a + t], device_id=to, device_id_type=MESH))
    return cps


_SPLIT_COPIES = {"xy": 3, "c": 4}


def _exchange_start(kind, arrays, name, after=None):
    n = len(arrays)
    order = [] if after is None else [after]

    def body(*refs):
        refs = refs[:2 * n] + refs[2 * n + len(order):]
        for cp in _split_copies(kind, n, refs):
            cp.start()
        refs[-1][...] = jnp.zeros_like(refs[-1])

    hbm = lambda a: pltpu.with_memory_space_constraint(a, pltpu.HBM)
    land = [a.shape if kind == "xy" else (4,) + a.shape[1:] for a in arrays]
    bufs = [pltpu.HBM(a.shape, a.dtype) for a in arrays] + [pltpu.HBM(s, a.dtype) for s, a in zip(land, arrays)]
    sems = pltpu.SemaphoreType.DMA((_SPLIT_COPIES[kind] * n,))
    res = _pcall(
        body, name=name, out_shape=(sems, sems, *bufs, SDS((SUB, LANES), F32)),
        in_specs=[_HBM] * (2 * n) + _hbm_specs(len(order)),
        out_specs=[_SEM, _SEM] + [_HBM] * (2 * n) + [pl.BlockSpec(memory_space=pltpu.VMEM)],
        input_output_aliases={k: 2 + k for k in range(2 * n)}, compiler_params=_SPLIT)(
            *[hbm(a) for a in arrays], *[hbm(lax.empty(s, a.dtype)) for s, a in zip(land, arrays)], *order)
    return (kind, n, res[:-1]), res[-1]


def _exchange_wait(started, after, name, sources=False):
    kind, n, (send_sems, recv_sems, *bufs) = started

    def body(*refs):
        for cp in _split_copies(kind, n, refs):
            cp.wait_send()
            cp.wait_recv()

    shapes = [pltpu.HBM(b.shape, b.dtype) for b in bufs]
    res = _pcall(
        body, name=name, out_shape=tuple(shapes),
        in_specs=[_HBM] * (2 * n) + [_SEM, _SEM, pl.BlockSpec(memory_space=pl.ANY)], out_specs=[_HBM] * (2 * n),
        input_output_aliases={k: k for k in range(2 * n)}, compiler_params=_SPLIT)(*bufs, send_sems, recv_sems, after)
    return (list(res[:n]), list(res[n:])) if sources else list(res[n:])


def _follow(token):
    nothing = lambda ins, outs, sems: None
    return _Exchange([token], [], [], nothing, nothing)


def _adamw_small(gathered, seg, params, conv_rows):
    names = list(params)
    c0, cn = conv_rows

    def body(*refs):
        gat_ref = refs[0]
        ins = refs[1:1 + 3 * len(names)]
        outs = refs[1 + 3 * len(names):]

        def total(r0, rn):
            tot = gat_ref[0, r0:r0 + rn, :]
            for dev in range(1, N_DEV):
                tot = tot + gat_ref[dev, r0:r0 + rn, :]
            return tot

        for k, nm in enumerate(names):
            g = total(*seg[nm])
            w_ref, m_ref, v_ref = ins[3 * k:3 * k + 3]
            g_ref, d_ref, nm_ref, nv_ref = outs[4 * k:4 * k + 4]
            g_ref[...] = g
            d_ref[...], nm_ref[...], nv_ref[...] = _adamw(w_ref[...], g, m_ref[...], v_ref[...])
        outs[-2][...] = total(c0, cn)
        outs[-1][...] = total(*seg["loss"])

    flat_in = [a for nm in names for a in params[nm]]
    out_shape = []
    for nm in names:
        out_shape += [SDS(params[nm][0].shape, F32)] * 4
    out_shape += [SDS((cn, LANES), F32), SDS((seg["loss"][1], LANES), F32)]
    res = _pcall(body, out_shape=tuple(out_shape), name="adamw_small")(gathered, *flat_in)
    per = {nm: res[4 * k:4 * k + 4] for k, nm in enumerate(names)}
    return per, res[-2], res[-1]


def _adamw_one(w, g, m, v, name):
    def body(w_ref, g_ref, m_ref, v_ref, d_ref, nm_ref, nv_ref):
        d_ref[...], nm_ref[...], nv_ref[...] = _adamw(w_ref[...], g_ref[...], m_ref[...], v_ref[...])

    return _pcall(body, out_shape=(SDS(w.shape, F32),) * 3, name=name)(w, g, m, v)


def _rows128(a):
    return a.reshape(-1, LANES)


def _pack_small(gs, loss_tile):
    seg, pieces, row = {}, [], 0
    for nm in SMALL + ("conv_w", "loss"):
        piece = loss_tile if nm == "loss" else _rows128(gs[nm])
        rn = _round_up(piece.shape[0], SUB)
        pieces.append(jnp.pad(piece, ((0, rn - piece.shape[0]), (0, 0))))
        seg[nm] = (row, piece.shape[0])
        row += rn
    return jnp.concatenate(pieces, axis=0), seg


def _step(x, mem, target, wb, conv_w, sp, pos):
    s, d = x.shape
    tm = min(TOKEN_TILE, s)
    tm_wide = min(2 * TOKEN_TILE, s)
    rows = lambda w8: w8.reshape(-1, w8.shape[2])
    shards = lambda g: g.reshape((N_DEV, -1) + g.shape[1:])
    bt = sp["b_spatial"].T

    (w_in8, conv8), = _run_exchanges([_all_gather([wb["w_in"], conv_w])], "gather_w_in")
    conv_full = conv8.transpose(1, 0, 2).reshape(3, -1)
    w_in_t = rows(w_in8)
    (xn1, h), ((w_out8, w_kv8, w_q8),) = _in_forward(
        x, sp["ln_mix_g"], w_in_t, tm, carry=[_all_gather([wb["w_out"], wb["w_kv"], wb["w_q"]])])
    w_out = rows(w_out8)
    (ycat, x1), ((w_o8, w_down8),) = _mix_forward(
        h, x, sp["sgu_ln_g"], sp["sgu_ln_b"], sp["w_spatial"], bt, conv_full, sp["grp_norm_a"], sp["grp_norm_b"], w_out, tm,
        carry=[_all_gather([wb["w_o"], wb["w_down"]])])
    w_q, w_o, w_down = rows(w_q8), rows(w_o8), rows(w_down8)
    memn, kv = _kv_forward(mem, sp["ln_mem_g"], w_kv8)
    (xn2, q, o, x2), ((w_gu8,),) = _attn_forward(
        x1, sp["ln_attn_g"], w_q, kv, w_o, tm, carry=[_all_gather([wb["w_gate_up"]])])
    w_gu = w_gu8.reshape((2, N_DEV // 2) + w_gu8.shape[1:])
    xn3, gu, x3 = _ffn_forward(x2, sp["ln_ffn_g"], w_gu, w_down, tm_wide)

    loss, d_lnf, dx3, dx3b = _final_backward(x3, target, sp["ln_final_g"], tm_wide)
    act, dgu, dxn3 = _swiglu_backward(dx3b, gu, w_gu, w_down, tm_wide)
    g_gu, _ = _wgrad_blocked_lhs(dgu.reshape((N_DEV,) + dgu.shape[2:]), xn3, "wgrad_gate_up")
    g_gu = shards(g_gu)
    g_down, ((rc_gu,),) = _wgrad_blocked_lhs(act, dx3b, "wgrad_down", carry=[_exchange_c([g_gu])])
    g_down = shards(g_down)
    keep, pending = {}, []
    (keep["w_gate_up"], send_gu), _ = _rs_combine(g_gu, rc_gu, pos, "rs_combine_w_gate_up")
    started, token = _exchange_start("xy", [send_gu], "exchange_xy_1_start")
    pending.append((("w_gate_up",), started))
    c_down, token = _exchange_start("c", [g_down], "exchange_c_1_start", after=token)
    (dx2b, dq, dx1, dx1b, dkv, d_lnattn, d_lnffn), _ = _attn_backward(
        dx3, dxn3, x2, sp["ln_ffn_g"], x1, sp["ln_attn_g"], q, kv, w_q, w_o, tm, carry=[_follow(token)])
    (g_down,), (rc_down,) = _exchange_wait(c_down, dx1b, "exchange_c_1_wait", sources=True)
    (keep["w_down"], send_down), _ = _rs_combine(g_down, rc_down, pos, "rs_combine_w_down")
    g_o, _ = _wgrad(o, dx2b, "wgrad_o")
    g_q, _ = _wgrad(xn2, dq, "wgrad_q")
    g_o, g_q = shards(g_o), shards(g_q)
    g_kv, d_lnmem = _kv_backward(dkv, memn, mem, sp["ln_mem_g"], w_kv8)
    c_oqkv, token = _exchange_start("c", [g_o, g_q, g_kv], "exchange_c_2_start")
    g_out, _ = _wgrad(ycat, dx1b, "wgrad_out", carry=[_follow(token)])
    g_out = shards(g_out)
    (g_o, g_q, g_kv), (rc_o, rc_q, rc_kv) = _exchange_wait(c_oqkv, g_out, "exchange_c_2_wait", sources=True)
    c_out, token = _exchange_start("c", [g_out], "exchange_c_3_start")
    (keep["w_o"], send_o), _ = _rs_combine(g_o, rc_o, pos, "rs_combine_w_o", carry=[_follow(token)])
    (keep["w_q"], send_q), _ = _rs_combine(g_q, rc_q, pos, "rs_combine_w_q")
    (keep["w_kv"], send_kv), _ = _rs_combine(g_kv, rc_kv, pos, "rs_combine_w_kv")
    (g_out,), (rc_out,) = _exchange_wait(c_out, send_kv, "exchange_c_3_wait", sources=True)
    (keep["w_out"], send_out), _ = _rs_combine(g_out, rc_out, pos, "rs_combine_w_out")
    started, token = _exchange_start("xy", [send_down, send_o, send_q, send_out, send_kv], "exchange_xy_2_start")
    pending.append((("w_down", "w_o", "w_q", "w_out", "w_kv"), started))
    (dh, dx, d_ga, d_gb, d_cw, d_lng, d_lnb, d_wsp, d_bs, d_lnmix), _ = _mix_backward(
        dx1, x, sp["ln_mix_g"], h, sp["sgu_ln_g"], sp["sgu_ln_b"], sp["w_spatial"], bt, conv_full,
        sp["grp_norm_a"], sp["grp_norm_b"], w_out, w_in_t, tm, carry=[_follow(token)])
    gs = {"ln_mix_g": d_lnmix, "sgu_ln_g": d_lng, "sgu_ln_b": d_lnb, "w_spatial": d_wsp, "b_spatial": _bias_grad(d_bs),
          "conv_w": d_cw[:3], "grp_norm_a": d_ga, "grp_norm_b": d_gb, "ln_attn_g": d_lnattn, "ln_mem_g": d_lnmem,
          "ln_ffn_g": d_lnffn, "ln_final_g": d_lnf}
    packed, seg = _pack_small(gs, loss)
    g_in, (_, (small_all,)) = _wgrad(dh, xn1, "wgrad_in", carry=[_follow(token), _all_gather([packed])])
    g_in = shards(g_in)
    c_in, token = _exchange_start("c", [g_in], "exchange_c_4_start")
    return dx, keep, pending, (g_in, c_in), token, small_all, seg


def kernel(x, mem, ln_mix_g, w_in, sgu_ln_g, sgu_ln_b, w_spatial, b_spatial, conv_w, grp_norm_a, grp_norm_b, w_out, ln_attn_g, ln_mem_g, w_q, w_kv, w_o, ln_ffn_g, w_gate_up, w_down, ln_final_g, loss_target, m_ln_mix_g, m_w_in, m_sgu_ln_g, m_sgu_ln_b, m_w_spatial, m_b_spatial, m_conv_w, m_grp_norm_a, m_grp_norm_b, m_w_out, m_ln_attn_g, m_ln_mem_g, m_w_q, m_w_kv, m_w_o, m_ln_ffn_g, m_w_gate_up, m_w_down, m_ln_final_g, v_ln_mix_g, v_w_in, v_sgu_ln_g, v_sgu_ln_b, v_w_spatial, v_b_spatial, v_conv_w, v_grp_norm_a, v_grp_norm_b, v_w_out, v_ln_attn_g, v_ln_mem_g, v_w_q, v_w_kv, v_w_o, v_ln_ffn_g, v_w_gate_up, v_w_down, v_ln_final_g):
    order = ["ln_mix_g", "w_in", "sgu_ln_g", "sgu_ln_b", "w_spatial", "b_spatial", "conv_w", "grp_norm_a", "grp_norm_b",
             "w_out", "ln_attn_g", "ln_mem_g", "w_q", "w_kv", "w_o", "ln_ffn_g", "w_gate_up", "w_down", "ln_final_g"]
    W = dict(ln_mix_g=ln_mix_g, w_in=w_in, sgu_ln_g=sgu_ln_g, sgu_ln_b=sgu_ln_b, w_spatial=w_spatial, b_spatial=b_spatial,
             conv_w=conv_w, grp_norm_a=grp_norm_a, grp_norm_b=grp_norm_b, w_out=w_out, ln_attn_g=ln_attn_g,
             ln_mem_g=ln_mem_g, w_q=w_q, w_kv=w_kv, w_o=w_o, ln_ffn_g=ln_ffn_g, w_gate_up=w_gate_up, w_down=w_down,
             ln_final_g=ln_final_g)
    M = dict(ln_mix_g=m_ln_mix_g, w_in=m_w_in, sgu_ln_g=m_sgu_ln_g, sgu_ln_b=m_sgu_ln_b, w_spatial=m_w_spatial,
             b_spatial=m_b_spatial, conv_w=m_conv_w, grp_norm_a=m_grp_norm_a, grp_norm_b=m_grp_norm_b, w_out=m_w_out,
             ln_attn_g=m_ln_attn_g, ln_mem_g=m_ln_mem_g, w_q=m_w_q, w_kv=m_w_kv, w_o=m_w_o, ln_ffn_g=m_ln_ffn_g,
             w_gate_up=m_w_gate_up, w_down=m_w_down, ln_final_g=m_ln_final_g)
    V = dict(ln_mix_g=v_ln_mix_g, w_in=v_w_in, sgu_ln_g=v_sgu_ln_g, sgu_ln_b=v_sgu_ln_b, w_spatial=v_w_spatial,
             b_spatial=v_b_spatial, conv_w=v_conv_w, grp_norm_a=v_grp_norm_a, grp_norm_b=v_grp_norm_b, w_out=v_w_out,
             ln_attn_g=v_ln_attn_g, ln_mem_g=v_ln_mem_g, w_q=v_w_q, w_kv=v_w_kv, w_o=v_w_o, ln_ffn_g=v_ln_ffn_g,
             w_gate_up=v_w_gate_up, w_down=v_w_down, ln_final_g=v_ln_final_g)

    bw = conv_w.shape[1] * N_DEV
    pos = jnp.stack([lax.axis_index("x"), lax.axis_index("y"), lax.axis_index("c")]).astype(jnp.int32)
    me = 4 * pos[0] + 2 * pos[1] + pos[2]

    sp = {nm: (W[nm].reshape(1, -1) if W[nm].ndim == 1 else W[nm]) for nm in SMALL}
    view = lambda a, nm: a.T if nm in TRANSPOSED else a
    wb = {nm: view(W[nm], nm).astype(BF16) for nm in BIG}
    grad_x, keep, pending, (g_in, c_in), token, small_all, seg = _step(
        x[0], mem[0], loss_target[0], wb, conv_w, sp, pos)

    out = {}

    def update(k, names, started, token):
        landed = _exchange_wait(started, token, "exchange_xy_%d_wait" % k)
        for nm, rxy in zip(names, landed):
            res = _adamw_shard(keep[nm], rxy, view(W[nm], nm), view(M[nm], nm), view(V[nm], nm), "adamw_" + nm)
            out[nm] = tuple(view(a, nm) for a in res)
            token = res[0]
        return token

    token = update(1, *pending[0], token)
    (g_in,), (rc_in,) = _exchange_wait(c_in, token, "exchange_c_4_wait", sources=True)
    (keep["w_in"], send_in), _ = _rs_combine(g_in, rc_in, pos, "rs_combine_w_in")
    xy_in, token = _exchange_start("xy", [send_in], "exchange_xy_3_start")
    token = update(2, *pending[1], token)

    params = {nm: (_rows128(W[nm]), _rows128(M[nm]), _rows128(V[nm])) for nm in SMALL}
    per, conv_g_rows, loss_sum = _adamw_small(small_all, seg, params, seg["conv_w"])
    for nm in SMALL:
        out[nm] = tuple(a.reshape(W[nm].shape) for a in per[nm])
    conv_g = lax.dynamic_slice_in_dim(conv_g_rows.reshape(3, bw), me * conv_w.shape[1], conv_w.shape[1], axis=1)
    out["conv_w"] = (conv_g,) + tuple(_adamw_one(conv_w, conv_g, m_conv_w, v_conv_w, "adamw_conv"))

    update(3, ("w_in",), xy_in, token[:1, :1] + out["conv_w"][1][:1, :1])

    loss = loss_sum[0, 0]
    res = [loss, grad_x[None]]
    for k in range(4):
        res += [out[nm][k] for nm in order]
    return tuple(res)
```

```python
import functools

import jax
import jax.numpy as jnp
from jax import lax
from jax.experimental import pallas as pl
from jax.experimental.pallas import tpu as pltpu

F32 = jnp.float32
BF16 = jnp.bfloat16
SDS = jax.ShapeDtypeStruct
MESH = pl.DeviceIdType.MESH

EPS = 1e-6
N_DEV = 8
HEADS = 4
CHUNK = 128
HALO = 16
SUB = 8
LANES = 128
TOKEN_TILE = 512
ROW_CHUNK = 256
RELAY_AT = 0.7

ADAM_LR = 0.001
ADAM_B1 = 0.9
ADAM_B2 = 0.999
ADAM_EPS = 1e-08
ADAM_WD = 0.01
ADAM_STEP = 10

BIG = ("w_in", "w_out", "w_q", "w_kv", "w_o", "w_gate_up", "w_down")
TRANSPOSED = ("w_in", "w_gate_up")
SMALL = ("ln_mix_g", "sgu_ln_g", "sgu_ln_b", "w_spatial", "b_spatial", "grp_norm_a", "grp_norm_b",
         "ln_attn_g", "ln_mem_g", "ln_ffn_g", "ln_final_g")


class _Exchange:
    def __init__(self, ins, out_shape, sems, start, finish, relay=None):
        self.ins, self.out_shape, self.sems = list(ins), list(out_shape), list(sems)
        self.start, self.finish, self.relay = start, finish, relay


def _pcall(body, carry=(), n_prefetch=0, **kw):
    if carry:
        return functools.partial(_carrying_call, body, tuple(carry), n_prefetch, kw)
    if n_prefetch:
        kw["grid_spec"] = pltpu.PrefetchScalarGridSpec(
            num_scalar_prefetch=n_prefetch, grid=kw.pop("grid"), in_specs=kw.pop("in_specs"),
            out_specs=kw.pop("out_specs"), scratch_shapes=kw.pop("scratch_shapes", ()))
    return pl.pallas_call(body, **kw)


def _carrying_call(body, carry, n_prefetch, kw, *args):
    kw = dict(kw)
    out_shape = kw.pop("out_shape")
    single = not isinstance(out_shape, (tuple, list))
    outs_shape = (out_shape,) if single else tuple(out_shape)
    out_specs = kw.pop("out_specs")
    out_specs = [out_specs] if single else list(out_specs)
    in_specs = list(kw.pop("in_specs"))
    scratch = list(kw.pop("scratch_shapes", ()))
    grid = tuple(kw.get("grid", ()))
    n_in, n_out, n_scr = len(args), len(outs_shape), len(scratch)

    def split(refs, k, counts):
        parts = []
        for cnt in counts:
            parts.append(refs[k:k + cnt])
            k += cnt
        return parts, k

    def wrapped(*refs):
        cins, k = split(refs, n_in, [len(p.ins) for p in carry])
        outs = refs[k:k + n_out]
        couts, k = split(refs, k + n_out, [len(p.out_shape) for p in carry])
        scr = refs[k:k + n_scr]
        csems, _ = split(refs, k + n_scr, [len(p.sems) for p in carry])
        first, last = True, True
        for a, g in enumerate(grid):
            first = (pl.program_id(a) == 0) & first
            last = (pl.program_id(a) == g - 1) & last

        def start_all():
            for p, ci, co, cs in zip(carry, cins, couts, csems):
                p.start(ci, co, cs)

        def relay_all():
            for p, ci, co, cs in zip(carry, cins, couts, csems):
                if p.relay is not None:
                    p.relay(ci, co, cs)

        def finish_all():
            for p, ci, co, cs in zip(carry, cins, couts, csems):
                p.finish(ci, co, cs)

        if len(grid) == 1:
            relay_now = pl.program_id(0) == min(int(RELAY_AT * grid[0]), grid[0] - 1)
        else:
            relay_now = last
        start_all() if not grid else pl.when(first)(start_all)
        relay_all() if not grid else pl.when(relay_now)(relay_all)
        body(*refs[:n_in], *outs, *scr)
        finish_all() if not grid else pl.when(last)(finish_all)

    c_in = [a for p in carry for a in p.ins]
    c_out = [s for p in carry for s in p.out_shape]
    c_sems = [s for p in carry for s in p.sems]
    res = _pcall(wrapped, n_prefetch=n_prefetch, out_shape=outs_shape + tuple(c_out),
                 in_specs=in_specs + _hbm_specs(len(c_in)), out_specs=out_specs + _hbm_specs(len(c_out)),
                 scratch_shapes=scratch + c_sems, **kw)(*args, *c_in)
    own = res[0] if single else tuple(res[:n_out])
    landed, k = [], n_out
    for p in carry:
        landed.append(list(res[k:k + len(p.out_shape)]))
        k += len(p.out_shape)
    return own, landed


def _hbm_specs(n):
    return [pl.BlockSpec(memory_space=pl.ANY)] * n


def _hosted(body, carry, **kw):
    if carry:
        return _pcall(body, carry=carry, **kw)
    call = _pcall(body, **kw)
    return lambda *args: (call(*args), [])


def _run_exchanges(parts, name):
    def body(*refs):
        pass

    _, landed = _pcall(body, carry=parts, out_shape=(), in_specs=[], out_specs=[], name=name)()
    return landed


def _arb(n):
    return pltpu.CompilerParams(dimension_semantics=("arbitrary",) * n)


def _tile(n, target, mult):
    best = None
    for t in range(mult, min(n, target) + 1, mult):
        if n % t == 0:
            best = t
    return n if best is None else best


def _round_up(n, m):
    return (n + m - 1) // m * m


def _dot(a, b):
    return jnp.dot(a, b, preferred_element_type=F32)


def _dot_nt(a, b):
    return lax.dot_general(a, b, (((1,), (1,)), ((), ())), preferred_element_type=F32)


def _dot_tn(a, b):
    return lax.dot_general(a, b, (((0,), (0,)), ((), ())), preferred_element_type=F32)


def _rstd(x):
    return lax.rsqrt(jnp.mean(x * x, axis=-1, keepdims=True) + EPS)


def _rms_bwd(dy, x, r, g):
    gdy = dy * g
    proj = jnp.sum(gdy * x, axis=-1, keepdims=True) * (1.0 / x.shape[-1])
    dx = r * gdy - x * (r * r * r) * proj
    dg = jnp.sum(dy * (x * r), axis=0, keepdims=True)
    return dx, dg


_GELU_C = 0.7978845608028654
_GELU_A = 0.044715


def _gelu(x):
    t = jnp.tanh(_GELU_C * (x + _GELU_A * x * x * x))
    return 0.5 * x * (1.0 + t), t


def _gelu_grad(x, t):
    return 0.5 * (1.0 + t) + 0.5 * x * (1.0 - t * t) * (_GELU_C * (1.0 + 3.0 * _GELU_A * x * x))


def _sigmoid(x):
    return 1.0 / (1.0 + jnp.exp(-x))


def _softmax(s):
    m = jnp.max(s, axis=-1, keepdims=True)
    e = jnp.exp(s - m)
    return e / jnp.sum(e, axis=-1, keepdims=True)


def _adamw(w, g, m, v):
    m = ADAM_B1 * m + (1.0 - ADAM_B1) * g
    v = ADAM_B2 * v + (1.0 - ADAM_B2) * (g * g)
    m_hat = m / (1.0 - ADAM_B1 ** ADAM_STEP)
    v_hat = v / (1.0 - ADAM_B2 ** ADAM_STEP)
    delta = -ADAM_LR * (m_hat / (jnp.sqrt(v_hat) + ADAM_EPS) + ADAM_WD * w)
    return delta, m, v


def _tril_mask():
    t = lax.broadcasted_iota(jnp.int32, (CHUNK, CHUNK), 0)
    s = lax.broadcasted_iota(jnp.int32, (CHUNK, CHUNK), 1)
    return (s <= t).astype(F32)


def _sgu_forward(ha, lng, lnb, wm, bt, mixed_s):
    aw = ha.shape[1] // 2
    hd = aw // HEADS
    a, th = _gelu(ha)
    u = a[:, :aw]
    v = a[:, aw:]
    mu = jnp.mean(v, axis=-1, keepdims=True)
    vc = v - mu
    rl = lax.rsqrt(jnp.mean(vc * vc, axis=-1, keepdims=True) + EPS)
    xhat = vc * rl
    vln = (xhat * lng + lnb).astype(BF16)
    for n in range(ha.shape[0] // CHUNK):
        rows = slice(n * CHUNK, (n + 1) * CHUNK)
        for h in range(HEADS):
            cols = slice(h * hd, (h + 1) * hd)
            mixed_s[rows, cols] = _dot(wm[h], vln[rows, cols]) + bt[:, h:h + 1]
    return th, u, xhat, rl, vln


def _conv_taps(zext):
    return pltpu.roll(zext, 2, 0), pltpu.roll(zext, 1, 0)


def _kv_forward(mem, g_mem, w_kv):
    ml, d = mem.shape
    xd = w_kv.shape[2]

    def body(mem_ref, g_ref, w_ref, memn_ref, kv_ref):
        x = mem_ref[...]
        memn = (x * _rstd(x) * g_ref[...]).astype(BF16)
        memn_ref[...] = memn
        for j in range(2 * HEADS):
            kv_ref[j] = _dot(memn, w_ref[j]).astype(BF16)

    return _pcall(body, out_shape=(SDS((ml, d), BF16), SDS((2 * HEADS, ml, xd), BF16)), name="kv_forward")(mem, g_mem, w_kv)


def _in_forward(x, g, w_in_t, tm, carry=()):
    s, d = x.shape
    n_in = w_in_t.shape[0]

    def body(x_ref, g_ref, w_ref, xn_ref, h_ref):
        xv = x_ref[...]
        xn = (xv * _rstd(xv) * g_ref[...]).astype(BF16)
        xn_ref[...] = xn
        h_ref[...] = _dot_nt(xn, w_ref[...])

    return _hosted(
        body, carry, grid=(s // tm,),
        in_specs=[pl.BlockSpec((tm, d), lambda i: (i, 0)), pl.BlockSpec((1, d), lambda i: (0, 0)),
                  pl.BlockSpec((n_in, d), lambda i: (0, 0))],
        out_specs=[pl.BlockSpec((tm, d), lambda i: (i, 0)), pl.BlockSpec((tm, n_in), lambda i: (i, 0))],
        out_shape=(SDS((s, d), BF16), SDS((s, n_in), F32)),
        compiler_params=_arb(1), name="in_forward")(x, g, w_in_t)


def _mix_forward(h, x, lng, lnb, w_sp, bt, conv_w, ga, gb, w_out, tm, carry=()):
    s, d = x.shape
    n_in = h.shape[1]
    aw = lng.shape[1]
    bw = d - aw
    in_a = 2 * aw
    hb_blocks = tm // HALO

    def body(h_ref, hprev_ref, x_ref, lng_ref, lnb_ref, wsp_ref, bt_ref, cw_ref, ga_ref, gb_ref, wout_ref,
             ycat_ref, x1_ref, mixed_s):
        i = pl.program_id(0)
        mask = _tril_mask()
        wm = [(wsp_ref[hh] * mask).astype(BF16) for hh in range(HEADS)]
        hv = h_ref[...]
        _, u, _, _, _ = _sgu_forward(hv[:, :in_a], lng_ref[...], lnb_ref[...], wm, bt_ref[...], mixed_s)
        sg = u * mixed_s[...]
        ycat_ref[:, :aw] = (sg * _rstd(sg) * ga_ref[...]).astype(BF16)

        gate_b = hv[:, in_a:in_a + bw]
        z = hv[:, in_a + bw:in_a + 2 * bw] * hv[:, in_a + 2 * bw:]
        hp = hprev_ref[...]
        zp = hp[:, in_a + bw:in_a + 2 * bw] * hp[:, in_a + 2 * bw:]
        zp = jnp.where(i == 0, 0.0, zp)
        zext = jnp.concatenate([zp, z], axis=0)
        z2, z1 = _conv_taps(zext)
        cw = cw_ref[...]
        conv = cw[0:1] * z2[HALO:] + cw[1:2] * z1[HALO:] + cw[2:3] * z
        sc = gate_b * conv
        ycat_ref[:, aw:] = (sc * _rstd(sc) * gb_ref[...]).astype(BF16)
        x1_ref[...] = x_ref[...] + _dot(ycat_ref[...], wout_ref[...])

    full = lambda shape: pl.BlockSpec(shape, lambda i: (0,) * len(shape))
    return _hosted(
        body, carry, grid=(s // tm,),
        in_specs=[pl.BlockSpec((tm, n_in), lambda i: (i, 0)),
                  pl.BlockSpec((HALO, n_in), lambda i: (jnp.maximum(i * hb_blocks - 1, 0), 0)),
                  pl.BlockSpec((tm, d), lambda i: (i, 0)),
                  full((1, aw)), full((1, aw)), full((HEADS, CHUNK, CHUNK)), full((CHUNK, HEADS)),
                  full((3, bw)), full((1, aw)), full((1, bw)), full((d, d))],
        out_specs=[pl.BlockSpec((tm, d), lambda i: (i, 0)), pl.BlockSpec((tm, d), lambda i: (i, 0))],
        out_shape=(SDS((s, d), BF16), SDS((s, d), F32)),
        scratch_shapes=[pltpu.VMEM((tm, aw), F32)],
        compiler_params=_arb(1), name="mix_forward")(h, h, x, lng, lnb, w_sp, bt, conv_w, ga, gb, w_out)


def _attn_forward(x1, g, w_q, kv, w_o, tm, carry=()):
    s, d = x1.shape
    _, ml, xd = kv.shape
    scale = xd ** -0.5

    def body(x1_ref, g_ref, wq_ref, kv_ref, wo_ref, xn_ref, q_ref, p_ref, o_ref, x2_ref):
        xv = x1_ref[...]
        xn = (xv * _rstd(xv) * g_ref[...]).astype(BF16)
        xn_ref[...] = xn
        q_ref[...] = _dot(xn, wq_ref[...]).astype(BF16)
        for hh in range(HEADS):
            cols = slice(hh * xd, (hh + 1) * xd)
            p = _softmax(_dot_nt(q_ref[:, cols], kv_ref[hh]) * scale).astype(BF16)
            p_ref[:, hh * ml:(hh + 1) * ml] = p
            o_ref[:, cols] = _dot(p, kv_ref[HEADS + hh]).astype(BF16)
        x2_ref[...] = xv + _dot(o_ref[...], wo_ref[...])

    tok = pl.BlockSpec((tm, d), lambda i: (i, 0))
    probs = pl.BlockSpec((tm, HEADS * ml), lambda i: (i, 0))
    return _hosted(
        body, carry, grid=(s // tm,),
        in_specs=[tok, pl.BlockSpec((1, d), lambda i: (0, 0)), pl.BlockSpec((d, d), lambda i: (0, 0)),
                  pl.BlockSpec((2 * HEADS, ml, xd), lambda i: (0, 0, 0)), pl.BlockSpec((d, d), lambda i: (0, 0))],
        out_specs=[tok, tok, probs, tok, tok],
        out_shape=(SDS((s, d), BF16), SDS((s, d), BF16), SDS((s, HEADS * ml), BF16), SDS((s, d), BF16), SDS((s, d), F32)),
        compiler_params=_arb(1), name="attn_forward")(x1, g, w_q, kv, w_o)


def _ffn_forward(x2, g, w_gu, w_down, tm):
    s, d = x2.shape
    _, nf, tf, _ = w_gu.shape

    def body(x2_ref, g_ref, wgu_ref, wd_ref, xn_ref, gu_ref, x3_ref):
        f = pl.program_id(1)

        @pl.when(f == 0)
        def _():
            xv = x2_ref[...]
            xn_ref[...] = (xv * _rstd(xv) * g_ref[...]).astype(BF16)
            x3_ref[...] = xv

        xn = xn_ref[...]
        gate = _dot_nt(xn, wgu_ref[0])
        up = _dot_nt(xn, wgu_ref[1])
        gu_ref[0] = gate.astype(BF16)
        gu_ref[1] = up.astype(BF16)
        act = (gate * _sigmoid(gate) * up).astype(BF16)
        x3_ref[...] += _dot(act, wd_ref[...])

    tok = pl.BlockSpec((tm, d), lambda i, f: (i, 0))
    return _pcall(
        body, grid=(s // tm, nf),
        in_specs=[tok, pl.BlockSpec((1, d), lambda i, f: (0, 0)),
                  pl.BlockSpec((2, None, tf, d), lambda i, f: (0, f, 0, 0)),
                  pl.BlockSpec((tf, d), lambda i, f: (f, 0))],
        out_specs=[tok, pl.BlockSpec((2, None, tm, tf), lambda i, f: (0, f, i, 0)), tok],
        out_shape=(SDS((s, d), BF16), SDS((2, nf, s, tf), BF16), SDS((s, d), F32)),
        compiler_params=_arb(2), name="ffn_forward")(x2, g, w_gu, w_down)


def _final_backward(x3, target, g_final, tm):
    s, d = x3.shape

    def body(x3_ref, tgt_ref, gf_ref, loss_ref, dgf_ref, dx3_ref, dx3b_ref):
        @pl.when(pl.program_id(0) == 0)
        def _():
            loss_ref[...] = jnp.zeros_like(loss_ref)
            dgf_ref[...] = jnp.zeros_like(dgf_ref)

        xv = x3_ref[...]
        r = _rstd(xv)
        diff = xv * r * gf_ref[...] - tgt_ref[...]
        loss_ref[...] += 0.5 * jnp.sum(jnp.sum(diff * diff, axis=-1, keepdims=True), axis=0, keepdims=True) * (1.0 / d)
        dx3, dgf = _rms_bwd(diff * (1.0 / d), xv, r, gf_ref[...])
        dgf_ref[...] += dgf
        dx3_ref[...] = dx3
        dx3b_ref[...] = dx3.astype(BF16)

    tok = pl.BlockSpec((tm, d), lambda i: (i, 0))
    vec = pl.BlockSpec((1, d), lambda i: (0, 0))
    return _pcall(
        body, grid=(s // tm,), in_specs=[tok, tok, vec],
        out_specs=[pl.BlockSpec((SUB, LANES), lambda i: (0, 0)), vec, tok, tok],
        out_shape=(SDS((SUB, LANES), F32), SDS((1, d), F32), SDS((s, d), F32), SDS((s, d), BF16)),
        compiler_params=_arb(1), name="final_backward")(x3, target, g_final)


def _swiglu_backward(dx3b, gu, w_gu, w_down, tm):
    s, d = dx3b.shape
    _, nf, tf, _ = w_gu.shape

    def body(dx3b_ref, gu_ref, wgu_ref, wd_ref, act_ref, dgu_ref, dxn_ref):
        @pl.when(pl.program_id(1) == 0)
        def _():
            dxn_ref[...] = jnp.zeros_like(dxn_ref)

        for r0 in range(0, tm, ROW_CHUNK):
            rows = slice(r0, r0 + ROW_CHUNK)
            dact = _dot_nt(dx3b_ref[rows, :], wd_ref[...])
            gv = gu_ref[0, rows, :].astype(F32)
            uv = gu_ref[1, rows, :].astype(F32)
            sg = _sigmoid(gv)
            silu = gv * sg
            act_ref[rows, :] = (silu * uv).astype(BF16)
            dgate = (dact * uv * (sg * (1.0 + gv * (1.0 - sg)))).astype(BF16)
            dup = (dact * silu).astype(BF16)
            dgu_ref[0, rows, :] = dgate
            dgu_ref[1, rows, :] = dup
            part = _dot(dgate, wgu_ref[0]) + _dot(dup, wgu_ref[1])
            dxn_ref[rows, :] += part

    tok = pl.BlockSpec((tm, d), lambda i, f: (i, 0))
    pair = pl.BlockSpec((2, None, tm, tf), lambda i, f: (0, f, i, 0))
    return _pcall(
        body, grid=(s // tm, nf),
        in_specs=[tok, pair, pl.BlockSpec((2, None, tf, d), lambda i, f: (0, f, 0, 0)),
                  pl.BlockSpec((tf, d), lambda i, f: (f, 0))],
        out_specs=[pl.BlockSpec((None, tm, tf), lambda i, f: (f, i, 0)), pair, tok],
        out_shape=(SDS((nf, s, tf), BF16), SDS((2, nf, s, tf), BF16), SDS((s, d), F32)),
        compiler_params=_arb(2), name="swiglu_backward")(dx3b, gu, w_gu, w_down)


def _attn_backward(dx3, dxn3, x2, g_ffn, x1, g, q, probs, kv, w_q, w_o, tm, carry=()):
    s, d = x1.shape
    _, ml, xd = kv.shape
    scale = xd ** -0.5

    def body(dx3_ref, dxn3_ref, x2_ref, g2_ref, x1_ref, g_ref, q_ref, p_ref, kv_ref, wq_ref, wo_ref,
             dx2b_ref, dq_ref, dx1_ref, dx1b_ref, dkv_ref, dg_ref, dg2_ref, do_s):
        i = pl.program_id(0)

        @pl.when(i == 0)
        def _():
            dkv_ref[...] = jnp.zeros_like(dkv_ref)
            dg_ref[...] = jnp.zeros_like(dg_ref)
            dg2_ref[...] = jnp.zeros_like(dg2_ref)

        x2v = x2_ref[...]
        dx2n, dg2 = _rms_bwd(dxn3_ref[...], x2v, _rstd(x2v), g2_ref[...])
        dg2_ref[...] += dg2
        dx2 = dx3_ref[...] + dx2n
        dx2b_ref[...] = dx2.astype(BF16)
        do_s[...] = _dot_nt(dx2b_ref[...], wo_ref[...]).astype(BF16)
        for hh in range(HEADS):
            kc = slice(hh * xd, (hh + 1) * xd)
            qh = q_ref[:, kc]
            kh = kv_ref[hh]
            doh = do_s[:, kc]
            pb = p_ref[:, hh * ml:(hh + 1) * ml]
            p = pb.astype(F32)
            dp = _dot_nt(doh, kv_ref[HEADS + hh])
            dkv_ref[HEADS + hh] += _dot_tn(pb, doh)
            ds = (p * (dp - jnp.sum(dp * p, axis=-1, keepdims=True)) * scale).astype(BF16)
            dq_ref[:, kc] = _dot(ds, kh).astype(BF16)
            dkv_ref[hh] += _dot_tn(ds, qh)
        dxn = _dot_nt(dq_ref[...], wq_ref[...])
        xv = x1_ref[...]
        dx, dg = _rms_bwd(dxn, xv, _rstd(xv), g_ref[...])
        dg_ref[...] += dg
        dx1 = dx2 + dx
        dx1_ref[...] = dx1
        dx1b_ref[...] = dx1.astype(BF16)

    tok = pl.BlockSpec((tm, d), lambda i: (i, 0))
    vec = pl.BlockSpec((1, d), lambda i: (0, 0))
    sq = pl.BlockSpec((d, d), lambda i: (0, 0))
    kvs = pl.BlockSpec((2 * HEADS, ml, xd), lambda i: (0, 0, 0))
    return _hosted(
        body, carry, grid=(s // tm,),
        in_specs=[tok, tok, tok, vec, tok, vec, tok, pl.BlockSpec((tm, HEADS * ml), lambda i: (i, 0)), kvs, sq, sq],
        out_specs=[tok, tok, tok, tok, kvs, vec, vec],
        out_shape=(SDS((s, d), BF16), SDS((s, d), BF16), SDS((s, d), F32), SDS((s, d), BF16),
                   SDS((2 * HEADS, ml, xd), F32), SDS((1, d), F32), SDS((1, d), F32)),
        scratch_shapes=[pltpu.VMEM((tm, d), BF16)],
        compiler_params=_arb(1), name="attn_backward")(dx3, dxn3, x2, g_ffn, x1, g, q, probs, kv, w_q, w_o)


def _kv_backward(dkv, memn, mem, g_mem, w_kv):
    ml, d = mem.shape
    xd = w_kv.shape[2]

    def body(dkv_ref, memn_ref, mem_ref, g_ref, w_ref, dw_ref, dg_ref):
        dmemn = jnp.zeros((ml, d), F32)
        for j in range(2 * HEADS):
            dkvb = dkv_ref[j].astype(BF16)
            dw_ref[j] = _dot_tn(memn_ref[...], dkvb)
            dmemn = dmemn + _dot_nt(dkvb, w_ref[j])
        x = mem_ref[...]
        dg_ref[...] = jnp.sum(dmemn * (x * _rstd(x)), axis=0, keepdims=True)

    return _pcall(body, out_shape=(SDS((2 * HEADS, d, xd), F32), SDS((1, d), F32)), name="kv_backward")(dkv, memn, mem, g_mem, w_kv)


def _mix_backward(dx1, x, g_mix, h, lng, lnb, w_sp, bt, conv_w, ga, gb, w_out, w_in, tm, carry=()):
    s, d = x.shape
    n_in = h.shape[1]
    aw = lng.shape[1]
    bw = d - aw
    hd = aw // HEADS
    in_a = 2 * aw
    hb_blocks = tm // HALO
    last_blk = s // HALO - 1
    nt = s // tm
    tc = tm
    te = tc + HALO
    tee = tc + 2 * HALO

    def body(dx1_ref, dx1n_ref, x_ref, gm_ref, h_ref, hp_ref, hn_ref, lng_ref, lnb_ref, wsp_ref, bt_ref, cw_ref,
             ga_ref, gb_ref, wout_ref, win_ref,
             dh_ref, dx_ref, dga_ref, dgb_ref, dcw_ref, dlng_ref, dlnb_ref, dwsp_ref, dbs_ref, dgm_ref,
             mixed_s, dvln_s):
        i = pl.program_id(0)

        @pl.when(i == 0)
        def _():
            for ref in (dga_ref, dgb_ref, dcw_ref, dlng_ref, dlnb_ref, dwsp_ref, dbs_ref, dgm_ref):
                ref[...] = jnp.zeros_like(ref)

        mask = _tril_mask()
        wm = [(wsp_ref[hh] * mask).astype(BF16) for hh in range(HEADS)]
        cw = cw_ref[...]

        def chain(r0):
            rows = slice(r0, r0 + tc)
            first, last = r0 == 0, r0 + tc == tm
            hv = h_ref[rows, :]
            dx1 = dx1_ref[rows, :]
            dx1n = dx1n_ref[...] if last else dx1_ref[r0 + tc:r0 + tc + HALO, :]
            hp = hp_ref[:, in_a:] if first else h_ref[r0 - HALO:r0, in_a:]
            hn = hn_ref[:, in_a:] if last else h_ref[r0 + tc:r0 + tc + HALO, in_a:]
            dx1e = jnp.concatenate([dx1, dx1n], axis=0).astype(BF16)
            dycat = _dot_nt(dx1e, wout_ref[...])

            hbe = jnp.concatenate([hp, hv[:, in_a:], hn], axis=0)
            row = lax.broadcasted_iota(jnp.int32, (tee, 1), 0)
            zext = hbe[:, bw:2 * bw] * hbe[:, 2 * bw:]
            if first:
                zext = jnp.where((i == 0) & (row < HALO), 0.0, zext)
            z2e, z1e = _conv_taps(zext)
            conv_e = (cw[0:1] * z2e + cw[1:2] * z1e + cw[2:3] * zext)[HALO:]
            gate_b_e = hbe[HALO:, :bw]
            sc_e = gate_b_e * conv_e
            rb = _rstd(sc_e)
            dyb = dycat[:, aw:]
            gdy = dyb * gb_ref[...]
            dsc_e = rb * gdy - sc_e * (rb * rb * rb) * (jnp.sum(gdy * sc_e, axis=-1, keepdims=True) * (1.0 / bw))
            dgb_ref[...] += jnp.sum((dyb * (sc_e * rb))[:tc], axis=0, keepdims=True)
            dconv_e = dsc_e * gate_b_e
            if last:
                dconv_e = jnp.where((i == nt - 1) & (row[:te] >= tc), 0.0, dconv_e)
            dconv = dconv_e[:tc]
            dc1 = pltpu.roll(dconv_e, te - 1, 0)[:tc]
            dc2 = pltpu.roll(dconv_e, te - 2, 0)[:tc]
            dz = cw[2:3] * dconv + cw[1:2] * dc1 + cw[0:1] * dc2
            z = zext[HALO:HALO + tc]
            z1 = z1e[HALO:HALO + tc]
            z2 = z2e[HALO:HALO + tc]
            dcw_ref[0:1, :] += jnp.sum(dconv * z2, axis=0, keepdims=True)
            dcw_ref[1:2, :] += jnp.sum(dconv * z1, axis=0, keepdims=True)
            dcw_ref[2:3, :] += jnp.sum(dconv * z, axis=0, keepdims=True)
            dh_ref[rows, in_a:in_a + bw] = (dsc_e[:tc] * conv_e[:tc]).astype(BF16)
            dh_ref[rows, in_a + bw:in_a + 2 * bw] = (dz * hv[:, in_a + 2 * bw:]).astype(BF16)
            dh_ref[rows, in_a + 2 * bw:] = (dz * hv[:, in_a + bw:in_a + 2 * bw]).astype(BF16)

            ha = hv[:, :in_a]
            mixed_c, dvln_c = mixed_s.at[rows, :], dvln_s.at[rows, :]
            th, u, xhat, rl, vln = _sgu_forward(ha, lng_ref[...], lnb_ref[...], wm, bt_ref[...], mixed_c)
            mixed = mixed_c[...]
            sg = u * mixed
            dsg, dga = _rms_bwd(dycat[:tc, :aw], sg, _rstd(sg), ga_ref[...])
            dga_ref[...] += dga
            du = dsg * mixed
            dmixed = dsg * u
            dmb = dmixed.astype(BF16)
            for n in range(tc // CHUNK):
                blk = slice(n * CHUNK, (n + 1) * CHUNK)
                dbs_ref[...] += dmixed[blk]
                for hh in range(HEADS):
                    cols = slice(hh * hd, (hh + 1) * hd)
                    dvln_c[blk, cols] = _dot_tn(wm[hh], dmb[blk, cols])
                    dwsp_ref[hh] += mask * _dot_nt(dmb[blk, cols], vln[blk, cols])
            dvln = dvln_c[...]
            dlng_ref[...] += jnp.sum(dvln * xhat, axis=0, keepdims=True)
            dlnb_ref[...] += jnp.sum(dvln, axis=0, keepdims=True)
            dxh = dvln * lng_ref[...]
            dv = rl * (dxh - jnp.mean(dxh, axis=-1, keepdims=True) - xhat * jnp.mean(dxh * xhat, axis=-1, keepdims=True))
            dh_ref[rows, :in_a] = (jnp.concatenate([du, dv], axis=-1) * _gelu_grad(ha, th)).astype(BF16)

            dxn = _dot(dh_ref[rows, :], win_ref[...])
            xv = x_ref[rows, :]
            dx, dgm = _rms_bwd(dxn, xv, _rstd(xv), gm_ref[...])
            dgm_ref[...] += dgm
            dx_ref[rows, :] = dx1 + dx

        for r0 in range(0, tm, tc):
            chain(r0)

    full = lambda shape: pl.BlockSpec(shape, lambda i: (0,) * len(shape))
    tok = pl.BlockSpec((tm, d), lambda i: (i, 0))
    nxt = lambda i: (jnp.minimum((i + 1) * hb_blocks, last_blk), 0)
    prv = lambda i: (jnp.maximum(i * hb_blocks - 1, 0), 0)
    return _hosted(
        body, carry, grid=(nt,),
        in_specs=[tok, pl.BlockSpec((HALO, d), nxt), tok, full((1, d)),
                  pl.BlockSpec((tm, n_in), lambda i: (i, 0)), pl.BlockSpec((HALO, n_in), prv), pl.BlockSpec((HALO, n_in), nxt),
                  full((1, aw)), full((1, aw)), full((HEADS, CHUNK, CHUNK)), full((CHUNK, HEADS)), full((3, bw)),
                  full((1, aw)), full((1, bw)), full((d, d)), full((n_in, d))],
        out_specs=[pl.BlockSpec((tm, n_in), lambda i: (i, 0)), tok,
                   full((1, aw)), full((1, bw)), full((SUB, bw)), full((1, aw)), full((1, aw)),
                   full((HEADS, CHUNK, CHUNK)), full((CHUNK, aw)), full((1, d))],
        out_shape=(SDS((s, n_in), BF16), SDS((s, d), F32),
                   SDS((1, aw), F32), SDS((1, bw), F32), SDS((SUB, bw), F32), SDS((1, aw), F32), SDS((1, aw), F32),
                   SDS((HEADS, CHUNK, CHUNK), F32), SDS((CHUNK, aw), F32), SDS((1, d), F32)),
        scratch_shapes=[pltpu.VMEM((tm, aw), F32), pltpu.VMEM((tm, aw), F32)],
        compiler_params=_arb(1), name="mix_backward")(dx1, dx1, x, g_mix, h, h, h, lng, lnb, w_sp, bt, conv_w, ga, gb, w_out, w_in)


def _bias_grad(dbs):
    aw = dbs.shape[1]
    hd = aw // HEADS

    def body(dbs_ref, out_ref):
        ones = jnp.ones((SUB, hd), F32)
        for hh in range(HEADS):
            r = lax.dot_general(ones, dbs_ref[:, hh * hd:(hh + 1) * hd], (((1,), (1,)), ((), ())),
                                precision=lax.Precision.HIGHEST, preferred_element_type=F32)
            out_ref[hh:hh + 1, :] = r[0:1]

    return _pcall(body, out_shape=SDS((HEADS, CHUNK), F32), name="bias_grad")(dbs)


def _wgrad_body(a_ref, b_ref, o_ref):
    o_ref[...] = _dot_tn(a_ref[...], b_ref[...])


def _wgrad(a, b, name, carry=()):
    k, m = a.shape
    n = b.shape[1]
    tm = _tile(m, 512, LANES)
    tn = _tile(n, 1024, LANES)
    return _hosted(
        functools.partial(_wgrad_body), carry, grid=(m // tm, n // tn),
        in_specs=[pl.BlockSpec((k, tm), lambda i, j: (0, i)), pl.BlockSpec((k, tn), lambda i, j: (0, j))],
        out_specs=pl.BlockSpec((tm, tn), lambda i, j: (i, j)),
        out_shape=SDS((m, n), F32), compiler_params=_arb(2), name=name)(a, b)


def _wgrad_blocked_lhs(a, b, name, carry=()):
    nb, k, t = a.shape
    n = b.shape[1]
    tn = _tile(n, 1024, LANES)
    return _hosted(
        functools.partial(_wgrad_body), carry, grid=(nb, n // tn),
        in_specs=[pl.BlockSpec((None, k, t), lambda i, j: (i, 0, 0)), pl.BlockSpec((k, tn), lambda i, j: (0, j))],
        out_specs=pl.BlockSpec((t, tn), lambda i, j: (i, j)),
        out_shape=SDS((nb * t, n), F32), compiler_params=_arb(2), name=name)(a, b)


def _wgrad_blocked_rhs(a, b, name, carry=()):
    k, m = a.shape
    nb, _, t = b.shape
    tm = _tile(m, 512, LANES)
    return _hosted(
        functools.partial(_wgrad_body), carry, grid=(m // tm, nb),
        in_specs=[pl.BlockSpec((k, tm), lambda i, j: (0, i)), pl.BlockSpec((None, k, t), lambda i, j: (j, 0, 0))],
        out_specs=pl.BlockSpec((None, tm, t), lambda i, j: (j, i, 0)),
        out_shape=SDS((nb, m, t), F32), compiler_params=_arb(2), name=name)(a, b)


def _unblock_cols(wb, name, carry=()):
    nb, r, t = wb.shape
    tr = _tile(r, 256, 16)

    def body(w_ref, o_ref):
        o_ref[...] = jnp.concatenate([w_ref[j].astype(F32) for j in range(nb)], axis=-1).astype(o_ref.dtype)

    return _hosted(
        body, carry, grid=(r // tr,),
        in_specs=[pl.BlockSpec((nb, tr, t), lambda i: (0, i, 0))], out_specs=pl.BlockSpec((tr, nb * t), lambda i: (i, 0)),
        out_shape=SDS((r, nb * t), wb.dtype), compiler_params=_arb(1), name=name)(wb)


def _block_cols(w, nb, name, carry=()):
    r, n = w.shape
    t = n // nb
    tr = _tile(r, 256, 16)

    def body(w_ref, o_ref):
        wv = w_ref[...]
        for j in range(nb):
            o_ref[j] = wv[:, j * t:(j + 1) * t]

    return _hosted(
        body, carry, grid=(r // tr,),
        in_specs=[pl.BlockSpec((tr, n), lambda i: (i, 0))], out_specs=pl.BlockSpec((nb, tr, t), lambda i: (0, i, 0)),
        out_shape=SDS((nb, r, t), w.dtype), compiler_params=_arb(1), name=name)(w)


def _place():
    x, y, c = lax.axis_index("x"), lax.axis_index("y"), lax.axis_index("c")
    return x, y, c, [(1 - x, y), (x, 1 - y), (1 - x, 1 - y)]


def _all_gather(shards):
    n = len(shards)
    slots = 9
    cut = [(s.shape[0] // 32) * 16 for s in shards]

    def build(ins, outs, sems):
        send_sems, recv_sems, local_sems = sems
        x, y, c, _ = _place()
        me, sib, xn, yn, dg = (x, y, c), (x, y, 1 - c), (1 - x, y, c), (x, 1 - y, c), (1 - x, 1 - y, c)
        other = lambda p: (p[0], p[1], 1 - p[2])

        def rows(a, p, part=None):
            ref = outs[a].at[4 * p[0] + 2 * p[1] + p[2]]
            if part is None or cut[a] == 0:
                return ref if part in (None, 0) else None
            return ref.at[pl.ds(0, cut[a])] if part == 0 else ref.at[pl.ds(cut[a], shards[a].shape[0] - cut[a])]

        def copy(a, k, ref, to, src=None):
            if ref is None:
                return None
            return pltpu.make_async_remote_copy(
                src_ref=ref if src is None else src, dst_ref=ref, send_sem=send_sems.at[slots * a + k],
                recv_sem=recv_sems.at[slots * a + k], device_id=to, device_id_type=MESH)

        def real(cps):
            return [cp for cp in cps if cp is not None]

        class Copies:
            own = lambda a: [copy(a, 1, rows(a, me), xn, ins[a]), copy(a, 2, rows(a, me), yn, ins[a]),
                             copy(a, 0, rows(a, me), sib, ins[a])]
            local = lambda a: pltpu.make_async_copy(ins[a], rows(a, me), local_sems.at[a])
            from_x = lambda a: copy(a, 1, rows(a, xn), me)
            from_y = lambda a: copy(a, 2, rows(a, yn), me)
            after_x = lambda a: real([copy(a, 4, rows(a, xn, 1), yn), copy(a, 5, rows(a, xn), sib)])
            after_y = lambda a: real([copy(a, 3, rows(a, yn, 0), xn), copy(a, 6, rows(a, yn), sib)])
            diag_in = lambda a: real([copy(a, 3, rows(a, dg, 0), me), copy(a, 4, rows(a, dg, 1), me)])
            diag_on = lambda a: real([copy(a, 7, rows(a, dg, 0), sib), copy(a, 8, rows(a, dg, 1), sib)])
            from_sib = lambda a: real([copy(a, 0, rows(a, sib), me), copy(a, 5, rows(a, other(xn)), me),
                                       copy(a, 6, rows(a, other(yn)), me), copy(a, 7, rows(a, other(dg), 0), me),
                                       copy(a, 8, rows(a, other(dg), 1), me)])

        return Copies

    def start(ins, outs, sems):
        cps = build(ins, outs, sems)
        for a in range(n):
            for cp in cps.own(a):
                cp.start()
        for a in range(n):
            cps.local(a).start()

    def relay(ins, outs, sems):
        cps = build(ins, outs, sems)
        for a in range(n):
            cps.from_x(a).wait_recv()
            for cp in cps.after_x(a):
                cp.start()
            cps.from_y(a).wait_recv()
            for cp in cps.after_y(a):
                cp.start()

    def finish(ins, outs, sems):
        cps = build(ins, outs, sems)
        for a in range(n):
            for arrived, onward in zip(cps.diag_in(a), cps.diag_on(a)):
                arrived.wait_recv()
                onward.start()
        for a in range(n):
            for cp in cps.from_sib(a):
                cp.wait_recv()
            for cp in cps.own(a) + cps.after_x(a) + cps.after_y(a) + cps.diag_on(a):
                cp.wait_send()
            cps.local(a).wait()

    return _Exchange(shards, [SDS((N_DEV,) + s.shape, s.dtype) for s in shards],
                     [pltpu.SemaphoreType.DMA((slots * n,)), pltpu.SemaphoreType.DMA((slots * n,)),
                      pltpu.SemaphoreType.DMA((n,))], start, finish, relay)


def _swap_exchange(ins, out_shape, per, copies):
    def start(i, o, sems):
        for cp in copies(i, o, sems):
            cp.start()

    def finish(i, o, sems):
        for cp in copies(i, o, sems):
            cp.wait()

    n = per * len(ins)
    return _Exchange(ins, out_shape, [pltpu.SemaphoreType.DMA((n,)), pltpu.SemaphoreType.DMA((n,))], start, finish)


def _exchange_c(gs):
    def copies(ins, outs, sems):
        x, y, c, _ = _place()
        return [pltpu.make_async_remote_copy(
                    src_ref=ins[a].at[2 * k + 1 - c], dst_ref=outs[a].at[k],
                    send_sem=sems[0].at[4 * a + k], recv_sem=sems[1].at[4 * a + k],
                    device_id=(x, y, 1 - c), device_id_type=MESH)
                for a in range(len(gs)) for k in range(4)]

    return _swap_exchange(gs, [SDS((4,) + g.shape[1:], g.dtype) for g in gs], 4, copies)


def _exchange_xy(sends):
    def copies(ins, outs, sems):
        x, y, c, chips = _place()
        return [pltpu.make_async_remote_copy(
                    src_ref=ins[a].at[t], dst_ref=outs[a].at[t],
                    send_sem=sems[0].at[3 * a + t], recv_sem=sems[1].at[3 * a + t],
                    device_id=(*chips[t], c), device_id_type=MESH)
                for a in range(len(sends)) for t in range(3)]

    return _swap_exchange(sends, [SDS(s.shape, s.dtype) for s in sends], 3, copies)


def _rs_combine(g, recv, pos, name, carry=()):
    _, r, cdim = g.shape
    tr = _tile(r, 256, 16)

    def body(pos_ref, g0, r0, g1, r1, g2, r2, g3, r3, keep_ref, send_ref):
        keep_ref[...] = g0[...] + r0[...]
        send_ref[0] = (g1[...] + r1[...]).astype(BF16)
        send_ref[1] = (g2[...] + r2[...]).astype(BF16)
        send_ref[2] = (g3[...] + r3[...]).astype(BF16)

    def k_of(p, t):
        px = p[0] if t in (0, 2) else 1 - p[0]
        py = p[1] if t in (0, 1) else 1 - p[1]
        return 2 * px + py

    blk = (None, tr, cdim)
    in_specs = []
    for t in range(4):
        in_specs.append(pl.BlockSpec(blk, functools.partial(lambda j, p, t: (2 * k_of(p, t) + p[2], j, 0), t=t)))
        in_specs.append(pl.BlockSpec(blk, functools.partial(lambda j, p, t: (k_of(p, t), j, 0), t=t)))
    return _hosted(
        body, carry, n_prefetch=1, out_shape=(SDS((r, cdim), F32), SDS((3, r, cdim), BF16)),
        grid=(r // tr,), in_specs=in_specs,
        out_specs=[pl.BlockSpec((tr, cdim), lambda j, p: (j, 0)), pl.BlockSpec((3, tr, cdim), lambda j, p: (0, j, 0))],
        compiler_params=_arb(1), name=name)(pos, g, recv, g, recv, g, recv, g, recv)


def _adamw_shard(keep, recv, w, m, v, name):
    r, cdim = w.shape
    tr = _tile(r, 256, 16)

    def body(k_ref, r_ref, w_ref, m_ref, v_ref, g_ref, d_ref, nm_ref, nv_ref):
        g = ((k_ref[...] + r_ref[0].astype(F32)) + r_ref[1].astype(F32)) + r_ref[2].astype(F32)
        g_ref[...] = g
        d_ref[...], nm_ref[...], nv_ref[...] = _adamw(w_ref[...], g, m_ref[...], v_ref[...])

    blk = pl.BlockSpec((tr, cdim), lambda j: (j, 0))
    out = SDS((r, cdim), F32)
    return _pcall(body, grid=(r // tr,), in_specs=[blk, pl.BlockSpec((3, tr, cdim), lambda j: (0, j, 0)), blk, blk, blk],
                  out_specs=[blk] * 4, out_shape=(out,) * 4, compiler_params=_arb(1), name=name)(keep, recv, w, m, v)


_HBM = pl.BlockSpec(memory_space=pltpu.HBM)
_SEM = pl.BlockSpec(memory_space=pltpu.SEMAPHORE)
_SPLIT = pltpu.CompilerParams(has_side_effects=pltpu.SideEffectType.DATAFLOW_SIDE_EFFECTING)


def _split_copies(kind, n, refs):
    srcs, lands, (send_sems, recv_sems) = refs[:n], refs[n:2 * n], refs[2 * n:2 * n + 2]
    x, y, c, chips = _place()
    per = _SPLIT_COPIES[kind]
    if kind == "xy":
        ends = lambda a, t: (srcs[a].at[t], lands[a].at[t], (*chips[t], c))
    else:
        ends = lambda a, k: (srcs[a].at[2 * k + 1 - c], lands[a].at[k], (x, y, 1 - c))
    cps = []
    for a in range(n):
        for t in range(per):
            src, dst, to = ends(a, t)
            cps.append(pltpu.make_async_remote_copy(src_ref=src, dst_ref=dst, send_sem=send_sems.at[per * a + t],
                                                    recv_sem=recv_sems.at[per * a + t], device_id=to, device_id_type=MESH))
    return cps


_SPLIT_COPIES = {"xy": 3, "c": 4}


def _exchange_start(kind, arrays, name, after=None):
    n = len(arrays)
    order = [] if after is None else [after]

    def body(*refs):
        refs = refs[:2 * n] + refs[2 * n + len(order):]
        for cp in _split_copies(kind, n, refs):
            cp.start()
        refs[-1][...] = jnp.zeros_like(refs[-1])

    hbm = lambda a: pltpu.with_memory_space_constraint(a, pltpu.HBM)
    land = [a.shape if kind == "xy" else (4,) + a.shape[1:] for a in arrays]
    bufs = [pltpu.HBM(a.shape, a.dtype) for a in arrays] + [pltpu.HBM(s, a.dtype) for s, a in zip(land, arrays)]
    sems = pltpu.SemaphoreType.DMA((_SPLIT_COPIES[kind] * n,))
    res = _pcall(
        body, name=name, out_shape=(sems, sems, *bufs, SDS((SUB, LANES), F32)),
        in_specs=[_HBM] * (2 * n) + _hbm_specs(len(order)),
        out_specs=[_SEM, _SEM] + [_HBM] * (2 * n) + [pl.BlockSpec(memory_space=pltpu.VMEM)],
        input_output_aliases={k: 2 + k for k in range(2 * n)}, compiler_params=_SPLIT)(
            *[hbm(a) for a in arrays], *[hbm(lax.empty(s, a.dtype)) for s, a in zip(land, arrays)], *order)
    return (kind, n, res[:-1]), res[-1]


def _exchange_wait(started, after, name, sources=False):
    kind, n, (send_sems, recv_sems, *bufs) = started

    def body(*refs):
        for cp in _split_copies(kind, n, refs):
            cp.wait_send()
            cp.wait_recv()

    shapes = [pltpu.HBM(b.shape, b.dtype) for b in bufs]
    res = _pcall(
        body, name=name, out_shape=tuple(shapes),
        in_specs=[_HBM] * (2 * n) + [_SEM, _SEM, pl.BlockSpec(memory_space=pl.ANY)], out_specs=[_HBM] * (2 * n),
        input_output_aliases={k: k for k in range(2 * n)}, compiler_params=_SPLIT)(*bufs, send_sems, recv_sems, after)
    return (list(res[:n]), list(res[n:])) if sources else list(res[n:])


def _follow(token):
    nothing = lambda ins, outs, sems: None
    return _Exchange([token], [], [], nothing, nothing)


def _adamw_small(gathered, seg, params, conv_rows):
    names = list(params)
    c0, cn = conv_rows

    def body(*refs):
        gat_ref = refs[0]
        ins = refs[1:1 + 3 * len(names)]
        outs = refs[1 + 3 * len(names):]

        def total(r0, rn):
            tot = gat_ref[0, r0:r0 + rn, :]
            for dev in range(1, N_DEV):
                tot = tot + gat_ref[dev, r0:r0 + rn, :]
            return tot

        for k, nm in enumerate(names):
            g = total(*seg[nm])
            w_ref, m_ref, v_ref = ins[3 * k:3 * k + 3]
            g_ref, d_ref, nm_ref, nv_ref = outs[4 * k:4 * k + 4]
            g_ref[...] = g
            d_ref[...], nm_ref[...], nv_ref[...] = _adamw(w_ref[...], g, m_ref[...], v_ref[...])
        outs[-2][...] = total(c0, cn)
        outs[-1][...] = total(*seg["loss"])

    flat_in = [a for nm in names for a in params[nm]]
    out_shape = []
    for nm in names:
        out_shape += [SDS(params[nm][0].shape, F32)] * 4
    out_shape += [SDS((cn, LANES), F32), SDS((seg["loss"][1], LANES), F32)]
    res = _pcall(body, out_shape=tuple(out_shape), name="adamw_small")(gathered, *flat_in)
    per = {nm: res[4 * k:4 * k + 4] for k, nm in enumerate(names)}
    return per, res[-2], res[-1]


def _adamw_one(w, g, m, v, name):
    def body(w_ref, g_ref, m_ref, v_ref, d_ref, nm_ref, nv_ref):
        d_ref[...], nm_ref[...], nv_ref[...] = _adamw(w_ref[...], g_ref[...], m_ref[...], v_ref[...])

    return _pcall(body, out_shape=(SDS(w.shape, F32),) * 3, name=name)(w, g, m, v)


def _rows128(a):
    return a.reshape(-1, LANES)


def _pack_small(gs, loss_tile):
    seg, pieces, row = {}, [], 0
    for nm in SMALL + ("conv_w", "loss"):
        piece = loss_tile if nm == "loss" else _rows128(gs[nm])
        rn = _round_up(piece.shape[0], SUB)
        pieces.append(jnp.pad(piece, ((0, rn - piece.shape[0]), (0, 0))))
        seg[nm] = (row, piece.shape[0])
        row += rn
    return jnp.concatenate(pieces, axis=0), seg


def _step(x, mem, target, wb, conv_w, sp, pos):
    s, d = x.shape
    tm = min(TOKEN_TILE, s)
    tm_wide = min(2 * TOKEN_TILE, s)
    rows = lambda w8: w8.reshape(-1, w8.shape[2])
    shards = lambda g: g.reshape((N_DEV, -1) + g.shape[1:])
    bt = sp["b_spatial"].T

    (w_in8, conv8), = _run_exchanges([_all_gather([wb["w_in"], conv_w])], "gather_w_in")
    conv_full = conv8.transpose(1, 0, 2).reshape(3, -1)
    w_in_t = rows(w_in8)
    (xn1, h), ((w_out8, w_kv8, w_q8),) = _in_forward(
        x, sp["ln_mix_g"], w_in_t, tm, carry=[_all_gather([wb["w_out"], wb["w_kv"], wb["w_q"]])])
    w_out = rows(w_out8)
    (ycat, x1), ((w_o8, w_down8),) = _mix_forward(
        h, x, sp["sgu_ln_g"], sp["sgu_ln_b"], sp["w_spatial"], bt, conv_full, sp["grp_norm_a"], sp["grp_norm_b"], w_out, tm,
        carry=[_all_gather([wb["w_o"], wb["w_down"]])])
    w_q, w_o, w_down = rows(w_q8), rows(w_o8), rows(w_down8)
    memn, kv = _kv_forward(mem, sp["ln_mem_g"], w_kv8)
    (xn2, q, probs, o, x2), ((w_gu8,),) = _attn_forward(
        x1, sp["ln_attn_g"], w_q, kv, w_o, tm, carry=[_all_gather([wb["w_gate_up"]])])
    w_gu = w_gu8.reshape((2, N_DEV // 2) + w_gu8.shape[1:])
    xn3, gu, x3 = _ffn_forward(x2, sp["ln_ffn_g"], w_gu, w_down, tm_wide)

    loss, d_lnf, dx3, dx3b = _final_backward(x3, target, sp["ln_final_g"], tm_wide)
    act, dgu, dxn3 = _swiglu_backward(dx3b, gu, w_gu, w_down, tm_wide)
    g_gu, _ = _wgrad_blocked_lhs(dgu.reshape((N_DEV,) + dgu.shape[2:]), xn3, "wgrad_gate_up")
    g_gu = shards(g_gu)
    g_down, ((rc_gu,),) = _wgrad_blocked_lhs(act, dx3b, "wgrad_down", carry=[_exchange_c([g_gu])])
    g_down = shards(g_down)
    keep, pending = {}, []
    (keep["w_gate_up"], send_gu), _ = _rs_combine(g_gu, rc_gu, pos, "rs_combine_w_gate_up")
    started, token = _exchange_start("xy", [send_gu], "exchange_xy_1_start")
    pending.append((("w_gate_up",), started))
    c_down, token = _exchange_start("c", [g_down], "exchange_c_1_start", after=token)
    (dx2b, dq, dx1, dx1b, dkv, d_lnattn, d_lnffn), _ = _attn_backward(
        dx3, dxn3, x2, sp["ln_ffn_g"], x1, sp["ln_attn_g"], q, probs, kv, w_q, w_o, tm, carry=[_follow(token)])
    (g_down,), (rc_down,) = _exchange_wait(c_down, dx1b, "exchange_c_1_wait", sources=True)
    (keep["w_down"], send_down), _ = _rs_combine(g_down, rc_down, pos, "rs_combine_w_down")
    g_o, _ = _wgrad(o, dx2b, "wgrad_o")
    g_q, _ = _wgrad(xn2, dq, "wgrad_q")
    g_o, g_q = shards(g_o), shards(g_q)
    g_kv, d_lnmem = _kv_backward(dkv, memn, mem, sp["ln_mem_g"], w_kv8)
    c_oqkv, token = _exchange_start("c", [g_o, g_q, g_kv], "exchange_c_2_start")
    g_out, _ = _wgrad(ycat, dx1b, "wgrad_out", carry=[_follow(token)])
    g_out = shards(g_out)
    (g_o, g_q, g_kv), (rc_o, rc_q, rc_kv) = _exchange_wait(c_oqkv, g_out, "exchange_c_2_wait", sources=True)
    c_out, token = _exchange_start("c", [g_out], "exchange_c_3_start")
    (keep["w_o"], send_o), _ = _rs_combine(g_o, rc_o, pos, "rs_combine_w_o", carry=[_follow(token)])
    (keep["w_q"], send_q), _ = _rs_combine(g_q, rc_q, pos, "rs_combine_w_q")
    (keep["w_kv"], send_kv), _ = _rs_combine(g_kv, rc_kv, pos, "rs_combine_w_kv")
    (g_out,), (rc_out,) = _exchange_wait(c_out, send_kv, "exchange_c_3_wait", sources=True)
    (keep["w_out"], send_out), _ = _rs_combine(g_out, rc_out, pos, "rs_combine_w_out")
    started, token = _exchange_start("xy", [send_down, send_o, send_q, send_out, send_kv], "exchange_xy_2_start")
    pending.append((("w_down", "w_o", "w_q", "w_out", "w_kv"), started))
    (dh, dx, d_ga, d_gb, d_cw, d_lng, d_lnb, d_wsp, d_bs, d_lnmix), _ = _mix_backward(
        dx1, x, sp["ln_mix_g"], h, sp["sgu_ln_g"], sp["sgu_ln_b"], sp["w_spatial"], bt, conv_full,
        sp["grp_norm_a"], sp["grp_norm_b"], w_out, w_in_t, tm, carry=[_follow(token)])
    gs = {"ln_mix_g": d_lnmix, "sgu_ln_g": d_lng, "sgu_ln_b": d_lnb, "w_spatial": d_wsp, "b_spatial": _bias_grad(d_bs),
          "conv_w": d_cw[:3], "grp_norm_a": d_ga, "grp_norm_b": d_gb, "ln_attn_g": d_lnattn, "ln_mem_g": d_lnmem,
          "ln_ffn_g": d_lnffn, "ln_final_g": d_lnf}
    packed, seg = _pack_small(gs, loss)
    g_in, (_, (small_all,)) = _wgrad(dh, xn1, "wgrad_in", carry=[_follow(token), _all_gather([packed])])
    g_in = shards(g_in)
    c_in, token = _exchange_start("c", [g_in], "exchange_c_4_start")
    return dx, keep, pending, (g_in, c_in), token, small_all, seg


def kernel(x, mem, ln_mix_g, w_in, sgu_ln_g, sgu_ln_b, w_spatial, b_spatial, conv_w, grp_norm_a, grp_norm_b, w_out, ln_attn_g, ln_mem_g, w_q, w_kv, w_o, ln_ffn_g, w_gate_up, w_down, ln_final_g, loss_target, m_ln_mix_g, m_w_in, m_sgu_ln_g, m_sgu_ln_b, m_w_spatial, m_b_spatial, m_conv_w, m_grp_norm_a, m_grp_norm_b, m_w_out, m_ln_attn_g, m_ln_mem_g, m_w_q, m_w_kv, m_w_o, m_ln_ffn_g, m_w_gate_up, m_w_down, m_ln_final_g, v_ln_mix_g, v_w_in, v_sgu_ln_g, v_sgu_ln_b, v_w_spatial, v_b_spatial, v_conv_w, v_grp_norm_a, v_grp_norm_b, v_w_out, v_ln_attn_g, v_ln_mem_g, v_w_q, v_w_kv, v_w_o, v_ln_ffn_g, v_w_gate_up, v_w_down, v_ln_final_g):
    order = ["ln_mix_g", "w_in", "sgu_ln_g", "sgu_ln_b", "w_spatial", "b_spatial", "conv_w", "grp_norm_a", "grp_norm_b",
             "w_out", "ln_attn_g", "ln_mem_g", "w_q", "w_kv", "w_o", "ln_ffn_g", "w_gate_up", "w_down", "ln_final_g"]
    W = dict(ln_mix_g=ln_mix_g, w_in=w_in, sgu_ln_g=sgu_ln_g, sgu_ln_b=sgu_ln_b, w_spatial=w_spatial, b_spatial=b_spatial,
             conv_w=conv_w, grp_norm_a=grp_norm_a, grp_norm_b=grp_norm_b, w_out=w_out, ln_attn_g=ln_attn_g,
             ln_mem_g=ln_mem_g, w_q=w_q, w_kv=w_kv, w_o=w_o, ln_ffn_g=ln_ffn_g, w_gate_up=w_gate_up, w_down=w_down,
             ln_final_g=ln_final_g)
    M = dict(ln_mix_g=m_ln_mix_g, w_in=m_w_in, sgu_ln_g=m_sgu_ln_g, sgu_ln_b=m_sgu_ln_b, w_spatial=m_w_spatial,
             b_spatial=m_b_spatial, conv_w=m_conv_w, grp_norm_a=m_grp_norm_a, grp_norm_b=m_grp_norm_b, w_out=m_w_out,
             ln_attn_g=m_ln_attn_g, ln_mem_g=m_ln_mem_g, w_q=m_w_q, w_kv=m_w_kv, w_o=m_w_o, ln_ffn_g=m_ln_ffn_g,
             w_gate_up=m_w_gate_up, w_down=m_w_down, ln_final_g=m_ln_final_g)
    V = dict(ln_mix_g=v_ln_mix_g, w_in=v_w_in, sgu_ln_g=v_sgu_ln_g, sgu_ln_b=v_sgu_ln_b, w_spatial=v_w_spatial,
             b_spatial=v_b_spatial, conv_w=v_conv_w, grp_norm_a=v_grp_norm_a, grp_norm_b=v_grp_norm_b, w_out=v_w_out,
             ln_attn_g=v_ln_attn_g, ln_mem_g=v_ln_mem_g, w_q=v_w_q, w_kv=v_w_kv, w_o=v_w_o, ln_ffn_g=v_ln_ffn_g,
             w_gate_up=v_w_gate_up, w_down=v_w_down, ln_final_g=v_ln_final_g)

    bw = conv_w.shape[1] * N_DEV
    pos = jnp.stack([lax.axis_index("x"), lax.axis_index("y"), lax.axis_index("c")]).astype(jnp.int32)
    me = 4 * pos[0] + 2 * pos[1] + pos[2]

    sp = {nm: (W[nm].reshape(1, -1) if W[nm].ndim == 1 else W[nm]) for nm in SMALL}
    view = lambda a, nm: a.T if nm in TRANSPOSED else a
    wb = {nm: view(W[nm], nm).astype(BF16) for nm in BIG}
    grad_x, keep, pending, (g_in, c_in), token, small_all, seg = _step(
        x[0], mem[0], loss_target[0], wb, conv_w, sp, pos)

    out = {}

    def update(k, names, started, token):
        landed = _exchange_wait(started, token, "exchange_xy_%d_wait" % k)
        for nm, rxy in zip(names, landed):
            res = _adamw_shard(keep[nm], rxy, view(W[nm], nm), view(M[nm], nm), view(V[nm], nm), "adamw_" + nm)
            out[nm] = tuple(view(a, nm) for a in res)
            token = res[0]
        return token

    token = update(1, *pending[0], token)
    (g_in,), (rc_in,) = _exchange_wait(c_in, token, "exchange_c_4_wait", sources=True)
    (keep["w_in"], send_in), _ = _rs_combine(g_in, rc_in, pos, "rs_combine_w_in")
    xy_in, token = _exchange_start("xy", [send_in], "exchange_xy_3_start")
    token = update(2, *pending[1], token)

    params = {nm: (_rows128(W[nm]), _rows128(M[nm]), _rows128(V[nm])) for nm in SMALL}
    per, conv_g_rows, loss_sum = _adamw_small(small_all, seg, params, seg["conv_w"])
    for nm in SMALL:
        out[nm] = tuple(a.reshape(W[nm].shape) for a in per[nm])
    conv_g = lax.dynamic_slice_in_dim(conv_g_rows.reshape(3, bw), me * conv_w.shape[1], conv_w.shape[1], axis=1)
    out["conv_w"] = (conv_g,) + tuple(_adamw_one(conv_w, conv_g, m_conv_w, v_conv_w, "adamw_conv"))

    update(3, ("w_in",), xy_in, token[:1, :1] + out["conv_w"][1][:1, :1])

    loss = loss_sum[0, 0]
    res = [loss, grad_x[None]]
    for k in range(4):
        res += [out[nm][k] for nm in order]
    return tuple(res)
```

```python
import functools

import jax
import jax.numpy as jnp
from jax import lax
from jax.experimental import pallas as pl
from jax.experimental.pallas import tpu as pltpu

F32 = jnp.float32
BF16 = jnp.bfloat16
SDS = jax.ShapeDtypeStruct
MESH = pl.DeviceIdType.MESH

EPS = 1e-6
N_DEV = 8
HEADS = 4
CHUNK = 128
HALO = 16
SUB = 8
LANES = 128
TOKEN_TILE = 512
ROW_CHUNK = 256
RELAY_AT = 0.7

ADAM_LR = 0.001
ADAM_B1 = 0.9
ADAM_B2 = 0.999
ADAM_EPS = 1e-08
ADAM_WD = 0.01
ADAM_STEP = 10

BIG = ("w_in", "w_out", "w_q", "w_kv", "w_o", "w_gate_up", "w_down")
TRANSPOSED = ("w_in", "w_gate_up")
SMALL = ("ln_mix_g", "sgu_ln_g", "sgu_ln_b", "w_spatial", "b_spatial", "grp_norm_a", "grp_norm_b",
         "ln_attn_g", "ln_mem_g", "ln_ffn_g", "ln_final_g")


class _Exchange:
    def __init__(self, ins, out_shape, sems, start, finish, relay=None):
        self.ins, self.out_shape, self.sems = list(ins), list(out_shape), list(sems)
        self.start, self.finish, self.relay = start, finish, relay


def _pcall(body, carry=(), n_prefetch=0, **kw):
    if carry:
        return functools.partial(_carrying_call, body, tuple(carry), n_prefetch, kw)
    if n_prefetch:
        kw["grid_spec"] = pltpu.PrefetchScalarGridSpec(
            num_scalar_prefetch=n_prefetch, grid=kw.pop("grid"), in_specs=kw.pop("in_specs"),
            out_specs=kw.pop("out_specs"), scratch_shapes=kw.pop("scratch_shapes", ()))
    return pl.pallas_call(body, **kw)


def _carrying_call(body, carry, n_prefetch, kw, *args):
    kw = dict(kw)
    out_shape = kw.pop("out_shape")
    single = not isinstance(out_shape, (tuple, list))
    outs_shape = (out_shape,) if single else tuple(out_shape)
    out_specs = kw.pop("out_specs")
    out_specs = [out_specs] if single else list(out_specs)
    in_specs = list(kw.pop("in_specs"))
    scratch = list(kw.pop("scratch_shapes", ()))
    grid = tuple(kw.get("grid", ()))
    n_in, n_out, n_scr = len(args), len(outs_shape), len(scratch)

    def split(refs, k, counts):
        parts = []
        for cnt in counts:
            parts.append(refs[k:k + cnt])
            k += cnt
        return parts, k

    def wrapped(*refs):
        cins, k = split(refs, n_in, [len(p.ins) for p in carry])
        outs = refs[k:k + n_out]
        couts, k = split(refs, k + n_out, [len(p.out_shape) for p in carry])
        scr = refs[k:k + n_scr]
        csems, _ = split(refs, k + n_scr, [len(p.sems) for p in carry])
        first, last = True, True
        for a, g in enumerate(grid):
            first = (pl.program_id(a) == 0) & first
            last = (pl.program_id(a) == g - 1) & last

        def start_all():
            for p, ci, co, cs in zip(carry, cins, couts, csems):
                p.start(ci, co, cs)

        def relay_all():
            for p, ci, co, cs in zip(carry, cins, couts, csems):
                if p.relay is not None:
                    p.relay(ci, co, cs)

        def finish_all():
            for p, ci, co, cs in zip(carry, cins, couts, csems):
                p.finish(ci, co, cs)

        if len(grid) == 1:
            relay_now = pl.program_id(0) == min(int(RELAY_AT * grid[0]), grid[0] - 1)
        else:
            relay_now = last
        start_all() if not grid else pl.when(first)(start_all)
        relay_all() if not grid else pl.when(relay_now)(relay_all)
        body(*refs[:n_in], *outs, *scr)
        finish_all() if not grid else pl.when(last)(finish_all)

    c_in = [a for p in carry for a in p.ins]
    c_out = [s for p in carry for s in p.out_shape]
    c_sems = [s for p in carry for s in p.sems]
    res = _pcall(wrapped, n_prefetch=n_prefetch, out_shape=outs_shape + tuple(c_out),
                 in_specs=in_specs + _hbm_specs(len(c_in)), out_specs=out_specs + _hbm_specs(len(c_out)),
                 scratch_shapes=scratch + c_sems, **kw)(*args, *c_in)
    own = res[0] if single else tuple(res[:n_out])
    landed, k = [], n_out
    for p in carry:
        landed.append(list(res[k:k + len(p.out_shape)]))
        k += len(p.out_shape)
    return own, landed


def _hbm_specs(n):
    return [pl.BlockSpec(memory_space=pl.ANY)] * n


def _hosted(body, carry, **kw):
    if carry:
        return _pcall(body, carry=carry, **kw)
    call = _pcall(body, **kw)
    return lambda *args: (call(*args), [])


def _run_exchanges(parts, name):
    def body(*refs):
        pass

    _, landed = _pcall(body, carry=parts, out_shape=(), in_specs=[], out_specs=[], name=name)()
    return landed


def _arb(n):
    return pltpu.CompilerParams(dimension_semantics=("arbitrary",) * n)


def _tile(n, target, mult):
    best = None
    for t in range(mult, min(n, target) + 1, mult):
        if n % t == 0:
            best = t
    return n if best is None else best


def _round_up(n, m):
    return (n + m - 1) // m * m


def _dot(a, b):
    return jnp.dot(a, b, preferred_element_type=F32)


def _dot_nt(a, b):
    return lax.dot_general(a, b, (((1,), (1,)), ((), ())), preferred_element_type=F32)


def _dot_tn(a, b):
    return lax.dot_general(a, b, (((0,), (0,)), ((), ())), preferred_element_type=F32)


def _rstd(x):
    return lax.rsqrt(jnp.mean(x * x, axis=-1, keepdims=True) + EPS)


def _rms_bwd(dy, x, r, g):
    gdy = dy * g
    proj = jnp.sum(gdy * x, axis=-1, keepdims=True) * (1.0 / x.shape[-1])
    dx = r * gdy - x * (r * r * r) * proj
    dg = jnp.sum(dy * (x * r), axis=0, keepdims=True)
    return dx, dg


_GELU_C = 0.7978845608028654
_GELU_A = 0.044715


def _gelu(x):
    t = jnp.tanh(_GELU_C * (x + _GELU_A * x * x * x))
    return 0.5 * x * (1.0 + t), t


def _gelu_grad(x, t):
    return 0.5 * (1.0 + t) + 0.5 * x * (1.0 - t * t) * (_GELU_C * (1.0 + 3.0 * _GELU_A * x * x))


def _sigmoid(x):
    return 1.0 / (1.0 + jnp.exp(-x))


def _softmax(s):
    m = jnp.max(s, axis=-1, keepdims=True)
    e = jnp.exp(s - m)
    return e / jnp.sum(e, axis=-1, keepdims=True)


def _adamw(w, g, m, v):
    m = ADAM_B1 * m + (1.0 - ADAM_B1) * g
    v = ADAM_B2 * v + (1.0 - ADAM_B2) * (g * g)
    m_hat = m / (1.0 - ADAM_B1 ** ADAM_STEP)
    v_hat = v / (1.0 - ADAM_B2 ** ADAM_STEP)
    delta = -ADAM_LR * (m_hat / (jnp.sqrt(v_hat) + ADAM_EPS) + ADAM_WD * w)
    return delta, m, v


def _tril_mask():
    t = lax.broadcasted_iota(jnp.int32, (CHUNK, CHUNK), 0)
    s = lax.broadcasted_iota(jnp.int32, (CHUNK, CHUNK), 1)
    return (s <= t).astype(F32)


def _sgu_forward(ha, lng, lnb, wm, bt, mixed_s, mix=True):
    aw = ha.shape[1] // 2
    hd = aw // HEADS
    a, th = _gelu(ha)
    u = a[:, :aw]
    v = a[:, aw:]
    mu = jnp.mean(v, axis=-1, keepdims=True)
    vc = v - mu
    rl = lax.rsqrt(jnp.mean(vc * vc, axis=-1, keepdims=True) + EPS)
    xhat = vc * rl
    vln = (xhat * lng + lnb).astype(BF16)
    for n in range(ha.shape[0] // CHUNK if mix else 0):
        rows = slice(n * CHUNK, (n + 1) * CHUNK)
        for h in range(HEADS):
            cols = slice(h * hd, (h + 1) * hd)
            mixed_s[rows, cols] = _dot(wm[h], vln[rows, cols]) + bt[:, h:h + 1]
    return th, u, xhat, rl, vln


def _conv_taps(zext):
    return pltpu.roll(zext, 2, 0), pltpu.roll(zext, 1, 0)


def _kv_forward(mem, g_mem, w_kv):
    ml, d = mem.shape
    xd = w_kv.shape[2]

    def body(mem_ref, g_ref, w_ref, memn_ref, kv_ref):
        x = mem_ref[...]
        memn = (x * _rstd(x) * g_ref[...]).astype(BF16)
        memn_ref[...] = memn
        for j in range(2 * HEADS):
            kv_ref[j] = _dot(memn, w_ref[j]).astype(BF16)

    return _pcall(body, out_shape=(SDS((ml, d), BF16), SDS((2 * HEADS, ml, xd), BF16)), name="kv_forward")(mem, g_mem, w_kv)


def _in_forward(x, g, w_in_t, tm, carry=()):
    s, d = x.shape
    n_in = w_in_t.shape[0]

    def body(x_ref, g_ref, w_ref, xn_ref, h_ref):
        xv = x_ref[...]
        xn = (xv * _rstd(xv) * g_ref[...]).astype(BF16)
        xn_ref[...] = xn
        h_ref[...] = _dot_nt(xn, w_ref[...])

    return _hosted(
        body, carry, grid=(s // tm,),
        in_specs=[pl.BlockSpec((tm, d), lambda i: (i, 0)), pl.BlockSpec((1, d), lambda i: (0, 0)),
                  pl.BlockSpec((n_in, d), lambda i: (0, 0))],
        out_specs=[pl.BlockSpec((tm, d), lambda i: (i, 0)), pl.BlockSpec((tm, n_in), lambda i: (i, 0))],
        out_shape=(SDS((s, d), BF16), SDS((s, n_in), F32)),
        compiler_params=_arb(1), name="in_forward")(x, g, w_in_t)


def _mix_forward(h, x, lng, lnb, w_sp, bt, conv_w, ga, gb, w_out, tm, carry=()):
    s, d = x.shape
    n_in = h.shape[1]
    aw = lng.shape[1]
    bw = d - aw
    in_a = 2 * aw
    hb_blocks = tm // HALO

    def body(h_ref, hprev_ref, x_ref, lng_ref, lnb_ref, wsp_ref, bt_ref, cw_ref, ga_ref, gb_ref, wout_ref,
             ycat_ref, x1_ref, mixed_s):
        i = pl.program_id(0)
        mask = _tril_mask()
        wm = [(wsp_ref[hh] * mask).astype(BF16) for hh in range(HEADS)]
        hv = h_ref[...]
        _, u, _, _, _ = _sgu_forward(hv[:, :in_a], lng_ref[...], lnb_ref[...], wm, bt_ref[...], mixed_s)
        sg = u * mixed_s[...]
        ycat_ref[:, :aw] = (sg * _rstd(sg) * ga_ref[...]).astype(BF16)

        gate_b = hv[:, in_a:in_a + bw]
        z = hv[:, in_a + bw:in_a + 2 * bw] * hv[:, in_a + 2 * bw:]
        hp = hprev_ref[...]
        zp = hp[:, in_a + bw:in_a + 2 * bw] * hp[:, in_a + 2 * bw:]
        zp = jnp.where(i == 0, 0.0, zp)
        zext = jnp.concatenate([zp, z], axis=0)
        z2, z1 = _conv_taps(zext)
        cw = cw_ref[...]
        conv = cw[0:1] * z2[HALO:] + cw[1:2] * z1[HALO:] + cw[2:3] * z
        sc = gate_b * conv
        ycat_ref[:, aw:] = (sc * _rstd(sc) * gb_ref[...]).astype(BF16)
        x1_ref[...] = x_ref[...] + _dot(ycat_ref[...], wout_ref[...])

    full = lambda shape: pl.BlockSpec(shape, lambda i: (0,) * len(shape))
    return _hosted(
        body, carry, grid=(s // tm,),
        in_specs=[pl.BlockSpec((tm, n_in), lambda i: (i, 0)),
                  pl.BlockSpec((HALO, n_in), lambda i: (jnp.maximum(i * hb_blocks - 1, 0), 0)),
                  pl.BlockSpec((tm, d), lambda i: (i, 0)),
                  full((1, aw)), full((1, aw)), full((HEADS, CHUNK, CHUNK)), full((CHUNK, HEADS)),
                  full((3, bw)), full((1, aw)), full((1, bw)), full((d, d))],
        out_specs=[pl.BlockSpec((tm, d), lambda i: (i, 0)), pl.BlockSpec((tm, d), lambda i: (i, 0)),
                   pl.BlockSpec((tm, aw), lambda i: (i, 0))],
        out_shape=(SDS((s, d), BF16), SDS((s, d), F32), SDS((s, aw), F32)),
        compiler_params=_arb(1), name="mix_forward")(h, h, x, lng, lnb, w_sp, bt, conv_w, ga, gb, w_out)


def _attn_forward(x1, g, w_q, kv, w_o, tm, carry=()):
    s, d = x1.shape
    _, ml, xd = kv.shape
    scale = xd ** -0.5

    def body(x1_ref, g_ref, wq_ref, kv_ref, wo_ref, xn_ref, q_ref, p_ref, o_ref, x2_ref):
        xv = x1_ref[...]
        xn = (xv * _rstd(xv) * g_ref[...]).astype(BF16)
        xn_ref[...] = xn
        q_ref[...] = _dot(xn, wq_ref[...]).astype(BF16)
        for hh in range(HEADS):
            cols = slice(hh * xd, (hh + 1) * xd)
            p = _softmax(_dot_nt(q_ref[:, cols], kv_ref[hh]) * scale).astype(BF16)
            p_ref[:, hh * ml:(hh + 1) * ml] = p
            o_ref[:, cols] = _dot(p, kv_ref[HEADS + hh]).astype(BF16)
        x2_ref[...] = xv + _dot(o_ref[...], wo_ref[...])

    tok = pl.BlockSpec((tm, d), lambda i: (i, 0))
    probs = pl.BlockSpec((tm, HEADS * ml), lambda i: (i, 0))
    return _hosted(
        body, carry, grid=(s // tm,),
        in_specs=[tok, pl.BlockSpec((1, d), lambda i: (0, 0)), pl.BlockSpec((d, d), lambda i: (0, 0)),
                  pl.BlockSpec((2 * HEADS, ml, xd), lambda i: (0, 0, 0)), pl.BlockSpec((d, d), lambda i: (0, 0))],
        out_specs=[tok, tok, probs, tok, tok],
        out_shape=(SDS((s, d), BF16), SDS((s, d), BF16), SDS((s, HEADS * ml), BF16), SDS((s, d), BF16), SDS((s, d), F32)),
        compiler_params=_arb(1), name="attn_forward")(x1, g, w_q, kv, w_o)


def _ffn_forward(x2, g, w_gu, w_down, tm):
    s, d = x2.shape
    _, nf, tf, _ = w_gu.shape

    def body(x2_ref, g_ref, wgu_ref, wd_ref, xn_ref, gu_ref, x3_ref):
        f = pl.program_id(1)

        @pl.when(f == 0)
        def _():
            xv = x2_ref[...]
            xn_ref[...] = (xv * _rstd(xv) * g_ref[...]).astype(BF16)
            x3_ref[...] = xv

        xn = xn_ref[...]
        gate = _dot_nt(xn, wgu_ref[0])
        up = _dot_nt(xn, wgu_ref[1])
        gu_ref[0] = gate.astype(BF16)
        gu_ref[1] = up.astype(BF16)
        act = (gate * _sigmoid(gate) * up).astype(BF16)
        x3_ref[...] += _dot(act, wd_ref[...])

    tok = pl.BlockSpec((tm, d), lambda i, f: (i, 0))
    return _pcall(
        body, grid=(s // tm, nf),
        in_specs=[tok, pl.BlockSpec((1, d), lambda i, f: (0, 0)),
                  pl.BlockSpec((2, None, tf, d), lambda i, f: (0, f, 0, 0)),
                  pl.BlockSpec((tf, d), lambda i, f: (f, 0))],
        out_specs=[tok, pl.BlockSpec((2, None, tm, tf), lambda i, f: (0, f, i, 0)), tok],
        out_shape=(SDS((s, d), BF16), SDS((2, nf, s, tf), BF16), SDS((s, d), F32)),
        compiler_params=_arb(2), name="ffn_forward")(x2, g, w_gu, w_down)


def _final_backward(x3, target, g_final, tm):
    s, d = x3.shape

    def body(x3_ref, tgt_ref, gf_ref, loss_ref, dgf_ref, dx3_ref, dx3b_ref):
        @pl.when(pl.program_id(0) == 0)
        def _():
            loss_ref[...] = jnp.zeros_like(loss_ref)
            dgf_ref[...] = jnp.zeros_like(dgf_ref)

        xv = x3_ref[...]
        r = _rstd(xv)
        diff = xv * r * gf_ref[...] - tgt_ref[...]
        loss_ref[...] += 0.5 * jnp.sum(jnp.sum(diff * diff, axis=-1, keepdims=True), axis=0, keepdims=True) * (1.0 / d)
        dx3, dgf = _rms_bwd(diff * (1.0 / d), xv, r, gf_ref[...])
        dgf_ref[...] += dgf
        dx3_ref[...] = dx3
        dx3b_ref[...] = dx3.astype(BF16)

    tok = pl.BlockSpec((tm, d), lambda i: (i, 0))
    vec = pl.BlockSpec((1, d), lambda i: (0, 0))
    return _pcall(
        body, grid=(s // tm,), in_specs=[tok, tok, vec],
        out_specs=[pl.BlockSpec((SUB, LANES), lambda i: (0, 0)), vec, tok, tok],
        out_shape=(SDS((SUB, LANES), F32), SDS((1, d), F32), SDS((s, d), F32), SDS((s, d), BF16)),
        compiler_params=_arb(1), name="final_backward")(x3, target, g_final)


def _swiglu_backward(dx3b, gu, w_gu, w_down, tm):
    s, d = dx3b.shape
    _, nf, tf, _ = w_gu.shape

    def body(dx3b_ref, gu_ref, wgu_ref, wd_ref, act_ref, dgu_ref, dxn_ref):
        @pl.when(pl.program_id(1) == 0)
        def _():
            dxn_ref[...] = jnp.zeros_like(dxn_ref)

        for r0 in range(0, tm, ROW_CHUNK):
            rows = slice(r0, r0 + ROW_CHUNK)
            dact = _dot_nt(dx3b_ref[rows, :], wd_ref[...])
            gv = gu_ref[0, rows, :].astype(F32)
            uv = gu_ref[1, rows, :].astype(F32)
            sg = _sigmoid(gv)
            silu = gv * sg
            act_ref[rows, :] = (silu * uv).astype(BF16)
            dgate = (dact * uv * (sg * (1.0 + gv * (1.0 - sg)))).astype(BF16)
            dup = (dact * silu).astype(BF16)
            dgu_ref[0, rows, :] = dgate
            dgu_ref[1, rows, :] = dup
            part = _dot(dgate, wgu_ref[0]) + _dot(dup, wgu_ref[1])
            dxn_ref[rows, :] += part

    tok = pl.BlockSpec((tm, d), lambda i, f: (i, 0))
    pair = pl.BlockSpec((2, None, tm, tf), lambda i, f: (0, f, i, 0))
    return _pcall(
        body, grid=(s // tm, nf),
        in_specs=[tok, pair, pl.BlockSpec((2, None, tf, d), lambda i, f: (0, f, 0, 0)),
                  pl.BlockSpec((tf, d), lambda i, f: (f, 0))],
        out_specs=[pl.BlockSpec((None, tm, tf), lambda i, f: (f, i, 0)), pair, tok],
        out_shape=(SDS((nf, s, tf), BF16), SDS((2, nf, s, tf), BF16), SDS((s, d), F32)),
        compiler_params=_arb(2), name="swiglu_backward")(dx3b, gu, w_gu, w_down)


def _attn_backward(dx3, dxn3, x2, g_ffn, x1, g, q, probs, kv, w_q, w_o, tm, carry=()):
    s, d = x1.shape
    _, ml, xd = kv.shape
    scale = xd ** -0.5

    def body(dx3_ref, dxn3_ref, x2_ref, g2_ref, x1_ref, g_ref, q_ref, p_ref, kv_ref, wq_ref, wo_ref,
             dx2b_ref, dq_ref, dx1_ref, dx1b_ref, dkv_ref, dg_ref, dg2_ref, do_s):
        i = pl.program_id(0)

        @pl.when(i == 0)
        def _():
            dkv_ref[...] = jnp.zeros_like(dkv_ref)
            dg_ref[...] = jnp.zeros_like(dg_ref)
            dg2_ref[...] = jnp.zeros_like(dg2_ref)

        x2v = x2_ref[...]
        dx2n, dg2 = _rms_bwd(dxn3_ref[...], x2v, _rstd(x2v), g2_ref[...])
        dg2_ref[...] += dg2
        dx2 = dx3_ref[...] + dx2n
        dx2b_ref[...] = dx2.astype(BF16)
        do_s[...] = _dot_nt(dx2b_ref[...], wo_ref[...]).astype(BF16)
        for hh in range(HEADS):
            kc = slice(hh * xd, (hh + 1) * xd)
            qh = q_ref[:, kc]
            kh = kv_ref[hh]
            doh = do_s[:, kc]
            pb = p_ref[:, hh * ml:(hh + 1) * ml]
            p = pb.astype(F32)
            dp = _dot_nt(doh, kv_ref[HEADS + hh])
            dkv_ref[HEADS + hh] += _dot_tn(pb, doh)
            ds = (p * (dp - jnp.sum(dp * p, axis=-1, keepdims=True)) * scale).astype(BF16)
            dq_ref[:, kc] = _dot(ds, kh).astype(BF16)
            dkv_ref[hh] += _dot_tn(ds, qh)
        dxn = _dot_nt(dq_ref[...], wq_ref[...])
        xv = x1_ref[...]
        dx, dg = _rms_bwd(dxn, xv, _rstd(xv), g_ref[...])
        dg_ref[...] += dg
        dx1 = dx2 + dx
        dx1_ref[...] = dx1
        dx1b_ref[...] = dx1.astype(BF16)

    tok = pl.BlockSpec((tm, d), lambda i: (i, 0))
    vec = pl.BlockSpec((1, d), lambda i: (0, 0))
    sq = pl.BlockSpec((d, d), lambda i: (0, 0))
    kvs = pl.BlockSpec((2 * HEADS, ml, xd), lambda i: (0, 0, 0))
    return _hosted(
        body, carry, grid=(s // tm,),
        in_specs=[tok, tok, tok, vec, tok, vec, tok, pl.BlockSpec((tm, HEADS * ml), lambda i: (i, 0)), kvs, sq, sq],
        out_specs=[tok, tok, tok, tok, kvs, vec, vec],
        out_shape=(SDS((s, d), BF16), SDS((s, d), BF16), SDS((s, d), F32), SDS((s, d), BF16),
                   SDS((2 * HEADS, ml, xd), F32), SDS((1, d), F32), SDS((1, d), F32)),
        scratch_shapes=[pltpu.VMEM((tm, d), BF16)],
        compiler_params=_arb(1), name="attn_backward")(dx3, dxn3, x2, g_ffn, x1, g, q, probs, kv, w_q, w_o)


def _kv_backward(dkv, memn, mem, g_mem, w_kv):
    ml, d = mem.shape
    xd = w_kv.shape[2]

    def body(dkv_ref, memn_ref, mem_ref, g_ref, w_ref, dw_ref, dg_ref):
        dmemn = jnp.zeros((ml, d), F32)
        for j in range(2 * HEADS):
            dkvb = dkv_ref[j].astype(BF16)
            dw_ref[j] = _dot_tn(memn_ref[...], dkvb)
            dmemn = dmemn + _dot_nt(dkvb, w_ref[j])
        x = mem_ref[...]
        dg_ref[...] = jnp.sum(dmemn * (x * _rstd(x)), axis=0, keepdims=True)

    return _pcall(body, out_shape=(SDS((2 * HEADS, d, xd), F32), SDS((1, d), F32)), name="kv_backward")(dkv, memn, mem, g_mem, w_kv)


def _mix_backward(dx1, x, g_mix, h, mixed, lng, lnb, w_sp, bt, conv_w, ga, gb, w_out, w_in, tm, carry=()):
    s, d = x.shape
    n_in = h.shape[1]
    aw = lng.shape[1]
    bw = d - aw
    hd = aw // HEADS
    in_a = 2 * aw
    hb_blocks = tm // HALO
    last_blk = s // HALO - 1
    nt = s // tm
    tc = tm
    te = tc + HALO
    tee = tc + 2 * HALO

    def body(dx1_ref, dx1n_ref, x_ref, gm_ref, h_ref, hp_ref, hn_ref, mixed_ref, lng_ref, lnb_ref, wsp_ref, bt_ref, cw_ref,
             ga_ref, gb_ref, wout_ref, win_ref,
             dh_ref, dx_ref, dga_ref, dgb_ref, dcw_ref, dlng_ref, dlnb_ref, dwsp_ref, dbs_ref, dgm_ref,
             dvln_s):
        i = pl.program_id(0)

        @pl.when(i == 0)
        def _():
            for ref in (dga_ref, dgb_ref, dcw_ref, dlng_ref, dlnb_ref, dwsp_ref, dbs_ref, dgm_ref):
                ref[...] = jnp.zeros_like(ref)

        mask = _tril_mask()
        wm = [(wsp_ref[hh] * mask).astype(BF16) for hh in range(HEADS)]
        cw = cw_ref[...]

        def chain(r0):
            rows = slice(r0, r0 + tc)
            first, last = r0 == 0, r0 + tc == tm
            hv = h_ref[rows, :]
            dx1 = dx1_ref[rows, :]
            dx1n = dx1n_ref[...] if last else dx1_ref[r0 + tc:r0 + tc + HALO, :]
            hp = hp_ref[:, in_a:] if first else h_ref[r0 - HALO:r0, in_a:]
            hn = hn_ref[:, in_a:] if last else h_ref[r0 + tc:r0 + tc + HALO, in_a:]
            dx1e = jnp.concatenate([dx1, dx1n], axis=0).astype(BF16)
            dycat = _dot_nt(dx1e, wout_ref[...])

            hbe = jnp.concatenate([hp, hv[:, in_a:], hn], axis=0)
            row = lax.broadcasted_iota(jnp.int32, (tee, 1), 0)
            zext = hbe[:, bw:2 * bw] * hbe[:, 2 * bw:]
            if first:
                zext = jnp.where((i == 0) & (row < HALO), 0.0, zext)
            z2e, z1e = _conv_taps(zext)
            conv_e = (cw[0:1] * z2e + cw[1:2] * z1e + cw[2:3] * zext)[HALO:]
            gate_b_e = hbe[HALO:, :bw]
            sc_e = gate_b_e * conv_e
            rb = _rstd(sc_e)
            dyb = dycat[:, aw:]
            gdy = dyb * gb_ref[...]
            dsc_e = rb * gdy - sc_e * (rb * rb * rb) * (jnp.sum(gdy * sc_e, axis=-1, keepdims=True) * (1.0 / bw))
            dgb_ref[...] += jnp.sum((dyb * (sc_e * rb))[:tc], axis=0, keepdims=True)
            dconv_e = dsc_e * gate_b_e
            if last:
                dconv_e = jnp.where((i == nt - 1) & (row[:te] >= tc), 0.0, dconv_e)
            dconv = dconv_e[:tc]
            dc1 = pltpu.roll(dconv_e, te - 1, 0)[:tc]
            dc2 = pltpu.roll(dconv_e, te - 2, 0)[:tc]
            dz = cw[2:3] * dconv + cw[1:2] * dc1 + cw[0:1] * dc2
            z = zext[HALO:HALO + tc]
            z1 = z1e[HALO:HALO + tc]
            z2 = z2e[HALO:HALO + tc]
            dcw_ref[0:1, :] += jnp.sum(dconv * z2, axis=0, keepdims=True)
            dcw_ref[1:2, :] += jnp.sum(dconv * z1, axis=0, keepdims=True)
            dcw_ref[2:3, :] += jnp.sum(dconv * z, axis=0, keepdims=True)
            dh_ref[rows, in_a:in_a + bw] = (dsc_e[:tc] * conv_e[:tc]).astype(BF16)
            dh_ref[rows, in_a + bw:in_a + 2 * bw] = (dz * hv[:, in_a + 2 * bw:]).astype(BF16)
            dh_ref[rows, in_a + 2 * bw:] = (dz * hv[:, in_a + bw:in_a + 2 * bw]).astype(BF16)

            ha = hv[:, :in_a]
            mixed_c, dvln_c = mixed_ref.at[rows, :], dvln_s.at[rows, :]
            th, u, xhat, rl, vln = _sgu_forward(ha, lng_ref[...], lnb_ref[...], wm, bt_ref[...], mixed_c, mix=False)
            mixed = mixed_c[...]
            sg = u * mixed
            dsg, dga = _rms_bwd(dycat[:tc, :aw], sg, _rstd(sg), ga_ref[...])
            dga_ref[...] += dga
            du = dsg * mixed
            dmixed = dsg * u
            dmb = dmixed.astype(BF16)
            for n in range(tc // CHUNK):
                blk = slice(n * CHUNK, (n + 1) * CHUNK)
                dbs_ref[...] += dmixed[blk]
                for hh in range(HEADS):
                    cols = slice(hh * hd, (hh + 1) * hd)
                    dvln_c[blk, cols] = _dot_tn(wm[hh], dmb[blk, cols])
                    dwsp_ref[hh] += mask * _dot_nt(dmb[blk, cols], vln[blk, cols])
            dvln = dvln_c[...]
            dlng_ref[...] += jnp.sum(dvln * xhat, axis=0, keepdims=True)
            dlnb_ref[...] += jnp.sum(dvln, axis=0, keepdims=True)
            dxh = dvln * lng_ref[...]
            dv = rl * (dxh - jnp.mean(dxh, axis=-1, keepdims=True) - xhat * jnp.mean(dxh * xhat, axis=-1, keepdims=True))
            dh_ref[rows, :in_a] = (jnp.concatenate([du, dv], axis=-1) * _gelu_grad(ha, th)).astype(BF16)

            dxn = _dot(dh_ref[rows, :], win_ref[...])
            xv = x_ref[rows, :]
            dx, dgm = _rms_bwd(dxn, xv, _rstd(xv), gm_ref[...])
            dgm_ref[...] += dgm
            dx_ref[rows, :] = dx1 + dx

        for r0 in range(0, tm, tc):
            chain(r0)

    full = lambda shape: pl.BlockSpec(shape, lambda i: (0,) * len(shape))
    tok = pl.BlockSpec((tm, d), lambda i: (i, 0))
    nxt = lambda i: (jnp.minimum((i + 1) * hb_blocks, last_blk), 0)
    prv = lambda i: (jnp.maximum(i * hb_blocks - 1, 0), 0)
    return _hosted(
        body, carry, grid=(nt,),
        in_specs=[tok, pl.BlockSpec((HALO, d), nxt), tok, full((1, d)),
                  pl.BlockSpec((tm, n_in), lambda i: (i, 0)), pl.BlockSpec((HALO, n_in), prv), pl.BlockSpec((HALO, n_in), nxt),
                  pl.BlockSpec((tm, aw), lambda i: (i, 0)),
                  full((1, aw)), full((1, aw)), full((HEADS, CHUNK, CHUNK)), full((CHUNK, HEADS)), full((3, bw)),
                  full((1, aw)), full((1, bw)), full((d, d)), full((n_in, d))],
        out_specs=[pl.BlockSpec((tm, n_in), lambda i: (i, 0)), tok,
                   full((1, aw)), full((1, bw)), full((SUB, bw)), full((1, aw)), full((1, aw)),
                   full((HEADS, CHUNK, CHUNK)), full((CHUNK, aw)), full((1, d))],
        out_shape=(SDS((s, n_in), BF16), SDS((s, d), F32),
                   SDS((1, aw), F32), SDS((1, bw), F32), SDS((SUB, bw), F32), SDS((1, aw), F32), SDS((1, aw), F32),
                   SDS((HEADS, CHUNK, CHUNK), F32), SDS((CHUNK, aw), F32), SDS((1, d), F32)),
        scratch_shapes=[pltpu.VMEM((tm, aw), F32)],
        compiler_params=_arb(1), name="mix_backward")(dx1, dx1, x, g_mix, h, h, h, mixed, lng, lnb, w_sp, bt, conv_w, ga, gb,
                                                      w_out, w_in)


def _bias_grad(dbs):
    aw = dbs.shape[1]
    hd = aw // HEADS

    def body(dbs_ref, out_ref):
        ones = jnp.ones((SUB, hd), F32)
        for hh in range(HEADS):
            r = lax.dot_general(ones, dbs_ref[:, hh * hd:(hh + 1) * hd], (((1,), (1,)), ((), ())),
                                precision=lax.Precision.HIGHEST, preferred_element_type=F32)
            out_ref[hh:hh + 1, :] = r[0:1]

    return _pcall(body, out_shape=SDS((HEADS, CHUNK), F32), name="bias_grad")(dbs)


def _wgrad_body(a_ref, b_ref, o_ref):
    o_ref[...] = _dot_tn(a_ref[...], b_ref[...])


def _wgrad(a, b, name, carry=()):
    k, m = a.shape
    n = b.shape[1]
    tm = _tile(m, 512, LANES)
    tn = _tile(n, 1024, LANES)
    return _hosted(
        functools.partial(_wgrad_body), carry, grid=(m // tm, n // tn),
        in_specs=[pl.BlockSpec((k, tm), lambda i, j: (0, i)), pl.BlockSpec((k, tn), lambda i, j: (0, j))],
        out_specs=pl.BlockSpec((tm, tn), lambda i, j: (i, j)),
        out_shape=SDS((m, n), F32), compiler_params=_arb(2), name=name)(a, b)


def _wgrad_blocked_lhs(a, b, name, carry=()):
    nb, k, t = a.shape
    n = b.shape[1]
    tn = _tile(n, 1024, LANES)
    return _hosted(
        functools.partial(_wgrad_body), carry, grid=(nb, n // tn),
        in_specs=[pl.BlockSpec((None, k, t), lambda i, j: (i, 0, 0)), pl.BlockSpec((k, tn), lambda i, j: (0, j))],
        out_specs=pl.BlockSpec((t, tn), lambda i, j: (i, j)),
        out_shape=SDS((nb * t, n), F32), compiler_params=_arb(2), name=name)(a, b)


def _wgrad_blocked_rhs(a, b, name, carry=()):
    k, m = a.shape
    nb, _, t = b.shape
    tm = _tile(m, 512, LANES)
    return _hosted(
        functools.partial(_wgrad_body), carry, grid=(m // tm, nb),
        in_specs=[pl.BlockSpec((k, tm), lambda i, j: (0, i)), pl.BlockSpec((None, k, t), lambda i, j: (j, 0, 0))],
        out_specs=pl.BlockSpec((None, tm, t), lambda i, j: (j, i, 0)),
        out_shape=SDS((nb, m, t), F32), compiler_params=_arb(2), name=name)(a, b)


def _unblock_cols(wb, name, carry=()):
    nb, r, t = wb.shape
    tr = _tile(r, 256, 16)

    def body(w_ref, o_ref):
        o_ref[...] = jnp.concatenate([w_ref[j].astype(F32) for j in range(nb)], axis=-1).astype(o_ref.dtype)

    return _hosted(
        body, carry, grid=(r // tr,),
        in_specs=[pl.BlockSpec((nb, tr, t), lambda i: (0, i, 0))], out_specs=pl.BlockSpec((tr, nb * t), lambda i: (i, 0)),
        out_shape=SDS((r, nb * t), wb.dtype), compiler_params=_arb(1), name=name)(wb)


def _block_cols(w, nb, name, carry=()):
    r, n = w.shape
    t = n // nb
    tr = _tile(r, 256, 16)

    def body(w_ref, o_ref):
        wv = w_ref[...]
        for j in range(nb):
            o_ref[j] = wv[:, j * t:(j + 1) * t]

    return _hosted(
        body, carry, grid=(r // tr,),
        in_specs=[pl.BlockSpec((tr, n), lambda i: (i, 0))], out_specs=pl.BlockSpec((nb, tr, t), lambda i: (0, i, 0)),
        out_shape=SDS((nb, r, t), w.dtype), compiler_params=_arb(1), name=name)(w)


def _place():
    x, y, c = lax.axis_index("x"), lax.axis_index("y"), lax.axis_index("c")
    return x, y, c, [(1 - x, y), (x, 1 - y), (1 - x, 1 - y)]


def _all_gather(shards):
    n = len(shards)
    slots = 9
    cut = [(s.shape[0] // 32) * 16 for s in shards]

    def build(ins, outs, sems):
        send_sems, recv_sems, local_sems = sems
        x, y, c, _ = _place()
        me, sib, xn, yn, dg = (x, y, c), (x, y, 1 - c), (1 - x, y, c), (x, 1 - y, c), (1 - x, 1 - y, c)
        other = lambda p: (p[0], p[1], 1 - p[2])

        def rows(a, p, part=None):
            ref = outs[a].at[4 * p[0] + 2 * p[1] + p[2]]
            if part is None or cut[a] == 0:
                return ref if part in (None, 0) else None
            return ref.at[pl.ds(0, cut[a])] if part == 0 else ref.at[pl.ds(cut[a], shards[a].shape[0] - cut[a])]

        def copy(a, k, ref, to, src=None):
            if ref is None:
                return None
            return pltpu.make_async_remote_copy(
                src_ref=ref if src is None else src, dst_ref=ref, send_sem=send_sems.at[slots * a + k],
                recv_sem=recv_sems.at[slots * a + k], device_id=to, device_id_type=MESH)

        def real(cps):
            return [cp for cp in cps if cp is not None]

        class Copies:
            own = lambda a: [copy(a, 1, rows(a, me), xn, ins[a]), copy(a, 2, rows(a, me), yn, ins[a]),
                             copy(a, 0, rows(a, me), sib, ins[a])]
            local = lambda a: pltpu.make_async_copy(ins[a], rows(a, me), local_sems.at[a])
            from_x = lambda a: copy(a, 1, rows(a, xn), me)
            from_y = lambda a: copy(a, 2, rows(a, yn), me)
            after_x = lambda a: real([copy(a, 4, rows(a, xn, 1), yn), copy(a, 5, rows(a, xn), sib)])
            after_y = lambda a: real([copy(a, 3, rows(a, yn, 0), xn), copy(a, 6, rows(a, yn), sib)])
            diag_in = lambda a: real([copy(a, 3, rows(a, dg, 0), me), copy(a, 4, rows(a, dg, 1), me)])
            diag_on = lambda a: real([copy(a, 7, rows(a, dg, 0), sib), copy(a, 8, rows(a, dg, 1), sib)])
            from_sib = lambda a: real([copy(a, 0, rows(a, sib), me), copy(a, 5, rows(a, other(xn)), me),
                                       copy(a, 6, rows(a, other(yn)), me), copy(a, 7, rows(a, other(dg), 0), me),
                                       copy(a, 8, rows(a, other(dg), 1), me)])

        return Copies

    def start(ins, outs, sems):
        cps = build(ins, outs, sems)
        for a in range(n):
            for cp in cps.own(a):
                cp.start()
        for a in range(n):
            cps.local(a).start()

    def relay(ins, outs, sems):
        cps = build(ins, outs, sems)
        for a in range(n):
            cps.from_x(a).wait_recv()
            for cp in cps.after_x(a):
                cp.start()
            cps.from_y(a).wait_recv()
            for cp in cps.after_y(a):
                cp.start()

    def finish(ins, outs, sems):
        cps = build(ins, outs, sems)
        for a in range(n):
            for arrived, onward in zip(cps.diag_in(a), cps.diag_on(a)):
                arrived.wait_recv()
                onward.start()
        for a in range(n):
            for cp in cps.from_sib(a):
                cp.wait_recv()
            for cp in cps.own(a) + cps.after_x(a) + cps.after_y(a) + cps.diag_on(a):
                cp.wait_send()
            cps.local(a).wait()

    return _Exchange(shards, [SDS((N_DEV,) + s.shape, s.dtype) for s in shards],
                     [pltpu.SemaphoreType.DMA((slots * n,)), pltpu.SemaphoreType.DMA((slots * n,)),
                      pltpu.SemaphoreType.DMA((n,))], start, finish, relay)


def _swap_exchange(ins, out_shape, per, copies):
    def start(i, o, sems):
        for cp in copies(i, o, sems):
            cp.start()

    def finish(i, o, sems):
        for cp in copies(i, o, sems):
            cp.wait()

    n = per * len(ins)
    return _Exchange(ins, out_shape, [pltpu.SemaphoreType.DMA((n,)), pltpu.SemaphoreType.DMA((n,))], start, finish)


def _exchange_c(gs):
    def copies(ins, outs, sems):
        x, y, c, _ = _place()
        return [pltpu.make_async_remote_copy(
                    src_ref=ins[a].at[2 * k + 1 - c], dst_ref=outs[a].at[k],
                    send_sem=sems[0].at[4 * a + k], recv_sem=sems[1].at[4 * a + k],
                    device_id=(x, y, 1 - c), device_id_type=MESH)
                for a in range(len(gs)) for k in range(4)]

    return _swap_exchange(gs, [SDS((4,) + g.shape[1:], g.dtype) for g in gs], 4, copies)


def _exchange_xy(sends):
    def copies(ins, outs, sems):
        x, y, c, chips = _place()
        return [pltpu.make_async_remote_copy(
                    src_ref=ins[a].at[t], dst_ref=outs[a].at[t],
                    send_sem=sems[0].at[3 * a + t], recv_sem=sems[1].at[3 * a + t],
                    device_id=(*chips[t], c), device_id_type=MESH)
                for a in range(len(sends)) for t in range(3)]

    return _swap_exchange(sends, [SDS(s.shape, s.dtype) for s in sends], 3, copies)


def _rs_combine(g, recv, pos, name, carry=()):
    _, r, cdim = g.shape
    tr = _tile(r, 256, 16)

    def body(pos_ref, g0, r0, g1, r1, g2, r2, g3, r3, keep_ref, send_ref):
        keep_ref[...] = g0[...] + r0[...]
        send_ref[0] = (g1[...] + r1[...]).astype(BF16)
        send_ref[1] = (g2[...] + r2[...]).astype(BF16)
        send_ref[2] = (g3[...] + r3[...]).astype(BF16)

    def k_of(p, t):
        px = p[0] if t in (0, 2) else 1 - p[0]
        py = p[1] if t in (0, 1) else 1 - p[1]
        return 2 * px + py

    blk = (None, tr, cdim)
    in_specs = []
    for t in range(4):
        in_specs.append(pl.BlockSpec(blk, functools.partial(lambda j, p, t: (2 * k_of(p, t) + p[2], j, 0), t=t)))
        in_specs.append(pl.BlockSpec(blk, functools.partial(lambda j, p, t: (k_of(p, t), j, 0), t=t)))
    return _hosted(
        body, carry, n_prefetch=1, out_shape=(SDS((r, cdim), F32), SDS((3, r, cdim), BF16)),
        grid=(r // tr,), in_specs=in_specs,
        out_specs=[pl.BlockSpec((tr, cdim), lambda j, p: (j, 0)), pl.BlockSpec((3, tr, cdim), lambda j, p: (0, j, 0))],
        compiler_params=_arb(1), name=name)(pos, g, recv, g, recv, g, recv, g, recv)


def _adamw_shard(keep, recv, w, m, v, name):
    r, cdim = w.shape
    tr = _tile(r, 256, 16)

    def body(k_ref, r_ref, w_ref, m_ref, v_ref, g_ref, d_ref, nm_ref, nv_ref):
        g = ((k_ref[...] + r_ref[0].astype(F32)) + r_ref[1].astype(F32)) + r_ref[2].astype(F32)
        g_ref[...] = g
        d_ref[...], nm_ref[...], nv_ref[...] = _adamw(w_ref[...], g, m_ref[...], v_ref[...])

    blk = pl.BlockSpec((tr, cdim), lambda j: (j, 0))
    out = SDS((r, cdim), F32)
    return _pcall(body, grid=(r // tr,), in_specs=[blk, pl.BlockSpec((3, tr, cdim), lambda j: (0, j, 0)), blk, blk, blk],
                  out_specs=[blk] * 4, out_shape=(out,) * 4, compiler_params=_arb(1), name=name)(keep, recv, w, m, v)


_HBM = pl.BlockSpec(memory_space=pltpu.HBM)
_SEM = pl.BlockSpec(memory_space=pltpu.SEMAPHORE)
_SPLIT = pltpu.CompilerParams(has_side_effects=pltpu.SideEffectType.DATAFLOW_SIDE_EFFECTING)


def _split_copies(kind, n, refs):
    srcs, lands, (send_sems, recv_sems) = refs[:n], refs[n:2 * n], refs[2 * n:2 * n + 2]
    x, y, c, chips = _place()
    per = _SPLIT_COPIES[kind]
    if kind == "xy":
        ends = lambda a, t: (srcs[a].at[t], lands[a].at[t], (*chips[t], c))
    else:
        ends = lambda a, k: (srcs[a].at[2 * k + 1 - c], lands[a].at[k], (x, y, 1 - c))
    cps = []
    for a in range(n):
        for t in range(per):
            src, dst, to = ends(a, t)
            cps.append(pltpu.make_async_remote_copy(src_ref=src, dst_ref=dst, send_sem=send_sems.at[per * a + t],
                                                    recv_sem=recv_sems.at[per * a + t], device_id=to, device_id_type=MESH))
    return cps


_SPLIT_COPIES = {"xy": 3, "c": 4}


def _exchange_start(kind, arrays, name, after=None):
    n = len(arrays)
    order = [] if after is None else [after]

    def body(*refs):
        refs = refs[:2 * n] + refs[2 * n + len(order):]
        for cp in _split_copies(kind, n, refs):
            cp.start()
        refs[-1][...] = jnp.zeros_like(refs[-1])

    hbm = lambda a: pltpu.with_memory_space_constraint(a, pltpu.HBM)
    land = [a.shape if kind == "xy" else (4,) + a.shape[1:] for a in arrays]
    bufs = [pltpu.HBM(a.shape, a.dtype) for a in arrays] + [pltpu.HBM(s, a.dtype) for s, a in zip(land, arrays)]
    sems = pltpu.SemaphoreType.DMA((_SPLIT_COPIES[kind] * n,))
    res = _pcall(
        body, name=name, out_shape=(sems, sems, *bufs, SDS((SUB, LANES), F32)),
        in_specs=[_HBM] * (2 * n) + _hbm_specs(len(order)),
        out_specs=[_SEM, _SEM] + [_HBM] * (2 * n) + [pl.BlockSpec(memory_space=pltpu.VMEM)],
        input_output_aliases={k: 2 + k for k in range(2 * n)}, compiler_params=_SPLIT)(
            *[hbm(a) for a in arrays], *[hbm(lax.empty(s, a.dtype)) for s, a in zip(land, arrays)], *order)
    return (kind, n, res[:-1]), res[-1]


def _exchange_wait(started, after, name, sources=False):
    kind, n, (send_sems, recv_sems, *bufs) = started

    def body(*refs):
        for cp in _split_copies(kind, n, refs):
            cp.wait_send()
            cp.wait_recv()

    shapes = [pltpu.HBM(b.shape, b.dtype) for b in bufs]
    res = _pcall(
        body, name=name, out_shape=tuple(shapes),
        in_specs=[_HBM] * (2 * n) + [_SEM, _SEM, pl.BlockSpec(memory_space=pl.ANY)], out_specs=[_HBM] * (2 * n),
        input_output_aliases={k: k for k in range(2 * n)}, compiler_params=_SPLIT)(*bufs, send_sems, recv_sems, after)
    return (list(res[:n]), list(res[n:])) if sources else list(res[n:])


def _follow(token):
    nothing = lambda ins, outs, sems: None
    return _Exchange([token], [], [], nothing, nothing)


def _adamw_small(gathered, seg, params, conv_rows):
    names = list(params)
    c0, cn = conv_rows

    def body(*refs):
        gat_ref = refs[0]
        ins = refs[1:1 + 3 * len(names)]
        outs = refs[1 + 3 * len(names):]

        def total(r0, rn):
            tot = gat_ref[0, r0:r0 + rn, :]
            for dev in range(1, N_DEV):
                tot = tot + gat_ref[dev, r0:r0 + rn, :]
            return tot

        for k, nm in enumerate(names):
            g = total(*seg[nm])
            w_ref, m_ref, v_ref = ins[3 * k:3 * k + 3]
            g_ref, d_ref, nm_ref, nv_ref = outs[4 * k:4 * k + 4]
            g_ref[...] = g
            d_ref[...], nm_ref[...], nv_ref[...] = _adamw(w_ref[...], g, m_ref[...], v_ref[...])
        outs[-2][...] = total(c0, cn)
        outs[-1][...] = total(*seg["loss"])

    flat_in = [a for nm in names for a in params[nm]]
    out_shape = []
    for nm in names:
        out_shape += [SDS(params[nm][0].shape, F32)] * 4
    out_shape += [SDS((cn, LANES), F32), SDS((seg["loss"][1], LANES), F32)]
    res = _pcall(body, out_shape=tuple(out_shape), name="adamw_small")(gathered, *flat_in)
    per = {nm: res[4 * k:4 * k + 4] for k, nm in enumerate(names)}
    return per, res[-2], res[-1]


def _adamw_one(w, g, m, v, name):
    def body(w_ref, g_ref, m_ref, v_ref, d_ref, nm_ref, nv_ref):
        d_ref[...], nm_ref[...], nv_ref[...] = _adamw(w_ref[...], g_ref[...], m_ref[...], v_ref[...])

    return _pcall(body, out_shape=(SDS(w.shape, F32),) * 3, name=name)(w, g, m, v)


def _rows128(a):
    return a.reshape(-1, LANES)


def _pack_small(gs, loss_tile):
    seg, pieces, row = {}, [], 0
    for nm in SMALL + ("conv_w", "loss"):
        piece = loss_tile if nm == "loss" else _rows128(gs[nm])
        rn = _round_up(piece.shape[0], SUB)
        pieces.append(jnp.pad(piece, ((0, rn - piece.shape[0]), (0, 0))))
        seg[nm] = (row, piece.shape[0])
        row += rn
    return jnp.concatenate(pieces, axis=0), seg


def _step(x, mem, target, wb, conv_w, sp, pos):
    s, d = x.shape
    tm = min(TOKEN_TILE, s)
    tm_wide = min(2 * TOKEN_TILE, s)
    rows = lambda w8: w8.reshape(-1, w8.shape[2])
    shards = lambda g: g.reshape((N_DEV, -1) + g.shape[1:])
    bt = sp["b_spatial"].T

    (w_in8, conv8), = _run_exchanges([_all_gather([wb["w_in"], conv_w])], "gather_w_in")
    conv_full = conv8.transpose(1, 0, 2).reshape(3, -1)
    w_in_t = rows(w_in8)
    (xn1, h), ((w_out8, w_kv8, w_q8),) = _in_forward(
        x, sp["ln_mix_g"], w_in_t, tm, carry=[_all_gather([wb["w_out"], wb["w_kv"], wb["w_q"]])])
    w_out = rows(w_out8)
    (ycat, x1, mixed), ((w_o8, w_down8),) = _mix_forward(
        h, x, sp["sgu_ln_g"], sp["sgu_ln_b"], sp["w_spatial"], bt, conv_full, sp["grp_norm_a"], sp["grp_norm_b"], w_out, tm,
        carry=[_all_gather([wb["w_o"], wb["w_down"]])])
    w_q, w_o, w_down = rows(w_q8), rows(w_o8), rows(w_down8)
    memn, kv = _kv_forward(mem, sp["ln_mem_g"], w_kv8)
    (xn2, q, probs, o, x2), ((w_gu8,),) = _attn_forward(
        x1, sp["ln_attn_g"], w_q, kv, w_o, tm, carry=[_all_gather([wb["w_gate_up"]])])
    w_gu = w_gu8.reshape((2, N_DEV // 2) + w_gu8.shape[1:])
    xn3, gu, x3 = _ffn_forward(x2, sp["ln_ffn_g"], w_gu, w_down, tm_wide)

    loss, d_lnf, dx3, dx3b = _final_backward(x3, target, sp["ln_final_g"], tm_wide)
    act, dgu, dxn3 = _swiglu_backward(dx3b, gu, w_gu, w_down, tm_wide)
    g_gu, _ = _wgrad_blocked_lhs(dgu.reshape((N_DEV,) + dgu.shape[2:]), xn3, "wgrad_gate_up")
    g_gu = shards(g_gu)
    g_down, ((rc_gu,),) = _wgrad_blocked_lhs(act, dx3b, "wgrad_down", carry=[_exchange_c([g_gu])])
    g_down = shards(g_down)
    keep, pending = {}, []
    (keep["w_gate_up"], send_gu), _ = _rs_combine(g_gu, rc_gu, pos, "rs_combine_w_gate_up")
    started, token = _exchange_start("xy", [send_gu], "exchange_xy_1_start")
    pending.append((("w_gate_up",), started))
    c_down, token = _exchange_start("c", [g_down], "exchange_c_1_start", after=token)
    (dx2b, dq, dx1, dx1b, dkv, d_lnattn, d_lnffn), _ = _attn_backward(
        dx3, dxn3, x2, sp["ln_ffn_g"], x1, sp["ln_attn_g"], q, probs, kv, w_q, w_o, tm, carry=[_follow(token)])
    (g_down,), (rc_down,) = _exchange_wait(c_down, dx1b, "exchange_c_1_wait", sources=True)
    (keep["w_down"], send_down), _ = _rs_combine(g_down, rc_down, pos, "rs_combine_w_down")
    g_o, _ = _wgrad(o, dx2b, "wgrad_o")
    g_q, _ = _wgrad(xn2, dq, "wgrad_q")
    g_o, g_q = shards(g_o), shards(g_q)
    g_kv, d_lnmem = _kv_backward(dkv, memn, mem, sp["ln_mem_g"], w_kv8)
    c_oqkv, token = _exchange_start("c", [g_o, g_q, g_kv], "exchange_c_2_start")
    g_out, _ = _wgrad(ycat, dx1b, "wgrad_out", carry=[_follow(token)])
    g_out = shards(g_out)
    (g_o, g_q, g_kv), (rc_o, rc_q, rc_kv) = _exchange_wait(c_oqkv, g_out, "exchange_c_2_wait", sources=True)
    c_out, token = _exchange_start("c", [g_out], "exchange_c_3_start")
    (keep["w_o"], send_o), _ = _rs_combine(g_o, rc_o, pos, "rs_combine_w_o", carry=[_follow(token)])
    (keep["w_q"], send_q), _ = _rs_combine(g_q, rc_q, pos, "rs_combine_w_q")
    (keep["w_kv"], send_kv), _ = _rs_combine(g_kv, rc_kv, pos, "rs_combine_w_kv")
    (g_out,), (rc_out,) = _exchange_wait(c_out, send_kv, "exchange_c_3_wait", sources=True)
    (keep["w_out"], send_out), _ = _rs_combine(g_out, rc_out, pos, "rs_combine_w_out")
    started, token = _exchange_start("xy", [send_down, send_o, send_q, send_out, send_kv], "exchange_xy_2_start")
    pending.append((("w_down", "w_o", "w_q", "w_out", "w_kv"), started))
    (dh, dx, d_ga, d_gb, d_cw, d_lng, d_lnb, d_wsp, d_bs, d_lnmix), _ = _mix_backward(
        dx1, x, sp["ln_mix_g"], h, mixed, sp["sgu_ln_g"], sp["sgu_ln_b"], sp["w_spatial"], bt, conv_full,
        sp["grp_norm_a"], sp["grp_norm_b"], w_out, w_in_t, tm, carry=[_follow(token)])
    gs = {"ln_mix_g": d_lnmix, "sgu_ln_g": d_lng, "sgu_ln_b": d_lnb, "w_spatial": d_wsp, "b_spatial": _bias_grad(d_bs),
          "conv_w": d_cw[:3], "grp_norm_a": d_ga, "grp_norm_b": d_gb, "ln_attn_g": d_lnattn, "ln_mem_g": d_lnmem,
          "ln_ffn_g": d_lnffn, "ln_final_g": d_lnf}
    packed, seg = _pack_small(gs, loss)
    g_in, (_, (small_all,)) = _wgrad(dh, xn1, "wgrad_in", carry=[_follow(token), _all_gather([packed])])
    g_in = shards(g_in)
    c_in, token = _exchange_start("c", [g_in], "exchange_c_4_start")
    return dx, keep, pending, (g_in, c_in), token, small_all, seg


def kernel(x, mem, ln_mix_g, w_in, sgu_ln_g, sgu_ln_b, w_spatial, b_spatial, conv_w, grp_norm_a, grp_norm_b, w_out, ln_attn_g, ln_mem_g, w_q, w_kv, w_o, ln_ffn_g, w_gate_up, w_down, ln_final_g, loss_target, m_ln_mix_g, m_w_in, m_sgu_ln_g, m_sgu_ln_b, m_w_spatial, m_b_spatial, m_conv_w, m_grp_norm_a, m_grp_norm_b, m_w_out, m_ln_attn_g, m_ln_mem_g, m_w_q, m_w_kv, m_w_o, m_ln_ffn_g, m_w_gate_up, m_w_down, m_ln_final_g, v_ln_mix_g, v_w_in, v_sgu_ln_g, v_sgu_ln_b, v_w_spatial, v_b_spatial, v_conv_w, v_grp_norm_a, v_grp_norm_b, v_w_out, v_ln_attn_g, v_ln_mem_g, v_w_q, v_w_kv, v_w_o, v_ln_ffn_g, v_w_gate_up, v_w_down, v_ln_final_g):
    order = ["ln_mix_g", "w_in", "sgu_ln_g", "sgu_ln_b", "w_spatial", "b_spatial", "conv_w", "grp_norm_a", "grp_norm_b",
             "w_out", "ln_attn_g", "ln_mem_g", "w_q", "w_kv", "w_o", "ln_ffn_g", "w_gate_up", "w_down", "ln_final_g"]
    W = dict(ln_mix_g=ln_mix_g, w_in=w_in, sgu_ln_g=sgu_ln_g, sgu_ln_b=sgu_ln_b, w_spatial=w_spatial, b_spatial=b_spatial,
             conv_w=conv_w, grp_norm_a=grp_norm_a, grp_norm_b=grp_norm_b, w_out=w_out, ln_attn_g=ln_attn_g,
             ln_mem_g=ln_mem_g, w_q=w_q, w_kv=w_kv, w_o=w_o, ln_ffn_g=ln_ffn_g, w_gate_up=w_gate_up, w_down=w_down,
             ln_final_g=ln_final_g)
    M = dict(ln_mix_g=m_ln_mix_g, w_in=m_w_in, sgu_ln_g=m_sgu_ln_g, sgu_ln_b=m_sgu_ln_b, w_spatial=m_w_spatial,
             b_spatial=m_b_spatial, conv_w=m_conv_w, grp_norm_a=m_grp_norm_a, grp_norm_b=m_grp_norm_b, w_out=m_w_out,
             ln_attn_g=m_ln_attn_g, ln_mem_g=m_ln_mem_g, w_q=m_w_q, w_kv=m_w_kv, w_o=m_w_o, ln_ffn_g=m_ln_ffn_g,
             w_gate_up=m_w_gate_up, w_down=m_w_down, ln_final_g=m_ln_final_g)
    V = dict(ln_mix_g=v_ln_mix_g, w_in=v_w_in, sgu_ln_g=v_sgu_ln_g, sgu_ln_b=v_sgu_ln_b, w_spatial=v_w_spatial,
             b_spatial=v_b_spatial, conv_w=v_conv_w, grp_norm_a=v_grp_norm_a, grp_norm_b=v_grp_norm_b, w_out=v_w_out,
             ln_attn_g=v_ln_attn_g, ln_mem_g=v_ln_mem_g, w_q=v_w_q, w_kv=v_w_kv, w_o=v_w_o, ln_ffn_g=v_ln_ffn_g,
             w_gate_up=v_w_gate_up, w_down=v_w_down, ln_final_g=v_ln_final_g)

    bw = conv_w.shape[1] * N_DEV
    pos = jnp.stack([lax.axis_index("x"), lax.axis_index("y"), lax.axis_index("c")]).astype(jnp.int32)
    me = 4 * pos[0] + 2 * pos[1] + pos[2]

    sp = {nm: (W[nm].reshape(1, -1) if W[nm].ndim == 1 else W[nm]) for nm in SMALL}
    view = lambda a, nm: a.T if nm in TRANSPOSED else a
    wb = {nm: view(W[nm], nm).astype(BF16) for nm in BIG}
    grad_x, keep, pending, (g_in, c_in), token, small_all, seg = _step(
        x[0], mem[0], loss_target[0], wb, conv_w, sp, pos)

    out = {}

    def update(k, names, started, token):
        landed = _exchange_wait(started, token, "exchange_xy_%d_wait" % k)
        for nm, rxy in zip(names, landed):
            res = _adamw_shard(keep[nm], rxy, view(W[nm], nm), view(M[nm], nm), view(V[nm], nm), "adamw_" + nm)
            out[nm] = tuple(view(a, nm) for a in res)
            token = res[0]
        return token

    token = update(1, *pending[0], token)
    (g_in,), (rc_in,) = _exchange_wait(c_in, token, "exchange_c_4_wait", sources=True)
    (keep["w_in"], send_in), _ = _rs_combine(g_in, rc_in, pos, "rs_combine_w_in")
    xy_in, token = _exchange_start("xy", [send_in], "exchange_xy_3_start")
    token = update(2, *pending[1], token)

    params = {nm: (_rows128(W[nm]), _rows128(M[nm]), _rows128(V[nm])) for nm in SMALL}
    per, conv_g_rows, loss_sum = _adamw_small(small_all, seg, params, seg["conv_w"])
    for nm in SMALL:
        out[nm] = tuple(a.reshape(W[nm].shape) for a in per[nm])
    conv_g = lax.dynamic_slice_in_dim(conv_g_rows.reshape(3, bw), me * conv_w.shape[1], conv_w.shape[1], axis=1)
    out["conv_w"] = (conv_g,) + tuple(_adamw_one(conv_w, conv_g, m_conv_w, v_conv_w, "adamw_conv"))

    update(3, ("w_in",), xy_in, token[:1, :1] + out["conv_w"][1][:1, :1])

    loss = loss_sum[0, 0]
    res = [loss, grad_x[None]]
    for k in range(4):
        res += [out[nm][k] for nm in order]
    return tuple(res)
```

```python
import functools

import jax
import jax.numpy as jnp
from jax import lax
from jax.experimental import pallas as pl
from jax.experimental.pallas import tpu as pltpu

F32 = jnp.float32
BF16 = jnp.bfloat16
SDS = jax.ShapeDtypeStruct
MESH = pl.DeviceIdType.MESH

EPS = 1e-6
N_DEV = 8
HEADS = 4
CHUNK = 128
HALO = 16
SUB = 8
LANES = 128
TOKEN_TILE = 512
ROW_CHUNK = 256
RELAY_AT = 0.7

ADAM_LR = 0.001
ADAM_B1 = 0.9
ADAM_B2 = 0.999
ADAM_EPS = 1e-08
ADAM_WD = 0.01
ADAM_STEP = 10

BIG = ("w_in", "w_out", "w_q", "w_kv", "w_o", "w_gate_up", "w_down")
TRANSPOSED = ("w_in", "w_gate_up")
SMALL = ("ln_mix_g", "sgu_ln_g", "sgu_ln_b", "w_spatial", "b_spatial", "grp_norm_a", "grp_norm_b",
         "ln_attn_g", "ln_mem_g", "ln_ffn_g", "ln_final_g")


class _Exchange:
    def __init__(self, ins, out_shape, sems, start, finish, relay=None):
        self.ins, self.out_shape, self.sems = list(ins), list(out_shape), list(sems)
        self.start, self.finish, self.relay = start, finish, relay


def _pcall(body, carry=(), n_prefetch=0, **kw):
    if carry:
        return functools.partial(_carrying_call, body, tuple(carry), n_prefetch, kw)
    if n_prefetch:
        kw["grid_spec"] = pltpu.PrefetchScalarGridSpec(
            num_scalar_prefetch=n_prefetch, grid=kw.pop("grid"), in_specs=kw.pop("in_specs"),
            out_specs=kw.pop("out_specs"), scratch_shapes=kw.pop("scratch_shapes", ()))
    return pl.pallas_call(body, **kw)


def _carrying_call(body, carry, n_prefetch, kw, *args):
    kw = dict(kw)
    out_shape = kw.pop("out_shape")
    single = not isinstance(out_shape, (tuple, list))
    outs_shape = (out_shape,) if single else tuple(out_shape)
    out_specs = kw.pop("out_specs")
    out_specs = [out_specs] if single else list(out_specs)
    in_specs = list(kw.pop("in_specs"))
    scratch = list(kw.pop("scratch_shapes", ()))
    grid = tuple(kw.get("grid", ()))
    n_in, n_out, n_scr = len(args), len(outs_shape), len(scratch)

    def split(refs, k, counts):
        parts = []
        for cnt in counts:
            parts.append(refs[k:k + cnt])
            k += cnt
        return parts, k

    def wrapped(*refs):
        cins, k = split(refs, n_in, [len(p.ins) for p in carry])
        outs = refs[k:k + n_out]
        couts, k = split(refs, k + n_out, [len(p.out_shape) for p in carry])
        scr = refs[k:k + n_scr]
        csems, _ = split(refs, k + n_scr, [len(p.sems) for p in carry])
        first, last = True, True
        for a, g in enumerate(grid):
            first = (pl.program_id(a) == 0) & first
            last = (pl.program_id(a) == g - 1) & last

        def start_all():
            for p, ci, co, cs in zip(carry, cins, couts, csems):
                p.start(ci, co, cs)

        def relay_all():
            for p, ci, co, cs in zip(carry, cins, couts, csems):
                if p.relay is not None:
                    p.relay(ci, co, cs)

        def finish_all():
            for p, ci, co, cs in zip(carry, cins, couts, csems):
                p.finish(ci, co, cs)

        if len(grid) == 1:
            relay_now = pl.program_id(0) == min(int(RELAY_AT * grid[0]), grid[0] - 1)
        else:
            relay_now = last
        start_all() if not grid else pl.when(first)(start_all)
        relay_all() if not grid else pl.when(relay_now)(relay_all)
        body(*refs[:n_in], *outs, *scr)
        finish_all() if not grid else pl.when(last)(finish_all)

    c_in = [a for p in carry for a in p.ins]
    c_out = [s for p in carry for s in p.out_shape]
    c_sems = [s for p in carry for s in p.sems]
    res = _pcall(wrapped, n_prefetch=n_prefetch, out_shape=outs_shape + tuple(c_out),
                 in_specs=in_specs + _hbm_specs(len(c_in)), out_specs=out_specs + _hbm_specs(len(c_out)),
                 scratch_shapes=scratch + c_sems, **kw)(*args, *c_in)
    own = res[0] if single else tuple(res[:n_out])
    landed, k = [], n_out
    for p in carry:
        landed.append(list(res[k:k + len(p.out_shape)]))
        k += len(p.out_shape)
    return own, landed


def _hbm_specs(n):
    return [pl.BlockSpec(memory_space=pl.ANY)] * n


def _hosted(body, carry, **kw):
    if carry:
        return _pcall(body, carry=carry, **kw)
    call = _pcall(body, **kw)
    return lambda *args: (call(*args), [])


def _run_exchanges(parts, name):
    def body(*refs):
        pass

    _, landed = _pcall(body, carry=parts, out_shape=(), in_specs=[], out_specs=[], name=name)()
    return landed


def _arb(n):
    return pltpu.CompilerParams(dimension_semantics=("arbitrary",) * n)


def _tile(n, target, mult):
    best = None
    for t in range(mult, min(n, target) + 1, mult):
        if n % t == 0:
            best = t
    return n if best is None else best


def _round_up(n, m):
    return (n + m - 1) // m * m


def _dot(a, b):
    return jnp.dot(a, b, preferred_element_type=F32)


def _dot_nt(a, b):
    return lax.dot_general(a, b, (((1,), (1,)), ((), ())), preferred_element_type=F32)


def _dot_tn(a, b):
    return lax.dot_general(a, b, (((0,), (0,)), ((), ())), preferred_element_type=F32)


def _rstd(x):
    return lax.rsqrt(jnp.mean(x * x, axis=-1, keepdims=True) + EPS)


def _rms_bwd(dy, x, r, g):
    gdy = dy * g
    proj = jnp.sum(gdy * x, axis=-1, keepdims=True) * (1.0 / x.shape[-1])
    dx = r * gdy - x * (r * r * r) * proj
    dg = jnp.sum(dy * (x * r), axis=0, keepdims=True)
    return dx, dg


_GELU_C = 0.7978845608028654
_GELU_A = 0.044715


def _gelu(x):
    t = jnp.tanh(_GELU_C * (x + _GELU_A * x * x * x))
    return 0.5 * x * (1.0 + t), t


def _gelu_grad(x, t):
    return 0.5 * (1.0 + t) + 0.5 * x * (1.0 - t * t) * (_GELU_C * (1.0 + 3.0 * _GELU_A * x * x))


def _sigmoid(x):
    return 1.0 / (1.0 + jnp.exp(-x))


def _softmax(s):
    m = jnp.max(s, axis=-1, keepdims=True)
    e = jnp.exp(s - m)
    return e / jnp.sum(e, axis=-1, keepdims=True)


def _adamw(w, g, m, v):
    m = ADAM_B1 * m + (1.0 - ADAM_B1) * g
    v = ADAM_B2 * v + (1.0 - ADAM_B2) * (g * g)
    m_hat = m / (1.0 - ADAM_B1 ** ADAM_STEP)
    v_hat = v / (1.0 - ADAM_B2 ** ADAM_STEP)
    delta = -ADAM_LR * (m_hat / (jnp.sqrt(v_hat) + ADAM_EPS) + ADAM_WD * w)
    return delta, m, v


def _tril_mask():
    t = lax.broadcasted_iota(jnp.int32, (CHUNK, CHUNK), 0)
    s = lax.broadcasted_iota(jnp.int32, (CHUNK, CHUNK), 1)
    return (s <= t).astype(F32)


def _sgu_forward(ha, lng, lnb, wm, bt, mixed_s):
    aw = ha.shape[1] // 2
    hd = aw // HEADS
    a, th = _gelu(ha)
    u = a[:, :aw]
    v = a[:, aw:]
    mu = jnp.mean(v, axis=-1, keepdims=True)
    vc = v - mu
    rl = lax.rsqrt(jnp.mean(vc * vc, axis=-1, keepdims=True) + EPS)
    xhat = vc * rl
    vln = (xhat * lng + lnb).astype(BF16)
    for n in range(ha.shape[0] // CHUNK):
        rows = slice(n * CHUNK, (n + 1) * CHUNK)
        for h in range(HEADS):
            cols = slice(h * hd, (h + 1) * hd)
            mixed_s[rows, cols] = _dot(wm[h], vln[rows, cols]) + bt[:, h:h + 1]
    return th, u, xhat, rl, vln


def _conv_taps(zext):
    return pltpu.roll(zext, 2, 0), pltpu.roll(zext, 1, 0)


def _kv_forward(mem, g_mem, w_kv):
    ml, d = mem.shape
    xd = w_kv.shape[2]

    def body(mem_ref, g_ref, w_ref, memn_ref, kv_ref):
        x = mem_ref[...]
        memn = (x * _rstd(x) * g_ref[...]).astype(BF16)
        memn_ref[...] = memn
        for j in range(2 * HEADS):
            kv_ref[j] = _dot(memn, w_ref[j]).astype(BF16)

    return _pcall(body, out_shape=(SDS((ml, d), BF16), SDS((2 * HEADS, ml, xd), BF16)), name="kv_forward")(mem, g_mem, w_kv)


def _in_forward(x, g, w_in_t, tm, carry=()):
    s, d = x.shape
    n_in = w_in_t.shape[0]

    def body(x_ref, g_ref, w_ref, xn_ref, h_ref):
        xv = x_ref[...]
        xn = (xv * _rstd(xv) * g_ref[...]).astype(BF16)
        xn_ref[...] = xn
        h_ref[...] = _dot_nt(xn, w_ref[...])

    return _hosted(
        body, carry, grid=(s // tm,),
        in_specs=[pl.BlockSpec((tm, d), lambda i: (i, 0)), pl.BlockSpec((1, d), lambda i: (0, 0)),
                  pl.BlockSpec((n_in, d), lambda i: (0, 0))],
        out_specs=[pl.BlockSpec((tm, d), lambda i: (i, 0)), pl.BlockSpec((tm, n_in), lambda i: (i, 0))],
        out_shape=(SDS((s, d), BF16), SDS((s, n_in), F32)),
        compiler_params=_arb(1), name="in_forward")(x, g, w_in_t)


def _mix_forward(h, x, lng, lnb, w_sp, bt, conv_w, ga, gb, w_out, tm, carry=()):
    s, d = x.shape
    n_in = h.shape[1]
    aw = lng.shape[1]
    bw = d - aw
    in_a = 2 * aw
    hb_blocks = tm // HALO

    def body(h_ref, hprev_ref, x_ref, lng_ref, lnb_ref, wsp_ref, bt_ref, cw_ref, ga_ref, gb_ref, wout_ref,
             ycat_ref, x1_ref, mixed_s):
        i = pl.program_id(0)
        mask = _tril_mask()
        wm = [(wsp_ref[hh] * mask).astype(BF16) for hh in range(HEADS)]
        hv = h_ref[...]
        _, u, _, _, _ = _sgu_forward(hv[:, :in_a], lng_ref[...], lnb_ref[...], wm, bt_ref[...], mixed_s)
        sg = u * mixed_s[...]
        ycat_ref[:, :aw] = (sg * _rstd(sg) * ga_ref[...]).astype(BF16)

        gate_b = hv[:, in_a:in_a + bw]
        z = hv[:, in_a + bw:in_a + 2 * bw] * hv[:, in_a + 2 * bw:]
        hp = hprev_ref[...]
        zp = hp[:, in_a + bw:in_a + 2 * bw] * hp[:, in_a + 2 * bw:]
        zp = jnp.where(i == 0, 0.0, zp)
        zext = jnp.concatenate([zp, z], axis=0)
        z2, z1 = _conv_taps(zext)
        cw = cw_ref[...]
        conv = cw[0:1] * z2[HALO:] + cw[1:2] * z1[HALO:] + cw[2:3] * z
        sc = gate_b * conv
        ycat_ref[:, aw:] = (sc * _rstd(sc) * gb_ref[...]).astype(BF16)
        x1_ref[...] = x_ref[...] + _dot(ycat_ref[...], wout_ref[...])

    full = lambda shape: pl.BlockSpec(shape, lambda i: (0,) * len(shape))
    return _hosted(
        body, carry, grid=(s // tm,),
        in_specs=[pl.BlockSpec((tm, n_in), lambda i: (i, 0)),
                  pl.BlockSpec((HALO, n_in), lambda i: (jnp.maximum(i * hb_blocks - 1, 0), 0)),
                  pl.BlockSpec((tm, d), lambda i: (i, 0)),
                  full((1, aw)), full((1, aw)), full((HEADS, CHUNK, CHUNK)), full((CHUNK, HEADS)),
                  full((3, bw)), full((1, aw)), full((1, bw)), full((d, d))],
        out_specs=[pl.BlockSpec((tm, d), lambda i: (i, 0)), pl.BlockSpec((tm, d), lambda i: (i, 0))],
        out_shape=(SDS((s, d), BF16), SDS((s, d), F32)),
        scratch_shapes=[pltpu.VMEM((tm, aw), F32)],
        compiler_params=_arb(1), name="mix_forward")(h, h, x, lng, lnb, w_sp, bt, conv_w, ga, gb, w_out)


def _attn_forward(x1, g, w_q, kv, w_o, tm, carry=()):
    s, d = x1.shape
    _, ml, xd = kv.shape
    scale = xd ** -0.5

    def body(x1_ref, g_ref, wq_ref, kv_ref, wo_ref, xn_ref, q_ref, p_ref, o_ref, x2_ref):
        xv = x1_ref[...]
        xn = (xv * _rstd(xv) * g_ref[...]).astype(BF16)
        xn_ref[...] = xn
        q_ref[...] = _dot(xn, wq_ref[...]).astype(BF16)
        for hh in range(HEADS):
            cols = slice(hh * xd, (hh + 1) * xd)
            p = _softmax(_dot_nt(q_ref[:, cols], kv_ref[hh]) * scale).astype(BF16)
            p_ref[:, hh * ml:(hh + 1) * ml] = p
            o_ref[:, cols] = _dot(p, kv_ref[HEADS + hh]).astype(BF16)
        x2_ref[...] = xv + _dot(o_ref[...], wo_ref[...])

    tok = pl.BlockSpec((tm, d), lambda i: (i, 0))
    probs = pl.BlockSpec((tm, HEADS * ml), lambda i: (i, 0))
    return _hosted(
        body, carry, grid=(s // tm,),
        in_specs=[tok, pl.BlockSpec((1, d), lambda i: (0, 0)), pl.BlockSpec((d, d), lambda i: (0, 0)),
                  pl.BlockSpec((2 * HEADS, ml, xd), lambda i: (0, 0, 0)), pl.BlockSpec((d, d), lambda i: (0, 0))],
        out_specs=[tok, tok, probs, tok, tok],
        out_shape=(SDS((s, d), BF16), SDS((s, d), BF16), SDS((s, HEADS * ml), BF16), SDS((s, d), BF16), SDS((s, d), F32)),
        compiler_params=_arb(1), name="attn_forward")(x1, g, w_q, kv, w_o)


def _ffn_forward(x2, g, w_gu, w_down, tm):
    s, d = x2.shape
    _, nf, tf, _ = w_gu.shape

    def body(x2_ref, g_ref, wgu_ref, wd_ref, xn_ref, gu_ref, x3_ref):
        f = pl.program_id(1)

        @pl.when(f == 0)
        def _():
            xv = x2_ref[...]
            xn_ref[...] = (xv * _rstd(xv) * g_ref[...]).astype(BF16)
            x3_ref[...] = xv

        xn = xn_ref[...]
        gate = _dot_nt(xn, wgu_ref[0])
        up = _dot_nt(xn, wgu_ref[1])
        gu_ref[0] = gate.astype(BF16)
        gu_ref[1] = up.astype(BF16)
        act = (gate * _sigmoid(gate) * up).astype(BF16)
        x3_ref[...] += _dot(act, wd_ref[...])

    tok = pl.BlockSpec((tm, d), lambda i, f: (i, 0))
    return _pcall(
        body, grid=(s // tm, nf),
        in_specs=[tok, pl.BlockSpec((1, d), lambda i, f: (0, 0)),
                  pl.BlockSpec((2, None, tf, d), lambda i, f: (0, f, 0, 0)),
                  pl.BlockSpec((tf, d), lambda i, f: (f, 0))],
        out_specs=[tok, pl.BlockSpec((2, None, tm, tf), lambda i, f: (0, f, i, 0)), tok],
        out_shape=(SDS((s, d), BF16), SDS((2, nf, s, tf), BF16), SDS((s, d), F32)),
        compiler_params=_arb(2), name="ffn_forward")(x2, g, w_gu, w_down)


def _final_backward(x3, target, g_final, tm):
    s, d = x3.shape

    def body(x3_ref, tgt_ref, gf_ref, loss_ref, dgf_ref, dx3_ref, dx3b_ref):
        @pl.when(pl.program_id(0) == 0)
        def _():
            loss_ref[...] = jnp.zeros_like(loss_ref)
            dgf_ref[...] = jnp.zeros_like(dgf_ref)

        xv = x3_ref[...]
        r = _rstd(xv)
        diff = xv * r * gf_ref[...] - tgt_ref[...]
        loss_ref[...] += 0.5 * jnp.sum(jnp.sum(diff * diff, axis=-1, keepdims=True), axis=0, keepdims=True) * (1.0 / d)
        dx3, dgf = _rms_bwd(diff * (1.0 / d), xv, r, gf_ref[...])
        dgf_ref[...] += dgf
        dx3_ref[...] = dx3
        dx3b_ref[...] = dx3.astype(BF16)

    tok = pl.BlockSpec((tm, d), lambda i: (i, 0))
    vec = pl.BlockSpec((1, d), lambda i: (0, 0))
    return _pcall(
        body, grid=(s // tm,), in_specs=[tok, tok, vec],
        out_specs=[pl.BlockSpec((SUB, LANES), lambda i: (0, 0)), vec, tok, tok],
        out_shape=(SDS((SUB, LANES), F32), SDS((1, d), F32), SDS((s, d), F32), SDS((s, d), BF16)),
        compiler_params=_arb(1), name="final_backward")(x3, target, g_final)


def _swiglu_backward(dx3b, gu, w_gu, w_down, tm):
    s, d = dx3b.shape
    _, nf, tf, _ = w_gu.shape

    def body(dx3b_ref, gu_ref, wgu_ref, wd_ref, act_ref, dgu_ref, dxn_ref):
        @pl.when(pl.program_id(1) == 0)
        def _():
            dxn_ref[...] = jnp.zeros_like(dxn_ref)

        for r0 in range(0, tm, ROW_CHUNK):
            rows = slice(r0, r0 + ROW_CHUNK)
            dact = _dot_nt(dx3b_ref[rows, :], wd_ref[...])
            gv = gu_ref[0, rows, :].astype(F32)
            uv = gu_ref[1, rows, :].astype(F32)
            sg = _sigmoid(gv)
            silu = gv * sg
            act_ref[rows, :] = (silu * uv).astype(BF16)
            dgate = (dact * uv * (sg * (1.0 + gv * (1.0 - sg)))).astype(BF16)
            dup = (dact * silu).astype(BF16)
            dgu_ref[0, rows, :] = dgate
            dgu_ref[1, rows, :] = dup
            part = _dot(dgate, wgu_ref[0]) + _dot(dup, wgu_ref[1])
            dxn_ref[rows, :] += part

    tok = pl.BlockSpec((tm, d), lambda i, f: (i, 0))
    pair = pl.BlockSpec((2, None, tm, tf), lambda i, f: (0, f, i, 0))
    return _pcall(
        body, grid=(s // tm, nf),
        in_specs=[tok, pair, pl.BlockSpec((2, None, tf, d), lambda i, f: (0, f, 0, 0)),
                  pl.BlockSpec((tf, d), lambda i, f: (f, 0))],
        out_specs=[pl.BlockSpec((None, tm, tf), lambda i, f: (f, i, 0)), pair, tok],
        out_shape=(SDS((nf, s, tf), BF16), SDS((2, nf, s, tf), BF16), SDS((s, d), F32)),
        compiler_params=_arb(2), name="swiglu_backward")(dx3b, gu, w_gu, w_down)


def _attn_backward(dx3, dxn3, x2, g_ffn, x1, g, q, probs, kv, w_q, w_o, tm, carry=()):
    s, d = x1.shape
    _, ml, xd = kv.shape
    scale = xd ** -0.5

    def body(dx3_ref, dxn3_ref, x2_ref, g2_ref, x1_ref, g_ref, q_ref, p_ref, kv_ref, wq_ref, wo_ref,
             dx2b_ref, dq_ref, dx1_ref, dx1b_ref, dkv_ref, dg_ref, dg2_ref, do_s):
        i = pl.program_id(0)

        @pl.when(i == 0)
        def _():
            dkv_ref[...] = jnp.zeros_like(dkv_ref)
            dg_ref[...] = jnp.zeros_like(dg_ref)
            dg2_ref[...] = jnp.zeros_like(dg2_ref)

        x2v = x2_ref[...]
        dx2n, dg2 = _rms_bwd(dxn3_ref[...], x2v, _rstd(x2v), g2_ref[...])
        dg2_ref[...] += dg2
        dx2 = dx3_ref[...] + dx2n
        dx2b_ref[...] = dx2.astype(BF16)
        do_s[...] = _dot_nt(dx2b_ref[...], wo_ref[...]).astype(BF16)
        for hh in range(HEADS):
            kc = slice(hh * xd, (hh + 1) * xd)
            qh = q_ref[:, kc]
            kh = kv_ref[hh]
            doh = do_s[:, kc]
            pb = p_ref[:, hh * ml:(hh + 1) * ml]
            p = pb.astype(F32)
            dp = _dot_nt(doh, kv_ref[HEADS + hh])
            dkv_ref[HEADS + hh] += _dot_tn(pb, doh)
            ds = (p * (dp - jnp.sum(dp * p, axis=-1, keepdims=True)) * scale).astype(BF16)
            dq_ref[:, kc] = _dot(ds, kh).astype(BF16)
            dkv_ref[hh] += _dot_tn(ds, qh)
        dxn = _dot_nt(dq_ref[...], wq_ref[...])
        xv = x1_ref[...]
        dx, dg = _rms_bwd(dxn, xv, _rstd(xv), g_ref[...])
        dg_ref[...] += dg
        dx1 = dx2 + dx
        dx1_ref[...] = dx1
        dx1b_ref[...] = dx1.astype(BF16)

    tok = pl.BlockSpec((tm, d), lambda i: (i, 0))
    vec = pl.BlockSpec((1, d), lambda i: (0, 0))
    sq = pl.BlockSpec((d, d), lambda i: (0, 0))
    kvs = pl.BlockSpec((2 * HEADS, ml, xd), lambda i: (0, 0, 0))
    return _hosted(
        body, carry, grid=(s // tm,),
        in_specs=[tok, tok, tok, vec, tok, vec, tok, pl.BlockSpec((tm, HEADS * ml), lambda i: (i, 0)), kvs, sq, sq],
        out_specs=[tok, tok, tok, tok, kvs, vec, vec],
        out_shape=(SDS((s, d), BF16), SDS((s, d), BF16), SDS((s, d), F32), SDS((s, d), BF16),
                   SDS((2 * HEADS, ml, xd), F32), SDS((1, d), F32), SDS((1, d), F32)),
        scratch_shapes=[pltpu.VMEM((tm, d), BF16)],
        compiler_params=_arb(1), name="attn_backward")(dx3, dxn3, x2, g_ffn, x1, g, q, probs, kv, w_q, w_o)


def _kv_backward(dkv, memn, mem, g_mem, w_kv):
    ml, d = mem.shape
    xd = w_kv.shape[2]

    def body(dkv_ref, memn_ref, mem_ref, g_ref, w_ref, dw_ref, dg_ref):
        dmemn = jnp.zeros((ml, d), F32)
        for j in range(2 * HEADS):
            dkvb = dkv_ref[j].astype(BF16)
            dw_ref[j] = _dot_tn(memn_ref[...], dkvb)
            dmemn = dmemn + _dot_nt(dkvb, w_ref[j])
        x = mem_ref[...]
        dg_ref[...] = jnp.sum(dmemn * (x * _rstd(x)), axis=0, keepdims=True)

    return _pcall(body, out_shape=(SDS((2 * HEADS, d, xd), F32), SDS((1, d), F32)), name="kv_backward")(dkv, memn, mem, g_mem, w_kv)


def _mix_backward(dx1, x, g_mix, h, lng, lnb, w_sp, bt, conv_w, ga, gb, w_out, w_in, tm, carry=()):
    s, d = x.shape
    n_in = h.shape[1]
    aw = lng.shape[1]
    bw = d - aw
    hd = aw // HEADS
    in_a = 2 * aw
    hb_blocks = tm // HALO
    last_blk = s // HALO - 1
    nt = s // tm
    tc = tm
    te = tc + HALO
    tee = tc + 2 * HALO

    def body(dx1_ref, dx1n_ref, x_ref, gm_ref, h_ref, hp_ref, hn_ref, lng_ref, lnb_ref, wsp_ref, bt_ref, cw_ref,
             ga_ref, gb_ref, wout_ref, win_ref,
             dh_ref, dx_ref, dga_ref, dgb_ref, dcw_ref, dlng_ref, dlnb_ref, dwsp_ref, dbs_ref, dgm_ref,
             mixed_s, dvln_s):
        i = pl.program_id(0)

        @pl.when(i == 0)
        def _():
            for ref in (dga_ref, dgb_ref, dcw_ref, dlng_ref, dlnb_ref, dwsp_ref, dbs_ref, dgm_ref):
                ref[...] = jnp.zeros_like(ref)

        mask = _tril_mask()
        wm = [(wsp_ref[hh] * mask).astype(BF16) for hh in range(HEADS)]
        cw = cw_ref[...]

        def chain(r0):
            rows = slice(r0, r0 + tc)
            first, last = r0 == 0, r0 + tc == tm
            hv = h_ref[rows, :]
            dx1 = dx1_ref[rows, :]
            dx1n = dx1n_ref[...] if last else dx1_ref[r0 + tc:r0 + tc + HALO, :]
            hp = hp_ref[:, in_a:] if first else h_ref[r0 - HALO:r0, in_a:]
            hn = hn_ref[:, in_a:] if last else h_ref[r0 + tc:r0 + tc + HALO, in_a:]
            dx1e = jnp.concatenate([dx1, dx1n], axis=0).astype(BF16)
            dycat = _dot_nt(dx1e, wout_ref[...])

            hbe = jnp.concatenate([hp, hv[:, in_a:], hn], axis=0)
            row = lax.broadcasted_iota(jnp.int32, (tee, 1), 0)
            zext = hbe[:, bw:2 * bw] * hbe[:, 2 * bw:]
            if first:
                zext = jnp.where((i == 0) & (row < HALO), 0.0, zext)
            z2e, z1e = _conv_taps(zext)
            conv_e = (cw[0:1] * z2e + cw[1:2] * z1e + cw[2:3] * zext)[HALO:]
            gate_b_e = hbe[HALO:, :bw]
            sc_e = gate_b_e * conv_e
            rb = _rstd(sc_e)
            dyb = dycat[:, aw:]
            gdy = dyb * gb_ref[...]
            dsc_e = rb * gdy - sc_e * (rb * rb * rb) * (jnp.sum(gdy * sc_e, axis=-1, keepdims=True) * (1.0 / bw))
            dgb_ref[...] += jnp.sum((dyb * (sc_e * rb))[:tc], axis=0, keepdims=True)
            dconv_e = dsc_e * gate_b_e
            if last:
                dconv_e = jnp.where((i == nt - 1) & (row[:te] >= tc), 0.0, dconv_e)
            dconv = dconv_e[:tc]
            dc1 = pltpu.roll(dconv_e, te - 1, 0)[:tc]
            dc2 = pltpu.roll(dconv_e, te - 2, 0)[:tc]
            dz = cw[2:3] * dconv + cw[1:2] * dc1 + cw[0:1] * dc2
            z = zext[HALO:HALO + tc]
            z1 = z1e[HALO:HALO + tc]
            z2 = z2e[HALO:HALO + tc]
            dcw_ref[0:1, :] += jnp.sum(dconv * z2, axis=0, keepdims=True)
            dcw_ref[1:2, :] += jnp.sum(dconv * z1, axis=0, keepdims=True)
            dcw_ref[2:3, :] += jnp.sum(dconv * z, axis=0, keepdims=True)
            dh_ref[rows, in_a:in_a + bw] = (dsc_e[:tc] * conv_e[:tc]).astype(BF16)
            dh_ref[rows, in_a + bw:in_a + 2 * bw] = (dz * hv[:, in_a + 2 * bw:]).astype(BF16)
            dh_ref[rows, in_a + 2 * bw:] = (dz * hv[:, in_a + bw:in_a + 2 * bw]).astype(BF16)

            ha = hv[:, :in_a]
            mixed_c, dvln_c = mixed_s.at[rows, :], dvln_s.at[rows, :]
            th, u, xhat, rl, vln = _sgu_forward(ha, lng_ref[...], lnb_ref[...], wm, bt_ref[...], mixed_c)
            mixed = mixed_c[...]
            sg = u * mixed
            dsg, dga = _rms_bwd(dycat[:tc, :aw], sg, _rstd(sg), ga_ref[...])
            dga_ref[...] += dga
            du = dsg * mixed
            dmixed = dsg * u
            dmb = dmixed.astype(BF16)
            for n in range(tc // CHUNK):
                blk = slice(n * CHUNK, (n + 1) * CHUNK)
                dbs_ref[...] += dmixed[blk]
                for hh in range(HEADS):
                    cols = slice(hh * hd, (hh + 1) * hd)
                    dvln_c[blk, cols] = _dot_tn(wm[hh], dmb[blk, cols])
                    dwsp_ref[hh] += mask * _dot_nt(dmb[blk, cols], vln[blk, cols])
            dvln = dvln_c[...]
            dlng_ref[...] += jnp.sum(dvln * xhat, axis=0, keepdims=True)
            dlnb_ref[...] += jnp.sum(dvln, axis=0, keepdims=True)
            dxh = dvln * lng_ref[...]
            dv = rl * (dxh - jnp.mean(dxh, axis=-1, keepdims=True) - xhat * jnp.mean(dxh * xhat, axis=-1, keepdims=True))
            dh_ref[rows, :in_a] = (jnp.concatenate([du, dv], axis=-1) * _gelu_grad(ha, th)).astype(BF16)

            dxn = _dot(dh_ref[rows, :], win_ref[...])
            xv = x_ref[rows, :]
            dx, dgm = _rms_bwd(dxn, xv, _rstd(xv), gm_ref[...])
            dgm_ref[...] += dgm
            dx_ref[rows, :] = dx1 + dx

        for r0 in range(0, tm, tc):
            chain(r0)

    full = lambda shape: pl.BlockSpec(shape, lambda i: (0,) * len(shape))
    tok = pl.BlockSpec((tm, d), lambda i: (i, 0))
    nxt = lambda i: (jnp.minimum((i + 1) * hb_blocks, last_blk), 0)
    prv = lambda i: (jnp.maximum(i * hb_blocks - 1, 0), 0)
    return _hosted(
        body, carry, grid=(nt,),
        in_specs=[tok, pl.BlockSpec((HALO, d), nxt), tok, full((1, d)),
                  pl.BlockSpec((tm, n_in), lambda i: (i, 0)), pl.BlockSpec((HALO, n_in), prv), pl.BlockSpec((HALO, n_in), nxt),
                  full((1, aw)), full((1, aw)), full((HEADS, CHUNK, CHUNK)), full((CHUNK, HEADS)), full((3, bw)),
                  full((1, aw)), full((1, bw)), full((d, d)), full((n_in, d))],
        out_specs=[pl.BlockSpec((tm, n_in), lambda i: (i, 0)), tok,
                   full((1, aw)), full((1, bw)), full((SUB, bw)), full((1, aw)), full((1, aw)),
                   full((HEADS, CHUNK, CHUNK)), full((CHUNK, aw)), full((1, d))],
        out_shape=(SDS((s, n_in), BF16), SDS((s, d), F32),
                   SDS((1, aw), F32), SDS((1, bw), F32), SDS((SUB, bw), F32), SDS((1, aw), F32), SDS((1, aw), F32),
                   SDS((HEADS, CHUNK, CHUNK), F32), SDS((CHUNK, aw), F32), SDS((1, d), F32)),
        scratch_shapes=[pltpu.VMEM((tm, aw), F32), pltpu.VMEM((tm, aw), F32)],
        compiler_params=_arb(1), name="mix_backward")(dx1, dx1, x, g_mix, h, h, h, lng, lnb, w_sp, bt, conv_w, ga, gb, w_out, w_in)


def _bias_grad(dbs):
    aw = dbs.shape[1]
    hd = aw // HEADS

    def body(dbs_ref, out_ref):
        ones = jnp.ones((SUB, hd), F32)
        for hh in range(HEADS):
            r = lax.dot_general(ones, dbs_ref[:, hh * hd:(hh + 1) * hd], (((1,), (1,)), ((), ())),
                                precision=lax.Precision.HIGHEST, preferred_element_type=F32)
            out_ref[hh:hh + 1, :] = r[0:1]

    return _pcall(body, out_shape=SDS((HEADS, CHUNK), F32), name="bias_grad")(dbs)


def _wgrad_body(a_ref, b_ref, o_ref):
    o_ref[...] = _dot_tn(a_ref[...], b_ref[...])


def _wgrad(a, b, name, carry=()):
    k, m = a.shape
    n = b.shape[1]
    tm = _tile(m, 512, LANES)
    tn = _tile(n, 1024, LANES)
    return _hosted(
        functools.partial(_wgrad_body), carry, grid=(m // tm, n // tn),
        in_specs=[pl.BlockSpec((k, tm), lambda i, j: (0, i)), pl.BlockSpec((k, tn), lambda i, j: (0, j))],
        out_specs=pl.BlockSpec((tm, tn), lambda i, j: (i, j)),
        out_shape=SDS((m, n), F32), compiler_params=_arb(2), name=name)(a, b)


def _wgrad_blocked_lhs(a, b, name, carry=()):
    nb, k, t = a.shape
    n = b.shape[1]
    tn = _tile(n, 1024, LANES)
    return _hosted(
        functools.partial(_wgrad_body), carry, grid=(nb, n // tn),
        in_specs=[pl.BlockSpec((None, k, t), lambda i, j: (i, 0, 0)), pl.BlockSpec((k, tn), lambda i, j: (0, j))],
        out_specs=pl.BlockSpec((t, tn), lambda i, j: (i, j)),
        out_shape=SDS((nb * t, n), F32), compiler_params=_arb(2), name=name)(a, b)


def _place():
    x, y, c = lax.axis_index("x"), lax.axis_index("y"), lax.axis_index("c")
    return x, y, c, [(1 - x, y), (x, 1 - y), (1 - x, 1 - y)]


def _all_gather(shards):
    n = len(shards)
    slots = 9
    cut = [(s.shape[0] // 32) * 16 for s in shards]

    def build(ins, outs, sems):
        send_sems, recv_sems, local_sems = sems
        x, y, c, _ = _place()
        me, sib, xn, yn, dg = (x, y, c), (x, y, 1 - c), (1 - x, y, c), (x, 1 - y, c), (1 - x, 1 - y, c)
        other = lambda p: (p[0], p[1], 1 - p[2])

        def rows(a, p, part=None):
            ref = outs[a].at[4 * p[0] + 2 * p[1] + p[2]]
            if part is None or cut[a] == 0:
                return ref if part in (None, 0) else None
            return ref.at[pl.ds(0, cut[a])] if part == 0 else ref.at[pl.ds(cut[a], shards[a].shape[0] - cut[a])]

        def copy(a, k, ref, to, src=None):
            if ref is None:
                return None
            return pltpu.make_async_remote_copy(
                src_ref=ref if src is None else src, dst_ref=ref, send_sem=send_sems.at[slots * a + k],
                recv_sem=recv_sems.at[slots * a + k], device_id=to, device_id_type=MESH)

        def real(cps):
            return [cp for cp in cps if cp is not None]

        class Copies:
            own = lambda a: [copy(a, 1, rows(a, me), xn, ins[a]), copy(a, 2, rows(a, me), yn, ins[a]),
                             copy(a, 0, rows(a, me), sib, ins[a])]
            local = lambda a: pltpu.make_async_copy(ins[a], rows(a, me), local_sems.at[a])
            from_x = lambda a: copy(a, 1, rows(a, xn), me)
            from_y = lambda a: copy(a, 2, rows(a, yn), me)
            after_x = lambda a: real([copy(a, 4, rows(a, xn, 1), yn), copy(a, 5, rows(a, xn), sib)])
            after_y = lambda a: real([copy(a, 3, rows(a, yn, 0), xn), copy(a, 6, rows(a, yn), sib)])
            diag_in = lambda a: real([copy(a, 3, rows(a, dg, 0), me), copy(a, 4, rows(a, dg, 1), me)])
            diag_on = lambda a: real([copy(a, 7, rows(a, dg, 0), sib), copy(a, 8, rows(a, dg, 1), sib)])
            from_sib = lambda a: real([copy(a, 0, rows(a, sib), me), copy(a, 5, rows(a, other(xn)), me),
                                       copy(a, 6, rows(a, other(yn)), me), copy(a, 7, rows(a, other(dg), 0), me),
                                       copy(a, 8, rows(a, other(dg), 1), me)])

        return Copies

    def start(ins, outs, sems):
        cps = build(ins, outs, sems)
        for a in range(n):
            for cp in cps.own(a):
                cp.start()
        for a in range(n):
            cps.local(a).start()

    def relay(ins, outs, sems):
        cps = build(ins, outs, sems)
        for a in range(n):
            cps.from_x(a).wait_recv()
            for cp in cps.after_x(a):
                cp.start()
            cps.from_y(a).wait_recv()
            for cp in cps.after_y(a):
                cp.start()

    def finish(ins, outs, sems):
        cps = build(ins, outs, sems)
        for a in range(n):
            for arrived, onward in zip(cps.diag_in(a), cps.diag_on(a)):
                arrived.wait_recv()
                onward.start()
        for a in range(n):
            for cp in cps.from_sib(a):
                cp.wait_recv()
            for cp in cps.own(a) + cps.after_x(a) + cps.after_y(a) + cps.diag_on(a):
                cp.wait_send()
            cps.local(a).wait()

    return _Exchange(shards, [SDS((N_DEV,) + s.shape, s.dtype) for s in shards],
                     [pltpu.SemaphoreType.DMA((slots * n,)), pltpu.SemaphoreType.DMA((slots * n,)),
                      pltpu.SemaphoreType.DMA((n,))], start, finish, relay)


def _swap_exchange(ins, out_shape, per, copies):
    def start(i, o, sems):
        for cp in copies(i, o, sems):
            cp.start()

    def finish(i, o, sems):
        for cp in copies(i, o, sems):
            cp.wait()

    n = per * len(ins)
    return _Exchange(ins, out_shape, [pltpu.SemaphoreType.DMA((n,)), pltpu.SemaphoreType.DMA((n,))], start, finish)


def _exchange_c(gs):
    def copies(ins, outs, sems):
        x, y, c, _ = _place()
        return [pltpu.make_async_remote_copy(
                    src_ref=ins[a].at[2 * k + 1 - c], dst_ref=outs[a].at[k],
                    send_sem=sems[0].at[4 * a + k], recv_sem=sems[1].at[4 * a + k],
                    device_id=(x, y, 1 - c), device_id_type=MESH)
                for a in range(len(gs)) for k in range(4)]

    return _swap_exchange(gs, [SDS((4,) + g.shape[1:], g.dtype) for g in gs], 4, copies)


def _rs_combine(g, recv, pos, name, carry=()):
    _, r, cdim = g.shape
    tr = _tile(r, 256, 16)

    def body(pos_ref, g0, r0, g1, r1, g2, r2, g3, r3, keep_ref, send_ref):
        keep_ref[...] = g0[...] + r0[...]
        send_ref[0] = (g1[...] + r1[...]).astype(BF16)
        send_ref[1] = (g2[...] + r2[...]).astype(BF16)
        send_ref[2] = (g3[...] + r3[...]).astype(BF16)

    def k_of(p, t):
        px = p[0] if t in (0, 2) else 1 - p[0]
        py = p[1] if t in (0, 1) else 1 - p[1]
        return 2 * px + py

    blk = (None, tr, cdim)
    in_specs = []
    for t in range(4):
        in_specs.append(pl.BlockSpec(blk, functools.partial(lambda j, p, t: (2 * k_of(p, t) + p[2], j, 0), t=t)))
        in_specs.append(pl.BlockSpec(blk, functools.partial(lambda j, p, t: (k_of(p, t), j, 0), t=t)))
    return _hosted(
        body, carry, n_prefetch=1, out_shape=(SDS((r, cdim), F32), SDS((3, r, cdim), BF16)),
        grid=(r // tr,), in_specs=in_specs,
        out_specs=[pl.BlockSpec((tr, cdim), lambda j, p: (j, 0)), pl.BlockSpec((3, tr, cdim), lambda j, p: (0, j, 0))],
        compiler_params=_arb(1), name=name)(pos, g, recv, g, recv, g, recv, g, recv)


def _adamw_shard(keep, recv, w, m, v, name):
    r, cdim = w.shape
    tr = _tile(r, 256, 16)

    def body(k_ref, r_ref, w_ref, m_ref, v_ref, g_ref, d_ref, nm_ref, nv_ref):
        g = ((k_ref[...] + r_ref[0].astype(F32)) + r_ref[1].astype(F32)) + r_ref[2].astype(F32)
        g_ref[...] = g
        d_ref[...], nm_ref[...], nv_ref[...] = _adamw(w_ref[...], g, m_ref[...], v_ref[...])

    blk = pl.BlockSpec((tr, cdim), lambda j: (j, 0))
    out = SDS((r, cdim), F32)
    return _pcall(body, grid=(r // tr,), in_specs=[blk, pl.BlockSpec((3, tr, cdim), lambda j: (0, j, 0)), blk, blk, blk],
                  out_specs=[blk] * 4, out_shape=(out,) * 4, compiler_params=_arb(1), name=name)(keep, recv, w, m, v)


_HBM = pl.BlockSpec(memory_space=pltpu.HBM)
_SEM = pl.BlockSpec(memory_space=pltpu.SEMAPHORE)
_SPLIT = pltpu.CompilerParams(has_side_effects=pltpu.SideEffectType.DATAFLOW_SIDE_EFFECTING)


def _split_copies(kind, n, refs):
    srcs, lands, (send_sems, recv_sems) = refs[:n], refs[n:2 * n], refs[2 * n:2 * n + 2]
    x, y, c, chips = _place()
    per = _SPLIT_COPIES[kind]
    if kind == "xy":
        ends = lambda a, t: (srcs[a].at[t], lands[a].at[t], (*chips[t], c))
    else:
        ends = lambda a, k: (srcs[a].at[2 * k + 1 - c], lands[a].at[k], (x, y, 1 - c))
    cps = []
    for a in range(n):
        for t in range(per):
            src, dst, to = ends(a, t)
            cps.append(pltpu.make_async_remote_copy(src_ref=src, dst_ref=dst, send_sem=send_sems.at[per * a + t],
                                                    recv_sem=recv_sems.at[per * a + t], device_id=to, device_id_type=MESH))
    return cps


_SPLIT_COPIES = {"xy": 3, "c": 4}
SIBLING_COLLECTIVE = 1


def _exchange_start(kind, arrays, name, after=None):
    n = len(arrays)
    order = [] if after is None else [after]

    def body(*refs):
        if kind == "c":
            x, y, c, _ = _place()
            barrier = pltpu.get_barrier_semaphore()
            pl.semaphore_signal(barrier, inc=1, device_id=(x, y, 1 - c), device_id_type=MESH)
            pl.semaphore_wait(barrier, 1)
        refs = refs[:2 * n] + refs[2 * n + len(order):]
        for cp in _split_copies(kind, n, refs):
            cp.start()
        refs[-1][...] = jnp.zeros_like(refs[-1])

    params = _SPLIT if kind == "xy" else pltpu.CompilerParams(
        has_side_effects=pltpu.SideEffectType.DATAFLOW_SIDE_EFFECTING, collective_id=SIBLING_COLLECTIVE)
    hbm = lambda a: pltpu.with_memory_space_constraint(a, pltpu.HBM)
    land = [a.shape if kind == "xy" else (4,) + a.shape[1:] for a in arrays]
    bufs = [pltpu.HBM(a.shape, a.dtype) for a in arrays] + [pltpu.HBM(s, a.dtype) for s, a in zip(land, arrays)]
    sems = pltpu.SemaphoreType.DMA((_SPLIT_COPIES[kind] * n,))
    res = _pcall(
        body, name=name, out_shape=(sems, sems, *bufs, SDS((SUB, LANES), F32)),
        in_specs=[_HBM] * (2 * n) + _hbm_specs(len(order)),
        out_specs=[_SEM, _SEM] + [_HBM] * (2 * n) + [pl.BlockSpec(memory_space=pltpu.VMEM)],
        input_output_aliases={k: 2 + k for k in range(2 * n)}, compiler_params=params)(
            *[hbm(a) for a in arrays], *[hbm(lax.empty(s, a.dtype)) for s, a in zip(land, arrays)], *order)
    return (kind, n, res[:-1]), res[-1]


def _exchange_wait(started, after, name, sources=False):
    kind, n, (send_sems, recv_sems, *bufs) = started

    def body(*refs):
        for cp in _split_copies(kind, n, refs):
            cp.wait_send()
            cp.wait_recv()

    shapes = [pltpu.HBM(b.shape, b.dtype) for b in bufs]
    res = _pcall(
        body, name=name, out_shape=tuple(shapes),
        in_specs=[_HBM] * (2 * n) + [_SEM, _SEM, pl.BlockSpec(memory_space=pl.ANY)], out_specs=[_HBM] * (2 * n),
        input_output_aliases={k: k for k in range(2 * n)}, compiler_params=_SPLIT)(*bufs, send_sems, recv_sems, after)
    return (list(res[:n]), list(res[n:])) if sources else list(res[n:])


def _follow(token):
    nothing = lambda ins, outs, sems: None
    return _Exchange([token], [], [], nothing, nothing)


def _adamw_small(gathered, seg, params, conv_rows):
    names = list(params)
    c0, cn = conv_rows

    def body(*refs):
        gat_ref = refs[0]
        ins = refs[1:1 + 3 * len(names)]
        outs = refs[1 + 3 * len(names):]

        def total(r0, rn):
            tot = gat_ref[0, r0:r0 + rn, :]
            for dev in range(1, N_DEV):
                tot = tot + gat_ref[dev, r0:r0 + rn, :]
            return tot

        for k, nm in enumerate(names):
            g = total(*seg[nm])
            w_ref, m_ref, v_ref = ins[3 * k:3 * k + 3]
            g_ref, d_ref, nm_ref, nv_ref = outs[4 * k:4 * k + 4]
            g_ref[...] = g
            d_ref[...], nm_ref[...], nv_ref[...] = _adamw(w_ref[...], g, m_ref[...], v_ref[...])
        outs[-2][...] = total(c0, cn)
        outs[-1][...] = total(*seg["loss"])

    flat_in = [a for nm in names for a in params[nm]]
    out_shape = []
    for nm in names:
        out_shape += [SDS(params[nm][0].shape, F32)] * 4
    out_shape += [SDS((cn, LANES), F32), SDS((seg["loss"][1], LANES), F32)]
    res = _pcall(body, out_shape=tuple(out_shape), name="adamw_small")(gathered, *flat_in)
    per = {nm: res[4 * k:4 * k + 4] for k, nm in enumerate(names)}
    return per, res[-2], res[-1]


def _adamw_one(w, g, m, v, name):
    def body(w_ref, g_ref, m_ref, v_ref, d_ref, nm_ref, nv_ref):
        d_ref[...], nm_ref[...], nv_ref[...] = _adamw(w_ref[...], g_ref[...], m_ref[...], v_ref[...])

    return _pcall(body, out_shape=(SDS(w.shape, F32),) * 3, name=name)(w, g, m, v)


def _rows128(a):
    return a.reshape(-1, LANES)


def _pack_small(gs, loss_tile):
    seg, pieces, row = {}, [], 0
    for nm in SMALL + ("conv_w", "loss"):
        piece = loss_tile if nm == "loss" else _rows128(gs[nm])
        rn = _round_up(piece.shape[0], SUB)
        pieces.append(jnp.pad(piece, ((0, rn - piece.shape[0]), (0, 0))))
        seg[nm] = (row, piece.shape[0])
        row += rn
    return jnp.concatenate(pieces, axis=0), seg


def _step(x, mem, target, wb, conv_w, sp, pos):
    s, d = x.shape
    tm = min(TOKEN_TILE, s)
    tm_wide = min(2 * TOKEN_TILE, s)
    rows = lambda w8: w8.reshape(-1, w8.shape[2])
    shards = lambda g: g.reshape((N_DEV, -1) + g.shape[1:])
    bt = sp["b_spatial"].T

    (w_in8, conv8), = _run_exchanges([_all_gather([wb["w_in"], conv_w])], "gather_w_in")
    conv_full = conv8.transpose(1, 0, 2).reshape(3, -1)
    w_in_t = rows(w_in8)
    (xn1, h), ((w_out8, w_kv8, w_q8),) = _in_forward(
        x, sp["ln_mix_g"], w_in_t, tm, carry=[_all_gather([wb["w_out"], wb["w_kv"], wb["w_q"]])])
    w_out = rows(w_out8)
    (ycat, x1), ((w_o8, w_down8),) = _mix_forward(
        h, x, sp["sgu_ln_g"], sp["sgu_ln_b"], sp["w_spatial"], bt, conv_full, sp["grp_norm_a"], sp["grp_norm_b"], w_out, tm,
        carry=[_all_gather([wb["w_o"], wb["w_down"]])])
    w_q, w_o, w_down = rows(w_q8), rows(w_o8), rows(w_down8)
    memn, kv = _kv_forward(mem, sp["ln_mem_g"], w_kv8)
    (xn2, q, probs, o, x2), ((w_gu8,),) = _attn_forward(
        x1, sp["ln_attn_g"], w_q, kv, w_o, tm, carry=[_all_gather([wb["w_gate_up"]])])
    w_gu = w_gu8.reshape((2, N_DEV // 2) + w_gu8.shape[1:])
    xn3, gu, x3 = _ffn_forward(x2, sp["ln_ffn_g"], w_gu, w_down, tm_wide)

    loss, d_lnf, dx3, dx3b = _final_backward(x3, target, sp["ln_final_g"], tm_wide)
    act, dgu, dxn3 = _swiglu_backward(dx3b, gu, w_gu, w_down, tm_wide)
    g_gu, _ = _wgrad_blocked_lhs(dgu.reshape((N_DEV,) + dgu.shape[2:]), xn3, "wgrad_gate_up")
    g_gu = shards(g_gu)
    g_down, ((rc_gu,),) = _wgrad_blocked_lhs(act, dx3b, "wgrad_down", carry=[_exchange_c([g_gu])])
    g_down = shards(g_down)
    keep, pending = {}, []
    (keep["w_gate_up"], send_gu), _ = _rs_combine(g_gu, rc_gu, pos, "rs_combine_w_gate_up")
    started, token = _exchange_start("xy", [send_gu], "exchange_xy_1_start")
    pending.append((("w_gate_up",), started))
    c_down, token = _exchange_start("c", [g_down], "exchange_c_1_start", after=token)
    (dx2b, dq, dx1, dx1b, dkv, d_lnattn, d_lnffn), _ = _attn_backward(
        dx3, dxn3, x2, sp["ln_ffn_g"], x1, sp["ln_attn_g"], q, probs, kv, w_q, w_o, tm, carry=[_follow(token)])
    (g_down,), (rc_down,) = _exchange_wait(c_down, dx1b, "exchange_c_1_wait", sources=True)
    (keep["w_down"], send_down), _ = _rs_combine(g_down, rc_down, pos, "rs_combine_w_down")
    g_o, _ = _wgrad(o, dx2b, "wgrad_o")
    g_q, _ = _wgrad(xn2, dq, "wgrad_q")
    g_o, g_q = shards(g_o), shards(g_q)
    g_kv, d_lnmem = _kv_backward(dkv, memn, mem, sp["ln_mem_g"], w_kv8)
    c_oqkv, token = _exchange_start("c", [g_o, g_q, g_kv], "exchange_c_2_start")
    g_out, _ = _wgrad(ycat, dx1b, "wgrad_out", carry=[_follow(token)])
    g_out = shards(g_out)
    (g_o, g_q, g_kv), (rc_o, rc_q, rc_kv) = _exchange_wait(c_oqkv, g_out, "exchange_c_2_wait", sources=True)
    c_out, token = _exchange_start("c", [g_out], "exchange_c_3_start")
    (keep["w_o"], send_o), _ = _rs_combine(g_o, rc_o, pos, "rs_combine_w_o", carry=[_follow(token)])
    (keep["w_q"], send_q), _ = _rs_combine(g_q, rc_q, pos, "rs_combine_w_q")
    (keep["w_kv"], send_kv), _ = _rs_combine(g_kv, rc_kv, pos, "rs_combine_w_kv")
    (g_out,), (rc_out,) = _exchange_wait(c_out, send_kv, "exchange_c_3_wait", sources=True)
    (keep["w_out"], send_out), _ = _rs_combine(g_out, rc_out, pos, "rs_combine_w_out")
    started, token = _exchange_start("xy", [send_down, send_o, send_q, send_out, send_kv], "exchange_xy_2_start")
    pending.append((("w_down", "w_o", "w_q", "w_out", "w_kv"), started))
    (dh, dx, d_ga, d_gb, d_cw, d_lng, d_lnb, d_wsp, d_bs, d_lnmix), _ = _mix_backward(
        dx1, x, sp["ln_mix_g"], h, sp["sgu_ln_g"], sp["sgu_ln_b"], sp["w_spatial"], bt, conv_full,
        sp["grp_norm_a"], sp["grp_norm_b"], w_out, w_in_t, tm, carry=[_follow(token)])
    gs = {"ln_mix_g": d_lnmix, "sgu_ln_g": d_lng, "sgu_ln_b": d_lnb, "w_spatial": d_wsp, "b_spatial": _bias_grad(d_bs),
          "conv_w": d_cw[:3], "grp_norm_a": d_ga, "grp_norm_b": d_gb, "ln_attn_g": d_lnattn, "ln_mem_g": d_lnmem,
          "ln_ffn_g": d_lnffn, "ln_final_g": d_lnf}
    packed, seg = _pack_small(gs, loss)
    g_in, (_, (small_all,)) = _wgrad(dh, xn1, "wgrad_in", carry=[_follow(token), _all_gather([packed])])
    g_in = shards(g_in)
    c_in, token = _exchange_start("c", [g_in], "exchange_c_4_start")
    return dx, keep, pending, (g_in, c_in), token, small_all, seg


def kernel(x, mem, ln_mix_g, w_in, sgu_ln_g, sgu_ln_b, w_spatial, b_spatial, conv_w, grp_norm_a, grp_norm_b, w_out, ln_attn_g, ln_mem_g, w_q, w_kv, w_o, ln_ffn_g, w_gate_up, w_down, ln_final_g, loss_target, m_ln_mix_g, m_w_in, m_sgu_ln_g, m_sgu_ln_b, m_w_spatial, m_b_spatial, m_conv_w, m_grp_norm_a, m_grp_norm_b, m_w_out, m_ln_attn_g, m_ln_mem_g, m_w_q, m_w_kv, m_w_o, m_ln_ffn_g, m_w_gate_up, m_w_down, m_ln_final_g, v_ln_mix_g, v_w_in, v_sgu_ln_g, v_sgu_ln_b, v_w_spatial, v_b_spatial, v_conv_w, v_grp_norm_a, v_grp_norm_b, v_w_out, v_ln_attn_g, v_ln_mem_g, v_w_q, v_w_kv, v_w_o, v_ln_ffn_g, v_w_gate_up, v_w_down, v_ln_final_g):
    order = ["ln_mix_g", "w_in", "sgu_ln_g", "sgu_ln_b", "w_spatial", "b_spatial", "conv_w", "grp_norm_a", "grp_norm_b",
             "w_out", "ln_attn_g", "ln_mem_g", "w_q", "w_kv", "w_o", "ln_ffn_g", "w_gate_up", "w_down", "ln_final_g"]
    W = dict(ln_mix_g=ln_mix_g, w_in=w_in, sgu_ln_g=sgu_ln_g, sgu_ln_b=sgu_ln_b, w_spatial=w_spatial, b_spatial=b_spatial,
             conv_w=conv_w, grp_norm_a=grp_norm_a, grp_norm_b=grp_norm_b, w_out=w_out, ln_attn_g=ln_attn_g,
             ln_mem_g=ln_mem_g, w_q=w_q, w_kv=w_kv, w_o=w_o, ln_ffn_g=ln_ffn_g, w_gate_up=w_gate_up, w_down=w_down,
             ln_final_g=ln_final_g)
    M = dict(ln_mix_g=m_ln_mix_g, w_in=m_w_in, sgu_ln_g=m_sgu_ln_g, sgu_ln_b=m_sgu_ln_b, w_spatial=m_w_spatial,
             b_spatial=m_b_spatial, conv_w=m_conv_w, grp_norm_a=m_grp_norm_a, grp_norm_b=m_grp_norm_b, w_out=m_w_out,
             ln_attn_g=m_ln_attn_g, ln_mem_g=m_ln_mem_g, w_q=m_w_q, w_kv=m_w_kv, w_o=m_w_o, ln_ffn_g=m_ln_ffn_g,
             w_gate_up=m_w_gate_up, w_down=m_w_down, ln_final_g=m_ln_final_g)
    V = dict(ln_mix_g=v_ln_mix_g, w_in=v_w_in, sgu_ln_g=v_sgu_ln_g, sgu_ln_b=v_sgu_ln_b, w_spatial=v_w_spatial,
             b_spatial=v_b_spatial, conv_w=v_conv_w, grp_norm_a=v_grp_norm_a, grp_norm_b=v_grp_norm_b, w_out=v_w_out,
             ln_attn_g=v_ln_attn_g, ln_mem_g=v_ln_mem_g, w_q=v_w_q, w_kv=v_w_kv, w_o=v_w_o, ln_ffn_g=v_ln_ffn_g,
             w_gate_up=v_w_gate_up, w_down=v_w_down, ln_final_g=v_ln_final_g)

    bw = conv_w.shape[1] * N_DEV
    pos = jnp.stack([lax.axis_index("x"), lax.axis_index("y"), lax.axis_index("c")]).astype(jnp.int32)
    me = 4 * pos[0] + 2 * pos[1] + pos[2]

    sp = {nm: (W[nm].reshape(1, -1) if W[nm].ndim == 1 else W[nm]) for nm in SMALL}
    view = lambda a, nm: a.T if nm in TRANSPOSED else a
    wb = {nm: view(W[nm], nm).astype(BF16) for nm in BIG}
    grad_x, keep, pending, (g_in, c_in), token, small_all, seg = _step(
        x[0], mem[0], loss_target[0], wb, conv_w, sp, pos)

    out = {}

    def update(k, names, started, token):
        landed = _exchange_wait(started, token, "exchange_xy_%d_wait" % k)
        for nm, rxy in zip(names, landed):
            res = _adamw_shard(keep[nm], rxy, view(W[nm], nm), view(M[nm], nm), view(V[nm], nm), "adamw_" + nm)
            out[nm] = tuple(view(a, nm) for a in res)
            token = res[0]
        return token

    token = update(1, *pending[0], token)
    (g_in,), (rc_in,) = _exchange_wait(c_in, token, "exchange_c_4_wait", sources=True)
    (keep["w_in"], send_in), _ = _rs_combine(g_in, rc_in, pos, "rs_combine_w_in")
    xy_in, token = _exchange_start("xy", [send_in], "exchange_xy_3_start")
    token = update(2, *pending[1], token)

    params = {nm: (_rows128(W[nm]), _rows128(M[nm]), _rows128(V[nm])) for nm in SMALL}
    per, conv_g_rows, loss_sum = _adamw_small(small_all, seg, params, seg["conv_w"])
    for nm in SMALL:
        out[nm] = tuple(a.reshape(W[nm].shape) for a in per[nm])
    conv_g = lax.dynamic_slice_in_dim(conv_g_rows.reshape(3, bw), me * conv_w.shape[1], conv_w.shape[1], axis=1)
    out["conv_w"] = (conv_g,) + tuple(_adamw_one(conv_w, conv_g, m_conv_w, v_conv_w, "adamw_conv"))

    update(3, ("w_in",), xy_in, token[:1, :1] + out["conv_w"][1][:1, :1])

    loss = loss_sum[0, 0]
    res = [loss, grad_x[None]]
    for k in range(4):
        res += [out[nm][k] for nm in order]
    return tuple(res)
```

```python
import functools

import jax
import jax.numpy as jnp
from jax import lax
from jax.experimental import pallas as pl
from jax.experimental.pallas import tpu as pltpu

F32 = jnp.float32
BF16 = jnp.bfloat16
SDS = jax.ShapeDtypeStruct
MESH = pl.DeviceIdType.MESH

EPS = 1e-6
N_DEV = 8
HEADS = 4
CHUNK = 128
HALO = 16
SUB = 8
LANES = 128
TOKEN_TILE = 512
ROW_CHUNK = 256
RELAY_AT = 0.7

ADAM_LR = 0.001
ADAM_B1 = 0.9
ADAM_B2 = 0.999
ADAM_EPS = 1e-08
ADAM_WD = 0.01
ADAM_STEP = 10

BIG = ("w_in", "w_out", "w_q", "w_kv", "w_o", "w_gate_up", "w_down")
TRANSPOSED = ("w_in", "w_gate_up")
SMALL = ("ln_mix_g", "sgu_ln_g", "sgu_ln_b", "w_spatial", "b_spatial", "grp_norm_a", "grp_norm_b",
         "ln_attn_g", "ln_mem_g", "ln_ffn_g", "ln_final_g")


class _Exchange:
    def __init__(self, ins, out_shape, sems, start, finish, relay=None):
        self.ins, self.out_shape, self.sems = list(ins), list(out_shape), list(sems)
        self.start, self.finish, self.relay = start, finish, relay


def _pcall(body, carry=(), n_prefetch=0, **kw):
    if carry:
        return functools.partial(_carrying_call, body, tuple(carry), n_prefetch, kw)
    if n_prefetch:
        kw["grid_spec"] = pltpu.PrefetchScalarGridSpec(
            num_scalar_prefetch=n_prefetch, grid=kw.pop("grid"), in_specs=kw.pop("in_specs"),
            out_specs=kw.pop("out_specs"), scratch_shapes=kw.pop("scratch_shapes", ()))
    return pl.pallas_call(body, **kw)


def _carrying_call(body, carry, n_prefetch, kw, *args):
    kw = dict(kw)
    out_shape = kw.pop("out_shape")
    single = not isinstance(out_shape, (tuple, list))
    outs_shape = (out_shape,) if single else tuple(out_shape)
    out_specs = kw.pop("out_specs")
    out_specs = [out_specs] if single else list(out_specs)
    in_specs = list(kw.pop("in_specs"))
    scratch = list(kw.pop("scratch_shapes", ()))
    grid = tuple(kw.get("grid", ()))
    n_in, n_out, n_scr = len(args), len(outs_shape), len(scratch)

    def split(refs, k, counts):
        parts = []
        for cnt in counts:
            parts.append(refs[k:k + cnt])
            k += cnt
        return parts, k

    def wrapped(*refs):
        cins, k = split(refs, n_in, [len(p.ins) for p in carry])
        outs = refs[k:k + n_out]
        couts, k = split(refs, k + n_out, [len(p.out_shape) for p in carry])
        scr = refs[k:k + n_scr]
        csems, _ = split(refs, k + n_scr, [len(p.sems) for p in carry])
        first, last = True, True
        for a, g in enumerate(grid):
            first = (pl.program_id(a) == 0) & first
            last = (pl.program_id(a) == g - 1) & last

        def start_all():
            for p, ci, co, cs in zip(carry, cins, couts, csems):
                p.start(ci, co, cs)

        def relay_all():
            for p, ci, co, cs in zip(carry, cins, couts, csems):
                if p.relay is not None:
                    p.relay(ci, co, cs)

        def finish_all():
            for p, ci, co, cs in zip(carry, cins, couts, csems):
                p.finish(ci, co, cs)

        if len(grid) == 1:
            relay_now = pl.program_id(0) == min(int(RELAY_AT * grid[0]), grid[0] - 1)
        else:
            relay_now = last
        start_all() if not grid else pl.when(first)(start_all)
        relay_all() if not grid else pl.when(relay_now)(relay_all)
        body(*refs[:n_in], *outs, *scr)
        finish_all() if not grid else pl.when(last)(finish_all)

    c_in = [a for p in carry for a in p.ins]
    c_out = [s for p in carry for s in p.out_shape]
    c_sems = [s for p in carry for s in p.sems]
    res = _pcall(wrapped, n_prefetch=n_prefetch, out_shape=outs_shape + tuple(c_out),
                 in_specs=in_specs + _hbm_specs(len(c_in)), out_specs=out_specs + _hbm_specs(len(c_out)),
                 scratch_shapes=scratch + c_sems, **kw)(*args, *c_in)
    own = res[0] if single else tuple(res[:n_out])
    landed, k = [], n_out
    for p in carry:
        landed.append(list(res[k:k + len(p.out_shape)]))
        k += len(p.out_shape)
    return own, landed


def _hbm_specs(n):
    return [pl.BlockSpec(memory_space=pl.ANY)] * n


def _hosted(body, carry, **kw):
    if carry:
        return _pcall(body, carry=carry, **kw)
    call = _pcall(body, **kw)
    return lambda *args: (call(*args), [])


def _run_exchanges(parts, name):
    def body(*refs):
        pass

    _, landed = _pcall(body, carry=parts, out_shape=(), in_specs=[], out_specs=[], name=name)()
    return landed


def _arb(n):
    return pltpu.CompilerParams(dimension_semantics=("arbitrary",) * n)


def _tile(n, target, mult):
    best = None
    for t in range(mult, min(n, target) + 1, mult):
        if n % t == 0:
            best = t
    return n if best is None else best


def _round_up(n, m):
    return (n + m - 1) // m * m


def _dot(a, b):
    return jnp.dot(a, b, preferred_element_type=F32)


def _dot_nt(a, b):
    return lax.dot_general(a, b, (((1,), (1,)), ((), ())), preferred_element_type=F32)


def _dot_tn(a, b):
    return lax.dot_general(a, b, (((0,), (0,)), ((), ())), preferred_element_type=F32)


def _rstd(x):
    return lax.rsqrt(jnp.mean(x * x, axis=-1, keepdims=True) + EPS)


def _rms_bwd(dy, x, r, g):
    gdy = dy * g
    proj = jnp.sum(gdy * x, axis=-1, keepdims=True) * (1.0 / x.shape[-1])
    dx = r * gdy - x * (r * r * r) * proj
    dg = jnp.sum(dy * (x * r), axis=0, keepdims=True)
    return dx, dg


_GELU_C = 0.7978845608028654
_GELU_A = 0.044715


def _gelu(x):
    t = jnp.tanh(_GELU_C * (x + _GELU_A * x * x * x))
    return 0.5 * x * (1.0 + t), t


def _gelu_grad(x, t):
    return 0.5 * (1.0 + t) + 0.5 * x * (1.0 - t * t) * (_GELU_C * (1.0 + 3.0 * _GELU_A * x * x))


def _sigmoid(x):
    return 1.0 / (1.0 + jnp.exp(-x))


def _softmax(s):
    m = jnp.max(s, axis=-1, keepdims=True)
    e = jnp.exp(s - m)
    return e / jnp.sum(e, axis=-1, keepdims=True)


def _adamw(w, g, m, v):
    m = ADAM_B1 * m + (1.0 - ADAM_B1) * g
    v = ADAM_B2 * v + (1.0 - ADAM_B2) * (g * g)
    m_hat = m / (1.0 - ADAM_B1 ** ADAM_STEP)
    v_hat = v / (1.0 - ADAM_B2 ** ADAM_STEP)
    delta = -ADAM_LR * (m_hat / (jnp.sqrt(v_hat) + ADAM_EPS) + ADAM_WD * w)
    return delta, m, v


def _tril_mask():
    t = lax.broadcasted_iota(jnp.int32, (CHUNK, CHUNK), 0)
    s = lax.broadcasted_iota(jnp.int32, (CHUNK, CHUNK), 1)
    return (s <= t).astype(F32)


def _sgu_forward(ha, lng, lnb, wm, bt, mixed_s):
    aw = ha.shape[1] // 2
    hd = aw // HEADS
    a, th = _gelu(ha)
    u = a[:, :aw]
    v = a[:, aw:]
    mu = jnp.mean(v, axis=-1, keepdims=True)
    vc = v - mu
    rl = lax.rsqrt(jnp.mean(vc * vc, axis=-1, keepdims=True) + EPS)
    xhat = vc * rl
    vln = (xhat * lng + lnb).astype(BF16)
    for n in range(ha.shape[0] // CHUNK):
        rows = slice(n * CHUNK, (n + 1) * CHUNK)
        for h in range(HEADS):
            cols = slice(h * hd, (h + 1) * hd)
            mixed_s[rows, cols] = _dot(wm[h], vln[rows, cols]) + bt[:, h:h + 1]
    return th, u, xhat, rl, vln


def _conv_taps(zext):
    return pltpu.roll(zext, 2, 0), pltpu.roll(zext, 1, 0)


def _kv_forward(mem, g_mem, w_kv):
    ml, d = mem.shape
    xd = w_kv.shape[2]

    def body(mem_ref, g_ref, w_ref, memn_ref, kv_ref):
        x = mem_ref[...]
        memn = (x * _rstd(x) * g_ref[...]).astype(BF16)
        memn_ref[...] = memn
        for j in range(2 * HEADS):
            kv_ref[j] = _dot(memn, w_ref[j]).astype(BF16)

    return _pcall(body, out_shape=(SDS((ml, d), BF16), SDS((2 * HEADS, ml, xd), BF16)), name="kv_forward")(mem, g_mem, w_kv)


def _in_forward(x, g, w_in_t, tm, carry=()):
    s, d = x.shape
    n_in = w_in_t.shape[0]

    def body(x_ref, g_ref, w_ref, xn_ref, h_ref):
        xv = x_ref[...]
        xn = (xv * _rstd(xv) * g_ref[...]).astype(BF16)
        xn_ref[...] = xn
        h_ref[...] = _dot_nt(xn, w_ref[...])

    return _hosted(
        body, carry, grid=(s // tm,),
        in_specs=[pl.BlockSpec((tm, d), lambda i: (i, 0)), pl.BlockSpec((1, d), lambda i: (0, 0)),
                  pl.BlockSpec((n_in, d), lambda i: (0, 0))],
        out_specs=[pl.BlockSpec((tm, d), lambda i: (i, 0)), pl.BlockSpec((tm, n_in), lambda i: (i, 0))],
        out_shape=(SDS((s, d), BF16), SDS((s, n_in), F32)),
        compiler_params=_arb(1), name="in_forward")(x, g, w_in_t)


def _mix_forward(h, x, lng, lnb, w_sp, bt, conv_w, ga, gb, w_out, tm, carry=()):
    s, d = x.shape
    n_in = h.shape[1]
    aw = lng.shape[1]
    bw = d - aw
    in_a = 2 * aw
    hb_blocks = tm // HALO

    def body(h_ref, hprev_ref, x_ref, lng_ref, lnb_ref, wsp_ref, bt_ref, cw_ref, ga_ref, gb_ref, wout_ref,
             ycat_ref, x1_ref, mixed_s):
        i = pl.program_id(0)
        mask = _tril_mask()
        wm = [(wsp_ref[hh] * mask).astype(BF16) for hh in range(HEADS)]
        hv = h_ref[...]
        _, u, _, _, _ = _sgu_forward(hv[:, :in_a], lng_ref[...], lnb_ref[...], wm, bt_ref[...], mixed_s)
        sg = u * mixed_s[...]
        ycat_ref[:, :aw] = (sg * _rstd(sg) * ga_ref[...]).astype(BF16)

        gate_b = hv[:, in_a:in_a + bw]
        z = hv[:, in_a + bw:in_a + 2 * bw] * hv[:, in_a + 2 * bw:]
        hp = hprev_ref[...]
        zp = hp[:, in_a + bw:in_a + 2 * bw] * hp[:, in_a + 2 * bw:]
        zp = jnp.where(i == 0, 0.0, zp)
        zext = jnp.concatenate([zp, z], axis=0)
        z2, z1 = _conv_taps(zext)
        cw = cw_ref[...]
        conv = cw[0:1] * z2[HALO:] + cw[1:2] * z1[HALO:] + cw[2:3] * z
        sc = gate_b * conv
        ycat_ref[:, aw:] = (sc * _rstd(sc) * gb_ref[...]).astype(BF16)
        x1_ref[...] = x_ref[...] + _dot(ycat_ref[...], wout_ref[...])

    full = lambda shape: pl.BlockSpec(shape, lambda i: (0,) * len(shape))
    return _hosted(
        body, carry, grid=(s // tm,),
        in_specs=[pl.BlockSpec((tm, n_in), lambda i: (i, 0)),
                  pl.BlockSpec((HALO, n_in), lambda i: (jnp.maximum(i * hb_blocks - 1, 0), 0)),
                  pl.BlockSpec((tm, d), lambda i: (i, 0)),
                  full((1, aw)), full((1, aw)), full((HEADS, CHUNK, CHUNK)), full((CHUNK, HEADS)),
                  full((3, bw)), full((1, aw)), full((1, bw)), full((d, d))],
        out_specs=[pl.BlockSpec((tm, d), lambda i: (i, 0)), pl.BlockSpec((tm, d), lambda i: (i, 0))],
        out_shape=(SDS((s, d), BF16), SDS((s, d), F32)),
        scratch_shapes=[pltpu.VMEM((tm, aw), F32)],
        compiler_params=_arb(1), name="mix_forward")(h, h, x, lng, lnb, w_sp, bt, conv_w, ga, gb, w_out)


def _attn_forward(x1, g, w_q, kv, w_o, tm, carry=()):
    s, d = x1.shape
    _, ml, xd = kv.shape
    scale = xd ** -0.5

    def body(x1_ref, g_ref, wq_ref, kv_ref, wo_ref, xn_ref, q_ref, p_ref, o_ref, x2_ref):
        xv = x1_ref[...]
        xn = (xv * _rstd(xv) * g_ref[...]).astype(BF16)
        xn_ref[...] = xn
        q_ref[...] = _dot(xn, wq_ref[...]).astype(BF16)
        for hh in range(HEADS):
            cols = slice(hh * xd, (hh + 1) * xd)
            p = _softmax(_dot_nt(q_ref[:, cols], kv_ref[hh]) * scale).astype(BF16)
            p_ref[:, hh * ml:(hh + 1) * ml] = p
            o_ref[:, cols] = _dot(p, kv_ref[HEADS + hh]).astype(BF16)
        x2_ref[...] = xv + _dot(o_ref[...], wo_ref[...])

    tok = pl.BlockSpec((tm, d), lambda i: (i, 0))
    probs = pl.BlockSpec((tm, HEADS * ml), lambda i: (i, 0))
    return _hosted(
        body, carry, grid=(s // tm,),
        in_specs=[tok, pl.BlockSpec((1, d), lambda i: (0, 0)), pl.BlockSpec((d, d), lambda i: (0, 0)),
                  pl.BlockSpec((2 * HEADS, ml, xd), lambda i: (0, 0, 0)), pl.BlockSpec((d, d), lambda i: (0, 0))],
        out_specs=[tok, tok, probs, tok, tok],
        out_shape=(SDS((s, d), BF16), SDS((s, d), BF16), SDS((s, HEADS * ml), BF16), SDS((s, d), BF16), SDS((s, d), F32)),
        compiler_params=_arb(1), name="attn_forward")(x1, g, w_q, kv, w_o)


def _ffn_forward(x2, g, w_gu, w_down, tm):
    s, d = x2.shape
    _, nf, tf, _ = w_gu.shape

    def body(x2_ref, g_ref, wgu_ref, wd_ref, xn_ref, gu_ref, x3_ref):
        f = pl.program_id(1)

        @pl.when(f == 0)
        def _():
            xv = x2_ref[...]
            xn_ref[...] = (xv * _rstd(xv) * g_ref[...]).astype(BF16)
            x3_ref[...] = xv

        xn = xn_ref[...]
        gate = _dot_nt(xn, wgu_ref[0])
        up = _dot_nt(xn, wgu_ref[1])
        gu_ref[0] = gate.astype(BF16)
        gu_ref[1] = up.astype(BF16)
        act = (gate * _sigmoid(gate) * up).astype(BF16)
        x3_ref[...] += _dot(act, wd_ref[...])

    tok = pl.BlockSpec((tm, d), lambda i, f: (i, 0))
    return _pcall(
        body, grid=(s // tm, nf),
        in_specs=[tok, pl.BlockSpec((1, d), lambda i, f: (0, 0)),
                  pl.BlockSpec((2, None, tf, d), lambda i, f: (0, f, 0, 0)),
                  pl.BlockSpec((tf, d), lambda i, f: (f, 0))],
        out_specs=[tok, pl.BlockSpec((2, None, tm, tf), lambda i, f: (0, f, i, 0)), tok],
        out_shape=(SDS((s, d), BF16), SDS((2, nf, s, tf), BF16), SDS((s, d), F32)),
        compiler_params=_arb(2), name="ffn_forward")(x2, g, w_gu, w_down)


def _final_backward(x3, target, g_final, tm):
    s, d = x3.shape

    def body(x3_ref, tgt_ref, gf_ref, loss_ref, dgf_ref, dx3_ref, dx3b_ref):
        @pl.when(pl.program_id(0) == 0)
        def _():
            loss_ref[...] = jnp.zeros_like(loss_ref)
            dgf_ref[...] = jnp.zeros_like(dgf_ref)

        xv = x3_ref[...]
        r = _rstd(xv)
        diff = xv * r * gf_ref[...] - tgt_ref[...]
        loss_ref[...] += 0.5 * jnp.sum(jnp.sum(diff * diff, axis=-1, keepdims=True), axis=0, keepdims=True) * (1.0 / d)
        dx3, dgf = _rms_bwd(diff * (1.0 / d), xv, r, gf_ref[...])
        dgf_ref[...] += dgf
        dx3_ref[...] = dx3
        dx3b_ref[...] = dx3.astype(BF16)

    tok = pl.BlockSpec((tm, d), lambda i: (i, 0))
    vec = pl.BlockSpec((1, d), lambda i: (0, 0))
    return _pcall(
        body, grid=(s // tm,), in_specs=[tok, tok, vec],
        out_specs=[pl.BlockSpec((SUB, LANES), lambda i: (0, 0)), vec, tok, tok],
        out_shape=(SDS((SUB, LANES), F32), SDS((1, d), F32), SDS((s, d), F32), SDS((s, d), BF16)),
        compiler_params=_arb(1), name="final_backward")(x3, target, g_final)


def _swiglu_backward(dx3b, gu, w_gu, w_down, tm):
    s, d = dx3b.shape
    _, nf, tf, _ = w_gu.shape

    def body(dx3b_ref, gu_ref, wgu_ref, wd_ref, act_ref, dgu_ref, dxn_ref):
        @pl.when(pl.program_id(1) == 0)
        def _():
            dxn_ref[...] = jnp.zeros_like(dxn_ref)

        for r0 in range(0, tm, ROW_CHUNK):
            rows = slice(r0, r0 + ROW_CHUNK)
            dact = _dot_nt(dx3b_ref[rows, :], wd_ref[...])
            gv = gu_ref[0, rows, :].astype(F32)
            uv = gu_ref[1, rows, :].astype(F32)
            sg = _sigmoid(gv)
            silu = gv * sg
            act_ref[rows, :] = (silu * uv).astype(BF16)
            dgate = (dact * uv * (sg * (1.0 + gv * (1.0 - sg)))).astype(BF16)
            dup = (dact * silu).astype(BF16)
            dgu_ref[0, rows, :] = dgate
            dgu_ref[1, rows, :] = dup
            part = _dot(dgate, wgu_ref[0]) + _dot(dup, wgu_ref[1])
            dxn_ref[rows, :] += part

    tok = pl.BlockSpec((tm, d), lambda i, f: (i, 0))
    pair = pl.BlockSpec((2, None, tm, tf), lambda i, f: (0, f, i, 0))
    return _pcall(
        body, grid=(s // tm, nf),
        in_specs=[tok, pair, pl.BlockSpec((2, None, tf, d), lambda i, f: (0, f, 0, 0)),
                  pl.BlockSpec((tf, d), lambda i, f: (f, 0))],
        out_specs=[pl.BlockSpec((None, tm, tf), lambda i, f: (f, i, 0)), pair, tok],
        out_shape=(SDS((nf, s, tf), BF16), SDS((2, nf, s, tf), BF16), SDS((s, d), F32)),
        compiler_params=_arb(2), name="swiglu_backward")(dx3b, gu, w_gu, w_down)


def _attn_backward(dx3, dxn3, x2, g_ffn, x1, g, q, probs, kv, w_q, w_o, tm, carry=()):
    s, d = x1.shape
    _, ml, xd = kv.shape
    scale = xd ** -0.5

    def body(dx3_ref, dxn3_ref, x2_ref, g2_ref, x1_ref, g_ref, q_ref, p_ref, kv_ref, wq_ref, wo_ref,
             dx2b_ref, dq_ref, dx1_ref, dx1b_ref, dkv_ref, dg_ref, dg2_ref, do_s):
        i = pl.program_id(0)

        @pl.when(i == 0)
        def _():
            dkv_ref[...] = jnp.zeros_like(dkv_ref)
            dg_ref[...] = jnp.zeros_like(dg_ref)
            dg2_ref[...] = jnp.zeros_like(dg2_ref)

        x2v = x2_ref[...]
        dx2n, dg2 = _rms_bwd(dxn3_ref[...], x2v, _rstd(x2v), g2_ref[...])
        dg2_ref[...] += dg2
        dx2 = dx3_ref[...] + dx2n
        dx2b_ref[...] = dx2.astype(BF16)
        do_s[...] = _dot_nt(dx2b_ref[...], wo_ref[...]).astype(BF16)
        for hh in range(HEADS):
            kc = slice(hh * xd, (hh + 1) * xd)
            qh = q_ref[:, kc]
            kh = kv_ref[hh]
            doh = do_s[:, kc]
            pb = p_ref[:, hh * ml:(hh + 1) * ml]
            p = pb.astype(F32)
            dp = _dot_nt(doh, kv_ref[HEADS + hh])
            dkv_ref[HEADS + hh] += _dot_tn(pb, doh)
            ds = (p * (dp - jnp.sum(dp * p, axis=-1, keepdims=True)) * scale).astype(BF16)
            dq_ref[:, kc] = _dot(ds, kh).astype(BF16)
            dkv_ref[hh] += _dot_tn(ds, qh)
        dxn = _dot_nt(dq_ref[...], wq_ref[...])
        xv = x1_ref[...]
        dx, dg = _rms_bwd(dxn, xv, _rstd(xv), g_ref[...])
        dg_ref[...] += dg
        dx1 = dx2 + dx
        dx1_ref[...] = dx1
        dx1b_ref[...] = dx1.astype(BF16)

    tok = pl.BlockSpec((tm, d), lambda i: (i, 0))
    vec = pl.BlockSpec((1, d), lambda i: (0, 0))
    sq = pl.BlockSpec((d, d), lambda i: (0, 0))
    kvs = pl.BlockSpec((2 * HEADS, ml, xd), lambda i: (0, 0, 0))
    return _hosted(
        body, carry, grid=(s // tm,),
        in_specs=[tok, tok, tok, vec, tok, vec, tok, pl.BlockSpec((tm, HEADS * ml), lambda i: (i, 0)), kvs, sq, sq],
        out_specs=[tok, tok, tok, tok, kvs, vec, vec],
        out_shape=(SDS((s, d), BF16), SDS((s, d), BF16), SDS((s, d), F32), SDS((s, d), BF16),
                   SDS((2 * HEADS, ml, xd), F32), SDS((1, d), F32), SDS((1, d), F32)),
        scratch_shapes=[pltpu.VMEM((tm, d), BF16)],
        compiler_params=_arb(1), name="attn_backward")(dx3, dxn3, x2, g_ffn, x1, g, q, probs, kv, w_q, w_o)


def _kv_backward(dkv, memn, mem, g_mem, w_kv):
    ml, d = mem.shape
    xd = w_kv.shape[2]

    def body(dkv_ref, memn_ref, mem_ref, g_ref, w_ref, dw_ref, dg_ref):
        dmemn = jnp.zeros((ml, d), F32)
        for j in range(2 * HEADS):
            dkvb = dkv_ref[j].astype(BF16)
            dw_ref[j] = _dot_tn(memn_ref[...], dkvb)
            dmemn = dmemn + _dot_nt(dkvb, w_ref[j])
        x = mem_ref[...]
        dg_ref[...] = jnp.sum(dmemn * (x * _rstd(x)), axis=0, keepdims=True)

    return _pcall(body, out_shape=(SDS((2 * HEADS, d, xd), F32), SDS((1, d), F32)), name="kv_backward")(dkv, memn, mem, g_mem, w_kv)


def _mix_backward(dx1, x, g_mix, h, lng, lnb, w_sp, bt, conv_w, ga, gb, w_out, w_in, tm, carry=()):
    s, d = x.shape
    n_in = h.shape[1]
    aw = lng.shape[1]
    bw = d - aw
    hd = aw // HEADS
    in_a = 2 * aw
    hb_blocks = tm // HALO
    last_blk = s // HALO - 1
    nt = s // tm
    tc = tm
    te = tc + HALO
    tee = tc + 2 * HALO

    def body(dx1_ref, dx1n_ref, x_ref, gm_ref, h_ref, hp_ref, hn_ref, lng_ref, lnb_ref, wsp_ref, bt_ref, cw_ref,
             ga_ref, gb_ref, wout_ref, win_ref,
             dh_ref, dx_ref, dga_ref, dgb_ref, dcw_ref, dlng_ref, dlnb_ref, dwsp_ref, dbs_ref, dgm_ref,
             mixed_s, dvln_s):
        i = pl.program_id(0)

        @pl.when(i == 0)
        def _():
            for ref in (dga_ref, dgb_ref, dcw_ref, dlng_ref, dlnb_ref, dwsp_ref, dbs_ref, dgm_ref):
                ref[...] = jnp.zeros_like(ref)

        mask = _tril_mask()
        wm = [(wsp_ref[hh] * mask).astype(BF16) for hh in range(HEADS)]
        cw = cw_ref[...]

        def chain(r0):
            rows = slice(r0, r0 + tc)
            first, last = r0 == 0, r0 + tc == tm
            hv = h_ref[rows, :]
            dx1 = dx1_ref[rows, :]
            dx1n = dx1n_ref[...] if last else dx1_ref[r0 + tc:r0 + tc + HALO, :]
            hp = hp_ref[:, in_a:] if first else h_ref[r0 - HALO:r0, in_a:]
            hn = hn_ref[:, in_a:] if last else h_ref[r0 + tc:r0 + tc + HALO, in_a:]
            dx1e = jnp.concatenate([dx1, dx1n], axis=0).astype(BF16)
            dycat = _dot_nt(dx1e, wout_ref[...])

            hbe = jnp.concatenate([hp, hv[:, in_a:], hn], axis=0)
            row = lax.broadcasted_iota(jnp.int32, (tee, 1), 0)
            zext = hbe[:, bw:2 * bw] * hbe[:, 2 * bw:]
            if first:
                zext = jnp.where((i == 0) & (row < HALO), 0.0, zext)
            z2e, z1e = _conv_taps(zext)
            conv_e = (cw[0:1] * z2e + cw[1:2] * z1e + cw[2:3] * zext)[HALO:]
            gate_b_e = hbe[HALO:, :bw]
            sc_e = gate_b_e * conv_e
            rb = _rstd(sc_e)
            dyb = dycat[:, aw:]
            gdy = dyb * gb_ref[...]
            dsc_e = rb * gdy - sc_e * (rb * rb * rb) * (jnp.sum(gdy * sc_e, axis=-1, keepdims=True) * (1.0 / bw))
            dgb_ref[...] += jnp.sum((dyb * (sc_e * rb))[:tc], axis=0, keepdims=True)
            dconv_e = dsc_e * gate_b_e
            if last:
                dconv_e = jnp.where((i == nt - 1) & (row[:te] >= tc), 0.0, dconv_e)
            dconv = dconv_e[:tc]
            dc1 = pltpu.roll(dconv_e, te - 1, 0)[:tc]
            dc2 = pltpu.roll(dconv_e, te - 2, 0)[:tc]
            dz = cw[2:3] * dconv + cw[1:2] * dc1 + cw[0:1] * dc2
            z = zext[HALO:HALO + tc]
            z1 = z1e[HALO:HALO + tc]
            z2 = z2e[HALO:HALO + tc]
            dcw_ref[0:1, :] += jnp.sum(dconv * z2, axis=0, keepdims=True)
            dcw_ref[1:2, :] += jnp.sum(dconv * z1, axis=0, keepdims=True)
            dcw_ref[2:3, :] += jnp.sum(dconv * z, axis=0, keepdims=True)
            dh_ref[rows, in_a:in_a + bw] = (dsc_e[:tc] * conv_e[:tc]).astype(BF16)
            dh_ref[rows, in_a + bw:in_a + 2 * bw] = (dz * hv[:, in_a + 2 * bw:]).astype(BF16)
            dh_ref[rows, in_a + 2 * bw:] = (dz * hv[:, in_a + bw:in_a + 2 * bw]).astype(BF16)

            ha = hv[:, :in_a]
            mixed_c, dvln_c = mixed_s.at[rows, :], dvln_s.at[rows, :]
            th, u, xhat, rl, vln = _sgu_forward(ha, lng_ref[...], lnb_ref[...], wm, bt_ref[...], mixed_c)
            mixed = mixed_c[...]
            sg = u * mixed
            dsg, dga = _rms_bwd(dycat[:tc, :aw], sg, _rstd(sg), ga_ref[...])
            dga_ref[...] += dga
            du = dsg * mixed
            dmixed = dsg * u
            dmb = dmixed.astype(BF16)
            for n in range(tc // CHUNK):
                blk = slice(n * CHUNK, (n + 1) * CHUNK)
                dbs_ref[...] += dmixed[blk]
                for hh in range(HEADS):
                    cols = slice(hh * hd, (hh + 1) * hd)
                    dvln_c[blk, cols] = _dot_tn(wm[hh], dmb[blk, cols])
                    dwsp_ref[hh] += mask * _dot_nt(dmb[blk, cols], vln[blk, cols])
            dvln = dvln_c[...]
            dlng_ref[...] += jnp.sum(dvln * xhat, axis=0, keepdims=True)
            dlnb_ref[...] += jnp.sum(dvln, axis=0, keepdims=True)
            dxh = dvln * lng_ref[...]
            dv = rl * (dxh - jnp.mean(dxh, axis=-1, keepdims=True) - xhat * jnp.mean(dxh * xhat, axis=-1, keepdims=True))
            dh_ref[rows, :in_a] = (jnp.concatenate([du, dv], axis=-1) * _gelu_grad(ha, th)).astype(BF16)

            dxn = _dot(dh_ref[rows, :], win_ref[...])
            xv = x_ref[rows, :]
            dx, dgm = _rms_bwd(dxn, xv, _rstd(xv), gm_ref[...])
            dgm_ref[...] += dgm
            dx_ref[rows, :] = dx1 + dx

        for r0 in range(0, tm, tc):
            chain(r0)

    full = lambda shape: pl.BlockSpec(shape, lambda i: (0,) * len(shape))
    tok = pl.BlockSpec((tm, d), lambda i: (i, 0))
    nxt = lambda i: (jnp.minimum((i + 1) * hb_blocks, last_blk), 0)
    prv = lambda i: (jnp.maximum(i * hb_blocks - 1, 0), 0)
    return _hosted(
        body, carry, grid=(nt,),
        in_specs=[tok, pl.BlockSpec((HALO, d), nxt), tok, full((1, d)),
                  pl.BlockSpec((tm, n_in), lambda i: (i, 0)), pl.BlockSpec((HALO, n_in), prv), pl.BlockSpec((HALO, n_in), nxt),
                  full((1, aw)), full((1, aw)), full((HEADS, CHUNK, CHUNK)), full((CHUNK, HEADS)), full((3, bw)),
                  full((1, aw)), full((1, bw)), full((d, d)), full((n_in, d))],
        out_specs=[pl.BlockSpec((tm, n_in), lambda i: (i, 0)), tok,
                   full((1, aw)), full((1, bw)), full((SUB, bw)), full((1, aw)), full((1, aw)),
                   full((HEADS, CHUNK, CHUNK)), full((CHUNK, aw)), full((1, d))],
        out_shape=(SDS((s, n_in), BF16), SDS((s, d), F32),
                   SDS((1, aw), F32), SDS((1, bw), F32), SDS((SUB, bw), F32), SDS((1, aw), F32), SDS((1, aw), F32),
                   SDS((HEADS, CHUNK, CHUNK), F32), SDS((CHUNK, aw), F32), SDS((1, d), F32)),
        scratch_shapes=[pltpu.VMEM((tm, aw), F32), pltpu.VMEM((tm, aw), F32)],
        compiler_params=_arb(1), name="mix_backward")(dx1, dx1, x, g_mix, h, h, h, lng, lnb, w_sp, bt, conv_w, ga, gb, w_out, w_in)


def _bias_grad(dbs):
    aw = dbs.shape[1]
    hd = aw // HEADS

    def body(dbs_ref, out_ref):
        ones = jnp.ones((SUB, hd), F32)
        for hh in range(HEADS):
            r = lax.dot_general(ones, dbs_ref[:, hh * hd:(hh + 1) * hd], (((1,), (1,)), ((), ())),
                                precision=lax.Precision.HIGHEST, preferred_element_type=F32)
            out_ref[hh:hh + 1, :] = r[0:1]

    return _pcall(body, out_shape=SDS((HEADS, CHUNK), F32), name="bias_grad")(dbs)


def _wgrad_body(a_ref, b_ref, o_ref):
    o_ref[...] = _dot_tn(a_ref[...], b_ref[...])


def _wgrad(a, b, name, carry=()):
    k, m = a.shape
    n = b.shape[1]
    tm = _tile(m, 512, LANES)
    tn = _tile(n, 1024, LANES)
    return _hosted(
        functools.partial(_wgrad_body), carry, grid=(m // tm, n // tn),
        in_specs=[pl.BlockSpec((k, tm), lambda i, j: (0, i)), pl.BlockSpec((k, tn), lambda i, j: (0, j))],
        out_specs=pl.BlockSpec((tm, tn), lambda i, j: (i, j)),
        out_shape=SDS((m, n), F32), compiler_params=_arb(2), name=name)(a, b)


def _wgrad_blocked_lhs(a, b, name, carry=()):
    nb, k, t = a.shape
    n = b.shape[1]
    tn = _tile(n, 1024, LANES)
    return _hosted(
        functools.partial(_wgrad_body), carry, grid=(nb, n // tn),
        in_specs=[pl.BlockSpec((None, k, t), lambda i, j: (i, 0, 0)), pl.BlockSpec((k, tn), lambda i, j: (0, j))],
        out_specs=pl.BlockSpec((t, tn), lambda i, j: (i, j)),
        out_shape=SDS((nb * t, n), F32), compiler_params=_arb(2), name=name)(a, b)


def _place():
    x, y, c = lax.axis_index("x"), lax.axis_index("y"), lax.axis_index("c")
    return x, y, c, [(1 - x, y), (x, 1 - y), (1 - x, 1 - y)]


def _all_gather(shards):
    n = len(shards)
    slots = 9
    cut = [(s.shape[0] // 32) * 16 for s in shards]

    def build(ins, outs, sems):
        send_sems, recv_sems, local_sems = sems
        x, y, c, _ = _place()
        me, sib, xn, yn, dg = (x, y, c), (x, y, 1 - c), (1 - x, y, c), (x, 1 - y, c), (1 - x, 1 - y, c)
        other = lambda p: (p[0], p[1], 1 - p[2])

        def rows(a, p, part=None):
            ref = outs[a].at[4 * p[0] + 2 * p[1] + p[2]]
            if part is None or cut[a] == 0:
                return ref if part in (None, 0) else None
            return ref.at[pl.ds(0, cut[a])] if part == 0 else ref.at[pl.ds(cut[a], shards[a].shape[0] - cut[a])]

        def copy(a, k, ref, to, src=None):
            if ref is None:
                return None
            return pltpu.make_async_remote_copy(
                src_ref=ref if src is None else src, dst_ref=ref, send_sem=send_sems.at[slots * a + k],
                recv_sem=recv_sems.at[slots * a + k], device_id=to, device_id_type=MESH)

        def real(cps):
            return [cp for cp in cps if cp is not None]

        class Copies:
            own = lambda a: [copy(a, 1, rows(a, me), xn, ins[a]), copy(a, 2, rows(a, me), yn, ins[a]),
                             copy(a, 0, rows(a, me), sib, ins[a])]
            local = lambda a: pltpu.make_async_copy(ins[a], rows(a, me), local_sems.at[a])
            from_x = lambda a: copy(a, 1, rows(a, xn), me)
            from_y = lambda a: copy(a, 2, rows(a, yn), me)
            after_x = lambda a: real([copy(a, 4, rows(a, xn, 1), yn), copy(a, 5, rows(a, xn), sib)])
            after_y = lambda a: real([copy(a, 3, rows(a, yn, 0), xn), copy(a, 6, rows(a, yn), sib)])
            diag_in = lambda a: real([copy(a, 3, rows(a, dg, 0), me), copy(a, 4, rows(a, dg, 1), me)])
            diag_on = lambda a: real([copy(a, 7, rows(a, dg, 0), sib), copy(a, 8, rows(a, dg, 1), sib)])
            from_sib = lambda a: real([copy(a, 0, rows(a, sib), me), copy(a, 5, rows(a, other(xn)), me),
                                       copy(a, 6, rows(a, other(yn)), me), copy(a, 7, rows(a, other(dg), 0), me),
                                       copy(a, 8, rows(a, other(dg), 1), me)])

        return Copies

    def start(ins, outs, sems):
        cps = build(ins, outs, sems)
        for a in range(n):
            for cp in cps.own(a):
                cp.start()
        for a in range(n):
            cps.local(a).start()

    def relay(ins, outs, sems):
        cps = build(ins, outs, sems)
        for a in range(n):
            cps.from_x(a).wait_recv()
            for cp in cps.after_x(a):
                cp.start()
            cps.from_y(a).wait_recv()
            for cp in cps.after_y(a):
                cp.start()

    def finish(ins, outs, sems):
        cps = build(ins, outs, sems)
        for a in range(n):
            for arrived, onward in zip(cps.diag_in(a), cps.diag_on(a)):
                arrived.wait_recv()
                onward.start()
        for a in range(n):
            for cp in cps.from_sib(a):
                cp.wait_recv()
            for cp in cps.own(a) + cps.after_x(a) + cps.after_y(a) + cps.diag_on(a):
                cp.wait_send()
            cps.local(a).wait()

    return _Exchange(shards, [SDS((N_DEV,) + s.shape, s.dtype) for s in shards],
                     [pltpu.SemaphoreType.DMA((slots * n,)), pltpu.SemaphoreType.DMA((slots * n,)),
                      pltpu.SemaphoreType.DMA((n,))], start, finish, relay)


def _swap_exchange(ins, out_shape, per, copies):
    def start(i, o, sems):
        for cp in copies(i, o, sems):
            cp.start()

    def finish(i, o, sems):
        for cp in copies(i, o, sems):
            cp.wait()

    n = per * len(ins)
    return _Exchange(ins, out_shape, [pltpu.SemaphoreType.DMA((n,)), pltpu.SemaphoreType.DMA((n,))], start, finish)


def _exchange_c(gs):
    def copies(ins, outs, sems):
        x, y, c, _ = _place()
        return [pltpu.make_async_remote_copy(
                    src_ref=ins[a].at[2 * k + 1 - c], dst_ref=outs[a].at[k],
                    send_sem=sems[0].at[4 * a + k], recv_sem=sems[1].at[4 * a + k],
                    device_id=(x, y, 1 - c), device_id_type=MESH)
                for a in range(len(gs)) for k in range(4)]

    return _swap_exchange(gs, [SDS((4,) + g.shape[1:], g.dtype) for g in gs], 4, copies)


def _rs_combine(g, recv, pos, name, carry=()):
    _, r, cdim = g.shape
    tr = _tile(r, 256, 16)

    def body(pos_ref, g0, r0, g1, r1, g2, r2, g3, r3, keep_ref, send_ref):
        keep_ref[...] = g0[...] + r0[...]
        send_ref[0] = (g1[...] + r1[...]).astype(BF16)
        send_ref[1] = (g2[...] + r2[...]).astype(BF16)
        send_ref[2] = (g3[...] + r3[...]).astype(BF16)

    def k_of(p, t):
        px = p[0] if t in (0, 2) else 1 - p[0]
        py = p[1] if t in (0, 1) else 1 - p[1]
        return 2 * px + py

    blk = (None, tr, cdim)
    in_specs = []
    for t in range(4):
        in_specs.append(pl.BlockSpec(blk, functools.partial(lambda j, p, t: (2 * k_of(p, t) + p[2], j, 0), t=t)))
        in_specs.append(pl.BlockSpec(blk, functools.partial(lambda j, p, t: (k_of(p, t), j, 0), t=t)))
    return _hosted(
        body, carry, n_prefetch=1, out_shape=(SDS((r, cdim), F32), SDS((3, r, cdim), BF16)),
        grid=(r // tr,), in_specs=in_specs,
        out_specs=[pl.BlockSpec((tr, cdim), lambda j, p: (j, 0)), pl.BlockSpec((3, tr, cdim), lambda j, p: (0, j, 0))],
        compiler_params=_arb(1), name=name)(pos, g, recv, g, recv, g, recv, g, recv)


def _adamw_shard(keep, recv, w, m, v, name):
    r, cdim = w.shape
    tr = _tile(r, 256, 16)

    def body(k_ref, r_ref, w_ref, m_ref, v_ref, g_ref, d_ref, nm_ref, nv_ref):
        g = ((k_ref[...] + r_ref[0].astype(F32)) + r_ref[1].astype(F32)) + r_ref[2].astype(F32)
        g_ref[...] = g
        d_ref[...], nm_ref[...], nv_ref[...] = _adamw(w_ref[...], g, m_ref[...], v_ref[...])

    blk = pl.BlockSpec((tr, cdim), lambda j: (j, 0))
    out = SDS((r, cdim), F32)
    return _pcall(body, grid=(r // tr,), in_specs=[blk, pl.BlockSpec((3, tr, cdim), lambda j: (0, j, 0)), blk, blk, blk],
                  out_specs=[blk] * 4, out_shape=(out,) * 4, compiler_params=_arb(1), name=name)(keep, recv, w, m, v)


_HBM = pl.BlockSpec(memory_space=pltpu.HBM)
_SEM = pl.BlockSpec(memory_space=pltpu.SEMAPHORE)
_SPLIT = pltpu.CompilerParams(has_side_effects=pltpu.SideEffectType.DATAFLOW_SIDE_EFFECTING)


def _split_copies(kind, n, refs):
    srcs, lands, (send_sems, recv_sems) = refs[:n], refs[n:2 * n], refs[2 * n:2 * n + 2]
    x, y, c, chips = _place()
    per = _SPLIT_COPIES[kind]
    if kind == "xy":
        ends = lambda a, t: (srcs[a].at[t], lands[a].at[t], (*chips[t], c))
    else:
        ends = lambda a, k: (srcs[a].at[2 * k + 1 - c], lands[a].at[k], (x, y, 1 - c))
    cps = []
    for a in range(n):
        for t in range(per):
            src, dst, to = ends(a, t)
            cps.append(pltpu.make_async_remote_copy(src_ref=src, dst_ref=dst, send_sem=send_sems.at[per * a + t],
                                                    recv_sem=recv_sems.at[per * a + t], device_id=to, device_id_type=MESH))
    return cps


_SPLIT_COPIES = {"xy": 3, "c": 4}
SIBLING_COLLECTIVE = 1


def _exchange_start(kind, arrays, name, after=None, collective=None):
    n = len(arrays)
    order = [] if after is None else [after]

    def body(*refs):
        x, y, c, chips = _place()
        peers = [(x, y, 1 - c)] if kind == "c" else [(*chip, c) for chip in chips]
        barrier = pltpu.get_barrier_semaphore()
        for peer in peers:
            pl.semaphore_signal(barrier, inc=1, device_id=peer, device_id_type=MESH)
        pl.semaphore_wait(barrier, len(peers))
        refs = refs[:2 * n] + refs[2 * n + len(order):]
        for cp in _split_copies(kind, n, refs):
            cp.start()
        refs[-1][...] = jnp.zeros_like(refs[-1])

    params = pltpu.CompilerParams(has_side_effects=pltpu.SideEffectType.DATAFLOW_SIDE_EFFECTING,
                                  collective_id=SIBLING_COLLECTIVE if kind == "c" else collective)
    hbm = lambda a: pltpu.with_memory_space_constraint(a, pltpu.HBM)
    land = [a.shape if kind == "xy" else (4,) + a.shape[1:] for a in arrays]
    bufs = [pltpu.HBM(a.shape, a.dtype) for a in arrays] + [pltpu.HBM(s, a.dtype) for s, a in zip(land, arrays)]
    sems = pltpu.SemaphoreType.DMA((_SPLIT_COPIES[kind] * n,))
    res = _pcall(
        body, name=name, out_shape=(sems, sems, *bufs, SDS((SUB, LANES), F32)),
        in_specs=[_HBM] * (2 * n) + _hbm_specs(len(order)),
        out_specs=[_SEM, _SEM] + [_HBM] * (2 * n) + [pl.BlockSpec(memory_space=pltpu.VMEM)],
        input_output_aliases={k: 2 + k for k in range(2 * n)}, compiler_params=params)(
            *[hbm(a) for a in arrays], *[hbm(lax.empty(s, a.dtype)) for s, a in zip(land, arrays)], *order)
    return (kind, n, res[:-1]), res[-1]


def _exchange_wait(started, after, name, sources=False):
    kind, n, (send_sems, recv_sems, *bufs) = started

    def body(*refs):
        for cp in _split_copies(kind, n, refs):
            cp.wait_send()
            cp.wait_recv()

    shapes = [pltpu.HBM(b.shape, b.dtype) for b in bufs]
    res = _pcall(
        body, name=name, out_shape=tuple(shapes),
        in_specs=[_HBM] * (2 * n) + [_SEM, _SEM, pl.BlockSpec(memory_space=pl.ANY)], out_specs=[_HBM] * (2 * n),
        input_output_aliases={k: k for k in range(2 * n)}, compiler_params=_SPLIT)(*bufs, send_sems, recv_sems, after)
    return (list(res[:n]), list(res[n:])) if sources else list(res[n:])


def _follow(token):
    nothing = lambda ins, outs, sems: None
    return _Exchange([token], [], [], nothing, nothing)


def _adamw_small(gathered, seg, params, conv_rows):
    names = list(params)
    c0, cn = conv_rows

    def body(*refs):
        gat_ref = refs[0]
        ins = refs[1:1 + 3 * len(names)]
        outs = refs[1 + 3 * len(names):]

        def total(r0, rn):
            tot = gat_ref[0, r0:r0 + rn, :]
            for dev in range(1, N_DEV):
                tot = tot + gat_ref[dev, r0:r0 + rn, :]
            return tot

        for k, nm in enumerate(names):
            g = total(*seg[nm])
            w_ref, m_ref, v_ref = ins[3 * k:3 * k + 3]
            g_ref, d_ref, nm_ref, nv_ref = outs[4 * k:4 * k + 4]
            g_ref[...] = g
            d_ref[...], nm_ref[...], nv_ref[...] = _adamw(w_ref[...], g, m_ref[...], v_ref[...])
        outs[-2][...] = total(c0, cn)
        outs[-1][...] = total(*seg["loss"])

    flat_in = [a for nm in names for a in params[nm]]
    out_shape = []
    for nm in names:
        out_shape += [SDS(params[nm][0].shape, F32)] * 4
    out_shape += [SDS((cn, LANES), F32), SDS((seg["loss"][1], LANES), F32)]
    res = _pcall(body, out_shape=tuple(out_shape), name="adamw_small")(gathered, *flat_in)
    per = {nm: res[4 * k:4 * k + 4] for k, nm in enumerate(names)}
    return per, res[-2], res[-1]


def _adamw_one(w, g, m, v, name):
    def body(w_ref, g_ref, m_ref, v_ref, d_ref, nm_ref, nv_ref):
        d_ref[...], nm_ref[...], nv_ref[...] = _adamw(w_ref[...], g_ref[...], m_ref[...], v_ref[...])

    return _pcall(body, out_shape=(SDS(w.shape, F32),) * 3, name=name)(w, g, m, v)


def _rows128(a):
    return a.reshape(-1, LANES)


def _pack_small(gs, loss_tile):
    seg, pieces, row = {}, [], 0
    for nm in SMALL + ("conv_w", "loss"):
        piece = loss_tile if nm == "loss" else _rows128(gs[nm])
        rn = _round_up(piece.shape[0], SUB)
        pieces.append(jnp.pad(piece, ((0, rn - piece.shape[0]), (0, 0))))
        seg[nm] = (row, piece.shape[0])
        row += rn
    return jnp.concatenate(pieces, axis=0), seg


def _step(x, mem, target, wb, conv_w, sp, pos):
    s, d = x.shape
    tm = min(TOKEN_TILE, s)
    tm_wide = min(2 * TOKEN_TILE, s)
    rows = lambda w8: w8.reshape(-1, w8.shape[2])
    shards = lambda g: g.reshape((N_DEV, -1) + g.shape[1:])
    bt = sp["b_spatial"].T

    (w_in8, conv8), = _run_exchanges([_all_gather([wb["w_in"], conv_w])], "gather_w_in")
    conv_full = conv8.transpose(1, 0, 2).reshape(3, -1)
    w_in_t = rows(w_in8)
    (xn1, h), ((w_out8, w_kv8, w_q8),) = _in_forward(
        x, sp["ln_mix_g"], w_in_t, tm, carry=[_all_gather([wb["w_out"], wb["w_kv"], wb["w_q"]])])
    w_out = rows(w_out8)
    (ycat, x1), ((w_o8, w_down8),) = _mix_forward(
        h, x, sp["sgu_ln_g"], sp["sgu_ln_b"], sp["w_spatial"], bt, conv_full, sp["grp_norm_a"], sp["grp_norm_b"], w_out, tm,
        carry=[_all_gather([wb["w_o"], wb["w_down"]])])
    w_q, w_o, w_down = rows(w_q8), rows(w_o8), rows(w_down8)
    memn, kv = _kv_forward(mem, sp["ln_mem_g"], w_kv8)
    (xn2, q, probs, o, x2), ((w_gu8,),) = _attn_forward(
        x1, sp["ln_attn_g"], w_q, kv, w_o, tm, carry=[_all_gather([wb["w_gate_up"]])])
    w_gu = w_gu8.reshape((2, N_DEV // 2) + w_gu8.shape[1:])
    xn3, gu, x3 = _ffn_forward(x2, sp["ln_ffn_g"], w_gu, w_down, tm_wide)

    loss, d_lnf, dx3, dx3b = _final_backward(x3, target, sp["ln_final_g"], tm_wide)
    act, dgu, dxn3 = _swiglu_backward(dx3b, gu, w_gu, w_down, tm_wide)
    g_gu, _ = _wgrad_blocked_lhs(dgu.reshape((N_DEV,) + dgu.shape[2:]), xn3, "wgrad_gate_up")
    g_gu = shards(g_gu)
    g_down, ((rc_gu,),) = _wgrad_blocked_lhs(act, dx3b, "wgrad_down", carry=[_exchange_c([g_gu])])
    g_down = shards(g_down)
    keep, pending = {}, []
    (keep["w_gate_up"], send_gu), _ = _rs_combine(g_gu, rc_gu, pos, "rs_combine_w_gate_up")
    started, token = _exchange_start("xy", [send_gu], "exchange_xy_1_start", collective=2)
    pending.append((("w_gate_up",), started))
    c_down, token = _exchange_start("c", [g_down], "exchange_c_1_start", after=token)
    (dx2b, dq, dx1, dx1b, dkv, d_lnattn, d_lnffn), _ = _attn_backward(
        dx3, dxn3, x2, sp["ln_ffn_g"], x1, sp["ln_attn_g"], q, probs, kv, w_q, w_o, tm, carry=[_follow(token)])
    (g_down,), (rc_down,) = _exchange_wait(c_down, dx1b, "exchange_c_1_wait", sources=True)
    (keep["w_down"], send_down), _ = _rs_combine(g_down, rc_down, pos, "rs_combine_w_down")
    g_o, _ = _wgrad(o, dx2b, "wgrad_o")
    g_q, _ = _wgrad(xn2, dq, "wgrad_q")
    g_o, g_q = shards(g_o), shards(g_q)
    g_kv, d_lnmem = _kv_backward(dkv, memn, mem, sp["ln_mem_g"], w_kv8)
    c_oqkv, token = _exchange_start("c", [g_o, g_q, g_kv], "exchange_c_2_start")
    g_out, _ = _wgrad(ycat, dx1b, "wgrad_out", carry=[_follow(token)])
    g_out = shards(g_out)
    (g_o, g_q, g_kv), (rc_o, rc_q, rc_kv) = _exchange_wait(c_oqkv, g_out, "exchange_c_2_wait", sources=True)
    c_out, token = _exchange_start("c", [g_out], "exchange_c_3_start")
    (keep["w_o"], send_o), _ = _rs_combine(g_o, rc_o, pos, "rs_combine_w_o", carry=[_follow(token)])
    (keep["w_q"], send_q), _ = _rs_combine(g_q, rc_q, pos, "rs_combine_w_q")
    (keep["w_kv"], send_kv), _ = _rs_combine(g_kv, rc_kv, pos, "rs_combine_w_kv")
    (g_out,), (rc_out,) = _exchange_wait(c_out, send_kv, "exchange_c_3_wait", sources=True)
    (keep["w_out"], send_out), _ = _rs_combine(g_out, rc_out, pos, "rs_combine_w_out")
    started, token = _exchange_start("xy", [send_down, send_o, send_q, send_out, send_kv], "exchange_xy_2_start",
                                     collective=3)
    pending.append((("w_down", "w_o", "w_q", "w_out", "w_kv"), started))
    (dh, dx, d_ga, d_gb, d_cw, d_lng, d_lnb, d_wsp, d_bs, d_lnmix), _ = _mix_backward(
        dx1, x, sp["ln_mix_g"], h, sp["sgu_ln_g"], sp["sgu_ln_b"], sp["w_spatial"], bt, conv_full,
        sp["grp_norm_a"], sp["grp_norm_b"], w_out, w_in_t, tm, carry=[_follow(token)])
    gs = {"ln_mix_g": d_lnmix, "sgu_ln_g": d_lng, "sgu_ln_b": d_lnb, "w_spatial": d_wsp, "b_spatial": _bias_grad(d_bs),
          "conv_w": d_cw[:3], "grp_norm_a": d_ga, "grp_norm_b": d_gb, "ln_attn_g": d_lnattn, "ln_mem_g": d_lnmem,
          "ln_ffn_g": d_lnffn, "ln_final_g": d_lnf}
    packed, seg = _pack_small(gs, loss)
    g_in, (_, (small_all,)) = _wgrad(dh, xn1, "wgrad_in", carry=[_follow(token), _all_gather([packed])])
    g_in = shards(g_in)
    c_in, token = _exchange_start("c", [g_in], "exchange_c_4_start")
    return dx, keep, pending, (g_in, c_in), token, small_all, seg


def kernel(x, mem, ln_mix_g, w_in, sgu_ln_g, sgu_ln_b, w_spatial, b_spatial, conv_w, grp_norm_a, grp_norm_b, w_out, ln_attn_g, ln_mem_g, w_q, w_kv, w_o, ln_ffn_g, w_gate_up, w_down, ln_final_g, loss_target, m_ln_mix_g, m_w_in, m_sgu_ln_g, m_sgu_ln_b, m_w_spatial, m_b_spatial, m_conv_w, m_grp_norm_a, m_grp_norm_b, m_w_out, m_ln_attn_g, m_ln_mem_g, m_w_q, m_w_kv, m_w_o, m_ln_ffn_g, m_w_gate_up, m_w_down, m_ln_final_g, v_ln_mix_g, v_w_in, v_sgu_ln_g, v_sgu_ln_b, v_w_spatial, v_b_spatial, v_conv_w, v_grp_norm_a, v_grp_norm_b, v_w_out, v_ln_attn_g, v_ln_mem_g, v_w_q, v_w_kv, v_w_o, v_ln_ffn_g, v_w_gate_up, v_w_down, v_ln_final_g):
    order = ["ln_mix_g", "w_in", "sgu_ln_g", "sgu_ln_b", "w_spatial", "b_spatial", "conv_w", "grp_norm_a", "grp_norm_b",
             "w_out", "ln_attn_g", "ln_mem_g", "w_q", "w_kv", "w_o", "ln_ffn_g", "w_gate_up", "w_down", "ln_final_g"]
    W = dict(ln_mix_g=ln_mix_g, w_in=w_in, sgu_ln_g=sgu_ln_g, sgu_ln_b=sgu_ln_b, w_spatial=w_spatial, b_spatial=b_spatial,
             conv_w=conv_w, grp_norm_a=grp_norm_a, grp_norm_b=grp_norm_b, w_out=w_out, ln_attn_g=ln_attn_g,
             ln_mem_g=ln_mem_g, w_q=w_q, w_kv=w_kv, w_o=w_o, ln_ffn_g=ln_ffn_g, w_gate_up=w_gate_up, w_down=w_down,
             ln_final_g=ln_final_g)
    M = dict(ln_mix_g=m_ln_mix_g, w_in=m_w_in, sgu_ln_g=m_sgu_ln_g, sgu_ln_b=m_sgu_ln_b, w_spatial=m_w_spatial,
             b_spatial=m_b_spatial, conv_w=m_conv_w, grp_norm_a=m_grp_norm_a, grp_norm_b=m_grp_norm_b, w_out=m_w_out,
             ln_attn_g=m_ln_attn_g, ln_mem_g=m_ln_mem_g, w_q=m_w_q, w_kv=m_w_kv, w_o=m_w_o, ln_ffn_g=m_ln_ffn_g,
             w_gate_up=m_w_gate_up, w_down=m_w_down, ln_final_g=m_ln_final_g)
    V = dict(ln_mix_g=v_ln_mix_g, w_in=v_w_in, sgu_ln_g=v_sgu_ln_g, sgu_ln_b=v_sgu_ln_b, w_spatial=v_w_spatial,
             b_spatial=v_b_spatial, conv_w=v_conv_w, grp_norm_a=v_grp_norm_a, grp_norm_b=v_grp_norm_b, w_out=v_w_out,
             ln_attn_g=v_ln_attn_g, ln_mem_g=v_ln_mem_g, w_q=v_w_q, w_kv=v_w_kv, w_o=v_w_o, ln_ffn_g=v_ln_ffn_g,
             w_gate_up=v_w_gate_up, w_down=v_w_down, ln_final_g=v_ln_final_g)

    bw = conv_w.shape[1] * N_DEV
    pos = jnp.stack([lax.axis_index("x"), lax.axis_index("y"), lax.axis_index("c")]).astype(jnp.int32)
    me = 4 * pos[0] + 2 * pos[1] + pos[2]

    sp = {nm: (W[nm].reshape(1, -1) if W[nm].ndim == 1 else W[nm]) for nm in SMALL}
    view = lambda a, nm: a.T if nm in TRANSPOSED else a
    wb = {nm: view(W[nm], nm).astype(BF16) for nm in BIG}
    grad_x, keep, pending, (g_in, c_in), token, small_all, seg = _step(
        x[0], mem[0], loss_target[0], wb, conv_w, sp, pos)

    out = {}

    def update(k, names, started, token):
        landed = _exchange_wait(started, token, "exchange_xy_%d_wait" % k)
        for nm, rxy in zip(names, landed):
            res = _adamw_shard(keep[nm], rxy, view(W[nm], nm), view(M[nm], nm), view(V[nm], nm), "adamw_" + nm)
            out[nm] = tuple(view(a, nm) for a in res)
            token = res[0]
        return token

    token = update(1, *pending[0], token)
    (g_in,), (rc_in,) = _exchange_wait(c_in, token, "exchange_c_4_wait", sources=True)
    (keep["w_in"], send_in), _ = _rs_combine(g_in, rc_in, pos, "rs_combine_w_in")
    xy_in, token = _exchange_start("xy", [send_in], "exchange_xy_3_start", collective=4)
    token = update(2, *pending[1], token)

    params = {nm: (_rows128(W[nm]), _rows128(M[nm]), _rows128(V[nm])) for nm in SMALL}
    per, conv_g_rows, loss_sum = _adamw_small(small_all, seg, params, seg["conv_w"])
    for nm in SMALL:
        out[nm] = tuple(a.reshape(W[nm].shape) for a in per[nm])
    conv_g = lax.dynamic_slice_in_dim(conv_g_rows.reshape(3, bw), me * conv_w.shape[1], conv_w.shape[1], axis=1)
    out["conv_w"] = (conv_g,) + tuple(_adamw_one(conv_w, conv_g, m_conv_w, v_conv_w, "adamw_conv"))

    update(3, ("w_in",), xy_in, token[:1, :1] + out["conv_w"][1][:1, :1])

    loss = loss_sum[0, 0]
    res = [loss, grad_x[None]]
    for k in range(4):
        res += [out[nm][k] for nm in order]
    return tuple(res)
```

```python
import functools

import jax
import jax.numpy as jnp
from jax import lax
from jax.experimental import pallas as pl
from jax.experimental.pallas import tpu as pltpu

F32 = jnp.float32
BF16 = jnp.bfloat16
SDS = jax.ShapeDtypeStruct
MESH = pl.DeviceIdType.MESH

EPS = 1e-6
N_DEV = 8
HEADS = 4
CHUNK = 128
HALO = 16
SUB = 8
LANES = 128
TOKEN_TILE = 512
ROW_CHUNK = 256
RELAY_AT = 0.7

ADAM_LR = 0.001
ADAM_B1 = 0.9
ADAM_B2 = 0.999
ADAM_EPS = 1e-08
ADAM_WD = 0.01
ADAM_STEP = 10

BIG = ("w_in", "w_out", "w_q", "w_kv", "w_o", "w_gate_up", "w_down")
TRANSPOSED = ("w_in", "w_gate_up")
SMALL = ("ln_mix_g", "sgu_ln_g", "sgu_ln_b", "w_spatial", "b_spatial", "grp_norm_a", "grp_norm_b",
         "ln_attn_g", "ln_mem_g", "ln_ffn_g", "ln_final_g")


class _Exchange:
    def __init__(self, ins, out_shape, sems, start, finish, relay=None, peers=None, collective=None):
        self.ins, self.out_shape, self.sems = list(ins), list(out_shape), list(sems)
        self.start, self.finish, self.relay = start, finish, relay
        self.peers, self.collective = peers, collective


def _pcall(body, carry=(), n_prefetch=0, **kw):
    if carry:
        return functools.partial(_carrying_call, body, tuple(carry), n_prefetch, kw)
    if n_prefetch:
        kw["grid_spec"] = pltpu.PrefetchScalarGridSpec(
            num_scalar_prefetch=n_prefetch, grid=kw.pop("grid"), in_specs=kw.pop("in_specs"),
            out_specs=kw.pop("out_specs"), scratch_shapes=kw.pop("scratch_shapes", ()))
    return pl.pallas_call(body, **kw)


def _carrying_call(body, carry, n_prefetch, kw, *args):
    kw = dict(kw)
    out_shape = kw.pop("out_shape")
    single = not isinstance(out_shape, (tuple, list))
    outs_shape = (out_shape,) if single else tuple(out_shape)
    out_specs = kw.pop("out_specs")
    out_specs = [out_specs] if single else list(out_specs)
    in_specs = list(kw.pop("in_specs"))
    scratch = list(kw.pop("scratch_shapes", ()))
    grid = tuple(kw.get("grid", ()))
    n_in, n_out, n_scr = len(args), len(outs_shape), len(scratch)
    copying = [p for p in carry if p.sems]
    shaking = copying[0] if len(copying) == 1 and copying[0].collective is not None else None
    if shaking is not None:
        old = kw.get("compiler_params")
        kw["compiler_params"] = pltpu.CompilerParams(
            dimension_semantics=None if old is None else old.dimension_semantics, collective_id=shaking.collective)

    def split(refs, k, counts):
        parts = []
        for cnt in counts:
            parts.append(refs[k:k + cnt])
            k += cnt
        return parts, k

    def wrapped(*refs):
        cins, k = split(refs, n_in, [len(p.ins) for p in carry])
        outs = refs[k:k + n_out]
        couts, k = split(refs, k + n_out, [len(p.out_shape) for p in carry])
        scr = refs[k:k + n_scr]
        csems, _ = split(refs, k + n_scr, [len(p.sems) for p in carry])
        first, last = True, True
        for a, g in enumerate(grid):
            first = (pl.program_id(a) == 0) & first
            last = (pl.program_id(a) == g - 1) & last

        def start_all():
            if shaking is not None:
                peers = shaking.peers()
                barrier = pltpu.get_barrier_semaphore()
                for peer in peers:
                    pl.semaphore_signal(barrier, inc=1, device_id=peer, device_id_type=MESH)
                pl.semaphore_wait(barrier, len(peers))
            for p, ci, co, cs in zip(carry, cins, couts, csems):
                p.start(ci, co, cs)

        def relay_all():
            for p, ci, co, cs in zip(carry, cins, couts, csems):
                if p.relay is not None:
                    p.relay(ci, co, cs)

        def finish_all():
            for p, ci, co, cs in zip(carry, cins, couts, csems):
                p.finish(ci, co, cs)

        if len(grid) == 1:
            relay_now = pl.program_id(0) == min(int(RELAY_AT * grid[0]), grid[0] - 1)
        else:
            relay_now = last
        start_all() if not grid else pl.when(first)(start_all)
        relay_all() if not grid else pl.when(relay_now)(relay_all)
        body(*refs[:n_in], *outs, *scr)
        finish_all() if not grid else pl.when(last)(finish_all)

    c_in = [a for p in carry for a in p.ins]
    c_out = [s for p in carry for s in p.out_shape]
    c_sems = [s for p in carry for s in p.sems]
    res = _pcall(wrapped, n_prefetch=n_prefetch, out_shape=outs_shape + tuple(c_out),
                 in_specs=in_specs + _hbm_specs(len(c_in)), out_specs=out_specs + _hbm_specs(len(c_out)),
                 scratch_shapes=scratch + c_sems, **kw)(*args, *c_in)
    own = res[0] if single else tuple(res[:n_out])
    landed, k = [], n_out
    for p in carry:
        landed.append(list(res[k:k + len(p.out_shape)]))
        k += len(p.out_shape)
    return own, landed


def _hbm_specs(n):
    return [pl.BlockSpec(memory_space=pl.ANY)] * n


def _hosted(body, carry, **kw):
    if carry:
        return _pcall(body, carry=carry, **kw)
    call = _pcall(body, **kw)
    return lambda *args: (call(*args), [])


def _run_exchanges(parts, name):
    def body(*refs):
        pass

    _, landed = _pcall(body, carry=parts, out_shape=(), in_specs=[], out_specs=[], name=name)()
    return landed


def _arb(n):
    return pltpu.CompilerParams(dimension_semantics=("arbitrary",) * n)


def _tile(n, target, mult):
    best = None
    for t in range(mult, min(n, target) + 1, mult):
        if n % t == 0:
            best = t
    return n if best is None else best


def _round_up(n, m):
    return (n + m - 1) // m * m


def _dot(a, b):
    return jnp.dot(a, b, preferred_element_type=F32)


def _dot_nt(a, b):
    return lax.dot_general(a, b, (((1,), (1,)), ((), ())), preferred_element_type=F32)


def _dot_tn(a, b):
    return lax.dot_general(a, b, (((0,), (0,)), ((), ())), preferred_element_type=F32)


def _rstd(x):
    return lax.rsqrt(jnp.mean(x * x, axis=-1, keepdims=True) + EPS)


def _rms_bwd(dy, x, r, g):
    gdy = dy * g
    proj = jnp.sum(gdy * x, axis=-1, keepdims=True) * (1.0 / x.shape[-1])
    dx = r * gdy - x * (r * r * r) * proj
    dg = jnp.sum(dy * (x * r), axis=0, keepdims=True)
    return dx, dg


_GELU_C = 0.7978845608028654
_GELU_A = 0.044715


def _gelu(x):
    t = jnp.tanh(_GELU_C * (x + _GELU_A * x * x * x))
    return 0.5 * x * (1.0 + t), t


def _gelu_grad(x, t):
    return 0.5 * (1.0 + t) + 0.5 * x * (1.0 - t * t) * (_GELU_C * (1.0 + 3.0 * _GELU_A * x * x))


def _sigmoid(x):
    return 1.0 / (1.0 + jnp.exp(-x))


def _softmax(s):
    m = jnp.max(s, axis=-1, keepdims=True)
    e = jnp.exp(s - m)
    return e / jnp.sum(e, axis=-1, keepdims=True)


def _adamw(w, g, m, v):
    m = ADAM_B1 * m + (1.0 - ADAM_B1) * g
    v = ADAM_B2 * v + (1.0 - ADAM_B2) * (g * g)
    m_hat = m / (1.0 - ADAM_B1 ** ADAM_STEP)
    v_hat = v / (1.0 - ADAM_B2 ** ADAM_STEP)
    delta = -ADAM_LR * (m_hat / (jnp.sqrt(v_hat) + ADAM_EPS) + ADAM_WD * w)
    return delta, m, v


def _tril_mask():
    t = lax.broadcasted_iota(jnp.int32, (CHUNK, CHUNK), 0)
    s = lax.broadcasted_iota(jnp.int32, (CHUNK, CHUNK), 1)
    return (s <= t).astype(F32)


def _sgu_forward(ha, lng, lnb, wm, bt, mixed_s):
    aw = ha.shape[1] // 2
    hd = aw // HEADS
    a, th = _gelu(ha)
    u = a[:, :aw]
    v = a[:, aw:]
    mu = jnp.mean(v, axis=-1, keepdims=True)
    vc = v - mu
    rl = lax.rsqrt(jnp.mean(vc * vc, axis=-1, keepdims=True) + EPS)
    xhat = vc * rl
    vln = (xhat * lng + lnb).astype(BF16)
    for n in range(ha.shape[0] // CHUNK):
        rows = slice(n * CHUNK, (n + 1) * CHUNK)
        for h in range(HEADS):
            cols = slice(h * hd, (h + 1) * hd)
            mixed_s[rows, cols] = _dot(wm[h], vln[rows, cols]) + bt[:, h:h + 1]
    return th, u, xhat, rl, vln


def _conv_taps(zext):
    return pltpu.roll(zext, 2, 0), pltpu.roll(zext, 1, 0)


def _kv_forward(mem, g_mem, w_kv):
    ml, d = mem.shape
    xd = w_kv.shape[2]

    def body(mem_ref, g_ref, w_ref, memn_ref, kv_ref):
        x = mem_ref[...]
        memn = (x * _rstd(x) * g_ref[...]).astype(BF16)
        memn_ref[...] = memn
        for j in range(2 * HEADS):
            kv_ref[j] = _dot(memn, w_ref[j]).astype(BF16)

    return _pcall(body, out_shape=(SDS((ml, d), BF16), SDS((2 * HEADS, ml, xd), BF16)), name="kv_forward")(mem, g_mem, w_kv)


def _in_forward(x, g, w_in_t, tm, carry=()):
    s, d = x.shape
    n_in = w_in_t.shape[0]

    def body(x_ref, g_ref, w_ref, xn_ref, h_ref):
        xv = x_ref[...]
        xn = (xv * _rstd(xv) * g_ref[...]).astype(BF16)
        xn_ref[...] = xn
        h_ref[...] = _dot_nt(xn, w_ref[...])

    return _hosted(
        body, carry, grid=(s // tm,),
        in_specs=[pl.BlockSpec((tm, d), lambda i: (i, 0)), pl.BlockSpec((1, d), lambda i: (0, 0)),
                  pl.BlockSpec((n_in, d), lambda i: (0, 0))],
        out_specs=[pl.BlockSpec((tm, d), lambda i: (i, 0)), pl.BlockSpec((tm, n_in), lambda i: (i, 0))],
        out_shape=(SDS((s, d), BF16), SDS((s, n_in), F32)),
        compiler_params=_arb(1), name="in_forward")(x, g, w_in_t)


def _mix_forward(h, x, lng, lnb, w_sp, bt, conv_w, ga, gb, w_out, tm, carry=()):
    s, d = x.shape
    n_in = h.shape[1]
    aw = lng.shape[1]
    bw = d - aw
    in_a = 2 * aw
    hb_blocks = tm // HALO

    def body(h_ref, hprev_ref, x_ref, lng_ref, lnb_ref, wsp_ref, bt_ref, cw_ref, ga_ref, gb_ref, wout_ref,
             ycat_ref, x1_ref, mixed_s):
        i = pl.program_id(0)
        mask = _tril_mask()
        wm = [(wsp_ref[hh] * mask).astype(BF16) for hh in range(HEADS)]
        hv = h_ref[...]
        _, u, _, _, _ = _sgu_forward(hv[:, :in_a], lng_ref[...], lnb_ref[...], wm, bt_ref[...], mixed_s)
        sg = u * mixed_s[...]
        ycat_ref[:, :aw] = (sg * _rstd(sg) * ga_ref[...]).astype(BF16)

        gate_b = hv[:, in_a:in_a + bw]
        z = hv[:, in_a + bw:in_a + 2 * bw] * hv[:, in_a + 2 * bw:]
        hp = hprev_ref[...]
        zp = hp[:, in_a + bw:in_a + 2 * bw] * hp[:, in_a + 2 * bw:]
        zp = jnp.where(i == 0, 0.0, zp)
        zext = jnp.concatenate([zp, z], axis=0)
        z2, z1 = _conv_taps(zext)
        cw = cw_ref[...]
        conv = cw[0:1] * z2[HALO:] + cw[1:2] * z1[HALO:] + cw[2:3] * z
        sc = gate_b * conv
        ycat_ref[:, aw:] = (sc * _rstd(sc) * gb_ref[...]).astype(BF16)
        x1_ref[...] = x_ref[...] + _dot(ycat_ref[...], wout_ref[...])

    full = lambda shape: pl.BlockSpec(shape, lambda i: (0,) * len(shape))
    return _hosted(
        body, carry, grid=(s // tm,),
        in_specs=[pl.BlockSpec((tm, n_in), lambda i: (i, 0)),
                  pl.BlockSpec((HALO, n_in), lambda i: (jnp.maximum(i * hb_blocks - 1, 0), 0)),
                  pl.BlockSpec((tm, d), lambda i: (i, 0)),
                  full((1, aw)), full((1, aw)), full((HEADS, CHUNK, CHUNK)), full((CHUNK, HEADS)),
                  full((3, bw)), full((1, aw)), full((1, bw)), full((d, d))],
        out_specs=[pl.BlockSpec((tm, d), lambda i: (i, 0)), pl.BlockSpec((tm, d), lambda i: (i, 0))],
        out_shape=(SDS((s, d), BF16), SDS((s, d), F32)),
        scratch_shapes=[pltpu.VMEM((tm, aw), F32)],
        compiler_params=_arb(1), name="mix_forward")(h, h, x, lng, lnb, w_sp, bt, conv_w, ga, gb, w_out)


def _attn_forward(x1, g, w_q, kv, w_o, tm, carry=()):
    s, d = x1.shape
    _, ml, xd = kv.shape
    scale = xd ** -0.5

    def body(x1_ref, g_ref, wq_ref, kv_ref, wo_ref, xn_ref, q_ref, p_ref, o_ref, x2_ref):
        xv = x1_ref[...]
        xn = (xv * _rstd(xv) * g_ref[...]).astype(BF16)
        xn_ref[...] = xn
        q_ref[...] = _dot(xn, wq_ref[...]).astype(BF16)
        for hh in range(HEADS):
            cols = slice(hh * xd, (hh + 1) * xd)
            p = _softmax(_dot_nt(q_ref[:, cols], kv_ref[hh]) * scale).astype(BF16)
            p_ref[:, hh * ml:(hh + 1) * ml] = p
            o_ref[:, cols] = _dot(p, kv_ref[HEADS + hh]).astype(BF16)
        x2_ref[...] = xv + _dot(o_ref[...], wo_ref[...])

    tok = pl.BlockSpec((tm, d), lambda i: (i, 0))
    probs = pl.BlockSpec((tm, HEADS * ml), lambda i: (i, 0))
    return _hosted(
        body, carry, grid=(s // tm,),
        in_specs=[tok, pl.BlockSpec((1, d), lambda i: (0, 0)), pl.BlockSpec((d, d), lambda i: (0, 0)),
                  pl.BlockSpec((2 * HEADS, ml, xd), lambda i: (0, 0, 0)), pl.BlockSpec((d, d), lambda i: (0, 0))],
        out_specs=[tok, tok, probs, tok, tok],
        out_shape=(SDS((s, d), BF16), SDS((s, d), BF16), SDS((s, HEADS * ml), BF16), SDS((s, d), BF16), SDS((s, d), F32)),
        compiler_params=_arb(1), name="attn_forward")(x1, g, w_q, kv, w_o)


def _ffn_forward(x2, g, w_gu, w_down, tm):
    s, d = x2.shape
    _, nf, tf, _ = w_gu.shape

    def body(x2_ref, g_ref, wgu_ref, wd_ref, xn_ref, gu_ref, x3_ref):
        f = pl.program_id(1)

        @pl.when(f == 0)
        def _():
            xv = x2_ref[...]
            xn_ref[...] = (xv * _rstd(xv) * g_ref[...]).astype(BF16)
            x3_ref[...] = xv

        xn = xn_ref[...]
        gate = _dot_nt(xn, wgu_ref[0])
        up = _dot_nt(xn, wgu_ref[1])
        gu_ref[0] = gate.astype(BF16)
        gu_ref[1] = up.astype(BF16)
        act = (gate * _sigmoid(gate) * up).astype(BF16)
        x3_ref[...] += _dot(act, wd_ref[...])

    tok = pl.BlockSpec((tm, d), lambda i, f: (i, 0))
    return _pcall(
        body, grid=(s // tm, nf),
        in_specs=[tok, pl.BlockSpec((1, d), lambda i, f: (0, 0)),
                  pl.BlockSpec((2, None, tf, d), lambda i, f: (0, f, 0, 0)),
                  pl.BlockSpec((tf, d), lambda i, f: (f, 0))],
        out_specs=[tok, pl.BlockSpec((2, None, tm, tf), lambda i, f: (0, f, i, 0)), tok],
        out_shape=(SDS((s, d), BF16), SDS((2, nf, s, tf), BF16), SDS((s, d), F32)),
        compiler_params=_arb(2), name="ffn_forward")(x2, g, w_gu, w_down)


def _final_backward(x3, target, g_final, tm):
    s, d = x3.shape

    def body(x3_ref, tgt_ref, gf_ref, loss_ref, dgf_ref, dx3_ref, dx3b_ref):
        @pl.when(pl.program_id(0) == 0)
        def _():
            loss_ref[...] = jnp.zeros_like(loss_ref)
            dgf_ref[...] = jnp.zeros_like(dgf_ref)

        xv = x3_ref[...]
        r = _rstd(xv)
        diff = xv * r * gf_ref[...] - tgt_ref[...]
        loss_ref[...] += 0.5 * jnp.sum(jnp.sum(diff * diff, axis=-1, keepdims=True), axis=0, keepdims=True) * (1.0 / d)
        dx3, dgf = _rms_bwd(diff * (1.0 / d), xv, r, gf_ref[...])
        dgf_ref[...] += dgf
        dx3_ref[...] = dx3
        dx3b_ref[...] = dx3.astype(BF16)

    tok = pl.BlockSpec((tm, d), lambda i: (i, 0))
    vec = pl.BlockSpec((1, d), lambda i: (0, 0))
    return _pcall(
        body, grid=(s // tm,), in_specs=[tok, tok, vec],
        out_specs=[pl.BlockSpec((SUB, LANES), lambda i: (0, 0)), vec, tok, tok],
        out_shape=(SDS((SUB, LANES), F32), SDS((1, d), F32), SDS((s, d), F32), SDS((s, d), BF16)),
        compiler_params=_arb(1), name="final_backward")(x3, target, g_final)


def _swiglu_backward(dx3b, gu, w_gu, w_down, tm):
    s, d = dx3b.shape
    _, nf, tf, _ = w_gu.shape

    def body(dx3b_ref, gu_ref, wgu_ref, wd_ref, act_ref, dgu_ref, dxn_ref):
        @pl.when(pl.program_id(1) == 0)
        def _():
            dxn_ref[...] = jnp.zeros_like(dxn_ref)

        for r0 in range(0, tm, ROW_CHUNK):
            rows = slice(r0, r0 + ROW_CHUNK)
            dact = _dot_nt(dx3b_ref[rows, :], wd_ref[...])
            gv = gu_ref[0, rows, :].astype(F32)
            uv = gu_ref[1, rows, :].astype(F32)
            sg = _sigmoid(gv)
            silu = gv * sg
            act_ref[rows, :] = (silu * uv).astype(BF16)
            dgate = (dact * uv * (sg * (1.0 + gv * (1.0 - sg)))).astype(BF16)
            dup = (dact * silu).astype(BF16)
            dgu_ref[0, rows, :] = dgate
            dgu_ref[1, rows, :] = dup
            part = _dot(dgate, wgu_ref[0]) + _dot(dup, wgu_ref[1])
            dxn_ref[rows, :] += part

    tok = pl.BlockSpec((tm, d), lambda i, f: (i, 0))
    pair = pl.BlockSpec((2, None, tm, tf), lambda i, f: (0, f, i, 0))
    return _pcall(
        body, grid=(s // tm, nf),
        in_specs=[tok, pair, pl.BlockSpec((2, None, tf, d), lambda i, f: (0, f, 0, 0)),
                  pl.BlockSpec((tf, d), lambda i, f: (f, 0))],
        out_specs=[pl.BlockSpec((None, tm, tf), lambda i, f: (f, i, 0)), pair, tok],
        out_shape=(SDS((nf, s, tf), BF16), SDS((2, nf, s, tf), BF16), SDS((s, d), F32)),
        compiler_params=_arb(2), name="swiglu_backward")(dx3b, gu, w_gu, w_down)


def _attn_backward(dx3, dxn3, x2, g_ffn, x1, g, q, probs, kv, w_q, w_o, tm, carry=()):
    s, d = x1.shape
    _, ml, xd = kv.shape
    scale = xd ** -0.5

    def body(dx3_ref, dxn3_ref, x2_ref, g2_ref, x1_ref, g_ref, q_ref, p_ref, kv_ref, wq_ref, wo_ref,
             dx2b_ref, dq_ref, dx1_ref, dx1b_ref, dkv_ref, dg_ref, dg2_ref, do_s):
        i = pl.program_id(0)

        @pl.when(i == 0)
        def _():
            dkv_ref[...] = jnp.zeros_like(dkv_ref)
            dg_ref[...] = jnp.zeros_like(dg_ref)
            dg2_ref[...] = jnp.zeros_like(dg2_ref)

        x2v = x2_ref[...]
        dx2n, dg2 = _rms_bwd(dxn3_ref[...], x2v, _rstd(x2v), g2_ref[...])
        dg2_ref[...] += dg2
        dx2 = dx3_ref[...] + dx2n
        dx2b_ref[...] = dx2.astype(BF16)
        do_s[...] = _dot_nt(dx2b_ref[...], wo_ref[...]).astype(BF16)
        for hh in range(HEADS):
            kc = slice(hh * xd, (hh + 1) * xd)
            qh = q_ref[:, kc]
            kh = kv_ref[hh]
            doh = do_s[:, kc]
            pb = p_ref[:, hh * ml:(hh + 1) * ml]
            p = pb.astype(F32)
            dp = _dot_nt(doh, kv_ref[HEADS + hh])
            dkv_ref[HEADS + hh] += _dot_tn(pb, doh)
            ds = (p * (dp - jnp.sum(dp * p, axis=-1, keepdims=True)) * scale).astype(BF16)
            dq_ref[:, kc] = _dot(ds, kh).astype(BF16)
            dkv_ref[hh] += _dot_tn(ds, qh)
        dxn = _dot_nt(dq_ref[...], wq_ref[...])
        xv = x1_ref[...]
        dx, dg = _rms_bwd(dxn, xv, _rstd(xv), g_ref[...])
        dg_ref[...] += dg
        dx1 = dx2 + dx
        dx1_ref[...] = dx1
        dx1b_ref[...] = dx1.astype(BF16)

    tok = pl.BlockSpec((tm, d), lambda i: (i, 0))
    vec = pl.BlockSpec((1, d), lambda i: (0, 0))
    sq = pl.BlockSpec((d, d), lambda i: (0, 0))
    kvs = pl.BlockSpec((2 * HEADS, ml, xd), lambda i: (0, 0, 0))
    return _hosted(
        body, carry, grid=(s // tm,),
        in_specs=[tok, tok, tok, vec, tok, vec, tok, pl.BlockSpec((tm, HEADS * ml), lambda i: (i, 0)), kvs, sq, sq],
        out_specs=[tok, tok, tok, tok, kvs, vec, vec],
        out_shape=(SDS((s, d), BF16), SDS((s, d), BF16), SDS((s, d), F32), SDS((s, d), BF16),
                   SDS((2 * HEADS, ml, xd), F32), SDS((1, d), F32), SDS((1, d), F32)),
        scratch_shapes=[pltpu.VMEM((tm, d), BF16)],
        compiler_params=_arb(1), name="attn_backward")(dx3, dxn3, x2, g_ffn, x1, g, q, probs, kv, w_q, w_o)


def _kv_backward(dkv, memn, mem, g_mem, w_kv):
    ml, d = mem.shape
    xd = w_kv.shape[2]

    def body(dkv_ref, memn_ref, mem_ref, g_ref, w_ref, dw_ref, dg_ref):
        dmemn = jnp.zeros((ml, d), F32)
        for j in range(2 * HEADS):
            dkvb = dkv_ref[j].astype(BF16)
            dw_ref[j] = _dot_tn(memn_ref[...], dkvb)
            dmemn = dmemn + _dot_nt(dkvb, w_ref[j])
        x = mem_ref[...]
        dg_ref[...] = jnp.sum(dmemn * (x * _rstd(x)), axis=0, keepdims=True)

    return _pcall(body, out_shape=(SDS((2 * HEADS, d, xd), F32), SDS((1, d), F32)), name="kv_backward")(dkv, memn, mem, g_mem, w_kv)


def _mix_backward(dx1, x, g_mix, h, lng, lnb, w_sp, bt, conv_w, ga, gb, w_out, w_in, tm, carry=()):
    s, d = x.shape
    n_in = h.shape[1]
    aw = lng.shape[1]
    bw = d - aw
    hd = aw // HEADS
    in_a = 2 * aw
    hb_blocks = tm // HALO
    last_blk = s // HALO - 1
    nt = s // tm
    tc = tm
    te = tc + HALO
    tee = tc + 2 * HALO

    def body(dx1_ref, dx1n_ref, x_ref, gm_ref, h_ref, hp_ref, hn_ref, lng_ref, lnb_ref, wsp_ref, bt_ref, cw_ref,
             ga_ref, gb_ref, wout_ref, win_ref,
             dh_ref, dx_ref, dga_ref, dgb_ref, dcw_ref, dlng_ref, dlnb_ref, dwsp_ref, dbs_ref, dgm_ref,
             mixed_s, dvln_s):
        i = pl.program_id(0)

        @pl.when(i == 0)
        def _():
            for ref in (dga_ref, dgb_ref, dcw_ref, dlng_ref, dlnb_ref, dwsp_ref, dbs_ref, dgm_ref):
                ref[...] = jnp.zeros_like(ref)

        mask = _tril_mask()
        wm = [(wsp_ref[hh] * mask).astype(BF16) for hh in range(HEADS)]
        cw = cw_ref[...]

        def chain(r0):
            rows = slice(r0, r0 + tc)
            first, last = r0 == 0, r0 + tc == tm
            hv = h_ref[rows, :]
            dx1 = dx1_ref[rows, :]
            dx1n = dx1n_ref[...] if last else dx1_ref[r0 + tc:r0 + tc + HALO, :]
            hp = hp_ref[:, in_a:] if first else h_ref[r0 - HALO:r0, in_a:]
            hn = hn_ref[:, in_a:] if last else h_ref[r0 + tc:r0 + tc + HALO, in_a:]
            dx1e = jnp.concatenate([dx1, dx1n], axis=0).astype(BF16)
            dycat = _dot_nt(dx1e, wout_ref[...])

            hbe = jnp.concatenate([hp, hv[:, in_a:], hn], axis=0)
            row = lax.broadcasted_iota(jnp.int32, (tee, 1), 0)
            zext = hbe[:, bw:2 * bw] * hbe[:, 2 * bw:]
            if first:
                zext = jnp.where((i == 0) & (row < HALO), 0.0, zext)
            z2e, z1e = _conv_taps(zext)
            conv_e = (cw[0:1] * z2e + cw[1:2] * z1e + cw[2:3] * zext)[HALO:]
            gate_b_e = hbe[HALO:, :bw]
            sc_e = gate_b_e * conv_e
            rb = _rstd(sc_e)
            dyb = dycat[:, aw:]
            gdy = dyb * gb_ref[...]
            dsc_e = rb * gdy - sc_e * (rb * rb * rb) * (jnp.sum(gdy * sc_e, axis=-1, keepdims=True) * (1.0 / bw))
            dgb_ref[...] += jnp.sum((dyb * (sc_e * rb))[:tc], axis=0, keepdims=True)
            dconv_e = dsc_e * gate_b_e
            if last:
                dconv_e = jnp.where((i == nt - 1) & (row[:te] >= tc), 0.0, dconv_e)
            dconv = dconv_e[:tc]
            dc1 = pltpu.roll(dconv_e, te - 1, 0)[:tc]
            dc2 = pltpu.roll(dconv_e, te - 2, 0)[:tc]
            dz = cw[2:3] * dconv + cw[1:2] * dc1 + cw[0:1] * dc2
            z = zext[HALO:HALO + tc]
            z1 = z1e[HALO:HALO + tc]
            z2 = z2e[HALO:HALO + tc]
            dcw_ref[0:1, :] += jnp.sum(dconv * z2, axis=0, keepdims=True)
            dcw_ref[1:2, :] += jnp.sum(dconv * z1, axis=0, keepdims=True)
            dcw_ref[2:3, :] += jnp.sum(dconv * z, axis=0, keepdims=True)
            dh_ref[rows, in_a:in_a + bw] = (dsc_e[:tc] * conv_e[:tc]).astype(BF16)
            dh_ref[rows, in_a + bw:in_a + 2 * bw] = (dz * hv[:, in_a + 2 * bw:]).astype(BF16)
            dh_ref[rows, in_a + 2 * bw:] = (dz * hv[:, in_a + bw:in_a + 2 * bw]).astype(BF16)

            ha = hv[:, :in_a]
            mixed_c, dvln_c = mixed_s.at[rows, :], dvln_s.at[rows, :]
            th, u, xhat, rl, vln = _sgu_forward(ha, lng_ref[...], lnb_ref[...], wm, bt_ref[...], mixed_c)
            mixed = mixed_c[...]
            sg = u * mixed
            dsg, dga = _rms_bwd(dycat[:tc, :aw], sg, _rstd(sg), ga_ref[...])
            dga_ref[...] += dga
            du = dsg * mixed
            dmixed = dsg * u
            dmb = dmixed.astype(BF16)
            for n in range(tc // CHUNK):
                blk = slice(n * CHUNK, (n + 1) * CHUNK)
                dbs_ref[...] += dmixed[blk]
                for hh in range(HEADS):
                    cols = slice(hh * hd, (hh + 1) * hd)
                    dvln_c[blk, cols] = _dot_tn(wm[hh], dmb[blk, cols])
                    dwsp_ref[hh] += mask * _dot_nt(dmb[blk, cols], vln[blk, cols])
            dvln = dvln_c[...]
            dlng_ref[...] += jnp.sum(dvln * xhat, axis=0, keepdims=True)
            dlnb_ref[...] += jnp.sum(dvln, axis=0, keepdims=True)
            dxh = dvln * lng_ref[...]
            dv = rl * (dxh - jnp.mean(dxh, axis=-1, keepdims=True) - xhat * jnp.mean(dxh * xhat, axis=-1, keepdims=True))
            dh_ref[rows, :in_a] = (jnp.concatenate([du, dv], axis=-1) * _gelu_grad(ha, th)).astype(BF16)

            dxn = _dot(dh_ref[rows, :], win_ref[...])
            xv = x_ref[rows, :]
            dx, dgm = _rms_bwd(dxn, xv, _rstd(xv), gm_ref[...])
            dgm_ref[...] += dgm
            dx_ref[rows, :] = dx1 + dx

        for r0 in range(0, tm, tc):
            chain(r0)

    full = lambda shape: pl.BlockSpec(shape, lambda i: (0,) * len(shape))
    tok = pl.BlockSpec((tm, d), lambda i: (i, 0))
    nxt = lambda i: (jnp.minimum((i + 1) * hb_blocks, last_blk), 0)
    prv = lambda i: (jnp.maximum(i * hb_blocks - 1, 0), 0)
    return _hosted(
        body, carry, grid=(nt,),
        in_specs=[tok, pl.BlockSpec((HALO, d), nxt), tok, full((1, d)),
                  pl.BlockSpec((tm, n_in), lambda i: (i, 0)), pl.BlockSpec((HALO, n_in), prv), pl.BlockSpec((HALO, n_in), nxt),
                  full((1, aw)), full((1, aw)), full((HEADS, CHUNK, CHUNK)), full((CHUNK, HEADS)), full((3, bw)),
                  full((1, aw)), full((1, bw)), full((d, d)), full((n_in, d))],
        out_specs=[pl.BlockSpec((tm, n_in), lambda i: (i, 0)), tok,
                   full((1, aw)), full((1, bw)), full((SUB, bw)), full((1, aw)), full((1, aw)),
                   full((HEADS, CHUNK, CHUNK)), full((CHUNK, aw)), full((1, d))],
        out_shape=(SDS((s, n_in), BF16), SDS((s, d), F32),
                   SDS((1, aw), F32), SDS((1, bw), F32), SDS((SUB, bw), F32), SDS((1, aw), F32), SDS((1, aw), F32),
                   SDS((HEADS, CHUNK, CHUNK), F32), SDS((CHUNK, aw), F32), SDS((1, d), F32)),
        scratch_shapes=[pltpu.VMEM((tm, aw), F32), pltpu.VMEM((tm, aw), F32)],
        compiler_params=_arb(1), name="mix_backward")(dx1, dx1, x, g_mix, h, h, h, lng, lnb, w_sp, bt, conv_w, ga, gb, w_out, w_in)


def _bias_grad(dbs):
    aw = dbs.shape[1]
    hd = aw // HEADS

    def body(dbs_ref, out_ref):
        ones = jnp.ones((SUB, hd), F32)
        for hh in range(HEADS):
            r = lax.dot_general(ones, dbs_ref[:, hh * hd:(hh + 1) * hd], (((1,), (1,)), ((), ())),
                                precision=lax.Precision.HIGHEST, preferred_element_type=F32)
            out_ref[hh:hh + 1, :] = r[0:1]

    return _pcall(body, out_shape=SDS((HEADS, CHUNK), F32), name="bias_grad")(dbs)


def _wgrad_body(a_ref, b_ref, o_ref):
    o_ref[...] = _dot_tn(a_ref[...], b_ref[...])


def _wgrad(a, b, name, carry=()):
    k, m = a.shape
    n = b.shape[1]
    tm = _tile(m, 512, LANES)
    tn = _tile(n, 1024, LANES)
    return _hosted(
        functools.partial(_wgrad_body), carry, grid=(m // tm, n // tn),
        in_specs=[pl.BlockSpec((k, tm), lambda i, j: (0, i)), pl.BlockSpec((k, tn), lambda i, j: (0, j))],
        out_specs=pl.BlockSpec((tm, tn), lambda i, j: (i, j)),
        out_shape=SDS((m, n), F32), compiler_params=_arb(2), name=name)(a, b)


def _wgrad_blocked_lhs(a, b, name, carry=()):
    nb, k, t = a.shape
    n = b.shape[1]
    tn = _tile(n, 1024, LANES)
    return _hosted(
        functools.partial(_wgrad_body), carry, grid=(nb, n // tn),
        in_specs=[pl.BlockSpec((None, k, t), lambda i, j: (i, 0, 0)), pl.BlockSpec((k, tn), lambda i, j: (0, j))],
        out_specs=pl.BlockSpec((t, tn), lambda i, j: (i, j)),
        out_shape=SDS((nb * t, n), F32), compiler_params=_arb(2), name=name)(a, b)


def _place():
    x, y, c = lax.axis_index("x"), lax.axis_index("y"), lax.axis_index("c")
    return x, y, c, [(1 - x, y), (x, 1 - y), (1 - x, 1 - y)]


def _all_gather(shards):
    n = len(shards)
    slots = 9
    cut = [(s.shape[0] // 32) * 16 for s in shards]

    def build(ins, outs, sems):
        send_sems, recv_sems, local_sems = sems
        x, y, c, _ = _place()
        me, sib, xn, yn, dg = (x, y, c), (x, y, 1 - c), (1 - x, y, c), (x, 1 - y, c), (1 - x, 1 - y, c)
        other = lambda p: (p[0], p[1], 1 - p[2])

        def rows(a, p, part=None):
            ref = outs[a].at[4 * p[0] + 2 * p[1] + p[2]]
            if part is None or cut[a] == 0:
                return ref if part in (None, 0) else None
            return ref.at[pl.ds(0, cut[a])] if part == 0 else ref.at[pl.ds(cut[a], shards[a].shape[0] - cut[a])]

        def copy(a, k, ref, to, src=None):
            if ref is None:
                return None
            return pltpu.make_async_remote_copy(
                src_ref=ref if src is None else src, dst_ref=ref, send_sem=send_sems.at[slots * a + k],
                recv_sem=recv_sems.at[slots * a + k], device_id=to, device_id_type=MESH)

        def real(cps):
            return [cp for cp in cps if cp is not None]

        class Copies:
            own = lambda a: [copy(a, 1, rows(a, me), xn, ins[a]), copy(a, 2, rows(a, me), yn, ins[a]),
                             copy(a, 0, rows(a, me), sib, ins[a])]
            local = lambda a: pltpu.make_async_copy(ins[a], rows(a, me), local_sems.at[a])
            from_x = lambda a: copy(a, 1, rows(a, xn), me)
            from_y = lambda a: copy(a, 2, rows(a, yn), me)
            after_x = lambda a: real([copy(a, 4, rows(a, xn, 1), yn), copy(a, 5, rows(a, xn), sib)])
            after_y = lambda a: real([copy(a, 3, rows(a, yn, 0), xn), copy(a, 6, rows(a, yn), sib)])
            diag_in = lambda a: real([copy(a, 3, rows(a, dg, 0), me), copy(a, 4, rows(a, dg, 1), me)])
            diag_on = lambda a: real([copy(a, 7, rows(a, dg, 0), sib), copy(a, 8, rows(a, dg, 1), sib)])
            from_sib = lambda a: real([copy(a, 0, rows(a, sib), me), copy(a, 5, rows(a, other(xn)), me),
                                       copy(a, 6, rows(a, other(yn)), me), copy(a, 7, rows(a, other(dg), 0), me),
                                       copy(a, 8, rows(a, other(dg), 1), me)])

        return Copies

    def start(ins, outs, sems):
        cps = build(ins, outs, sems)
        for a in range(n):
            for cp in cps.own(a):
                cp.start()
        for a in range(n):
            cps.local(a).start()

    def relay(ins, outs, sems):
        cps = build(ins, outs, sems)
        for a in range(n):
            cps.from_x(a).wait_recv()
            for cp in cps.after_x(a):
                cp.start()
            cps.from_y(a).wait_recv()
            for cp in cps.after_y(a):
                cp.start()

    def finish(ins, outs, sems):
        cps = build(ins, outs, sems)
        for a in range(n):
            for arrived, onward in zip(cps.diag_in(a), cps.diag_on(a)):
                arrived.wait_recv()
                onward.start()
        for a in range(n):
            for cp in cps.from_sib(a):
                cp.wait_recv()
            for cp in cps.own(a) + cps.after_x(a) + cps.after_y(a) + cps.diag_on(a):
                cp.wait_send()
            cps.local(a).wait()

    def peers():
        x, y, c, _ = _place()
        return [(x, y, 1 - c), (1 - x, y, c), (x, 1 - y, c)]

    return _Exchange(shards, [SDS((N_DEV,) + s.shape, s.dtype) for s in shards],
                     [pltpu.SemaphoreType.DMA((slots * n,)), pltpu.SemaphoreType.DMA((slots * n,)),
                      pltpu.SemaphoreType.DMA((n,))], start, finish, relay, peers, GATHER_COLLECTIVE)


def _swap_exchange(ins, out_shape, per, copies):
    def start(i, o, sems):
        for cp in copies(i, o, sems):
            cp.start()

    def finish(i, o, sems):
        for cp in copies(i, o, sems):
            cp.wait()

    def sibling():
        x, y, c, _ = _place()
        return [(x, y, 1 - c)]

    n = per * len(ins)
    return _Exchange(ins, out_shape, [pltpu.SemaphoreType.DMA((n,)), pltpu.SemaphoreType.DMA((n,))], start, finish,
                     peers=sibling, collective=SIBLING_COLLECTIVE)


def _exchange_c(gs):
    def copies(ins, outs, sems):
        x, y, c, _ = _place()
        return [pltpu.make_async_remote_copy(
                    src_ref=ins[a].at[2 * k + 1 - c], dst_ref=outs[a].at[k],
                    send_sem=sems[0].at[4 * a + k], recv_sem=sems[1].at[4 * a + k],
                    device_id=(x, y, 1 - c), device_id_type=MESH)
                for a in range(len(gs)) for k in range(4)]

    return _swap_exchange(gs, [SDS((4,) + g.shape[1:], g.dtype) for g in gs], 4, copies)


def _rs_combine(g, recv, pos, name, carry=()):
    _, r, cdim = g.shape
    tr = _tile(r, 256, 16)

    def body(pos_ref, g0, r0, g1, r1, g2, r2, g3, r3, keep_ref, send_ref):
        keep_ref[...] = g0[...] + r0[...]
        send_ref[0] = (g1[...] + r1[...]).astype(BF16)
        send_ref[1] = (g2[...] + r2[...]).astype(BF16)
        send_ref[2] = (g3[...] + r3[...]).astype(BF16)

    def k_of(p, t):
        px = p[0] if t in (0, 2) else 1 - p[0]
        py = p[1] if t in (0, 1) else 1 - p[1]
        return 2 * px + py

    blk = (None, tr, cdim)
    in_specs = []
    for t in range(4):
        in_specs.append(pl.BlockSpec(blk, functools.partial(lambda j, p, t: (2 * k_of(p, t) + p[2], j, 0), t=t)))
        in_specs.append(pl.BlockSpec(blk, functools.partial(lambda j, p, t: (k_of(p, t), j, 0), t=t)))
    return _hosted(
        body, carry, n_prefetch=1, out_shape=(SDS((r, cdim), F32), SDS((3, r, cdim), BF16)),
        grid=(r // tr,), in_specs=in_specs,
        out_specs=[pl.BlockSpec((tr, cdim), lambda j, p: (j, 0)), pl.BlockSpec((3, tr, cdim), lambda j, p: (0, j, 0))],
        compiler_params=_arb(1), name=name)(pos, g, recv, g, recv, g, recv, g, recv)


def _adamw_shard(keep, recv, w, m, v, name):
    r, cdim = w.shape
    tr = _tile(r, 256, 16)

    def body(k_ref, r_ref, w_ref, m_ref, v_ref, g_ref, d_ref, nm_ref, nv_ref):
        g = ((k_ref[...] + r_ref[0].astype(F32)) + r_ref[1].astype(F32)) + r_ref[2].astype(F32)
        g_ref[...] = g
        d_ref[...], nm_ref[...], nv_ref[...] = _adamw(w_ref[...], g, m_ref[...], v_ref[...])

    blk = pl.BlockSpec((tr, cdim), lambda j: (j, 0))
    out = SDS((r, cdim), F32)
    return _pcall(body, grid=(r // tr,), in_specs=[blk, pl.BlockSpec((3, tr, cdim), lambda j: (0, j, 0)), blk, blk, blk],
                  out_specs=[blk] * 4, out_shape=(out,) * 4, compiler_params=_arb(1), name=name)(keep, recv, w, m, v)


_HBM = pl.BlockSpec(memory_space=pltpu.HBM)
_SEM = pl.BlockSpec(memory_space=pltpu.SEMAPHORE)
_SPLIT = pltpu.CompilerParams(has_side_effects=pltpu.SideEffectType.DATAFLOW_SIDE_EFFECTING)


def _split_copies(kind, n, refs):
    srcs, lands, (send_sems, recv_sems) = refs[:n], refs[n:2 * n], refs[2 * n:2 * n + 2]
    x, y, c, chips = _place()
    per = _SPLIT_COPIES[kind]
    if kind == "xy":
        ends = lambda a, t: (srcs[a].at[t], lands[a].at[t], (*chips[t], c))
    else:
        ends = lambda a, k: (srcs[a].at[2 * k + 1 - c], lands[a].at[k], (x, y, 1 - c))
    cps = []
    for a in range(n):
        for t in range(per):
            src, dst, to = ends(a, t)
            cps.append(pltpu.make_async_remote_copy(src_ref=src, dst_ref=dst, send_sem=send_sems.at[per * a + t],
                                                    recv_sem=recv_sems.at[per * a + t], device_id=to, device_id_type=MESH))
    return cps


_SPLIT_COPIES = {"xy": 3, "c": 4}
SIBLING_COLLECTIVE = 1
GATHER_COLLECTIVE = 5


def _exchange_start(kind, arrays, name, after=None, collective=None):
    n = len(arrays)
    order = [] if after is None else [after]

    def body(*refs):
        x, y, c, chips = _place()
        peers = [(x, y, 1 - c)] if kind == "c" else [(*chip, c) for chip in chips]
        barrier = pltpu.get_barrier_semaphore()
        for peer in peers:
            pl.semaphore_signal(barrier, inc=1, device_id=peer, device_id_type=MESH)
        pl.semaphore_wait(barrier, len(peers))
        refs = refs[:2 * n] + refs[2 * n + len(order):]
        for cp in _split_copies(kind, n, refs):
            cp.start()
        refs[-1][...] = jnp.zeros_like(refs[-1])

    params = pltpu.CompilerParams(has_side_effects=pltpu.SideEffectType.DATAFLOW_SIDE_EFFECTING,
                                  collective_id=SIBLING_COLLECTIVE if kind == "c" else collective)
    hbm = lambda a: pltpu.with_memory_space_constraint(a, pltpu.HBM)
    land = [a.shape if kind == "xy" else (4,) + a.shape[1:] for a in arrays]
    bufs = [pltpu.HBM(a.shape, a.dtype) for a in arrays] + [pltpu.HBM(s, a.dtype) for s, a in zip(land, arrays)]
    sems = pltpu.SemaphoreType.DMA((_SPLIT_COPIES[kind] * n,))
    res = _pcall(
        body, name=name, out_shape=(sems, sems, *bufs, SDS((SUB, LANES), F32)),
        in_specs=[_HBM] * (2 * n) + _hbm_specs(len(order)),
        out_specs=[_SEM, _SEM] + [_HBM] * (2 * n) + [pl.BlockSpec(memory_space=pltpu.VMEM)],
        input_output_aliases={k: 2 + k for k in range(2 * n)}, compiler_params=params)(
            *[hbm(a) for a in arrays], *[hbm(lax.empty(s, a.dtype)) for s, a in zip(land, arrays)], *order)
    return (kind, n, res[:-1]), res[-1]


def _exchange_wait(started, after, name, sources=False):
    kind, n, (send_sems, recv_sems, *bufs) = started

    def body(*refs):
        for cp in _split_copies(kind, n, refs):
            cp.wait_send()
            cp.wait_recv()

    shapes = [pltpu.HBM(b.shape, b.dtype) for b in bufs]
    res = _pcall(
        body, name=name, out_shape=tuple(shapes),
        in_specs=[_HBM] * (2 * n) + [_SEM, _SEM, pl.BlockSpec(memory_space=pl.ANY)], out_specs=[_HBM] * (2 * n),
        input_output_aliases={k: k for k in range(2 * n)}, compiler_params=_SPLIT)(*bufs, send_sems, recv_sems, after)
    return (list(res[:n]), list(res[n:])) if sources else list(res[n:])


def _follow(token):
    nothing = lambda ins, outs, sems: None
    return _Exchange([token], [], [], nothing, nothing)


def _adamw_small(gathered, seg, params, conv_rows):
    names = list(params)
    c0, cn = conv_rows

    def body(*refs):
        gat_ref = refs[0]
        ins = refs[1:1 + 3 * len(names)]
        outs = refs[1 + 3 * len(names):]

        def total(r0, rn):
            tot = gat_ref[0, r0:r0 + rn, :]
            for dev in range(1, N_DEV):
                tot = tot + gat_ref[dev, r0:r0 + rn, :]
            return tot

        for k, nm in enumerate(names):
            g = total(*seg[nm])
            w_ref, m_ref, v_ref = ins[3 * k:3 * k + 3]
            g_ref, d_ref, nm_ref, nv_ref = outs[4 * k:4 * k + 4]
            g_ref[...] = g
            d_ref[...], nm_ref[...], nv_ref[...] = _adamw(w_ref[...], g, m_ref[...], v_ref[...])
        outs[-2][...] = total(c0, cn)
        outs[-1][...] = total(*seg["loss"])

    flat_in = [a for nm in names for a in params[nm]]
    out_shape = []
    for nm in names:
        out_shape += [SDS(params[nm][0].shape, F32)] * 4
    out_shape += [SDS((cn, LANES), F32), SDS((seg["loss"][1], LANES), F32)]
    res = _pcall(body, out_shape=tuple(out_shape), name="adamw_small")(gathered, *flat_in)
    per = {nm: res[4 * k:4 * k + 4] for k, nm in enumerate(names)}
    return per, res[-2], res[-1]


def _adamw_one(w, g, m, v, name):
    def body(w_ref, g_ref, m_ref, v_ref, d_ref, nm_ref, nv_ref):
        d_ref[...], nm_ref[...], nv_ref[...] = _adamw(w_ref[...], g_ref[...], m_ref[...], v_ref[...])

    return _pcall(body, out_shape=(SDS(w.shape, F32),) * 3, name=name)(w, g, m, v)


def _rows128(a):
    return a.reshape(-1, LANES)


def _pack_small(gs, loss_tile):
    seg, pieces, row = {}, [], 0
    for nm in SMALL + ("conv_w", "loss"):
        piece = loss_tile if nm == "loss" else _rows128(gs[nm])
        rn = _round_up(piece.shape[0], SUB)
        pieces.append(jnp.pad(piece, ((0, rn - piece.shape[0]), (0, 0))))
        seg[nm] = (row, piece.shape[0])
        row += rn
    return jnp.concatenate(pieces, axis=0), seg


def _step(x, mem, target, wb, conv_w, sp, pos):
    s, d = x.shape
    tm = min(TOKEN_TILE, s)
    tm_wide = min(2 * TOKEN_TILE, s)
    rows = lambda w8: w8.reshape(-1, w8.shape[2])
    shards = lambda g: g.reshape((N_DEV, -1) + g.shape[1:])
    bt = sp["b_spatial"].T

    (w_in8, conv8), = _run_exchanges([_all_gather([wb["w_in"], conv_w])], "gather_w_in")
    conv_full = conv8.transpose(1, 0, 2).reshape(3, -1)
    w_in_t = rows(w_in8)
    (xn1, h), ((w_out8, w_kv8, w_q8),) = _in_forward(
        x, sp["ln_mix_g"], w_in_t, tm, carry=[_all_gather([wb["w_out"], wb["w_kv"], wb["w_q"]])])
    w_out = rows(w_out8)
    (ycat, x1), ((w_o8, w_down8),) = _mix_forward(
        h, x, sp["sgu_ln_g"], sp["sgu_ln_b"], sp["w_spatial"], bt, conv_full, sp["grp_norm_a"], sp["grp_norm_b"], w_out, tm,
        carry=[_all_gather([wb["w_o"], wb["w_down"]])])
    w_q, w_o, w_down = rows(w_q8), rows(w_o8), rows(w_down8)
    memn, kv = _kv_forward(mem, sp["ln_mem_g"], w_kv8)
    (xn2, q, probs, o, x2), ((w_gu8,),) = _attn_forward(
        x1, sp["ln_attn_g"], w_q, kv, w_o, tm, carry=[_all_gather([wb["w_gate_up"]])])
    w_gu = w_gu8.reshape((2, N_DEV // 2) + w_gu8.shape[1:])
    xn3, gu, x3 = _ffn_forward(x2, sp["ln_ffn_g"], w_gu, w_down, tm_wide)

    loss, d_lnf, dx3, dx3b = _final_backward(x3, target, sp["ln_final_g"], tm_wide)
    act, dgu, dxn3 = _swiglu_backward(dx3b, gu, w_gu, w_down, tm_wide)
    g_gu, _ = _wgrad_blocked_lhs(dgu.reshape((N_DEV,) + dgu.shape[2:]), xn3, "wgrad_gate_up")
    g_gu = shards(g_gu)
    g_down, ((rc_gu,),) = _wgrad_blocked_lhs(act, dx3b, "wgrad_down", carry=[_exchange_c([g_gu])])
    g_down = shards(g_down)
    keep, pending = {}, []
    (keep["w_gate_up"], send_gu), _ = _rs_combine(g_gu, rc_gu, pos, "rs_combine_w_gate_up")
    started, token = _exchange_start("xy", [send_gu], "exchange_xy_1_start", collective=2)
    pending.append((("w_gate_up",), started))
    c_down, token = _exchange_start("c", [g_down], "exchange_c_1_start", after=token)
    (dx2b, dq, dx1, dx1b, dkv, d_lnattn, d_lnffn), _ = _attn_backward(
        dx3, dxn3, x2, sp["ln_ffn_g"], x1, sp["ln_attn_g"], q, probs, kv, w_q, w_o, tm, carry=[_follow(token)])
    (g_down,), (rc_down,) = _exchange_wait(c_down, dx1b, "exchange_c_1_wait", sources=True)
    (keep["w_down"], send_down), _ = _rs_combine(g_down, rc_down, pos, "rs_combine_w_down")
    g_o, _ = _wgrad(o, dx2b, "wgrad_o")
    g_q, _ = _wgrad(xn2, dq, "wgrad_q")
    g_o, g_q = shards(g_o), shards(g_q)
    g_kv, d_lnmem = _kv_backward(dkv, memn, mem, sp["ln_mem_g"], w_kv8)
    c_oqkv, token = _exchange_start("c", [g_o, g_q, g_kv], "exchange_c_2_start")
    g_out, _ = _wgrad(ycat, dx1b, "wgrad_out", carry=[_follow(token)])
    g_out = shards(g_out)
    (g_o, g_q, g_kv), (rc_o, rc_q, rc_kv) = _exchange_wait(c_oqkv, g_out, "exchange_c_2_wait", sources=True)
    c_out, token = _exchange_start("c", [g_out], "exchange_c_3_start")
    (keep["w_o"], send_o), _ = _rs_combine(g_o, rc_o, pos, "rs_combine_w_o", carry=[_follow(token)])
    (keep["w_q"], send_q), _ = _rs_combine(g_q, rc_q, pos, "rs_combine_w_q")
    (keep["w_kv"], send_kv), _ = _rs_combine(g_kv, rc_kv, pos, "rs_combine_w_kv")
    (g_out,), (rc_out,) = _exchange_wait(c_out, send_kv, "exchange_c_3_wait", sources=True)
    (keep["w_out"], send_out), _ = _rs_combine(g_out, rc_out, pos, "rs_combine_w_out")
    started, token = _exchange_start("xy", [send_down, send_o, send_q, send_out, send_kv], "exchange_xy_2_start",
                                     collective=3)
    pending.append((("w_down", "w_o", "w_q", "w_out", "w_kv"), started))
    (dh, dx, d_ga, d_gb, d_cw, d_lng, d_lnb, d_wsp, d_bs, d_lnmix), _ = _mix_backward(
        dx1, x, sp["ln_mix_g"], h, sp["sgu_ln_g"], sp["sgu_ln_b"], sp["w_spatial"], bt, conv_full,
        sp["grp_norm_a"], sp["grp_norm_b"], w_out, w_in_t, tm, carry=[_follow(token)])
    gs = {"ln_mix_g": d_lnmix, "sgu_ln_g": d_lng, "sgu_ln_b": d_lnb, "w_spatial": d_wsp, "b_spatial": _bias_grad(d_bs),
          "conv_w": d_cw[:3], "grp_norm_a": d_ga, "grp_norm_b": d_gb, "ln_attn_g": d_lnattn, "ln_mem_g": d_lnmem,
          "ln_ffn_g": d_lnffn, "ln_final_g": d_lnf}
    packed, seg = _pack_small(gs, loss)
    g_in, (_, (small_all,)) = _wgrad(dh, xn1, "wgrad_in", carry=[_follow(token), _all_gather([packed])])
    g_in = shards(g_in)
    c_in, token = _exchange_start("c", [g_in], "exchange_c_4_start")
    return dx, keep, pending, (g_in, c_in), token, small_all, seg


def kernel(x, mem, ln_mix_g, w_in, sgu_ln_g, sgu_ln_b, w_spatial, b_spatial, conv_w, grp_norm_a, grp_norm_b, w_out, ln_attn_g, ln_mem_g, w_q, w_kv, w_o, ln_ffn_g, w_gate_up, w_down, ln_final_g, loss_target, m_ln_mix_g, m_w_in, m_sgu_ln_g, m_sgu_ln_b, m_w_spatial, m_b_spatial, m_conv_w, m_grp_norm_a, m_grp_norm_b, m_w_out, m_ln_attn_g, m_ln_mem_g, m_w_q, m_w_kv, m_w_o, m_ln_ffn_g, m_w_gate_up, m_w_down, m_ln_final_g, v_ln_mix_g, v_w_in, v_sgu_ln_g, v_sgu_ln_b, v_w_spatial, v_b_spatial, v_conv_w, v_grp_norm_a, v_grp_norm_b, v_w_out, v_ln_attn_g, v_ln_mem_g, v_w_q, v_w_kv, v_w_o, v_ln_ffn_g, v_w_gate_up, v_w_down, v_ln_final_g):
    order = ["ln_mix_g", "w_in", "sgu_ln_g", "sgu_ln_b", "w_spatial", "b_spatial", "conv_w", "grp_norm_a", "grp_norm_b",
             "w_out", "ln_attn_g", "ln_mem_g", "w_q", "w_kv", "w_o", "ln_ffn_g", "w_gate_up", "w_down", "ln_final_g"]
    W = dict(ln_mix_g=ln_mix_g, w_in=w_in, sgu_ln_g=sgu_ln_g, sgu_ln_b=sgu_ln_b, w_spatial=w_spatial, b_spatial=b_spatial,
             conv_w=conv_w, grp_norm_a=grp_norm_a, grp_norm_b=grp_norm_b, w_out=w_out, ln_attn_g=ln_attn_g,
             ln_mem_g=ln_mem_g, w_q=w_q, w_kv=w_kv, w_o=w_o, ln_ffn_g=ln_ffn_g, w_gate_up=w_gate_up, w_down=w_down,
             ln_final_g=ln_final_g)
    M = dict(ln_mix_g=m_ln_mix_g, w_in=m_w_in, sgu_ln_g=m_sgu_ln_g, sgu_ln_b=m_sgu_ln_b, w_spatial=m_w_spatial,
             b_spatial=m_b_spatial, conv_w=m_conv_w, grp_norm_a=m_grp_norm_a, grp_norm_b=m_grp_norm_b, w_out=m_w_out,
             ln_attn_g=m_ln_attn_g, ln_mem_g=m_ln_mem_g, w_q=m_w_q, w_kv=m_w_kv, w_o=m_w_o, ln_ffn_g=m_ln_ffn_g,
             w_gate_up=m_w_gate_up, w_down=m_w_down, ln_final_g=m_ln_final_g)
    V = dict(ln_mix_g=v_ln_mix_g, w_in=v_w_in, sgu_ln_g=v_sgu_ln_g, sgu_ln_b=v_sgu_ln_b, w_spatial=v_w_spatial,
             b_spatial=v_b_spatial, conv_w=v_conv_w, grp_norm_a=v_grp_norm_a, grp_norm_b=v_grp_norm_b, w_out=v_w_out,
             ln_attn_g=v_ln_attn_g, ln_mem_g=v_ln_mem_g, w_q=v_w_q, w_kv=v_w_kv, w_o=v_w_o, ln_ffn_g=v_ln_ffn_g,
             w_gate_up=v_w_gate_up, w_down=v_w_down, ln_final_g=v_ln_final_g)

    bw = conv_w.shape[1] * N_DEV
    pos = jnp.stack([lax.axis_index("x"), lax.axis_index("y"), lax.axis_index("c")]).astype(jnp.int32)
    me = 4 * pos[0] + 2 * pos[1] + pos[2]

    sp = {nm: (W[nm].reshape(1, -1) if W[nm].ndim == 1 else W[nm]) for nm in SMALL}
    view = lambda a, nm: a.T if nm in TRANSPOSED else a
    wb = {nm: view(W[nm], nm).astype(BF16) for nm in BIG}
    grad_x, keep, pending, (g_in, c_in), token, small_all, seg = _step(
        x[0], mem[0], loss_target[0], wb, conv_w, sp, pos)

    out = {}

    def update(k, names, started, token):
        landed = _exchange_wait(started, token, "exchange_xy_%d_wait" % k)
        for nm, rxy in zip(names, landed):
            res = _adamw_shard(keep[nm], rxy, view(W[nm], nm), view(M[nm], nm), view(V[nm], nm), "adamw_" + nm)
            out[nm] = tuple(view(a, nm) for a in res)
            token = res[0]
        return token

    token = update(1, *pending[0], token)
    (g_in,), (rc_in,) = _exchange_wait(c_in, token, "exchange_c_4_wait", sources=True)
    (keep["w_in"], send_in), _ = _rs_combine(g_in, rc_in, pos, "rs_combine_w_in")
    xy_in, token = _exchange_start("xy", [send_in], "exchange_xy_3_start", collective=4)
    token = update(2, *pending[1], token)

    params = {nm: (_rows128(W[nm]), _rows128(M[nm]), _rows128(V[nm])) for nm in SMALL}
    per, conv_g_rows, loss_sum = _adamw_small(small_all, seg, params, seg["conv_w"])
    for nm in SMALL:
        out[nm] = tuple(a.reshape(W[nm].shape) for a in per[nm])
    conv_g = lax.dynamic_slice_in_dim(conv_g_rows.reshape(3, bw), me * conv_w.shape[1], conv_w.shape[1], axis=1)
    out["conv_w"] = (conv_g,) + tuple(_adamw_one(conv_w, conv_g, m_conv_w, v_conv_w, "adamw_conv"))

    update(3, ("w_in",), xy_in, token[:1, :1] + out["conv_w"][1][:1, :1])

    loss = loss_sum[0, 0]
    res = [loss, grad_x[None]]
    for k in range(4):
        res += [out[nm][k] for nm in order]
    return tuple(res)
```

```python
import functools

import jax
import jax.numpy as jnp
from jax import lax
from jax.experimental import pallas as pl
from jax.experimental.pallas import tpu as pltpu

F32 = jnp.float32
BF16 = jnp.bfloat16
SDS = jax.ShapeDtypeStruct
MESH = pl.DeviceIdType.MESH

EPS = 1e-6
N_DEV = 8
HEADS = 4
CHUNK = 128
HALO = 16
SUB = 8
LANES = 128
TOKEN_TILE = 512
ROW_CHUNK = 256
RELAY_AT = 0.7

ADAM_LR = 0.001
ADAM_B1 = 0.9
ADAM_B2 = 0.999
ADAM_EPS = 1e-08
ADAM_WD = 0.01
ADAM_STEP = 10

BIG = ("w_in", "w_out", "w_q", "w_kv", "w_o", "w_gate_up", "w_down")
TRANSPOSED = ("w_in", "w_gate_up")
SMALL = ("ln_mix_g", "sgu_ln_g", "sgu_ln_b", "w_spatial", "b_spatial", "grp_norm_a", "grp_norm_b",
         "ln_attn_g", "ln_mem_g", "ln_ffn_g", "ln_final_g")


class _Exchange:
    def __init__(self, ins, out_shape, sems, start, finish, relay=None, peers=None, collective=None):
        self.ins, self.out_shape, self.sems = list(ins), list(out_shape), list(sems)
        self.start, self.finish, self.relay = start, finish, relay
        self.peers, self.collective = peers, collective


def _pcall(body, carry=(), n_prefetch=0, **kw):
    if carry:
        return functools.partial(_carrying_call, body, tuple(carry), n_prefetch, kw)
    if n_prefetch:
        kw["grid_spec"] = pltpu.PrefetchScalarGridSpec(
            num_scalar_prefetch=n_prefetch, grid=kw.pop("grid"), in_specs=kw.pop("in_specs"),
            out_specs=kw.pop("out_specs"), scratch_shapes=kw.pop("scratch_shapes", ()))
    return pl.pallas_call(body, **kw)


def _carrying_call(body, carry, n_prefetch, kw, *args):
    kw = dict(kw)
    out_shape = kw.pop("out_shape")
    single = not isinstance(out_shape, (tuple, list))
    outs_shape = (out_shape,) if single else tuple(out_shape)
    out_specs = kw.pop("out_specs")
    out_specs = [out_specs] if single else list(out_specs)
    in_specs = list(kw.pop("in_specs"))
    scratch = list(kw.pop("scratch_shapes", ()))
    grid = tuple(kw.get("grid", ()))
    n_in, n_out, n_scr = len(args), len(outs_shape), len(scratch)
    copying = [p for p in carry if p.sems]
    shaking = copying[0] if len(copying) == 1 and copying[0].collective is not None else None
    if shaking is not None:
        old = kw.get("compiler_params")
        kw["compiler_params"] = pltpu.CompilerParams(
            dimension_semantics=None if old is None else old.dimension_semantics, collective_id=shaking.collective)

    def split(refs, k, counts):
        parts = []
        for cnt in counts:
            parts.append(refs[k:k + cnt])
            k += cnt
        return parts, k

    def wrapped(*refs):
        cins, k = split(refs, n_in, [len(p.ins) for p in carry])
        outs = refs[k:k + n_out]
        couts, k = split(refs, k + n_out, [len(p.out_shape) for p in carry])
        scr = refs[k:k + n_scr]
        csems, _ = split(refs, k + n_scr, [len(p.sems) for p in carry])
        first, last = True, True
        for a, g in enumerate(grid):
            first = (pl.program_id(a) == 0) & first
            last = (pl.program_id(a) == g - 1) & last

        def start_all():
            if shaking is not None:
                peers = shaking.peers()
                barrier = pltpu.get_barrier_semaphore()
                for peer in peers:
                    pl.semaphore_signal(barrier, inc=1, device_id=peer, device_id_type=MESH)
                pl.semaphore_wait(barrier, len(peers))
            for p, ci, co, cs in zip(carry, cins, couts, csems):
                p.start(ci, co, cs)

        def relay_all():
            for p, ci, co, cs in zip(carry, cins, couts, csems):
                if p.relay is not None:
                    p.relay(ci, co, cs)

        def finish_all():
            for p, ci, co, cs in zip(carry, cins, couts, csems):
                p.finish(ci, co, cs)

        if len(grid) == 1:
            relay_now = pl.program_id(0) == min(int(RELAY_AT * grid[0]), grid[0] - 1)
        else:
            relay_now = last
        if not grid:
            start_all()
            body(*refs[:n_in], *outs, *scr)
            relay_all()
            finish_all()
            return
        pl.when(first)(start_all)
        pl.when(relay_now)(relay_all)
        body(*refs[:n_in], *outs, *scr)
        pl.when(last)(finish_all)

    c_in = [a for p in carry for a in p.ins]
    c_out = [s for p in carry for s in p.out_shape]
    c_sems = [s for p in carry for s in p.sems]
    res = _pcall(wrapped, n_prefetch=n_prefetch, out_shape=outs_shape + tuple(c_out),
                 in_specs=in_specs + _hbm_specs(len(c_in)), out_specs=out_specs + _hbm_specs(len(c_out)),
                 scratch_shapes=scratch + c_sems, **kw)(*args, *c_in)
    own = res[0] if single else tuple(res[:n_out])
    landed, k = [], n_out
    for p in carry:
        landed.append(list(res[k:k + len(p.out_shape)]))
        k += len(p.out_shape)
    return own, landed


def _hbm_specs(n):
    return [pl.BlockSpec(memory_space=pl.ANY)] * n


def _hosted(body, carry, **kw):
    if carry:
        return _pcall(body, carry=carry, **kw)
    call = _pcall(body, **kw)
    return lambda *args: (call(*args), [])


def _cast_bf16(arrays, name, carry=()):
    n = len(arrays)

    def body(*refs):
        for k in range(n):
            refs[n + k][...] = refs[k][...].astype(BF16)

    vmem = [pl.BlockSpec(memory_space=pltpu.VMEM)] * n
    return _hosted(body, carry, in_specs=vmem, out_specs=vmem, out_shape=tuple(SDS(a.shape, BF16) for a in arrays),
                   name=name)(*arrays)


def _run_exchanges(parts, name):
    def body(*refs):
        pass

    _, landed = _pcall(body, carry=parts, out_shape=(), in_specs=[], out_specs=[], name=name)()
    return landed


def _arb(n):
    return pltpu.CompilerParams(dimension_semantics=("arbitrary",) * n)


def _tile(n, target, mult):
    best = None
    for t in range(mult, min(n, target) + 1, mult):
        if n % t == 0:
            best = t
    return n if best is None else best


def _round_up(n, m):
    return (n + m - 1) // m * m


def _dot(a, b):
    return jnp.dot(a, b, preferred_element_type=F32)


def _dot_nt(a, b):
    return lax.dot_general(a, b, (((1,), (1,)), ((), ())), preferred_element_type=F32)


def _dot_tn(a, b):
    return lax.dot_general(a, b, (((0,), (0,)), ((), ())), preferred_element_type=F32)


def _rstd(x):
    return lax.rsqrt(jnp.mean(x * x, axis=-1, keepdims=True) + EPS)


def _rms_bwd(dy, x, r, g):
    gdy = dy * g
    proj = jnp.sum(gdy * x, axis=-1, keepdims=True) * (1.0 / x.shape[-1])
    dx = r * gdy - x * (r * r * r) * proj
    dg = jnp.sum(dy * (x * r), axis=0, keepdims=True)
    return dx, dg


_GELU_C = 0.7978845608028654
_GELU_A = 0.044715


def _gelu(x):
    t = jnp.tanh(_GELU_C * (x + _GELU_A * x * x * x))
    return 0.5 * x * (1.0 + t), t


def _gelu_grad(x, t):
    return 0.5 * (1.0 + t) + 0.5 * x * (1.0 - t * t) * (_GELU_C * (1.0 + 3.0 * _GELU_A * x * x))


def _sigmoid(x):
    return 1.0 / (1.0 + jnp.exp(-x))


def _softmax(s):
    m = jnp.max(s, axis=-1, keepdims=True)
    e = jnp.exp(s - m)
    return e / jnp.sum(e, axis=-1, keepdims=True)


def _adamw(w, g, m, v):
    m = ADAM_B1 * m + (1.0 - ADAM_B1) * g
    v = ADAM_B2 * v + (1.0 - ADAM_B2) * (g * g)
    m_hat = m / (1.0 - ADAM_B1 ** ADAM_STEP)
    v_hat = v / (1.0 - ADAM_B2 ** ADAM_STEP)
    delta = -ADAM_LR * (m_hat / (jnp.sqrt(v_hat) + ADAM_EPS) + ADAM_WD * w)
    return delta, m, v


def _tril_mask():
    t = lax.broadcasted_iota(jnp.int32, (CHUNK, CHUNK), 0)
    s = lax.broadcasted_iota(jnp.int32, (CHUNK, CHUNK), 1)
    return (s <= t).astype(F32)


def _sgu_forward(ha, lng, lnb, wm, bt, mixed_s):
    aw = ha.shape[1] // 2
    hd = aw // HEADS
    a, th = _gelu(ha)
    u = a[:, :aw]
    v = a[:, aw:]
    mu = jnp.mean(v, axis=-1, keepdims=True)
    vc = v - mu
    rl = lax.rsqrt(jnp.mean(vc * vc, axis=-1, keepdims=True) + EPS)
    xhat = vc * rl
    vln = (xhat * lng + lnb).astype(BF16)
    for n in range(ha.shape[0] // CHUNK):
        rows = slice(n * CHUNK, (n + 1) * CHUNK)
        for h in range(HEADS):
            cols = slice(h * hd, (h + 1) * hd)
            mixed_s[rows, cols] = _dot(wm[h], vln[rows, cols]) + bt[:, h:h + 1]
    return th, u, xhat, rl, vln


def _conv_taps(zext):
    return pltpu.roll(zext, 2, 0), pltpu.roll(zext, 1, 0)


def _kv_forward(mem, g_mem, w_kv):
    ml, d = mem.shape
    xd = w_kv.shape[2]

    def body(mem_ref, g_ref, w_ref, memn_ref, kv_ref):
        x = mem_ref[...]
        memn = (x * _rstd(x) * g_ref[...]).astype(BF16)
        memn_ref[...] = memn
        for j in range(2 * HEADS):
            kv_ref[j] = _dot(memn, w_ref[j]).astype(BF16)

    return _pcall(body, out_shape=(SDS((ml, d), BF16), SDS((2 * HEADS, ml, xd), BF16)), name="kv_forward")(mem, g_mem, w_kv)


def _in_forward(x, g, w_in_t, tm, carry=()):
    s, d = x.shape
    n_in = w_in_t.shape[0]

    def body(x_ref, g_ref, w_ref, xn_ref, h_ref):
        xv = x_ref[...]
        xn = (xv * _rstd(xv) * g_ref[...]).astype(BF16)
        xn_ref[...] = xn
        h_ref[...] = _dot_nt(xn, w_ref[...])

    return _hosted(
        body, carry, grid=(s // tm,),
        in_specs=[pl.BlockSpec((tm, d), lambda i: (i, 0)), pl.BlockSpec((1, d), lambda i: (0, 0)),
                  pl.BlockSpec((n_in, d), lambda i: (0, 0))],
        out_specs=[pl.BlockSpec((tm, d), lambda i: (i, 0)), pl.BlockSpec((tm, n_in), lambda i: (i, 0))],
        out_shape=(SDS((s, d), BF16), SDS((s, n_in), F32)),
        compiler_params=_arb(1), name="in_forward")(x, g, w_in_t)


def _mix_forward(h, x, lng, lnb, w_sp, bt, conv_w, ga, gb, w_out, tm, carry=()):
    s, d = x.shape
    n_in = h.shape[1]
    aw = lng.shape[1]
    bw = d - aw
    in_a = 2 * aw
    hb_blocks = tm // HALO

    def body(h_ref, hprev_ref, x_ref, lng_ref, lnb_ref, wsp_ref, bt_ref, cw_ref, ga_ref, gb_ref, wout_ref,
             ycat_ref, x1_ref, mixed_s):
        i = pl.program_id(0)
        mask = _tril_mask()
        wm = [(wsp_ref[hh] * mask).astype(BF16) for hh in range(HEADS)]
        hv = h_ref[...]
        _, u, _, _, _ = _sgu_forward(hv[:, :in_a], lng_ref[...], lnb_ref[...], wm, bt_ref[...], mixed_s)
        sg = u * mixed_s[...]
        ycat_ref[:, :aw] = (sg * _rstd(sg) * ga_ref[...]).astype(BF16)

        gate_b = hv[:, in_a:in_a + bw]
        z = hv[:, in_a + bw:in_a + 2 * bw] * hv[:, in_a + 2 * bw:]
        hp = hprev_ref[...]
        zp = hp[:, in_a + bw:in_a + 2 * bw] * hp[:, in_a + 2 * bw:]
        zp = jnp.where(i == 0, 0.0, zp)
        zext = jnp.concatenate([zp, z], axis=0)
        z2, z1 = _conv_taps(zext)
        cw = cw_ref[...]
        conv = cw[0:1] * z2[HALO:] + cw[1:2] * z1[HALO:] + cw[2:3] * z
        sc = gate_b * conv
        ycat_ref[:, aw:] = (sc * _rstd(sc) * gb_ref[...]).astype(BF16)
        x1_ref[...] = x_ref[...] + _dot(ycat_ref[...], wout_ref[...])

    full = lambda shape: pl.BlockSpec(shape, lambda i: (0,) * len(shape))
    return _hosted(
        body, carry, grid=(s // tm,),
        in_specs=[pl.BlockSpec((tm, n_in), lambda i: (i, 0)),
                  pl.BlockSpec((HALO, n_in), lambda i: (jnp.maximum(i * hb_blocks - 1, 0), 0)),
                  pl.BlockSpec((tm, d), lambda i: (i, 0)),
                  full((1, aw)), full((1, aw)), full((HEADS, CHUNK, CHUNK)), full((CHUNK, HEADS)),
                  full((3, bw)), full((1, aw)), full((1, bw)), full((d, d))],
        out_specs=[pl.BlockSpec((tm, d), lambda i: (i, 0)), pl.BlockSpec((tm, d), lambda i: (i, 0))],
        out_shape=(SDS((s, d), BF16), SDS((s, d), F32)),
        scratch_shapes=[pltpu.VMEM((tm, aw), F32)],
        compiler_params=_arb(1), name="mix_forward")(h, h, x, lng, lnb, w_sp, bt, conv_w, ga, gb, w_out)


def _attn_forward(x1, g, w_q, kv, w_o, tm, carry=()):
    s, d = x1.shape
    _, ml, xd = kv.shape
    scale = xd ** -0.5

    def body(x1_ref, g_ref, wq_ref, kv_ref, wo_ref, xn_ref, q_ref, p_ref, o_ref, x2_ref):
        xv = x1_ref[...]
        xn = (xv * _rstd(xv) * g_ref[...]).astype(BF16)
        xn_ref[...] = xn
        q_ref[...] = _dot(xn, wq_ref[...]).astype(BF16)
        for hh in range(HEADS):
            cols = slice(hh * xd, (hh + 1) * xd)
            p = _softmax(_dot_nt(q_ref[:, cols], kv_ref[hh]) * scale).astype(BF16)
            p_ref[:, hh * ml:(hh + 1) * ml] = p
            o_ref[:, cols] = _dot(p, kv_ref[HEADS + hh]).astype(BF16)
        x2_ref[...] = xv + _dot(o_ref[...], wo_ref[...])

    tok = pl.BlockSpec((tm, d), lambda i: (i, 0))
    probs = pl.BlockSpec((tm, HEADS * ml), lambda i: (i, 0))
    return _hosted(
        body, carry, grid=(s // tm,),
        in_specs=[tok, pl.BlockSpec((1, d), lambda i: (0, 0)), pl.BlockSpec((d, d), lambda i: (0, 0)),
                  pl.BlockSpec((2 * HEADS, ml, xd), lambda i: (0, 0, 0)), pl.BlockSpec((d, d), lambda i: (0, 0))],
        out_specs=[tok, tok, probs, tok, tok],
        out_shape=(SDS((s, d), BF16), SDS((s, d), BF16), SDS((s, HEADS * ml), BF16), SDS((s, d), BF16), SDS((s, d), F32)),
        compiler_params=_arb(1), name="attn_forward")(x1, g, w_q, kv, w_o)


def _ffn_forward(x2, g, w_gu, w_down, tm):
    s, d = x2.shape
    _, nf, tf, _ = w_gu.shape

    def body(x2_ref, g_ref, wgu_ref, wd_ref, xn_ref, gu_ref, x3_ref):
        f = pl.program_id(1)

        @pl.when(f == 0)
        def _():
            xv = x2_ref[...]
            xn_ref[...] = (xv * _rstd(xv) * g_ref[...]).astype(BF16)
            x3_ref[...] = xv

        xn = xn_ref[...]
        gate = _dot_nt(xn, wgu_ref[0])
        up = _dot_nt(xn, wgu_ref[1])
        gu_ref[0] = gate.astype(BF16)
        gu_ref[1] = up.astype(BF16)
        act = (gate * _sigmoid(gate) * up).astype(BF16)
        x3_ref[...] += _dot(act, wd_ref[...])

    tok = pl.BlockSpec((tm, d), lambda i, f: (i, 0))
    return _pcall(
        body, grid=(s // tm, nf),
        in_specs=[tok, pl.BlockSpec((1, d), lambda i, f: (0, 0)),
                  pl.BlockSpec((2, None, tf, d), lambda i, f: (0, f, 0, 0)),
                  pl.BlockSpec((tf, d), lambda i, f: (f, 0))],
        out_specs=[tok, pl.BlockSpec((2, None, tm, tf), lambda i, f: (0, f, i, 0)), tok],
        out_shape=(SDS((s, d), BF16), SDS((2, nf, s, tf), BF16), SDS((s, d), F32)),
        compiler_params=_arb(2), name="ffn_forward")(x2, g, w_gu, w_down)


def _final_backward(x3, target, g_final, tm):
    s, d = x3.shape

    def body(x3_ref, tgt_ref, gf_ref, loss_ref, dgf_ref, dx3_ref, dx3b_ref):
        @pl.when(pl.program_id(0) == 0)
        def _():
            loss_ref[...] = jnp.zeros_like(loss_ref)
            dgf_ref[...] = jnp.zeros_like(dgf_ref)

        xv = x3_ref[...]
        r = _rstd(xv)
        diff = xv * r * gf_ref[...] - tgt_ref[...]
        loss_ref[...] += 0.5 * jnp.sum(jnp.sum(diff * diff, axis=-1, keepdims=True), axis=0, keepdims=True) * (1.0 / d)
        dx3, dgf = _rms_bwd(diff * (1.0 / d), xv, r, gf_ref[...])
        dgf_ref[...] += dgf
        dx3_ref[...] = dx3
        dx3b_ref[...] = dx3.astype(BF16)

    tok = pl.BlockSpec((tm, d), lambda i: (i, 0))
    vec = pl.BlockSpec((1, d), lambda i: (0, 0))
    return _pcall(
        body, grid=(s // tm,), in_specs=[tok, tok, vec],
        out_specs=[pl.BlockSpec((SUB, LANES), lambda i: (0, 0)), vec, tok, tok],
        out_shape=(SDS((SUB, LANES), F32), SDS((1, d), F32), SDS((s, d), F32), SDS((s, d), BF16)),
        compiler_params=_arb(1), name="final_backward")(x3, target, g_final)


def _swiglu_backward(dx3b, gu, w_gu, w_down, tm):
    s, d = dx3b.shape
    _, nf, tf, _ = w_gu.shape

    def body(dx3b_ref, gu_ref, wgu_ref, wd_ref, act_ref, dgu_ref, dxn_ref):
        @pl.when(pl.program_id(1) == 0)
        def _():
            dxn_ref[...] = jnp.zeros_like(dxn_ref)

        for r0 in range(0, tm, ROW_CHUNK):
            rows = slice(r0, r0 + ROW_CHUNK)
            dact = _dot_nt(dx3b_ref[rows, :], wd_ref[...])
            gv = gu_ref[0, rows, :].astype(F32)
            uv = gu_ref[1, rows, :].astype(F32)
            sg = _sigmoid(gv)
            silu = gv * sg
            act_ref[rows, :] = (silu * uv).astype(BF16)
            dgate = (dact * uv * (sg * (1.0 + gv * (1.0 - sg)))).astype(BF16)
            dup = (dact * silu).astype(BF16)
            dgu_ref[0, rows, :] = dgate
            dgu_ref[1, rows, :] = dup
            part = _dot(dgate, wgu_ref[0]) + _dot(dup, wgu_ref[1])
            dxn_ref[rows, :] += part

    tok = pl.BlockSpec((tm, d), lambda i, f: (i, 0))
    pair = pl.BlockSpec((2, None, tm, tf), lambda i, f: (0, f, i, 0))
    return _pcall(
        body, grid=(s // tm, nf),
        in_specs=[tok, pair, pl.BlockSpec((2, None, tf, d), lambda i, f: (0, f, 0, 0)),
                  pl.BlockSpec((tf, d), lambda i, f: (f, 0))],
        out_specs=[pl.BlockSpec((None, tm, tf), lambda i, f: (f, i, 0)), pair, tok],
        out_shape=(SDS((nf, s, tf), BF16), SDS((2, nf, s, tf), BF16), SDS((s, d), F32)),
        compiler_params=_arb(2), name="swiglu_backward")(dx3b, gu, w_gu, w_down)


def _attn_backward(dx3, dxn3, x2, g_ffn, x1, g, q, probs, kv, w_q, w_o, tm, carry=()):
    s, d = x1.shape
    _, ml, xd = kv.shape
    scale = xd ** -0.5

    def body(dx3_ref, dxn3_ref, x2_ref, g2_ref, x1_ref, g_ref, q_ref, p_ref, kv_ref, wq_ref, wo_ref,
             dx2b_ref, dq_ref, dx1_ref, dx1b_ref, dkv_ref, dg_ref, dg2_ref, do_s):
        i = pl.program_id(0)

        @pl.when(i == 0)
        def _():
            dkv_ref[...] = jnp.zeros_like(dkv_ref)
            dg_ref[...] = jnp.zeros_like(dg_ref)
            dg2_ref[...] = jnp.zeros_like(dg2_ref)

        x2v = x2_ref[...]
        dx2n, dg2 = _rms_bwd(dxn3_ref[...], x2v, _rstd(x2v), g2_ref[...])
        dg2_ref[...] += dg2
        dx2 = dx3_ref[...] + dx2n
        dx2b_ref[...] = dx2.astype(BF16)
        do_s[...] = _dot_nt(dx2b_ref[...], wo_ref[...]).astype(BF16)
        for hh in range(HEADS):
            kc = slice(hh * xd, (hh + 1) * xd)
            qh = q_ref[:, kc]
            kh = kv_ref[hh]
            doh = do_s[:, kc]
            pb = p_ref[:, hh * ml:(hh + 1) * ml]
            p = pb.astype(F32)
            dp = _dot_nt(doh, kv_ref[HEADS + hh])
            dkv_ref[HEADS + hh] += _dot_tn(pb, doh)
            ds = (p * (dp - jnp.sum(dp * p, axis=-1, keepdims=True)) * scale).astype(BF16)
            dq_ref[:, kc] = _dot(ds, kh).astype(BF16)
            dkv_ref[hh] += _dot_tn(ds, qh)
        dxn = _dot_nt(dq_ref[...], wq_ref[...])
        xv = x1_ref[...]
        dx, dg = _rms_bwd(dxn, xv, _rstd(xv), g_ref[...])
        dg_ref[...] += dg
        dx1 = dx2 + dx
        dx1_ref[...] = dx1
        dx1b_ref[...] = dx1.astype(BF16)

    tok = pl.BlockSpec((tm, d), lambda i: (i, 0))
    vec = pl.BlockSpec((1, d), lambda i: (0, 0))
    sq = pl.BlockSpec((d, d), lambda i: (0, 0))
    kvs = pl.BlockSpec((2 * HEADS, ml, xd), lambda i: (0, 0, 0))
    return _hosted(
        body, carry, grid=(s // tm,),
        in_specs=[tok, tok, tok, vec, tok, vec, tok, pl.BlockSpec((tm, HEADS * ml), lambda i: (i, 0)), kvs, sq, sq],
        out_specs=[tok, tok, tok, tok, kvs, vec, vec],
        out_shape=(SDS((s, d), BF16), SDS((s, d), BF16), SDS((s, d), F32), SDS((s, d), BF16),
                   SDS((2 * HEADS, ml, xd), F32), SDS((1, d), F32), SDS((1, d), F32)),
        scratch_shapes=[pltpu.VMEM((tm, d), BF16)],
        compiler_params=_arb(1), name="attn_backward")(dx3, dxn3, x2, g_ffn, x1, g, q, probs, kv, w_q, w_o)


def _kv_backward(dkv, memn, mem, g_mem, w_kv):
    ml, d = mem.shape
    xd = w_kv.shape[2]

    def body(dkv_ref, memn_ref, mem_ref, g_ref, w_ref, dw_ref, dg_ref):
        dmemn = jnp.zeros((ml, d), F32)
        for j in range(2 * HEADS):
            dkvb = dkv_ref[j].astype(BF16)
            dw_ref[j] = _dot_tn(memn_ref[...], dkvb)
            dmemn = dmemn + _dot_nt(dkvb, w_ref[j])
        x = mem_ref[...]
        dg_ref[...] = jnp.sum(dmemn * (x * _rstd(x)), axis=0, keepdims=True)

    return _pcall(body, out_shape=(SDS((2 * HEADS, d, xd), F32), SDS((1, d), F32)), name="kv_backward")(dkv, memn, mem, g_mem, w_kv)


def _mix_backward(dx1, x, g_mix, h, lng, lnb, w_sp, bt, conv_w, ga, gb, w_out, w_in, tm, carry=()):
    s, d = x.shape
    n_in = h.shape[1]
    aw = lng.shape[1]
    bw = d - aw
    hd = aw // HEADS
    in_a = 2 * aw
    hb_blocks = tm // HALO
    last_blk = s // HALO - 1
    nt = s // tm
    tc = tm
    te = tc + HALO
    tee = tc + 2 * HALO

    def body(dx1_ref, dx1n_ref, x_ref, gm_ref, h_ref, hp_ref, hn_ref, lng_ref, lnb_ref, wsp_ref, bt_ref, cw_ref,
             ga_ref, gb_ref, wout_ref, win_ref,
             dh_ref, dx_ref, dga_ref, dgb_ref, dcw_ref, dlng_ref, dlnb_ref, dwsp_ref, dbs_ref, dgm_ref,
             mixed_s, dvln_s):
        i = pl.program_id(0)

        @pl.when(i == 0)
        def _():
            for ref in (dga_ref, dgb_ref, dcw_ref, dlng_ref, dlnb_ref, dwsp_ref, dbs_ref, dgm_ref):
                ref[...] = jnp.zeros_like(ref)

        mask = _tril_mask()
        wm = [(wsp_ref[hh] * mask).astype(BF16) for hh in range(HEADS)]
        cw = cw_ref[...]

        def chain(r0):
            rows = slice(r0, r0 + tc)
            first, last = r0 == 0, r0 + tc == tm
            hv = h_ref[rows, :]
            dx1 = dx1_ref[rows, :]
            dx1n = dx1n_ref[...] if last else dx1_ref[r0 + tc:r0 + tc + HALO, :]
            hp = hp_ref[:, in_a:] if first else h_ref[r0 - HALO:r0, in_a:]
            hn = hn_ref[:, in_a:] if last else h_ref[r0 + tc:r0 + tc + HALO, in_a:]
            dx1e = jnp.concatenate([dx1, dx1n], axis=0).astype(BF16)
            dycat = _dot_nt(dx1e, wout_ref[...])

            hbe = jnp.concatenate([hp, hv[:, in_a:], hn], axis=0)
            row = lax.broadcasted_iota(jnp.int32, (tee, 1), 0)
            zext = hbe[:, bw:2 * bw] * hbe[:, 2 * bw:]
            if first:
                zext = jnp.where((i == 0) & (row < HALO), 0.0, zext)
            z2e, z1e = _conv_taps(zext)
            conv_e = (cw[0:1] * z2e + cw[1:2] * z1e + cw[2:3] * zext)[HALO:]
            gate_b_e = hbe[HALO:, :bw]
            sc_e = gate_b_e * conv_e
            rb = _rstd(sc_e)
            dyb = dycat[:, aw:]
            gdy = dyb * gb_ref[...]
            dsc_e = rb * gdy - sc_e * (rb * rb * rb) * (jnp.sum(gdy * sc_e, axis=-1, keepdims=True) * (1.0 / bw))
            dgb_ref[...] += jnp.sum((dyb * (sc_e * rb))[:tc], axis=0, keepdims=True)
            dconv_e = dsc_e * gate_b_e
            if last:
                dconv_e = jnp.where((i == nt - 1) & (row[:te] >= tc), 0.0, dconv_e)
            dconv = dconv_e[:tc]
            dc1 = pltpu.roll(dconv_e, te - 1, 0)[:tc]
            dc2 = pltpu.roll(dconv_e, te - 2, 0)[:tc]
            dz = cw[2:3] * dconv + cw[1:2] * dc1 + cw[0:1] * dc2
            z = zext[HALO:HALO + tc]
            z1 = z1e[HALO:HALO + tc]
            z2 = z2e[HALO:HALO + tc]
            dcw_ref[0:1, :] += jnp.sum(dconv * z2, axis=0, keepdims=True)
            dcw_ref[1:2, :] += jnp.sum(dconv * z1, axis=0, keepdims=True)
            dcw_ref[2:3, :] += jnp.sum(dconv * z, axis=0, keepdims=True)
            dh_ref[rows, in_a:in_a + bw] = (dsc_e[:tc] * conv_e[:tc]).astype(BF16)
            dh_ref[rows, in_a + bw:in_a + 2 * bw] = (dz * hv[:, in_a + 2 * bw:]).astype(BF16)
            dh_ref[rows, in_a + 2 * bw:] = (dz * hv[:, in_a + bw:in_a + 2 * bw]).astype(BF16)

            ha = hv[:, :in_a]
            mixed_c, dvln_c = mixed_s.at[rows, :], dvln_s.at[rows, :]
            th, u, xhat, rl, vln = _sgu_forward(ha, lng_ref[...], lnb_ref[...], wm, bt_ref[...], mixed_c)
            mixed = mixed_c[...]
            sg = u * mixed
            dsg, dga = _rms_bwd(dycat[:tc, :aw], sg, _rstd(sg), ga_ref[...])
            dga_ref[...] += dga
            du = dsg * mixed
            dmixed = dsg * u
            dmb = dmixed.astype(BF16)
            for n in range(tc // CHUNK):
                blk = slice(n * CHUNK, (n + 1) * CHUNK)
                dbs_ref[...] += dmixed[blk]
                for hh in range(HEADS):
                    cols = slice(hh * hd, (hh + 1) * hd)
                    dvln_c[blk, cols] = _dot_tn(wm[hh], dmb[blk, cols])
                    dwsp_ref[hh] += mask * _dot_nt(dmb[blk, cols], vln[blk, cols])
            dvln = dvln_c[...]
            dlng_ref[...] += jnp.sum(dvln * xhat, axis=0, keepdims=True)
            dlnb_ref[...] += jnp.sum(dvln, axis=0, keepdims=True)
            dxh = dvln * lng_ref[...]
            dv = rl * (dxh - jnp.mean(dxh, axis=-1, keepdims=True) - xhat * jnp.mean(dxh * xhat, axis=-1, keepdims=True))
            dh_ref[rows, :in_a] = (jnp.concatenate([du, dv], axis=-1) * _gelu_grad(ha, th)).astype(BF16)

            dxn = _dot(dh_ref[rows, :], win_ref[...])
            xv = x_ref[rows, :]
            dx, dgm = _rms_bwd(dxn, xv, _rstd(xv), gm_ref[...])
            dgm_ref[...] += dgm
            dx_ref[rows, :] = dx1 + dx

        for r0 in range(0, tm, tc):
            chain(r0)

    full = lambda shape: pl.BlockSpec(shape, lambda i: (0,) * len(shape))
    tok = pl.BlockSpec((tm, d), lambda i: (i, 0))
    nxt = lambda i: (jnp.minimum((i + 1) * hb_blocks, last_blk), 0)
    prv = lambda i: (jnp.maximum(i * hb_blocks - 1, 0), 0)
    return _hosted(
        body, carry, grid=(nt,),
        in_specs=[tok, pl.BlockSpec((HALO, d), nxt), tok, full((1, d)),
                  pl.BlockSpec((tm, n_in), lambda i: (i, 0)), pl.BlockSpec((HALO, n_in), prv), pl.BlockSpec((HALO, n_in), nxt),
                  full((1, aw)), full((1, aw)), full((HEADS, CHUNK, CHUNK)), full((CHUNK, HEADS)), full((3, bw)),
                  full((1, aw)), full((1, bw)), full((d, d)), full((n_in, d))],
        out_specs=[pl.BlockSpec((tm, n_in), lambda i: (i, 0)), tok,
                   full((1, aw)), full((1, bw)), full((SUB, bw)), full((1, aw)), full((1, aw)),
                   full((HEADS, CHUNK, CHUNK)), full((CHUNK, aw)), full((1, d))],
        out_shape=(SDS((s, n_in), BF16), SDS((s, d), F32),
                   SDS((1, aw), F32), SDS((1, bw), F32), SDS((SUB, bw), F32), SDS((1, aw), F32), SDS((1, aw), F32),
                   SDS((HEADS, CHUNK, CHUNK), F32), SDS((CHUNK, aw), F32), SDS((1, d), F32)),
        scratch_shapes=[pltpu.VMEM((tm, aw), F32), pltpu.VMEM((tm, aw), F32)],
        compiler_params=_arb(1), name="mix_backward")(dx1, dx1, x, g_mix, h, h, h, lng, lnb, w_sp, bt, conv_w, ga, gb, w_out, w_in)


def _bias_grad(dbs):
    aw = dbs.shape[1]
    hd = aw // HEADS

    def body(dbs_ref, out_ref):
        ones = jnp.ones((SUB, hd), F32)
        for hh in range(HEADS):
            r = lax.dot_general(ones, dbs_ref[:, hh * hd:(hh + 1) * hd], (((1,), (1,)), ((), ())),
                                precision=lax.Precision.HIGHEST, preferred_element_type=F32)
            out_ref[hh:hh + 1, :] = r[0:1]

    return _pcall(body, out_shape=SDS((HEADS, CHUNK), F32), name="bias_grad")(dbs)


def _wgrad_body(a_ref, b_ref, o_ref):
    o_ref[...] = _dot_tn(a_ref[...], b_ref[...])


def _wgrad(a, b, name, carry=()):
    k, m = a.shape
    n = b.shape[1]
    tm = _tile(m, 512, LANES)
    tn = _tile(n, 1024, LANES)
    return _hosted(
        functools.partial(_wgrad_body), carry, grid=(m // tm, n // tn),
        in_specs=[pl.BlockSpec((k, tm), lambda i, j: (0, i)), pl.BlockSpec((k, tn), lambda i, j: (0, j))],
        out_specs=pl.BlockSpec((tm, tn), lambda i, j: (i, j)),
        out_shape=SDS((m, n), F32), compiler_params=_arb(2), name=name)(a, b)


def _wgrad_blocked_lhs(a, b, name, carry=()):
    nb, k, t = a.shape
    n = b.shape[1]
    tn = _tile(n, 1024, LANES)
    return _hosted(
        functools.partial(_wgrad_body), carry, grid=(nb, n // tn),
        in_specs=[pl.BlockSpec((None, k, t), lambda i, j: (i, 0, 0)), pl.BlockSpec((k, tn), lambda i, j: (0, j))],
        out_specs=pl.BlockSpec((t, tn), lambda i, j: (i, j)),
        out_shape=SDS((nb * t, n), F32), compiler_params=_arb(2), name=name)(a, b)


def _place():
    x, y, c = lax.axis_index("x"), lax.axis_index("y"), lax.axis_index("c")
    return x, y, c, [(1 - x, y), (x, 1 - y), (1 - x, 1 - y)]


def _all_gather(shards):
    n = len(shards)
    slots = 9
    cut = [(s.shape[0] // 32) * 16 for s in shards]

    def build(ins, outs, sems):
        send_sems, recv_sems, local_sems = sems
        x, y, c, _ = _place()
        me, sib, xn, yn, dg = (x, y, c), (x, y, 1 - c), (1 - x, y, c), (x, 1 - y, c), (1 - x, 1 - y, c)
        other = lambda p: (p[0], p[1], 1 - p[2])

        def rows(a, p, part=None):
            ref = outs[a].at[4 * p[0] + 2 * p[1] + p[2]]
            if part is None or cut[a] == 0:
                return ref if part in (None, 0) else None
            return ref.at[pl.ds(0, cut[a])] if part == 0 else ref.at[pl.ds(cut[a], shards[a].shape[0] - cut[a])]

        def copy(a, k, ref, to, src=None):
            if ref is None:
                return None
            return pltpu.make_async_remote_copy(
                src_ref=ref if src is None else src, dst_ref=ref, send_sem=send_sems.at[slots * a + k],
                recv_sem=recv_sems.at[slots * a + k], device_id=to, device_id_type=MESH)

        def real(cps):
            return [cp for cp in cps if cp is not None]

        class Copies:
            own = lambda a: [copy(a, 1, rows(a, me), xn, ins[a]), copy(a, 2, rows(a, me), yn, ins[a]),
                             copy(a, 0, rows(a, me), sib, ins[a])]
            local = lambda a: pltpu.make_async_copy(ins[a], rows(a, me), local_sems.at[a])
            from_x = lambda a: copy(a, 1, rows(a, xn), me)
            from_y = lambda a: copy(a, 2, rows(a, yn), me)
            after_x = lambda a: real([copy(a, 4, rows(a, xn, 1), yn), copy(a, 5, rows(a, xn), sib)])
            after_y = lambda a: real([copy(a, 3, rows(a, yn, 0), xn), copy(a, 6, rows(a, yn), sib)])
            diag_in = lambda a: real([copy(a, 3, rows(a, dg, 0), me), copy(a, 4, rows(a, dg, 1), me)])
            diag_on = lambda a: real([copy(a, 7, rows(a, dg, 0), sib), copy(a, 8, rows(a, dg, 1), sib)])
            from_sib = lambda a: real([copy(a, 0, rows(a, sib), me), copy(a, 5, rows(a, other(xn)), me),
                                       copy(a, 6, rows(a, other(yn)), me), copy(a, 7, rows(a, other(dg), 0), me),
                                       copy(a, 8, rows(a, other(dg), 1), me)])

        return Copies

    def start(ins, outs, sems):
        cps = build(ins, outs, sems)
        for a in range(n):
            for cp in cps.own(a):
                cp.start()
        for a in range(n):
            cps.local(a).start()

    def relay(ins, outs, sems):
        cps = build(ins, outs, sems)
        for a in range(n):
            cps.from_x(a).wait_recv()
            for cp in cps.after_x(a):
                cp.start()
            cps.from_y(a).wait_recv()
            for cp in cps.after_y(a):
                cp.start()

    def finish(ins, outs, sems):
        cps = build(ins, outs, sems)
        for a in range(n):
            for arrived, onward in zip(cps.diag_in(a), cps.diag_on(a)):
                arrived.wait_recv()
                onward.start()
        for a in range(n):
            for cp in cps.from_sib(a):
                cp.wait_recv()
            for cp in cps.own(a) + cps.after_x(a) + cps.after_y(a) + cps.diag_on(a):
                cp.wait_send()
            cps.local(a).wait()

    def peers():
        x, y, c, _ = _place()
        return [(x, y, 1 - c), (1 - x, y, c), (x, 1 - y, c)]

    return _Exchange(shards, [SDS((N_DEV,) + s.shape, s.dtype) for s in shards],
                     [pltpu.SemaphoreType.DMA((slots * n,)), pltpu.SemaphoreType.DMA((slots * n,)),
                      pltpu.SemaphoreType.DMA((n,))], start, finish, relay, peers, GATHER_COLLECTIVE)


def _swap_exchange(ins, out_shape, per, copies):
    def start(i, o, sems):
        for cp in copies(i, o, sems):
            cp.start()

    def finish(i, o, sems):
        for cp in copies(i, o, sems):
            cp.wait()

    def sibling():
        x, y, c, _ = _place()
        return [(x, y, 1 - c)]

    n = per * len(ins)
    return _Exchange(ins, out_shape, [pltpu.SemaphoreType.DMA((n,)), pltpu.SemaphoreType.DMA((n,))], start, finish,
                     peers=sibling, collective=SIBLING_COLLECTIVE)


def _exchange_c(gs):
    def copies(ins, outs, sems):
        x, y, c, _ = _place()
        return [pltpu.make_async_remote_copy(
                    src_ref=ins[a].at[2 * k + 1 - c], dst_ref=outs[a].at[k],
                    send_sem=sems[0].at[4 * a + k], recv_sem=sems[1].at[4 * a + k],
                    device_id=(x, y, 1 - c), device_id_type=MESH)
                for a in range(len(gs)) for k in range(4)]

    return _swap_exchange(gs, [SDS((4,) + g.shape[1:], g.dtype) for g in gs], 4, copies)


def _rs_combine(g, recv, pos, name, carry=()):
    _, r, cdim = g.shape
    tr = _tile(r, 256, 16)

    def body(pos_ref, g0, r0, g1, r1, g2, r2, g3, r3, keep_ref, send_ref):
        keep_ref[...] = g0[...] + r0[...]
        send_ref[0] = (g1[...] + r1[...]).astype(BF16)
        send_ref[1] = (g2[...] + r2[...]).astype(BF16)
        send_ref[2] = (g3[...] + r3[...]).astype(BF16)

    def k_of(p, t):
        px = p[0] if t in (0, 2) else 1 - p[0]
        py = p[1] if t in (0, 1) else 1 - p[1]
        return 2 * px + py

    blk = (None, tr, cdim)
    in_specs = []
    for t in range(4):
        in_specs.append(pl.BlockSpec(blk, functools.partial(lambda j, p, t: (2 * k_of(p, t) + p[2], j, 0), t=t)))
        in_specs.append(pl.BlockSpec(blk, functools.partial(lambda j, p, t: (k_of(p, t), j, 0), t=t)))
    return _hosted(
        body, carry, n_prefetch=1, out_shape=(SDS((r, cdim), F32), SDS((3, r, cdim), BF16)),
        grid=(r // tr,), in_specs=in_specs,
        out_specs=[pl.BlockSpec((tr, cdim), lambda j, p: (j, 0)), pl.BlockSpec((3, tr, cdim), lambda j, p: (0, j, 0))],
        compiler_params=_arb(1), name=name)(pos, g, recv, g, recv, g, recv, g, recv)


def _adamw_shard(keep, recv, w, m, v, name):
    r, cdim = w.shape
    tr = _tile(r, 256, 16)

    def body(k_ref, r_ref, w_ref, m_ref, v_ref, g_ref, d_ref, nm_ref, nv_ref):
        g = ((k_ref[...] + r_ref[0].astype(F32)) + r_ref[1].astype(F32)) + r_ref[2].astype(F32)
        g_ref[...] = g
        d_ref[...], nm_ref[...], nv_ref[...] = _adamw(w_ref[...], g, m_ref[...], v_ref[...])

    blk = pl.BlockSpec((tr, cdim), lambda j: (j, 0))
    out = SDS((r, cdim), F32)
    return _pcall(body, grid=(r // tr,), in_specs=[blk, pl.BlockSpec((3, tr, cdim), lambda j: (0, j, 0)), blk, blk, blk],
                  out_specs=[blk] * 4, out_shape=(out,) * 4, compiler_params=_arb(1), name=name)(keep, recv, w, m, v)


_HBM = pl.BlockSpec(memory_space=pltpu.HBM)
_SEM = pl.BlockSpec(memory_space=pltpu.SEMAPHORE)
_SPLIT = pltpu.CompilerParams(has_side_effects=pltpu.SideEffectType.DATAFLOW_SIDE_EFFECTING)


def _split_copies(kind, n, refs):
    srcs, lands, (send_sems, recv_sems) = refs[:n], refs[n:2 * n], refs[2 * n:2 * n + 2]
    x, y, c, chips = _place()
    per = _SPLIT_COPIES[kind]
    if kind == "xy":
        ends = lambda a, t: (srcs[a].at[t], lands[a].at[t], (*chips[t], c))
    else:
        ends = lambda a, k: (srcs[a].at[2 * k + 1 - c], lands[a].at[k], (x, y, 1 - c))
    cps = []
    for a in range(n):
        for t in range(per):
            src, dst, to = ends(a, t)
            cps.append(pltpu.make_async_remote_copy(src_ref=src, dst_ref=dst, send_sem=send_sems.at[per * a + t],
                                                    recv_sem=recv_sems.at[per * a + t], device_id=to, device_id_type=MESH))
    return cps


_SPLIT_COPIES = {"xy": 3, "c": 4}
SIBLING_COLLECTIVE = 1
GATHER_COLLECTIVE = 5


def _exchange_start(kind, arrays, name, after=None, collective=None):
    n = len(arrays)
    order = [] if after is None else [after]

    def body(*refs):
        x, y, c, chips = _place()
        peers = [(x, y, 1 - c)] if kind == "c" else [(*chip, c) for chip in chips]
        barrier = pltpu.get_barrier_semaphore()
        for peer in peers:
            pl.semaphore_signal(barrier, inc=1, device_id=peer, device_id_type=MESH)
        pl.semaphore_wait(barrier, len(peers))
        refs = refs[:2 * n] + refs[2 * n + len(order):]
        for cp in _split_copies(kind, n, refs):
            cp.start()
        refs[-1][...] = jnp.zeros_like(refs[-1])

    params = pltpu.CompilerParams(has_side_effects=pltpu.SideEffectType.DATAFLOW_SIDE_EFFECTING,
                                  collective_id=SIBLING_COLLECTIVE if kind == "c" else collective)
    hbm = lambda a: pltpu.with_memory_space_constraint(a, pltpu.HBM)
    land = [a.shape if kind == "xy" else (4,) + a.shape[1:] for a in arrays]
    bufs = [pltpu.HBM(a.shape, a.dtype) for a in arrays] + [pltpu.HBM(s, a.dtype) for s, a in zip(land, arrays)]
    sems = pltpu.SemaphoreType.DMA((_SPLIT_COPIES[kind] * n,))
    res = _pcall(
        body, name=name, out_shape=(sems, sems, *bufs, SDS((SUB, LANES), F32)),
        in_specs=[_HBM] * (2 * n) + _hbm_specs(len(order)),
        out_specs=[_SEM, _SEM] + [_HBM] * (2 * n) + [pl.BlockSpec(memory_space=pltpu.VMEM)],
        input_output_aliases={k: 2 + k for k in range(2 * n)}, compiler_params=params)(
            *[hbm(a) for a in arrays], *[hbm(lax.empty(s, a.dtype)) for s, a in zip(land, arrays)], *order)
    return (kind, n, res[:-1]), res[-1]


def _exchange_wait(started, after, name, sources=False):
    kind, n, (send_sems, recv_sems, *bufs) = started

    def body(*refs):
        for cp in _split_copies(kind, n, refs):
            cp.wait_send()
            cp.wait_recv()

    shapes = [pltpu.HBM(b.shape, b.dtype) for b in bufs]
    res = _pcall(
        body, name=name, out_shape=tuple(shapes),
        in_specs=[_HBM] * (2 * n) + [_SEM, _SEM, pl.BlockSpec(memory_space=pl.ANY)], out_specs=[_HBM] * (2 * n),
        input_output_aliases={k: k for k in range(2 * n)}, compiler_params=_SPLIT)(*bufs, send_sems, recv_sems, after)
    return (list(res[:n]), list(res[n:])) if sources else list(res[n:])


def _follow(token):
    nothing = lambda ins, outs, sems: None
    return _Exchange([token], [], [], nothing, nothing)


def _adamw_small(gathered, seg, params, conv_rows):
    names = list(params)
    c0, cn = conv_rows

    def body(*refs):
        gat_ref = refs[0]
        ins = refs[1:1 + 3 * len(names)]
        outs = refs[1 + 3 * len(names):]

        def total(r0, rn):
            tot = gat_ref[0, r0:r0 + rn, :]
            for dev in range(1, N_DEV):
                tot = tot + gat_ref[dev, r0:r0 + rn, :]
            return tot

        for k, nm in enumerate(names):
            g = total(*seg[nm])
            w_ref, m_ref, v_ref = ins[3 * k:3 * k + 3]
            g_ref, d_ref, nm_ref, nv_ref = outs[4 * k:4 * k + 4]
            g_ref[...] = g
            d_ref[...], nm_ref[...], nv_ref[...] = _adamw(w_ref[...], g, m_ref[...], v_ref[...])
        outs[-2][...] = total(c0, cn)
        outs[-1][...] = total(*seg["loss"])

    flat_in = [a for nm in names for a in params[nm]]
    out_shape = []
    for nm in names:
        out_shape += [SDS(params[nm][0].shape, F32)] * 4
    out_shape += [SDS((cn, LANES), F32), SDS((seg["loss"][1], LANES), F32)]
    res = _pcall(body, out_shape=tuple(out_shape), name="adamw_small")(gathered, *flat_in)
    per = {nm: res[4 * k:4 * k + 4] for k, nm in enumerate(names)}
    return per, res[-2], res[-1]


def _adamw_one(w, g, m, v, name):
    def body(w_ref, g_ref, m_ref, v_ref, d_ref, nm_ref, nv_ref):
        d_ref[...], nm_ref[...], nv_ref[...] = _adamw(w_ref[...], g_ref[...], m_ref[...], v_ref[...])

    return _pcall(body, out_shape=(SDS(w.shape, F32),) * 3, name=name)(w, g, m, v)


def _rows128(a):
    return a.reshape(-1, LANES)


def _pack_small(gs, loss_tile):
    seg, pieces, row = {}, [], 0
    for nm in SMALL + ("conv_w", "loss"):
        piece = loss_tile if nm == "loss" else _rows128(gs[nm])
        rn = _round_up(piece.shape[0], SUB)
        pieces.append(jnp.pad(piece, ((0, rn - piece.shape[0]), (0, 0))))
        seg[nm] = (row, piece.shape[0])
        row += rn
    return jnp.concatenate(pieces, axis=0), seg


def _step(x, mem, target, wb, conv_w, sp, pos):
    s, d = x.shape
    tm = min(TOKEN_TILE, s)
    tm_wide = min(2 * TOKEN_TILE, s)
    rows = lambda w8: w8.reshape(-1, w8.shape[2])
    shards = lambda g: g.reshape((N_DEV, -1) + g.shape[1:])
    bt = sp["b_spatial"].T

    rest = [nm for nm in BIG if nm != "w_in"]
    cast, ((w_in8, conv8),) = _cast_bf16([wb[nm] for nm in rest], "gather_w_in", carry=[_all_gather([wb["w_in"], conv_w])])
    wb = dict(wb, **dict(zip(rest, cast)))
    conv_full = conv8.transpose(1, 0, 2).reshape(3, -1)
    w_in_t = rows(w_in8)
    (xn1, h), ((w_out8, w_kv8, w_q8),) = _in_forward(
        x, sp["ln_mix_g"], w_in_t, tm, carry=[_all_gather([wb["w_out"], wb["w_kv"], wb["w_q"]])])
    w_out = rows(w_out8)
    (ycat, x1), ((w_o8, w_down8),) = _mix_forward(
        h, x, sp["sgu_ln_g"], sp["sgu_ln_b"], sp["w_spatial"], bt, conv_full, sp["grp_norm_a"], sp["grp_norm_b"], w_out, tm,
        carry=[_all_gather([wb["w_o"], wb["w_down"]])])
    w_q, w_o, w_down = rows(w_q8), rows(w_o8), rows(w_down8)
    memn, kv = _kv_forward(mem, sp["ln_mem_g"], w_kv8)
    (xn2, q, probs, o, x2), ((w_gu8,),) = _attn_forward(
        x1, sp["ln_attn_g"], w_q, kv, w_o, tm, carry=[_all_gather([wb["w_gate_up"]])])
    w_gu = w_gu8.reshape((2, N_DEV // 2) + w_gu8.shape[1:])
    xn3, gu, x3 = _ffn_forward(x2, sp["ln_ffn_g"], w_gu, w_down, tm_wide)

    loss, d_lnf, dx3, dx3b = _final_backward(x3, target, sp["ln_final_g"], tm_wide)
    act, dgu, dxn3 = _swiglu_backward(dx3b, gu, w_gu, w_down, tm_wide)
    g_gu, _ = _wgrad_blocked_lhs(dgu.reshape((N_DEV,) + dgu.shape[2:]), xn3, "wgrad_gate_up")
    g_gu = shards(g_gu)
    g_down, ((rc_gu,),) = _wgrad_blocked_lhs(act, dx3b, "wgrad_down", carry=[_exchange_c([g_gu])])
    g_down = shards(g_down)
    keep, pending = {}, []
    (keep["w_gate_up"], send_gu), _ = _rs_combine(g_gu, rc_gu, pos, "rs_combine_w_gate_up")
    started, token = _exchange_start("xy", [send_gu], "exchange_xy_1_start", collective=2)
    pending.append((("w_gate_up",), started))
    c_down, token = _exchange_start("c", [g_down], "exchange_c_1_start", after=token)
    (dx2b, dq, dx1, dx1b, dkv, d_lnattn, d_lnffn), _ = _attn_backward(
        dx3, dxn3, x2, sp["ln_ffn_g"], x1, sp["ln_attn_g"], q, probs, kv, w_q, w_o, tm, carry=[_follow(token)])
    (g_down,), (rc_down,) = _exchange_wait(c_down, dx1b, "exchange_c_1_wait", sources=True)
    (keep["w_down"], send_down), _ = _rs_combine(g_down, rc_down, pos, "rs_combine_w_down")
    g_o, _ = _wgrad(o, dx2b, "wgrad_o")
    g_q, _ = _wgrad(xn2, dq, "wgrad_q")
    g_o, g_q = shards(g_o), shards(g_q)
    g_kv, d_lnmem = _kv_backward(dkv, memn, mem, sp["ln_mem_g"], w_kv8)
    c_oqkv, token = _exchange_start("c", [g_o, g_q, g_kv], "exchange_c_2_start")
    g_out, _ = _wgrad(ycat, dx1b, "wgrad_out", carry=[_follow(token)])
    g_out = shards(g_out)
    (g_o, g_q, g_kv), (rc_o, rc_q, rc_kv) = _exchange_wait(c_oqkv, g_out, "exchange_c_2_wait", sources=True)
    c_out, token = _exchange_start("c", [g_out], "exchange_c_3_start")
    (keep["w_o"], send_o), _ = _rs_combine(g_o, rc_o, pos, "rs_combine_w_o", carry=[_follow(token)])
    (keep["w_q"], send_q), _ = _rs_combine(g_q, rc_q, pos, "rs_combine_w_q")
    (keep["w_kv"], send_kv), _ = _rs_combine(g_kv, rc_kv, pos, "rs_combine_w_kv")
    (g_out,), (rc_out,) = _exchange_wait(c_out, send_kv, "exchange_c_3_wait", sources=True)
    (keep["w_out"], send_out), _ = _rs_combine(g_out, rc_out, pos, "rs_combine_w_out")
    started, token = _exchange_start("xy", [send_down, send_o, send_q, send_out, send_kv], "exchange_xy_2_start",
                                     collective=3)
    pending.append((("w_down", "w_o", "w_q", "w_out", "w_kv"), started))
    (dh, dx, d_ga, d_gb, d_cw, d_lng, d_lnb, d_wsp, d_bs, d_lnmix), _ = _mix_backward(
        dx1, x, sp["ln_mix_g"], h, sp["sgu_ln_g"], sp["sgu_ln_b"], sp["w_spatial"], bt, conv_full,
        sp["grp_norm_a"], sp["grp_norm_b"], w_out, w_in_t, tm, carry=[_follow(token)])
    gs = {"ln_mix_g": d_lnmix, "sgu_ln_g": d_lng, "sgu_ln_b": d_lnb, "w_spatial": d_wsp, "b_spatial": _bias_grad(d_bs),
          "conv_w": d_cw[:3], "grp_norm_a": d_ga, "grp_norm_b": d_gb, "ln_attn_g": d_lnattn, "ln_mem_g": d_lnmem,
          "ln_ffn_g": d_lnffn, "ln_final_g": d_lnf}
    packed, seg = _pack_small(gs, loss)
    g_in, (_, (small_all,)) = _wgrad(dh, xn1, "wgrad_in", carry=[_follow(token), _all_gather([packed])])
    g_in = shards(g_in)
    c_in, token = _exchange_start("c", [g_in], "exchange_c_4_start")
    return dx, keep, pending, (g_in, c_in), token, small_all, seg


def kernel(x, mem, ln_mix_g, w_in, sgu_ln_g, sgu_ln_b, w_spatial, b_spatial, conv_w, grp_norm_a, grp_norm_b, w_out, ln_attn_g, ln_mem_g, w_q, w_kv, w_o, ln_ffn_g, w_gate_up, w_down, ln_final_g, loss_target, m_ln_mix_g, m_w_in, m_sgu_ln_g, m_sgu_ln_b, m_w_spatial, m_b_spatial, m_conv_w, m_grp_norm_a, m_grp_norm_b, m_w_out, m_ln_attn_g, m_ln_mem_g, m_w_q, m_w_kv, m_w_o, m_ln_ffn_g, m_w_gate_up, m_w_down, m_ln_final_g, v_ln_mix_g, v_w_in, v_sgu_ln_g, v_sgu_ln_b, v_w_spatial, v_b_spatial, v_conv_w, v_grp_norm_a, v_grp_norm_b, v_w_out, v_ln_attn_g, v_ln_mem_g, v_w_q, v_w_kv, v_w_o, v_ln_ffn_g, v_w_gate_up, v_w_down, v_ln_final_g):
    order = ["ln_mix_g", "w_in", "sgu_ln_g", "sgu_ln_b", "w_spatial", "b_spatial", "conv_w", "grp_norm_a", "grp_norm_b",
             "w_out", "ln_attn_g", "ln_mem_g", "w_q", "w_kv", "w_o", "ln_ffn_g", "w_gate_up", "w_down", "ln_final_g"]
    W = dict(ln_mix_g=ln_mix_g, w_in=w_in, sgu_ln_g=sgu_ln_g, sgu_ln_b=sgu_ln_b, w_spatial=w_spatial, b_spatial=b_spatial,
             conv_w=conv_w, grp_norm_a=grp_norm_a, grp_norm_b=grp_norm_b, w_out=w_out, ln_attn_g=ln_attn_g,
             ln_mem_g=ln_mem_g, w_q=w_q, w_kv=w_kv, w_o=w_o, ln_ffn_g=ln_ffn_g, w_gate_up=w_gate_up, w_down=w_down,
             ln_final_g=ln_final_g)
    M = dict(ln_mix_g=m_ln_mix_g, w_in=m_w_in, sgu_ln_g=m_sgu_ln_g, sgu_ln_b=m_sgu_ln_b, w_spatial=m_w_spatial,
             b_spatial=m_b_spatial, conv_w=m_conv_w, grp_norm_a=m_grp_norm_a, grp_norm_b=m_grp_norm_b, w_out=m_w_out,
             ln_attn_g=m_ln_attn_g, ln_mem_g=m_ln_mem_g, w_q=m_w_q, w_kv=m_w_kv, w_o=m_w_o, ln_ffn_g=m_ln_ffn_g,
             w_gate_up=m_w_gate_up, w_down=m_w_down, ln_final_g=m_ln_final_g)
    V = dict(ln_mix_g=v_ln_mix_g, w_in=v_w_in, sgu_ln_g=v_sgu_ln_g, sgu_ln_b=v_sgu_ln_b, w_spatial=v_w_spatial,
             b_spatial=v_b_spatial, conv_w=v_conv_w, grp_norm_a=v_grp_norm_a, grp_norm_b=v_grp_norm_b, w_out=v_w_out,
             ln_attn_g=v_ln_attn_g, ln_mem_g=v_ln_mem_g, w_q=v_w_q, w_kv=v_w_kv, w_o=v_w_o, ln_ffn_g=v_ln_ffn_g,
             w_gate_up=v_w_gate_up, w_down=v_w_down, ln_final_g=v_ln_final_g)

    bw = conv_w.shape[1] * N_DEV
    pos = jnp.stack([lax.axis_index("x"), lax.axis_index("y"), lax.axis_index("c")]).astype(jnp.int32)
    me = 4 * pos[0] + 2 * pos[1] + pos[2]

    sp = {nm: (W[nm].reshape(1, -1) if W[nm].ndim == 1 else W[nm]) for nm in SMALL}
    view = lambda a, nm: a.T if nm in TRANSPOSED else a
    wb = {nm: view(W[nm], nm) for nm in BIG}
    wb["w_in"] = wb["w_in"].astype(BF16)
    grad_x, keep, pending, (g_in, c_in), token, small_all, seg = _step(
        x[0], mem[0], loss_target[0], wb, conv_w, sp, pos)

    out = {}

    def update(k, names, started, token):
        landed = _exchange_wait(started, token, "exchange_xy_%d_wait" % k)
        for nm, rxy in zip(names, landed):
            res = _adamw_shard(keep[nm], rxy, view(W[nm], nm), view(M[nm], nm), view(V[nm], nm), "adamw_" + nm)
            out[nm] = tuple(view(a, nm) for a in res)
            token = res[0]
        return token

    token = update(1, *pending[0], token)
    (g_in,), (rc_in,) = _exchange_wait(c_in, token, "exchange_c_4_wait", sources=True)
    (keep["w_in"], send_in), _ = _rs_combine(g_in, rc_in, pos, "rs_combine_w_in")
    xy_in, token = _exchange_start("xy", [send_in], "exchange_xy_3_start", collective=4)
    token = update(2, *pending[1], token)

    params = {nm: (_rows128(W[nm]), _rows128(M[nm]), _rows128(V[nm])) for nm in SMALL}
    per, conv_g_rows, loss_sum = _adamw_small(small_all, seg, params, seg["conv_w"])
    for nm in SMALL:
        out[nm] = tuple(a.reshape(W[nm].shape) for a in per[nm])
    conv_g = lax.dynamic_slice_in_dim(conv_g_rows.reshape(3, bw), me * conv_w.shape[1], conv_w.shape[1], axis=1)
    out["conv_w"] = (conv_g,) + tuple(_adamw_one(conv_w, conv_g, m_conv_w, v_conv_w, "adamw_conv"))

    update(3, ("w_in",), xy_in, token[:1, :1] + out["conv_w"][1][:1, :1])

    loss = loss_sum[0, 0]
    res = [loss, grad_x[None]]
    for k in range(4):
        res += [out[nm][k] for nm in order]
    return tuple(res)
```

```python
import functools

import jax
import jax.numpy as jnp
from jax import lax
from jax.experimental import pallas as pl
from jax.experimental.pallas import tpu as pltpu

F32 = jnp.float32
BF16 = jnp.bfloat16
SDS = jax.ShapeDtypeStruct
MESH = pl.DeviceIdType.MESH

EPS = 1e-6
N_DEV = 8
HEADS = 4
CHUNK = 128
HALO = 16
SUB = 8
LANES = 128
TOKEN_TILE = 512
ROW_CHUNK = 256
RELAY_AT = 0.7

ADAM_LR = 0.001
ADAM_B1 = 0.9
ADAM_B2 = 0.999
ADAM_EPS = 1e-08
ADAM_WD = 0.01
ADAM_STEP = 10

BIG = ("w_in", "w_out", "w_q", "w_kv", "w_o", "w_gate_up", "w_down")
TRANSPOSED = ("w_in", "w_gate_up")
SQUARE = ("w_out", "w_q", "w_o")
SMALL = ("ln_mix_g", "sgu_ln_g", "sgu_ln_b", "w_spatial", "b_spatial", "grp_norm_a", "grp_norm_b",
         "ln_attn_g", "ln_mem_g", "ln_ffn_g", "ln_final_g")


class _Exchange:
    def __init__(self, ins, out_shape, sems, start, finish, relay=None, peers=None, collective=None):
        self.ins, self.out_shape, self.sems = list(ins), list(out_shape), list(sems)
        self.start, self.finish, self.relay = start, finish, relay
        self.peers, self.collective = peers, collective


def _pcall(body, carry=(), n_prefetch=0, **kw):
    if carry:
        return functools.partial(_carrying_call, body, tuple(carry), n_prefetch, kw)
    if n_prefetch:
        kw["grid_spec"] = pltpu.PrefetchScalarGridSpec(
            num_scalar_prefetch=n_prefetch, grid=kw.pop("grid"), in_specs=kw.pop("in_specs"),
            out_specs=kw.pop("out_specs"), scratch_shapes=kw.pop("scratch_shapes", ()))
    return pl.pallas_call(body, **kw)


def _carrying_call(body, carry, n_prefetch, kw, *args):
    kw = dict(kw)
    out_shape = kw.pop("out_shape")
    single = not isinstance(out_shape, (tuple, list))
    outs_shape = (out_shape,) if single else tuple(out_shape)
    out_specs = kw.pop("out_specs")
    out_specs = [out_specs] if single else list(out_specs)
    in_specs = list(kw.pop("in_specs"))
    scratch = list(kw.pop("scratch_shapes", ()))
    grid = tuple(kw.get("grid", ()))
    n_in, n_out, n_scr = len(args), len(outs_shape), len(scratch)
    copying = [p for p in carry if p.sems]
    shaking = copying[0] if len(copying) == 1 and copying[0].collective is not None else None
    if shaking is not None:
        old = kw.get("compiler_params")
        kw["compiler_params"] = pltpu.CompilerParams(
            dimension_semantics=None if old is None else old.dimension_semantics, collective_id=shaking.collective)

    def split(refs, k, counts):
        parts = []
        for cnt in counts:
            parts.append(refs[k:k + cnt])
            k += cnt
        return parts, k

    def wrapped(*refs):
        cins, k = split(refs, n_in, [len(p.ins) for p in carry])
        outs = refs[k:k + n_out]
        couts, k = split(refs, k + n_out, [len(p.out_shape) for p in carry])
        scr = refs[k:k + n_scr]
        csems, _ = split(refs, k + n_scr, [len(p.sems) for p in carry])
        first, last = True, True
        for a, g in enumerate(grid):
            first = (pl.program_id(a) == 0) & first
            last = (pl.program_id(a) == g - 1) & last

        def start_all():
            if shaking is not None:
                peers = shaking.peers()
                barrier = pltpu.get_barrier_semaphore()
                for peer in peers:
                    pl.semaphore_signal(barrier, inc=1, device_id=peer, device_id_type=MESH)
                pl.semaphore_wait(barrier, len(peers))
            for p, ci, co, cs in zip(carry, cins, couts, csems):
                p.start(ci, co, cs)

        def relay_all():
            for p, ci, co, cs in zip(carry, cins, couts, csems):
                if p.relay is not None:
                    p.relay(ci, co, cs)

        def finish_all():
            for p, ci, co, cs in zip(carry, cins, couts, csems):
                p.finish(ci, co, cs)

        if len(grid) == 1:
            relay_now = pl.program_id(0) == min(int(RELAY_AT * grid[0]), grid[0] - 1)
        else:
            relay_now = last
        start_all() if not grid else pl.when(first)(start_all)
        relay_all() if not grid else pl.when(relay_now)(relay_all)
        body(*refs[:n_in], *outs, *scr)
        finish_all() if not grid else pl.when(last)(finish_all)

    c_in = [a for p in carry for a in p.ins]
    c_out = [s for p in carry for s in p.out_shape]
    c_sems = [s for p in carry for s in p.sems]
    res = _pcall(wrapped, n_prefetch=n_prefetch, out_shape=outs_shape + tuple(c_out),
                 in_specs=in_specs + _hbm_specs(len(c_in)), out_specs=out_specs + _hbm_specs(len(c_out)),
                 scratch_shapes=scratch + c_sems, **kw)(*args, *c_in)
    own = res[0] if single else tuple(res[:n_out])
    landed, k = [], n_out
    for p in carry:
        landed.append(list(res[k:k + len(p.out_shape)]))
        k += len(p.out_shape)
    return own, landed


def _hbm_specs(n):
    return [pl.BlockSpec(memory_space=pl.ANY)] * n


def _hosted(body, carry, **kw):
    if carry:
        return _pcall(body, carry=carry, **kw)
    call = _pcall(body, **kw)
    return lambda *args: (call(*args), [])


def _run_exchanges(parts, name):
    def body(*refs):
        pass

    _, landed = _pcall(body, carry=parts, out_shape=(), in_specs=[], out_specs=[], name=name)()
    return landed


def _arb(n):
    return pltpu.CompilerParams(dimension_semantics=("arbitrary",) * n)


def _tile(n, target, mult):
    best = None
    for t in range(mult, min(n, target) + 1, mult):
        if n % t == 0:
            best = t
    return n if best is None else best


def _round_up(n, m):
    return (n + m - 1) // m * m


def _dot(a, b):
    return jnp.dot(a, b, preferred_element_type=F32)


def _dot_nt(a, b):
    return lax.dot_general(a, b, (((1,), (1,)), ((), ())), preferred_element_type=F32)


def _dot_tn(a, b):
    return lax.dot_general(a, b, (((0,), (0,)), ((), ())), preferred_element_type=F32)


def _rstd(x):
    return lax.rsqrt(jnp.mean(x * x, axis=-1, keepdims=True) + EPS)


def _rms_bwd(dy, x, r, g):
    gdy = dy * g
    proj = jnp.sum(gdy * x, axis=-1, keepdims=True) * (1.0 / x.shape[-1])
    dx = r * gdy - x * (r * r * r) * proj
    dg = jnp.sum(dy * (x * r), axis=0, keepdims=True)
    return dx, dg


_GELU_C = 0.7978845608028654
_GELU_A = 0.044715


def _gelu(x):
    t = jnp.tanh(_GELU_C * (x + _GELU_A * x * x * x))
    return 0.5 * x * (1.0 + t), t


def _gelu_grad(x, t):
    return 0.5 * (1.0 + t) + 0.5 * x * (1.0 - t * t) * (_GELU_C * (1.0 + 3.0 * _GELU_A * x * x))


def _sigmoid(x):
    return 1.0 / (1.0 + jnp.exp(-x))


def _softmax(s):
    m = jnp.max(s, axis=-1, keepdims=True)
    e = jnp.exp(s - m)
    return e / jnp.sum(e, axis=-1, keepdims=True)


def _adamw(w, g, m, v):
    m = ADAM_B1 * m + (1.0 - ADAM_B1) * g
    v = ADAM_B2 * v + (1.0 - ADAM_B2) * (g * g)
    m_hat = m / (1.0 - ADAM_B1 ** ADAM_STEP)
    v_hat = v / (1.0 - ADAM_B2 ** ADAM_STEP)
    delta = -ADAM_LR * (m_hat / (jnp.sqrt(v_hat) + ADAM_EPS) + ADAM_WD * w)
    return delta, m, v


def _tril_mask():
    t = lax.broadcasted_iota(jnp.int32, (CHUNK, CHUNK), 0)
    s = lax.broadcasted_iota(jnp.int32, (CHUNK, CHUNK), 1)
    return (s <= t).astype(F32)


def _sgu_forward(ha, lng, lnb, wm, bt, mixed_s):
    aw = ha.shape[1] // 2
    hd = aw // HEADS
    a, th = _gelu(ha)
    u = a[:, :aw]
    v = a[:, aw:]
    mu = jnp.mean(v, axis=-1, keepdims=True)
    vc = v - mu
    rl = lax.rsqrt(jnp.mean(vc * vc, axis=-1, keepdims=True) + EPS)
    xhat = vc * rl
    vln = (xhat * lng + lnb).astype(BF16)
    for n in range(ha.shape[0] // CHUNK):
        rows = slice(n * CHUNK, (n + 1) * CHUNK)
        for h in range(HEADS):
            cols = slice(h * hd, (h + 1) * hd)
            mixed_s[rows, cols] = _dot(wm[h], vln[rows, cols]) + bt[:, h:h + 1]
    return th, u, xhat, rl, vln


def _conv_taps(zext):
    return pltpu.roll(zext, 2, 0), pltpu.roll(zext, 1, 0)


def _kv_forward(mem, g_mem, w_kv):
    ml, d = mem.shape
    xd = w_kv.shape[2]

    def body(mem_ref, g_ref, w_ref, memn_ref, kv_ref):
        x = mem_ref[...]
        memn = (x * _rstd(x) * g_ref[...]).astype(BF16)
        memn_ref[...] = memn
        for j in range(2 * HEADS):
            kv_ref[j] = _dot(memn, w_ref[j]).astype(BF16)

    return _pcall(body, out_shape=(SDS((ml, d), BF16), SDS((2 * HEADS, ml, xd), BF16)), name="kv_forward")(mem, g_mem, w_kv)


def _in_forward(x, g, w_in_t, tm, carry=()):
    s, d = x.shape
    n_in = w_in_t.shape[0]

    def body(x_ref, g_ref, w_ref, xn_ref, h_ref):
        xv = x_ref[...]
        xn = (xv * _rstd(xv) * g_ref[...]).astype(BF16)
        xn_ref[...] = xn
        h_ref[...] = _dot_nt(xn, w_ref[...])

    return _hosted(
        body, carry, grid=(s // tm,),
        in_specs=[pl.BlockSpec((tm, d), lambda i: (i, 0)), pl.BlockSpec((1, d), lambda i: (0, 0)),
                  pl.BlockSpec((n_in, d), lambda i: (0, 0))],
        out_specs=[pl.BlockSpec((tm, d), lambda i: (i, 0)), pl.BlockSpec((tm, n_in), lambda i: (i, 0))],
        out_shape=(SDS((s, d), BF16), SDS((s, n_in), F32)),
        compiler_params=_arb(1), name="in_forward")(x, g, w_in_t)


def _mix_forward(h, x, lng, lnb, w_sp, bt, conv_w, ga, gb, w_out, tm, carry=()):
    s, d = x.shape
    n_in = h.shape[1]
    aw = lng.shape[1]
    bw = d - aw
    in_a = 2 * aw
    hb_blocks = tm // HALO

    def body(h_ref, hprev_ref, x_ref, lng_ref, lnb_ref, wsp_ref, bt_ref, cw_ref, ga_ref, gb_ref, wout_ref,
             ycat_ref, x1_ref, mixed_s):
        i = pl.program_id(0)
        mask = _tril_mask()
        wm = [(wsp_ref[hh] * mask).astype(BF16) for hh in range(HEADS)]
        hv = h_ref[...]
        _, u, _, _, _ = _sgu_forward(hv[:, :in_a], lng_ref[...], lnb_ref[...], wm, bt_ref[...], mixed_s)
        sg = u * mixed_s[...]
        ycat_ref[:, :aw] = (sg * _rstd(sg) * ga_ref[...]).astype(BF16)

        gate_b = hv[:, in_a:in_a + bw]
        z = hv[:, in_a + bw:in_a + 2 * bw] * hv[:, in_a + 2 * bw:]
        hp = hprev_ref[...]
        zp = hp[:, in_a + bw:in_a + 2 * bw] * hp[:, in_a + 2 * bw:]
        zp = jnp.where(i == 0, 0.0, zp)
        zext = jnp.concatenate([zp, z], axis=0)
        z2, z1 = _conv_taps(zext)
        cw = cw_ref[...]
        conv = cw[0:1] * z2[HALO:] + cw[1:2] * z1[HALO:] + cw[2:3] * z
        sc = gate_b * conv
        ycat_ref[:, aw:] = (sc * _rstd(sc) * gb_ref[...]).astype(BF16)
        x1_ref[...] = x_ref[...] + _dot(ycat_ref[...], wout_ref[...])

    full = lambda shape: pl.BlockSpec(shape, lambda i: (0,) * len(shape))
    return _hosted(
        body, carry, grid=(s // tm,),
        in_specs=[pl.BlockSpec((tm, n_in), lambda i: (i, 0)),
                  pl.BlockSpec((HALO, n_in), lambda i: (jnp.maximum(i * hb_blocks - 1, 0), 0)),
                  pl.BlockSpec((tm, d), lambda i: (i, 0)),
                  full((1, aw)), full((1, aw)), full((HEADS, CHUNK, CHUNK)), full((CHUNK, HEADS)),
                  full((3, bw)), full((1, aw)), full((1, bw)), full((d, d))],
        out_specs=[pl.BlockSpec((tm, d), lambda i: (i, 0)), pl.BlockSpec((tm, d), lambda i: (i, 0))],
        out_shape=(SDS((s, d), BF16), SDS((s, d), F32)),
        scratch_shapes=[pltpu.VMEM((tm, aw), F32)],
        compiler_params=_arb(1), name="mix_forward")(h, h, x, lng, lnb, w_sp, bt, conv_w, ga, gb, w_out)


def _attn_forward(x1, g, w_q, kv, w_o, tm, carry=()):
    s, d = x1.shape
    _, ml, xd = kv.shape
    scale = xd ** -0.5

    def body(x1_ref, g_ref, wq_ref, kv_ref, wo_ref, xn_ref, q_ref, p_ref, o_ref, x2_ref):
        xv = x1_ref[...]
        xn = (xv * _rstd(xv) * g_ref[...]).astype(BF16)
        xn_ref[...] = xn
        q_ref[...] = _dot(xn, wq_ref[...]).astype(BF16)
        for hh in range(HEADS):
            cols = slice(hh * xd, (hh + 1) * xd)
            p = _softmax(_dot_nt(q_ref[:, cols], kv_ref[hh]) * scale).astype(BF16)
            p_ref[:, hh * ml:(hh + 1) * ml] = p
            o_ref[:, cols] = _dot(p, kv_ref[HEADS + hh]).astype(BF16)
        x2_ref[...] = xv + _dot(o_ref[...], wo_ref[...])

    tok = pl.BlockSpec((tm, d), lambda i: (i, 0))
    probs = pl.BlockSpec((tm, HEADS * ml), lambda i: (i, 0))
    return _hosted(
        body, carry, grid=(s // tm,),
        in_specs=[tok, pl.BlockSpec((1, d), lambda i: (0, 0)), pl.BlockSpec((d, d), lambda i: (0, 0)),
                  pl.BlockSpec((2 * HEADS, ml, xd), lambda i: (0, 0, 0)), pl.BlockSpec((d, d), lambda i: (0, 0))],
        out_specs=[tok, tok, probs, tok, tok],
        out_shape=(SDS((s, d), BF16), SDS((s, d), BF16), SDS((s, HEADS * ml), BF16), SDS((s, d), BF16), SDS((s, d), F32)),
        compiler_params=_arb(1), name="attn_forward")(x1, g, w_q, kv, w_o)


def _ffn_forward(x2, g, w_gu, w_down, tm):
    s, d = x2.shape
    _, nf, tf, _ = w_gu.shape

    def body(x2_ref, g_ref, wgu_ref, wd_ref, xn_ref, gu_ref, x3_ref):
        f = pl.program_id(1)

        @pl.when(f == 0)
        def _():
            xv = x2_ref[...]
            xn_ref[...] = (xv * _rstd(xv) * g_ref[...]).astype(BF16)
            x3_ref[...] = xv

        xn = xn_ref[...]
        gate = _dot_nt(xn, wgu_ref[0])
        up = _dot_nt(xn, wgu_ref[1])
        gu_ref[0] = gate.astype(BF16)
        gu_ref[1] = up.astype(BF16)
        act = (gate * _sigmoid(gate) * up).astype(BF16)
        x3_ref[...] += _dot(act, wd_ref[...])

    tok = pl.BlockSpec((tm, d), lambda i, f: (i, 0))
    return _pcall(
        body, grid=(s // tm, nf),
        in_specs=[tok, pl.BlockSpec((1, d), lambda i, f: (0, 0)),
                  pl.BlockSpec((2, None, tf, d), lambda i, f: (0, f, 0, 0)),
                  pl.BlockSpec((tf, d), lambda i, f: (f, 0))],
        out_specs=[tok, pl.BlockSpec((2, None, tm, tf), lambda i, f: (0, f, i, 0)), tok],
        out_shape=(SDS((s, d), BF16), SDS((2, nf, s, tf), BF16), SDS((s, d), F32)),
        compiler_params=_arb(2), name="ffn_forward")(x2, g, w_gu, w_down)


def _final_backward(x3, target, g_final, tm):
    s, d = x3.shape

    def body(x3_ref, tgt_ref, gf_ref, loss_ref, dgf_ref, dx3_ref, dx3b_ref):
        @pl.when(pl.program_id(0) == 0)
        def _():
            loss_ref[...] = jnp.zeros_like(loss_ref)
            dgf_ref[...] = jnp.zeros_like(dgf_ref)

        xv = x3_ref[...]
        r = _rstd(xv)
        diff = xv * r * gf_ref[...] - tgt_ref[...]
        loss_ref[...] += 0.5 * jnp.sum(jnp.sum(diff * diff, axis=-1, keepdims=True), axis=0, keepdims=True) * (1.0 / d)
        dx3, dgf = _rms_bwd(diff * (1.0 / d), xv, r, gf_ref[...])
        dgf_ref[...] += dgf
        dx3_ref[...] = dx3
        dx3b_ref[...] = dx3.astype(BF16)

    tok = pl.BlockSpec((tm, d), lambda i: (i, 0))
    vec = pl.BlockSpec((1, d), lambda i: (0, 0))
    return _pcall(
        body, grid=(s // tm,), in_specs=[tok, tok, vec],
        out_specs=[pl.BlockSpec((SUB, LANES), lambda i: (0, 0)), vec, tok, tok],
        out_shape=(SDS((SUB, LANES), F32), SDS((1, d), F32), SDS((s, d), F32), SDS((s, d), BF16)),
        compiler_params=_arb(1), name="final_backward")(x3, target, g_final)


def _swiglu_backward(dx3b, gu, w_gu, w_down, tm):
    s, d = dx3b.shape
    _, nf, tf, _ = w_gu.shape

    def body(dx3b_ref, gu_ref, wgu_ref, wd_ref, act_ref, dgu_ref, dxn_ref):
        @pl.when(pl.program_id(1) == 0)
        def _():
            dxn_ref[...] = jnp.zeros_like(dxn_ref)

        for r0 in range(0, tm, ROW_CHUNK):
            rows = slice(r0, r0 + ROW_CHUNK)
            dact = _dot_nt(dx3b_ref[rows, :], wd_ref[...])
            gv = gu_ref[0, rows, :].astype(F32)
            uv = gu_ref[1, rows, :].astype(F32)
            sg = _sigmoid(gv)
            silu = gv * sg
            act_ref[rows, :] = (silu * uv).astype(BF16)
            dgate = (dact * uv * (sg * (1.0 + gv * (1.0 - sg)))).astype(BF16)
            dup = (dact * silu).astype(BF16)
            dgu_ref[0, rows, :] = dgate
            dgu_ref[1, rows, :] = dup
            part = _dot(dgate, wgu_ref[0]) + _dot(dup, wgu_ref[1])
            dxn_ref[rows, :] += part

    tok = pl.BlockSpec((tm, d), lambda i, f: (i, 0))
    pair = pl.BlockSpec((2, None, tm, tf), lambda i, f: (0, f, i, 0))
    return _pcall(
        body, grid=(s // tm, nf),
        in_specs=[tok, pair, pl.BlockSpec((2, None, tf, d), lambda i, f: (0, f, 0, 0)),
                  pl.BlockSpec((tf, d), lambda i, f: (f, 0))],
        out_specs=[pl.BlockSpec((None, tm, tf), lambda i, f: (f, i, 0)), pair, tok],
        out_shape=(SDS((nf, s, tf), BF16), SDS((2, nf, s, tf), BF16), SDS((s, d), F32)),
        compiler_params=_arb(2), name="swiglu_backward")(dx3b, gu, w_gu, w_down)


def _attn_backward(dx3, dxn3, x2, g_ffn, x1, g, q, probs, kv, w_q, w_o, tm, carry=()):
    s, d = x1.shape
    _, ml, xd = kv.shape
    scale = xd ** -0.5

    def body(dx3_ref, dxn3_ref, x2_ref, g2_ref, x1_ref, g_ref, q_ref, p_ref, kv_ref, wq_ref, wo_ref,
             dx2b_ref, dq_ref, dx1_ref, dx1b_ref, dkv_ref, dg_ref, dg2_ref, do_s):
        i = pl.program_id(0)

        @pl.when(i == 0)
        def _():
            dkv_ref[...] = jnp.zeros_like(dkv_ref)
            dg_ref[...] = jnp.zeros_like(dg_ref)
            dg2_ref[...] = jnp.zeros_like(dg2_ref)

        x2v = x2_ref[...]
        dx2n, dg2 = _rms_bwd(dxn3_ref[...], x2v, _rstd(x2v), g2_ref[...])
        dg2_ref[...] += dg2
        dx2 = dx3_ref[...] + dx2n
        dx2b_ref[...] = dx2.astype(BF16)
        do_s[...] = _dot_nt(dx2b_ref[...], wo_ref[...]).astype(BF16)
        for hh in range(HEADS):
            kc = slice(hh * xd, (hh + 1) * xd)
            qh = q_ref[:, kc]
            kh = kv_ref[hh]
            doh = do_s[:, kc]
            pb = p_ref[:, hh * ml:(hh + 1) * ml]
            p = pb.astype(F32)
            dp = _dot_nt(doh, kv_ref[HEADS + hh])
            dkv_ref[HEADS + hh] += _dot_tn(pb, doh)
            ds = (p * (dp - jnp.sum(dp * p, axis=-1, keepdims=True)) * scale).astype(BF16)
            dq_ref[:, kc] = _dot(ds, kh).astype(BF16)
            dkv_ref[hh] += _dot_tn(ds, qh)
        dxn = _dot_nt(dq_ref[...], wq_ref[...])
        xv = x1_ref[...]
        dx, dg = _rms_bwd(dxn, xv, _rstd(xv), g_ref[...])
        dg_ref[...] += dg
        dx1 = dx2 + dx
        dx1_ref[...] = dx1
        dx1b_ref[...] = dx1.astype(BF16)

    tok = pl.BlockSpec((tm, d), lambda i: (i, 0))
    vec = pl.BlockSpec((1, d), lambda i: (0, 0))
    sq = pl.BlockSpec((d, d), lambda i: (0, 0))
    kvs = pl.BlockSpec((2 * HEADS, ml, xd), lambda i: (0, 0, 0))
    return _hosted(
        body, carry, grid=(s // tm,),
        in_specs=[tok, tok, tok, vec, tok, vec, tok, pl.BlockSpec((tm, HEADS * ml), lambda i: (i, 0)), kvs, sq, sq],
        out_specs=[tok, tok, tok, tok, kvs, vec, vec],
        out_shape=(SDS((s, d), BF16), SDS((s, d), BF16), SDS((s, d), F32), SDS((s, d), BF16),
                   SDS((2 * HEADS, ml, xd), F32), SDS((1, d), F32), SDS((1, d), F32)),
        scratch_shapes=[pltpu.VMEM((tm, d), BF16)],
        compiler_params=_arb(1), name="attn_backward")(dx3, dxn3, x2, g_ffn, x1, g, q, probs, kv, w_q, w_o)


def _kv_backward(dkv, memn, mem, g_mem, w_kv):
    ml, d = mem.shape
    xd = w_kv.shape[2]

    def body(dkv_ref, memn_ref, mem_ref, g_ref, w_ref, dw_ref, dg_ref):
        dmemn = jnp.zeros((ml, d), F32)
        for j in range(2 * HEADS):
            dkvb = dkv_ref[j].astype(BF16)
            dw_ref[j] = _dot_tn(memn_ref[...], dkvb)
            dmemn = dmemn + _dot_nt(dkvb, w_ref[j])
        x = mem_ref[...]
        dg_ref[...] = jnp.sum(dmemn * (x * _rstd(x)), axis=0, keepdims=True)

    return _pcall(body, out_shape=(SDS((2 * HEADS, d, xd), F32), SDS((1, d), F32)), name="kv_backward")(dkv, memn, mem, g_mem, w_kv)


def _mix_backward(dx1, x, g_mix, h, lng, lnb, w_sp, bt, conv_w, ga, gb, w_out, w_in, tm, carry=()):
    s, d = x.shape
    n_in = h.shape[1]
    aw = lng.shape[1]
    bw = d - aw
    hd = aw // HEADS
    in_a = 2 * aw
    hb_blocks = tm // HALO
    last_blk = s // HALO - 1
    nt = s // tm
    tc = tm
    te = tc + HALO
    tee = tc + 2 * HALO

    def body(dx1_ref, dx1n_ref, x_ref, gm_ref, h_ref, hp_ref, hn_ref, lng_ref, lnb_ref, wsp_ref, bt_ref, cw_ref,
             ga_ref, gb_ref, wout_ref, win_ref,
             dh_ref, dx_ref, dga_ref, dgb_ref, dcw_ref, dlng_ref, dlnb_ref, dwsp_ref, dbs_ref, dgm_ref,
             mixed_s, dvln_s):
        i = pl.program_id(0)

        @pl.when(i == 0)
        def _():
            for ref in (dga_ref, dgb_ref, dcw_ref, dlng_ref, dlnb_ref, dwsp_ref, dbs_ref, dgm_ref):
                ref[...] = jnp.zeros_like(ref)

        mask = _tril_mask()
        wm = [(wsp_ref[hh] * mask).astype(BF16) for hh in range(HEADS)]
        cw = cw_ref[...]

        def chain(r0):
            rows = slice(r0, r0 + tc)
            first, last = r0 == 0, r0 + tc == tm
            hv = h_ref[rows, :]
            dx1 = dx1_ref[rows, :]
            dx1n = dx1n_ref[...] if last else dx1_ref[r0 + tc:r0 + tc + HALO, :]
            hp = hp_ref[:, in_a:] if first else h_ref[r0 - HALO:r0, in_a:]
            hn = hn_ref[:, in_a:] if last else h_ref[r0 + tc:r0 + tc + HALO, in_a:]
            dx1e = jnp.concatenate([dx1, dx1n], axis=0).astype(BF16)
            dycat = _dot_nt(dx1e, wout_ref[...])

            hbe = jnp.concatenate([hp, hv[:, in_a:], hn], axis=0)
            row = lax.broadcasted_iota(jnp.int32, (tee, 1), 0)
            zext = hbe[:, bw:2 * bw] * hbe[:, 2 * bw:]
            if first:
                zext = jnp.where((i == 0) & (row < HALO), 0.0, zext)
            z2e, z1e = _conv_taps(zext)
            conv_e = (cw[0:1] * z2e + cw[1:2] * z1e + cw[2:3] * zext)[HALO:]
            gate_b_e = hbe[HALO:, :bw]
            sc_e = gate_b_e * conv_e
            rb = _rstd(sc_e)
            dyb = dycat[:, aw:]
            gdy = dyb * gb_ref[...]
            dsc_e = rb * gdy - sc_e * (rb * rb * rb) * (jnp.sum(gdy * sc_e, axis=-1, keepdims=True) * (1.0 / bw))
            dgb_ref[...] += jnp.sum((dyb * (sc_e * rb))[:tc], axis=0, keepdims=True)
            dconv_e = dsc_e * gate_b_e
            if last:
                dconv_e = jnp.where((i == nt - 1) & (row[:te] >= tc), 0.0, dconv_e)
            dconv = dconv_e[:tc]
            dc1 = pltpu.roll(dconv_e, te - 1, 0)[:tc]
            dc2 = pltpu.roll(dconv_e, te - 2, 0)[:tc]
            dz = cw[2:3] * dconv + cw[1:2] * dc1 + cw[0:1] * dc2
            z = zext[HALO:HALO + tc]
            z1 = z1e[HALO:HALO + tc]
            z2 = z2e[HALO:HALO + tc]
            dcw_ref[0:1, :] += jnp.sum(dconv * z2, axis=0, keepdims=True)
            dcw_ref[1:2, :] += jnp.sum(dconv * z1, axis=0, keepdims=True)
            dcw_ref[2:3, :] += jnp.sum(dconv * z, axis=0, keepdims=True)
            dh_ref[rows, in_a:in_a + bw] = (dsc_e[:tc] * conv_e[:tc]).astype(BF16)
            dh_ref[rows, in_a + bw:in_a + 2 * bw] = (dz * hv[:, in_a + 2 * bw:]).astype(BF16)
            dh_ref[rows, in_a + 2 * bw:] = (dz * hv[:, in_a + bw:in_a + 2 * bw]).astype(BF16)

            ha = hv[:, :in_a]
            mixed_c, dvln_c = mixed_s.at[rows, :], dvln_s.at[rows, :]
            th, u, xhat, rl, vln = _sgu_forward(ha, lng_ref[...], lnb_ref[...], wm, bt_ref[...], mixed_c)
            mixed = mixed_c[...]
            sg = u * mixed
            dsg, dga = _rms_bwd(dycat[:tc, :aw], sg, _rstd(sg), ga_ref[...])
            dga_ref[...] += dga
            du = dsg * mixed
            dmixed = dsg * u
            dmb = dmixed.astype(BF16)
            for n in range(tc // CHUNK):
                blk = slice(n * CHUNK, (n + 1) * CHUNK)
                dbs_ref[...] += dmixed[blk]
                for hh in range(HEADS):
                    cols = slice(hh * hd, (hh + 1) * hd)
                    dvln_c[blk, cols] = _dot_tn(wm[hh], dmb[blk, cols])
                    dwsp_ref[hh] += mask * _dot_nt(dmb[blk, cols], vln[blk, cols])
            dvln = dvln_c[...]
            dlng_ref[...] += jnp.sum(dvln * xhat, axis=0, keepdims=True)
            dlnb_ref[...] += jnp.sum(dvln, axis=0, keepdims=True)
            dxh = dvln * lng_ref[...]
            dv = rl * (dxh - jnp.mean(dxh, axis=-1, keepdims=True) - xhat * jnp.mean(dxh * xhat, axis=-1, keepdims=True))
            dh_ref[rows, :in_a] = (jnp.concatenate([du, dv], axis=-1) * _gelu_grad(ha, th)).astype(BF16)

            dxn = _dot(dh_ref[rows, :], win_ref[...])
            xv = x_ref[rows, :]
            dx, dgm = _rms_bwd(dxn, xv, _rstd(xv), gm_ref[...])
            dgm_ref[...] += dgm
            dx_ref[rows, :] = dx1 + dx

        for r0 in range(0, tm, tc):
            chain(r0)

    full = lambda shape: pl.BlockSpec(shape, lambda i: (0,) * len(shape))
    tok = pl.BlockSpec((tm, d), lambda i: (i, 0))
    nxt = lambda i: (jnp.minimum((i + 1) * hb_blocks, last_blk), 0)
    prv = lambda i: (jnp.maximum(i * hb_blocks - 1, 0), 0)
    return _hosted(
        body, carry, grid=(nt,),
        in_specs=[tok, pl.BlockSpec((HALO, d), nxt), tok, full((1, d)),
                  pl.BlockSpec((tm, n_in), lambda i: (i, 0)), pl.BlockSpec((HALO, n_in), prv), pl.BlockSpec((HALO, n_in), nxt),
                  full((1, aw)), full((1, aw)), full((HEADS, CHUNK, CHUNK)), full((CHUNK, HEADS)), full((3, bw)),
                  full((1, aw)), full((1, bw)), full((d, d)), full((n_in, d))],
        out_specs=[pl.BlockSpec((tm, n_in), lambda i: (i, 0)), tok,
                   full((1, aw)), full((1, bw)), full((SUB, bw)), full((1, aw)), full((1, aw)),
                   full((HEADS, CHUNK, CHUNK)), full((CHUNK, aw)), full((1, d))],
        out_shape=(SDS((s, n_in), BF16), SDS((s, d), F32),
                   SDS((1, aw), F32), SDS((1, bw), F32), SDS((SUB, bw), F32), SDS((1, aw), F32), SDS((1, aw), F32),
                   SDS((HEADS, CHUNK, CHUNK), F32), SDS((CHUNK, aw), F32), SDS((1, d), F32)),
        scratch_shapes=[pltpu.VMEM((tm, aw), F32), pltpu.VMEM((tm, aw), F32)],
        compiler_params=_arb(1), name="mix_backward")(dx1, dx1, x, g_mix, h, h, h, lng, lnb, w_sp, bt, conv_w, ga, gb, w_out, w_in)


def _bias_grad(dbs):
    aw = dbs.shape[1]
    hd = aw // HEADS

    def body(dbs_ref, out_ref):
        ones = jnp.ones((SUB, hd), F32)
        for hh in range(HEADS):
            r = lax.dot_general(ones, dbs_ref[:, hh * hd:(hh + 1) * hd], (((1,), (1,)), ((), ())),
                                precision=lax.Precision.HIGHEST, preferred_element_type=F32)
            out_ref[hh:hh + 1, :] = r[0:1]

    return _pcall(body, out_shape=SDS((HEADS, CHUNK), F32), name="bias_grad")(dbs)


def _wgrad_body(a_ref, b_ref, o_ref):
    o_ref[...] = _dot_tn(a_ref[...], b_ref[...])


def _wgrad(a, b, name, carry=()):
    k, m = a.shape
    n = b.shape[1]
    tm = _tile(m, 512, LANES)
    tn = _tile(n, 1024, LANES)
    return _hosted(
        functools.partial(_wgrad_body), carry, grid=(m // tm, n // tn),
        in_specs=[pl.BlockSpec((k, tm), lambda i, j: (0, i)), pl.BlockSpec((k, tn), lambda i, j: (0, j))],
        out_specs=pl.BlockSpec((tm, tn), lambda i, j: (i, j)),
        out_shape=SDS((m, n), F32), compiler_params=_arb(2), name=name)(a, b)


def _wgrad_blocked_lhs(a, b, name, carry=()):
    nb, k, t = a.shape
    n = b.shape[1]
    tn = _tile(n, 1024, LANES)
    return _hosted(
        functools.partial(_wgrad_body), carry, grid=(nb, n // tn),
        in_specs=[pl.BlockSpec((None, k, t), lambda i, j: (i, 0, 0)), pl.BlockSpec((k, tn), lambda i, j: (0, j))],
        out_specs=pl.BlockSpec((t, tn), lambda i, j: (i, j)),
        out_shape=SDS((nb * t, n), F32), compiler_params=_arb(2), name=name)(a, b)


def _place():
    x, y, c = lax.axis_index("x"), lax.axis_index("y"), lax.axis_index("c")
    return x, y, c, [(1 - x, y), (x, 1 - y), (1 - x, 1 - y)]


def _all_gather(shards):
    n = len(shards)
    slots = 9
    cut = [(s.shape[0] // 32) * 16 for s in shards]

    def build(ins, outs, sems):
        send_sems, recv_sems, local_sems = sems
        x, y, c, _ = _place()
        me, sib, xn, yn, dg = (x, y, c), (x, y, 1 - c), (1 - x, y, c), (x, 1 - y, c), (1 - x, 1 - y, c)
        other = lambda p: (p[0], p[1], 1 - p[2])

        def rows(a, p, part=None):
            ref = outs[a].at[4 * p[0] + 2 * p[1] + p[2]]
            if part is None or cut[a] == 0:
                return ref if part in (None, 0) else None
            return ref.at[pl.ds(0, cut[a])] if part == 0 else ref.at[pl.ds(cut[a], shards[a].shape[0] - cut[a])]

        def copy(a, k, ref, to, src=None):
            if ref is None:
                return None
            return pltpu.make_async_remote_copy(
                src_ref=ref if src is None else src, dst_ref=ref, send_sem=send_sems.at[slots * a + k],
                recv_sem=recv_sems.at[slots * a + k], device_id=to, device_id_type=MESH)

        def real(cps):
            return [cp for cp in cps if cp is not None]

        class Copies:
            own = lambda a: [copy(a, 1, rows(a, me), xn, ins[a]), copy(a, 2, rows(a, me), yn, ins[a]),
                             copy(a, 0, rows(a, me), sib, ins[a])]
            local = lambda a: pltpu.make_async_copy(ins[a], rows(a, me), local_sems.at[a])
            from_x = lambda a: copy(a, 1, rows(a, xn), me)
            from_y = lambda a: copy(a, 2, rows(a, yn), me)
            after_x = lambda a: real([copy(a, 4, rows(a, xn, 1), yn), copy(a, 5, rows(a, xn), sib)])
            after_y = lambda a: real([copy(a, 3, rows(a, yn, 0), xn), copy(a, 6, rows(a, yn), sib)])
            diag_in = lambda a: real([copy(a, 3, rows(a, dg, 0), me), copy(a, 4, rows(a, dg, 1), me)])
            diag_on = lambda a: real([copy(a, 7, rows(a, dg, 0), sib), copy(a, 8, rows(a, dg, 1), sib)])
            from_sib = lambda a: real([copy(a, 0, rows(a, sib), me), copy(a, 5, rows(a, other(xn)), me),
                                       copy(a, 6, rows(a, other(yn)), me), copy(a, 7, rows(a, other(dg), 0), me),
                                       copy(a, 8, rows(a, other(dg), 1), me)])

        return Copies

    def start(ins, outs, sems):
        cps = build(ins, outs, sems)
        for a in range(n):
            for cp in cps.own(a):
                cp.start()
        for a in range(n):
            cps.local(a).start()

    def relay(ins, outs, sems):
        cps = build(ins, outs, sems)
        for a in range(n):
            cps.from_x(a).wait_recv()
            for cp in cps.after_x(a):
                cp.start()
            cps.from_y(a).wait_recv()
            for cp in cps.after_y(a):
                cp.start()

    def finish(ins, outs, sems):
        cps = build(ins, outs, sems)
        for a in range(n):
            for arrived, onward in zip(cps.diag_in(a), cps.diag_on(a)):
                arrived.wait_recv()
                onward.start()
        for a in range(n):
            for cp in cps.from_sib(a):
                cp.wait_recv()
            for cp in cps.own(a) + cps.after_x(a) + cps.after_y(a) + cps.diag_on(a):
                cp.wait_send()
            cps.local(a).wait()

    def peers():
        x, y, c, _ = _place()
        return [(x, y, 1 - c), (1 - x, y, c), (x, 1 - y, c)]

    return _Exchange(shards, [SDS((N_DEV,) + s.shape, s.dtype) for s in shards],
                     [pltpu.SemaphoreType.DMA((slots * n,)), pltpu.SemaphoreType.DMA((slots * n,)),
                      pltpu.SemaphoreType.DMA((n,))], start, finish, relay, peers, GATHER_COLLECTIVE)


def _swap_exchange(ins, out_shape, per, copies):
    def start(i, o, sems):
        for cp in copies(i, o, sems):
            cp.start()

    def finish(i, o, sems):
        for cp in copies(i, o, sems):
            cp.wait()

    def sibling():
        x, y, c, _ = _place()
        return [(x, y, 1 - c)]

    n = per * len(ins)
    return _Exchange(ins, out_shape, [pltpu.SemaphoreType.DMA((n,)), pltpu.SemaphoreType.DMA((n,))], start, finish,
                     peers=sibling, collective=SIBLING_COLLECTIVE)


def _exchange_c(gs):
    def copies(ins, outs, sems):
        x, y, c, _ = _place()
        return [pltpu.make_async_remote_copy(
                    src_ref=ins[a].at[2 * k + 1 - c], dst_ref=outs[a].at[k],
                    send_sem=sems[0].at[4 * a + k], recv_sem=sems[1].at[4 * a + k],
                    device_id=(x, y, 1 - c), device_id_type=MESH)
                for a in range(len(gs)) for k in range(4)]

    return _swap_exchange(gs, [SDS((4,) + g.shape[1:], g.dtype) for g in gs], 4, copies)


def _rs_combine(g, recv, pos, name, carry=()):
    _, r, cdim = g.shape
    tr = _tile(r, 256, 16)

    def body(pos_ref, g0, r0, g1, r1, g2, r2, g3, r3, keep_ref, send_ref):
        keep_ref[...] = g0[...] + r0[...]
        send_ref[0] = (g1[...] + r1[...]).astype(BF16)
        send_ref[1] = (g2[...] + r2[...]).astype(BF16)
        send_ref[2] = (g3[...] + r3[...]).astype(BF16)

    def k_of(p, t):
        px = p[0] if t in (0, 2) else 1 - p[0]
        py = p[1] if t in (0, 1) else 1 - p[1]
        return 2 * px + py

    blk = (None, tr, cdim)
    in_specs = []
    for t in range(4):
        in_specs.append(pl.BlockSpec(blk, functools.partial(lambda j, p, t: (2 * k_of(p, t) + p[2], j, 0), t=t)))
        in_specs.append(pl.BlockSpec(blk, functools.partial(lambda j, p, t: (k_of(p, t), j, 0), t=t)))
    return _hosted(
        body, carry, n_prefetch=1, out_shape=(SDS((r, cdim), F32), SDS((3, r, cdim), BF16)),
        grid=(r // tr,), in_specs=in_specs,
        out_specs=[pl.BlockSpec((tr, cdim), lambda j, p: (j, 0)), pl.BlockSpec((3, tr, cdim), lambda j, p: (0, j, 0))],
        compiler_params=_arb(1), name=name)(pos, g, recv, g, recv, g, recv, g, recv)


def _adamw_shard(keep, recv, w, m, v, name):
    r, cdim = w.shape
    tr = _tile(r, 256, 16)

    def body(k_ref, r_ref, w_ref, m_ref, v_ref, g_ref, d_ref, nm_ref, nv_ref):
        g = ((k_ref[...] + r_ref[0].astype(F32)) + r_ref[1].astype(F32)) + r_ref[2].astype(F32)
        g_ref[...] = g
        d_ref[...], nm_ref[...], nv_ref[...] = _adamw(w_ref[...], g, m_ref[...], v_ref[...])

    blk = pl.BlockSpec((tr, cdim), lambda j: (j, 0))
    out = SDS((r, cdim), F32)
    return _pcall(body, grid=(r // tr,), in_specs=[blk, pl.BlockSpec((3, tr, cdim), lambda j: (0, j, 0)), blk, blk, blk],
                  out_specs=[blk] * 4, out_shape=(out,) * 4, compiler_params=_arb(1), name=name)(keep, recv, w, m, v)


def _adamw_small_shards(items, name):
    n = len(items)

    def body(*refs):
        ins, outs = refs[:5 * n], refs[5 * n:]
        for k in range(n):
            k_ref, r_ref, w_ref, m_ref, v_ref = ins[5 * k:5 * k + 5]
            g = ((k_ref[...] + r_ref[0].astype(F32)) + r_ref[1].astype(F32)) + r_ref[2].astype(F32)
            outs[4 * k][...] = g
            outs[4 * k + 1][...], outs[4 * k + 2][...], outs[4 * k + 3][...] = _adamw(w_ref[...], g, m_ref[...], v_ref[...])

    out_shape = tuple(SDS(it[2].shape, F32) for it in items for _ in range(4))
    res = _pcall(body, out_shape=out_shape, name=name)(*[a for it in items for a in it])
    return [res[4 * k:4 * k + 4] for k in range(n)]


_HBM = pl.BlockSpec(memory_space=pltpu.HBM)
_SEM = pl.BlockSpec(memory_space=pltpu.SEMAPHORE)
_SPLIT = pltpu.CompilerParams(has_side_effects=pltpu.SideEffectType.DATAFLOW_SIDE_EFFECTING)


def _split_copies(kind, n, refs):
    srcs, lands, (send_sems, recv_sems) = refs[:n], refs[n:2 * n], refs[2 * n:2 * n + 2]
    x, y, c, chips = _place()
    per = _SPLIT_COPIES[kind]
    if kind == "xy":
        ends = lambda a, t: (srcs[a].at[t], lands[a].at[t], (*chips[t], c))
    else:
        ends = lambda a, k: (srcs[a].at[2 * k + 1 - c], lands[a].at[k], (x, y, 1 - c))
    cps = []
    for a in range(n):
        for t in range(per):
            src, dst, to = ends(a, t)
            cps.append(pltpu.make_async_remote_copy(src_ref=src, dst_ref=dst, send_sem=send_sems.at[per * a + t],
                                                    recv_sem=recv_sems.at[per * a + t], device_id=to, device_id_type=MESH))
    return cps


_SPLIT_COPIES = {"xy": 3, "c": 4}
SIBLING_COLLECTIVE = 1
GATHER_COLLECTIVE = 5


def _exchange_start(kind, arrays, name, after=None, collective=None):
    n = len(arrays)
    order = [] if after is None else [after]

    def body(*refs):
        x, y, c, chips = _place()
        peers = [(x, y, 1 - c)] if kind == "c" else [(*chip, c) for chip in chips]
        barrier = pltpu.get_barrier_semaphore()
        for peer in peers:
            pl.semaphore_signal(barrier, inc=1, device_id=peer, device_id_type=MESH)
        pl.semaphore_wait(barrier, len(peers))
        refs = refs[:2 * n] + refs[2 * n + len(order):]
        for cp in _split_copies(kind, n, refs):
            cp.start()
        refs[-1][...] = jnp.zeros_like(refs[-1])

    params = pltpu.CompilerParams(has_side_effects=pltpu.SideEffectType.DATAFLOW_SIDE_EFFECTING,
                                  collective_id=SIBLING_COLLECTIVE if kind == "c" else collective)
    hbm = lambda a: pltpu.with_memory_space_constraint(a, pltpu.HBM)
    land = [a.shape if kind == "xy" else (4,) + a.shape[1:] for a in arrays]
    bufs = [pltpu.HBM(a.shape, a.dtype) for a in arrays] + [pltpu.HBM(s, a.dtype) for s, a in zip(land, arrays)]
    sems = pltpu.SemaphoreType.DMA((_SPLIT_COPIES[kind] * n,))
    res = _pcall(
        body, name=name, out_shape=(sems, sems, *bufs, SDS((SUB, LANES), F32)),
        in_specs=[_HBM] * (2 * n) + _hbm_specs(len(order)),
        out_specs=[_SEM, _SEM] + [_HBM] * (2 * n) + [pl.BlockSpec(memory_space=pltpu.VMEM)],
        input_output_aliases={k: 2 + k for k in range(2 * n)}, compiler_params=params)(
            *[hbm(a) for a in arrays], *[hbm(lax.empty(s, a.dtype)) for s, a in zip(land, arrays)], *order)
    return (kind, n, res[:-1]), res[-1]


def _exchange_wait(started, after, name, sources=False):
    kind, n, (send_sems, recv_sems, *bufs) = started

    def body(*refs):
        for cp in _split_copies(kind, n, refs):
            cp.wait_send()
            cp.wait_recv()

    shapes = [pltpu.HBM(b.shape, b.dtype) for b in bufs]
    res = _pcall(
        body, name=name, out_shape=tuple(shapes),
        in_specs=[_HBM] * (2 * n) + [_SEM, _SEM, pl.BlockSpec(memory_space=pl.ANY)], out_specs=[_HBM] * (2 * n),
        input_output_aliases={k: k for k in range(2 * n)}, compiler_params=_SPLIT)(*bufs, send_sems, recv_sems, after)
    return (list(res[:n]), list(res[n:])) if sources else list(res[n:])


def _follow(token):
    nothing = lambda ins, outs, sems: None
    return _Exchange([token], [], [], nothing, nothing)


def _adamw_small(gathered, seg, params, conv_rows):
    names = list(params)
    c0, cn = conv_rows

    def body(*refs):
        gat_ref = refs[0]
        ins = refs[1:1 + 3 * len(names)]
        outs = refs[1 + 3 * len(names):]

        def total(r0, rn):
            tot = gat_ref[0, r0:r0 + rn, :]
            for dev in range(1, N_DEV):
                tot = tot + gat_ref[dev, r0:r0 + rn, :]
            return tot

        for k, nm in enumerate(names):
            g = total(*seg[nm])
            w_ref, m_ref, v_ref = ins[3 * k:3 * k + 3]
            g_ref, d_ref, nm_ref, nv_ref = outs[4 * k:4 * k + 4]
            g_ref[...] = g
            d_ref[...], nm_ref[...], nv_ref[...] = _adamw(w_ref[...], g, m_ref[...], v_ref[...])
        outs[-2][...] = total(c0, cn)
        outs[-1][...] = total(*seg["loss"])

    flat_in = [a for nm in names for a in params[nm]]
    out_shape = []
    for nm in names:
        out_shape += [SDS(params[nm][0].shape, F32)] * 4
    out_shape += [SDS((cn, LANES), F32), SDS((seg["loss"][1], LANES), F32)]
    res = _pcall(body, out_shape=tuple(out_shape), name="adamw_small")(gathered, *flat_in)
    per = {nm: res[4 * k:4 * k + 4] for k, nm in enumerate(names)}
    return per, res[-2], res[-1]


def _adamw_one(w, g, m, v, name):
    def body(w_ref, g_ref, m_ref, v_ref, d_ref, nm_ref, nv_ref):
        d_ref[...], nm_ref[...], nv_ref[...] = _adamw(w_ref[...], g_ref[...], m_ref[...], v_ref[...])

    return _pcall(body, out_shape=(SDS(w.shape, F32),) * 3, name=name)(w, g, m, v)


def _rows128(a):
    return a.reshape(-1, LANES)


def _pack_small(gs, loss_tile):
    seg, pieces, row = {}, [], 0
    for nm in SMALL + ("conv_w", "loss"):
        piece = loss_tile if nm == "loss" else _rows128(gs[nm])
        rn = _round_up(piece.shape[0], SUB)
        pieces.append(jnp.pad(piece, ((0, rn - piece.shape[0]), (0, 0))))
        seg[nm] = (row, piece.shape[0])
        row += rn
    return jnp.concatenate(pieces, axis=0), seg


def _step(x, mem, target, wb, conv_w, sp, pos):
    s, d = x.shape
    tm = min(TOKEN_TILE, s)
    tm_wide = min(2 * TOKEN_TILE, s)
    rows = lambda w8: w8.reshape(-1, w8.shape[2])
    shards = lambda g: g.reshape((N_DEV, -1) + g.shape[1:])
    bt = sp["b_spatial"].T

    (w_in8, conv8), = _run_exchanges([_all_gather([wb["w_in"], conv_w])], "gather_w_in")
    conv_full = conv8.transpose(1, 0, 2).reshape(3, -1)
    w_in_t = rows(w_in8)
    (xn1, h), ((w_out8, w_kv8, w_q8),) = _in_forward(
        x, sp["ln_mix_g"], w_in_t, tm, carry=[_all_gather([wb["w_out"], wb["w_kv"], wb["w_q"]])])
    w_out = rows(w_out8)
    (ycat, x1), ((w_o8, w_down8),) = _mix_forward(
        h, x, sp["sgu_ln_g"], sp["sgu_ln_b"], sp["w_spatial"], bt, conv_full, sp["grp_norm_a"], sp["grp_norm_b"], w_out, tm,
        carry=[_all_gather([wb["w_o"], wb["w_down"]])])
    w_q, w_o, w_down = rows(w_q8), rows(w_o8), rows(w_down8)
    memn, kv = _kv_forward(mem, sp["ln_mem_g"], w_kv8)
    (xn2, q, probs, o, x2), ((w_gu8,),) = _attn_forward(
        x1, sp["ln_attn_g"], w_q, kv, w_o, tm, carry=[_all_gather([wb["w_gate_up"]])])
    w_gu = w_gu8.reshape((2, N_DEV // 2) + w_gu8.shape[1:])
    xn3, gu, x3 = _ffn_forward(x2, sp["ln_ffn_g"], w_gu, w_down, tm_wide)

    loss, d_lnf, dx3, dx3b = _final_backward(x3, target, sp["ln_final_g"], tm_wide)
    act, dgu, dxn3 = _swiglu_backward(dx3b, gu, w_gu, w_down, tm_wide)
    g_gu, _ = _wgrad_blocked_lhs(dgu.reshape((N_DEV,) + dgu.shape[2:]), xn3, "wgrad_gate_up")
    g_gu = shards(g_gu)
    g_down, ((rc_gu,),) = _wgrad_blocked_lhs(act, dx3b, "wgrad_down", carry=[_exchange_c([g_gu])])
    g_down = shards(g_down)
    keep, pending = {}, []
    (keep["w_gate_up"], send_gu), _ = _rs_combine(g_gu, rc_gu, pos, "rs_combine_w_gate_up")
    started, token = _exchange_start("xy", [send_gu], "exchange_xy_1_start", collective=2)
    pending.append((("w_gate_up",), started))
    c_down, token = _exchange_start("c", [g_down], "exchange_c_1_start", after=token)
    (dx2b, dq, dx1, dx1b, dkv, d_lnattn, d_lnffn), _ = _attn_backward(
        dx3, dxn3, x2, sp["ln_ffn_g"], x1, sp["ln_attn_g"], q, probs, kv, w_q, w_o, tm, carry=[_follow(token)])
    (g_down,), (rc_down,) = _exchange_wait(c_down, dx1b, "exchange_c_1_wait", sources=True)
    (keep["w_down"], send_down), _ = _rs_combine(g_down, rc_down, pos, "rs_combine_w_down")
    g_o, _ = _wgrad(o, dx2b, "wgrad_o")
    g_q, _ = _wgrad(xn2, dq, "wgrad_q")
    g_o, g_q = shards(g_o), shards(g_q)
    g_kv, d_lnmem = _kv_backward(dkv, memn, mem, sp["ln_mem_g"], w_kv8)
    c_oqkv, token = _exchange_start("c", [g_o, g_q, g_kv], "exchange_c_2_start")
    g_out, _ = _wgrad(ycat, dx1b, "wgrad_out", carry=[_follow(token)])
    g_out = shards(g_out)
    (g_o, g_q, g_kv), (rc_o, rc_q, rc_kv) = _exchange_wait(c_oqkv, g_out, "exchange_c_2_wait", sources=True)
    c_out, token = _exchange_start("c", [g_out], "exchange_c_3_start")
    (keep["w_o"], send_o), _ = _rs_combine(g_o, rc_o, pos, "rs_combine_w_o", carry=[_follow(token)])
    (keep["w_q"], send_q), _ = _rs_combine(g_q, rc_q, pos, "rs_combine_w_q")
    (keep["w_kv"], send_kv), _ = _rs_combine(g_kv, rc_kv, pos, "rs_combine_w_kv")
    (g_out,), (rc_out,) = _exchange_wait(c_out, send_kv, "exchange_c_3_wait", sources=True)
    (keep["w_out"], send_out), _ = _rs_combine(g_out, rc_out, pos, "rs_combine_w_out")
    started, token = _exchange_start("xy", [send_down, send_o, send_q, send_out, send_kv], "exchange_xy_2_start",
                                     collective=3)
    pending.append((("w_down", "w_o", "w_q", "w_out", "w_kv"), started))
    (dh, dx, d_ga, d_gb, d_cw, d_lng, d_lnb, d_wsp, d_bs, d_lnmix), _ = _mix_backward(
        dx1, x, sp["ln_mix_g"], h, sp["sgu_ln_g"], sp["sgu_ln_b"], sp["w_spatial"], bt, conv_full,
        sp["grp_norm_a"], sp["grp_norm_b"], w_out, w_in_t, tm, carry=[_follow(token)])
    gs = {"ln_mix_g": d_lnmix, "sgu_ln_g": d_lng, "sgu_ln_b": d_lnb, "w_spatial": d_wsp, "b_spatial": _bias_grad(d_bs),
          "conv_w": d_cw[:3], "grp_norm_a": d_ga, "grp_norm_b": d_gb, "ln_attn_g": d_lnattn, "ln_mem_g": d_lnmem,
          "ln_ffn_g": d_lnffn, "ln_final_g": d_lnf}
    packed, seg = _pack_small(gs, loss)
    g_in, (_, (small_all,)) = _wgrad(dh, xn1, "wgrad_in", carry=[_follow(token), _all_gather([packed])])
    g_in = shards(g_in)
    c_in, token = _exchange_start("c", [g_in], "exchange_c_4_start")
    return dx, keep, pending, (g_in, c_in), token, small_all, seg


def kernel(x, mem, ln_mix_g, w_in, sgu_ln_g, sgu_ln_b, w_spatial, b_spatial, conv_w, grp_norm_a, grp_norm_b, w_out, ln_attn_g, ln_mem_g, w_q, w_kv, w_o, ln_ffn_g, w_gate_up, w_down, ln_final_g, loss_target, m_ln_mix_g, m_w_in, m_sgu_ln_g, m_sgu_ln_b, m_w_spatial, m_b_spatial, m_conv_w, m_grp_norm_a, m_grp_norm_b, m_w_out, m_ln_attn_g, m_ln_mem_g, m_w_q, m_w_kv, m_w_o, m_ln_ffn_g, m_w_gate_up, m_w_down, m_ln_final_g, v_ln_mix_g, v_w_in, v_sgu_ln_g, v_sgu_ln_b, v_w_spatial, v_b_spatial, v_conv_w, v_grp_norm_a, v_grp_norm_b, v_w_out, v_ln_attn_g, v_ln_mem_g, v_w_q, v_w_kv, v_w_o, v_ln_ffn_g, v_w_gate_up, v_w_down, v_ln_final_g):
    order = ["ln_mix_g", "w_in", "sgu_ln_g", "sgu_ln_b", "w_spatial", "b_spatial", "conv_w", "grp_norm_a", "grp_norm_b",
             "w_out", "ln_attn_g", "ln_mem_g", "w_q", "w_kv", "w_o", "ln_ffn_g", "w_gate_up", "w_down", "ln_final_g"]
    W = dict(ln_mix_g=ln_mix_g, w_in=w_in, sgu_ln_g=sgu_ln_g, sgu_ln_b=sgu_ln_b, w_spatial=w_spatial, b_spatial=b_spatial,
             conv_w=conv_w, grp_norm_a=grp_norm_a, grp_norm_b=grp_norm_b, w_out=w_out, ln_attn_g=ln_attn_g,
             ln_mem_g=ln_mem_g, w_q=w_q, w_kv=w_kv, w_o=w_o, ln_ffn_g=ln_ffn_g, w_gate_up=w_gate_up, w_down=w_down,
             ln_final_g=ln_final_g)
    M = dict(ln_mix_g=m_ln_mix_g, w_in=m_w_in, sgu_ln_g=m_sgu_ln_g, sgu_ln_b=m_sgu_ln_b, w_spatial=m_w_spatial,
             b_spatial=m_b_spatial, conv_w=m_conv_w, grp_norm_a=m_grp_norm_a, grp_norm_b=m_grp_norm_b, w_out=m_w_out,
             ln_attn_g=m_ln_attn_g, ln_mem_g=m_ln_mem_g, w_q=m_w_q, w_kv=m_w_kv, w_o=m_w_o, ln_ffn_g=m_ln_ffn_g,
             w_gate_up=m_w_gate_up, w_down=m_w_down, ln_final_g=m_ln_final_g)
    V = dict(ln_mix_g=v_ln_mix_g, w_in=v_w_in, sgu_ln_g=v_sgu_ln_g, sgu_ln_b=v_sgu_ln_b, w_spatial=v_w_spatial,
             b_spatial=v_b_spatial, conv_w=v_conv_w, grp_norm_a=v_grp_norm_a, grp_norm_b=v_grp_norm_b, w_out=v_w_out,
             ln_attn_g=v_ln_attn_g, ln_mem_g=v_ln_mem_g, w_q=v_w_q, w_kv=v_w_kv, w_o=v_w_o, ln_ffn_g=v_ln_ffn_g,
             w_gate_up=v_w_gate_up, w_down=v_w_down, ln_final_g=v_ln_final_g)

    bw = conv_w.shape[1] * N_DEV
    pos = jnp.stack([lax.axis_index("x"), lax.axis_index("y"), lax.axis_index("c")]).astype(jnp.int32)
    me = 4 * pos[0] + 2 * pos[1] + pos[2]

    sp = {nm: (W[nm].reshape(1, -1) if W[nm].ndim == 1 else W[nm]) for nm in SMALL}
    view = lambda a, nm: a.T if nm in TRANSPOSED else a
    wb = {nm: view(W[nm], nm).astype(BF16) for nm in BIG}
    grad_x, keep, pending, (g_in, c_in), token, small_all, seg = _step(
        x[0], mem[0], loss_target[0], wb, conv_w, sp, pos)

    out = {}

    def update(k, names, started, token):
        landed = dict(zip(names, _exchange_wait(started, token, "exchange_xy_%d_wait" % k)))
        args = lambda nm: (keep[nm], landed[nm], view(W[nm], nm), view(M[nm], nm), view(V[nm], nm))
        small = [nm for nm in names if nm in SQUARE]
        if len(small) > 1:
            for nm, res in zip(small, _adamw_small_shards([args(nm) for nm in small], "adamw_square")):
                out[nm] = tuple(res)
                token = res[0]
        for nm in names:
            if nm not in out:
                res = _adamw_shard(*args(nm), "adamw_" + nm)
                out[nm] = tuple(view(a, nm) for a in res)
                token = res[0]
        return token

    token = update(1, *pending[0], token)
    (g_in,), (rc_in,) = _exchange_wait(c_in, token, "exchange_c_4_wait", sources=True)
    (keep["w_in"], send_in), _ = _rs_combine(g_in, rc_in, pos, "rs_combine_w_in")
    xy_in, token = _exchange_start("xy", [send_in], "exchange_xy_3_start", collective=4)
    token = update(2, *pending[1], token)

    params = {nm: (_rows128(W[nm]), _rows128(M[nm]), _rows128(V[nm])) for nm in SMALL}
    per, conv_g_rows, loss_sum = _adamw_small(small_all, seg, params, seg["conv_w"])
    for nm in SMALL:
        out[nm] = tuple(a.reshape(W[nm].shape) for a in per[nm])
    conv_g = lax.dynamic_slice_in_dim(conv_g_rows.reshape(3, bw), me * conv_w.shape[1], conv_w.shape[1], axis=1)
    out["conv_w"] = (conv_g,) + tuple(_adamw_one(conv_w, conv_g, m_conv_w, v_conv_w, "adamw_conv"))

    update(3, ("w_in",), xy_in, token[:1, :1] + out["conv_w"][1][:1, :1])

    loss = loss_sum[0, 0]
    res = [loss, grad_x[None]]
    for k in range(4):
        res += [out[nm][k] for nm in order]
    return tuple(res)
```

```python
import functools

import jax
import jax.numpy as jnp
from jax import lax
from jax.experimental import pallas as pl
from jax.experimental.pallas import tpu as pltpu

F32 = jnp.float32
BF16 = jnp.bfloat16
SDS = jax.ShapeDtypeStruct
MESH = pl.DeviceIdType.MESH

EPS = 1e-6
N_DEV = 8
HEADS = 4
CHUNK = 128
HALO = 16
SUB = 8
LANES = 128
TOKEN_TILE = 512
ROW_CHUNK = 256
RELAY_AT = 0.7

ADAM_LR = 0.001
ADAM_B1 = 0.9
ADAM_B2 = 0.999
ADAM_EPS = 1e-08
ADAM_WD = 0.01
ADAM_STEP = 10

BIG = ("w_in", "w_out", "w_q", "w_kv", "w_o", "w_gate_up", "w_down")
TRANSPOSED = ("w_in", "w_gate_up")
SMALL = ("ln_mix_g", "sgu_ln_g", "sgu_ln_b", "w_spatial", "b_spatial", "grp_norm_a", "grp_norm_b",
         "ln_attn_g", "ln_mem_g", "ln_ffn_g", "ln_final_g")


class _Exchange:
    def __init__(self, ins, out_shape, sems, start, finish, relay=None, peers=None, collective=None):
        self.ins, self.out_shape, self.sems = list(ins), list(out_shape), list(sems)
        self.start, self.finish, self.relay = start, finish, relay
        self.peers, self.collective = peers, collective


def _pcall(body, carry=(), n_prefetch=0, **kw):
    if carry:
        return functools.partial(_carrying_call, body, tuple(carry), n_prefetch, kw)
    if n_prefetch:
        kw["grid_spec"] = pltpu.PrefetchScalarGridSpec(
            num_scalar_prefetch=n_prefetch, grid=kw.pop("grid"), in_specs=kw.pop("in_specs"),
            out_specs=kw.pop("out_specs"), scratch_shapes=kw.pop("scratch_shapes", ()))
    return pl.pallas_call(body, **kw)


def _carrying_call(body, carry, n_prefetch, kw, *args):
    kw = dict(kw)
    out_shape = kw.pop("out_shape")
    single = not isinstance(out_shape, (tuple, list))
    outs_shape = (out_shape,) if single else tuple(out_shape)
    out_specs = kw.pop("out_specs")
    out_specs = [out_specs] if single else list(out_specs)
    in_specs = list(kw.pop("in_specs"))
    scratch = list(kw.pop("scratch_shapes", ()))
    grid = tuple(kw.get("grid", ()))
    n_in, n_out, n_scr = len(args), len(outs_shape), len(scratch)
    copying = [p for p in carry if p.sems]
    shaking = copying[0] if len(copying) == 1 and copying[0].collective is not None else None
    if shaking is not None:
        old = kw.get("compiler_params")
        kw["compiler_params"] = pltpu.CompilerParams(
            dimension_semantics=None if old is None else old.dimension_semantics, collective_id=shaking.collective)

    def split(refs, k, counts):
        parts = []
        for cnt in counts:
            parts.append(refs[k:k + cnt])
            k += cnt
        return parts, k

    def wrapped(*refs):
        cins, k = split(refs, n_in, [len(p.ins) for p in carry])
        outs = refs[k:k + n_out]
        couts, k = split(refs, k + n_out, [len(p.out_shape) for p in carry])
        scr = refs[k:k + n_scr]
        csems, _ = split(refs, k + n_scr, [len(p.sems) for p in carry])
        first, last = True, True
        for a, g in enumerate(grid):
            first = (pl.program_id(a) == 0) & first
            last = (pl.program_id(a) == g - 1) & last

        def start_all():
            if shaking is not None:
                peers = shaking.peers()
                barrier = pltpu.get_barrier_semaphore()
                for peer in peers:
                    pl.semaphore_signal(barrier, inc=1, device_id=peer, device_id_type=MESH)
                pl.semaphore_wait(barrier, len(peers))
            for p, ci, co, cs in zip(carry, cins, couts, csems):
                p.start(ci, co, cs)

        def relay_all():
            for p, ci, co, cs in zip(carry, cins, couts, csems):
                if p.relay is not None:
                    p.relay(ci, co, cs)

        def finish_all():
            for p, ci, co, cs in zip(carry, cins, couts, csems):
                p.finish(ci, co, cs)

        if len(grid) == 1:
            relay_now = pl.program_id(0) == min(int(RELAY_AT * grid[0]), grid[0] - 1)
        else:
            relay_now = last
        start_all() if not grid else pl.when(first)(start_all)
        relay_all() if not grid else pl.when(relay_now)(relay_all)
        body(*refs[:n_in], *outs, *scr)
        finish_all() if not grid else pl.when(last)(finish_all)

    c_in = [a for p in carry for a in p.ins]
    c_out = [s for p in carry for s in p.out_shape]
    c_sems = [s for p in carry for s in p.sems]
    res = _pcall(wrapped, n_prefetch=n_prefetch, out_shape=outs_shape + tuple(c_out),
                 in_specs=in_specs + _hbm_specs(len(c_in)), out_specs=out_specs + _hbm_specs(len(c_out)),
                 scratch_shapes=scratch + c_sems, **kw)(*args, *c_in)
    own = res[0] if single else tuple(res[:n_out])
    landed, k = [], n_out
    for p in carry:
        landed.append(list(res[k:k + len(p.out_shape)]))
        k += len(p.out_shape)
    return own, landed


def _hbm_specs(n):
    return [pl.BlockSpec(memory_space=pl.ANY)] * n


def _hosted(body, carry, **kw):
    if carry:
        return _pcall(body, carry=carry, **kw)
    call = _pcall(body, **kw)
    return lambda *args: (call(*args), [])


def _run_exchanges(parts, name):
    def body(*refs):
        pass

    _, landed = _pcall(body, carry=parts, out_shape=(), in_specs=[], out_specs=[], name=name)()
    return landed


def _arb(n):
    return pltpu.CompilerParams(dimension_semantics=("arbitrary",) * n)


def _tile(n, target, mult):
    best = None
    for t in range(mult, min(n, target) + 1, mult):
        if n % t == 0:
            best = t
    return n if best is None else best


def _round_up(n, m):
    return (n + m - 1) // m * m


def _dot(a, b):
    return jnp.dot(a, b, preferred_element_type=F32)


def _dot_nt(a, b):
    return lax.dot_general(a, b, (((1,), (1,)), ((), ())), preferred_element_type=F32)


def _dot_tn(a, b):
    return lax.dot_general(a, b, (((0,), (0,)), ((), ())), preferred_element_type=F32)


def _rstd(x):
    return lax.rsqrt(jnp.mean(x * x, axis=-1, keepdims=True) + EPS)


def _rms_bwd(dy, x, r, g):
    gdy = dy * g
    proj = jnp.sum(gdy * x, axis=-1, keepdims=True) * (1.0 / x.shape[-1])
    dx = r * gdy - x * (r * r * r) * proj
    dg = jnp.sum(dy * (x * r), axis=0, keepdims=True)
    return dx, dg


_GELU_C = 0.7978845608028654
_GELU_A = 0.044715


def _gelu(x):
    t = jnp.tanh(_GELU_C * (x + _GELU_A * x * x * x))
    return 0.5 * x * (1.0 + t), t


def _gelu_grad(x, t):
    return 0.5 * (1.0 + t) + 0.5 * x * (1.0 - t * t) * (_GELU_C * (1.0 + 3.0 * _GELU_A * x * x))


def _sigmoid(x):
    return 1.0 / (1.0 + jnp.exp(-x))


def _softmax(s):
    m = jnp.max(s, axis=-1, keepdims=True)
    e = jnp.exp(s - m)
    return e / jnp.sum(e, axis=-1, keepdims=True)


def _adamw(w, g, m, v):
    m = ADAM_B1 * m + (1.0 - ADAM_B1) * g
    v = ADAM_B2 * v + (1.0 - ADAM_B2) * (g * g)
    m_hat = m / (1.0 - ADAM_B1 ** ADAM_STEP)
    v_hat = v / (1.0 - ADAM_B2 ** ADAM_STEP)
    delta = -ADAM_LR * (m_hat / (jnp.sqrt(v_hat) + ADAM_EPS) + ADAM_WD * w)
    return delta, m, v


def _tril_mask():
    t = lax.broadcasted_iota(jnp.int32, (CHUNK, CHUNK), 0)
    s = lax.broadcasted_iota(jnp.int32, (CHUNK, CHUNK), 1)
    return (s <= t).astype(F32)


def _sgu_forward(ha, lng, lnb, wm, bt, mixed_s):
    aw = ha.shape[1] // 2
    hd = aw // HEADS
    a, th = _gelu(ha)
    u = a[:, :aw]
    v = a[:, aw:]
    mu = jnp.mean(v, axis=-1, keepdims=True)
    vc = v - mu
    rl = lax.rsqrt(jnp.mean(vc * vc, axis=-1, keepdims=True) + EPS)
    xhat = vc * rl
    vln = (xhat * lng + lnb).astype(BF16)
    for n in range(ha.shape[0] // CHUNK):
        rows = slice(n * CHUNK, (n + 1) * CHUNK)
        for h in range(HEADS):
            cols = slice(h * hd, (h + 1) * hd)
            mixed_s[rows, cols] = _dot(wm[h], vln[rows, cols]) + bt[:, h:h + 1]
    return th, u, xhat, rl, vln


def _conv_taps(zext):
    return pltpu.roll(zext, 2, 0), pltpu.roll(zext, 1, 0)


def _kv_forward(mem, g_mem, w_kv):
    ml, d = mem.shape
    xd = w_kv.shape[2]

    def body(mem_ref, g_ref, w_ref, memn_ref, kv_ref):
        x = mem_ref[...]
        memn = (x * _rstd(x) * g_ref[...]).astype(BF16)
        memn_ref[...] = memn
        for j in range(2 * HEADS):
            kv_ref[j] = _dot(memn, w_ref[j]).astype(BF16)

    return _pcall(body, out_shape=(SDS((ml, d), BF16), SDS((2 * HEADS, ml, xd), BF16)), name="kv_forward")(mem, g_mem, w_kv)


def _in_forward(x, g, w_in_t, tm, carry=()):
    s, d = x.shape
    n_in = w_in_t.shape[0]

    def body(x_ref, g_ref, w_ref, xn_ref, h_ref):
        xv = x_ref[...]
        xn = (xv * _rstd(xv) * g_ref[...]).astype(BF16)
        xn_ref[...] = xn
        h_ref[...] = _dot_nt(xn, w_ref[...])

    return _hosted(
        body, carry, grid=(s // tm,),
        in_specs=[pl.BlockSpec((tm, d), lambda i: (i, 0)), pl.BlockSpec((1, d), lambda i: (0, 0)),
                  pl.BlockSpec((n_in, d), lambda i: (0, 0))],
        out_specs=[pl.BlockSpec((tm, d), lambda i: (i, 0)), pl.BlockSpec((tm, n_in), lambda i: (i, 0))],
        out_shape=(SDS((s, d), BF16), SDS((s, n_in), F32)),
        compiler_params=_arb(1), name="in_forward")(x, g, w_in_t)


def _mix_forward(h, x, lng, lnb, w_sp, bt, conv_w, ga, gb, w_out, tm, carry=()):
    s, d = x.shape
    n_in = h.shape[1]
    aw = lng.shape[1]
    bw = d - aw
    in_a = 2 * aw
    hb_blocks = tm // HALO

    def body(h_ref, hprev_ref, x_ref, lng_ref, lnb_ref, wsp_ref, bt_ref, cw_ref, ga_ref, gb_ref, wout_ref,
             ycat_ref, x1_ref, mixed_s):
        i = pl.program_id(0)
        mask = _tril_mask()
        wm = [(wsp_ref[hh] * mask).astype(BF16) for hh in range(HEADS)]
        hv = h_ref[...]
        _, u, _, _, _ = _sgu_forward(hv[:, :in_a], lng_ref[...], lnb_ref[...], wm, bt_ref[...], mixed_s)
        sg = u * mixed_s[...]
        ycat_ref[:, :aw] = (sg * _rstd(sg) * ga_ref[...]).astype(BF16)

        gate_b = hv[:, in_a:in_a + bw]
        z = hv[:, in_a + bw:in_a + 2 * bw] * hv[:, in_a + 2 * bw:]
        hp = hprev_ref[...]
        zp = hp[:, in_a + bw:in_a + 2 * bw] * hp[:, in_a + 2 * bw:]
        zp = jnp.where(i == 0, 0.0, zp)
        zext = jnp.concatenate([zp, z], axis=0)
        z2, z1 = _conv_taps(zext)
        cw = cw_ref[...]
        conv = cw[0:1] * z2[HALO:] + cw[1:2] * z1[HALO:] + cw[2:3] * z
        sc = gate_b * conv
        ycat_ref[:, aw:] = (sc * _rstd(sc) * gb_ref[...]).astype(BF16)
        x1_ref[...] = x_ref[...] + _dot(ycat_ref[...], wout_ref[...])

    full = lambda shape: pl.BlockSpec(shape, lambda i: (0,) * len(shape))
    return _hosted(
        body, carry, grid=(s // tm,),
        in_specs=[pl.BlockSpec((tm, n_in), lambda i: (i, 0)),
                  pl.BlockSpec((HALO, n_in), lambda i: (jnp.maximum(i * hb_blocks - 1, 0), 0)),
                  pl.BlockSpec((tm, d), lambda i: (i, 0)),
                  full((1, aw)), full((1, aw)), full((HEADS, CHUNK, CHUNK)), full((CHUNK, HEADS)),
                  full((3, bw)), full((1, aw)), full((1, bw)), full((d, d))],
        out_specs=[pl.BlockSpec((tm, d), lambda i: (i, 0)), pl.BlockSpec((tm, d), lambda i: (i, 0))],
        out_shape=(SDS((s, d), BF16), SDS((s, d), F32)),
        scratch_shapes=[pltpu.VMEM((tm, aw), F32)],
        compiler_params=_arb(1), name="mix_forward")(h, h, x, lng, lnb, w_sp, bt, conv_w, ga, gb, w_out)


def _attn_forward(x1, g, w_q, kv, w_o, tm, carry=()):
    s, d = x1.shape
    _, ml, xd = kv.shape
    scale = xd ** -0.5

    def body(x1_ref, g_ref, wq_ref, kv_ref, wo_ref, xn_ref, q_ref, p_ref, o_ref, x2_ref):
        xv = x1_ref[...]
        xn = (xv * _rstd(xv) * g_ref[...]).astype(BF16)
        xn_ref[...] = xn
        q_ref[...] = _dot(xn, wq_ref[...]).astype(BF16)
        for hh in range(HEADS):
            cols = slice(hh * xd, (hh + 1) * xd)
            p = _softmax(_dot_nt(q_ref[:, cols], kv_ref[hh]) * scale).astype(BF16)
            p_ref[:, hh * ml:(hh + 1) * ml] = p
            o_ref[:, cols] = _dot(p, kv_ref[HEADS + hh]).astype(BF16)
        x2_ref[...] = xv + _dot(o_ref[...], wo_ref[...])

    tok = pl.BlockSpec((tm, d), lambda i: (i, 0))
    probs = pl.BlockSpec((tm, HEADS * ml), lambda i: (i, 0))
    return _hosted(
        body, carry, grid=(s // tm,),
        in_specs=[tok, pl.BlockSpec((1, d), lambda i: (0, 0)), pl.BlockSpec((d, d), lambda i: (0, 0)),
                  pl.BlockSpec((2 * HEADS, ml, xd), lambda i: (0, 0, 0)), pl.BlockSpec((d, d), lambda i: (0, 0))],
        out_specs=[tok, tok, probs, tok, tok],
        out_shape=(SDS((s, d), BF16), SDS((s, d), BF16), SDS((s, HEADS * ml), BF16), SDS((s, d), BF16), SDS((s, d), F32)),
        compiler_params=_arb(1), name="attn_forward")(x1, g, w_q, kv, w_o)


def _ffn_forward(x2, g, w_gu, w_down, tm):
    s, d = x2.shape
    _, nf, tf, _ = w_gu.shape

    def body(x2_ref, g_ref, wgu_ref, wd_ref, xn_ref, gu_ref, x3_ref):
        f = pl.program_id(1)

        @pl.when(f == 0)
        def _():
            xv = x2_ref[...]
            xn_ref[...] = (xv * _rstd(xv) * g_ref[...]).astype(BF16)
            x3_ref[...] = xv

        xn = xn_ref[...]
        gate = _dot_nt(xn, wgu_ref[0])
        up = _dot_nt(xn, wgu_ref[1])
        gu_ref[0] = gate.astype(BF16)
        gu_ref[1] = up.astype(BF16)
        act = (gate * _sigmoid(gate) * up).astype(BF16)
        x3_ref[...] += _dot(act, wd_ref[...])

    tok = pl.BlockSpec((tm, d), lambda i, f: (i, 0))
    return _pcall(
        body, grid=(s // tm, nf),
        in_specs=[tok, pl.BlockSpec((1, d), lambda i, f: (0, 0)),
                  pl.BlockSpec((2, None, tf, d), lambda i, f: (0, f, 0, 0)),
                  pl.BlockSpec((tf, d), lambda i, f: (f, 0))],
        out_specs=[tok, pl.BlockSpec((2, None, tm, tf), lambda i, f: (0, f, i, 0)), tok],
        out_shape=(SDS((s, d), BF16), SDS((2, nf, s, tf), BF16), SDS((s, d), F32)),
        compiler_params=_arb(2), name="ffn_forward")(x2, g, w_gu, w_down)


def _final_backward(x3, target, g_final, tm):
    s, d = x3.shape

    def body(x3_ref, tgt_ref, gf_ref, loss_ref, dgf_ref, dx3_ref, dx3b_ref):
        @pl.when(pl.program_id(0) == 0)
        def _():
            loss_ref[...] = jnp.zeros_like(loss_ref)
            dgf_ref[...] = jnp.zeros_like(dgf_ref)

        xv = x3_ref[...]
        r = _rstd(xv)
        diff = xv * r * gf_ref[...] - tgt_ref[...]
        loss_ref[...] += 0.5 * jnp.sum(jnp.sum(diff * diff, axis=-1, keepdims=True), axis=0, keepdims=True) * (1.0 / d)
        dx3, dgf = _rms_bwd(diff * (1.0 / d), xv, r, gf_ref[...])
        dgf_ref[...] += dgf
        dx3_ref[...] = dx3
        dx3b_ref[...] = dx3.astype(BF16)

    tok = pl.BlockSpec((tm, d), lambda i: (i, 0))
    vec = pl.BlockSpec((1, d), lambda i: (0, 0))
    return _pcall(
        body, grid=(s // tm,), in_specs=[tok, tok, vec],
        out_specs=[pl.BlockSpec((SUB, LANES), lambda i: (0, 0)), vec, tok, tok],
        out_shape=(SDS((SUB, LANES), F32), SDS((1, d), F32), SDS((s, d), F32), SDS((s, d), BF16)),
        compiler_params=_arb(1), name="final_backward")(x3, target, g_final)


def _swiglu_backward(dx3b, gu, w_gu, w_down, tm):
    s, d = dx3b.shape
    _, nf, tf, _ = w_gu.shape

    def body(dx3b_ref, gu_ref, wgu_ref, wd_ref, act_ref, dgu_ref, dxn_ref):
        @pl.when(pl.program_id(1) == 0)
        def _():
            dxn_ref[...] = jnp.zeros_like(dxn_ref)

        for r0 in range(0, tm, ROW_CHUNK):
            rows = slice(r0, r0 + ROW_CHUNK)
            dact = _dot_nt(dx3b_ref[rows, :], wd_ref[...])
            gv = gu_ref[0, rows, :].astype(F32)
            uv = gu_ref[1, rows, :].astype(F32)
            sg = _sigmoid(gv)
            silu = gv * sg
            act_ref[rows, :] = (silu * uv).astype(BF16)
            dgate = (dact * uv * (sg * (1.0 + gv * (1.0 - sg)))).astype(BF16)
            dup = (dact * silu).astype(BF16)
            dgu_ref[0, rows, :] = dgate
            dgu_ref[1, rows, :] = dup
            part = _dot(dgate, wgu_ref[0]) + _dot(dup, wgu_ref[1])
            dxn_ref[rows, :] += part

    tok = pl.BlockSpec((tm, d), lambda i, f: (i, 0))
    pair = pl.BlockSpec((2, None, tm, tf), lambda i, f: (0, f, i, 0))
    return _pcall(
        body, grid=(s // tm, nf),
        in_specs=[tok, pair, pl.BlockSpec((2, None, tf, d), lambda i, f: (0, f, 0, 0)),
                  pl.BlockSpec((tf, d), lambda i, f: (f, 0))],
        out_specs=[pl.BlockSpec((None, tm, tf), lambda i, f: (f, i, 0)), pair, tok],
        out_shape=(SDS((nf, s, tf), BF16), SDS((2, nf, s, tf), BF16), SDS((s, d), F32)),
        compiler_params=_arb(2), name="swiglu_backward")(dx3b, gu, w_gu, w_down)


def _attn_backward(dx3, dxn3, x2, g_ffn, x1, g, q, probs, kv, w_q, w_o, tm, carry=()):
    s, d = x1.shape
    _, ml, xd = kv.shape
    scale = xd ** -0.5

    def body(dx3_ref, dxn3_ref, x2_ref, g2_ref, x1_ref, g_ref, q_ref, p_ref, kv_ref, wq_ref, wo_ref,
             dx2b_ref, dq_ref, dx1_ref, dx1b_ref, dkv_ref, dg_ref, dg2_ref, do_s):
        i = pl.program_id(0)

        @pl.when(i == 0)
        def _():
            dkv_ref[...] = jnp.zeros_like(dkv_ref)
            dg_ref[...] = jnp.zeros_like(dg_ref)
            dg2_ref[...] = jnp.zeros_like(dg2_ref)

        x2v = x2_ref[...]
        dx2n, dg2 = _rms_bwd(dxn3_ref[...], x2v, _rstd(x2v), g2_ref[...])
        dg2_ref[...] += dg2
        dx2 = dx3_ref[...] + dx2n
        dx2b_ref[...] = dx2.astype(BF16)
        do_s[...] = _dot_nt(dx2b_ref[...], wo_ref[...]).astype(BF16)
        for hh in range(HEADS):
            kc = slice(hh * xd, (hh + 1) * xd)
            qh = q_ref[:, kc]
            kh = kv_ref[hh]
            doh = do_s[:, kc]
            pb = p_ref[:, hh * ml:(hh + 1) * ml]
            p = pb.astype(F32)
            dp = _dot_nt(doh, kv_ref[HEADS + hh])
            dkv_ref[HEADS + hh] += _dot_tn(pb, doh)
            ds = (p * (dp - jnp.sum(dp * p, axis=-1, keepdims=True)) * scale).astype(BF16)
            dq_ref[:, kc] = _dot(ds, kh).astype(BF16)
            dkv_ref[hh] += _dot_tn(ds, qh)
        dxn = _dot_nt(dq_ref[...], wq_ref[...])
        xv = x1_ref[...]
        dx, dg = _rms_bwd(dxn, xv, _rstd(xv), g_ref[...])
        dg_ref[...] += dg
        dx1 = dx2 + dx
        dx1_ref[...] = dx1
        dx1b_ref[...] = dx1.astype(BF16)

    tok = pl.BlockSpec((tm, d), lambda i: (i, 0))
    vec = pl.BlockSpec((1, d), lambda i: (0, 0))
    sq = pl.BlockSpec((d, d), lambda i: (0, 0))
    kvs = pl.BlockSpec((2 * HEADS, ml, xd), lambda i: (0, 0, 0))
    return _hosted(
        body, carry, grid=(s // tm,),
        in_specs=[tok, tok, tok, vec, tok, vec, tok, pl.BlockSpec((tm, HEADS * ml), lambda i: (i, 0)), kvs, sq, sq],
        out_specs=[tok, tok, tok, tok, kvs, vec, vec],
        out_shape=(SDS((s, d), BF16), SDS((s, d), BF16), SDS((s, d), F32), SDS((s, d), BF16),
                   SDS((2 * HEADS, ml, xd), F32), SDS((1, d), F32), SDS((1, d), F32)),
        scratch_shapes=[pltpu.VMEM((tm, d), BF16)],
        compiler_params=_arb(1), name="attn_backward")(dx3, dxn3, x2, g_ffn, x1, g, q, probs, kv, w_q, w_o)


def _kv_backward(dkv, memn, mem, g_mem, w_kv):
    ml, d = mem.shape
    xd = w_kv.shape[2]

    def body(dkv_ref, memn_ref, mem_ref, g_ref, w_ref, dw_ref, dg_ref):
        dmemn = jnp.zeros((ml, d), F32)
        for j in range(2 * HEADS):
            dkvb = dkv_ref[j].astype(BF16)
            dw_ref[j] = _dot_tn(memn_ref[...], dkvb)
            dmemn = dmemn + _dot_nt(dkvb, w_ref[j])
        x = mem_ref[...]
        dg_ref[...] = jnp.sum(dmemn * (x * _rstd(x)), axis=0, keepdims=True)

    return _pcall(body, out_shape=(SDS((2 * HEADS, d, xd), F32), SDS((1, d), F32)), name="kv_backward")(dkv, memn, mem, g_mem, w_kv)


def _mix_backward(dx1, x, g_mix, h, lng, lnb, w_sp, bt, conv_w, ga, gb, w_out, w_in, tm, carry=()):
    s, d = x.shape
    n_in = h.shape[1]
    aw = lng.shape[1]
    bw = d - aw
    hd = aw // HEADS
    in_a = 2 * aw
    hb_blocks = tm // HALO
    last_blk = s // HALO - 1
    nt = s // tm
    tc = tm
    te = tc + HALO
    tee = tc + 2 * HALO

    def body(dx1_ref, dx1n_ref, x_ref, gm_ref, h_ref, hp_ref, hn_ref, lng_ref, lnb_ref, wsp_ref, bt_ref, cw_ref,
             ga_ref, gb_ref, wout_ref, win_ref,
             dh_ref, dx_ref, dga_ref, dgb_ref, dcw_ref, dlng_ref, dlnb_ref, dwsp_ref, dbs_ref, dgm_ref,
             mixed_s, dvln_s):
        i = pl.program_id(0)

        @pl.when(i == 0)
        def _():
            for ref in (dga_ref, dgb_ref, dcw_ref, dlng_ref, dlnb_ref, dwsp_ref, dbs_ref, dgm_ref):
                ref[...] = jnp.zeros_like(ref)

        mask = _tril_mask()
        wm = [(wsp_ref[hh] * mask).astype(BF16) for hh in range(HEADS)]
        cw = cw_ref[...]

        def chain(r0):
            rows = slice(r0, r0 + tc)
            first, last = r0 == 0, r0 + tc == tm
            hv = h_ref[rows, :]
            dx1 = dx1_ref[rows, :]
            dx1n = dx1n_ref[...] if last else dx1_ref[r0 + tc:r0 + tc + HALO, :]
            hp = hp_ref[:, in_a:] if first else h_ref[r0 - HALO:r0, in_a:]
            hn = hn_ref[:, in_a:] if last else h_ref[r0 + tc:r0 + tc + HALO, in_a:]
            dx1e = jnp.concatenate([dx1, dx1n], axis=0).astype(BF16)
            dycat = _dot_nt(dx1e, wout_ref[...])

            hbe = jnp.concatenate([hp, hv[:, in_a:], hn], axis=0)
            row = lax.broadcasted_iota(jnp.int32, (tee, 1), 0)
            zext = hbe[:, bw:2 * bw] * hbe[:, 2 * bw:]
            if first:
                zext = jnp.where((i == 0) & (row < HALO), 0.0, zext)
            z2e, z1e = _conv_taps(zext)
            conv_e = (cw[0:1] * z2e + cw[1:2] * z1e + cw[2:3] * zext)[HALO:]
            gate_b_e = hbe[HALO:, :bw]
            sc_e = gate_b_e * conv_e
            rb = _rstd(sc_e)
            dyb = dycat[:, aw:]
            gdy = dyb * gb_ref[...]
            dsc_e = rb * gdy - sc_e * (rb * rb * rb) * (jnp.sum(gdy * sc_e, axis=-1, keepdims=True) * (1.0 / bw))
            dgb_ref[...] += jnp.sum((dyb * (sc_e * rb))[:tc], axis=0, keepdims=True)
            dconv_e = dsc_e * gate_b_e
            if last:
                dconv_e = jnp.where((i == nt - 1) & (row[:te] >= tc), 0.0, dconv_e)
            dconv = dconv_e[:tc]
            dc1 = pltpu.roll(dconv_e, te - 1, 0)[:tc]
            dc2 = pltpu.roll(dconv_e, te - 2, 0)[:tc]
            dz = cw[2:3] * dconv + cw[1:2] * dc1 + cw[0:1] * dc2
            z = zext[HALO:HALO + tc]
            z1 = z1e[HALO:HALO + tc]
            z2 = z2e[HALO:HALO + tc]
            dcw_ref[0:1, :] += jnp.sum(dconv * z2, axis=0, keepdims=True)
            dcw_ref[1:2, :] += jnp.sum(dconv * z1, axis=0, keepdims=True)
            dcw_ref[2:3, :] += jnp.sum(dconv * z, axis=0, keepdims=True)
            dh_ref[rows, in_a:in_a + bw] = (dsc_e[:tc] * conv_e[:tc]).astype(BF16)
            dh_ref[rows, in_a + bw:in_a + 2 * bw] = (dz * hv[:, in_a + 2 * bw:]).astype(BF16)
            dh_ref[rows, in_a + 2 * bw:] = (dz * hv[:, in_a + bw:in_a + 2 * bw]).astype(BF16)

            ha = hv[:, :in_a]
            mixed_c, dvln_c = mixed_s.at[rows, :], dvln_s.at[rows, :]
            th, u, xhat, rl, vln = _sgu_forward(ha, lng_ref[...], lnb_ref[...], wm, bt_ref[...], mixed_c)
            mixed = mixed_c[...]
            sg = u * mixed
            dsg, dga = _rms_bwd(dycat[:tc, :aw], sg, _rstd(sg), ga_ref[...])
            dga_ref[...] += dga
            du = dsg * mixed
            dmixed = dsg * u
            dmb = dmixed.astype(BF16)
            for n in range(tc // CHUNK):
                blk = slice(n * CHUNK, (n + 1) * CHUNK)
                dbs_ref[...] += dmixed[blk]
                for hh in range(HEADS):
                    cols = slice(hh * hd, (hh + 1) * hd)
                    dvln_c[blk, cols] = _dot_tn(wm[hh], dmb[blk, cols])
                    dwsp_ref[hh] += mask * _dot_nt(dmb[blk, cols], vln[blk, cols])
            dvln = dvln_c[...]
            dlng_ref[...] += jnp.sum(dvln * xhat, axis=0, keepdims=True)
            dlnb_ref[...] += jnp.sum(dvln, axis=0, keepdims=True)
            dxh = dvln * lng_ref[...]
            dv = rl * (dxh - jnp.mean(dxh, axis=-1, keepdims=True) - xhat * jnp.mean(dxh * xhat, axis=-1, keepdims=True))
            dh_ref[rows, :in_a] = (jnp.concatenate([du, dv], axis=-1) * _gelu_grad(ha, th)).astype(BF16)

            dxn = _dot(dh_ref[rows, :], win_ref[...])
            xv = x_ref[rows, :]
            dx, dgm = _rms_bwd(dxn, xv, _rstd(xv), gm_ref[...])
            dgm_ref[...] += dgm
            dx_ref[rows, :] = dx1 + dx

        for r0 in range(0, tm, tc):
            chain(r0)

    full = lambda shape: pl.BlockSpec(shape, lambda i: (0,) * len(shape))
    tok = pl.BlockSpec((tm, d), lambda i: (i, 0))
    nxt = lambda i: (jnp.minimum((i + 1) * hb_blocks, last_blk), 0)
    prv = lambda i: (jnp.maximum(i * hb_blocks - 1, 0), 0)
    return _hosted(
        body, carry, grid=(nt,),
        in_specs=[tok, pl.BlockSpec((HALO, d), nxt), tok, full((1, d)),
                  pl.BlockSpec((tm, n_in), lambda i: (i, 0)), pl.BlockSpec((HALO, n_in), prv), pl.BlockSpec((HALO, n_in), nxt),
                  full((1, aw)), full((1, aw)), full((HEADS, CHUNK, CHUNK)), full((CHUNK, HEADS)), full((3, bw)),
                  full((1, aw)), full((1, bw)), full((d, d)), full((n_in, d))],
        out_specs=[pl.BlockSpec((tm, n_in), lambda i: (i, 0)), tok,
                   full((1, aw)), full((1, bw)), full((SUB, bw)), full((1, aw)), full((1, aw)),
                   full((HEADS, CHUNK, CHUNK)), full((CHUNK, aw)), full((1, d))],
        out_shape=(SDS((s, n_in), BF16), SDS((s, d), F32),
                   SDS((1, aw), F32), SDS((1, bw), F32), SDS((SUB, bw), F32), SDS((1, aw), F32), SDS((1, aw), F32),
                   SDS((HEADS, CHUNK, CHUNK), F32), SDS((CHUNK, aw), F32), SDS((1, d), F32)),
        scratch_shapes=[pltpu.VMEM((tm, aw), F32), pltpu.VMEM((tm, aw), F32)],
        compiler_params=_arb(1), name="mix_backward")(dx1, dx1, x, g_mix, h, h, h, lng, lnb, w_sp, bt, conv_w, ga, gb, w_out, w_in)


def _bias_grad(dbs):
    aw = dbs.shape[1]
    hd = aw // HEADS

    def body(dbs_ref, out_ref):
        ones = jnp.ones((SUB, hd), F32)
        for hh in range(HEADS):
            r = lax.dot_general(ones, dbs_ref[:, hh * hd:(hh + 1) * hd], (((1,), (1,)), ((), ())),
                                precision=lax.Precision.HIGHEST, preferred_element_type=F32)
            out_ref[hh:hh + 1, :] = r[0:1]

    return _pcall(body, out_shape=SDS((HEADS, CHUNK), F32), name="bias_grad")(dbs)


def _wgrad_body(a_ref, b_ref, o_ref):
    o_ref[...] = _dot_tn(a_ref[...], b_ref[...])


def _wgrad(a, b, name, carry=()):
    k, m = a.shape
    n = b.shape[1]
    tm = _tile(m, 512, LANES)
    tn = _tile(n, 1024, LANES)
    return _hosted(
        functools.partial(_wgrad_body), carry, grid=(m // tm, n // tn),
        in_specs=[pl.BlockSpec((k, tm), lambda i, j: (0, i)), pl.BlockSpec((k, tn), lambda i, j: (0, j))],
        out_specs=pl.BlockSpec((tm, tn), lambda i, j: (i, j)),
        out_shape=SDS((m, n), F32), compiler_params=_arb(2), name=name)(a, b)


def _wgrad_blocked_lhs(a, b, name, carry=()):
    nb, k, t = a.shape
    n = b.shape[1]
    tn = _tile(n, 1024, LANES)
    return _hosted(
        functools.partial(_wgrad_body), carry, grid=(nb, n // tn),
        in_specs=[pl.BlockSpec((None, k, t), lambda i, j: (i, 0, 0)), pl.BlockSpec((k, tn), lambda i, j: (0, j))],
        out_specs=pl.BlockSpec((t, tn), lambda i, j: (i, j)),
        out_shape=SDS((nb * t, n), F32), compiler_params=_arb(2), name=name)(a, b)


def _place():
    x, y, c = lax.axis_index("x"), lax.axis_index("y"), lax.axis_index("c")
    return x, y, c, [(1 - x, y), (x, 1 - y), (1 - x, 1 - y)]


def _all_gather(shards):
    n = len(shards)
    slots = 9
    cut = [(s.shape[0] // 32) * 16 for s in shards]

    def build(ins, outs, sems):
        send_sems, recv_sems, local_sems = sems
        x, y, c, _ = _place()
        me, sib, xn, yn, dg = (x, y, c), (x, y, 1 - c), (1 - x, y, c), (x, 1 - y, c), (1 - x, 1 - y, c)
        other = lambda p: (p[0], p[1], 1 - p[2])

        def rows(a, p, part=None):
            ref = outs[a].at[4 * p[0] + 2 * p[1] + p[2]]
            if part is None or cut[a] == 0:
                return ref if part in (None, 0) else None
            return ref.at[pl.ds(0, cut[a])] if part == 0 else ref.at[pl.ds(cut[a], shards[a].shape[0] - cut[a])]

        def copy(a, k, ref, to, src=None):
            if ref is None:
                return None
            return pltpu.make_async_remote_copy(
                src_ref=ref if src is None else src, dst_ref=ref, send_sem=send_sems.at[slots * a + k],
                recv_sem=recv_sems.at[slots * a + k], device_id=to, device_id_type=MESH)

        def real(cps):
            return [cp for cp in cps if cp is not None]

        class Copies:
            own = lambda a: [copy(a, 1, rows(a, me), xn, ins[a]), copy(a, 2, rows(a, me), yn, ins[a]),
                             copy(a, 0, rows(a, me), sib, ins[a])]
            local = lambda a: pltpu.make_async_copy(ins[a], rows(a, me), local_sems.at[a])
            from_x = lambda a: copy(a, 1, rows(a, xn), me)
            from_y = lambda a: copy(a, 2, rows(a, yn), me)
            after_x = lambda a: real([copy(a, 4, rows(a, xn, 1), yn), copy(a, 5, rows(a, xn), sib)])
            after_y = lambda a: real([copy(a, 3, rows(a, yn, 0), xn), copy(a, 6, rows(a, yn), sib)])
            diag_in = lambda a: real([copy(a, 3, rows(a, dg, 0), me), copy(a, 4, rows(a, dg, 1), me)])
            diag_on = lambda a: real([copy(a, 7, rows(a, dg, 0), sib), copy(a, 8, rows(a, dg, 1), sib)])
            from_sib = lambda a: real([copy(a, 0, rows(a, sib), me), copy(a, 5, rows(a, other(xn)), me),
                                       copy(a, 6, rows(a, other(yn)), me), copy(a, 7, rows(a, other(dg), 0), me),
                                       copy(a, 8, rows(a, other(dg), 1), me)])

        return Copies

    def start(ins, outs, sems):
        cps = build(ins, outs, sems)
        for a in range(n):
            for cp in cps.own(a):
                cp.start()
        for a in range(n):
            cps.local(a).start()

    def relay(ins, outs, sems):
        cps = build(ins, outs, sems)
        for a in range(n):
            cps.from_x(a).wait_recv()
            for cp in cps.after_x(a):
                cp.start()
            cps.from_y(a).wait_recv()
            for cp in cps.after_y(a):
                cp.start()

    def finish(ins, outs, sems):
        cps = build(ins, outs, sems)
        for a in range(n):
            for arrived, onward in zip(cps.diag_in(a), cps.diag_on(a)):
                arrived.wait_recv()
                onward.start()
        for a in range(n):
            for cp in cps.from_sib(a):
                cp.wait_recv()
            for cp in cps.own(a) + cps.after_x(a) + cps.after_y(a) + cps.diag_on(a):
                cp.wait_send()
            cps.local(a).wait()

    def peers():
        x, y, c, _ = _place()
        return [(x, y, 1 - c), (1 - x, y, c), (x, 1 - y, c)]

    return _Exchange(shards, [SDS((N_DEV,) + s.shape, s.dtype) for s in shards],
                     [pltpu.SemaphoreType.DMA((slots * n,)), pltpu.SemaphoreType.DMA((slots * n,)),
                      pltpu.SemaphoreType.DMA((n,))], start, finish, relay, peers, GATHER_COLLECTIVE)


def _swap_exchange(ins, out_shape, per, copies):
    def start(i, o, sems):
        for cp in copies(i, o, sems):
            cp.start()

    def finish(i, o, sems):
        for cp in copies(i, o, sems):
            cp.wait()

    def sibling():
        x, y, c, _ = _place()
        return [(x, y, 1 - c)]

    n = per * len(ins)
    return _Exchange(ins, out_shape, [pltpu.SemaphoreType.DMA((n,)), pltpu.SemaphoreType.DMA((n,))], start, finish,
                     peers=sibling, collective=SIBLING_COLLECTIVE)


def _exchange_c(gs):
    def copies(ins, outs, sems):
        x, y, c, _ = _place()
        return [pltpu.make_async_remote_copy(
                    src_ref=ins[a].at[2 * k + 1 - c], dst_ref=outs[a].at[k],
                    send_sem=sems[0].at[4 * a + k], recv_sem=sems[1].at[4 * a + k],
                    device_id=(x, y, 1 - c), device_id_type=MESH)
                for a in range(len(gs)) for k in range(4)]

    return _swap_exchange(gs, [SDS((4,) + g.shape[1:], g.dtype) for g in gs], 4, copies)


def _rs_combine(g, recv, pos, name, carry=()):
    _, r, cdim = g.shape
    tr = _tile(r, 256, 16)

    def body(pos_ref, g0, r0, g1, r1, g2, r2, g3, r3, keep_ref, send_ref):
        keep_ref[...] = g0[...] + r0[...]
        send_ref[0] = (g1[...] + r1[...]).astype(BF16)
        send_ref[1] = (g2[...] + r2[...]).astype(BF16)
        send_ref[2] = (g3[...] + r3[...]).astype(BF16)

    def k_of(p, t):
        px = p[0] if t in (0, 2) else 1 - p[0]
        py = p[1] if t in (0, 1) else 1 - p[1]
        return 2 * px + py

    blk = (None, tr, cdim)
    in_specs = []
    for t in range(4):
        in_specs.append(pl.BlockSpec(blk, functools.partial(lambda j, p, t: (2 * k_of(p, t) + p[2], j, 0), t=t)))
        in_specs.append(pl.BlockSpec(blk, functools.partial(lambda j, p, t: (k_of(p, t), j, 0), t=t)))
    return _hosted(
        body, carry, n_prefetch=1, out_shape=(SDS((r, cdim), F32), SDS((3, r, cdim), BF16)),
        grid=(r // tr,), in_specs=in_specs,
        out_specs=[pl.BlockSpec((tr, cdim), lambda j, p: (j, 0)), pl.BlockSpec((3, tr, cdim), lambda j, p: (0, j, 0))],
        compiler_params=_arb(1), name=name)(pos, g, recv, g, recv, g, recv, g, recv)


def _adamw_shard(keep, recv, w, m, v, name):
    r, cdim = w.shape
    tr = _tile(r, 256, 16)

    def body(k_ref, r_ref, w_ref, m_ref, v_ref, g_ref, d_ref, nm_ref, nv_ref):
        g = ((k_ref[...] + r_ref[0].astype(F32)) + r_ref[1].astype(F32)) + r_ref[2].astype(F32)
        g_ref[...] = g
        d_ref[...], nm_ref[...], nv_ref[...] = _adamw(w_ref[...], g, m_ref[...], v_ref[...])

    blk = pl.BlockSpec((tr, cdim), lambda j: (j, 0))
    out = SDS((r, cdim), F32)
    return _pcall(body, grid=(r // tr,), in_specs=[blk, pl.BlockSpec((3, tr, cdim), lambda j: (0, j, 0)), blk, blk, blk],
                  out_specs=[blk] * 4, out_shape=(out,) * 4, compiler_params=_arb(1), name=name)(keep, recv, w, m, v)


_HBM = pl.BlockSpec(memory_space=pltpu.HBM)
_SEM = pl.BlockSpec(memory_space=pltpu.SEMAPHORE)
_SPLIT = pltpu.CompilerParams(has_side_effects=pltpu.SideEffectType.DATAFLOW_SIDE_EFFECTING)


def _split_copies(kind, n, refs):
    srcs, lands, (send_sems, recv_sems) = refs[:n], refs[n:2 * n], refs[2 * n:2 * n + 2]
    x, y, c, chips = _place()
    per = _SPLIT_COPIES[kind]
    if kind == "xy":
        ends = lambda a, t: (srcs[a].at[t], lands[a].at[t], (*chips[t], c))
    else:
        ends = lambda a, k: (srcs[a].at[2 * k + 1 - c], lands[a].at[k], (x, y, 1 - c))
    cps = []
    for a in range(n):
        for t in range(per):
            src, dst, to = ends(a, t)
            cps.append(pltpu.make_async_remote_copy(src_ref=src, dst_ref=dst, send_sem=send_sems.at[per * a + t],
                                                    recv_sem=recv_sems.at[per * a + t], device_id=to, device_id_type=MESH))
    return cps


_SPLIT_COPIES = {"xy": 3, "c": 4}
SIBLING_COLLECTIVE = 1
GATHER_COLLECTIVE = 5


def _exchange_start(kind, arrays, name, after=None, collective=None):
    n = len(arrays)
    order = [] if after is None else [after]

    def body(*refs):
        x, y, c, chips = _place()
        peers = [(x, y, 1 - c)] if kind == "c" else [(*chip, c) for chip in chips]
        barrier = pltpu.get_barrier_semaphore()
        for peer in peers:
            pl.semaphore_signal(barrier, inc=1, device_id=peer, device_id_type=MESH)
        pl.semaphore_wait(barrier, len(peers))
        refs = refs[:2 * n] + refs[2 * n + len(order):]
        for cp in _split_copies(kind, n, refs):
            cp.start()
        refs[-1][...] = jnp.zeros_like(refs[-1])

    params = pltpu.CompilerParams(has_side_effects=pltpu.SideEffectType.DATAFLOW_SIDE_EFFECTING,
                                  collective_id=SIBLING_COLLECTIVE if kind == "c" else collective)
    hbm = lambda a: pltpu.with_memory_space_constraint(a, pltpu.HBM)
    land = [a.shape if kind == "xy" else (4,) + a.shape[1:] for a in arrays]
    bufs = [pltpu.HBM(a.shape, a.dtype) for a in arrays] + [pltpu.HBM(s, a.dtype) for s, a in zip(land, arrays)]
    sems = pltpu.SemaphoreType.DMA((_SPLIT_COPIES[kind] * n,))
    res = _pcall(
        body, name=name, out_shape=(sems, sems, *bufs, SDS((SUB, LANES), F32)),
        in_specs=[_HBM] * (2 * n) + _hbm_specs(len(order)),
        out_specs=[_SEM, _SEM] + [_HBM] * (2 * n) + [pl.BlockSpec(memory_space=pltpu.VMEM)],
        input_output_aliases={k: 2 + k for k in range(2 * n)}, compiler_params=params)(
            *[hbm(a) for a in arrays], *[hbm(lax.empty(s, a.dtype)) for s, a in zip(land, arrays)], *order)
    return (kind, n, res[:-1]), res[-1]


def _exchange_wait(started, after, name, sources=False):
    kind, n, (send_sems, recv_sems, *bufs) = started

    def body(*refs):
        for cp in _split_copies(kind, n, refs):
            cp.wait_send()
            cp.wait_recv()

    shapes = [pltpu.HBM(b.shape, b.dtype) for b in bufs]
    res = _pcall(
        body, name=name, out_shape=tuple(shapes),
        in_specs=[_HBM] * (2 * n) + [_SEM, _SEM, pl.BlockSpec(memory_space=pl.ANY)], out_specs=[_HBM] * (2 * n),
        input_output_aliases={k: k for k in range(2 * n)}, compiler_params=_SPLIT)(*bufs, send_sems, recv_sems, after)
    return (list(res[:n]), list(res[n:])) if sources else list(res[n:])


def _follow(token):
    nothing = lambda ins, outs, sems: None
    return _Exchange([token], [], [], nothing, nothing)


def _adamw_small(gathered, seg, params, conv_rows):
    names = list(params)
    c0, cn = conv_rows

    def body(*refs):
        gat_ref = refs[0]
        ins = refs[1:1 + 3 * len(names)]
        outs = refs[1 + 3 * len(names):]

        def total(r0, rn):
            tot = gat_ref[0, r0:r0 + rn, :]
            for dev in range(1, N_DEV):
                tot = tot + gat_ref[dev, r0:r0 + rn, :]
            return tot

        for k, nm in enumerate(names):
            g = total(*seg[nm])
            w_ref, m_ref, v_ref = ins[3 * k:3 * k + 3]
            g_ref, d_ref, nm_ref, nv_ref = outs[4 * k:4 * k + 4]
            g_ref[...] = g
            d_ref[...], nm_ref[...], nv_ref[...] = _adamw(w_ref[...], g, m_ref[...], v_ref[...])
        outs[-2][...] = total(c0, cn)
        outs[-1][...] = total(*seg["loss"])

    flat_in = [a for nm in names for a in params[nm]]
    out_shape = []
    for nm in names:
        out_shape += [SDS(params[nm][0].shape, F32)] * 4
    out_shape += [SDS((cn, LANES), F32), SDS((seg["loss"][1], LANES), F32)]
    res = _pcall(body, out_shape=tuple(out_shape), name="adamw_small")(gathered, *flat_in)
    per = {nm: res[4 * k:4 * k + 4] for k, nm in enumerate(names)}
    return per, res[-2], res[-1]


def _adamw_one(w, g, m, v, name):
    def body(w_ref, g_ref, m_ref, v_ref, d_ref, nm_ref, nv_ref):
        d_ref[...], nm_ref[...], nv_ref[...] = _adamw(w_ref[...], g_ref[...], m_ref[...], v_ref[...])

    return _pcall(body, out_shape=(SDS(w.shape, F32),) * 3, name=name)(w, g, m, v)


def _rows128(a):
    return a.reshape(-1, LANES)


def _pack_small(gs, loss_tile):
    seg, pieces, row = {}, [], 0
    for nm in SMALL + ("conv_w", "loss"):
        piece = loss_tile if nm == "loss" else _rows128(gs[nm])
        rn = _round_up(piece.shape[0], SUB)
        pieces.append(jnp.pad(piece, ((0, rn - piece.shape[0]), (0, 0))))
        seg[nm] = (row, piece.shape[0])
        row += rn
    return jnp.concatenate(pieces, axis=0), seg


def _step(x, mem, target, wb, conv_w, sp, pos):
    s, d = x.shape
    tm = min(TOKEN_TILE, s)
    tm_wide = min(2 * TOKEN_TILE, s)
    rows = lambda w8: w8.reshape(-1, w8.shape[2])
    shards = lambda g: g.reshape((N_DEV, -1) + g.shape[1:])
    bt = sp["b_spatial"].T

    (w_in8, conv8), = _run_exchanges([_all_gather([wb["w_in"], conv_w])], "gather_w_in")
    conv_full = conv8.transpose(1, 0, 2).reshape(3, -1)
    w_in_t = rows(w_in8)
    (xn1, h), ((w_out8, w_kv8, w_q8, w_o8),) = _in_forward(
        x, sp["ln_mix_g"], w_in_t, tm, carry=[_all_gather([wb["w_out"], wb["w_kv"], wb["w_q"], wb["w_o"]])])
    w_out = rows(w_out8)
    (ycat, x1), ((w_gu8,),) = _mix_forward(
        h, x, sp["sgu_ln_g"], sp["sgu_ln_b"], sp["w_spatial"], bt, conv_full, sp["grp_norm_a"], sp["grp_norm_b"], w_out, tm,
        carry=[_all_gather([wb["w_gate_up"]])])
    w_q, w_o = rows(w_q8), rows(w_o8)
    memn, kv = _kv_forward(mem, sp["ln_mem_g"], w_kv8)
    (xn2, q, probs, o, x2), ((w_down8,),) = _attn_forward(
        x1, sp["ln_attn_g"], w_q, kv, w_o, tm, carry=[_all_gather([wb["w_down"]])])
    w_down = rows(w_down8)
    w_gu = w_gu8.reshape((2, N_DEV // 2) + w_gu8.shape[1:])
    xn3, gu, x3 = _ffn_forward(x2, sp["ln_ffn_g"], w_gu, w_down, tm_wide)

    loss, d_lnf, dx3, dx3b = _final_backward(x3, target, sp["ln_final_g"], tm_wide)
    act, dgu, dxn3 = _swiglu_backward(dx3b, gu, w_gu, w_down, tm_wide)
    g_gu, _ = _wgrad_blocked_lhs(dgu.reshape((N_DEV,) + dgu.shape[2:]), xn3, "wgrad_gate_up")
    g_gu = shards(g_gu)
    g_down, ((rc_gu,),) = _wgrad_blocked_lhs(act, dx3b, "wgrad_down", carry=[_exchange_c([g_gu])])
    g_down = shards(g_down)
    keep, pending = {}, []
    (keep["w_gate_up"], send_gu), _ = _rs_combine(g_gu, rc_gu, pos, "rs_combine_w_gate_up")
    started, token = _exchange_start("xy", [send_gu], "exchange_xy_1_start", collective=2)
    pending.append((("w_gate_up",), started))
    c_down, token = _exchange_start("c", [g_down], "exchange_c_1_start", after=token)
    (dx2b, dq, dx1, dx1b, dkv, d_lnattn, d_lnffn), _ = _attn_backward(
        dx3, dxn3, x2, sp["ln_ffn_g"], x1, sp["ln_attn_g"], q, probs, kv, w_q, w_o, tm, carry=[_follow(token)])
    (g_down,), (rc_down,) = _exchange_wait(c_down, dx1b, "exchange_c_1_wait", sources=True)
    (keep["w_down"], send_down), _ = _rs_combine(g_down, rc_down, pos, "rs_combine_w_down")
    g_o, _ = _wgrad(o, dx2b, "wgrad_o")
    g_q, _ = _wgrad(xn2, dq, "wgrad_q")
    g_o, g_q = shards(g_o), shards(g_q)
    g_kv, d_lnmem = _kv_backward(dkv, memn, mem, sp["ln_mem_g"], w_kv8)
    c_oqkv, token = _exchange_start("c", [g_o, g_q, g_kv], "exchange_c_2_start")
    g_out, _ = _wgrad(ycat, dx1b, "wgrad_out", carry=[_follow(token)])
    g_out = shards(g_out)
    (g_o, g_q, g_kv), (rc_o, rc_q, rc_kv) = _exchange_wait(c_oqkv, g_out, "exchange_c_2_wait", sources=True)
    c_out, token = _exchange_start("c", [g_out], "exchange_c_3_start")
    (keep["w_o"], send_o), _ = _rs_combine(g_o, rc_o, pos, "rs_combine_w_o", carry=[_follow(token)])
    (keep["w_q"], send_q), _ = _rs_combine(g_q, rc_q, pos, "rs_combine_w_q")
    (keep["w_kv"], send_kv), _ = _rs_combine(g_kv, rc_kv, pos, "rs_combine_w_kv")
    (g_out,), (rc_out,) = _exchange_wait(c_out, send_kv, "exchange_c_3_wait", sources=True)
    (keep["w_out"], send_out), _ = _rs_combine(g_out, rc_out, pos, "rs_combine_w_out")
    started, token = _exchange_start("xy", [send_down, send_o, send_q, send_out, send_kv], "exchange_xy_2_start",
                                     collective=3)
    pending.append((("w_down", "w_o", "w_q", "w_out", "w_kv"), started))
    (dh, dx, d_ga, d_gb, d_cw, d_lng, d_lnb, d_wsp, d_bs, d_lnmix), _ = _mix_backward(
        dx1, x, sp["ln_mix_g"], h, sp["sgu_ln_g"], sp["sgu_ln_b"], sp["w_spatial"], bt, conv_full,
        sp["grp_norm_a"], sp["grp_norm_b"], w_out, w_in_t, tm, carry=[_follow(token)])
    gs = {"ln_mix_g": d_lnmix, "sgu_ln_g": d_lng, "sgu_ln_b": d_lnb, "w_spatial": d_wsp, "b_spatial": _bias_grad(d_bs),
          "conv_w": d_cw[:3], "grp_norm_a": d_ga, "grp_norm_b": d_gb, "ln_attn_g": d_lnattn, "ln_mem_g": d_lnmem,
          "ln_ffn_g": d_lnffn, "ln_final_g": d_lnf}
    packed, seg = _pack_small(gs, loss)
    g_in, (_, (small_all,)) = _wgrad(dh, xn1, "wgrad_in", carry=[_follow(token), _all_gather([packed])])
    g_in = shards(g_in)
    c_in, token = _exchange_start("c", [g_in], "exchange_c_4_start")
    return dx, keep, pending, (g_in, c_in), token, small_all, seg


def kernel(x, mem, ln_mix_g, w_in, sgu_ln_g, sgu_ln_b, w_spatial, b_spatial, conv_w, grp_norm_a, grp_norm_b, w_out, ln_attn_g, ln_mem_g, w_q, w_kv, w_o, ln_ffn_g, w_gate_up, w_down, ln_final_g, loss_target, m_ln_mix_g, m_w_in, m_sgu_ln_g, m_sgu_ln_b, m_w_spatial, m_b_spatial, m_conv_w, m_grp_norm_a, m_grp_norm_b, m_w_out, m_ln_attn_g, m_ln_mem_g, m_w_q, m_w_kv, m_w_o, m_ln_ffn_g, m_w_gate_up, m_w_down, m_ln_final_g, v_ln_mix_g, v_w_in, v_sgu_ln_g, v_sgu_ln_b, v_w_spatial, v_b_spatial, v_conv_w, v_grp_norm_a, v_grp_norm_b, v_w_out, v_ln_attn_g, v_ln_mem_g, v_w_q, v_w_kv, v_w_o, v_ln_ffn_g, v_w_gate_up, v_w_down, v_ln_final_g):
    order = ["ln_mix_g", "w_in", "sgu_ln_g", "sgu_ln_b", "w_spatial", "b_spatial", "conv_w", "grp_norm_a", "grp_norm_b",
             "w_out", "ln_attn_g", "ln_mem_g", "w_q", "w_kv", "w_o", "ln_ffn_g", "w_gate_up", "w_down", "ln_final_g"]
    W = dict(ln_mix_g=ln_mix_g, w_in=w_in, sgu_ln_g=sgu_ln_g, sgu_ln_b=sgu_ln_b, w_spatial=w_spatial, b_spatial=b_spatial,
             conv_w=conv_w, grp_norm_a=grp_norm_a, grp_norm_b=grp_norm_b, w_out=w_out, ln_attn_g=ln_attn_g,
             ln_mem_g=ln_mem_g, w_q=w_q, w_kv=w_kv, w_o=w_o, ln_ffn_g=ln_ffn_g, w_gate_up=w_gate_up, w_down=w_down,
             ln_final_g=ln_final_g)
    M = dict(ln_mix_g=m_ln_mix_g, w_in=m_w_in, sgu_ln_g=m_sgu_ln_g, sgu_ln_b=m_sgu_ln_b, w_spatial=m_w_spatial,
             b_spatial=m_b_spatial, conv_w=m_conv_w, grp_norm_a=m_grp_norm_a, grp_norm_b=m_grp_norm_b, w_out=m_w_out,
             ln_attn_g=m_ln_attn_g, ln_mem_g=m_ln_mem_g, w_q=m_w_q, w_kv=m_w_kv, w_o=m_w_o, ln_ffn_g=m_ln_ffn_g,
             w_gate_up=m_w_gate_up, w_down=m_w_down, ln_final_g=m_ln_final_g)
    V = dict(ln_mix_g=v_ln_mix_g, w_in=v_w_in, sgu_ln_g=v_sgu_ln_g, sgu_ln_b=v_sgu_ln_b, w_spatial=v_w_spatial,
             b_spatial=v_b_spatial, conv_w=v_conv_w, grp_norm_a=v_grp_norm_a, grp_norm_b=v_grp_norm_b, w_out=v_w_out,
             ln_attn_g=v_ln_attn_g, ln_mem_g=v_ln_mem_g, w_q=v_w_q, w_kv=v_w_kv, w_o=v_w_o, ln_ffn_g=v_ln_ffn_g,
             w_gate_up=v_w_gate_up, w_down=v_w_down, ln_final_g=v_ln_final_g)

    bw = conv_w.shape[1] * N_DEV
    pos = jnp.stack([lax.axis_index("x"), lax.axis_index("y"), lax.axis_index("c")]).astype(jnp.int32)
    me = 4 * pos[0] + 2 * pos[1] + pos[2]

    sp = {nm: (W[nm].reshape(1, -1) if W[nm].ndim == 1 else W[nm]) for nm in SMALL}
    view = lambda a, nm: a.T if nm in TRANSPOSED else a
    wb = {nm: view(W[nm], nm).astype(BF16) for nm in BIG}
    grad_x, keep, pending, (g_in, c_in), token, small_all, seg = _step(
        x[0], mem[0], loss_target[0], wb, conv_w, sp, pos)

    out = {}

    def update(k, names, started, token):
        landed = _exchange_wait(started, token, "exchange_xy_%d_wait" % k)
        for nm, rxy in zip(names, landed):
            res = _adamw_shard(keep[nm], rxy, view(W[nm], nm), view(M[nm], nm), view(V[nm], nm), "adamw_" + nm)
            out[nm] = tuple(view(a, nm) for a in res)
            token = res[0]
        return token

    token = update(1, *pending[0], token)
    (g_in,), (rc_in,) = _exchange_wait(c_in, token, "exchange_c_4_wait", sources=True)
    (keep["w_in"], send_in), _ = _rs_combine(g_in, rc_in, pos, "rs_combine_w_in")
    xy_in, token = _exchange_start("xy", [send_in], "exchange_xy_3_start", collective=4)
    token = update(2, *pending[1], token)

    params = {nm: (_rows128(W[nm]), _rows128(M[nm]), _rows128(V[nm])) for nm in SMALL}
    per, conv_g_rows, loss_sum = _adamw_small(small_all, seg, params, seg["conv_w"])
    for nm in SMALL:
        out[nm] = tuple(a.reshape(W[nm].shape) for a in per[nm])
    conv_g = lax.dynamic_slice_in_dim(conv_g_rows.reshape(3, bw), me * conv_w.shape[1], conv_w.shape[1], axis=1)
    out["conv_w"] = (conv_g,) + tuple(_adamw_one(conv_w, conv_g, m_conv_w, v_conv_w, "adamw_conv"))

    update(3, ("w_in",), xy_in, token[:1, :1] + out["conv_w"][1][:1, :1])

    loss = loss_sum[0, 0]
    res = [loss, grad_x[None]]
    for k in range(4):
        res += [out[nm][k] for nm in order]
    return tuple(res)
```

```python
import functools

import jax
import jax.numpy as jnp
from jax import lax
from jax.experimental import pallas as pl
from jax.experimental.pallas import tpu as pltpu

F32 = jnp.float32
BF16 = jnp.bfloat16
SDS = jax.ShapeDtypeStruct
MESH = pl.DeviceIdType.MESH

EPS = 1e-6
N_DEV = 8
HEADS = 4
CHUNK = 128
HALO = 16
SUB = 8
LANES = 128
TOKEN_TILE = 512
ROW_CHUNK = 256
RELAY_AT = 0.7

ADAM_LR = 0.001
ADAM_B1 = 0.9
ADAM_B2 = 0.999
ADAM_EPS = 1e-08
ADAM_WD = 0.01
ADAM_STEP = 10

BIG = ("w_in", "w_out", "w_q", "w_kv", "w_o", "w_gate_up", "w_down")
TRANSPOSED = ("w_in", "w_gate_up")
SMALL = ("ln_mix_g", "sgu_ln_g", "sgu_ln_b", "w_spatial", "b_spatial", "grp_norm_a", "grp_norm_b",
         "ln_attn_g", "ln_mem_g", "ln_ffn_g", "ln_final_g")


class _Exchange:
    def __init__(self, ins, out_shape, sems, start, finish, relay=None, peers=None, collective=None):
        self.ins, self.out_shape, self.sems = list(ins), list(out_shape), list(sems)
        self.start, self.finish, self.relay = start, finish, relay
        self.peers, self.collective = peers, collective


def _pcall(body, carry=(), n_prefetch=0, **kw):
    if carry:
        return functools.partial(_carrying_call, body, tuple(carry), n_prefetch, kw)
    if n_prefetch:
        kw["grid_spec"] = pltpu.PrefetchScalarGridSpec(
            num_scalar_prefetch=n_prefetch, grid=kw.pop("grid"), in_specs=kw.pop("in_specs"),
            out_specs=kw.pop("out_specs"), scratch_shapes=kw.pop("scratch_shapes", ()))
    return pl.pallas_call(body, **kw)


def _carrying_call(body, carry, n_prefetch, kw, *args):
    kw = dict(kw)
    out_shape = kw.pop("out_shape")
    single = not isinstance(out_shape, (tuple, list))
    outs_shape = (out_shape,) if single else tuple(out_shape)
    out_specs = kw.pop("out_specs")
    out_specs = [out_specs] if single else list(out_specs)
    in_specs = list(kw.pop("in_specs"))
    scratch = list(kw.pop("scratch_shapes", ()))
    grid = tuple(kw.get("grid", ()))
    n_in, n_out, n_scr = len(args), len(outs_shape), len(scratch)
    copying = [p for p in carry if p.sems]
    shaking = copying[0] if len(copying) == 1 and copying[0].collective is not None else None
    if shaking is not None:
        old = kw.get("compiler_params")
        kw["compiler_params"] = pltpu.CompilerParams(
            dimension_semantics=None if old is None else old.dimension_semantics, collective_id=shaking.collective)

    def split(refs, k, counts):
        parts = []
        for cnt in counts:
            parts.append(refs[k:k + cnt])
            k += cnt
        return parts, k

    def wrapped(*refs):
        cins, k = split(refs, n_in, [len(p.ins) for p in carry])
        outs = refs[k:k + n_out]
        couts, k = split(refs, k + n_out, [len(p.out_shape) for p in carry])
        scr = refs[k:k + n_scr]
        csems, _ = split(refs, k + n_scr, [len(p.sems) for p in carry])
        first, last = True, True
        for a, g in enumerate(grid):
            first = (pl.program_id(a) == 0) & first
            last = (pl.program_id(a) == g - 1) & last

        def start_all():
            if shaking is not None:
                peers = shaking.peers()
                barrier = pltpu.get_barrier_semaphore()
                for peer in peers:
                    pl.semaphore_signal(barrier, inc=1, device_id=peer, device_id_type=MESH)
                pl.semaphore_wait(barrier, len(peers))
            for p, ci, co, cs in zip(carry, cins, couts, csems):
                p.start(ci, co, cs)

        def relay_all():
            for p, ci, co, cs in zip(carry, cins, couts, csems):
                if p.relay is not None:
                    p.relay(ci, co, cs)

        def finish_all():
            for p, ci, co, cs in zip(carry, cins, couts, csems):
                p.finish(ci, co, cs)

        step, n_steps = 0, 1
        for a, g in enumerate(grid):
            step, n_steps = step * g + pl.program_id(a), n_steps * g
        relay_now = step == min(int(RELAY_AT * n_steps), n_steps - 1)
        start_all() if not grid else pl.when(first)(start_all)
        relay_all() if not grid else pl.when(relay_now)(relay_all)
        body(*refs[:n_in], *outs, *scr)
        finish_all() if not grid else pl.when(last)(finish_all)

    c_in = [a for p in carry for a in p.ins]
    c_out = [s for p in carry for s in p.out_shape]
    c_sems = [s for p in carry for s in p.sems]
    res = _pcall(wrapped, n_prefetch=n_prefetch, out_shape=outs_shape + tuple(c_out),
                 in_specs=in_specs + _hbm_specs(len(c_in)), out_specs=out_specs + _hbm_specs(len(c_out)),
                 scratch_shapes=scratch + c_sems, **kw)(*args, *c_in)
    own = res[0] if single else tuple(res[:n_out])
    landed, k = [], n_out
    for p in carry:
        landed.append(list(res[k:k + len(p.out_shape)]))
        k += len(p.out_shape)
    return own, landed


def _hbm_specs(n):
    return [pl.BlockSpec(memory_space=pl.ANY)] * n


def _hosted(body, carry, **kw):
    if carry:
        return _pcall(body, carry=carry, **kw)
    call = _pcall(body, **kw)
    return lambda *args: (call(*args), [])


def _run_exchanges(parts, name):
    def body(*refs):
        pass

    _, landed = _pcall(body, carry=parts, out_shape=(), in_specs=[], out_specs=[], name=name)()
    return landed


def _arb(n):
    return pltpu.CompilerParams(dimension_semantics=("arbitrary",) * n)


def _tile(n, target, mult):
    best = None
    for t in range(mult, min(n, target) + 1, mult):
        if n % t == 0:
            best = t
    return n if best is None else best


def _round_up(n, m):
    return (n + m - 1) // m * m


def _dot(a, b):
    return jnp.dot(a, b, preferred_element_type=F32)


def _dot_nt(a, b):
    return lax.dot_general(a, b, (((1,), (1,)), ((), ())), preferred_element_type=F32)


def _dot_tn(a, b):
    return lax.dot_general(a, b, (((0,), (0,)), ((), ())), preferred_element_type=F32)


def _rstd(x):
    return lax.rsqrt(jnp.mean(x * x, axis=-1, keepdims=True) + EPS)


def _rms_bwd(dy, x, r, g):
    gdy = dy * g
    proj = jnp.sum(gdy * x, axis=-1, keepdims=True) * (1.0 / x.shape[-1])
    dx = r * gdy - x * (r * r * r) * proj
    dg = jnp.sum(dy * (x * r), axis=0, keepdims=True)
    return dx, dg


_GELU_C = 0.7978845608028654
_GELU_A = 0.044715


def _gelu(x):
    t = jnp.tanh(_GELU_C * (x + _GELU_A * x * x * x))
    return 0.5 * x * (1.0 + t), t


def _gelu_grad(x, t):
    return 0.5 * (1.0 + t) + 0.5 * x * (1.0 - t * t) * (_GELU_C * (1.0 + 3.0 * _GELU_A * x * x))


def _sigmoid(x):
    return 1.0 / (1.0 + jnp.exp(-x))


def _softmax(s):
    m = jnp.max(s, axis=-1, keepdims=True)
    e = jnp.exp(s - m)
    return e / jnp.sum(e, axis=-1, keepdims=True)


def _adamw(w, g, m, v):
    m = ADAM_B1 * m + (1.0 - ADAM_B1) * g
    v = ADAM_B2 * v + (1.0 - ADAM_B2) * (g * g)
    m_hat = m / (1.0 - ADAM_B1 ** ADAM_STEP)
    v_hat = v / (1.0 - ADAM_B2 ** ADAM_STEP)
    delta = -ADAM_LR * (m_hat / (jnp.sqrt(v_hat) + ADAM_EPS) + ADAM_WD * w)
    return delta, m, v


def _tril_mask():
    t = lax.broadcasted_iota(jnp.int32, (CHUNK, CHUNK), 0)
    s = lax.broadcasted_iota(jnp.int32, (CHUNK, CHUNK), 1)
    return (s <= t).astype(F32)


def _sgu_forward(ha, lng, lnb, wm, bt, mixed_s):
    aw = ha.shape[1] // 2
    hd = aw // HEADS
    a, th = _gelu(ha)
    u = a[:, :aw]
    v = a[:, aw:]
    mu = jnp.mean(v, axis=-1, keepdims=True)
    vc = v - mu
    rl = lax.rsqrt(jnp.mean(vc * vc, axis=-1, keepdims=True) + EPS)
    xhat = vc * rl
    vln = (xhat * lng + lnb).astype(BF16)
    for n in range(ha.shape[0] // CHUNK):
        rows = slice(n * CHUNK, (n + 1) * CHUNK)
        for h in range(HEADS):
            cols = slice(h * hd, (h + 1) * hd)
            mixed_s[rows, cols] = _dot(wm[h], vln[rows, cols]) + bt[:, h:h + 1]
    return th, u, xhat, rl, vln


def _conv_taps(zext):
    return pltpu.roll(zext, 2, 0), pltpu.roll(zext, 1, 0)


def _kv_forward(mem, g_mem, w_kv):
    ml, d = mem.shape
    xd = w_kv.shape[2]

    def body(mem_ref, g_ref, w_ref, memn_ref, kv_ref):
        x = mem_ref[...]
        memn = (x * _rstd(x) * g_ref[...]).astype(BF16)
        memn_ref[...] = memn
        for j in range(2 * HEADS):
            kv_ref[j] = _dot(memn, w_ref[j]).astype(BF16)

    return _pcall(body, out_shape=(SDS((ml, d), BF16), SDS((2 * HEADS, ml, xd), BF16)), name="kv_forward")(mem, g_mem, w_kv)


def _in_forward(x, g, w_in_t, tm, carry=()):
    s, d = x.shape
    n_in = w_in_t.shape[0]

    def body(x_ref, g_ref, w_ref, xn_ref, h_ref):
        xv = x_ref[...]
        xn = (xv * _rstd(xv) * g_ref[...]).astype(BF16)
        xn_ref[...] = xn
        h_ref[...] = _dot_nt(xn, w_ref[...])

    return _hosted(
        body, carry, grid=(s // tm,),
        in_specs=[pl.BlockSpec((tm, d), lambda i: (i, 0)), pl.BlockSpec((1, d), lambda i: (0, 0)),
                  pl.BlockSpec((n_in, d), lambda i: (0, 0))],
        out_specs=[pl.BlockSpec((tm, d), lambda i: (i, 0)), pl.BlockSpec((tm, n_in), lambda i: (i, 0))],
        out_shape=(SDS((s, d), BF16), SDS((s, n_in), F32)),
        compiler_params=_arb(1), name="in_forward")(x, g, w_in_t)


def _mix_forward(h, x, lng, lnb, w_sp, bt, conv_w, ga, gb, w_out, tm, carry=()):
    s, d = x.shape
    n_in = h.shape[1]
    aw = lng.shape[1]
    bw = d - aw
    in_a = 2 * aw
    hb_blocks = tm // HALO

    def body(h_ref, hprev_ref, x_ref, lng_ref, lnb_ref, wsp_ref, bt_ref, cw_ref, ga_ref, gb_ref, wout_ref,
             ycat_ref, x1_ref, mixed_s):
        i = pl.program_id(0)
        mask = _tril_mask()
        wm = [(wsp_ref[hh] * mask).astype(BF16) for hh in range(HEADS)]
        hv = h_ref[...]
        _, u, _, _, _ = _sgu_forward(hv[:, :in_a], lng_ref[...], lnb_ref[...], wm, bt_ref[...], mixed_s)
        sg = u * mixed_s[...]
        ycat_ref[:, :aw] = (sg * _rstd(sg) * ga_ref[...]).astype(BF16)

        gate_b = hv[:, in_a:in_a + bw]
        z = hv[:, in_a + bw:in_a + 2 * bw] * hv[:, in_a + 2 * bw:]
        hp = hprev_ref[...]
        zp = hp[:, in_a + bw:in_a + 2 * bw] * hp[:, in_a + 2 * bw:]
        zp = jnp.where(i == 0, 0.0, zp)
        zext = jnp.concatenate([zp, z], axis=0)
        z2, z1 = _conv_taps(zext)
        cw = cw_ref[...]
        conv = cw[0:1] * z2[HALO:] + cw[1:2] * z1[HALO:] + cw[2:3] * z
        sc = gate_b * conv
        ycat_ref[:, aw:] = (sc * _rstd(sc) * gb_ref[...]).astype(BF16)
        x1_ref[...] = x_ref[...] + _dot(ycat_ref[...], wout_ref[...])

    full = lambda shape: pl.BlockSpec(shape, lambda i: (0,) * len(shape))
    return _hosted(
        body, carry, grid=(s // tm,),
        in_specs=[pl.BlockSpec((tm, n_in), lambda i: (i, 0)),
                  pl.BlockSpec((HALO, n_in), lambda i: (jnp.maximum(i * hb_blocks - 1, 0), 0)),
                  pl.BlockSpec((tm, d), lambda i: (i, 0)),
                  full((1, aw)), full((1, aw)), full((HEADS, CHUNK, CHUNK)), full((CHUNK, HEADS)),
                  full((3, bw)), full((1, aw)), full((1, bw)), full((d, d))],
        out_specs=[pl.BlockSpec((tm, d), lambda i: (i, 0)), pl.BlockSpec((tm, d), lambda i: (i, 0))],
        out_shape=(SDS((s, d), BF16), SDS((s, d), F32)),
        scratch_shapes=[pltpu.VMEM((tm, aw), F32)],
        compiler_params=_arb(1), name="mix_forward")(h, h, x, lng, lnb, w_sp, bt, conv_w, ga, gb, w_out)


def _attn_forward(x1, g, w_q, kv, w_o, tm, carry=()):
    s, d = x1.shape
    _, ml, xd = kv.shape
    scale = xd ** -0.5

    def body(x1_ref, g_ref, wq_ref, kv_ref, wo_ref, xn_ref, q_ref, p_ref, o_ref, x2_ref):
        xv = x1_ref[...]
        xn = (xv * _rstd(xv) * g_ref[...]).astype(BF16)
        xn_ref[...] = xn
        q_ref[...] = _dot(xn, wq_ref[...]).astype(BF16)
        for hh in range(HEADS):
            cols = slice(hh * xd, (hh + 1) * xd)
            p = _softmax(_dot_nt(q_ref[:, cols], kv_ref[hh]) * scale).astype(BF16)
            p_ref[:, hh * ml:(hh + 1) * ml] = p
            o_ref[:, cols] = _dot(p, kv_ref[HEADS + hh]).astype(BF16)
        x2_ref[...] = xv + _dot(o_ref[...], wo_ref[...])

    tok = pl.BlockSpec((tm, d), lambda i: (i, 0))
    probs = pl.BlockSpec((tm, HEADS * ml), lambda i: (i, 0))
    return _hosted(
        body, carry, grid=(s // tm,),
        in_specs=[tok, pl.BlockSpec((1, d), lambda i: (0, 0)), pl.BlockSpec((d, d), lambda i: (0, 0)),
                  pl.BlockSpec((2 * HEADS, ml, xd), lambda i: (0, 0, 0)), pl.BlockSpec((d, d), lambda i: (0, 0))],
        out_specs=[tok, tok, probs, tok, tok],
        out_shape=(SDS((s, d), BF16), SDS((s, d), BF16), SDS((s, HEADS * ml), BF16), SDS((s, d), BF16), SDS((s, d), F32)),
        compiler_params=_arb(1), name="attn_forward")(x1, g, w_q, kv, w_o)


def _ffn_up(x2, g, w_gu, tm, carry=()):
    s, d = x2.shape
    _, nf, tf, _ = w_gu.shape

    def body(x2_ref, g_ref, wgu_ref, xn_ref, gu_ref, act_ref):
        @pl.when(pl.program_id(1) == 0)
        def _():
            xv = x2_ref[...]
            xn_ref[...] = (xv * _rstd(xv) * g_ref[...]).astype(BF16)

        xn = xn_ref[...]
        gate = _dot_nt(xn, wgu_ref[0])
        up = _dot_nt(xn, wgu_ref[1])
        gu_ref[0] = gate.astype(BF16)
        gu_ref[1] = up.astype(BF16)
        act_ref[...] = (gate * _sigmoid(gate) * up).astype(BF16)

    tok = pl.BlockSpec((tm, d), lambda i, f: (i, 0))
    return _hosted(
        body, carry, grid=(s // tm, nf),
        in_specs=[tok, pl.BlockSpec((1, d), lambda i, f: (0, 0)),
                  pl.BlockSpec((2, None, tf, d), lambda i, f: (0, f, 0, 0))],
        out_specs=[tok, pl.BlockSpec((2, None, tm, tf), lambda i, f: (0, f, i, 0)),
                   pl.BlockSpec((None, tm, tf), lambda i, f: (f, i, 0))],
        out_shape=(SDS((s, d), BF16), SDS((2, nf, s, tf), BF16), SDS((nf, s, tf), BF16)),
        compiler_params=_arb(2), name="ffn_up")(x2, g, w_gu)


def _ffn_down(x2, act, w_down, tm):
    s, d = x2.shape
    nf, _, tf = act.shape

    def body(x2_ref, act_ref, wd_ref, x3_ref):
        @pl.when(pl.program_id(1) == 0)
        def _():
            x3_ref[...] = x2_ref[...]

        x3_ref[...] += _dot(act_ref[...], wd_ref[...])

    tok = pl.BlockSpec((tm, d), lambda i, f: (i, 0))
    return _pcall(
        body, grid=(s // tm, nf),
        in_specs=[tok, pl.BlockSpec((None, tm, tf), lambda i, f: (f, i, 0)), pl.BlockSpec((tf, d), lambda i, f: (f, 0))],
        out_specs=tok, out_shape=SDS((s, d), F32), compiler_params=_arb(2), name="ffn_down")(x2, act, w_down)


def _final_backward(x3, target, g_final, tm):
    s, d = x3.shape

    def body(x3_ref, tgt_ref, gf_ref, loss_ref, dgf_ref, dx3_ref, dx3b_ref):
        @pl.when(pl.program_id(0) == 0)
        def _():
            loss_ref[...] = jnp.zeros_like(loss_ref)
            dgf_ref[...] = jnp.zeros_like(dgf_ref)

        xv = x3_ref[...]
        r = _rstd(xv)
        diff = xv * r * gf_ref[...] - tgt_ref[...]
        loss_ref[...] += 0.5 * jnp.sum(jnp.sum(diff * diff, axis=-1, keepdims=True), axis=0, keepdims=True) * (1.0 / d)
        dx3, dgf = _rms_bwd(diff * (1.0 / d), xv, r, gf_ref[...])
        dgf_ref[...] += dgf
        dx3_ref[...] = dx3
        dx3b_ref[...] = dx3.astype(BF16)

    tok = pl.BlockSpec((tm, d), lambda i: (i, 0))
    vec = pl.BlockSpec((1, d), lambda i: (0, 0))
    return _pcall(
        body, grid=(s // tm,), in_specs=[tok, tok, vec],
        out_specs=[pl.BlockSpec((SUB, LANES), lambda i: (0, 0)), vec, tok, tok],
        out_shape=(SDS((SUB, LANES), F32), SDS((1, d), F32), SDS((s, d), F32), SDS((s, d), BF16)),
        compiler_params=_arb(1), name="final_backward")(x3, target, g_final)


def _swiglu_backward(dx3b, gu, w_gu, w_down, tm):
    s, d = dx3b.shape
    _, nf, tf, _ = w_gu.shape

    def body(dx3b_ref, gu_ref, wgu_ref, wd_ref, act_ref, dgu_ref, dxn_ref):
        @pl.when(pl.program_id(1) == 0)
        def _():
            dxn_ref[...] = jnp.zeros_like(dxn_ref)

        for r0 in range(0, tm, ROW_CHUNK):
            rows = slice(r0, r0 + ROW_CHUNK)
            dact = _dot_nt(dx3b_ref[rows, :], wd_ref[...])
            gv = gu_ref[0, rows, :].astype(F32)
            uv = gu_ref[1, rows, :].astype(F32)
            sg = _sigmoid(gv)
            silu = gv * sg
            act_ref[rows, :] = (silu * uv).astype(BF16)
            dgate = (dact * uv * (sg * (1.0 + gv * (1.0 - sg)))).astype(BF16)
            dup = (dact * silu).astype(BF16)
            dgu_ref[0, rows, :] = dgate
            dgu_ref[1, rows, :] = dup
            part = _dot(dgate, wgu_ref[0]) + _dot(dup, wgu_ref[1])
            dxn_ref[rows, :] += part

    tok = pl.BlockSpec((tm, d), lambda i, f: (i, 0))
    pair = pl.BlockSpec((2, None, tm, tf), lambda i, f: (0, f, i, 0))
    return _pcall(
        body, grid=(s // tm, nf),
        in_specs=[tok, pair, pl.BlockSpec((2, None, tf, d), lambda i, f: (0, f, 0, 0)),
                  pl.BlockSpec((tf, d), lambda i, f: (f, 0))],
        out_specs=[pl.BlockSpec((None, tm, tf), lambda i, f: (f, i, 0)), pair, tok],
        out_shape=(SDS((nf, s, tf), BF16), SDS((2, nf, s, tf), BF16), SDS((s, d), F32)),
        compiler_params=_arb(2), name="swiglu_backward")(dx3b, gu, w_gu, w_down)


def _attn_backward(dx3, dxn3, x2, g_ffn, x1, g, q, probs, kv, w_q, w_o, tm, carry=()):
    s, d = x1.shape
    _, ml, xd = kv.shape
    scale = xd ** -0.5

    def body(dx3_ref, dxn3_ref, x2_ref, g2_ref, x1_ref, g_ref, q_ref, p_ref, kv_ref, wq_ref, wo_ref,
             dx2b_ref, dq_ref, dx1_ref, dx1b_ref, dkv_ref, dg_ref, dg2_ref, do_s):
        i = pl.program_id(0)

        @pl.when(i == 0)
        def _():
            dkv_ref[...] = jnp.zeros_like(dkv_ref)
            dg_ref[...] = jnp.zeros_like(dg_ref)
            dg2_ref[...] = jnp.zeros_like(dg2_ref)

        x2v = x2_ref[...]
        dx2n, dg2 = _rms_bwd(dxn3_ref[...], x2v, _rstd(x2v), g2_ref[...])
        dg2_ref[...] += dg2
        dx2 = dx3_ref[...] + dx2n
        dx2b_ref[...] = dx2.astype(BF16)
        do_s[...] = _dot_nt(dx2b_ref[...], wo_ref[...]).astype(BF16)
        for hh in range(HEADS):
            kc = slice(hh * xd, (hh + 1) * xd)
            qh = q_ref[:, kc]
            kh = kv_ref[hh]
            doh = do_s[:, kc]
            pb = p_ref[:, hh * ml:(hh + 1) * ml]
            p = pb.astype(F32)
            dp = _dot_nt(doh, kv_ref[HEADS + hh])
            dkv_ref[HEADS + hh] += _dot_tn(pb, doh)
            ds = (p * (dp - jnp.sum(dp * p, axis=-1, keepdims=True)) * scale).astype(BF16)
            dq_ref[:, kc] = _dot(ds, kh).astype(BF16)
            dkv_ref[hh] += _dot_tn(ds, qh)
        dxn = _dot_nt(dq_ref[...], wq_ref[...])
        xv = x1_ref[...]
        dx, dg = _rms_bwd(dxn, xv, _rstd(xv), g_ref[...])
        dg_ref[...] += dg
        dx1 = dx2 + dx
        dx1_ref[...] = dx1
        dx1b_ref[...] = dx1.astype(BF16)

    tok = pl.BlockSpec((tm, d), lambda i: (i, 0))
    vec = pl.BlockSpec((1, d), lambda i: (0, 0))
    sq = pl.BlockSpec((d, d), lambda i: (0, 0))
    kvs = pl.BlockSpec((2 * HEADS, ml, xd), lambda i: (0, 0, 0))
    return _hosted(
        body, carry, grid=(s // tm,),
        in_specs=[tok, tok, tok, vec, tok, vec, tok, pl.BlockSpec((tm, HEADS * ml), lambda i: (i, 0)), kvs, sq, sq],
        out_specs=[tok, tok, tok, tok, kvs, vec, vec],
        out_shape=(SDS((s, d), BF16), SDS((s, d), BF16), SDS((s, d), F32), SDS((s, d), BF16),
                   SDS((2 * HEADS, ml, xd), F32), SDS((1, d), F32), SDS((1, d), F32)),
        scratch_shapes=[pltpu.VMEM((tm, d), BF16)],
        compiler_params=_arb(1), name="attn_backward")(dx3, dxn3, x2, g_ffn, x1, g, q, probs, kv, w_q, w_o)


def _kv_backward(dkv, memn, mem, g_mem, w_kv):
    ml, d = mem.shape
    xd = w_kv.shape[2]

    def body(dkv_ref, memn_ref, mem_ref, g_ref, w_ref, dw_ref, dg_ref):
        dmemn = jnp.zeros((ml, d), F32)
        for j in range(2 * HEADS):
            dkvb = dkv_ref[j].astype(BF16)
            dw_ref[j] = _dot_tn(memn_ref[...], dkvb)
            dmemn = dmemn + _dot_nt(dkvb, w_ref[j])
        x = mem_ref[...]
        dg_ref[...] = jnp.sum(dmemn * (x * _rstd(x)), axis=0, keepdims=True)

    return _pcall(body, out_shape=(SDS((2 * HEADS, d, xd), F32), SDS((1, d), F32)), name="kv_backward")(dkv, memn, mem, g_mem, w_kv)


def _mix_backward(dx1, x, g_mix, h, lng, lnb, w_sp, bt, conv_w, ga, gb, w_out, w_in, tm, carry=()):
    s, d = x.shape
    n_in = h.shape[1]
    aw = lng.shape[1]
    bw = d - aw
    hd = aw // HEADS
    in_a = 2 * aw
    hb_blocks = tm // HALO
    last_blk = s // HALO - 1
    nt = s // tm
    tc = tm
    te = tc + HALO
    tee = tc + 2 * HALO

    def body(dx1_ref, dx1n_ref, x_ref, gm_ref, h_ref, hp_ref, hn_ref, lng_ref, lnb_ref, wsp_ref, bt_ref, cw_ref,
             ga_ref, gb_ref, wout_ref, win_ref,
             dh_ref, dx_ref, dga_ref, dgb_ref, dcw_ref, dlng_ref, dlnb_ref, dwsp_ref, dbs_ref, dgm_ref,
             mixed_s, dvln_s):
        i = pl.program_id(0)

        @pl.when(i == 0)
        def _():
            for ref in (dga_ref, dgb_ref, dcw_ref, dlng_ref, dlnb_ref, dwsp_ref, dbs_ref, dgm_ref):
                ref[...] = jnp.zeros_like(ref)

        mask = _tril_mask()
        wm = [(wsp_ref[hh] * mask).astype(BF16) for hh in range(HEADS)]
        cw = cw_ref[...]

        def chain(r0):
            rows = slice(r0, r0 + tc)
            first, last = r0 == 0, r0 + tc == tm
            hv = h_ref[rows, :]
            dx1 = dx1_ref[rows, :]
            dx1n = dx1n_ref[...] if last else dx1_ref[r0 + tc:r0 + tc + HALO, :]
            hp = hp_ref[:, in_a:] if first else h_ref[r0 - HALO:r0, in_a:]
            hn = hn_ref[:, in_a:] if last else h_ref[r0 + tc:r0 + tc + HALO, in_a:]
            dx1e = jnp.concatenate([dx1, dx1n], axis=0).astype(BF16)
            dycat = _dot_nt(dx1e, wout_ref[...])

            hbe = jnp.concatenate([hp, hv[:, in_a:], hn], axis=0)
            row = lax.broadcasted_iota(jnp.int32, (tee, 1), 0)
            zext = hbe[:, bw:2 * bw] * hbe[:, 2 * bw:]
            if first:
                zext = jnp.where((i == 0) & (row < HALO), 0.0, zext)
            z2e, z1e = _conv_taps(zext)
            conv_e = (cw[0:1] * z2e + cw[1:2] * z1e + cw[2:3] * zext)[HALO:]
            gate_b_e = hbe[HALO:, :bw]
            sc_e = gate_b_e * conv_e
            rb = _rstd(sc_e)
            dyb = dycat[:, aw:]
            gdy = dyb * gb_ref[...]
            dsc_e = rb * gdy - sc_e * (rb * rb * rb) * (jnp.sum(gdy * sc_e, axis=-1, keepdims=True) * (1.0 / bw))
            dgb_ref[...] += jnp.sum((dyb * (sc_e * rb))[:tc], axis=0, keepdims=True)
            dconv_e = dsc_e * gate_b_e
            if last:
                dconv_e = jnp.where((i == nt - 1) & (row[:te] >= tc), 0.0, dconv_e)
            dconv = dconv_e[:tc]
            dc1 = pltpu.roll(dconv_e, te - 1, 0)[:tc]
            dc2 = pltpu.roll(dconv_e, te - 2, 0)[:tc]
            dz = cw[2:3] * dconv + cw[1:2] * dc1 + cw[0:1] * dc2
            z = zext[HALO:HALO + tc]
            z1 = z1e[HALO:HALO + tc]
            z2 = z2e[HALO:HALO + tc]
            dcw_ref[0:1, :] += jnp.sum(dconv * z2, axis=0, keepdims=True)
            dcw_ref[1:2, :] += jnp.sum(dconv * z1, axis=0, keepdims=True)
            dcw_ref[2:3, :] += jnp.sum(dconv * z, axis=0, keepdims=True)
            dh_ref[rows, in_a:in_a + bw] = (dsc_e[:tc] * conv_e[:tc]).astype(BF16)
            dh_ref[rows, in_a + bw:in_a + 2 * bw] = (dz * hv[:, in_a + 2 * bw:]).astype(BF16)
            dh_ref[rows, in_a + 2 * bw:] = (dz * hv[:, in_a + bw:in_a + 2 * bw]).astype(BF16)

            ha = hv[:, :in_a]
            mixed_c, dvln_c = mixed_s.at[rows, :], dvln_s.at[rows, :]
            th, u, xhat, rl, vln = _sgu_forward(ha, lng_ref[...], lnb_ref[...], wm, bt_ref[...], mixed_c)
            mixed = mixed_c[...]
            sg = u * mixed
            dsg, dga = _rms_bwd(dycat[:tc, :aw], sg, _rstd(sg), ga_ref[...])
            dga_ref[...] += dga
            du = dsg * mixed
            dmixed = dsg * u
            dmb = dmixed.astype(BF16)
            for n in range(tc // CHUNK):
                blk = slice(n * CHUNK, (n + 1) * CHUNK)
                dbs_ref[...] += dmixed[blk]
                for hh in range(HEADS):
                    cols = slice(hh * hd, (hh + 1) * hd)
                    dvln_c[blk, cols] = _dot_tn(wm[hh], dmb[blk, cols])
                    dwsp_ref[hh] += mask * _dot_nt(dmb[blk, cols], vln[blk, cols])
            dvln = dvln_c[...]
            dlng_ref[...] += jnp.sum(dvln * xhat, axis=0, keepdims=True)
            dlnb_ref[...] += jnp.sum(dvln, axis=0, keepdims=True)
            dxh = dvln * lng_ref[...]
            dv = rl * (dxh - jnp.mean(dxh, axis=-1, keepdims=True) - xhat * jnp.mean(dxh * xhat, axis=-1, keepdims=True))
            dh_ref[rows, :in_a] = (jnp.concatenate([du, dv], axis=-1) * _gelu_grad(ha, th)).astype(BF16)

            dxn = _dot(dh_ref[rows, :], win_ref[...])
            xv = x_ref[rows, :]
            dx, dgm = _rms_bwd(dxn, xv, _rstd(xv), gm_ref[...])
            dgm_ref[...] += dgm
            dx_ref[rows, :] = dx1 + dx

        for r0 in range(0, tm, tc):
            chain(r0)

    full = lambda shape: pl.BlockSpec(shape, lambda i: (0,) * len(shape))
    tok = pl.BlockSpec((tm, d), lambda i: (i, 0))
    nxt = lambda i: (jnp.minimum((i + 1) * hb_blocks, last_blk), 0)
    prv = lambda i: (jnp.maximum(i * hb_blocks - 1, 0), 0)
    return _hosted(
        body, carry, grid=(nt,),
        in_specs=[tok, pl.BlockSpec((HALO, d), nxt), tok, full((1, d)),
                  pl.BlockSpec((tm, n_in), lambda i: (i, 0)), pl.BlockSpec((HALO, n_in), prv), pl.BlockSpec((HALO, n_in), nxt),
                  full((1, aw)), full((1, aw)), full((HEADS, CHUNK, CHUNK)), full((CHUNK, HEADS)), full((3, bw)),
                  full((1, aw)), full((1, bw)), full((d, d)), full((n_in, d))],
        out_specs=[pl.BlockSpec((tm, n_in), lambda i: (i, 0)), tok,
                   full((1, aw)), full((1, bw)), full((SUB, bw)), full((1, aw)), full((1, aw)),
                   full((HEADS, CHUNK, CHUNK)), full((CHUNK, aw)), full((1, d))],
        out_shape=(SDS((s, n_in), BF16), SDS((s, d), F32),
                   SDS((1, aw), F32), SDS((1, bw), F32), SDS((SUB, bw), F32), SDS((1, aw), F32), SDS((1, aw), F32),
                   SDS((HEADS, CHUNK, CHUNK), F32), SDS((CHUNK, aw), F32), SDS((1, d), F32)),
        scratch_shapes=[pltpu.VMEM((tm, aw), F32), pltpu.VMEM((tm, aw), F32)],
        compiler_params=_arb(1), name="mix_backward")(dx1, dx1, x, g_mix, h, h, h, lng, lnb, w_sp, bt, conv_w, ga, gb, w_out, w_in)


def _bias_grad(dbs):
    aw = dbs.shape[1]
    hd = aw // HEADS

    def body(dbs_ref, out_ref):
        ones = jnp.ones((SUB, hd), F32)
        for hh in range(HEADS):
            r = lax.dot_general(ones, dbs_ref[:, hh * hd:(hh + 1) * hd], (((1,), (1,)), ((), ())),
                                precision=lax.Precision.HIGHEST, preferred_element_type=F32)
            out_ref[hh:hh + 1, :] = r[0:1]

    return _pcall(body, out_shape=SDS((HEADS, CHUNK), F32), name="bias_grad")(dbs)


def _wgrad_body(a_ref, b_ref, o_ref):
    o_ref[...] = _dot_tn(a_ref[...], b_ref[...])


def _wgrad(a, b, name, carry=()):
    k, m = a.shape
    n = b.shape[1]
    tm = _tile(m, 512, LANES)
    tn = _tile(n, 1024, LANES)
    return _hosted(
        functools.partial(_wgrad_body), carry, grid=(m // tm, n // tn),
        in_specs=[pl.BlockSpec((k, tm), lambda i, j: (0, i)), pl.BlockSpec((k, tn), lambda i, j: (0, j))],
        out_specs=pl.BlockSpec((tm, tn), lambda i, j: (i, j)),
        out_shape=SDS((m, n), F32), compiler_params=_arb(2), name=name)(a, b)


def _wgrad_blocked_lhs(a, b, name, carry=()):
    nb, k, t = a.shape
    n = b.shape[1]
    tn = _tile(n, 1024, LANES)
    return _hosted(
        functools.partial(_wgrad_body), carry, grid=(nb, n // tn),
        in_specs=[pl.BlockSpec((None, k, t), lambda i, j: (i, 0, 0)), pl.BlockSpec((k, tn), lambda i, j: (0, j))],
        out_specs=pl.BlockSpec((t, tn), lambda i, j: (i, j)),
        out_shape=SDS((nb * t, n), F32), compiler_params=_arb(2), name=name)(a, b)


def _place():
    x, y, c = lax.axis_index("x"), lax.axis_index("y"), lax.axis_index("c")
    return x, y, c, [(1 - x, y), (x, 1 - y), (1 - x, 1 - y)]


def _all_gather(shards):
    n = len(shards)
    slots = 9
    cut = [(s.shape[0] // 32) * 16 for s in shards]

    def build(ins, outs, sems):
        send_sems, recv_sems, local_sems = sems
        x, y, c, _ = _place()
        me, sib, xn, yn, dg = (x, y, c), (x, y, 1 - c), (1 - x, y, c), (x, 1 - y, c), (1 - x, 1 - y, c)
        other = lambda p: (p[0], p[1], 1 - p[2])

        def rows(a, p, part=None):
            ref = outs[a].at[4 * p[0] + 2 * p[1] + p[2]]
            if part is None or cut[a] == 0:
                return ref if part in (None, 0) else None
            return ref.at[pl.ds(0, cut[a])] if part == 0 else ref.at[pl.ds(cut[a], shards[a].shape[0] - cut[a])]

        def copy(a, k, ref, to, src=None):
            if ref is None:
                return None
            return pltpu.make_async_remote_copy(
                src_ref=ref if src is None else src, dst_ref=ref, send_sem=send_sems.at[slots * a + k],
                recv_sem=recv_sems.at[slots * a + k], device_id=to, device_id_type=MESH)

        def real(cps):
            return [cp for cp in cps if cp is not None]

        class Copies:
            own = lambda a: [copy(a, 1, rows(a, me), xn, ins[a]), copy(a, 2, rows(a, me), yn, ins[a]),
                             copy(a, 0, rows(a, me), sib, ins[a])]
            local = lambda a: pltpu.make_async_copy(ins[a], rows(a, me), local_sems.at[a])
            from_x = lambda a: copy(a, 1, rows(a, xn), me)
            from_y = lambda a: copy(a, 2, rows(a, yn), me)
            after_x = lambda a: real([copy(a, 4, rows(a, xn, 1), yn), copy(a, 5, rows(a, xn), sib)])
            after_y = lambda a: real([copy(a, 3, rows(a, yn, 0), xn), copy(a, 6, rows(a, yn), sib)])
            diag_in = lambda a: real([copy(a, 3, rows(a, dg, 0), me), copy(a, 4, rows(a, dg, 1), me)])
            diag_on = lambda a: real([copy(a, 7, rows(a, dg, 0), sib), copy(a, 8, rows(a, dg, 1), sib)])
            from_sib = lambda a: real([copy(a, 0, rows(a, sib), me), copy(a, 5, rows(a, other(xn)), me),
                                       copy(a, 6, rows(a, other(yn)), me), copy(a, 7, rows(a, other(dg), 0), me),
                                       copy(a, 8, rows(a, other(dg), 1), me)])

        return Copies

    def start(ins, outs, sems):
        cps = build(ins, outs, sems)
        for a in range(n):
            for cp in cps.own(a):
                cp.start()
        for a in range(n):
            cps.local(a).start()

    def relay(ins, outs, sems):
        cps = build(ins, outs, sems)
        for a in range(n):
            cps.from_x(a).wait_recv()
            for cp in cps.after_x(a):
                cp.start()
            cps.from_y(a).wait_recv()
            for cp in cps.after_y(a):
                cp.start()

    def finish(ins, outs, sems):
        cps = build(ins, outs, sems)
        for a in range(n):
            for arrived, onward in zip(cps.diag_in(a), cps.diag_on(a)):
                arrived.wait_recv()
                onward.start()
        for a in range(n):
            for cp in cps.from_sib(a):
                cp.wait_recv()
            for cp in cps.own(a) + cps.after_x(a) + cps.after_y(a) + cps.diag_on(a):
                cp.wait_send()
            cps.local(a).wait()

    def peers():
        x, y, c, _ = _place()
        return [(x, y, 1 - c), (1 - x, y, c), (x, 1 - y, c)]

    return _Exchange(shards, [SDS((N_DEV,) + s.shape, s.dtype) for s in shards],
                     [pltpu.SemaphoreType.DMA((slots * n,)), pltpu.SemaphoreType.DMA((slots * n,)),
                      pltpu.SemaphoreType.DMA((n,))], start, finish, relay, peers, GATHER_COLLECTIVE)


def _swap_exchange(ins, out_shape, per, copies):
    def start(i, o, sems):
        for cp in copies(i, o, sems):
            cp.start()

    def finish(i, o, sems):
        for cp in copies(i, o, sems):
            cp.wait()

    def sibling():
        x, y, c, _ = _place()
        return [(x, y, 1 - c)]

    n = per * len(ins)
    return _Exchange(ins, out_shape, [pltpu.SemaphoreType.DMA((n,)), pltpu.SemaphoreType.DMA((n,))], start, finish,
                     peers=sibling, collective=SIBLING_COLLECTIVE)


def _exchange_c(gs):
    def copies(ins, outs, sems):
        x, y, c, _ = _place()
        return [pltpu.make_async_remote_copy(
                    src_ref=ins[a].at[2 * k + 1 - c], dst_ref=outs[a].at[k],
                    send_sem=sems[0].at[4 * a + k], recv_sem=sems[1].at[4 * a + k],
                    device_id=(x, y, 1 - c), device_id_type=MESH)
                for a in range(len(gs)) for k in range(4)]

    return _swap_exchange(gs, [SDS((4,) + g.shape[1:], g.dtype) for g in gs], 4, copies)


def _rs_combine(g, recv, pos, name, carry=()):
    _, r, cdim = g.shape
    tr = _tile(r, 256, 16)

    def body(pos_ref, g0, r0, g1, r1, g2, r2, g3, r3, keep_ref, send_ref):
        keep_ref[...] = g0[...] + r0[...]
        send_ref[0] = (g1[...] + r1[...]).astype(BF16)
        send_ref[1] = (g2[...] + r2[...]).astype(BF16)
        send_ref[2] = (g3[...] + r3[...]).astype(BF16)

    def k_of(p, t):
        px = p[0] if t in (0, 2) else 1 - p[0]
        py = p[1] if t in (0, 1) else 1 - p[1]
        return 2 * px + py

    blk = (None, tr, cdim)
    in_specs = []
    for t in range(4):
        in_specs.append(pl.BlockSpec(blk, functools.partial(lambda j, p, t: (2 * k_of(p, t) + p[2], j, 0), t=t)))
        in_specs.append(pl.BlockSpec(blk, functools.partial(lambda j, p, t: (k_of(p, t), j, 0), t=t)))
    return _hosted(
        body, carry, n_prefetch=1, out_shape=(SDS((r, cdim), F32), SDS((3, r, cdim), BF16)),
        grid=(r // tr,), in_specs=in_specs,
        out_specs=[pl.BlockSpec((tr, cdim), lambda j, p: (j, 0)), pl.BlockSpec((3, tr, cdim), lambda j, p: (0, j, 0))],
        compiler_params=_arb(1), name=name)(pos, g, recv, g, recv, g, recv, g, recv)


def _adamw_shard(keep, recv, w, m, v, name):
    r, cdim = w.shape
    tr = _tile(r, 256, 16)

    def body(k_ref, r_ref, w_ref, m_ref, v_ref, g_ref, d_ref, nm_ref, nv_ref):
        g = ((k_ref[...] + r_ref[0].astype(F32)) + r_ref[1].astype(F32)) + r_ref[2].astype(F32)
        g_ref[...] = g
        d_ref[...], nm_ref[...], nv_ref[...] = _adamw(w_ref[...], g, m_ref[...], v_ref[...])

    blk = pl.BlockSpec((tr, cdim), lambda j: (j, 0))
    out = SDS((r, cdim), F32)
    return _pcall(body, grid=(r // tr,), in_specs=[blk, pl.BlockSpec((3, tr, cdim), lambda j: (0, j, 0)), blk, blk, blk],
                  out_specs=[blk] * 4, out_shape=(out,) * 4, compiler_params=_arb(1), name=name)(keep, recv, w, m, v)


_HBM = pl.BlockSpec(memory_space=pltpu.HBM)
_SEM = pl.BlockSpec(memory_space=pltpu.SEMAPHORE)
_SPLIT = pltpu.CompilerParams(has_side_effects=pltpu.SideEffectType.DATAFLOW_SIDE_EFFECTING)


def _split_copies(kind, n, refs):
    srcs, lands, (send_sems, recv_sems) = refs[:n], refs[n:2 * n], refs[2 * n:2 * n + 2]
    x, y, c, chips = _place()
    per = _SPLIT_COPIES[kind]
    if kind == "xy":
        ends = lambda a, t: (srcs[a].at[t], lands[a].at[t], (*chips[t], c))
    else:
        ends = lambda a, k: (srcs[a].at[2 * k + 1 - c], lands[a].at[k], (x, y, 1 - c))
    cps = []
    for a in range(n):
        for t in range(per):
            src, dst, to = ends(a, t)
            cps.append(pltpu.make_async_remote_copy(src_ref=src, dst_ref=dst, send_sem=send_sems.at[per * a + t],
                                                    recv_sem=recv_sems.at[per * a + t], device_id=to, device_id_type=MESH))
    return cps


_SPLIT_COPIES = {"xy": 3, "c": 4}
SIBLING_COLLECTIVE = 1
GATHER_COLLECTIVE = 5


def _exchange_start(kind, arrays, name, after=None, collective=None):
    n = len(arrays)
    order = [] if after is None else [after]

    def body(*refs):
        x, y, c, chips = _place()
        peers = [(x, y, 1 - c)] if kind == "c" else [(*chip, c) for chip in chips]
        barrier = pltpu.get_barrier_semaphore()
        for peer in peers:
            pl.semaphore_signal(barrier, inc=1, device_id=peer, device_id_type=MESH)
        pl.semaphore_wait(barrier, len(peers))
        refs = refs[:2 * n] + refs[2 * n + len(order):]
        for cp in _split_copies(kind, n, refs):
            cp.start()
        refs[-1][...] = jnp.zeros_like(refs[-1])

    params = pltpu.CompilerParams(has_side_effects=pltpu.SideEffectType.DATAFLOW_SIDE_EFFECTING,
                                  collective_id=SIBLING_COLLECTIVE if kind == "c" else collective)
    hbm = lambda a: pltpu.with_memory_space_constraint(a, pltpu.HBM)
    land = [a.shape if kind == "xy" else (4,) + a.shape[1:] for a in arrays]
    bufs = [pltpu.HBM(a.shape, a.dtype) for a in arrays] + [pltpu.HBM(s, a.dtype) for s, a in zip(land, arrays)]
    sems = pltpu.SemaphoreType.DMA((_SPLIT_COPIES[kind] * n,))
    res = _pcall(
        body, name=name, out_shape=(sems, sems, *bufs, SDS((SUB, LANES), F32)),
        in_specs=[_HBM] * (2 * n) + _hbm_specs(len(order)),
        out_specs=[_SEM, _SEM] + [_HBM] * (2 * n) + [pl.BlockSpec(memory_space=pltpu.VMEM)],
        input_output_aliases={k: 2 + k for k in range(2 * n)}, compiler_params=params)(
            *[hbm(a) for a in arrays], *[hbm(lax.empty(s, a.dtype)) for s, a in zip(land, arrays)], *order)
    return (kind, n, res[:-1]), res[-1]


def _exchange_wait(started, after, name, sources=False):
    kind, n, (send_sems, recv_sems, *bufs) = started

    def body(*refs):
        for cp in _split_copies(kind, n, refs):
            cp.wait_send()
            cp.wait_recv()

    shapes = [pltpu.HBM(b.shape, b.dtype) for b in bufs]
    res = _pcall(
        body, name=name, out_shape=tuple(shapes),
        in_specs=[_HBM] * (2 * n) + [_SEM, _SEM, pl.BlockSpec(memory_space=pl.ANY)], out_specs=[_HBM] * (2 * n),
        input_output_aliases={k: k for k in range(2 * n)}, compiler_params=_SPLIT)(*bufs, send_sems, recv_sems, after)
    return (list(res[:n]), list(res[n:])) if sources else list(res[n:])


def _follow(token):
    nothing = lambda ins, outs, sems: None
    return _Exchange([token], [], [], nothing, nothing)


def _adamw_small(gathered, seg, params, conv_rows):
    names = list(params)
    c0, cn = conv_rows

    def body(*refs):
        gat_ref = refs[0]
        ins = refs[1:1 + 3 * len(names)]
        outs = refs[1 + 3 * len(names):]

        def total(r0, rn):
            tot = gat_ref[0, r0:r0 + rn, :]
            for dev in range(1, N_DEV):
                tot = tot + gat_ref[dev, r0:r0 + rn, :]
            return tot

        for k, nm in enumerate(names):
            g = total(*seg[nm])
            w_ref, m_ref, v_ref = ins[3 * k:3 * k + 3]
            g_ref, d_ref, nm_ref, nv_ref = outs[4 * k:4 * k + 4]
            g_ref[...] = g
            d_ref[...], nm_ref[...], nv_ref[...] = _adamw(w_ref[...], g, m_ref[...], v_ref[...])
        outs[-2][...] = total(c0, cn)
        outs[-1][...] = total(*seg["loss"])

    flat_in = [a for nm in names for a in params[nm]]
    out_shape = []
    for nm in names:
        out_shape += [SDS(params[nm][0].shape, F32)] * 4
    out_shape += [SDS((cn, LANES), F32), SDS((seg["loss"][1], LANES), F32)]
    res = _pcall(body, out_shape=tuple(out_shape), name="adamw_small")(gathered, *flat_in)
    per = {nm: res[4 * k:4 * k + 4] for k, nm in enumerate(names)}
    return per, res[-2], res[-1]


def _adamw_one(w, g, m, v, name):
    def body(w_ref, g_ref, m_ref, v_ref, d_ref, nm_ref, nv_ref):
        d_ref[...], nm_ref[...], nv_ref[...] = _adamw(w_ref[...], g_ref[...], m_ref[...], v_ref[...])

    return _pcall(body, out_shape=(SDS(w.shape, F32),) * 3, name=name)(w, g, m, v)


def _rows128(a):
    return a.reshape(-1, LANES)


def _pack_small(gs, loss_tile):
    seg, pieces, row = {}, [], 0
    for nm in SMALL + ("conv_w", "loss"):
        piece = loss_tile if nm == "loss" else _rows128(gs[nm])
        rn = _round_up(piece.shape[0], SUB)
        pieces.append(jnp.pad(piece, ((0, rn - piece.shape[0]), (0, 0))))
        seg[nm] = (row, piece.shape[0])
        row += rn
    return jnp.concatenate(pieces, axis=0), seg


def _step(x, mem, target, wb, conv_w, sp, pos):
    s, d = x.shape
    tm = min(TOKEN_TILE, s)
    tm_wide = min(2 * TOKEN_TILE, s)
    rows = lambda w8: w8.reshape(-1, w8.shape[2])
    shards = lambda g: g.reshape((N_DEV, -1) + g.shape[1:])
    bt = sp["b_spatial"].T

    (w_in8, conv8), = _run_exchanges([_all_gather([wb["w_in"], conv_w])], "gather_w_in")
    conv_full = conv8.transpose(1, 0, 2).reshape(3, -1)
    w_in_t = rows(w_in8)
    (xn1, h), ((w_out8, w_kv8, w_q8),) = _in_forward(
        x, sp["ln_mix_g"], w_in_t, tm, carry=[_all_gather([wb["w_out"], wb["w_kv"], wb["w_q"]])])
    w_out = rows(w_out8)
    (ycat, x1), ((w_o8,),) = _mix_forward(
        h, x, sp["sgu_ln_g"], sp["sgu_ln_b"], sp["w_spatial"], bt, conv_full, sp["grp_norm_a"], sp["grp_norm_b"], w_out, tm,
        carry=[_all_gather([wb["w_o"]])])
    w_q, w_o = rows(w_q8), rows(w_o8)
    memn, kv = _kv_forward(mem, sp["ln_mem_g"], w_kv8)
    (xn2, q, probs, o, x2), ((w_gu8,),) = _attn_forward(
        x1, sp["ln_attn_g"], w_q, kv, w_o, tm, carry=[_all_gather([wb["w_gate_up"]])])
    w_gu = w_gu8.reshape((2, N_DEV // 2) + w_gu8.shape[1:])
    (xn3, gu, act_fwd), ((w_down8,),) = _ffn_up(x2, sp["ln_ffn_g"], w_gu, tm_wide, carry=[_all_gather([wb["w_down"]])])
    w_down = rows(w_down8)
    x3 = _ffn_down(x2, act_fwd, w_down, tm_wide)

    loss, d_lnf, dx3, dx3b = _final_backward(x3, target, sp["ln_final_g"], tm_wide)
    act, dgu, dxn3 = _swiglu_backward(dx3b, gu, w_gu, w_down, tm_wide)
    g_gu, _ = _wgrad_blocked_lhs(dgu.reshape((N_DEV,) + dgu.shape[2:]), xn3, "wgrad_gate_up")
    g_gu = shards(g_gu)
    g_down, ((rc_gu,),) = _wgrad_blocked_lhs(act, dx3b, "wgrad_down", carry=[_exchange_c([g_gu])])
    g_down = shards(g_down)
    keep, pending = {}, []
    (keep["w_gate_up"], send_gu), _ = _rs_combine(g_gu, rc_gu, pos, "rs_combine_w_gate_up")
    started, token = _exchange_start("xy", [send_gu], "exchange_xy_1_start", collective=2)
    pending.append((("w_gate_up",), started))
    c_down, token = _exchange_start("c", [g_down], "exchange_c_1_start", after=token)
    (dx2b, dq, dx1, dx1b, dkv, d_lnattn, d_lnffn), _ = _attn_backward(
        dx3, dxn3, x2, sp["ln_ffn_g"], x1, sp["ln_attn_g"], q, probs, kv, w_q, w_o, tm, carry=[_follow(token)])
    (g_down,), (rc_down,) = _exchange_wait(c_down, dx1b, "exchange_c_1_wait", sources=True)
    (keep["w_down"], send_down), _ = _rs_combine(g_down, rc_down, pos, "rs_combine_w_down")
    g_o, _ = _wgrad(o, dx2b, "wgrad_o")
    g_q, _ = _wgrad(xn2, dq, "wgrad_q")
    g_o, g_q = shards(g_o), shards(g_q)
    g_kv, d_lnmem = _kv_backward(dkv, memn, mem, sp["ln_mem_g"], w_kv8)
    c_oqkv, token = _exchange_start("c", [g_o, g_q, g_kv], "exchange_c_2_start")
    g_out, _ = _wgrad(ycat, dx1b, "wgrad_out", carry=[_follow(token)])
    g_out = shards(g_out)
    (g_o, g_q, g_kv), (rc_o, rc_q, rc_kv) = _exchange_wait(c_oqkv, g_out, "exchange_c_2_wait", sources=True)
    c_out, token = _exchange_start("c", [g_out], "exchange_c_3_start")
    (keep["w_o"], send_o), _ = _rs_combine(g_o, rc_o, pos, "rs_combine_w_o", carry=[_follow(token)])
    (keep["w_q"], send_q), _ = _rs_combine(g_q, rc_q, pos, "rs_combine_w_q")
    (keep["w_kv"], send_kv), _ = _rs_combine(g_kv, rc_kv, pos, "rs_combine_w_kv")
    (g_out,), (rc_out,) = _exchange_wait(c_out, send_kv, "exchange_c_3_wait", sources=True)
    (keep["w_out"], send_out), _ = _rs_combine(g_out, rc_out, pos, "rs_combine_w_out")
    started, token = _exchange_start("xy", [send_down, send_o, send_q, send_out, send_kv], "exchange_xy_2_start",
                                     collective=3)
    pending.append((("w_down", "w_o", "w_q", "w_out", "w_kv"), started))
    (dh, dx, d_ga, d_gb, d_cw, d_lng, d_lnb, d_wsp, d_bs, d_lnmix), _ = _mix_backward(
        dx1, x, sp["ln_mix_g"], h, sp["sgu_ln_g"], sp["sgu_ln_b"], sp["w_spatial"], bt, conv_full,
        sp["grp_norm_a"], sp["grp_norm_b"], w_out, w_in_t, tm, carry=[_follow(token)])
    gs = {"ln_mix_g": d_lnmix, "sgu_ln_g": d_lng, "sgu_ln_b": d_lnb, "w_spatial": d_wsp, "b_spatial": _bias_grad(d_bs),
          "conv_w": d_cw[:3], "grp_norm_a": d_ga, "grp_norm_b": d_gb, "ln_attn_g": d_lnattn, "ln_mem_g": d_lnmem,
          "ln_ffn_g": d_lnffn, "ln_final_g": d_lnf}
    packed, seg = _pack_small(gs, loss)
    g_in, (_, (small_all,)) = _wgrad(dh, xn1, "wgrad_in", carry=[_follow(token), _all_gather([packed])])
    g_in = shards(g_in)
    c_in, token = _exchange_start("c", [g_in], "exchange_c_4_start")
    return dx, keep, pending, (g_in, c_in), token, small_all, seg


def kernel(x, mem, ln_mix_g, w_in, sgu_ln_g, sgu_ln_b, w_spatial, b_spatial, conv_w, grp_norm_a, grp_norm_b, w_out, ln_attn_g, ln_mem_g, w_q, w_kv, w_o, ln_ffn_g, w_gate_up, w_down, ln_final_g, loss_target, m_ln_mix_g, m_w_in, m_sgu_ln_g, m_sgu_ln_b, m_w_spatial, m_b_spatial, m_conv_w, m_grp_norm_a, m_grp_norm_b, m_w_out, m_ln_attn_g, m_ln_mem_g, m_w_q, m_w_kv, m_w_o, m_ln_ffn_g, m_w_gate_up, m_w_down, m_ln_final_g, v_ln_mix_g, v_w_in, v_sgu_ln_g, v_sgu_ln_b, v_w_spatial, v_b_spatial, v_conv_w, v_grp_norm_a, v_grp_norm_b, v_w_out, v_ln_attn_g, v_ln_mem_g, v_w_q, v_w_kv, v_w_o, v_ln_ffn_g, v_w_gate_up, v_w_down, v_ln_final_g):
    order = ["ln_mix_g", "w_in", "sgu_ln_g", "sgu_ln_b", "w_spatial", "b_spatial", "conv_w", "grp_norm_a", "grp_norm_b",
             "w_out", "ln_attn_g", "ln_mem_g", "w_q", "w_kv", "w_o", "ln_ffn_g", "w_gate_up", "w_down", "ln_final_g"]
    W = dict(ln_mix_g=ln_mix_g, w_in=w_in, sgu_ln_g=sgu_ln_g, sgu_ln_b=sgu_ln_b, w_spatial=w_spatial, b_spatial=b_spatial,
             conv_w=conv_w, grp_norm_a=grp_norm_a, grp_norm_b=grp_norm_b, w_out=w_out, ln_attn_g=ln_attn_g,
             ln_mem_g=ln_mem_g, w_q=w_q, w_kv=w_kv, w_o=w_o, ln_ffn_g=ln_ffn_g, w_gate_up=w_gate_up, w_down=w_down,
             ln_final_g=ln_final_g)
    M = dict(ln_mix_g=m_ln_mix_g, w_in=m_w_in, sgu_ln_g=m_sgu_ln_g, sgu_ln_b=m_sgu_ln_b, w_spatial=m_w_spatial,
             b_spatial=m_b_spatial, conv_w=m_conv_w, grp_norm_a=m_grp_norm_a, grp_norm_b=m_grp_norm_b, w_out=m_w_out,
             ln_attn_g=m_ln_attn_g, ln_mem_g=m_ln_mem_g, w_q=m_w_q, w_kv=m_w_kv, w_o=m_w_o, ln_ffn_g=m_ln_ffn_g,
             w_gate_up=m_w_gate_up, w_down=m_w_down, ln_final_g=m_ln_final_g)
    V = dict(ln_mix_g=v_ln_mix_g, w_in=v_w_in, sgu_ln_g=v_sgu_ln_g, sgu_ln_b=v_sgu_ln_b, w_spatial=v_w_spatial,
             b_spatial=v_b_spatial, conv_w=v_conv_w, grp_norm_a=v_grp_norm_a, grp_norm_b=v_grp_norm_b, w_out=v_w_out,
             ln_attn_g=v_ln_attn_g, ln_mem_g=v_ln_mem_g, w_q=v_w_q, w_kv=v_w_kv, w_o=v_w_o, ln_ffn_g=v_ln_ffn_g,
             w_gate_up=v_w_gate_up, w_down=v_w_down, ln_final_g=v_ln_final_g)

    bw = conv_w.shape[1] * N_DEV
    pos = jnp.stack([lax.axis_index("x"), lax.axis_index("y"), lax.axis_index("c")]).astype(jnp.int32)
    me = 4 * pos[0] + 2 * pos[1] + pos[2]

    sp = {nm: (W[nm].reshape(1, -1) if W[nm].ndim == 1 else W[nm]) for nm in SMALL}
    view = lambda a, nm: a.T if nm in TRANSPOSED else a
    wb = {nm: view(W[nm], nm).astype(BF16) for nm in BIG}
    grad_x, keep, pending, (g_in, c_in), token, small_all, seg = _step(
        x[0], mem[0], loss_target[0], wb, conv_w, sp, pos)

    out = {}

    def update(k, names, started, token):
        landed = _exchange_wait(started, token, "exchange_xy_%d_wait" % k)
        for nm, rxy in zip(names, landed):
            res = _adamw_shard(keep[nm], rxy, view(W[nm], nm), view(M[nm], nm), view(V[nm], nm), "adamw_" + nm)
            out[nm] = tuple(view(a, nm) for a in res)
            token = res[0]
        return token

    token = update(1, *pending[0], token)
    (g_in,), (rc_in,) = _exchange_wait(c_in, token, "exchange_c_4_wait", sources=True)
    (keep["w_in"], send_in), _ = _rs_combine(g_in, rc_in, pos, "rs_combine_w_in")
    xy_in, token = _exchange_start("xy", [send_in], "exchange_xy_3_start", collective=4)
    token = update(2, *pending[1], token)

    params = {nm: (_rows128(W[nm]), _rows128(M[nm]), _rows128(V[nm])) for nm in SMALL}
    per, conv_g_rows, loss_sum = _adamw_small(small_all, seg, params, seg["conv_w"])
    for nm in SMALL:
        out[nm] = tuple(a.reshape(W[nm].shape) for a in per[nm])
    conv_g = lax.dynamic_slice_in_dim(conv_g_rows.reshape(3, bw), me * conv_w.shape[1], conv_w.shape[1], axis=1)
    out["conv_w"] = (conv_g,) + tuple(_adamw_one(conv_w, conv_g, m_conv_w, v_conv_w, "adamw_conv"))

    update(3, ("w_in",), xy_in, token[:1, :1] + out["conv_w"][1][:1, :1])

    loss = loss_sum[0, 0]
    res = [loss, grad_x[None]]
    for k in range(4):
        res += [out[nm][k] for nm in order]
    return tuple(res)
```
